```python
import math
import jax, jax.numpy as jnp
from jax import lax
import numpy as np

D_MODEL = 1024
BATCH = 8
SEQ = 8192
DEPTH = 2

MLA_HEADS = 8
MLA_NOPE_DIM = 128
MLA_ROPE_DIM = 64
MLA_V_DIM = 128
MLA_Q_RANK = 384
MLA_KV_RANK = 256
ROPE_THETA = 10000.0
SWA_Q_HEADS = 16
SWA_KV_HEADS = 4
SWA_HEAD_DIM = 64
WINDOW = 128
REL_BUCKETS = 32
REL_MAX_DIST = 128
D_FF = 4 * D_MODEL
BLOCK = 128
LN_EPS = 1e-5
RMS_EPS = 1e-6

kernel_name = 'yoco_mla_swa_sink_t5_deepnorm'


def _layernorm(x, g, b):
    xf = x.astype(jnp.float32)
    mu = xf.mean(-1, keepdims=True)
    var = jnp.square(xf - mu).mean(-1, keepdims=True)
    return ((xf - mu) * lax.rsqrt(var + LN_EPS) * g + b).astype(x.dtype)


def _rmsnorm(x, g):
    xf = x.astype(jnp.float32)
    return (xf * lax.rsqrt(jnp.mean(xf * xf, -1, keepdims=True) + RMS_EPS) * g).astype(x.dtype)


def _rope(x, pos):
    half = x.shape[-1] // 2
    inv = ROPE_THETA ** (-jnp.arange(half, dtype=jnp.float32) / half)
    ang = pos.astype(jnp.float32)[:, None] * inv[None, :]
    ang = ang.reshape(ang.shape[:1] + (1,) * (x.ndim - 3) + ang.shape[1:])
    cos, sin = jnp.cos(ang), jnp.sin(ang)
    xf = x.astype(jnp.float32)
    x1, x2 = xf[..., :half], xf[..., half:]
    return jnp.concatenate([x1 * cos - x2 * sin, x1 * sin + x2 * cos], -1).astype(x.dtype)


def _t5_bucket(dist):
    n = jnp.maximum(dist, 0)
    max_exact = REL_BUCKETS // 2
    nf = jnp.maximum(n, 1).astype(jnp.float32)
    large = max_exact + (jnp.log(nf / max_exact) / math.log(REL_MAX_DIST / max_exact)
                         * (REL_BUCKETS - max_exact)).astype(jnp.int32)
    large = jnp.minimum(large, REL_BUCKETS - 1)
    return jnp.where(n < max_exact, n, large)


def _to_blocks(t, nb):
    return t.reshape((t.shape[0], nb, BLOCK) + t.shape[2:]).swapaxes(0, 1)


def _mla(x, w_in, g_q, g_kv, w_uq, w_uk, w_uv, w_o, pos):
    B, S, _ = x.shape
    nb = S // BLOCK
    h = x @ w_in
    c_q = _rmsnorm(h[..., :MLA_Q_RANK], g_q)
    c_kv = _rmsnorm(h[..., MLA_Q_RANK:MLA_Q_RANK + MLA_KV_RANK], g_kv)
    k_r = _rope(h[..., MLA_Q_RANK + MLA_KV_RANK:], pos)
    q = jnp.einsum('bsr,rhd->bshd', c_q, w_uq)
    q_n = q[..., :MLA_NOPE_DIM]
    q_r = _rope(q[..., MLA_NOPE_DIM:], pos)
    q_lat = jnp.einsum('bshn,chn->bshc', q_n, w_uk)
    scale = (MLA_NOPE_DIM + MLA_ROPE_DIM) ** -0.5
    kpos = jnp.arange(S)

    def block(args):
        ql, qr, i = args
        s = (jnp.einsum('bqhc,bkc->bhqk', ql, c_kv)
             + jnp.einsum('bqhr,bkr->bhqk', qr, k_r)).astype(jnp.float32) * scale
        qpos = i * BLOCK + jnp.arange(BLOCK)
        s = jnp.where(kpos[None, :] <= qpos[:, None], s, -jnp.inf)
        p = jax.nn.softmax(s, axis=-1).astype(x.dtype)
        return jnp.einsum('bhqk,bkc->bqhc', p, c_kv)

    o_lat = lax.map(block, (_to_blocks(q_lat, nb), _to_blocks(q_r, nb), jnp.arange(nb)))
    o_lat = o_lat.swapaxes(0, 1).reshape(B, S, MLA_HEADS, MLA_KV_RANK)
    o = jnp.einsum('bshc,chv->bshv', o_lat, w_uv).reshape(B, S, MLA_HEADS * MLA_V_DIM)
    return o @ w_o


def _shared_kv(x, w_kv):
    B, S, _ = x.shape
    nb = S // BLOCK
    kv = (x @ w_kv).reshape(B, nb, BLOCK, 2, SWA_KV_HEADS, SWA_HEAD_DIM)
    k, v = kv[:, :, :, 0], kv[:, :, :, 1]

    def band(t):
        prev = jnp.pad(t, ((0, 0), (1, 0), (0, 0), (0, 0), (0, 0)))[:, :-1]
        return jnp.concatenate([prev, t], axis=2)

    return band(k), band(v)


def _swa(x, w_q, sinks, w_o, k_band, v_band, rel_bias):
    B, S, _ = x.shape
    nb = S // BLOCK
    G = SWA_Q_HEADS // SWA_KV_HEADS
    q = (x @ w_q).reshape(B, nb, BLOCK, SWA_KV_HEADS, G, SWA_HEAD_DIM)
    s = jnp.einsum('bnqkgd,bnjkd->bnkgqj', q, k_band).astype(jnp.float32) * SWA_HEAD_DIM ** -0.5
    i = jnp.arange(BLOCK)
    j = jnp.arange(2 * BLOCK)
    dist = i[:, None] + BLOCK - j[None, :]
    bias = rel_bias[_t5_bucket(dist)].astype(jnp.float32)
    bias = bias.transpose(2, 0, 1).reshape(SWA_KV_HEADS, G, BLOCK, 2 * BLOCK)
    kpos = jnp.arange(nb)[:, None] * BLOCK - BLOCK + j[None, :]
    valid = (dist >= 0) & (dist < WINDOW)
    mask = valid[None] & (kpos >= 0)[:, None, :]
    s = jnp.where(mask[None, :, None, None], s + bias, -jnp.inf)
    sink = sinks.astype(jnp.float32).reshape(SWA_KV_HEADS, G)[..., None]
    m = jnp.maximum(s.max(-1), sink)
    p = jnp.exp(s - m[..., None])
    denom = p.sum(-1) + jnp.exp(sink - m)
    p = (p / denom[..., None]).astype(x.dtype)
    o = jnp.einsum('bnkgqj,bnjkd->bnqkgd', p, v_band).reshape(B, S, SWA_Q_HEADS * SWA_HEAD_DIM)
    return o @ w_o


def _mlp(x, w_up, w_down):
    h = jax.nn.relu(x @ w_up)
    return (h * h) @ w_down


def _fwd_setup_inputs(seed: int = 0) -> dict:
    key = jax.random.key(seed)
    ks = jax.random.split(key, 20)
    n_a = DEPTH // 2
    n_b = DEPTH - n_a
    beta = (8 * DEPTH) ** -0.25
    f32 = jnp.float32

    def nrm(k, shape, fan_in, gain=1.0):
        return jax.random.normal(k, shape, f32) * (gain * fan_in ** -0.5)

    in_w = MLA_Q_RANK + MLA_KV_RANK + MLA_ROPE_DIM
    kv_k = nrm(ks[8], (D_MODEL, 1, SWA_KV_HEADS * SWA_HEAD_DIM), D_MODEL)
    kv_v = nrm(ks[9], (D_MODEL, 1, SWA_KV_HEADS * SWA_HEAD_DIM), D_MODEL, beta)
    return {
        'x': jax.random.normal(ks[0], (BATCH, SEQ, D_MODEL), f32),
        'mla_w_in': nrm(ks[1], (n_a, D_MODEL, in_w), D_MODEL),
        'mla_g_q': 1.0 + 0.05 * jax.random.normal(ks[2], (n_a, MLA_Q_RANK), f32),
        'mla_g_kv': 1.0 + 0.05 * jax.random.normal(ks[3], (n_a, MLA_KV_RANK), f32),
        'mla_w_uq': nrm(ks[4], (n_a, MLA_Q_RANK, MLA_HEADS, MLA_NOPE_DIM + MLA_ROPE_DIM), MLA_Q_RANK),
        'mla_w_uk': nrm(ks[5], (n_a, MLA_KV_RANK, MLA_HEADS, MLA_NOPE_DIM), MLA_KV_RANK),
        'mla_w_uv': nrm(ks[6], (n_a, MLA_KV_RANK, MLA_HEADS, MLA_V_DIM), MLA_KV_RANK, beta),
        'mla_w_o': nrm(ks[7], (n_a, MLA_HEADS * MLA_V_DIM, D_MODEL), MLA_HEADS * MLA_V_DIM, beta),
        'kv_w_shared': jnp.concatenate([kv_k, kv_v], axis=1).reshape(D_MODEL, 2 * SWA_KV_HEADS * SWA_HEAD_DIM),
        'swa_w_q': nrm(ks[10], (n_b, D_MODEL, SWA_Q_HEADS * SWA_HEAD_DIM), D_MODEL),
        'swa_sinks': 0.5 * jax.random.normal(ks[11], (n_b, SWA_Q_HEADS), f32),
        'swa_w_o': nrm(ks[12], (n_b, SWA_Q_HEADS * SWA_HEAD_DIM, D_MODEL), SWA_Q_HEADS * SWA_HEAD_DIM, beta),
        'rel_bias': 0.5 * jax.random.normal(ks[13], (REL_BUCKETS, SWA_Q_HEADS), f32),
        'mlp_w_up': nrm(ks[14], (DEPTH, D_MODEL, D_FF), D_MODEL),
        'mlp_w_down': nrm(ks[15], (DEPTH, D_FF, D_MODEL), D_FF, beta),
        'ln_mix_g': 1.0 + 0.05 * jax.random.normal(ks[16], (DEPTH, D_MODEL), f32),
        'ln_mix_b': 0.02 * jax.random.normal(ks[17], (DEPTH, D_MODEL), f32),
        'ln_mlp_g': 1.0 + 0.05 * jax.random.normal(ks[18], (DEPTH, D_MODEL), f32),
        'ln_mlp_b': 0.02 * jax.random.normal(ks[19], (DEPTH, D_MODEL), f32),
    }


def _fwd_reference(x, mla_w_in, mla_g_q, mla_g_kv, mla_w_uq, mla_w_uk, mla_w_uv, mla_w_o,
              kv_w_shared, swa_w_q, swa_sinks, swa_w_o, rel_bias,
              mlp_w_up, mlp_w_down, ln_mix_g, ln_mix_b, ln_mlp_g, ln_mlp_b):
    alpha = (2 * DEPTH) ** 0.25
    n_a = DEPTH // 2
    pos = jnp.arange(x.shape[1])
    k_band = v_band = None
    for l in range(DEPTH):
        if l < n_a:
            y = _mla(x, mla_w_in[l], mla_g_q[l], mla_g_kv[l], mla_w_uq[l], mla_w_uk[l],
                     mla_w_uv[l], mla_w_o[l], pos)
        else:
            if l == n_a:
                k_band, v_band = _shared_kv(x, kv_w_shared)
            b = l - n_a
            y = _swa(x, swa_w_q[b], swa_sinks[b], swa_w_o[b], k_band, v_band, rel_bias)
        x = _layernorm(alpha * x + y, ln_mix_g[l], ln_mix_b[l])
        x = _layernorm(alpha * x + _mlp(x, mlp_w_up[l], mlp_w_down[l]), ln_mlp_g[l], ln_mlp_b[l])
    return x


import jax as _jax
import jax.numpy as _jnp

TWIN_FORMAT = 'train_step'
FWD_PARAMS = ['x', 'mla_w_in', 'mla_g_q', 'mla_g_kv', 'mla_w_uq', 'mla_w_uk', 'mla_w_uv', 'mla_w_o', 'kv_w_shared', 'swa_w_q', 'swa_sinks', 'swa_w_o', 'rel_bias', 'mlp_w_up', 'mlp_w_down', 'ln_mix_g', 'ln_mix_b', 'ln_mlp_g', 'ln_mlp_b']
TWIN_WEIGHTS = ['mla_w_in', 'mla_g_q', 'mla_g_kv', 'mla_w_uq', 'mla_w_uk', 'mla_w_uv', 'mla_w_o', 'kv_w_shared', 'swa_w_q', 'swa_sinks', 'swa_w_o', 'rel_bias', 'mlp_w_up', 'mlp_w_down', 'ln_mix_g', 'ln_mix_b', 'ln_mlp_g', 'ln_mlp_b']
TWIN_DIFF_INPUT = 'x'
TWIN_INPUTS = ['x', 'mla_w_in', 'mla_g_q', 'mla_g_kv', 'mla_w_uq', 'mla_w_uk', 'mla_w_uv', 'mla_w_o', 'kv_w_shared', 'swa_w_q', 'swa_sinks', 'swa_w_o', 'rel_bias', 'mlp_w_up', 'mlp_w_down', 'ln_mix_g', 'ln_mix_b', 'ln_mlp_g', 'ln_mlp_b', 'loss_target', 'm_mla_w_in', 'm_mla_g_q', 'm_mla_g_kv', 'm_mla_w_uq', 'm_mla_w_uk', 'm_mla_w_uv', 'm_mla_w_o', 'm_kv_w_shared', 'm_swa_w_q', 'm_swa_sinks', 'm_swa_w_o', 'm_rel_bias', 'm_mlp_w_up', 'm_mlp_w_down', 'm_ln_mix_g', 'm_ln_mix_b', 'm_ln_mlp_g', 'm_ln_mlp_b', 'v_mla_w_in', 'v_mla_g_q', 'v_mla_g_kv', 'v_mla_w_uq', 'v_mla_w_uk', 'v_mla_w_uv', 'v_mla_w_o', 'v_kv_w_shared', 'v_swa_w_q', 'v_swa_sinks', 'v_swa_w_o', 'v_rel_bias', 'v_mlp_w_up', 'v_mlp_w_down', 'v_ln_mix_g', 'v_ln_mix_b', 'v_ln_mlp_g', 'v_ln_mlp_b']
TWIN_OUTPUTS = ['loss', 'grad_x', 'grad_mla_w_in', 'grad_mla_g_q', 'grad_mla_g_kv', 'grad_mla_w_uq', 'grad_mla_w_uk', 'grad_mla_w_uv', 'grad_mla_w_o', 'grad_kv_w_shared', 'grad_swa_w_q', 'grad_swa_sinks', 'grad_swa_w_o', 'grad_rel_bias', 'grad_mlp_w_up', 'grad_mlp_w_down', 'grad_ln_mix_g', 'grad_ln_mix_b', 'grad_ln_mlp_g', 'grad_ln_mlp_b', 'delta_mla_w_in', 'delta_mla_g_q', 'delta_mla_g_kv', 'delta_mla_w_uq', 'delta_mla_w_uk', 'delta_mla_w_uv', 'delta_mla_w_o', 'delta_kv_w_shared', 'delta_swa_w_q', 'delta_swa_sinks', 'delta_swa_w_o', 'delta_rel_bias', 'delta_mlp_w_up', 'delta_mlp_w_down', 'delta_ln_mix_g', 'delta_ln_mix_b', 'delta_ln_mlp_g', 'delta_ln_mlp_b', 'new_m_mla_w_in', 'new_m_mla_g_q', 'new_m_mla_g_kv', 'new_m_mla_w_uq', 'new_m_mla_w_uk', 'new_m_mla_w_uv', 'new_m_mla_w_o', 'new_m_kv_w_shared', 'new_m_swa_w_q', 'new_m_swa_sinks', 'new_m_swa_w_o', 'new_m_rel_bias', 'new_m_mlp_w_up', 'new_m_mlp_w_down', 'new_m_ln_mix_g', 'new_m_ln_mix_b', 'new_m_ln_mlp_g', 'new_m_ln_mlp_b', 'new_v_mla_w_in', 'new_v_mla_g_q', 'new_v_mla_g_kv', 'new_v_mla_w_uq', 'new_v_mla_w_uk', 'new_v_mla_w_uv', 'new_v_mla_w_o', 'new_v_kv_w_shared', 'new_v_swa_w_q', 'new_v_swa_sinks', 'new_v_swa_w_o', 'new_v_rel_bias', 'new_v_mlp_w_up', 'new_v_mlp_w_down', 'new_v_ln_mix_g', 'new_v_ln_mix_b', 'new_v_ln_mlp_g', 'new_v_ln_mlp_b']
TWIN_LEAF_KINDS = {'loss': 'loss', 'grad_x': 'grad_x', 'grad_mla_w_in': 'grad_w', 'grad_mla_g_q': 'grad_w', 'grad_mla_g_kv': 'grad_w', 'grad_mla_w_uq': 'grad_w', 'grad_mla_w_uk': 'grad_w', 'grad_mla_w_uv': 'grad_w', 'grad_mla_w_o': 'grad_w', 'grad_kv_w_shared': 'grad_w', 'grad_swa_w_q': 'grad_w', 'grad_swa_sinks': 'grad_w', 'grad_swa_w_o': 'grad_w', 'grad_rel_bias': 'grad_w', 'grad_mlp_w_up': 'grad_w', 'grad_mlp_w_down': 'grad_w', 'grad_ln_mix_g': 'grad_w', 'grad_ln_mix_b': 'grad_w', 'grad_ln_mlp_g': 'grad_w', 'grad_ln_mlp_b': 'grad_w', 'delta_mla_w_in': 'delta_w', 'delta_mla_g_q': 'delta_w', 'delta_mla_g_kv': 'delta_w', 'delta_mla_w_uq': 'delta_w', 'delta_mla_w_uk': 'delta_w', 'delta_mla_w_uv': 'delta_w', 'delta_mla_w_o': 'delta_w', 'delta_kv_w_shared': 'delta_w', 'delta_swa_w_q': 'delta_w', 'delta_swa_sinks': 'delta_w', 'delta_swa_w_o': 'delta_w', 'delta_rel_bias': 'delta_w', 'delta_mlp_w_up': 'delta_w', 'delta_mlp_w_down': 'delta_w', 'delta_ln_mix_g': 'delta_w', 'delta_ln_mix_b': 'delta_w', 'delta_ln_mlp_g': 'delta_w', 'delta_ln_mlp_b': 'delta_w', 'new_m_mla_w_in': 'new_m', 'new_m_mla_g_q': 'new_m', 'new_m_mla_g_kv': 'new_m', 'new_m_mla_w_uq': 'new_m', 'new_m_mla_w_uk': 'new_m', 'new_m_mla_w_uv': 'new_m', 'new_m_mla_w_o': 'new_m', 'new_m_kv_w_shared': 'new_m', 'new_m_swa_w_q': 'new_m', 'new_m_swa_sinks': 'new_m', 'new_m_swa_w_o': 'new_m', 'new_m_rel_bias': 'new_m', 'new_m_mlp_w_up': 'new_m', 'new_m_mlp_w_down': 'new_m', 'new_m_ln_mix_g': 'new_m', 'new_m_ln_mix_b': 'new_m', 'new_m_ln_mlp_g': 'new_m', 'new_m_ln_mlp_b': 'new_m', 'new_v_mla_w_in': 'new_v', 'new_v_mla_g_q': 'new_v', 'new_v_mla_g_kv': 'new_v', 'new_v_mla_w_uq': 'new_v', 'new_v_mla_w_uk': 'new_v', 'new_v_mla_w_uv': 'new_v', 'new_v_mla_w_o': 'new_v', 'new_v_kv_w_shared': 'new_v', 'new_v_swa_w_q': 'new_v', 'new_v_swa_sinks': 'new_v', 'new_v_swa_w_o': 'new_v', 'new_v_rel_bias': 'new_v', 'new_v_mlp_w_up': 'new_v', 'new_v_mlp_w_down': 'new_v', 'new_v_ln_mix_g': 'new_v', 'new_v_ln_mix_b': 'new_v', 'new_v_ln_mlp_g': 'new_v', 'new_v_ln_mlp_b': 'new_v'}


def _forward(args):
    return _fwd_reference(*[args[k] for k in FWD_PARAMS])


def _output_shape():
    def fwd():
        inp = _fwd_setup_inputs(0)
        return _fwd_reference(*[inp[k] for k in FWD_PARAMS])
    out = _jax.eval_shape(fwd)
    return out.shape, out.dtype

N_MICROBATCH = 1
ADAM_LR = 0.001
ADAM_B1 = 0.9
ADAM_B2 = 0.999
ADAM_EPS = 1e-08
ADAM_WD = 0.01
ADAM_STEP = 10
PER_EXAMPLE_BATCH_AXIS = {'x': 0, 'loss_target': 0}
SHARED_INPUTS = []
_WEIGHT_DTYPES = {'mla_w_in': _jnp.float32, 'mla_g_q': _jnp.float32, 'mla_g_kv': _jnp.float32, 'mla_w_uq': _jnp.float32, 'mla_w_uk': _jnp.float32, 'mla_w_uv': _jnp.float32, 'mla_w_o': _jnp.float32, 'kv_w_shared': _jnp.float32, 'swa_w_q': _jnp.float32, 'swa_sinks': _jnp.float32, 'swa_w_o': _jnp.float32, 'rel_bias': _jnp.float32, 'mlp_w_up': _jnp.float32, 'mlp_w_down': _jnp.float32, 'ln_mix_g': _jnp.float32, 'ln_mix_b': _jnp.float32, 'ln_mlp_g': _jnp.float32, 'ln_mlp_b': _jnp.float32}
MOMENT_SCALE = {'mla_w_in': 1.895305e-02, 'mla_g_q': 1.450087e-02, 'mla_g_kv': 2.714606e-02, 'mla_w_uq': 6.956967e-03, 'mla_w_uk': 7.137665e-03, 'mla_w_uv': 2.111865e-02, 'mla_w_o': 2.103699e-02, 'kv_w_shared': 9.725129e-02, 'swa_w_q': 8.315028e-03, 'swa_sinks': 7.950159e-03, 'swa_w_o': 7.796726e-02, 'rel_bias': 1.135764e-02, 'mlp_w_up': 6.260068e-02, 'mlp_w_down': 4.634569e-01, 'ln_mix_g': 5.112976e+00, 'ln_mix_b': 1.483586e+00, 'ln_mlp_g': 4.598349e+01, 'ln_mlp_b': 1.030832e+01}


def _to_microbatches(a, axis):
    t = _jnp.moveaxis(a, axis, 0)
    t = t.reshape((N_MICROBATCH, t.shape[0] // N_MICROBATCH) + t.shape[1:])
    return _jnp.moveaxis(t, 1, axis + 1)


def setup_inputs(seed: int = 0) -> dict:
    inp = _fwd_setup_inputs(seed)
    key = _jax.random.fold_in(_jax.random.key(seed), 7919)
    shape, _ = _output_shape()
    out = dict(inp)
    out["loss_target"] = _jax.random.normal(_jax.random.fold_in(key, 0), shape, _jnp.float32)
    for i, name in enumerate(TWIN_WEIGHTS):
        w = inp[name].astype(_jnp.float32)
        if MOMENT_SCALE is None:
            s = _jnp.sqrt(_jnp.mean(_jnp.square(w)) + 1e-30)
        else:
            s = MOMENT_SCALE[name]
        km, kv = _jax.random.split(_jax.random.fold_in(key, i + 1))
        out[name] = w
        out["m_" + name] = s * _jax.random.normal(km, w.shape, _jnp.float32)
        out["v_" + name] = (s * s) * _jax.random.uniform(kv, w.shape, _jnp.float32, 0.5, 1.5)
    if N_MICROBATCH > 1:
        for name, axis in PER_EXAMPLE_BATCH_AXIS.items():
            out[name] = _to_microbatches(out[name], axis)
    return {'x': out['x'], 'mla_w_in': out['mla_w_in'], 'mla_g_q': out['mla_g_q'], 'mla_g_kv': out['mla_g_kv'], 'mla_w_uq': out['mla_w_uq'], 'mla_w_uk': out['mla_w_uk'], 'mla_w_uv': out['mla_w_uv'], 'mla_w_o': out['mla_w_o'], 'kv_w_shared': out['kv_w_shared'], 'swa_w_q': out['swa_w_q'], 'swa_sinks': out['swa_sinks'], 'swa_w_o': out['swa_w_o'], 'rel_bias': out['rel_bias'], 'mlp_w_up': out['mlp_w_up'], 'mlp_w_down': out['mlp_w_down'], 'ln_mix_g': out['ln_mix_g'], 'ln_mix_b': out['ln_mix_b'], 'ln_mlp_g': out['ln_mlp_g'], 'ln_mlp_b': out['ln_mlp_b'], 'loss_target': out['loss_target'], 'm_mla_w_in': out['m_mla_w_in'], 'm_mla_g_q': out['m_mla_g_q'], 'm_mla_g_kv': out['m_mla_g_kv'], 'm_mla_w_uq': out['m_mla_w_uq'], 'm_mla_w_uk': out['m_mla_w_uk'], 'm_mla_w_uv': out['m_mla_w_uv'], 'm_mla_w_o': out['m_mla_w_o'], 'm_kv_w_shared': out['m_kv_w_shared'], 'm_swa_w_q': out['m_swa_w_q'], 'm_swa_sinks': out['m_swa_sinks'], 'm_swa_w_o': out['m_swa_w_o'], 'm_rel_bias': out['m_rel_bias'], 'm_mlp_w_up': out['m_mlp_w_up'], 'm_mlp_w_down': out['m_mlp_w_down'], 'm_ln_mix_g': out['m_ln_mix_g'], 'm_ln_mix_b': out['m_ln_mix_b'], 'm_ln_mlp_g': out['m_ln_mlp_g'], 'm_ln_mlp_b': out['m_ln_mlp_b'], 'v_mla_w_in': out['v_mla_w_in'], 'v_mla_g_q': out['v_mla_g_q'], 'v_mla_g_kv': out['v_mla_g_kv'], 'v_mla_w_uq': out['v_mla_w_uq'], 'v_mla_w_uk': out['v_mla_w_uk'], 'v_mla_w_uv': out['v_mla_w_uv'], 'v_mla_w_o': out['v_mla_w_o'], 'v_kv_w_shared': out['v_kv_w_shared'], 'v_swa_w_q': out['v_swa_w_q'], 'v_swa_sinks': out['v_swa_sinks'], 'v_swa_w_o': out['v_swa_w_o'], 'v_rel_bias': out['v_rel_bias'], 'v_mlp_w_up': out['v_mlp_w_up'], 'v_mlp_w_down': out['v_mlp_w_down'], 'v_ln_mix_g': out['v_ln_mix_g'], 'v_ln_mix_b': out['v_ln_mix_b'], 'v_ln_mlp_g': out['v_ln_mlp_g'], 'v_ln_mlp_b': out['v_ln_mlp_b']}


def _loss(weights, diff, rest, loss_target):
    with _jax.named_scope("forward"):
        args = {**rest, TWIN_DIFF_INPUT: diff, **{k: w.astype(_WEIGHT_DTYPES[k]) for k, w in weights.items()}}
        y = _forward(args)
    with _jax.named_scope("loss_head"):
        err = _jnp.square(y.astype(_jnp.float32) - loss_target)
        return 0.5 * _jnp.sum(_jnp.mean(err, axis=-1)) if err.ndim else 0.5 * err


def _adamw(w, g, m, v):
    m = ADAM_B1 * m + (1.0 - ADAM_B1) * g
    v = ADAM_B2 * v + (1.0 - ADAM_B2) * _jnp.square(g)
    m_hat = m / (1.0 - ADAM_B1 ** ADAM_STEP)
    v_hat = v / (1.0 - ADAM_B2 ** ADAM_STEP)
    delta = -ADAM_LR * (m_hat / (_jnp.sqrt(v_hat) + ADAM_EPS) + ADAM_WD * w)
    return delta, m, v


def reference(x, mla_w_in, mla_g_q, mla_g_kv, mla_w_uq, mla_w_uk, mla_w_uv, mla_w_o, kv_w_shared, swa_w_q, swa_sinks, swa_w_o, rel_bias, mlp_w_up, mlp_w_down, ln_mix_g, ln_mix_b, ln_mlp_g, ln_mlp_b, loss_target, m_mla_w_in, m_mla_g_q, m_mla_g_kv, m_mla_w_uq, m_mla_w_uk, m_mla_w_uv, m_mla_w_o, m_kv_w_shared, m_swa_w_q, m_swa_sinks, m_swa_w_o, m_rel_bias, m_mlp_w_up, m_mlp_w_down, m_ln_mix_g, m_ln_mix_b, m_ln_mlp_g, m_ln_mlp_b, v_mla_w_in, v_mla_g_q, v_mla_g_kv, v_mla_w_uq, v_mla_w_uk, v_mla_w_uv, v_mla_w_o, v_kv_w_shared, v_swa_w_q, v_swa_sinks, v_swa_w_o, v_rel_bias, v_mlp_w_up, v_mlp_w_down, v_ln_mix_g, v_ln_mix_b, v_ln_mlp_g, v_ln_mlp_b):
    given = dict(x=x, mla_w_in=mla_w_in, mla_g_q=mla_g_q, mla_g_kv=mla_g_kv, mla_w_uq=mla_w_uq, mla_w_uk=mla_w_uk, mla_w_uv=mla_w_uv, mla_w_o=mla_w_o, kv_w_shared=kv_w_shared, swa_w_q=swa_w_q, swa_sinks=swa_sinks, swa_w_o=swa_w_o, rel_bias=rel_bias, mlp_w_up=mlp_w_up, mlp_w_down=mlp_w_down, ln_mix_g=ln_mix_g, ln_mix_b=ln_mix_b, ln_mlp_g=ln_mlp_g, ln_mlp_b=ln_mlp_b, loss_target=loss_target, m_mla_w_in=m_mla_w_in, m_mla_g_q=m_mla_g_q, m_mla_g_kv=m_mla_g_kv, m_mla_w_uq=m_mla_w_uq, m_mla_w_uk=m_mla_w_uk, m_mla_w_uv=m_mla_w_uv, m_mla_w_o=m_mla_w_o, m_kv_w_shared=m_kv_w_shared, m_swa_w_q=m_swa_w_q, m_swa_sinks=m_swa_sinks, m_swa_w_o=m_swa_w_o, m_rel_bias=m_rel_bias, m_mlp_w_up=m_mlp_w_up, m_mlp_w_down=m_mlp_w_down, m_ln_mix_g=m_ln_mix_g, m_ln_mix_b=m_ln_mix_b, m_ln_mlp_g=m_ln_mlp_g, m_ln_mlp_b=m_ln_mlp_b, v_mla_w_in=v_mla_w_in, v_mla_g_q=v_mla_g_q, v_mla_g_kv=v_mla_g_kv, v_mla_w_uq=v_mla_w_uq, v_mla_w_uk=v_mla_w_uk, v_mla_w_uv=v_mla_w_uv, v_mla_w_o=v_mla_w_o, v_kv_w_shared=v_kv_w_shared, v_swa_w_q=v_swa_w_q, v_swa_sinks=v_swa_sinks, v_swa_w_o=v_swa_w_o, v_rel_bias=v_rel_bias, v_mlp_w_up=v_mlp_w_up, v_mlp_w_down=v_mlp_w_down, v_ln_mix_g=v_ln_mix_g, v_ln_mix_b=v_ln_mix_b, v_ln_mlp_g=v_ln_mlp_g, v_ln_mlp_b=v_ln_mlp_b)
    weights = {n: given[n] for n in TWIN_WEIGHTS}
    shared = {n: given[n] for n in SHARED_INPUTS}
    per_example = {n: given[n] for n in ['x']}
    grad_fn = _jax.value_and_grad(_loss, argnums=(0, 1))

    def one_microbatch(ex, loss_target):
        ex = dict(ex)
        diff = ex.pop(TWIN_DIFF_INPUT)
        return grad_fn(weights, diff, {**shared, **ex}, loss_target)

    if N_MICROBATCH == 1:
        loss, (grad_w, grad_x) = one_microbatch(per_example, given["loss_target"])
    else:
        def body(carry, xs):
            loss_sum, grad_sum = carry
            l_k, (gw_k, gx_k) = one_microbatch(xs[0], xs[1])
            with _jax.named_scope("update"):
                return (loss_sum + l_k, _jax.tree.map(_jnp.add, grad_sum, gw_k)), gx_k

        init = (_jnp.zeros((), _jnp.float32), _jax.tree.map(_jnp.zeros_like, weights))
        (loss, grad_w), grad_x = _jax.lax.scan(body, init, (per_example, given["loss_target"]))
    with _jax.named_scope("update"):
        delta_w, new_m, new_v = {}, {}, {}
        for n in TWIN_WEIGHTS:
            delta_w[n], new_m[n], new_v[n] = _adamw(weights[n], grad_w[n], given["m_" + n], given["v_" + n])
    return (loss, grad_x, *[grad_w[n] for n in TWIN_WEIGHTS], *[delta_w[n] for n in TWIN_WEIGHTS],
            *[new_m[n] for n in TWIN_WEIGHTS], *[new_v[n] for n in TWIN_WEIGHTS])
```

```python
import functools
import math

import numpy as np
import jax
import jax.numpy as jnp
from jax import lax
from jax.experimental import pallas as pl
from jax.experimental.pallas import tpu as pltpu

F32 = jnp.float32
BF16 = jnp.bfloat16
MESH = pl.DeviceIdType.MESH

D = 1024
DFF = 4096
H = 8
NOPE = 128
ROPE = 64
QR = 384
KVR = 256
RP = 128
KD = KVR + RP
HW = 768
QH = 16
KVH = 4
HD = 64
G = QH // KVH
WIN = 128
NBKT = 32
ALPHA = 4.0 ** 0.25
LN_EPS = 1e-5
RMS_EPS = 1e-6
MLA_SCALE = (NOPE + ROPE) ** -0.5
SWA_SCALE = HD ** -0.5
NEG = -1e30
LR, B1, B2, ADAM_EPS, WD, STEP = 0.001, 0.9, 0.999, 1e-8, 0.01, 10

VMEM_LIMIT = 48 * 1024 * 1024

NN = (((1,), (0,)), ((), ()))
NT = (((1,), (1,)), ((), ()))
TN = (((0,), (0,)), ((), ()))

PACK = (("mlp_w_up", 2048), ("mlp_w_down", 2048), ("mla_w_o", 256), ("swa_w_q", 256), ("swa_w_o", 256),
        ("kv_w_shared", 128), ("mla_w_in", 176), ("mla_w_uq", 144), ("mla_w_uk", 64), ("mla_w_uv", 64))
PACK_ROWS = sum(r for _, r in PACK)
HALF_ROWS = PACK_ROWS // 2
SMALL_ROWS = 16


def _cp(**kw):
    return pltpu.CompilerParams(vmem_limit_bytes=VMEM_LIMIT, **kw)


def _tile(n, pref):
    t = min(n, pref)
    while n % t:
        t -= 128
    return t


def _dot(a, b, dims):
    return lax.dot_general(a, b, dims, preferred_element_type=F32)


def _mm(a, b, mode, name, out_dtype=F32, out_t=False, addend=None, add_scale=1.0, relu2=False, gate_u=None,
        tm=1024, tn=1024, tk=512):
    if mode == "nn":
        (m, k), (k2, n) = a.shape, b.shape
    elif mode == "nt":
        (m, k), (n, k2) = a.shape, b.shape
    else:
        (k, m), (k2, n) = a.shape, b.shape
    assert k == k2, (name, a.shape, b.shape)
    tm, tn, tk = _tile(m, tm), _tile(n, tn), _tile(k, tk)
    nk = k // tk
    dims = {"nn": NN, "nt": NT, "tn": TN}[mode]
    if mode == "tn":
        a_spec = pl.BlockSpec((tk, tm), lambda i, j, kk: (kk, i))
    else:
        a_spec = pl.BlockSpec((tm, tk), lambda i, j, kk: (i, kk))
    if mode == "nt":
        b_spec = pl.BlockSpec((tn, tk), lambda i, j, kk: (j, kk))
    else:
        b_spec = pl.BlockSpec((tk, tn), lambda i, j, kk: (kk, j))
    mn_spec = pl.BlockSpec((tm, tn), lambda i, j, kk: (i, j))
    ins, in_specs = [a, b], [a_spec, b_spec]
    if addend is not None:
        ins.append(addend)
        in_specs.append(mn_spec)
    if gate_u is not None:
        ins.append(gate_u)
        in_specs.append(mn_spec)
    if out_t:
        out_shape = [jax.ShapeDtypeStruct((n, m), out_dtype)]
        out_specs = [pl.BlockSpec((tn, tm), lambda i, j, kk: (j, i))]
    else:
        out_shape = [jax.ShapeDtypeStruct((m, n), out_dtype)]
        out_specs = [mn_spec]
    if relu2:
        out_shape.append(jax.ShapeDtypeStruct((m, n), BF16))
        out_specs.append(mn_spec)
    has_add, has_gate = addend is not None, gate_u is not None

    def kern(*refs):
        a_ref, b_ref = refs[0], refs[1]
        pos = 2
        add_ref = gate_ref = None
        if has_add:
            add_ref = refs[pos]
            pos += 1
        if has_gate:
            gate_ref = refs[pos]
            pos += 1
        o_ref = refs[pos]
        a2_ref = refs[pos + 1] if relu2 else None
        acc = refs[-1]
        kk = pl.program_id(2)

        @pl.when(kk == 0)
        def _():
            acc[...] = jnp.zeros_like(acc)

        acc[...] += _dot(a_ref[...].astype(BF16), b_ref[...].astype(BF16), dims)

        @pl.when(kk == nk - 1)
        def _():
            r = acc[...]
            if has_add:
                r = r + add_scale * add_ref[...].astype(F32)
            if has_gate:
                r = r * (2.0 * jnp.maximum(gate_ref[...], 0.0))
            if relu2:
                hh = jnp.maximum(r, 0.0)
                a2_ref[...] = (hh * hh).astype(BF16)
            if out_t:
                r = r.T
            o_ref[...] = r.astype(out_dtype)

    outs = pl.pallas_call(
        kern, out_shape=out_shape, grid=(m // tm, n // tn, nk), in_specs=in_specs, out_specs=out_specs,
        scratch_shapes=[pltpu.VMEM((tm, tn), F32)], name=name, compiler_params=_cp())(*ins)
    return outs if relu2 else outs[0]


def _add_ln(xres, y, g, b, name, tm=256):
    t = xres.shape[0]
    tm = min(tm, t)

    def kern(x_ref, y_ref, g_ref, b_ref, o_ref, ob_ref, xh_ref, r_ref):
        z = ALPHA * x_ref[...] + y_ref[...]
        mu = jnp.mean(z, axis=-1, keepdims=True)
        zc = z - mu
        var = jnp.mean(zc * zc, axis=-1, keepdims=True)
        r = lax.rsqrt(var + LN_EPS)
        xh = zc * r
        o = xh * g_ref[...] + b_ref[...]
        o_ref[...] = o
        ob_ref[...] = o.astype(BF16)
        xh_ref[...] = xh
        r_ref[...] = r

    row = pl.BlockSpec((tm, D), lambda i: (i, 0))
    vec = pl.BlockSpec((1, D), lambda i: (0, 0))
    st = pl.BlockSpec((tm, 1), lambda i: (i, 0))
    return pl.pallas_call(
        kern, grid=(t // tm,), in_specs=[row, row, vec, vec], out_specs=[row, row, row, st],
        out_shape=[jax.ShapeDtypeStruct((t, D), F32), jax.ShapeDtypeStruct((t, D), BF16),
                   jax.ShapeDtypeStruct((t, D), F32), jax.ShapeDtypeStruct((t, 1), F32)],
        name=name, compiler_params=_cp())(xres, y, g.reshape(1, D), b.reshape(1, D))


def _ln_bwd(dout, xhat, rstd, g, name, tm=256):
    t = dout.shape[0]
    tm = min(tm, t)

    def kern(do_ref, xh_ref, r_ref, g_ref, dz_ref, dzb_ref, dg_ref, db_ref):
        @pl.when(pl.program_id(0) == 0)
        def _():
            dg_ref[...] = jnp.zeros_like(dg_ref)
            db_ref[...] = jnp.zeros_like(db_ref)

        do = do_ref[...]
        xh = xh_ref[...]
        dxh = do * g_ref[...]
        m1 = jnp.mean(dxh, axis=-1, keepdims=True)
        m2 = jnp.mean(dxh * xh, axis=-1, keepdims=True)
        dz = r_ref[...] * (dxh - m1 - xh * m2)
        dz_ref[...] = dz
        dzb_ref[...] = dz.astype(BF16)
        dg_ref[...] += jnp.sum(do * xh, axis=0, keepdims=True)
        db_ref[...] += jnp.sum(do, axis=0, keepdims=True)

    row = pl.BlockSpec((tm, D), lambda i: (i, 0))
    vec = pl.BlockSpec((1, D), lambda i: (0, 0))
    st = pl.BlockSpec((tm, 1), lambda i: (i, 0))
    return pl.pallas_call(
        kern, grid=(t // tm,), in_specs=[row, row, st, vec], out_specs=[row, row, vec, vec],
        out_shape=[jax.ShapeDtypeStruct((t, D), F32), jax.ShapeDtypeStruct((t, D), BF16),
                   jax.ShapeDtypeStruct((1, D), F32), jax.ShapeDtypeStruct((1, D), F32)],
        name=name, compiler_params=_cp())(dout, xhat, rstd, g.reshape(1, D))


def _loss_grad(y, target, name="loss_grad", tm=256):
    t = y.shape[0]
    tm = min(tm, t)

    def kern(y_ref, t_ref, d_ref, l_ref):
        @pl.when(pl.program_id(0) == 0)
        def _():
            l_ref[...] = jnp.zeros_like(l_ref)

        e = y_ref[...] - t_ref[...]
        d_ref[...] = e * (1.0 / D)
        l_ref[...] += jnp.sum(e * e, axis=0, keepdims=True)

    row = pl.BlockSpec((tm, D), lambda i: (i, 0))
    vec = pl.BlockSpec((1, D), lambda i: (0, 0))
    return pl.pallas_call(
        kern, grid=(t // tm,), in_specs=[row, row], out_specs=[row, vec],
        out_shape=[jax.ShapeDtypeStruct((t, D), F32), jax.ShapeDtypeStruct((1, D), F32)],
        name=name, compiler_params=_cp())(y, target)


def _rope_tables(t):
    half = ROPE // 2
    inv = 10000.0 ** (-jnp.arange(half, dtype=F32) / half)
    ang = jnp.arange(t).astype(F32)[:, None] * inv[None, :]
    cos, sin = jnp.cos(ang), jnp.sin(ang)
    z = jnp.zeros((t, RP - ROPE), F32)
    return jnp.concatenate([cos, cos, z], axis=1), jnp.concatenate([-sin, sin, z], axis=1)


def _swap_halves(x):
    lane = lax.broadcasted_iota(jnp.int32, x.shape, 1)
    return jnp.where(lane < ROPE // 2, pltpu.roll(x, RP - ROPE // 2, 1), pltpu.roll(x, ROPE // 2, 1))


def _rope(x, cos, sin):
    return x * cos + _swap_halves(x) * sin


def _rope_t(gy, cos, sin):
    return gy * cos + _swap_halves(gy * sin)


def _mla_pre(hh, g_q, g_kv, cos, sin, tm=256):
    t = hh.shape[0]
    tm = min(tm, t)

    def kern(h_ref, gq_ref, gkv_ref, c_ref, s_ref, cq_ref, k_ref):
        xq = h_ref[:, 0:QR]
        rq = lax.rsqrt(jnp.mean(xq * xq, axis=-1, keepdims=True) + RMS_EPS)
        cq_ref[...] = (xq * rq * gq_ref[...]).astype(BF16)
        xk = h_ref[:, QR:QR + KVR]
        rk = lax.rsqrt(jnp.mean(xk * xk, axis=-1, keepdims=True) + RMS_EPS)
        k_ref[:, 0:KVR] = (xk * rk * gkv_ref[...]).astype(BF16)
        k_ref[:, KVR:KD] = _rope(h_ref[:, QR + KVR:HW], c_ref[...], s_ref[...]).astype(BF16)

    return pl.pallas_call(
        kern, grid=(t // tm,),
        in_specs=[pl.BlockSpec((tm, HW), lambda i: (i, 0)), pl.BlockSpec((1, QR), lambda i: (0, 0)),
                  pl.BlockSpec((1, KVR), lambda i: (0, 0)), pl.BlockSpec((tm, RP), lambda i: (i, 0)),
                  pl.BlockSpec((tm, RP), lambda i: (i, 0))],
        out_specs=[pl.BlockSpec((tm, QR), lambda i: (i, 0)), pl.BlockSpec((tm, KD), lambda i: (i, 0))],
        out_shape=[jax.ShapeDtypeStruct((t, QR), BF16), jax.ShapeDtypeStruct((t, KD), BF16)],
        name="mla_pre", compiler_params=_cp())(hh, g_q.reshape(1, QR), g_kv.reshape(1, KVR), cos, sin)


def _mla_pre_bwd(hh, dcq, dk, g_q, g_kv, cos, sin, tm=256):
    t = hh.shape[0]
    tm = min(tm, t)

    def rms_bwd(x, dy, g):
        r = lax.rsqrt(jnp.mean(x * x, axis=-1, keepdims=True) + RMS_EPS)
        gdy = dy * g
        dx = r * gdy - x * (r * r * r) * jnp.mean(gdy * x, axis=-1, keepdims=True)
        return dx, jnp.sum(dy * x * r, axis=0, keepdims=True)

    def kern(h_ref, dcq_ref, dk_ref, gq_ref, gkv_ref, c_ref, s_ref, dh_ref, dgq_ref, dgkv_ref):
        @pl.when(pl.program_id(0) == 0)
        def _():
            dgq_ref[...] = jnp.zeros_like(dgq_ref)
            dgkv_ref[...] = jnp.zeros_like(dgkv_ref)

        dxq, dgq = rms_bwd(h_ref[:, 0:QR], dcq_ref[...], gq_ref[...])
        dxk, dgk = rms_bwd(h_ref[:, QR:QR + KVR], dk_ref[:, 0:KVR], gkv_ref[...])
        dh_ref[:, 0:QR] = dxq.astype(BF16)
        dh_ref[:, QR:QR + KVR] = dxk.astype(BF16)
        dh_ref[:, QR + KVR:HW] = _rope_t(dk_ref[:, KVR:KD], c_ref[...], s_ref[...]).astype(BF16)
        dgq_ref[...] += dgq
        dgkv_ref[...] += dgk

    return pl.pallas_call(
        kern, grid=(t // tm,),
        in_specs=[pl.BlockSpec((tm, HW), lambda i: (i, 0)), pl.BlockSpec((tm, QR), lambda i: (i, 0)),
                  pl.BlockSpec((tm, KD), lambda i: (i, 0)), pl.BlockSpec((1, QR), lambda i: (0, 0)),
                  pl.BlockSpec((1, KVR), lambda i: (0, 0)), pl.BlockSpec((tm, RP), lambda i: (i, 0)),
                  pl.BlockSpec((tm, RP), lambda i: (i, 0))],
        out_specs=[pl.BlockSpec((tm, HW), lambda i: (i, 0)), pl.BlockSpec((1, QR), lambda i: (0, 0)),
                   pl.BlockSpec((1, KVR), lambda i: (0, 0))],
        out_shape=[jax.ShapeDtypeStruct((t, HW), BF16), jax.ShapeDtypeStruct((1, QR), F32),
                   jax.ShapeDtypeStruct((1, KVR), F32)],
        name="mla_pre_bwd", compiler_params=_cp())(hh, dcq, dk, g_q.reshape(1, QR), g_kv.reshape(1, KVR), cos, sin)


def _q_prep(q2, wuk_t, cos, sin, tm=256):
    t = q2.shape[0]
    tm = min(tm, t)

    def kern(q_ref, w_ref, c_ref, s_ref, o_ref):
        cos_, sin_ = c_ref[...], s_ref[...]
        for h in range(H):
            qn = q_ref[:, h * NOPE:(h + 1) * NOPE].astype(BF16)
            o_ref[:, h * KD:h * KD + KVR] = _dot(qn, w_ref[h], NN).astype(BF16)
            qr = q_ref[:, H * NOPE + h * RP:H * NOPE + (h + 1) * RP]
            o_ref[:, h * KD + KVR:(h + 1) * KD] = _rope(qr, cos_, sin_).astype(BF16)

    return pl.pallas_call(
        kern, grid=(t // tm,),
        in_specs=[pl.BlockSpec((tm, 2 * H * NOPE), lambda i: (i, 0)), pl.BlockSpec((H, NOPE, KVR), lambda i: (0, 0, 0)),
                  pl.BlockSpec((tm, RP), lambda i: (i, 0)), pl.BlockSpec((tm, RP), lambda i: (i, 0))],
        out_specs=pl.BlockSpec((tm, H * KD), lambda i: (i, 0)),
        out_shape=jax.ShapeDtypeStruct((t, H * KD), BF16),
        name="q_prep", compiler_params=_cp())(q2, wuk_t, cos, sin)


def _q_prep_bwd(dq_cat, q2, wuk_h, cos, sin, tm=256):
    t = q2.shape[0]
    tm = min(tm, t)

    def kern(dq_ref, q_ref, w_ref, c_ref, s_ref, o_ref, dw_ref):
        @pl.when(pl.program_id(0) == 0)
        def _():
            dw_ref[...] = jnp.zeros_like(dw_ref)

        cos_, sin_ = c_ref[...], s_ref[...]
        for h in range(H):
            dql = dq_ref[:, h * KD:h * KD + KVR].astype(BF16)
            o_ref[:, h * NOPE:(h + 1) * NOPE] = _dot(dql, w_ref[h], NN).astype(BF16)
            dqr = dq_ref[:, h * KD + KVR:(h + 1) * KD]
            o_ref[:, H * NOPE + h * RP:H * NOPE + (h + 1) * RP] = _rope_t(dqr, cos_, sin_).astype(BF16)
            qn = q_ref[:, h * NOPE:(h + 1) * NOPE].astype(BF16)
            dw_ref[h] += _dot(qn, dql, TN)

    return pl.pallas_call(
        kern, grid=(t // tm,),
        in_specs=[pl.BlockSpec((tm, H * KD), lambda i: (i, 0)), pl.BlockSpec((tm, 2 * H * NOPE), lambda i: (i, 0)),
                  pl.BlockSpec((H, KVR, NOPE), lambda i: (0, 0, 0)),
                  pl.BlockSpec((tm, RP), lambda i: (i, 0)), pl.BlockSpec((tm, RP), lambda i: (i, 0))],
        out_specs=[pl.BlockSpec((tm, 2 * H * NOPE), lambda i: (i, 0)), pl.BlockSpec((H, NOPE, KVR), lambda i: (0, 0, 0))],
        out_shape=[jax.ShapeDtypeStruct((t, 2 * H * NOPE), BF16), jax.ShapeDtypeStruct((H, NOPE, KVR), F32)],
        name="q_prep_bwd", compiler_params=_cp())(dq_cat, q2, wuk_h, cos, sin)


def _o_up(o_lat, wuv_h, tm=256):
    t = o_lat.shape[0]
    tm = min(tm, t)

    def kern(x_ref, w_ref, o_ref):
        for h in range(H):
            xl = x_ref[:, h * KVR:(h + 1) * KVR].astype(BF16)
            o_ref[:, h * NOPE:(h + 1) * NOPE] = _dot(xl, w_ref[h], NN).astype(BF16)

    return pl.pallas_call(
        kern, grid=(t // tm,),
        in_specs=[pl.BlockSpec((tm, H * KVR), lambda i: (i, 0)), pl.BlockSpec((H, KVR, NOPE), lambda i: (0, 0, 0))],
        out_specs=pl.BlockSpec((tm, H * NOPE), lambda i: (i, 0)),
        out_shape=jax.ShapeDtypeStruct((t, H * NOPE), BF16),
        name="o_up", compiler_params=_cp())(o_lat, wuv_h)


def _o_up_bwd(do, o_lat, wuv_h, tm=256):
    t = do.shape[0]
    tm = min(tm, t)

    def kern(do_ref, x_ref, w_ref, dx_ref, dw_ref):
        @pl.when(pl.program_id(0) == 0)
        def _():
            dw_ref[...] = jnp.zeros_like(dw_ref)

        for h in range(H):
            dh_ = do_ref[:, h * NOPE:(h + 1) * NOPE]
            dx_ref[:, h * KVR:(h + 1) * KVR] = _dot(dh_, w_ref[h], NT)
            xl = x_ref[:, h * KVR:(h + 1) * KVR].astype(BF16)
            dw_ref[h] += _dot(xl, dh_, TN)

    return pl.pallas_call(
        kern, grid=(t // tm,),
        in_specs=[pl.BlockSpec((tm, H * NOPE), lambda i: (i, 0)), pl.BlockSpec((tm, H * KVR), lambda i: (i, 0)),
                  pl.BlockSpec((H, KVR, NOPE), lambda i: (0, 0, 0))],
        out_specs=[pl.BlockSpec((tm, H * KVR), lambda i: (i, 0)), pl.BlockSpec((H, KVR, NOPE), lambda i: (0, 0, 0))],
        out_shape=[jax.ShapeDtypeStruct((t, H * KVR), F32), jax.ShapeDtypeStruct((H, KVR, NOPE), F32)],
        name="o_up_bwd", compiler_params=_cp())(do, o_lat, wuv_h)


def _causal_pairs(nq):
    return [(i, j) for i in range(nq) for j in range(i + 1)]


def _diag_mask(s, bq):
    row = lax.broadcasted_iota(jnp.int32, (bq, bq), 0)
    col = lax.broadcasted_iota(jnp.int32, (bq, bq), 1)
    return jnp.where(col <= row, s, NEG)


def _flash_fwd(qcat, kc, bq):
    t = kc.shape[0]
    nq = t // bq
    pairs = _causal_pairs(nq)
    itab = jnp.asarray(np.array([p[0] for p in pairs], np.int32))
    jtab = jnp.asarray(np.array([p[1] for p in pairs], np.int32))

    def kern(it, jt, q_ref, k_ref, o_ref, lse_ref, m_sc, l_sc, acc_sc):
        st = pl.program_id(1)
        i, j = it[st], jt[st]

        @pl.when(j == 0)
        def _():
            m_sc[...] = jnp.full_like(m_sc, NEG)
            l_sc[...] = jnp.zeros_like(l_sc)
            acc_sc[...] = jnp.zeros_like(acc_sc)

        def update(masked):
            k = k_ref[...]
            s = _dot(q_ref[...], k, NT) * MLA_SCALE
            if masked:
                s = _diag_mask(s, bq)
            m_old = m_sc[...]
            m_new = jnp.maximum(m_old, jnp.max(s, axis=-1, keepdims=True))
            a = jnp.exp(m_old - m_new)
            p = jnp.exp(s - m_new)
            l_sc[...] = a * l_sc[...] + jnp.sum(p, axis=-1, keepdims=True)
            acc_sc[...] = a * acc_sc[...] + _dot(p.astype(BF16), k[:, 0:KVR], NN)
            m_sc[...] = m_new

        @pl.when(j < i)
        def _():
            update(False)

        @pl.when(j == i)
        def _():
            update(True)
            l = l_sc[...]
            o_ref[...] = acc_sc[...] / l
            lse_ref[0] = m_sc[...] + jnp.log(l)

    gs = pltpu.PrefetchScalarGridSpec(
        num_scalar_prefetch=2, grid=(H, len(pairs)),
        in_specs=[pl.BlockSpec((bq, KD), lambda h, s, it, jt: (it[s], h)),
                  pl.BlockSpec((bq, KD), lambda h, s, it, jt: (jt[s], 0))],
        out_specs=[pl.BlockSpec((bq, KVR), lambda h, s, it, jt: (it[s], h)),
                   pl.BlockSpec((1, bq, 1), lambda h, s, it, jt: (h, it[s], 0))],
        scratch_shapes=[pltpu.VMEM((bq, 1), F32), pltpu.VMEM((bq, 1), F32), pltpu.VMEM((bq, KVR), F32)])
    return pl.pallas_call(
        kern, grid_spec=gs,
        out_shape=[jax.ShapeDtypeStruct((t, H * KVR), F32), jax.ShapeDtypeStruct((H, t, 1), F32)],
        name="mla_flash_fwd", compiler_params=_cp())(itab, jtab, qcat, kc)


def _flash_dq(qcat, kc, do_lat, o_lat, lse, bq):
    t = kc.shape[0]
    nq = t // bq
    pairs = _causal_pairs(nq)
    itab = jnp.asarray(np.array([p[0] for p in pairs], np.int32))
    jtab = jnp.asarray(np.array([p[1] for p in pairs], np.int32))

    def kern(it, jt, q_ref, k_ref, do_ref, o_ref, lse_ref, dq_ref, acc_sc, dl_sc):
        st = pl.program_id(1)
        i, j = it[st], jt[st]

        @pl.when(j == 0)
        def _():
            acc_sc[...] = jnp.zeros_like(acc_sc)
            dl_sc[...] = jnp.sum(do_ref[...] * o_ref[...], axis=-1, keepdims=True)

        def update(masked):
            k = k_ref[...]
            s = _dot(q_ref[...], k, NT) * MLA_SCALE
            if masked:
                s = _diag_mask(s, bq)
            p = jnp.exp(s - lse_ref[0])
            dp = _dot(do_ref[...].astype(BF16), k[:, 0:KVR], NT)
            ds = p * (dp - dl_sc[...]) * MLA_SCALE
            acc_sc[...] += _dot(ds.astype(BF16), k, NN)

        @pl.when(j < i)
        def _():
            update(False)

        @pl.when(j == i)
        def _():
            update(True)
            dq_ref[...] = acc_sc[...]

    gs = pltpu.PrefetchScalarGridSpec(
        num_scalar_prefetch=2, grid=(H, len(pairs)),
        in_specs=[pl.BlockSpec((bq, KD), lambda h, s, it, jt: (it[s], h)),
                  pl.BlockSpec((bq, KD), lambda h, s, it, jt: (jt[s], 0)),
                  pl.BlockSpec((bq, KVR), lambda h, s, it, jt: (it[s], h)),
                  pl.BlockSpec((bq, KVR), lambda h, s, it, jt: (it[s], h)),
                  pl.BlockSpec((1, bq, 1), lambda h, s, it, jt: (h, it[s], 0))],
        out_specs=pl.BlockSpec((bq, KD), lambda h, s, it, jt: (it[s], h)),
        scratch_shapes=[pltpu.VMEM((bq, KD), F32), pltpu.VMEM((bq, 1), F32)])
    return pl.pallas_call(
        kern, grid_spec=gs, out_shape=jax.ShapeDtypeStruct((t, H * KD), F32),
        name="mla_flash_dq", compiler_params=_cp())(itab, jtab, qcat, kc, do_lat, o_lat, lse)


def _flash_dkv(qcat, kc, do_lat, o_lat, lse, bq):
    t = kc.shape[0]
    nq = t // bq
    steps = [(j, h, i) for j in range(nq) for h in range(H) for i in range(j, nq)]
    jtab = jnp.asarray(np.array([s[0] for s in steps], np.int32))
    htab = jnp.asarray(np.array([s[1] for s in steps], np.int32))
    itab = jnp.asarray(np.array([s[2] for s in steps], np.int32))

    def kern(jt, ht, it, q_ref, k_ref, do_ref, o_ref, lse_ref, dk_ref, dk_sc, dv_sc):
        st = pl.program_id(0)
        j, h, i = jt[st], ht[st], it[st]

        @pl.when((h == 0) & (i == j))
        def _():
            dk_sc[...] = jnp.zeros_like(dk_sc)
            dv_sc[...] = jnp.zeros_like(dv_sc)

        def update(masked):
            k = k_ref[...]
            q = q_ref[...]
            do = do_ref[...]
            s = _dot(q, k, NT) * MLA_SCALE
            if masked:
                s = _diag_mask(s, bq)
            p = jnp.exp(s - lse_ref[0])
            dob = do.astype(BF16)
            dv_sc[...] += _dot(p.astype(BF16), dob, TN)
            dp = _dot(dob, k[:, 0:KVR], NT)
            dl = jnp.sum(do * o_ref[...], axis=-1, keepdims=True)
            ds = p * (dp - dl) * MLA_SCALE
            dk_sc[...] += _dot(ds.astype(BF16), q, TN)

        @pl.when(i > j)
        def _():
            update(False)

        @pl.when(i == j)
        def _():
            update(True)

        @pl.when((h == H - 1) & (i == nq - 1))
        def _():
            dk_ref[:, 0:KVR] = dk_sc[:, 0:KVR] + dv_sc[...]
            dk_ref[:, KVR:KD] = dk_sc[:, KVR:KD]

    gs = pltpu.PrefetchScalarGridSpec(
        num_scalar_prefetch=3, grid=(len(steps),),
        in_specs=[pl.BlockSpec((bq, KD), lambda s, jt, ht, it: (it[s], ht[s])),
                  pl.BlockSpec((bq, KD), lambda s, jt, ht, it: (jt[s], 0)),
                  pl.BlockSpec((bq, KVR), lambda s, jt, ht, it: (it[s], ht[s])),
                  pl.BlockSpec((bq, KVR), lambda s, jt, ht, it: (it[s], ht[s])),
                  pl.BlockSpec((1, bq, 1), lambda s, jt, ht, it: (ht[s], it[s], 0))],
        out_specs=pl.BlockSpec((bq, KD), lambda s, jt, ht, it: (jt[s], 0)),
        scratch_shapes=[pltpu.VMEM((bq, KD), F32), pltpu.VMEM((bq, KVR), F32)])
    return pl.pallas_call(
        kern, grid_spec=gs, out_shape=jax.ShapeDtypeStruct((t, KD), F32),
        name="mla_flash_dkv", compiler_params=_cp())(jtab, htab, itab, qcat, kc, do_lat, o_lat, lse)


def _bucket_table():
    d = np.arange(WIN)
    max_exact = NBKT // 2
    nf = np.maximum(d, 1).astype(np.float32)
    large = max_exact + (np.log(nf / np.float32(max_exact)) / np.float32(math.log(WIN / max_exact))
                         * np.float32(NBKT - max_exact)).astype(np.int32)
    large = np.minimum(large, NBKT - 1)
    bucket = np.where(d < max_exact, d, large).astype(np.int32)
    jj = np.arange(2 * WIN)[:, None]
    ii = np.arange(WIN)[None, :]
    dist = ii + WIN - jj
    valid = (dist >= 0) & (dist < WIN)
    return np.where(valid, bucket[np.clip(dist, 0, WIN - 1)], -1).astype(np.int32)


def _bias_build(rel_bias, bkt):
    def kern(bk_ref, rb_ref, o_ref):
        bk = bk_ref[...]
        for hd in range(QH):
            acc = jnp.full((2 * WIN, WIN), NEG, F32)
            for b in range(NBKT):
                acc = jnp.where(bk == b, rb_ref[b, hd], acc)
            o_ref[hd] = acc

    return pl.pallas_call(
        kern, in_specs=[pl.BlockSpec(memory_space=pltpu.VMEM), pl.BlockSpec(memory_space=pltpu.SMEM)],
        out_specs=pl.BlockSpec(memory_space=pltpu.VMEM),
        out_shape=jax.ShapeDtypeStruct((QH, 2 * WIN, WIN), F32), name="swa_bias_build")(bkt, rel_bias)


def _bias_bwd(dbias, bkt):
    def kern(db_ref, bk_ref, o_ref):
        bk = bk_ref[...]
        for hd in range(QH):
            g = db_ref[hd]
            for b in range(NBKT):
                r = b * QH + hd
                o_ref[r:r + 1, :] = jnp.sum(jnp.where(bk == b, g, 0.0), axis=0, keepdims=True)

    return pl.pallas_call(
        kern, in_specs=[pl.BlockSpec(memory_space=pltpu.VMEM), pl.BlockSpec(memory_space=pltpu.VMEM)],
        out_specs=pl.BlockSpec(memory_space=pltpu.VMEM),
        out_shape=jax.ShapeDtypeStruct((NBKT * QH, WIN), F32), name="swa_bias_bwd")(dbias, bkt)


def _swa_scores(k_band, q_t, bias, first):
    s = _dot(k_band, q_t, TN) * SWA_SCALE + bias
    if first is not None:
        row = lax.broadcasted_iota(jnp.int32, s.shape, 0)
        s = jnp.where(jnp.logical_or(jnp.logical_not(first), row >= WIN), s, NEG)
    return s


def _swa_fwd(qkv_t, bias, sinks, qb):
    t = qkv_t.shape[1]
    w = qb * WIN
    nst = t // w

    def kern(q_ref, kc_ref, kp_ref, vc_ref, vp_ref, b_ref, sk_ref, o_ref, lse_ref):
        n = pl.program_id(0)
        kfull = jnp.concatenate([kp_ref[...], kc_ref[...]], axis=1)
        vfull = jnp.concatenate([vp_ref[...], vc_ref[...]], axis=1)
        for b in range(qb):
            cs = slice(b * WIN, (b + 1) * WIN)
            bs = slice(b * WIN, (b + 2) * WIN)
            for kh in range(KVH):
                k_band = kfull[kh * HD:(kh + 1) * HD, bs]
                v_band = vfull[kh * HD:(kh + 1) * HD, bs]
                for g in range(G):
                    hd = kh * G + g
                    rs = slice(hd * HD, (hd + 1) * HD)
                    s = _swa_scores(k_band, q_ref[rs, cs], b_ref[hd], (n == 0) if b == 0 else None)
                    sink = sk_ref[hd]
                    m = jnp.maximum(jnp.max(s, axis=0, keepdims=True), sink)
                    p = jnp.exp(s - m)
                    den = jnp.sum(p, axis=0, keepdims=True) + jnp.exp(sink - m)
                    p = p / den
                    o_ref[rs, cs] = _dot(v_band, p.astype(BF16), NN)
                    lse_ref[hd:hd + 1, cs] = m + jnp.log(den)

    prev = lambda r: (lambda n: (r, jnp.maximum(n * qb - 1, 0)))
    return pl.pallas_call(
        kern, grid=(nst,),
        in_specs=[pl.BlockSpec((QH * HD, w), lambda n: (0, n)),
                  pl.BlockSpec((KVH * HD, w), lambda n: (4, n)), pl.BlockSpec((KVH * HD, WIN), prev(4)),
                  pl.BlockSpec((KVH * HD, w), lambda n: (5, n)), pl.BlockSpec((KVH * HD, WIN), prev(5)),
                  pl.BlockSpec((QH, 2 * WIN, WIN), lambda n: (0, 0, 0)),
                  pl.BlockSpec(memory_space=pltpu.SMEM)],
        out_specs=[pl.BlockSpec((QH * HD, w), lambda n: (0, n)), pl.BlockSpec((QH, w), lambda n: (0, n))],
        out_shape=[jax.ShapeDtypeStruct((QH * HD, t), F32), jax.ShapeDtypeStruct((QH, t), F32)],
        name="swa_fwd", compiler_params=_cp())(qkv_t, qkv_t, qkv_t, qkv_t, qkv_t, bias, sinks)


def _swa_bwd(qkv_t, do_t, o_t, lse, bias, sinks, qb):
    t = qkv_t.shape[1]
    w = qb * WIN
    nst = t // w
    nblk = t // WIN

    def kern(q_ref, kc_ref, kp_ref, vc_ref, vp_ref, do_ref, o_ref, lse_ref, qn_ref, don_ref, on_ref, lsen_ref,
             b_ref, sk_ref, dqkv_ref, db_ref, dsk_ref, acc_sc):
        n = pl.program_id(0)

        @pl.when(n == 0)
        def _():
            db_ref[...] = jnp.zeros_like(db_ref)
            dsk_ref[...] = jnp.zeros_like(dsk_ref)

        acc_sc[...] = jnp.zeros_like(acc_sc)
        kfull = jnp.concatenate([kp_ref[...], kc_ref[...]], axis=1)
        vfull = jnp.concatenate([vp_ref[...], vc_ref[...]], axis=1)
        for b in range(qb):
            cs = slice(b * WIN, (b + 1) * WIN)
            bs = slice(b * WIN, (b + 2) * WIN)
            for kh in range(KVH):
                k_band = kfull[kh * HD:(kh + 1) * HD, bs]
                v_band = vfull[kh * HD:(kh + 1) * HD, bs]
                dk_b = jnp.zeros((HD, 2 * WIN), F32)
                dv_b = jnp.zeros((HD, 2 * WIN), F32)
                for g in range(G):
                    hd = kh * G + g
                    rs = slice(hd * HD, (hd + 1) * HD)
                    q_t = q_ref[rs, cs]
                    do = do_ref[rs, cs]
                    lse_h = lse_ref[hd:hd + 1, cs]
                    s = _swa_scores(k_band, q_t, b_ref[hd], (n == 0) if b == 0 else None)
                    p = jnp.exp(s - lse_h)
                    dob = do.astype(BF16)
                    dp = _dot(v_band, dob, TN)
                    dl = jnp.sum(do * o_ref[rs, cs], axis=0, keepdims=True)
                    ds = p * (dp - dl)
                    db_ref[hd] += ds
                    dsk_ref[hd:hd + 1, :] += -jnp.exp(sk_ref[hd] - lse_h) * dl
                    dss = (ds * SWA_SCALE).astype(BF16)
                    dqkv_ref[rs, cs] = _dot(k_band, dss, NN).astype(BF16)
                    dk_b += _dot(q_t, dss, NT)
                    dv_b += _dot(dob, p.astype(BF16), NT)
                acc_sc[kh * HD:(kh + 1) * HD, bs] += dk_b
                acc_sc[KVH * HD + kh * HD:KVH * HD + (kh + 1) * HD, bs] += dv_b

        @pl.when(n < nst - 1)
        def _():
            ls = slice((qb - 1) * WIN, qb * WIN)
            ts = slice(qb * WIN, (qb + 1) * WIN)
            for kh in range(KVH):
                k_last = kc_ref[kh * HD:(kh + 1) * HD, ls]
                v_last = vc_ref[kh * HD:(kh + 1) * HD, ls]
                dk_b = jnp.zeros((HD, WIN), F32)
                dv_b = jnp.zeros((HD, WIN), F32)
                for g in range(G):
                    hd = kh * G + g
                    rs = slice(hd * HD, (hd + 1) * HD)
                    q_t = qn_ref[rs, :]
                    do = don_ref[rs, :]
                    s = _dot(k_last, q_t, TN) * SWA_SCALE + b_ref[hd, 0:WIN, :]
                    p = jnp.exp(s - lsen_ref[hd:hd + 1, :])
                    dob = do.astype(BF16)
                    dp = _dot(v_last, dob, TN)
                    dl = jnp.sum(do * on_ref[rs, :], axis=0, keepdims=True)
                    dss = (p * (dp - dl) * SWA_SCALE).astype(BF16)
                    dk_b += _dot(q_t, dss, NT)
                    dv_b += _dot(dob, p.astype(BF16), NT)
                acc_sc[kh * HD:(kh + 1) * HD, ts] += dk_b
                acc_sc[KVH * HD + kh * HD:KVH * HD + (kh + 1) * HD, ts] += dv_b

        dqkv_ref[QH * HD:QH * HD + 2 * KVH * HD, :] = acc_sc[:, WIN:].astype(BF16)

    prev = lambda r: (lambda n: (r, jnp.maximum(n * qb - 1, 0)))
    nxt = lambda n: (0, jnp.minimum((n + 1) * qb, nblk - 1))
    big = lambda: pl.BlockSpec((QH * HD, w), lambda n: (0, n))
    return pl.pallas_call(
        kern, grid=(nst,),
        in_specs=[big(),
                  pl.BlockSpec((KVH * HD, w), lambda n: (4, n)), pl.BlockSpec((KVH * HD, WIN), prev(4)),
                  pl.BlockSpec((KVH * HD, w), lambda n: (5, n)), pl.BlockSpec((KVH * HD, WIN), prev(5)),
                  big(), big(), pl.BlockSpec((QH, w), lambda n: (0, n)),
                  pl.BlockSpec((QH * HD, WIN), nxt), pl.BlockSpec((QH * HD, WIN), nxt),
                  pl.BlockSpec((QH * HD, WIN), nxt), pl.BlockSpec((QH, WIN), nxt),
                  pl.BlockSpec((QH, 2 * WIN, WIN), lambda n: (0, 0, 0)),
                  pl.BlockSpec(memory_space=pltpu.SMEM)],
        out_specs=[pl.BlockSpec(((QH + 2 * KVH) * HD, w), lambda n: (0, n)),
                   pl.BlockSpec((QH, 2 * WIN, WIN), lambda n: (0, 0, 0)),
                   pl.BlockSpec((QH, WIN), lambda n: (0, 0))],
        out_shape=[jax.ShapeDtypeStruct(((QH + 2 * KVH) * HD, t), BF16),
                   jax.ShapeDtypeStruct((QH, 2 * WIN, WIN), F32), jax.ShapeDtypeStruct((QH, WIN), F32)],
        scratch_shapes=[pltpu.VMEM((2 * KVH * HD, w + WIN), F32)],
        name="swa_bwd", compiler_params=_cp())(
            qkv_t, qkv_t, qkv_t, qkv_t, qkv_t, do_t, o_t, lse, qkv_t, do_t, o_t, lse, bias, sinks)


def _adamw(w, g, m, v, name, tm=544):
    r = w.shape[0]
    tm = r if r % tm else tm
    c1 = 1.0 / (1.0 - B1 ** STEP)
    c2 = 1.0 / (1.0 - B2 ** STEP)

    def kern(w_ref, g_ref, m_ref, v_ref, d_ref, nm_ref, nv_ref):
        g_ = g_ref[...]
        nm = B1 * m_ref[...] + (1.0 - B1) * g_
        nv = B2 * v_ref[...] + (1.0 - B2) * (g_ * g_)
        d_ref[...] = -LR * ((nm * c1) / (jnp.sqrt(nv * c2) + ADAM_EPS) + WD * w_ref[...])
        nm_ref[...] = nm
        nv_ref[...] = nv

    row = pl.BlockSpec((tm, D), lambda i: (i, 0))
    sds = jax.ShapeDtypeStruct((r, D), F32)
    return pl.pallas_call(kern, grid=(r // tm,), in_specs=[row] * 4, out_specs=[row] * 3, out_shape=[sds] * 3,
                          name=name, compiler_params=_cp())(w, g, m, v)


def _mesh_pos():
    return lax.axis_index("x"), lax.axis_index("y"), lax.axis_index("c")


ANY = pl.BlockSpec(memory_space=pl.ANY)


def _allgather_weights(wpack):
    r = wpack.shape[0]
    half = r // 2

    def body(w_ref, out_ref, send_sems, recv_sems, local_sem):
        x, y, c = _mesh_pos()
        sibling = (x, y, 1 - c)
        chips = [(1 - x, y), (x, 1 - y), (1 - x, 1 - y)]

        def rows(px, py, pc):
            return out_ref.at[2 * px + py, pl.ds(pc * half, half), :]

        def copy(k, block, to, src=None):
            return pltpu.make_async_remote_copy(
                src_ref=rows(*block) if src is None else src, dst_ref=rows(*block),
                send_sem=send_sems.at[k], recv_sem=recv_sems.at[k], device_id=to, device_id_type=MESH)

        mine = pltpu.make_async_copy(w_ref, out_ref.at[2 * x + y], local_sem)
        mine.start()
        first = [copy(j, (x, y, c), (*chip, c), src=w_ref.at[pl.ds(c * half, half), :]) for j, chip in enumerate(chips)]
        for cp in first:
            cp.start()
        passed = [copy(3 + j, (*chip, c), sibling) for j, chip in enumerate(chips)]
        for j, chip in enumerate(chips):
            copy(j, (*chip, c), (x, y, c)).wait_recv()
            passed[j].start()
        for j, chip in enumerate(chips):
            copy(3 + j, (*chip, 1 - c), (x, y, c)).wait_recv()
        for cp in first + passed:
            cp.wait_send()
        mine.wait()

    return pl.pallas_call(
        body, out_shape=jax.ShapeDtypeStruct((4, r, D), wpack.dtype), in_specs=[ANY], out_specs=ANY,
        scratch_shapes=[pltpu.SemaphoreType.DMA((6,)), pltpu.SemaphoreType.DMA((6,)), pltpu.SemaphoreType.DMA],
        name="allgather_weights")(wpack)


def _exchange_core_halves(g):
    half = g.shape[1] // 2

    def body(g_ref, out_ref, send_sem, recv_sem):
        x, y, c = _mesh_pos()
        cp = pltpu.make_async_remote_copy(
            src_ref=g_ref.at[:, pl.ds((1 - c) * half, half), :], dst_ref=out_ref,
            send_sem=send_sem, recv_sem=recv_sem, device_id=(x, y, 1 - c), device_id_type=MESH)
        cp.start()
        cp.wait()

    return pl.pallas_call(
        body, out_shape=jax.ShapeDtypeStruct((4, half, D), g.dtype), in_specs=[ANY], out_specs=ANY,
        scratch_shapes=[pltpu.SemaphoreType.DMA, pltpu.SemaphoreType.DMA], name="rs_exchange_cores")(g)


def _add_core_halves(g, other, cidx, tm=544):
    half = other.shape[1]
    nb = half // tm

    def kern(c_ref, a_ref, b_ref, o_ref):
        o_ref[...] = a_ref[...] + b_ref[...]

    gs = pltpu.PrefetchScalarGridSpec(
        num_scalar_prefetch=1, grid=(4, nb),
        in_specs=[pl.BlockSpec((1, tm, D), lambda s, i, c: (s, c[0] * nb + i, 0)),
                  pl.BlockSpec((1, tm, D), lambda s, i, c: (s, i, 0))],
        out_specs=pl.BlockSpec((1, tm, D), lambda s, i, c: (s, i, 0)))
    return pl.pallas_call(kern, grid_spec=gs, out_shape=jax.ShapeDtypeStruct(other.shape, F32),
                          name="rs_add_cores", compiler_params=_cp())(cidx, g, other)


def _exchange_chip_shards(p):
    def body(p_ref, out_ref, send_sems, recv_sems, local_sem):
        x, y, c = _mesh_pos()
        me = 2 * x + y
        chips = [(1 - x, y), (x, 1 - y), (1 - x, 1 - y)]
        mine = pltpu.make_async_copy(p_ref.at[me], out_ref.at[me], local_sem)
        mine.start()
        sends = []
        for j, (px, py) in enumerate(chips):
            cp = pltpu.make_async_remote_copy(
                src_ref=p_ref.at[2 * px + py], dst_ref=out_ref.at[me],
                send_sem=send_sems.at[j], recv_sem=recv_sems.at[j], device_id=(px, py, c), device_id_type=MESH)
            cp.start()
            sends.append(cp)
        for j, (px, py) in enumerate(chips):
            pltpu.make_async_remote_copy(
                src_ref=p_ref.at[me], dst_ref=out_ref.at[2 * px + py],
                send_sem=send_sems.at[j], recv_sem=recv_sems.at[j], device_id=(px, py, c),
                device_id_type=MESH).wait_recv()
        for cp in sends:
            cp.wait_send()
        mine.wait()

    return pl.pallas_call(
        body, out_shape=jax.ShapeDtypeStruct(p.shape, p.dtype), in_specs=[ANY], out_specs=ANY,
        scratch_shapes=[pltpu.SemaphoreType.DMA((3,)), pltpu.SemaphoreType.DMA((3,)), pltpu.SemaphoreType.DMA],
        name="rs_exchange_chips")(p)


def _sum_slots(b, tm=544):
    half = b.shape[1]

    def kern(b_ref, o_ref):
        o_ref[...] = ((b_ref[0].astype(F32) + b_ref[1].astype(F32)) + b_ref[2].astype(F32)) + b_ref[3].astype(F32)

    return pl.pallas_call(
        kern, grid=(half // tm,), in_specs=[pl.BlockSpec((4, tm, D), lambda i: (0, i, 0))],
        out_specs=pl.BlockSpec((tm, D), lambda i: (i, 0)), out_shape=jax.ShapeDtypeStruct((half, D), F32),
        name="rs_sum_chips", compiler_params=_cp())(b)


def _join_core_halves(rh):
    half = rh.shape[0]

    def body(r_ref, out_ref, send_sem, recv_sem, local_sem):
        x, y, c = _mesh_pos()
        mine = pltpu.make_async_copy(r_ref, out_ref.at[pl.ds(c * half, half), :], local_sem)
        mine.start()
        cp = pltpu.make_async_remote_copy(
            src_ref=r_ref, dst_ref=out_ref.at[pl.ds(c * half, half), :],
            send_sem=send_sem, recv_sem=recv_sem, device_id=(x, y, 1 - c), device_id_type=MESH)
        cp.start()
        pltpu.make_async_remote_copy(
            src_ref=r_ref, dst_ref=out_ref.at[pl.ds((1 - c) * half, half), :],
            send_sem=send_sem, recv_sem=recv_sem, device_id=(x, y, 1 - c), device_id_type=MESH).wait_recv()
        cp.wait_send()
        mine.wait()

    return pl.pallas_call(
        body, out_shape=jax.ShapeDtypeStruct((2 * half, D), rh.dtype), in_specs=[ANY], out_specs=ANY,
        scratch_shapes=[pltpu.SemaphoreType.DMA, pltpu.SemaphoreType.DMA, pltpu.SemaphoreType.DMA],
        name="rs_join_cores")(rh)


def _allreduce_small(v, name):
    def body(v_ref, out_ref, gat, send_sems, recv_sems):
        x, y, c = _mesh_pos()
        me = 4 * x + 2 * y + c
        gat[me] = v_ref[...]
        sends = []
        for k in range(1, 8):
            peer = (x ^ (k >> 2), y ^ ((k >> 1) & 1), c ^ (k & 1))
            cp = pltpu.make_async_remote_copy(
                src_ref=v_ref, dst_ref=gat.at[me], send_sem=send_sems.at[k - 1], recv_sem=recv_sems.at[k - 1],
                device_id=peer, device_id_type=MESH)
            cp.start()
            sends.append(cp)
        for k in range(1, 8):
            px, py, pc = x ^ (k >> 2), y ^ ((k >> 1) & 1), c ^ (k & 1)
            pltpu.make_async_remote_copy(
                src_ref=v_ref, dst_ref=gat.at[4 * px + 2 * py + pc], send_sem=send_sems.at[k - 1],
                recv_sem=recv_sems.at[k - 1], device_id=(px, py, pc), device_id_type=MESH).wait_recv()
        for cp in sends:
            cp.wait_send()
        acc = gat[0]
        for d in range(1, 8):
            acc = acc + gat[d]
        out_ref[...] = acc

    return pl.pallas_call(
        body, out_shape=jax.ShapeDtypeStruct(v.shape, F32),
        in_specs=[pl.BlockSpec(memory_space=pltpu.VMEM)], out_specs=pl.BlockSpec(memory_space=pltpu.VMEM),
        scratch_shapes=[pltpu.VMEM((8,) + v.shape, F32), pltpu.SemaphoreType.DMA((7,)), pltpu.SemaphoreType.DMA((7,))],
        name=name)(v)


def _mlp_fwd(xb, w_up, w_down, tag):
    u, a = _mm(xb, w_up, "nn", f"mlp_up_{tag}", relu2=True)
    return u, a, _mm(a, w_down, "nn", f"mlp_down_{tag}")


def _mlp_bwd(dz, dzb, xb, u, a, w_up, w_down, tag):
    du = _mm(dzb, w_down, "nt", f"mlp_down_dx_{tag}", out_dtype=BF16, gate_u=u)
    dw_down = _mm(a, dzb, "tn", f"mlp_down_dw_{tag}")
    dw_up = _mm(xb, du, "tn", f"mlp_up_dw_{tag}")
    dx = _mm(du, w_up, "nt", f"mlp_up_dx_{tag}", addend=dz, add_scale=ALPHA)
    return dx, dw_up, dw_down


def _fwd_bwd(x, target, w, bq=512, qb=4):
    t = x.shape[0]
    bq = min(bq, t)
    qb = min(qb, t // WIN)
    cos, sin = _rope_tables(t)
    bkt = jnp.asarray(_bucket_table())
    w_in = jnp.pad(w["mla_w_in"], ((0, 0), (0, HW - w["mla_w_in"].shape[1])))
    wuq = w["mla_w_uq"]
    wq2 = jnp.concatenate([wuq[:, :, :NOPE].reshape(QR, H * NOPE),
                           jnp.pad(wuq[:, :, NOPE:], ((0, 0), (0, 0), (0, RP - ROPE))).reshape(QR, H * RP)], axis=1)
    wuk_t = w["mla_w_uk"].transpose(1, 2, 0)
    wuk_h = w["mla_w_uk"].transpose(1, 0, 2)
    wuv_h = w["mla_w_uv"].transpose(1, 0, 2)
    w_o = w["mla_w_o"]
    wqkv = jnp.concatenate([w["swa_w_q"], w["kv_w_shared"]], axis=1)
    wqkv_t = wqkv.T
    wo_s = w["swa_w_o"]
    sinks = w["swa_sinks"].reshape(QH)
    lnp = lambda n, l: w[n][l]

    hh = _mm(x, w_in, "nn", "mla_in")
    cq, kc = _mla_pre(hh, w["mla_g_q"], w["mla_g_kv"], cos, sin)
    q2 = _mm(cq, wq2, "nn", "mla_uq")
    qcat = _q_prep(q2, wuk_t, cos, sin)
    o_lat, lse0 = _flash_fwd(qcat, kc, bq)
    o0 = _o_up(o_lat, wuv_h)
    y0 = _mm(o0, w_o, "nn", "mla_out")
    x1, x1b, xh1, r1 = _add_ln(x, y0, lnp("ln_mix_g", 0), lnp("ln_mix_b", 0), "ln_mix_0")
    u0, a0, f0 = _mlp_fwd(x1b, w["mlp_w_up"][0], w["mlp_w_down"][0], 0)
    x2, x2b, xh2, r2 = _add_ln(x1, f0, lnp("ln_mlp_g", 0), lnp("ln_mlp_b", 0), "ln_mlp_0")
    bias = _bias_build(w["rel_bias"], bkt)
    qkv_t = _mm(x2b, wqkv, "nn", "swa_qkv", out_dtype=BF16, out_t=True)
    os_t, lse1 = _swa_fwd(qkv_t, bias, sinks, qb)
    y1 = _mm(os_t, wo_s, "tn", "swa_out")
    x3, x3b, xh3, r3 = _add_ln(x2, y1, lnp("ln_mix_g", 1), lnp("ln_mix_b", 1), "ln_mix_1")
    u1, a1, f1 = _mlp_fwd(x3b, w["mlp_w_up"][1], w["mlp_w_down"][1], 1)
    x4, _, xh4, r4 = _add_ln(x3, f1, lnp("ln_mlp_g", 1), lnp("ln_mlp_b", 1), "ln_mlp_1")
    dx4, lpart = _loss_grad(x4, target)

    g = {}
    dz4, dz4b, dg_mlp1, db_mlp1 = _ln_bwd(dx4, xh4, r4, lnp("ln_mlp_g", 1), "ln_mlp_1_bwd")
    dx3, dwu1, dwd1 = _mlp_bwd(dz4, dz4b, x3b, u1, a1, w["mlp_w_up"][1], w["mlp_w_down"][1], 1)
    dz3, dz3b, dg_mix1, db_mix1 = _ln_bwd(dx3, xh3, r3, lnp("ln_mix_g", 1), "ln_mix_1_bwd")
    dos_t = _mm(dz3b, wo_s, "nt", "swa_out_dx", out_t=True)
    g["swa_w_o"] = _mm(os_t, dz3b, "nn", "swa_out_dw")
    dqkv_t, dbias, dsk = _swa_bwd(qkv_t, dos_t, os_t, lse1, bias, sinks, qb)
    dwqkv = _mm(dqkv_t, x2b, "nn", "swa_qkv_dw").T
    g["swa_w_q"], g["kv_w_shared"] = dwqkv[:, :QH * HD], dwqkv[:, QH * HD:]
    dx2 = _mm(dqkv_t, wqkv_t, "tn", "swa_qkv_dx", addend=dz3, add_scale=ALPHA)
    g["rel_bias"] = jnp.sum(_bias_bwd(dbias, bkt), axis=-1).reshape(NBKT, QH)
    g["swa_sinks"] = jnp.sum(dsk, axis=-1).reshape(1, QH)
    dz2, dz2b, dg_mlp0, db_mlp0 = _ln_bwd(dx2, xh2, r2, lnp("ln_mlp_g", 0), "ln_mlp_0_bwd")
    dx1, dwu0, dwd0 = _mlp_bwd(dz2, dz2b, x1b, u0, a0, w["mlp_w_up"][0], w["mlp_w_down"][0], 0)
    dz1, dz1b, dg_mix0, db_mix0 = _ln_bwd(dx1, xh1, r1, lnp("ln_mix_g", 0), "ln_mix_0_bwd")
    do0 = _mm(dz1b, w_o, "nt", "mla_out_dx", out_dtype=BF16)
    g["mla_w_o"] = _mm(o0, dz1b, "tn", "mla_out_dw")
    do_lat, dwuv = _o_up_bwd(do0, o_lat, wuv_h)
    g["mla_w_uv"] = dwuv.transpose(1, 0, 2)
    dq_cat = _flash_dq(qcat, kc, do_lat, o_lat, lse0, bq)
    dk = _flash_dkv(qcat, kc, do_lat, o_lat, lse0, bq)
    dq2, dwuk = _q_prep_bwd(dq_cat, q2, wuk_h, cos, sin)
    g["mla_w_uk"] = dwuk.transpose(2, 0, 1)
    dcq = _mm(dq2, wq2, "nt", "mla_uq_dx")
    dwq2 = _mm(cq, dq2, "tn", "mla_uq_dw")
    g["mla_w_uq"] = jnp.concatenate([dwq2[:, :H * NOPE].reshape(QR, H, NOPE),
                                     dwq2[:, H * NOPE:].reshape(QR, H, RP)[:, :, :ROPE]], axis=2)
    dh, dgq, dgkv = _mla_pre_bwd(hh, dcq, dk, w["mla_g_q"], w["mla_g_kv"], cos, sin)
    g["mla_w_in"] = _mm(x, dh, "tn", "mla_in_dw")[:, :QR + KVR + ROPE]
    grad_x = _mm(dh, w_in, "nt", "mla_in_dx", addend=dz1, add_scale=ALPHA)
    g["mla_g_q"], g["mla_g_kv"] = dgq, dgkv
    g["mlp_w_up"] = jnp.stack([dwu0, dwu1])
    g["mlp_w_down"] = jnp.stack([dwd0, dwd1])
    g["ln_mix_g"] = jnp.concatenate([dg_mix0, dg_mix1], axis=0)
    g["ln_mix_b"] = jnp.concatenate([db_mix0, db_mix1], axis=0)
    g["ln_mlp_g"] = jnp.concatenate([dg_mlp0, dg_mlp1], axis=0)
    g["ln_mlp_b"] = jnp.concatenate([db_mlp0, db_mlp1], axis=0)
    return lpart, grad_x, g


def _rows(a):
    return a.reshape(-1, D)


def _pack_shards(parts):
    return jnp.concatenate([_rows(parts[n]) for n, _ in PACK], axis=0)


def _unpack_shards(buf, like):
    out, off = {}, 0
    for n, r in PACK:
        out[n] = buf[off:off + r].reshape(like[n].shape)
        off += r
    return out


def _full_from_gathered(wall, shard_shapes):
    out, off = {}, 0
    for n, r in PACK:
        sl = wall[:, off:off + r]
        off += r
        shp = shard_shapes[n]
        if n == "mlp_w_up":
            out[n] = sl.reshape(4, 2, D, D).transpose(1, 2, 0, 3).reshape(2, D, DFF)
        elif n == "mlp_w_down":
            out[n] = sl.reshape(4, 2, D, D).transpose(1, 0, 2, 3).reshape(2, DFF, D)
        elif n == "kv_w_shared":
            out[n] = sl.reshape((4 * shp[0],) + tuple(shp[1:]))
        else:
            out[n] = sl.reshape((4 * shp[1],) + tuple(shp[2:]))
    return out


def _shards_from_full(grads):
    cols = []
    for n, r in PACK:
        gfull = grads[n]
        if n == "mlp_w_up":
            cols.append(gfull.reshape(2, D, 4, D).transpose(2, 0, 1, 3).reshape(4, r, D))
        elif n == "mlp_w_down":
            cols.append(gfull.reshape(2, 4, D, D).transpose(1, 0, 2, 3).reshape(4, r, D))
        else:
            cols.append(gfull.reshape(4, r, D))
    return jnp.concatenate(cols, axis=1)


SMALL = (("ln_mix_g", 0, 2), ("ln_mix_b", 2, 2), ("ln_mlp_g", 4, 2), ("ln_mlp_b", 6, 2),
         ("swa_sinks", 8, 1), ("mla_g_q", 9, 1), ("mla_g_kv", 10, 1), ("rel_bias", 11, 1))


def _pack_small(parts):
    rows = []
    for n, _, nr in SMALL:
        a = parts[n].reshape(nr, -1).astype(F32)
        rows.append(jnp.pad(a, ((0, 0), (0, D - a.shape[1]))))
    rows.append(jnp.zeros((SMALL_ROWS - 12, D), F32))
    return jnp.concatenate(rows, axis=0)


def _unpack_small(buf, like):
    out = {}
    for n, r0, nr in SMALL:
        size = like[n].size // nr
        out[n] = buf[r0:r0 + nr, :size].reshape(like[n].shape)
    return out


def kernel(x, mla_w_in, mla_g_q, mla_g_kv, mla_w_uq, mla_w_uk, mla_w_uv, mla_w_o, kv_w_shared, swa_w_q, swa_sinks, swa_w_o, rel_bias, mlp_w_up, mlp_w_down, ln_mix_g, ln_mix_b, ln_mlp_g, ln_mlp_b, loss_target, m_mla_w_in, m_mla_g_q, m_mla_g_kv, m_mla_w_uq, m_mla_w_uk, m_mla_w_uv, m_mla_w_o, m_kv_w_shared, m_swa_w_q, m_swa_sinks, m_swa_w_o, m_rel_bias, m_mlp_w_up, m_mlp_w_down, m_ln_mix_g, m_ln_mix_b, m_ln_mlp_g, m_ln_mlp_b, v_mla_w_in, v_mla_g_q, v_mla_g_kv, v_mla_w_uq, v_mla_w_uk, v_mla_w_uv, v_mla_w_o, v_kv_w_shared, v_swa_w_q, v_swa_sinks, v_swa_w_o, v_rel_bias, v_mlp_w_up, v_mlp_w_down, v_ln_mix_g, v_ln_mix_b, v_ln_mlp_g, v_ln_mlp_b):
    names = ["mla_w_in", "mla_g_q", "mla_g_kv", "mla_w_uq", "mla_w_uk", "mla_w_uv", "mla_w_o", "kv_w_shared",
             "swa_w_q", "swa_sinks", "swa_w_o", "rel_bias", "mlp_w_up", "mlp_w_down",
             "ln_mix_g", "ln_mix_b", "ln_mlp_g", "ln_mlp_b"]
    ws = dict(zip(names, [mla_w_in, mla_g_q, mla_g_kv, mla_w_uq, mla_w_uk, mla_w_uv, mla_w_o, kv_w_shared,
                          swa_w_q, swa_sinks, swa_w_o, rel_bias, mlp_w_up, mlp_w_down,
                          ln_mix_g, ln_mix_b, ln_mlp_g, ln_mlp_b]))
    ms = dict(zip(names, [m_mla_w_in, m_mla_g_q, m_mla_g_kv, m_mla_w_uq, m_mla_w_uk, m_mla_w_uv, m_mla_w_o,
                          m_kv_w_shared, m_swa_w_q, m_swa_sinks, m_swa_w_o, m_rel_bias, m_mlp_w_up, m_mlp_w_down,
                          m_ln_mix_g, m_ln_mix_b, m_ln_mlp_g, m_ln_mlp_b]))
    vs = dict(zip(names, [v_mla_w_in, v_mla_g_q, v_mla_g_kv, v_mla_w_uq, v_mla_w_uk, v_mla_w_uv, v_mla_w_o,
                          v_kv_w_shared, v_swa_w_q, v_swa_sinks, v_swa_w_o, v_rel_bias, v_mlp_w_up, v_mlp_w_down,
                          v_ln_mix_g, v_ln_mix_b, v_ln_mlp_g, v_ln_mlp_b]))
    xi, yi, ci = _mesh_pos()
    shard = 2 * xi + yi
    big = [n for n, _ in PACK]
    shard_shapes = {n: ws[n].shape for n in big}

    wall = _allgather_weights(_pack_shards({n: ws[n].astype(BF16) for n in big}))
    w = _full_from_gathered(wall, shard_shapes)
    gq_slot = lax.dynamic_update_slice(jnp.zeros((1, QR), F32), mla_g_q, (0, shard * (QR // 4)))
    gkv_slot = lax.dynamic_update_slice(jnp.zeros((1, KVR), F32), mla_g_kv, (0, shard * (KVR // 4)))
    gains = jnp.concatenate([jnp.pad(gq_slot, ((0, 0), (0, D - QR))), jnp.pad(gkv_slot, ((0, 0), (0, D - KVR))),
                             jnp.zeros((SMALL_ROWS - 2, D), F32)], axis=0)
    gains = _allreduce_small(gains * 0.5, "allgather_gains")
    w["mla_g_q"], w["mla_g_kv"] = gains[0, :QR], gains[1, :KVR]
    for n in ("swa_sinks", "rel_bias", "ln_mix_g", "ln_mix_b", "ln_mlp_g", "ln_mlp_b"):
        w[n] = ws[n]

    lpart, grad_x, g = _fwd_bwd(x[0], loss_target[0], w)
    loss = lax.psum(0.5 * jnp.sum(lpart) / D, ("x", "y", "c"))

    gsh = _shards_from_full(g)
    other = _exchange_core_halves(gsh)
    cidx = jnp.reshape(ci, (1,)).astype(jnp.int32)
    chip_part = _add_core_halves(gsh, other, cidx)
    slots = _exchange_chip_shards(chip_part)
    gred = _join_core_halves(_sum_slots(slots))
    gbig = _unpack_shards(gred, ws)

    small_like = {n: g[n] for n, _, _ in SMALL}
    gsm = _unpack_small(_allreduce_small(_pack_small(g), "allreduce_small_grads"), small_like)
    gsm["mla_g_q"] = lax.dynamic_slice(gsm["mla_g_q"], (0, shard * (QR // 4)), (1, QR // 4))
    gsm["mla_g_kv"] = lax.dynamic_slice(gsm["mla_g_kv"], (0, shard * (KVR // 4)), (1, KVR // 4))
    grads = {**gbig, **gsm}

    dbig, mbig, vbig = _adamw(_pack_shards({n: ws[n] for n in big}), gred,
                              _pack_shards({n: ms[n] for n in big}), _pack_shards({n: vs[n] for n in big}),
                              "adamw_big")
    dsm, msm, vsm = _adamw(_pack_small(ws), _pack_small(gsm), _pack_small(ms), _pack_small(vs), "adamw_small", tm=16)
    delta = {**_unpack_shards(dbig, ws), **_unpack_small(dsm, ws)}
    new_m = {**_unpack_shards(mbig, ws), **_unpack_small(msm, ws)}
    new_v = {**_unpack_shards(vbig, ws), **_unpack_small(vsm, ws)}
    grads = {n: grads[n].reshape(ws[n].shape) for n in names}
    return (loss, grad_x[None], *[grads[n] for n in names], *[delta[n] for n in names],
            *[new_m[n] for n in names], *[new_v[n] for n in names])
```

```python
import functools
import math

import numpy as np
import jax
import jax.numpy as jnp
from jax import lax
from jax.experimental import pallas as pl
from jax.experimental.pallas import tpu as pltpu

F32 = jnp.float32
BF16 = jnp.bfloat16
MESH = pl.DeviceIdType.MESH

D = 1024
DFF = 4096
H = 8
NOPE = 128
ROPE = 64
QR = 384
KVR = 256
RP = 128
KD = KVR + RP
HW = 768
QH = 16
KVH = 4
HD = 64
G = QH // KVH
WIN = 128
NBKT = 32
ALPHA = 4.0 ** 0.25
LN_EPS = 1e-5
RMS_EPS = 1e-6
MLA_SCALE = (NOPE + ROPE) ** -0.5
LOG2E = 1.4426950408889634
LN2 = 0.6931471805599453
QSCALE = MLA_SCALE * LOG2E
SWA_SCALE = HD ** -0.5
NEG = -1e30
LR, B1, B2, ADAM_EPS, WD, STEP = 0.001, 0.9, 0.999, 1e-8, 0.01, 10

VMEM_LIMIT = 48 * 1024 * 1024

NN = (((1,), (0,)), ((), ()))
NT = (((1,), (1,)), ((), ()))
TN = (((0,), (0,)), ((), ()))

PACK = (("mlp_w_up", 2048), ("mlp_w_down", 2048), ("mla_w_o", 256), ("swa_w_q", 256), ("swa_w_o", 256),
        ("kv_w_shared", 128), ("mla_w_in", 176), ("mla_w_uq", 144), ("mla_w_uk", 64), ("mla_w_uv", 64))
PACK_ROWS = sum(r for _, r in PACK)
HALF_ROWS = PACK_ROWS // 2
SMALL_ROWS = 16


def _cp(**kw):
    return pltpu.CompilerParams(vmem_limit_bytes=VMEM_LIMIT, **kw)


def _tile(n, pref):
    t = min(n, pref)
    while n % t:
        t -= 128
    return t


def _dot(a, b, dims):
    return lax.dot_general(a, b, dims, preferred_element_type=F32)


def _mm(a, b, mode, name, out_dtype=F32, out_t=False, addend=None, add_scale=1.0, relu2=False, gate_u=None,
        tm=1024, tn=1024, tk=512):
    if mode == "nn":
        (m, k), (k2, n) = a.shape, b.shape
    elif mode == "nt":
        (m, k), (n, k2) = a.shape, b.shape
    else:
        (k, m), (k2, n) = a.shape, b.shape
    assert k == k2, (name, a.shape, b.shape)
    tm, tn, tk = _tile(m, tm), _tile(n, tn), _tile(k, tk)
    nk = k // tk
    dims = {"nn": NN, "nt": NT, "tn": TN}[mode]
    if mode == "tn":
        a_spec = pl.BlockSpec((tk, tm), lambda i, j, kk: (kk, i))
    else:
        a_spec = pl.BlockSpec((tm, tk), lambda i, j, kk: (i, kk))
    if mode == "nt":
        b_spec = pl.BlockSpec((tn, tk), lambda i, j, kk: (j, kk))
    else:
        b_spec = pl.BlockSpec((tk, tn), lambda i, j, kk: (kk, j))
    mn_spec = pl.BlockSpec((tm, tn), lambda i, j, kk: (i, j))
    ins, in_specs = [a, b], [a_spec, b_spec]
    if addend is not None:
        ins.append(addend)
        in_specs.append(mn_spec)
    if gate_u is not None:
        ins.append(gate_u)
        in_specs.append(mn_spec)
    if out_t:
        out_shape = [jax.ShapeDtypeStruct((n, m), out_dtype)]
        out_specs = [pl.BlockSpec((tn, tm), lambda i, j, kk: (j, i))]
    else:
        out_shape = [jax.ShapeDtypeStruct((m, n), out_dtype)]
        out_specs = [mn_spec]
    if relu2:
        out_shape.append(jax.ShapeDtypeStruct((m, n), BF16))
        out_specs.append(mn_spec)
    has_add, has_gate = addend is not None, gate_u is not None

    def kern(*refs):
        a_ref, b_ref = refs[0], refs[1]
        pos = 2
        add_ref = gate_ref = None
        if has_add:
            add_ref = refs[pos]
            pos += 1
        if has_gate:
            gate_ref = refs[pos]
            pos += 1
        o_ref = refs[pos]
        a2_ref = refs[pos + 1] if relu2 else None
        acc = refs[-1]
        kk = pl.program_id(2)

        @pl.when(kk == 0)
        def _():
            acc[...] = jnp.zeros_like(acc)

        acc[...] += _dot(a_ref[...].astype(BF16), b_ref[...].astype(BF16), dims)

        @pl.when(kk == nk - 1)
        def _():
            r = acc[...]
            if has_add:
                r = r + add_scale * add_ref[...].astype(F32)
            if has_gate:
                r = r * (2.0 * jnp.maximum(gate_ref[...], 0.0))
            if relu2:
                hh = jnp.maximum(r, 0.0)
                a2_ref[...] = (hh * hh).astype(BF16)
            if out_t:
                r = r.T
            o_ref[...] = r.astype(out_dtype)

    outs = pl.pallas_call(
        kern, out_shape=out_shape, grid=(m // tm, n // tn, nk), in_specs=in_specs, out_specs=out_specs,
        scratch_shapes=[pltpu.VMEM((tm, tn), F32)], name=name, compiler_params=_cp())(*ins)
    return outs if relu2 else outs[0]


def _add_ln(xres, y, g, b, name, tm=256):
    t = xres.shape[0]
    tm = min(tm, t)

    def kern(x_ref, y_ref, g_ref, b_ref, o_ref, ob_ref, xh_ref, r_ref):
        z = ALPHA * x_ref[...] + y_ref[...]
        mu = jnp.mean(z, axis=-1, keepdims=True)
        zc = z - mu
        var = jnp.mean(zc * zc, axis=-1, keepdims=True)
        r = lax.rsqrt(var + LN_EPS)
        xh = zc * r
        o = xh * g_ref[...] + b_ref[...]
        o_ref[...] = o
        ob_ref[...] = o.astype(BF16)
        xh_ref[...] = xh
        r_ref[...] = r

    row = pl.BlockSpec((tm, D), lambda i: (i, 0))
    vec = pl.BlockSpec((1, D), lambda i: (0, 0))
    st = pl.BlockSpec((tm, 1), lambda i: (i, 0))
    return pl.pallas_call(
        kern, grid=(t // tm,), in_specs=[row, row, vec, vec], out_specs=[row, row, row, st],
        out_shape=[jax.ShapeDtypeStruct((t, D), F32), jax.ShapeDtypeStruct((t, D), BF16),
                   jax.ShapeDtypeStruct((t, D), F32), jax.ShapeDtypeStruct((t, 1), F32)],
        name=name, compiler_params=_cp())(xres, y, g.reshape(1, D), b.reshape(1, D))


def _ln_bwd(dout, xhat, rstd, g, name, tm=256):
    t = dout.shape[0]
    tm = min(tm, t)

    def kern(do_ref, xh_ref, r_ref, g_ref, dz_ref, dzb_ref, dg_ref, db_ref):
        @pl.when(pl.program_id(0) == 0)
        def _():
            dg_ref[...] = jnp.zeros_like(dg_ref)
            db_ref[...] = jnp.zeros_like(db_ref)

        do = do_ref[...]
        xh = xh_ref[...]
        dxh = do * g_ref[...]
        m1 = jnp.mean(dxh, axis=-1, keepdims=True)
        m2 = jnp.mean(dxh * xh, axis=-1, keepdims=True)
        dz = r_ref[...] * (dxh - m1 - xh * m2)
        dz_ref[...] = dz
        dzb_ref[...] = dz.astype(BF16)
        dg_ref[...] += jnp.sum(do * xh, axis=0, keepdims=True)
        db_ref[...] += jnp.sum(do, axis=0, keepdims=True)

    row = pl.BlockSpec((tm, D), lambda i: (i, 0))
    vec = pl.BlockSpec((1, D), lambda i: (0, 0))
    st = pl.BlockSpec((tm, 1), lambda i: (i, 0))
    return pl.pallas_call(
        kern, grid=(t // tm,), in_specs=[row, row, st, vec], out_specs=[row, row, vec, vec],
        out_shape=[jax.ShapeDtypeStruct((t, D), F32), jax.ShapeDtypeStruct((t, D), BF16),
                   jax.ShapeDtypeStruct((1, D), F32), jax.ShapeDtypeStruct((1, D), F32)],
        name=name, compiler_params=_cp())(dout, xhat, rstd, g.reshape(1, D))


def _loss_grad(y, target, name="loss_grad", tm=256):
    t = y.shape[0]
    tm = min(tm, t)

    def kern(y_ref, t_ref, d_ref, l_ref):
        @pl.when(pl.program_id(0) == 0)
        def _():
            l_ref[...] = jnp.zeros_like(l_ref)

        e = y_ref[...] - t_ref[...]
        d_ref[...] = e * (1.0 / D)
        l_ref[...] += jnp.sum(e * e, axis=0, keepdims=True)

    row = pl.BlockSpec((tm, D), lambda i: (i, 0))
    vec = pl.BlockSpec((1, D), lambda i: (0, 0))
    return pl.pallas_call(
        kern, grid=(t // tm,), in_specs=[row, row], out_specs=[row, vec],
        out_shape=[jax.ShapeDtypeStruct((t, D), F32), jax.ShapeDtypeStruct((1, D), F32)],
        name=name, compiler_params=_cp())(y, target)


def _rope_tables(t):
    half = ROPE // 2
    inv = 10000.0 ** (-jnp.arange(half, dtype=F32) / half)
    ang = jnp.arange(t).astype(F32)[:, None] * inv[None, :]
    cos, sin = jnp.cos(ang), jnp.sin(ang)
    z = jnp.zeros((t, RP - ROPE), F32)
    return jnp.concatenate([cos, cos, z], axis=1), jnp.concatenate([-sin, sin, z], axis=1)


def _swap_halves(x):
    lane = lax.broadcasted_iota(jnp.int32, x.shape, 1)
    return jnp.where(lane < ROPE // 2, pltpu.roll(x, RP - ROPE // 2, 1), pltpu.roll(x, ROPE // 2, 1))


def _rope(x, cos, sin):
    return x * cos + _swap_halves(x) * sin


def _rope_t(gy, cos, sin):
    return gy * cos + _swap_halves(gy * sin)


def _mla_pre(hh, g_q, g_kv, cos, sin, tm=256):
    t = hh.shape[0]
    tm = min(tm, t)

    def kern(h_ref, gq_ref, gkv_ref, c_ref, s_ref, cq_ref, k_ref):
        xq = h_ref[:, 0:QR]
        rq = lax.rsqrt(jnp.mean(xq * xq, axis=-1, keepdims=True) + RMS_EPS)
        cq_ref[...] = (xq * rq * gq_ref[...]).astype(BF16)
        xk = h_ref[:, QR:QR + KVR]
        rk = lax.rsqrt(jnp.mean(xk * xk, axis=-1, keepdims=True) + RMS_EPS)
        k_ref[:, 0:KVR] = (xk * rk * gkv_ref[...]).astype(BF16)
        k_ref[:, KVR:KD] = _rope(h_ref[:, QR + KVR:HW], c_ref[...], s_ref[...]).astype(BF16)

    return pl.pallas_call(
        kern, grid=(t // tm,),
        in_specs=[pl.BlockSpec((tm, HW), lambda i: (i, 0)), pl.BlockSpec((1, QR), lambda i: (0, 0)),
                  pl.BlockSpec((1, KVR), lambda i: (0, 0)), pl.BlockSpec((tm, RP), lambda i: (i, 0)),
                  pl.BlockSpec((tm, RP), lambda i: (i, 0))],
        out_specs=[pl.BlockSpec((tm, QR), lambda i: (i, 0)), pl.BlockSpec((tm, KD), lambda i: (i, 0))],
        out_shape=[jax.ShapeDtypeStruct((t, QR), BF16), jax.ShapeDtypeStruct((t, KD), BF16)],
        name="mla_pre", compiler_params=_cp())(hh, g_q.reshape(1, QR), g_kv.reshape(1, KVR), cos, sin)


def _mla_pre_bwd(hh, dcq, dk, g_q, g_kv, cos, sin, tm=256):
    t = hh.shape[0]
    tm = min(tm, t)

    def rms_bwd(x, dy, g):
        r = lax.rsqrt(jnp.mean(x * x, axis=-1, keepdims=True) + RMS_EPS)
        gdy = dy * g
        dx = r * gdy - x * (r * r * r) * jnp.mean(gdy * x, axis=-1, keepdims=True)
        return dx, jnp.sum(dy * x * r, axis=0, keepdims=True)

    def kern(h_ref, dcq_ref, dk_ref, gq_ref, gkv_ref, c_ref, s_ref, dh_ref, dgq_ref, dgkv_ref):
        @pl.when(pl.program_id(0) == 0)
        def _():
            dgq_ref[...] = jnp.zeros_like(dgq_ref)
            dgkv_ref[...] = jnp.zeros_like(dgkv_ref)

        dxq, dgq = rms_bwd(h_ref[:, 0:QR], dcq_ref[...], gq_ref[...])
        dxk, dgk = rms_bwd(h_ref[:, QR:QR + KVR], dk_ref[:, 0:KVR], gkv_ref[...])
        dh_ref[:, 0:QR] = dxq.astype(BF16)
        dh_ref[:, QR:QR + KVR] = dxk.astype(BF16)
        dh_ref[:, QR + KVR:HW] = _rope_t(dk_ref[:, KVR:KD], c_ref[...], s_ref[...]).astype(BF16)
        dgq_ref[...] += dgq
        dgkv_ref[...] += dgk

    return pl.pallas_call(
        kern, grid=(t // tm,),
        in_specs=[pl.BlockSpec((tm, HW), lambda i: (i, 0)), pl.BlockSpec((tm, QR), lambda i: (i, 0)),
                  pl.BlockSpec((tm, KD), lambda i: (i, 0)), pl.BlockSpec((1, QR), lambda i: (0, 0)),
                  pl.BlockSpec((1, KVR), lambda i: (0, 0)), pl.BlockSpec((tm, RP), lambda i: (i, 0)),
                  pl.BlockSpec((tm, RP), lambda i: (i, 0))],
        out_specs=[pl.BlockSpec((tm, HW), lambda i: (i, 0)), pl.BlockSpec((1, QR), lambda i: (0, 0)),
                   pl.BlockSpec((1, KVR), lambda i: (0, 0))],
        out_shape=[jax.ShapeDtypeStruct((t, HW), BF16), jax.ShapeDtypeStruct((1, QR), F32),
                   jax.ShapeDtypeStruct((1, KVR), F32)],
        name="mla_pre_bwd", compiler_params=_cp())(hh, dcq, dk, g_q.reshape(1, QR), g_kv.reshape(1, KVR), cos, sin)


def _q_prep(q2, wuk_t, cos, sin, tm=256):
    t = q2.shape[0]
    tm = min(tm, t)

    def kern(q_ref, w_ref, c_ref, s_ref, o_ref):
        cos_, sin_ = c_ref[...], s_ref[...]
        for h in range(H):
            qn = q_ref[:, h * NOPE:(h + 1) * NOPE].astype(BF16)
            o_ref[:, h * KD:h * KD + KVR] = (_dot(qn, w_ref[h], NN) * QSCALE).astype(BF16)
            qr = q_ref[:, H * NOPE + h * RP:H * NOPE + (h + 1) * RP]
            o_ref[:, h * KD + KVR:(h + 1) * KD] = (_rope(qr, cos_, sin_) * QSCALE).astype(BF16)

    return pl.pallas_call(
        kern, grid=(t // tm,),
        in_specs=[pl.BlockSpec((tm, 2 * H * NOPE), lambda i: (i, 0)), pl.BlockSpec((H, NOPE, KVR), lambda i: (0, 0, 0)),
                  pl.BlockSpec((tm, RP), lambda i: (i, 0)), pl.BlockSpec((tm, RP), lambda i: (i, 0))],
        out_specs=pl.BlockSpec((tm, H * KD), lambda i: (i, 0)),
        out_shape=jax.ShapeDtypeStruct((t, H * KD), BF16),
        name="q_prep", compiler_params=_cp())(q2, wuk_t, cos, sin)


def _q_prep_bwd(dq_cat, q2, wuk_h, cos, sin, tm=256):
    t = q2.shape[0]
    tm = min(tm, t)

    def kern(dq_ref, q_ref, w_ref, c_ref, s_ref, o_ref, dw_ref):
        @pl.when(pl.program_id(0) == 0)
        def _():
            dw_ref[...] = jnp.zeros_like(dw_ref)

        cos_, sin_ = c_ref[...], s_ref[...]
        for h in range(H):
            dql = dq_ref[:, h * KD:h * KD + KVR].astype(BF16)
            o_ref[:, h * NOPE:(h + 1) * NOPE] = _dot(dql, w_ref[h], NN).astype(BF16)
            dqr = dq_ref[:, h * KD + KVR:(h + 1) * KD]
            o_ref[:, H * NOPE + h * RP:H * NOPE + (h + 1) * RP] = _rope_t(dqr, cos_, sin_).astype(BF16)
            qn = q_ref[:, h * NOPE:(h + 1) * NOPE].astype(BF16)
            dw_ref[h] += _dot(qn, dql, TN)

    return pl.pallas_call(
        kern, grid=(t // tm,),
        in_specs=[pl.BlockSpec((tm, H * KD), lambda i: (i, 0)), pl.BlockSpec((tm, 2 * H * NOPE), lambda i: (i, 0)),
                  pl.BlockSpec((H, KVR, NOPE), lambda i: (0, 0, 0)),
                  pl.BlockSpec((tm, RP), lambda i: (i, 0)), pl.BlockSpec((tm, RP), lambda i: (i, 0))],
        out_specs=[pl.BlockSpec((tm, 2 * H * NOPE), lambda i: (i, 0)), pl.BlockSpec((H, NOPE, KVR), lambda i: (0, 0, 0))],
        out_shape=[jax.ShapeDtypeStruct((t, 2 * H * NOPE), BF16), jax.ShapeDtypeStruct((H, NOPE, KVR), F32)],
        name="q_prep_bwd", compiler_params=_cp())(dq_cat, q2, wuk_h, cos, sin)


def _o_up(o_lat, wuv_h, tm=256):
    t = o_lat.shape[0]
    tm = min(tm, t)

    def kern(x_ref, w_ref, o_ref):
        for h in range(H):
            xl = x_ref[:, h * KVR:(h + 1) * KVR].astype(BF16)
            o_ref[:, h * NOPE:(h + 1) * NOPE] = _dot(xl, w_ref[h], NN).astype(BF16)

    return pl.pallas_call(
        kern, grid=(t // tm,),
        in_specs=[pl.BlockSpec((tm, H * KVR), lambda i: (i, 0)), pl.BlockSpec((H, KVR, NOPE), lambda i: (0, 0, 0))],
        out_specs=pl.BlockSpec((tm, H * NOPE), lambda i: (i, 0)),
        out_shape=jax.ShapeDtypeStruct((t, H * NOPE), BF16),
        name="o_up", compiler_params=_cp())(o_lat, wuv_h)


def _o_up_bwd(do, o_lat, wuv_h, tm=256):
    t = do.shape[0]
    tm = min(tm, t)

    def kern(do_ref, x_ref, w_ref, dx_ref, dw_ref):
        @pl.when(pl.program_id(0) == 0)
        def _():
            dw_ref[...] = jnp.zeros_like(dw_ref)

        for h in range(H):
            dh_ = do_ref[:, h * NOPE:(h + 1) * NOPE]
            dx_ref[:, h * KVR:(h + 1) * KVR] = _dot(dh_, w_ref[h], NT)
            xl = x_ref[:, h * KVR:(h + 1) * KVR].astype(BF16)
            dw_ref[h] += _dot(xl, dh_, TN)

    return pl.pallas_call(
        kern, grid=(t // tm,),
        in_specs=[pl.BlockSpec((tm, H * NOPE), lambda i: (i, 0)), pl.BlockSpec((tm, H * KVR), lambda i: (i, 0)),
                  pl.BlockSpec((H, KVR, NOPE), lambda i: (0, 0, 0))],
        out_specs=[pl.BlockSpec((tm, H * KVR), lambda i: (i, 0)), pl.BlockSpec((H, KVR, NOPE), lambda i: (0, 0, 0))],
        out_shape=[jax.ShapeDtypeStruct((t, H * KVR), F32), jax.ShapeDtypeStruct((H, KVR, NOPE), F32)],
        name="o_up_bwd", compiler_params=_cp())(do, o_lat, wuv_h)


def _causal_pairs(nq):
    return [(i, j) for i in range(nq) for j in range(i + 1)]


def _lane_tile(stat, width):
    return jnp.tile(stat, (1, width // 128))


def _flash_fwd(qcat, kc, bq, hb):
    t = kc.shape[0]
    nq = t // bq
    pairs = _causal_pairs(nq)
    itab = jnp.asarray(np.array([p[0] for p in pairs], np.int32))
    jtab = jnp.asarray(np.array([p[1] for p in pairs], np.int32))

    def kern(it, jt, q_ref, k_ref, o_ref, lse_ref, lset_ref, m_sc, l_sc, acc_sc):
        st = pl.program_id(1)
        i, j = it[st], jt[st]

        @pl.when(j == 0)
        def _():
            m_sc[...] = jnp.full_like(m_sc, NEG)
            l_sc[...] = jnp.zeros_like(l_sc)
            acc_sc[...] = jnp.zeros_like(acc_sc)

        def update(masked):
            k = k_ref[...]
            v = k[:, 0:KVR]
            if masked:
                row = lax.broadcasted_iota(jnp.int32, (bq, bq), 0)
                col = lax.broadcasted_iota(jnp.int32, (bq, bq), 1)
                keep = col <= row
            for hh in range(hb):
                s = _dot(q_ref[:, hh * KD:(hh + 1) * KD], k, NT)
                if masked:
                    s = jnp.where(keep, s, NEG)
                m_prev = m_sc[hh]
                m_next = jnp.maximum(m_prev, jnp.max(s, axis=1)[:, None])
                p = jnp.exp2(s - _lane_tile(m_next, bq))
                a = jnp.exp2(m_prev - m_next)
                l_sc[hh] = a * l_sc[hh] + jnp.sum(p, axis=1)[:, None]
                acc_sc[hh] = _lane_tile(a, KVR) * acc_sc[hh] + _dot(p.astype(BF16), v, NN)
                m_sc[hh] = m_next

        @pl.when(j < i)
        def _():
            update(False)

        @pl.when(j == i)
        def _():
            update(True)
            for hh in range(hb):
                l = l_sc[hh]
                o_ref[:, hh * KVR:(hh + 1) * KVR] = acc_sc[hh] / _lane_tile(l, KVR)
                lse = m_sc[hh] + jnp.log2(l)
                lse_ref[hh] = lse
                lset_ref[hh] = lse.T[0:1, :]

    gs = pltpu.PrefetchScalarGridSpec(
        num_scalar_prefetch=2, grid=(H // hb, len(pairs)),
        in_specs=[pl.BlockSpec((bq, hb * KD), lambda g, s, it, jt: (it[s], g)),
                  pl.BlockSpec((bq, KD), lambda g, s, it, jt: (jt[s], 0))],
        out_specs=[pl.BlockSpec((bq, hb * KVR), lambda g, s, it, jt: (it[s], g)),
                   pl.BlockSpec((hb, bq, 128), lambda g, s, it, jt: (g, it[s], 0)),
                   pl.BlockSpec((hb, 1, bq), lambda g, s, it, jt: (g, 0, it[s]))],
        scratch_shapes=[pltpu.VMEM((hb, bq, 128), F32), pltpu.VMEM((hb, bq, 128), F32),
                        pltpu.VMEM((hb, bq, KVR), F32)])
    return pl.pallas_call(
        kern, grid_spec=gs,
        out_shape=[jax.ShapeDtypeStruct((t, H * KVR), F32), jax.ShapeDtypeStruct((H, t, 128), F32),
                   jax.ShapeDtypeStruct((H, 1, t), F32)],
        name="mla_flash_fwd", compiler_params=_cp())(itab, jtab, qcat, kc)


def _flash_dq(qcat, kc, do_lat, o_lat, lse, bq, hb):
    t = kc.shape[0]
    nq = t // bq
    pairs = _causal_pairs(nq)
    itab = jnp.asarray(np.array([p[0] for p in pairs], np.int32))
    jtab = jnp.asarray(np.array([p[1] for p in pairs], np.int32))

    def kern(it, jt, q_ref, k_ref, do_ref, o_ref, lse_ref, dq_ref, dlt_ref, acc_sc, dl_sc):
        st = pl.program_id(1)
        i, j = it[st], jt[st]

        @pl.when(j == 0)
        def _():
            acc_sc[...] = jnp.zeros_like(acc_sc)
            for hh in range(hb):
                cs = slice(hh * KVR, (hh + 1) * KVR)
                dl = jnp.broadcast_to(jnp.sum(do_ref[:, cs] * o_ref[:, cs], axis=1)[:, None], (bq, 128))
                dl_sc[hh] = dl
                dlt_ref[hh] = dl.T[0:1, :]

        def update(masked):
            k = k_ref[...]
            v = k[:, 0:KVR]
            if masked:
                row = lax.broadcasted_iota(jnp.int32, (bq, bq), 0)
                col = lax.broadcasted_iota(jnp.int32, (bq, bq), 1)
                keep = col <= row
            for hh in range(hb):
                s = _dot(q_ref[:, hh * KD:(hh + 1) * KD], k, NT)
                if masked:
                    s = jnp.where(keep, s, NEG)
                p = jnp.exp2(s - _lane_tile(lse_ref[hh], bq))
                dp = _dot(do_ref[:, hh * KVR:(hh + 1) * KVR].astype(BF16), v, NT)
                ds = p * (dp - _lane_tile(dl_sc[hh], bq))
                acc_sc[hh] += _dot(ds.astype(BF16), k, NN)

        @pl.when(j < i)
        def _():
            update(False)

        @pl.when(j == i)
        def _():
            update(True)
            for hh in range(hb):
                dq_ref[:, hh * KD:(hh + 1) * KD] = acc_sc[hh] * MLA_SCALE

    gs = pltpu.PrefetchScalarGridSpec(
        num_scalar_prefetch=2, grid=(H // hb, len(pairs)),
        in_specs=[pl.BlockSpec((bq, hb * KD), lambda g, s, it, jt: (it[s], g)),
                  pl.BlockSpec((bq, KD), lambda g, s, it, jt: (jt[s], 0)),
                  pl.BlockSpec((bq, hb * KVR), lambda g, s, it, jt: (it[s], g)),
                  pl.BlockSpec((bq, hb * KVR), lambda g, s, it, jt: (it[s], g)),
                  pl.BlockSpec((hb, bq, 128), lambda g, s, it, jt: (g, it[s], 0))],
        out_specs=[pl.BlockSpec((bq, hb * KD), lambda g, s, it, jt: (it[s], g)),
                   pl.BlockSpec((hb, 1, bq), lambda g, s, it, jt: (g, 0, it[s]))],
        scratch_shapes=[pltpu.VMEM((hb, bq, KD), F32), pltpu.VMEM((hb, bq, 128), F32)])
    return pl.pallas_call(
        kern, grid_spec=gs,
        out_shape=[jax.ShapeDtypeStruct((t, H * KD), F32), jax.ShapeDtypeStruct((H, 1, t), F32)],
        name="mla_flash_dq", compiler_params=_cp())(itab, jtab, qcat, kc, do_lat, o_lat, lse)


def _flash_dkv(qcat, kc, do_lat, lse_t, delta_t, bq, hb):
    t = kc.shape[0]
    nq = t // bq
    ng = H // hb
    steps = [(j, g, i) for j in range(nq) for g in range(ng) for i in range(j, nq)]
    jtab = jnp.asarray(np.array([s[0] for s in steps], np.int32))
    gtab = jnp.asarray(np.array([s[1] for s in steps], np.int32))
    itab = jnp.asarray(np.array([s[2] for s in steps], np.int32))

    def kern(jt, gt, it, q_ref, k_ref, do_ref, lset_ref, dlt_ref, dk_ref, dk_sc, dv_sc):
        st = pl.program_id(0)
        j, g, i = jt[st], gt[st], it[st]

        @pl.when((g == 0) & (i == j))
        def _():
            dk_sc[...] = jnp.zeros_like(dk_sc)
            dv_sc[...] = jnp.zeros_like(dv_sc)

        def update(masked):
            k = k_ref[...]
            v = k[:, 0:KVR]
            if masked:
                row = lax.broadcasted_iota(jnp.int32, (bq, bq), 0)
                col = lax.broadcasted_iota(jnp.int32, (bq, bq), 1)
                keep = row <= col
            for hh in range(hb):
                q = q_ref[:, hh * KD:(hh + 1) * KD]
                dob = do_ref[:, hh * KVR:(hh + 1) * KVR].astype(BF16)
                s = _dot(k, q, NT)
                if masked:
                    s = jnp.where(keep, s, NEG)
                p = jnp.exp2(s - lset_ref[hh])
                dv_sc[...] += _dot(p.astype(BF16), dob, NN)
                dp = _dot(v, dob, NT)
                ds = p * (dp - dlt_ref[hh])
                dk_sc[...] += _dot(ds.astype(BF16), q, NN)

        @pl.when(i > j)
        def _():
            update(False)

        @pl.when(i == j)
        def _():
            update(True)

        @pl.when((g == ng - 1) & (i == nq - 1))
        def _():
            dk_ref[:, 0:KVR] = dk_sc[:, 0:KVR] * LN2 + dv_sc[...]
            dk_ref[:, KVR:KD] = dk_sc[:, KVR:KD] * LN2

    gs = pltpu.PrefetchScalarGridSpec(
        num_scalar_prefetch=3, grid=(len(steps),),
        in_specs=[pl.BlockSpec((bq, hb * KD), lambda s, jt, gt, it: (it[s], gt[s])),
                  pl.BlockSpec((bq, KD), lambda s, jt, gt, it: (jt[s], 0)),
                  pl.BlockSpec((bq, hb * KVR), lambda s, jt, gt, it: (it[s], gt[s])),
                  pl.BlockSpec((hb, 1, bq), lambda s, jt, gt, it: (gt[s], 0, it[s])),
                  pl.BlockSpec((hb, 1, bq), lambda s, jt, gt, it: (gt[s], 0, it[s]))],
        out_specs=pl.BlockSpec((bq, KD), lambda s, jt, gt, it: (jt[s], 0)),
        scratch_shapes=[pltpu.VMEM((bq, KD), F32), pltpu.VMEM((bq, KVR), F32)])
    return pl.pallas_call(
        kern, grid_spec=gs, out_shape=jax.ShapeDtypeStruct((t, KD), F32),
        name="mla_flash_dkv", compiler_params=_cp())(jtab, gtab, itab, qcat, kc, do_lat, lse_t, delta_t)


def _bucket_table():
    d = np.arange(WIN)
    max_exact = NBKT // 2
    nf = np.maximum(d, 1).astype(np.float32)
    large = max_exact + (np.log(nf / np.float32(max_exact)) / np.float32(math.log(WIN / max_exact))
                         * np.float32(NBKT - max_exact)).astype(np.int32)
    large = np.minimum(large, NBKT - 1)
    bucket = np.where(d < max_exact, d, large).astype(np.int32)
    jj = np.arange(2 * WIN)[:, None]
    ii = np.arange(WIN)[None, :]
    dist = ii + WIN - jj
    valid = (dist >= 0) & (dist < WIN)
    return np.where(valid, bucket[np.clip(dist, 0, WIN - 1)], -1).astype(np.int32)


def _bias_build(rel_bias, bkt):
    def kern(bk_ref, rb_ref, o_ref):
        bk = bk_ref[...]
        for hd in range(QH):
            acc = jnp.full((2 * WIN, WIN), NEG, F32)
            for b in range(NBKT):
                acc = jnp.where(bk == b, rb_ref[b, hd], acc)
            o_ref[hd] = acc

    return pl.pallas_call(
        kern, in_specs=[pl.BlockSpec(memory_space=pltpu.VMEM), pl.BlockSpec(memory_space=pltpu.SMEM)],
        out_specs=pl.BlockSpec(memory_space=pltpu.VMEM),
        out_shape=jax.ShapeDtypeStruct((QH, 2 * WIN, WIN), F32), name="swa_bias_build")(bkt, rel_bias)


def _bias_bwd(dbias, bkt):
    def kern(db_ref, bk_ref, o_ref):
        bk = bk_ref[...]
        for hd in range(QH):
            g = db_ref[hd]
            for b in range(NBKT):
                r = b * QH + hd
                o_ref[r:r + 1, :] = jnp.sum(jnp.where(bk == b, g, 0.0), axis=0, keepdims=True)

    return pl.pallas_call(
        kern, in_specs=[pl.BlockSpec(memory_space=pltpu.VMEM), pl.BlockSpec(memory_space=pltpu.VMEM)],
        out_specs=pl.BlockSpec(memory_space=pltpu.VMEM),
        out_shape=jax.ShapeDtypeStruct((NBKT * QH, WIN), F32), name="swa_bias_bwd")(dbias, bkt)


def _swa_scores(k_band, q_t, bias, first):
    s = _dot(k_band, q_t, TN) * SWA_SCALE + bias
    if first is not None:
        row = lax.broadcasted_iota(jnp.int32, s.shape, 0)
        s = jnp.where(jnp.logical_or(jnp.logical_not(first), row >= WIN), s, NEG)
    return s


def _swa_fwd(qkv_t, bias, sinks, qb):
    t = qkv_t.shape[1]
    w = qb * WIN
    nst = t // w

    def kern(q_ref, kc_ref, kp_ref, vc_ref, vp_ref, b_ref, sk_ref, o_ref, lse_ref):
        n = pl.program_id(0)
        kfull = jnp.concatenate([kp_ref[...], kc_ref[...]], axis=1)
        vfull = jnp.concatenate([vp_ref[...], vc_ref[...]], axis=1)
        for b in range(qb):
            cs = slice(b * WIN, (b + 1) * WIN)
            bs = slice(b * WIN, (b + 2) * WIN)
            for kh in range(KVH):
                k_band = kfull[kh * HD:(kh + 1) * HD, bs]
                v_band = vfull[kh * HD:(kh + 1) * HD, bs]
                for g in range(G):
                    hd = kh * G + g
                    rs = slice(hd * HD, (hd + 1) * HD)
                    s = _swa_scores(k_band, q_ref[rs, cs], b_ref[hd], (n == 0) if b == 0 else None)
                    sink = sk_ref[hd]
                    m = jnp.maximum(jnp.max(s, axis=0, keepdims=True), sink)
                    p = jnp.exp(s - m)
                    den = jnp.sum(p, axis=0, keepdims=True) + jnp.exp(sink - m)
                    p = p / den
                    o_ref[rs, cs] = _dot(v_band, p.astype(BF16), NN)
                    lse_ref[hd:hd + 1, cs] = m + jnp.log(den)

    prev = lambda r: (lambda n: (r, jnp.maximum(n * qb - 1, 0)))
    return pl.pallas_call(
        kern, grid=(nst,),
        in_specs=[pl.BlockSpec((QH * HD, w), lambda n: (0, n)),
                  pl.BlockSpec((KVH * HD, w), lambda n: (4, n)), pl.BlockSpec((KVH * HD, WIN), prev(4)),
                  pl.BlockSpec((KVH * HD, w), lambda n: (5, n)), pl.BlockSpec((KVH * HD, WIN), prev(5)),
                  pl.BlockSpec((QH, 2 * WIN, WIN), lambda n: (0, 0, 0)),
                  pl.BlockSpec(memory_space=pltpu.SMEM)],
        out_specs=[pl.BlockSpec((QH * HD, w), lambda n: (0, n)), pl.BlockSpec((QH, w), lambda n: (0, n))],
        out_shape=[jax.ShapeDtypeStruct((QH * HD, t), F32), jax.ShapeDtypeStruct((QH, t), F32)],
        name="swa_fwd", compiler_params=_cp())(qkv_t, qkv_t, qkv_t, qkv_t, qkv_t, bias, sinks)


def _swa_bwd(qkv_t, do_t, o_t, lse, bias, sinks, qb):
    t = qkv_t.shape[1]
    w = qb * WIN
    nst = t // w
    nblk = t // WIN

    def kern(q_ref, kc_ref, kp_ref, vc_ref, vp_ref, do_ref, o_ref, lse_ref, qn_ref, don_ref, on_ref, lsen_ref,
             b_ref, sk_ref, dqkv_ref, db_ref, dsk_ref, acc_sc):
        n = pl.program_id(0)

        @pl.when(n == 0)
        def _():
            db_ref[...] = jnp.zeros_like(db_ref)
            dsk_ref[...] = jnp.zeros_like(dsk_ref)

        acc_sc[...] = jnp.zeros_like(acc_sc)
        kfull = jnp.concatenate([kp_ref[...], kc_ref[...]], axis=1)
        vfull = jnp.concatenate([vp_ref[...], vc_ref[...]], axis=1)
        for b in range(qb):
            cs = slice(b * WIN, (b + 1) * WIN)
            bs = slice(b * WIN, (b + 2) * WIN)
            for kh in range(KVH):
                k_band = kfull[kh * HD:(kh + 1) * HD, bs]
                v_band = vfull[kh * HD:(kh + 1) * HD, bs]
                dk_b = jnp.zeros((HD, 2 * WIN), F32)
                dv_b = jnp.zeros((HD, 2 * WIN), F32)
                for g in range(G):
                    hd = kh * G + g
                    rs = slice(hd * HD, (hd + 1) * HD)
                    q_t = q_ref[rs, cs]
                    do = do_ref[rs, cs]
                    lse_h = lse_ref[hd:hd + 1, cs]
                    s = _swa_scores(k_band, q_t, b_ref[hd], (n == 0) if b == 0 else None)
                    p = jnp.exp(s - lse_h)
                    dob = do.astype(BF16)
                    dp = _dot(v_band, dob, TN)
                    dl = jnp.sum(do * o_ref[rs, cs], axis=0, keepdims=True)
                    ds = p * (dp - dl)
                    db_ref[hd] += ds
                    dsk_ref[hd:hd + 1, :] += -jnp.exp(sk_ref[hd] - lse_h) * dl
                    dss = (ds * SWA_SCALE).astype(BF16)
                    dqkv_ref[rs, cs] = _dot(k_band, dss, NN).astype(BF16)
                    dk_b += _dot(q_t, dss, NT)
                    dv_b += _dot(dob, p.astype(BF16), NT)
                acc_sc[kh * HD:(kh + 1) * HD, bs] += dk_b
                acc_sc[KVH * HD + kh * HD:KVH * HD + (kh + 1) * HD, bs] += dv_b

        @pl.when(n < nst - 1)
        def _():
            ls = slice((qb - 1) * WIN, qb * WIN)
            ts = slice(qb * WIN, (qb + 1) * WIN)
            for kh in range(KVH):
                k_last = kc_ref[kh * HD:(kh + 1) * HD, ls]
                v_last = vc_ref[kh * HD:(kh + 1) * HD, ls]
                dk_b = jnp.zeros((HD, WIN), F32)
                dv_b = jnp.zeros((HD, WIN), F32)
                for g in range(G):
                    hd = kh * G + g
                    rs = slice(hd * HD, (hd + 1) * HD)
                    q_t = qn_ref[rs, :]
                    do = don_ref[rs, :]
                    s = _dot(k_last, q_t, TN) * SWA_SCALE + b_ref[hd, 0:WIN, :]
                    p = jnp.exp(s - lsen_ref[hd:hd + 1, :])
                    dob = do.astype(BF16)
                    dp = _dot(v_last, dob, TN)
                    dl = jnp.sum(do * on_ref[rs, :], axis=0, keepdims=True)
                    dss = (p * (dp - dl) * SWA_SCALE).astype(BF16)
                    dk_b += _dot(q_t, dss, NT)
                    dv_b += _dot(dob, p.astype(BF16), NT)
                acc_sc[kh * HD:(kh + 1) * HD, ts] += dk_b
                acc_sc[KVH * HD + kh * HD:KVH * HD + (kh + 1) * HD, ts] += dv_b

        dqkv_ref[QH * HD:QH * HD + 2 * KVH * HD, :] = acc_sc[:, WIN:].astype(BF16)

    prev = lambda r: (lambda n: (r, jnp.maximum(n * qb - 1, 0)))
    nxt = lambda n: (0, jnp.minimum((n + 1) * qb, nblk - 1))
    big = lambda: pl.BlockSpec((QH * HD, w), lambda n: (0, n))
    return pl.pallas_call(
        kern, grid=(nst,),
        in_specs=[big(),
                  pl.BlockSpec((KVH * HD, w), lambda n: (4, n)), pl.BlockSpec((KVH * HD, WIN), prev(4)),
                  pl.BlockSpec((KVH * HD, w), lambda n: (5, n)), pl.BlockSpec((KVH * HD, WIN), prev(5)),
                  big(), big(), pl.BlockSpec((QH, w), lambda n: (0, n)),
                  pl.BlockSpec((QH * HD, WIN), nxt), pl.BlockSpec((QH * HD, WIN), nxt),
                  pl.BlockSpec((QH * HD, WIN), nxt), pl.BlockSpec((QH, WIN), nxt),
                  pl.BlockSpec((QH, 2 * WIN, WIN), lambda n: (0, 0, 0)),
                  pl.BlockSpec(memory_space=pltpu.SMEM)],
        out_specs=[pl.BlockSpec(((QH + 2 * KVH) * HD, w), lambda n: (0, n)),
                   pl.BlockSpec((QH, 2 * WIN, WIN), lambda n: (0, 0, 0)),
                   pl.BlockSpec((QH, WIN), lambda n: (0, 0))],
        out_shape=[jax.ShapeDtypeStruct(((QH + 2 * KVH) * HD, t), BF16),
                   jax.ShapeDtypeStruct((QH, 2 * WIN, WIN), F32), jax.ShapeDtypeStruct((QH, WIN), F32)],
        scratch_shapes=[pltpu.VMEM((2 * KVH * HD, w + WIN), F32)],
        name="swa_bwd", compiler_params=_cp())(
            qkv_t, qkv_t, qkv_t, qkv_t, qkv_t, do_t, o_t, lse, qkv_t, do_t, o_t, lse, bias, sinks)


def _adamw(w, g, m, v, name, tm=544):
    r = w.shape[0]
    tm = r if r % tm else tm
    c1 = 1.0 / (1.0 - B1 ** STEP)
    c2 = 1.0 / (1.0 - B2 ** STEP)

    def kern(w_ref, g_ref, m_ref, v_ref, d_ref, nm_ref, nv_ref):
        g_ = g_ref[...]
        nm = B1 * m_ref[...] + (1.0 - B1) * g_
        nv = B2 * v_ref[...] + (1.0 - B2) * (g_ * g_)
        d_ref[...] = -LR * ((nm * c1) / (jnp.sqrt(nv * c2) + ADAM_EPS) + WD * w_ref[...])
        nm_ref[...] = nm
        nv_ref[...] = nv

    row = pl.BlockSpec((tm, D), lambda i: (i, 0))
    sds = jax.ShapeDtypeStruct((r, D), F32)
    return pl.pallas_call(kern, grid=(r // tm,), in_specs=[row] * 4, out_specs=[row] * 3, out_shape=[sds] * 3,
                          name=name, compiler_params=_cp())(w, g, m, v)


def _mesh_pos():
    return lax.axis_index("x"), lax.axis_index("y"), lax.axis_index("c")


ANY = pl.BlockSpec(memory_space=pl.ANY)


def _allgather_weights(wpack):
    r = wpack.shape[0]
    half = r // 2

    def body(w_ref, out_ref, send_sems, recv_sems, local_sem):
        x, y, c = _mesh_pos()
        sibling = (x, y, 1 - c)
        chips = [(1 - x, y), (x, 1 - y), (1 - x, 1 - y)]

        def rows(px, py, pc):
            return out_ref.at[2 * px + py, pl.ds(pc * half, half), :]

        def copy(k, block, to, src=None):
            return pltpu.make_async_remote_copy(
                src_ref=rows(*block) if src is None else src, dst_ref=rows(*block),
                send_sem=send_sems.at[k], recv_sem=recv_sems.at[k], device_id=to, device_id_type=MESH)

        mine = pltpu.make_async_copy(w_ref, out_ref.at[2 * x + y], local_sem)
        mine.start()
        first = [copy(j, (x, y, c), (*chip, c), src=w_ref.at[pl.ds(c * half, half), :]) for j, chip in enumerate(chips)]
        for cp in first:
            cp.start()
        passed = [copy(3 + j, (*chip, c), sibling) for j, chip in enumerate(chips)]
        for j, chip in enumerate(chips):
            copy(j, (*chip, c), (x, y, c)).wait_recv()
            passed[j].start()
        for j, chip in enumerate(chips):
            copy(3 + j, (*chip, 1 - c), (x, y, c)).wait_recv()
        for cp in first + passed:
            cp.wait_send()
        mine.wait()

    return pl.pallas_call(
        body, out_shape=jax.ShapeDtypeStruct((4, r, D), wpack.dtype), in_specs=[ANY], out_specs=ANY,
        scratch_shapes=[pltpu.SemaphoreType.DMA((6,)), pltpu.SemaphoreType.DMA((6,)), pltpu.SemaphoreType.DMA],
        name="allgather_weights")(wpack)


def _exchange_core_halves(g):
    half = g.shape[1] // 2

    def body(g_ref, out_ref, send_sem, recv_sem):
        x, y, c = _mesh_pos()
        cp = pltpu.make_async_remote_copy(
            src_ref=g_ref.at[:, pl.ds((1 - c) * half, half), :], dst_ref=out_ref,
            send_sem=send_sem, recv_sem=recv_sem, device_id=(x, y, 1 - c), device_id_type=MESH)
        cp.start()
        cp.wait()

    return pl.pallas_call(
        body, out_shape=jax.ShapeDtypeStruct((4, half, D), g.dtype), in_specs=[ANY], out_specs=ANY,
        scratch_shapes=[pltpu.SemaphoreType.DMA, pltpu.SemaphoreType.DMA], name="rs_exchange_cores")(g)


def _add_core_halves(g, other, cidx, tm=544):
    half = other.shape[1]
    nb = half // tm

    def kern(c_ref, a_ref, b_ref, o_ref):
        o_ref[...] = a_ref[...] + b_ref[...]

    gs = pltpu.PrefetchScalarGridSpec(
        num_scalar_prefetch=1, grid=(4, nb),
        in_specs=[pl.BlockSpec((1, tm, D), lambda s, i, c: (s, c[0] * nb + i, 0)),
                  pl.BlockSpec((1, tm, D), lambda s, i, c: (s, i, 0))],
        out_specs=pl.BlockSpec((1, tm, D), lambda s, i, c: (s, i, 0)))
    return pl.pallas_call(kern, grid_spec=gs, out_shape=jax.ShapeDtypeStruct(other.shape, F32),
                          name="rs_add_cores", compiler_params=_cp())(cidx, g, other)


def _exchange_chip_shards(p):
    def body(p_ref, out_ref, send_sems, recv_sems, local_sem):
        x, y, c = _mesh_pos()
        me = 2 * x + y
        chips = [(1 - x, y), (x, 1 - y), (1 - x, 1 - y)]
        mine = pltpu.make_async_copy(p_ref.at[me], out_ref.at[me], local_sem)
        mine.start()
        sends = []
        for j, (px, py) in enumerate(chips):
            cp = pltpu.make_async_remote_copy(
                src_ref=p_ref.at[2 * px + py], dst_ref=out_ref.at[me],
                send_sem=send_sems.at[j], recv_sem=recv_sems.at[j], device_id=(px, py, c), device_id_type=MESH)
            cp.start()
            sends.append(cp)
        for j, (px, py) in enumerate(chips):
            pltpu.make_async_remote_copy(
                src_ref=p_ref.at[me], dst_ref=out_ref.at[2 * px + py],
                send_sem=send_sems.at[j], recv_sem=recv_sems.at[j], device_id=(px, py, c),
                device_id_type=MESH).wait_recv()
        for cp in sends:
            cp.wait_send()
        mine.wait()

    return pl.pallas_call(
        body, out_shape=jax.ShapeDtypeStruct(p.shape, p.dtype), in_specs=[ANY], out_specs=ANY,
        scratch_shapes=[pltpu.SemaphoreType.DMA((3,)), pltpu.SemaphoreType.DMA((3,)), pltpu.SemaphoreType.DMA],
        name="rs_exchange_chips")(p)


def _sum_slots(b, tm=544):
    half = b.shape[1]

    def kern(b_ref, o_ref):
        o_ref[...] = ((b_ref[0].astype(F32) + b_ref[1].astype(F32)) + b_ref[2].astype(F32)) + b_ref[3].astype(F32)

    return pl.pallas_call(
        kern, grid=(half // tm,), in_specs=[pl.BlockSpec((4, tm, D), lambda i: (0, i, 0))],
        out_specs=pl.BlockSpec((tm, D), lambda i: (i, 0)), out_shape=jax.ShapeDtypeStruct((half, D), F32),
        name="rs_sum_chips", compiler_params=_cp())(b)


def _join_core_halves(rh):
    half = rh.shape[0]

    def body(r_ref, out_ref, send_sem, recv_sem, local_sem):
        x, y, c = _mesh_pos()
        mine = pltpu.make_async_copy(r_ref, out_ref.at[pl.ds(c * half, half), :], local_sem)
        mine.start()
        cp = pltpu.make_async_remote_copy(
            src_ref=r_ref, dst_ref=out_ref.at[pl.ds(c * half, half), :],
            send_sem=send_sem, recv_sem=recv_sem, device_id=(x, y, 1 - c), device_id_type=MESH)
        cp.start()
        pltpu.make_async_remote_copy(
            src_ref=r_ref, dst_ref=out_ref.at[pl.ds((1 - c) * half, half), :],
            send_sem=send_sem, recv_sem=recv_sem, device_id=(x, y, 1 - c), device_id_type=MESH).wait_recv()
        cp.wait_send()
        mine.wait()

    return pl.pallas_call(
        body, out_shape=jax.ShapeDtypeStruct((2 * half, D), rh.dtype), in_specs=[ANY], out_specs=ANY,
        scratch_shapes=[pltpu.SemaphoreType.DMA, pltpu.SemaphoreType.DMA, pltpu.SemaphoreType.DMA],
        name="rs_join_cores")(rh)


def _allreduce_small(v, name):
    def body(v_ref, out_ref, gat, send_sems, recv_sems):
        x, y, c = _mesh_pos()
        me = 4 * x + 2 * y + c
        gat[me] = v_ref[...]
        sends = []
        for k in range(1, 8):
            peer = (x ^ (k >> 2), y ^ ((k >> 1) & 1), c ^ (k & 1))
            cp = pltpu.make_async_remote_copy(
                src_ref=v_ref, dst_ref=gat.at[me], send_sem=send_sems.at[k - 1], recv_sem=recv_sems.at[k - 1],
                device_id=peer, device_id_type=MESH)
            cp.start()
            sends.append(cp)
        for k in range(1, 8):
            px, py, pc = x ^ (k >> 2), y ^ ((k >> 1) & 1), c ^ (k & 1)
            pltpu.make_async_remote_copy(
                src_ref=v_ref, dst_ref=gat.at[4 * px + 2 * py + pc], send_sem=send_sems.at[k - 1],
                recv_sem=recv_sems.at[k - 1], device_id=(px, py, pc), device_id_type=MESH).wait_recv()
        for cp in sends:
            cp.wait_send()
        acc = gat[0]
        for d in range(1, 8):
            acc = acc + gat[d]
        out_ref[...] = acc

    return pl.pallas_call(
        body, out_shape=jax.ShapeDtypeStruct(v.shape, F32),
        in_specs=[pl.BlockSpec(memory_space=pltpu.VMEM)], out_specs=pl.BlockSpec(memory_space=pltpu.VMEM),
        scratch_shapes=[pltpu.VMEM((8,) + v.shape, F32), pltpu.SemaphoreType.DMA((7,)), pltpu.SemaphoreType.DMA((7,))],
        name=name)(v)


def _mlp_fwd(xb, w_up, w_down, tag):
    u, a = _mm(xb, w_up, "nn", f"mlp_up_{tag}", relu2=True)
    return u, a, _mm(a, w_down, "nn", f"mlp_down_{tag}")


def _mlp_bwd(dz, dzb, xb, u, a, w_up, w_down, tag):
    du = _mm(dzb, w_down, "nt", f"mlp_down_dx_{tag}", out_dtype=BF16, gate_u=u)
    dw_down = _mm(a, dzb, "tn", f"mlp_down_dw_{tag}")
    dw_up = _mm(xb, du, "tn", f"mlp_up_dw_{tag}")
    dx = _mm(du, w_up, "nt", f"mlp_up_dx_{tag}", addend=dz, add_scale=ALPHA)
    return dx, dw_up, dw_down


def _fwd_bwd(x, target, w, bq=512, qb=4, hb=2):
    t = x.shape[0]
    bq = min(bq, t)
    qb = min(qb, t // WIN)
    cos, sin = _rope_tables(t)
    bkt = jnp.asarray(_bucket_table())
    w_in = jnp.pad(w["mla_w_in"], ((0, 0), (0, HW - w["mla_w_in"].shape[1])))
    wuq = w["mla_w_uq"]
    wq2 = jnp.concatenate([wuq[:, :, :NOPE].reshape(QR, H * NOPE),
                           jnp.pad(wuq[:, :, NOPE:], ((0, 0), (0, 0), (0, RP - ROPE))).reshape(QR, H * RP)], axis=1)
    wuk_t = w["mla_w_uk"].transpose(1, 2, 0)
    wuk_h = w["mla_w_uk"].transpose(1, 0, 2)
    wuv_h = w["mla_w_uv"].transpose(1, 0, 2)
    w_o = w["mla_w_o"]
    wqkv = jnp.concatenate([w["swa_w_q"], w["kv_w_shared"]], axis=1)
    wqkv_t = wqkv.T
    wo_s = w["swa_w_o"]
    sinks = w["swa_sinks"].reshape(QH)
    lnp = lambda n, l: w[n][l]

    hh = _mm(x, w_in, "nn", "mla_in")
    cq, kc = _mla_pre(hh, w["mla_g_q"], w["mla_g_kv"], cos, sin)
    q2 = _mm(cq, wq2, "nn", "mla_uq")
    qcat = _q_prep(q2, wuk_t, cos, sin)
    o_lat, lse0, lse0_t = _flash_fwd(qcat, kc, bq, hb)
    o0 = _o_up(o_lat, wuv_h)
    y0 = _mm(o0, w_o, "nn", "mla_out")
    x1, x1b, xh1, r1 = _add_ln(x, y0, lnp("ln_mix_g", 0), lnp("ln_mix_b", 0), "ln_mix_0")
    u0, a0, f0 = _mlp_fwd(x1b, w["mlp_w_up"][0], w["mlp_w_down"][0], 0)
    x2, x2b, xh2, r2 = _add_ln(x1, f0, lnp("ln_mlp_g", 0), lnp("ln_mlp_b", 0), "ln_mlp_0")
    bias = _bias_build(w["rel_bias"], bkt)
    qkv_t = _mm(x2b, wqkv, "nn", "swa_qkv", out_dtype=BF16, out_t=True)
    os_t, lse1 = _swa_fwd(qkv_t, bias, sinks, qb)
    y1 = _mm(os_t, wo_s, "tn", "swa_out")
    x3, x3b, xh3, r3 = _add_ln(x2, y1, lnp("ln_mix_g", 1), lnp("ln_mix_b", 1), "ln_mix_1")
    u1, a1, f1 = _mlp_fwd(x3b, w["mlp_w_up"][1], w["mlp_w_down"][1], 1)
    x4, _, xh4, r4 = _add_ln(x3, f1, lnp("ln_mlp_g", 1), lnp("ln_mlp_b", 1), "ln_mlp_1")
    dx4, lpart = _loss_grad(x4, target)

    g = {}
    dz4, dz4b, dg_mlp1, db_mlp1 = _ln_bwd(dx4, xh4, r4, lnp("ln_mlp_g", 1), "ln_mlp_1_bwd")
    dx3, dwu1, dwd1 = _mlp_bwd(dz4, dz4b, x3b, u1, a1, w["mlp_w_up"][1], w["mlp_w_down"][1], 1)
    dz3, dz3b, dg_mix1, db_mix1 = _ln_bwd(dx3, xh3, r3, lnp("ln_mix_g", 1), "ln_mix_1_bwd")
    dos_t = _mm(dz3b, wo_s, "nt", "swa_out_dx", out_t=True)
    g["swa_w_o"] = _mm(os_t, dz3b, "nn", "swa_out_dw")
    dqkv_t, dbias, dsk = _swa_bwd(qkv_t, dos_t, os_t, lse1, bias, sinks, qb)
    dwqkv = _mm(dqkv_t, x2b, "nn", "swa_qkv_dw").T
    g["swa_w_q"], g["kv_w_shared"] = dwqkv[:, :QH * HD], dwqkv[:, QH * HD:]
    dx2 = _mm(dqkv_t, wqkv_t, "tn", "swa_qkv_dx", addend=dz3, add_scale=ALPHA)
    g["rel_bias"] = jnp.sum(_bias_bwd(dbias, bkt), axis=-1).reshape(NBKT, QH)
    g["swa_sinks"] = jnp.sum(dsk, axis=-1).reshape(1, QH)
    dz2, dz2b, dg_mlp0, db_mlp0 = _ln_bwd(dx2, xh2, r2, lnp("ln_mlp_g", 0), "ln_mlp_0_bwd")
    dx1, dwu0, dwd0 = _mlp_bwd(dz2, dz2b, x1b, u0, a0, w["mlp_w_up"][0], w["mlp_w_down"][0], 0)
    dz1, dz1b, dg_mix0, db_mix0 = _ln_bwd(dx1, xh1, r1, lnp("ln_mix_g", 0), "ln_mix_0_bwd")
    do0 = _mm(dz1b, w_o, "nt", "mla_out_dx", out_dtype=BF16)
    g["mla_w_o"] = _mm(o0, dz1b, "tn", "mla_out_dw")
    do_lat, dwuv = _o_up_bwd(do0, o_lat, wuv_h)
    g["mla_w_uv"] = dwuv.transpose(1, 0, 2)
    dq_cat, delta_t = _flash_dq(qcat, kc, do_lat, o_lat, lse0, bq, hb)
    dk = _flash_dkv(qcat, kc, do_lat, lse0_t, delta_t, bq, hb)
    dq2, dwuk = _q_prep_bwd(dq_cat, q2, wuk_h, cos, sin)
    g["mla_w_uk"] = dwuk.transpose(2, 0, 1)
    dcq = _mm(dq2, wq2, "nt", "mla_uq_dx")
    dwq2 = _mm(cq, dq2, "tn", "mla_uq_dw")
    g["mla_w_uq"] = jnp.concatenate([dwq2[:, :H * NOPE].reshape(QR, H, NOPE),
                                     dwq2[:, H * NOPE:].reshape(QR, H, RP)[:, :, :ROPE]], axis=2)
    dh, dgq, dgkv = _mla_pre_bwd(hh, dcq, dk, w["mla_g_q"], w["mla_g_kv"], cos, sin)
    g["mla_w_in"] = _mm(x, dh, "tn", "mla_in_dw")[:, :QR + KVR + ROPE]
    grad_x = _mm(dh, w_in, "nt", "mla_in_dx", addend=dz1, add_scale=ALPHA)
    g["mla_g_q"], g["mla_g_kv"] = dgq, dgkv
    g["mlp_w_up"] = jnp.stack([dwu0, dwu1])
    g["mlp_w_down"] = jnp.stack([dwd0, dwd1])
    g["ln_mix_g"] = jnp.concatenate([dg_mix0, dg_mix1], axis=0)
    g["ln_mix_b"] = jnp.concatenate([db_mix0, db_mix1], axis=0)
    g["ln_mlp_g"] = jnp.concatenate([dg_mlp0, dg_mlp1], axis=0)
    g["ln_mlp_b"] = jnp.concatenate([db_mlp0, db_mlp1], axis=0)
    return lpart, grad_x, g


def _rows(a):
    return a.reshape(-1, D)


def _pack_shards(parts):
    return jnp.concatenate([_rows(parts[n]) for n, _ in PACK], axis=0)


def _unpack_shards(buf, like):
    out, off = {}, 0
    for n, r in PACK:
        out[n] = buf[off:off + r].reshape(like[n].shape)
        off += r
    return out


def _full_from_gathered(wall, shard_shapes):
    out, off = {}, 0
    for n, r in PACK:
        sl = wall[:, off:off + r]
        off += r
        shp = shard_shapes[n]
        if n == "mlp_w_up":
            out[n] = sl.reshape(4, 2, D, D).transpose(1, 2, 0, 3).reshape(2, D, DFF)
        elif n == "mlp_w_down":
            out[n] = sl.reshape(4, 2, D, D).transpose(1, 0, 2, 3).reshape(2, DFF, D)
        elif n == "kv_w_shared":
            out[n] = sl.reshape((4 * shp[0],) + tuple(shp[1:]))
        else:
            out[n] = sl.reshape((4 * shp[1],) + tuple(shp[2:]))
    return out


def _shards_from_full(grads):
    cols = []
    for n, r in PACK:
        gfull = grads[n]
        if n == "mlp_w_up":
            cols.append(gfull.reshape(2, D, 4, D).transpose(2, 0, 1, 3).reshape(4, r, D))
        elif n == "mlp_w_down":
            cols.append(gfull.reshape(2, 4, D, D).transpose(1, 0, 2, 3).reshape(4, r, D))
        else:
            cols.append(gfull.reshape(4, r, D))
    return jnp.concatenate(cols, axis=1)


SMALL = (("ln_mix_g", 0, 2), ("ln_mix_b", 2, 2), ("ln_mlp_g", 4, 2), ("ln_mlp_b", 6, 2),
         ("swa_sinks", 8, 1), ("mla_g_q", 9, 1), ("mla_g_kv", 10, 1), ("rel_bias", 11, 1))


def _pack_small(parts):
    rows = []
    for n, _, nr in SMALL:
        a = parts[n].reshape(nr, -1).astype(F32)
        rows.append(jnp.pad(a, ((0, 0), (0, D - a.shape[1]))))
    rows.append(jnp.zeros((SMALL_ROWS - 12, D), F32))
    return jnp.concatenate(rows, axis=0)


def _unpack_small(buf, like):
    out = {}
    for n, r0, nr in SMALL:
        size = like[n].size // nr
        out[n] = buf[r0:r0 + nr, :size].reshape(like[n].shape)
    return out


def kernel(x, mla_w_in, mla_g_q, mla_g_kv, mla_w_uq, mla_w_uk, mla_w_uv, mla_w_o, kv_w_shared, swa_w_q, swa_sinks, swa_w_o, rel_bias, mlp_w_up, mlp_w_down, ln_mix_g, ln_mix_b, ln_mlp_g, ln_mlp_b, loss_target, m_mla_w_in, m_mla_g_q, m_mla_g_kv, m_mla_w_uq, m_mla_w_uk, m_mla_w_uv, m_mla_w_o, m_kv_w_shared, m_swa_w_q, m_swa_sinks, m_swa_w_o, m_rel_bias, m_mlp_w_up, m_mlp_w_down, m_ln_mix_g, m_ln_mix_b, m_ln_mlp_g, m_ln_mlp_b, v_mla_w_in, v_mla_g_q, v_mla_g_kv, v_mla_w_uq, v_mla_w_uk, v_mla_w_uv, v_mla_w_o, v_kv_w_shared, v_swa_w_q, v_swa_sinks, v_swa_w_o, v_rel_bias, v_mlp_w_up, v_mlp_w_down, v_ln_mix_g, v_ln_mix_b, v_ln_mlp_g, v_ln_mlp_b):
    names = ["mla_w_in", "mla_g_q", "mla_g_kv", "mla_w_uq", "mla_w_uk", "mla_w_uv", "mla_w_o", "kv_w_shared",
             "swa_w_q", "swa_sinks", "swa_w_o", "rel_bias", "mlp_w_up", "mlp_w_down",
             "ln_mix_g", "ln_mix_b", "ln_mlp_g", "ln_mlp_b"]
    ws = dict(zip(names, [mla_w_in, mla_g_q, mla_g_kv, mla_w_uq, mla_w_uk, mla_w_uv, mla_w_o, kv_w_shared,
                          swa_w_q, swa_sinks, swa_w_o, rel_bias, mlp_w_up, mlp_w_down,
                          ln_mix_g, ln_mix_b, ln_mlp_g, ln_mlp_b]))
    ms = dict(zip(names, [m_mla_w_in, m_mla_g_q, m_mla_g_kv, m_mla_w_uq, m_mla_w_uk, m_mla_w_uv, m_mla_w_o,
                          m_kv_w_shared, m_swa_w_q, m_swa_sinks, m_swa_w_o, m_rel_bias, m_mlp_w_up, m_mlp_w_down,
                          m_ln_mix_g, m_ln_mix_b, m_ln_mlp_g, m_ln_mlp_b]))
    vs = dict(zip(names, [v_mla_w_in, v_mla_g_q, v_mla_g_kv, v_mla_w_uq, v_mla_w_uk, v_mla_w_uv, v_mla_w_o,
                          v_kv_w_shared, v_swa_w_q, v_swa_sinks, v_swa_w_o, v_rel_bias, v_mlp_w_up, v_mlp_w_down,
                          v_ln_mix_g, v_ln_mix_b, v_ln_mlp_g, v_ln_mlp_b]))
    xi, yi, ci = _mesh_pos()
    shard = 2 * xi + yi
    big = [n for n, _ in PACK]
    shard_shapes = {n: ws[n].shape for n in big}

    wall = _allgather_weights(_pack_shards({n: ws[n].astype(BF16) for n in big}))
    w = _full_from_gathered(wall, shard_shapes)
    gq_slot = lax.dynamic_update_slice(jnp.zeros((1, QR), F32), mla_g_q, (0, shard * (QR // 4)))
    gkv_slot = lax.dynamic_update_slice(jnp.zeros((1, KVR), F32), mla_g_kv, (0, shard * (KVR // 4)))
    gains = jnp.concatenate([jnp.pad(gq_slot, ((0, 0), (0, D - QR))), jnp.pad(gkv_slot, ((0, 0), (0, D - KVR))),
                             jnp.zeros((SMALL_ROWS - 2, D), F32)], axis=0)
    gains = _allreduce_small(gains * 0.5, "allgather_gains")
    w["mla_g_q"], w["mla_g_kv"] = gains[0, :QR], gains[1, :KVR]
    for n in ("swa_sinks", "rel_bias", "ln_mix_g", "ln_mix_b", "ln_mlp_g", "ln_mlp_b"):
        w[n] = ws[n]

    lpart, grad_x, g = _fwd_bwd(x[0], loss_target[0], w)
    loss = lax.psum(0.5 * jnp.sum(lpart) / D, ("x", "y", "c"))

    gsh = _shards_from_full(g)
    other = _exchange_core_halves(gsh)
    cidx = jnp.reshape(ci, (1,)).astype(jnp.int32)
    chip_part = _add_core_halves(gsh, other, cidx)
    slots = _exchange_chip_shards(chip_part)
    gred = _join_core_halves(_sum_slots(slots))
    gbig = _unpack_shards(gred, ws)

    small_like = {n: g[n] for n, _, _ in SMALL}
    gsm = _unpack_small(_allreduce_small(_pack_small(g), "allreduce_small_grads"), small_like)
    gsm["mla_g_q"] = lax.dynamic_slice(gsm["mla_g_q"], (0, shard * (QR // 4)), (1, QR // 4))
    gsm["mla_g_kv"] = lax.dynamic_slice(gsm["mla_g_kv"], (0, shard * (KVR // 4)), (1, KVR // 4))
    grads = {**gbig, **gsm}

    dbig, mbig, vbig = _adamw(_pack_shards({n: ws[n] for n in big}), gred,
                              _pack_shards({n: ms[n] for n in big}), _pack_shards({n: vs[n] for n in big}),
                              "adamw_big")
    dsm, msm, vsm = _adamw(_pack_small(ws), _pack_small(gsm), _pack_small(ms), _pack_small(vs), "adamw_small", tm=16)
    delta = {**_unpack_shards(dbig, ws), **_unpack_small(dsm, ws)}
    new_m = {**_unpack_shards(mbig, ws), **_unpack_small(msm, ws)}
    new_v = {**_unpack_shards(vbig, ws), **_unpack_small(vsm, ws)}
    grads = {n: grads[n].reshape(ws[n].shape) for n in names}
    return (loss, grad_x[None], *[grads[n] for n in names], *[delta[n] for n in names],
            *[new_m[n] for n in names], *[new_v[n] for n in names])
```

```python
import functools
import math

import numpy as np
import jax
import jax.numpy as jnp
from jax import lax
from jax.experimental import pallas as pl
from jax.experimental.pallas import tpu as pltpu

F32 = jnp.float32
BF16 = jnp.bfloat16
MESH = pl.DeviceIdType.MESH

D = 1024
DFF = 4096
H = 8
NOPE = 128
ROPE = 64
QR = 384
KVR = 256
RP = 128
KD = KVR + RP
HW = 768
QH = 16
KVH = 4
HD = 64
G = QH // KVH
WIN = 128
NBKT = 32
ALPHA = 4.0 ** 0.25
LN_EPS = 1e-5
RMS_EPS = 1e-6
MLA_SCALE = (NOPE + ROPE) ** -0.5
LOG2E = 1.4426950408889634
LN2 = 0.6931471805599453
QSCALE = MLA_SCALE * LOG2E
SWA_SCALE = HD ** -0.5
NEG = -1e30
LR, B1, B2, ADAM_EPS, WD, STEP = 0.001, 0.9, 0.999, 1e-8, 0.01, 10

VMEM_LIMIT = 48 * 1024 * 1024

NN = (((1,), (0,)), ((), ()))
NT = (((1,), (1,)), ((), ()))
TN = (((0,), (0,)), ((), ()))

PACK = (("mlp_w_up", 2048), ("mlp_w_down", 2048), ("mla_w_o", 256), ("swa_w_q", 256), ("swa_w_o", 256),
        ("kv_w_shared", 128), ("mla_w_in", 176), ("mla_w_uq", 144), ("mla_w_uk", 64), ("mla_w_uv", 64))
PACK_ROWS = sum(r for _, r in PACK)
HALF_ROWS = PACK_ROWS // 2
SMALL_ROWS = 16


def _cp(**kw):
    return pltpu.CompilerParams(vmem_limit_bytes=VMEM_LIMIT, **kw)


def _tile(n, pref):
    t = min(n, pref)
    while n % t:
        t -= 128
    return t


def _dot(a, b, dims):
    return lax.dot_general(a, b, dims, preferred_element_type=F32)


def _mm(a, b, mode, name, out_dtype=F32, out_t=False, addend=None, add_scale=1.0, relu2=False, gate_u=None,
        tm=1024, tn=1024, tk=512):
    if mode == "nn":
        (m, k), (k2, n) = a.shape, b.shape
    elif mode == "nt":
        (m, k), (n, k2) = a.shape, b.shape
    else:
        (k, m), (k2, n) = a.shape, b.shape
    assert k == k2, (name, a.shape, b.shape)
    tm, tn, tk = _tile(m, tm), _tile(n, tn), _tile(k, tk)
    nk = k // tk
    dims = {"nn": NN, "nt": NT, "tn": TN}[mode]
    if mode == "tn":
        a_spec = pl.BlockSpec((tk, tm), lambda i, j, kk: (kk, i))
    else:
        a_spec = pl.BlockSpec((tm, tk), lambda i, j, kk: (i, kk))
    if mode == "nt":
        b_spec = pl.BlockSpec((tn, tk), lambda i, j, kk: (j, kk))
    else:
        b_spec = pl.BlockSpec((tk, tn), lambda i, j, kk: (kk, j))
    mn_spec = pl.BlockSpec((tm, tn), lambda i, j, kk: (i, j))
    ins, in_specs = [a, b], [a_spec, b_spec]
    if addend is not None:
        ins.append(addend)
        in_specs.append(mn_spec)
    if gate_u is not None:
        ins.append(gate_u)
        in_specs.append(mn_spec)
    if out_t:
        out_shape = [jax.ShapeDtypeStruct((n, m), out_dtype)]
        out_specs = [pl.BlockSpec((tn, tm), lambda i, j, kk: (j, i))]
    else:
        out_shape = [jax.ShapeDtypeStruct((m, n), out_dtype)]
        out_specs = [mn_spec]
    if relu2:
        out_shape.append(jax.ShapeDtypeStruct((m, n), BF16))
        out_specs.append(mn_spec)
    has_add, has_gate = addend is not None, gate_u is not None

    def kern(*refs):
        a_ref, b_ref = refs[0], refs[1]
        pos = 2
        add_ref = gate_ref = None
        if has_add:
            add_ref = refs[pos]
            pos += 1
        if has_gate:
            gate_ref = refs[pos]
            pos += 1
        o_ref = refs[pos]
        a2_ref = refs[pos + 1] if relu2 else None
        acc = refs[-1]
        kk = pl.program_id(2)

        @pl.when(kk == 0)
        def _():
            acc[...] = jnp.zeros_like(acc)

        acc[...] += _dot(a_ref[...].astype(BF16), b_ref[...].astype(BF16), dims)

        @pl.when(kk == nk - 1)
        def _():
            r = acc[...]
            if has_add:
                r = r + add_scale * add_ref[...].astype(F32)
            if has_gate:
                r = r * (2.0 * jnp.maximum(gate_ref[...], 0.0))
            if relu2:
                hh = jnp.maximum(r, 0.0)
                a2_ref[...] = (hh * hh).astype(BF16)
            if out_t:
                r = r.T
            o_ref[...] = r.astype(out_dtype)

    outs = pl.pallas_call(
        kern, out_shape=out_shape, grid=(m // tm, n // tn, nk), in_specs=in_specs, out_specs=out_specs,
        scratch_shapes=[pltpu.VMEM((tm, tn), F32)], name=name, compiler_params=_cp())(*ins)
    return outs if relu2 else outs[0]


def _add_ln(xres, y, g, b, name, tm=256):
    t = xres.shape[0]
    tm = min(tm, t)

    def kern(x_ref, y_ref, g_ref, b_ref, o_ref, ob_ref, xh_ref, r_ref):
        z = ALPHA * x_ref[...] + y_ref[...]
        mu = jnp.mean(z, axis=-1, keepdims=True)
        zc = z - mu
        var = jnp.mean(zc * zc, axis=-1, keepdims=True)
        r = lax.rsqrt(var + LN_EPS)
        xh = zc * r
        o = xh * g_ref[...] + b_ref[...]
        o_ref[...] = o
        ob_ref[...] = o.astype(BF16)
        xh_ref[...] = xh
        r_ref[...] = r

    row = pl.BlockSpec((tm, D), lambda i: (i, 0))
    vec = pl.BlockSpec((1, D), lambda i: (0, 0))
    st = pl.BlockSpec((tm, 1), lambda i: (i, 0))
    return pl.pallas_call(
        kern, grid=(t // tm,), in_specs=[row, row, vec, vec], out_specs=[row, row, row, st],
        out_shape=[jax.ShapeDtypeStruct((t, D), F32), jax.ShapeDtypeStruct((t, D), BF16),
                   jax.ShapeDtypeStruct((t, D), F32), jax.ShapeDtypeStruct((t, 1), F32)],
        name=name, compiler_params=_cp())(xres, y, g.reshape(1, D), b.reshape(1, D))


def _ln_bwd(dout, xhat, rstd, g, name, tm=256):
    t = dout.shape[0]
    tm = min(tm, t)

    def kern(do_ref, xh_ref, r_ref, g_ref, dz_ref, dzb_ref, dg_ref, db_ref):
        @pl.when(pl.program_id(0) == 0)
        def _():
            dg_ref[...] = jnp.zeros_like(dg_ref)
            db_ref[...] = jnp.zeros_like(db_ref)

        do = do_ref[...]
        xh = xh_ref[...]
        dxh = do * g_ref[...]
        m1 = jnp.mean(dxh, axis=-1, keepdims=True)
        m2 = jnp.mean(dxh * xh, axis=-1, keepdims=True)
        dz = r_ref[...] * (dxh - m1 - xh * m2)
        dz_ref[...] = dz
        dzb_ref[...] = dz.astype(BF16)
        dg_ref[...] += jnp.sum(do * xh, axis=0, keepdims=True)
        db_ref[...] += jnp.sum(do, axis=0, keepdims=True)

    row = pl.BlockSpec((tm, D), lambda i: (i, 0))
    vec = pl.BlockSpec((1, D), lambda i: (0, 0))
    st = pl.BlockSpec((tm, 1), lambda i: (i, 0))
    return pl.pallas_call(
        kern, grid=(t // tm,), in_specs=[row, row, st, vec], out_specs=[row, row, vec, vec],
        out_shape=[jax.ShapeDtypeStruct((t, D), F32), jax.ShapeDtypeStruct((t, D), BF16),
                   jax.ShapeDtypeStruct((1, D), F32), jax.ShapeDtypeStruct((1, D), F32)],
        name=name, compiler_params=_cp())(dout, xhat, rstd, g.reshape(1, D))


def _loss_grad(y, target, name="loss_grad", tm=256):
    t = y.shape[0]
    tm = min(tm, t)

    def kern(y_ref, t_ref, d_ref, l_ref):
        @pl.when(pl.program_id(0) == 0)
        def _():
            l_ref[...] = jnp.zeros_like(l_ref)

        e = y_ref[...] - t_ref[...]
        d_ref[...] = e * (1.0 / D)
        l_ref[...] += jnp.sum(e * e, axis=0, keepdims=True)

    row = pl.BlockSpec((tm, D), lambda i: (i, 0))
    vec = pl.BlockSpec((1, D), lambda i: (0, 0))
    return pl.pallas_call(
        kern, grid=(t // tm,), in_specs=[row, row], out_specs=[row, vec],
        out_shape=[jax.ShapeDtypeStruct((t, D), F32), jax.ShapeDtypeStruct((1, D), F32)],
        name=name, compiler_params=_cp())(y, target)


def _rope_tables(t):
    half = ROPE // 2
    inv = 10000.0 ** (-jnp.arange(half, dtype=F32) / half)
    ang = jnp.arange(t).astype(F32)[:, None] * inv[None, :]
    cos, sin = jnp.cos(ang), jnp.sin(ang)
    z = jnp.zeros((t, RP - ROPE), F32)
    return jnp.concatenate([cos, cos, z], axis=1), jnp.concatenate([-sin, sin, z], axis=1)


def _swap_halves(x):
    lane = lax.broadcasted_iota(jnp.int32, x.shape, 1)
    return jnp.where(lane < ROPE // 2, pltpu.roll(x, RP - ROPE // 2, 1), pltpu.roll(x, ROPE // 2, 1))


def _rope(x, cos, sin):
    return x * cos + _swap_halves(x) * sin


def _rope_t(gy, cos, sin):
    return gy * cos + _swap_halves(gy * sin)


def _mla_pre(hh, g_q, g_kv, cos, sin, tm=256):
    t = hh.shape[0]
    tm = min(tm, t)

    def kern(h_ref, gq_ref, gkv_ref, c_ref, s_ref, cq_ref, k_ref):
        xq = h_ref[:, 0:QR]
        rq = lax.rsqrt(jnp.mean(xq * xq, axis=-1, keepdims=True) + RMS_EPS)
        cq_ref[...] = (xq * rq * gq_ref[...]).astype(BF16)
        xk = h_ref[:, QR:QR + KVR]
        rk = lax.rsqrt(jnp.mean(xk * xk, axis=-1, keepdims=True) + RMS_EPS)
        k_ref[:, 0:KVR] = (xk * rk * gkv_ref[...]).astype(BF16)
        k_ref[:, KVR:KD] = _rope(h_ref[:, QR + KVR:HW], c_ref[...], s_ref[...]).astype(BF16)

    return pl.pallas_call(
        kern, grid=(t // tm,),
        in_specs=[pl.BlockSpec((tm, HW), lambda i: (i, 0)), pl.BlockSpec((1, QR), lambda i: (0, 0)),
                  pl.BlockSpec((1, KVR), lambda i: (0, 0)), pl.BlockSpec((tm, RP), lambda i: (i, 0)),
                  pl.BlockSpec((tm, RP), lambda i: (i, 0))],
        out_specs=[pl.BlockSpec((tm, QR), lambda i: (i, 0)), pl.BlockSpec((tm, KD), lambda i: (i, 0))],
        out_shape=[jax.ShapeDtypeStruct((t, QR), BF16), jax.ShapeDtypeStruct((t, KD), BF16)],
        name="mla_pre", compiler_params=_cp())(hh, g_q.reshape(1, QR), g_kv.reshape(1, KVR), cos, sin)


def _mla_pre_bwd(hh, dcq, dk, g_q, g_kv, cos, sin, tm=256):
    t = hh.shape[0]
    tm = min(tm, t)

    def rms_bwd(x, dy, g):
        r = lax.rsqrt(jnp.mean(x * x, axis=-1, keepdims=True) + RMS_EPS)
        gdy = dy * g
        dx = r * gdy - x * (r * r * r) * jnp.mean(gdy * x, axis=-1, keepdims=True)
        return dx, jnp.sum(dy * x * r, axis=0, keepdims=True)

    def kern(h_ref, dcq_ref, dk_ref, gq_ref, gkv_ref, c_ref, s_ref, dh_ref, dgq_ref, dgkv_ref):
        @pl.when(pl.program_id(0) == 0)
        def _():
            dgq_ref[...] = jnp.zeros_like(dgq_ref)
            dgkv_ref[...] = jnp.zeros_like(dgkv_ref)

        dxq, dgq = rms_bwd(h_ref[:, 0:QR], dcq_ref[...], gq_ref[...])
        dxk, dgk = rms_bwd(h_ref[:, QR:QR + KVR], dk_ref[:, 0:KVR], gkv_ref[...])
        dh_ref[:, 0:QR] = dxq.astype(BF16)
        dh_ref[:, QR:QR + KVR] = dxk.astype(BF16)
        dh_ref[:, QR + KVR:HW] = _rope_t(dk_ref[:, KVR:KD], c_ref[...], s_ref[...]).astype(BF16)
        dgq_ref[...] += dgq
        dgkv_ref[...] += dgk

    return pl.pallas_call(
        kern, grid=(t // tm,),
        in_specs=[pl.BlockSpec((tm, HW), lambda i: (i, 0)), pl.BlockSpec((tm, QR), lambda i: (i, 0)),
                  pl.BlockSpec((tm, KD), lambda i: (i, 0)), pl.BlockSpec((1, QR), lambda i: (0, 0)),
                  pl.BlockSpec((1, KVR), lambda i: (0, 0)), pl.BlockSpec((tm, RP), lambda i: (i, 0)),
                  pl.BlockSpec((tm, RP), lambda i: (i, 0))],
        out_specs=[pl.BlockSpec((tm, HW), lambda i: (i, 0)), pl.BlockSpec((1, QR), lambda i: (0, 0)),
                   pl.BlockSpec((1, KVR), lambda i: (0, 0))],
        out_shape=[jax.ShapeDtypeStruct((t, HW), BF16), jax.ShapeDtypeStruct((1, QR), F32),
                   jax.ShapeDtypeStruct((1, KVR), F32)],
        name="mla_pre_bwd", compiler_params=_cp())(hh, dcq, dk, g_q.reshape(1, QR), g_kv.reshape(1, KVR), cos, sin)


def _q_prep(q2, wuk_t, cos, sin, tm=256):
    t = q2.shape[0]
    tm = min(tm, t)

    def kern(q_ref, w_ref, c_ref, s_ref, o_ref):
        cos_, sin_ = c_ref[...], s_ref[...]
        for h in range(H):
            qn = q_ref[:, h * NOPE:(h + 1) * NOPE].astype(BF16)
            o_ref[:, h * KD:h * KD + KVR] = (_dot(qn, w_ref[h], NN) * QSCALE).astype(BF16)
            qr = q_ref[:, H * NOPE + h * RP:H * NOPE + (h + 1) * RP]
            o_ref[:, h * KD + KVR:(h + 1) * KD] = (_rope(qr, cos_, sin_) * QSCALE).astype(BF16)

    return pl.pallas_call(
        kern, grid=(t // tm,),
        in_specs=[pl.BlockSpec((tm, 2 * H * NOPE), lambda i: (i, 0)), pl.BlockSpec((H, NOPE, KVR), lambda i: (0, 0, 0)),
                  pl.BlockSpec((tm, RP), lambda i: (i, 0)), pl.BlockSpec((tm, RP), lambda i: (i, 0))],
        out_specs=pl.BlockSpec((tm, H * KD), lambda i: (i, 0)),
        out_shape=jax.ShapeDtypeStruct((t, H * KD), BF16),
        name="q_prep", compiler_params=_cp())(q2, wuk_t, cos, sin)


def _q_prep_bwd(dq_cat, q2, wuk_h, cos, sin, tm=256):
    t = q2.shape[0]
    tm = min(tm, t)

    def kern(dq_ref, q_ref, w_ref, c_ref, s_ref, o_ref, dw_ref):
        @pl.when(pl.program_id(0) == 0)
        def _():
            dw_ref[...] = jnp.zeros_like(dw_ref)

        cos_, sin_ = c_ref[...], s_ref[...]
        for h in range(H):
            dql = dq_ref[:, h * KD:h * KD + KVR].astype(BF16)
            o_ref[:, h * NOPE:(h + 1) * NOPE] = _dot(dql, w_ref[h], NN).astype(BF16)
            dqr = dq_ref[:, h * KD + KVR:(h + 1) * KD]
            o_ref[:, H * NOPE + h * RP:H * NOPE + (h + 1) * RP] = _rope_t(dqr, cos_, sin_).astype(BF16)
            qn = q_ref[:, h * NOPE:(h + 1) * NOPE].astype(BF16)
            dw_ref[h] += _dot(qn, dql, TN)

    return pl.pallas_call(
        kern, grid=(t // tm,),
        in_specs=[pl.BlockSpec((tm, H * KD), lambda i: (i, 0)), pl.BlockSpec((tm, 2 * H * NOPE), lambda i: (i, 0)),
                  pl.BlockSpec((H, KVR, NOPE), lambda i: (0, 0, 0)),
                  pl.BlockSpec((tm, RP), lambda i: (i, 0)), pl.BlockSpec((tm, RP), lambda i: (i, 0))],
        out_specs=[pl.BlockSpec((tm, 2 * H * NOPE), lambda i: (i, 0)), pl.BlockSpec((H, NOPE, KVR), lambda i: (0, 0, 0))],
        out_shape=[jax.ShapeDtypeStruct((t, 2 * H * NOPE), BF16), jax.ShapeDtypeStruct((H, NOPE, KVR), F32)],
        name="q_prep_bwd", compiler_params=_cp())(dq_cat, q2, wuk_h, cos, sin)


def _o_up(o_lat, wuv_h, tm=256):
    t = o_lat.shape[0]
    tm = min(tm, t)

    def kern(x_ref, w_ref, o_ref):
        for h in range(H):
            xl = x_ref[:, h * KVR:(h + 1) * KVR].astype(BF16)
            o_ref[:, h * NOPE:(h + 1) * NOPE] = _dot(xl, w_ref[h], NN).astype(BF16)

    return pl.pallas_call(
        kern, grid=(t // tm,),
        in_specs=[pl.BlockSpec((tm, H * KVR), lambda i: (i, 0)), pl.BlockSpec((H, KVR, NOPE), lambda i: (0, 0, 0))],
        out_specs=pl.BlockSpec((tm, H * NOPE), lambda i: (i, 0)),
        out_shape=jax.ShapeDtypeStruct((t, H * NOPE), BF16),
        name="o_up", compiler_params=_cp())(o_lat, wuv_h)


def _o_up_bwd(do, o_lat, wuv_h, tm=256):
    t = do.shape[0]
    tm = min(tm, t)

    def kern(do_ref, x_ref, w_ref, dx_ref, dw_ref):
        @pl.when(pl.program_id(0) == 0)
        def _():
            dw_ref[...] = jnp.zeros_like(dw_ref)

        for h in range(H):
            dh_ = do_ref[:, h * NOPE:(h + 1) * NOPE]
            dx_ref[:, h * KVR:(h + 1) * KVR] = _dot(dh_, w_ref[h], NT)
            xl = x_ref[:, h * KVR:(h + 1) * KVR].astype(BF16)
            dw_ref[h] += _dot(xl, dh_, TN)

    return pl.pallas_call(
        kern, grid=(t // tm,),
        in_specs=[pl.BlockSpec((tm, H * NOPE), lambda i: (i, 0)), pl.BlockSpec((tm, H * KVR), lambda i: (i, 0)),
                  pl.BlockSpec((H, KVR, NOPE), lambda i: (0, 0, 0))],
        out_specs=[pl.BlockSpec((tm, H * KVR), lambda i: (i, 0)), pl.BlockSpec((H, KVR, NOPE), lambda i: (0, 0, 0))],
        out_shape=[jax.ShapeDtypeStruct((t, H * KVR), F32), jax.ShapeDtypeStruct((H, KVR, NOPE), F32)],
        name="o_up_bwd", compiler_params=_cp())(do, o_lat, wuv_h)


def _causal_pairs(nq):
    return [(i, j) for i in range(nq) for j in range(i + 1)]


def _lane_tile(stat, width):
    return jnp.tile(stat, (1, width // 128))


def _flash_fwd(qcat, kc, bq, hb):
    t = kc.shape[0]
    nq = t // bq
    pairs = _causal_pairs(nq)
    itab = jnp.asarray(np.array([p[0] for p in pairs], np.int32))
    jtab = jnp.asarray(np.array([p[1] for p in pairs], np.int32))

    def kern(it, jt, q_ref, k_ref, o_ref, lse_ref, lset_ref, m_sc, l_sc, acc_sc):
        st = pl.program_id(1)
        i, j = it[st], jt[st]

        @pl.when(j == 0)
        def _():
            m_sc[...] = jnp.full_like(m_sc, NEG)
            l_sc[...] = jnp.zeros_like(l_sc)
            acc_sc[...] = jnp.zeros_like(acc_sc)

        def update(masked):
            k = k_ref[...]
            v = k[:, 0:KVR]
            if masked:
                row = lax.broadcasted_iota(jnp.int32, (bq, bq), 0)
                col = lax.broadcasted_iota(jnp.int32, (bq, bq), 1)
                keep = col <= row
            for hh in range(hb):
                s = _dot(q_ref[:, hh * KD:(hh + 1) * KD], k, NT)
                if masked:
                    s = jnp.where(keep, s, NEG)
                m_prev = m_sc[hh]
                m_next = jnp.maximum(m_prev, jnp.max(s, axis=1)[:, None])
                p = jnp.exp2(s - _lane_tile(m_next, bq))
                a = jnp.exp2(m_prev - m_next)
                l_sc[hh] = a * l_sc[hh] + jnp.sum(p, axis=1)[:, None]
                acc_sc[hh] = _lane_tile(a, KVR) * acc_sc[hh] + _dot(p.astype(BF16), v, NN)
                m_sc[hh] = m_next

        @pl.when(j < i)
        def _():
            update(False)

        @pl.when(j == i)
        def _():
            update(True)
            for hh in range(hb):
                l = l_sc[hh]
                o_ref[:, hh * KVR:(hh + 1) * KVR] = acc_sc[hh] / _lane_tile(l, KVR)
                lse = m_sc[hh] + jnp.log2(l)
                lse_ref[hh] = lse
                lset_ref[hh] = lse.T[0:1, :]

    gs = pltpu.PrefetchScalarGridSpec(
        num_scalar_prefetch=2, grid=(H // hb, len(pairs)),
        in_specs=[pl.BlockSpec((bq, hb * KD), lambda g, s, it, jt: (it[s], g)),
                  pl.BlockSpec((bq, KD), lambda g, s, it, jt: (jt[s], 0))],
        out_specs=[pl.BlockSpec((bq, hb * KVR), lambda g, s, it, jt: (it[s], g)),
                   pl.BlockSpec((hb, bq, 128), lambda g, s, it, jt: (g, it[s], 0)),
                   pl.BlockSpec((hb, 1, bq), lambda g, s, it, jt: (g, 0, it[s]))],
        scratch_shapes=[pltpu.VMEM((hb, bq, 128), F32), pltpu.VMEM((hb, bq, 128), F32),
                        pltpu.VMEM((hb, bq, KVR), F32)])
    return pl.pallas_call(
        kern, grid_spec=gs,
        out_shape=[jax.ShapeDtypeStruct((t, H * KVR), F32), jax.ShapeDtypeStruct((H, t, 128), F32),
                   jax.ShapeDtypeStruct((H, 1, t), F32)],
        name="mla_flash_fwd", compiler_params=_cp())(itab, jtab, qcat, kc)


def _flash_dq(qcat, kc, do_lat, o_lat, lse, bq, hb):
    t = kc.shape[0]
    nq = t // bq
    pairs = _causal_pairs(nq)
    itab = jnp.asarray(np.array([p[0] for p in pairs], np.int32))
    jtab = jnp.asarray(np.array([p[1] for p in pairs], np.int32))

    def kern(it, jt, q_ref, k_ref, do_ref, o_ref, lse_ref, dq_ref, dlt_ref, acc_sc, dl_sc):
        st = pl.program_id(1)
        i, j = it[st], jt[st]

        @pl.when(j == 0)
        def _():
            acc_sc[...] = jnp.zeros_like(acc_sc)
            for hh in range(hb):
                cs = slice(hh * KVR, (hh + 1) * KVR)
                dl = jnp.broadcast_to(jnp.sum(do_ref[:, cs] * o_ref[:, cs], axis=1)[:, None], (bq, 128))
                dl_sc[hh] = dl
                dlt_ref[hh] = dl.T[0:1, :]

        def update(masked):
            k = k_ref[...]
            v = k[:, 0:KVR]
            if masked:
                row = lax.broadcasted_iota(jnp.int32, (bq, bq), 0)
                col = lax.broadcasted_iota(jnp.int32, (bq, bq), 1)
                keep = col <= row
            for hh in range(hb):
                s = _dot(q_ref[:, hh * KD:(hh + 1) * KD], k, NT)
                if masked:
                    s = jnp.where(keep, s, NEG)
                p = jnp.exp2(s - _lane_tile(lse_ref[hh], bq))
                dp = _dot(do_ref[:, hh * KVR:(hh + 1) * KVR].astype(BF16), v, NT)
                ds = p * (dp - _lane_tile(dl_sc[hh], bq))
                acc_sc[hh] += _dot(ds.astype(BF16), k, NN)

        @pl.when(j < i)
        def _():
            update(False)

        @pl.when(j == i)
        def _():
            update(True)
            for hh in range(hb):
                dq_ref[:, hh * KD:(hh + 1) * KD] = acc_sc[hh] * MLA_SCALE

    gs = pltpu.PrefetchScalarGridSpec(
        num_scalar_prefetch=2, grid=(H // hb, len(pairs)),
        in_specs=[pl.BlockSpec((bq, hb * KD), lambda g, s, it, jt: (it[s], g)),
                  pl.BlockSpec((bq, KD), lambda g, s, it, jt: (jt[s], 0)),
                  pl.BlockSpec((bq, hb * KVR), lambda g, s, it, jt: (it[s], g)),
                  pl.BlockSpec((bq, hb * KVR), lambda g, s, it, jt: (it[s], g)),
                  pl.BlockSpec((hb, bq, 128), lambda g, s, it, jt: (g, it[s], 0))],
        out_specs=[pl.BlockSpec((bq, hb * KD), lambda g, s, it, jt: (it[s], g)),
                   pl.BlockSpec((hb, 1, bq), lambda g, s, it, jt: (g, 0, it[s]))],
        scratch_shapes=[pltpu.VMEM((hb, bq, KD), F32), pltpu.VMEM((hb, bq, 128), F32)])
    return pl.pallas_call(
        kern, grid_spec=gs,
        out_shape=[jax.ShapeDtypeStruct((t, H * KD), F32), jax.ShapeDtypeStruct((H, 1, t), F32)],
        name="mla_flash_dq", compiler_params=_cp())(itab, jtab, qcat, kc, do_lat, o_lat, lse)


def _flash_dkv(qcat, kc, do_lat, lse_t, delta_t, bq, hb):
    t = kc.shape[0]
    nq = t // bq
    ng = H // hb
    steps = [(j, g, i) for j in range(nq) for g in range(ng) for i in range(j, nq)]
    jtab = jnp.asarray(np.array([s[0] for s in steps], np.int32))
    gtab = jnp.asarray(np.array([s[1] for s in steps], np.int32))
    itab = jnp.asarray(np.array([s[2] for s in steps], np.int32))

    def kern(jt, gt, it, q_ref, k_ref, do_ref, lset_ref, dlt_ref, dk_ref, dk_sc, dv_sc):
        st = pl.program_id(0)
        j, g, i = jt[st], gt[st], it[st]

        @pl.when((g == 0) & (i == j))
        def _():
            dk_sc[...] = jnp.zeros_like(dk_sc)
            dv_sc[...] = jnp.zeros_like(dv_sc)

        def update(masked):
            k = k_ref[...]
            v = k[:, 0:KVR]
            if masked:
                row = lax.broadcasted_iota(jnp.int32, (bq, bq), 0)
                col = lax.broadcasted_iota(jnp.int32, (bq, bq), 1)
                keep = row <= col
            for hh in range(hb):
                q = q_ref[:, hh * KD:(hh + 1) * KD]
                dob = do_ref[:, hh * KVR:(hh + 1) * KVR].astype(BF16)
                s = _dot(k, q, NT)
                if masked:
                    s = jnp.where(keep, s, NEG)
                p = jnp.exp2(s - lset_ref[hh])
                dv_sc[...] += _dot(p.astype(BF16), dob, NN)
                dp = _dot(v, dob, NT)
                ds = p * (dp - dlt_ref[hh])
                dk_sc[...] += _dot(ds.astype(BF16), q, NN)

        @pl.when(i > j)
        def _():
            update(False)

        @pl.when(i == j)
        def _():
            update(True)

        @pl.when((g == ng - 1) & (i == nq - 1))
        def _():
            dk_ref[:, 0:KVR] = dk_sc[:, 0:KVR] * LN2 + dv_sc[...]
            dk_ref[:, KVR:KD] = dk_sc[:, KVR:KD] * LN2

    gs = pltpu.PrefetchScalarGridSpec(
        num_scalar_prefetch=3, grid=(len(steps),),
        in_specs=[pl.BlockSpec((bq, hb * KD), lambda s, jt, gt, it: (it[s], gt[s])),
                  pl.BlockSpec((bq, KD), lambda s, jt, gt, it: (jt[s], 0)),
                  pl.BlockSpec((bq, hb * KVR), lambda s, jt, gt, it: (it[s], gt[s])),
                  pl.BlockSpec((hb, 1, bq), lambda s, jt, gt, it: (gt[s], 0, it[s])),
                  pl.BlockSpec((hb, 1, bq), lambda s, jt, gt, it: (gt[s], 0, it[s]))],
        out_specs=pl.BlockSpec((bq, KD), lambda s, jt, gt, it: (jt[s], 0)),
        scratch_shapes=[pltpu.VMEM((bq, KD), F32), pltpu.VMEM((bq, KVR), F32)])
    return pl.pallas_call(
        kern, grid_spec=gs, out_shape=jax.ShapeDtypeStruct((t, KD), F32),
        name="mla_flash_dkv", compiler_params=_cp())(jtab, gtab, itab, qcat, kc, do_lat, lse_t, delta_t)


def _bucket_table():
    d = np.arange(WIN)
    max_exact = NBKT // 2
    nf = np.maximum(d, 1).astype(np.float32)
    large = max_exact + (np.log(nf / np.float32(max_exact)) / np.float32(math.log(WIN / max_exact))
                         * np.float32(NBKT - max_exact)).astype(np.int32)
    large = np.minimum(large, NBKT - 1)
    bucket = np.where(d < max_exact, d, large).astype(np.int32)
    jj = np.arange(2 * WIN)[:, None]
    ii = np.arange(WIN)[None, :]
    dist = ii + WIN - jj
    valid = (dist >= 0) & (dist < WIN)
    return np.where(valid, bucket[np.clip(dist, 0, WIN - 1)], -1).astype(np.int32)


def _bias_build(rel_bias, bkt):
    def kern(bk_ref, rb_ref, o_ref):
        bk = bk_ref[...]
        for hd in range(QH):
            acc = jnp.full((2 * WIN, WIN), NEG, F32)
            for b in range(NBKT):
                acc = jnp.where(bk == b, rb_ref[b, hd], acc)
            o_ref[hd] = acc

    return pl.pallas_call(
        kern, in_specs=[pl.BlockSpec(memory_space=pltpu.VMEM), pl.BlockSpec(memory_space=pltpu.SMEM)],
        out_specs=pl.BlockSpec(memory_space=pltpu.VMEM),
        out_shape=jax.ShapeDtypeStruct((QH, 2 * WIN, WIN), F32), name="swa_bias_build")(bkt, rel_bias)


def _bias_bwd(dbias, bkt):
    def kern(db_ref, bk_ref, o_ref):
        bk = bk_ref[...]
        for hd in range(QH):
            g = db_ref[hd]
            for b in range(NBKT):
                r = b * QH + hd
                o_ref[r:r + 1, :] = jnp.sum(jnp.where(bk == b, g, 0.0), axis=0, keepdims=True)

    return pl.pallas_call(
        kern, in_specs=[pl.BlockSpec(memory_space=pltpu.VMEM), pl.BlockSpec(memory_space=pltpu.VMEM)],
        out_specs=pl.BlockSpec(memory_space=pltpu.VMEM),
        out_shape=jax.ShapeDtypeStruct((NBKT * QH, WIN), F32), name="swa_bias_bwd")(dbias, bkt)


def _swa_scores(k_band, q_t, bias, first):
    s = _dot(k_band, q_t, TN) * SWA_SCALE + bias
    if first is not None:
        row = lax.broadcasted_iota(jnp.int32, s.shape, 0)
        s = jnp.where(jnp.logical_or(jnp.logical_not(first), row >= WIN), s, NEG)
    return s


def _swa_fwd(qkv_t, bias, sinks, qb):
    t = qkv_t.shape[1]
    w = qb * WIN
    nst = t // w

    def kern(q_ref, kc_ref, kp_ref, vc_ref, vp_ref, b_ref, sk_ref, o_ref, lse_ref):
        n = pl.program_id(0)
        kfull = jnp.concatenate([kp_ref[...], kc_ref[...]], axis=1)
        vfull = jnp.concatenate([vp_ref[...], vc_ref[...]], axis=1)
        for b in range(qb):
            cs = slice(b * WIN, (b + 1) * WIN)
            bs = slice(b * WIN, (b + 2) * WIN)
            for kh in range(KVH):
                k_band = kfull[kh * HD:(kh + 1) * HD, bs]
                v_band = vfull[kh * HD:(kh + 1) * HD, bs]
                for g in range(G):
                    hd = kh * G + g
                    rs = slice(hd * HD, (hd + 1) * HD)
                    s = _swa_scores(k_band, q_ref[rs, cs], b_ref[hd], (n == 0) if b == 0 else None)
                    sink = sk_ref[hd]
                    m = jnp.maximum(jnp.max(s, axis=0, keepdims=True), sink)
                    p = jnp.exp(s - m)
                    den = jnp.sum(p, axis=0, keepdims=True) + jnp.exp(sink - m)
                    p = p / den
                    o_ref[rs, cs] = _dot(v_band, p.astype(BF16), NN)
                    lse_ref[hd:hd + 1, cs] = m + jnp.log(den)

    prev = lambda r: (lambda n: (r, jnp.maximum(n * qb - 1, 0)))
    return pl.pallas_call(
        kern, grid=(nst,),
        in_specs=[pl.BlockSpec((QH * HD, w), lambda n: (0, n)),
                  pl.BlockSpec((KVH * HD, w), lambda n: (4, n)), pl.BlockSpec((KVH * HD, WIN), prev(4)),
                  pl.BlockSpec((KVH * HD, w), lambda n: (5, n)), pl.BlockSpec((KVH * HD, WIN), prev(5)),
                  pl.BlockSpec((QH, 2 * WIN, WIN), lambda n: (0, 0, 0)),
                  pl.BlockSpec(memory_space=pltpu.SMEM)],
        out_specs=[pl.BlockSpec((QH * HD, w), lambda n: (0, n)), pl.BlockSpec((QH, w), lambda n: (0, n))],
        out_shape=[jax.ShapeDtypeStruct((QH * HD, t), F32), jax.ShapeDtypeStruct((QH, t), F32)],
        name="swa_fwd", compiler_params=_cp())(qkv_t, qkv_t, qkv_t, qkv_t, qkv_t, bias, sinks)


def _swa_bwd(qkv_t, do_t, o_t, lse, bias, sinks, qb):
    t = qkv_t.shape[1]
    w = qb * WIN
    nst = t // w
    nblk = t // WIN

    def kern(q_ref, kc_ref, kp_ref, vc_ref, vp_ref, do_ref, o_ref, lse_ref, qn_ref, don_ref, on_ref, lsen_ref,
             b_ref, sk_ref, dqkv_ref, db_ref, dsk_ref, acc_sc):
        n = pl.program_id(0)

        @pl.when(n == 0)
        def _():
            db_ref[...] = jnp.zeros_like(db_ref)
            dsk_ref[...] = jnp.zeros_like(dsk_ref)

        acc_sc[...] = jnp.zeros_like(acc_sc)
        kfull = jnp.concatenate([kp_ref[...], kc_ref[...]], axis=1)
        vfull = jnp.concatenate([vp_ref[...], vc_ref[...]], axis=1)
        for b in range(qb):
            cs = slice(b * WIN, (b + 1) * WIN)
            bs = slice(b * WIN, (b + 2) * WIN)
            for kh in range(KVH):
                k_band = kfull[kh * HD:(kh + 1) * HD, bs]
                v_band = vfull[kh * HD:(kh + 1) * HD, bs]
                dk_b = jnp.zeros((HD, 2 * WIN), F32)
                dv_b = jnp.zeros((HD, 2 * WIN), F32)
                for g in range(G):
                    hd = kh * G + g
                    rs = slice(hd * HD, (hd + 1) * HD)
                    q_t = q_ref[rs, cs]
                    do = do_ref[rs, cs]
                    lse_h = lse_ref[hd:hd + 1, cs]
                    s = _swa_scores(k_band, q_t, b_ref[hd], (n == 0) if b == 0 else None)
                    p = jnp.exp(s - lse_h)
                    dob = do.astype(BF16)
                    dp = _dot(v_band, dob, TN)
                    dl = jnp.sum(do * o_ref[rs, cs], axis=0, keepdims=True)
                    ds = p * (dp - dl)
                    db_ref[hd] += ds
                    dsk_ref[hd:hd + 1, :] += -jnp.exp(sk_ref[hd] - lse_h) * dl
                    dss = (ds * SWA_SCALE).astype(BF16)
                    dqkv_ref[rs, cs] = _dot(k_band, dss, NN).astype(BF16)
                    dk_b += _dot(q_t, dss, NT)
                    dv_b += _dot(dob, p.astype(BF16), NT)
                acc_sc[kh * HD:(kh + 1) * HD, bs] += dk_b
                acc_sc[KVH * HD + kh * HD:KVH * HD + (kh + 1) * HD, bs] += dv_b

        @pl.when(n < nst - 1)
        def _():
            ls = slice((qb - 1) * WIN, qb * WIN)
            ts = slice(qb * WIN, (qb + 1) * WIN)
            for kh in range(KVH):
                k_last = kc_ref[kh * HD:(kh + 1) * HD, ls]
                v_last = vc_ref[kh * HD:(kh + 1) * HD, ls]
                dk_b = jnp.zeros((HD, WIN), F32)
                dv_b = jnp.zeros((HD, WIN), F32)
                for g in range(G):
                    hd = kh * G + g
                    rs = slice(hd * HD, (hd + 1) * HD)
                    q_t = qn_ref[rs, :]
                    do = don_ref[rs, :]
                    s = _dot(k_last, q_t, TN) * SWA_SCALE + b_ref[hd, 0:WIN, :]
                    p = jnp.exp(s - lsen_ref[hd:hd + 1, :])
                    dob = do.astype(BF16)
                    dp = _dot(v_last, dob, TN)
                    dl = jnp.sum(do * on_ref[rs, :], axis=0, keepdims=True)
                    dss = (p * (dp - dl) * SWA_SCALE).astype(BF16)
                    dk_b += _dot(q_t, dss, NT)
                    dv_b += _dot(dob, p.astype(BF16), NT)
                acc_sc[kh * HD:(kh + 1) * HD, ts] += dk_b
                acc_sc[KVH * HD + kh * HD:KVH * HD + (kh + 1) * HD, ts] += dv_b

        dqkv_ref[QH * HD:QH * HD + 2 * KVH * HD, :] = acc_sc[:, WIN:].astype(BF16)

    prev = lambda r: (lambda n: (r, jnp.maximum(n * qb - 1, 0)))
    nxt = lambda n: (0, jnp.minimum((n + 1) * qb, nblk - 1))
    big = lambda: pl.BlockSpec((QH * HD, w), lambda n: (0, n))
    return pl.pallas_call(
        kern, grid=(nst,),
        in_specs=[big(),
                  pl.BlockSpec((KVH * HD, w), lambda n: (4, n)), pl.BlockSpec((KVH * HD, WIN), prev(4)),
                  pl.BlockSpec((KVH * HD, w), lambda n: (5, n)), pl.BlockSpec((KVH * HD, WIN), prev(5)),
                  big(), big(), pl.BlockSpec((QH, w), lambda n: (0, n)),
                  pl.BlockSpec((QH * HD, WIN), nxt), pl.BlockSpec((QH * HD, WIN), nxt),
                  pl.BlockSpec((QH * HD, WIN), nxt), pl.BlockSpec((QH, WIN), nxt),
                  pl.BlockSpec((QH, 2 * WIN, WIN), lambda n: (0, 0, 0)),
                  pl.BlockSpec(memory_space=pltpu.SMEM)],
        out_specs=[pl.BlockSpec(((QH + 2 * KVH) * HD, w), lambda n: (0, n)),
                   pl.BlockSpec((QH, 2 * WIN, WIN), lambda n: (0, 0, 0)),
                   pl.BlockSpec((QH, WIN), lambda n: (0, 0))],
        out_shape=[jax.ShapeDtypeStruct(((QH + 2 * KVH) * HD, t), BF16),
                   jax.ShapeDtypeStruct((QH, 2 * WIN, WIN), F32), jax.ShapeDtypeStruct((QH, WIN), F32)],
        scratch_shapes=[pltpu.VMEM((2 * KVH * HD, w + WIN), F32)],
        name="swa_bwd", compiler_params=_cp())(
            qkv_t, qkv_t, qkv_t, qkv_t, qkv_t, do_t, o_t, lse, qkv_t, do_t, o_t, lse, bias, sinks)


def _adamw(w, g, m, v, name, tm=544):
    r = w.shape[0]
    tm = r if r % tm else tm
    c1 = 1.0 / (1.0 - B1 ** STEP)
    c2 = 1.0 / (1.0 - B2 ** STEP)

    def kern(w_ref, g_ref, m_ref, v_ref, d_ref, nm_ref, nv_ref):
        g_ = g_ref[...]
        nm = B1 * m_ref[...] + (1.0 - B1) * g_
        nv = B2 * v_ref[...] + (1.0 - B2) * (g_ * g_)
        d_ref[...] = -LR * ((nm * c1) / (jnp.sqrt(nv * c2) + ADAM_EPS) + WD * w_ref[...])
        nm_ref[...] = nm
        nv_ref[...] = nv

    row = pl.BlockSpec((tm, D), lambda i: (i, 0))
    sds = jax.ShapeDtypeStruct((r, D), F32)
    return pl.pallas_call(kern, grid=(r // tm,), in_specs=[row] * 4, out_specs=[row] * 3, out_shape=[sds] * 3,
                          name=name, compiler_params=_cp())(w, g, m, v)


def _mesh_pos():
    return lax.axis_index("x"), lax.axis_index("y"), lax.axis_index("c")


ANY = pl.BlockSpec(memory_space=pl.ANY)


def _allgather_weights(wpack):
    r = wpack.shape[0]
    half = r // 2

    def body(w_ref, out_ref, send_sems, recv_sems):
        x, y, c = _mesh_pos()
        sibling = (x, y, 1 - c)
        chips = [(1 - x, y), (x, 1 - y), (1 - x, 1 - y)]

        def rows(px, py, pc):
            return out_ref.at[2 * px + py, pl.ds(pc * half, half), :]

        def copy(k, block, to, src=None):
            return pltpu.make_async_remote_copy(
                src_ref=rows(*block) if src is None else src, dst_ref=rows(*block),
                send_sem=send_sems.at[k], recv_sem=recv_sems.at[k], device_id=to, device_id_type=MESH)

        first = [copy(j, (x, y, c), (*chip, c), src=w_ref.at[pl.ds(c * half, half), :]) for j, chip in enumerate(chips)]
        for cp in first:
            cp.start()
        passed = [copy(3 + j, (*chip, c), sibling) for j, chip in enumerate(chips)]
        for j, chip in enumerate(chips):
            copy(j, (*chip, c), (x, y, c)).wait_recv()
            passed[j].start()
        for j, chip in enumerate(chips):
            copy(3 + j, (*chip, 1 - c), (x, y, c)).wait_recv()
        for cp in first + passed:
            cp.wait_send()

    return pl.pallas_call(
        body, out_shape=jax.ShapeDtypeStruct((4, r, D), wpack.dtype), in_specs=[ANY], out_specs=ANY,
        scratch_shapes=[pltpu.SemaphoreType.DMA((6,)), pltpu.SemaphoreType.DMA((6,))],
        name="allgather_weights")(wpack)


def _exchange_core_halves(g):
    half = g.shape[1] // 2

    def body(g_ref, out_ref, send_sem, recv_sem):
        x, y, c = _mesh_pos()
        cp = pltpu.make_async_remote_copy(
            src_ref=g_ref.at[:, pl.ds((1 - c) * half, half), :], dst_ref=out_ref,
            send_sem=send_sem, recv_sem=recv_sem, device_id=(x, y, 1 - c), device_id_type=MESH)
        cp.start()
        cp.wait()

    return pl.pallas_call(
        body, out_shape=jax.ShapeDtypeStruct((4, half, D), g.dtype), in_specs=[ANY], out_specs=ANY,
        scratch_shapes=[pltpu.SemaphoreType.DMA, pltpu.SemaphoreType.DMA], name="rs_exchange_cores")(g)


def _add_core_halves(g, other, cidx, tm=544):
    half = other.shape[1]
    nb = half // tm

    def kern(c_ref, a_ref, b_ref, o_ref):
        o_ref[...] = (a_ref[...] + b_ref[...]).astype(BF16)

    gs = pltpu.PrefetchScalarGridSpec(
        num_scalar_prefetch=1, grid=(4, nb),
        in_specs=[pl.BlockSpec((1, tm, D), lambda s, i, c: (s, c[0] * nb + i, 0)),
                  pl.BlockSpec((1, tm, D), lambda s, i, c: (s, i, 0))],
        out_specs=pl.BlockSpec((1, tm, D), lambda s, i, c: (s, i, 0)))
    return pl.pallas_call(kern, grid_spec=gs, out_shape=jax.ShapeDtypeStruct(other.shape, BF16),
                          name="rs_add_cores", compiler_params=_cp())(cidx, g, other)


def _exchange_chip_shards(p):
    def body(p_ref, out_ref, send_sems, recv_sems):
        x, y, c = _mesh_pos()
        me = 2 * x + y
        chips = [(1 - x, y), (x, 1 - y), (1 - x, 1 - y)]
        sends = []
        for j, (px, py) in enumerate(chips):
            cp = pltpu.make_async_remote_copy(
                src_ref=p_ref.at[2 * px + py], dst_ref=out_ref.at[me],
                send_sem=send_sems.at[j], recv_sem=recv_sems.at[j], device_id=(px, py, c), device_id_type=MESH)
            cp.start()
            sends.append(cp)
        for j, (px, py) in enumerate(chips):
            pltpu.make_async_remote_copy(
                src_ref=p_ref.at[me], dst_ref=out_ref.at[2 * px + py],
                send_sem=send_sems.at[j], recv_sem=recv_sems.at[j], device_id=(px, py, c),
                device_id_type=MESH).wait_recv()
        for cp in sends:
            cp.wait_send()

    return pl.pallas_call(
        body, out_shape=jax.ShapeDtypeStruct(p.shape, p.dtype), in_specs=[ANY], out_specs=ANY,
        scratch_shapes=[pltpu.SemaphoreType.DMA((3,)), pltpu.SemaphoreType.DMA((3,))],
        name="rs_exchange_chips")(p)


def _sum_slots(slots, p, pos, tm=544):
    half = slots.shape[1]
    nb = half // tm

    def kern(pos_ref, p_ref, s1_ref, s2_ref, s3_ref, o_ref):
        o_ref[...] = ((p_ref[0].astype(F32) + s1_ref[0].astype(F32)) + s2_ref[0].astype(F32)) + s3_ref[0].astype(F32)

    def slot(k):
        return pl.BlockSpec((1, tm, D), lambda i, pos: ((pos[0] + k) % 4, i, 0))

    gs = pltpu.PrefetchScalarGridSpec(
        num_scalar_prefetch=1, grid=(nb,), in_specs=[slot(0), slot(1), slot(2), slot(3)],
        out_specs=pl.BlockSpec((tm, D), lambda i, pos: (pos[1] * nb + i, 0)))
    return pl.pallas_call(kern, grid_spec=gs, out_shape=jax.ShapeDtypeStruct((2 * half, D), F32),
                          name="rs_sum_chips", compiler_params=_cp())(pos, p, slots, slots, slots)


def _join_core_halves(r):
    half = r.shape[0] // 2

    def body(r_ref, out_ref, send_sem, recv_sem):
        x, y, c = _mesh_pos()
        mine = out_ref.at[pl.ds(c * half, half), :]
        cp = pltpu.make_async_remote_copy(
            src_ref=mine, dst_ref=mine, send_sem=send_sem, recv_sem=recv_sem,
            device_id=(x, y, 1 - c), device_id_type=MESH)
        cp.start()
        theirs = out_ref.at[pl.ds((1 - c) * half, half), :]
        pltpu.make_async_remote_copy(
            src_ref=theirs, dst_ref=theirs, send_sem=send_sem, recv_sem=recv_sem,
            device_id=(x, y, 1 - c), device_id_type=MESH).wait_recv()
        cp.wait_send()

    return pl.pallas_call(
        body, out_shape=jax.ShapeDtypeStruct(r.shape, r.dtype), in_specs=[ANY], out_specs=ANY,
        input_output_aliases={0: 0},
        scratch_shapes=[pltpu.SemaphoreType.DMA, pltpu.SemaphoreType.DMA],
        name="rs_join_cores")(r)


def _allreduce_small(v, name):
    def body(v_ref, out_ref, gat, send_sems, recv_sems):
        x, y, c = _mesh_pos()
        me = 4 * x + 2 * y + c
        gat[me] = v_ref[...]
        sends = []
        for k in range(1, 8):
            peer = (x ^ (k >> 2), y ^ ((k >> 1) & 1), c ^ (k & 1))
            cp = pltpu.make_async_remote_copy(
                src_ref=v_ref, dst_ref=gat.at[me], send_sem=send_sems.at[k - 1], recv_sem=recv_sems.at[k - 1],
                device_id=peer, device_id_type=MESH)
            cp.start()
            sends.append(cp)
        for k in range(1, 8):
            px, py, pc = x ^ (k >> 2), y ^ ((k >> 1) & 1), c ^ (k & 1)
            pltpu.make_async_remote_copy(
                src_ref=v_ref, dst_ref=gat.at[4 * px + 2 * py + pc], send_sem=send_sems.at[k - 1],
                recv_sem=recv_sems.at[k - 1], device_id=(px, py, pc), device_id_type=MESH).wait_recv()
        for cp in sends:
            cp.wait_send()
        acc = gat[0]
        for d in range(1, 8):
            acc = acc + gat[d]
        out_ref[...] = acc

    return pl.pallas_call(
        body, out_shape=jax.ShapeDtypeStruct(v.shape, F32),
        in_specs=[pl.BlockSpec(memory_space=pltpu.VMEM)], out_specs=pl.BlockSpec(memory_space=pltpu.VMEM),
        scratch_shapes=[pltpu.VMEM((8,) + v.shape, F32), pltpu.SemaphoreType.DMA((7,)), pltpu.SemaphoreType.DMA((7,))],
        name=name)(v)


def _mlp_fwd(xb, w_up, w_down, tag):
    u, a = _mm(xb, w_up, "nn", f"mlp_up_{tag}", relu2=True)
    return u, a, _mm(a, w_down, "nn", f"mlp_down_{tag}")


def _mlp_bwd(dz, dzb, xb, u, a, w_up, w_down, tag):
    du = _mm(dzb, w_down, "nt", f"mlp_down_dx_{tag}", out_dtype=BF16, gate_u=u)
    dw_down = _mm(a, dzb, "tn", f"mlp_down_dw_{tag}")
    dw_up = _mm(xb, du, "tn", f"mlp_up_dw_{tag}")
    dx = _mm(du, w_up, "nt", f"mlp_up_dx_{tag}", addend=dz, add_scale=ALPHA)
    return dx, dw_up, dw_down


def _fwd_bwd(x, target, w, bq=512, qb=4, hb=2):
    t = x.shape[0]
    bq = min(bq, t)
    qb = min(qb, t // WIN)
    cos, sin = _rope_tables(t)
    bkt = jnp.asarray(_bucket_table())
    w_in = jnp.pad(w["mla_w_in"], ((0, 0), (0, HW - w["mla_w_in"].shape[1])))
    wuq = w["mla_w_uq"]
    wq2 = jnp.concatenate([wuq[:, :, :NOPE].reshape(QR, H * NOPE),
                           jnp.pad(wuq[:, :, NOPE:], ((0, 0), (0, 0), (0, RP - ROPE))).reshape(QR, H * RP)], axis=1)
    wuk_t = w["mla_w_uk"].transpose(1, 2, 0)
    wuk_h = w["mla_w_uk"].transpose(1, 0, 2)
    wuv_h = w["mla_w_uv"].transpose(1, 0, 2)
    w_o = w["mla_w_o"]
    wqkv = jnp.concatenate([w["swa_w_q"], w["kv_w_shared"]], axis=1)
    wqkv_t = wqkv.T
    wo_s = w["swa_w_o"]
    sinks = w["swa_sinks"].reshape(QH)
    lnp = lambda n, l: w[n][l]

    hh = _mm(x, w_in, "nn", "mla_in")
    cq, kc = _mla_pre(hh, w["mla_g_q"], w["mla_g_kv"], cos, sin)
    q2 = _mm(cq, wq2, "nn", "mla_uq")
    qcat = _q_prep(q2, wuk_t, cos, sin)
    o_lat, lse0, lse0_t = _flash_fwd(qcat, kc, bq, hb)
    o0 = _o_up(o_lat, wuv_h)
    y0 = _mm(o0, w_o, "nn", "mla_out")
    x1, x1b, xh1, r1 = _add_ln(x, y0, lnp("ln_mix_g", 0), lnp("ln_mix_b", 0), "ln_mix_0")
    u0, a0, f0 = _mlp_fwd(x1b, w["mlp_w_up"][0], w["mlp_w_down"][0], 0)
    x2, x2b, xh2, r2 = _add_ln(x1, f0, lnp("ln_mlp_g", 0), lnp("ln_mlp_b", 0), "ln_mlp_0")
    bias = _bias_build(w["rel_bias"], bkt)
    qkv_t = _mm(x2b, wqkv, "nn", "swa_qkv", out_dtype=BF16, out_t=True)
    os_t, lse1 = _swa_fwd(qkv_t, bias, sinks, qb)
    y1 = _mm(os_t, wo_s, "tn", "swa_out")
    x3, x3b, xh3, r3 = _add_ln(x2, y1, lnp("ln_mix_g", 1), lnp("ln_mix_b", 1), "ln_mix_1")
    u1, a1, f1 = _mlp_fwd(x3b, w["mlp_w_up"][1], w["mlp_w_down"][1], 1)
    x4, _, xh4, r4 = _add_ln(x3, f1, lnp("ln_mlp_g", 1), lnp("ln_mlp_b", 1), "ln_mlp_1")
    dx4, lpart = _loss_grad(x4, target)

    g = {}
    dz4, dz4b, dg_mlp1, db_mlp1 = _ln_bwd(dx4, xh4, r4, lnp("ln_mlp_g", 1), "ln_mlp_1_bwd")
    dx3, dwu1, dwd1 = _mlp_bwd(dz4, dz4b, x3b, u1, a1, w["mlp_w_up"][1], w["mlp_w_down"][1], 1)
    dz3, dz3b, dg_mix1, db_mix1 = _ln_bwd(dx3, xh3, r3, lnp("ln_mix_g", 1), "ln_mix_1_bwd")
    dos_t = _mm(dz3b, wo_s, "nt", "swa_out_dx", out_t=True)
    g["swa_w_o"] = _mm(os_t, dz3b, "nn", "swa_out_dw")
    dqkv_t, dbias, dsk = _swa_bwd(qkv_t, dos_t, os_t, lse1, bias, sinks, qb)
    dwqkv = _mm(dqkv_t, x2b, "nn", "swa_qkv_dw").T
    g["swa_w_q"], g["kv_w_shared"] = dwqkv[:, :QH * HD], dwqkv[:, QH * HD:]
    dx2 = _mm(dqkv_t, wqkv_t, "tn", "swa_qkv_dx", addend=dz3, add_scale=ALPHA)
    g["rel_bias"] = jnp.sum(_bias_bwd(dbias, bkt), axis=-1).reshape(NBKT, QH)
    g["swa_sinks"] = jnp.sum(dsk, axis=-1).reshape(1, QH)
    dz2, dz2b, dg_mlp0, db_mlp0 = _ln_bwd(dx2, xh2, r2, lnp("ln_mlp_g", 0), "ln_mlp_0_bwd")
    dx1, dwu0, dwd0 = _mlp_bwd(dz2, dz2b, x1b, u0, a0, w["mlp_w_up"][0], w["mlp_w_down"][0], 0)
    dz1, dz1b, dg_mix0, db_mix0 = _ln_bwd(dx1, xh1, r1, lnp("ln_mix_g", 0), "ln_mix_0_bwd")
    do0 = _mm(dz1b, w_o, "nt", "mla_out_dx", out_dtype=BF16)
    g["mla_w_o"] = _mm(o0, dz1b, "tn", "mla_out_dw")
    do_lat, dwuv = _o_up_bwd(do0, o_lat, wuv_h)
    g["mla_w_uv"] = dwuv.transpose(1, 0, 2)
    dq_cat, delta_t = _flash_dq(qcat, kc, do_lat, o_lat, lse0, bq, hb)
    dk = _flash_dkv(qcat, kc, do_lat, lse0_t, delta_t, bq, hb)
    dq2, dwuk = _q_prep_bwd(dq_cat, q2, wuk_h, cos, sin)
    g["mla_w_uk"] = dwuk.transpose(2, 0, 1)
    dcq = _mm(dq2, wq2, "nt", "mla_uq_dx")
    dwq2 = _mm(cq, dq2, "tn", "mla_uq_dw")
    g["mla_w_uq"] = jnp.concatenate([dwq2[:, :H * NOPE].reshape(QR, H, NOPE),
                                     dwq2[:, H * NOPE:].reshape(QR, H, RP)[:, :, :ROPE]], axis=2)
    dh, dgq, dgkv = _mla_pre_bwd(hh, dcq, dk, w["mla_g_q"], w["mla_g_kv"], cos, sin)
    g["mla_w_in"] = _mm(x, dh, "tn", "mla_in_dw")[:, :QR + KVR + ROPE]
    grad_x = _mm(dh, w_in, "nt", "mla_in_dx", addend=dz1, add_scale=ALPHA)
    g["mla_g_q"], g["mla_g_kv"] = dgq, dgkv
    g["mlp_w_up"] = jnp.stack([dwu0, dwu1])
    g["mlp_w_down"] = jnp.stack([dwd0, dwd1])
    g["ln_mix_g"] = jnp.concatenate([dg_mix0, dg_mix1], axis=0)
    g["ln_mix_b"] = jnp.concatenate([db_mix0, db_mix1], axis=0)
    g["ln_mlp_g"] = jnp.concatenate([dg_mlp0, dg_mlp1], axis=0)
    g["ln_mlp_b"] = jnp.concatenate([db_mlp0, db_mlp1], axis=0)
    return lpart, grad_x, g


def _rows(a):
    return a.reshape(-1, D)


def _pack_shards(parts):
    return jnp.concatenate([_rows(parts[n]) for n, _ in PACK], axis=0)


def _unpack_shards(buf, like):
    out, off = {}, 0
    for n, r in PACK:
        out[n] = buf[off:off + r].reshape(like[n].shape)
        off += r
    return out


def _full_from_gathered(wall, shard_shapes):
    out, off = {}, 0
    for n, r in PACK:
        sl = wall[:, off:off + r]
        off += r
        shp = shard_shapes[n]
        if n == "mlp_w_up":
            out[n] = sl.reshape(4, 2, D, D).transpose(1, 2, 0, 3).reshape(2, D, DFF)
        elif n == "mlp_w_down":
            out[n] = sl.reshape(4, 2, D, D).transpose(1, 0, 2, 3).reshape(2, DFF, D)
        elif n == "kv_w_shared":
            out[n] = sl.reshape((4 * shp[0],) + tuple(shp[1:]))
        else:
            out[n] = sl.reshape((4 * shp[1],) + tuple(shp[2:]))
    return out


def _shards_from_full(grads):
    cols = []
    for n, r in PACK:
        gfull = grads[n]
        if n == "mlp_w_up":
            cols.append(gfull.reshape(2, D, 4, D).transpose(2, 0, 1, 3).reshape(4, r, D))
        elif n == "mlp_w_down":
            cols.append(gfull.reshape(2, 4, D, D).transpose(1, 0, 2, 3).reshape(4, r, D))
        else:
            cols.append(gfull.reshape(4, r, D))
    return jnp.concatenate(cols, axis=1)


SMALL = (("ln_mix_g", 0, 2), ("ln_mix_b", 2, 2), ("ln_mlp_g", 4, 2), ("ln_mlp_b", 6, 2),
         ("swa_sinks", 8, 1), ("mla_g_q", 9, 1), ("mla_g_kv", 10, 1), ("rel_bias", 11, 1))


def _pack_small(parts):
    rows = []
    for n, _, nr in SMALL:
        a = parts[n].reshape(nr, -1).astype(F32)
        rows.append(jnp.pad(a, ((0, 0), (0, D - a.shape[1]))))
    rows.append(jnp.zeros((SMALL_ROWS - 12, D), F32))
    return jnp.concatenate(rows, axis=0)


def _unpack_small(buf, like):
    out = {}
    for n, r0, nr in SMALL:
        size = like[n].size // nr
        out[n] = buf[r0:r0 + nr, :size].reshape(like[n].shape)
    return out


def kernel(x, mla_w_in, mla_g_q, mla_g_kv, mla_w_uq, mla_w_uk, mla_w_uv, mla_w_o, kv_w_shared, swa_w_q, swa_sinks, swa_w_o, rel_bias, mlp_w_up, mlp_w_down, ln_mix_g, ln_mix_b, ln_mlp_g, ln_mlp_b, loss_target, m_mla_w_in, m_mla_g_q, m_mla_g_kv, m_mla_w_uq, m_mla_w_uk, m_mla_w_uv, m_mla_w_o, m_kv_w_shared, m_swa_w_q, m_swa_sinks, m_swa_w_o, m_rel_bias, m_mlp_w_up, m_mlp_w_down, m_ln_mix_g, m_ln_mix_b, m_ln_mlp_g, m_ln_mlp_b, v_mla_w_in, v_mla_g_q, v_mla_g_kv, v_mla_w_uq, v_mla_w_uk, v_mla_w_uv, v_mla_w_o, v_kv_w_shared, v_swa_w_q, v_swa_sinks, v_swa_w_o, v_rel_bias, v_mlp_w_up, v_mlp_w_down, v_ln_mix_g, v_ln_mix_b, v_ln_mlp_g, v_ln_mlp_b):
    names = ["mla_w_in", "mla_g_q", "mla_g_kv", "mla_w_uq", "mla_w_uk", "mla_w_uv", "mla_w_o", "kv_w_shared",
             "swa_w_q", "swa_sinks", "swa_w_o", "rel_bias", "mlp_w_up", "mlp_w_down",
             "ln_mix_g", "ln_mix_b", "ln_mlp_g", "ln_mlp_b"]
    ws = dict(zip(names, [mla_w_in, mla_g_q, mla_g_kv, mla_w_uq, mla_w_uk, mla_w_uv, mla_w_o, kv_w_shared,
                          swa_w_q, swa_sinks, swa_w_o, rel_bias, mlp_w_up, mlp_w_down,
                          ln_mix_g, ln_mix_b, ln_mlp_g, ln_mlp_b]))
    ms = dict(zip(names, [m_mla_w_in, m_mla_g_q, m_mla_g_kv, m_mla_w_uq, m_mla_w_uk, m_mla_w_uv, m_mla_w_o,
                          m_kv_w_shared, m_swa_w_q, m_swa_sinks, m_swa_w_o, m_rel_bias, m_mlp_w_up, m_mlp_w_down,
                          m_ln_mix_g, m_ln_mix_b, m_ln_mlp_g, m_ln_mlp_b]))
    vs = dict(zip(names, [v_mla_w_in, v_mla_g_q, v_mla_g_kv, v_mla_w_uq, v_mla_w_uk, v_mla_w_uv, v_mla_w_o,
                          v_kv_w_shared, v_swa_w_q, v_swa_sinks, v_swa_w_o, v_rel_bias, v_mlp_w_up, v_mlp_w_down,
                          v_ln_mix_g, v_ln_mix_b, v_ln_mlp_g, v_ln_mlp_b]))
    xi, yi, ci = _mesh_pos()
    shard = 2 * xi + yi
    big = [n for n, _ in PACK]
    shard_shapes = {n: ws[n].shape for n in big}

    wpack = _pack_shards({n: ws[n].astype(BF16) for n in big})
    wall = lax.dynamic_update_slice(_allgather_weights(wpack), wpack[None], (shard, 0, 0))
    w = _full_from_gathered(wall, shard_shapes)
    gq_slot = lax.dynamic_update_slice(jnp.zeros((1, QR), F32), mla_g_q, (0, shard * (QR // 4)))
    gkv_slot = lax.dynamic_update_slice(jnp.zeros((1, KVR), F32), mla_g_kv, (0, shard * (KVR // 4)))
    gains = jnp.concatenate([jnp.pad(gq_slot, ((0, 0), (0, D - QR))), jnp.pad(gkv_slot, ((0, 0), (0, D - KVR))),
                             jnp.zeros((SMALL_ROWS - 2, D), F32)], axis=0)
    gains = _allreduce_small(gains * 0.5, "allgather_gains")
    w["mla_g_q"], w["mla_g_kv"] = gains[0, :QR], gains[1, :KVR]
    for n in ("swa_sinks", "rel_bias", "ln_mix_g", "ln_mix_b", "ln_mlp_g", "ln_mlp_b"):
        w[n] = ws[n]

    lpart, grad_x, g = _fwd_bwd(x[0], loss_target[0], w)
    loss = lax.psum(0.5 * jnp.sum(lpart) / D, ("x", "y", "c"))

    gsh = _shards_from_full(g)
    other = _exchange_core_halves(gsh)
    cidx = jnp.reshape(ci, (1,)).astype(jnp.int32)
    chip_part = _add_core_halves(gsh, other, cidx)
    slots = _exchange_chip_shards(chip_part)
    pos = jnp.stack([shard, ci]).astype(jnp.int32)
    gred = _join_core_halves(_sum_slots(slots, chip_part, pos))
    gbig = _unpack_shards(gred, ws)

    small_like = {n: g[n] for n, _, _ in SMALL}
    gsm = _unpack_small(_allreduce_small(_pack_small(g), "allreduce_small_grads"), small_like)
    gsm["mla_g_q"] = lax.dynamic_slice(gsm["mla_g_q"], (0, shard * (QR // 4)), (1, QR // 4))
    gsm["mla_g_kv"] = lax.dynamic_slice(gsm["mla_g_kv"], (0, shard * (KVR // 4)), (1, KVR // 4))
    grads = {**gbig, **gsm}

    dbig, mbig, vbig = _adamw(_pack_shards({n: ws[n] for n in big}), gred,
                              _pack_shards({n: ms[n] for n in big}), _pack_shards({n: vs[n] for n in big}),
                              "adamw_big")
    dsm, msm, vsm = _adamw(_pack_small(ws), _pack_small(gsm), _pack_small(ms), _pack_small(vs), "adamw_small", tm=16)
    delta = {**_unpack_shards(dbig, ws), **_unpack_small(dsm, ws)}
    new_m = {**_unpack_shards(mbig, ws), **_unpack_small(msm, ws)}
    new_v = {**_unpack_shards(vbig, ws), **_unpack_small(vsm, ws)}
    grads = {n: grads[n].reshape(ws[n].shape) for n in names}
    return (loss, grad_x[None], *[grads[n] for n in names], *[delta[n] for n in names],
            *[new_m[n] for n in names], *[new_v[n] for n in names])
```

```python
import functools
import math

import numpy as np
import jax
import jax.numpy as jnp
from jax import lax
from jax.experimental import pallas as pl
from jax.experimental.pallas import tpu as pltpu

F32 = jnp.float32
BF16 = jnp.bfloat16
MESH = pl.DeviceIdType.MESH

D = 1024
DFF = 4096
H = 8
NOPE = 128
ROPE = 64
QR = 384
KVR = 256
RP = 128
KD = KVR + RP
HW = 768
QH = 16
KVH = 4
HD = 64
G = QH // KVH
WIN = 128
NBKT = 32
ALPHA = 4.0 ** 0.25
LN_EPS = 1e-5
RMS_EPS = 1e-6
MLA_SCALE = (NOPE + ROPE) ** -0.5
LOG2E = 1.4426950408889634
LN2 = 0.6931471805599453
QSCALE = MLA_SCALE * LOG2E
SWA_SCALE = HD ** -0.5
NEG = -1e30
LR, B1, B2, ADAM_EPS, WD, STEP = 0.001, 0.9, 0.999, 1e-8, 0.01, 10

VMEM_LIMIT = 48 * 1024 * 1024

NN = (((1,), (0,)), ((), ()))
NT = (((1,), (1,)), ((), ()))
TN = (((0,), (0,)), ((), ()))

PACK = (("mlp_w_up", 2048), ("mlp_w_down", 2048), ("mla_w_o", 256), ("swa_w_q", 256), ("swa_w_o", 256),
        ("kv_w_shared", 128), ("mla_w_in", 176), ("mla_w_uq", 144), ("mla_w_uk", 64), ("mla_w_uv", 64))
PACK_ROWS = sum(r for _, r in PACK)
HALF_ROWS = PACK_ROWS // 2
SMALL_ROWS = 16


def _cp(**kw):
    return pltpu.CompilerParams(vmem_limit_bytes=VMEM_LIMIT, **kw)


def _tile(n, pref):
    t = min(n, pref)
    while n % t:
        t -= 128
    return t


def _dot(a, b, dims):
    return lax.dot_general(a, b, dims, preferred_element_type=F32)


def _mm(a, b, mode, name, out_dtype=F32, out_t=False, addend=None, add_scale=1.0, relu2=False, gate_u=None,
        tm=1024, tn=1024, tk=1024):
    if mode == "nn":
        (m, k), (k2, n) = a.shape, b.shape
    elif mode == "nt":
        (m, k), (n, k2) = a.shape, b.shape
    else:
        (k, m), (k2, n) = a.shape, b.shape
    assert k == k2, (name, a.shape, b.shape)
    tm, tn, tk = _tile(m, tm), _tile(n, tn), _tile(k, tk)
    nk = k // tk
    dims = {"nn": NN, "nt": NT, "tn": TN}[mode]
    if mode == "tn":
        a_spec = pl.BlockSpec((tk, tm), lambda i, j, kk: (kk, i))
    else:
        a_spec = pl.BlockSpec((tm, tk), lambda i, j, kk: (i, kk))
    if mode == "nt":
        b_spec = pl.BlockSpec((tn, tk), lambda i, j, kk: (j, kk))
    else:
        b_spec = pl.BlockSpec((tk, tn), lambda i, j, kk: (kk, j))
    mn_spec = pl.BlockSpec((tm, tn), lambda i, j, kk: (i, j))
    ins, in_specs = [a, b], [a_spec, b_spec]
    if addend is not None:
        ins.append(addend)
        in_specs.append(mn_spec)
    if gate_u is not None:
        ins.append(gate_u)
        in_specs.append(mn_spec)
    if out_t:
        out_shape = [jax.ShapeDtypeStruct((n, m), out_dtype)]
        out_specs = [pl.BlockSpec((tn, tm), lambda i, j, kk: (j, i))]
    else:
        out_shape = [jax.ShapeDtypeStruct((m, n), out_dtype)]
        out_specs = [mn_spec]
    if relu2:
        out_shape.append(jax.ShapeDtypeStruct((m, n), BF16))
        out_specs.append(mn_spec)
    has_add, has_gate = addend is not None, gate_u is not None

    def kern(*refs):
        a_ref, b_ref = refs[0], refs[1]
        pos = 2
        add_ref = gate_ref = None
        if has_add:
            add_ref = refs[pos]
            pos += 1
        if has_gate:
            gate_ref = refs[pos]
            pos += 1
        o_ref = refs[pos]
        a2_ref = refs[pos + 1] if relu2 else None
        acc = refs[-1] if nk > 1 else None
        kk = pl.program_id(2)

        def partial():
            return _dot(a_ref[...].astype(BF16), b_ref[...].astype(BF16), dims)

        if nk > 1:
            @pl.when(kk == 0)
            def _():
                acc[...] = partial()

            @pl.when((kk > 0) & (kk < nk - 1))
            def _():
                acc[...] += partial()

        @pl.when(kk == nk - 1)
        def _():
            r = partial() + acc[...] if nk > 1 else partial()
            if has_add:
                r = r + add_scale * add_ref[...].astype(F32)
            if has_gate:
                r = r * (2.0 * jnp.maximum(gate_ref[...], 0.0))
            if relu2:
                hh = jnp.maximum(r, 0.0)
                a2_ref[...] = (hh * hh).astype(BF16)
            if out_t:
                r = r.T
            o_ref[...] = r.astype(out_dtype)

    outs = pl.pallas_call(
        kern, out_shape=out_shape, grid=(m // tm, n // tn, nk), in_specs=in_specs, out_specs=out_specs,
        scratch_shapes=[pltpu.VMEM((tm, tn), F32)] if nk > 1 else [], name=name, compiler_params=_cp())(*ins)
    return outs if relu2 else outs[0]


def _add_ln(xres, y, g, b, name, tm=256):
    t = xres.shape[0]
    tm = min(tm, t)

    def kern(x_ref, y_ref, g_ref, b_ref, o_ref, ob_ref, xh_ref, r_ref):
        z = ALPHA * x_ref[...] + y_ref[...]
        mu = jnp.mean(z, axis=-1, keepdims=True)
        zc = z - mu
        var = jnp.mean(zc * zc, axis=-1, keepdims=True)
        r = lax.rsqrt(var + LN_EPS)
        xh = zc * r
        o = xh * g_ref[...] + b_ref[...]
        o_ref[...] = o
        ob_ref[...] = o.astype(BF16)
        xh_ref[...] = xh
        r_ref[...] = r

    row = pl.BlockSpec((tm, D), lambda i: (i, 0))
    vec = pl.BlockSpec((1, D), lambda i: (0, 0))
    st = pl.BlockSpec((tm, 1), lambda i: (i, 0))
    return pl.pallas_call(
        kern, grid=(t // tm,), in_specs=[row, row, vec, vec], out_specs=[row, row, row, st],
        out_shape=[jax.ShapeDtypeStruct((t, D), F32), jax.ShapeDtypeStruct((t, D), BF16),
                   jax.ShapeDtypeStruct((t, D), F32), jax.ShapeDtypeStruct((t, 1), F32)],
        name=name, compiler_params=_cp())(xres, y, g.reshape(1, D), b.reshape(1, D))


def _ln_bwd(dout, xhat, rstd, g, name, tm=256):
    t = dout.shape[0]
    tm = min(tm, t)

    def kern(do_ref, xh_ref, r_ref, g_ref, dz_ref, dzb_ref, dg_ref, db_ref):
        @pl.when(pl.program_id(0) == 0)
        def _():
            dg_ref[...] = jnp.zeros_like(dg_ref)
            db_ref[...] = jnp.zeros_like(db_ref)

        do = do_ref[...]
        xh = xh_ref[...]
        dxh = do * g_ref[...]
        m1 = jnp.mean(dxh, axis=-1, keepdims=True)
        m2 = jnp.mean(dxh * xh, axis=-1, keepdims=True)
        dz = r_ref[...] * (dxh - m1 - xh * m2)
        dz_ref[...] = dz
        dzb_ref[...] = dz.astype(BF16)
        dg_ref[...] += jnp.sum(do * xh, axis=0, keepdims=True)
        db_ref[...] += jnp.sum(do, axis=0, keepdims=True)

    row = pl.BlockSpec((tm, D), lambda i: (i, 0))
    vec = pl.BlockSpec((1, D), lambda i: (0, 0))
    st = pl.BlockSpec((tm, 1), lambda i: (i, 0))
    return pl.pallas_call(
        kern, grid=(t // tm,), in_specs=[row, row, st, vec], out_specs=[row, row, vec, vec],
        out_shape=[jax.ShapeDtypeStruct((t, D), F32), jax.ShapeDtypeStruct((t, D), BF16),
                   jax.ShapeDtypeStruct((1, D), F32), jax.ShapeDtypeStruct((1, D), F32)],
        name=name, compiler_params=_cp())(dout, xhat, rstd, g.reshape(1, D))


def _loss_grad(y, target, name="loss_grad", tm=256):
    t = y.shape[0]
    tm = min(tm, t)

    def kern(y_ref, t_ref, d_ref, l_ref):
        @pl.when(pl.program_id(0) == 0)
        def _():
            l_ref[...] = jnp.zeros_like(l_ref)

        e = y_ref[...] - t_ref[...]
        d_ref[...] = e * (1.0 / D)
        l_ref[...] += jnp.sum(e * e, axis=0, keepdims=True)

    row = pl.BlockSpec((tm, D), lambda i: (i, 0))
    vec = pl.BlockSpec((1, D), lambda i: (0, 0))
    return pl.pallas_call(
        kern, grid=(t // tm,), in_specs=[row, row], out_specs=[row, vec],
        out_shape=[jax.ShapeDtypeStruct((t, D), F32), jax.ShapeDtypeStruct((1, D), F32)],
        name=name, compiler_params=_cp())(y, target)


def _rope_tables(t):
    half = ROPE // 2
    inv = 10000.0 ** (-jnp.arange(half, dtype=F32) / half)
    ang = jnp.arange(t).astype(F32)[:, None] * inv[None, :]
    cos, sin = jnp.cos(ang), jnp.sin(ang)
    z = jnp.zeros((t, RP - ROPE), F32)
    return jnp.concatenate([cos, cos, z], axis=1), jnp.concatenate([-sin, sin, z], axis=1)


def _swap_halves(x):
    lane = lax.broadcasted_iota(jnp.int32, x.shape, 1)
    return jnp.where(lane < ROPE // 2, pltpu.roll(x, RP - ROPE // 2, 1), pltpu.roll(x, ROPE // 2, 1))


def _rope(x, cos, sin):
    return x * cos + _swap_halves(x) * sin


def _rope_t(gy, cos, sin):
    return gy * cos + _swap_halves(gy * sin)


def _mla_pre(hh, g_q, g_kv, cos, sin, tm=256):
    t = hh.shape[0]
    tm = min(tm, t)

    def kern(h_ref, gq_ref, gkv_ref, c_ref, s_ref, cq_ref, k_ref):
        xq = h_ref[:, 0:QR]
        rq = lax.rsqrt(jnp.mean(xq * xq, axis=-1, keepdims=True) + RMS_EPS)
        cq_ref[...] = (xq * rq * gq_ref[...]).astype(BF16)
        xk = h_ref[:, QR:QR + KVR]
        rk = lax.rsqrt(jnp.mean(xk * xk, axis=-1, keepdims=True) + RMS_EPS)
        k_ref[:, 0:KVR] = (xk * rk * gkv_ref[...]).astype(BF16)
        k_ref[:, KVR:KD] = _rope(h_ref[:, QR + KVR:HW], c_ref[...], s_ref[...]).astype(BF16)

    return pl.pallas_call(
        kern, grid=(t // tm,),
        in_specs=[pl.BlockSpec((tm, HW), lambda i: (i, 0)), pl.BlockSpec((1, QR), lambda i: (0, 0)),
                  pl.BlockSpec((1, KVR), lambda i: (0, 0)), pl.BlockSpec((tm, RP), lambda i: (i, 0)),
                  pl.BlockSpec((tm, RP), lambda i: (i, 0))],
        out_specs=[pl.BlockSpec((tm, QR), lambda i: (i, 0)), pl.BlockSpec((tm, KD), lambda i: (i, 0))],
        out_shape=[jax.ShapeDtypeStruct((t, QR), BF16), jax.ShapeDtypeStruct((t, KD), BF16)],
        name="mla_pre", compiler_params=_cp())(hh, g_q.reshape(1, QR), g_kv.reshape(1, KVR), cos, sin)


def _mla_pre_bwd(hh, dcq, dk, g_q, g_kv, cos, sin, tm=256):
    t = hh.shape[0]
    tm = min(tm, t)

    def rms_bwd(x, dy, g):
        r = lax.rsqrt(jnp.mean(x * x, axis=-1, keepdims=True) + RMS_EPS)
        gdy = dy * g
        dx = r * gdy - x * (r * r * r) * jnp.mean(gdy * x, axis=-1, keepdims=True)
        return dx, jnp.sum(dy * x * r, axis=0, keepdims=True)

    def kern(h_ref, dcq_ref, dk_ref, gq_ref, gkv_ref, c_ref, s_ref, dh_ref, dgq_ref, dgkv_ref):
        @pl.when(pl.program_id(0) == 0)
        def _():
            dgq_ref[...] = jnp.zeros_like(dgq_ref)
            dgkv_ref[...] = jnp.zeros_like(dgkv_ref)

        dxq, dgq = rms_bwd(h_ref[:, 0:QR], dcq_ref[...], gq_ref[...])
        dxk, dgk = rms_bwd(h_ref[:, QR:QR + KVR], dk_ref[:, 0:KVR], gkv_ref[...])
        dh_ref[:, 0:QR] = dxq.astype(BF16)
        dh_ref[:, QR:QR + KVR] = dxk.astype(BF16)
        dh_ref[:, QR + KVR:HW] = _rope_t(dk_ref[:, KVR:KD], c_ref[...], s_ref[...]).astype(BF16)
        dgq_ref[...] += dgq
        dgkv_ref[...] += dgk

    return pl.pallas_call(
        kern, grid=(t // tm,),
        in_specs=[pl.BlockSpec((tm, HW), lambda i: (i, 0)), pl.BlockSpec((tm, QR), lambda i: (i, 0)),
                  pl.BlockSpec((tm, KD), lambda i: (i, 0)), pl.BlockSpec((1, QR), lambda i: (0, 0)),
                  pl.BlockSpec((1, KVR), lambda i: (0, 0)), pl.BlockSpec((tm, RP), lambda i: (i, 0)),
                  pl.BlockSpec((tm, RP), lambda i: (i, 0))],
        out_specs=[pl.BlockSpec((tm, HW), lambda i: (i, 0)), pl.BlockSpec((1, QR), lambda i: (0, 0)),
                   pl.BlockSpec((1, KVR), lambda i: (0, 0))],
        out_shape=[jax.ShapeDtypeStruct((t, HW), BF16), jax.ShapeDtypeStruct((1, QR), F32),
                   jax.ShapeDtypeStruct((1, KVR), F32)],
        name="mla_pre_bwd", compiler_params=_cp())(hh, dcq, dk, g_q.reshape(1, QR), g_kv.reshape(1, KVR), cos, sin)


def _q_prep(q2, wuk_t, cos, sin, tm=256):
    t = q2.shape[0]
    tm = min(tm, t)

    def kern(q_ref, w_ref, c_ref, s_ref, o_ref):
        cos_, sin_ = c_ref[...], s_ref[...]
        for h in range(H):
            qn = q_ref[:, h * NOPE:(h + 1) * NOPE].astype(BF16)
            o_ref[:, h * KD:h * KD + KVR] = (_dot(qn, w_ref[h], NN) * QSCALE).astype(BF16)
            qr = q_ref[:, H * NOPE + h * RP:H * NOPE + (h + 1) * RP]
            o_ref[:, h * KD + KVR:(h + 1) * KD] = (_rope(qr, cos_, sin_) * QSCALE).astype(BF16)

    return pl.pallas_call(
        kern, grid=(t // tm,),
        in_specs=[pl.BlockSpec((tm, 2 * H * NOPE), lambda i: (i, 0)), pl.BlockSpec((H, NOPE, KVR), lambda i: (0, 0, 0)),
                  pl.BlockSpec((tm, RP), lambda i: (i, 0)), pl.BlockSpec((tm, RP), lambda i: (i, 0))],
        out_specs=pl.BlockSpec((tm, H * KD), lambda i: (i, 0)),
        out_shape=jax.ShapeDtypeStruct((t, H * KD), BF16),
        name="q_prep", compiler_params=_cp())(q2, wuk_t, cos, sin)


def _q_prep_bwd(dq_cat, q2, wuk_h, cos, sin, tm=256):
    t = q2.shape[0]
    tm = min(tm, t)

    def kern(dq_ref, q_ref, w_ref, c_ref, s_ref, o_ref, dw_ref):
        @pl.when(pl.program_id(0) == 0)
        def _():
            dw_ref[...] = jnp.zeros_like(dw_ref)

        cos_, sin_ = c_ref[...], s_ref[...]
        for h in range(H):
            dql = dq_ref[:, h * KD:h * KD + KVR].astype(BF16)
            o_ref[:, h * NOPE:(h + 1) * NOPE] = _dot(dql, w_ref[h], NN).astype(BF16)
            dqr = dq_ref[:, h * KD + KVR:(h + 1) * KD]
            o_ref[:, H * NOPE + h * RP:H * NOPE + (h + 1) * RP] = _rope_t(dqr, cos_, sin_).astype(BF16)
            qn = q_ref[:, h * NOPE:(h + 1) * NOPE].astype(BF16)
            dw_ref[h] += _dot(qn, dql, TN)

    return pl.pallas_call(
        kern, grid=(t // tm,),
        in_specs=[pl.BlockSpec((tm, H * KD), lambda i: (i, 0)), pl.BlockSpec((tm, 2 * H * NOPE), lambda i: (i, 0)),
                  pl.BlockSpec((H, KVR, NOPE), lambda i: (0, 0, 0)),
                  pl.BlockSpec((tm, RP), lambda i: (i, 0)), pl.BlockSpec((tm, RP), lambda i: (i, 0))],
        out_specs=[pl.BlockSpec((tm, 2 * H * NOPE), lambda i: (i, 0)), pl.BlockSpec((H, NOPE, KVR), lambda i: (0, 0, 0))],
        out_shape=[jax.ShapeDtypeStruct((t, 2 * H * NOPE), BF16), jax.ShapeDtypeStruct((H, NOPE, KVR), F32)],
        name="q_prep_bwd", compiler_params=_cp())(dq_cat, q2, wuk_h, cos, sin)


def _o_up(o_lat, wuv_h, tm=256):
    t = o_lat.shape[0]
    tm = min(tm, t)

    def kern(x_ref, w_ref, o_ref):
        for h in range(H):
            xl = x_ref[:, h * KVR:(h + 1) * KVR].astype(BF16)
            o_ref[:, h * NOPE:(h + 1) * NOPE] = _dot(xl, w_ref[h], NN).astype(BF16)

    return pl.pallas_call(
        kern, grid=(t // tm,),
        in_specs=[pl.BlockSpec((tm, H * KVR), lambda i: (i, 0)), pl.BlockSpec((H, KVR, NOPE), lambda i: (0, 0, 0))],
        out_specs=pl.BlockSpec((tm, H * NOPE), lambda i: (i, 0)),
        out_shape=jax.ShapeDtypeStruct((t, H * NOPE), BF16),
        name="o_up", compiler_params=_cp())(o_lat, wuv_h)


def _o_up_bwd(do, o_lat, wuv_h, tm=256):
    t = do.shape[0]
    tm = min(tm, t)

    def kern(do_ref, x_ref, w_ref, dx_ref, dw_ref):
        @pl.when(pl.program_id(0) == 0)
        def _():
            dw_ref[...] = jnp.zeros_like(dw_ref)

        for h in range(H):
            dh_ = do_ref[:, h * NOPE:(h + 1) * NOPE]
            dx_ref[:, h * KVR:(h + 1) * KVR] = _dot(dh_, w_ref[h], NT)
            xl = x_ref[:, h * KVR:(h + 1) * KVR].astype(BF16)
            dw_ref[h] += _dot(xl, dh_, TN)

    return pl.pallas_call(
        kern, grid=(t // tm,),
        in_specs=[pl.BlockSpec((tm, H * NOPE), lambda i: (i, 0)), pl.BlockSpec((tm, H * KVR), lambda i: (i, 0)),
                  pl.BlockSpec((H, KVR, NOPE), lambda i: (0, 0, 0))],
        out_specs=[pl.BlockSpec((tm, H * KVR), lambda i: (i, 0)), pl.BlockSpec((H, KVR, NOPE), lambda i: (0, 0, 0))],
        out_shape=[jax.ShapeDtypeStruct((t, H * KVR), F32), jax.ShapeDtypeStruct((H, KVR, NOPE), F32)],
        name="o_up_bwd", compiler_params=_cp())(do, o_lat, wuv_h)


def _causal_pairs(nq):
    return [(i, j) for i in range(nq) for j in range(i + 1)]


def _lane_tile(stat, width):
    return jnp.tile(stat, (1, width // 128))


def _flash_fwd(qcat, kc, bq, hb):
    t = kc.shape[0]
    nq = t // bq
    pairs = _causal_pairs(nq)
    itab = jnp.asarray(np.array([p[0] for p in pairs], np.int32))
    jtab = jnp.asarray(np.array([p[1] for p in pairs], np.int32))

    def kern(it, jt, q_ref, k_ref, o_ref, lse_ref, lset_ref, m_sc, l_sc, acc_sc):
        st = pl.program_id(1)
        i, j = it[st], jt[st]

        @pl.when(j == 0)
        def _():
            m_sc[...] = jnp.full_like(m_sc, NEG)
            l_sc[...] = jnp.zeros_like(l_sc)
            acc_sc[...] = jnp.zeros_like(acc_sc)

        def update(masked):
            k = k_ref[...]
            v = k[:, 0:KVR]
            if masked:
                row = lax.broadcasted_iota(jnp.int32, (bq, bq), 0)
                col = lax.broadcasted_iota(jnp.int32, (bq, bq), 1)
                keep = col <= row
            for hh in range(hb):
                s = _dot(q_ref[:, hh * KD:(hh + 1) * KD], k, NT)
                if masked:
                    s = jnp.where(keep, s, NEG)
                m_prev = m_sc[hh]
                m_next = jnp.maximum(m_prev, jnp.max(s, axis=1)[:, None])
                p = jnp.exp2(s - _lane_tile(m_next, bq))
                a = jnp.exp2(m_prev - m_next)
                l_sc[hh] = a * l_sc[hh] + jnp.sum(p, axis=1)[:, None]
                acc_sc[hh] = _lane_tile(a, KVR) * acc_sc[hh] + _dot(p.astype(BF16), v, NN)
                m_sc[hh] = m_next

        @pl.when(j < i)
        def _():
            update(False)

        @pl.when(j == i)
        def _():
            update(True)
            for hh in range(hb):
                l = l_sc[hh]
                o_ref[:, hh * KVR:(hh + 1) * KVR] = acc_sc[hh] / _lane_tile(l, KVR)
                lse = m_sc[hh] + jnp.log2(l)
                lse_ref[hh] = lse
                lset_ref[hh] = lse.T[0:1, :]

    gs = pltpu.PrefetchScalarGridSpec(
        num_scalar_prefetch=2, grid=(H // hb, len(pairs)),
        in_specs=[pl.BlockSpec((bq, hb * KD), lambda g, s, it, jt: (it[s], g)),
                  pl.BlockSpec((bq, KD), lambda g, s, it, jt: (jt[s], 0))],
        out_specs=[pl.BlockSpec((bq, hb * KVR), lambda g, s, it, jt: (it[s], g)),
                   pl.BlockSpec((hb, bq, 128), lambda g, s, it, jt: (g, it[s], 0)),
                   pl.BlockSpec((hb, 1, bq), lambda g, s, it, jt: (g, 0, it[s]))],
        scratch_shapes=[pltpu.VMEM((hb, bq, 128), F32), pltpu.VMEM((hb, bq, 128), F32),
                        pltpu.VMEM((hb, bq, KVR), F32)])
    return pl.pallas_call(
        kern, grid_spec=gs,
        out_shape=[jax.ShapeDtypeStruct((t, H * KVR), F32), jax.ShapeDtypeStruct((H, t, 128), F32),
                   jax.ShapeDtypeStruct((H, 1, t), F32)],
        name="mla_flash_fwd", compiler_params=_cp())(itab, jtab, qcat, kc)


def _flash_dq(qcat, kc, do_lat, o_lat, lse, bq, hb):
    t = kc.shape[0]
    nq = t // bq
    pairs = _causal_pairs(nq)
    itab = jnp.asarray(np.array([p[0] for p in pairs], np.int32))
    jtab = jnp.asarray(np.array([p[1] for p in pairs], np.int32))

    def kern(it, jt, q_ref, k_ref, do_ref, o_ref, lse_ref, dq_ref, dlt_ref, acc_sc, dl_sc):
        st = pl.program_id(1)
        i, j = it[st], jt[st]

        @pl.when(j == 0)
        def _():
            acc_sc[...] = jnp.zeros_like(acc_sc)
            for hh in range(hb):
                cs = slice(hh * KVR, (hh + 1) * KVR)
                dl = jnp.broadcast_to(jnp.sum(do_ref[:, cs] * o_ref[:, cs], axis=1)[:, None], (bq, 128))
                dl_sc[hh] = dl
                dlt_ref[hh] = dl.T[0:1, :]

        def update(masked):
            k = k_ref[...]
            v = k[:, 0:KVR]
            if masked:
                row = lax.broadcasted_iota(jnp.int32, (bq, bq), 0)
                col = lax.broadcasted_iota(jnp.int32, (bq, bq), 1)
                keep = col <= row
            for hh in range(hb):
                s = _dot(q_ref[:, hh * KD:(hh + 1) * KD], k, NT)
                if masked:
                    s = jnp.where(keep, s, NEG)
                p = jnp.exp2(s - _lane_tile(lse_ref[hh], bq))
                dp = _dot(do_ref[:, hh * KVR:(hh + 1) * KVR].astype(BF16), v, NT)
                ds = p * (dp - _lane_tile(dl_sc[hh], bq))
                acc_sc[hh] += _dot(ds.astype(BF16), k, NN)

        @pl.when(j < i)
        def _():
            update(False)

        @pl.when(j == i)
        def _():
            update(True)
            for hh in range(hb):
                dq_ref[:, hh * KD:(hh + 1) * KD] = acc_sc[hh] * MLA_SCALE

    gs = pltpu.PrefetchScalarGridSpec(
        num_scalar_prefetch=2, grid=(H // hb, len(pairs)),
        in_specs=[pl.BlockSpec((bq, hb * KD), lambda g, s, it, jt: (it[s], g)),
                  pl.BlockSpec((bq, KD), lambda g, s, it, jt: (jt[s], 0)),
                  pl.BlockSpec((bq, hb * KVR), lambda g, s, it, jt: (it[s], g)),
                  pl.BlockSpec((bq, hb * KVR), lambda g, s, it, jt: (it[s], g)),
                  pl.BlockSpec((hb, bq, 128), lambda g, s, it, jt: (g, it[s], 0))],
        out_specs=[pl.BlockSpec((bq, hb * KD), lambda g, s, it, jt: (it[s], g)),
                   pl.BlockSpec((hb, 1, bq), lambda g, s, it, jt: (g, 0, it[s]))],
        scratch_shapes=[pltpu.VMEM((hb, bq, KD), F32), pltpu.VMEM((hb, bq, 128), F32)])
    return pl.pallas_call(
        kern, grid_spec=gs,
        out_shape=[jax.ShapeDtypeStruct((t, H * KD), F32), jax.ShapeDtypeStruct((H, 1, t), F32)],
        name="mla_flash_dq", compiler_params=_cp())(itab, jtab, qcat, kc, do_lat, o_lat, lse)


def _flash_dkv(qcat, kc, do_lat, lse_t, delta_t, bq, hb):
    t = kc.shape[0]
    nq = t // bq
    ng = H // hb
    steps = [(j, g, i) for j in range(nq) for g in range(ng) for i in range(j, nq)]
    jtab = jnp.asarray(np.array([s[0] for s in steps], np.int32))
    gtab = jnp.asarray(np.array([s[1] for s in steps], np.int32))
    itab = jnp.asarray(np.array([s[2] for s in steps], np.int32))

    def kern(jt, gt, it, q_ref, k_ref, do_ref, lset_ref, dlt_ref, dk_ref, dk_sc, dv_sc):
        st = pl.program_id(0)
        j, g, i = jt[st], gt[st], it[st]

        @pl.when((g == 0) & (i == j))
        def _():
            dk_sc[...] = jnp.zeros_like(dk_sc)
            dv_sc[...] = jnp.zeros_like(dv_sc)

        def update(masked):
            k = k_ref[...]
            v = k[:, 0:KVR]
            if masked:
                row = lax.broadcasted_iota(jnp.int32, (bq, bq), 0)
                col = lax.broadcasted_iota(jnp.int32, (bq, bq), 1)
                keep = row <= col
            for hh in range(hb):
                q = q_ref[:, hh * KD:(hh + 1) * KD]
                dob = do_ref[:, hh * KVR:(hh + 1) * KVR].astype(BF16)
                s = _dot(k, q, NT)
                if masked:
                    s = jnp.where(keep, s, NEG)
                p = jnp.exp2(s - lset_ref[hh])
                dv_sc[...] += _dot(p.astype(BF16), dob, NN)
                dp = _dot(v, dob, NT)
                ds = p * (dp - dlt_ref[hh])
                dk_sc[...] += _dot(ds.astype(BF16), q, NN)

        @pl.when(i > j)
        def _():
            update(False)

        @pl.when(i == j)
        def _():
            update(True)

        @pl.when((g == ng - 1) & (i == nq - 1))
        def _():
            dk_ref[:, 0:KVR] = dk_sc[:, 0:KVR] * LN2 + dv_sc[...]
            dk_ref[:, KVR:KD] = dk_sc[:, KVR:KD] * LN2

    gs = pltpu.PrefetchScalarGridSpec(
        num_scalar_prefetch=3, grid=(len(steps),),
        in_specs=[pl.BlockSpec((bq, hb * KD), lambda s, jt, gt, it: (it[s], gt[s])),
                  pl.BlockSpec((bq, KD), lambda s, jt, gt, it: (jt[s], 0)),
                  pl.BlockSpec((bq, hb * KVR), lambda s, jt, gt, it: (it[s], gt[s])),
                  pl.BlockSpec((hb, 1, bq), lambda s, jt, gt, it: (gt[s], 0, it[s])),
                  pl.BlockSpec((hb, 1, bq), lambda s, jt, gt, it: (gt[s], 0, it[s]))],
        out_specs=pl.BlockSpec((bq, KD), lambda s, jt, gt, it: (jt[s], 0)),
        scratch_shapes=[pltpu.VMEM((bq, KD), F32), pltpu.VMEM((bq, KVR), F32)])
    return pl.pallas_call(
        kern, grid_spec=gs, out_shape=jax.ShapeDtypeStruct((t, KD), F32),
        name="mla_flash_dkv", compiler_params=_cp())(jtab, gtab, itab, qcat, kc, do_lat, lse_t, delta_t)


def _bucket_table():
    d = np.arange(WIN)
    max_exact = NBKT // 2
    nf = np.maximum(d, 1).astype(np.float32)
    large = max_exact + (np.log(nf / np.float32(max_exact)) / np.float32(math.log(WIN / max_exact))
                         * np.float32(NBKT - max_exact)).astype(np.int32)
    large = np.minimum(large, NBKT - 1)
    bucket = np.where(d < max_exact, d, large).astype(np.int32)
    jj = np.arange(2 * WIN)[:, None]
    ii = np.arange(WIN)[None, :]
    dist = ii + WIN - jj
    valid = (dist >= 0) & (dist < WIN)
    return np.where(valid, bucket[np.clip(dist, 0, WIN - 1)], -1).astype(np.int32)


def _bias_build(rel_bias, bkt):
    def kern(bk_ref, rb_ref, o_ref):
        bk = bk_ref[...]
        for hd in range(QH):
            acc = jnp.full((2 * WIN, WIN), NEG, F32)
            for b in range(NBKT):
                acc = jnp.where(bk == b, rb_ref[b, hd], acc)
            o_ref[hd] = acc

    return pl.pallas_call(
        kern, in_specs=[pl.BlockSpec(memory_space=pltpu.VMEM), pl.BlockSpec(memory_space=pltpu.SMEM)],
        out_specs=pl.BlockSpec(memory_space=pltpu.VMEM),
        out_shape=jax.ShapeDtypeStruct((QH, 2 * WIN, WIN), F32), name="swa_bias_build")(bkt, rel_bias)


def _bias_bwd(dbias, bkt):
    def kern(db_ref, bk_ref, o_ref):
        bk = bk_ref[...]
        for hd in range(QH):
            g = db_ref[hd]
            for b in range(NBKT):
                r = b * QH + hd
                o_ref[r:r + 1, :] = jnp.sum(jnp.where(bk == b, g, 0.0), axis=0, keepdims=True)

    return pl.pallas_call(
        kern, in_specs=[pl.BlockSpec(memory_space=pltpu.VMEM), pl.BlockSpec(memory_space=pltpu.VMEM)],
        out_specs=pl.BlockSpec(memory_space=pltpu.VMEM),
        out_shape=jax.ShapeDtypeStruct((NBKT * QH, WIN), F32), name="swa_bias_bwd")(dbias, bkt)


def _swa_scores(k_band, q_t, bias, first):
    s = _dot(k_band, q_t, TN) * SWA_SCALE + bias
    if first is not None:
        row = lax.broadcasted_iota(jnp.int32, s.shape, 0)
        s = jnp.where(jnp.logical_or(jnp.logical_not(first), row >= WIN), s, NEG)
    return s


def _swa_fwd(qkv_t, bias, sinks, qb):
    t = qkv_t.shape[1]
    w = qb * WIN
    nst = t // w

    def kern(q_ref, kc_ref, kp_ref, vc_ref, vp_ref, b_ref, sk_ref, o_ref, lse_ref):
        n = pl.program_id(0)
        kfull = jnp.concatenate([kp_ref[...], kc_ref[...]], axis=1)
        vfull = jnp.concatenate([vp_ref[...], vc_ref[...]], axis=1)
        for b in range(qb):
            cs = slice(b * WIN, (b + 1) * WIN)
            bs = slice(b * WIN, (b + 2) * WIN)
            for kh in range(KVH):
                k_band = kfull[kh * HD:(kh + 1) * HD, bs]
                v_band = vfull[kh * HD:(kh + 1) * HD, bs]
                for g in range(G):
                    hd = kh * G + g
                    rs = slice(hd * HD, (hd + 1) * HD)
                    s = _swa_scores(k_band, q_ref[rs, cs], b_ref[hd], (n == 0) if b == 0 else None)
                    sink = sk_ref[hd]
                    m = jnp.maximum(jnp.max(s, axis=0, keepdims=True), sink)
                    p = jnp.exp(s - m)
                    den = jnp.sum(p, axis=0, keepdims=True) + jnp.exp(sink - m)
                    p = p / den
                    o_ref[rs, cs] = _dot(v_band, p.astype(BF16), NN)
                    lse_ref[hd:hd + 1, cs] = m + jnp.log(den)

    prev = lambda r: (lambda n: (r, jnp.maximum(n * qb - 1, 0)))
    return pl.pallas_call(
        kern, grid=(nst,),
        in_specs=[pl.BlockSpec((QH * HD, w), lambda n: (0, n)),
                  pl.BlockSpec((KVH * HD, w), lambda n: (4, n)), pl.BlockSpec((KVH * HD, WIN), prev(4)),
                  pl.BlockSpec((KVH * HD, w), lambda n: (5, n)), pl.BlockSpec((KVH * HD, WIN), prev(5)),
                  pl.BlockSpec((QH, 2 * WIN, WIN), lambda n: (0, 0, 0)),
                  pl.BlockSpec(memory_space=pltpu.SMEM)],
        out_specs=[pl.BlockSpec((QH * HD, w), lambda n: (0, n)), pl.BlockSpec((QH, w), lambda n: (0, n))],
        out_shape=[jax.ShapeDtypeStruct((QH * HD, t), F32), jax.ShapeDtypeStruct((QH, t), F32)],
        name="swa_fwd", compiler_params=_cp())(qkv_t, qkv_t, qkv_t, qkv_t, qkv_t, bias, sinks)


def _swa_bwd(qkv_t, do_t, o_t, lse, bias, sinks, qb):
    t = qkv_t.shape[1]
    w = qb * WIN
    nst = t // w
    nblk = t // WIN

    def kern(q_ref, kc_ref, kp_ref, vc_ref, vp_ref, do_ref, o_ref, lse_ref, qn_ref, don_ref, on_ref, lsen_ref,
             b_ref, sk_ref, dqkv_ref, db_ref, dsk_ref, acc_sc):
        n = pl.program_id(0)

        @pl.when(n == 0)
        def _():
            db_ref[...] = jnp.zeros_like(db_ref)
            dsk_ref[...] = jnp.zeros_like(dsk_ref)

        acc_sc[...] = jnp.zeros_like(acc_sc)
        kfull = jnp.concatenate([kp_ref[...], kc_ref[...]], axis=1)
        vfull = jnp.concatenate([vp_ref[...], vc_ref[...]], axis=1)
        for b in range(qb):
            cs = slice(b * WIN, (b + 1) * WIN)
            bs = slice(b * WIN, (b + 2) * WIN)
            for kh in range(KVH):
                k_band = kfull[kh * HD:(kh + 1) * HD, bs]
                v_band = vfull[kh * HD:(kh + 1) * HD, bs]
                dk_b = jnp.zeros((HD, 2 * WIN), F32)
                dv_b = jnp.zeros((HD, 2 * WIN), F32)
                for g in range(G):
                    hd = kh * G + g
                    rs = slice(hd * HD, (hd + 1) * HD)
                    q_t = q_ref[rs, cs]
                    do = do_ref[rs, cs]
                    lse_h = lse_ref[hd:hd + 1, cs]
                    s = _swa_scores(k_band, q_t, b_ref[hd], (n == 0) if b == 0 else None)
                    p = jnp.exp(s - lse_h)
                    dob = do.astype(BF16)
                    dp = _dot(v_band, dob, TN)
                    dl = jnp.sum(do * o_ref[rs, cs], axis=0, keepdims=True)
                    ds = p * (dp - dl)
                    db_ref[hd] += ds
                    dsk_ref[hd:hd + 1, :] += -jnp.exp(sk_ref[hd] - lse_h) * dl
                    dss = (ds * SWA_SCALE).astype(BF16)
                    dqkv_ref[rs, cs] = _dot(k_band, dss, NN).astype(BF16)
                    dk_b += _dot(q_t, dss, NT)
                    dv_b += _dot(dob, p.astype(BF16), NT)
                acc_sc[kh * HD:(kh + 1) * HD, bs] += dk_b
                acc_sc[KVH * HD + kh * HD:KVH * HD + (kh + 1) * HD, bs] += dv_b

        @pl.when(n < nst - 1)
        def _():
            ls = slice((qb - 1) * WIN, qb * WIN)
            ts = slice(qb * WIN, (qb + 1) * WIN)
            for kh in range(KVH):
                k_last = kc_ref[kh * HD:(kh + 1) * HD, ls]
                v_last = vc_ref[kh * HD:(kh + 1) * HD, ls]
                dk_b = jnp.zeros((HD, WIN), F32)
                dv_b = jnp.zeros((HD, WIN), F32)
                for g in range(G):
                    hd = kh * G + g
                    rs = slice(hd * HD, (hd + 1) * HD)
                    q_t = qn_ref[rs, :]
                    do = don_ref[rs, :]
                    s = _dot(k_last, q_t, TN) * SWA_SCALE + b_ref[hd, 0:WIN, :]
                    p = jnp.exp(s - lsen_ref[hd:hd + 1, :])
                    dob = do.astype(BF16)
                    dp = _dot(v_last, dob, TN)
                    dl = jnp.sum(do * on_ref[rs, :], axis=0, keepdims=True)
                    dss = (p * (dp - dl) * SWA_SCALE).astype(BF16)
                    dk_b += _dot(q_t, dss, NT)
                    dv_b += _dot(dob, p.astype(BF16), NT)
                acc_sc[kh * HD:(kh + 1) * HD, ts] += dk_b
                acc_sc[KVH * HD + kh * HD:KVH * HD + (kh + 1) * HD, ts] += dv_b

        dqkv_ref[QH * HD:QH * HD + 2 * KVH * HD, :] = acc_sc[:, WIN:].astype(BF16)

    prev = lambda r: (lambda n: (r, jnp.maximum(n * qb - 1, 0)))
    nxt = lambda n: (0, jnp.minimum((n + 1) * qb, nblk - 1))
    big = lambda: pl.BlockSpec((QH * HD, w), lambda n: (0, n))
    return pl.pallas_call(
        kern, grid=(nst,),
        in_specs=[big(),
                  pl.BlockSpec((KVH * HD, w), lambda n: (4, n)), pl.BlockSpec((KVH * HD, WIN), prev(4)),
                  pl.BlockSpec((KVH * HD, w), lambda n: (5, n)), pl.BlockSpec((KVH * HD, WIN), prev(5)),
                  big(), big(), pl.BlockSpec((QH, w), lambda n: (0, n)),
                  pl.BlockSpec((QH * HD, WIN), nxt), pl.BlockSpec((QH * HD, WIN), nxt),
                  pl.BlockSpec((QH * HD, WIN), nxt), pl.BlockSpec((QH, WIN), nxt),
                  pl.BlockSpec((QH, 2 * WIN, WIN), lambda n: (0, 0, 0)),
                  pl.BlockSpec(memory_space=pltpu.SMEM)],
        out_specs=[pl.BlockSpec(((QH + 2 * KVH) * HD, w), lambda n: (0, n)),
                   pl.BlockSpec((QH, 2 * WIN, WIN), lambda n: (0, 0, 0)),
                   pl.BlockSpec((QH, WIN), lambda n: (0, 0))],
        out_shape=[jax.ShapeDtypeStruct(((QH + 2 * KVH) * HD, t), BF16),
                   jax.ShapeDtypeStruct((QH, 2 * WIN, WIN), F32), jax.ShapeDtypeStruct((QH, WIN), F32)],
        scratch_shapes=[pltpu.VMEM((2 * KVH * HD, w + WIN), F32)],
        name="swa_bwd", compiler_params=_cp())(
            qkv_t, qkv_t, qkv_t, qkv_t, qkv_t, do_t, o_t, lse, qkv_t, do_t, o_t, lse, bias, sinks)


def _adamw(w, g, m, v, name, tm=544):
    r = w.shape[0]
    tm = r if r % tm else tm
    c1 = 1.0 / (1.0 - B1 ** STEP)
    c2 = 1.0 / (1.0 - B2 ** STEP)

    def kern(w_ref, g_ref, m_ref, v_ref, d_ref, nm_ref, nv_ref):
        g_ = g_ref[...]
        nm = B1 * m_ref[...] + (1.0 - B1) * g_
        nv = B2 * v_ref[...] + (1.0 - B2) * (g_ * g_)
        d_ref[...] = -LR * ((nm * c1) / (jnp.sqrt(nv * c2) + ADAM_EPS) + WD * w_ref[...])
        nm_ref[...] = nm
        nv_ref[...] = nv

    row = pl.BlockSpec((tm, D), lambda i: (i, 0))
    sds = jax.ShapeDtypeStruct((r, D), F32)
    return pl.pallas_call(kern, grid=(r // tm,), in_specs=[row] * 4, out_specs=[row] * 3, out_shape=[sds] * 3,
                          name=name, compiler_params=_cp())(w, g, m, v)


def _mesh_pos():
    return lax.axis_index("x"), lax.axis_index("y"), lax.axis_index("c")


ANY = pl.BlockSpec(memory_space=pl.ANY)


def _allgather_weights(wpack):
    r = wpack.shape[0]
    half = r // 2

    def body(w_ref, out_ref, send_sems, recv_sems):
        x, y, c = _mesh_pos()
        sibling = (x, y, 1 - c)
        chips = [(1 - x, y), (x, 1 - y), (1 - x, 1 - y)]

        def rows(px, py, pc):
            return out_ref.at[2 * px + py, pl.ds(pc * half, half), :]

        def copy(k, block, to, src=None):
            return pltpu.make_async_remote_copy(
                src_ref=rows(*block) if src is None else src, dst_ref=rows(*block),
                send_sem=send_sems.at[k], recv_sem=recv_sems.at[k], device_id=to, device_id_type=MESH)

        first = [copy(j, (x, y, c), (*chip, c), src=w_ref.at[pl.ds(c * half, half), :]) for j, chip in enumerate(chips)]
        for cp in first:
            cp.start()
        passed = [copy(3 + j, (*chip, c), sibling) for j, chip in enumerate(chips)]
        for j, chip in enumerate(chips):
            copy(j, (*chip, c), (x, y, c)).wait_recv()
            passed[j].start()
        for j, chip in enumerate(chips):
            copy(3 + j, (*chip, 1 - c), (x, y, c)).wait_recv()
        for cp in first + passed:
            cp.wait_send()

    return pl.pallas_call(
        body, out_shape=jax.ShapeDtypeStruct((4, r, D), wpack.dtype), in_specs=[ANY], out_specs=ANY,
        scratch_shapes=[pltpu.SemaphoreType.DMA((6,)), pltpu.SemaphoreType.DMA((6,))],
        name="allgather_weights")(wpack)


def _exchange_core_halves(g):
    half = g.shape[1] // 2

    def body(g_ref, out_ref, send_sem, recv_sem):
        x, y, c = _mesh_pos()
        cp = pltpu.make_async_remote_copy(
            src_ref=g_ref.at[:, pl.ds((1 - c) * half, half), :], dst_ref=out_ref,
            send_sem=send_sem, recv_sem=recv_sem, device_id=(x, y, 1 - c), device_id_type=MESH)
        cp.start()
        cp.wait()

    return pl.pallas_call(
        body, out_shape=jax.ShapeDtypeStruct((4, half, D), g.dtype), in_specs=[ANY], out_specs=ANY,
        scratch_shapes=[pltpu.SemaphoreType.DMA, pltpu.SemaphoreType.DMA], name="rs_exchange_cores")(g)


def _add_core_halves(g, other, cidx, tm=544):
    half = other.shape[1]
    nb = half // tm

    def kern(c_ref, a_ref, b_ref, o_ref):
        o_ref[...] = (a_ref[...] + b_ref[...]).astype(BF16)

    gs = pltpu.PrefetchScalarGridSpec(
        num_scalar_prefetch=1, grid=(4, nb),
        in_specs=[pl.BlockSpec((1, tm, D), lambda s, i, c: (s, c[0] * nb + i, 0)),
                  pl.BlockSpec((1, tm, D), lambda s, i, c: (s, i, 0))],
        out_specs=pl.BlockSpec((1, tm, D), lambda s, i, c: (s, i, 0)))
    return pl.pallas_call(kern, grid_spec=gs, out_shape=jax.ShapeDtypeStruct(other.shape, BF16),
                          name="rs_add_cores", compiler_params=_cp())(cidx, g, other)


def _exchange_chip_shards(p):
    def body(p_ref, out_ref, send_sems, recv_sems):
        x, y, c = _mesh_pos()
        me = 2 * x + y
        chips = [(1 - x, y), (x, 1 - y), (1 - x, 1 - y)]
        sends = []
        for j, (px, py) in enumerate(chips):
            cp = pltpu.make_async_remote_copy(
                src_ref=p_ref.at[2 * px + py], dst_ref=out_ref.at[me],
                send_sem=send_sems.at[j], recv_sem=recv_sems.at[j], device_id=(px, py, c), device_id_type=MESH)
            cp.start()
            sends.append(cp)
        for j, (px, py) in enumerate(chips):
            pltpu.make_async_remote_copy(
                src_ref=p_ref.at[me], dst_ref=out_ref.at[2 * px + py],
                send_sem=send_sems.at[j], recv_sem=recv_sems.at[j], device_id=(px, py, c),
                device_id_type=MESH).wait_recv()
        for cp in sends:
            cp.wait_send()

    return pl.pallas_call(
        body, out_shape=jax.ShapeDtypeStruct(p.shape, p.dtype), in_specs=[ANY], out_specs=ANY,
        scratch_shapes=[pltpu.SemaphoreType.DMA((3,)), pltpu.SemaphoreType.DMA((3,))],
        name="rs_exchange_chips")(p)


def _sum_slots(slots, p, pos, tm=544):
    half = slots.shape[1]
    nb = half // tm

    def kern(pos_ref, p_ref, s1_ref, s2_ref, s3_ref, o_ref):
        o_ref[...] = ((p_ref[0].astype(F32) + s1_ref[0].astype(F32)) + s2_ref[0].astype(F32)) + s3_ref[0].astype(F32)

    def slot(k):
        return pl.BlockSpec((1, tm, D), lambda i, pos: ((pos[0] + k) % 4, i, 0))

    gs = pltpu.PrefetchScalarGridSpec(
        num_scalar_prefetch=1, grid=(nb,), in_specs=[slot(0), slot(1), slot(2), slot(3)],
        out_specs=pl.BlockSpec((tm, D), lambda i, pos: (pos[1] * nb + i, 0)))
    return pl.pallas_call(kern, grid_spec=gs, out_shape=jax.ShapeDtypeStruct((2 * half, D), F32),
                          name="rs_sum_chips", compiler_params=_cp())(pos, p, slots, slots, slots)


def _join_core_halves(r):
    half = r.shape[0] // 2

    def body(r_ref, out_ref, send_sem, recv_sem):
        x, y, c = _mesh_pos()
        mine = out_ref.at[pl.ds(c * half, half), :]
        cp = pltpu.make_async_remote_copy(
            src_ref=mine, dst_ref=mine, send_sem=send_sem, recv_sem=recv_sem,
            device_id=(x, y, 1 - c), device_id_type=MESH)
        cp.start()
        theirs = out_ref.at[pl.ds((1 - c) * half, half), :]
        pltpu.make_async_remote_copy(
            src_ref=theirs, dst_ref=theirs, send_sem=send_sem, recv_sem=recv_sem,
            device_id=(x, y, 1 - c), device_id_type=MESH).wait_recv()
        cp.wait_send()

    return pl.pallas_call(
        body, out_shape=jax.ShapeDtypeStruct(r.shape, r.dtype), in_specs=[ANY], out_specs=ANY,
        input_output_aliases={0: 0},
        scratch_shapes=[pltpu.SemaphoreType.DMA, pltpu.SemaphoreType.DMA],
        name="rs_join_cores")(r)


def _allreduce_small(v, name):
    def body(v_ref, out_ref, gat, send_sems, recv_sems):
        x, y, c = _mesh_pos()
        me = 4 * x + 2 * y + c
        gat[me] = v_ref[...]
        sends = []
        for k in range(1, 8):
            peer = (x ^ (k >> 2), y ^ ((k >> 1) & 1), c ^ (k & 1))
            cp = pltpu.make_async_remote_copy(
                src_ref=v_ref, dst_ref=gat.at[me], send_sem=send_sems.at[k - 1], recv_sem=recv_sems.at[k - 1],
                device_id=peer, device_id_type=MESH)
            cp.start()
            sends.append(cp)
        for k in range(1, 8):
            px, py, pc = x ^ (k >> 2), y ^ ((k >> 1) & 1), c ^ (k & 1)
            pltpu.make_async_remote_copy(
                src_ref=v_ref, dst_ref=gat.at[4 * px + 2 * py + pc], send_sem=send_sems.at[k - 1],
                recv_sem=recv_sems.at[k - 1], device_id=(px, py, pc), device_id_type=MESH).wait_recv()
        for cp in sends:
            cp.wait_send()
        acc = gat[0]
        for d in range(1, 8):
            acc = acc + gat[d]
        out_ref[...] = acc

    return pl.pallas_call(
        body, out_shape=jax.ShapeDtypeStruct(v.shape, F32),
        in_specs=[pl.BlockSpec(memory_space=pltpu.VMEM)], out_specs=pl.BlockSpec(memory_space=pltpu.VMEM),
        scratch_shapes=[pltpu.VMEM((8,) + v.shape, F32), pltpu.SemaphoreType.DMA((7,)), pltpu.SemaphoreType.DMA((7,))],
        name=name)(v)


def _mlp_fwd(xb, w_up, w_down, tag):
    u, a = _mm(xb, w_up, "nn", f"mlp_up_{tag}", relu2=True)
    return u, a, _mm(a, w_down, "nn", f"mlp_down_{tag}")


def _mlp_bwd(dz, dzb, xb, u, a, w_up, w_down, tag):
    du = _mm(dzb, w_down, "nt", f"mlp_down_dx_{tag}", out_dtype=BF16, gate_u=u)
    dw_down = _mm(a, dzb, "tn", f"mlp_down_dw_{tag}")
    dw_up = _mm(xb, du, "tn", f"mlp_up_dw_{tag}")
    dx = _mm(du, w_up, "nt", f"mlp_up_dx_{tag}", addend=dz, add_scale=ALPHA)
    return dx, dw_up, dw_down


def _fwd_bwd(x, target, w, bq=512, qb=4, hb=4):
    t = x.shape[0]
    bq = min(bq, t)
    qb = min(qb, t // WIN)
    cos, sin = _rope_tables(t)
    bkt = jnp.asarray(_bucket_table())
    w_in = jnp.pad(w["mla_w_in"], ((0, 0), (0, HW - w["mla_w_in"].shape[1])))
    wuq = w["mla_w_uq"]
    wq2 = jnp.concatenate([wuq[:, :, :NOPE].reshape(QR, H * NOPE),
                           jnp.pad(wuq[:, :, NOPE:], ((0, 0), (0, 0), (0, RP - ROPE))).reshape(QR, H * RP)], axis=1)
    wuk_t = w["mla_w_uk"].transpose(1, 2, 0)
    wuk_h = w["mla_w_uk"].transpose(1, 0, 2)
    wuv_h = w["mla_w_uv"].transpose(1, 0, 2)
    w_o = w["mla_w_o"]
    wqkv = jnp.concatenate([w["swa_w_q"], w["kv_w_shared"]], axis=1)
    wqkv_t = wqkv.T
    wo_s = w["swa_w_o"]
    sinks = w["swa_sinks"].reshape(QH)
    lnp = lambda n, l: w[n][l]

    hh = _mm(x, w_in, "nn", "mla_in")
    cq, kc = _mla_pre(hh, w["mla_g_q"], w["mla_g_kv"], cos, sin)
    q2 = _mm(cq, wq2, "nn", "mla_uq")
    qcat = _q_prep(q2, wuk_t, cos, sin)
    o_lat, lse0, lse0_t = _flash_fwd(qcat, kc, bq, hb)
    o0 = _o_up(o_lat, wuv_h)
    y0 = _mm(o0, w_o, "nn", "mla_out")
    x1, x1b, xh1, r1 = _add_ln(x, y0, lnp("ln_mix_g", 0), lnp("ln_mix_b", 0), "ln_mix_0")
    u0, a0, f0 = _mlp_fwd(x1b, w["mlp_w_up"][0], w["mlp_w_down"][0], 0)
    x2, x2b, xh2, r2 = _add_ln(x1, f0, lnp("ln_mlp_g", 0), lnp("ln_mlp_b", 0), "ln_mlp_0")
    bias = _bias_build(w["rel_bias"], bkt)
    qkv_t = _mm(x2b, wqkv, "nn", "swa_qkv", out_dtype=BF16, out_t=True)
    os_t, lse1 = _swa_fwd(qkv_t, bias, sinks, qb)
    y1 = _mm(os_t, wo_s, "tn", "swa_out")
    x3, x3b, xh3, r3 = _add_ln(x2, y1, lnp("ln_mix_g", 1), lnp("ln_mix_b", 1), "ln_mix_1")
    u1, a1, f1 = _mlp_fwd(x3b, w["mlp_w_up"][1], w["mlp_w_down"][1], 1)
    x4, _, xh4, r4 = _add_ln(x3, f1, lnp("ln_mlp_g", 1), lnp("ln_mlp_b", 1), "ln_mlp_1")
    dx4, lpart = _loss_grad(x4, target)

    g = {}
    dz4, dz4b, dg_mlp1, db_mlp1 = _ln_bwd(dx4, xh4, r4, lnp("ln_mlp_g", 1), "ln_mlp_1_bwd")
    dx3, dwu1, dwd1 = _mlp_bwd(dz4, dz4b, x3b, u1, a1, w["mlp_w_up"][1], w["mlp_w_down"][1], 1)
    dz3, dz3b, dg_mix1, db_mix1 = _ln_bwd(dx3, xh3, r3, lnp("ln_mix_g", 1), "ln_mix_1_bwd")
    dos_t = _mm(dz3b, wo_s, "nt", "swa_out_dx", out_t=True)
    g["swa_w_o"] = _mm(os_t, dz3b, "nn", "swa_out_dw")
    dqkv_t, dbias, dsk = _swa_bwd(qkv_t, dos_t, os_t, lse1, bias, sinks, qb)
    dwqkv = _mm(dqkv_t, x2b, "nn", "swa_qkv_dw").T
    g["swa_w_q"], g["kv_w_shared"] = dwqkv[:, :QH * HD], dwqkv[:, QH * HD:]
    dx2 = _mm(dqkv_t, wqkv_t, "tn", "swa_qkv_dx", addend=dz3, add_scale=ALPHA)
    g["rel_bias"] = jnp.sum(_bias_bwd(dbias, bkt), axis=-1).reshape(NBKT, QH)
    g["swa_sinks"] = jnp.sum(dsk, axis=-1).reshape(1, QH)
    dz2, dz2b, dg_mlp0, db_mlp0 = _ln_bwd(dx2, xh2, r2, lnp("ln_mlp_g", 0), "ln_mlp_0_bwd")
    dx1, dwu0, dwd0 = _mlp_bwd(dz2, dz2b, x1b, u0, a0, w["mlp_w_up"][0], w["mlp_w_down"][0], 0)
    dz1, dz1b, dg_mix0, db_mix0 = _ln_bwd(dx1, xh1, r1, lnp("ln_mix_g", 0), "ln_mix_0_bwd")
    do0 = _mm(dz1b, w_o, "nt", "mla_out_dx", out_dtype=BF16)
    g["mla_w_o"] = _mm(o0, dz1b, "tn", "mla_out_dw")
    do_lat, dwuv = _o_up_bwd(do0, o_lat, wuv_h)
    g["mla_w_uv"] = dwuv.transpose(1, 0, 2)
    dq_cat, delta_t = _flash_dq(qcat, kc, do_lat, o_lat, lse0, bq, hb)
    dk = _flash_dkv(qcat, kc, do_lat, lse0_t, delta_t, bq, hb)
    dq2, dwuk = _q_prep_bwd(dq_cat, q2, wuk_h, cos, sin)
    g["mla_w_uk"] = dwuk.transpose(2, 0, 1)
    dcq = _mm(dq2, wq2, "nt", "mla_uq_dx")
    dwq2 = _mm(cq, dq2, "tn", "mla_uq_dw")
    g["mla_w_uq"] = jnp.concatenate([dwq2[:, :H * NOPE].reshape(QR, H, NOPE),
                                     dwq2[:, H * NOPE:].reshape(QR, H, RP)[:, :, :ROPE]], axis=2)
    dh, dgq, dgkv = _mla_pre_bwd(hh, dcq, dk, w["mla_g_q"], w["mla_g_kv"], cos, sin)
    g["mla_w_in"] = _mm(x, dh, "tn", "mla_in_dw")[:, :QR + KVR + ROPE]
    grad_x = _mm(dh, w_in, "nt", "mla_in_dx", addend=dz1, add_scale=ALPHA)
    g["mla_g_q"], g["mla_g_kv"] = dgq, dgkv
    g["mlp_w_up"] = jnp.stack([dwu0, dwu1])
    g["mlp_w_down"] = jnp.stack([dwd0, dwd1])
    g["ln_mix_g"] = jnp.concatenate([dg_mix0, dg_mix1], axis=0)
    g["ln_mix_b"] = jnp.concatenate([db_mix0, db_mix1], axis=0)
    g["ln_mlp_g"] = jnp.concatenate([dg_mlp0, dg_mlp1], axis=0)
    g["ln_mlp_b"] = jnp.concatenate([db_mlp0, db_mlp1], axis=0)
    return lpart, grad_x, g


def _rows(a):
    return a.reshape(-1, D)


def _pack_shards(parts):
    return jnp.concatenate([_rows(parts[n]) for n, _ in PACK], axis=0)


def _unpack_shards(buf, like):
    out, off = {}, 0
    for n, r in PACK:
        out[n] = buf[off:off + r].reshape(like[n].shape)
        off += r
    return out


def _full_from_gathered(wall, shard_shapes):
    out, off = {}, 0
    for n, r in PACK:
        sl = wall[:, off:off + r]
        off += r
        shp = shard_shapes[n]
        if n == "mlp_w_up":
            out[n] = sl.reshape(4, 2, D, D).transpose(1, 2, 0, 3).reshape(2, D, DFF)
        elif n == "mlp_w_down":
            out[n] = sl.reshape(4, 2, D, D).transpose(1, 0, 2, 3).reshape(2, DFF, D)
        elif n == "kv_w_shared":
            out[n] = sl.reshape((4 * shp[0],) + tuple(shp[1:]))
        else:
            out[n] = sl.reshape((4 * shp[1],) + tuple(shp[2:]))
    return out


def _shards_from_full(grads):
    cols = []
    for n, r in PACK:
        gfull = grads[n]
        if n == "mlp_w_up":
            cols.append(gfull.reshape(2, D, 4, D).transpose(2, 0, 1, 3).reshape(4, r, D))
        elif n == "mlp_w_down":
            cols.append(gfull.reshape(2, 4, D, D).transpose(1, 0, 2, 3).reshape(4, r, D))
        else:
            cols.append(gfull.reshape(4, r, D))
    return jnp.concatenate(cols, axis=1)


SMALL = (("ln_mix_g", 0, 2), ("ln_mix_b", 2, 2), ("ln_mlp_g", 4, 2), ("ln_mlp_b", 6, 2),
         ("swa_sinks", 8, 1), ("mla_g_q", 9, 1), ("mla_g_kv", 10, 1), ("rel_bias", 11, 1))


def _pack_small(parts):
    rows = []
    for n, _, nr in SMALL:
        a = parts[n].reshape(nr, -1).astype(F32)
        rows.append(jnp.pad(a, ((0, 0), (0, D - a.shape[1]))))
    rows.append(jnp.zeros((SMALL_ROWS - 12, D), F32))
    return jnp.concatenate(rows, axis=0)


def _unpack_small(buf, like):
    out = {}
    for n, r0, nr in SMALL:
        size = like[n].size // nr
        out[n] = buf[r0:r0 + nr, :size].reshape(like[n].shape)
    return out


def kernel(x, mla_w_in, mla_g_q, mla_g_kv, mla_w_uq, mla_w_uk, mla_w_uv, mla_w_o, kv_w_shared, swa_w_q, swa_sinks, swa_w_o, rel_bias, mlp_w_up, mlp_w_down, ln_mix_g, ln_mix_b, ln_mlp_g, ln_mlp_b, loss_target, m_mla_w_in, m_mla_g_q, m_mla_g_kv, m_mla_w_uq, m_mla_w_uk, m_mla_w_uv, m_mla_w_o, m_kv_w_shared, m_swa_w_q, m_swa_sinks, m_swa_w_o, m_rel_bias, m_mlp_w_up, m_mlp_w_down, m_ln_mix_g, m_ln_mix_b, m_ln_mlp_g, m_ln_mlp_b, v_mla_w_in, v_mla_g_q, v_mla_g_kv, v_mla_w_uq, v_mla_w_uk, v_mla_w_uv, v_mla_w_o, v_kv_w_shared, v_swa_w_q, v_swa_sinks, v_swa_w_o, v_rel_bias, v_mlp_w_up, v_mlp_w_down, v_ln_mix_g, v_ln_mix_b, v_ln_mlp_g, v_ln_mlp_b):
    names = ["mla_w_in", "mla_g_q", "mla_g_kv", "mla_w_uq", "mla_w_uk", "mla_w_uv", "mla_w_o", "kv_w_shared",
             "swa_w_q", "swa_sinks", "swa_w_o", "rel_bias", "mlp_w_up", "mlp_w_down",
             "ln_mix_g", "ln_mix_b", "ln_mlp_g", "ln_mlp_b"]
    ws = dict(zip(names, [mla_w_in, mla_g_q, mla_g_kv, mla_w_uq, mla_w_uk, mla_w_uv, mla_w_o, kv_w_shared,
                          swa_w_q, swa_sinks, swa_w_o, rel_bias, mlp_w_up, mlp_w_down,
                          ln_mix_g, ln_mix_b, ln_mlp_g, ln_mlp_b]))
    ms = dict(zip(names, [m_mla_w_in, m_mla_g_q, m_mla_g_kv, m_mla_w_uq, m_mla_w_uk, m_mla_w_uv, m_mla_w_o,
                          m_kv_w_shared, m_swa_w_q, m_swa_sinks, m_swa_w_o, m_rel_bias, m_mlp_w_up, m_mlp_w_down,
                          m_ln_mix_g, m_ln_mix_b, m_ln_mlp_g, m_ln_mlp_b]))
    vs = dict(zip(names, [v_mla_w_in, v_mla_g_q, v_mla_g_kv, v_mla_w_uq, v_mla_w_uk, v_mla_w_uv, v_mla_w_o,
                          v_kv_w_shared, v_swa_w_q, v_swa_sinks, v_swa_w_o, v_rel_bias, v_mlp_w_up, v_mlp_w_down,
                          v_ln_mix_g, v_ln_mix_b, v_ln_mlp_g, v_ln_mlp_b]))
    xi, yi, ci = _mesh_pos()
    shard = 2 * xi + yi
    big = [n for n, _ in PACK]
    shard_shapes = {n: ws[n].shape for n in big}

    wpack = _pack_shards({n: ws[n].astype(BF16) for n in big})
    wall = lax.dynamic_update_slice(_allgather_weights(wpack), wpack[None], (shard, 0, 0))
    w = _full_from_gathered(wall, shard_shapes)
    gq_slot = lax.dynamic_update_slice(jnp.zeros((1, QR), F32), mla_g_q, (0, shard * (QR // 4)))
    gkv_slot = lax.dynamic_update_slice(jnp.zeros((1, KVR), F32), mla_g_kv, (0, shard * (KVR // 4)))
    gains = jnp.concatenate([jnp.pad(gq_slot, ((0, 0), (0, D - QR))), jnp.pad(gkv_slot, ((0, 0), (0, D - KVR))),
                             jnp.zeros((SMALL_ROWS - 2, D), F32)], axis=0)
    gains = _allreduce_small(gains * 0.5, "allgather_gains")
    w["mla_g_q"], w["mla_g_kv"] = gains[0, :QR], gains[1, :KVR]
    for n in ("swa_sinks", "rel_bias", "ln_mix_g", "ln_mix_b", "ln_mlp_g", "ln_mlp_b"):
        w[n] = ws[n]

    lpart, grad_x, g = _fwd_bwd(x[0], loss_target[0], w)
    loss = lax.psum(0.5 * jnp.sum(lpart) / D, ("x", "y", "c"))

    gsh = _shards_from_full(g)
    other = _exchange_core_halves(gsh)
    cidx = jnp.reshape(ci, (1,)).astype(jnp.int32)
    chip_part = _add_core_halves(gsh, other, cidx)
    slots = _exchange_chip_shards(chip_part)
    pos = jnp.stack([shard, ci]).astype(jnp.int32)
    gred = _join_core_halves(_sum_slots(slots, chip_part, pos))
    gbig = _unpack_shards(gred, ws)

    small_like = {n: g[n] for n, _, _ in SMALL}
    gsm = _unpack_small(_allreduce_small(_pack_small(g), "allreduce_small_grads"), small_like)
    gsm["mla_g_q"] = lax.dynamic_slice(gsm["mla_g_q"], (0, shard * (QR // 4)), (1, QR // 4))
    gsm["mla_g_kv"] = lax.dynamic_slice(gsm["mla_g_kv"], (0, shard * (KVR // 4)), (1, KVR // 4))
    grads = {**gbig, **gsm}

    dbig, mbig, vbig = _adamw(_pack_shards({n: ws[n] for n in big}), gred,
                              _pack_shards({n: ms[n] for n in big}), _pack_shards({n: vs[n] for n in big}),
                              "adamw_big")
    dsm, msm, vsm = _adamw(_pack_small(ws), _pack_small(gsm), _pack_small(ms), _pack_small(vs), "adamw_small", tm=16)
    delta = {**_unpack_shards(dbig, ws), **_unpack_small(dsm, ws)}
    new_m = {**_unpack_shards(mbig, ws), **_unpack_small(msm, ws)}
    new_v = {**_unpack_shards(vbig, ws), **_unpack_small(vsm, ws)}
    grads = {n: grads[n].reshape(ws[n].shape) for n in names}
    return (loss, grad_x[None], *[grads[n] for n in names], *[delta[n] for n in names],
            *[new_m[n] for n in names], *[new_v[n] for n in names])
```

```python
import functools
import math

import numpy as np
import jax
import jax.numpy as jnp
from jax import lax
from jax.experimental import pallas as pl
from jax.experimental.pallas import tpu as pltpu

F32 = jnp.float32
BF16 = jnp.bfloat16
MESH = pl.DeviceIdType.MESH

D = 1024
DFF = 4096
H = 8
NOPE = 128
ROPE = 64
QR = 384
KVR = 256
RP = 128
KD = KVR + RP
HW = 768
QH = 16
KVH = 4
HD = 64
G = QH // KVH
WIN = 128
NBKT = 32
ALPHA = 4.0 ** 0.25
LN_EPS = 1e-5
RMS_EPS = 1e-6
MLA_SCALE = (NOPE + ROPE) ** -0.5
LOG2E = 1.4426950408889634
LN2 = 0.6931471805599453
QSCALE = MLA_SCALE * LOG2E
SWA_SCALE = HD ** -0.5
NEG = -1e30
LR, B1, B2, ADAM_EPS, WD, STEP = 0.001, 0.9, 0.999, 1e-8, 0.01, 10

VMEM_LIMIT = 48 * 1024 * 1024

NN = (((1,), (0,)), ((), ()))
NT = (((1,), (1,)), ((), ()))
TN = (((0,), (0,)), ((), ()))

PACK = (("mlp_w_up", 2048), ("mlp_w_down", 2048), ("mla_w_o", 256), ("swa_w_q", 256), ("swa_w_o", 256),
        ("kv_w_shared", 128), ("mla_w_in", 176), ("mla_w_uq", 144), ("mla_w_uk", 64), ("mla_w_uv", 64))
PACK_ROWS = sum(r for _, r in PACK)
HALF_ROWS = PACK_ROWS // 2
SMALL_ROWS = 16


def _cp(**kw):
    return pltpu.CompilerParams(vmem_limit_bytes=VMEM_LIMIT, **kw)


def _tile(n, pref):
    t = min(n, pref)
    while n % t:
        t -= 128
    return t


def _dot(a, b, dims):
    return lax.dot_general(a, b, dims, preferred_element_type=F32)


def _mm(a, b, mode, name, out_dtype=F32, out_t=False, addend=None, add_scale=1.0, relu2=False, gate_u=None,
        tm=1024, tn=1024, tk=1024):
    if mode == "nn":
        (m, k), (k2, n) = a.shape, b.shape
    elif mode == "nt":
        (m, k), (n, k2) = a.shape, b.shape
    else:
        (k, m), (k2, n) = a.shape, b.shape
    assert k == k2, (name, a.shape, b.shape)
    tm, tn, tk = _tile(m, tm), _tile(n, tn), _tile(k, tk)
    nk = k // tk
    dims = {"nn": NN, "nt": NT, "tn": TN}[mode]
    if mode == "tn":
        a_spec = pl.BlockSpec((tk, tm), lambda i, j, kk: (kk, i))
    else:
        a_spec = pl.BlockSpec((tm, tk), lambda i, j, kk: (i, kk))
    if mode == "nt":
        b_spec = pl.BlockSpec((tn, tk), lambda i, j, kk: (j, kk))
    else:
        b_spec = pl.BlockSpec((tk, tn), lambda i, j, kk: (kk, j))
    mn_spec = pl.BlockSpec((tm, tn), lambda i, j, kk: (i, j))
    ins, in_specs = [a, b], [a_spec, b_spec]
    if addend is not None:
        ins.append(addend)
        in_specs.append(mn_spec)
    if gate_u is not None:
        ins.append(gate_u)
        in_specs.append(mn_spec)
    if out_t:
        out_shape = [jax.ShapeDtypeStruct((n, m), out_dtype)]
        out_specs = [pl.BlockSpec((tn, tm), lambda i, j, kk: (j, i))]
    else:
        out_shape = [jax.ShapeDtypeStruct((m, n), out_dtype)]
        out_specs = [mn_spec]
    if relu2:
        out_shape.append(jax.ShapeDtypeStruct((m, n), BF16))
        out_specs.append(mn_spec)
    has_add, has_gate = addend is not None, gate_u is not None

    def kern(*refs):
        a_ref, b_ref = refs[0], refs[1]
        pos = 2
        add_ref = gate_ref = None
        if has_add:
            add_ref = refs[pos]
            pos += 1
        if has_gate:
            gate_ref = refs[pos]
            pos += 1
        o_ref = refs[pos]
        a2_ref = refs[pos + 1] if relu2 else None
        acc = refs[-1] if nk > 1 else None
        kk = pl.program_id(2)

        def partial():
            return _dot(a_ref[...].astype(BF16), b_ref[...].astype(BF16), dims)

        if nk > 1:
            @pl.when(kk == 0)
            def _():
                acc[...] = partial()

            @pl.when((kk > 0) & (kk < nk - 1))
            def _():
                acc[...] += partial()

        @pl.when(kk == nk - 1)
        def _():
            r = partial() + acc[...] if nk > 1 else partial()
            if has_add:
                r = r + add_scale * add_ref[...].astype(F32)
            if has_gate:
                r = r * (2.0 * jnp.maximum(gate_ref[...], 0.0))
            if relu2:
                hh = jnp.maximum(r, 0.0)
                a2_ref[...] = (hh * hh).astype(BF16)
            if out_t:
                r = r.T
            o_ref[...] = r.astype(out_dtype)

    outs = pl.pallas_call(
        kern, out_shape=out_shape, grid=(m // tm, n // tn, nk), in_specs=in_specs, out_specs=out_specs,
        scratch_shapes=[pltpu.VMEM((tm, tn), F32)] if nk > 1 else [], name=name, compiler_params=_cp())(*ins)
    return outs if relu2 else outs[0]


def _add_ln(xres, y, g, b, name, tm=256):
    t = xres.shape[0]
    tm = min(tm, t)

    def kern(x_ref, y_ref, g_ref, b_ref, o_ref, ob_ref, xh_ref, r_ref):
        z = ALPHA * x_ref[...] + y_ref[...]
        mu = jnp.mean(z, axis=-1, keepdims=True)
        zc = z - mu
        var = jnp.mean(zc * zc, axis=-1, keepdims=True)
        r = lax.rsqrt(var + LN_EPS)
        xh = zc * r
        o = xh * g_ref[...] + b_ref[...]
        o_ref[...] = o
        ob_ref[...] = o.astype(BF16)
        xh_ref[...] = xh
        r_ref[...] = r

    row = pl.BlockSpec((tm, D), lambda i: (i, 0))
    vec = pl.BlockSpec((1, D), lambda i: (0, 0))
    st = pl.BlockSpec((tm, 1), lambda i: (i, 0))
    return pl.pallas_call(
        kern, grid=(t // tm,), in_specs=[row, row, vec, vec], out_specs=[row, row, row, st],
        out_shape=[jax.ShapeDtypeStruct((t, D), F32), jax.ShapeDtypeStruct((t, D), BF16),
                   jax.ShapeDtypeStruct((t, D), F32), jax.ShapeDtypeStruct((t, 1), F32)],
        name=name, compiler_params=_cp())(xres, y, g.reshape(1, D), b.reshape(1, D))


def _ln_bwd(dout, xhat, rstd, g, name, tm=256):
    t = dout.shape[0]
    tm = min(tm, t)

    def kern(do_ref, xh_ref, r_ref, g_ref, dz_ref, dzb_ref, dg_ref, db_ref):
        @pl.when(pl.program_id(0) == 0)
        def _():
            dg_ref[...] = jnp.zeros_like(dg_ref)
            db_ref[...] = jnp.zeros_like(db_ref)

        do = do_ref[...]
        xh = xh_ref[...]
        dxh = do * g_ref[...]
        m1 = jnp.mean(dxh, axis=-1, keepdims=True)
        m2 = jnp.mean(dxh * xh, axis=-1, keepdims=True)
        dz = r_ref[...] * (dxh - m1 - xh * m2)
        dz_ref[...] = dz
        dzb_ref[...] = dz.astype(BF16)
        dg_ref[...] += jnp.sum(do * xh, axis=0, keepdims=True)
        db_ref[...] += jnp.sum(do, axis=0, keepdims=True)

    row = pl.BlockSpec((tm, D), lambda i: (i, 0))
    vec = pl.BlockSpec((1, D), lambda i: (0, 0))
    st = pl.BlockSpec((tm, 1), lambda i: (i, 0))
    return pl.pallas_call(
        kern, grid=(t // tm,), in_specs=[row, row, st, vec], out_specs=[row, row, vec, vec],
        out_shape=[jax.ShapeDtypeStruct((t, D), F32), jax.ShapeDtypeStruct((t, D), BF16),
                   jax.ShapeDtypeStruct((1, D), F32), jax.ShapeDtypeStruct((1, D), F32)],
        name=name, compiler_params=_cp())(dout, xhat, rstd, g.reshape(1, D))


def _loss_grad(y, target, name="loss_grad", tm=256):
    t = y.shape[0]
    tm = min(tm, t)

    def kern(y_ref, t_ref, d_ref, l_ref):
        @pl.when(pl.program_id(0) == 0)
        def _():
            l_ref[...] = jnp.zeros_like(l_ref)

        e = y_ref[...] - t_ref[...]
        d_ref[...] = e * (1.0 / D)
        l_ref[...] += jnp.sum(e * e, axis=0, keepdims=True)

    row = pl.BlockSpec((tm, D), lambda i: (i, 0))
    vec = pl.BlockSpec((1, D), lambda i: (0, 0))
    return pl.pallas_call(
        kern, grid=(t // tm,), in_specs=[row, row], out_specs=[row, vec],
        out_shape=[jax.ShapeDtypeStruct((t, D), F32), jax.ShapeDtypeStruct((1, D), F32)],
        name=name, compiler_params=_cp())(y, target)


def _rope_tables(t):
    half = ROPE // 2
    inv = 10000.0 ** (-jnp.arange(half, dtype=F32) / half)
    ang = jnp.arange(t).astype(F32)[:, None] * inv[None, :]
    cos, sin = jnp.cos(ang), jnp.sin(ang)
    z = jnp.zeros((t, RP - ROPE), F32)
    return jnp.concatenate([cos, cos, z], axis=1), jnp.concatenate([-sin, sin, z], axis=1)


def _swap_halves(x):
    lane = lax.broadcasted_iota(jnp.int32, x.shape, 1)
    return jnp.where(lane < ROPE // 2, pltpu.roll(x, RP - ROPE // 2, 1), pltpu.roll(x, ROPE // 2, 1))


def _rope(x, cos, sin):
    return x * cos + _swap_halves(x) * sin


def _rope_t(gy, cos, sin):
    return gy * cos + _swap_halves(gy * sin)


def _mla_pre(hh, g_q, g_kv, cos, sin, tm=256):
    t = hh.shape[0]
    tm = min(tm, t)

    def kern(h_ref, gq_ref, gkv_ref, c_ref, s_ref, cq_ref, k_ref):
        xq = h_ref[:, 0:QR]
        rq = lax.rsqrt(jnp.mean(xq * xq, axis=-1, keepdims=True) + RMS_EPS)
        cq_ref[...] = (xq * rq * gq_ref[...]).astype(BF16)
        xk = h_ref[:, QR:QR + KVR]
        rk = lax.rsqrt(jnp.mean(xk * xk, axis=-1, keepdims=True) + RMS_EPS)
        k_ref[:, 0:KVR] = (xk * rk * gkv_ref[...]).astype(BF16)
        k_ref[:, KVR:KD] = _rope(h_ref[:, QR + KVR:HW], c_ref[...], s_ref[...]).astype(BF16)

    return pl.pallas_call(
        kern, grid=(t // tm,),
        in_specs=[pl.BlockSpec((tm, HW), lambda i: (i, 0)), pl.BlockSpec((1, QR), lambda i: (0, 0)),
                  pl.BlockSpec((1, KVR), lambda i: (0, 0)), pl.BlockSpec((tm, RP), lambda i: (i, 0)),
                  pl.BlockSpec((tm, RP), lambda i: (i, 0))],
        out_specs=[pl.BlockSpec((tm, QR), lambda i: (i, 0)), pl.BlockSpec((tm, KD), lambda i: (i, 0))],
        out_shape=[jax.ShapeDtypeStruct((t, QR), BF16), jax.ShapeDtypeStruct((t, KD), BF16)],
        name="mla_pre", compiler_params=_cp())(hh, g_q.reshape(1, QR), g_kv.reshape(1, KVR), cos, sin)


def _mla_pre_bwd(hh, dcq, dk, g_q, g_kv, cos, sin, tm=256):
    t = hh.shape[0]
    tm = min(tm, t)

    def rms_bwd(x, dy, g):
        r = lax.rsqrt(jnp.mean(x * x, axis=-1, keepdims=True) + RMS_EPS)
        gdy = dy * g
        dx = r * gdy - x * (r * r * r) * jnp.mean(gdy * x, axis=-1, keepdims=True)
        return dx, jnp.sum(dy * x * r, axis=0, keepdims=True)

    def kern(h_ref, dcq_ref, dk_ref, gq_ref, gkv_ref, c_ref, s_ref, dh_ref, dgq_ref, dgkv_ref):
        @pl.when(pl.program_id(0) == 0)
        def _():
            dgq_ref[...] = jnp.zeros_like(dgq_ref)
            dgkv_ref[...] = jnp.zeros_like(dgkv_ref)

        dxq, dgq = rms_bwd(h_ref[:, 0:QR], dcq_ref[...], gq_ref[...])
        dxk, dgk = rms_bwd(h_ref[:, QR:QR + KVR], dk_ref[:, 0:KVR], gkv_ref[...])
        dh_ref[:, 0:QR] = dxq.astype(BF16)
        dh_ref[:, QR:QR + KVR] = dxk.astype(BF16)
        dh_ref[:, QR + KVR:HW] = _rope_t(dk_ref[:, KVR:KD], c_ref[...], s_ref[...]).astype(BF16)
        dgq_ref[...] += dgq
        dgkv_ref[...] += dgk

    return pl.pallas_call(
        kern, grid=(t // tm,),
        in_specs=[pl.BlockSpec((tm, HW), lambda i: (i, 0)), pl.BlockSpec((tm, QR), lambda i: (i, 0)),
                  pl.BlockSpec((tm, KD), lambda i: (i, 0)), pl.BlockSpec((1, QR), lambda i: (0, 0)),
                  pl.BlockSpec((1, KVR), lambda i: (0, 0)), pl.BlockSpec((tm, RP), lambda i: (i, 0)),
                  pl.BlockSpec((tm, RP), lambda i: (i, 0))],
        out_specs=[pl.BlockSpec((tm, HW), lambda i: (i, 0)), pl.BlockSpec((1, QR), lambda i: (0, 0)),
                   pl.BlockSpec((1, KVR), lambda i: (0, 0))],
        out_shape=[jax.ShapeDtypeStruct((t, HW), BF16), jax.ShapeDtypeStruct((1, QR), F32),
                   jax.ShapeDtypeStruct((1, KVR), F32)],
        name="mla_pre_bwd", compiler_params=_cp())(hh, dcq, dk, g_q.reshape(1, QR), g_kv.reshape(1, KVR), cos, sin)


def _q_prep(q2, wuk_t, cos, sin, tm=256):
    t = q2.shape[0]
    tm = min(tm, t)

    def kern(q_ref, w_ref, c_ref, s_ref, o_ref):
        cos_, sin_ = c_ref[...], s_ref[...]
        for h in range(H):
            qn = q_ref[:, h * NOPE:(h + 1) * NOPE].astype(BF16)
            o_ref[:, h * KD:h * KD + KVR] = (_dot(qn, w_ref[h], NN) * QSCALE).astype(BF16)
            qr = q_ref[:, H * NOPE + h * RP:H * NOPE + (h + 1) * RP]
            o_ref[:, h * KD + KVR:(h + 1) * KD] = (_rope(qr, cos_, sin_) * QSCALE).astype(BF16)

    return pl.pallas_call(
        kern, grid=(t // tm,),
        in_specs=[pl.BlockSpec((tm, 2 * H * NOPE), lambda i: (i, 0)), pl.BlockSpec((H, NOPE, KVR), lambda i: (0, 0, 0)),
                  pl.BlockSpec((tm, RP), lambda i: (i, 0)), pl.BlockSpec((tm, RP), lambda i: (i, 0))],
        out_specs=pl.BlockSpec((tm, H * KD), lambda i: (i, 0)),
        out_shape=jax.ShapeDtypeStruct((t, H * KD), BF16),
        name="q_prep", compiler_params=_cp())(q2, wuk_t, cos, sin)


def _q_prep_bwd(dq_cat, q2, wuk_h, cos, sin, tm=256):
    t = q2.shape[0]
    tm = min(tm, t)

    def kern(dq_ref, q_ref, w_ref, c_ref, s_ref, o_ref, dw_ref):
        @pl.when(pl.program_id(0) == 0)
        def _():
            dw_ref[...] = jnp.zeros_like(dw_ref)

        cos_, sin_ = c_ref[...], s_ref[...]
        for h in range(H):
            dql = dq_ref[:, h * KD:h * KD + KVR].astype(BF16)
            o_ref[:, h * NOPE:(h + 1) * NOPE] = _dot(dql, w_ref[h], NN).astype(BF16)
            dqr = dq_ref[:, h * KD + KVR:(h + 1) * KD]
            o_ref[:, H * NOPE + h * RP:H * NOPE + (h + 1) * RP] = _rope_t(dqr, cos_, sin_).astype(BF16)
            qn = q_ref[:, h * NOPE:(h + 1) * NOPE].astype(BF16)
            dw_ref[h] += _dot(qn, dql, TN)

    return pl.pallas_call(
        kern, grid=(t // tm,),
        in_specs=[pl.BlockSpec((tm, H * KD), lambda i: (i, 0)), pl.BlockSpec((tm, 2 * H * NOPE), lambda i: (i, 0)),
                  pl.BlockSpec((H, KVR, NOPE), lambda i: (0, 0, 0)),
                  pl.BlockSpec((tm, RP), lambda i: (i, 0)), pl.BlockSpec((tm, RP), lambda i: (i, 0))],
        out_specs=[pl.BlockSpec((tm, 2 * H * NOPE), lambda i: (i, 0)), pl.BlockSpec((H, NOPE, KVR), lambda i: (0, 0, 0))],
        out_shape=[jax.ShapeDtypeStruct((t, 2 * H * NOPE), BF16), jax.ShapeDtypeStruct((H, NOPE, KVR), F32)],
        name="q_prep_bwd", compiler_params=_cp())(dq_cat, q2, wuk_h, cos, sin)


def _o_up(o_lat, wuv_h, tm=256):
    t = o_lat.shape[0]
    tm = min(tm, t)

    def kern(x_ref, w_ref, o_ref):
        for h in range(H):
            xl = x_ref[:, h * KVR:(h + 1) * KVR].astype(BF16)
            o_ref[:, h * NOPE:(h + 1) * NOPE] = _dot(xl, w_ref[h], NN).astype(BF16)

    return pl.pallas_call(
        kern, grid=(t // tm,),
        in_specs=[pl.BlockSpec((tm, H * KVR), lambda i: (i, 0)), pl.BlockSpec((H, KVR, NOPE), lambda i: (0, 0, 0))],
        out_specs=pl.BlockSpec((tm, H * NOPE), lambda i: (i, 0)),
        out_shape=jax.ShapeDtypeStruct((t, H * NOPE), BF16),
        name="o_up", compiler_params=_cp())(o_lat, wuv_h)


def _o_up_bwd(do, o_lat, wuv_h, tm=256):
    t = do.shape[0]
    tm = min(tm, t)

    def kern(do_ref, x_ref, w_ref, dx_ref, dw_ref):
        @pl.when(pl.program_id(0) == 0)
        def _():
            dw_ref[...] = jnp.zeros_like(dw_ref)

        for h in range(H):
            dh_ = do_ref[:, h * NOPE:(h + 1) * NOPE]
            dx_ref[:, h * KVR:(h + 1) * KVR] = _dot(dh_, w_ref[h], NT)
            xl = x_ref[:, h * KVR:(h + 1) * KVR].astype(BF16)
            dw_ref[h] += _dot(xl, dh_, TN)

    return pl.pallas_call(
        kern, grid=(t // tm,),
        in_specs=[pl.BlockSpec((tm, H * NOPE), lambda i: (i, 0)), pl.BlockSpec((tm, H * KVR), lambda i: (i, 0)),
                  pl.BlockSpec((H, KVR, NOPE), lambda i: (0, 0, 0))],
        out_specs=[pl.BlockSpec((tm, H * KVR), lambda i: (i, 0)), pl.BlockSpec((H, KVR, NOPE), lambda i: (0, 0, 0))],
        out_shape=[jax.ShapeDtypeStruct((t, H * KVR), F32), jax.ShapeDtypeStruct((H, KVR, NOPE), F32)],
        name="o_up_bwd", compiler_params=_cp())(do, o_lat, wuv_h)


def _causal_pairs(nq):
    return [(i, j) for i in range(nq) for j in range(i + 1)]


def _lane_tile(stat, width):
    return jnp.tile(stat, (1, width // 128))


def _flash_fwd(qcat, kc, bq, hb):
    t = kc.shape[0]
    nq = t // bq
    pairs = _causal_pairs(nq)
    itab = jnp.asarray(np.array([p[0] for p in pairs], np.int32))
    jtab = jnp.asarray(np.array([p[1] for p in pairs], np.int32))

    def kern(it, jt, q_ref, k_ref, o_ref, lse_ref, lset_ref, m_sc, l_sc, acc_sc):
        st = pl.program_id(1)
        i, j = it[st], jt[st]

        @pl.when(j == 0)
        def _():
            m_sc[...] = jnp.full_like(m_sc, NEG)
            l_sc[...] = jnp.zeros_like(l_sc)
            acc_sc[...] = jnp.zeros_like(acc_sc)

        def update(masked):
            k = k_ref[...]
            v = k[:, 0:KVR]
            if masked:
                row = lax.broadcasted_iota(jnp.int32, (bq, bq), 0)
                col = lax.broadcasted_iota(jnp.int32, (bq, bq), 1)
                keep = col <= row
            s_next = _dot(q_ref[:, 0:KD], k, NT)
            for hh in range(hb):
                s = s_next
                if hh + 1 < hb:
                    s_next = _dot(q_ref[:, (hh + 1) * KD:(hh + 2) * KD], k, NT)
                if masked:
                    s = jnp.where(keep, s, NEG)
                m_prev = m_sc[hh]
                m_next = jnp.maximum(m_prev, jnp.max(s, axis=1)[:, None])
                p = jnp.exp2(s - _lane_tile(m_next, bq))
                a = jnp.exp2(m_prev - m_next)
                l_sc[hh] = a * l_sc[hh] + jnp.sum(p, axis=1)[:, None]
                acc_sc[hh] = _lane_tile(a, KVR) * acc_sc[hh] + _dot(p.astype(BF16), v, NN)
                m_sc[hh] = m_next

        @pl.when(j < i)
        def _():
            update(False)

        @pl.when(j == i)
        def _():
            update(True)
            for hh in range(hb):
                l = l_sc[hh]
                o_ref[:, hh * KVR:(hh + 1) * KVR] = acc_sc[hh] / _lane_tile(l, KVR)
                lse = m_sc[hh] + jnp.log2(l)
                lse_ref[hh] = lse
                lset_ref[hh] = lse.T[0:1, :]

    gs = pltpu.PrefetchScalarGridSpec(
        num_scalar_prefetch=2, grid=(H // hb, len(pairs)),
        in_specs=[pl.BlockSpec((bq, hb * KD), lambda g, s, it, jt: (it[s], g)),
                  pl.BlockSpec((bq, KD), lambda g, s, it, jt: (jt[s], 0))],
        out_specs=[pl.BlockSpec((bq, hb * KVR), lambda g, s, it, jt: (it[s], g)),
                   pl.BlockSpec((hb, bq, 128), lambda g, s, it, jt: (g, it[s], 0)),
                   pl.BlockSpec((hb, 1, bq), lambda g, s, it, jt: (g, 0, it[s]))],
        scratch_shapes=[pltpu.VMEM((hb, bq, 128), F32), pltpu.VMEM((hb, bq, 128), F32),
                        pltpu.VMEM((hb, bq, KVR), F32)])
    return pl.pallas_call(
        kern, grid_spec=gs,
        out_shape=[jax.ShapeDtypeStruct((t, H * KVR), F32), jax.ShapeDtypeStruct((H, t, 128), F32),
                   jax.ShapeDtypeStruct((H, 1, t), F32)],
        name="mla_flash_fwd", compiler_params=_cp())(itab, jtab, qcat, kc)


def _flash_dq(qcat, kc, do_lat, o_lat, lse, bq, hb):
    t = kc.shape[0]
    nq = t // bq
    pairs = _causal_pairs(nq)
    itab = jnp.asarray(np.array([p[0] for p in pairs], np.int32))
    jtab = jnp.asarray(np.array([p[1] for p in pairs], np.int32))

    def kern(it, jt, q_ref, k_ref, do_ref, o_ref, lse_ref, dq_ref, dlt_ref, acc_sc, dl_sc):
        st = pl.program_id(1)
        i, j = it[st], jt[st]

        @pl.when(j == 0)
        def _():
            acc_sc[...] = jnp.zeros_like(acc_sc)
            for hh in range(hb):
                cs = slice(hh * KVR, (hh + 1) * KVR)
                dl = jnp.broadcast_to(jnp.sum(do_ref[:, cs] * o_ref[:, cs], axis=1)[:, None], (bq, 128))
                dl_sc[hh] = dl
                dlt_ref[hh] = dl.T[0:1, :]

        def update(masked):
            k = k_ref[...]
            v = k[:, 0:KVR]
            if masked:
                row = lax.broadcasted_iota(jnp.int32, (bq, bq), 0)
                col = lax.broadcasted_iota(jnp.int32, (bq, bq), 1)
                keep = col <= row
            for hh in range(hb):
                s = _dot(q_ref[:, hh * KD:(hh + 1) * KD], k, NT)
                if masked:
                    s = jnp.where(keep, s, NEG)
                p = jnp.exp2(s - _lane_tile(lse_ref[hh], bq))
                dp = _dot(do_ref[:, hh * KVR:(hh + 1) * KVR].astype(BF16), v, NT)
                ds = p * (dp - _lane_tile(dl_sc[hh], bq))
                acc_sc[hh] += _dot(ds.astype(BF16), k, NN)

        @pl.when(j < i)
        def _():
            update(False)

        @pl.when(j == i)
        def _():
            update(True)
            for hh in range(hb):
                dq_ref[:, hh * KD:(hh + 1) * KD] = acc_sc[hh] * MLA_SCALE

    gs = pltpu.PrefetchScalarGridSpec(
        num_scalar_prefetch=2, grid=(H // hb, len(pairs)),
        in_specs=[pl.BlockSpec((bq, hb * KD), lambda g, s, it, jt: (it[s], g)),
                  pl.BlockSpec((bq, KD), lambda g, s, it, jt: (jt[s], 0)),
                  pl.BlockSpec((bq, hb * KVR), lambda g, s, it, jt: (it[s], g)),
                  pl.BlockSpec((bq, hb * KVR), lambda g, s, it, jt: (it[s], g)),
                  pl.BlockSpec((hb, bq, 128), lambda g, s, it, jt: (g, it[s], 0))],
        out_specs=[pl.BlockSpec((bq, hb * KD), lambda g, s, it, jt: (it[s], g)),
                   pl.BlockSpec((hb, 1, bq), lambda g, s, it, jt: (g, 0, it[s]))],
        scratch_shapes=[pltpu.VMEM((hb, bq, KD), F32), pltpu.VMEM((hb, bq, 128), F32)])
    return pl.pallas_call(
        kern, grid_spec=gs,
        out_shape=[jax.ShapeDtypeStruct((t, H * KD), F32), jax.ShapeDtypeStruct((H, 1, t), F32)],
        name="mla_flash_dq", compiler_params=_cp())(itab, jtab, qcat, kc, do_lat, o_lat, lse)


def _flash_dkv(qcat, kc, do_lat, lse_t, delta_t, bq, hb):
    t = kc.shape[0]
    nq = t // bq
    ng = H // hb
    steps = [(j, g, i) for j in range(nq) for g in range(ng) for i in range(j, nq)]
    jtab = jnp.asarray(np.array([s[0] for s in steps], np.int32))
    gtab = jnp.asarray(np.array([s[1] for s in steps], np.int32))
    itab = jnp.asarray(np.array([s[2] for s in steps], np.int32))

    def kern(jt, gt, it, q_ref, k_ref, do_ref, lset_ref, dlt_ref, dk_ref, dk_sc, dv_sc):
        st = pl.program_id(0)
        j, g, i = jt[st], gt[st], it[st]

        @pl.when((g == 0) & (i == j))
        def _():
            dk_sc[...] = jnp.zeros_like(dk_sc)
            dv_sc[...] = jnp.zeros_like(dv_sc)

        def update(masked):
            k = k_ref[...]
            v = k[:, 0:KVR]
            if masked:
                row = lax.broadcasted_iota(jnp.int32, (bq, bq), 0)
                col = lax.broadcasted_iota(jnp.int32, (bq, bq), 1)
                keep = row <= col
            for hh in range(hb):
                q = q_ref[:, hh * KD:(hh + 1) * KD]
                dob = do_ref[:, hh * KVR:(hh + 1) * KVR].astype(BF16)
                s = _dot(k, q, NT)
                if masked:
                    s = jnp.where(keep, s, NEG)
                p = jnp.exp2(s - lset_ref[hh])
                dv_sc[...] += _dot(p.astype(BF16), dob, NN)
                dp = _dot(v, dob, NT)
                ds = p * (dp - dlt_ref[hh])
                dk_sc[...] += _dot(ds.astype(BF16), q, NN)

        @pl.when(i > j)
        def _():
            update(False)

        @pl.when(i == j)
        def _():
            update(True)

        @pl.when((g == ng - 1) & (i == nq - 1))
        def _():
            dk_ref[:, 0:KVR] = dk_sc[:, 0:KVR] * LN2 + dv_sc[...]
            dk_ref[:, KVR:KD] = dk_sc[:, KVR:KD] * LN2

    gs = pltpu.PrefetchScalarGridSpec(
        num_scalar_prefetch=3, grid=(len(steps),),
        in_specs=[pl.BlockSpec((bq, hb * KD), lambda s, jt, gt, it: (it[s], gt[s])),
                  pl.BlockSpec((bq, KD), lambda s, jt, gt, it: (jt[s], 0)),
                  pl.BlockSpec((bq, hb * KVR), lambda s, jt, gt, it: (it[s], gt[s])),
                  pl.BlockSpec((hb, 1, bq), lambda s, jt, gt, it: (gt[s], 0, it[s])),
                  pl.BlockSpec((hb, 1, bq), lambda s, jt, gt, it: (gt[s], 0, it[s]))],
        out_specs=pl.BlockSpec((bq, KD), lambda s, jt, gt, it: (jt[s], 0)),
        scratch_shapes=[pltpu.VMEM((bq, KD), F32), pltpu.VMEM((bq, KVR), F32)])
    return pl.pallas_call(
        kern, grid_spec=gs, out_shape=jax.ShapeDtypeStruct((t, KD), F32),
        name="mla_flash_dkv", compiler_params=_cp())(jtab, gtab, itab, qcat, kc, do_lat, lse_t, delta_t)


def _bucket_table():
    d = np.arange(WIN)
    max_exact = NBKT // 2
    nf = np.maximum(d, 1).astype(np.float32)
    large = max_exact + (np.log(nf / np.float32(max_exact)) / np.float32(math.log(WIN / max_exact))
                         * np.float32(NBKT - max_exact)).astype(np.int32)
    large = np.minimum(large, NBKT - 1)
    bucket = np.where(d < max_exact, d, large).astype(np.int32)
    jj = np.arange(2 * WIN)[:, None]
    ii = np.arange(WIN)[None, :]
    dist = ii + WIN - jj
    valid = (dist >= 0) & (dist < WIN)
    return np.where(valid, bucket[np.clip(dist, 0, WIN - 1)], -1).astype(np.int32)


def _bias_build(rel_bias, bkt):
    def kern(bk_ref, rb_ref, o_ref):
        bk = bk_ref[...]
        for hd in range(QH):
            acc = jnp.full((2 * WIN, WIN), NEG, F32)
            for b in range(NBKT):
                acc = jnp.where(bk == b, rb_ref[b, hd], acc)
            o_ref[hd] = acc

    return pl.pallas_call(
        kern, in_specs=[pl.BlockSpec(memory_space=pltpu.VMEM), pl.BlockSpec(memory_space=pltpu.SMEM)],
        out_specs=pl.BlockSpec(memory_space=pltpu.VMEM),
        out_shape=jax.ShapeDtypeStruct((QH, 2 * WIN, WIN), F32), name="swa_bias_build")(bkt, rel_bias)


def _bias_bwd(dbias, bkt):
    def kern(db_ref, bk_ref, o_ref):
        bk = bk_ref[...]
        for hd in range(QH):
            g = db_ref[hd]
            for b in range(NBKT):
                r = b * QH + hd
                o_ref[r:r + 1, :] = jnp.sum(jnp.where(bk == b, g, 0.0), axis=0, keepdims=True)

    return pl.pallas_call(
        kern, in_specs=[pl.BlockSpec(memory_space=pltpu.VMEM), pl.BlockSpec(memory_space=pltpu.VMEM)],
        out_specs=pl.BlockSpec(memory_space=pltpu.VMEM),
        out_shape=jax.ShapeDtypeStruct((NBKT * QH, WIN), F32), name="swa_bias_bwd")(dbias, bkt)


def _swa_finish_scores(raw, bias, first):
    s = raw * SWA_SCALE + bias
    if first is not None:
        row = lax.broadcasted_iota(jnp.int32, s.shape, 0)
        s = jnp.where(jnp.logical_or(jnp.logical_not(first), row >= WIN), s, NEG)
    return s


def _swa_fwd(qkv_t, bias, sinks, qb):
    t = qkv_t.shape[1]
    w = qb * WIN
    nst = t // w

    def kern(q_ref, kc_ref, kp_ref, vc_ref, vp_ref, b_ref, sk_ref, o_ref, lse_ref):
        n = pl.program_id(0)
        kfull = jnp.concatenate([kp_ref[...], kc_ref[...]], axis=1)
        vfull = jnp.concatenate([vp_ref[...], vc_ref[...]], axis=1)
        head_row = lax.broadcasted_iota(jnp.int32, (QH, WIN), 0)
        groups = [(b, kh) for b in range(qb) for kh in range(KVH)]

        def raw_scores(b, kh):
            k_band = kfull[kh * HD:(kh + 1) * HD, b * WIN:(b + 2) * WIN]
            return [_dot(k_band, q_ref[(kh * G + g) * HD:(kh * G + g + 1) * HD, b * WIN:(b + 1) * WIN], TN)
                    for g in range(G)]

        o_rows = [[] for _ in range(qb)]
        lse_tiles = [jnp.zeros((QH, WIN), F32) for _ in range(qb)]
        nxt_scores = raw_scores(*groups[0])
        for gi, (b, kh) in enumerate(groups):
            scores = nxt_scores
            if gi + 1 < len(groups):
                nxt_scores = raw_scores(*groups[gi + 1])
            v_band = vfull[kh * HD:(kh + 1) * HD, b * WIN:(b + 2) * WIN]
            for g in range(G):
                hd = kh * G + g
                s = _swa_finish_scores(scores[g], b_ref[hd], (n == 0) if b == 0 else None)
                sink = sk_ref[hd]
                m = jnp.maximum(jnp.max(s, axis=0, keepdims=True), sink)
                p = jnp.exp(s - m)
                den = jnp.sum(p, axis=0, keepdims=True) + jnp.exp(sink - m)
                p = p / den
                o_rows[b].append(_dot(v_band, p.astype(BF16), NN))
                lse_tiles[b] = jnp.where(head_row == hd, m + jnp.log(den), lse_tiles[b])
        o_ref[...] = jnp.concatenate([jnp.concatenate(rows, axis=0) for rows in o_rows], axis=1)
        lse_ref[...] = jnp.concatenate(lse_tiles, axis=1)

    prev = lambda r: (lambda n: (r, jnp.maximum(n * qb - 1, 0)))
    return pl.pallas_call(
        kern, grid=(nst,),
        in_specs=[pl.BlockSpec((QH * HD, w), lambda n: (0, n)),
                  pl.BlockSpec((KVH * HD, w), lambda n: (4, n)), pl.BlockSpec((KVH * HD, WIN), prev(4)),
                  pl.BlockSpec((KVH * HD, w), lambda n: (5, n)), pl.BlockSpec((KVH * HD, WIN), prev(5)),
                  pl.BlockSpec((QH, 2 * WIN, WIN), lambda n: (0, 0, 0)),
                  pl.BlockSpec(memory_space=pltpu.SMEM)],
        out_specs=[pl.BlockSpec((QH * HD, w), lambda n: (0, n)), pl.BlockSpec((QH, w), lambda n: (0, n))],
        out_shape=[jax.ShapeDtypeStruct((QH * HD, t), F32), jax.ShapeDtypeStruct((QH, t), F32)],
        name="swa_fwd", compiler_params=_cp())(qkv_t, qkv_t, qkv_t, qkv_t, qkv_t, bias, sinks)


def _swa_bwd(qkv_t, do_t, o_t, lse, bias, sinks, qb):
    t = qkv_t.shape[1]
    w = qb * WIN
    nst = t // w
    nblk = t // WIN

    def kern(q_ref, kc_ref, kp_ref, vc_ref, vp_ref, do_ref, o_ref, lse_ref, qn_ref, don_ref, on_ref, lsen_ref,
             b_ref, sk_ref, dqkv_ref, db_ref, dsk_ref):
        n = pl.program_id(0)

        @pl.when(n == 0)
        def _():
            db_ref[...] = jnp.zeros_like(db_ref)
            dsk_ref[...] = jnp.zeros_like(dsk_ref)

        kfull = jnp.concatenate([kp_ref[...], kc_ref[...]], axis=1)
        vfull = jnp.concatenate([vp_ref[...], vc_ref[...]], axis=1)
        head_row = lax.broadcasted_iota(jnp.int32, (QH, WIN), 0)
        db_acc = [None] * QH
        dsk_tile = jnp.zeros((QH, WIN), F32)
        prev_part = [[[None] * qb for _ in range(KVH)] for _ in range(2)]
        cur_part = [[[None] * qb for _ in range(KVH)] for _ in range(2)]
        groups = [(b, kh) for b in range(qb) for kh in range(KVH)]

        def first_matmuls(b, kh):
            k_band = kfull[kh * HD:(kh + 1) * HD, b * WIN:(b + 2) * WIN]
            v_band = vfull[kh * HD:(kh + 1) * HD, b * WIN:(b + 2) * WIN]
            out = []
            for g in range(G):
                rs = slice((kh * G + g) * HD, (kh * G + g + 1) * HD)
                dob = do_ref[rs, b * WIN:(b + 1) * WIN].astype(BF16)
                out.append((_dot(k_band, q_ref[rs, b * WIN:(b + 1) * WIN], TN), _dot(v_band, dob, TN), dob))
            return out

        dq_rows = [[] for _ in range(qb)]
        nxt_first = first_matmuls(*groups[0])
        for gi, (b, kh) in enumerate(groups):
            first = nxt_first
            if gi + 1 < len(groups):
                nxt_first = first_matmuls(*groups[gi + 1])
            cs = slice(b * WIN, (b + 1) * WIN)
            k_band = kfull[kh * HD:(kh + 1) * HD, b * WIN:(b + 2) * WIN]
            dk_b = dv_b = None
            for g in range(G):
                hd = kh * G + g
                rs = slice(hd * HD, (hd + 1) * HD)
                raw, dp, dob = first[g]
                lse_h = lse_ref[hd:hd + 1, cs]
                s = _swa_finish_scores(raw, b_ref[hd], (n == 0) if b == 0 else None)
                p = jnp.exp(s - lse_h)
                dl = jnp.sum(do_ref[rs, cs] * o_ref[rs, cs], axis=0, keepdims=True)
                ds = p * (dp - dl)
                db_acc[hd] = ds if db_acc[hd] is None else db_acc[hd] + ds
                dsk_tile = jnp.where(head_row == hd, dsk_tile - jnp.exp(sk_ref[hd] - lse_h) * dl, dsk_tile)
                dss = (ds * SWA_SCALE).astype(BF16)
                dq_rows[b].append(_dot(k_band, dss, NN).astype(BF16))
                dk_h = _dot(q_ref[rs, cs], dss, NT)
                dv_h = _dot(dob, p.astype(BF16), NT)
                dk_b = dk_h if dk_b is None else dk_b + dk_h
                dv_b = dv_h if dv_b is None else dv_b + dv_h
            for which, val in ((0, dk_b), (1, dv_b)):
                prev_part[which][kh][b] = val[:, 0:WIN]
                cur_part[which][kh][b] = val[:, WIN:2 * WIN]
        dq_cols = [jnp.concatenate(rows, axis=0) for rows in dq_rows]

        live = n < nst - 1
        ls = slice((qb - 1) * WIN, qb * WIN)
        halo = [[None] * KVH for _ in range(2)]
        for kh in range(KVH):
            k_last = kc_ref[kh * HD:(kh + 1) * HD, ls]
            v_last = vc_ref[kh * HD:(kh + 1) * HD, ls]
            dk_b = dv_b = None
            for g in range(G):
                hd = kh * G + g
                rs = slice(hd * HD, (hd + 1) * HD)
                q_t = qn_ref[rs, :]
                do = don_ref[rs, :]
                s = _dot(k_last, q_t, TN) * SWA_SCALE + b_ref[hd, 0:WIN, :]
                p = jnp.exp(s - lsen_ref[hd:hd + 1, :])
                dob = do.astype(BF16)
                dp = _dot(v_last, dob, TN)
                dl = jnp.sum(do * on_ref[rs, :], axis=0, keepdims=True)
                dss = (p * (dp - dl) * SWA_SCALE).astype(BF16)
                dk_h = _dot(q_t, dss, NT)
                dv_h = _dot(dob, p.astype(BF16), NT)
                dk_b = dk_h if dk_b is None else dk_b + dk_h
                dv_b = dv_h if dv_b is None else dv_b + dv_h
            halo[0][kh] = jnp.where(live, dk_b, 0.0)
            halo[1][kh] = jnp.where(live, dv_b, 0.0)

        kv_rows = []
        for which in range(2):
            for kh in range(KVH):
                blocks = [cur_part[which][kh][p] + (prev_part[which][kh][p + 1] if p + 1 < qb else halo[which][kh])
                          for p in range(qb)]
                kv_rows.append(jnp.concatenate(blocks, axis=1))
        dqkv_ref[...] = jnp.concatenate(
            [jnp.concatenate(dq_cols, axis=1), jnp.concatenate(kv_rows, axis=0).astype(BF16)], axis=0)
        db_ref[...] += jnp.stack(db_acc)
        dsk_ref[...] += dsk_tile

    prev = lambda r: (lambda n: (r, jnp.maximum(n * qb - 1, 0)))
    nxt = lambda n: (0, jnp.minimum((n + 1) * qb, nblk - 1))
    big = lambda: pl.BlockSpec((QH * HD, w), lambda n: (0, n))
    return pl.pallas_call(
        kern, grid=(nst,),
        in_specs=[big(),
                  pl.BlockSpec((KVH * HD, w), lambda n: (4, n)), pl.BlockSpec((KVH * HD, WIN), prev(4)),
                  pl.BlockSpec((KVH * HD, w), lambda n: (5, n)), pl.BlockSpec((KVH * HD, WIN), prev(5)),
                  big(), big(), pl.BlockSpec((QH, w), lambda n: (0, n)),
                  pl.BlockSpec((QH * HD, WIN), nxt), pl.BlockSpec((QH * HD, WIN), nxt),
                  pl.BlockSpec((QH * HD, WIN), nxt), pl.BlockSpec((QH, WIN), nxt),
                  pl.BlockSpec((QH, 2 * WIN, WIN), lambda n: (0, 0, 0)),
                  pl.BlockSpec(memory_space=pltpu.SMEM)],
        out_specs=[pl.BlockSpec(((QH + 2 * KVH) * HD, w), lambda n: (0, n)),
                   pl.BlockSpec((QH, 2 * WIN, WIN), lambda n: (0, 0, 0)),
                   pl.BlockSpec((QH, WIN), lambda n: (0, 0))],
        out_shape=[jax.ShapeDtypeStruct(((QH + 2 * KVH) * HD, t), BF16),
                   jax.ShapeDtypeStruct((QH, 2 * WIN, WIN), F32), jax.ShapeDtypeStruct((QH, WIN), F32)],
        name="swa_bwd", compiler_params=_cp())(
            qkv_t, qkv_t, qkv_t, qkv_t, qkv_t, do_t, o_t, lse, qkv_t, do_t, o_t, lse, bias, sinks)


def _adamw(w, g, m, v, name, tm=544):
    r = w.shape[0]
    tm = r if r % tm else tm
    c1 = 1.0 / (1.0 - B1 ** STEP)
    c2 = 1.0 / (1.0 - B2 ** STEP)

    def kern(w_ref, g_ref, m_ref, v_ref, d_ref, nm_ref, nv_ref):
        g_ = g_ref[...]
        nm = B1 * m_ref[...] + (1.0 - B1) * g_
        nv = B2 * v_ref[...] + (1.0 - B2) * (g_ * g_)
        d_ref[...] = -LR * ((nm * c1) / (jnp.sqrt(nv * c2) + ADAM_EPS) + WD * w_ref[...])
        nm_ref[...] = nm
        nv_ref[...] = nv

    row = pl.BlockSpec((tm, D), lambda i: (i, 0))
    sds = jax.ShapeDtypeStruct((r, D), F32)
    return pl.pallas_call(kern, grid=(r // tm,), in_specs=[row] * 4, out_specs=[row] * 3, out_shape=[sds] * 3,
                          name=name, compiler_params=_cp())(w, g, m, v)


def _mesh_pos():
    return lax.axis_index("x"), lax.axis_index("y"), lax.axis_index("c")


ANY = pl.BlockSpec(memory_space=pl.ANY)


def _allgather_weights(wpack):
    r = wpack.shape[0]
    half = r // 2

    def body(w_ref, out_ref, send_sems, recv_sems):
        x, y, c = _mesh_pos()
        sibling = (x, y, 1 - c)
        chips = [(1 - x, y), (x, 1 - y), (1 - x, 1 - y)]

        def rows(px, py, pc):
            return out_ref.at[2 * px + py, pl.ds(pc * half, half), :]

        def copy(k, block, to, src=None):
            return pltpu.make_async_remote_copy(
                src_ref=rows(*block) if src is None else src, dst_ref=rows(*block),
                send_sem=send_sems.at[k], recv_sem=recv_sems.at[k], device_id=to, device_id_type=MESH)

        first = [copy(j, (x, y, c), (*chip, c), src=w_ref.at[pl.ds(c * half, half), :]) for j, chip in enumerate(chips)]
        for cp in first:
            cp.start()
        passed = [copy(3 + j, (*chip, c), sibling) for j, chip in enumerate(chips)]
        for j, chip in enumerate(chips):
            copy(j, (*chip, c), (x, y, c)).wait_recv()
            passed[j].start()
        for j, chip in enumerate(chips):
            copy(3 + j, (*chip, 1 - c), (x, y, c)).wait_recv()
        for cp in first + passed:
            cp.wait_send()

    return pl.pallas_call(
        body, out_shape=jax.ShapeDtypeStruct((4, r, D), wpack.dtype), in_specs=[ANY], out_specs=ANY,
        scratch_shapes=[pltpu.SemaphoreType.DMA((6,)), pltpu.SemaphoreType.DMA((6,))],
        name="allgather_weights")(wpack)


def _exchange_core_halves(g):
    half = g.shape[1] // 2

    def body(g_ref, out_ref, send_sem, recv_sem):
        x, y, c = _mesh_pos()
        cp = pltpu.make_async_remote_copy(
            src_ref=g_ref.at[:, pl.ds((1 - c) * half, half), :], dst_ref=out_ref,
            send_sem=send_sem, recv_sem=recv_sem, device_id=(x, y, 1 - c), device_id_type=MESH)
        cp.start()
        cp.wait()

    return pl.pallas_call(
        body, out_shape=jax.ShapeDtypeStruct((4, half, D), g.dtype), in_specs=[ANY], out_specs=ANY,
        scratch_shapes=[pltpu.SemaphoreType.DMA, pltpu.SemaphoreType.DMA], name="rs_exchange_cores")(g)


def _add_core_halves(g, other, cidx, tm=544):
    half = other.shape[1]
    nb = half // tm

    def kern(c_ref, a_ref, b_ref, o_ref):
        o_ref[...] = (a_ref[...] + b_ref[...]).astype(BF16)

    gs = pltpu.PrefetchScalarGridSpec(
        num_scalar_prefetch=1, grid=(4, nb),
        in_specs=[pl.BlockSpec((1, tm, D), lambda s, i, c: (s, c[0] * nb + i, 0)),
                  pl.BlockSpec((1, tm, D), lambda s, i, c: (s, i, 0))],
        out_specs=pl.BlockSpec((1, tm, D), lambda s, i, c: (s, i, 0)))
    return pl.pallas_call(kern, grid_spec=gs, out_shape=jax.ShapeDtypeStruct(other.shape, BF16),
                          name="rs_add_cores", compiler_params=_cp())(cidx, g, other)


def _exchange_chip_shards(p):
    def body(p_ref, out_ref, send_sems, recv_sems):
        x, y, c = _mesh_pos()
        me = 2 * x + y
        chips = [(1 - x, y), (x, 1 - y), (1 - x, 1 - y)]
        sends = []
        for j, (px, py) in enumerate(chips):
            cp = pltpu.make_async_remote_copy(
                src_ref=p_ref.at[2 * px + py], dst_ref=out_ref.at[me],
                send_sem=send_sems.at[j], recv_sem=recv_sems.at[j], device_id=(px, py, c), device_id_type=MESH)
            cp.start()
            sends.append(cp)
        for j, (px, py) in enumerate(chips):
            pltpu.make_async_remote_copy(
                src_ref=p_ref.at[me], dst_ref=out_ref.at[2 * px + py],
                send_sem=send_sems.at[j], recv_sem=recv_sems.at[j], device_id=(px, py, c),
                device_id_type=MESH).wait_recv()
        for cp in sends:
            cp.wait_send()

    return pl.pallas_call(
        body, out_shape=jax.ShapeDtypeStruct(p.shape, p.dtype), in_specs=[ANY], out_specs=ANY,
        scratch_shapes=[pltpu.SemaphoreType.DMA((3,)), pltpu.SemaphoreType.DMA((3,))],
        name="rs_exchange_chips")(p)


def _sum_slots(slots, p, pos, tm=544):
    half = slots.shape[1]
    nb = half // tm

    def kern(pos_ref, p_ref, s1_ref, s2_ref, s3_ref, o_ref):
        o_ref[...] = ((p_ref[0].astype(F32) + s1_ref[0].astype(F32)) + s2_ref[0].astype(F32)) + s3_ref[0].astype(F32)

    def slot(k):
        return pl.BlockSpec((1, tm, D), lambda i, pos: ((pos[0] + k) % 4, i, 0))

    gs = pltpu.PrefetchScalarGridSpec(
        num_scalar_prefetch=1, grid=(nb,), in_specs=[slot(0), slot(1), slot(2), slot(3)],
        out_specs=pl.BlockSpec((tm, D), lambda i, pos: (pos[1] * nb + i, 0)))
    return pl.pallas_call(kern, grid_spec=gs, out_shape=jax.ShapeDtypeStruct((2 * half, D), F32),
                          name="rs_sum_chips", compiler_params=_cp())(pos, p, slots, slots, slots)


def _join_core_halves(r):
    half = r.shape[0] // 2

    def body(r_ref, out_ref, send_sem, recv_sem):
        x, y, c = _mesh_pos()
        mine = out_ref.at[pl.ds(c * half, half), :]
        cp = pltpu.make_async_remote_copy(
            src_ref=mine, dst_ref=mine, send_sem=send_sem, recv_sem=recv_sem,
            device_id=(x, y, 1 - c), device_id_type=MESH)
        cp.start()
        theirs = out_ref.at[pl.ds((1 - c) * half, half), :]
        pltpu.make_async_remote_copy(
            src_ref=theirs, dst_ref=theirs, send_sem=send_sem, recv_sem=recv_sem,
            device_id=(x, y, 1 - c), device_id_type=MESH).wait_recv()
        cp.wait_send()

    return pl.pallas_call(
        body, out_shape=jax.ShapeDtypeStruct(r.shape, r.dtype), in_specs=[ANY], out_specs=ANY,
        input_output_aliases={0: 0},
        scratch_shapes=[pltpu.SemaphoreType.DMA, pltpu.SemaphoreType.DMA],
        name="rs_join_cores")(r)


def _allreduce_small(v, name):
    def body(v_ref, out_ref, gat, send_sems, recv_sems):
        x, y, c = _mesh_pos()
        me = 4 * x + 2 * y + c
        gat[me] = v_ref[...]
        sends = []
        for k in range(1, 8):
            peer = (x ^ (k >> 2), y ^ ((k >> 1) & 1), c ^ (k & 1))
            cp = pltpu.make_async_remote_copy(
                src_ref=v_ref, dst_ref=gat.at[me], send_sem=send_sems.at[k - 1], recv_sem=recv_sems.at[k - 1],
                device_id=peer, device_id_type=MESH)
            cp.start()
            sends.append(cp)
        for k in range(1, 8):
            px, py, pc = x ^ (k >> 2), y ^ ((k >> 1) & 1), c ^ (k & 1)
            pltpu.make_async_remote_copy(
                src_ref=v_ref, dst_ref=gat.at[4 * px + 2 * py + pc], send_sem=send_sems.at[k - 1],
                recv_sem=recv_sems.at[k - 1], device_id=(px, py, pc), device_id_type=MESH).wait_recv()
        for cp in sends:
            cp.wait_send()
        acc = gat[0]
        for d in range(1, 8):
            acc = acc + gat[d]
        out_ref[...] = acc

    return pl.pallas_call(
        body, out_shape=jax.ShapeDtypeStruct(v.shape, F32),
        in_specs=[pl.BlockSpec(memory_space=pltpu.VMEM)], out_specs=pl.BlockSpec(memory_space=pltpu.VMEM),
        scratch_shapes=[pltpu.VMEM((8,) + v.shape, F32), pltpu.SemaphoreType.DMA((7,)), pltpu.SemaphoreType.DMA((7,))],
        name=name)(v)


def _mlp_fwd(xb, w_up, w_down, tag):
    u, a = _mm(xb, w_up, "nn", f"mlp_up_{tag}", relu2=True)
    return u, a, _mm(a, w_down, "nn", f"mlp_down_{tag}")


def _mlp_bwd(dz, dzb, xb, u, a, w_up, w_down, tag):
    du = _mm(dzb, w_down, "nt", f"mlp_down_dx_{tag}", out_dtype=BF16, gate_u=u)
    dw_down = _mm(a, dzb, "tn", f"mlp_down_dw_{tag}")
    dw_up = _mm(xb, du, "tn", f"mlp_up_dw_{tag}")
    dx = _mm(du, w_up, "nt", f"mlp_up_dx_{tag}", addend=dz, add_scale=ALPHA)
    return dx, dw_up, dw_down


def _fwd_bwd(x, target, w, bq=512, qb=4, hb=4):
    t = x.shape[0]
    bq = min(bq, t)
    qb = min(qb, t // WIN)
    cos, sin = _rope_tables(t)
    bkt = jnp.asarray(_bucket_table())
    w_in = jnp.pad(w["mla_w_in"], ((0, 0), (0, HW - w["mla_w_in"].shape[1])))
    wuq = w["mla_w_uq"]
    wq2 = jnp.concatenate([wuq[:, :, :NOPE].reshape(QR, H * NOPE),
                           jnp.pad(wuq[:, :, NOPE:], ((0, 0), (0, 0), (0, RP - ROPE))).reshape(QR, H * RP)], axis=1)
    wuk_t = w["mla_w_uk"].transpose(1, 2, 0)
    wuk_h = w["mla_w_uk"].transpose(1, 0, 2)
    wuv_h = w["mla_w_uv"].transpose(1, 0, 2)
    w_o = w["mla_w_o"]
    wqkv = jnp.concatenate([w["swa_w_q"], w["kv_w_shared"]], axis=1)
    wqkv_t = wqkv.T
    wo_s = w["swa_w_o"]
    sinks = w["swa_sinks"].reshape(QH)
    lnp = lambda n, l: w[n][l]

    hh = _mm(x, w_in, "nn", "mla_in")
    cq, kc = _mla_pre(hh, w["mla_g_q"], w["mla_g_kv"], cos, sin)
    q2 = _mm(cq, wq2, "nn", "mla_uq")
    qcat = _q_prep(q2, wuk_t, cos, sin)
    o_lat, lse0, lse0_t = _flash_fwd(qcat, kc, bq, hb)
    o0 = _o_up(o_lat, wuv_h)
    y0 = _mm(o0, w_o, "nn", "mla_out")
    x1, x1b, xh1, r1 = _add_ln(x, y0, lnp("ln_mix_g", 0), lnp("ln_mix_b", 0), "ln_mix_0")
    u0, a0, f0 = _mlp_fwd(x1b, w["mlp_w_up"][0], w["mlp_w_down"][0], 0)
    x2, x2b, xh2, r2 = _add_ln(x1, f0, lnp("ln_mlp_g", 0), lnp("ln_mlp_b", 0), "ln_mlp_0")
    bias = _bias_build(w["rel_bias"], bkt)
    qkv_t = _mm(x2b, wqkv, "nn", "swa_qkv", out_dtype=BF16, out_t=True)
    os_t, lse1 = _swa_fwd(qkv_t, bias, sinks, qb)
    y1 = _mm(os_t, wo_s, "tn", "swa_out")
    x3, x3b, xh3, r3 = _add_ln(x2, y1, lnp("ln_mix_g", 1), lnp("ln_mix_b", 1), "ln_mix_1")
    u1, a1, f1 = _mlp_fwd(x3b, w["mlp_w_up"][1], w["mlp_w_down"][1], 1)
    x4, _, xh4, r4 = _add_ln(x3, f1, lnp("ln_mlp_g", 1), lnp("ln_mlp_b", 1), "ln_mlp_1")
    dx4, lpart = _loss_grad(x4, target)

    g = {}
    dz4, dz4b, dg_mlp1, db_mlp1 = _ln_bwd(dx4, xh4, r4, lnp("ln_mlp_g", 1), "ln_mlp_1_bwd")
    dx3, dwu1, dwd1 = _mlp_bwd(dz4, dz4b, x3b, u1, a1, w["mlp_w_up"][1], w["mlp_w_down"][1], 1)
    dz3, dz3b, dg_mix1, db_mix1 = _ln_bwd(dx3, xh3, r3, lnp("ln_mix_g", 1), "ln_mix_1_bwd")
    dos_t = _mm(dz3b, wo_s, "nt", "swa_out_dx", out_t=True)
    g["swa_w_o"] = _mm(os_t, dz3b, "nn", "swa_out_dw")
    dqkv_t, dbias, dsk = _swa_bwd(qkv_t, dos_t, os_t, lse1, bias, sinks, qb)
    dwqkv = _mm(dqkv_t, x2b, "nn", "swa_qkv_dw").T
    g["swa_w_q"], g["kv_w_shared"] = dwqkv[:, :QH * HD], dwqkv[:, QH * HD:]
    dx2 = _mm(dqkv_t, wqkv_t, "tn", "swa_qkv_dx", addend=dz3, add_scale=ALPHA)
    g["rel_bias"] = jnp.sum(_bias_bwd(dbias, bkt), axis=-1).reshape(NBKT, QH)
    g["swa_sinks"] = jnp.sum(dsk, axis=-1).reshape(1, QH)
    dz2, dz2b, dg_mlp0, db_mlp0 = _ln_bwd(dx2, xh2, r2, lnp("ln_mlp_g", 0), "ln_mlp_0_bwd")
    dx1, dwu0, dwd0 = _mlp_bwd(dz2, dz2b, x1b, u0, a0, w["mlp_w_up"][0], w["mlp_w_down"][0], 0)
    dz1, dz1b, dg_mix0, db_mix0 = _ln_bwd(dx1, xh1, r1, lnp("ln_mix_g", 0), "ln_mix_0_bwd")
    do0 = _mm(dz1b, w_o, "nt", "mla_out_dx", out_dtype=BF16)
    g["mla_w_o"] = _mm(o0, dz1b, "tn", "mla_out_dw")
    do_lat, dwuv = _o_up_bwd(do0, o_lat, wuv_h)
    g["mla_w_uv"] = dwuv.transpose(1, 0, 2)
    dq_cat, delta_t = _flash_dq(qcat, kc, do_lat, o_lat, lse0, bq, hb)
    dk = _flash_dkv(qcat, kc, do_lat, lse0_t, delta_t, bq, hb)
    dq2, dwuk = _q_prep_bwd(dq_cat, q2, wuk_h, cos, sin)
    g["mla_w_uk"] = dwuk.transpose(2, 0, 1)
    dcq = _mm(dq2, wq2, "nt", "mla_uq_dx")
    dwq2 = _mm(cq, dq2, "tn", "mla_uq_dw")
    g["mla_w_uq"] = jnp.concatenate([dwq2[:, :H * NOPE].reshape(QR, H, NOPE),
                                     dwq2[:, H * NOPE:].reshape(QR, H, RP)[:, :, :ROPE]], axis=2)
    dh, dgq, dgkv = _mla_pre_bwd(hh, dcq, dk, w["mla_g_q"], w["mla_g_kv"], cos, sin)
    g["mla_w_in"] = _mm(x, dh, "tn", "mla_in_dw")[:, :QR + KVR + ROPE]
    grad_x = _mm(dh, w_in, "nt", "mla_in_dx", addend=dz1, add_scale=ALPHA)
    g["mla_g_q"], g["mla_g_kv"] = dgq, dgkv
    g["mlp_w_up"] = jnp.stack([dwu0, dwu1])
    g["mlp_w_down"] = jnp.stack([dwd0, dwd1])
    g["ln_mix_g"] = jnp.concatenate([dg_mix0, dg_mix1], axis=0)
    g["ln_mix_b"] = jnp.concatenate([db_mix0, db_mix1], axis=0)
    g["ln_mlp_g"] = jnp.concatenate([dg_mlp0, dg_mlp1], axis=0)
    g["ln_mlp_b"] = jnp.concatenate([db_mlp0, db_mlp1], axis=0)
    return lpart, grad_x, g


def _rows(a):
    return a.reshape(-1, D)


def _pack_shards(parts):
    return jnp.concatenate([_rows(parts[n]) for n, _ in PACK], axis=0)


def _unpack_shards(buf, like):
    out, off = {}, 0
    for n, r in PACK:
        out[n] = buf[off:off + r].reshape(like[n].shape)
        off += r
    return out


def _full_from_gathered(wall, shard_shapes):
    out, off = {}, 0
    for n, r in PACK:
        sl = wall[:, off:off + r]
        off += r
        shp = shard_shapes[n]
        if n == "mlp_w_up":
            out[n] = sl.reshape(4, 2, D, D).transpose(1, 2, 0, 3).reshape(2, D, DFF)
        elif n == "mlp_w_down":
            out[n] = sl.reshape(4, 2, D, D).transpose(1, 0, 2, 3).reshape(2, DFF, D)
        elif n == "kv_w_shared":
            out[n] = sl.reshape((4 * shp[0],) + tuple(shp[1:]))
        else:
            out[n] = sl.reshape((4 * shp[1],) + tuple(shp[2:]))
    return out


def _shards_from_full(grads):
    cols = []
    for n, r in PACK:
        gfull = grads[n]
        if n == "mlp_w_up":
            cols.append(gfull.reshape(2, D, 4, D).transpose(2, 0, 1, 3).reshape(4, r, D))
        elif n == "mlp_w_down":
            cols.append(gfull.reshape(2, 4, D, D).transpose(1, 0, 2, 3).reshape(4, r, D))
        else:
            cols.append(gfull.reshape(4, r, D))
    return jnp.concatenate(cols, axis=1)


SMALL = (("ln_mix_g", 0, 2), ("ln_mix_b", 2, 2), ("ln_mlp_g", 4, 2), ("ln_mlp_b", 6, 2),
         ("swa_sinks", 8, 1), ("mla_g_q", 9, 1), ("mla_g_kv", 10, 1), ("rel_bias", 11, 1))


def _pack_small(parts):
    rows = []
    for n, _, nr in SMALL:
        a = parts[n].reshape(nr, -1).astype(F32)
        rows.append(jnp.pad(a, ((0, 0), (0, D - a.shape[1]))))
    rows.append(jnp.zeros((SMALL_ROWS - 12, D), F32))
    return jnp.concatenate(rows, axis=0)


def _unpack_small(buf, like):
    out = {}
    for n, r0, nr in SMALL:
        size = like[n].size // nr
        out[n] = buf[r0:r0 + nr, :size].reshape(like[n].shape)
    return out


def kernel(x, mla_w_in, mla_g_q, mla_g_kv, mla_w_uq, mla_w_uk, mla_w_uv, mla_w_o, kv_w_shared, swa_w_q, swa_sinks, swa_w_o, rel_bias, mlp_w_up, mlp_w_down, ln_mix_g, ln_mix_b, ln_mlp_g, ln_mlp_b, loss_target, m_mla_w_in, m_mla_g_q, m_mla_g_kv, m_mla_w_uq, m_mla_w_uk, m_mla_w_uv, m_mla_w_o, m_kv_w_shared, m_swa_w_q, m_swa_sinks, m_swa_w_o, m_rel_bias, m_mlp_w_up, m_mlp_w_down, m_ln_mix_g, m_ln_mix_b, m_ln_mlp_g, m_ln_mlp_b, v_mla_w_in, v_mla_g_q, v_mla_g_kv, v_mla_w_uq, v_mla_w_uk, v_mla_w_uv, v_mla_w_o, v_kv_w_shared, v_swa_w_q, v_swa_sinks, v_swa_w_o, v_rel_bias, v_mlp_w_up, v_mlp_w_down, v_ln_mix_g, v_ln_mix_b, v_ln_mlp_g, v_ln_mlp_b):
    names = ["mla_w_in", "mla_g_q", "mla_g_kv", "mla_w_uq", "mla_w_uk", "mla_w_uv", "mla_w_o", "kv_w_shared",
             "swa_w_q", "swa_sinks", "swa_w_o", "rel_bias", "mlp_w_up", "mlp_w_down",
             "ln_mix_g", "ln_mix_b", "ln_mlp_g", "ln_mlp_b"]
    ws = dict(zip(names, [mla_w_in, mla_g_q, mla_g_kv, mla_w_uq, mla_w_uk, mla_w_uv, mla_w_o, kv_w_shared,
                          swa_w_q, swa_sinks, swa_w_o, rel_bias, mlp_w_up, mlp_w_down,
                          ln_mix_g, ln_mix_b, ln_mlp_g, ln_mlp_b]))
    ms = dict(zip(names, [m_mla_w_in, m_mla_g_q, m_mla_g_kv, m_mla_w_uq, m_mla_w_uk, m_mla_w_uv, m_mla_w_o,
                          m_kv_w_shared, m_swa_w_q, m_swa_sinks, m_swa_w_o, m_rel_bias, m_mlp_w_up, m_mlp_w_down,
                          m_ln_mix_g, m_ln_mix_b, m_ln_mlp_g, m_ln_mlp_b]))
    vs = dict(zip(names, [v_mla_w_in, v_mla_g_q, v_mla_g_kv, v_mla_w_uq, v_mla_w_uk, v_mla_w_uv, v_mla_w_o,
                          v_kv_w_shared, v_swa_w_q, v_swa_sinks, v_swa_w_o, v_rel_bias, v_mlp_w_up, v_mlp_w_down,
                          v_ln_mix_g, v_ln_mix_b, v_ln_mlp_g, v_ln_mlp_b]))
    xi, yi, ci = _mesh_pos()
    shard = 2 * xi + yi
    big = [n for n, _ in PACK]
    shard_shapes = {n: ws[n].shape for n in big}

    wpack = _pack_shards({n: ws[n].astype(BF16) for n in big})
    wall = lax.dynamic_update_slice(_allgather_weights(wpack), wpack[None], (shard, 0, 0))
    w = _full_from_gathered(wall, shard_shapes)
    gq_slot = lax.dynamic_update_slice(jnp.zeros((1, QR), F32), mla_g_q, (0, shard * (QR // 4)))
    gkv_slot = lax.dynamic_update_slice(jnp.zeros((1, KVR), F32), mla_g_kv, (0, shard * (KVR // 4)))
    gains = jnp.concatenate([jnp.pad(gq_slot, ((0, 0), (0, D - QR))), jnp.pad(gkv_slot, ((0, 0), (0, D - KVR))),
                             jnp.zeros((SMALL_ROWS - 2, D), F32)], axis=0)
    gains = _allreduce_small(gains * 0.5, "allgather_gains")
    w["mla_g_q"], w["mla_g_kv"] = gains[0, :QR], gains[1, :KVR]
    for n in ("swa_sinks", "rel_bias", "ln_mix_g", "ln_mix_b", "ln_mlp_g", "ln_mlp_b"):
        w[n] = ws[n]

    lpart, grad_x, g = _fwd_bwd(x[0], loss_target[0], w)
    loss = lax.psum(0.5 * jnp.sum(lpart) / D, ("x", "y", "c"))

    gsh = _shards_from_full(g)
    other = _exchange_core_halves(gsh)
    cidx = jnp.reshape(ci, (1,)).astype(jnp.int32)
    chip_part = _add_core_halves(gsh, other, cidx)
    slots = _exchange_chip_shards(chip_part)
    pos = jnp.stack([shard, ci]).astype(jnp.int32)
    gred = _join_core_halves(_sum_slots(slots, chip_part, pos))
    gbig = _unpack_shards(gred, ws)

    small_like = {n: g[n] for n, _, _ in SMALL}
    gsm = _unpack_small(_allreduce_small(_pack_small(g), "allreduce_small_grads"), small_like)
    gsm["mla_g_q"] = lax.dynamic_slice(gsm["mla_g_q"], (0, shard * (QR // 4)), (1, QR // 4))
    gsm["mla_g_kv"] = lax.dynamic_slice(gsm["mla_g_kv"], (0, shard * (KVR // 4)), (1, KVR // 4))
    grads = {**gbig, **gsm}

    dbig, mbig, vbig = _adamw(_pack_shards({n: ws[n] for n in big}), gred,
                              _pack_shards({n: ms[n] for n in big}), _pack_shards({n: vs[n] for n in big}),
                              "adamw_big")
    dsm, msm, vsm = _adamw(_pack_small(ws), _pack_small(gsm), _pack_small(ms), _pack_small(vs), "adamw_small", tm=16)
    delta = {**_unpack_shards(dbig, ws), **_unpack_small(dsm, ws)}
    new_m = {**_unpack_shards(mbig, ws), **_unpack_small(msm, ws)}
    new_v = {**_unpack_shards(vbig, ws), **_unpack_small(vsm, ws)}
    grads = {n: grads[n].reshape(ws[n].shape) for n in names}
    return (loss, grad_x[None], *[grads[n] for n in names], *[delta[n] for n in names],
            *[new_m[n] for n in names], *[new_v[n] for n in names])
```

```python
import functools
import math

import numpy as np
import jax
import jax.numpy as jnp
from jax import lax
from jax.experimental import pallas as pl
from jax.experimental.pallas import tpu as pltpu

F32 = jnp.float32
BF16 = jnp.bfloat16
MESH = pl.DeviceIdType.MESH

D = 1024
DFF = 4096
H = 8
NOPE = 128
ROPE = 64
QR = 384
KVR = 256
RP = 128
KD = KVR + RP
HW = 768
QH = 16
KVH = 4
HD = 64
G = QH // KVH
WIN = 128
NBKT = 32
ALPHA = 4.0 ** 0.25
LN_EPS = 1e-5
RMS_EPS = 1e-6
MLA_SCALE = (NOPE + ROPE) ** -0.5
LOG2E = 1.4426950408889634
LN2 = 0.6931471805599453
QSCALE = MLA_SCALE * LOG2E
SWA_SCALE = HD ** -0.5
NEG = -1e30
LR, B1, B2, ADAM_EPS, WD, STEP = 0.001, 0.9, 0.999, 1e-8, 0.01, 10

VMEM_LIMIT = 48 * 1024 * 1024

NN = (((1,), (0,)), ((), ()))
NT = (((1,), (1,)), ((), ()))
TN = (((0,), (0,)), ((), ()))

PACK = (("mlp_w_up", 2048), ("mlp_w_down", 2048), ("mla_w_o", 256), ("swa_w_q", 256), ("swa_w_o", 256),
        ("kv_w_shared", 128), ("mla_w_in", 176), ("mla_w_uq", 144), ("mla_w_uk", 64), ("mla_w_uv", 64))
PACK_ROWS = sum(r for _, r in PACK)
HALF_ROWS = PACK_ROWS // 2
SMALL_ROWS = 16


def _cp(**kw):
    return pltpu.CompilerParams(vmem_limit_bytes=VMEM_LIMIT, **kw)


def _tile(n, pref):
    t = min(n, pref)
    while n % t:
        t -= 128
    return t


def _dot(a, b, dims):
    return lax.dot_general(a, b, dims, preferred_element_type=F32)


def _mm(a, b, mode, name, out_dtype=F32, out_t=False, addend=None, add_scale=1.0, relu2=False, gate_u=None,
        tm=1024, tn=1024, tk=1024):
    if mode == "nn":
        (m, k), (k2, n) = a.shape, b.shape
    elif mode == "nt":
        (m, k), (n, k2) = a.shape, b.shape
    else:
        (k, m), (k2, n) = a.shape, b.shape
    assert k == k2, (name, a.shape, b.shape)
    tm, tn, tk = _tile(m, tm), _tile(n, tn), _tile(k, tk)
    nk = k // tk
    dims = {"nn": NN, "nt": NT, "tn": TN}[mode]
    if mode == "tn":
        a_spec = pl.BlockSpec((tk, tm), lambda i, j, kk: (kk, i))
    else:
        a_spec = pl.BlockSpec((tm, tk), lambda i, j, kk: (i, kk))
    if mode == "nt":
        b_spec = pl.BlockSpec((tn, tk), lambda i, j, kk: (j, kk))
    else:
        b_spec = pl.BlockSpec((tk, tn), lambda i, j, kk: (kk, j))
    mn_spec = pl.BlockSpec((tm, tn), lambda i, j, kk: (i, j))
    ins, in_specs = [a, b], [a_spec, b_spec]
    if addend is not None:
        ins.append(addend)
        in_specs.append(mn_spec)
    if gate_u is not None:
        ins.append(gate_u)
        in_specs.append(mn_spec)
    if out_t:
        out_shape = [jax.ShapeDtypeStruct((n, m), out_dtype)]
        out_specs = [pl.BlockSpec((tn, tm), lambda i, j, kk: (j, i))]
    else:
        out_shape = [jax.ShapeDtypeStruct((m, n), out_dtype)]
        out_specs = [mn_spec]
    if relu2:
        out_shape.append(jax.ShapeDtypeStruct((m, n), BF16))
        out_specs.append(mn_spec)
    has_add, has_gate = addend is not None, gate_u is not None

    def kern(*refs):
        a_ref, b_ref = refs[0], refs[1]
        pos = 2
        add_ref = gate_ref = None
        if has_add:
            add_ref = refs[pos]
            pos += 1
        if has_gate:
            gate_ref = refs[pos]
            pos += 1
        o_ref = refs[pos]
        a2_ref = refs[pos + 1] if relu2 else None
        acc = refs[-1] if nk > 1 else None
        kk = pl.program_id(2)

        def partial():
            return _dot(a_ref[...].astype(BF16), b_ref[...].astype(BF16), dims)

        if nk > 1:
            @pl.when(kk == 0)
            def _():
                acc[...] = partial()

            @pl.when((kk > 0) & (kk < nk - 1))
            def _():
                acc[...] += partial()

        @pl.when(kk == nk - 1)
        def _():
            r = partial() + acc[...] if nk > 1 else partial()
            if has_add:
                r = r + add_scale * add_ref[...].astype(F32)
            if has_gate:
                r = r * (2.0 * jnp.maximum(gate_ref[...], 0.0))
            if relu2:
                hh = jnp.maximum(r, 0.0)
                a2_ref[...] = (hh * hh).astype(BF16)
            if out_t:
                r = r.T
            o_ref[...] = r.astype(out_dtype)

    outs = pl.pallas_call(
        kern, out_shape=out_shape, grid=(m // tm, n // tn, nk), in_specs=in_specs, out_specs=out_specs,
        scratch_shapes=[pltpu.VMEM((tm, tn), F32)] if nk > 1 else [], name=name, compiler_params=_cp())(*ins)
    return outs if relu2 else outs[0]


def _add_ln(xres, y, g, b, name, tm=256):
    t = xres.shape[0]
    tm = min(tm, t)

    def kern(x_ref, y_ref, g_ref, b_ref, o_ref, ob_ref, xh_ref, r_ref):
        z = ALPHA * x_ref[...] + y_ref[...]
        mu = jnp.mean(z, axis=-1, keepdims=True)
        zc = z - mu
        var = jnp.mean(zc * zc, axis=-1, keepdims=True)
        r = lax.rsqrt(var + LN_EPS)
        xh = zc * r
        o = xh * g_ref[...] + b_ref[...]
        o_ref[...] = o
        ob_ref[...] = o.astype(BF16)
        xh_ref[...] = xh
        r_ref[...] = r

    row = pl.BlockSpec((tm, D), lambda i: (i, 0))
    vec = pl.BlockSpec((1, D), lambda i: (0, 0))
    st = pl.BlockSpec((tm, 1), lambda i: (i, 0))
    return pl.pallas_call(
        kern, grid=(t // tm,), in_specs=[row, row, vec, vec], out_specs=[row, row, row, st],
        out_shape=[jax.ShapeDtypeStruct((t, D), F32), jax.ShapeDtypeStruct((t, D), BF16),
                   jax.ShapeDtypeStruct((t, D), F32), jax.ShapeDtypeStruct((t, 1), F32)],
        name=name, compiler_params=_cp())(xres, y, g.reshape(1, D), b.reshape(1, D))


def _ln_bwd(dout, xhat, rstd, g, name, tm=256):
    t = dout.shape[0]
    tm = min(tm, t)

    def kern(do_ref, xh_ref, r_ref, g_ref, dz_ref, dzb_ref, dg_ref, db_ref):
        @pl.when(pl.program_id(0) == 0)
        def _():
            dg_ref[...] = jnp.zeros_like(dg_ref)
            db_ref[...] = jnp.zeros_like(db_ref)

        do = do_ref[...]
        xh = xh_ref[...]
        dxh = do * g_ref[...]
        m1 = jnp.mean(dxh, axis=-1, keepdims=True)
        m2 = jnp.mean(dxh * xh, axis=-1, keepdims=True)
        dz = r_ref[...] * (dxh - m1 - xh * m2)
        dz_ref[...] = dz
        dzb_ref[...] = dz.astype(BF16)
        dg_ref[...] += jnp.sum(do * xh, axis=0, keepdims=True)
        db_ref[...] += jnp.sum(do, axis=0, keepdims=True)

    row = pl.BlockSpec((tm, D), lambda i: (i, 0))
    vec = pl.BlockSpec((1, D), lambda i: (0, 0))
    st = pl.BlockSpec((tm, 1), lambda i: (i, 0))
    return pl.pallas_call(
        kern, grid=(t // tm,), in_specs=[row, row, st, vec], out_specs=[row, row, vec, vec],
        out_shape=[jax.ShapeDtypeStruct((t, D), F32), jax.ShapeDtypeStruct((t, D), BF16),
                   jax.ShapeDtypeStruct((1, D), F32), jax.ShapeDtypeStruct((1, D), F32)],
        name=name, compiler_params=_cp())(dout, xhat, rstd, g.reshape(1, D))


def _loss_grad(y, target, name="loss_grad", tm=256):
    t = y.shape[0]
    tm = min(tm, t)

    def kern(y_ref, t_ref, d_ref, l_ref):
        @pl.when(pl.program_id(0) == 0)
        def _():
            l_ref[...] = jnp.zeros_like(l_ref)

        e = y_ref[...] - t_ref[...]
        d_ref[...] = e * (1.0 / D)
        l_ref[...] += jnp.sum(e * e, axis=0, keepdims=True)

    row = pl.BlockSpec((tm, D), lambda i: (i, 0))
    vec = pl.BlockSpec((1, D), lambda i: (0, 0))
    return pl.pallas_call(
        kern, grid=(t // tm,), in_specs=[row, row], out_specs=[row, vec],
        out_shape=[jax.ShapeDtypeStruct((t, D), F32), jax.ShapeDtypeStruct((1, D), F32)],
        name=name, compiler_params=_cp())(y, target)


def _rope_tables(t):
    half = ROPE // 2
    inv = 10000.0 ** (-jnp.arange(half, dtype=F32) / half)
    ang = jnp.arange(t).astype(F32)[:, None] * inv[None, :]
    cos, sin = jnp.cos(ang), jnp.sin(ang)
    z = jnp.zeros((t, RP - ROPE), F32)
    return jnp.concatenate([cos, cos, z], axis=1), jnp.concatenate([-sin, sin, z], axis=1)


def _swap_halves(x):
    lane = lax.broadcasted_iota(jnp.int32, x.shape, 1)
    return jnp.where(lane < ROPE // 2, pltpu.roll(x, RP - ROPE // 2, 1), pltpu.roll(x, ROPE // 2, 1))


def _rope(x, cos, sin):
    return x * cos + _swap_halves(x) * sin


def _rope_t(gy, cos, sin):
    return gy * cos + _swap_halves(gy * sin)


def _mla_pre(hh, g_q, g_kv, cos, sin, tm=256):
    t = hh.shape[0]
    tm = min(tm, t)

    def kern(h_ref, gq_ref, gkv_ref, c_ref, s_ref, cq_ref, k_ref):
        xq = h_ref[:, 0:QR]
        rq = lax.rsqrt(jnp.mean(xq * xq, axis=-1, keepdims=True) + RMS_EPS)
        cq_ref[...] = (xq * rq * gq_ref[...]).astype(BF16)
        xk = h_ref[:, QR:QR + KVR]
        rk = lax.rsqrt(jnp.mean(xk * xk, axis=-1, keepdims=True) + RMS_EPS)
        k_ref[:, 0:KVR] = (xk * rk * gkv_ref[...]).astype(BF16)
        k_ref[:, KVR:KD] = _rope(h_ref[:, QR + KVR:HW], c_ref[...], s_ref[...]).astype(BF16)

    return pl.pallas_call(
        kern, grid=(t // tm,),
        in_specs=[pl.BlockSpec((tm, HW), lambda i: (i, 0)), pl.BlockSpec((1, QR), lambda i: (0, 0)),
                  pl.BlockSpec((1, KVR), lambda i: (0, 0)), pl.BlockSpec((tm, RP), lambda i: (i, 0)),
                  pl.BlockSpec((tm, RP), lambda i: (i, 0))],
        out_specs=[pl.BlockSpec((tm, QR), lambda i: (i, 0)), pl.BlockSpec((tm, KD), lambda i: (i, 0))],
        out_shape=[jax.ShapeDtypeStruct((t, QR), BF16), jax.ShapeDtypeStruct((t, KD), BF16)],
        name="mla_pre", compiler_params=_cp())(hh, g_q.reshape(1, QR), g_kv.reshape(1, KVR), cos, sin)


def _mla_pre_bwd(hh, dcq, dk, g_q, g_kv, cos, sin, tm=256):
    t = hh.shape[0]
    tm = min(tm, t)

    def rms_bwd(x, dy, g):
        r = lax.rsqrt(jnp.mean(x * x, axis=-1, keepdims=True) + RMS_EPS)
        gdy = dy * g
        dx = r * gdy - x * (r * r * r) * jnp.mean(gdy * x, axis=-1, keepdims=True)
        return dx, jnp.sum(dy * x * r, axis=0, keepdims=True)

    def kern(h_ref, dcq_ref, dk_ref, gq_ref, gkv_ref, c_ref, s_ref, dh_ref, dgq_ref, dgkv_ref):
        @pl.when(pl.program_id(0) == 0)
        def _():
            dgq_ref[...] = jnp.zeros_like(dgq_ref)
            dgkv_ref[...] = jnp.zeros_like(dgkv_ref)

        dxq, dgq = rms_bwd(h_ref[:, 0:QR], dcq_ref[...], gq_ref[...])
        dxk, dgk = rms_bwd(h_ref[:, QR:QR + KVR], dk_ref[:, 0:KVR], gkv_ref[...])
        dh_ref[:, 0:QR] = dxq.astype(BF16)
        dh_ref[:, QR:QR + KVR] = dxk.astype(BF16)
        dh_ref[:, QR + KVR:HW] = _rope_t(dk_ref[:, KVR:KD], c_ref[...], s_ref[...]).astype(BF16)
        dgq_ref[...] += dgq
        dgkv_ref[...] += dgk

    return pl.pallas_call(
        kern, grid=(t // tm,),
        in_specs=[pl.BlockSpec((tm, HW), lambda i: (i, 0)), pl.BlockSpec((tm, QR), lambda i: (i, 0)),
                  pl.BlockSpec((tm, KD), lambda i: (i, 0)), pl.BlockSpec((1, QR), lambda i: (0, 0)),
                  pl.BlockSpec((1, KVR), lambda i: (0, 0)), pl.BlockSpec((tm, RP), lambda i: (i, 0)),
                  pl.BlockSpec((tm, RP), lambda i: (i, 0))],
        out_specs=[pl.BlockSpec((tm, HW), lambda i: (i, 0)), pl.BlockSpec((1, QR), lambda i: (0, 0)),
                   pl.BlockSpec((1, KVR), lambda i: (0, 0))],
        out_shape=[jax.ShapeDtypeStruct((t, HW), BF16), jax.ShapeDtypeStruct((1, QR), F32),
                   jax.ShapeDtypeStruct((1, KVR), F32)],
        name="mla_pre_bwd", compiler_params=_cp())(hh, dcq, dk, g_q.reshape(1, QR), g_kv.reshape(1, KVR), cos, sin)


def _q_prep(q2, wuk_t, cos, sin, tm=256):
    t = q2.shape[0]
    tm = min(tm, t)

    def kern(q_ref, w_ref, c_ref, s_ref, o_ref):
        cos_, sin_ = c_ref[...], s_ref[...]
        for h in range(H):
            qn = q_ref[:, h * NOPE:(h + 1) * NOPE].astype(BF16)
            o_ref[:, h * KD:h * KD + KVR] = (_dot(qn, w_ref[h], NN) * QSCALE).astype(BF16)
            qr = q_ref[:, H * NOPE + h * RP:H * NOPE + (h + 1) * RP]
            o_ref[:, h * KD + KVR:(h + 1) * KD] = (_rope(qr, cos_, sin_) * QSCALE).astype(BF16)

    return pl.pallas_call(
        kern, grid=(t // tm,),
        in_specs=[pl.BlockSpec((tm, 2 * H * NOPE), lambda i: (i, 0)), pl.BlockSpec((H, NOPE, KVR), lambda i: (0, 0, 0)),
                  pl.BlockSpec((tm, RP), lambda i: (i, 0)), pl.BlockSpec((tm, RP), lambda i: (i, 0))],
        out_specs=pl.BlockSpec((tm, H * KD), lambda i: (i, 0)),
        out_shape=jax.ShapeDtypeStruct((t, H * KD), BF16),
        name="q_prep", compiler_params=_cp())(q2, wuk_t, cos, sin)


def _q_prep_bwd(dq_cat, q2, wuk_h, cos, sin, tm=256):
    t = q2.shape[0]
    tm = min(tm, t)

    def kern(dq_ref, q_ref, w_ref, c_ref, s_ref, o_ref, dw_ref):
        @pl.when(pl.program_id(0) == 0)
        def _():
            dw_ref[...] = jnp.zeros_like(dw_ref)

        cos_, sin_ = c_ref[...], s_ref[...]
        for h in range(H):
            dql = dq_ref[:, h * KD:h * KD + KVR].astype(BF16)
            o_ref[:, h * NOPE:(h + 1) * NOPE] = _dot(dql, w_ref[h], NN).astype(BF16)
            dqr = dq_ref[:, h * KD + KVR:(h + 1) * KD]
            o_ref[:, H * NOPE + h * RP:H * NOPE + (h + 1) * RP] = _rope_t(dqr, cos_, sin_).astype(BF16)
            qn = q_ref[:, h * NOPE:(h + 1) * NOPE].astype(BF16)
            dw_ref[h] += _dot(qn, dql, TN)

    return pl.pallas_call(
        kern, grid=(t // tm,),
        in_specs=[pl.BlockSpec((tm, H * KD), lambda i: (i, 0)), pl.BlockSpec((tm, 2 * H * NOPE), lambda i: (i, 0)),
                  pl.BlockSpec((H, KVR, NOPE), lambda i: (0, 0, 0)),
                  pl.BlockSpec((tm, RP), lambda i: (i, 0)), pl.BlockSpec((tm, RP), lambda i: (i, 0))],
        out_specs=[pl.BlockSpec((tm, 2 * H * NOPE), lambda i: (i, 0)), pl.BlockSpec((H, NOPE, KVR), lambda i: (0, 0, 0))],
        out_shape=[jax.ShapeDtypeStruct((t, 2 * H * NOPE), BF16), jax.ShapeDtypeStruct((H, NOPE, KVR), F32)],
        name="q_prep_bwd", compiler_params=_cp())(dq_cat, q2, wuk_h, cos, sin)


def _o_up(o_lat, wuv_h, tm=256):
    t = o_lat.shape[0]
    tm = min(tm, t)

    def kern(x_ref, w_ref, o_ref):
        for h in range(H):
            xl = x_ref[:, h * KVR:(h + 1) * KVR].astype(BF16)
            o_ref[:, h * NOPE:(h + 1) * NOPE] = _dot(xl, w_ref[h], NN).astype(BF16)

    return pl.pallas_call(
        kern, grid=(t // tm,),
        in_specs=[pl.BlockSpec((tm, H * KVR), lambda i: (i, 0)), pl.BlockSpec((H, KVR, NOPE), lambda i: (0, 0, 0))],
        out_specs=pl.BlockSpec((tm, H * NOPE), lambda i: (i, 0)),
        out_shape=jax.ShapeDtypeStruct((t, H * NOPE), BF16),
        name="o_up", compiler_params=_cp())(o_lat, wuv_h)


def _o_up_bwd(do, o_lat, wuv_h, tm=256):
    t = do.shape[0]
    tm = min(tm, t)

    def kern(do_ref, x_ref, w_ref, dx_ref, dw_ref, dlt_ref):
        @pl.when(pl.program_id(0) == 0)
        def _():
            dw_ref[...] = jnp.zeros_like(dw_ref)

        for h in range(H):
            dh_ = do_ref[:, h * NOPE:(h + 1) * NOPE]
            x = x_ref[:, h * KVR:(h + 1) * KVR]
            dx = _dot(dh_, w_ref[h], NT)
            dx_ref[:, h * KVR:(h + 1) * KVR] = dx
            dw_ref[h] += _dot(x.astype(BF16), dh_, TN)
            dl = jnp.broadcast_to(jnp.sum(dx * x, axis=1)[:, None], (tm, 128))
            dlt_ref[h] = dl.T[0:1, :]

    return pl.pallas_call(
        kern, grid=(t // tm,),
        in_specs=[pl.BlockSpec((tm, H * NOPE), lambda i: (i, 0)), pl.BlockSpec((tm, H * KVR), lambda i: (i, 0)),
                  pl.BlockSpec((H, KVR, NOPE), lambda i: (0, 0, 0))],
        out_specs=[pl.BlockSpec((tm, H * KVR), lambda i: (i, 0)), pl.BlockSpec((H, KVR, NOPE), lambda i: (0, 0, 0)),
                   pl.BlockSpec((H, 1, tm), lambda i: (0, 0, i))],
        out_shape=[jax.ShapeDtypeStruct((t, H * KVR), F32), jax.ShapeDtypeStruct((H, KVR, NOPE), F32),
                   jax.ShapeDtypeStruct((H, 1, t), F32)],
        name="o_up_bwd", compiler_params=_cp())(do, o_lat, wuv_h)


def _causal_pairs(nq):
    return [(i, j) for i in range(nq) for j in range(i + 1)]


def _lane_tile(stat, width):
    return jnp.tile(stat, (1, width // 128))


def _flash_fwd(qcat, kc, bq, hb):
    t = kc.shape[0]
    nq = t // bq
    pairs = _causal_pairs(nq)
    itab = jnp.asarray(np.array([p[0] for p in pairs], np.int32))
    jtab = jnp.asarray(np.array([p[1] for p in pairs], np.int32))

    def kern(it, jt, q_ref, k_ref, o_ref, lset_ref, m_sc, l_sc, acc_sc):
        st = pl.program_id(1)
        i, j = it[st], jt[st]

        @pl.when(j == 0)
        def _():
            m_sc[...] = jnp.full_like(m_sc, NEG)
            l_sc[...] = jnp.zeros_like(l_sc)
            acc_sc[...] = jnp.zeros_like(acc_sc)

        def update(masked):
            k = k_ref[...]
            v = k[:, 0:KVR]
            if masked:
                row = lax.broadcasted_iota(jnp.int32, (bq, bq), 0)
                col = lax.broadcasted_iota(jnp.int32, (bq, bq), 1)
                keep = col <= row
            s_next = _dot(q_ref[:, 0:KD], k, NT)
            for hh in range(hb):
                s = s_next
                if hh + 1 < hb:
                    s_next = _dot(q_ref[:, (hh + 1) * KD:(hh + 2) * KD], k, NT)
                if masked:
                    s = jnp.where(keep, s, NEG)
                m_prev = m_sc[hh]
                m_next = jnp.maximum(m_prev, jnp.max(s, axis=1)[:, None])
                p = jnp.exp2(s - _lane_tile(m_next, bq))
                a = jnp.exp2(m_prev - m_next)
                l_sc[hh] = a * l_sc[hh] + jnp.sum(p, axis=1)[:, None]
                acc_sc[hh] = _lane_tile(a, KVR) * acc_sc[hh] + _dot(p.astype(BF16), v, NN)
                m_sc[hh] = m_next

        @pl.when(j < i)
        def _():
            update(False)

        @pl.when(j == i)
        def _():
            update(True)
            for hh in range(hb):
                l = l_sc[hh]
                o_ref[:, hh * KVR:(hh + 1) * KVR] = acc_sc[hh] / _lane_tile(l, KVR)
                lset_ref[hh] = (m_sc[hh] + jnp.log2(l)).T[0:1, :]

    gs = pltpu.PrefetchScalarGridSpec(
        num_scalar_prefetch=2, grid=(H // hb, len(pairs)),
        in_specs=[pl.BlockSpec((bq, hb * KD), lambda g, s, it, jt: (it[s], g)),
                  pl.BlockSpec((bq, KD), lambda g, s, it, jt: (jt[s], 0))],
        out_specs=[pl.BlockSpec((bq, hb * KVR), lambda g, s, it, jt: (it[s], g)),
                   pl.BlockSpec((hb, 1, bq), lambda g, s, it, jt: (g, 0, it[s]))],
        scratch_shapes=[pltpu.VMEM((hb, bq, 128), F32), pltpu.VMEM((hb, bq, 128), F32),
                        pltpu.VMEM((hb, bq, KVR), F32)])
    return pl.pallas_call(
        kern, grid_spec=gs,
        out_shape=[jax.ShapeDtypeStruct((t, H * KVR), F32), jax.ShapeDtypeStruct((H, 1, t), F32)],
        name="mla_flash_fwd", compiler_params=_cp())(itab, jtab, qcat, kc)


def _flash_dkv(qcat, kc, do_lat, lse_t, delta_t, bq, hb):
    t = kc.shape[0]
    nq = t // bq
    ng = H // hb
    npairs = nq * (nq + 1) // 2
    steps = [(j, g, i) for j in range(nq) for g in range(ng) for i in range(j, nq)]
    jtab = jnp.asarray(np.array([s[0] for s in steps], np.int32))
    gtab = jnp.asarray(np.array([s[1] for s in steps], np.int32))
    itab = jnp.asarray(np.array([s[2] for s in steps], np.int32))
    ptab = jnp.asarray(np.array([s[2] * (s[2] + 1) // 2 + s[0] for s in steps], np.int32))

    def kern(jt, gt, it, pt, q_ref, k_ref, do_ref, lset_ref, dlt_ref, dk_ref, ds_ref, dk_sc, dv_sc):
        st = pl.program_id(0)
        j, g, i = jt[st], gt[st], it[st]

        @pl.when((g == 0) & (i == j))
        def _():
            dk_sc[...] = jnp.zeros_like(dk_sc)
            dv_sc[...] = jnp.zeros_like(dv_sc)

        def update(masked):
            k = k_ref[...]
            v = k[:, 0:KVR]
            if masked:
                row = lax.broadcasted_iota(jnp.int32, (bq, bq), 0)
                col = lax.broadcasted_iota(jnp.int32, (bq, bq), 1)
                keep = row <= col

            def first_matmuls(hh):
                dob = do_ref[:, hh * KVR:(hh + 1) * KVR].astype(BF16)
                return _dot(k, q_ref[:, hh * KD:(hh + 1) * KD], NT), _dot(v, dob, NT), dob

            nxt = first_matmuls(0)
            for hh in range(hb):
                s, dp, dob = nxt
                if hh + 1 < hb:
                    nxt = first_matmuls(hh + 1)
                if masked:
                    s = jnp.where(keep, s, NEG)
                p = jnp.exp2(s - lset_ref[hh])
                dv_sc[...] += _dot(p.astype(BF16), dob, NN)
                dsb = (p * (dp - dlt_ref[hh])).astype(BF16)
                ds_ref[0, 0, hh] = dsb
                dk_sc[...] += _dot(dsb, q_ref[:, hh * KD:(hh + 1) * KD], NN)

        @pl.when(i > j)
        def _():
            update(False)

        @pl.when(i == j)
        def _():
            update(True)

        @pl.when((g == ng - 1) & (i == nq - 1))
        def _():
            dk_ref[:, 0:KVR] = dk_sc[:, 0:KVR] * LN2 + dv_sc[...]
            dk_ref[:, KVR:KD] = dk_sc[:, KVR:KD] * LN2

    gs = pltpu.PrefetchScalarGridSpec(
        num_scalar_prefetch=4, grid=(len(steps),),
        in_specs=[pl.BlockSpec((bq, hb * KD), lambda s, jt, gt, it, pt: (it[s], gt[s])),
                  pl.BlockSpec((bq, KD), lambda s, jt, gt, it, pt: (jt[s], 0)),
                  pl.BlockSpec((bq, hb * KVR), lambda s, jt, gt, it, pt: (it[s], gt[s])),
                  pl.BlockSpec((hb, 1, bq), lambda s, jt, gt, it, pt: (gt[s], 0, it[s])),
                  pl.BlockSpec((hb, 1, bq), lambda s, jt, gt, it, pt: (gt[s], 0, it[s]))],
        out_specs=[pl.BlockSpec((bq, KD), lambda s, jt, gt, it, pt: (jt[s], 0)),
                   pl.BlockSpec((1, 1, hb, bq, bq), lambda s, jt, gt, it, pt: (gt[s], pt[s], 0, 0, 0))],
        scratch_shapes=[pltpu.VMEM((bq, KD), F32), pltpu.VMEM((bq, KVR), F32)])
    return pl.pallas_call(
        kern, grid_spec=gs,
        out_shape=[jax.ShapeDtypeStruct((t, KD), F32), jax.ShapeDtypeStruct((ng, npairs, hb, bq, bq), BF16)],
        name="mla_flash_dkv", compiler_params=_cp())(jtab, gtab, itab, ptab, qcat, kc, do_lat, lse_t, delta_t)


def _flash_dq(ds_all, kc_t, bq, hb):
    t = kc_t.shape[1]
    nq = t // bq
    pairs = _causal_pairs(nq)
    itab = jnp.asarray(np.array([p[0] for p in pairs], np.int32))
    jtab = jnp.asarray(np.array([p[1] for p in pairs], np.int32))

    def kern(it, jt, ds_ref, kt_ref, dq_ref, acc_sc):
        st = pl.program_id(1)
        i, j = it[st], jt[st]
        kt = kt_ref[...]

        @pl.when(j == 0)
        def _():
            for hh in range(hb):
                acc_sc[hh] = _dot(kt, ds_ref[0, 0, hh], NN)

        @pl.when((j > 0) & (j < i))
        def _():
            for hh in range(hb):
                acc_sc[hh] += _dot(kt, ds_ref[0, 0, hh], NN)

        @pl.when(j == i)
        def _():
            for hh in range(hb):
                tot = _dot(kt, ds_ref[0, 0, hh], NN)
                tot = jnp.where(i > 0, tot + acc_sc[hh], tot)
                dq_ref[:, hh * KD:(hh + 1) * KD] = tot.T * MLA_SCALE

    gs = pltpu.PrefetchScalarGridSpec(
        num_scalar_prefetch=2, grid=(H // hb, len(pairs)),
        in_specs=[pl.BlockSpec((1, 1, hb, bq, bq), lambda g, s, it, jt: (g, s, 0, 0, 0)),
                  pl.BlockSpec((KD, bq), lambda g, s, it, jt: (0, jt[s]))],
        out_specs=pl.BlockSpec((bq, hb * KD), lambda g, s, it, jt: (it[s], g)),
        scratch_shapes=[pltpu.VMEM((hb, KD, bq), F32)])
    return pl.pallas_call(
        kern, grid_spec=gs, out_shape=jax.ShapeDtypeStruct((t, H * KD), F32),
        name="mla_flash_dq", compiler_params=_cp())(itab, jtab, ds_all, kc_t)


def _bucket_table():
    d = np.arange(WIN)
    max_exact = NBKT // 2
    nf = np.maximum(d, 1).astype(np.float32)
    large = max_exact + (np.log(nf / np.float32(max_exact)) / np.float32(math.log(WIN / max_exact))
                         * np.float32(NBKT - max_exact)).astype(np.int32)
    large = np.minimum(large, NBKT - 1)
    bucket = np.where(d < max_exact, d, large).astype(np.int32)
    jj = np.arange(2 * WIN)[:, None]
    ii = np.arange(WIN)[None, :]
    dist = ii + WIN - jj
    valid = (dist >= 0) & (dist < WIN)
    return np.where(valid, bucket[np.clip(dist, 0, WIN - 1)], -1).astype(np.int32)


def _bias_build(rel_bias, bkt):
    def kern(bk_ref, rb_ref, o_ref):
        bk = bk_ref[...]
        for hd in range(QH):
            acc = jnp.full((2 * WIN, WIN), NEG, F32)
            for b in range(NBKT):
                acc = jnp.where(bk == b, rb_ref[b, hd], acc)
            o_ref[hd] = acc

    return pl.pallas_call(
        kern, in_specs=[pl.BlockSpec(memory_space=pltpu.VMEM), pl.BlockSpec(memory_space=pltpu.SMEM)],
        out_specs=pl.BlockSpec(memory_space=pltpu.VMEM),
        out_shape=jax.ShapeDtypeStruct((QH, 2 * WIN, WIN), F32), name="swa_bias_build")(bkt, rel_bias)


def _bias_bwd(dbias, bkt):
    def kern(db_ref, bk_ref, o_ref):
        bk = bk_ref[...]
        for hd in range(QH):
            g = db_ref[hd]
            for b in range(NBKT):
                r = b * QH + hd
                o_ref[r:r + 1, :] = jnp.sum(jnp.where(bk == b, g, 0.0), axis=0, keepdims=True)

    return pl.pallas_call(
        kern, in_specs=[pl.BlockSpec(memory_space=pltpu.VMEM), pl.BlockSpec(memory_space=pltpu.VMEM)],
        out_specs=pl.BlockSpec(memory_space=pltpu.VMEM),
        out_shape=jax.ShapeDtypeStruct((NBKT * QH, WIN), F32), name="swa_bias_bwd")(dbias, bkt)


def _swa_finish_scores(raw, bias, first):
    s = raw * SWA_SCALE + bias
    if first is not None:
        row = lax.broadcasted_iota(jnp.int32, s.shape, 0)
        s = jnp.where(jnp.logical_or(jnp.logical_not(first), row >= WIN), s, NEG)
    return s


def _swa_fwd(qkv_t, bias, sinks, qb):
    t = qkv_t.shape[1]
    w = qb * WIN
    nst = t // w

    def kern(q_ref, kc_ref, kp_ref, vc_ref, vp_ref, b_ref, sk_ref, o_ref, lse_ref):
        n = pl.program_id(0)
        kfull = jnp.concatenate([kp_ref[...], kc_ref[...]], axis=1)
        vfull = jnp.concatenate([vp_ref[...], vc_ref[...]], axis=1)
        head_row = lax.broadcasted_iota(jnp.int32, (QH, WIN), 0)
        groups = [(b, kh) for b in range(qb) for kh in range(KVH)]

        def raw_scores(b, kh):
            k_band = kfull[kh * HD:(kh + 1) * HD, b * WIN:(b + 2) * WIN]
            return [_dot(k_band, q_ref[(kh * G + g) * HD:(kh * G + g + 1) * HD, b * WIN:(b + 1) * WIN], TN)
                    for g in range(G)]

        o_rows = [[] for _ in range(qb)]
        lse_tiles = [jnp.zeros((QH, WIN), F32) for _ in range(qb)]
        nxt_scores = raw_scores(*groups[0])
        for gi, (b, kh) in enumerate(groups):
            scores = nxt_scores
            if gi + 1 < len(groups):
                nxt_scores = raw_scores(*groups[gi + 1])
            v_band = vfull[kh * HD:(kh + 1) * HD, b * WIN:(b + 2) * WIN]
            for g in range(G):
                hd = kh * G + g
                s = _swa_finish_scores(scores[g], b_ref[hd], (n == 0) if b == 0 else None)
                sink = sk_ref[hd]
                m = jnp.maximum(jnp.max(s, axis=0, keepdims=True), sink)
                p = jnp.exp(s - m)
                den = jnp.sum(p, axis=0, keepdims=True) + jnp.exp(sink - m)
                p = p / den
                o_rows[b].append(_dot(v_band, p.astype(BF16), NN))
                lse_tiles[b] = jnp.where(head_row == hd, m + jnp.log(den), lse_tiles[b])
        o_ref[...] = jnp.concatenate([jnp.concatenate(rows, axis=0) for rows in o_rows], axis=1)
        lse_ref[...] = jnp.concatenate(lse_tiles, axis=1)

    prev = lambda r: (lambda n: (r, jnp.maximum(n * qb - 1, 0)))
    return pl.pallas_call(
        kern, grid=(nst,),
        in_specs=[pl.BlockSpec((QH * HD, w), lambda n: (0, n)),
                  pl.BlockSpec((KVH * HD, w), lambda n: (4, n)), pl.BlockSpec((KVH * HD, WIN), prev(4)),
                  pl.BlockSpec((KVH * HD, w), lambda n: (5, n)), pl.BlockSpec((KVH * HD, WIN), prev(5)),
                  pl.BlockSpec((QH, 2 * WIN, WIN), lambda n: (0, 0, 0)),
                  pl.BlockSpec(memory_space=pltpu.SMEM)],
        out_specs=[pl.BlockSpec((QH * HD, w), lambda n: (0, n)), pl.BlockSpec((QH, w), lambda n: (0, n))],
        out_shape=[jax.ShapeDtypeStruct((QH * HD, t), F32), jax.ShapeDtypeStruct((QH, t), F32)],
        name="swa_fwd", compiler_params=_cp())(qkv_t, qkv_t, qkv_t, qkv_t, qkv_t, bias, sinks)


def _swa_bwd(qkv_t, do_t, o_t, lse, bias, sinks, qb):
    t = qkv_t.shape[1]
    w = qb * WIN
    nst = t // w
    nblk = t // WIN

    def kern(q_ref, kc_ref, kp_ref, vc_ref, vp_ref, do_ref, o_ref, lse_ref, qn_ref, don_ref, on_ref, lsen_ref,
             b_ref, sk_ref, dqkv_ref, db_ref, dsk_ref):
        n = pl.program_id(0)

        @pl.when(n == 0)
        def _():
            db_ref[...] = jnp.zeros_like(db_ref)
            dsk_ref[...] = jnp.zeros_like(dsk_ref)

        kfull = jnp.concatenate([kp_ref[...], kc_ref[...]], axis=1)
        vfull = jnp.concatenate([vp_ref[...], vc_ref[...]], axis=1)
        head_row = lax.broadcasted_iota(jnp.int32, (QH, WIN), 0)
        db_acc = [None] * QH
        dsk_tile = jnp.zeros((QH, WIN), F32)
        prev_part = [[[None] * qb for _ in range(KVH)] for _ in range(2)]
        cur_part = [[[None] * qb for _ in range(KVH)] for _ in range(2)]
        groups = [(b, kh) for b in range(qb) for kh in range(KVH)]

        def first_matmuls(b, kh):
            k_band = kfull[kh * HD:(kh + 1) * HD, b * WIN:(b + 2) * WIN]
            v_band = vfull[kh * HD:(kh + 1) * HD, b * WIN:(b + 2) * WIN]
            out = []
            for g in range(G):
                rs = slice((kh * G + g) * HD, (kh * G + g + 1) * HD)
                dob = do_ref[rs, b * WIN:(b + 1) * WIN].astype(BF16)
                out.append((_dot(k_band, q_ref[rs, b * WIN:(b + 1) * WIN], TN), _dot(v_band, dob, TN), dob))
            return out

        dq_rows = [[] for _ in range(qb)]
        nxt_first = first_matmuls(*groups[0])
        for gi, (b, kh) in enumerate(groups):
            first = nxt_first
            if gi + 1 < len(groups):
                nxt_first = first_matmuls(*groups[gi + 1])
            cs = slice(b * WIN, (b + 1) * WIN)
            k_band = kfull[kh * HD:(kh + 1) * HD, b * WIN:(b + 2) * WIN]
            dk_b = dv_b = None
            for g in range(G):
                hd = kh * G + g
                rs = slice(hd * HD, (hd + 1) * HD)
                raw, dp, dob = first[g]
                lse_h = lse_ref[hd:hd + 1, cs]
                s = _swa_finish_scores(raw, b_ref[hd], (n == 0) if b == 0 else None)
                p = jnp.exp(s - lse_h)
                dl = jnp.sum(do_ref[rs, cs] * o_ref[rs, cs], axis=0, keepdims=True)
                ds = p * (dp - dl)
                db_acc[hd] = ds if db_acc[hd] is None else db_acc[hd] + ds
                dsk_tile = jnp.where(head_row == hd, dsk_tile - jnp.exp(sk_ref[hd] - lse_h) * dl, dsk_tile)
                dss = (ds * SWA_SCALE).astype(BF16)
                dq_rows[b].append(_dot(k_band, dss, NN).astype(BF16))
                dk_h = _dot(q_ref[rs, cs], dss, NT)
                dv_h = _dot(dob, p.astype(BF16), NT)
                dk_b = dk_h if dk_b is None else dk_b + dk_h
                dv_b = dv_h if dv_b is None else dv_b + dv_h
            for which, val in ((0, dk_b), (1, dv_b)):
                prev_part[which][kh][b] = val[:, 0:WIN]
                cur_part[which][kh][b] = val[:, WIN:2 * WIN]
        dq_cols = [jnp.concatenate(rows, axis=0) for rows in dq_rows]

        live = n < nst - 1
        ls = slice((qb - 1) * WIN, qb * WIN)
        halo = [[None] * KVH for _ in range(2)]
        for kh in range(KVH):
            k_last = kc_ref[kh * HD:(kh + 1) * HD, ls]
            v_last = vc_ref[kh * HD:(kh + 1) * HD, ls]
            dk_b = dv_b = None
            for g in range(G):
                hd = kh * G + g
                rs = slice(hd * HD, (hd + 1) * HD)
                q_t = qn_ref[rs, :]
                do = don_ref[rs, :]
                s = _dot(k_last, q_t, TN) * SWA_SCALE + b_ref[hd, 0:WIN, :]
                p = jnp.exp(s - lsen_ref[hd:hd + 1, :])
                dob = do.astype(BF16)
                dp = _dot(v_last, dob, TN)
                dl = jnp.sum(do * on_ref[rs, :], axis=0, keepdims=True)
                dss = (p * (dp - dl) * SWA_SCALE).astype(BF16)
                dk_h = _dot(q_t, dss, NT)
                dv_h = _dot(dob, p.astype(BF16), NT)
                dk_b = dk_h if dk_b is None else dk_b + dk_h
                dv_b = dv_h if dv_b is None else dv_b + dv_h
            halo[0][kh] = jnp.where(live, dk_b, 0.0)
            halo[1][kh] = jnp.where(live, dv_b, 0.0)

        kv_rows = []
        for which in range(2):
            for kh in range(KVH):
                blocks = [cur_part[which][kh][p] + (prev_part[which][kh][p + 1] if p + 1 < qb else halo[which][kh])
                          for p in range(qb)]
                kv_rows.append(jnp.concatenate(blocks, axis=1))
        dqkv_ref[...] = jnp.concatenate(
            [jnp.concatenate(dq_cols, axis=1), jnp.concatenate(kv_rows, axis=0).astype(BF16)], axis=0)
        db_ref[...] += jnp.stack(db_acc)
        dsk_ref[...] += dsk_tile

    prev = lambda r: (lambda n: (r, jnp.maximum(n * qb - 1, 0)))
    nxt = lambda n: (0, jnp.minimum((n + 1) * qb, nblk - 1))
    big = lambda: pl.BlockSpec((QH * HD, w), lambda n: (0, n))
    return pl.pallas_call(
        kern, grid=(nst,),
        in_specs=[big(),
                  pl.BlockSpec((KVH * HD, w), lambda n: (4, n)), pl.BlockSpec((KVH * HD, WIN), prev(4)),
                  pl.BlockSpec((KVH * HD, w), lambda n: (5, n)), pl.BlockSpec((KVH * HD, WIN), prev(5)),
                  big(), big(), pl.BlockSpec((QH, w), lambda n: (0, n)),
                  pl.BlockSpec((QH * HD, WIN), nxt), pl.BlockSpec((QH * HD, WIN), nxt),
                  pl.BlockSpec((QH * HD, WIN), nxt), pl.BlockSpec((QH, WIN), nxt),
                  pl.BlockSpec((QH, 2 * WIN, WIN), lambda n: (0, 0, 0)),
                  pl.BlockSpec(memory_space=pltpu.SMEM)],
        out_specs=[pl.BlockSpec(((QH + 2 * KVH) * HD, w), lambda n: (0, n)),
                   pl.BlockSpec((QH, 2 * WIN, WIN), lambda n: (0, 0, 0)),
                   pl.BlockSpec((QH, WIN), lambda n: (0, 0))],
        out_shape=[jax.ShapeDtypeStruct(((QH + 2 * KVH) * HD, t), BF16),
                   jax.ShapeDtypeStruct((QH, 2 * WIN, WIN), F32), jax.ShapeDtypeStruct((QH, WIN), F32)],
        name="swa_bwd", compiler_params=_cp())(
            qkv_t, qkv_t, qkv_t, qkv_t, qkv_t, do_t, o_t, lse, qkv_t, do_t, o_t, lse, bias, sinks)


def _adamw(w, g, m, v, name, tm=544):
    r = w.shape[0]
    tm = r if r % tm else tm
    c1 = 1.0 / (1.0 - B1 ** STEP)
    c2 = 1.0 / (1.0 - B2 ** STEP)

    def kern(w_ref, g_ref, m_ref, v_ref, d_ref, nm_ref, nv_ref):
        g_ = g_ref[...]
        nm = B1 * m_ref[...] + (1.0 - B1) * g_
        nv = B2 * v_ref[...] + (1.0 - B2) * (g_ * g_)
        d_ref[...] = -LR * ((nm * c1) / (jnp.sqrt(nv * c2) + ADAM_EPS) + WD * w_ref[...])
        nm_ref[...] = nm
        nv_ref[...] = nv

    row = pl.BlockSpec((tm, D), lambda i: (i, 0))
    sds = jax.ShapeDtypeStruct((r, D), F32)
    return pl.pallas_call(kern, grid=(r // tm,), in_specs=[row] * 4, out_specs=[row] * 3, out_shape=[sds] * 3,
                          name=name, compiler_params=_cp())(w, g, m, v)


def _mesh_pos():
    return lax.axis_index("x"), lax.axis_index("y"), lax.axis_index("c")


ANY = pl.BlockSpec(memory_space=pl.ANY)


def _allgather_weights(wpack):
    r = wpack.shape[0]
    half = r // 2

    def body(w_ref, out_ref, send_sems, recv_sems):
        x, y, c = _mesh_pos()
        sibling = (x, y, 1 - c)
        chips = [(1 - x, y), (x, 1 - y), (1 - x, 1 - y)]

        def rows(px, py, pc):
            return out_ref.at[2 * px + py, pl.ds(pc * half, half), :]

        def copy(k, block, to, src=None):
            return pltpu.make_async_remote_copy(
                src_ref=rows(*block) if src is None else src, dst_ref=rows(*block),
                send_sem=send_sems.at[k], recv_sem=recv_sems.at[k], device_id=to, device_id_type=MESH)

        first = [copy(j, (x, y, c), (*chip, c), src=w_ref.at[pl.ds(c * half, half), :]) for j, chip in enumerate(chips)]
        for cp in first:
            cp.start()
        passed = [copy(3 + j, (*chip, c), sibling) for j, chip in enumerate(chips)]
        for j, chip in enumerate(chips):
            copy(j, (*chip, c), (x, y, c)).wait_recv()
            passed[j].start()
        for j, chip in enumerate(chips):
            copy(3 + j, (*chip, 1 - c), (x, y, c)).wait_recv()
        for cp in first + passed:
            cp.wait_send()

    return pl.pallas_call(
        body, out_shape=jax.ShapeDtypeStruct((4, r, D), wpack.dtype), in_specs=[ANY], out_specs=ANY,
        scratch_shapes=[pltpu.SemaphoreType.DMA((6,)), pltpu.SemaphoreType.DMA((6,))],
        name="allgather_weights")(wpack)


def _exchange_core_halves(g):
    half = g.shape[1] // 2

    def body(g_ref, out_ref, send_sem, recv_sem):
        x, y, c = _mesh_pos()
        cp = pltpu.make_async_remote_copy(
            src_ref=g_ref.at[:, pl.ds((1 - c) * half, half), :], dst_ref=out_ref,
            send_sem=send_sem, recv_sem=recv_sem, device_id=(x, y, 1 - c), device_id_type=MESH)
        cp.start()
        cp.wait()

    return pl.pallas_call(
        body, out_shape=jax.ShapeDtypeStruct((4, half, D), g.dtype), in_specs=[ANY], out_specs=ANY,
        scratch_shapes=[pltpu.SemaphoreType.DMA, pltpu.SemaphoreType.DMA], name="rs_exchange_cores")(g)


def _add_core_halves(g, other, cidx, tm=544):
    half = other.shape[1]
    nb = half // tm

    def kern(c_ref, a_ref, b_ref, o_ref):
        o_ref[...] = (a_ref[...] + b_ref[...]).astype(BF16)

    gs = pltpu.PrefetchScalarGridSpec(
        num_scalar_prefetch=1, grid=(4, nb),
        in_specs=[pl.BlockSpec((1, tm, D), lambda s, i, c: (s, c[0] * nb + i, 0)),
                  pl.BlockSpec((1, tm, D), lambda s, i, c: (s, i, 0))],
        out_specs=pl.BlockSpec((1, tm, D), lambda s, i, c: (s, i, 0)))
    return pl.pallas_call(kern, grid_spec=gs, out_shape=jax.ShapeDtypeStruct(other.shape, BF16),
                          name="rs_add_cores", compiler_params=_cp())(cidx, g, other)


def _exchange_chip_shards(p):
    def body(p_ref, out_ref, send_sems, recv_sems):
        x, y, c = _mesh_pos()
        me = 2 * x + y
        chips = [(1 - x, y), (x, 1 - y), (1 - x, 1 - y)]
        sends = []
        for j, (px, py) in enumerate(chips):
            cp = pltpu.make_async_remote_copy(
                src_ref=p_ref.at[2 * px + py], dst_ref=out_ref.at[me],
                send_sem=send_sems.at[j], recv_sem=recv_sems.at[j], device_id=(px, py, c), device_id_type=MESH)
            cp.start()
            sends.append(cp)
        for j, (px, py) in enumerate(chips):
            pltpu.make_async_remote_copy(
                src_ref=p_ref.at[me], dst_ref=out_ref.at[2 * px + py],
                send_sem=send_sems.at[j], recv_sem=recv_sems.at[j], device_id=(px, py, c),
                device_id_type=MESH).wait_recv()
        for cp in sends:
            cp.wait_send()

    return pl.pallas_call(
        body, out_shape=jax.ShapeDtypeStruct(p.shape, p.dtype), in_specs=[ANY], out_specs=ANY,
        scratch_shapes=[pltpu.SemaphoreType.DMA((3,)), pltpu.SemaphoreType.DMA((3,))],
        name="rs_exchange_chips")(p)


def _sum_slots(slots, p, pos, tm=544):
    half = slots.shape[1]
    nb = half // tm

    def kern(pos_ref, p_ref, s1_ref, s2_ref, s3_ref, o_ref):
        o_ref[...] = ((p_ref[0].astype(F32) + s1_ref[0].astype(F32)) + s2_ref[0].astype(F32)) + s3_ref[0].astype(F32)

    def slot(k):
        return pl.BlockSpec((1, tm, D), lambda i, pos: ((pos[0] + k) % 4, i, 0))

    gs = pltpu.PrefetchScalarGridSpec(
        num_scalar_prefetch=1, grid=(nb,), in_specs=[slot(0), slot(1), slot(2), slot(3)],
        out_specs=pl.BlockSpec((tm, D), lambda i, pos: (pos[1] * nb + i, 0)))
    return pl.pallas_call(kern, grid_spec=gs, out_shape=jax.ShapeDtypeStruct((2 * half, D), F32),
                          name="rs_sum_chips", compiler_params=_cp())(pos, p, slots, slots, slots)


def _join_core_halves(r):
    half = r.shape[0] // 2

    def body(r_ref, out_ref, send_sem, recv_sem):
        x, y, c = _mesh_pos()
        mine = out_ref.at[pl.ds(c * half, half), :]
        cp = pltpu.make_async_remote_copy(
            src_ref=mine, dst_ref=mine, send_sem=send_sem, recv_sem=recv_sem,
            device_id=(x, y, 1 - c), device_id_type=MESH)
        cp.start()
        theirs = out_ref.at[pl.ds((1 - c) * half, half), :]
        pltpu.make_async_remote_copy(
            src_ref=theirs, dst_ref=theirs, send_sem=send_sem, recv_sem=recv_sem,
            device_id=(x, y, 1 - c), device_id_type=MESH).wait_recv()
        cp.wait_send()

    return pl.pallas_call(
        body, out_shape=jax.ShapeDtypeStruct(r.shape, r.dtype), in_specs=[ANY], out_specs=ANY,
        input_output_aliases={0: 0},
        scratch_shapes=[pltpu.SemaphoreType.DMA, pltpu.SemaphoreType.DMA],
        name="rs_join_cores")(r)


def _allreduce_small(v, name):
    def body(v_ref, out_ref, gat, send_sems, recv_sems):
        x, y, c = _mesh_pos()
        me = 4 * x + 2 * y + c
        gat[me] = v_ref[...]
        sends = []
        for k in range(1, 8):
            peer = (x ^ (k >> 2), y ^ ((k >> 1) & 1), c ^ (k & 1))
            cp = pltpu.make_async_remote_copy(
                src_ref=v_ref, dst_ref=gat.at[me], send_sem=send_sems.at[k - 1], recv_sem=recv_sems.at[k - 1],
                device_id=peer, device_id_type=MESH)
            cp.start()
            sends.append(cp)
        for k in range(1, 8):
            px, py, pc = x ^ (k >> 2), y ^ ((k >> 1) & 1), c ^ (k & 1)
            pltpu.make_async_remote_copy(
                src_ref=v_ref, dst_ref=gat.at[4 * px + 2 * py + pc], send_sem=send_sems.at[k - 1],
                recv_sem=recv_sems.at[k - 1], device_id=(px, py, pc), device_id_type=MESH).wait_recv()
        for cp in sends:
            cp.wait_send()
        acc = gat[0]
        for d in range(1, 8):
            acc = acc + gat[d]
        out_ref[...] = acc

    return pl.pallas_call(
        body, out_shape=jax.ShapeDtypeStruct(v.shape, F32),
        in_specs=[pl.BlockSpec(memory_space=pltpu.VMEM)], out_specs=pl.BlockSpec(memory_space=pltpu.VMEM),
        scratch_shapes=[pltpu.VMEM((8,) + v.shape, F32), pltpu.SemaphoreType.DMA((7,)), pltpu.SemaphoreType.DMA((7,))],
        name=name)(v)


def _mlp_fwd(xb, w_up, w_down, tag):
    u, a = _mm(xb, w_up, "nn", f"mlp_up_{tag}", relu2=True)
    return u, a, _mm(a, w_down, "nn", f"mlp_down_{tag}")


def _mlp_bwd(dz, dzb, xb, u, a, w_up, w_down, tag):
    du = _mm(dzb, w_down, "nt", f"mlp_down_dx_{tag}", out_dtype=BF16, gate_u=u)
    dw_down = _mm(a, dzb, "tn", f"mlp_down_dw_{tag}")
    dw_up = _mm(xb, du, "tn", f"mlp_up_dw_{tag}")
    dx = _mm(du, w_up, "nt", f"mlp_up_dx_{tag}", addend=dz, add_scale=ALPHA)
    return dx, dw_up, dw_down


def _fwd_bwd(x, target, w, bq=512, qb=4, hb=4):
    t = x.shape[0]
    bq = min(bq, t)
    qb = min(qb, t // WIN)
    cos, sin = _rope_tables(t)
    bkt = jnp.asarray(_bucket_table())
    w_in = jnp.pad(w["mla_w_in"], ((0, 0), (0, HW - w["mla_w_in"].shape[1])))
    wuq = w["mla_w_uq"]
    wq2 = jnp.concatenate([wuq[:, :, :NOPE].reshape(QR, H * NOPE),
                           jnp.pad(wuq[:, :, NOPE:], ((0, 0), (0, 0), (0, RP - ROPE))).reshape(QR, H * RP)], axis=1)
    wuk_t = w["mla_w_uk"].transpose(1, 2, 0)
    wuk_h = w["mla_w_uk"].transpose(1, 0, 2)
    wuv_h = w["mla_w_uv"].transpose(1, 0, 2)
    w_o = w["mla_w_o"]
    wqkv = jnp.concatenate([w["swa_w_q"], w["kv_w_shared"]], axis=1)
    wqkv_t = wqkv.T
    wo_s = w["swa_w_o"]
    sinks = w["swa_sinks"].reshape(QH)
    lnp = lambda n, l: w[n][l]

    hh = _mm(x, w_in, "nn", "mla_in")
    cq, kc = _mla_pre(hh, w["mla_g_q"], w["mla_g_kv"], cos, sin)
    q2 = _mm(cq, wq2, "nn", "mla_uq")
    qcat = _q_prep(q2, wuk_t, cos, sin)
    o_lat, lse0_t = _flash_fwd(qcat, kc, bq, hb)
    o0 = _o_up(o_lat, wuv_h)
    y0 = _mm(o0, w_o, "nn", "mla_out")
    x1, x1b, xh1, r1 = _add_ln(x, y0, lnp("ln_mix_g", 0), lnp("ln_mix_b", 0), "ln_mix_0")
    u0, a0, f0 = _mlp_fwd(x1b, w["mlp_w_up"][0], w["mlp_w_down"][0], 0)
    x2, x2b, xh2, r2 = _add_ln(x1, f0, lnp("ln_mlp_g", 0), lnp("ln_mlp_b", 0), "ln_mlp_0")
    bias = _bias_build(w["rel_bias"], bkt)
    qkv_t = _mm(x2b, wqkv, "nn", "swa_qkv", out_dtype=BF16, out_t=True)
    os_t, lse1 = _swa_fwd(qkv_t, bias, sinks, qb)
    y1 = _mm(os_t, wo_s, "tn", "swa_out")
    x3, x3b, xh3, r3 = _add_ln(x2, y1, lnp("ln_mix_g", 1), lnp("ln_mix_b", 1), "ln_mix_1")
    u1, a1, f1 = _mlp_fwd(x3b, w["mlp_w_up"][1], w["mlp_w_down"][1], 1)
    x4, _, xh4, r4 = _add_ln(x3, f1, lnp("ln_mlp_g", 1), lnp("ln_mlp_b", 1), "ln_mlp_1")
    dx4, lpart = _loss_grad(x4, target)

    g = {}
    dz4, dz4b, dg_mlp1, db_mlp1 = _ln_bwd(dx4, xh4, r4, lnp("ln_mlp_g", 1), "ln_mlp_1_bwd")
    dx3, dwu1, dwd1 = _mlp_bwd(dz4, dz4b, x3b, u1, a1, w["mlp_w_up"][1], w["mlp_w_down"][1], 1)
    dz3, dz3b, dg_mix1, db_mix1 = _ln_bwd(dx3, xh3, r3, lnp("ln_mix_g", 1), "ln_mix_1_bwd")
    dos_t = _mm(dz3b, wo_s, "nt", "swa_out_dx", out_t=True)
    g["swa_w_o"] = _mm(os_t, dz3b, "nn", "swa_out_dw")
    dqkv_t, dbias, dsk = _swa_bwd(qkv_t, dos_t, os_t, lse1, bias, sinks, qb)
    dwqkv = _mm(dqkv_t, x2b, "nn", "swa_qkv_dw").T
    g["swa_w_q"], g["kv_w_shared"] = dwqkv[:, :QH * HD], dwqkv[:, QH * HD:]
    dx2 = _mm(dqkv_t, wqkv_t, "tn", "swa_qkv_dx", addend=dz3, add_scale=ALPHA)
    g["rel_bias"] = jnp.sum(_bias_bwd(dbias, bkt), axis=-1).reshape(NBKT, QH)
    g["swa_sinks"] = jnp.sum(dsk, axis=-1).reshape(1, QH)
    dz2, dz2b, dg_mlp0, db_mlp0 = _ln_bwd(dx2, xh2, r2, lnp("ln_mlp_g", 0), "ln_mlp_0_bwd")
    dx1, dwu0, dwd0 = _mlp_bwd(dz2, dz2b, x1b, u0, a0, w["mlp_w_up"][0], w["mlp_w_down"][0], 0)
    dz1, dz1b, dg_mix0, db_mix0 = _ln_bwd(dx1, xh1, r1, lnp("ln_mix_g", 0), "ln_mix_0_bwd")
    do0 = _mm(dz1b, w_o, "nt", "mla_out_dx", out_dtype=BF16)
    g["mla_w_o"] = _mm(o0, dz1b, "tn", "mla_out_dw")
    do_lat, dwuv, delta_t = _o_up_bwd(do0, o_lat, wuv_h)
    g["mla_w_uv"] = dwuv.transpose(1, 0, 2)
    dk, ds_all = _flash_dkv(qcat, kc, do_lat, lse0_t, delta_t, bq, hb)
    dq_cat = _flash_dq(ds_all, kc.T, bq, hb)
    dq2, dwuk = _q_prep_bwd(dq_cat, q2, wuk_h, cos, sin)
    g["mla_w_uk"] = dwuk.transpose(2, 0, 1)
    dcq = _mm(dq2, wq2, "nt", "mla_uq_dx")
    dwq2 = _mm(cq, dq2, "tn", "mla_uq_dw")
    g["mla_w_uq"] = jnp.concatenate([dwq2[:, :H * NOPE].reshape(QR, H, NOPE),
                                     dwq2[:, H * NOPE:].reshape(QR, H, RP)[:, :, :ROPE]], axis=2)
    dh, dgq, dgkv = _mla_pre_bwd(hh, dcq, dk, w["mla_g_q"], w["mla_g_kv"], cos, sin)
    g["mla_w_in"] = _mm(x, dh, "tn", "mla_in_dw")[:, :QR + KVR + ROPE]
    grad_x = _mm(dh, w_in, "nt", "mla_in_dx", addend=dz1, add_scale=ALPHA)
    g["mla_g_q"], g["mla_g_kv"] = dgq, dgkv
    g["mlp_w_up"] = jnp.stack([dwu0, dwu1])
    g["mlp_w_down"] = jnp.stack([dwd0, dwd1])
    g["ln_mix_g"] = jnp.concatenate([dg_mix0, dg_mix1], axis=0)
    g["ln_mix_b"] = jnp.concatenate([db_mix0, db_mix1], axis=0)
    g["ln_mlp_g"] = jnp.concatenate([dg_mlp0, dg_mlp1], axis=0)
    g["ln_mlp_b"] = jnp.concatenate([db_mlp0, db_mlp1], axis=0)
    return lpart, grad_x, g


def _rows(a):
    return a.reshape(-1, D)


def _pack_shards(parts):
    return jnp.concatenate([_rows(parts[n]) for n, _ in PACK], axis=0)


def _unpack_shards(buf, like):
    out, off = {}, 0
    for n, r in PACK:
        out[n] = buf[off:off + r].reshape(like[n].shape)
        off += r
    return out


def _full_from_gathered(wall, shard_shapes):
    out, off = {}, 0
    for n, r in PACK:
        sl = wall[:, off:off + r]
        off += r
        shp = shard_shapes[n]
        if n == "mlp_w_up":
            out[n] = sl.reshape(4, 2, D, D).transpose(1, 2, 0, 3).reshape(2, D, DFF)
        elif n == "mlp_w_down":
            out[n] = sl.reshape(4, 2, D, D).transpose(1, 0, 2, 3).reshape(2, DFF, D)
        elif n == "kv_w_shared":
            out[n] = sl.reshape((4 * shp[0],) + tuple(shp[1:]))
        else:
            out[n] = sl.reshape((4 * shp[1],) + tuple(shp[2:]))
    return out


def _shards_from_full(grads):
    cols = []
    for n, r in PACK:
        gfull = grads[n]
        if n == "mlp_w_up":
            cols.append(gfull.reshape(2, D, 4, D).transpose(2, 0, 1, 3).reshape(4, r, D))
        elif n == "mlp_w_down":
            cols.append(gfull.reshape(2, 4, D, D).transpose(1, 0, 2, 3).reshape(4, r, D))
        else:
            cols.append(gfull.reshape(4, r, D))
    return jnp.concatenate(cols, axis=1)


SMALL = (("ln_mix_g", 0, 2), ("ln_mix_b", 2, 2), ("ln_mlp_g", 4, 2), ("ln_mlp_b", 6, 2),
         ("swa_sinks", 8, 1), ("mla_g_q", 9, 1), ("mla_g_kv", 10, 1), ("rel_bias", 11, 1))


def _pack_small(parts):
    rows = []
    for n, _, nr in SMALL:
        a = parts[n].reshape(nr, -1).astype(F32)
        rows.append(jnp.pad(a, ((0, 0), (0, D - a.shape[1]))))
    rows.append(jnp.zeros((SMALL_ROWS - 12, D), F32))
    return jnp.concatenate(rows, axis=0)


def _unpack_small(buf, like):
    out = {}
    for n, r0, nr in SMALL:
        size = like[n].size // nr
        out[n] = buf[r0:r0 + nr, :size].reshape(like[n].shape)
    return out


def kernel(x, mla_w_in, mla_g_q, mla_g_kv, mla_w_uq, mla_w_uk, mla_w_uv, mla_w_o, kv_w_shared, swa_w_q, swa_sinks, swa_w_o, rel_bias, mlp_w_up, mlp_w_down, ln_mix_g, ln_mix_b, ln_mlp_g, ln_mlp_b, loss_target, m_mla_w_in, m_mla_g_q, m_mla_g_kv, m_mla_w_uq, m_mla_w_uk, m_mla_w_uv, m_mla_w_o, m_kv_w_shared, m_swa_w_q, m_swa_sinks, m_swa_w_o, m_rel_bias, m_mlp_w_up, m_mlp_w_down, m_ln_mix_g, m_ln_mix_b, m_ln_mlp_g, m_ln_mlp_b, v_mla_w_in, v_mla_g_q, v_mla_g_kv, v_mla_w_uq, v_mla_w_uk, v_mla_w_uv, v_mla_w_o, v_kv_w_shared, v_swa_w_q, v_swa_sinks, v_swa_w_o, v_rel_bias, v_mlp_w_up, v_mlp_w_down, v_ln_mix_g, v_ln_mix_b, v_ln_mlp_g, v_ln_mlp_b):
    names = ["mla_w_in", "mla_g_q", "mla_g_kv", "mla_w_uq", "mla_w_uk", "mla_w_uv", "mla_w_o", "kv_w_shared",
             "swa_w_q", "swa_sinks", "swa_w_o", "rel_bias", "mlp_w_up", "mlp_w_down",
             "ln_mix_g", "ln_mix_b", "ln_mlp_g", "ln_mlp_b"]
    ws = dict(zip(names, [mla_w_in, mla_g_q, mla_g_kv, mla_w_uq, mla_w_uk, mla_w_uv, mla_w_o, kv_w_shared,
                          swa_w_q, swa_sinks, swa_w_o, rel_bias, mlp_w_up, mlp_w_down,
                          ln_mix_g, ln_mix_b, ln_mlp_g, ln_mlp_b]))
    ms = dict(zip(names, [m_mla_w_in, m_mla_g_q, m_mla_g_kv, m_mla_w_uq, m_mla_w_uk, m_mla_w_uv, m_mla_w_o,
                          m_kv_w_shared, m_swa_w_q, m_swa_sinks, m_swa_w_o, m_rel_bias, m_mlp_w_up, m_mlp_w_down,
                          m_ln_mix_g, m_ln_mix_b, m_ln_mlp_g, m_ln_mlp_b]))
    vs = dict(zip(names, [v_mla_w_in, v_mla_g_q, v_mla_g_kv, v_mla_w_uq, v_mla_w_uk, v_mla_w_uv, v_mla_w_o,
                          v_kv_w_shared, v_swa_w_q, v_swa_sinks, v_swa_w_o, v_rel_bias, v_mlp_w_up, v_mlp_w_down,
                          v_ln_mix_g, v_ln_mix_b, v_ln_mlp_g, v_ln_mlp_b]))
    xi, yi, ci = _mesh_pos()
    shard = 2 * xi + yi
    big = [n for n, _ in PACK]
    shard_shapes = {n: ws[n].shape for n in big}

    wpack = _pack_shards({n: ws[n].astype(BF16) for n in big})
    wall = lax.dynamic_update_slice(_allgather_weights(wpack), wpack[None], (shard, 0, 0))
    w = _full_from_gathered(wall, shard_shapes)
    gq_slot = lax.dynamic_update_slice(jnp.zeros((1, QR), F32), mla_g_q, (0, shard * (QR // 4)))
    gkv_slot = lax.dynamic_update_slice(jnp.zeros((1, KVR), F32), mla_g_kv, (0, shard * (KVR // 4)))
    gains = jnp.concatenate([jnp.pad(gq_slot, ((0, 0), (0, D - QR))), jnp.pad(gkv_slot, ((0, 0), (0, D - KVR))),
                             jnp.zeros((SMALL_ROWS - 2, D), F32)], axis=0)
    gains = _allreduce_small(gains * 0.5, "allgather_gains")
    w["mla_g_q"], w["mla_g_kv"] = gains[0, :QR], gains[1, :KVR]
    for n in ("swa_sinks", "rel_bias", "ln_mix_g", "ln_mix_b", "ln_mlp_g", "ln_mlp_b"):
        w[n] = ws[n]

    lpart, grad_x, g = _fwd_bwd(x[0], loss_target[0], w)
    loss = lax.psum(0.5 * jnp.sum(lpart) / D, ("x", "y", "c"))

    gsh = _shards_from_full(g)
    other = _exchange_core_halves(gsh)
    cidx = jnp.reshape(ci, (1,)).astype(jnp.int32)
    chip_part = _add_core_halves(gsh, other, cidx)
    slots = _exchange_chip_shards(chip_part)
    pos = jnp.stack([shard, ci]).astype(jnp.int32)
    gred = _join_core_halves(_sum_slots(slots, chip_part, pos))
    gbig = _unpack_shards(gred, ws)

    small_like = {n: g[n] for n, _, _ in SMALL}
    gsm = _unpack_small(_allreduce_small(_pack_small(g), "allreduce_small_grads"), small_like)
    gsm["mla_g_q"] = lax.dynamic_slice(gsm["mla_g_q"], (0, shard * (QR // 4)), (1, QR // 4))
    gsm["mla_g_kv"] = lax.dynamic_slice(gsm["mla_g_kv"], (0, shard * (KVR // 4)), (1, KVR // 4))
    grads = {**gbig, **gsm}

    dbig, mbig, vbig = _adamw(_pack_shards({n: ws[n] for n in big}), gred,
                              _pack_shards({n: ms[n] for n in big}), _pack_shards({n: vs[n] for n in big}),
                              "adamw_big")
    dsm, msm, vsm = _adamw(_pack_small(ws), _pack_small(gsm), _pack_small(ms), _pack_small(vs), "adamw_small", tm=16)
    delta = {**_unpack_shards(dbig, ws), **_unpack_small(dsm, ws)}
    new_m = {**_unpack_shards(mbig, ws), **_unpack_small(msm, ws)}
    new_v = {**_unpack_shards(vbig, ws), **_unpack_small(vsm, ws)}
    grads = {n: grads[n].reshape(ws[n].shape) for n in names}
    return (loss, grad_x[None], *[grads[n] for n in names], *[delta[n] for n in names],
            *[new_m[n] for n in names], *[new_v[n] for n in names])
```

```python
import collections
import math

import numpy as np
import jax
import jax.numpy as jnp
from jax import lax
from jax.experimental import pallas as pl
from jax.experimental.pallas import tpu as pltpu

F32 = jnp.float32
BF16 = jnp.bfloat16
MESH = pl.DeviceIdType.MESH

D = 1024
DFF = 4096
H = 8
NOPE = 128
ROPE = 64
QR = 384
KVR = 256
RP = 128
KD = KVR + RP
HW = 768
QH = 16
KVH = 4
HD = 64
G = QH // KVH
WIN = 128
NBKT = 32
ALPHA = 4.0 ** 0.25
LN_EPS = 1e-5
RMS_EPS = 1e-6
MLA_SCALE = (NOPE + ROPE) ** -0.5
LOG2E = 1.4426950408889634
LN2 = 0.6931471805599453
QSCALE = MLA_SCALE * LOG2E
SWA_SCALE = HD ** -0.5
NEG = -1e30
LR, B1, B2, ADAM_EPS, WD, STEP = 0.001, 0.9, 0.999, 1e-8, 0.01, 10

VMEM_LIMIT = 48 * 1024 * 1024

NN = (((1,), (0,)), ((), ()))
NT = (((1,), (1,)), ((), ()))
TN = (((0,), (0,)), ((), ()))

ROWS = {"mlp_w_up": 1024, "mlp_w_down": 1024, "mla_w_o": 256, "swa_w_q": 256, "swa_w_o": 256,
        "kv_w_shared": 128, "mla_w_in": 176, "mla_w_uq": 144, "mla_w_uk": 64, "mla_w_uv": 64}
AG_EARLY = (("mla_w_in", None), ("mla_w_uq", None), ("mla_w_uk", None), ("mla_w_uv", None), ("mla_w_o", None))
AG_LATE = (("mlp_w_up", 0), ("mlp_w_up", 1), ("mlp_w_down", 0), ("mlp_w_down", 1),
           ("swa_w_q", None), ("swa_w_o", None), ("kv_w_shared", None))
RS_MLP1 = (("mlp_w_up", 1), ("mlp_w_down", 1))
RS_MLP0 = (("mlp_w_up", 0), ("mlp_w_down", 0))
RS_REST = (("mla_w_o", None), ("swa_w_q", None), ("swa_w_o", None), ("kv_w_shared", None),
           ("mla_w_in", None), ("mla_w_uq", None), ("mla_w_uk", None), ("mla_w_uv", None))
SMALL_ROWS = 16
_Dist = collections.namedtuple("_Dist", "shard cidx pos late_pack shard_shapes")


def _cp(**kw):
    return pltpu.CompilerParams(vmem_limit_bytes=VMEM_LIMIT, **kw)


def _tile(n, pref):
    t = min(n, pref)
    while n % t:
        t -= 128
    return t


def _dot(a, b, dims):
    return lax.dot_general(a, b, dims, preferred_element_type=F32)


def _mm(a, b, mode, name, out_dtype=F32, out_t=False, addend=None, add_scale=1.0, relu2=False, gate_u=None,
        b_blocked=False, out_blocked=False, tm=1024, tn=1024, tk=1024):
    if b_blocked:
        nb, brow, bcol = b.shape
        bshape = (brow, nb * bcol)
    else:
        bshape = b.shape
    if mode == "nn":
        (m, k), (k2, n) = a.shape, bshape
    elif mode == "nt":
        (m, k), (n, k2) = a.shape, bshape
    else:
        (k, m), (k2, n) = a.shape, bshape
    assert k == k2, (name, a.shape, b.shape)
    tm, tn, tk = _tile(m, tm), _tile(n, tn), _tile(k, tk)
    nk = k // tk
    dims = {"nn": NN, "nt": NT, "tn": TN}[mode]
    if mode == "tn":
        a_spec = pl.BlockSpec((tk, tm), lambda i, j, kk: (kk, i))
    else:
        a_spec = pl.BlockSpec((tm, tk), lambda i, j, kk: (i, kk))
    if b_blocked and mode == "nn":
        assert tn == b.shape[2]
        b_spec = pl.BlockSpec((None, tk, tn), lambda i, j, kk: (j, kk, 0))
    elif b_blocked and mode == "nt":
        assert tk == b.shape[2]
        b_spec = pl.BlockSpec((None, tn, tk), lambda i, j, kk: (kk, j, 0))
    elif mode == "nt":
        b_spec = pl.BlockSpec((tn, tk), lambda i, j, kk: (j, kk))
    else:
        assert not b_blocked
        b_spec = pl.BlockSpec((tk, tn), lambda i, j, kk: (kk, j))
    mn_spec = pl.BlockSpec((tm, tn), lambda i, j, kk: (i, j))
    ins, in_specs = [a, b], [a_spec, b_spec]
    if addend is not None:
        ins.append(addend)
        in_specs.append(mn_spec)
    if gate_u is not None:
        ins.append(gate_u)
        in_specs.append(mn_spec)
    if out_blocked:
        assert not out_t and not relu2
        out_shape = [jax.ShapeDtypeStruct((n // tn, m, tn), out_dtype)]
        out_specs = [pl.BlockSpec((None, tm, tn), lambda i, j, kk: (j, i, 0))]
    elif out_t:
        out_shape = [jax.ShapeDtypeStruct((n, m), out_dtype)]
        out_specs = [pl.BlockSpec((tn, tm), lambda i, j, kk: (j, i))]
    else:
        out_shape = [jax.ShapeDtypeStruct((m, n), out_dtype)]
        out_specs = [mn_spec]
    if relu2:
        out_shape.append(jax.ShapeDtypeStruct((m, n), BF16))
        out_specs.append(mn_spec)
    has_add, has_gate = addend is not None, gate_u is not None

    def kern(*refs):
        a_ref, b_ref = refs[0], refs[1]
        pos = 2
        add_ref = gate_ref = None
        if has_add:
            add_ref = refs[pos]
            pos += 1
        if has_gate:
            gate_ref = refs[pos]
            pos += 1
        o_ref = refs[pos]
        a2_ref = refs[pos + 1] if relu2 else None
        acc = refs[-1] if nk > 1 else None
        kk = pl.program_id(2)

        def partial():
            return _dot(a_ref[...].astype(BF16), b_ref[...].astype(BF16), dims)

        if nk > 1:
            @pl.when(kk == 0)
            def _():
                acc[...] = partial()

            @pl.when((kk > 0) & (kk < nk - 1))
            def _():
                acc[...] += partial()

        @pl.when(kk == nk - 1)
        def _():
            r = partial() + acc[...] if nk > 1 else partial()
            if has_add:
                r = r + add_scale * add_ref[...].astype(F32)
            if has_gate:
                r = r * (2.0 * jnp.maximum(gate_ref[...], 0.0))
            if relu2:
                hh = jnp.maximum(r, 0.0)
                a2_ref[...] = (hh * hh).astype(BF16)
            if out_t:
                r = r.T
            o_ref[...] = r.astype(out_dtype)

    outs = pl.pallas_call(
        kern, out_shape=out_shape, grid=(m // tm, n // tn, nk), in_specs=in_specs, out_specs=out_specs,
        scratch_shapes=[pltpu.VMEM((tm, tn), F32)] if nk > 1 else [], name=name, compiler_params=_cp())(*ins)
    return outs if relu2 else outs[0]


def _add_ln(xres, y, g, b, name, tm=256):
    t = xres.shape[0]
    tm = min(tm, t)

    def kern(x_ref, y_ref, g_ref, b_ref, o_ref, ob_ref, xh_ref, r_ref):
        z = ALPHA * x_ref[...] + y_ref[...]
        mu = jnp.mean(z, axis=-1, keepdims=True)
        zc = z - mu
        var = jnp.mean(zc * zc, axis=-1, keepdims=True)
        r = lax.rsqrt(var + LN_EPS)
        xh = zc * r
        o = xh * g_ref[...] + b_ref[...]
        o_ref[...] = o
        ob_ref[...] = o.astype(BF16)
        xh_ref[...] = xh
        r_ref[...] = r

    row = pl.BlockSpec((tm, D), lambda i: (i, 0))
    vec = pl.BlockSpec((1, D), lambda i: (0, 0))
    st = pl.BlockSpec((tm, 1), lambda i: (i, 0))
    return pl.pallas_call(
        kern, grid=(t // tm,), in_specs=[row, row, vec, vec], out_specs=[row, row, row, st],
        out_shape=[jax.ShapeDtypeStruct((t, D), F32), jax.ShapeDtypeStruct((t, D), BF16),
                   jax.ShapeDtypeStruct((t, D), F32), jax.ShapeDtypeStruct((t, 1), F32)],
        name=name, compiler_params=_cp())(xres, y, g.reshape(1, D), b.reshape(1, D))


def _ln_bwd(dout, xhat, rstd, g, name, tm=256):
    t = dout.shape[0]
    tm = min(tm, t)

    def kern(do_ref, xh_ref, r_ref, g_ref, dz_ref, dzb_ref, dg_ref, db_ref):
        @pl.when(pl.program_id(0) == 0)
        def _():
            dg_ref[...] = jnp.zeros_like(dg_ref)
            db_ref[...] = jnp.zeros_like(db_ref)

        do = do_ref[...]
        xh = xh_ref[...]
        dxh = do * g_ref[...]
        m1 = jnp.mean(dxh, axis=-1, keepdims=True)
        m2 = jnp.mean(dxh * xh, axis=-1, keepdims=True)
        dz = r_ref[...] * (dxh - m1 - xh * m2)
        dz_ref[...] = dz
        dzb_ref[...] = dz.astype(BF16)
        dg_ref[...] += jnp.sum(do * xh, axis=0, keepdims=True)
        db_ref[...] += jnp.sum(do, axis=0, keepdims=True)

    row = pl.BlockSpec((tm, D), lambda i: (i, 0))
    vec = pl.BlockSpec((1, D), lambda i: (0, 0))
    st = pl.BlockSpec((tm, 1), lambda i: (i, 0))
    return pl.pallas_call(
        kern, grid=(t // tm,), in_specs=[row, row, st, vec], out_specs=[row, row, vec, vec],
        out_shape=[jax.ShapeDtypeStruct((t, D), F32), jax.ShapeDtypeStruct((t, D), BF16),
                   jax.ShapeDtypeStruct((1, D), F32), jax.ShapeDtypeStruct((1, D), F32)],
        name=name, compiler_params=_cp())(dout, xhat, rstd, g.reshape(1, D))


def _loss_grad(y, target, name="loss_grad", tm=256):
    t = y.shape[0]
    tm = min(tm, t)

    def kern(y_ref, t_ref, d_ref, l_ref):
        @pl.when(pl.program_id(0) == 0)
        def _():
            l_ref[...] = jnp.zeros_like(l_ref)

        e = y_ref[...] - t_ref[...]
        d_ref[...] = e * (1.0 / D)
        l_ref[...] += jnp.sum(e * e, axis=0, keepdims=True)

    row = pl.BlockSpec((tm, D), lambda i: (i, 0))
    vec = pl.BlockSpec((1, D), lambda i: (0, 0))
    return pl.pallas_call(
        kern, grid=(t // tm,), in_specs=[row, row], out_specs=[row, vec],
        out_shape=[jax.ShapeDtypeStruct((t, D), F32), jax.ShapeDtypeStruct((1, D), F32)],
        name=name, compiler_params=_cp())(y, target)


def _rope_tables(t):
    half = ROPE // 2
    inv = 10000.0 ** (-jnp.arange(half, dtype=F32) / half)
    ang = jnp.arange(t).astype(F32)[:, None] * inv[None, :]
    cos, sin = jnp.cos(ang), jnp.sin(ang)
    z = jnp.zeros((t, RP - ROPE), F32)
    return jnp.concatenate([cos, cos, z], axis=1), jnp.concatenate([-sin, sin, z], axis=1)


def _swap_halves(x):
    lane = lax.broadcasted_iota(jnp.int32, x.shape, 1)
    return jnp.where(lane < ROPE // 2, pltpu.roll(x, RP - ROPE // 2, 1), pltpu.roll(x, ROPE // 2, 1))


def _rope(x, cos, sin):
    return x * cos + _swap_halves(x) * sin


def _rope_t(gy, cos, sin):
    return gy * cos + _swap_halves(gy * sin)


def _mla_pre(hh, g_q, g_kv, cos, sin, tm=256):
    t = hh.shape[0]
    tm = min(tm, t)

    def kern(h_ref, gq_ref, gkv_ref, c_ref, s_ref, cq_ref, k_ref):
        xq = h_ref[:, 0:QR]
        rq = lax.rsqrt(jnp.mean(xq * xq, axis=-1, keepdims=True) + RMS_EPS)
        cq_ref[...] = (xq * rq * gq_ref[...]).astype(BF16)
        xk = h_ref[:, QR:QR + KVR]
        rk = lax.rsqrt(jnp.mean(xk * xk, axis=-1, keepdims=True) + RMS_EPS)
        k_ref[:, 0:KVR] = (xk * rk * gkv_ref[...]).astype(BF16)
        k_ref[:, KVR:KD] = _rope(h_ref[:, QR + KVR:HW], c_ref[...], s_ref[...]).astype(BF16)

    return pl.pallas_call(
        kern, grid=(t // tm,),
        in_specs=[pl.BlockSpec((tm, HW), lambda i: (i, 0)), pl.BlockSpec((1, QR), lambda i: (0, 0)),
                  pl.BlockSpec((1, KVR), lambda i: (0, 0)), pl.BlockSpec((tm, RP), lambda i: (i, 0)),
                  pl.BlockSpec((tm, RP), lambda i: (i, 0))],
        out_specs=[pl.BlockSpec((tm, QR), lambda i: (i, 0)), pl.BlockSpec((tm, KD), lambda i: (i, 0))],
        out_shape=[jax.ShapeDtypeStruct((t, QR), BF16), jax.ShapeDtypeStruct((t, KD), BF16)],
        name="mla_pre", compiler_params=_cp())(hh, g_q.reshape(1, QR), g_kv.reshape(1, KVR), cos, sin)


def _mla_pre_bwd(hh, dcq, dk, g_q, g_kv, cos, sin, tm=256):
    t = hh.shape[0]
    tm = min(tm, t)

    def rms_bwd(x, dy, g):
        r = lax.rsqrt(jnp.mean(x * x, axis=-1, keepdims=True) + RMS_EPS)
        gdy = dy * g
        dx = r * gdy - x * (r * r * r) * jnp.mean(gdy * x, axis=-1, keepdims=True)
        return dx, jnp.sum(dy * x * r, axis=0, keepdims=True)

    def kern(h_ref, dcq_ref, dk_ref, gq_ref, gkv_ref, c_ref, s_ref, dh_ref, dgq_ref, dgkv_ref):
        @pl.when(pl.program_id(0) == 0)
        def _():
            dgq_ref[...] = jnp.zeros_like(dgq_ref)
            dgkv_ref[...] = jnp.zeros_like(dgkv_ref)

        dxq, dgq = rms_bwd(h_ref[:, 0:QR], dcq_ref[...], gq_ref[...])
        dxk, dgk = rms_bwd(h_ref[:, QR:QR + KVR], dk_ref[:, 0:KVR], gkv_ref[...])
        dh_ref[:, 0:QR] = dxq.astype(BF16)
        dh_ref[:, QR:QR + KVR] = dxk.astype(BF16)
        dh_ref[:, QR + KVR:HW] = _rope_t(dk_ref[:, KVR:KD], c_ref[...], s_ref[...]).astype(BF16)
        dgq_ref[...] += dgq
        dgkv_ref[...] += dgk

    return pl.pallas_call(
        kern, grid=(t // tm,),
        in_specs=[pl.BlockSpec((tm, HW), lambda i: (i, 0)), pl.BlockSpec((tm, QR), lambda i: (i, 0)),
                  pl.BlockSpec((tm, KD), lambda i: (i, 0)), pl.BlockSpec((1, QR), lambda i: (0, 0)),
                  pl.BlockSpec((1, KVR), lambda i: (0, 0)), pl.BlockSpec((tm, RP), lambda i: (i, 0)),
                  pl.BlockSpec((tm, RP), lambda i: (i, 0))],
        out_specs=[pl.BlockSpec((tm, HW), lambda i: (i, 0)), pl.BlockSpec((1, QR), lambda i: (0, 0)),
                   pl.BlockSpec((1, KVR), lambda i: (0, 0))],
        out_shape=[jax.ShapeDtypeStruct((t, HW), BF16), jax.ShapeDtypeStruct((1, QR), F32),
                   jax.ShapeDtypeStruct((1, KVR), F32)],
        name="mla_pre_bwd", compiler_params=_cp())(hh, dcq, dk, g_q.reshape(1, QR), g_kv.reshape(1, KVR), cos, sin)


def _q_prep(q2, wuk_t, cos, sin, tm=256):
    t = q2.shape[0]
    tm = min(tm, t)

    def kern(q_ref, w_ref, c_ref, s_ref, o_ref):
        cos_, sin_ = c_ref[...], s_ref[...]
        for h in range(H):
            qn = q_ref[:, h * NOPE:(h + 1) * NOPE].astype(BF16)
            o_ref[:, h * KD:h * KD + KVR] = (_dot(qn, w_ref[h], NN) * QSCALE).astype(BF16)
            qr = q_ref[:, H * NOPE + h * RP:H * NOPE + (h + 1) * RP]
            o_ref[:, h * KD + KVR:(h + 1) * KD] = (_rope(qr, cos_, sin_) * QSCALE).astype(BF16)

    return pl.pallas_call(
        kern, grid=(t // tm,),
        in_specs=[pl.BlockSpec((tm, 2 * H * NOPE), lambda i: (i, 0)), pl.BlockSpec((H, NOPE, KVR), lambda i: (0, 0, 0)),
                  pl.BlockSpec((tm, RP), lambda i: (i, 0)), pl.BlockSpec((tm, RP), lambda i: (i, 0))],
        out_specs=pl.BlockSpec((tm, H * KD), lambda i: (i, 0)),
        out_shape=jax.ShapeDtypeStruct((t, H * KD), BF16),
        name="q_prep", compiler_params=_cp())(q2, wuk_t, cos, sin)


def _q_prep_bwd(dq_cat, q2, wuk_h, cos, sin, tm=256):
    t = q2.shape[0]
    tm = min(tm, t)

    def kern(dq_ref, q_ref, w_ref, c_ref, s_ref, o_ref, dw_ref):
        @pl.when(pl.program_id(0) == 0)
        def _():
            dw_ref[...] = jnp.zeros_like(dw_ref)

        cos_, sin_ = c_ref[...], s_ref[...]
        for h in range(H):
            dql = dq_ref[:, h * KD:h * KD + KVR].astype(BF16)
            o_ref[:, h * NOPE:(h + 1) * NOPE] = _dot(dql, w_ref[h], NN).astype(BF16)
            dqr = dq_ref[:, h * KD + KVR:(h + 1) * KD]
            o_ref[:, H * NOPE + h * RP:H * NOPE + (h + 1) * RP] = _rope_t(dqr, cos_, sin_).astype(BF16)
            qn = q_ref[:, h * NOPE:(h + 1) * NOPE].astype(BF16)
            dw_ref[h] += _dot(qn, dql, TN)

    return pl.pallas_call(
        kern, grid=(t // tm,),
        in_specs=[pl.BlockSpec((tm, H * KD), lambda i: (i, 0)), pl.BlockSpec((tm, 2 * H * NOPE), lambda i: (i, 0)),
                  pl.BlockSpec((H, KVR, NOPE), lambda i: (0, 0, 0)),
                  pl.BlockSpec((tm, RP), lambda i: (i, 0)), pl.BlockSpec((tm, RP), lambda i: (i, 0))],
        out_specs=[pl.BlockSpec((tm, 2 * H * NOPE), lambda i: (i, 0)), pl.BlockSpec((H, NOPE, KVR), lambda i: (0, 0, 0))],
        out_shape=[jax.ShapeDtypeStruct((t, 2 * H * NOPE), BF16), jax.ShapeDtypeStruct((H, NOPE, KVR), F32)],
        name="q_prep_bwd", compiler_params=_cp())(dq_cat, q2, wuk_h, cos, sin)


def _o_up(o_lat, wuv_h, tm=256):
    t = o_lat.shape[0]
    tm = min(tm, t)

    def kern(x_ref, w_ref, o_ref):
        for h in range(H):
            xl = x_ref[:, h * KVR:(h + 1) * KVR].astype(BF16)
            o_ref[:, h * NOPE:(h + 1) * NOPE] = _dot(xl, w_ref[h], NN).astype(BF16)

    return pl.pallas_call(
        kern, grid=(t // tm,),
        in_specs=[pl.BlockSpec((tm, H * KVR), lambda i: (i, 0)), pl.BlockSpec((H, KVR, NOPE), lambda i: (0, 0, 0))],
        out_specs=pl.BlockSpec((tm, H * NOPE), lambda i: (i, 0)),
        out_shape=jax.ShapeDtypeStruct((t, H * NOPE), BF16),
        name="o_up", compiler_params=_cp())(o_lat, wuv_h)


def _o_up_bwd(do, o_lat, wuv_h, tm=256):
    t = do.shape[0]
    tm = min(tm, t)

    def kern(do_ref, x_ref, w_ref, dx_ref, dw_ref, dlt_ref):
        @pl.when(pl.program_id(0) == 0)
        def _():
            dw_ref[...] = jnp.zeros_like(dw_ref)

        for h in range(H):
            dh_ = do_ref[:, h * NOPE:(h + 1) * NOPE]
            x = x_ref[:, h * KVR:(h + 1) * KVR]
            dx = _dot(dh_, w_ref[h], NT)
            dx_ref[:, h * KVR:(h + 1) * KVR] = dx
            dw_ref[h] += _dot(x.astype(BF16), dh_, TN)
            dl = jnp.broadcast_to(jnp.sum(dx * x, axis=1)[:, None], (tm, 128))
            dlt_ref[h] = dl.T[0:1, :]

    return pl.pallas_call(
        kern, grid=(t // tm,),
        in_specs=[pl.BlockSpec((tm, H * NOPE), lambda i: (i, 0)), pl.BlockSpec((tm, H * KVR), lambda i: (i, 0)),
                  pl.BlockSpec((H, KVR, NOPE), lambda i: (0, 0, 0))],
        out_specs=[pl.BlockSpec((tm, H * KVR), lambda i: (i, 0)), pl.BlockSpec((H, KVR, NOPE), lambda i: (0, 0, 0)),
                   pl.BlockSpec((H, 1, tm), lambda i: (0, 0, i))],
        out_shape=[jax.ShapeDtypeStruct((t, H * KVR), F32), jax.ShapeDtypeStruct((H, KVR, NOPE), F32),
                   jax.ShapeDtypeStruct((H, 1, t), F32)],
        name="o_up_bwd", compiler_params=_cp())(do, o_lat, wuv_h)


def _causal_pairs(nq):
    return [(i, j) for i in range(nq) for j in range(i + 1)]


def _lane_tile(stat, width):
    return jnp.tile(stat, (1, width // 128))


def _flash_fwd(qcat, kc, bq, hb, gather=None):
    t = kc.shape[0]
    nq = t // bq
    pairs = _causal_pairs(nq)
    itab = jnp.asarray(np.array([p[0] for p in pairs], np.int32))
    jtab = jnp.asarray(np.array([p[1] for p in pairs], np.int32))

    ng = H // hb
    hosting = gather is not None

    def kern(it, jt, q_ref, k_ref, *rest):
        if hosting:
            w_ref, o_ref, lset_ref, wall_ref, m_sc, l_sc, acc_sc, send_sems, recv_sems = rest
            ag_start, ag_forward, ag_finish = _allgather_schedule(w_ref, wall_ref, send_sems, recv_sems)
        else:
            o_ref, lset_ref, m_sc, l_sc, acc_sc = rest
        grp = pl.program_id(0)
        st = pl.program_id(1)
        i, j = it[st], jt[st]

        if hosting:
            @pl.when((grp == 0) & (st == 0))
            def _():
                ag_start()

        @pl.when(j == 0)
        def _():
            m_sc[...] = jnp.full_like(m_sc, NEG)
            l_sc[...] = jnp.zeros_like(l_sc)
            acc_sc[...] = jnp.zeros_like(acc_sc)

        def update(masked):
            k = k_ref[...]
            v = k[:, 0:KVR]
            if masked:
                row = lax.broadcasted_iota(jnp.int32, (bq, bq), 0)
                col = lax.broadcasted_iota(jnp.int32, (bq, bq), 1)
                keep = col <= row
            s_next = _dot(q_ref[:, 0:KD], k, NT)
            for hh in range(hb):
                s = s_next
                if hh + 1 < hb:
                    s_next = _dot(q_ref[:, (hh + 1) * KD:(hh + 2) * KD], k, NT)
                if masked:
                    s = jnp.where(keep, s, NEG)
                m_prev = m_sc[hh]
                m_next = jnp.maximum(m_prev, jnp.max(s, axis=1)[:, None])
                p = jnp.exp2(s - _lane_tile(m_next, bq))
                a = jnp.exp2(m_prev - m_next)
                l_sc[hh] = a * l_sc[hh] + jnp.sum(p, axis=1)[:, None]
                acc_sc[hh] = _lane_tile(a, KVR) * acc_sc[hh] + _dot(p.astype(BF16), v, NN)
                m_sc[hh] = m_next

        @pl.when(j < i)
        def _():
            update(False)

        @pl.when(j == i)
        def _():
            update(True)
            for hh in range(hb):
                l = l_sc[hh]
                o_ref[:, hh * KVR:(hh + 1) * KVR] = acc_sc[hh] / _lane_tile(l, KVR)
                lset_ref[hh] = (m_sc[hh] + jnp.log2(l)).T[0:1, :]

        if hosting:
            @pl.when((grp == ng - 1) & (st == 0))
            def _():
                ag_forward()

            @pl.when((grp == ng - 1) & (st == len(pairs) - 1))
            def _():
                ag_finish()

    in_specs = [pl.BlockSpec((bq, hb * KD), lambda g, s, it, jt: (it[s], g)),
                pl.BlockSpec((bq, KD), lambda g, s, it, jt: (jt[s], 0))]
    out_specs = [pl.BlockSpec((bq, hb * KVR), lambda g, s, it, jt: (it[s], g)),
                 pl.BlockSpec((hb, 1, bq), lambda g, s, it, jt: (g, 0, it[s]))]
    out_shape = [jax.ShapeDtypeStruct((t, H * KVR), F32), jax.ShapeDtypeStruct((H, 1, t), F32)]
    scratch = [pltpu.VMEM((hb, bq, 128), F32), pltpu.VMEM((hb, bq, 128), F32), pltpu.VMEM((hb, bq, KVR), F32)]
    args = [itab, jtab, qcat, kc]
    if hosting:
        in_specs.append(ANY)
        out_specs.append(ANY)
        out_shape.append(jax.ShapeDtypeStruct((4,) + gather.shape, gather.dtype))
        scratch += AG_SEMS
        args.append(gather)
    gs = pltpu.PrefetchScalarGridSpec(num_scalar_prefetch=2, grid=(ng, len(pairs)), in_specs=in_specs,
                                      out_specs=out_specs, scratch_shapes=scratch)
    return pl.pallas_call(kern, grid_spec=gs, out_shape=out_shape, name="mla_flash_fwd",
                          compiler_params=_cp())(*args)


def _flash_dkv(qcat, kc, do_lat, lse_t, delta_t, bq, hb, exchange=None):
    hosting = exchange is not None
    t = kc.shape[0]
    nq = t // bq
    ng = H // hb
    npairs = nq * (nq + 1) // 2
    steps = [(j, g, i) for j in range(nq) for g in range(ng) for i in range(j, nq)]
    jtab = jnp.asarray(np.array([s[0] for s in steps], np.int32))
    gtab = jnp.asarray(np.array([s[1] for s in steps], np.int32))
    itab = jnp.asarray(np.array([s[2] for s in steps], np.int32))
    ptab = jnp.asarray(np.array([s[2] * (s[2] + 1) // 2 + s[0] for s in steps], np.int32))

    def kern(jt, gt, it, pt, q_ref, k_ref, do_ref, lset_ref, dlt_ref, *rest):
        if hosting:
            p_ref, dk_ref, ds_ref, slots_ref, dk_sc, dv_sc, send_sems, recv_sems = rest
            xc_start, xc_finish = _chip_exchange_schedule(p_ref, slots_ref, send_sems, recv_sems)
        else:
            dk_ref, ds_ref, dk_sc, dv_sc = rest
        st = pl.program_id(0)
        j, g, i = jt[st], gt[st], it[st]

        if hosting:
            @pl.when(st == 0)
            def _():
                xc_start()

        @pl.when((g == 0) & (i == j))
        def _():
            dk_sc[...] = jnp.zeros_like(dk_sc)
            dv_sc[...] = jnp.zeros_like(dv_sc)

        def update(masked):
            k = k_ref[...]
            v = k[:, 0:KVR]
            if masked:
                row = lax.broadcasted_iota(jnp.int32, (bq, bq), 0)
                col = lax.broadcasted_iota(jnp.int32, (bq, bq), 1)
                keep = row <= col

            def first_matmuls(hh):
                dob = do_ref[:, hh * KVR:(hh + 1) * KVR].astype(BF16)
                return _dot(k, q_ref[:, hh * KD:(hh + 1) * KD], NT), _dot(v, dob, NT), dob

            nxt = first_matmuls(0)
            for hh in range(hb):
                s, dp, dob = nxt
                if hh + 1 < hb:
                    nxt = first_matmuls(hh + 1)
                if masked:
                    s = jnp.where(keep, s, NEG)
                p = jnp.exp2(s - lset_ref[hh])
                dv_sc[...] += _dot(p.astype(BF16), dob, NN)
                dsb = (p * (dp - dlt_ref[hh])).astype(BF16)
                ds_ref[0, 0, hh] = dsb
                dk_sc[...] += _dot(dsb, q_ref[:, hh * KD:(hh + 1) * KD], NN)

        @pl.when(i > j)
        def _():
            update(False)

        @pl.when(i == j)
        def _():
            update(True)

        @pl.when((g == ng - 1) & (i == nq - 1))
        def _():
            dk_ref[:, 0:KVR] = dk_sc[:, 0:KVR] * LN2 + dv_sc[...]
            dk_ref[:, KVR:KD] = dk_sc[:, KVR:KD] * LN2

        if hosting:
            @pl.when(st == len(steps) - 1)
            def _():
                xc_finish()

    in_specs = [pl.BlockSpec((bq, hb * KD), lambda s, jt, gt, it, pt: (it[s], gt[s])),
                pl.BlockSpec((bq, KD), lambda s, jt, gt, it, pt: (jt[s], 0)),
                pl.BlockSpec((bq, hb * KVR), lambda s, jt, gt, it, pt: (it[s], gt[s])),
                pl.BlockSpec((hb, 1, bq), lambda s, jt, gt, it, pt: (gt[s], 0, it[s])),
                pl.BlockSpec((hb, 1, bq), lambda s, jt, gt, it, pt: (gt[s], 0, it[s]))]
    out_specs = [pl.BlockSpec((bq, KD), lambda s, jt, gt, it, pt: (jt[s], 0)),
                 pl.BlockSpec((1, 1, hb, bq, bq), lambda s, jt, gt, it, pt: (gt[s], pt[s], 0, 0, 0))]
    out_shape = [jax.ShapeDtypeStruct((t, KD), F32), jax.ShapeDtypeStruct((ng, npairs, hb, bq, bq), BF16)]
    scratch = [pltpu.VMEM((bq, KD), F32), pltpu.VMEM((bq, KVR), F32)]
    args = [jtab, gtab, itab, ptab, qcat, kc, do_lat, lse_t, delta_t]
    if hosting:
        in_specs.append(ANY)
        out_specs.append(ANY)
        out_shape.append(jax.ShapeDtypeStruct(exchange.shape, exchange.dtype))
        scratch += XCHG_SEMS
        args.append(exchange)
    gs = pltpu.PrefetchScalarGridSpec(num_scalar_prefetch=4, grid=(len(steps),), in_specs=in_specs,
                                      out_specs=out_specs, scratch_shapes=scratch)
    return pl.pallas_call(kern, grid_spec=gs, out_shape=out_shape, name="mla_flash_dkv",
                          compiler_params=_cp())(*args)


def _flash_dq(ds_all, kc_t, bq, hb, exchange=None):
    t = kc_t.shape[1]
    nq = t // bq
    ng = H // hb
    pairs = _causal_pairs(nq)
    itab = jnp.asarray(np.array([p[0] for p in pairs], np.int32))
    jtab = jnp.asarray(np.array([p[1] for p in pairs], np.int32))
    hosting = exchange is not None

    def kern(it, jt, ds_ref, kt_ref, *rest):
        if hosting:
            p_ref, dq_ref, slots_ref, acc_sc, send_sems, recv_sems = rest
            xc_start, xc_finish = _chip_exchange_schedule(p_ref, slots_ref, send_sems, recv_sems)
        else:
            dq_ref, acc_sc = rest
        grp = pl.program_id(0)
        st = pl.program_id(1)
        i, j = it[st], jt[st]
        kt = kt_ref[...]

        if hosting:
            @pl.when((grp == 0) & (st == 0))
            def _():
                xc_start()

        @pl.when(j == 0)
        def _():
            for hh in range(hb):
                acc_sc[hh] = _dot(kt, ds_ref[0, 0, hh], NN)

        @pl.when((j > 0) & (j < i))
        def _():
            for hh in range(hb):
                acc_sc[hh] += _dot(kt, ds_ref[0, 0, hh], NN)

        @pl.when(j == i)
        def _():
            for hh in range(hb):
                tot = _dot(kt, ds_ref[0, 0, hh], NN)
                tot = jnp.where(i > 0, tot + acc_sc[hh], tot)
                dq_ref[:, hh * KD:(hh + 1) * KD] = tot.T * MLA_SCALE

        if hosting:
            @pl.when((grp == ng - 1) & (st == len(pairs) - 1))
            def _():
                xc_finish()

    in_specs = [pl.BlockSpec((1, 1, hb, bq, bq), lambda g, s, it, jt: (g, s, 0, 0, 0)),
                pl.BlockSpec((KD, bq), lambda g, s, it, jt: (0, jt[s]))]
    out_specs = [pl.BlockSpec((bq, hb * KD), lambda g, s, it, jt: (it[s], g))]
    out_shape = [jax.ShapeDtypeStruct((t, H * KD), F32)]
    scratch = [pltpu.VMEM((hb, KD, bq), F32)]
    args = [itab, jtab, ds_all, kc_t]
    if hosting:
        in_specs.append(ANY)
        out_specs.append(ANY)
        out_shape.append(jax.ShapeDtypeStruct(exchange.shape, exchange.dtype))
        scratch += XCHG_SEMS
        args.append(exchange)
    gs = pltpu.PrefetchScalarGridSpec(num_scalar_prefetch=2, grid=(ng, len(pairs)), in_specs=in_specs,
                                      out_specs=out_specs, scratch_shapes=scratch)
    outs = pl.pallas_call(kern, grid_spec=gs, out_shape=out_shape, name="mla_flash_dq",
                          compiler_params=_cp())(*args)
    return outs if hosting else outs[0]


def _bucket_table():
    d = np.arange(WIN)
    max_exact = NBKT // 2
    nf = np.maximum(d, 1).astype(np.float32)
    large = max_exact + (np.log(nf / np.float32(max_exact)) / np.float32(math.log(WIN / max_exact))
                         * np.float32(NBKT - max_exact)).astype(np.int32)
    large = np.minimum(large, NBKT - 1)
    bucket = np.where(d < max_exact, d, large).astype(np.int32)
    jj = np.arange(2 * WIN)[:, None]
    ii = np.arange(WIN)[None, :]
    dist = ii + WIN - jj
    valid = (dist >= 0) & (dist < WIN)
    return np.where(valid, bucket[np.clip(dist, 0, WIN - 1)], -1).astype(np.int32)


def _bias_build(rel_bias, bkt):
    def kern(bk_ref, rb_ref, o_ref):
        bk = bk_ref[...]
        for hd in range(QH):
            acc = jnp.full((2 * WIN, WIN), NEG, F32)
            for b in range(NBKT):
                acc = jnp.where(bk == b, rb_ref[b, hd], acc)
            o_ref[hd] = acc

    return pl.pallas_call(
        kern, in_specs=[pl.BlockSpec(memory_space=pltpu.VMEM), pl.BlockSpec(memory_space=pltpu.SMEM)],
        out_specs=pl.BlockSpec(memory_space=pltpu.VMEM),
        out_shape=jax.ShapeDtypeStruct((QH, 2 * WIN, WIN), F32), name="swa_bias_build")(bkt, rel_bias)


def _bias_bwd(dbias, bkt):
    def kern(db_ref, bk_ref, o_ref):
        bk = bk_ref[...]
        for hd in range(QH):
            g = db_ref[hd]
            for b in range(NBKT):
                r = b * QH + hd
                o_ref[r:r + 1, :] = jnp.sum(jnp.where(bk == b, g, 0.0), axis=0, keepdims=True)

    return pl.pallas_call(
        kern, in_specs=[pl.BlockSpec(memory_space=pltpu.VMEM), pl.BlockSpec(memory_space=pltpu.VMEM)],
        out_specs=pl.BlockSpec(memory_space=pltpu.VMEM),
        out_shape=jax.ShapeDtypeStruct((NBKT * QH, WIN), F32), name="swa_bias_bwd")(dbias, bkt)


def _swa_finish_scores(raw, bias, first):
    s = raw * SWA_SCALE + bias
    if first is not None:
        row = lax.broadcasted_iota(jnp.int32, s.shape, 0)
        s = jnp.where(jnp.logical_or(jnp.logical_not(first), row >= WIN), s, NEG)
    return s


def _swa_fwd(qkv_t, bias, sinks, qb):
    t = qkv_t.shape[1]
    w = qb * WIN
    nst = t // w

    def kern(q_ref, kc_ref, kp_ref, vc_ref, vp_ref, b_ref, sk_ref, o_ref, lse_ref):
        n = pl.program_id(0)
        kfull = jnp.concatenate([kp_ref[...], kc_ref[...]], axis=1)
        vfull = jnp.concatenate([vp_ref[...], vc_ref[...]], axis=1)
        head_row = lax.broadcasted_iota(jnp.int32, (QH, WIN), 0)
        groups = [(b, kh) for b in range(qb) for kh in range(KVH)]

        def raw_scores(b, kh):
            k_band = kfull[kh * HD:(kh + 1) * HD, b * WIN:(b + 2) * WIN]
            return [_dot(k_band, q_ref[(kh * G + g) * HD:(kh * G + g + 1) * HD, b * WIN:(b + 1) * WIN], TN)
                    for g in range(G)]

        o_rows = [[] for _ in range(qb)]
        lse_tiles = [jnp.zeros((QH, WIN), F32) for _ in range(qb)]
        nxt_scores = raw_scores(*groups[0])
        for gi, (b, kh) in enumerate(groups):
            scores = nxt_scores
            if gi + 1 < len(groups):
                nxt_scores = raw_scores(*groups[gi + 1])
            v_band = vfull[kh * HD:(kh + 1) * HD, b * WIN:(b + 2) * WIN]
            for g in range(G):
                hd = kh * G + g
                s = _swa_finish_scores(scores[g], b_ref[hd], (n == 0) if b == 0 else None)
                sink = sk_ref[hd]
                m = jnp.maximum(jnp.max(s, axis=0, keepdims=True), sink)
                p = jnp.exp(s - m)
                den = jnp.sum(p, axis=0, keepdims=True) + jnp.exp(sink - m)
                p = p / den
                o_rows[b].append(_dot(v_band, p.astype(BF16), NN))
                lse_tiles[b] = jnp.where(head_row == hd, m + jnp.log(den), lse_tiles[b])
        o_ref[...] = jnp.concatenate([jnp.concatenate(rows, axis=0) for rows in o_rows], axis=1)
        lse_ref[...] = jnp.concatenate(lse_tiles, axis=1)

    prev = lambda r: (lambda n: (r, jnp.maximum(n * qb - 1, 0)))
    return pl.pallas_call(
        kern, grid=(nst,),
        in_specs=[pl.BlockSpec((QH * HD, w), lambda n: (0, n)),
                  pl.BlockSpec((KVH * HD, w), lambda n: (4, n)), pl.BlockSpec((KVH * HD, WIN), prev(4)),
                  pl.BlockSpec((KVH * HD, w), lambda n: (5, n)), pl.BlockSpec((KVH * HD, WIN), prev(5)),
                  pl.BlockSpec((QH, 2 * WIN, WIN), lambda n: (0, 0, 0)),
                  pl.BlockSpec(memory_space=pltpu.SMEM)],
        out_specs=[pl.BlockSpec((QH * HD, w), lambda n: (0, n)), pl.BlockSpec((QH, w), lambda n: (0, n))],
        out_shape=[jax.ShapeDtypeStruct((QH * HD, t), F32), jax.ShapeDtypeStruct((QH, t), F32)],
        name="swa_fwd", compiler_params=_cp())(qkv_t, qkv_t, qkv_t, qkv_t, qkv_t, bias, sinks)


def _swa_bwd(qkv_t, do_t, o_t, lse, bias, sinks, qb):
    t = qkv_t.shape[1]
    w = qb * WIN
    nst = t // w
    nblk = t // WIN

    def kern(q_ref, kc_ref, kp_ref, vc_ref, vp_ref, do_ref, o_ref, lse_ref, qn_ref, don_ref, on_ref, lsen_ref,
             b_ref, sk_ref, dqkv_ref, db_ref, dsk_ref):
        n = pl.program_id(0)

        @pl.when(n == 0)
        def _():
            db_ref[...] = jnp.zeros_like(db_ref)
            dsk_ref[...] = jnp.zeros_like(dsk_ref)

        kfull = jnp.concatenate([kp_ref[...], kc_ref[...]], axis=1)
        vfull = jnp.concatenate([vp_ref[...], vc_ref[...]], axis=1)
        head_row = lax.broadcasted_iota(jnp.int32, (QH, WIN), 0)
        db_acc = [None] * QH
        dsk_tile = jnp.zeros((QH, WIN), F32)
        prev_part = [[[None] * qb for _ in range(KVH)] for _ in range(2)]
        cur_part = [[[None] * qb for _ in range(KVH)] for _ in range(2)]
        groups = [(b, kh) for b in range(qb) for kh in range(KVH)]

        def first_matmuls(b, kh):
            k_band = kfull[kh * HD:(kh + 1) * HD, b * WIN:(b + 2) * WIN]
            v_band = vfull[kh * HD:(kh + 1) * HD, b * WIN:(b + 2) * WIN]
            out = []
            for g in range(G):
                rs = slice((kh * G + g) * HD, (kh * G + g + 1) * HD)
                dob = do_ref[rs, b * WIN:(b + 1) * WIN].astype(BF16)
                out.append((_dot(k_band, q_ref[rs, b * WIN:(b + 1) * WIN], TN), _dot(v_band, dob, TN), dob))
            return out

        dq_rows = [[] for _ in range(qb)]
        nxt_first = first_matmuls(*groups[0])
        for gi, (b, kh) in enumerate(groups):
            first = nxt_first
            if gi + 1 < len(groups):
                nxt_first = first_matmuls(*groups[gi + 1])
            cs = slice(b * WIN, (b + 1) * WIN)
            k_band = kfull[kh * HD:(kh + 1) * HD, b * WIN:(b + 2) * WIN]
            dk_b = dv_b = None
            for g in range(G):
                hd = kh * G + g
                rs = slice(hd * HD, (hd + 1) * HD)
                raw, dp, dob = first[g]
                lse_h = lse_ref[hd:hd + 1, cs]
                s = _swa_finish_scores(raw, b_ref[hd], (n == 0) if b == 0 else None)
                p = jnp.exp(s - lse_h)
                dl = jnp.sum(do_ref[rs, cs] * o_ref[rs, cs], axis=0, keepdims=True)
                ds = p * (dp - dl)
                db_acc[hd] = ds if db_acc[hd] is None else db_acc[hd] + ds
                dsk_tile = jnp.where(head_row == hd, dsk_tile - jnp.exp(sk_ref[hd] - lse_h) * dl, dsk_tile)
                dss = (ds * SWA_SCALE).astype(BF16)
                dq_rows[b].append(_dot(k_band, dss, NN).astype(BF16))
                dk_h = _dot(q_ref[rs, cs], dss, NT)
                dv_h = _dot(dob, p.astype(BF16), NT)
                dk_b = dk_h if dk_b is None else dk_b + dk_h
                dv_b = dv_h if dv_b is None else dv_b + dv_h
            for which, val in ((0, dk_b), (1, dv_b)):
                prev_part[which][kh][b] = val[:, 0:WIN]
                cur_part[which][kh][b] = val[:, WIN:2 * WIN]
        dq_cols = [jnp.concatenate(rows, axis=0) for rows in dq_rows]

        live = n < nst - 1
        ls = slice((qb - 1) * WIN, qb * WIN)
        halo = [[None] * KVH for _ in range(2)]
        for kh in range(KVH):
            k_last = kc_ref[kh * HD:(kh + 1) * HD, ls]
            v_last = vc_ref[kh * HD:(kh + 1) * HD, ls]
            dk_b = dv_b = None
            for g in range(G):
                hd = kh * G + g
                rs = slice(hd * HD, (hd + 1) * HD)
                q_t = qn_ref[rs, :]
                do = don_ref[rs, :]
                s = _dot(k_last, q_t, TN) * SWA_SCALE + b_ref[hd, 0:WIN, :]
                p = jnp.exp(s - lsen_ref[hd:hd + 1, :])
                dob = do.astype(BF16)
                dp = _dot(v_last, dob, TN)
                dl = jnp.sum(do * on_ref[rs, :], axis=0, keepdims=True)
                dss = (p * (dp - dl) * SWA_SCALE).astype(BF16)
                dk_h = _dot(q_t, dss, NT)
                dv_h = _dot(dob, p.astype(BF16), NT)
                dk_b = dk_h if dk_b is None else dk_b + dk_h
                dv_b = dv_h if dv_b is None else dv_b + dv_h
            halo[0][kh] = jnp.where(live, dk_b, 0.0)
            halo[1][kh] = jnp.where(live, dv_b, 0.0)

        kv_rows = []
        for which in range(2):
            for kh in range(KVH):
                blocks = [cur_part[which][kh][p] + (prev_part[which][kh][p + 1] if p + 1 < qb else halo[which][kh])
                          for p in range(qb)]
                kv_rows.append(jnp.concatenate(blocks, axis=1))
        dqkv_ref[...] = jnp.concatenate(
            [jnp.concatenate(dq_cols, axis=1), jnp.concatenate(kv_rows, axis=0).astype(BF16)], axis=0)
        db_ref[...] += jnp.stack(db_acc)
        dsk_ref[...] += dsk_tile

    prev = lambda r: (lambda n: (r, jnp.maximum(n * qb - 1, 0)))
    nxt = lambda n: (0, jnp.minimum((n + 1) * qb, nblk - 1))
    big = lambda: pl.BlockSpec((QH * HD, w), lambda n: (0, n))
    return pl.pallas_call(
        kern, grid=(nst,),
        in_specs=[big(),
                  pl.BlockSpec((KVH * HD, w), lambda n: (4, n)), pl.BlockSpec((KVH * HD, WIN), prev(4)),
                  pl.BlockSpec((KVH * HD, w), lambda n: (5, n)), pl.BlockSpec((KVH * HD, WIN), prev(5)),
                  big(), big(), pl.BlockSpec((QH, w), lambda n: (0, n)),
                  pl.BlockSpec((QH * HD, WIN), nxt), pl.BlockSpec((QH * HD, WIN), nxt),
                  pl.BlockSpec((QH * HD, WIN), nxt), pl.BlockSpec((QH, WIN), nxt),
                  pl.BlockSpec((QH, 2 * WIN, WIN), lambda n: (0, 0, 0)),
                  pl.BlockSpec(memory_space=pltpu.SMEM)],
        out_specs=[pl.BlockSpec(((QH + 2 * KVH) * HD, w), lambda n: (0, n)),
                   pl.BlockSpec((QH, 2 * WIN, WIN), lambda n: (0, 0, 0)),
                   pl.BlockSpec((QH, WIN), lambda n: (0, 0))],
        out_shape=[jax.ShapeDtypeStruct(((QH + 2 * KVH) * HD, t), BF16),
                   jax.ShapeDtypeStruct((QH, 2 * WIN, WIN), F32), jax.ShapeDtypeStruct((QH, WIN), F32)],
        name="swa_bwd", compiler_params=_cp())(
            qkv_t, qkv_t, qkv_t, qkv_t, qkv_t, do_t, o_t, lse, qkv_t, do_t, o_t, lse, bias, sinks)


def _adamw(w, g, m, v, name, tm=544):
    r = w.shape[0]
    tm = r if r % tm else tm
    c1 = 1.0 / (1.0 - B1 ** STEP)
    c2 = 1.0 / (1.0 - B2 ** STEP)

    def kern(w_ref, g_ref, m_ref, v_ref, d_ref, nm_ref, nv_ref):
        g_ = g_ref[...]
        nm = B1 * m_ref[...] + (1.0 - B1) * g_
        nv = B2 * v_ref[...] + (1.0 - B2) * (g_ * g_)
        d_ref[...] = -LR * ((nm * c1) / (jnp.sqrt(nv * c2) + ADAM_EPS) + WD * w_ref[...])
        nm_ref[...] = nm
        nv_ref[...] = nv

    row = pl.BlockSpec((tm, D), lambda i: (i, 0))
    sds = jax.ShapeDtypeStruct((r, D), F32)
    return pl.pallas_call(kern, grid=(r // tm,), in_specs=[row] * 4, out_specs=[row] * 3, out_shape=[sds] * 3,
                          name=name, compiler_params=_cp())(w, g, m, v)


def _mesh_pos():
    return lax.axis_index("x"), lax.axis_index("y"), lax.axis_index("c")


ANY = pl.BlockSpec(memory_space=pl.ANY)


AG_SEMS = [pltpu.SemaphoreType.DMA((6,)), pltpu.SemaphoreType.DMA((6,))]
XCHG_SEMS = [pltpu.SemaphoreType.DMA((3,)), pltpu.SemaphoreType.DMA((3,))]


def _allgather_schedule(w_ref, out_ref, send_sems, recv_sems):
    half = w_ref.shape[0] // 2
    x, y, c = _mesh_pos()
    me, sibling = (x, y, c), (x, y, 1 - c)
    chips = [(1 - x, y), (x, 1 - y), (1 - x, 1 - y)]

    def rows(px, py, pc):
        return out_ref.at[2 * px + py, pl.ds(pc * half, half), :]

    def copy(k, block, to, src=None):
        return pltpu.make_async_remote_copy(
            src_ref=rows(*block) if src is None else src, dst_ref=rows(*block),
            send_sem=send_sems.at[k], recv_sem=recv_sems.at[k], device_id=to, device_id_type=MESH)

    def first():
        return [copy(j, me, (*chip, c), src=w_ref.at[pl.ds(c * half, half), :]) for j, chip in enumerate(chips)]

    def passed():
        return [copy(3 + j, (*chip, c), sibling) for j, chip in enumerate(chips)]

    def start():
        for cp in first():
            cp.start()

    def forward():
        for j, chip in enumerate(chips):
            copy(j, (*chip, c), me).wait_recv()
            passed()[j].start()

    def finish():
        for j, chip in enumerate(chips):
            copy(3 + j, (*chip, 1 - c), me).wait_recv()
        for cp in first() + passed():
            cp.wait_send()

    return start, forward, finish


def _allgather_weights(wpack):
    def body(w_ref, out_ref, send_sems, recv_sems):
        start, forward, finish = _allgather_schedule(w_ref, out_ref, send_sems, recv_sems)
        start()
        forward()
        finish()

    return pl.pallas_call(
        body, out_shape=jax.ShapeDtypeStruct((4,) + wpack.shape, wpack.dtype), in_specs=[ANY], out_specs=ANY,
        scratch_shapes=AG_SEMS, name="allgather_weights")(wpack)


def _row_tile(rows):
    t = min(rows, 512)
    while rows % t or t % 16:
        t -= 16
    return t


def _exchange_core_halves(g, tag):
    half = g.shape[1] // 2

    def body(g_ref, out_ref, send_sem, recv_sem):
        x, y, c = _mesh_pos()
        cp = pltpu.make_async_remote_copy(
            src_ref=g_ref.at[:, pl.ds((1 - c) * half, half), :], dst_ref=out_ref,
            send_sem=send_sem, recv_sem=recv_sem, device_id=(x, y, 1 - c), device_id_type=MESH)
        cp.start()
        cp.wait()

    return pl.pallas_call(
        body, out_shape=jax.ShapeDtypeStruct((4, half, D), g.dtype), in_specs=[ANY], out_specs=ANY,
        scratch_shapes=[pltpu.SemaphoreType.DMA, pltpu.SemaphoreType.DMA], name=f"rs_exchange_cores_{tag}")(g)


def _add_core_halves(g, other, cidx, tag):
    half = other.shape[1]
    tm = _row_tile(half)
    nb = half // tm

    def kern(c_ref, a_ref, b_ref, o_ref):
        o_ref[...] = (a_ref[...] + b_ref[...]).astype(BF16)

    gs = pltpu.PrefetchScalarGridSpec(
        num_scalar_prefetch=1, grid=(4, nb),
        in_specs=[pl.BlockSpec((1, tm, D), lambda s, i, c: (s, c[0] * nb + i, 0)),
                  pl.BlockSpec((1, tm, D), lambda s, i, c: (s, i, 0))],
        out_specs=pl.BlockSpec((1, tm, D), lambda s, i, c: (s, i, 0)))
    return pl.pallas_call(kern, grid_spec=gs, out_shape=jax.ShapeDtypeStruct(other.shape, BF16),
                          name=f"rs_add_cores_{tag}", compiler_params=_cp())(cidx, g, other)


def _chip_exchange_schedule(p_ref, out_ref, send_sems, recv_sems):
    x, y, c = _mesh_pos()
    me = 2 * x + y
    chips = [(1 - x, y), (x, 1 - y), (1 - x, 1 - y)]

    def sends():
        return [pltpu.make_async_remote_copy(
            src_ref=p_ref.at[2 * px + py], dst_ref=out_ref.at[me], send_sem=send_sems.at[j],
            recv_sem=recv_sems.at[j], device_id=(px, py, c), device_id_type=MESH) for j, (px, py) in enumerate(chips)]

    def start():
        for cp in sends():
            cp.start()

    def finish():
        for j, (px, py) in enumerate(chips):
            pltpu.make_async_remote_copy(
                src_ref=p_ref.at[me], dst_ref=out_ref.at[2 * px + py], send_sem=send_sems.at[j],
                recv_sem=recv_sems.at[j], device_id=(px, py, c), device_id_type=MESH).wait_recv()
        for cp in sends():
            cp.wait_send()

    return start, finish


def _exchange_chip_shards(p, tag):
    def body(p_ref, out_ref, send_sems, recv_sems):
        start, finish = _chip_exchange_schedule(p_ref, out_ref, send_sems, recv_sems)
        start()
        finish()

    return pl.pallas_call(
        body, out_shape=jax.ShapeDtypeStruct(p.shape, p.dtype), in_specs=[ANY], out_specs=ANY,
        scratch_shapes=XCHG_SEMS, name=f"rs_exchange_chips_{tag}")(p)


def _sum_slots(slots, p, pos, tag):
    half = slots.shape[1]
    tm = _row_tile(half)
    nb = half // tm

    def kern(pos_ref, p_ref, s1_ref, s2_ref, s3_ref, o_ref):
        o_ref[...] = ((p_ref[0].astype(F32) + s1_ref[0].astype(F32)) + s2_ref[0].astype(F32)) + s3_ref[0].astype(F32)

    def slot(k):
        return pl.BlockSpec((1, tm, D), lambda i, pos: ((pos[0] + k) % 4, i, 0))

    gs = pltpu.PrefetchScalarGridSpec(
        num_scalar_prefetch=1, grid=(nb,), in_specs=[slot(0), slot(1), slot(2), slot(3)],
        out_specs=pl.BlockSpec((tm, D), lambda i, pos: (pos[1] * nb + i, 0)))
    return pl.pallas_call(kern, grid_spec=gs, out_shape=jax.ShapeDtypeStruct((2 * half, D), F32),
                          name=f"rs_sum_chips_{tag}", compiler_params=_cp())(pos, p, slots, slots, slots)


def _join_core_halves(r, tag):
    half = r.shape[0] // 2

    def body(r_ref, out_ref, send_sem, recv_sem):
        x, y, c = _mesh_pos()
        mine = out_ref.at[pl.ds(c * half, half), :]
        cp = pltpu.make_async_remote_copy(
            src_ref=mine, dst_ref=mine, send_sem=send_sem, recv_sem=recv_sem,
            device_id=(x, y, 1 - c), device_id_type=MESH)
        cp.start()
        theirs = out_ref.at[pl.ds((1 - c) * half, half), :]
        pltpu.make_async_remote_copy(
            src_ref=theirs, dst_ref=theirs, send_sem=send_sem, recv_sem=recv_sem,
            device_id=(x, y, 1 - c), device_id_type=MESH).wait_recv()
        cp.wait_send()

    return pl.pallas_call(
        body, out_shape=jax.ShapeDtypeStruct(r.shape, r.dtype), in_specs=[ANY], out_specs=ANY,
        input_output_aliases={0: 0},
        scratch_shapes=[pltpu.SemaphoreType.DMA, pltpu.SemaphoreType.DMA],
        name=f"rs_join_cores_{tag}")(r)


def _allreduce_small(v, name):
    def body(v_ref, out_ref, gat, send_sems, recv_sems):
        x, y, c = _mesh_pos()
        me = 4 * x + 2 * y + c
        gat[me] = v_ref[...]
        sends = []
        for k in range(1, 8):
            peer = (x ^ (k >> 2), y ^ ((k >> 1) & 1), c ^ (k & 1))
            cp = pltpu.make_async_remote_copy(
                src_ref=v_ref, dst_ref=gat.at[me], send_sem=send_sems.at[k - 1], recv_sem=recv_sems.at[k - 1],
                device_id=peer, device_id_type=MESH)
            cp.start()
            sends.append(cp)
        for k in range(1, 8):
            px, py, pc = x ^ (k >> 2), y ^ ((k >> 1) & 1), c ^ (k & 1)
            pltpu.make_async_remote_copy(
                src_ref=v_ref, dst_ref=gat.at[4 * px + 2 * py + pc], send_sem=send_sems.at[k - 1],
                recv_sem=recv_sems.at[k - 1], device_id=(px, py, pc), device_id_type=MESH).wait_recv()
        for cp in sends:
            cp.wait_send()
        acc = gat[0]
        for d in range(1, 8):
            acc = acc + gat[d]
        out_ref[...] = acc

    return pl.pallas_call(
        body, out_shape=jax.ShapeDtypeStruct(v.shape, F32),
        in_specs=[pl.BlockSpec(memory_space=pltpu.VMEM)], out_specs=pl.BlockSpec(memory_space=pltpu.VMEM),
        scratch_shapes=[pltpu.VMEM((8,) + v.shape, F32), pltpu.SemaphoreType.DMA((7,)), pltpu.SemaphoreType.DMA((7,))],
        name=name)(v)


def _mlp_fwd(xb, w_up, w_down, tag):
    u, a = _mm(xb, w_up, "nn", f"mlp_up_{tag}", relu2=True, b_blocked=True)
    return u, a, _mm(a, w_down, "nn", f"mlp_down_{tag}")


def _mlp_bwd(dz, dzb, xb, u, a, w_up, w_down, tag):
    du = _mm(dzb, w_down, "nt", f"mlp_down_dx_{tag}", out_dtype=BF16, gate_u=u)
    dw_down = _mm(a, dzb, "tn", f"mlp_down_dw_{tag}")
    dw_up = _mm(xb, du, "tn", f"mlp_up_dw_{tag}", out_blocked=True)
    dx = _mm(du, w_up, "nt", f"mlp_up_dx_{tag}", addend=dz, add_scale=ALPHA, b_blocked=True)
    return dx, dw_up, dw_down


def _fwd_bwd(x, target, w, dist=None, bq=512, qb=4, hb=4):
    t = x.shape[0]
    bq = min(bq, t)
    qb = min(qb, t // WIN)
    cos, sin = _rope_tables(t)
    bkt = jnp.asarray(_bucket_table())
    w_in = jnp.pad(w[("mla_w_in", None)], ((0, 0), (0, HW - (QR + KVR + ROPE))))
    wuq = w[("mla_w_uq", None)]
    wq2 = jnp.concatenate([wuq[:, :, :NOPE].reshape(QR, H * NOPE),
                           jnp.pad(wuq[:, :, NOPE:], ((0, 0), (0, 0), (0, RP - ROPE))).reshape(QR, H * RP)], axis=1)
    wuk_t = w[("mla_w_uk", None)].transpose(1, 2, 0)
    wuk_h = w[("mla_w_uk", None)].transpose(1, 0, 2)
    wuv_h = w[("mla_w_uv", None)].transpose(1, 0, 2)
    w_o = w[("mla_w_o", None)]
    sinks = w["swa_sinks"].reshape(QH)
    lnp = lambda n, l: w[n][l]
    reduced = {}

    hh = _mm(x, w_in, "nn", "mla_in")
    cq, kc = _mla_pre(hh, w["mla_g_q"], w["mla_g_kv"], cos, sin)
    q2 = _mm(cq, wq2, "nn", "mla_uq")
    qcat = _q_prep(q2, wuk_t, cos, sin)
    if dist is None:
        o_lat, lse0_t = _flash_fwd(qcat, kc, bq, hb)
    else:
        o_lat, lse0_t, wall = _flash_fwd(qcat, kc, bq, hb, gather=dist.late_pack)
        wall = lax.dynamic_update_slice(wall, dist.late_pack[None], (dist.shard, 0, 0))
        w = {**w, **_full_from_gathered(AG_LATE, wall, dist.shard_shapes)}
    wqkv = jnp.concatenate([w[("swa_w_q", None)], w[("kv_w_shared", None)]], axis=1)
    wqkv_t = wqkv.T
    wo_s = w[("swa_w_o", None)]
    o0 = _o_up(o_lat, wuv_h)
    y0 = _mm(o0, w_o, "nn", "mla_out")
    x1, x1b, xh1, r1 = _add_ln(x, y0, lnp("ln_mix_g", 0), lnp("ln_mix_b", 0), "ln_mix_0")
    u0, a0, f0 = _mlp_fwd(x1b, w[("mlp_w_up", 0)], w[("mlp_w_down", 0)], 0)
    x2, x2b, xh2, r2 = _add_ln(x1, f0, lnp("ln_mlp_g", 0), lnp("ln_mlp_b", 0), "ln_mlp_0")
    bias = _bias_build(w["rel_bias"], bkt)
    qkv_t = _mm(x2b, wqkv, "nn", "swa_qkv", out_dtype=BF16, out_t=True)
    os_t, lse1 = _swa_fwd(qkv_t, bias, sinks, qb)
    y1 = _mm(os_t, wo_s, "tn", "swa_out")
    x3, x3b, xh3, r3 = _add_ln(x2, y1, lnp("ln_mix_g", 1), lnp("ln_mix_b", 1), "ln_mix_1")
    u1, a1, f1 = _mlp_fwd(x3b, w[("mlp_w_up", 1)], w[("mlp_w_down", 1)], 1)
    x4, _, xh4, r4 = _add_ln(x3, f1, lnp("ln_mlp_g", 1), lnp("ln_mlp_b", 1), "ln_mlp_1")
    dx4, lpart = _loss_grad(x4, target)

    g = {}
    dz4, dz4b, dg_mlp1, db_mlp1 = _ln_bwd(dx4, xh4, r4, lnp("ln_mlp_g", 1), "ln_mlp_1_bwd")
    dx3, g[("mlp_w_up", 1)], g[("mlp_w_down", 1)] = _mlp_bwd(
        dz4, dz4b, x3b, u1, a1, w[("mlp_w_up", 1)], w[("mlp_w_down", 1)], 1)
    part1 = _rs_chip_partials(RS_MLP1, g, dist, "mlp1") if dist is not None else None
    dz3, dz3b, dg_mix1, db_mix1 = _ln_bwd(dx3, xh3, r3, lnp("ln_mix_g", 1), "ln_mix_1_bwd")
    dos_t = _mm(dz3b, wo_s, "nt", "swa_out_dx", out_t=True)
    g[("swa_w_o", None)] = _mm(os_t, dz3b, "nn", "swa_out_dw")
    dqkv_t, dbias, dsk = _swa_bwd(qkv_t, dos_t, os_t, lse1, bias, sinks, qb)
    dwqkv = _mm(dqkv_t, x2b, "nn", "swa_qkv_dw").T
    g[("swa_w_q", None)], g[("kv_w_shared", None)] = dwqkv[:, :QH * HD], dwqkv[:, QH * HD:]
    dx2 = _mm(dqkv_t, wqkv_t, "tn", "swa_qkv_dx", addend=dz3, add_scale=ALPHA)
    g["rel_bias"] = jnp.sum(_bias_bwd(dbias, bkt), axis=-1).reshape(NBKT, QH)
    g["swa_sinks"] = jnp.sum(dsk, axis=-1).reshape(1, QH)
    dz2, dz2b, dg_mlp0, db_mlp0 = _ln_bwd(dx2, xh2, r2, lnp("ln_mlp_g", 0), "ln_mlp_0_bwd")
    dx1, g[("mlp_w_up", 0)], g[("mlp_w_down", 0)] = _mlp_bwd(
        dz2, dz2b, x1b, u0, a0, w[("mlp_w_up", 0)], w[("mlp_w_down", 0)], 0)
    part0 = _rs_chip_partials(RS_MLP0, g, dist, "mlp0") if dist is not None else None
    dz1, dz1b, dg_mix0, db_mix0 = _ln_bwd(dx1, xh1, r1, lnp("ln_mix_g", 0), "ln_mix_0_bwd")
    do0 = _mm(dz1b, w_o, "nt", "mla_out_dx", out_dtype=BF16)
    g[("mla_w_o", None)] = _mm(o0, dz1b, "tn", "mla_out_dw")
    do_lat, dwuv, delta_t = _o_up_bwd(do0, o_lat, wuv_h)
    g[("mla_w_uv", None)] = dwuv.transpose(1, 0, 2)
    if dist is None:
        dk, ds_all = _flash_dkv(qcat, kc, do_lat, lse0_t, delta_t, bq, hb)
        dq_cat = _flash_dq(ds_all, kc.T, bq, hb)
    else:
        dk, ds_all, slots1 = _flash_dkv(qcat, kc, do_lat, lse0_t, delta_t, bq, hb, exchange=part1)
        dq_cat, slots0 = _flash_dq(ds_all, kc.T, bq, hb, exchange=part0)
        reduced["mlp1"] = _rs_finish(part1, slots1, dist, "mlp1")
        reduced["mlp0"] = _rs_finish(part0, slots0, dist, "mlp0")
    dq2, dwuk = _q_prep_bwd(dq_cat, q2, wuk_h, cos, sin)
    g[("mla_w_uk", None)] = dwuk.transpose(2, 0, 1)
    dcq = _mm(dq2, wq2, "nt", "mla_uq_dx")
    dwq2 = _mm(cq, dq2, "tn", "mla_uq_dw")
    g[("mla_w_uq", None)] = jnp.concatenate([dwq2[:, :H * NOPE].reshape(QR, H, NOPE),
                                             dwq2[:, H * NOPE:].reshape(QR, H, RP)[:, :, :ROPE]], axis=2)
    dh, dgq, dgkv = _mla_pre_bwd(hh, dcq, dk, w["mla_g_q"], w["mla_g_kv"], cos, sin)
    g[("mla_w_in", None)] = _mm(x, dh, "tn", "mla_in_dw")[:, :QR + KVR + ROPE]
    grad_x = _mm(dh, w_in, "nt", "mla_in_dx", addend=dz1, add_scale=ALPHA)
    g["mla_g_q"], g["mla_g_kv"] = dgq, dgkv
    g["ln_mix_g"] = jnp.concatenate([dg_mix0, dg_mix1], axis=0)
    g["ln_mix_b"] = jnp.concatenate([db_mix0, db_mix1], axis=0)
    g["ln_mlp_g"] = jnp.concatenate([dg_mlp0, dg_mlp1], axis=0)
    g["ln_mlp_b"] = jnp.concatenate([db_mlp0, db_mlp1], axis=0)
    return lpart, grad_x, g, reduced


def _rows(a):
    return a.reshape(-1, D)


def _piece(a, layer):
    return _rows(a if layer is None else a[layer])


def _pack_group(group, parts):
    return jnp.concatenate([_piece(parts[n], l) for n, l in group], axis=0)


def _unpack_group(group, buf, like):
    out, off = {}, 0
    for n, l in group:
        shp = like[n].shape if l is None else like[n].shape[1:]
        out[(n, l)] = buf[off:off + ROWS[n]].reshape(shp)
        off += ROWS[n]
    return out


def _by_name(pieces):
    out = {n: a for (n, l), a in pieces.items() if l is None}
    for n in {n for (n, l) in pieces if l is not None}:
        out[n] = jnp.stack([pieces[(n, 0)], pieces[(n, 1)]])
    return out


def _full_from_gathered(group, wall, shard_shapes):
    out, off = {}, 0
    for n, l in group:
        sl = wall[:, off:off + ROWS[n]]
        off += ROWS[n]
        shp = tuple(shard_shapes[n])
        if n == "mlp_w_up":
            out[(n, l)] = sl
        elif n == "mlp_w_down":
            out[(n, l)] = sl.reshape(DFF, D)
        elif n == "kv_w_shared":
            out[(n, l)] = sl.reshape((4 * shp[0],) + shp[1:])
        else:
            out[(n, l)] = sl.reshape((4 * shp[1],) + shp[2:])
    return out


def _grad_shards(group, g):
    return jnp.concatenate([g[(n, l)].reshape(4, ROWS[n], D) for n, l in group], axis=1)


def _rs_chip_partials(group, g, dist, tag):
    gsh = _grad_shards(group, g)
    return _add_core_halves(gsh, _exchange_core_halves(gsh, tag), dist.cidx, tag)


def _rs_finish(part, slots, dist, tag):
    return _join_core_halves(_sum_slots(slots, part, dist.pos, tag), tag)


SMALL = (("ln_mix_g", 0, 2), ("ln_mix_b", 2, 2), ("ln_mlp_g", 4, 2), ("ln_mlp_b", 6, 2),
         ("swa_sinks", 8, 1), ("mla_g_q", 9, 1), ("mla_g_kv", 10, 1), ("rel_bias", 11, 1))


def _pack_small(parts):
    rows = []
    for n, _, nr in SMALL:
        a = parts[n].reshape(nr, -1).astype(F32)
        rows.append(jnp.pad(a, ((0, 0), (0, D - a.shape[1]))))
    rows.append(jnp.zeros((SMALL_ROWS - 12, D), F32))
    return jnp.concatenate(rows, axis=0)


def _unpack_small(buf, like):
    out = {}
    for n, r0, nr in SMALL:
        size = like[n].size // nr
        out[n] = buf[r0:r0 + nr, :size].reshape(like[n].shape)
    return out


def kernel(x, mla_w_in, mla_g_q, mla_g_kv, mla_w_uq, mla_w_uk, mla_w_uv, mla_w_o, kv_w_shared, swa_w_q, swa_sinks, swa_w_o, rel_bias, mlp_w_up, mlp_w_down, ln_mix_g, ln_mix_b, ln_mlp_g, ln_mlp_b, loss_target, m_mla_w_in, m_mla_g_q, m_mla_g_kv, m_mla_w_uq, m_mla_w_uk, m_mla_w_uv, m_mla_w_o, m_kv_w_shared, m_swa_w_q, m_swa_sinks, m_swa_w_o, m_rel_bias, m_mlp_w_up, m_mlp_w_down, m_ln_mix_g, m_ln_mix_b, m_ln_mlp_g, m_ln_mlp_b, v_mla_w_in, v_mla_g_q, v_mla_g_kv, v_mla_w_uq, v_mla_w_uk, v_mla_w_uv, v_mla_w_o, v_kv_w_shared, v_swa_w_q, v_swa_sinks, v_swa_w_o, v_rel_bias, v_mlp_w_up, v_mlp_w_down, v_ln_mix_g, v_ln_mix_b, v_ln_mlp_g, v_ln_mlp_b):
    names = ["mla_w_in", "mla_g_q", "mla_g_kv", "mla_w_uq", "mla_w_uk", "mla_w_uv", "mla_w_o", "kv_w_shared",
             "swa_w_q", "swa_sinks", "swa_w_o", "rel_bias", "mlp_w_up", "mlp_w_down",
             "ln_mix_g", "ln_mix_b", "ln_mlp_g", "ln_mlp_b"]
    ws = dict(zip(names, [mla_w_in, mla_g_q, mla_g_kv, mla_w_uq, mla_w_uk, mla_w_uv, mla_w_o, kv_w_shared,
                          swa_w_q, swa_sinks, swa_w_o, rel_bias, mlp_w_up, mlp_w_down,
                          ln_mix_g, ln_mix_b, ln_mlp_g, ln_mlp_b]))
    ms = dict(zip(names, [m_mla_w_in, m_mla_g_q, m_mla_g_kv, m_mla_w_uq, m_mla_w_uk, m_mla_w_uv, m_mla_w_o,
                          m_kv_w_shared, m_swa_w_q, m_swa_sinks, m_swa_w_o, m_rel_bias, m_mlp_w_up, m_mlp_w_down,
                          m_ln_mix_g, m_ln_mix_b, m_ln_mlp_g, m_ln_mlp_b]))
    vs = dict(zip(names, [v_mla_w_in, v_mla_g_q, v_mla_g_kv, v_mla_w_uq, v_mla_w_uk, v_mla_w_uv, v_mla_w_o,
                          v_kv_w_shared, v_swa_w_q, v_swa_sinks, v_swa_w_o, v_rel_bias, v_mlp_w_up, v_mlp_w_down,
                          v_ln_mix_g, v_ln_mix_b, v_ln_mlp_g, v_ln_mlp_b]))
    xi, yi, ci = _mesh_pos()
    shard = 2 * xi + yi
    shard_shapes = {n: ws[n].shape for n in ROWS}
    wbf = {n: ws[n].astype(BF16) for n in ROWS}

    early = _pack_group(AG_EARLY, wbf)
    wall = lax.dynamic_update_slice(_allgather_weights(early), early[None], (shard, 0, 0))
    w = _full_from_gathered(AG_EARLY, wall, shard_shapes)
    dist = _Dist(shard=shard, cidx=jnp.reshape(ci, (1,)).astype(jnp.int32),
                 pos=jnp.stack([shard, ci]).astype(jnp.int32), late_pack=_pack_group(AG_LATE, wbf),
                 shard_shapes=shard_shapes)
    gq_slot = lax.dynamic_update_slice(jnp.zeros((1, QR), F32), mla_g_q, (0, shard * (QR // 4)))
    gkv_slot = lax.dynamic_update_slice(jnp.zeros((1, KVR), F32), mla_g_kv, (0, shard * (KVR // 4)))
    gains = jnp.concatenate([jnp.pad(gq_slot, ((0, 0), (0, D - QR))), jnp.pad(gkv_slot, ((0, 0), (0, D - KVR))),
                             jnp.zeros((SMALL_ROWS - 2, D), F32)], axis=0)
    gains = _allreduce_small(gains * 0.5, "allgather_gains")
    w["mla_g_q"], w["mla_g_kv"] = gains[0, :QR], gains[1, :KVR]
    for n in ("swa_sinks", "rel_bias", "ln_mix_g", "ln_mix_b", "ln_mlp_g", "ln_mlp_b"):
        w[n] = ws[n]

    lpart, grad_x, g, reduced = _fwd_bwd(x[0], loss_target[0], w, dist)
    loss = lax.psum(0.5 * jnp.sum(lpart) / D, ("x", "y", "c"))

    part = _rs_chip_partials(RS_REST, g, dist, "rest")
    reduced["rest"] = _rs_finish(part, _exchange_chip_shards(part, "rest"), dist, "rest")

    small_like = {n: g[n] for n, _, _ in SMALL}
    gsm = _unpack_small(_allreduce_small(_pack_small(g), "allreduce_small_grads"), small_like)
    gsm["mla_g_q"] = lax.dynamic_slice(gsm["mla_g_q"], (0, shard * (QR // 4)), (1, QR // 4))
    gsm["mla_g_kv"] = lax.dynamic_slice(gsm["mla_g_kv"], (0, shard * (KVR // 4)), (1, KVR // 4))

    pieces = [{}, {}, {}, {}]
    for key, group in (("mlp1", RS_MLP1), ("mlp0", RS_MLP0), ("rest", RS_REST)):
        outs = _adamw(_pack_group(group, ws), reduced[key], _pack_group(group, ms), _pack_group(group, vs),
                      f"adamw_{key}", tm=_row_tile(reduced[key].shape[0]))
        for dst, buf in zip(pieces, (reduced[key], *outs)):
            dst.update(_unpack_group(group, buf, ws))
    gbig, dbig, mbig, vbig = [_by_name(p) for p in pieces]
    dsm, msm, vsm = _adamw(_pack_small(ws), _pack_small(gsm), _pack_small(ms), _pack_small(vs), "adamw_small", tm=16)
    grads = {**gbig, **gsm}
    delta = {**dbig, **_unpack_small(dsm, ws)}
    new_m = {**mbig, **_unpack_small(msm, ws)}
    new_v = {**vbig, **_unpack_small(vsm, ws)}
    grads = {n: grads[n].reshape(ws[n].shape) for n in names}
    return (loss, grad_x[None], *[grads[n] for n in names], *[delta[n] for n in names],
            *[new_m[n] for n in names], *[new_v[n] for n in names])
```

```python
import collections
import math

import numpy as np
import jax
import jax.numpy as jnp
from jax import lax
from jax.experimental import pallas as pl
from jax.experimental.pallas import tpu as pltpu

F32 = jnp.float32
BF16 = jnp.bfloat16
MESH = pl.DeviceIdType.MESH

D = 1024
DFF = 4096
H = 8
NOPE = 128
ROPE = 64
QR = 384
KVR = 256
RP = 128
KD = KVR + RP
HW = 768
QH = 16
KVH = 4
HD = 64
G = QH // KVH
WIN = 128
NBKT = 32
ALPHA = 4.0 ** 0.25
LN_EPS = 1e-5
RMS_EPS = 1e-6
MLA_SCALE = (NOPE + ROPE) ** -0.5
LOG2E = 1.4426950408889634
LN2 = 0.6931471805599453
QSCALE = MLA_SCALE * LOG2E
SWA_SCALE = HD ** -0.5
NEG = -1e30
LR, B1, B2, ADAM_EPS, WD, STEP = 0.001, 0.9, 0.999, 1e-8, 0.01, 10

VMEM_LIMIT = 48 * 1024 * 1024

NN = (((1,), (0,)), ((), ()))
NT = (((1,), (1,)), ((), ()))
TN = (((0,), (0,)), ((), ()))

ROWS = {"mlp_w_up": 1024, "mlp_w_down": 1024, "mla_w_o": 256, "swa_w_q": 256, "swa_w_o": 256,
        "kv_w_shared": 128, "mla_w_in": 176, "mla_w_uq": 144, "mla_w_uk": 64, "mla_w_uv": 64}
AG_EARLY = (("mla_w_in", None), ("mla_w_uq", None), ("mla_w_uk", None), ("mla_w_uv", None), ("mla_w_o", None))
AG_LATE = (("mlp_w_up", 0), ("mlp_w_up", 1), ("mlp_w_down", 0), ("mlp_w_down", 1),
           ("swa_w_q", None), ("swa_w_o", None), ("kv_w_shared", None))
RS_MLP1 = (("mlp_w_up", 1), ("mlp_w_down", 1))
RS_MLP0 = (("mlp_w_up", 0), ("mlp_w_down", 0))
RS_REST = (("mla_w_o", None), ("swa_w_q", None), ("swa_w_o", None), ("kv_w_shared", None),
           ("mla_w_in", None), ("mla_w_uq", None), ("mla_w_uk", None), ("mla_w_uv", None))
SMALL_ROWS = 16
_Dist = collections.namedtuple("_Dist", "shard cidx pos late_pack shard_shapes")


def _cp(**kw):
    return pltpu.CompilerParams(vmem_limit_bytes=VMEM_LIMIT, **kw)


def _tile(n, pref):
    t = min(n, pref)
    while n % t:
        t -= 128
    return t


def _dot(a, b, dims):
    return lax.dot_general(a, b, dims, preferred_element_type=F32)


def _mm(a, b, mode, name, out_dtype=F32, out_t=False, addend=None, add_scale=1.0, relu2=False, gate_a=None,
        b_blocked=False, out_blocked=False, tm=1024, tn=1024, tk=1024):
    if b_blocked:
        nb, brow, bcol = b.shape
        bshape = (brow, nb * bcol)
    else:
        bshape = b.shape
    if mode == "nn":
        (m, k), (k2, n) = a.shape, bshape
    elif mode == "nt":
        (m, k), (n, k2) = a.shape, bshape
    else:
        (k, m), (k2, n) = a.shape, bshape
    assert k == k2, (name, a.shape, b.shape)
    tm, tn, tk = _tile(m, tm), _tile(n, tn), _tile(k, tk)
    nk = k // tk
    dims = {"nn": NN, "nt": NT, "tn": TN}[mode]
    if mode == "tn":
        a_spec = pl.BlockSpec((tk, tm), lambda i, j, kk: (kk, i))
    else:
        a_spec = pl.BlockSpec((tm, tk), lambda i, j, kk: (i, kk))
    if b_blocked and mode == "nn":
        assert tn == b.shape[2]
        b_spec = pl.BlockSpec((None, tk, tn), lambda i, j, kk: (j, kk, 0))
    elif b_blocked and mode == "nt":
        assert tk == b.shape[2]
        b_spec = pl.BlockSpec((None, tn, tk), lambda i, j, kk: (kk, j, 0))
    elif mode == "nt":
        b_spec = pl.BlockSpec((tn, tk), lambda i, j, kk: (j, kk))
    else:
        assert not b_blocked
        b_spec = pl.BlockSpec((tk, tn), lambda i, j, kk: (kk, j))
    mn_spec = pl.BlockSpec((tm, tn), lambda i, j, kk: (i, j))
    ins, in_specs = [a, b], [a_spec, b_spec]
    if addend is not None:
        ins.append(addend)
        in_specs.append(mn_spec)
    if gate_a is not None:
        ins.append(gate_a)
        in_specs.append(mn_spec)
    if out_blocked:
        assert not out_t
        out_shape = [jax.ShapeDtypeStruct((n // tn, m, tn), out_dtype)]
        out_specs = [pl.BlockSpec((None, tm, tn), lambda i, j, kk: (j, i, 0))]
    elif out_t:
        out_shape = [jax.ShapeDtypeStruct((n, m), out_dtype)]
        out_specs = [pl.BlockSpec((tn, tm), lambda i, j, kk: (j, i))]
    else:
        out_shape = [jax.ShapeDtypeStruct((m, n), out_dtype)]
        out_specs = [mn_spec]
    has_add, has_gate = addend is not None, gate_a is not None

    def kern(*refs):
        a_ref, b_ref = refs[0], refs[1]
        pos = 2
        add_ref = gate_ref = None
        if has_add:
            add_ref = refs[pos]
            pos += 1
        if has_gate:
            gate_ref = refs[pos]
            pos += 1
        o_ref = refs[pos]
        acc = refs[-1] if nk > 1 else None
        kk = pl.program_id(2)

        def partial():
            return _dot(a_ref[...].astype(BF16), b_ref[...].astype(BF16), dims)

        if nk > 1:
            @pl.when(kk == 0)
            def _():
                acc[...] = partial()

            @pl.when((kk > 0) & (kk < nk - 1))
            def _():
                acc[...] += partial()

        @pl.when(kk == nk - 1)
        def _():
            r = partial() + acc[...] if nk > 1 else partial()
            if has_add:
                r = r + add_scale * add_ref[...].astype(F32)
            if has_gate:
                r = r * (2.0 * jnp.sqrt(gate_ref[...].astype(F32)))
            if relu2:
                hh = jnp.maximum(r, 0.0)
                r = hh * hh
            if out_t:
                r = r.T
            o_ref[...] = r.astype(out_dtype)

    return pl.pallas_call(
        kern, out_shape=out_shape, grid=(m // tm, n // tn, nk), in_specs=in_specs, out_specs=out_specs,
        scratch_shapes=[pltpu.VMEM((tm, tn), F32)] if nk > 1 else [], name=name, compiler_params=_cp())(*ins)[0]


def _add_ln(res, y, g, b, name, res_affine=None, tm=256):
    t = res.shape[0]
    tm = min(tm, t)
    affine = res_affine is not None

    def kern(*refs):
        if affine:
            x_ref, y_ref, g_ref, b_ref, g0_ref, b0_ref, ob_ref, xh_ref, r_ref = refs
            x = x_ref[...] * g0_ref[...] + b0_ref[...]
        else:
            x_ref, y_ref, g_ref, b_ref, ob_ref, xh_ref, r_ref = refs
            x = x_ref[...]
        z = ALPHA * x + y_ref[...]
        mu = jnp.mean(z, axis=-1, keepdims=True)
        zc = z - mu
        var = jnp.mean(zc * zc, axis=-1, keepdims=True)
        r = lax.rsqrt(var + LN_EPS)
        xh = zc * r
        ob_ref[...] = (xh * g_ref[...] + b_ref[...]).astype(BF16)
        xh_ref[...] = xh
        r_ref[...] = r

    row = pl.BlockSpec((tm, D), lambda i: (i, 0))
    vec = pl.BlockSpec((1, D), lambda i: (0, 0))
    st = pl.BlockSpec((tm, 1), lambda i: (i, 0))
    ins = [res, y, g.reshape(1, D), b.reshape(1, D)]
    if affine:
        ins += [res_affine[0].reshape(1, D), res_affine[1].reshape(1, D)]
    return pl.pallas_call(
        kern, grid=(t // tm,), in_specs=[row, row] + [vec] * (len(ins) - 2), out_specs=[row, row, st],
        out_shape=[jax.ShapeDtypeStruct((t, D), BF16), jax.ShapeDtypeStruct((t, D), F32),
                   jax.ShapeDtypeStruct((t, 1), F32)],
        name=name, compiler_params=_cp())(*ins)


def _ln_bwd(dout, xhat, rstd, g, name, loss_b=None, tm=256):
    t = dout.shape[0]
    tm = min(tm, t)
    head = loss_b is not None

    def kern(*refs):
        if head:
            do_ref, xh_ref, r_ref, g_ref, b_ref, dz_ref, dzb_ref, dg_ref, db_ref, l_ref = refs
        else:
            do_ref, xh_ref, r_ref, g_ref, dz_ref, dzb_ref, dg_ref, db_ref = refs

        @pl.when(pl.program_id(0) == 0)
        def _():
            dg_ref[...] = jnp.zeros_like(dg_ref)
            db_ref[...] = jnp.zeros_like(db_ref)
            if head:
                l_ref[...] = jnp.zeros_like(l_ref)

        xh = xh_ref[...]
        if head:
            e = xh * g_ref[...] + b_ref[...] - do_ref[...]
            l_ref[...] += jnp.sum(e * e, axis=0, keepdims=True)
            do = e * (1.0 / D)
        else:
            do = do_ref[...]
        dxh = do * g_ref[...]
        m1 = jnp.mean(dxh, axis=-1, keepdims=True)
        m2 = jnp.mean(dxh * xh, axis=-1, keepdims=True)
        dz = r_ref[...] * (dxh - m1 - xh * m2)
        dz_ref[...] = dz
        dzb_ref[...] = dz.astype(BF16)
        dg_ref[...] += jnp.sum(do * xh, axis=0, keepdims=True)
        db_ref[...] += jnp.sum(do, axis=0, keepdims=True)

    row = pl.BlockSpec((tm, D), lambda i: (i, 0))
    vec = pl.BlockSpec((1, D), lambda i: (0, 0))
    st = pl.BlockSpec((tm, 1), lambda i: (i, 0))
    ins = [dout, xhat, rstd, g.reshape(1, D)] + ([loss_b.reshape(1, D)] if head else [])
    return pl.pallas_call(
        kern, grid=(t // tm,), in_specs=[row, row, st] + [vec] * (len(ins) - 3),
        out_specs=[row, row, vec, vec] + ([vec] if head else []),
        out_shape=[jax.ShapeDtypeStruct((t, D), F32), jax.ShapeDtypeStruct((t, D), BF16)]
        + [jax.ShapeDtypeStruct((1, D), F32)] * (3 if head else 2),
        name=name, compiler_params=_cp())(*ins)


def _rope_tables(t):
    half = ROPE // 2
    inv = 10000.0 ** (-jnp.arange(half, dtype=F32) / half)
    ang = jnp.arange(t).astype(F32)[:, None] * inv[None, :]
    cos, sin = jnp.cos(ang), jnp.sin(ang)
    z = jnp.zeros((t, RP - ROPE), F32)
    return jnp.concatenate([cos, cos, z], axis=1), jnp.concatenate([-sin, sin, z], axis=1)


def _swap_halves(x):
    lane = lax.broadcasted_iota(jnp.int32, x.shape, 1)
    return jnp.where(lane < ROPE // 2, pltpu.roll(x, RP - ROPE // 2, 1), pltpu.roll(x, ROPE // 2, 1))


def _rope(x, cos, sin):
    return x * cos + _swap_halves(x) * sin


def _rope_t(gy, cos, sin):
    return gy * cos + _swap_halves(gy * sin)


def _mla_pre(hh, g_q, g_kv, cos, sin, tm=256):
    t = hh.shape[0]
    tm = min(tm, t)

    def kern(h_ref, gq_ref, gkv_ref, c_ref, s_ref, cq_ref, k_ref):
        xq = h_ref[:, 0:QR]
        rq = lax.rsqrt(jnp.mean(xq * xq, axis=-1, keepdims=True) + RMS_EPS)
        cq_ref[...] = (xq * rq * gq_ref[...]).astype(BF16)
        xk = h_ref[:, QR:QR + KVR]
        rk = lax.rsqrt(jnp.mean(xk * xk, axis=-1, keepdims=True) + RMS_EPS)
        k_ref[:, 0:KVR] = (xk * rk * gkv_ref[...]).astype(BF16)
        k_ref[:, KVR:KD] = _rope(h_ref[:, QR + KVR:HW], c_ref[...], s_ref[...]).astype(BF16)

    return pl.pallas_call(
        kern, grid=(t // tm,),
        in_specs=[pl.BlockSpec((tm, HW), lambda i: (i, 0)), pl.BlockSpec((1, QR), lambda i: (0, 0)),
                  pl.BlockSpec((1, KVR), lambda i: (0, 0)), pl.BlockSpec((tm, RP), lambda i: (i, 0)),
                  pl.BlockSpec((tm, RP), lambda i: (i, 0))],
        out_specs=[pl.BlockSpec((tm, QR), lambda i: (i, 0)), pl.BlockSpec((tm, KD), lambda i: (i, 0))],
        out_shape=[jax.ShapeDtypeStruct((t, QR), BF16), jax.ShapeDtypeStruct((t, KD), BF16)],
        name="mla_pre", compiler_params=_cp())(hh, g_q.reshape(1, QR), g_kv.reshape(1, KVR), cos, sin)


def _mla_pre_bwd(hh, dcq, dk, g_q, g_kv, cos, sin, tm=256):
    t = hh.shape[0]
    tm = min(tm, t)

    def rms_bwd(x, dy, g):
        r = lax.rsqrt(jnp.mean(x * x, axis=-1, keepdims=True) + RMS_EPS)
        gdy = dy * g
        dx = r * gdy - x * (r * r * r) * jnp.mean(gdy * x, axis=-1, keepdims=True)
        return dx, jnp.sum(dy * x * r, axis=0, keepdims=True)

    def kern(h_ref, dcq_ref, dk_ref, gq_ref, gkv_ref, c_ref, s_ref, dh_ref, dgq_ref, dgkv_ref):
        @pl.when(pl.program_id(0) == 0)
        def _():
            dgq_ref[...] = jnp.zeros_like(dgq_ref)
            dgkv_ref[...] = jnp.zeros_like(dgkv_ref)

        dxq, dgq = rms_bwd(h_ref[:, 0:QR], dcq_ref[...], gq_ref[...])
        dxk, dgk = rms_bwd(h_ref[:, QR:QR + KVR], dk_ref[:, 0:KVR], gkv_ref[...])
        dh_ref[:, 0:QR] = dxq.astype(BF16)
        dh_ref[:, QR:QR + KVR] = dxk.astype(BF16)
        dh_ref[:, QR + KVR:HW] = _rope_t(dk_ref[:, KVR:KD], c_ref[...], s_ref[...]).astype(BF16)
        dgq_ref[...] += dgq
        dgkv_ref[...] += dgk

    return pl.pallas_call(
        kern, grid=(t // tm,),
        in_specs=[pl.BlockSpec((tm, HW), lambda i: (i, 0)), pl.BlockSpec((tm, QR), lambda i: (i, 0)),
                  pl.BlockSpec((tm, KD), lambda i: (i, 0)), pl.BlockSpec((1, QR), lambda i: (0, 0)),
                  pl.BlockSpec((1, KVR), lambda i: (0, 0)), pl.BlockSpec((tm, RP), lambda i: (i, 0)),
                  pl.BlockSpec((tm, RP), lambda i: (i, 0))],
        out_specs=[pl.BlockSpec((tm, HW), lambda i: (i, 0)), pl.BlockSpec((1, QR), lambda i: (0, 0)),
                   pl.BlockSpec((1, KVR), lambda i: (0, 0))],
        out_shape=[jax.ShapeDtypeStruct((t, HW), BF16), jax.ShapeDtypeStruct((1, QR), F32),
                   jax.ShapeDtypeStruct((1, KVR), F32)],
        name="mla_pre_bwd", compiler_params=_cp())(hh, dcq, dk, g_q.reshape(1, QR), g_kv.reshape(1, KVR), cos, sin)


def _q_prep(q2, wuk_t, cos, sin, tm=256):
    t = q2.shape[0]
    tm = min(tm, t)

    def kern(q_ref, w_ref, c_ref, s_ref, o_ref):
        cos_, sin_ = c_ref[...], s_ref[...]
        for h in range(H):
            qn = q_ref[:, h * NOPE:(h + 1) * NOPE].astype(BF16)
            o_ref[:, h * KD:h * KD + KVR] = (_dot(qn, w_ref[h], NN) * QSCALE).astype(BF16)
            qr = q_ref[:, H * NOPE + h * RP:H * NOPE + (h + 1) * RP]
            o_ref[:, h * KD + KVR:(h + 1) * KD] = (_rope(qr, cos_, sin_) * QSCALE).astype(BF16)

    return pl.pallas_call(
        kern, grid=(t // tm,),
        in_specs=[pl.BlockSpec((tm, 2 * H * NOPE), lambda i: (i, 0)), pl.BlockSpec((H, NOPE, KVR), lambda i: (0, 0, 0)),
                  pl.BlockSpec((tm, RP), lambda i: (i, 0)), pl.BlockSpec((tm, RP), lambda i: (i, 0))],
        out_specs=pl.BlockSpec((tm, H * KD), lambda i: (i, 0)),
        out_shape=jax.ShapeDtypeStruct((t, H * KD), BF16),
        name="q_prep", compiler_params=_cp())(q2, wuk_t, cos, sin)


def _q_prep_bwd(dq_cat, q2, wuk_h, cos, sin, tm=256):
    t = q2.shape[0]
    tm = min(tm, t)

    def kern(dq_ref, q_ref, w_ref, c_ref, s_ref, o_ref, dw_ref):
        @pl.when(pl.program_id(0) == 0)
        def _():
            dw_ref[...] = jnp.zeros_like(dw_ref)

        cos_, sin_ = c_ref[...], s_ref[...]
        for h in range(H):
            dql = dq_ref[:, h * KD:h * KD + KVR].astype(BF16)
            o_ref[:, h * NOPE:(h + 1) * NOPE] = _dot(dql, w_ref[h], NN).astype(BF16)
            dqr = dq_ref[:, h * KD + KVR:(h + 1) * KD]
            o_ref[:, H * NOPE + h * RP:H * NOPE + (h + 1) * RP] = _rope_t(dqr, cos_, sin_).astype(BF16)
            qn = q_ref[:, h * NOPE:(h + 1) * NOPE].astype(BF16)
            dw_ref[h] += _dot(qn, dql, TN)

    return pl.pallas_call(
        kern, grid=(t // tm,),
        in_specs=[pl.BlockSpec((tm, H * KD), lambda i: (i, 0)), pl.BlockSpec((tm, 2 * H * NOPE), lambda i: (i, 0)),
                  pl.BlockSpec((H, KVR, NOPE), lambda i: (0, 0, 0)),
                  pl.BlockSpec((tm, RP), lambda i: (i, 0)), pl.BlockSpec((tm, RP), lambda i: (i, 0))],
        out_specs=[pl.BlockSpec((tm, 2 * H * NOPE), lambda i: (i, 0)), pl.BlockSpec((H, NOPE, KVR), lambda i: (0, 0, 0))],
        out_shape=[jax.ShapeDtypeStruct((t, 2 * H * NOPE), BF16), jax.ShapeDtypeStruct((H, NOPE, KVR), F32)],
        name="q_prep_bwd", compiler_params=_cp())(dq_cat, q2, wuk_h, cos, sin)


def _o_up(o_lat, wuv_h, tm=256):
    t = o_lat.shape[0]
    tm = min(tm, t)

    def kern(x_ref, w_ref, o_ref):
        for h in range(H):
            xl = x_ref[:, h * KVR:(h + 1) * KVR].astype(BF16)
            o_ref[:, h * NOPE:(h + 1) * NOPE] = _dot(xl, w_ref[h], NN).astype(BF16)

    return pl.pallas_call(
        kern, grid=(t // tm,),
        in_specs=[pl.BlockSpec((tm, H * KVR), lambda i: (i, 0)), pl.BlockSpec((H, KVR, NOPE), lambda i: (0, 0, 0))],
        out_specs=pl.BlockSpec((tm, H * NOPE), lambda i: (i, 0)),
        out_shape=jax.ShapeDtypeStruct((t, H * NOPE), BF16),
        name="o_up", compiler_params=_cp())(o_lat, wuv_h)


def _o_up_bwd(do, o_lat, wuv_h, tm=256):
    t = do.shape[0]
    tm = min(tm, t)

    def kern(do_ref, x_ref, w_ref, dx_ref, dw_ref, dlt_ref):
        @pl.when(pl.program_id(0) == 0)
        def _():
            dw_ref[...] = jnp.zeros_like(dw_ref)

        for h in range(H):
            dh_ = do_ref[:, h * NOPE:(h + 1) * NOPE]
            x = x_ref[:, h * KVR:(h + 1) * KVR]
            dx = _dot(dh_, w_ref[h], NT)
            dx_ref[:, h * KVR:(h + 1) * KVR] = dx
            dw_ref[h] += _dot(x.astype(BF16), dh_, TN)
            dl = jnp.broadcast_to(jnp.sum(dx * x, axis=1)[:, None], (tm, 128))
            dlt_ref[h] = dl.T[0:1, :]

    return pl.pallas_call(
        kern, grid=(t // tm,),
        in_specs=[pl.BlockSpec((tm, H * NOPE), lambda i: (i, 0)), pl.BlockSpec((tm, H * KVR), lambda i: (i, 0)),
                  pl.BlockSpec((H, KVR, NOPE), lambda i: (0, 0, 0))],
        out_specs=[pl.BlockSpec((tm, H * KVR), lambda i: (i, 0)), pl.BlockSpec((H, KVR, NOPE), lambda i: (0, 0, 0)),
                   pl.BlockSpec((H, 1, tm), lambda i: (0, 0, i))],
        out_shape=[jax.ShapeDtypeStruct((t, H * KVR), F32), jax.ShapeDtypeStruct((H, KVR, NOPE), F32),
                   jax.ShapeDtypeStruct((H, 1, t), F32)],
        name="o_up_bwd", compiler_params=_cp())(do, o_lat, wuv_h)


def _causal_pairs(nq):
    return [(i, j) for i in range(nq) for j in range(i + 1)]


def _lane_tile(stat, width):
    return jnp.tile(stat, (1, width // 128))


def _flash_fwd(qcat, kc, bq, hb, gather=None):
    t = kc.shape[0]
    nq = t // bq
    pairs = _causal_pairs(nq)
    itab = jnp.asarray(np.array([p[0] for p in pairs], np.int32))
    jtab = jnp.asarray(np.array([p[1] for p in pairs], np.int32))

    ng = H // hb
    hosting = gather is not None

    def kern(it, jt, q_ref, k_ref, *rest):
        if hosting:
            w_ref, o_ref, lset_ref, wall_ref, m_sc, l_sc, acc_sc, send_sems, recv_sems = rest
            ag_start, ag_forward, ag_finish = _allgather_schedule(w_ref, wall_ref, send_sems, recv_sems)
        else:
            o_ref, lset_ref, m_sc, l_sc, acc_sc = rest
        grp = pl.program_id(0)
        st = pl.program_id(1)
        i, j = it[st], jt[st]

        if hosting:
            @pl.when((grp == 0) & (st == 0))
            def _():
                ag_start()

        @pl.when(j == 0)
        def _():
            m_sc[...] = jnp.full_like(m_sc, NEG)
            l_sc[...] = jnp.zeros_like(l_sc)
            acc_sc[...] = jnp.zeros_like(acc_sc)

        def update(masked):
            k = k_ref[...]
            v = k[:, 0:KVR]
            if masked:
                row = lax.broadcasted_iota(jnp.int32, (bq, bq), 0)
                col = lax.broadcasted_iota(jnp.int32, (bq, bq), 1)
                keep = col <= row
            s_next = _dot(q_ref[:, 0:KD], k, NT)
            for hh in range(hb):
                s = s_next
                if hh + 1 < hb:
                    s_next = _dot(q_ref[:, (hh + 1) * KD:(hh + 2) * KD], k, NT)
                if masked:
                    s = jnp.where(keep, s, NEG)
                m_prev = m_sc[hh]
                m_next = jnp.maximum(m_prev, jnp.max(s, axis=1)[:, None])
                p = jnp.exp2(s - _lane_tile(m_next, bq))
                a = jnp.exp2(m_prev - m_next)
                l_sc[hh] = a * l_sc[hh] + jnp.sum(p, axis=1)[:, None]
                acc_sc[hh] = _lane_tile(a, KVR) * acc_sc[hh] + _dot(p.astype(BF16), v, NN)
                m_sc[hh] = m_next

        @pl.when(j < i)
        def _():
            update(False)

        @pl.when(j == i)
        def _():
            update(True)
            for hh in range(hb):
                l = l_sc[hh]
                o_ref[:, hh * KVR:(hh + 1) * KVR] = acc_sc[hh] / _lane_tile(l, KVR)
                lset_ref[hh] = (m_sc[hh] + jnp.log2(l)).T[0:1, :]

        if hosting:
            @pl.when((grp == ng - 1) & (st == 0))
            def _():
                ag_forward()

            @pl.when((grp == ng - 1) & (st == len(pairs) - 1))
            def _():
                ag_finish()

    in_specs = [pl.BlockSpec((bq, hb * KD), lambda g, s, it, jt: (it[s], g)),
                pl.BlockSpec((bq, KD), lambda g, s, it, jt: (jt[s], 0))]
    out_specs = [pl.BlockSpec((bq, hb * KVR), lambda g, s, it, jt: (it[s], g)),
                 pl.BlockSpec((hb, 1, bq), lambda g, s, it, jt: (g, 0, it[s]))]
    out_shape = [jax.ShapeDtypeStruct((t, H * KVR), F32), jax.ShapeDtypeStruct((H, 1, t), F32)]
    scratch = [pltpu.VMEM((hb, bq, 128), F32), pltpu.VMEM((hb, bq, 128), F32), pltpu.VMEM((hb, bq, KVR), F32)]
    args = [itab, jtab, qcat, kc]
    if hosting:
        in_specs.append(ANY)
        out_specs.append(ANY)
        out_shape.append(jax.ShapeDtypeStruct((4,) + gather.shape, gather.dtype))
        scratch += AG_SEMS
        args.append(gather)
    gs = pltpu.PrefetchScalarGridSpec(num_scalar_prefetch=2, grid=(ng, len(pairs)), in_specs=in_specs,
                                      out_specs=out_specs, scratch_shapes=scratch)
    return pl.pallas_call(kern, grid_spec=gs, out_shape=out_shape, name="mla_flash_fwd",
                          compiler_params=_cp())(*args)


def _flash_dkv(qcat, kc, do_lat, lse_t, delta_t, bq, hb, exchange=None):
    hosting = exchange is not None
    t = kc.shape[0]
    nq = t // bq
    ng = H // hb
    npairs = nq * (nq + 1) // 2
    steps = [(j, g, i) for j in range(nq) for g in range(ng) for i in range(j, nq)]
    jtab = jnp.asarray(np.array([s[0] for s in steps], np.int32))
    gtab = jnp.asarray(np.array([s[1] for s in steps], np.int32))
    itab = jnp.asarray(np.array([s[2] for s in steps], np.int32))
    ptab = jnp.asarray(np.array([s[2] * (s[2] + 1) // 2 + s[0] for s in steps], np.int32))

    def kern(jt, gt, it, pt, q_ref, k_ref, do_ref, lset_ref, dlt_ref, *rest):
        if hosting:
            p_ref, dk_ref, ds_ref, slots_ref, dk_sc, dv_sc, send_sems, recv_sems = rest
            xc_start, xc_finish = _chip_exchange_schedule(p_ref, slots_ref, send_sems, recv_sems)
        else:
            dk_ref, ds_ref, dk_sc, dv_sc = rest
        st = pl.program_id(0)
        j, g, i = jt[st], gt[st], it[st]

        if hosting:
            @pl.when(st == 0)
            def _():
                xc_start()

        @pl.when((g == 0) & (i == j))
        def _():
            dk_sc[...] = jnp.zeros_like(dk_sc)
            dv_sc[...] = jnp.zeros_like(dv_sc)

        def update(masked):
            k = k_ref[...]
            v = k[:, 0:KVR]
            if masked:
                row = lax.broadcasted_iota(jnp.int32, (bq, bq), 0)
                col = lax.broadcasted_iota(jnp.int32, (bq, bq), 1)
                keep = row <= col

            def first_matmuls(hh):
                dob = do_ref[:, hh * KVR:(hh + 1) * KVR].astype(BF16)
                return _dot(k, q_ref[:, hh * KD:(hh + 1) * KD], NT), _dot(v, dob, NT), dob

            nxt = first_matmuls(0)
            for hh in range(hb):
                s, dp, dob = nxt
                if hh + 1 < hb:
                    nxt = first_matmuls(hh + 1)
                if masked:
                    s = jnp.where(keep, s, NEG)
                p = jnp.exp2(s - lset_ref[hh])
                dv_sc[...] += _dot(p.astype(BF16), dob, NN)
                dsb = (p * (dp - dlt_ref[hh])).astype(BF16)
                ds_ref[0, 0, hh] = dsb
                dk_sc[...] += _dot(dsb, q_ref[:, hh * KD:(hh + 1) * KD], NN)

        @pl.when(i > j)
        def _():
            update(False)

        @pl.when(i == j)
        def _():
            update(True)

        @pl.when((g == ng - 1) & (i == nq - 1))
        def _():
            dk_ref[:, 0:KVR] = dk_sc[:, 0:KVR] * LN2 + dv_sc[...]
            dk_ref[:, KVR:KD] = dk_sc[:, KVR:KD] * LN2

        if hosting:
            @pl.when(st == len(steps) - 1)
            def _():
                xc_finish()

    in_specs = [pl.BlockSpec((bq, hb * KD), lambda s, jt, gt, it, pt: (it[s], gt[s])),
                pl.BlockSpec((bq, KD), lambda s, jt, gt, it, pt: (jt[s], 0)),
                pl.BlockSpec((bq, hb * KVR), lambda s, jt, gt, it, pt: (it[s], gt[s])),
                pl.BlockSpec((hb, 1, bq), lambda s, jt, gt, it, pt: (gt[s], 0, it[s])),
                pl.BlockSpec((hb, 1, bq), lambda s, jt, gt, it, pt: (gt[s], 0, it[s]))]
    out_specs = [pl.BlockSpec((bq, KD), lambda s, jt, gt, it, pt: (jt[s], 0)),
                 pl.BlockSpec((1, 1, hb, bq, bq), lambda s, jt, gt, it, pt: (gt[s], pt[s], 0, 0, 0))]
    out_shape = [jax.ShapeDtypeStruct((t, KD), F32), jax.ShapeDtypeStruct((ng, npairs, hb, bq, bq), BF16)]
    scratch = [pltpu.VMEM((bq, KD), F32), pltpu.VMEM((bq, KVR), F32)]
    args = [jtab, gtab, itab, ptab, qcat, kc, do_lat, lse_t, delta_t]
    if hosting:
        in_specs.append(ANY)
        out_specs.append(ANY)
        out_shape.append(jax.ShapeDtypeStruct(exchange.shape, exchange.dtype))
        scratch += XCHG_SEMS
        args.append(exchange)
    gs = pltpu.PrefetchScalarGridSpec(num_scalar_prefetch=4, grid=(len(steps),), in_specs=in_specs,
                                      out_specs=out_specs, scratch_shapes=scratch)
    return pl.pallas_call(kern, grid_spec=gs, out_shape=out_shape, name="mla_flash_dkv",
                          compiler_params=_cp())(*args)


def _flash_dq(ds_all, kc_t, bq, hb, exchange=None):
    t = kc_t.shape[1]
    nq = t // bq
    ng = H // hb
    pairs = _causal_pairs(nq)
    itab = jnp.asarray(np.array([p[0] for p in pairs], np.int32))
    jtab = jnp.asarray(np.array([p[1] for p in pairs], np.int32))
    hosting = exchange is not None

    def kern(it, jt, ds_ref, kt_ref, *rest):
        if hosting:
            p_ref, dq_ref, slots_ref, acc_sc, send_sems, recv_sems = rest
            xc_start, xc_finish = _chip_exchange_schedule(p_ref, slots_ref, send_sems, recv_sems)
        else:
            dq_ref, acc_sc = rest
        grp = pl.program_id(0)
        st = pl.program_id(1)
        i, j = it[st], jt[st]
        kt = kt_ref[...]

        if hosting:
            @pl.when((grp == 0) & (st == 0))
            def _():
                xc_start()

        @pl.when(j == 0)
        def _():
            for hh in range(hb):
                acc_sc[hh] = _dot(kt, ds_ref[0, 0, hh], NN)

        @pl.when((j > 0) & (j < i))
        def _():
            for hh in range(hb):
                acc_sc[hh] += _dot(kt, ds_ref[0, 0, hh], NN)

        @pl.when(j == i)
        def _():
            for hh in range(hb):
                tot = _dot(kt, ds_ref[0, 0, hh], NN)
                tot = jnp.where(i > 0, tot + acc_sc[hh], tot)
                dq_ref[:, hh * KD:(hh + 1) * KD] = tot.T * MLA_SCALE

        if hosting:
            @pl.when((grp == ng - 1) & (st == len(pairs) - 1))
            def _():
                xc_finish()

    in_specs = [pl.BlockSpec((1, 1, hb, bq, bq), lambda g, s, it, jt: (g, s, 0, 0, 0)),
                pl.BlockSpec((KD, bq), lambda g, s, it, jt: (0, jt[s]))]
    out_specs = [pl.BlockSpec((bq, hb * KD), lambda g, s, it, jt: (it[s], g))]
    out_shape = [jax.ShapeDtypeStruct((t, H * KD), F32)]
    scratch = [pltpu.VMEM((hb, KD, bq), F32)]
    args = [itab, jtab, ds_all, kc_t]
    if hosting:
        in_specs.append(ANY)
        out_specs.append(ANY)
        out_shape.append(jax.ShapeDtypeStruct(exchange.shape, exchange.dtype))
        scratch += XCHG_SEMS
        args.append(exchange)
    gs = pltpu.PrefetchScalarGridSpec(num_scalar_prefetch=2, grid=(ng, len(pairs)), in_specs=in_specs,
                                      out_specs=out_specs, scratch_shapes=scratch)
    outs = pl.pallas_call(kern, grid_spec=gs, out_shape=out_shape, name="mla_flash_dq",
                          compiler_params=_cp())(*args)
    return outs if hosting else outs[0]


def _bucket_table():
    d = np.arange(WIN)
    max_exact = NBKT // 2
    nf = np.maximum(d, 1).astype(np.float32)
    large = max_exact + (np.log(nf / np.float32(max_exact)) / np.float32(math.log(WIN / max_exact))
                         * np.float32(NBKT - max_exact)).astype(np.int32)
    large = np.minimum(large, NBKT - 1)
    bucket = np.where(d < max_exact, d, large).astype(np.int32)
    jj = np.arange(2 * WIN)[:, None]
    ii = np.arange(WIN)[None, :]
    dist = ii + WIN - jj
    valid = (dist >= 0) & (dist < WIN)
    return np.where(valid, bucket[np.clip(dist, 0, WIN - 1)], -1).astype(np.int32)


def _bias_build(rel_bias, bkt):
    def kern(bk_ref, rb_ref, o_ref):
        bk = bk_ref[...]
        for hd in range(QH):
            acc = jnp.full((2 * WIN, WIN), NEG, F32)
            for b in range(NBKT):
                acc = jnp.where(bk == b, rb_ref[b, hd], acc)
            o_ref[hd] = acc

    return pl.pallas_call(
        kern, in_specs=[pl.BlockSpec(memory_space=pltpu.VMEM), pl.BlockSpec(memory_space=pltpu.SMEM)],
        out_specs=pl.BlockSpec(memory_space=pltpu.VMEM),
        out_shape=jax.ShapeDtypeStruct((QH, 2 * WIN, WIN), F32), name="swa_bias_build")(bkt, rel_bias)


def _bias_bwd(dbias, bkt):
    def kern(db_ref, bk_ref, o_ref):
        bk = bk_ref[...]
        for hd in range(QH):
            g = db_ref[hd]
            for b in range(NBKT):
                r = b * QH + hd
                o_ref[r:r + 1, :] = jnp.sum(jnp.where(bk == b, g, 0.0), axis=0, keepdims=True)

    return pl.pallas_call(
        kern, in_specs=[pl.BlockSpec(memory_space=pltpu.VMEM), pl.BlockSpec(memory_space=pltpu.VMEM)],
        out_specs=pl.BlockSpec(memory_space=pltpu.VMEM),
        out_shape=jax.ShapeDtypeStruct((NBKT * QH, WIN), F32), name="swa_bias_bwd")(dbias, bkt)


def _swa_finish_scores(raw, bias, first):
    s = raw * SWA_SCALE + bias
    if first is not None:
        row = lax.broadcasted_iota(jnp.int32, s.shape, 0)
        s = jnp.where(jnp.logical_or(jnp.logical_not(first), row >= WIN), s, NEG)
    return s


def _swa_fwd(qkv_t, bias, sinks, qb):
    t = qkv_t.shape[1]
    w = qb * WIN
    nst = t // w

    def kern(q_ref, kc_ref, kp_ref, vc_ref, vp_ref, b_ref, sk_ref, o_ref, lse_ref):
        n = pl.program_id(0)
        kfull = jnp.concatenate([kp_ref[...], kc_ref[...]], axis=1)
        vfull = jnp.concatenate([vp_ref[...], vc_ref[...]], axis=1)
        head_row = lax.broadcasted_iota(jnp.int32, (QH, WIN), 0)
        groups = [(b, kh) for b in range(qb) for kh in range(KVH)]

        def raw_scores(b, kh):
            k_band = kfull[kh * HD:(kh + 1) * HD, b * WIN:(b + 2) * WIN]
            return [_dot(k_band, q_ref[(kh * G + g) * HD:(kh * G + g + 1) * HD, b * WIN:(b + 1) * WIN], TN)
                    for g in range(G)]

        o_rows = [[] for _ in range(qb)]
        lse_tiles = [jnp.zeros((QH, WIN), F32) for _ in range(qb)]
        nxt_scores = raw_scores(*groups[0])
        for gi, (b, kh) in enumerate(groups):
            scores = nxt_scores
            if gi + 1 < len(groups):
                nxt_scores = raw_scores(*groups[gi + 1])
            v_band = vfull[kh * HD:(kh + 1) * HD, b * WIN:(b + 2) * WIN]
            for g in range(G):
                hd = kh * G + g
                s = _swa_finish_scores(scores[g], b_ref[hd], (n == 0) if b == 0 else None)
                sink = sk_ref[hd]
                m = jnp.maximum(jnp.max(s, axis=0, keepdims=True), sink)
                p = jnp.exp(s - m)
                den = jnp.sum(p, axis=0, keepdims=True) + jnp.exp(sink - m)
                p = p / den
                o_rows[b].append(_dot(v_band, p.astype(BF16), NN))
                lse_tiles[b] = jnp.where(head_row == hd, m + jnp.log(den), lse_tiles[b])
        o_ref[...] = jnp.concatenate([jnp.concatenate(rows, axis=0) for rows in o_rows], axis=1)
        lse_ref[...] = jnp.concatenate(lse_tiles, axis=1)

    prev = lambda r: (lambda n: (r, jnp.maximum(n * qb - 1, 0)))
    return pl.pallas_call(
        kern, grid=(nst,),
        in_specs=[pl.BlockSpec((QH * HD, w), lambda n: (0, n)),
                  pl.BlockSpec((KVH * HD, w), lambda n: (4, n)), pl.BlockSpec((KVH * HD, WIN), prev(4)),
                  pl.BlockSpec((KVH * HD, w), lambda n: (5, n)), pl.BlockSpec((KVH * HD, WIN), prev(5)),
                  pl.BlockSpec((QH, 2 * WIN, WIN), lambda n: (0, 0, 0)),
                  pl.BlockSpec(memory_space=pltpu.SMEM)],
        out_specs=[pl.BlockSpec((QH * HD, w), lambda n: (0, n)), pl.BlockSpec((QH, w), lambda n: (0, n))],
        out_shape=[jax.ShapeDtypeStruct((QH * HD, t), F32), jax.ShapeDtypeStruct((QH, t), F32)],
        name="swa_fwd", compiler_params=_cp())(qkv_t, qkv_t, qkv_t, qkv_t, qkv_t, bias, sinks)


def _swa_bwd(qkv_t, do_t, o_t, lse, bias, sinks, qb):
    t = qkv_t.shape[1]
    w = qb * WIN
    nst = t // w
    nblk = t // WIN

    def kern(q_ref, kc_ref, kp_ref, vc_ref, vp_ref, do_ref, o_ref, lse_ref, qn_ref, don_ref, on_ref, lsen_ref,
             b_ref, sk_ref, dqkv_ref, db_ref, dsk_ref):
        n = pl.program_id(0)

        @pl.when(n == 0)
        def _():
            db_ref[...] = jnp.zeros_like(db_ref)
            dsk_ref[...] = jnp.zeros_like(dsk_ref)

        kfull = jnp.concatenate([kp_ref[...], kc_ref[...]], axis=1)
        vfull = jnp.concatenate([vp_ref[...], vc_ref[...]], axis=1)
        head_row = lax.broadcasted_iota(jnp.int32, (QH, WIN), 0)
        db_acc = [None] * QH
        dsk_tile = jnp.zeros((QH, WIN), F32)
        prev_part = [[[None] * qb for _ in range(KVH)] for _ in range(2)]
        cur_part = [[[None] * qb for _ in range(KVH)] for _ in range(2)]
        groups = [(b, kh) for b in range(qb) for kh in range(KVH)]

        def first_matmuls(b, kh):
            k_band = kfull[kh * HD:(kh + 1) * HD, b * WIN:(b + 2) * WIN]
            v_band = vfull[kh * HD:(kh + 1) * HD, b * WIN:(b + 2) * WIN]
            out = []
            for g in range(G):
                rs = slice((kh * G + g) * HD, (kh * G + g + 1) * HD)
                dob = do_ref[rs, b * WIN:(b + 1) * WIN].astype(BF16)
                out.append((_dot(k_band, q_ref[rs, b * WIN:(b + 1) * WIN], TN), _dot(v_band, dob, TN), dob))
            return out

        dq_rows = [[] for _ in range(qb)]
        nxt_first = first_matmuls(*groups[0])
        for gi, (b, kh) in enumerate(groups):
            first = nxt_first
            if gi + 1 < len(groups):
                nxt_first = first_matmuls(*groups[gi + 1])
            cs = slice(b * WIN, (b + 1) * WIN)
            k_band = kfull[kh * HD:(kh + 1) * HD, b * WIN:(b + 2) * WIN]
            dk_b = dv_b = None
            for g in range(G):
                hd = kh * G + g
                rs = slice(hd * HD, (hd + 1) * HD)
                raw, dp, dob = first[g]
                lse_h = lse_ref[hd:hd + 1, cs]
                s = _swa_finish_scores(raw, b_ref[hd], (n == 0) if b == 0 else None)
                p = jnp.exp(s - lse_h)
                dl = jnp.sum(do_ref[rs, cs] * o_ref[rs, cs], axis=0, keepdims=True)
                ds = p * (dp - dl)
                db_acc[hd] = ds if db_acc[hd] is None else db_acc[hd] + ds
                dsk_tile = jnp.where(head_row == hd, dsk_tile - jnp.exp(sk_ref[hd] - lse_h) * dl, dsk_tile)
                dss = (ds * SWA_SCALE).astype(BF16)
                dq_rows[b].append(_dot(k_band, dss, NN).astype(BF16))
                dk_h = _dot(q_ref[rs, cs], dss, NT)
                dv_h = _dot(dob, p.astype(BF16), NT)
                dk_b = dk_h if dk_b is None else dk_b + dk_h
                dv_b = dv_h if dv_b is None else dv_b + dv_h
            for which, val in ((0, dk_b), (1, dv_b)):
                prev_part[which][kh][b] = val[:, 0:WIN]
                cur_part[which][kh][b] = val[:, WIN:2 * WIN]
        dq_cols = [jnp.concatenate(rows, axis=0) for rows in dq_rows]

        live = n < nst - 1
        ls = slice((qb - 1) * WIN, qb * WIN)
        halo = [[None] * KVH for _ in range(2)]
        for kh in range(KVH):
            k_last = kc_ref[kh * HD:(kh + 1) * HD, ls]
            v_last = vc_ref[kh * HD:(kh + 1) * HD, ls]
            dk_b = dv_b = None
            for g in range(G):
                hd = kh * G + g
                rs = slice(hd * HD, (hd + 1) * HD)
                q_t = qn_ref[rs, :]
                do = don_ref[rs, :]
                s = _dot(k_last, q_t, TN) * SWA_SCALE + b_ref[hd, 0:WIN, :]
                p = jnp.exp(s - lsen_ref[hd:hd + 1, :])
                dob = do.astype(BF16)
                dp = _dot(v_last, dob, TN)
                dl = jnp.sum(do * on_ref[rs, :], axis=0, keepdims=True)
                dss = (p * (dp - dl) * SWA_SCALE).astype(BF16)
                dk_h = _dot(q_t, dss, NT)
                dv_h = _dot(dob, p.astype(BF16), NT)
                dk_b = dk_h if dk_b is None else dk_b + dk_h
                dv_b = dv_h if dv_b is None else dv_b + dv_h
            halo[0][kh] = jnp.where(live, dk_b, 0.0)
            halo[1][kh] = jnp.where(live, dv_b, 0.0)

        kv_rows = []
        for which in range(2):
            for kh in range(KVH):
                blocks = [cur_part[which][kh][p] + (prev_part[which][kh][p + 1] if p + 1 < qb else halo[which][kh])
                          for p in range(qb)]
                kv_rows.append(jnp.concatenate(blocks, axis=1))
        dqkv_ref[...] = jnp.concatenate(
            [jnp.concatenate(dq_cols, axis=1), jnp.concatenate(kv_rows, axis=0).astype(BF16)], axis=0)
        db_ref[...] += jnp.stack(db_acc)
        dsk_ref[...] += dsk_tile

    prev = lambda r: (lambda n: (r, jnp.maximum(n * qb - 1, 0)))
    nxt = lambda n: (0, jnp.minimum((n + 1) * qb, nblk - 1))
    big = lambda: pl.BlockSpec((QH * HD, w), lambda n: (0, n))
    return pl.pallas_call(
        kern, grid=(nst,),
        in_specs=[big(),
                  pl.BlockSpec((KVH * HD, w), lambda n: (4, n)), pl.BlockSpec((KVH * HD, WIN), prev(4)),
                  pl.BlockSpec((KVH * HD, w), lambda n: (5, n)), pl.BlockSpec((KVH * HD, WIN), prev(5)),
                  big(), big(), pl.BlockSpec((QH, w), lambda n: (0, n)),
                  pl.BlockSpec((QH * HD, WIN), nxt), pl.BlockSpec((QH * HD, WIN), nxt),
                  pl.BlockSpec((QH * HD, WIN), nxt), pl.BlockSpec((QH, WIN), nxt),
                  pl.BlockSpec((QH, 2 * WIN, WIN), lambda n: (0, 0, 0)),
                  pl.BlockSpec(memory_space=pltpu.SMEM)],
        out_specs=[pl.BlockSpec(((QH + 2 * KVH) * HD, w), lambda n: (0, n)),
                   pl.BlockSpec((QH, 2 * WIN, WIN), lambda n: (0, 0, 0)),
                   pl.BlockSpec((QH, WIN), lambda n: (0, 0))],
        out_shape=[jax.ShapeDtypeStruct(((QH + 2 * KVH) * HD, t), BF16),
                   jax.ShapeDtypeStruct((QH, 2 * WIN, WIN), F32), jax.ShapeDtypeStruct((QH, WIN), F32)],
        name="swa_bwd", compiler_params=_cp())(
            qkv_t, qkv_t, qkv_t, qkv_t, qkv_t, do_t, o_t, lse, qkv_t, do_t, o_t, lse, bias, sinks)


def _adamw(w, g, m, v, name, tm=544):
    r = w.shape[0]
    tm = r if r % tm else tm
    c1 = 1.0 / (1.0 - B1 ** STEP)
    c2 = 1.0 / (1.0 - B2 ** STEP)

    def kern(w_ref, g_ref, m_ref, v_ref, d_ref, nm_ref, nv_ref):
        g_ = g_ref[...]
        nm = B1 * m_ref[...] + (1.0 - B1) * g_
        nv = B2 * v_ref[...] + (1.0 - B2) * (g_ * g_)
        d_ref[...] = -LR * ((nm * c1) / (jnp.sqrt(nv * c2) + ADAM_EPS) + WD * w_ref[...])
        nm_ref[...] = nm
        nv_ref[...] = nv

    row = pl.BlockSpec((tm, D), lambda i: (i, 0))
    sds = jax.ShapeDtypeStruct((r, D), F32)
    return pl.pallas_call(kern, grid=(r // tm,), in_specs=[row] * 4, out_specs=[row] * 3, out_shape=[sds] * 3,
                          name=name, compiler_params=_cp())(w, g, m, v)


def _mesh_pos():
    return lax.axis_index("x"), lax.axis_index("y"), lax.axis_index("c")


ANY = pl.BlockSpec(memory_space=pl.ANY)


AG_SEMS = [pltpu.SemaphoreType.DMA((6,)), pltpu.SemaphoreType.DMA((6,))]
XCHG_SEMS = [pltpu.SemaphoreType.DMA((3,)), pltpu.SemaphoreType.DMA((3,))]


def _allgather_schedule(w_ref, out_ref, send_sems, recv_sems):
    half = w_ref.shape[0] // 2
    x, y, c = _mesh_pos()
    me, sibling = (x, y, c), (x, y, 1 - c)
    chips = [(1 - x, y), (x, 1 - y), (1 - x, 1 - y)]

    def rows(px, py, pc):
        return out_ref.at[2 * px + py, pl.ds(pc * half, half), :]

    def copy(k, block, to, src=None):
        return pltpu.make_async_remote_copy(
            src_ref=rows(*block) if src is None else src, dst_ref=rows(*block),
            send_sem=send_sems.at[k], recv_sem=recv_sems.at[k], device_id=to, device_id_type=MESH)

    def first():
        return [copy(j, me, (*chip, c), src=w_ref.at[pl.ds(c * half, half), :]) for j, chip in enumerate(chips)]

    def passed():
        return [copy(3 + j, (*chip, c), sibling) for j, chip in enumerate(chips)]

    def start():
        for cp in first():
            cp.start()

    def forward():
        for j, chip in enumerate(chips):
            copy(j, (*chip, c), me).wait_recv()
            passed()[j].start()

    def finish():
        for j, chip in enumerate(chips):
            copy(3 + j, (*chip, 1 - c), me).wait_recv()
        for cp in first() + passed():
            cp.wait_send()

    return start, forward, finish


def _allgather_weights(wpack):
    def body(w_ref, out_ref, send_sems, recv_sems):
        start, forward, finish = _allgather_schedule(w_ref, out_ref, send_sems, recv_sems)
        start()
        forward()
        finish()

    return pl.pallas_call(
        body, out_shape=jax.ShapeDtypeStruct((4,) + wpack.shape, wpack.dtype), in_specs=[ANY], out_specs=ANY,
        scratch_shapes=AG_SEMS, name="allgather_weights")(wpack)


def _row_tile(rows):
    t = min(rows, 512)
    while rows % t or t % 16:
        t -= 16
    return t


def _exchange_core_halves(g, tag):
    half = g.shape[1] // 2

    def body(g_ref, out_ref, send_sem, recv_sem):
        x, y, c = _mesh_pos()
        cp = pltpu.make_async_remote_copy(
            src_ref=g_ref.at[:, pl.ds((1 - c) * half, half), :], dst_ref=out_ref,
            send_sem=send_sem, recv_sem=recv_sem, device_id=(x, y, 1 - c), device_id_type=MESH)
        cp.start()
        cp.wait()

    return pl.pallas_call(
        body, out_shape=jax.ShapeDtypeStruct((4, half, D), g.dtype), in_specs=[ANY], out_specs=ANY,
        scratch_shapes=[pltpu.SemaphoreType.DMA, pltpu.SemaphoreType.DMA], name=f"rs_exchange_cores_{tag}")(g)


def _add_core_halves(g, other, cidx, tag):
    half = other.shape[1]
    tm = _row_tile(half)
    nb = half // tm

    def kern(c_ref, a_ref, b_ref, o_ref):
        o_ref[...] = (a_ref[...] + b_ref[...]).astype(BF16)

    gs = pltpu.PrefetchScalarGridSpec(
        num_scalar_prefetch=1, grid=(4, nb),
        in_specs=[pl.BlockSpec((1, tm, D), lambda s, i, c: (s, c[0] * nb + i, 0)),
                  pl.BlockSpec((1, tm, D), lambda s, i, c: (s, i, 0))],
        out_specs=pl.BlockSpec((1, tm, D), lambda s, i, c: (s, i, 0)))
    return pl.pallas_call(kern, grid_spec=gs, out_shape=jax.ShapeDtypeStruct(other.shape, BF16),
                          name=f"rs_add_cores_{tag}", compiler_params=_cp())(cidx, g, other)


def _chip_exchange_schedule(p_ref, out_ref, send_sems, recv_sems):
    x, y, c = _mesh_pos()
    me = 2 * x + y
    chips = [(1 - x, y), (x, 1 - y), (1 - x, 1 - y)]

    def sends():
        return [pltpu.make_async_remote_copy(
            src_ref=p_ref.at[2 * px + py], dst_ref=out_ref.at[me], send_sem=send_sems.at[j],
            recv_sem=recv_sems.at[j], device_id=(px, py, c), device_id_type=MESH) for j, (px, py) in enumerate(chips)]

    def start():
        for cp in sends():
            cp.start()

    def finish():
        for j, (px, py) in enumerate(chips):
            pltpu.make_async_remote_copy(
                src_ref=p_ref.at[me], dst_ref=out_ref.at[2 * px + py], send_sem=send_sems.at[j],
                recv_sem=recv_sems.at[j], device_id=(px, py, c), device_id_type=MESH).wait_recv()
        for cp in sends():
            cp.wait_send()

    return start, finish


def _exchange_chip_shards(p, tag):
    def body(p_ref, out_ref, send_sems, recv_sems):
        start, finish = _chip_exchange_schedule(p_ref, out_ref, send_sems, recv_sems)
        start()
        finish()

    return pl.pallas_call(
        body, out_shape=jax.ShapeDtypeStruct(p.shape, p.dtype), in_specs=[ANY], out_specs=ANY,
        scratch_shapes=XCHG_SEMS, name=f"rs_exchange_chips_{tag}")(p)


def _sum_slots(slots, p, pos, tag):
    half = slots.shape[1]
    tm = _row_tile(half)
    nb = half // tm

    def kern(pos_ref, p_ref, s1_ref, s2_ref, s3_ref, o_ref):
        o_ref[...] = ((p_ref[0].astype(F32) + s1_ref[0].astype(F32)) + s2_ref[0].astype(F32)) + s3_ref[0].astype(F32)

    def slot(k):
        return pl.BlockSpec((1, tm, D), lambda i, pos: ((pos[0] + k) % 4, i, 0))

    gs = pltpu.PrefetchScalarGridSpec(
        num_scalar_prefetch=1, grid=(nb,), in_specs=[slot(0), slot(1), slot(2), slot(3)],
        out_specs=pl.BlockSpec((tm, D), lambda i, pos: (pos[1] * nb + i, 0)))
    return pl.pallas_call(kern, grid_spec=gs, out_shape=jax.ShapeDtypeStruct((2 * half, D), F32),
                          name=f"rs_sum_chips_{tag}", compiler_params=_cp())(pos, p, slots, slots, slots)


def _join_core_halves(r, tag):
    half = r.shape[0] // 2

    def body(r_ref, out_ref, send_sem, recv_sem):
        x, y, c = _mesh_pos()
        mine = out_ref.at[pl.ds(c * half, half), :]
        cp = pltpu.make_async_remote_copy(
            src_ref=mine, dst_ref=mine, send_sem=send_sem, recv_sem=recv_sem,
            device_id=(x, y, 1 - c), device_id_type=MESH)
        cp.start()
        theirs = out_ref.at[pl.ds((1 - c) * half, half), :]
        pltpu.make_async_remote_copy(
            src_ref=theirs, dst_ref=theirs, send_sem=send_sem, recv_sem=recv_sem,
            device_id=(x, y, 1 - c), device_id_type=MESH).wait_recv()
        cp.wait_send()

    return pl.pallas_call(
        body, out_shape=jax.ShapeDtypeStruct(r.shape, r.dtype), in_specs=[ANY], out_specs=ANY,
        input_output_aliases={0: 0},
        scratch_shapes=[pltpu.SemaphoreType.DMA, pltpu.SemaphoreType.DMA],
        name=f"rs_join_cores_{tag}")(r)


def _allreduce_small(v, name):
    def body(v_ref, out_ref, gat, send_sems, recv_sems):
        x, y, c = _mesh_pos()
        me = 4 * x + 2 * y + c
        gat[me] = v_ref[...]
        sends = []
        for k in range(1, 8):
            peer = (x ^ (k >> 2), y ^ ((k >> 1) & 1), c ^ (k & 1))
            cp = pltpu.make_async_remote_copy(
                src_ref=v_ref, dst_ref=gat.at[me], send_sem=send_sems.at[k - 1], recv_sem=recv_sems.at[k - 1],
                device_id=peer, device_id_type=MESH)
            cp.start()
            sends.append(cp)
        for k in range(1, 8):
            px, py, pc = x ^ (k >> 2), y ^ ((k >> 1) & 1), c ^ (k & 1)
            pltpu.make_async_remote_copy(
                src_ref=v_ref, dst_ref=gat.at[4 * px + 2 * py + pc], send_sem=send_sems.at[k - 1],
                recv_sem=recv_sems.at[k - 1], device_id=(px, py, pc), device_id_type=MESH).wait_recv()
        for cp in sends:
            cp.wait_send()
        acc = gat[0]
        for d in range(1, 8):
            acc = acc + gat[d]
        out_ref[...] = acc

    return pl.pallas_call(
        body, out_shape=jax.ShapeDtypeStruct(v.shape, F32),
        in_specs=[pl.BlockSpec(memory_space=pltpu.VMEM)], out_specs=pl.BlockSpec(memory_space=pltpu.VMEM),
        scratch_shapes=[pltpu.VMEM((8,) + v.shape, F32), pltpu.SemaphoreType.DMA((7,)), pltpu.SemaphoreType.DMA((7,))],
        name=name)(v)


def _mlp_fwd(xb, w_up, w_down, tag):
    a = _mm(xb, w_up, "nn", f"mlp_up_{tag}", out_dtype=BF16, relu2=True, b_blocked=True)
    return a, _mm(a, w_down, "nn", f"mlp_down_{tag}")


def _mlp_bwd(dz, dzb, xb, a, w_up, w_down, tag):
    du = _mm(dzb, w_down, "nt", f"mlp_down_dx_{tag}", out_dtype=BF16, gate_a=a)
    dw_down = _mm(a, dzb, "tn", f"mlp_down_dw_{tag}")
    dw_up = _mm(xb, du, "tn", f"mlp_up_dw_{tag}", out_blocked=True)
    dx = _mm(du, w_up, "nt", f"mlp_up_dx_{tag}", addend=dz, add_scale=ALPHA, b_blocked=True)
    return dx, dw_up, dw_down


def _fwd_bwd(x, target, w, dist=None, bq=512, qb=4, hb=4):
    t = x.shape[0]
    bq = min(bq, t)
    qb = min(qb, t // WIN)
    cos, sin = _rope_tables(t)
    bkt = jnp.asarray(_bucket_table())
    w_in = jnp.pad(w[("mla_w_in", None)], ((0, 0), (0, HW - (QR + KVR + ROPE))))
    wuq = w[("mla_w_uq", None)]
    wq2 = jnp.concatenate([wuq[:, :, :NOPE].reshape(QR, H * NOPE),
                           jnp.pad(wuq[:, :, NOPE:], ((0, 0), (0, 0), (0, RP - ROPE))).reshape(QR, H * RP)], axis=1)
    wuk_t = w[("mla_w_uk", None)].transpose(1, 2, 0)
    wuk_h = w[("mla_w_uk", None)].transpose(1, 0, 2)
    wuv_h = w[("mla_w_uv", None)].transpose(1, 0, 2)
    w_o = w[("mla_w_o", None)]
    sinks = w["swa_sinks"].reshape(QH)
    lnp = lambda n, l: w[n][l]
    reduced = {}

    hh = _mm(x, w_in, "nn", "mla_in")
    cq, kc = _mla_pre(hh, w["mla_g_q"], w["mla_g_kv"], cos, sin)
    q2 = _mm(cq, wq2, "nn", "mla_uq")
    qcat = _q_prep(q2, wuk_t, cos, sin)
    if dist is None:
        o_lat, lse0_t = _flash_fwd(qcat, kc, bq, hb)
    else:
        o_lat, lse0_t, wall = _flash_fwd(qcat, kc, bq, hb, gather=dist.late_pack)
        wall = lax.dynamic_update_slice(wall, dist.late_pack[None], (dist.shard, 0, 0))
        w = {**w, **_full_from_gathered(AG_LATE, wall, dist.shard_shapes)}
    wqkv = jnp.concatenate([w[("swa_w_q", None)], w[("kv_w_shared", None)]], axis=1)
    wqkv_t = wqkv.T
    wo_s = w[("swa_w_o", None)]
    o0 = _o_up(o_lat, wuv_h)
    y0 = _mm(o0, w_o, "nn", "mla_out")
    x1b, xh1, r1 = _add_ln(x, y0, lnp("ln_mix_g", 0), lnp("ln_mix_b", 0), "ln_mix_0")
    a0, f0 = _mlp_fwd(x1b, w[("mlp_w_up", 0)], w[("mlp_w_down", 0)], 0)
    x2b, xh2, r2 = _add_ln(xh1, f0, lnp("ln_mlp_g", 0), lnp("ln_mlp_b", 0), "ln_mlp_0",
                           res_affine=(lnp("ln_mix_g", 0), lnp("ln_mix_b", 0)))
    bias = _bias_build(w["rel_bias"], bkt)
    qkv_t = _mm(x2b, wqkv, "nn", "swa_qkv", out_dtype=BF16, out_t=True)
    os_t, lse1 = _swa_fwd(qkv_t, bias, sinks, qb)
    y1 = _mm(os_t, wo_s, "tn", "swa_out")
    x3b, xh3, r3 = _add_ln(xh2, y1, lnp("ln_mix_g", 1), lnp("ln_mix_b", 1), "ln_mix_1",
                           res_affine=(lnp("ln_mlp_g", 0), lnp("ln_mlp_b", 0)))
    a1, f1 = _mlp_fwd(x3b, w[("mlp_w_up", 1)], w[("mlp_w_down", 1)], 1)
    _, xh4, r4 = _add_ln(xh3, f1, lnp("ln_mlp_g", 1), lnp("ln_mlp_b", 1), "ln_mlp_1",
                         res_affine=(lnp("ln_mix_g", 1), lnp("ln_mix_b", 1)))

    g = {}
    dz4, dz4b, dg_mlp1, db_mlp1, lpart = _ln_bwd(target, xh4, r4, lnp("ln_mlp_g", 1), "ln_mlp_1_bwd",
                                                 loss_b=lnp("ln_mlp_b", 1))
    dx3, g[("mlp_w_up", 1)], g[("mlp_w_down", 1)] = _mlp_bwd(
        dz4, dz4b, x3b, a1, w[("mlp_w_up", 1)], w[("mlp_w_down", 1)], 1)
    part1 = _rs_chip_partials(RS_MLP1, g, dist, "mlp1") if dist is not None else None
    dz3, dz3b, dg_mix1, db_mix1 = _ln_bwd(dx3, xh3, r3, lnp("ln_mix_g", 1), "ln_mix_1_bwd")
    dos_t = _mm(dz3b, wo_s, "nt", "swa_out_dx", out_t=True)
    g[("swa_w_o", None)] = _mm(os_t, dz3b, "nn", "swa_out_dw")
    dqkv_t, dbias, dsk = _swa_bwd(qkv_t, dos_t, os_t, lse1, bias, sinks, qb)
    dwqkv = _mm(dqkv_t, x2b, "nn", "swa_qkv_dw").T
    g[("swa_w_q", None)], g[("kv_w_shared", None)] = dwqkv[:, :QH * HD], dwqkv[:, QH * HD:]
    dx2 = _mm(dqkv_t, wqkv_t, "tn", "swa_qkv_dx", addend=dz3, add_scale=ALPHA)
    g["rel_bias"] = jnp.sum(_bias_bwd(dbias, bkt), axis=-1).reshape(NBKT, QH)
    g["swa_sinks"] = jnp.sum(dsk, axis=-1).reshape(1, QH)
    dz2, dz2b, dg_mlp0, db_mlp0 = _ln_bwd(dx2, xh2, r2, lnp("ln_mlp_g", 0), "ln_mlp_0_bwd")
    dx1, g[("mlp_w_up", 0)], g[("mlp_w_down", 0)] = _mlp_bwd(
        dz2, dz2b, x1b, a0, w[("mlp_w_up", 0)], w[("mlp_w_down", 0)], 0)
    part0 = _rs_chip_partials(RS_MLP0, g, dist, "mlp0") if dist is not None else None
    dz1, dz1b, dg_mix0, db_mix0 = _ln_bwd(dx1, xh1, r1, lnp("ln_mix_g", 0), "ln_mix_0_bwd")
    do0 = _mm(dz1b, w_o, "nt", "mla_out_dx", out_dtype=BF16)
    g[("mla_w_o", None)] = _mm(o0, dz1b, "tn", "mla_out_dw")
    do_lat, dwuv, delta_t = _o_up_bwd(do0, o_lat, wuv_h)
    g[("mla_w_uv", None)] = dwuv.transpose(1, 0, 2)
    if dist is None:
        dk, ds_all = _flash_dkv(qcat, kc, do_lat, lse0_t, delta_t, bq, hb)
        dq_cat = _flash_dq(ds_all, kc.T, bq, hb)
    else:
        dk, ds_all, slots1 = _flash_dkv(qcat, kc, do_lat, lse0_t, delta_t, bq, hb, exchange=part1)
        dq_cat, slots0 = _flash_dq(ds_all, kc.T, bq, hb, exchange=part0)
        reduced["mlp1"] = _rs_finish(part1, slots1, dist, "mlp1")
        reduced["mlp0"] = _rs_finish(part0, slots0, dist, "mlp0")
    dq2, dwuk = _q_prep_bwd(dq_cat, q2, wuk_h, cos, sin)
    g[("mla_w_uk", None)] = dwuk.transpose(2, 0, 1)
    dcq = _mm(dq2, wq2, "nt", "mla_uq_dx")
    dwq2 = _mm(cq, dq2, "tn", "mla_uq_dw")
    g[("mla_w_uq", None)] = jnp.concatenate([dwq2[:, :H * NOPE].reshape(QR, H, NOPE),
                                             dwq2[:, H * NOPE:].reshape(QR, H, RP)[:, :, :ROPE]], axis=2)
    dh, dgq, dgkv = _mla_pre_bwd(hh, dcq, dk, w["mla_g_q"], w["mla_g_kv"], cos, sin)
    g[("mla_w_in", None)] = _mm(x, dh, "tn", "mla_in_dw")[:, :QR + KVR + ROPE]
    grad_x = _mm(dh, w_in, "nt", "mla_in_dx", addend=dz1, add_scale=ALPHA)
    g["mla_g_q"], g["mla_g_kv"] = dgq, dgkv
    g["ln_mix_g"] = jnp.concatenate([dg_mix0, dg_mix1], axis=0)
    g["ln_mix_b"] = jnp.concatenate([db_mix0, db_mix1], axis=0)
    g["ln_mlp_g"] = jnp.concatenate([dg_mlp0, dg_mlp1], axis=0)
    g["ln_mlp_b"] = jnp.concatenate([db_mlp0, db_mlp1], axis=0)
    return lpart, grad_x, g, reduced


def _rows(a):
    return a.reshape(-1, D)


def _piece(a, layer):
    return _rows(a if layer is None else a[layer])


def _pack_group(group, parts):
    return jnp.concatenate([_piece(parts[n], l) for n, l in group], axis=0)


def _unpack_group(group, buf, like):
    out, off = {}, 0
    for n, l in group:
        shp = like[n].shape if l is None else like[n].shape[1:]
        out[(n, l)] = buf[off:off + ROWS[n]].reshape(shp)
        off += ROWS[n]
    return out


def _by_name(pieces):
    out = {n: a for (n, l), a in pieces.items() if l is None}
    for n in {n for (n, l) in pieces if l is not None}:
        out[n] = jnp.stack([pieces[(n, 0)], pieces[(n, 1)]])
    return out


def _full_from_gathered(group, wall, shard_shapes):
    out, off = {}, 0
    for n, l in group:
        sl = wall[:, off:off + ROWS[n]]
        off += ROWS[n]
        shp = tuple(shard_shapes[n])
        if n == "mlp_w_up":
            out[(n, l)] = sl
        elif n == "mlp_w_down":
            out[(n, l)] = sl.reshape(DFF, D)
        elif n == "kv_w_shared":
            out[(n, l)] = sl.reshape((4 * shp[0],) + shp[1:])
        else:
            out[(n, l)] = sl.reshape((4 * shp[1],) + shp[2:])
    return out


def _grad_shards(group, g):
    return jnp.concatenate([g[(n, l)].reshape(4, ROWS[n], D) for n, l in group], axis=1)


def _rs_chip_partials(group, g, dist, tag):
    gsh = _grad_shards(group, g)
    return _add_core_halves(gsh, _exchange_core_halves(gsh, tag), dist.cidx, tag)


def _rs_finish(part, slots, dist, tag):
    return _join_core_halves(_sum_slots(slots, part, dist.pos, tag), tag)


SMALL = (("ln_mix_g", 0, 2), ("ln_mix_b", 2, 2), ("ln_mlp_g", 4, 2), ("ln_mlp_b", 6, 2),
         ("swa_sinks", 8, 1), ("mla_g_q", 9, 1), ("mla_g_kv", 10, 1), ("rel_bias", 11, 1))


def _pack_small(parts):
    rows = []
    for n, _, nr in SMALL:
        a = parts[n].reshape(nr, -1).astype(F32)
        rows.append(jnp.pad(a, ((0, 0), (0, D - a.shape[1]))))
    rows.append(jnp.zeros((SMALL_ROWS - 12, D), F32))
    return jnp.concatenate(rows, axis=0)


def _unpack_small(buf, like):
    out = {}
    for n, r0, nr in SMALL:
        size = like[n].size // nr
        out[n] = buf[r0:r0 + nr, :size].reshape(like[n].shape)
    return out


def kernel(x, mla_w_in, mla_g_q, mla_g_kv, mla_w_uq, mla_w_uk, mla_w_uv, mla_w_o, kv_w_shared, swa_w_q, swa_sinks, swa_w_o, rel_bias, mlp_w_up, mlp_w_down, ln_mix_g, ln_mix_b, ln_mlp_g, ln_mlp_b, loss_target, m_mla_w_in, m_mla_g_q, m_mla_g_kv, m_mla_w_uq, m_mla_w_uk, m_mla_w_uv, m_mla_w_o, m_kv_w_shared, m_swa_w_q, m_swa_sinks, m_swa_w_o, m_rel_bias, m_mlp_w_up, m_mlp_w_down, m_ln_mix_g, m_ln_mix_b, m_ln_mlp_g, m_ln_mlp_b, v_mla_w_in, v_mla_g_q, v_mla_g_kv, v_mla_w_uq, v_mla_w_uk, v_mla_w_uv, v_mla_w_o, v_kv_w_shared, v_swa_w_q, v_swa_sinks, v_swa_w_o, v_rel_bias, v_mlp_w_up, v_mlp_w_down, v_ln_mix_g, v_ln_mix_b, v_ln_mlp_g, v_ln_mlp_b):
    names = ["mla_w_in", "mla_g_q", "mla_g_kv", "mla_w_uq", "mla_w_uk", "mla_w_uv", "mla_w_o", "kv_w_shared",
             "swa_w_q", "swa_sinks", "swa_w_o", "rel_bias", "mlp_w_up", "mlp_w_down",
             "ln_mix_g", "ln_mix_b", "ln_mlp_g", "ln_mlp_b"]
    ws = dict(zip(names, [mla_w_in, mla_g_q, mla_g_kv, mla_w_uq, mla_w_uk, mla_w_uv, mla_w_o, kv_w_shared,
                          swa_w_q, swa_sinks, swa_w_o, rel_bias, mlp_w_up, mlp_w_down,
                          ln_mix_g, ln_mix_b, ln_mlp_g, ln_mlp_b]))
    ms = dict(zip(names, [m_mla_w_in, m_mla_g_q, m_mla_g_kv, m_mla_w_uq, m_mla_w_uk, m_mla_w_uv, m_mla_w_o,
                          m_kv_w_shared, m_swa_w_q, m_swa_sinks, m_swa_w_o, m_rel_bias, m_mlp_w_up, m_mlp_w_down,
                          m_ln_mix_g, m_ln_mix_b, m_ln_mlp_g, m_ln_mlp_b]))
    vs = dict(zip(names, [v_mla_w_in, v_mla_g_q, v_mla_g_kv, v_mla_w_uq, v_mla_w_uk, v_mla_w_uv, v_mla_w_o,
                          v_kv_w_shared, v_swa_w_q, v_swa_sinks, v_swa_w_o, v_rel_bias, v_mlp_w_up, v_mlp_w_down,
                          v_ln_mix_g, v_ln_mix_b, v_ln_mlp_g, v_ln_mlp_b]))
    xi, yi, ci = _mesh_pos()
    shard = 2 * xi + yi
    shard_shapes = {n: ws[n].shape for n in ROWS}
    wbf = {n: ws[n].astype(BF16) for n in ROWS}

    early = _pack_group(AG_EARLY, wbf)
    wall = lax.dynamic_update_slice(_allgather_weights(early), early[None], (shard, 0, 0))
    w = _full_from_gathered(AG_EARLY, wall, shard_shapes)
    dist = _Dist(shard=shard, cidx=jnp.reshape(ci, (1,)).astype(jnp.int32),
                 pos=jnp.stack([shard, ci]).astype(jnp.int32), late_pack=_pack_group(AG_LATE, wbf),
                 shard_shapes=shard_shapes)
    gq_slot = lax.dynamic_update_slice(jnp.zeros((1, QR), F32), mla_g_q, (0, shard * (QR // 4)))
    gkv_slot = lax.dynamic_update_slice(jnp.zeros((1, KVR), F32), mla_g_kv, (0, shard * (KVR // 4)))
    gains = jnp.concatenate([jnp.pad(gq_slot, ((0, 0), (0, D - QR))), jnp.pad(gkv_slot, ((0, 0), (0, D - KVR))),
                             jnp.zeros((SMALL_ROWS - 2, D), F32)], axis=0)
    gains = _allreduce_small(gains * 0.5, "allgather_gains")
    w["mla_g_q"], w["mla_g_kv"] = gains[0, :QR], gains[1, :KVR]
    for n in ("swa_sinks", "rel_bias", "ln_mix_g", "ln_mix_b", "ln_mlp_g", "ln_mlp_b"):
        w[n] = ws[n]

    lpart, grad_x, g, reduced = _fwd_bwd(x[0], loss_target[0], w, dist)
    loss = lax.psum(0.5 * jnp.sum(lpart) / D, ("x", "y", "c"))

    part = _rs_chip_partials(RS_REST, g, dist, "rest")
    reduced["rest"] = _rs_finish(part, _exchange_chip_shards(part, "rest"), dist, "rest")

    small_like = {n: g[n] for n, _, _ in SMALL}
    gsm = _unpack_small(_allreduce_small(_pack_small(g), "allreduce_small_grads"), small_like)
    gsm["mla_g_q"] = lax.dynamic_slice(gsm["mla_g_q"], (0, shard * (QR // 4)), (1, QR // 4))
    gsm["mla_g_kv"] = lax.dynamic_slice(gsm["mla_g_kv"], (0, shard * (KVR // 4)), (1, KVR // 4))

    pieces = [{}, {}, {}, {}]
    for key, group in (("mlp1", RS_MLP1), ("mlp0", RS_MLP0), ("rest", RS_REST)):
        outs = _adamw(_pack_group(group, ws), reduced[key], _pack_group(group, ms), _pack_group(group, vs),
                      f"adamw_{key}", tm=_row_tile(reduced[key].shape[0]))
        for dst, buf in zip(pieces, (reduced[key], *outs)):
            dst.update(_unpack_group(group, buf, ws))
    gbig, dbig, mbig, vbig = [_by_name(p) for p in pieces]
    dsm, msm, vsm = _adamw(_pack_small(ws), _pack_small(gsm), _pack_small(ms), _pack_small(vs), "adamw_small", tm=16)
    grads = {**gbig, **gsm}
    delta = {**dbig, **_unpack_small(dsm, ws)}
    new_m = {**mbig, **_unpack_small(msm, ws)}
    new_v = {**vbig, **_unpack_small(vsm, ws)}
    grads = {n: grads[n].reshape(ws[n].shape) for n in names}
    return (loss, grad_x[None], *[grads[n] for n in names], *[delta[n] for n in names],
            *[new_m[n] for n in names], *[new_v[n] for n in names])
```

```python
import collections
import math

import numpy as np
import jax
import jax.numpy as jnp
from jax import lax
from jax.experimental import pallas as pl
from jax.experimental.pallas import tpu as pltpu

F32 = jnp.float32
BF16 = jnp.bfloat16
MESH = pl.DeviceIdType.MESH

D = 1024
DFF = 4096
H = 8
NOPE = 128
ROPE = 64
QR = 384
KVR = 256
RP = 128
KD = KVR + RP
HW = 768
QH = 16
KVH = 4
HD = 64
G = QH // KVH
WIN = 128
NBKT = 32
ALPHA = 4.0 ** 0.25
LN_EPS = 1e-5
RMS_EPS = 1e-6
MLA_SCALE = (NOPE + ROPE) ** -0.5
LOG2E = 1.4426950408889634
LN2 = 0.6931471805599453
QSCALE = MLA_SCALE * LOG2E
SWA_SCALE = HD ** -0.5
NEG = -1e30
LR, B1, B2, ADAM_EPS, WD, STEP = 0.001, 0.9, 0.999, 1e-8, 0.01, 10

VMEM_LIMIT = 48 * 1024 * 1024

NN = (((1,), (0,)), ((), ()))
NT = (((1,), (1,)), ((), ()))
TN = (((0,), (0,)), ((), ()))

ROWS = {"mlp_w_up": 1024, "mlp_w_down": 1024, "mla_w_o": 256, "swa_w_q": 256, "swa_w_o": 256,
        "kv_w_shared": 128, "mla_w_in": 176, "mla_w_uq": 144, "mla_w_uk": 64, "mla_w_uv": 64}
AG_EARLY = (("mla_w_in", None), ("mla_w_uq", None), ("mla_w_uk", None), ("mla_w_uv", None), ("mla_w_o", None))
AG_LATE = (("mlp_w_up", 0), ("mlp_w_up", 1), ("mlp_w_down", 0), ("mlp_w_down", 1),
           ("swa_w_q", None), ("swa_w_o", None), ("kv_w_shared", None))
RS_MLP1 = (("mlp_w_up", 1), ("mlp_w_down", 1))
RS_MLP0 = (("mlp_w_up", 0), ("mlp_w_down", 0))
RS_REST = (("mla_w_o", None), ("swa_w_q", None), ("swa_w_o", None), ("kv_w_shared", None),
           ("mla_w_in", None), ("mla_w_uq", None), ("mla_w_uk", None), ("mla_w_uv", None))
SMALL_ROWS = 16
_Dist = collections.namedtuple("_Dist", "shard cidx pos late_pack shard_shapes")


def _cp(**kw):
    return pltpu.CompilerParams(vmem_limit_bytes=VMEM_LIMIT, **kw)


def _tile(n, pref):
    t = min(n, pref)
    while n % t:
        t -= 128
    return t


def _dot(a, b, dims):
    return lax.dot_general(a, b, dims, preferred_element_type=F32)


def _mm(a, b, mode, name, out_dtype=F32, out_t=False, addend=None, add_scale=1.0, relu2=False, gate_a=None,
        b_view=None, out_view=None, tm=1024, tn=1024, tk=1024):
    blk = 1024
    if b_view is not None:
        kind, b_off = b_view
        assert b.shape[0] == 4 and b.shape[2] == blk and b_off % blk == 0
        bshape = {("cols", "nn"): (blk, 4 * blk), ("cols", "nt"): (blk, 4 * blk),
                  ("rows", "nn"): (4 * blk, blk), ("rows", "nt"): (4 * blk, blk)}[(kind, mode)]
    else:
        bshape = b.shape
    if mode == "nn":
        (m, k), (k2, n) = a.shape, bshape
    elif mode == "nt":
        (m, k), (n, k2) = a.shape, bshape
    else:
        (k, m), (k2, n) = a.shape, bshape
    assert k == k2, (name, a.shape, b.shape)
    tm, tn, tk = _tile(m, tm), _tile(n, tn), _tile(k, tk)
    nk = k // tk
    dims = {"nn": NN, "nt": NT, "tn": TN}[mode]
    if mode == "tn":
        a_spec = pl.BlockSpec((tk, tm), lambda i, j, kk: (kk, i))
    else:
        a_spec = pl.BlockSpec((tm, tk), lambda i, j, kk: (i, kk))
    if b_view is not None:
        assert tn == blk and tk == blk
        ob = b_off // blk
        b_spec = {("cols", "nn"): pl.BlockSpec((None, tk, tn), lambda i, j, kk: (j, ob, 0)),
                  ("cols", "nt"): pl.BlockSpec((None, tn, tk), lambda i, j, kk: (kk, ob, 0)),
                  ("rows", "nn"): pl.BlockSpec((None, tk, tn), lambda i, j, kk: (kk, ob, 0)),
                  ("rows", "nt"): pl.BlockSpec((None, tn, tk), lambda i, j, kk: (j, ob, 0))}[(kind, mode)]
    elif mode == "nt":
        b_spec = pl.BlockSpec((tn, tk), lambda i, j, kk: (j, kk))
    else:
        b_spec = pl.BlockSpec((tk, tn), lambda i, j, kk: (kk, j))
    mn_spec = pl.BlockSpec((tm, tn), lambda i, j, kk: (i, j))
    ins, in_specs = [a, b], [a_spec, b_spec]
    if addend is not None:
        ins.append(addend)
        in_specs.append(mn_spec)
    if gate_a is not None:
        ins.append(gate_a)
        in_specs.append(mn_spec)
    aliases = {}
    if out_view is not None:
        okind, total_rows, o_off, buf = out_view
        assert not out_t and tm == blk and tn == blk and o_off % blk == 0
        oo = o_off // blk
        out_shape = [jax.ShapeDtypeStruct((4, total_rows, blk), out_dtype)]
        if okind == "cols":
            out_specs = [pl.BlockSpec((None, tm, tn), lambda i, j, kk: (j, oo, 0))]
        else:
            out_specs = [pl.BlockSpec((None, tm, tn), lambda i, j, kk: (i, oo, 0))]
        if buf is not None:
            aliases = {len(ins): 0}
            ins.append(buf)
            in_specs.append(pl.BlockSpec(memory_space=pl.ANY))
    elif out_t:
        out_shape = [jax.ShapeDtypeStruct((n, m), out_dtype)]
        out_specs = [pl.BlockSpec((tn, tm), lambda i, j, kk: (j, i))]
    else:
        out_shape = [jax.ShapeDtypeStruct((m, n), out_dtype)]
        out_specs = [mn_spec]
    has_add, has_gate = addend is not None, gate_a is not None

    def kern(*refs):
        a_ref, b_ref = refs[0], refs[1]
        pos = 2
        add_ref = gate_ref = None
        if has_add:
            add_ref = refs[pos]
            pos += 1
        if has_gate:
            gate_ref = refs[pos]
            pos += 1
        o_ref = refs[pos + len(aliases)]
        acc = refs[-1] if nk > 1 else None
        kk = pl.program_id(2)

        def partial():
            return _dot(a_ref[...].astype(BF16), b_ref[...].astype(BF16), dims)

        if nk > 1:
            @pl.when(kk == 0)
            def _():
                acc[...] = partial()

            @pl.when((kk > 0) & (kk < nk - 1))
            def _():
                acc[...] += partial()

        @pl.when(kk == nk - 1)
        def _():
            r = partial() + acc[...] if nk > 1 else partial()
            if has_add:
                r = r + add_scale * add_ref[...].astype(F32)
            if has_gate:
                r = r * (2.0 * jnp.sqrt(gate_ref[...].astype(F32)))
            if relu2:
                hh = jnp.maximum(r, 0.0)
                r = hh * hh
            if out_t:
                r = r.T
            o_ref[...] = r.astype(out_dtype)

    return pl.pallas_call(
        kern, out_shape=out_shape, grid=(m // tm, n // tn, nk), in_specs=in_specs, out_specs=out_specs,
        scratch_shapes=[pltpu.VMEM((tm, tn), F32)] if nk > 1 else [], input_output_aliases=aliases,
        name=name, compiler_params=_cp())(*ins)[0]


def _add_ln(res, y, g, b, name, res_affine=None, tm=256):
    t = res.shape[0]
    tm = min(tm, t)
    affine = res_affine is not None

    def kern(*refs):
        if affine:
            x_ref, y_ref, g_ref, b_ref, g0_ref, b0_ref, ob_ref, xh_ref, r_ref = refs
            x = x_ref[...] * g0_ref[...] + b0_ref[...]
        else:
            x_ref, y_ref, g_ref, b_ref, ob_ref, xh_ref, r_ref = refs
            x = x_ref[...]
        z = ALPHA * x + y_ref[...]
        mu = jnp.mean(z, axis=-1, keepdims=True)
        zc = z - mu
        var = jnp.mean(zc * zc, axis=-1, keepdims=True)
        r = lax.rsqrt(var + LN_EPS)
        xh = zc * r
        ob_ref[...] = (xh * g_ref[...] + b_ref[...]).astype(BF16)
        xh_ref[...] = xh
        r_ref[...] = r

    row = pl.BlockSpec((tm, D), lambda i: (i, 0))
    vec = pl.BlockSpec((1, D), lambda i: (0, 0))
    st = pl.BlockSpec((tm, 1), lambda i: (i, 0))
    ins = [res, y, g.reshape(1, D), b.reshape(1, D)]
    if affine:
        ins += [res_affine[0].reshape(1, D), res_affine[1].reshape(1, D)]
    return pl.pallas_call(
        kern, grid=(t // tm,), in_specs=[row, row] + [vec] * (len(ins) - 2), out_specs=[row, row, st],
        out_shape=[jax.ShapeDtypeStruct((t, D), BF16), jax.ShapeDtypeStruct((t, D), F32),
                   jax.ShapeDtypeStruct((t, 1), F32)],
        name=name, compiler_params=_cp())(*ins)


def _ln_bwd(dout, xhat, rstd, g, name, loss_b=None, tm=256):
    t = dout.shape[0]
    tm = min(tm, t)
    head = loss_b is not None

    def kern(*refs):
        if head:
            do_ref, xh_ref, r_ref, g_ref, b_ref, dz_ref, dzb_ref, dg_ref, db_ref, l_ref = refs
        else:
            do_ref, xh_ref, r_ref, g_ref, dz_ref, dzb_ref, dg_ref, db_ref = refs

        @pl.when(pl.program_id(0) == 0)
        def _():
            dg_ref[...] = jnp.zeros_like(dg_ref)
            db_ref[...] = jnp.zeros_like(db_ref)
            if head:
                l_ref[...] = jnp.zeros_like(l_ref)

        xh = xh_ref[...]
        if head:
            e = xh * g_ref[...] + b_ref[...] - do_ref[...]
            l_ref[...] += jnp.sum(e * e, axis=0, keepdims=True)
            do = e * (1.0 / D)
        else:
            do = do_ref[...]
        dxh = do * g_ref[...]
        m1 = jnp.mean(dxh, axis=-1, keepdims=True)
        m2 = jnp.mean(dxh * xh, axis=-1, keepdims=True)
        dz = r_ref[...] * (dxh - m1 - xh * m2)
        dz_ref[...] = dz
        dzb_ref[...] = dz.astype(BF16)
        dg_ref[...] += jnp.sum(do * xh, axis=0, keepdims=True)
        db_ref[...] += jnp.sum(do, axis=0, keepdims=True)

    row = pl.BlockSpec((tm, D), lambda i: (i, 0))
    vec = pl.BlockSpec((1, D), lambda i: (0, 0))
    st = pl.BlockSpec((tm, 1), lambda i: (i, 0))
    ins = [dout, xhat, rstd, g.reshape(1, D)] + ([loss_b.reshape(1, D)] if head else [])
    return pl.pallas_call(
        kern, grid=(t // tm,), in_specs=[row, row, st] + [vec] * (len(ins) - 3),
        out_specs=[row, row, vec, vec] + ([vec] if head else []),
        out_shape=[jax.ShapeDtypeStruct((t, D), F32), jax.ShapeDtypeStruct((t, D), BF16)]
        + [jax.ShapeDtypeStruct((1, D), F32)] * (3 if head else 2),
        name=name, compiler_params=_cp())(*ins)


def _rope_tables(t):
    half = ROPE // 2
    inv = 10000.0 ** (-jnp.arange(half, dtype=F32) / half)
    ang = jnp.arange(t).astype(F32)[:, None] * inv[None, :]
    cos, sin = jnp.cos(ang), jnp.sin(ang)
    z = jnp.zeros((t, RP - ROPE), F32)
    return jnp.concatenate([cos, cos, z], axis=1), jnp.concatenate([-sin, sin, z], axis=1)


def _swap_halves(x):
    lane = lax.broadcasted_iota(jnp.int32, x.shape, 1)
    return jnp.where(lane < ROPE // 2, pltpu.roll(x, RP - ROPE // 2, 1), pltpu.roll(x, ROPE // 2, 1))


def _rope(x, cos, sin):
    return x * cos + _swap_halves(x) * sin


def _rope_t(gy, cos, sin):
    return gy * cos + _swap_halves(gy * sin)


def _mla_pre(hh, g_q, g_kv, cos, sin, tm=256):
    t = hh.shape[0]
    tm = min(tm, t)

    def kern(h_ref, gq_ref, gkv_ref, c_ref, s_ref, cq_ref, k_ref):
        xq = h_ref[:, 0:QR]
        rq = lax.rsqrt(jnp.mean(xq * xq, axis=-1, keepdims=True) + RMS_EPS)
        cq_ref[...] = (xq * rq * gq_ref[...]).astype(BF16)
        xk = h_ref[:, QR:QR + KVR]
        rk = lax.rsqrt(jnp.mean(xk * xk, axis=-1, keepdims=True) + RMS_EPS)
        k_ref[:, 0:KVR] = (xk * rk * gkv_ref[...]).astype(BF16)
        k_ref[:, KVR:KD] = _rope(h_ref[:, QR + KVR:HW], c_ref[...], s_ref[...]).astype(BF16)

    return pl.pallas_call(
        kern, grid=(t // tm,),
        in_specs=[pl.BlockSpec((tm, HW), lambda i: (i, 0)), pl.BlockSpec((1, QR), lambda i: (0, 0)),
                  pl.BlockSpec((1, KVR), lambda i: (0, 0)), pl.BlockSpec((tm, RP), lambda i: (i, 0)),
                  pl.BlockSpec((tm, RP), lambda i: (i, 0))],
        out_specs=[pl.BlockSpec((tm, QR), lambda i: (i, 0)), pl.BlockSpec((tm, KD), lambda i: (i, 0))],
        out_shape=[jax.ShapeDtypeStruct((t, QR), BF16), jax.ShapeDtypeStruct((t, KD), BF16)],
        name="mla_pre", compiler_params=_cp())(hh, g_q.reshape(1, QR), g_kv.reshape(1, KVR), cos, sin)


def _mla_pre_bwd(hh, dcq, dk, g_q, g_kv, cos, sin, tm=256):
    t = hh.shape[0]
    tm = min(tm, t)

    def rms_bwd(x, dy, g):
        r = lax.rsqrt(jnp.mean(x * x, axis=-1, keepdims=True) + RMS_EPS)
        gdy = dy * g
        dx = r * gdy - x * (r * r * r) * jnp.mean(gdy * x, axis=-1, keepdims=True)
        return dx, jnp.sum(dy * x * r, axis=0, keepdims=True)

    def kern(h_ref, dcq_ref, dk_ref, gq_ref, gkv_ref, c_ref, s_ref, dh_ref, dgq_ref, dgkv_ref):
        @pl.when(pl.program_id(0) == 0)
        def _():
            dgq_ref[...] = jnp.zeros_like(dgq_ref)
            dgkv_ref[...] = jnp.zeros_like(dgkv_ref)

        dxq, dgq = rms_bwd(h_ref[:, 0:QR], dcq_ref[...], gq_ref[...])
        dxk, dgk = rms_bwd(h_ref[:, QR:QR + KVR], dk_ref[:, 0:KVR], gkv_ref[...])
        dh_ref[:, 0:QR] = dxq.astype(BF16)
        dh_ref[:, QR:QR + KVR] = dxk.astype(BF16)
        dh_ref[:, QR + KVR:HW] = _rope_t(dk_ref[:, KVR:KD], c_ref[...], s_ref[...]).astype(BF16)
        dgq_ref[...] += dgq
        dgkv_ref[...] += dgk

    return pl.pallas_call(
        kern, grid=(t // tm,),
        in_specs=[pl.BlockSpec((tm, HW), lambda i: (i, 0)), pl.BlockSpec((tm, QR), lambda i: (i, 0)),
                  pl.BlockSpec((tm, KD), lambda i: (i, 0)), pl.BlockSpec((1, QR), lambda i: (0, 0)),
                  pl.BlockSpec((1, KVR), lambda i: (0, 0)), pl.BlockSpec((tm, RP), lambda i: (i, 0)),
                  pl.BlockSpec((tm, RP), lambda i: (i, 0))],
        out_specs=[pl.BlockSpec((tm, HW), lambda i: (i, 0)), pl.BlockSpec((1, QR), lambda i: (0, 0)),
                   pl.BlockSpec((1, KVR), lambda i: (0, 0))],
        out_shape=[jax.ShapeDtypeStruct((t, HW), BF16), jax.ShapeDtypeStruct((1, QR), F32),
                   jax.ShapeDtypeStruct((1, KVR), F32)],
        name="mla_pre_bwd", compiler_params=_cp())(hh, dcq, dk, g_q.reshape(1, QR), g_kv.reshape(1, KVR), cos, sin)


def _q_prep(q2, wuk_t, cos, sin, tm=256):
    t = q2.shape[0]
    tm = min(tm, t)

    def kern(q_ref, w_ref, c_ref, s_ref, o_ref):
        cos_, sin_ = c_ref[...], s_ref[...]
        for h in range(H):
            qn = q_ref[:, h * NOPE:(h + 1) * NOPE].astype(BF16)
            o_ref[:, h * KD:h * KD + KVR] = (_dot(qn, w_ref[h], NN) * QSCALE).astype(BF16)
            qr = q_ref[:, H * NOPE + h * RP:H * NOPE + (h + 1) * RP]
            o_ref[:, h * KD + KVR:(h + 1) * KD] = (_rope(qr, cos_, sin_) * QSCALE).astype(BF16)

    return pl.pallas_call(
        kern, grid=(t // tm,),
        in_specs=[pl.BlockSpec((tm, 2 * H * NOPE), lambda i: (i, 0)), pl.BlockSpec((H, NOPE, KVR), lambda i: (0, 0, 0)),
                  pl.BlockSpec((tm, RP), lambda i: (i, 0)), pl.BlockSpec((tm, RP), lambda i: (i, 0))],
        out_specs=pl.BlockSpec((tm, H * KD), lambda i: (i, 0)),
        out_shape=jax.ShapeDtypeStruct((t, H * KD), BF16),
        name="q_prep", compiler_params=_cp())(q2, wuk_t, cos, sin)


def _q_prep_bwd(dq_cat, q2, wuk_h, cos, sin, tm=256):
    t = q2.shape[0]
    tm = min(tm, t)

    def kern(dq_ref, q_ref, w_ref, c_ref, s_ref, o_ref, dw_ref):
        @pl.when(pl.program_id(0) == 0)
        def _():
            dw_ref[...] = jnp.zeros_like(dw_ref)

        cos_, sin_ = c_ref[...], s_ref[...]
        for h in range(H):
            dql = dq_ref[:, h * KD:h * KD + KVR].astype(BF16)
            o_ref[:, h * NOPE:(h + 1) * NOPE] = _dot(dql, w_ref[h], NN).astype(BF16)
            dqr = dq_ref[:, h * KD + KVR:(h + 1) * KD]
            o_ref[:, H * NOPE + h * RP:H * NOPE + (h + 1) * RP] = _rope_t(dqr, cos_, sin_).astype(BF16)
            qn = q_ref[:, h * NOPE:(h + 1) * NOPE].astype(BF16)
            dw_ref[h] += _dot(qn, dql, TN)

    return pl.pallas_call(
        kern, grid=(t // tm,),
        in_specs=[pl.BlockSpec((tm, H * KD), lambda i: (i, 0)), pl.BlockSpec((tm, 2 * H * NOPE), lambda i: (i, 0)),
                  pl.BlockSpec((H, KVR, NOPE), lambda i: (0, 0, 0)),
                  pl.BlockSpec((tm, RP), lambda i: (i, 0)), pl.BlockSpec((tm, RP), lambda i: (i, 0))],
        out_specs=[pl.BlockSpec((tm, 2 * H * NOPE), lambda i: (i, 0)), pl.BlockSpec((H, NOPE, KVR), lambda i: (0, 0, 0))],
        out_shape=[jax.ShapeDtypeStruct((t, 2 * H * NOPE), BF16), jax.ShapeDtypeStruct((H, NOPE, KVR), F32)],
        name="q_prep_bwd", compiler_params=_cp())(dq_cat, q2, wuk_h, cos, sin)


def _o_up(o_lat, wuv_h, tm=256):
    t = o_lat.shape[0]
    tm = min(tm, t)

    def kern(x_ref, w_ref, o_ref):
        for h in range(H):
            xl = x_ref[:, h * KVR:(h + 1) * KVR].astype(BF16)
            o_ref[:, h * NOPE:(h + 1) * NOPE] = _dot(xl, w_ref[h], NN).astype(BF16)

    return pl.pallas_call(
        kern, grid=(t // tm,),
        in_specs=[pl.BlockSpec((tm, H * KVR), lambda i: (i, 0)), pl.BlockSpec((H, KVR, NOPE), lambda i: (0, 0, 0))],
        out_specs=pl.BlockSpec((tm, H * NOPE), lambda i: (i, 0)),
        out_shape=jax.ShapeDtypeStruct((t, H * NOPE), BF16),
        name="o_up", compiler_params=_cp())(o_lat, wuv_h)


def _o_up_bwd(do, o_lat, wuv_h, tm=256):
    t = do.shape[0]
    tm = min(tm, t)

    def kern(do_ref, x_ref, w_ref, dx_ref, dw_ref, dlt_ref):
        @pl.when(pl.program_id(0) == 0)
        def _():
            dw_ref[...] = jnp.zeros_like(dw_ref)

        for h in range(H):
            dh_ = do_ref[:, h * NOPE:(h + 1) * NOPE]
            x = x_ref[:, h * KVR:(h + 1) * KVR]
            dx = _dot(dh_, w_ref[h], NT)
            dx_ref[:, h * KVR:(h + 1) * KVR] = dx
            dw_ref[h] += _dot(x.astype(BF16), dh_, TN)
            dl = jnp.broadcast_to(jnp.sum(dx * x, axis=1)[:, None], (tm, 128))
            dlt_ref[h] = dl.T[0:1, :]

    return pl.pallas_call(
        kern, grid=(t // tm,),
        in_specs=[pl.BlockSpec((tm, H * NOPE), lambda i: (i, 0)), pl.BlockSpec((tm, H * KVR), lambda i: (i, 0)),
                  pl.BlockSpec((H, KVR, NOPE), lambda i: (0, 0, 0))],
        out_specs=[pl.BlockSpec((tm, H * KVR), lambda i: (i, 0)), pl.BlockSpec((H, KVR, NOPE), lambda i: (0, 0, 0)),
                   pl.BlockSpec((H, 1, tm), lambda i: (0, 0, i))],
        out_shape=[jax.ShapeDtypeStruct((t, H * KVR), F32), jax.ShapeDtypeStruct((H, KVR, NOPE), F32),
                   jax.ShapeDtypeStruct((H, 1, t), F32)],
        name="o_up_bwd", compiler_params=_cp())(do, o_lat, wuv_h)


def _causal_pairs(nq):
    return [(i, j) for i in range(nq) for j in range(i + 1)]


def _lane_tile(stat, width):
    return jnp.tile(stat, (1, width // 128))


def _flash_fwd(qcat, kc, bq, hb, gather=None):
    t = kc.shape[0]
    nq = t // bq
    pairs = _causal_pairs(nq)
    itab = jnp.asarray(np.array([p[0] for p in pairs], np.int32))
    jtab = jnp.asarray(np.array([p[1] for p in pairs], np.int32))

    ng = H // hb
    hosting = gather is not None

    def kern(it, jt, q_ref, k_ref, *rest):
        if hosting:
            w_ref, o_ref, lset_ref, wall_ref, m_sc, l_sc, acc_sc, send_sems, recv_sems = rest
            ag_start, ag_forward, ag_finish = _allgather_schedule(w_ref, wall_ref, send_sems, recv_sems)
        else:
            o_ref, lset_ref, m_sc, l_sc, acc_sc = rest
        grp = pl.program_id(0)
        st = pl.program_id(1)
        i, j = it[st], jt[st]

        if hosting:
            @pl.when((grp == 0) & (st == 0))
            def _():
                ag_start()

        @pl.when(j == 0)
        def _():
            m_sc[...] = jnp.full_like(m_sc, NEG)
            l_sc[...] = jnp.zeros_like(l_sc)
            acc_sc[...] = jnp.zeros_like(acc_sc)

        def update(masked):
            k = k_ref[...]
            v = k[:, 0:KVR]
            if masked:
                row = lax.broadcasted_iota(jnp.int32, (bq, bq), 0)
                col = lax.broadcasted_iota(jnp.int32, (bq, bq), 1)
                keep = col <= row
            s_next = _dot(q_ref[:, 0:KD], k, NT)
            for hh in range(hb):
                s = s_next
                if hh + 1 < hb:
                    s_next = _dot(q_ref[:, (hh + 1) * KD:(hh + 2) * KD], k, NT)
                if masked:
                    s = jnp.where(keep, s, NEG)
                m_prev = m_sc[hh]
                m_next = jnp.maximum(m_prev, jnp.max(s, axis=1)[:, None])
                p = jnp.exp2(s - _lane_tile(m_next, bq))
                a = jnp.exp2(m_prev - m_next)
                l_sc[hh] = a * l_sc[hh] + jnp.sum(p, axis=1)[:, None]
                acc_sc[hh] = _lane_tile(a, KVR) * acc_sc[hh] + _dot(p.astype(BF16), v, NN)
                m_sc[hh] = m_next

        @pl.when(j < i)
        def _():
            update(False)

        @pl.when(j == i)
        def _():
            update(True)
            for hh in range(hb):
                l = l_sc[hh]
                o_ref[:, hh * KVR:(hh + 1) * KVR] = acc_sc[hh] / _lane_tile(l, KVR)
                lset_ref[hh] = (m_sc[hh] + jnp.log2(l)).T[0:1, :]

        if hosting:
            @pl.when((grp == ng - 1) & (st == 0))
            def _():
                ag_forward()

            @pl.when((grp == ng - 1) & (st == len(pairs) - 1))
            def _():
                ag_finish()

    in_specs = [pl.BlockSpec((bq, hb * KD), lambda g, s, it, jt: (it[s], g)),
                pl.BlockSpec((bq, KD), lambda g, s, it, jt: (jt[s], 0))]
    out_specs = [pl.BlockSpec((bq, hb * KVR), lambda g, s, it, jt: (it[s], g)),
                 pl.BlockSpec((hb, 1, bq), lambda g, s, it, jt: (g, 0, it[s]))]
    out_shape = [jax.ShapeDtypeStruct((t, H * KVR), F32), jax.ShapeDtypeStruct((H, 1, t), F32)]
    scratch = [pltpu.VMEM((hb, bq, 128), F32), pltpu.VMEM((hb, bq, 128), F32), pltpu.VMEM((hb, bq, KVR), F32)]
    args = [itab, jtab, qcat, kc]
    if hosting:
        in_specs.append(ANY)
        out_specs.append(ANY)
        out_shape.append(jax.ShapeDtypeStruct((4,) + gather.shape, gather.dtype))
        scratch += AG_SEMS
        args.append(gather)
    gs = pltpu.PrefetchScalarGridSpec(num_scalar_prefetch=2, grid=(ng, len(pairs)), in_specs=in_specs,
                                      out_specs=out_specs, scratch_shapes=scratch)
    return pl.pallas_call(kern, grid_spec=gs, out_shape=out_shape, name="mla_flash_fwd",
                          compiler_params=_cp())(*args)


def _flash_dkv(qcat, kc, do_lat, lse_t, delta_t, bq, hb, exchange=None):
    hosting = exchange is not None
    t = kc.shape[0]
    nq = t // bq
    ng = H // hb
    npairs = nq * (nq + 1) // 2
    steps = [(j, g, i) for j in range(nq) for g in range(ng) for i in range(j, nq)]
    jtab = jnp.asarray(np.array([s[0] for s in steps], np.int32))
    gtab = jnp.asarray(np.array([s[1] for s in steps], np.int32))
    itab = jnp.asarray(np.array([s[2] for s in steps], np.int32))
    ptab = jnp.asarray(np.array([s[2] * (s[2] + 1) // 2 + s[0] for s in steps], np.int32))

    def kern(jt, gt, it, pt, q_ref, k_ref, do_ref, lset_ref, dlt_ref, *rest):
        if hosting:
            p_ref, dk_ref, ds_ref, slots_ref, dk_sc, dv_sc, send_sems, recv_sems = rest
            xc_start, xc_finish = _chip_exchange_schedule(p_ref, slots_ref, send_sems, recv_sems)
        else:
            dk_ref, ds_ref, dk_sc, dv_sc = rest
        st = pl.program_id(0)
        j, g, i = jt[st], gt[st], it[st]

        if hosting:
            @pl.when(st == 0)
            def _():
                xc_start()

        @pl.when((g == 0) & (i == j))
        def _():
            dk_sc[...] = jnp.zeros_like(dk_sc)
            dv_sc[...] = jnp.zeros_like(dv_sc)

        def update(masked):
            k = k_ref[...]
            v = k[:, 0:KVR]
            if masked:
                row = lax.broadcasted_iota(jnp.int32, (bq, bq), 0)
                col = lax.broadcasted_iota(jnp.int32, (bq, bq), 1)
                keep = row <= col

            def first_matmuls(hh):
                dob = do_ref[:, hh * KVR:(hh + 1) * KVR].astype(BF16)
                return _dot(k, q_ref[:, hh * KD:(hh + 1) * KD], NT), _dot(v, dob, NT), dob

            nxt = first_matmuls(0)
            for hh in range(hb):
                s, dp, dob = nxt
                if hh + 1 < hb:
                    nxt = first_matmuls(hh + 1)
                if masked:
                    s = jnp.where(keep, s, NEG)
                p = jnp.exp2(s - lset_ref[hh])
                dv_sc[...] += _dot(p.astype(BF16), dob, NN)
                dsb = (p * (dp - dlt_ref[hh])).astype(BF16)
                ds_ref[0, 0, hh] = dsb
                dk_sc[...] += _dot(dsb, q_ref[:, hh * KD:(hh + 1) * KD], NN)

        @pl.when(i > j)
        def _():
            update(False)

        @pl.when(i == j)
        def _():
            update(True)

        @pl.when((g == ng - 1) & (i == nq - 1))
        def _():
            dk_ref[:, 0:KVR] = dk_sc[:, 0:KVR] * LN2 + dv_sc[...]
            dk_ref[:, KVR:KD] = dk_sc[:, KVR:KD] * LN2

        if hosting:
            @pl.when(st == len(steps) - 1)
            def _():
                xc_finish()

    in_specs = [pl.BlockSpec((bq, hb * KD), lambda s, jt, gt, it, pt: (it[s], gt[s])),
                pl.BlockSpec((bq, KD), lambda s, jt, gt, it, pt: (jt[s], 0)),
                pl.BlockSpec((bq, hb * KVR), lambda s, jt, gt, it, pt: (it[s], gt[s])),
                pl.BlockSpec((hb, 1, bq), lambda s, jt, gt, it, pt: (gt[s], 0, it[s])),
                pl.BlockSpec((hb, 1, bq), lambda s, jt, gt, it, pt: (gt[s], 0, it[s]))]
    out_specs = [pl.BlockSpec((bq, KD), lambda s, jt, gt, it, pt: (jt[s], 0)),
                 pl.BlockSpec((1, 1, hb, bq, bq), lambda s, jt, gt, it, pt: (gt[s], pt[s], 0, 0, 0))]
    out_shape = [jax.ShapeDtypeStruct((t, KD), F32), jax.ShapeDtypeStruct((ng, npairs, hb, bq, bq), BF16)]
    scratch = [pltpu.VMEM((bq, KD), F32), pltpu.VMEM((bq, KVR), F32)]
    args = [jtab, gtab, itab, ptab, qcat, kc, do_lat, lse_t, delta_t]
    if hosting:
        in_specs.append(ANY)
        out_specs.append(ANY)
        out_shape.append(jax.ShapeDtypeStruct(exchange.shape, exchange.dtype))
        scratch += XCHG_SEMS
        args.append(exchange)
    gs = pltpu.PrefetchScalarGridSpec(num_scalar_prefetch=4, grid=(len(steps),), in_specs=in_specs,
                                      out_specs=out_specs, scratch_shapes=scratch)
    return pl.pallas_call(kern, grid_spec=gs, out_shape=out_shape, name="mla_flash_dkv",
                          compiler_params=_cp())(*args)


def _flash_dq(ds_all, kc_t, bq, hb, exchange=None):
    t = kc_t.shape[1]
    nq = t // bq
    ng = H // hb
    pairs = _causal_pairs(nq)
    itab = jnp.asarray(np.array([p[0] for p in pairs], np.int32))
    jtab = jnp.asarray(np.array([p[1] for p in pairs], np.int32))
    hosting = exchange is not None

    def kern(it, jt, ds_ref, kt_ref, *rest):
        if hosting:
            p_ref, dq_ref, slots_ref, acc_sc, send_sems, recv_sems = rest
            xc_start, xc_finish = _chip_exchange_schedule(p_ref, slots_ref, send_sems, recv_sems)
        else:
            dq_ref, acc_sc = rest
        grp = pl.program_id(0)
        st = pl.program_id(1)
        i, j = it[st], jt[st]
        kt = kt_ref[...]

        if hosting:
            @pl.when((grp == 0) & (st == 0))
            def _():
                xc_start()

        @pl.when(j == 0)
        def _():
            for hh in range(hb):
                acc_sc[hh] = _dot(kt, ds_ref[0, 0, hh], NN)

        @pl.when((j > 0) & (j < i))
        def _():
            for hh in range(hb):
                acc_sc[hh] += _dot(kt, ds_ref[0, 0, hh], NN)

        @pl.when(j == i)
        def _():
            for hh in range(hb):
                tot = _dot(kt, ds_ref[0, 0, hh], NN)
                tot = jnp.where(i > 0, tot + acc_sc[hh], tot)
                dq_ref[:, hh * KD:(hh + 1) * KD] = tot.T * MLA_SCALE

        if hosting:
            @pl.when((grp == ng - 1) & (st == len(pairs) - 1))
            def _():
                xc_finish()

    in_specs = [pl.BlockSpec((1, 1, hb, bq, bq), lambda g, s, it, jt: (g, s, 0, 0, 0)),
                pl.BlockSpec((KD, bq), lambda g, s, it, jt: (0, jt[s]))]
    out_specs = [pl.BlockSpec((bq, hb * KD), lambda g, s, it, jt: (it[s], g))]
    out_shape = [jax.ShapeDtypeStruct((t, H * KD), F32)]
    scratch = [pltpu.VMEM((hb, KD, bq), F32)]
    args = [itab, jtab, ds_all, kc_t]
    if hosting:
        in_specs.append(ANY)
        out_specs.append(ANY)
        out_shape.append(jax.ShapeDtypeStruct(exchange.shape, exchange.dtype))
        scratch += XCHG_SEMS
        args.append(exchange)
    gs = pltpu.PrefetchScalarGridSpec(num_scalar_prefetch=2, grid=(ng, len(pairs)), in_specs=in_specs,
                                      out_specs=out_specs, scratch_shapes=scratch)
    outs = pl.pallas_call(kern, grid_spec=gs, out_shape=out_shape, name="mla_flash_dq",
                          compiler_params=_cp())(*args)
    return outs if hosting else outs[0]


def _bucket_table():
    d = np.arange(WIN)
    max_exact = NBKT // 2
    nf = np.maximum(d, 1).astype(np.float32)
    large = max_exact + (np.log(nf / np.float32(max_exact)) / np.float32(math.log(WIN / max_exact))
                         * np.float32(NBKT - max_exact)).astype(np.int32)
    large = np.minimum(large, NBKT - 1)
    bucket = np.where(d < max_exact, d, large).astype(np.int32)
    jj = np.arange(2 * WIN)[:, None]
    ii = np.arange(WIN)[None, :]
    dist = ii + WIN - jj
    valid = (dist >= 0) & (dist < WIN)
    return np.where(valid, bucket[np.clip(dist, 0, WIN - 1)], -1).astype(np.int32)


def _bias_build(rel_bias, bkt):
    def kern(bk_ref, rb_ref, o_ref):
        bk = bk_ref[...]
        for hd in range(QH):
            acc = jnp.full((2 * WIN, WIN), NEG, F32)
            for b in range(NBKT):
                acc = jnp.where(bk == b, rb_ref[b, hd], acc)
            o_ref[hd] = acc

    return pl.pallas_call(
        kern, in_specs=[pl.BlockSpec(memory_space=pltpu.VMEM), pl.BlockSpec(memory_space=pltpu.SMEM)],
        out_specs=pl.BlockSpec(memory_space=pltpu.VMEM),
        out_shape=jax.ShapeDtypeStruct((QH, 2 * WIN, WIN), F32), name="swa_bias_build")(bkt, rel_bias)


def _bias_bwd(dbias, bkt):
    def kern(db_ref, bk_ref, o_ref):
        bk = bk_ref[...]
        for hd in range(QH):
            g = db_ref[hd]
            for b in range(NBKT):
                r = b * QH + hd
                o_ref[r:r + 1, :] = jnp.sum(jnp.where(bk == b, g, 0.0), axis=0, keepdims=True)

    return pl.pallas_call(
        kern, in_specs=[pl.BlockSpec(memory_space=pltpu.VMEM), pl.BlockSpec(memory_space=pltpu.VMEM)],
        out_specs=pl.BlockSpec(memory_space=pltpu.VMEM),
        out_shape=jax.ShapeDtypeStruct((NBKT * QH, WIN), F32), name="swa_bias_bwd")(dbias, bkt)


def _swa_finish_scores(raw, bias, first):
    s = raw * SWA_SCALE + bias
    if first is not None:
        row = lax.broadcasted_iota(jnp.int32, s.shape, 0)
        s = jnp.where(jnp.logical_or(jnp.logical_not(first), row >= WIN), s, NEG)
    return s


def _swa_fwd(qkv_t, bias, sinks, qb):
    t = qkv_t.shape[1]
    w = qb * WIN
    nst = t // w

    def kern(q_ref, kc_ref, kp_ref, vc_ref, vp_ref, b_ref, sk_ref, o_ref, lse_ref):
        n = pl.program_id(0)
        kfull = jnp.concatenate([kp_ref[...], kc_ref[...]], axis=1)
        vfull = jnp.concatenate([vp_ref[...], vc_ref[...]], axis=1)
        head_row = lax.broadcasted_iota(jnp.int32, (QH, WIN), 0)
        groups = [(b, kh) for b in range(qb) for kh in range(KVH)]

        def raw_scores(b, kh):
            k_band = kfull[kh * HD:(kh + 1) * HD, b * WIN:(b + 2) * WIN]
            return [_dot(k_band, q_ref[(kh * G + g) * HD:(kh * G + g + 1) * HD, b * WIN:(b + 1) * WIN], TN)
                    for g in range(G)]

        o_rows = [[] for _ in range(qb)]
        lse_tiles = [jnp.zeros((QH, WIN), F32) for _ in range(qb)]
        nxt_scores = raw_scores(*groups[0])
        for gi, (b, kh) in enumerate(groups):
            scores = nxt_scores
            if gi + 1 < len(groups):
                nxt_scores = raw_scores(*groups[gi + 1])
            v_band = vfull[kh * HD:(kh + 1) * HD, b * WIN:(b + 2) * WIN]
            for g in range(G):
                hd = kh * G + g
                s = _swa_finish_scores(scores[g], b_ref[hd], (n == 0) if b == 0 else None)
                sink = sk_ref[hd]
                m = jnp.maximum(jnp.max(s, axis=0, keepdims=True), sink)
                p = jnp.exp(s - m)
                den = jnp.sum(p, axis=0, keepdims=True) + jnp.exp(sink - m)
                p = p / den
                o_rows[b].append(_dot(v_band, p.astype(BF16), NN))
                lse_tiles[b] = jnp.where(head_row == hd, m + jnp.log(den), lse_tiles[b])
        o_ref[...] = jnp.concatenate([jnp.concatenate(rows, axis=0) for rows in o_rows], axis=1)
        lse_ref[...] = jnp.concatenate(lse_tiles, axis=1)

    prev = lambda r: (lambda n: (r, jnp.maximum(n * qb - 1, 0)))
    return pl.pallas_call(
        kern, grid=(nst,),
        in_specs=[pl.BlockSpec((QH * HD, w), lambda n: (0, n)),
                  pl.BlockSpec((KVH * HD, w), lambda n: (4, n)), pl.BlockSpec((KVH * HD, WIN), prev(4)),
                  pl.BlockSpec((KVH * HD, w), lambda n: (5, n)), pl.BlockSpec((KVH * HD, WIN), prev(5)),
                  pl.BlockSpec((QH, 2 * WIN, WIN), lambda n: (0, 0, 0)),
                  pl.BlockSpec(memory_space=pltpu.SMEM)],
        out_specs=[pl.BlockSpec((QH * HD, w), lambda n: (0, n)), pl.BlockSpec((QH, w), lambda n: (0, n))],
        out_shape=[jax.ShapeDtypeStruct((QH * HD, t), F32), jax.ShapeDtypeStruct((QH, t), F32)],
        name="swa_fwd", compiler_params=_cp())(qkv_t, qkv_t, qkv_t, qkv_t, qkv_t, bias, sinks)


def _swa_bwd(qkv_t, do_t, o_t, lse, bias, sinks, qb):
    t = qkv_t.shape[1]
    w = qb * WIN
    nst = t // w
    nblk = t // WIN

    def kern(q_ref, kc_ref, kp_ref, vc_ref, vp_ref, do_ref, o_ref, lse_ref, qn_ref, don_ref, on_ref, lsen_ref,
             b_ref, sk_ref, dqkv_ref, db_ref, dsk_ref):
        n = pl.program_id(0)

        @pl.when(n == 0)
        def _():
            db_ref[...] = jnp.zeros_like(db_ref)
            dsk_ref[...] = jnp.zeros_like(dsk_ref)

        kfull = jnp.concatenate([kp_ref[...], kc_ref[...]], axis=1)
        vfull = jnp.concatenate([vp_ref[...], vc_ref[...]], axis=1)
        head_row = lax.broadcasted_iota(jnp.int32, (QH, WIN), 0)
        db_acc = [None] * QH
        dsk_tile = jnp.zeros((QH, WIN), F32)
        prev_part = [[[None] * qb for _ in range(KVH)] for _ in range(2)]
        cur_part = [[[None] * qb for _ in range(KVH)] for _ in range(2)]
        groups = [(b, kh) for b in range(qb) for kh in range(KVH)]

        def first_matmuls(b, kh):
            k_band = kfull[kh * HD:(kh + 1) * HD, b * WIN:(b + 2) * WIN]
            v_band = vfull[kh * HD:(kh + 1) * HD, b * WIN:(b + 2) * WIN]
            out = []
            for g in range(G):
                rs = slice((kh * G + g) * HD, (kh * G + g + 1) * HD)
                dob = do_ref[rs, b * WIN:(b + 1) * WIN].astype(BF16)
                out.append((_dot(k_band, q_ref[rs, b * WIN:(b + 1) * WIN], TN), _dot(v_band, dob, TN), dob))
            return out

        dq_rows = [[] for _ in range(qb)]
        nxt_first = first_matmuls(*groups[0])
        for gi, (b, kh) in enumerate(groups):
            first = nxt_first
            if gi + 1 < len(groups):
                nxt_first = first_matmuls(*groups[gi + 1])
            cs = slice(b * WIN, (b + 1) * WIN)
            k_band = kfull[kh * HD:(kh + 1) * HD, b * WIN:(b + 2) * WIN]
            dk_b = dv_b = None
            for g in range(G):
                hd = kh * G + g
                rs = slice(hd * HD, (hd + 1) * HD)
                raw, dp, dob = first[g]
                lse_h = lse_ref[hd:hd + 1, cs]
                s = _swa_finish_scores(raw, b_ref[hd], (n == 0) if b == 0 else None)
                p = jnp.exp(s - lse_h)
                dl = jnp.sum(do_ref[rs, cs] * o_ref[rs, cs], axis=0, keepdims=True)
                ds = p * (dp - dl)
                db_acc[hd] = ds if db_acc[hd] is None else db_acc[hd] + ds
                dsk_tile = jnp.where(head_row == hd, dsk_tile - jnp.exp(sk_ref[hd] - lse_h) * dl, dsk_tile)
                dss = (ds * SWA_SCALE).astype(BF16)
                dq_rows[b].append(_dot(k_band, dss, NN).astype(BF16))
                dk_h = _dot(q_ref[rs, cs], dss, NT)
                dv_h = _dot(dob, p.astype(BF16), NT)
                dk_b = dk_h if dk_b is None else dk_b + dk_h
                dv_b = dv_h if dv_b is None else dv_b + dv_h
            for which, val in ((0, dk_b), (1, dv_b)):
                prev_part[which][kh][b] = val[:, 0:WIN]
                cur_part[which][kh][b] = val[:, WIN:2 * WIN]
        dq_cols = [jnp.concatenate(rows, axis=0) for rows in dq_rows]

        live = n < nst - 1
        ls = slice((qb - 1) * WIN, qb * WIN)
        halo = [[None] * KVH for _ in range(2)]
        for kh in range(KVH):
            k_last = kc_ref[kh * HD:(kh + 1) * HD, ls]
            v_last = vc_ref[kh * HD:(kh + 1) * HD, ls]
            dk_b = dv_b = None
            for g in range(G):
                hd = kh * G + g
                rs = slice(hd * HD, (hd + 1) * HD)
                q_t = qn_ref[rs, :]
                do = don_ref[rs, :]
                s = _dot(k_last, q_t, TN) * SWA_SCALE + b_ref[hd, 0:WIN, :]
                p = jnp.exp(s - lsen_ref[hd:hd + 1, :])
                dob = do.astype(BF16)
                dp = _dot(v_last, dob, TN)
                dl = jnp.sum(do * on_ref[rs, :], axis=0, keepdims=True)
                dss = (p * (dp - dl) * SWA_SCALE).astype(BF16)
                dk_h = _dot(q_t, dss, NT)
                dv_h = _dot(dob, p.astype(BF16), NT)
                dk_b = dk_h if dk_b is None else dk_b + dk_h
                dv_b = dv_h if dv_b is None else dv_b + dv_h
            halo[0][kh] = jnp.where(live, dk_b, 0.0)
            halo[1][kh] = jnp.where(live, dv_b, 0.0)

        kv_rows = []
        for which in range(2):
            for kh in range(KVH):
                blocks = [cur_part[which][kh][p] + (prev_part[which][kh][p + 1] if p + 1 < qb else halo[which][kh])
                          for p in range(qb)]
                kv_rows.append(jnp.concatenate(blocks, axis=1))
        dqkv_ref[...] = jnp.concatenate(
            [jnp.concatenate(dq_cols, axis=1), jnp.concatenate(kv_rows, axis=0).astype(BF16)], axis=0)
        db_ref[...] += jnp.stack(db_acc)
        dsk_ref[...] += dsk_tile

    prev = lambda r: (lambda n: (r, jnp.maximum(n * qb - 1, 0)))
    nxt = lambda n: (0, jnp.minimum((n + 1) * qb, nblk - 1))
    big = lambda: pl.BlockSpec((QH * HD, w), lambda n: (0, n))
    return pl.pallas_call(
        kern, grid=(nst,),
        in_specs=[big(),
                  pl.BlockSpec((KVH * HD, w), lambda n: (4, n)), pl.BlockSpec((KVH * HD, WIN), prev(4)),
                  pl.BlockSpec((KVH * HD, w), lambda n: (5, n)), pl.BlockSpec((KVH * HD, WIN), prev(5)),
                  big(), big(), pl.BlockSpec((QH, w), lambda n: (0, n)),
                  pl.BlockSpec((QH * HD, WIN), nxt), pl.BlockSpec((QH * HD, WIN), nxt),
                  pl.BlockSpec((QH * HD, WIN), nxt), pl.BlockSpec((QH, WIN), nxt),
                  pl.BlockSpec((QH, 2 * WIN, WIN), lambda n: (0, 0, 0)),
                  pl.BlockSpec(memory_space=pltpu.SMEM)],
        out_specs=[pl.BlockSpec(((QH + 2 * KVH) * HD, w), lambda n: (0, n)),
                   pl.BlockSpec((QH, 2 * WIN, WIN), lambda n: (0, 0, 0)),
                   pl.BlockSpec((QH, WIN), lambda n: (0, 0))],
        out_shape=[jax.ShapeDtypeStruct(((QH + 2 * KVH) * HD, t), BF16),
                   jax.ShapeDtypeStruct((QH, 2 * WIN, WIN), F32), jax.ShapeDtypeStruct((QH, WIN), F32)],
        name="swa_bwd", compiler_params=_cp())(
            qkv_t, qkv_t, qkv_t, qkv_t, qkv_t, do_t, o_t, lse, qkv_t, do_t, o_t, lse, bias, sinks)


def _adamw_math(w, g, m, v):
    nm = B1 * m + (1.0 - B1) * g
    nv = B2 * v + (1.0 - B2) * (g * g)
    mhat = nm * (1.0 / (1.0 - B1 ** STEP))
    vhat = nv * (1.0 / (1.0 - B2 ** STEP))
    return -LR * (mhat / (jnp.sqrt(vhat) + ADAM_EPS) + WD * w), nm, nv


def _adamw_layers(w, m, v, g0buf, g1buf, off, name, tm=512):
    rows = w.shape[1]
    nb, ob = rows // tm, off // tm

    def kern(w_ref, m_ref, v_ref, g0_ref, g1_ref, gr_ref, d_ref, nm_ref, nv_ref):
        g_ = jnp.where(pl.program_id(0) == 0, g0_ref[...], g1_ref[...])
        gr_ref[...] = g_
        d_ref[...], nm_ref[...], nv_ref[...] = _adamw_math(w_ref[...], g_, m_ref[...], v_ref[...])

    lay = pl.BlockSpec((None, tm, D), lambda l, i: (l, i, 0))
    gsp = pl.BlockSpec((tm, D), lambda l, i: (ob + i, 0))
    return pl.pallas_call(
        kern, grid=(2, nb), in_specs=[lay, lay, lay, gsp, gsp], out_specs=[lay] * 4,
        out_shape=[jax.ShapeDtypeStruct(w.shape, F32)] * 4, name=name, compiler_params=_cp())(w, m, v, g0buf, g1buf)


def _adamw(w, g, m, v, name, tm=544):
    r = w.shape[0]
    tm = r if r % tm else tm

    def kern(w_ref, g_ref, m_ref, v_ref, d_ref, nm_ref, nv_ref):
        d_ref[...], nm_ref[...], nv_ref[...] = _adamw_math(w_ref[...], g_ref[...], m_ref[...], v_ref[...])

    row = pl.BlockSpec((tm, D), lambda i: (i, 0))
    sds = jax.ShapeDtypeStruct((r, D), F32)
    return pl.pallas_call(kern, grid=(r // tm,), in_specs=[row] * 4, out_specs=[row] * 3, out_shape=[sds] * 3,
                          name=name, compiler_params=_cp())(w, g, m, v)


def _mesh_pos():
    return lax.axis_index("x"), lax.axis_index("y"), lax.axis_index("c")


ANY = pl.BlockSpec(memory_space=pl.ANY)


AG_SEMS = [pltpu.SemaphoreType.DMA((6,)), pltpu.SemaphoreType.DMA((6,))]
XCHG_SEMS = [pltpu.SemaphoreType.DMA((3,)), pltpu.SemaphoreType.DMA((3,))]


def _allgather_schedule(w_ref, out_ref, send_sems, recv_sems):
    half = w_ref.shape[0] // 2
    x, y, c = _mesh_pos()
    me, sibling = (x, y, c), (x, y, 1 - c)
    chips = [(1 - x, y), (x, 1 - y), (1 - x, 1 - y)]

    def rows(px, py, pc):
        return out_ref.at[2 * px + py, pl.ds(pc * half, half), :]

    def copy(k, block, to, src=None):
        return pltpu.make_async_remote_copy(
            src_ref=rows(*block) if src is None else src, dst_ref=rows(*block),
            send_sem=send_sems.at[k], recv_sem=recv_sems.at[k], device_id=to, device_id_type=MESH)

    def first():
        return [copy(j, me, (*chip, c), src=w_ref.at[pl.ds(c * half, half), :]) for j, chip in enumerate(chips)]

    def passed():
        return [copy(3 + j, (*chip, c), sibling) for j, chip in enumerate(chips)]

    def start():
        for cp in first():
            cp.start()

    def forward():
        for j, chip in enumerate(chips):
            copy(j, (*chip, c), me).wait_recv()
            passed()[j].start()

    def finish():
        for j, chip in enumerate(chips):
            copy(3 + j, (*chip, 1 - c), me).wait_recv()
        for cp in first() + passed():
            cp.wait_send()

    return start, forward, finish


def _allgather_weights(wpack):
    def body(w_ref, out_ref, send_sems, recv_sems):
        start, forward, finish = _allgather_schedule(w_ref, out_ref, send_sems, recv_sems)
        start()
        forward()
        finish()

    return pl.pallas_call(
        body, out_shape=jax.ShapeDtypeStruct((4,) + wpack.shape, wpack.dtype), in_specs=[ANY], out_specs=ANY,
        scratch_shapes=AG_SEMS, name="allgather_weights")(wpack)


def _row_tile(rows):
    t = min(rows, 512)
    while rows % t or t % 16:
        t -= 16
    return t


def _exchange_core_halves(g, tag):
    half = g.shape[1] // 2

    def body(g_ref, out_ref, send_sem, recv_sem):
        x, y, c = _mesh_pos()
        cp = pltpu.make_async_remote_copy(
            src_ref=g_ref.at[:, pl.ds((1 - c) * half, half), :], dst_ref=out_ref,
            send_sem=send_sem, recv_sem=recv_sem, device_id=(x, y, 1 - c), device_id_type=MESH)
        cp.start()
        cp.wait()

    return pl.pallas_call(
        body, out_shape=jax.ShapeDtypeStruct((4, half, D), g.dtype), in_specs=[ANY], out_specs=ANY,
        scratch_shapes=[pltpu.SemaphoreType.DMA, pltpu.SemaphoreType.DMA], name=f"rs_exchange_cores_{tag}")(g)


def _add_core_halves(g, other, cidx, tag):
    half = other.shape[1]
    tm = _row_tile(half)
    nb = half // tm

    def kern(c_ref, a_ref, b_ref, o_ref):
        o_ref[...] = (a_ref[...] + b_ref[...]).astype(BF16)

    gs = pltpu.PrefetchScalarGridSpec(
        num_scalar_prefetch=1, grid=(4, nb),
        in_specs=[pl.BlockSpec((1, tm, D), lambda s, i, c: (s, c[0] * nb + i, 0)),
                  pl.BlockSpec((1, tm, D), lambda s, i, c: (s, i, 0))],
        out_specs=pl.BlockSpec((1, tm, D), lambda s, i, c: (s, i, 0)))
    return pl.pallas_call(kern, grid_spec=gs, out_shape=jax.ShapeDtypeStruct(other.shape, BF16),
                          name=f"rs_add_cores_{tag}", compiler_params=_cp())(cidx, g, other)


def _chip_exchange_schedule(p_ref, out_ref, send_sems, recv_sems):
    x, y, c = _mesh_pos()
    me = 2 * x + y
    chips = [(1 - x, y), (x, 1 - y), (1 - x, 1 - y)]

    def sends():
        return [pltpu.make_async_remote_copy(
            src_ref=p_ref.at[2 * px + py], dst_ref=out_ref.at[me], send_sem=send_sems.at[j],
            recv_sem=recv_sems.at[j], device_id=(px, py, c), device_id_type=MESH) for j, (px, py) in enumerate(chips)]

    def start():
        for cp in sends():
            cp.start()

    def finish():
        for j, (px, py) in enumerate(chips):
            pltpu.make_async_remote_copy(
                src_ref=p_ref.at[me], dst_ref=out_ref.at[2 * px + py], send_sem=send_sems.at[j],
                recv_sem=recv_sems.at[j], device_id=(px, py, c), device_id_type=MESH).wait_recv()
        for cp in sends():
            cp.wait_send()

    return start, finish


def _exchange_chip_shards(p, tag):
    def body(p_ref, out_ref, send_sems, recv_sems):
        start, finish = _chip_exchange_schedule(p_ref, out_ref, send_sems, recv_sems)
        start()
        finish()

    return pl.pallas_call(
        body, out_shape=jax.ShapeDtypeStruct(p.shape, p.dtype), in_specs=[ANY], out_specs=ANY,
        scratch_shapes=XCHG_SEMS, name=f"rs_exchange_chips_{tag}")(p)


def _sum_slots(slots, p, pos, tag):
    half = slots.shape[1]
    tm = _row_tile(half)
    nb = half // tm

    def kern(pos_ref, p_ref, s1_ref, s2_ref, s3_ref, o_ref):
        o_ref[...] = ((p_ref[0].astype(F32) + s1_ref[0].astype(F32)) + s2_ref[0].astype(F32)) + s3_ref[0].astype(F32)

    def slot(k):
        return pl.BlockSpec((1, tm, D), lambda i, pos: ((pos[0] + k) % 4, i, 0))

    gs = pltpu.PrefetchScalarGridSpec(
        num_scalar_prefetch=1, grid=(nb,), in_specs=[slot(0), slot(1), slot(2), slot(3)],
        out_specs=pl.BlockSpec((tm, D), lambda i, pos: (pos[1] * nb + i, 0)))
    return pl.pallas_call(kern, grid_spec=gs, out_shape=jax.ShapeDtypeStruct((2 * half, D), F32),
                          name=f"rs_sum_chips_{tag}", compiler_params=_cp())(pos, p, slots, slots, slots)


def _join_core_halves(r, tag):
    half = r.shape[0] // 2

    def body(r_ref, out_ref, send_sem, recv_sem):
        x, y, c = _mesh_pos()
        mine = out_ref.at[pl.ds(c * half, half), :]
        cp = pltpu.make_async_remote_copy(
            src_ref=mine, dst_ref=mine, send_sem=send_sem, recv_sem=recv_sem,
            device_id=(x, y, 1 - c), device_id_type=MESH)
        cp.start()
        theirs = out_ref.at[pl.ds((1 - c) * half, half), :]
        pltpu.make_async_remote_copy(
            src_ref=theirs, dst_ref=theirs, send_sem=send_sem, recv_sem=recv_sem,
            device_id=(x, y, 1 - c), device_id_type=MESH).wait_recv()
        cp.wait_send()

    return pl.pallas_call(
        body, out_shape=jax.ShapeDtypeStruct(r.shape, r.dtype), in_specs=[ANY], out_specs=ANY,
        input_output_aliases={0: 0},
        scratch_shapes=[pltpu.SemaphoreType.DMA, pltpu.SemaphoreType.DMA],
        name=f"rs_join_cores_{tag}")(r)


def _allreduce_small(v, name):
    def body(v_ref, out_ref, gat, send_sems, recv_sems):
        x, y, c = _mesh_pos()
        me = 4 * x + 2 * y + c
        gat[me] = v_ref[...]
        sends = []
        for k in range(1, 8):
            peer = (x ^ (k >> 2), y ^ ((k >> 1) & 1), c ^ (k & 1))
            cp = pltpu.make_async_remote_copy(
                src_ref=v_ref, dst_ref=gat.at[me], send_sem=send_sems.at[k - 1], recv_sem=recv_sems.at[k - 1],
                device_id=peer, device_id_type=MESH)
            cp.start()
            sends.append(cp)
        for k in range(1, 8):
            px, py, pc = x ^ (k >> 2), y ^ ((k >> 1) & 1), c ^ (k & 1)
            pltpu.make_async_remote_copy(
                src_ref=v_ref, dst_ref=gat.at[4 * px + 2 * py + pc], send_sem=send_sems.at[k - 1],
                recv_sem=recv_sems.at[k - 1], device_id=(px, py, pc), device_id_type=MESH).wait_recv()
        for cp in sends:
            cp.wait_send()
        acc = gat[0]
        for d in range(1, 8):
            acc = acc + gat[d]
        out_ref[...] = acc

    return pl.pallas_call(
        body, out_shape=jax.ShapeDtypeStruct(v.shape, F32),
        in_specs=[pl.BlockSpec(memory_space=pltpu.VMEM)], out_specs=pl.BlockSpec(memory_space=pltpu.VMEM),
        scratch_shapes=[pltpu.VMEM((8,) + v.shape, F32), pltpu.SemaphoreType.DMA((7,)), pltpu.SemaphoreType.DMA((7,))],
        name=name)(v)


def _mlp_fwd(xb, w_up, w_down, tag):
    a = _mm(xb, w_up[0], "nn", f"mlp_up_{tag}", out_dtype=BF16, relu2=True, b_view=("cols", w_up[1]))
    return a, _mm(a, w_down[0], "nn", f"mlp_down_{tag}", b_view=("rows", w_down[1]))


def _mlp_bwd(dz, dzb, xb, a, w_up, w_down, tag):
    du = _mm(dzb, w_down[0], "nt", f"mlp_down_dx_{tag}", out_dtype=BF16, gate_a=a, b_view=("rows", w_down[1]))
    gsh = _mm(xb, du, "tn", f"mlp_up_dw_{tag}", out_view=("cols", 2 * ROWS["mlp_w_up"], 0, None))
    gsh = _mm(a, dzb, "tn", f"mlp_down_dw_{tag}", out_view=("rows", 2 * ROWS["mlp_w_up"], ROWS["mlp_w_up"], gsh))
    dx = _mm(du, w_up[0], "nt", f"mlp_up_dx_{tag}", addend=dz, add_scale=ALPHA, b_view=("cols", w_up[1]))
    return dx, gsh


def _fwd_bwd(x, target, w, dist=None, bq=512, qb=4, hb=4):
    t = x.shape[0]
    bq = min(bq, t)
    qb = min(qb, t // WIN)
    cos, sin = _rope_tables(t)
    bkt = jnp.asarray(_bucket_table())
    w_in = jnp.pad(w[("mla_w_in", None)], ((0, 0), (0, HW - (QR + KVR + ROPE))))
    wuq = w[("mla_w_uq", None)]
    wq2 = jnp.concatenate([wuq[:, :, :NOPE].reshape(QR, H * NOPE),
                           jnp.pad(wuq[:, :, NOPE:], ((0, 0), (0, 0), (0, RP - ROPE))).reshape(QR, H * RP)], axis=1)
    wuk_t = w[("mla_w_uk", None)].transpose(1, 2, 0)
    wuk_h = w[("mla_w_uk", None)].transpose(1, 0, 2)
    wuv_h = w[("mla_w_uv", None)].transpose(1, 0, 2)
    w_o = w[("mla_w_o", None)]
    sinks = w["swa_sinks"].reshape(QH)
    lnp = lambda n, l: w[n][l]
    reduced = {}

    hh = _mm(x, w_in, "nn", "mla_in")
    cq, kc = _mla_pre(hh, w["mla_g_q"], w["mla_g_kv"], cos, sin)
    q2 = _mm(cq, wq2, "nn", "mla_uq")
    qcat = _q_prep(q2, wuk_t, cos, sin)
    if dist is None:
        o_lat, lse0_t = _flash_fwd(qcat, kc, bq, hb)
    else:
        o_lat, lse0_t, wall = _flash_fwd(qcat, kc, bq, hb, gather=dist.late_pack)
        wall = lax.dynamic_update_slice(wall, dist.late_pack[None], (dist.shard, 0, 0))
        w = {**w, **_full_from_gathered(AG_LATE, wall, dist.shard_shapes)}
    wqkv = jnp.concatenate([w[("swa_w_q", None)], w[("kv_w_shared", None)]], axis=1)
    wqkv_t = wqkv.T
    wo_s = w[("swa_w_o", None)]
    o0 = _o_up(o_lat, wuv_h)
    y0 = _mm(o0, w_o, "nn", "mla_out")
    x1b, xh1, r1 = _add_ln(x, y0, lnp("ln_mix_g", 0), lnp("ln_mix_b", 0), "ln_mix_0")
    a0, f0 = _mlp_fwd(x1b, w[("mlp_w_up", 0)], w[("mlp_w_down", 0)], 0)
    x2b, xh2, r2 = _add_ln(xh1, f0, lnp("ln_mlp_g", 0), lnp("ln_mlp_b", 0), "ln_mlp_0",
                           res_affine=(lnp("ln_mix_g", 0), lnp("ln_mix_b", 0)))
    bias = _bias_build(w["rel_bias"], bkt)
    qkv_t = _mm(x2b, wqkv, "nn", "swa_qkv", out_dtype=BF16, out_t=True)
    os_t, lse1 = _swa_fwd(qkv_t, bias, sinks, qb)
    y1 = _mm(os_t, wo_s, "tn", "swa_out")
    x3b, xh3, r3 = _add_ln(xh2, y1, lnp("ln_mix_g", 1), lnp("ln_mix_b", 1), "ln_mix_1",
                           res_affine=(lnp("ln_mlp_g", 0), lnp("ln_mlp_b", 0)))
    a1, f1 = _mlp_fwd(x3b, w[("mlp_w_up", 1)], w[("mlp_w_down", 1)], 1)
    _, xh4, r4 = _add_ln(xh3, f1, lnp("ln_mlp_g", 1), lnp("ln_mlp_b", 1), "ln_mlp_1",
                         res_affine=(lnp("ln_mix_g", 1), lnp("ln_mix_b", 1)))

    g = {}
    dz4, dz4b, dg_mlp1, db_mlp1, lpart = _ln_bwd(target, xh4, r4, lnp("ln_mlp_g", 1), "ln_mlp_1_bwd",
                                                 loss_b=lnp("ln_mlp_b", 1))
    dx3, g["mlp1"] = _mlp_bwd(dz4, dz4b, x3b, a1, w[("mlp_w_up", 1)], w[("mlp_w_down", 1)], 1)
    part1 = _rs_chip_partials(g["mlp1"], dist, "mlp1") if dist is not None else None
    dz3, dz3b, dg_mix1, db_mix1 = _ln_bwd(dx3, xh3, r3, lnp("ln_mix_g", 1), "ln_mix_1_bwd")
    dos_t = _mm(dz3b, wo_s, "nt", "swa_out_dx", out_t=True)
    g[("swa_w_o", None)] = _mm(os_t, dz3b, "nn", "swa_out_dw")
    dqkv_t, dbias, dsk = _swa_bwd(qkv_t, dos_t, os_t, lse1, bias, sinks, qb)
    dwqkv = _mm(dqkv_t, x2b, "nn", "swa_qkv_dw").T
    g[("swa_w_q", None)], g[("kv_w_shared", None)] = dwqkv[:, :QH * HD], dwqkv[:, QH * HD:]
    dx2 = _mm(dqkv_t, wqkv_t, "tn", "swa_qkv_dx", addend=dz3, add_scale=ALPHA)
    g["rel_bias"] = jnp.sum(_bias_bwd(dbias, bkt), axis=-1).reshape(NBKT, QH)
    g["swa_sinks"] = jnp.sum(dsk, axis=-1).reshape(1, QH)
    dz2, dz2b, dg_mlp0, db_mlp0 = _ln_bwd(dx2, xh2, r2, lnp("ln_mlp_g", 0), "ln_mlp_0_bwd")
    dx1, g["mlp0"] = _mlp_bwd(dz2, dz2b, x1b, a0, w[("mlp_w_up", 0)], w[("mlp_w_down", 0)], 0)
    part0 = _rs_chip_partials(g["mlp0"], dist, "mlp0") if dist is not None else None
    dz1, dz1b, dg_mix0, db_mix0 = _ln_bwd(dx1, xh1, r1, lnp("ln_mix_g", 0), "ln_mix_0_bwd")
    do0 = _mm(dz1b, w_o, "nt", "mla_out_dx", out_dtype=BF16)
    g[("mla_w_o", None)] = _mm(o0, dz1b, "tn", "mla_out_dw")
    do_lat, dwuv, delta_t = _o_up_bwd(do0, o_lat, wuv_h)
    g[("mla_w_uv", None)] = dwuv.transpose(1, 0, 2)
    if dist is None:
        dk, ds_all = _flash_dkv(qcat, kc, do_lat, lse0_t, delta_t, bq, hb)
        dq_cat = _flash_dq(ds_all, kc.T, bq, hb)
    else:
        dk, ds_all, slots1 = _flash_dkv(qcat, kc, do_lat, lse0_t, delta_t, bq, hb, exchange=part1)
        dq_cat, slots0 = _flash_dq(ds_all, kc.T, bq, hb, exchange=part0)
        reduced["mlp1"] = _rs_finish(part1, slots1, dist, "mlp1")
        reduced["mlp0"] = _rs_finish(part0, slots0, dist, "mlp0")
    dq2, dwuk = _q_prep_bwd(dq_cat, q2, wuk_h, cos, sin)
    g[("mla_w_uk", None)] = dwuk.transpose(2, 0, 1)
    dcq = _mm(dq2, wq2, "nt", "mla_uq_dx")
    dwq2 = _mm(cq, dq2, "tn", "mla_uq_dw")
    g[("mla_w_uq", None)] = jnp.concatenate([dwq2[:, :H * NOPE].reshape(QR, H, NOPE),
                                             dwq2[:, H * NOPE:].reshape(QR, H, RP)[:, :, :ROPE]], axis=2)
    dh, dgq, dgkv = _mla_pre_bwd(hh, dcq, dk, w["mla_g_q"], w["mla_g_kv"], cos, sin)
    g[("mla_w_in", None)] = _mm(x, dh, "tn", "mla_in_dw")[:, :QR + KVR + ROPE]
    grad_x = _mm(dh, w_in, "nt", "mla_in_dx", addend=dz1, add_scale=ALPHA)
    g["mla_g_q"], g["mla_g_kv"] = dgq, dgkv
    g["ln_mix_g"] = jnp.concatenate([dg_mix0, dg_mix1], axis=0)
    g["ln_mix_b"] = jnp.concatenate([db_mix0, db_mix1], axis=0)
    g["ln_mlp_g"] = jnp.concatenate([dg_mlp0, dg_mlp1], axis=0)
    g["ln_mlp_b"] = jnp.concatenate([db_mlp0, db_mlp1], axis=0)
    return lpart, grad_x, g, reduced


def _rows(a):
    return a.reshape(-1, D)


def _piece(a, layer):
    return _rows(a if layer is None else a[layer])


def _pack_group(group, parts):
    return jnp.concatenate([_piece(parts[n], l) for n, l in group], axis=0)


def _unpack_group(group, buf, like):
    out, off = {}, 0
    for n, l in group:
        shp = like[n].shape if l is None else like[n].shape[1:]
        out[(n, l)] = buf[off:off + ROWS[n]].reshape(shp)
        off += ROWS[n]
    return out


def _by_name(pieces):
    out = {n: a for (n, l), a in pieces.items() if l is None}
    for n in {n for (n, l) in pieces if l is not None}:
        out[n] = jnp.stack([pieces[(n, 0)], pieces[(n, 1)]])
    return out


def _full_from_gathered(group, wall, shard_shapes):
    out, off = {}, 0
    for n, l in group:
        shp = tuple(shard_shapes[n])
        if n in ("mlp_w_up", "mlp_w_down"):
            out[(n, l)] = (wall, off)
        elif n == "kv_w_shared":
            out[(n, l)] = wall[:, off:off + ROWS[n]].reshape((4 * shp[0],) + shp[1:])
        else:
            out[(n, l)] = wall[:, off:off + ROWS[n]].reshape((4 * shp[1],) + shp[2:])
        off += ROWS[n]
    return out


def _grad_shards(group, g):
    return jnp.concatenate([g[(n, l)].reshape(4, ROWS[n], D) for n, l in group], axis=1)


def _rs_chip_partials(gsh, dist, tag):
    return _add_core_halves(gsh, _exchange_core_halves(gsh, tag), dist.cidx, tag)


def _rs_finish(part, slots, dist, tag):
    return _join_core_halves(_sum_slots(slots, part, dist.pos, tag), tag)


SMALL = (("ln_mix_g", 0, 2), ("ln_mix_b", 2, 2), ("ln_mlp_g", 4, 2), ("ln_mlp_b", 6, 2),
         ("swa_sinks", 8, 1), ("mla_g_q", 9, 1), ("mla_g_kv", 10, 1), ("rel_bias", 11, 1))


def _pack_small(parts):
    rows = []
    for n, _, nr in SMALL:
        a = parts[n].reshape(nr, -1).astype(F32)
        rows.append(jnp.pad(a, ((0, 0), (0, D - a.shape[1]))))
    rows.append(jnp.zeros((SMALL_ROWS - 12, D), F32))
    return jnp.concatenate(rows, axis=0)


def _unpack_small(buf, like):
    out = {}
    for n, r0, nr in SMALL:
        size = like[n].size // nr
        out[n] = buf[r0:r0 + nr, :size].reshape(like[n].shape)
    return out


def kernel(x, mla_w_in, mla_g_q, mla_g_kv, mla_w_uq, mla_w_uk, mla_w_uv, mla_w_o, kv_w_shared, swa_w_q, swa_sinks, swa_w_o, rel_bias, mlp_w_up, mlp_w_down, ln_mix_g, ln_mix_b, ln_mlp_g, ln_mlp_b, loss_target, m_mla_w_in, m_mla_g_q, m_mla_g_kv, m_mla_w_uq, m_mla_w_uk, m_mla_w_uv, m_mla_w_o, m_kv_w_shared, m_swa_w_q, m_swa_sinks, m_swa_w_o, m_rel_bias, m_mlp_w_up, m_mlp_w_down, m_ln_mix_g, m_ln_mix_b, m_ln_mlp_g, m_ln_mlp_b, v_mla_w_in, v_mla_g_q, v_mla_g_kv, v_mla_w_uq, v_mla_w_uk, v_mla_w_uv, v_mla_w_o, v_kv_w_shared, v_swa_w_q, v_swa_sinks, v_swa_w_o, v_rel_bias, v_mlp_w_up, v_mlp_w_down, v_ln_mix_g, v_ln_mix_b, v_ln_mlp_g, v_ln_mlp_b):
    names = ["mla_w_in", "mla_g_q", "mla_g_kv", "mla_w_uq", "mla_w_uk", "mla_w_uv", "mla_w_o", "kv_w_shared",
             "swa_w_q", "swa_sinks", "swa_w_o", "rel_bias", "mlp_w_up", "mlp_w_down",
             "ln_mix_g", "ln_mix_b", "ln_mlp_g", "ln_mlp_b"]
    ws = dict(zip(names, [mla_w_in, mla_g_q, mla_g_kv, mla_w_uq, mla_w_uk, mla_w_uv, mla_w_o, kv_w_shared,
                          swa_w_q, swa_sinks, swa_w_o, rel_bias, mlp_w_up, mlp_w_down,
                          ln_mix_g, ln_mix_b, ln_mlp_g, ln_mlp_b]))
    ms = dict(zip(names, [m_mla_w_in, m_mla_g_q, m_mla_g_kv, m_mla_w_uq, m_mla_w_uk, m_mla_w_uv, m_mla_w_o,
                          m_kv_w_shared, m_swa_w_q, m_swa_sinks, m_swa_w_o, m_rel_bias, m_mlp_w_up, m_mlp_w_down,
                          m_ln_mix_g, m_ln_mix_b, m_ln_mlp_g, m_ln_mlp_b]))
    vs = dict(zip(names, [v_mla_w_in, v_mla_g_q, v_mla_g_kv, v_mla_w_uq, v_mla_w_uk, v_mla_w_uv, v_mla_w_o,
                          v_kv_w_shared, v_swa_w_q, v_swa_sinks, v_swa_w_o, v_rel_bias, v_mlp_w_up, v_mlp_w_down,
                          v_ln_mix_g, v_ln_mix_b, v_ln_mlp_g, v_ln_mlp_b]))
    xi, yi, ci = _mesh_pos()
    shard = 2 * xi + yi
    shard_shapes = {n: ws[n].shape for n in ROWS}
    wbf = {n: ws[n].astype(BF16) for n in ROWS}

    early = _pack_group(AG_EARLY, wbf)
    wall = lax.dynamic_update_slice(_allgather_weights(early), early[None], (shard, 0, 0))
    w = _full_from_gathered(AG_EARLY, wall, shard_shapes)
    dist = _Dist(shard=shard, cidx=jnp.reshape(ci, (1,)).astype(jnp.int32),
                 pos=jnp.stack([shard, ci]).astype(jnp.int32), late_pack=_pack_group(AG_LATE, wbf),
                 shard_shapes=shard_shapes)
    gq_slot = lax.dynamic_update_slice(jnp.zeros((1, QR), F32), mla_g_q, (0, shard * (QR // 4)))
    gkv_slot = lax.dynamic_update_slice(jnp.zeros((1, KVR), F32), mla_g_kv, (0, shard * (KVR // 4)))
    gains = jnp.concatenate([jnp.pad(gq_slot, ((0, 0), (0, D - QR))), jnp.pad(gkv_slot, ((0, 0), (0, D - KVR))),
                             jnp.zeros((SMALL_ROWS - 2, D), F32)], axis=0)
    gains = _allreduce_small(gains * 0.5, "allgather_gains")
    w["mla_g_q"], w["mla_g_kv"] = gains[0, :QR], gains[1, :KVR]
    for n in ("swa_sinks", "rel_bias", "ln_mix_g", "ln_mix_b", "ln_mlp_g", "ln_mlp_b"):
        w[n] = ws[n]

    lpart, grad_x, g, reduced = _fwd_bwd(x[0], loss_target[0], w, dist)
    loss = lax.psum(0.5 * jnp.sum(lpart) / D, ("x", "y", "c"))

    part = _rs_chip_partials(_grad_shards(RS_REST, g), dist, "rest")
    reduced["rest"] = _rs_finish(part, _exchange_chip_shards(part, "rest"), dist, "rest")

    small_like = {n: g[n] for n, _, _ in SMALL}
    gsm = _unpack_small(_allreduce_small(_pack_small(g), "allreduce_small_grads"), small_like)
    gsm["mla_g_q"] = lax.dynamic_slice(gsm["mla_g_q"], (0, shard * (QR // 4)), (1, QR // 4))
    gsm["mla_g_kv"] = lax.dynamic_slice(gsm["mla_g_kv"], (0, shard * (KVR // 4)), (1, KVR // 4))

    gbig, dbig, mbig, vbig = {}, {}, {}, {}
    for n in ("mlp_w_up", "mlp_w_down"):
        off = 0 if n == "mlp_w_up" else ROWS["mlp_w_up"]
        gbig[n], dbig[n], mbig[n], vbig[n] = _adamw_layers(
            ws[n], ms[n], vs[n], reduced["mlp0"], reduced["mlp1"], off, f"adamw_{n}")
    outs = _adamw(_pack_group(RS_REST, ws), reduced["rest"], _pack_group(RS_REST, ms), _pack_group(RS_REST, vs),
                  "adamw_rest", tm=_row_tile(reduced["rest"].shape[0]))
    for dst, buf in zip((gbig, dbig, mbig, vbig), (reduced["rest"], *outs)):
        dst.update(_by_name(_unpack_group(RS_REST, buf, ws)))
    dsm, msm, vsm = _adamw(_pack_small(ws), _pack_small(gsm), _pack_small(ms), _pack_small(vs), "adamw_small", tm=16)
    grads = {**gbig, **gsm}
    delta = {**dbig, **_unpack_small(dsm, ws)}
    new_m = {**mbig, **_unpack_small(msm, ws)}
    new_v = {**vbig, **_unpack_small(vsm, ws)}
    grads = {n: grads[n].reshape(ws[n].shape) for n in names}
    return (loss, grad_x[None], *[grads[n] for n in names], *[delta[n] for n in names],
            *[new_m[n] for n in names], *[new_v[n] for n in names])
```

```python
import collections
import math

import numpy as np
import jax
import jax.numpy as jnp
from jax import lax
from jax.experimental import pallas as pl
from jax.experimental.pallas import tpu as pltpu

F32 = jnp.float32
BF16 = jnp.bfloat16
MESH = pl.DeviceIdType.MESH

D = 1024
DFF = 4096
H = 8
NOPE = 128
ROPE = 64
QR = 384
KVR = 256
RP = 128
KD = KVR + RP
HW = 768
QH = 16
KVH = 4
HD = 64
G = QH // KVH
WIN = 128
NBKT = 32
ALPHA = 4.0 ** 0.25
LN_EPS = 1e-5
RMS_EPS = 1e-6
MLA_SCALE = (NOPE + ROPE) ** -0.5
LOG2E = 1.4426950408889634
LN2 = 0.6931471805599453
QSCALE = MLA_SCALE * LOG2E
SWA_SCALE = HD ** -0.5
NEG = -1e30
LR, B1, B2, ADAM_EPS, WD, STEP = 0.001, 0.9, 0.999, 1e-8, 0.01, 10

VMEM_LIMIT = 48 * 1024 * 1024

NN = (((1,), (0,)), ((), ()))
NT = (((1,), (1,)), ((), ()))
TN = (((0,), (0,)), ((), ()))

ROWS = {"mlp_w_up": 1024, "mlp_w_down": 1024, "mla_w_o": 256, "swa_w_q": 256, "swa_w_o": 256,
        "kv_w_shared": 128, "mla_w_in": 176, "mla_w_uq": 144, "mla_w_uk": 64, "mla_w_uv": 64}
AG_EARLY = (("mla_w_in", None), ("mla_w_uq", None), ("mla_w_uk", None), ("mla_w_uv", None), ("mla_w_o", None))
AG_LATE = (("mlp_w_up", 0), ("mlp_w_up", 1), ("mlp_w_down", 0), ("mlp_w_down", 1),
           ("swa_w_q", None), ("swa_w_o", None), ("kv_w_shared", None))
RS_MLP1 = (("mlp_w_up", 1), ("mlp_w_down", 1))
RS_MLP0 = (("mlp_w_up", 0), ("mlp_w_down", 0))
RS_REST = (("mla_w_o", None), ("swa_w_q", None), ("swa_w_o", None), ("kv_w_shared", None),
           ("mla_w_in", None), ("mla_w_uq", None), ("mla_w_uk", None), ("mla_w_uv", None))
SMALL_ROWS = 16
_Dist = collections.namedtuple("_Dist", "shard cidx pos late_pack shard_shapes")


def _cp(**kw):
    return pltpu.CompilerParams(vmem_limit_bytes=VMEM_LIMIT, **kw)


def _tile(n, pref):
    t = min(n, pref)
    while n % t:
        t -= 128
    return t


def _dot(a, b, dims):
    return lax.dot_general(a, b, dims, preferred_element_type=F32)


def _mm(a, b, mode, name, out_dtype=F32, out_t=False, addend=None, add_scale=1.0, relu2=False, gate_a=None,
        b_view=None, out_view=None, tm=1024, tn=1024, tk=1024):
    blk = 1024
    if b_view is not None:
        kind, b_off = b_view
        assert b.shape[0] == 4 and b.shape[2] == blk and b_off % blk == 0
        bshape = {("cols", "nn"): (blk, 4 * blk), ("cols", "nt"): (blk, 4 * blk),
                  ("rows", "nn"): (4 * blk, blk), ("rows", "nt"): (4 * blk, blk)}[(kind, mode)]
    else:
        bshape = b.shape
    if mode == "nn":
        (m, k), (k2, n) = a.shape, bshape
    elif mode == "nt":
        (m, k), (n, k2) = a.shape, bshape
    else:
        (k, m), (k2, n) = a.shape, bshape
    assert k == k2, (name, a.shape, b.shape)
    tm, tn, tk = _tile(m, tm), _tile(n, tn), _tile(k, tk)
    nk = k // tk
    dims = {"nn": NN, "nt": NT, "tn": TN}[mode]
    if mode == "tn":
        a_spec = pl.BlockSpec((tk, tm), lambda i, j, kk: (kk, i))
    else:
        a_spec = pl.BlockSpec((tm, tk), lambda i, j, kk: (i, kk))
    if b_view is not None:
        assert tn == blk and tk == blk
        ob = b_off // blk
        b_spec = {("cols", "nn"): pl.BlockSpec((None, tk, tn), lambda i, j, kk: (j, ob, 0)),
                  ("cols", "nt"): pl.BlockSpec((None, tn, tk), lambda i, j, kk: (kk, ob, 0)),
                  ("rows", "nn"): pl.BlockSpec((None, tk, tn), lambda i, j, kk: (kk, ob, 0)),
                  ("rows", "nt"): pl.BlockSpec((None, tn, tk), lambda i, j, kk: (j, ob, 0))}[(kind, mode)]
    elif mode == "nt":
        b_spec = pl.BlockSpec((tn, tk), lambda i, j, kk: (j, kk))
    else:
        b_spec = pl.BlockSpec((tk, tn), lambda i, j, kk: (kk, j))
    mn_spec = pl.BlockSpec((tm, tn), lambda i, j, kk: (i, j))
    ins, in_specs = [a, b], [a_spec, b_spec]
    if addend is not None:
        ins.append(addend)
        in_specs.append(mn_spec)
    if gate_a is not None:
        ins.append(gate_a)
        in_specs.append(mn_spec)
    aliases = {}
    if out_view is not None:
        okind, total_rows, o_off, buf = out_view
        assert not out_t and tm == blk and tn == blk and o_off % blk == 0
        oo = o_off // blk
        out_shape = [jax.ShapeDtypeStruct((4, total_rows, blk), out_dtype)]
        if okind == "cols":
            out_specs = [pl.BlockSpec((None, tm, tn), lambda i, j, kk: (j, oo, 0))]
        else:
            out_specs = [pl.BlockSpec((None, tm, tn), lambda i, j, kk: (i, oo, 0))]
        if buf is not None:
            aliases = {len(ins): 0}
            ins.append(buf)
            in_specs.append(pl.BlockSpec(memory_space=pl.ANY))
    elif out_t:
        out_shape = [jax.ShapeDtypeStruct((n, m), out_dtype)]
        out_specs = [pl.BlockSpec((tn, tm), lambda i, j, kk: (j, i))]
    else:
        out_shape = [jax.ShapeDtypeStruct((m, n), out_dtype)]
        out_specs = [mn_spec]
    has_add, has_gate = addend is not None, gate_a is not None

    def kern(*refs):
        a_ref, b_ref = refs[0], refs[1]
        pos = 2
        add_ref = gate_ref = None
        if has_add:
            add_ref = refs[pos]
            pos += 1
        if has_gate:
            gate_ref = refs[pos]
            pos += 1
        o_ref = refs[pos + len(aliases)]
        acc = refs[-1] if nk > 1 else None
        kk = pl.program_id(2)

        def partial():
            return _dot(a_ref[...].astype(BF16), b_ref[...].astype(BF16), dims)

        if nk > 1:
            @pl.when(kk == 0)
            def _():
                acc[...] = partial()

            @pl.when((kk > 0) & (kk < nk - 1))
            def _():
                acc[...] += partial()

        @pl.when(kk == nk - 1)
        def _():
            r = partial() + acc[...] if nk > 1 else partial()
            if has_add:
                r = r + add_scale * add_ref[...].astype(F32)
            if has_gate:
                r = r * (2.0 * jnp.sqrt(gate_ref[...].astype(F32)))
            if relu2:
                hh = jnp.maximum(r, 0.0)
                r = hh * hh
            if out_t:
                r = r.T
            o_ref[...] = r.astype(out_dtype)

    return pl.pallas_call(
        kern, out_shape=out_shape, grid=(m // tm, n // tn, nk), in_specs=in_specs, out_specs=out_specs,
        scratch_shapes=[pltpu.VMEM((tm, tn), F32)] if nk > 1 else [], input_output_aliases=aliases,
        name=name, compiler_params=_cp())(*ins)[0]


def _add_ln(res, y, g, b, name, res_affine=None, tm=256):
    t = res.shape[0]
    tm = min(tm, t)
    affine = res_affine is not None

    def kern(*refs):
        if affine:
            x_ref, y_ref, g_ref, b_ref, g0_ref, b0_ref, ob_ref, xh_ref, r_ref = refs
            x = x_ref[...] * g0_ref[...] + b0_ref[...]
        else:
            x_ref, y_ref, g_ref, b_ref, ob_ref, xh_ref, r_ref = refs
            x = x_ref[...]
        z = ALPHA * x + y_ref[...]
        mu = jnp.mean(z, axis=-1, keepdims=True)
        zc = z - mu
        var = jnp.mean(zc * zc, axis=-1, keepdims=True)
        r = lax.rsqrt(var + LN_EPS)
        xh = zc * r
        ob_ref[...] = (xh * g_ref[...] + b_ref[...]).astype(BF16)
        xh_ref[...] = xh
        r_ref[...] = r

    row = pl.BlockSpec((tm, D), lambda i: (i, 0))
    vec = pl.BlockSpec((1, D), lambda i: (0, 0))
    st = pl.BlockSpec((tm, 1), lambda i: (i, 0))
    ins = [res, y, g.reshape(1, D), b.reshape(1, D)]
    if affine:
        ins += [res_affine[0].reshape(1, D), res_affine[1].reshape(1, D)]
    return pl.pallas_call(
        kern, grid=(t // tm,), in_specs=[row, row] + [vec] * (len(ins) - 2), out_specs=[row, row, st],
        out_shape=[jax.ShapeDtypeStruct((t, D), BF16), jax.ShapeDtypeStruct((t, D), F32),
                   jax.ShapeDtypeStruct((t, 1), F32)],
        name=name, compiler_params=_cp())(*ins)


def _ln_bwd(dout, xhat, rstd, g, name, loss_b=None, tm=256):
    t = dout.shape[0]
    tm = min(tm, t)
    head = loss_b is not None

    def kern(*refs):
        if head:
            do_ref, xh_ref, r_ref, g_ref, b_ref, dz_ref, dzb_ref, dg_ref, db_ref, l_ref = refs
        else:
            do_ref, xh_ref, r_ref, g_ref, dz_ref, dzb_ref, dg_ref, db_ref = refs

        @pl.when(pl.program_id(0) == 0)
        def _():
            dg_ref[...] = jnp.zeros_like(dg_ref)
            db_ref[...] = jnp.zeros_like(db_ref)
            if head:
                l_ref[...] = jnp.zeros_like(l_ref)

        xh = xh_ref[...]
        if head:
            e = xh * g_ref[...] + b_ref[...] - do_ref[...]
            l_ref[...] += jnp.sum(e * e, axis=0, keepdims=True)
            do = e * (1.0 / D)
        else:
            do = do_ref[...]
        dxh = do * g_ref[...]
        m1 = jnp.mean(dxh, axis=-1, keepdims=True)
        m2 = jnp.mean(dxh * xh, axis=-1, keepdims=True)
        dz = r_ref[...] * (dxh - m1 - xh * m2)
        dz_ref[...] = dz
        dzb_ref[...] = dz.astype(BF16)
        dg_ref[...] += jnp.sum(do * xh, axis=0, keepdims=True)
        db_ref[...] += jnp.sum(do, axis=0, keepdims=True)

    row = pl.BlockSpec((tm, D), lambda i: (i, 0))
    vec = pl.BlockSpec((1, D), lambda i: (0, 0))
    st = pl.BlockSpec((tm, 1), lambda i: (i, 0))
    ins = [dout, xhat, rstd, g.reshape(1, D)] + ([loss_b.reshape(1, D)] if head else [])
    return pl.pallas_call(
        kern, grid=(t // tm,), in_specs=[row, row, st] + [vec] * (len(ins) - 3),
        out_specs=[row, row, vec, vec] + ([vec] if head else []),
        out_shape=[jax.ShapeDtypeStruct((t, D), F32), jax.ShapeDtypeStruct((t, D), BF16)]
        + [jax.ShapeDtypeStruct((1, D), F32)] * (3 if head else 2),
        name=name, compiler_params=_cp())(*ins)


def _rope_tables(t):
    half = ROPE // 2
    inv = 10000.0 ** (-jnp.arange(half, dtype=F32) / half)
    ang = jnp.arange(t).astype(F32)[:, None] * inv[None, :]
    cos, sin = jnp.cos(ang), jnp.sin(ang)
    z = jnp.zeros((t, RP - ROPE), F32)
    return jnp.concatenate([cos, cos, z], axis=1), jnp.concatenate([-sin, sin, z], axis=1)


def _swap_halves(x):
    lane = lax.broadcasted_iota(jnp.int32, x.shape, 1)
    return jnp.where(lane < ROPE // 2, pltpu.roll(x, RP - ROPE // 2, 1), pltpu.roll(x, ROPE // 2, 1))


def _rope(x, cos, sin):
    return x * cos + _swap_halves(x) * sin


def _rope_t(gy, cos, sin):
    return gy * cos + _swap_halves(gy * sin)


def _mla_pre(hh, g_q, g_kv, cos, sin, tm=256):
    t = hh.shape[0]
    tm = min(tm, t)

    def kern(h_ref, gq_ref, gkv_ref, c_ref, s_ref, cq_ref, k_ref):
        xq = h_ref[:, 0:QR]
        rq = lax.rsqrt(jnp.mean(xq * xq, axis=-1, keepdims=True) + RMS_EPS)
        cq_ref[...] = (xq * rq * gq_ref[...]).astype(BF16)
        xk = h_ref[:, QR:QR + KVR]
        rk = lax.rsqrt(jnp.mean(xk * xk, axis=-1, keepdims=True) + RMS_EPS)
        k_ref[:, 0:KVR] = (xk * rk * gkv_ref[...]).astype(BF16)
        k_ref[:, KVR:KD] = _rope(h_ref[:, QR + KVR:HW], c_ref[...], s_ref[...]).astype(BF16)

    return pl.pallas_call(
        kern, grid=(t // tm,),
        in_specs=[pl.BlockSpec((tm, HW), lambda i: (i, 0)), pl.BlockSpec((1, QR), lambda i: (0, 0)),
                  pl.BlockSpec((1, KVR), lambda i: (0, 0)), pl.BlockSpec((tm, RP), lambda i: (i, 0)),
                  pl.BlockSpec((tm, RP), lambda i: (i, 0))],
        out_specs=[pl.BlockSpec((tm, QR), lambda i: (i, 0)), pl.BlockSpec((tm, KD), lambda i: (i, 0))],
        out_shape=[jax.ShapeDtypeStruct((t, QR), BF16), jax.ShapeDtypeStruct((t, KD), BF16)],
        name="mla_pre", compiler_params=_cp())(hh, g_q.reshape(1, QR), g_kv.reshape(1, KVR), cos, sin)


def _mla_pre_bwd(hh, dcq, dk, g_q, g_kv, cos, sin, tm=256):
    t = hh.shape[0]
    tm = min(tm, t)

    def rms_bwd(x, dy, g):
        r = lax.rsqrt(jnp.mean(x * x, axis=-1, keepdims=True) + RMS_EPS)
        gdy = dy * g
        dx = r * gdy - x * (r * r * r) * jnp.mean(gdy * x, axis=-1, keepdims=True)
        return dx, jnp.sum(dy * x * r, axis=0, keepdims=True)

    def kern(h_ref, dcq_ref, dk_ref, gq_ref, gkv_ref, c_ref, s_ref, dh_ref, dgq_ref, dgkv_ref):
        @pl.when(pl.program_id(0) == 0)
        def _():
            dgq_ref[...] = jnp.zeros_like(dgq_ref)
            dgkv_ref[...] = jnp.zeros_like(dgkv_ref)

        dxq, dgq = rms_bwd(h_ref[:, 0:QR], dcq_ref[...], gq_ref[...])
        dxk, dgk = rms_bwd(h_ref[:, QR:QR + KVR], dk_ref[:, 0:KVR], gkv_ref[...])
        dh_ref[:, 0:QR] = dxq.astype(BF16)
        dh_ref[:, QR:QR + KVR] = dxk.astype(BF16)
        dh_ref[:, QR + KVR:HW] = _rope_t(dk_ref[:, KVR:KD], c_ref[...], s_ref[...]).astype(BF16)
        dgq_ref[...] += dgq
        dgkv_ref[...] += dgk

    return pl.pallas_call(
        kern, grid=(t // tm,),
        in_specs=[pl.BlockSpec((tm, HW), lambda i: (i, 0)), pl.BlockSpec((tm, QR), lambda i: (i, 0)),
                  pl.BlockSpec((tm, KD), lambda i: (i, 0)), pl.BlockSpec((1, QR), lambda i: (0, 0)),
                  pl.BlockSpec((1, KVR), lambda i: (0, 0)), pl.BlockSpec((tm, RP), lambda i: (i, 0)),
                  pl.BlockSpec((tm, RP), lambda i: (i, 0))],
        out_specs=[pl.BlockSpec((tm, HW), lambda i: (i, 0)), pl.BlockSpec((1, QR), lambda i: (0, 0)),
                   pl.BlockSpec((1, KVR), lambda i: (0, 0))],
        out_shape=[jax.ShapeDtypeStruct((t, HW), BF16), jax.ShapeDtypeStruct((1, QR), F32),
                   jax.ShapeDtypeStruct((1, KVR), F32)],
        name="mla_pre_bwd", compiler_params=_cp())(hh, dcq, dk, g_q.reshape(1, QR), g_kv.reshape(1, KVR), cos, sin)


def _q_prep(q2, wuk_t, cos, sin, tm=256):
    t = q2.shape[0]
    tm = min(tm, t)

    def kern(q_ref, w_ref, c_ref, s_ref, o_ref):
        cos_, sin_ = c_ref[...], s_ref[...]
        for h in range(H):
            qn = q_ref[:, h * NOPE:(h + 1) * NOPE].astype(BF16)
            o_ref[:, h * KD:h * KD + KVR] = (_dot(qn, w_ref[h], NN) * QSCALE).astype(BF16)
            qr = q_ref[:, H * NOPE + h * RP:H * NOPE + (h + 1) * RP]
            o_ref[:, h * KD + KVR:(h + 1) * KD] = (_rope(qr, cos_, sin_) * QSCALE).astype(BF16)

    return pl.pallas_call(
        kern, grid=(t // tm,),
        in_specs=[pl.BlockSpec((tm, 2 * H * NOPE), lambda i: (i, 0)), pl.BlockSpec((H, NOPE, KVR), lambda i: (0, 0, 0)),
                  pl.BlockSpec((tm, RP), lambda i: (i, 0)), pl.BlockSpec((tm, RP), lambda i: (i, 0))],
        out_specs=pl.BlockSpec((tm, H * KD), lambda i: (i, 0)),
        out_shape=jax.ShapeDtypeStruct((t, H * KD), BF16),
        name="q_prep", compiler_params=_cp())(q2, wuk_t, cos, sin)


def _q_prep_bwd(dq_cat, q2, wuk_h, cos, sin, tm=256):
    t = q2.shape[0]
    tm = min(tm, t)

    def kern(dq_ref, q_ref, w_ref, c_ref, s_ref, o_ref, dw_ref):
        @pl.when(pl.program_id(0) == 0)
        def _():
            dw_ref[...] = jnp.zeros_like(dw_ref)

        cos_, sin_ = c_ref[...], s_ref[...]
        for h in range(H):
            dql = dq_ref[:, h * KD:h * KD + KVR].astype(BF16)
            o_ref[:, h * NOPE:(h + 1) * NOPE] = _dot(dql, w_ref[h], NN).astype(BF16)
            dqr = dq_ref[:, h * KD + KVR:(h + 1) * KD]
            o_ref[:, H * NOPE + h * RP:H * NOPE + (h + 1) * RP] = _rope_t(dqr, cos_, sin_).astype(BF16)
            qn = q_ref[:, h * NOPE:(h + 1) * NOPE].astype(BF16)
            dw_ref[h] += _dot(qn, dql, TN)

    return pl.pallas_call(
        kern, grid=(t // tm,),
        in_specs=[pl.BlockSpec((tm, H * KD), lambda i: (i, 0)), pl.BlockSpec((tm, 2 * H * NOPE), lambda i: (i, 0)),
                  pl.BlockSpec((H, KVR, NOPE), lambda i: (0, 0, 0)),
                  pl.BlockSpec((tm, RP), lambda i: (i, 0)), pl.BlockSpec((tm, RP), lambda i: (i, 0))],
        out_specs=[pl.BlockSpec((tm, 2 * H * NOPE), lambda i: (i, 0)), pl.BlockSpec((H, NOPE, KVR), lambda i: (0, 0, 0))],
        out_shape=[jax.ShapeDtypeStruct((t, 2 * H * NOPE), BF16), jax.ShapeDtypeStruct((H, NOPE, KVR), F32)],
        name="q_prep_bwd", compiler_params=_cp())(dq_cat, q2, wuk_h, cos, sin)


def _o_up(o_lat, wuv_h, tm=256):
    t = o_lat.shape[0]
    tm = min(tm, t)

    def kern(x_ref, w_ref, o_ref):
        for h in range(H):
            xl = x_ref[:, h * KVR:(h + 1) * KVR].astype(BF16)
            o_ref[:, h * NOPE:(h + 1) * NOPE] = _dot(xl, w_ref[h], NN).astype(BF16)

    return pl.pallas_call(
        kern, grid=(t // tm,),
        in_specs=[pl.BlockSpec((tm, H * KVR), lambda i: (i, 0)), pl.BlockSpec((H, KVR, NOPE), lambda i: (0, 0, 0))],
        out_specs=pl.BlockSpec((tm, H * NOPE), lambda i: (i, 0)),
        out_shape=jax.ShapeDtypeStruct((t, H * NOPE), BF16),
        name="o_up", compiler_params=_cp())(o_lat, wuv_h)


def _o_up_bwd(do, o_lat, wuv_h, tm=256):
    t = do.shape[0]
    tm = min(tm, t)

    def kern(do_ref, x_ref, w_ref, dx_ref, dw_ref, dlt_ref):
        @pl.when(pl.program_id(0) == 0)
        def _():
            dw_ref[...] = jnp.zeros_like(dw_ref)

        for h in range(H):
            dh_ = do_ref[:, h * NOPE:(h + 1) * NOPE]
            x = x_ref[:, h * KVR:(h + 1) * KVR]
            dx = _dot(dh_, w_ref[h], NT)
            dx_ref[:, h * KVR:(h + 1) * KVR] = dx.astype(BF16)
            dw_ref[h] += _dot(x.astype(BF16), dh_, TN)
            dl = jnp.broadcast_to(jnp.sum(dx * x, axis=1)[:, None], (tm, 128))
            dlt_ref[h] = dl.T[0:1, :]

    return pl.pallas_call(
        kern, grid=(t // tm,),
        in_specs=[pl.BlockSpec((tm, H * NOPE), lambda i: (i, 0)), pl.BlockSpec((tm, H * KVR), lambda i: (i, 0)),
                  pl.BlockSpec((H, KVR, NOPE), lambda i: (0, 0, 0))],
        out_specs=[pl.BlockSpec((tm, H * KVR), lambda i: (i, 0)), pl.BlockSpec((H, KVR, NOPE), lambda i: (0, 0, 0)),
                   pl.BlockSpec((H, 1, tm), lambda i: (0, 0, i))],
        out_shape=[jax.ShapeDtypeStruct((t, H * KVR), BF16), jax.ShapeDtypeStruct((H, KVR, NOPE), F32),
                   jax.ShapeDtypeStruct((H, 1, t), F32)],
        name="o_up_bwd", compiler_params=_cp())(do, o_lat, wuv_h)


def _causal_pairs(nq):
    return [(i, j) for i in range(nq) for j in range(i + 1)]


def _lane_tile(stat, width):
    return jnp.tile(stat, (1, width // 128))


def _flash_fwd(qcat, kc, bq, hb, gather=None):
    t = kc.shape[0]
    nq = t // bq
    pairs = _causal_pairs(nq)
    itab = jnp.asarray(np.array([p[0] for p in pairs], np.int32))
    jtab = jnp.asarray(np.array([p[1] for p in pairs], np.int32))

    ng = H // hb
    hosting = gather is not None

    def kern(it, jt, q_ref, k_ref, *rest):
        if hosting:
            w_ref, o_ref, lset_ref, wall_ref, m_sc, l_sc, acc_sc, send_sems, recv_sems = rest
            ag_start, ag_forward, ag_finish = _allgather_schedule(w_ref, wall_ref, send_sems, recv_sems)
        else:
            o_ref, lset_ref, m_sc, l_sc, acc_sc = rest
        grp = pl.program_id(0)
        st = pl.program_id(1)
        i, j = it[st], jt[st]

        if hosting:
            @pl.when((grp == 0) & (st == 0))
            def _():
                ag_start()

        @pl.when(j == 0)
        def _():
            m_sc[...] = jnp.full_like(m_sc, NEG)
            l_sc[...] = jnp.zeros_like(l_sc)
            acc_sc[...] = jnp.zeros_like(acc_sc)

        def update(masked):
            k = k_ref[...]
            v = k[:, 0:KVR]
            if masked:
                row = lax.broadcasted_iota(jnp.int32, (bq, bq), 0)
                col = lax.broadcasted_iota(jnp.int32, (bq, bq), 1)
                keep = col <= row
            s_next = _dot(q_ref[:, 0:KD], k, NT)
            for hh in range(hb):
                s = s_next
                if hh + 1 < hb:
                    s_next = _dot(q_ref[:, (hh + 1) * KD:(hh + 2) * KD], k, NT)
                if masked:
                    s = jnp.where(keep, s, NEG)
                m_prev = m_sc[hh]
                m_next = jnp.maximum(m_prev, jnp.max(s, axis=1)[:, None])
                p = jnp.exp2(s - _lane_tile(m_next, bq))
                a = jnp.exp2(m_prev - m_next)
                l_sc[hh] = a * l_sc[hh] + jnp.sum(p, axis=1)[:, None]
                acc_sc[hh] = _lane_tile(a, KVR) * acc_sc[hh] + _dot(p.astype(BF16), v, NN)
                m_sc[hh] = m_next

        @pl.when(j < i)
        def _():
            update(False)

        @pl.when(j == i)
        def _():
            update(True)
            for hh in range(hb):
                l = l_sc[hh]
                o_ref[:, hh * KVR:(hh + 1) * KVR] = acc_sc[hh] / _lane_tile(l, KVR)
                lset_ref[hh] = (m_sc[hh] + jnp.log2(l)).T[0:1, :]

        if hosting:
            @pl.when((grp == ng - 1) & (st == 0))
            def _():
                ag_forward()

            @pl.when((grp == ng - 1) & (st == len(pairs) - 1))
            def _():
                ag_finish()

    in_specs = [pl.BlockSpec((bq, hb * KD), lambda g, s, it, jt: (it[s], g)),
                pl.BlockSpec((bq, KD), lambda g, s, it, jt: (jt[s], 0))]
    out_specs = [pl.BlockSpec((bq, hb * KVR), lambda g, s, it, jt: (it[s], g)),
                 pl.BlockSpec((hb, 1, bq), lambda g, s, it, jt: (g, 0, it[s]))]
    out_shape = [jax.ShapeDtypeStruct((t, H * KVR), F32), jax.ShapeDtypeStruct((H, 1, t), F32)]
    scratch = [pltpu.VMEM((hb, bq, 128), F32), pltpu.VMEM((hb, bq, 128), F32), pltpu.VMEM((hb, bq, KVR), F32)]
    args = [itab, jtab, qcat, kc]
    if hosting:
        in_specs.append(ANY)
        out_specs.append(ANY)
        out_shape.append(jax.ShapeDtypeStruct((4,) + gather.shape, gather.dtype))
        scratch += AG_SEMS
        args.append(gather)
    gs = pltpu.PrefetchScalarGridSpec(num_scalar_prefetch=2, grid=(ng, len(pairs)), in_specs=in_specs,
                                      out_specs=out_specs, scratch_shapes=scratch)
    return pl.pallas_call(kern, grid_spec=gs, out_shape=out_shape, name="mla_flash_fwd",
                          compiler_params=_cp())(*args)


def _flash_dkv(qcat, kc, do_lat, lse_t, delta_t, bq, hb, exchange=None):
    hosting = exchange is not None
    t = kc.shape[0]
    nq = t // bq
    ng = H // hb
    npairs = nq * (nq + 1) // 2
    steps = [(j, g, i) for j in range(nq) for g in range(ng) for i in range(j, nq)]
    jtab = jnp.asarray(np.array([s[0] for s in steps], np.int32))
    gtab = jnp.asarray(np.array([s[1] for s in steps], np.int32))
    itab = jnp.asarray(np.array([s[2] for s in steps], np.int32))
    ptab = jnp.asarray(np.array([s[2] * (s[2] + 1) // 2 + s[0] for s in steps], np.int32))

    def kern(jt, gt, it, pt, q_ref, k_ref, do_ref, lset_ref, dlt_ref, *rest):
        if hosting:
            p_ref, dk_ref, ds_ref, slots_ref, dk_sc, dv_sc, send_sems, recv_sems = rest
            xc_start, xc_finish = _chip_exchange_schedule(p_ref, slots_ref, send_sems, recv_sems)
        else:
            dk_ref, ds_ref, dk_sc, dv_sc = rest
        st = pl.program_id(0)
        j, g, i = jt[st], gt[st], it[st]

        if hosting:
            @pl.when(st == 0)
            def _():
                xc_start()

        @pl.when((g == 0) & (i == j))
        def _():
            dk_sc[...] = jnp.zeros_like(dk_sc)
            dv_sc[...] = jnp.zeros_like(dv_sc)

        def update(masked):
            k = k_ref[...]
            v = k[:, 0:KVR]
            if masked:
                row = lax.broadcasted_iota(jnp.int32, (bq, bq), 0)
                col = lax.broadcasted_iota(jnp.int32, (bq, bq), 1)
                keep = row <= col

            def first_matmuls(hh):
                dob = do_ref[:, hh * KVR:(hh + 1) * KVR].astype(BF16)
                return _dot(k, q_ref[:, hh * KD:(hh + 1) * KD], NT), _dot(v, dob, NT), dob

            nxt = first_matmuls(0)
            for hh in range(hb):
                s, dp, dob = nxt
                if hh + 1 < hb:
                    nxt = first_matmuls(hh + 1)
                if masked:
                    s = jnp.where(keep, s, NEG)
                p = jnp.exp2(s - lset_ref[hh])
                dv_sc[...] += _dot(p.astype(BF16), dob, NN)
                dsb = (p * (dp - dlt_ref[hh])).astype(BF16)
                ds_ref[0, 0, hh] = dsb
                dk_sc[...] += _dot(dsb, q_ref[:, hh * KD:(hh + 1) * KD], NN)

        @pl.when(i > j)
        def _():
            update(False)

        @pl.when(i == j)
        def _():
            update(True)

        @pl.when((g == ng - 1) & (i == nq - 1))
        def _():
            dk_ref[:, 0:KVR] = dk_sc[:, 0:KVR] * LN2 + dv_sc[...]
            dk_ref[:, KVR:KD] = dk_sc[:, KVR:KD] * LN2

        if hosting:
            @pl.when(st == len(steps) - 1)
            def _():
                xc_finish()

    in_specs = [pl.BlockSpec((bq, hb * KD), lambda s, jt, gt, it, pt: (it[s], gt[s])),
                pl.BlockSpec((bq, KD), lambda s, jt, gt, it, pt: (jt[s], 0)),
                pl.BlockSpec((bq, hb * KVR), lambda s, jt, gt, it, pt: (it[s], gt[s])),
                pl.BlockSpec((hb, 1, bq), lambda s, jt, gt, it, pt: (gt[s], 0, it[s])),
                pl.BlockSpec((hb, 1, bq), lambda s, jt, gt, it, pt: (gt[s], 0, it[s]))]
    out_specs = [pl.BlockSpec((bq, KD), lambda s, jt, gt, it, pt: (jt[s], 0)),
                 pl.BlockSpec((1, 1, hb, bq, bq), lambda s, jt, gt, it, pt: (gt[s], pt[s], 0, 0, 0))]
    out_shape = [jax.ShapeDtypeStruct((t, KD), F32), jax.ShapeDtypeStruct((ng, npairs, hb, bq, bq), BF16)]
    scratch = [pltpu.VMEM((bq, KD), F32), pltpu.VMEM((bq, KVR), F32)]
    args = [jtab, gtab, itab, ptab, qcat, kc, do_lat, lse_t, delta_t]
    if hosting:
        in_specs.append(ANY)
        out_specs.append(ANY)
        out_shape.append(jax.ShapeDtypeStruct(exchange.shape, exchange.dtype))
        scratch += XCHG_SEMS
        args.append(exchange)
    gs = pltpu.PrefetchScalarGridSpec(num_scalar_prefetch=4, grid=(len(steps),), in_specs=in_specs,
                                      out_specs=out_specs, scratch_shapes=scratch)
    return pl.pallas_call(kern, grid_spec=gs, out_shape=out_shape, name="mla_flash_dkv",
                          compiler_params=_cp())(*args)


def _flash_dq(ds_all, kc_t, bq, hb, exchange=None):
    nq = kc_t.shape[0]
    t = nq * bq
    ng = H // hb
    pairs = _causal_pairs(nq)
    itab = jnp.asarray(np.array([p[0] for p in pairs], np.int32))
    jtab = jnp.asarray(np.array([p[1] for p in pairs], np.int32))
    hosting = exchange is not None

    hh2 = hb // 2

    def kern(it, jt, dsa_ref, dsb_ref, kt_ref, *rest):
        if hosting:
            p_ref, dq_ref, slots_ref, acc_sc, send_sems, recv_sems = rest
            xc_start, xc_finish = _chip_exchange_schedule(p_ref, slots_ref, send_sems, recv_sems)
        else:
            dq_ref, acc_sc = rest
        grp = pl.program_id(0)
        st = pl.program_id(1)
        i, j = it[st], jt[st]
        kt = kt_ref[...]

        def ds(hh):
            return dsa_ref[0, 0, hh] if hh < hh2 else dsb_ref[0, 0, hh - hh2]

        if hosting:
            @pl.when((grp == 0) & (st == 0))
            def _():
                xc_start()

        @pl.when(j == 0)
        def _():
            for hh in range(hb):
                acc_sc[hh] = _dot(kt, ds(hh), NN)

        @pl.when((j > 0) & (j < i))
        def _():
            for hh in range(hb):
                acc_sc[hh] += _dot(kt, ds(hh), NN)

        @pl.when(j == i)
        def _():
            for hh in range(hb):
                tot = _dot(kt, ds(hh), NN)
                tot = jnp.where(i > 0, tot + acc_sc[hh], tot)
                dq_ref[:, hh * KD:(hh + 1) * KD] = tot.T * MLA_SCALE

        if hosting:
            @pl.when((grp == ng - 1) & (st == len(pairs) - 1))
            def _():
                xc_finish()

    in_specs = [pl.BlockSpec((1, 1, hh2, bq, bq), lambda g, s, it, jt: (g, s, 0, 0, 0)),
                pl.BlockSpec((1, 1, hh2, bq, bq), lambda g, s, it, jt: (g, s, 1, 0, 0)),
                pl.BlockSpec((None, KD, bq), lambda g, s, it, jt: (jt[s], 0, 0))]
    out_specs = [pl.BlockSpec((bq, hb * KD), lambda g, s, it, jt: (it[s], g))]
    out_shape = [jax.ShapeDtypeStruct((t, H * KD), F32)]
    scratch = [pltpu.VMEM((hb, KD, bq), F32)]
    args = [itab, jtab, ds_all, ds_all, kc_t]
    if hosting:
        in_specs.append(ANY)
        out_specs.append(ANY)
        out_shape.append(jax.ShapeDtypeStruct(exchange.shape, exchange.dtype))
        scratch += XCHG_SEMS
        args.append(exchange)
    gs = pltpu.PrefetchScalarGridSpec(num_scalar_prefetch=2, grid=(ng, len(pairs)), in_specs=in_specs,
                                      out_specs=out_specs, scratch_shapes=scratch)
    outs = pl.pallas_call(kern, grid_spec=gs, out_shape=out_shape, name="mla_flash_dq",
                          compiler_params=_cp())(*args)
    return outs if hosting else outs[0]


def _bucket_table():
    d = np.arange(WIN)
    max_exact = NBKT // 2
    nf = np.maximum(d, 1).astype(np.float32)
    large = max_exact + (np.log(nf / np.float32(max_exact)) / np.float32(math.log(WIN / max_exact))
                         * np.float32(NBKT - max_exact)).astype(np.int32)
    large = np.minimum(large, NBKT - 1)
    bucket = np.where(d < max_exact, d, large).astype(np.int32)
    jj = np.arange(2 * WIN)[:, None]
    ii = np.arange(WIN)[None, :]
    dist = ii + WIN - jj
    valid = (dist >= 0) & (dist < WIN)
    return np.where(valid, bucket[np.clip(dist, 0, WIN - 1)], -1).astype(np.int32)


def _bias_build(rel_bias, bkt):
    def kern(bk_ref, rb_ref, o_ref):
        bk = bk_ref[...]
        for hd in range(QH):
            acc = jnp.full((2 * WIN, WIN), NEG, F32)
            for b in range(NBKT):
                acc = jnp.where(bk == b, rb_ref[b, hd], acc)
            o_ref[hd] = acc

    return pl.pallas_call(
        kern, in_specs=[pl.BlockSpec(memory_space=pltpu.VMEM), pl.BlockSpec(memory_space=pltpu.SMEM)],
        out_specs=pl.BlockSpec(memory_space=pltpu.VMEM),
        out_shape=jax.ShapeDtypeStruct((QH, 2 * WIN, WIN), F32), name="swa_bias_build")(bkt, rel_bias)


def _bias_bwd(dbias, bkt):
    def kern(db_ref, bk_ref, o_ref):
        bk = bk_ref[...]
        for hd in range(QH):
            g = db_ref[hd]
            for b in range(NBKT):
                r = b * QH + hd
                o_ref[r:r + 1, :] = jnp.sum(jnp.where(bk == b, g, 0.0), axis=0, keepdims=True)

    return pl.pallas_call(
        kern, in_specs=[pl.BlockSpec(memory_space=pltpu.VMEM), pl.BlockSpec(memory_space=pltpu.VMEM)],
        out_specs=pl.BlockSpec(memory_space=pltpu.VMEM),
        out_shape=jax.ShapeDtypeStruct((NBKT * QH, WIN), F32), name="swa_bias_bwd")(dbias, bkt)


def _swa_finish_scores(raw, bias, first):
    s = raw * SWA_SCALE + bias
    if first is not None:
        row = lax.broadcasted_iota(jnp.int32, s.shape, 0)
        s = jnp.where(jnp.logical_or(jnp.logical_not(first), row >= WIN), s, NEG)
    return s


def _swa_fwd(qkv_t, bias, sinks, qb):
    t = qkv_t.shape[1]
    w = qb * WIN
    nst = t // w

    def kern(q_ref, kc_ref, kp_ref, vc_ref, vp_ref, b_ref, sk_ref, o_ref, lse_ref):
        n = pl.program_id(0)
        kfull = jnp.concatenate([kp_ref[...], kc_ref[...]], axis=1)
        vfull = jnp.concatenate([vp_ref[...], vc_ref[...]], axis=1)
        head_row = lax.broadcasted_iota(jnp.int32, (QH, WIN), 0)
        groups = [(b, kh) for b in range(qb) for kh in range(KVH)]

        def raw_scores(b, kh):
            k_band = kfull[kh * HD:(kh + 1) * HD, b * WIN:(b + 2) * WIN]
            return [_dot(k_band, q_ref[(kh * G + g) * HD:(kh * G + g + 1) * HD, b * WIN:(b + 1) * WIN], TN)
                    for g in range(G)]

        o_rows = [[] for _ in range(qb)]
        lse_tiles = [jnp.zeros((QH, WIN), F32) for _ in range(qb)]
        nxt_scores = raw_scores(*groups[0])
        for gi, (b, kh) in enumerate(groups):
            scores = nxt_scores
            if gi + 1 < len(groups):
                nxt_scores = raw_scores(*groups[gi + 1])
            v_band = vfull[kh * HD:(kh + 1) * HD, b * WIN:(b + 2) * WIN]
            for g in range(G):
                hd = kh * G + g
                s = _swa_finish_scores(scores[g], b_ref[hd], (n == 0) if b == 0 else None)
                sink = sk_ref[hd]
                m = jnp.maximum(jnp.max(s, axis=0, keepdims=True), sink)
                p = jnp.exp(s - m)
                den = jnp.sum(p, axis=0, keepdims=True) + jnp.exp(sink - m)
                p = p / den
                o_rows[b].append(_dot(v_band, p.astype(BF16), NN))
                lse_tiles[b] = jnp.where(head_row == hd, m + jnp.log(den), lse_tiles[b])
        o_ref[...] = jnp.concatenate([jnp.concatenate(rows, axis=0) for rows in o_rows], axis=1)
        lse_ref[...] = jnp.concatenate(lse_tiles, axis=1)

    prev = lambda r: (lambda n: (r, jnp.maximum(n * qb - 1, 0)))
    return pl.pallas_call(
        kern, grid=(nst,),
        in_specs=[pl.BlockSpec((QH * HD, w), lambda n: (0, n)),
                  pl.BlockSpec((KVH * HD, w), lambda n: (4, n)), pl.BlockSpec((KVH * HD, WIN), prev(4)),
                  pl.BlockSpec((KVH * HD, w), lambda n: (5, n)), pl.BlockSpec((KVH * HD, WIN), prev(5)),
                  pl.BlockSpec((QH, 2 * WIN, WIN), lambda n: (0, 0, 0)),
                  pl.BlockSpec(memory_space=pltpu.SMEM)],
        out_specs=[pl.BlockSpec((QH * HD, w), lambda n: (0, n)), pl.BlockSpec((QH, w), lambda n: (0, n))],
        out_shape=[jax.ShapeDtypeStruct((QH * HD, t), F32), jax.ShapeDtypeStruct((QH, t), F32)],
        name="swa_fwd", compiler_params=_cp())(qkv_t, qkv_t, qkv_t, qkv_t, qkv_t, bias, sinks)


def _swa_bwd(qkv_t, do_t, o_t, lse, bias, sinks, qb):
    t = qkv_t.shape[1]
    w = qb * WIN
    nst = t // w
    nblk = t // WIN

    def kern(q_ref, kc_ref, kp_ref, vc_ref, vp_ref, do_ref, o_ref, lse_ref, qn_ref, don_ref, on_ref, lsen_ref,
             b_ref, sk_ref, dqkv_ref, db_ref, dsk_ref):
        n = pl.program_id(0)

        @pl.when(n == 0)
        def _():
            db_ref[...] = jnp.zeros_like(db_ref)
            dsk_ref[...] = jnp.zeros_like(dsk_ref)

        kfull = jnp.concatenate([kp_ref[...], kc_ref[...]], axis=1)
        vfull = jnp.concatenate([vp_ref[...], vc_ref[...]], axis=1)
        head_row = lax.broadcasted_iota(jnp.int32, (QH, WIN), 0)
        db_acc = [None] * QH
        dsk_tile = jnp.zeros((QH, WIN), F32)
        prev_part = [[[None] * qb for _ in range(KVH)] for _ in range(2)]
        cur_part = [[[None] * qb for _ in range(KVH)] for _ in range(2)]
        groups = [(b, kh) for b in range(qb) for kh in range(KVH)]

        def first_matmuls(b, kh):
            k_band = kfull[kh * HD:(kh + 1) * HD, b * WIN:(b + 2) * WIN]
            v_band = vfull[kh * HD:(kh + 1) * HD, b * WIN:(b + 2) * WIN]
            out = []
            for g in range(G):
                rs = slice((kh * G + g) * HD, (kh * G + g + 1) * HD)
                dob = do_ref[rs, b * WIN:(b + 1) * WIN].astype(BF16)
                out.append((_dot(k_band, q_ref[rs, b * WIN:(b + 1) * WIN], TN), _dot(v_band, dob, TN), dob))
            return out

        dq_rows = [[] for _ in range(qb)]
        nxt_first = first_matmuls(*groups[0])
        for gi, (b, kh) in enumerate(groups):
            first = nxt_first
            if gi + 1 < len(groups):
                nxt_first = first_matmuls(*groups[gi + 1])
            cs = slice(b * WIN, (b + 1) * WIN)
            k_band = kfull[kh * HD:(kh + 1) * HD, b * WIN:(b + 2) * WIN]
            dk_b = dv_b = None
            for g in range(G):
                hd = kh * G + g
                rs = slice(hd * HD, (hd + 1) * HD)
                raw, dp, dob = first[g]
                lse_h = lse_ref[hd:hd + 1, cs]
                s = _swa_finish_scores(raw, b_ref[hd], (n == 0) if b == 0 else None)
                p = jnp.exp(s - lse_h)
                dl = jnp.sum(do_ref[rs, cs] * o_ref[rs, cs], axis=0, keepdims=True)
                ds = p * (dp - dl)
                db_acc[hd] = ds if db_acc[hd] is None else db_acc[hd] + ds
                dsk_tile = jnp.where(head_row == hd, dsk_tile - jnp.exp(sk_ref[hd] - lse_h) * dl, dsk_tile)
                dss = (ds * SWA_SCALE).astype(BF16)
                dq_rows[b].append(_dot(k_band, dss, NN).astype(BF16))
                dk_h = _dot(q_ref[rs, cs], dss, NT)
                dv_h = _dot(dob, p.astype(BF16), NT)
                dk_b = dk_h if dk_b is None else dk_b + dk_h
                dv_b = dv_h if dv_b is None else dv_b + dv_h
            for which, val in ((0, dk_b), (1, dv_b)):
                prev_part[which][kh][b] = val[:, 0:WIN]
                cur_part[which][kh][b] = val[:, WIN:2 * WIN]
        dq_cols = [jnp.concatenate(rows, axis=0) for rows in dq_rows]

        live = n < nst - 1
        ls = slice((qb - 1) * WIN, qb * WIN)
        halo = [[None] * KVH for _ in range(2)]
        for kh in range(KVH):
            k_last = kc_ref[kh * HD:(kh + 1) * HD, ls]
            v_last = vc_ref[kh * HD:(kh + 1) * HD, ls]
            dk_b = dv_b = None
            for g in range(G):
                hd = kh * G + g
                rs = slice(hd * HD, (hd + 1) * HD)
                q_t = qn_ref[rs, :]
                do = don_ref[rs, :]
                s = _dot(k_last, q_t, TN) * SWA_SCALE + b_ref[hd, 0:WIN, :]
                p = jnp.exp(s - lsen_ref[hd:hd + 1, :])
                dob = do.astype(BF16)
                dp = _dot(v_last, dob, TN)
                dl = jnp.sum(do * on_ref[rs, :], axis=0, keepdims=True)
                dss = (p * (dp - dl) * SWA_SCALE).astype(BF16)
                dk_h = _dot(q_t, dss, NT)
                dv_h = _dot(dob, p.astype(BF16), NT)
                dk_b = dk_h if dk_b is None else dk_b + dk_h
                dv_b = dv_h if dv_b is None else dv_b + dv_h
            halo[0][kh] = jnp.where(live, dk_b, 0.0)
            halo[1][kh] = jnp.where(live, dv_b, 0.0)

        kv_rows = []
        for which in range(2):
            for kh in range(KVH):
                blocks = [cur_part[which][kh][p] + (prev_part[which][kh][p + 1] if p + 1 < qb else halo[which][kh])
                          for p in range(qb)]
                kv_rows.append(jnp.concatenate(blocks, axis=1))
        dqkv_ref[...] = jnp.concatenate(
            [jnp.concatenate(dq_cols, axis=1), jnp.concatenate(kv_rows, axis=0).astype(BF16)], axis=0)
        db_ref[...] += jnp.stack(db_acc)
        dsk_ref[...] += dsk_tile

    prev = lambda r: (lambda n: (r, jnp.maximum(n * qb - 1, 0)))
    nxt = lambda n: (0, jnp.minimum((n + 1) * qb, nblk - 1))
    big = lambda: pl.BlockSpec((QH * HD, w), lambda n: (0, n))
    return pl.pallas_call(
        kern, grid=(nst,),
        in_specs=[big(),
                  pl.BlockSpec((KVH * HD, w), lambda n: (4, n)), pl.BlockSpec((KVH * HD, WIN), prev(4)),
                  pl.BlockSpec((KVH * HD, w), lambda n: (5, n)), pl.BlockSpec((KVH * HD, WIN), prev(5)),
                  big(), big(), pl.BlockSpec((QH, w), lambda n: (0, n)),
                  pl.BlockSpec((QH * HD, WIN), nxt), pl.BlockSpec((QH * HD, WIN), nxt),
                  pl.BlockSpec((QH * HD, WIN), nxt), pl.BlockSpec((QH, WIN), nxt),
                  pl.BlockSpec((QH, 2 * WIN, WIN), lambda n: (0, 0, 0)),
                  pl.BlockSpec(memory_space=pltpu.SMEM)],
        out_specs=[pl.BlockSpec(((QH + 2 * KVH) * HD, w), lambda n: (0, n)),
                   pl.BlockSpec((QH, 2 * WIN, WIN), lambda n: (0, 0, 0)),
                   pl.BlockSpec((QH, WIN), lambda n: (0, 0))],
        out_shape=[jax.ShapeDtypeStruct(((QH + 2 * KVH) * HD, t), BF16),
                   jax.ShapeDtypeStruct((QH, 2 * WIN, WIN), F32), jax.ShapeDtypeStruct((QH, WIN), F32)],
        name="swa_bwd", compiler_params=_cp())(
            qkv_t, qkv_t, qkv_t, qkv_t, qkv_t, do_t, o_t, lse, qkv_t, do_t, o_t, lse, bias, sinks)


def _adamw_math(w, g, m, v):
    nm = B1 * m + (1.0 - B1) * g
    nv = B2 * v + (1.0 - B2) * (g * g)
    mhat = nm * (1.0 / (1.0 - B1 ** STEP))
    vhat = nv * (1.0 / (1.0 - B2 ** STEP))
    return -LR * (mhat / (jnp.sqrt(vhat) + ADAM_EPS) + WD * w), nm, nv


def _adamw_layers(w, m, v, g0buf, g1buf, off, name, tm=512):
    rows = w.shape[1]
    nb, ob = rows // tm, off // tm

    def kern(w_ref, m_ref, v_ref, g0_ref, g1_ref, gr_ref, d_ref, nm_ref, nv_ref):
        g_ = jnp.where(pl.program_id(0) == 0, g0_ref[...], g1_ref[...])
        gr_ref[...] = g_
        d_ref[...], nm_ref[...], nv_ref[...] = _adamw_math(w_ref[...], g_, m_ref[...], v_ref[...])

    lay = pl.BlockSpec((None, tm, D), lambda l, i: (l, i, 0))
    gsp = pl.BlockSpec((tm, D), lambda l, i: (ob + i, 0))
    return pl.pallas_call(
        kern, grid=(2, nb), in_specs=[lay, lay, lay, gsp, gsp], out_specs=[lay] * 4,
        out_shape=[jax.ShapeDtypeStruct(w.shape, F32)] * 4, name=name, compiler_params=_cp())(w, m, v, g0buf, g1buf)


def _adamw(w, g, m, v, name, tm=544):
    r = w.shape[0]
    tm = r if r % tm else tm

    def kern(w_ref, g_ref, m_ref, v_ref, d_ref, nm_ref, nv_ref):
        d_ref[...], nm_ref[...], nv_ref[...] = _adamw_math(w_ref[...], g_ref[...], m_ref[...], v_ref[...])

    row = pl.BlockSpec((tm, D), lambda i: (i, 0))
    sds = jax.ShapeDtypeStruct((r, D), F32)
    return pl.pallas_call(kern, grid=(r // tm,), in_specs=[row] * 4, out_specs=[row] * 3, out_shape=[sds] * 3,
                          name=name, compiler_params=_cp())(w, g, m, v)


def _mesh_pos():
    return lax.axis_index("x"), lax.axis_index("y"), lax.axis_index("c")


ANY = pl.BlockSpec(memory_space=pl.ANY)


AG_SEMS = [pltpu.SemaphoreType.DMA((6,)), pltpu.SemaphoreType.DMA((6,))]
XCHG_SEMS = [pltpu.SemaphoreType.DMA((3,)), pltpu.SemaphoreType.DMA((3,))]


def _allgather_schedule(w_ref, out_ref, send_sems, recv_sems):
    half = w_ref.shape[0] // 2
    x, y, c = _mesh_pos()
    me, sibling = (x, y, c), (x, y, 1 - c)
    chips = [(1 - x, y), (x, 1 - y), (1 - x, 1 - y)]

    def rows(px, py, pc):
        return out_ref.at[2 * px + py, pl.ds(pc * half, half), :]

    def copy(k, block, to, src=None):
        return pltpu.make_async_remote_copy(
            src_ref=rows(*block) if src is None else src, dst_ref=rows(*block),
            send_sem=send_sems.at[k], recv_sem=recv_sems.at[k], device_id=to, device_id_type=MESH)

    def first():
        return [copy(j, me, (*chip, c), src=w_ref.at[pl.ds(c * half, half), :]) for j, chip in enumerate(chips)]

    def passed():
        return [copy(3 + j, (*chip, c), sibling) for j, chip in enumerate(chips)]

    def start():
        for cp in first():
            cp.start()

    def forward():
        for j, chip in enumerate(chips):
            copy(j, (*chip, c), me).wait_recv()
            passed()[j].start()

    def finish():
        for j, chip in enumerate(chips):
            copy(3 + j, (*chip, 1 - c), me).wait_recv()
        for cp in first() + passed():
            cp.wait_send()

    return start, forward, finish


def _allgather_weights(wpack):
    def body(w_ref, out_ref, send_sems, recv_sems):
        start, forward, finish = _allgather_schedule(w_ref, out_ref, send_sems, recv_sems)
        start()
        forward()
        finish()

    return pl.pallas_call(
        body, out_shape=jax.ShapeDtypeStruct((4,) + wpack.shape, wpack.dtype), in_specs=[ANY], out_specs=ANY,
        scratch_shapes=AG_SEMS, name="allgather_weights")(wpack)


def _row_tile(rows):
    t = min(rows, 512)
    while rows % t or t % 16:
        t -= 16
    return t


def _exchange_core_halves(g, tag):
    half = g.shape[1] // 2

    def body(g_ref, out_ref, send_sem, recv_sem):
        x, y, c = _mesh_pos()
        cp = pltpu.make_async_remote_copy(
            src_ref=g_ref.at[:, pl.ds((1 - c) * half, half), :], dst_ref=out_ref,
            send_sem=send_sem, recv_sem=recv_sem, device_id=(x, y, 1 - c), device_id_type=MESH)
        cp.start()
        cp.wait()

    return pl.pallas_call(
        body, out_shape=jax.ShapeDtypeStruct((4, half, D), g.dtype), in_specs=[ANY], out_specs=ANY,
        scratch_shapes=[pltpu.SemaphoreType.DMA, pltpu.SemaphoreType.DMA], name=f"rs_exchange_cores_{tag}")(g)


def _add_core_halves(g, other, cidx, tag):
    half = other.shape[1]
    tm = _row_tile(half)
    nb = half // tm

    def kern(c_ref, a_ref, b_ref, o_ref):
        o_ref[...] = (a_ref[...] + b_ref[...]).astype(BF16)

    gs = pltpu.PrefetchScalarGridSpec(
        num_scalar_prefetch=1, grid=(4, nb),
        in_specs=[pl.BlockSpec((1, tm, D), lambda s, i, c: (s, c[0] * nb + i, 0)),
                  pl.BlockSpec((1, tm, D), lambda s, i, c: (s, i, 0))],
        out_specs=pl.BlockSpec((1, tm, D), lambda s, i, c: (s, i, 0)))
    return pl.pallas_call(kern, grid_spec=gs, out_shape=jax.ShapeDtypeStruct(other.shape, BF16),
                          name=f"rs_add_cores_{tag}", compiler_params=_cp())(cidx, g, other)


def _chip_exchange_schedule(p_ref, out_ref, send_sems, recv_sems):
    x, y, c = _mesh_pos()
    me = 2 * x + y
    chips = [(1 - x, y), (x, 1 - y), (1 - x, 1 - y)]

    def sends():
        return [pltpu.make_async_remote_copy(
            src_ref=p_ref.at[2 * px + py], dst_ref=out_ref.at[me], send_sem=send_sems.at[j],
            recv_sem=recv_sems.at[j], device_id=(px, py, c), device_id_type=MESH) for j, (px, py) in enumerate(chips)]

    def start():
        for cp in sends():
            cp.start()

    def finish():
        for j, (px, py) in enumerate(chips):
            pltpu.make_async_remote_copy(
                src_ref=p_ref.at[me], dst_ref=out_ref.at[2 * px + py], send_sem=send_sems.at[j],
                recv_sem=recv_sems.at[j], device_id=(px, py, c), device_id_type=MESH).wait_recv()
        for cp in sends():
            cp.wait_send()

    return start, finish


def _exchange_chip_shards(p, tag):
    def body(p_ref, out_ref, send_sems, recv_sems):
        start, finish = _chip_exchange_schedule(p_ref, out_ref, send_sems, recv_sems)
        start()
        finish()

    return pl.pallas_call(
        body, out_shape=jax.ShapeDtypeStruct(p.shape, p.dtype), in_specs=[ANY], out_specs=ANY,
        scratch_shapes=XCHG_SEMS, name=f"rs_exchange_chips_{tag}")(p)


def _sum_slots(slots, p, pos, tag):
    half = slots.shape[1]
    tm = _row_tile(half)
    nb = half // tm

    def kern(pos_ref, p_ref, s1_ref, s2_ref, s3_ref, o_ref):
        o_ref[...] = ((p_ref[0].astype(F32) + s1_ref[0].astype(F32)) + s2_ref[0].astype(F32)) + s3_ref[0].astype(F32)

    def slot(k):
        return pl.BlockSpec((1, tm, D), lambda i, pos: ((pos[0] + k) % 4, i, 0))

    gs = pltpu.PrefetchScalarGridSpec(
        num_scalar_prefetch=1, grid=(nb,), in_specs=[slot(0), slot(1), slot(2), slot(3)],
        out_specs=pl.BlockSpec((tm, D), lambda i, pos: (pos[1] * nb + i, 0)))
    return pl.pallas_call(kern, grid_spec=gs, out_shape=jax.ShapeDtypeStruct((2 * half, D), F32),
                          name=f"rs_sum_chips_{tag}", compiler_params=_cp())(pos, p, slots, slots, slots)


def _join_core_halves(r, tag):
    half = r.shape[0] // 2

    def body(r_ref, out_ref, send_sem, recv_sem):
        x, y, c = _mesh_pos()
        mine = out_ref.at[pl.ds(c * half, half), :]
        cp = pltpu.make_async_remote_copy(
            src_ref=mine, dst_ref=mine, send_sem=send_sem, recv_sem=recv_sem,
            device_id=(x, y, 1 - c), device_id_type=MESH)
        cp.start()
        theirs = out_ref.at[pl.ds((1 - c) * half, half), :]
        pltpu.make_async_remote_copy(
            src_ref=theirs, dst_ref=theirs, send_sem=send_sem, recv_sem=recv_sem,
            device_id=(x, y, 1 - c), device_id_type=MESH).wait_recv()
        cp.wait_send()

    return pl.pallas_call(
        body, out_shape=jax.ShapeDtypeStruct(r.shape, r.dtype), in_specs=[ANY], out_specs=ANY,
        input_output_aliases={0: 0},
        scratch_shapes=[pltpu.SemaphoreType.DMA, pltpu.SemaphoreType.DMA],
        name=f"rs_join_cores_{tag}")(r)


def _allreduce_small(v, name):
    def body(v_ref, out_ref, gat, send_sems, recv_sems):
        x, y, c = _mesh_pos()
        me = 4 * x + 2 * y + c
        gat[me] = v_ref[...]
        sends = []
        for k in range(1, 8):
            peer = (x ^ (k >> 2), y ^ ((k >> 1) & 1), c ^ (k & 1))
            cp = pltpu.make_async_remote_copy(
                src_ref=v_ref, dst_ref=gat.at[me], send_sem=send_sems.at[k - 1], recv_sem=recv_sems.at[k - 1],
                device_id=peer, device_id_type=MESH)
            cp.start()
            sends.append(cp)
        for k in range(1, 8):
            px, py, pc = x ^ (k >> 2), y ^ ((k >> 1) & 1), c ^ (k & 1)
            pltpu.make_async_remote_copy(
                src_ref=v_ref, dst_ref=gat.at[4 * px + 2 * py + pc], send_sem=send_sems.at[k - 1],
                recv_sem=recv_sems.at[k - 1], device_id=(px, py, pc), device_id_type=MESH).wait_recv()
        for cp in sends:
            cp.wait_send()
        acc = gat[0]
        for d in range(1, 8):
            acc = acc + gat[d]
        out_ref[...] = acc

    return pl.pallas_call(
        body, out_shape=jax.ShapeDtypeStruct(v.shape, F32),
        in_specs=[pl.BlockSpec(memory_space=pltpu.VMEM)], out_specs=pl.BlockSpec(memory_space=pltpu.VMEM),
        scratch_shapes=[pltpu.VMEM((8,) + v.shape, F32), pltpu.SemaphoreType.DMA((7,)), pltpu.SemaphoreType.DMA((7,))],
        name=name)(v)


def _mlp_fwd(xb, w_up, w_down, tag):
    a = _mm(xb, w_up[0], "nn", f"mlp_up_{tag}", out_dtype=BF16, relu2=True, b_view=("cols", w_up[1]))
    return a, _mm(a, w_down[0], "nn", f"mlp_down_{tag}", b_view=("rows", w_down[1]))


def _mlp_bwd(dz, dzb, xb, a, w_up, w_down, tag):
    du = _mm(dzb, w_down[0], "nt", f"mlp_down_dx_{tag}", out_dtype=BF16, gate_a=a, b_view=("rows", w_down[1]))
    gsh = _mm(xb, du, "tn", f"mlp_up_dw_{tag}", out_view=("cols", 2 * ROWS["mlp_w_up"], 0, None))
    gsh = _mm(a, dzb, "tn", f"mlp_down_dw_{tag}", out_view=("rows", 2 * ROWS["mlp_w_up"], ROWS["mlp_w_up"], gsh))
    dx = _mm(du, w_up[0], "nt", f"mlp_up_dx_{tag}", addend=dz, add_scale=ALPHA, b_view=("cols", w_up[1]))
    return dx, gsh


def _fwd_bwd(x, target, w, dist=None, bq=512, qb=4, hb=4):
    t = x.shape[0]
    bq = min(bq, t)
    qb = min(qb, t // WIN)
    cos, sin = _rope_tables(t)
    bkt = jnp.asarray(_bucket_table())
    w_in = jnp.pad(w[("mla_w_in", None)], ((0, 0), (0, HW - (QR + KVR + ROPE))))
    wuq = w[("mla_w_uq", None)]
    wq2 = jnp.concatenate([wuq[:, :, :NOPE].reshape(QR, H * NOPE),
                           jnp.pad(wuq[:, :, NOPE:], ((0, 0), (0, 0), (0, RP - ROPE))).reshape(QR, H * RP)], axis=1)
    wuk_t = w[("mla_w_uk", None)].transpose(1, 2, 0)
    wuk_h = w[("mla_w_uk", None)].transpose(1, 0, 2)
    wuv_h = w[("mla_w_uv", None)].transpose(1, 0, 2)
    w_o = w[("mla_w_o", None)]
    sinks = w["swa_sinks"].reshape(QH)
    lnp = lambda n, l: w[n][l]
    reduced = {}

    hh = _mm(x, w_in, "nn", "mla_in")
    cq, kc = _mla_pre(hh, w["mla_g_q"], w["mla_g_kv"], cos, sin)
    q2 = _mm(cq, wq2, "nn", "mla_uq")
    qcat = _q_prep(q2, wuk_t, cos, sin)
    if dist is None:
        o_lat, lse0_t = _flash_fwd(qcat, kc, bq, hb)
    else:
        o_lat, lse0_t, wall = _flash_fwd(qcat, kc, bq, hb, gather=dist.late_pack)
        wall = lax.dynamic_update_slice(wall, dist.late_pack[None], (dist.shard, 0, 0))
        w = {**w, **_full_from_gathered(AG_LATE, wall, dist.shard_shapes)}
    wqkv = jnp.concatenate([w[("swa_w_q", None)], w[("kv_w_shared", None)]], axis=1)
    wqkv_t = wqkv.T
    wo_s = w[("swa_w_o", None)]
    o0 = _o_up(o_lat, wuv_h)
    y0 = _mm(o0, w_o, "nn", "mla_out")
    x1b, xh1, r1 = _add_ln(x, y0, lnp("ln_mix_g", 0), lnp("ln_mix_b", 0), "ln_mix_0")
    a0, f0 = _mlp_fwd(x1b, w[("mlp_w_up", 0)], w[("mlp_w_down", 0)], 0)
    x2b, xh2, r2 = _add_ln(xh1, f0, lnp("ln_mlp_g", 0), lnp("ln_mlp_b", 0), "ln_mlp_0",
                           res_affine=(lnp("ln_mix_g", 0), lnp("ln_mix_b", 0)))
    bias = _bias_build(w["rel_bias"], bkt)
    qkv_t = _mm(x2b, wqkv, "nn", "swa_qkv", out_dtype=BF16, out_t=True)
    os_t, lse1 = _swa_fwd(qkv_t, bias, sinks, qb)
    y1 = _mm(os_t, wo_s, "tn", "swa_out")
    x3b, xh3, r3 = _add_ln(xh2, y1, lnp("ln_mix_g", 1), lnp("ln_mix_b", 1), "ln_mix_1",
                           res_affine=(lnp("ln_mlp_g", 0), lnp("ln_mlp_b", 0)))
    a1, f1 = _mlp_fwd(x3b, w[("mlp_w_up", 1)], w[("mlp_w_down", 1)], 1)
    _, xh4, r4 = _add_ln(xh3, f1, lnp("ln_mlp_g", 1), lnp("ln_mlp_b", 1), "ln_mlp_1",
                         res_affine=(lnp("ln_mix_g", 1), lnp("ln_mix_b", 1)))

    g = {}
    dz4, dz4b, dg_mlp1, db_mlp1, lpart = _ln_bwd(target, xh4, r4, lnp("ln_mlp_g", 1), "ln_mlp_1_bwd",
                                                 loss_b=lnp("ln_mlp_b", 1))
    dx3, g["mlp1"] = _mlp_bwd(dz4, dz4b, x3b, a1, w[("mlp_w_up", 1)], w[("mlp_w_down", 1)], 1)
    part1 = _rs_chip_partials(g["mlp1"], dist, "mlp1") if dist is not None else None
    dz3, dz3b, dg_mix1, db_mix1 = _ln_bwd(dx3, xh3, r3, lnp("ln_mix_g", 1), "ln_mix_1_bwd")
    dos_t = _mm(dz3b, wo_s, "nt", "swa_out_dx", out_t=True)
    g[("swa_w_o", None)] = _mm(os_t, dz3b, "nn", "swa_out_dw")
    dqkv_t, dbias, dsk = _swa_bwd(qkv_t, dos_t, os_t, lse1, bias, sinks, qb)
    dwqkv = _mm(dqkv_t, x2b, "nn", "swa_qkv_dw").T
    g[("swa_w_q", None)], g[("kv_w_shared", None)] = dwqkv[:, :QH * HD], dwqkv[:, QH * HD:]
    dx2 = _mm(dqkv_t, wqkv_t, "tn", "swa_qkv_dx", addend=dz3, add_scale=ALPHA)
    g["rel_bias"] = jnp.sum(_bias_bwd(dbias, bkt), axis=-1).reshape(NBKT, QH)
    g["swa_sinks"] = jnp.sum(dsk, axis=-1).reshape(1, QH)
    dz2, dz2b, dg_mlp0, db_mlp0 = _ln_bwd(dx2, xh2, r2, lnp("ln_mlp_g", 0), "ln_mlp_0_bwd")
    dx1, g["mlp0"] = _mlp_bwd(dz2, dz2b, x1b, a0, w[("mlp_w_up", 0)], w[("mlp_w_down", 0)], 0)
    part0 = _rs_chip_partials(g["mlp0"], dist, "mlp0") if dist is not None else None
    dz1, dz1b, dg_mix0, db_mix0 = _ln_bwd(dx1, xh1, r1, lnp("ln_mix_g", 0), "ln_mix_0_bwd")
    do0 = _mm(dz1b, w_o, "nt", "mla_out_dx", out_dtype=BF16)
    g[("mla_w_o", None)] = _mm(o0, dz1b, "tn", "mla_out_dw")
    do_lat, dwuv, delta_t = _o_up_bwd(do0, o_lat, wuv_h)
    g[("mla_w_uv", None)] = dwuv.transpose(1, 0, 2)
    kc_t = kc.reshape(t // bq, bq, KD).transpose(0, 2, 1)
    if dist is None:
        dk, ds_all = _flash_dkv(qcat, kc, do_lat, lse0_t, delta_t, bq, hb)
        dq_cat = _flash_dq(ds_all, kc_t, bq, hb)
    else:
        dk, ds_all, slots1 = _flash_dkv(qcat, kc, do_lat, lse0_t, delta_t, bq, hb, exchange=part1)
        dq_cat, slots0 = _flash_dq(ds_all, kc_t, bq, hb, exchange=part0)
        reduced["mlp1"] = _rs_finish(part1, slots1, dist, "mlp1")
        reduced["mlp0"] = _rs_finish(part0, slots0, dist, "mlp0")
    dq2, dwuk = _q_prep_bwd(dq_cat, q2, wuk_h, cos, sin)
    g[("mla_w_uk", None)] = dwuk.transpose(2, 0, 1)
    dcq = _mm(dq2, wq2, "nt", "mla_uq_dx")
    dwq2 = _mm(cq, dq2, "tn", "mla_uq_dw")
    g[("mla_w_uq", None)] = jnp.concatenate([dwq2[:, :H * NOPE].reshape(QR, H, NOPE),
                                             dwq2[:, H * NOPE:].reshape(QR, H, RP)[:, :, :ROPE]], axis=2)
    dh, dgq, dgkv = _mla_pre_bwd(hh, dcq, dk, w["mla_g_q"], w["mla_g_kv"], cos, sin)
    g[("mla_w_in", None)] = _mm(x, dh, "tn", "mla_in_dw")[:, :QR + KVR + ROPE]
    grad_x = _mm(dh, w_in, "nt", "mla_in_dx", addend=dz1, add_scale=ALPHA)
    g["mla_g_q"], g["mla_g_kv"] = dgq, dgkv
    g["ln_mix_g"] = jnp.concatenate([dg_mix0, dg_mix1], axis=0)
    g["ln_mix_b"] = jnp.concatenate([db_mix0, db_mix1], axis=0)
    g["ln_mlp_g"] = jnp.concatenate([dg_mlp0, dg_mlp1], axis=0)
    g["ln_mlp_b"] = jnp.concatenate([db_mlp0, db_mlp1], axis=0)
    return lpart, grad_x, g, reduced


def _rows(a):
    return a.reshape(-1, D)


def _piece(a, layer):
    return _rows(a if layer is None else a[layer])


def _pack_group(group, parts):
    return jnp.concatenate([_piece(parts[n], l) for n, l in group], axis=0)


def _unpack_group(group, buf, like):
    out, off = {}, 0
    for n, l in group:
        shp = like[n].shape if l is None else like[n].shape[1:]
        out[(n, l)] = buf[off:off + ROWS[n]].reshape(shp)
        off += ROWS[n]
    return out


def _by_name(pieces):
    out = {n: a for (n, l), a in pieces.items() if l is None}
    for n in {n for (n, l) in pieces if l is not None}:
        out[n] = jnp.stack([pieces[(n, 0)], pieces[(n, 1)]])
    return out


def _full_from_gathered(group, wall, shard_shapes):
    out, off = {}, 0
    for n, l in group:
        shp = tuple(shard_shapes[n])
        if n in ("mlp_w_up", "mlp_w_down"):
            out[(n, l)] = (wall, off)
        elif n == "kv_w_shared":
            out[(n, l)] = wall[:, off:off + ROWS[n]].reshape((4 * shp[0],) + shp[1:])
        else:
            out[(n, l)] = wall[:, off:off + ROWS[n]].reshape((4 * shp[1],) + shp[2:])
        off += ROWS[n]
    return out


def _grad_shards(group, g):
    return jnp.concatenate([g[(n, l)].reshape(4, ROWS[n], D) for n, l in group], axis=1)


def _rs_chip_partials(gsh, dist, tag):
    return _add_core_halves(gsh, _exchange_core_halves(gsh, tag), dist.cidx, tag)


def _rs_finish(part, slots, dist, tag):
    return _join_core_halves(_sum_slots(slots, part, dist.pos, tag), tag)


SMALL = (("ln_mix_g", 0, 2), ("ln_mix_b", 2, 2), ("ln_mlp_g", 4, 2), ("ln_mlp_b", 6, 2),
         ("swa_sinks", 8, 1), ("mla_g_q", 9, 1), ("mla_g_kv", 10, 1), ("rel_bias", 11, 1))


def _pack_small(parts):
    rows = []
    for n, _, nr in SMALL:
        a = parts[n].reshape(nr, -1).astype(F32)
        rows.append(jnp.pad(a, ((0, 0), (0, D - a.shape[1]))))
    rows.append(jnp.zeros((SMALL_ROWS - 12, D), F32))
    return jnp.concatenate(rows, axis=0)


def _unpack_small(buf, like):
    out = {}
    for n, r0, nr in SMALL:
        size = like[n].size // nr
        out[n] = buf[r0:r0 + nr, :size].reshape(like[n].shape)
    return out


def kernel(x, mla_w_in, mla_g_q, mla_g_kv, mla_w_uq, mla_w_uk, mla_w_uv, mla_w_o, kv_w_shared, swa_w_q, swa_sinks, swa_w_o, rel_bias, mlp_w_up, mlp_w_down, ln_mix_g, ln_mix_b, ln_mlp_g, ln_mlp_b, loss_target, m_mla_w_in, m_mla_g_q, m_mla_g_kv, m_mla_w_uq, m_mla_w_uk, m_mla_w_uv, m_mla_w_o, m_kv_w_shared, m_swa_w_q, m_swa_sinks, m_swa_w_o, m_rel_bias, m_mlp_w_up, m_mlp_w_down, m_ln_mix_g, m_ln_mix_b, m_ln_mlp_g, m_ln_mlp_b, v_mla_w_in, v_mla_g_q, v_mla_g_kv, v_mla_w_uq, v_mla_w_uk, v_mla_w_uv, v_mla_w_o, v_kv_w_shared, v_swa_w_q, v_swa_sinks, v_swa_w_o, v_rel_bias, v_mlp_w_up, v_mlp_w_down, v_ln_mix_g, v_ln_mix_b, v_ln_mlp_g, v_ln_mlp_b):
    names = ["mla_w_in", "mla_g_q", "mla_g_kv", "mla_w_uq", "mla_w_uk", "mla_w_uv", "mla_w_o", "kv_w_shared",
             "swa_w_q", "swa_sinks", "swa_w_o", "rel_bias", "mlp_w_up", "mlp_w_down",
             "ln_mix_g", "ln_mix_b", "ln_mlp_g", "ln_mlp_b"]
    ws = dict(zip(names, [mla_w_in, mla_g_q, mla_g_kv, mla_w_uq, mla_w_uk, mla_w_uv, mla_w_o, kv_w_shared,
                          swa_w_q, swa_sinks, swa_w_o, rel_bias, mlp_w_up, mlp_w_down,
                          ln_mix_g, ln_mix_b, ln_mlp_g, ln_mlp_b]))
    ms = dict(zip(names, [m_mla_w_in, m_mla_g_q, m_mla_g_kv, m_mla_w_uq, m_mla_w_uk, m_mla_w_uv, m_mla_w_o,
                          m_kv_w_shared, m_swa_w_q, m_swa_sinks, m_swa_w_o, m_rel_bias, m_mlp_w_up, m_mlp_w_down,
                          m_ln_mix_g, m_ln_mix_b, m_ln_mlp_g, m_ln_mlp_b]))
    vs = dict(zip(names, [v_mla_w_in, v_mla_g_q, v_mla_g_kv, v_mla_w_uq, v_mla_w_uk, v_mla_w_uv, v_mla_w_o,
                          v_kv_w_shared, v_swa_w_q, v_swa_sinks, v_swa_w_o, v_rel_bias, v_mlp_w_up, v_mlp_w_down,
                          v_ln_mix_g, v_ln_mix_b, v_ln_mlp_g, v_ln_mlp_b]))
    xi, yi, ci = _mesh_pos()
    shard = 2 * xi + yi
    shard_shapes = {n: ws[n].shape for n in ROWS}
    wbf = {n: ws[n].astype(BF16) for n in ROWS}

    early = _pack_group(AG_EARLY, wbf)
    wall = lax.dynamic_update_slice(_allgather_weights(early), early[None], (shard, 0, 0))
    w = _full_from_gathered(AG_EARLY, wall, shard_shapes)
    dist = _Dist(shard=shard, cidx=jnp.reshape(ci, (1,)).astype(jnp.int32),
                 pos=jnp.stack([shard, ci]).astype(jnp.int32), late_pack=_pack_group(AG_LATE, wbf),
                 shard_shapes=shard_shapes)
    gq_slot = lax.dynamic_update_slice(jnp.zeros((1, QR), F32), mla_g_q, (0, shard * (QR // 4)))
    gkv_slot = lax.dynamic_update_slice(jnp.zeros((1, KVR), F32), mla_g_kv, (0, shard * (KVR // 4)))
    gains = jnp.concatenate([jnp.pad(gq_slot, ((0, 0), (0, D - QR))), jnp.pad(gkv_slot, ((0, 0), (0, D - KVR))),
                             jnp.zeros((SMALL_ROWS - 2, D), F32)], axis=0)
    gains = _allreduce_small(gains * 0.5, "allgather_gains")
    w["mla_g_q"], w["mla_g_kv"] = gains[0, :QR], gains[1, :KVR]
    for n in ("swa_sinks", "rel_bias", "ln_mix_g", "ln_mix_b", "ln_mlp_g", "ln_mlp_b"):
        w[n] = ws[n]

    lpart, grad_x, g, reduced = _fwd_bwd(x[0], loss_target[0], w, dist)
    loss = lax.psum(0.5 * jnp.sum(lpart) / D, ("x", "y", "c"))

    part = _rs_chip_partials(_grad_shards(RS_REST, g), dist, "rest")
    reduced["rest"] = _rs_finish(part, _exchange_chip_shards(part, "rest"), dist, "rest")

    small_like = {n: g[n] for n, _, _ in SMALL}
    gsm = _unpack_small(_allreduce_small(_pack_small(g), "allreduce_small_grads"), small_like)
    gsm["mla_g_q"] = lax.dynamic_slice(gsm["mla_g_q"], (0, shard * (QR // 4)), (1, QR // 4))
    gsm["mla_g_kv"] = lax.dynamic_slice(gsm["mla_g_kv"], (0, shard * (KVR // 4)), (1, KVR // 4))

    gbig, dbig, mbig, vbig = {}, {}, {}, {}
    for n in ("mlp_w_up", "mlp_w_down"):
        off = 0 if n == "mlp_w_up" else ROWS["mlp_w_up"]
        gbig[n], dbig[n], mbig[n], vbig[n] = _adamw_layers(
            ws[n], ms[n], vs[n], reduced["mlp0"], reduced["mlp1"], off, f"adamw_{n}")
    outs = _adamw(_pack_group(RS_REST, ws), reduced["rest"], _pack_group(RS_REST, ms), _pack_group(RS_REST, vs),
                  "adamw_rest", tm=_row_tile(reduced["rest"].shape[0]))
    for dst, buf in zip((gbig, dbig, mbig, vbig), (reduced["rest"], *outs)):
        dst.update(_by_name(_unpack_group(RS_REST, buf, ws)))
    dsm, msm, vsm = _adamw(_pack_small(ws), _pack_small(gsm), _pack_small(ms), _pack_small(vs), "adamw_small", tm=16)
    grads = {**gbig, **gsm}
    delta = {**dbig, **_unpack_small(dsm, ws)}
    new_m = {**mbig, **_unpack_small(msm, ws)}
    new_v = {**vbig, **_unpack_small(vsm, ws)}
    grads = {n: grads[n].reshape(ws[n].shape) for n in names}
    return (loss, grad_x[None], *[grads[n] for n in names], *[delta[n] for n in names],
            *[new_m[n] for n in names], *[new_v[n] for n in names])
```

```python
import collections
import math

import numpy as np
import jax
import jax.numpy as jnp
from jax import lax
from jax.experimental import pallas as pl
from jax.experimental.pallas import tpu as pltpu

F32 = jnp.float32
BF16 = jnp.bfloat16
MESH = pl.DeviceIdType.MESH

D = 1024
DFF = 4096
H = 8
NOPE = 128
ROPE = 64
QR = 384
KVR = 256
RP = 128
KD = KVR + RP
HW = 768
QH = 16
KVH = 4
HD = 64
G = QH // KVH
WIN = 128
NBKT = 32
ALPHA = 4.0 ** 0.25
LN_EPS = 1e-5
RMS_EPS = 1e-6
MLA_SCALE = (NOPE + ROPE) ** -0.5
LOG2E = 1.4426950408889634
LN2 = 0.6931471805599453
QSCALE = MLA_SCALE * LOG2E
SWA_SCALE = HD ** -0.5
NEG = -1e30
LR, B1, B2, ADAM_EPS, WD, STEP = 0.001, 0.9, 0.999, 1e-8, 0.01, 10

VMEM_LIMIT = 48 * 1024 * 1024

NN = (((1,), (0,)), ((), ()))
NT = (((1,), (1,)), ((), ()))
TN = (((0,), (0,)), ((), ()))

ROWS = {"mlp_w_up": 1024, "mlp_w_down": 1024, "mla_w_o": 256, "swa_w_q": 256, "swa_w_o": 256,
        "kv_w_shared": 128, "mla_w_in": 176, "mla_w_uq": 144, "mla_w_uk": 64, "mla_w_uv": 64}
AG_EARLY = (("mla_w_in", None), ("mla_w_uq", None), ("mla_w_uk", None), ("mla_w_uv", None), ("mla_w_o", None))
AG_LATE = (("mlp_w_up", 0), ("mlp_w_up", 1), ("mlp_w_down", 0), ("mlp_w_down", 1),
           ("swa_w_q", None), ("swa_w_o", None), ("kv_w_shared", None))
RS_MLP1 = (("mlp_w_up", 1), ("mlp_w_down", 1))
RS_MLP0 = (("mlp_w_up", 0), ("mlp_w_down", 0))
RS_REST = (("mla_w_o", None), ("swa_w_q", None), ("swa_w_o", None), ("kv_w_shared", None),
           ("mla_w_in", None), ("mla_w_uq", None), ("mla_w_uk", None), ("mla_w_uv", None))
SMALL_ROWS = 16
_Dist = collections.namedtuple("_Dist", "shard cidx pos late_pack shard_shapes")


def _cp(**kw):
    return pltpu.CompilerParams(vmem_limit_bytes=VMEM_LIMIT, **kw)


def _tile(n, pref):
    t = min(n, pref)
    while n % t:
        t -= 128
    return t


def _dot(a, b, dims):
    return lax.dot_general(a, b, dims, preferred_element_type=F32)


def _mm(a, b, mode, name, out_dtype=F32, out_t=False, addend=None, add_scale=1.0, relu2=False, gate_a=None,
        b_view=None, out_view=None, tm=1024, tn=1024, tk=1024):
    blk = 1024
    if b_view is not None:
        kind, b_off = b_view
        assert b.shape[0] == 4 and b.shape[2] == blk and b_off % blk == 0
        bshape = {("cols", "nn"): (blk, 4 * blk), ("cols", "nt"): (blk, 4 * blk),
                  ("rows", "nn"): (4 * blk, blk), ("rows", "nt"): (4 * blk, blk)}[(kind, mode)]
    else:
        bshape = b.shape
    if mode == "nn":
        (m, k), (k2, n) = a.shape, bshape
    elif mode == "nt":
        (m, k), (n, k2) = a.shape, bshape
    else:
        (k, m), (k2, n) = a.shape, bshape
    assert k == k2, (name, a.shape, b.shape)
    tm, tn, tk = _tile(m, tm), _tile(n, tn), _tile(k, tk)
    nk = k // tk
    dims = {"nn": NN, "nt": NT, "tn": TN}[mode]
    if mode == "tn":
        a_spec = pl.BlockSpec((tk, tm), lambda i, j, kk: (kk, i))
    else:
        a_spec = pl.BlockSpec((tm, tk), lambda i, j, kk: (i, kk))
    if b_view is not None:
        assert tn == blk and tk == blk
        ob = b_off // blk
        b_spec = {("cols", "nn"): pl.BlockSpec((None, tk, tn), lambda i, j, kk: (j, ob, 0)),
                  ("cols", "nt"): pl.BlockSpec((None, tn, tk), lambda i, j, kk: (kk, ob, 0)),
                  ("rows", "nn"): pl.BlockSpec((None, tk, tn), lambda i, j, kk: (kk, ob, 0)),
                  ("rows", "nt"): pl.BlockSpec((None, tn, tk), lambda i, j, kk: (j, ob, 0))}[(kind, mode)]
    elif mode == "nt":
        b_spec = pl.BlockSpec((tn, tk), lambda i, j, kk: (j, kk))
    else:
        b_spec = pl.BlockSpec((tk, tn), lambda i, j, kk: (kk, j))
    mn_spec = pl.BlockSpec((tm, tn), lambda i, j, kk: (i, j))
    ins, in_specs = [a, b], [a_spec, b_spec]
    if addend is not None:
        ins.append(addend)
        in_specs.append(mn_spec)
    if gate_a is not None:
        ins.append(gate_a)
        in_specs.append(mn_spec)
    aliases = {}
    if out_view is not None:
        okind, total_rows, o_off, buf = out_view
        assert not out_t and tm == blk and tn == blk and o_off % blk == 0
        oo = o_off // blk
        out_shape = [jax.ShapeDtypeStruct((4, total_rows, blk), out_dtype)]
        if okind == "cols":
            out_specs = [pl.BlockSpec((None, tm, tn), lambda i, j, kk: (j, oo, 0))]
        else:
            out_specs = [pl.BlockSpec((None, tm, tn), lambda i, j, kk: (i, oo, 0))]
        if buf is not None:
            aliases = {len(ins): 0}
            ins.append(buf)
            in_specs.append(pl.BlockSpec(memory_space=pl.ANY))
    elif out_t:
        out_shape = [jax.ShapeDtypeStruct((n, m), out_dtype)]
        out_specs = [pl.BlockSpec((tn, tm), lambda i, j, kk: (j, i))]
    else:
        out_shape = [jax.ShapeDtypeStruct((m, n), out_dtype)]
        out_specs = [mn_spec]
    has_add, has_gate = addend is not None, gate_a is not None

    def kern(*refs):
        a_ref, b_ref = refs[0], refs[1]
        pos = 2
        add_ref = gate_ref = None
        if has_add:
            add_ref = refs[pos]
            pos += 1
        if has_gate:
            gate_ref = refs[pos]
            pos += 1
        o_ref = refs[pos + len(aliases)]
        acc = refs[-1] if nk > 1 else None
        kk = pl.program_id(2)

        def partial():
            return _dot(a_ref[...].astype(BF16), b_ref[...].astype(BF16), dims)

        if nk > 1:
            @pl.when(kk == 0)
            def _():
                acc[...] = partial()

            @pl.when((kk > 0) & (kk < nk - 1))
            def _():
                acc[...] += partial()

        @pl.when(kk == nk - 1)
        def _():
            r = partial() + acc[...] if nk > 1 else partial()
            if has_add:
                r = r + add_scale * add_ref[...].astype(F32)
            if has_gate:
                r = r * (2.0 * jnp.sqrt(gate_ref[...].astype(F32)))
            if relu2:
                hh = jnp.maximum(r, 0.0)
                r = hh * hh
            if out_t:
                r = r.T
            o_ref[...] = r.astype(out_dtype)

    return pl.pallas_call(
        kern, out_shape=out_shape, grid=(m // tm, n // tn, nk), in_specs=in_specs, out_specs=out_specs,
        scratch_shapes=[pltpu.VMEM((tm, tn), F32)] if nk > 1 else [], input_output_aliases=aliases,
        name=name, compiler_params=_cp())(*ins)[0]


def _add_ln(res, y, g, b, name, res_affine=None, tm=512):
    t = res.shape[0]
    tm = min(tm, t)
    affine = res_affine is not None

    def kern(*refs):
        if affine:
            x_ref, y_ref, g_ref, b_ref, g0_ref, b0_ref, ob_ref, xh_ref, r_ref = refs
            x = x_ref[...] * g0_ref[...] + b0_ref[...]
        else:
            x_ref, y_ref, g_ref, b_ref, ob_ref, xh_ref, r_ref = refs
            x = x_ref[...]
        z = ALPHA * x + y_ref[...]
        mu = jnp.mean(z, axis=-1, keepdims=True)
        zc = z - mu
        var = jnp.mean(zc * zc, axis=-1, keepdims=True)
        r = lax.rsqrt(var + LN_EPS)
        xh = zc * r
        ob_ref[...] = (xh * g_ref[...] + b_ref[...]).astype(BF16)
        xh_ref[...] = xh
        r_ref[...] = r

    row = pl.BlockSpec((tm, D), lambda i: (i, 0))
    vec = pl.BlockSpec((1, D), lambda i: (0, 0))
    st = pl.BlockSpec((tm, 1), lambda i: (i, 0))
    ins = [res, y, g.reshape(1, D), b.reshape(1, D)]
    if affine:
        ins += [res_affine[0].reshape(1, D), res_affine[1].reshape(1, D)]
    return pl.pallas_call(
        kern, grid=(t // tm,), in_specs=[row, row] + [vec] * (len(ins) - 2), out_specs=[row, row, st],
        out_shape=[jax.ShapeDtypeStruct((t, D), BF16), jax.ShapeDtypeStruct((t, D), F32),
                   jax.ShapeDtypeStruct((t, 1), F32)],
        name=name, compiler_params=_cp())(*ins)


def _ln_bwd(dout, xhat, rstd, g, name, loss_b=None, tm=512):
    t = dout.shape[0]
    tm = min(tm, t)
    head = loss_b is not None

    def kern(*refs):
        if head:
            do_ref, xh_ref, r_ref, g_ref, b_ref, dz_ref, dzb_ref, dg_ref, db_ref, l_ref = refs
        else:
            do_ref, xh_ref, r_ref, g_ref, dz_ref, dzb_ref, dg_ref, db_ref = refs

        @pl.when(pl.program_id(0) == 0)
        def _():
            dg_ref[...] = jnp.zeros_like(dg_ref)
            db_ref[...] = jnp.zeros_like(db_ref)
            if head:
                l_ref[...] = jnp.zeros_like(l_ref)

        xh = xh_ref[...]
        if head:
            e = xh * g_ref[...] + b_ref[...] - do_ref[...]
            l_ref[...] += jnp.sum(e * e, axis=0, keepdims=True)
            do = e * (1.0 / D)
        else:
            do = do_ref[...]
        dxh = do * g_ref[...]
        m1 = jnp.mean(dxh, axis=-1, keepdims=True)
        m2 = jnp.mean(dxh * xh, axis=-1, keepdims=True)
        dz = r_ref[...] * (dxh - m1 - xh * m2)
        dz_ref[...] = dz
        dzb_ref[...] = dz.astype(BF16)
        dg_ref[...] += jnp.sum(do * xh, axis=0, keepdims=True)
        db_ref[...] += jnp.sum(do, axis=0, keepdims=True)

    row = pl.BlockSpec((tm, D), lambda i: (i, 0))
    vec = pl.BlockSpec((1, D), lambda i: (0, 0))
    st = pl.BlockSpec((tm, 1), lambda i: (i, 0))
    ins = [dout, xhat, rstd, g.reshape(1, D)] + ([loss_b.reshape(1, D)] if head else [])
    return pl.pallas_call(
        kern, grid=(t // tm,), in_specs=[row, row, st] + [vec] * (len(ins) - 3),
        out_specs=[row, row, vec, vec] + ([vec] if head else []),
        out_shape=[jax.ShapeDtypeStruct((t, D), F32), jax.ShapeDtypeStruct((t, D), BF16)]
        + [jax.ShapeDtypeStruct((1, D), F32)] * (3 if head else 2),
        name=name, compiler_params=_cp())(*ins)


def _rope_tables(t):
    half = ROPE // 2
    inv = 10000.0 ** (-jnp.arange(half, dtype=F32) / half)
    ang = jnp.arange(t).astype(F32)[:, None] * inv[None, :]
    cos, sin = jnp.cos(ang), jnp.sin(ang)
    z = jnp.zeros((t, RP - ROPE), F32)
    return jnp.concatenate([cos, cos, z], axis=1), jnp.concatenate([-sin, sin, z], axis=1)


def _swap_halves(x):
    lane = lax.broadcasted_iota(jnp.int32, x.shape, 1)
    return jnp.where(lane < ROPE // 2, pltpu.roll(x, RP - ROPE // 2, 1), pltpu.roll(x, ROPE // 2, 1))


def _rope(x, cos, sin):
    return x * cos + _swap_halves(x) * sin


def _rope_t(gy, cos, sin):
    return gy * cos + _swap_halves(gy * sin)


def _mla_pre(hh, g_q, g_kv, cos, sin, tm=512):
    t = hh.shape[0]
    tm = min(tm, t)

    def kern(h_ref, gq_ref, gkv_ref, c_ref, s_ref, cq_ref, k_ref):
        xq = h_ref[:, 0:QR]
        rq = lax.rsqrt(jnp.mean(xq * xq, axis=-1, keepdims=True) + RMS_EPS)
        cq_ref[...] = (xq * rq * gq_ref[...]).astype(BF16)
        xk = h_ref[:, QR:QR + KVR]
        rk = lax.rsqrt(jnp.mean(xk * xk, axis=-1, keepdims=True) + RMS_EPS)
        k_ref[:, 0:KVR] = (xk * rk * gkv_ref[...]).astype(BF16)
        k_ref[:, KVR:KD] = _rope(h_ref[:, QR + KVR:HW], c_ref[...], s_ref[...]).astype(BF16)

    return pl.pallas_call(
        kern, grid=(t // tm,),
        in_specs=[pl.BlockSpec((tm, HW), lambda i: (i, 0)), pl.BlockSpec((1, QR), lambda i: (0, 0)),
                  pl.BlockSpec((1, KVR), lambda i: (0, 0)), pl.BlockSpec((tm, RP), lambda i: (i, 0)),
                  pl.BlockSpec((tm, RP), lambda i: (i, 0))],
        out_specs=[pl.BlockSpec((tm, QR), lambda i: (i, 0)), pl.BlockSpec((tm, KD), lambda i: (i, 0))],
        out_shape=[jax.ShapeDtypeStruct((t, QR), BF16), jax.ShapeDtypeStruct((t, KD), BF16)],
        name="mla_pre", compiler_params=_cp())(hh, g_q.reshape(1, QR), g_kv.reshape(1, KVR), cos, sin)


def _mla_pre_bwd(hh, dcq, dk, g_q, g_kv, cos, sin, tm=512):
    t = hh.shape[0]
    tm = min(tm, t)

    def rms_bwd(x, dy, g):
        r = lax.rsqrt(jnp.mean(x * x, axis=-1, keepdims=True) + RMS_EPS)
        gdy = dy * g
        dx = r * gdy - x * (r * r * r) * jnp.mean(gdy * x, axis=-1, keepdims=True)
        return dx, jnp.sum(dy * x * r, axis=0, keepdims=True)

    def kern(h_ref, dcq_ref, dk_ref, gq_ref, gkv_ref, c_ref, s_ref, dh_ref, dgq_ref, dgkv_ref):
        @pl.when(pl.program_id(0) == 0)
        def _():
            dgq_ref[...] = jnp.zeros_like(dgq_ref)
            dgkv_ref[...] = jnp.zeros_like(dgkv_ref)

        dxq, dgq = rms_bwd(h_ref[:, 0:QR], dcq_ref[...], gq_ref[...])
        dxk, dgk = rms_bwd(h_ref[:, QR:QR + KVR], dk_ref[:, 0:KVR], gkv_ref[...])
        dh_ref[:, 0:QR] = dxq.astype(BF16)
        dh_ref[:, QR:QR + KVR] = dxk.astype(BF16)
        dh_ref[:, QR + KVR:HW] = _rope_t(dk_ref[:, KVR:KD], c_ref[...], s_ref[...]).astype(BF16)
        dgq_ref[...] += dgq
        dgkv_ref[...] += dgk

    return pl.pallas_call(
        kern, grid=(t // tm,),
        in_specs=[pl.BlockSpec((tm, HW), lambda i: (i, 0)), pl.BlockSpec((tm, QR), lambda i: (i, 0)),
                  pl.BlockSpec((tm, KD), lambda i: (i, 0)), pl.BlockSpec((1, QR), lambda i: (0, 0)),
                  pl.BlockSpec((1, KVR), lambda i: (0, 0)), pl.BlockSpec((tm, RP), lambda i: (i, 0)),
                  pl.BlockSpec((tm, RP), lambda i: (i, 0))],
        out_specs=[pl.BlockSpec((tm, HW), lambda i: (i, 0)), pl.BlockSpec((1, QR), lambda i: (0, 0)),
                   pl.BlockSpec((1, KVR), lambda i: (0, 0))],
        out_shape=[jax.ShapeDtypeStruct((t, HW), BF16), jax.ShapeDtypeStruct((1, QR), F32),
                   jax.ShapeDtypeStruct((1, KVR), F32)],
        name="mla_pre_bwd", compiler_params=_cp())(hh, dcq, dk, g_q.reshape(1, QR), g_kv.reshape(1, KVR), cos, sin)


def _q_prep(q2, wuk_t, cos, sin, tm=512):
    t = q2.shape[0]
    tm = min(tm, t)

    def kern(q_ref, w_ref, c_ref, s_ref, o_ref):
        cos_, sin_ = c_ref[...], s_ref[...]
        for h in range(H):
            qn = q_ref[:, h * NOPE:(h + 1) * NOPE].astype(BF16)
            o_ref[:, h * KD:h * KD + KVR] = (_dot(qn, w_ref[h], NN) * QSCALE).astype(BF16)
            qr = q_ref[:, H * NOPE + h * RP:H * NOPE + (h + 1) * RP]
            o_ref[:, h * KD + KVR:(h + 1) * KD] = (_rope(qr, cos_, sin_) * QSCALE).astype(BF16)

    return pl.pallas_call(
        kern, grid=(t // tm,),
        in_specs=[pl.BlockSpec((tm, 2 * H * NOPE), lambda i: (i, 0)), pl.BlockSpec((H, NOPE, KVR), lambda i: (0, 0, 0)),
                  pl.BlockSpec((tm, RP), lambda i: (i, 0)), pl.BlockSpec((tm, RP), lambda i: (i, 0))],
        out_specs=pl.BlockSpec((tm, H * KD), lambda i: (i, 0)),
        out_shape=jax.ShapeDtypeStruct((t, H * KD), BF16),
        name="q_prep", compiler_params=_cp())(q2, wuk_t, cos, sin)


def _q_prep_bwd(dq_cat, q2, wuk_h, cos, sin, tm=512):
    t = q2.shape[0]
    tm = min(tm, t)

    def kern(dq_ref, q_ref, w_ref, c_ref, s_ref, o_ref, dw_ref):
        @pl.when(pl.program_id(0) == 0)
        def _():
            dw_ref[...] = jnp.zeros_like(dw_ref)

        cos_, sin_ = c_ref[...], s_ref[...]
        for h in range(H):
            dql = dq_ref[:, h * KD:h * KD + KVR].astype(BF16)
            o_ref[:, h * NOPE:(h + 1) * NOPE] = _dot(dql, w_ref[h], NN).astype(BF16)
            dqr = dq_ref[:, h * KD + KVR:(h + 1) * KD]
            o_ref[:, H * NOPE + h * RP:H * NOPE + (h + 1) * RP] = _rope_t(dqr, cos_, sin_).astype(BF16)
            qn = q_ref[:, h * NOPE:(h + 1) * NOPE].astype(BF16)
            dw_ref[h] += _dot(qn, dql, TN)

    return pl.pallas_call(
        kern, grid=(t // tm,),
        in_specs=[pl.BlockSpec((tm, H * KD), lambda i: (i, 0)), pl.BlockSpec((tm, 2 * H * NOPE), lambda i: (i, 0)),
                  pl.BlockSpec((H, KVR, NOPE), lambda i: (0, 0, 0)),
                  pl.BlockSpec((tm, RP), lambda i: (i, 0)), pl.BlockSpec((tm, RP), lambda i: (i, 0))],
        out_specs=[pl.BlockSpec((tm, 2 * H * NOPE), lambda i: (i, 0)), pl.BlockSpec((H, NOPE, KVR), lambda i: (0, 0, 0))],
        out_shape=[jax.ShapeDtypeStruct((t, 2 * H * NOPE), BF16), jax.ShapeDtypeStruct((H, NOPE, KVR), F32)],
        name="q_prep_bwd", compiler_params=_cp())(dq_cat, q2, wuk_h, cos, sin)


def _o_up(o_lat, wuv_h, tm=512):
    t = o_lat.shape[0]
    tm = min(tm, t)

    def kern(x_ref, w_ref, o_ref):
        for h in range(H):
            xl = x_ref[:, h * KVR:(h + 1) * KVR].astype(BF16)
            o_ref[:, h * NOPE:(h + 1) * NOPE] = _dot(xl, w_ref[h], NN).astype(BF16)

    return pl.pallas_call(
        kern, grid=(t // tm,),
        in_specs=[pl.BlockSpec((tm, H * KVR), lambda i: (i, 0)), pl.BlockSpec((H, KVR, NOPE), lambda i: (0, 0, 0))],
        out_specs=pl.BlockSpec((tm, H * NOPE), lambda i: (i, 0)),
        out_shape=jax.ShapeDtypeStruct((t, H * NOPE), BF16),
        name="o_up", compiler_params=_cp())(o_lat, wuv_h)


def _o_up_bwd(do, o_lat, wuv_h, tm=512):
    t = do.shape[0]
    tm = min(tm, t)

    def kern(do_ref, x_ref, w_ref, dx_ref, dw_ref, dlt_ref):
        @pl.when(pl.program_id(0) == 0)
        def _():
            dw_ref[...] = jnp.zeros_like(dw_ref)

        for h in range(H):
            dh_ = do_ref[:, h * NOPE:(h + 1) * NOPE]
            x = x_ref[:, h * KVR:(h + 1) * KVR]
            dx = _dot(dh_, w_ref[h], NT)
            dx_ref[:, h * KVR:(h + 1) * KVR] = dx.astype(BF16)
            dw_ref[h] += _dot(x.astype(BF16), dh_, TN)
            dl = jnp.broadcast_to(jnp.sum(dx * x, axis=1)[:, None], (tm, 128))
            dlt_ref[h] = dl.T[0:1, :]

    return pl.pallas_call(
        kern, grid=(t // tm,),
        in_specs=[pl.BlockSpec((tm, H * NOPE), lambda i: (i, 0)), pl.BlockSpec((tm, H * KVR), lambda i: (i, 0)),
                  pl.BlockSpec((H, KVR, NOPE), lambda i: (0, 0, 0))],
        out_specs=[pl.BlockSpec((tm, H * KVR), lambda i: (i, 0)), pl.BlockSpec((H, KVR, NOPE), lambda i: (0, 0, 0)),
                   pl.BlockSpec((H, 1, tm), lambda i: (0, 0, i))],
        out_shape=[jax.ShapeDtypeStruct((t, H * KVR), BF16), jax.ShapeDtypeStruct((H, KVR, NOPE), F32),
                   jax.ShapeDtypeStruct((H, 1, t), F32)],
        name="o_up_bwd", compiler_params=_cp())(do, o_lat, wuv_h)


def _causal_pairs(nq):
    return [(i, j) for i in range(nq) for j in range(i + 1)]


def _lane_tile(stat, width):
    return jnp.tile(stat, (1, width // 128))


def _flash_fwd(qcat, kc, bq, hb, gather=None):
    t = kc.shape[0]
    nq = t // bq
    pairs = _causal_pairs(nq)
    itab = jnp.asarray(np.array([p[0] for p in pairs], np.int32))
    jtab = jnp.asarray(np.array([p[1] for p in pairs], np.int32))

    ng = H // hb
    hosting = gather is not None

    def kern(it, jt, q_ref, k_ref, *rest):
        if hosting:
            w_ref, o_ref, lset_ref, wall_ref, m_sc, l_sc, acc_sc, send_sems, recv_sems = rest
            ag_start, ag_forward, ag_finish = _allgather_schedule(w_ref, wall_ref, send_sems, recv_sems)
        else:
            o_ref, lset_ref, m_sc, l_sc, acc_sc = rest
        grp = pl.program_id(0)
        st = pl.program_id(1)
        i, j = it[st], jt[st]

        if hosting:
            @pl.when((grp == 0) & (st == 0))
            def _():
                ag_start()

        @pl.when(j == 0)
        def _():
            m_sc[...] = jnp.full_like(m_sc, NEG)
            l_sc[...] = jnp.zeros_like(l_sc)
            acc_sc[...] = jnp.zeros_like(acc_sc)

        def update(masked):
            k = k_ref[...]
            v = k[:, 0:KVR]
            if masked:
                row = lax.broadcasted_iota(jnp.int32, (bq, bq), 0)
                col = lax.broadcasted_iota(jnp.int32, (bq, bq), 1)
                keep = col <= row
            s_next = _dot(q_ref[:, 0:KD], k, NT)
            for hh in range(hb):
                s = s_next
                if hh + 1 < hb:
                    s_next = _dot(q_ref[:, (hh + 1) * KD:(hh + 2) * KD], k, NT)
                if masked:
                    s = jnp.where(keep, s, NEG)
                m_prev = m_sc[hh]
                m_next = jnp.maximum(m_prev, jnp.max(s, axis=1)[:, None])
                p = jnp.exp2(s - _lane_tile(m_next, bq))
                a = jnp.exp2(m_prev - m_next)
                l_sc[hh] = a * l_sc[hh] + jnp.sum(p, axis=1)[:, None]
                acc_sc[hh] = _lane_tile(a, KVR) * acc_sc[hh] + _dot(p.astype(BF16), v, NN)
                m_sc[hh] = m_next

        @pl.when(j < i)
        def _():
            update(False)

        @pl.when(j == i)
        def _():
            update(True)
            for hh in range(hb):
                l = l_sc[hh]
                o_ref[:, hh * KVR:(hh + 1) * KVR] = acc_sc[hh] / _lane_tile(l, KVR)
                lset_ref[hh] = (m_sc[hh] + jnp.log2(l)).T[0:1, :]

        if hosting:
            @pl.when((grp == ng - 1) & (st == 0))
            def _():
                ag_forward()

            @pl.when((grp == ng - 1) & (st == len(pairs) - 1))
            def _():
                ag_finish()

    in_specs = [pl.BlockSpec((bq, hb * KD), lambda g, s, it, jt: (it[s], g)),
                pl.BlockSpec((bq, KD), lambda g, s, it, jt: (jt[s], 0))]
    out_specs = [pl.BlockSpec((bq, hb * KVR), lambda g, s, it, jt: (it[s], g)),
                 pl.BlockSpec((hb, 1, bq), lambda g, s, it, jt: (g, 0, it[s]))]
    out_shape = [jax.ShapeDtypeStruct((t, H * KVR), F32), jax.ShapeDtypeStruct((H, 1, t), F32)]
    scratch = [pltpu.VMEM((hb, bq, 128), F32), pltpu.VMEM((hb, bq, 128), F32), pltpu.VMEM((hb, bq, KVR), F32)]
    args = [itab, jtab, qcat, kc]
    if hosting:
        in_specs.append(ANY)
        out_specs.append(ANY)
        out_shape.append(jax.ShapeDtypeStruct((4,) + gather.shape, gather.dtype))
        scratch += AG_SEMS
        args.append(gather)
    gs = pltpu.PrefetchScalarGridSpec(num_scalar_prefetch=2, grid=(ng, len(pairs)), in_specs=in_specs,
                                      out_specs=out_specs, scratch_shapes=scratch)
    return pl.pallas_call(kern, grid_spec=gs, out_shape=out_shape, name="mla_flash_fwd",
                          compiler_params=_cp())(*args)


def _flash_dkv(qcat, kc, do_lat, lse_t, delta_t, bq, hb, exchange=None):
    hosting = exchange is not None
    t = kc.shape[0]
    nq = t // bq
    ng = H // hb
    npairs = nq * (nq + 1) // 2
    steps = [(j, g, i) for j in range(nq) for g in range(ng) for i in range(j, nq)]
    jtab = jnp.asarray(np.array([s[0] for s in steps], np.int32))
    gtab = jnp.asarray(np.array([s[1] for s in steps], np.int32))
    itab = jnp.asarray(np.array([s[2] for s in steps], np.int32))
    ptab = jnp.asarray(np.array([s[2] * (s[2] + 1) // 2 + s[0] for s in steps], np.int32))

    def kern(jt, gt, it, pt, q_ref, k_ref, do_ref, lset_ref, dlt_ref, *rest):
        if hosting:
            p_ref, dk_ref, ds_ref, slots_ref, dk_sc, dv_sc, send_sems, recv_sems = rest
            xc_start, xc_finish = _device_exchange_schedule(p_ref, slots_ref, send_sems, recv_sems)
        else:
            dk_ref, ds_ref, dk_sc, dv_sc = rest
        st = pl.program_id(0)
        j, g, i = jt[st], gt[st], it[st]

        if hosting:
            @pl.when(st == 0)
            def _():
                xc_start()

        @pl.when((g == 0) & (i == j))
        def _():
            dk_sc[...] = jnp.zeros_like(dk_sc)
            dv_sc[...] = jnp.zeros_like(dv_sc)

        def update(masked):
            k = k_ref[...]
            v = k[:, 0:KVR]
            if masked:
                row = lax.broadcasted_iota(jnp.int32, (bq, bq), 0)
                col = lax.broadcasted_iota(jnp.int32, (bq, bq), 1)
                keep = row <= col

            def first_matmuls(hh):
                dob = do_ref[:, hh * KVR:(hh + 1) * KVR].astype(BF16)
                return _dot(k, q_ref[:, hh * KD:(hh + 1) * KD], NT), _dot(v, dob, NT), dob

            nxt = first_matmuls(0)
            for hh in range(hb):
                s, dp, dob = nxt
                if hh + 1 < hb:
                    nxt = first_matmuls(hh + 1)
                if masked:
                    s = jnp.where(keep, s, NEG)
                p = jnp.exp2(s - lset_ref[hh])
                dv_sc[...] += _dot(p.astype(BF16), dob, NN)
                dsb = (p * (dp - dlt_ref[hh])).astype(BF16)
                ds_ref[0, 0, hh] = dsb
                dk_sc[...] += _dot(dsb, q_ref[:, hh * KD:(hh + 1) * KD], NN)

        @pl.when(i > j)
        def _():
            update(False)

        @pl.when(i == j)
        def _():
            update(True)

        @pl.when((g == ng - 1) & (i == nq - 1))
        def _():
            dk_ref[:, 0:KVR] = dk_sc[:, 0:KVR] * LN2 + dv_sc[...]
            dk_ref[:, KVR:KD] = dk_sc[:, KVR:KD] * LN2

        if hosting:
            @pl.when(st == len(steps) - 1)
            def _():
                xc_finish()

    in_specs = [pl.BlockSpec((bq, hb * KD), lambda s, jt, gt, it, pt: (it[s], gt[s])),
                pl.BlockSpec((bq, KD), lambda s, jt, gt, it, pt: (jt[s], 0)),
                pl.BlockSpec((bq, hb * KVR), lambda s, jt, gt, it, pt: (it[s], gt[s])),
                pl.BlockSpec((hb, 1, bq), lambda s, jt, gt, it, pt: (gt[s], 0, it[s])),
                pl.BlockSpec((hb, 1, bq), lambda s, jt, gt, it, pt: (gt[s], 0, it[s]))]
    out_specs = [pl.BlockSpec((bq, KD), lambda s, jt, gt, it, pt: (jt[s], 0)),
                 pl.BlockSpec((1, 1, hb, bq, bq), lambda s, jt, gt, it, pt: (gt[s], pt[s], 0, 0, 0))]
    out_shape = [jax.ShapeDtypeStruct((t, KD), F32), jax.ShapeDtypeStruct((ng, npairs, hb, bq, bq), BF16)]
    scratch = [pltpu.VMEM((bq, KD), F32), pltpu.VMEM((bq, KVR), F32)]
    args = [jtab, gtab, itab, ptab, qcat, kc, do_lat, lse_t, delta_t]
    if hosting:
        in_specs.append(ANY)
        out_specs.append(ANY)
        out_shape.append(jax.ShapeDtypeStruct((8, exchange.shape[1] // 2, D), exchange.dtype))
        scratch += ALL_SEMS
        args.append(exchange)
    gs = pltpu.PrefetchScalarGridSpec(num_scalar_prefetch=4, grid=(len(steps),), in_specs=in_specs,
                                      out_specs=out_specs, scratch_shapes=scratch)
    return pl.pallas_call(kern, grid_spec=gs, out_shape=out_shape, name="mla_flash_dkv",
                          compiler_params=_cp())(*args)


def _flash_dq(ds_all, kc_t, bq, hb, exchange=None):
    nq = kc_t.shape[0]
    t = nq * bq
    ng = H // hb
    pairs = _causal_pairs(nq)
    itab = jnp.asarray(np.array([p[0] for p in pairs], np.int32))
    jtab = jnp.asarray(np.array([p[1] for p in pairs], np.int32))
    hosting = exchange is not None

    hh2 = hb // 2

    def kern(it, jt, dsa_ref, dsb_ref, kt_ref, *rest):
        if hosting:
            p_ref, dq_ref, slots_ref, acc_sc, send_sems, recv_sems = rest
            xc_start, xc_finish = _device_exchange_schedule(p_ref, slots_ref, send_sems, recv_sems)
        else:
            dq_ref, acc_sc = rest
        grp = pl.program_id(0)
        st = pl.program_id(1)
        i, j = it[st], jt[st]
        kt = kt_ref[...]

        def ds(hh):
            return dsa_ref[0, 0, hh] if hh < hh2 else dsb_ref[0, 0, hh - hh2]

        if hosting:
            @pl.when((grp == 0) & (st == 0))
            def _():
                xc_start()

        @pl.when(j == 0)
        def _():
            for hh in range(hb):
                acc_sc[hh] = _dot(kt, ds(hh), NN)

        @pl.when((j > 0) & (j < i))
        def _():
            for hh in range(hb):
                acc_sc[hh] += _dot(kt, ds(hh), NN)

        @pl.when(j == i)
        def _():
            for hh in range(hb):
                tot = _dot(kt, ds(hh), NN)
                tot = jnp.where(i > 0, tot + acc_sc[hh], tot)
                dq_ref[:, hh * KD:(hh + 1) * KD] = tot.T * MLA_SCALE

        if hosting:
            @pl.when((grp == ng - 1) & (st == len(pairs) - 1))
            def _():
                xc_finish()

    in_specs = [pl.BlockSpec((1, 1, hh2, bq, bq), lambda g, s, it, jt: (g, s, 0, 0, 0)),
                pl.BlockSpec((1, 1, hh2, bq, bq), lambda g, s, it, jt: (g, s, 1, 0, 0)),
                pl.BlockSpec((None, KD, bq), lambda g, s, it, jt: (jt[s], 0, 0))]
    out_specs = [pl.BlockSpec((bq, hb * KD), lambda g, s, it, jt: (it[s], g))]
    out_shape = [jax.ShapeDtypeStruct((t, H * KD), F32)]
    scratch = [pltpu.VMEM((hb, KD, bq), F32)]
    args = [itab, jtab, ds_all, ds_all, kc_t]
    if hosting:
        in_specs.append(ANY)
        out_specs.append(ANY)
        out_shape.append(jax.ShapeDtypeStruct((8, exchange.shape[1] // 2, D), exchange.dtype))
        scratch += ALL_SEMS
        args.append(exchange)
    gs = pltpu.PrefetchScalarGridSpec(num_scalar_prefetch=2, grid=(ng, len(pairs)), in_specs=in_specs,
                                      out_specs=out_specs, scratch_shapes=scratch)
    outs = pl.pallas_call(kern, grid_spec=gs, out_shape=out_shape, name="mla_flash_dq",
                          compiler_params=_cp())(*args)
    return outs if hosting else outs[0]


def _bucket_table():
    d = np.arange(WIN)
    max_exact = NBKT // 2
    nf = np.maximum(d, 1).astype(np.float32)
    large = max_exact + (np.log(nf / np.float32(max_exact)) / np.float32(math.log(WIN / max_exact))
                         * np.float32(NBKT - max_exact)).astype(np.int32)
    large = np.minimum(large, NBKT - 1)
    bucket = np.where(d < max_exact, d, large).astype(np.int32)
    jj = np.arange(2 * WIN)[:, None]
    ii = np.arange(WIN)[None, :]
    dist = ii + WIN - jj
    valid = (dist >= 0) & (dist < WIN)
    return np.where(valid, bucket[np.clip(dist, 0, WIN - 1)], -1).astype(np.int32)


def _bias_build(rel_bias, bkt):
    def kern(bk_ref, rb_ref, o_ref):
        bk = bk_ref[...]
        for hd in range(QH):
            acc = jnp.full((2 * WIN, WIN), NEG, F32)
            for b in range(NBKT):
                acc = jnp.where(bk == b, rb_ref[b, hd], acc)
            o_ref[hd] = acc

    return pl.pallas_call(
        kern, in_specs=[pl.BlockSpec(memory_space=pltpu.VMEM), pl.BlockSpec(memory_space=pltpu.SMEM)],
        out_specs=pl.BlockSpec(memory_space=pltpu.VMEM),
        out_shape=jax.ShapeDtypeStruct((QH, 2 * WIN, WIN), F32), name="swa_bias_build")(bkt, rel_bias)


def _bias_bwd(dbias, bkt):
    def kern(db_ref, bk_ref, o_ref):
        bk = bk_ref[...]
        for hd in range(QH):
            g = db_ref[hd]
            for b in range(NBKT):
                r = b * QH + hd
                o_ref[r:r + 1, :] = jnp.sum(jnp.where(bk == b, g, 0.0), axis=0, keepdims=True)

    return pl.pallas_call(
        kern, in_specs=[pl.BlockSpec(memory_space=pltpu.VMEM), pl.BlockSpec(memory_space=pltpu.VMEM)],
        out_specs=pl.BlockSpec(memory_space=pltpu.VMEM),
        out_shape=jax.ShapeDtypeStruct((NBKT * QH, WIN), F32), name="swa_bias_bwd")(dbias, bkt)


def _swa_finish_scores(raw, bias, first):
    s = raw * SWA_SCALE + bias
    if first is not None:
        row = lax.broadcasted_iota(jnp.int32, s.shape, 0)
        s = jnp.where(jnp.logical_or(jnp.logical_not(first), row >= WIN), s, NEG)
    return s


def _swa_fwd(qkv_t, bias, sinks, qb):
    t = qkv_t.shape[1]
    w = qb * WIN
    nst = t // w

    def kern(q_ref, kc_ref, kp_ref, vc_ref, vp_ref, b_ref, sk_ref, o_ref, lse_ref):
        n = pl.program_id(0)
        kfull = jnp.concatenate([kp_ref[...], kc_ref[...]], axis=1)
        vfull = jnp.concatenate([vp_ref[...], vc_ref[...]], axis=1)
        head_row = lax.broadcasted_iota(jnp.int32, (QH, WIN), 0)
        groups = [(b, kh) for b in range(qb) for kh in range(KVH)]

        def raw_scores(b, kh):
            k_band = kfull[kh * HD:(kh + 1) * HD, b * WIN:(b + 2) * WIN]
            return [_dot(k_band, q_ref[(kh * G + g) * HD:(kh * G + g + 1) * HD, b * WIN:(b + 1) * WIN], TN)
                    for g in range(G)]

        o_rows = [[] for _ in range(qb)]
        lse_tiles = [jnp.zeros((QH, WIN), F32) for _ in range(qb)]
        nxt_scores = raw_scores(*groups[0])
        for gi, (b, kh) in enumerate(groups):
            scores = nxt_scores
            if gi + 1 < len(groups):
                nxt_scores = raw_scores(*groups[gi + 1])
            v_band = vfull[kh * HD:(kh + 1) * HD, b * WIN:(b + 2) * WIN]
            for g in range(G):
                hd = kh * G + g
                s = _swa_finish_scores(scores[g], b_ref[hd], (n == 0) if b == 0 else None)
                sink = sk_ref[hd]
                m = jnp.maximum(jnp.max(s, axis=0, keepdims=True), sink)
                p = jnp.exp(s - m)
                den = jnp.sum(p, axis=0, keepdims=True) + jnp.exp(sink - m)
                p = p / den
                o_rows[b].append(_dot(v_band, p.astype(BF16), NN))
                lse_tiles[b] = jnp.where(head_row == hd, m + jnp.log(den), lse_tiles[b])
        o_ref[...] = jnp.concatenate([jnp.concatenate(rows, axis=0) for rows in o_rows], axis=1)
        lse_ref[...] = jnp.concatenate(lse_tiles, axis=1)

    prev = lambda r: (lambda n: (r, jnp.maximum(n * qb - 1, 0)))
    return pl.pallas_call(
        kern, grid=(nst,),
        in_specs=[pl.BlockSpec((QH * HD, w), lambda n: (0, n)),
                  pl.BlockSpec((KVH * HD, w), lambda n: (4, n)), pl.BlockSpec((KVH * HD, WIN), prev(4)),
                  pl.BlockSpec((KVH * HD, w), lambda n: (5, n)), pl.BlockSpec((KVH * HD, WIN), prev(5)),
                  pl.BlockSpec((QH, 2 * WIN, WIN), lambda n: (0, 0, 0)),
                  pl.BlockSpec(memory_space=pltpu.SMEM)],
        out_specs=[pl.BlockSpec((QH * HD, w), lambda n: (0, n)), pl.BlockSpec((QH, w), lambda n: (0, n))],
        out_shape=[jax.ShapeDtypeStruct((QH * HD, t), F32), jax.ShapeDtypeStruct((QH, t), F32)],
        name="swa_fwd", compiler_params=_cp())(qkv_t, qkv_t, qkv_t, qkv_t, qkv_t, bias, sinks)


def _swa_bwd(qkv_t, do_t, o_t, lse, bias, sinks, qb):
    t = qkv_t.shape[1]
    w = qb * WIN
    nst = t // w
    nblk = t // WIN

    def kern(q_ref, kc_ref, kp_ref, vc_ref, vp_ref, do_ref, o_ref, lse_ref, qn_ref, don_ref, on_ref, lsen_ref,
             b_ref, sk_ref, dqkv_ref, db_ref, dsk_ref):
        n = pl.program_id(0)

        @pl.when(n == 0)
        def _():
            db_ref[...] = jnp.zeros_like(db_ref)
            dsk_ref[...] = jnp.zeros_like(dsk_ref)

        kfull = jnp.concatenate([kp_ref[...], kc_ref[...]], axis=1)
        vfull = jnp.concatenate([vp_ref[...], vc_ref[...]], axis=1)
        head_row = lax.broadcasted_iota(jnp.int32, (QH, WIN), 0)
        db_acc = [None] * QH
        dsk_tile = jnp.zeros((QH, WIN), F32)
        prev_part = [[[None] * qb for _ in range(KVH)] for _ in range(2)]
        cur_part = [[[None] * qb for _ in range(KVH)] for _ in range(2)]
        groups = [(b, kh) for b in range(qb) for kh in range(KVH)]

        def first_matmuls(b, kh):
            k_band = kfull[kh * HD:(kh + 1) * HD, b * WIN:(b + 2) * WIN]
            v_band = vfull[kh * HD:(kh + 1) * HD, b * WIN:(b + 2) * WIN]
            out = []
            for g in range(G):
                rs = slice((kh * G + g) * HD, (kh * G + g + 1) * HD)
                dob = do_ref[rs, b * WIN:(b + 1) * WIN].astype(BF16)
                out.append((_dot(k_band, q_ref[rs, b * WIN:(b + 1) * WIN], TN), _dot(v_band, dob, TN), dob))
            return out

        dq_rows = [[] for _ in range(qb)]
        nxt_first = first_matmuls(*groups[0])
        for gi, (b, kh) in enumerate(groups):
            first = nxt_first
            if gi + 1 < len(groups):
                nxt_first = first_matmuls(*groups[gi + 1])
            cs = slice(b * WIN, (b + 1) * WIN)
            k_band = kfull[kh * HD:(kh + 1) * HD, b * WIN:(b + 2) * WIN]
            dk_b = dv_b = None
            for g in range(G):
                hd = kh * G + g
                rs = slice(hd * HD, (hd + 1) * HD)
                raw, dp, dob = first[g]
                lse_h = lse_ref[hd:hd + 1, cs]
                s = _swa_finish_scores(raw, b_ref[hd], (n == 0) if b == 0 else None)
                p = jnp.exp(s - lse_h)
                dl = jnp.sum(do_ref[rs, cs] * o_ref[rs, cs], axis=0, keepdims=True)
                ds = p * (dp - dl)
                db_acc[hd] = ds if db_acc[hd] is None else db_acc[hd] + ds
                dsk_tile = jnp.where(head_row == hd, dsk_tile - jnp.exp(sk_ref[hd] - lse_h) * dl, dsk_tile)
                dss = (ds * SWA_SCALE).astype(BF16)
                dq_rows[b].append(_dot(k_band, dss, NN).astype(BF16))
                dk_h = _dot(q_ref[rs, cs], dss, NT)
                dv_h = _dot(dob, p.astype(BF16), NT)
                dk_b = dk_h if dk_b is None else dk_b + dk_h
                dv_b = dv_h if dv_b is None else dv_b + dv_h
            for which, val in ((0, dk_b), (1, dv_b)):
                prev_part[which][kh][b] = val[:, 0:WIN]
                cur_part[which][kh][b] = val[:, WIN:2 * WIN]
        dq_cols = [jnp.concatenate(rows, axis=0) for rows in dq_rows]

        live = n < nst - 1
        ls = slice((qb - 1) * WIN, qb * WIN)
        halo = [[None] * KVH for _ in range(2)]
        for kh in range(KVH):
            k_last = kc_ref[kh * HD:(kh + 1) * HD, ls]
            v_last = vc_ref[kh * HD:(kh + 1) * HD, ls]
            dk_b = dv_b = None
            for g in range(G):
                hd = kh * G + g
                rs = slice(hd * HD, (hd + 1) * HD)
                q_t = qn_ref[rs, :]
                do = don_ref[rs, :]
                s = _dot(k_last, q_t, TN) * SWA_SCALE + b_ref[hd, 0:WIN, :]
                p = jnp.exp(s - lsen_ref[hd:hd + 1, :])
                dob = do.astype(BF16)
                dp = _dot(v_last, dob, TN)
                dl = jnp.sum(do * on_ref[rs, :], axis=0, keepdims=True)
                dss = (p * (dp - dl) * SWA_SCALE).astype(BF16)
                dk_h = _dot(q_t, dss, NT)
                dv_h = _dot(dob, p.astype(BF16), NT)
                dk_b = dk_h if dk_b is None else dk_b + dk_h
                dv_b = dv_h if dv_b is None else dv_b + dv_h
            halo[0][kh] = jnp.where(live, dk_b, 0.0)
            halo[1][kh] = jnp.where(live, dv_b, 0.0)

        kv_rows = []
        for which in range(2):
            for kh in range(KVH):
                blocks = [cur_part[which][kh][p] + (prev_part[which][kh][p + 1] if p + 1 < qb else halo[which][kh])
                          for p in range(qb)]
                kv_rows.append(jnp.concatenate(blocks, axis=1))
        dqkv_ref[...] = jnp.concatenate(
            [jnp.concatenate(dq_cols, axis=1), jnp.concatenate(kv_rows, axis=0).astype(BF16)], axis=0)
        db_ref[...] += jnp.stack(db_acc)
        dsk_ref[...] += dsk_tile

    prev = lambda r: (lambda n: (r, jnp.maximum(n * qb - 1, 0)))
    nxt = lambda n: (0, jnp.minimum((n + 1) * qb, nblk - 1))
    big = lambda: pl.BlockSpec((QH * HD, w), lambda n: (0, n))
    return pl.pallas_call(
        kern, grid=(nst,),
        in_specs=[big(),
                  pl.BlockSpec((KVH * HD, w), lambda n: (4, n)), pl.BlockSpec((KVH * HD, WIN), prev(4)),
                  pl.BlockSpec((KVH * HD, w), lambda n: (5, n)), pl.BlockSpec((KVH * HD, WIN), prev(5)),
                  big(), big(), pl.BlockSpec((QH, w), lambda n: (0, n)),
                  pl.BlockSpec((QH * HD, WIN), nxt), pl.BlockSpec((QH * HD, WIN), nxt),
                  pl.BlockSpec((QH * HD, WIN), nxt), pl.BlockSpec((QH, WIN), nxt),
                  pl.BlockSpec((QH, 2 * WIN, WIN), lambda n: (0, 0, 0)),
                  pl.BlockSpec(memory_space=pltpu.SMEM)],
        out_specs=[pl.BlockSpec(((QH + 2 * KVH) * HD, w), lambda n: (0, n)),
                   pl.BlockSpec((QH, 2 * WIN, WIN), lambda n: (0, 0, 0)),
                   pl.BlockSpec((QH, WIN), lambda n: (0, 0))],
        out_shape=[jax.ShapeDtypeStruct(((QH + 2 * KVH) * HD, t), BF16),
                   jax.ShapeDtypeStruct((QH, 2 * WIN, WIN), F32), jax.ShapeDtypeStruct((QH, WIN), F32)],
        name="swa_bwd", compiler_params=_cp())(
            qkv_t, qkv_t, qkv_t, qkv_t, qkv_t, do_t, o_t, lse, qkv_t, do_t, o_t, lse, bias, sinks)


def _adamw_math(w, g, m, v):
    nm = B1 * m + (1.0 - B1) * g
    nv = B2 * v + (1.0 - B2) * (g * g)
    mhat = nm * (1.0 / (1.0 - B1 ** STEP))
    vhat = nv * (1.0 / (1.0 - B2 ** STEP))
    return -LR * (mhat / (jnp.sqrt(vhat) + ADAM_EPS) + WD * w), nm, nv


def _adamw_layers(w, m, v, g0buf, g1buf, off, name, tm=512):
    rows = w.shape[1]
    nb, ob = rows // tm, off // tm

    def kern(w_ref, m_ref, v_ref, g0_ref, g1_ref, gr_ref, d_ref, nm_ref, nv_ref):
        g_ = jnp.where(pl.program_id(0) == 0, g0_ref[...], g1_ref[...])
        gr_ref[...] = g_
        d_ref[...], nm_ref[...], nv_ref[...] = _adamw_math(w_ref[...], g_, m_ref[...], v_ref[...])

    lay = pl.BlockSpec((None, tm, D), lambda l, i: (l, i, 0))
    gsp = pl.BlockSpec((tm, D), lambda l, i: (ob + i, 0))
    return pl.pallas_call(
        kern, grid=(2, nb), in_specs=[lay, lay, lay, gsp, gsp], out_specs=[lay] * 4,
        out_shape=[jax.ShapeDtypeStruct(w.shape, F32)] * 4, name=name, compiler_params=_cp())(w, m, v, g0buf, g1buf)


def _adamw(w, g, m, v, name, tm=544):
    r = w.shape[0]
    tm = r if r % tm else tm

    def kern(w_ref, g_ref, m_ref, v_ref, d_ref, nm_ref, nv_ref):
        d_ref[...], nm_ref[...], nv_ref[...] = _adamw_math(w_ref[...], g_ref[...], m_ref[...], v_ref[...])

    row = pl.BlockSpec((tm, D), lambda i: (i, 0))
    sds = jax.ShapeDtypeStruct((r, D), F32)
    return pl.pallas_call(kern, grid=(r // tm,), in_specs=[row] * 4, out_specs=[row] * 3, out_shape=[sds] * 3,
                          name=name, compiler_params=_cp())(w, g, m, v)


def _mesh_pos():
    return lax.axis_index("x"), lax.axis_index("y"), lax.axis_index("c")


ANY = pl.BlockSpec(memory_space=pl.ANY)


AG_SEMS = [pltpu.SemaphoreType.DMA((6,)), pltpu.SemaphoreType.DMA((6,))]
XCHG_SEMS = [pltpu.SemaphoreType.DMA((3,)), pltpu.SemaphoreType.DMA((3,))]


def _allgather_schedule(w_ref, out_ref, send_sems, recv_sems):
    half = w_ref.shape[0] // 2
    x, y, c = _mesh_pos()
    me, sibling = (x, y, c), (x, y, 1 - c)
    chips = [(1 - x, y), (x, 1 - y), (1 - x, 1 - y)]

    def rows(px, py, pc):
        return out_ref.at[2 * px + py, pl.ds(pc * half, half), :]

    def copy(k, block, to, src=None):
        return pltpu.make_async_remote_copy(
            src_ref=rows(*block) if src is None else src, dst_ref=rows(*block),
            send_sem=send_sems.at[k], recv_sem=recv_sems.at[k], device_id=to, device_id_type=MESH)

    def first():
        return [copy(j, me, (*chip, c), src=w_ref.at[pl.ds(c * half, half), :]) for j, chip in enumerate(chips)]

    def passed():
        return [copy(3 + j, (*chip, c), sibling) for j, chip in enumerate(chips)]

    def start():
        for cp in first():
            cp.start()

    def forward():
        for j, chip in enumerate(chips):
            copy(j, (*chip, c), me).wait_recv()
            passed()[j].start()

    def finish():
        for j, chip in enumerate(chips):
            copy(3 + j, (*chip, 1 - c), me).wait_recv()
        for cp in first() + passed():
            cp.wait_send()

    return start, forward, finish


def _allgather_weights(wpack):
    def body(w_ref, out_ref, send_sems, recv_sems):
        start, forward, finish = _allgather_schedule(w_ref, out_ref, send_sems, recv_sems)
        start()
        forward()
        finish()

    return pl.pallas_call(
        body, out_shape=jax.ShapeDtypeStruct((4,) + wpack.shape, wpack.dtype), in_specs=[ANY], out_specs=ANY,
        scratch_shapes=AG_SEMS, name="allgather_weights")(wpack)


def _row_tile(rows):
    t = min(rows, 512)
    while rows % t or t % 16:
        t -= 16
    return t


def _exchange_core_halves(g, tag):
    half = g.shape[1] // 2

    def body(g_ref, out_ref, send_sem, recv_sem):
        x, y, c = _mesh_pos()
        cp = pltpu.make_async_remote_copy(
            src_ref=g_ref.at[:, pl.ds((1 - c) * half, half), :], dst_ref=out_ref,
            send_sem=send_sem, recv_sem=recv_sem, device_id=(x, y, 1 - c), device_id_type=MESH)
        cp.start()
        cp.wait()

    return pl.pallas_call(
        body, out_shape=jax.ShapeDtypeStruct((4, half, D), g.dtype), in_specs=[ANY], out_specs=ANY,
        scratch_shapes=[pltpu.SemaphoreType.DMA, pltpu.SemaphoreType.DMA], name=f"rs_exchange_cores_{tag}")(g)


def _add_core_halves(g, other, cidx, tag):
    half = other.shape[1]
    tm = _row_tile(half)
    nb = half // tm

    def kern(c_ref, a_ref, b_ref, o_ref):
        o_ref[...] = (a_ref[...] + b_ref[...]).astype(BF16)

    gs = pltpu.PrefetchScalarGridSpec(
        num_scalar_prefetch=1, grid=(4, nb),
        in_specs=[pl.BlockSpec((1, tm, D), lambda s, i, c: (s, c[0] * nb + i, 0)),
                  pl.BlockSpec((1, tm, D), lambda s, i, c: (s, i, 0))],
        out_specs=pl.BlockSpec((1, tm, D), lambda s, i, c: (s, i, 0)))
    return pl.pallas_call(kern, grid_spec=gs, out_shape=jax.ShapeDtypeStruct(other.shape, BF16),
                          name=f"rs_add_cores_{tag}", compiler_params=_cp())(cidx, g, other)


def _chip_exchange_schedule(p_ref, out_ref, send_sems, recv_sems):
    x, y, c = _mesh_pos()
    me = 2 * x + y
    chips = [(1 - x, y), (x, 1 - y), (1 - x, 1 - y)]

    def sends():
        return [pltpu.make_async_remote_copy(
            src_ref=p_ref.at[2 * px + py], dst_ref=out_ref.at[me], send_sem=send_sems.at[j],
            recv_sem=recv_sems.at[j], device_id=(px, py, c), device_id_type=MESH) for j, (px, py) in enumerate(chips)]

    def start():
        for cp in sends():
            cp.start()

    def finish():
        for j, (px, py) in enumerate(chips):
            pltpu.make_async_remote_copy(
                src_ref=p_ref.at[me], dst_ref=out_ref.at[2 * px + py], send_sem=send_sems.at[j],
                recv_sem=recv_sems.at[j], device_id=(px, py, c), device_id_type=MESH).wait_recv()
        for cp in sends():
            cp.wait_send()

    return start, finish


ALL_SEMS = [pltpu.SemaphoreType.DMA((7,)), pltpu.SemaphoreType.DMA((7,))]


def _device_exchange_schedule(g_ref, out_ref, send_sems, recv_sems):
    half = g_ref.shape[1] // 2
    x, y, c = _mesh_pos()
    me = 4 * x + 2 * y + c
    peers = [(x ^ (k >> 2), y ^ ((k >> 1) & 1), c ^ (k & 1)) for k in range(1, 8)]

    def sends():
        return [pltpu.make_async_remote_copy(
            src_ref=g_ref.at[2 * px + py, pl.ds(pc * half, half), :], dst_ref=out_ref.at[me],
            send_sem=send_sems.at[j], recv_sem=recv_sems.at[j], device_id=(px, py, pc), device_id_type=MESH)
            for j, (px, py, pc) in enumerate(peers)]

    def start():
        for cp in sends():
            cp.start()

    def finish():
        for j, (px, py, pc) in enumerate(peers):
            pltpu.make_async_remote_copy(
                src_ref=out_ref.at[me], dst_ref=out_ref.at[4 * px + 2 * py + pc], send_sem=send_sems.at[j],
                recv_sem=recv_sems.at[j], device_id=(px, py, pc), device_id_type=MESH).wait_recv()
        for cp in sends():
            cp.wait_send()

    return start, finish


def _sum_devices(slots, g, pos, tag):
    half = slots.shape[1]
    tm = _row_tile(half)
    nb = half // tm

    def kern(pos_ref, own_ref, *refs):
        acc = own_ref[0].astype(F32)
        for s_ref in refs[:7]:
            acc = acc + s_ref[0].astype(F32)
        refs[7][...] = acc

    def slot(k):
        return pl.BlockSpec((1, tm, D), lambda i, pos: (jnp.bitwise_xor(pos[2], k), i, 0))

    gs = pltpu.PrefetchScalarGridSpec(
        num_scalar_prefetch=1, grid=(nb,),
        in_specs=[pl.BlockSpec((1, tm, D), lambda i, pos: (pos[0], pos[1] * nb + i, 0))] + [slot(k) for k in range(1, 8)],
        out_specs=pl.BlockSpec((tm, D), lambda i, pos: (pos[1] * nb + i, 0)))
    return pl.pallas_call(kern, grid_spec=gs, out_shape=jax.ShapeDtypeStruct((2 * half, D), F32),
                          name=f"rs_sum_devices_{tag}", compiler_params=_cp())(pos, g, *([slots] * 7))


def _exchange_chip_shards(p, tag):
    def body(p_ref, out_ref, send_sems, recv_sems):
        start, finish = _chip_exchange_schedule(p_ref, out_ref, send_sems, recv_sems)
        start()
        finish()

    return pl.pallas_call(
        body, out_shape=jax.ShapeDtypeStruct(p.shape, p.dtype), in_specs=[ANY], out_specs=ANY,
        scratch_shapes=XCHG_SEMS, name=f"rs_exchange_chips_{tag}")(p)


def _sum_slots(slots, p, pos, tag):
    half = slots.shape[1]
    tm = _row_tile(half)
    nb = half // tm

    def kern(pos_ref, p_ref, s1_ref, s2_ref, s3_ref, o_ref):
        o_ref[...] = ((p_ref[0].astype(F32) + s1_ref[0].astype(F32)) + s2_ref[0].astype(F32)) + s3_ref[0].astype(F32)

    def slot(k):
        return pl.BlockSpec((1, tm, D), lambda i, pos: ((pos[0] + k) % 4, i, 0))

    gs = pltpu.PrefetchScalarGridSpec(
        num_scalar_prefetch=1, grid=(nb,), in_specs=[slot(0), slot(1), slot(2), slot(3)],
        out_specs=pl.BlockSpec((tm, D), lambda i, pos: (pos[1] * nb + i, 0)))
    return pl.pallas_call(kern, grid_spec=gs, out_shape=jax.ShapeDtypeStruct((2 * half, D), F32),
                          name=f"rs_sum_chips_{tag}", compiler_params=_cp())(pos, p, slots, slots, slots)


def _join_core_halves(r, tag):
    half = r.shape[0] // 2

    def body(r_ref, out_ref, send_sem, recv_sem):
        x, y, c = _mesh_pos()
        mine = out_ref.at[pl.ds(c * half, half), :]
        cp = pltpu.make_async_remote_copy(
            src_ref=mine, dst_ref=mine, send_sem=send_sem, recv_sem=recv_sem,
            device_id=(x, y, 1 - c), device_id_type=MESH)
        cp.start()
        theirs = out_ref.at[pl.ds((1 - c) * half, half), :]
        pltpu.make_async_remote_copy(
            src_ref=theirs, dst_ref=theirs, send_sem=send_sem, recv_sem=recv_sem,
            device_id=(x, y, 1 - c), device_id_type=MESH).wait_recv()
        cp.wait_send()

    return pl.pallas_call(
        body, out_shape=jax.ShapeDtypeStruct(r.shape, r.dtype), in_specs=[ANY], out_specs=ANY,
        input_output_aliases={0: 0},
        scratch_shapes=[pltpu.SemaphoreType.DMA, pltpu.SemaphoreType.DMA],
        name=f"rs_join_cores_{tag}")(r)


def _allreduce_small(v, name):
    def body(v_ref, out_ref, gat, send_sems, recv_sems):
        x, y, c = _mesh_pos()
        me = 4 * x + 2 * y + c
        gat[me] = v_ref[...]
        sends = []
        for k in range(1, 8):
            peer = (x ^ (k >> 2), y ^ ((k >> 1) & 1), c ^ (k & 1))
            cp = pltpu.make_async_remote_copy(
                src_ref=v_ref, dst_ref=gat.at[me], send_sem=send_sems.at[k - 1], recv_sem=recv_sems.at[k - 1],
                device_id=peer, device_id_type=MESH)
            cp.start()
            sends.append(cp)
        for k in range(1, 8):
            px, py, pc = x ^ (k >> 2), y ^ ((k >> 1) & 1), c ^ (k & 1)
            pltpu.make_async_remote_copy(
                src_ref=v_ref, dst_ref=gat.at[4 * px + 2 * py + pc], send_sem=send_sems.at[k - 1],
                recv_sem=recv_sems.at[k - 1], device_id=(px, py, pc), device_id_type=MESH).wait_recv()
        for cp in sends:
            cp.wait_send()
        acc = gat[0]
        for d in range(1, 8):
            acc = acc + gat[d]
        out_ref[...] = acc

    return pl.pallas_call(
        body, out_shape=jax.ShapeDtypeStruct(v.shape, F32),
        in_specs=[pl.BlockSpec(memory_space=pltpu.VMEM)], out_specs=pl.BlockSpec(memory_space=pltpu.VMEM),
        scratch_shapes=[pltpu.VMEM((8,) + v.shape, F32), pltpu.SemaphoreType.DMA((7,)), pltpu.SemaphoreType.DMA((7,))],
        name=name)(v)


def _mlp_fwd(xb, w_up, w_down, tag):
    a = _mm(xb, w_up[0], "nn", f"mlp_up_{tag}", out_dtype=BF16, relu2=True, b_view=("cols", w_up[1]))
    return a, _mm(a, w_down[0], "nn", f"mlp_down_{tag}", b_view=("rows", w_down[1]))


def _mlp_bwd(dz, dzb, xb, a, w_up, w_down, tag):
    du = _mm(dzb, w_down[0], "nt", f"mlp_down_dx_{tag}", out_dtype=BF16, gate_a=a, b_view=("rows", w_down[1]))
    gsh = _mm(xb, du, "tn", f"mlp_up_dw_{tag}", out_dtype=BF16, out_view=("cols", 2 * ROWS["mlp_w_up"], 0, None))
    gsh = _mm(a, dzb, "tn", f"mlp_down_dw_{tag}", out_dtype=BF16,
              out_view=("rows", 2 * ROWS["mlp_w_up"], ROWS["mlp_w_up"], gsh))
    dx = _mm(du, w_up[0], "nt", f"mlp_up_dx_{tag}", addend=dz, add_scale=ALPHA, b_view=("cols", w_up[1]))
    return dx, gsh


def _fwd_bwd(x, target, w, dist=None, bq=512, qb=4, hb=4):
    t = x.shape[0]
    bq = min(bq, t)
    qb = min(qb, t // WIN)
    cos, sin = _rope_tables(t)
    bkt = jnp.asarray(_bucket_table())
    w_in = jnp.pad(w[("mla_w_in", None)], ((0, 0), (0, HW - (QR + KVR + ROPE))))
    wuq = w[("mla_w_uq", None)]
    wq2 = jnp.concatenate([wuq[:, :, :NOPE].reshape(QR, H * NOPE),
                           jnp.pad(wuq[:, :, NOPE:], ((0, 0), (0, 0), (0, RP - ROPE))).reshape(QR, H * RP)], axis=1)
    wuk_t = w[("mla_w_uk", None)].transpose(1, 2, 0)
    wuk_h = w[("mla_w_uk", None)].transpose(1, 0, 2)
    wuv_h = w[("mla_w_uv", None)].transpose(1, 0, 2)
    w_o = w[("mla_w_o", None)]
    sinks = w["swa_sinks"].reshape(QH)
    lnp = lambda n, l: w[n][l]
    reduced = {}

    hh = _mm(x, w_in, "nn", "mla_in")
    cq, kc = _mla_pre(hh, w["mla_g_q"], w["mla_g_kv"], cos, sin)
    q2 = _mm(cq, wq2, "nn", "mla_uq")
    qcat = _q_prep(q2, wuk_t, cos, sin)
    if dist is None:
        o_lat, lse0_t = _flash_fwd(qcat, kc, bq, hb)
    else:
        o_lat, lse0_t, wall = _flash_fwd(qcat, kc, bq, hb, gather=dist.late_pack)
        wall = lax.dynamic_update_slice(wall, dist.late_pack[None], (dist.shard, 0, 0))
        w = {**w, **_full_from_gathered(AG_LATE, wall, dist.shard_shapes)}
    wqkv = jnp.concatenate([w[("swa_w_q", None)], w[("kv_w_shared", None)]], axis=1)
    wqkv_t = wqkv.T
    wo_s = w[("swa_w_o", None)]
    o0 = _o_up(o_lat, wuv_h)
    y0 = _mm(o0, w_o, "nn", "mla_out")
    x1b, xh1, r1 = _add_ln(x, y0, lnp("ln_mix_g", 0), lnp("ln_mix_b", 0), "ln_mix_0")
    a0, f0 = _mlp_fwd(x1b, w[("mlp_w_up", 0)], w[("mlp_w_down", 0)], 0)
    x2b, xh2, r2 = _add_ln(xh1, f0, lnp("ln_mlp_g", 0), lnp("ln_mlp_b", 0), "ln_mlp_0",
                           res_affine=(lnp("ln_mix_g", 0), lnp("ln_mix_b", 0)))
    bias = _bias_build(w["rel_bias"], bkt)
    qkv_t = _mm(x2b, wqkv, "nn", "swa_qkv", out_dtype=BF16, out_t=True)
    os_t, lse1 = _swa_fwd(qkv_t, bias, sinks, qb)
    y1 = _mm(os_t, wo_s, "tn", "swa_out")
    x3b, xh3, r3 = _add_ln(xh2, y1, lnp("ln_mix_g", 1), lnp("ln_mix_b", 1), "ln_mix_1",
                           res_affine=(lnp("ln_mlp_g", 0), lnp("ln_mlp_b", 0)))
    a1, f1 = _mlp_fwd(x3b, w[("mlp_w_up", 1)], w[("mlp_w_down", 1)], 1)
    _, xh4, r4 = _add_ln(xh3, f1, lnp("ln_mlp_g", 1), lnp("ln_mlp_b", 1), "ln_mlp_1",
                         res_affine=(lnp("ln_mix_g", 1), lnp("ln_mix_b", 1)))

    g = {}
    dz4, dz4b, dg_mlp1, db_mlp1, lpart = _ln_bwd(target, xh4, r4, lnp("ln_mlp_g", 1), "ln_mlp_1_bwd",
                                                 loss_b=lnp("ln_mlp_b", 1))
    dx3, g["mlp1"] = _mlp_bwd(dz4, dz4b, x3b, a1, w[("mlp_w_up", 1)], w[("mlp_w_down", 1)], 1)
    dz3, dz3b, dg_mix1, db_mix1 = _ln_bwd(dx3, xh3, r3, lnp("ln_mix_g", 1), "ln_mix_1_bwd")
    dos_t = _mm(dz3b, wo_s, "nt", "swa_out_dx", out_t=True)
    g[("swa_w_o", None)] = _mm(os_t, dz3b, "nn", "swa_out_dw")
    dqkv_t, dbias, dsk = _swa_bwd(qkv_t, dos_t, os_t, lse1, bias, sinks, qb)
    dwqkv = _mm(dqkv_t, x2b, "nn", "swa_qkv_dw").T
    g[("swa_w_q", None)], g[("kv_w_shared", None)] = dwqkv[:, :QH * HD], dwqkv[:, QH * HD:]
    dx2 = _mm(dqkv_t, wqkv_t, "tn", "swa_qkv_dx", addend=dz3, add_scale=ALPHA)
    g["rel_bias"] = jnp.sum(_bias_bwd(dbias, bkt), axis=-1).reshape(NBKT, QH)
    g["swa_sinks"] = jnp.sum(dsk, axis=-1).reshape(1, QH)
    dz2, dz2b, dg_mlp0, db_mlp0 = _ln_bwd(dx2, xh2, r2, lnp("ln_mlp_g", 0), "ln_mlp_0_bwd")
    dx1, g["mlp0"] = _mlp_bwd(dz2, dz2b, x1b, a0, w[("mlp_w_up", 0)], w[("mlp_w_down", 0)], 0)
    dz1, dz1b, dg_mix0, db_mix0 = _ln_bwd(dx1, xh1, r1, lnp("ln_mix_g", 0), "ln_mix_0_bwd")
    do0 = _mm(dz1b, w_o, "nt", "mla_out_dx", out_dtype=BF16)
    g[("mla_w_o", None)] = _mm(o0, dz1b, "tn", "mla_out_dw")
    do_lat, dwuv, delta_t = _o_up_bwd(do0, o_lat, wuv_h)
    g[("mla_w_uv", None)] = dwuv.transpose(1, 0, 2)
    kc_t = kc.reshape(t // bq, bq, KD).transpose(0, 2, 1)
    if dist is None:
        dk, ds_all = _flash_dkv(qcat, kc, do_lat, lse0_t, delta_t, bq, hb)
        dq_cat = _flash_dq(ds_all, kc_t, bq, hb)
    else:
        dk, ds_all, slots1 = _flash_dkv(qcat, kc, do_lat, lse0_t, delta_t, bq, hb, exchange=g["mlp1"])
        dq_cat, slots0 = _flash_dq(ds_all, kc_t, bq, hb, exchange=g["mlp0"])
        for key, slots in (("mlp1", slots1), ("mlp0", slots0)):
            reduced[key] = _join_core_halves(_sum_devices(slots, g[key], dist.pos, key), key)
    dq2, dwuk = _q_prep_bwd(dq_cat, q2, wuk_h, cos, sin)
    g[("mla_w_uk", None)] = dwuk.transpose(2, 0, 1)
    dcq = _mm(dq2, wq2, "nt", "mla_uq_dx")
    dwq2 = _mm(cq, dq2, "tn", "mla_uq_dw")
    g[("mla_w_uq", None)] = jnp.concatenate([dwq2[:, :H * NOPE].reshape(QR, H, NOPE),
                                             dwq2[:, H * NOPE:].reshape(QR, H, RP)[:, :, :ROPE]], axis=2)
    dh, dgq, dgkv = _mla_pre_bwd(hh, dcq, dk, w["mla_g_q"], w["mla_g_kv"], cos, sin)
    g[("mla_w_in", None)] = _mm(x, dh, "tn", "mla_in_dw")[:, :QR + KVR + ROPE]
    grad_x = _mm(dh, w_in, "nt", "mla_in_dx", addend=dz1, add_scale=ALPHA)
    g["mla_g_q"], g["mla_g_kv"] = dgq, dgkv
    g["ln_mix_g"] = jnp.concatenate([dg_mix0, dg_mix1], axis=0)
    g["ln_mix_b"] = jnp.concatenate([db_mix0, db_mix1], axis=0)
    g["ln_mlp_g"] = jnp.concatenate([dg_mlp0, dg_mlp1], axis=0)
    g["ln_mlp_b"] = jnp.concatenate([db_mlp0, db_mlp1], axis=0)
    return lpart, grad_x, g, reduced


def _rows(a):
    return a.reshape(-1, D)


def _piece(a, layer):
    return _rows(a if layer is None else a[layer])


def _pack_group(group, parts):
    return jnp.concatenate([_piece(parts[n], l) for n, l in group], axis=0)


def _unpack_group(group, buf, like):
    out, off = {}, 0
    for n, l in group:
        shp = like[n].shape if l is None else like[n].shape[1:]
        out[(n, l)] = buf[off:off + ROWS[n]].reshape(shp)
        off += ROWS[n]
    return out


def _by_name(pieces):
    out = {n: a for (n, l), a in pieces.items() if l is None}
    for n in {n for (n, l) in pieces if l is not None}:
        out[n] = jnp.stack([pieces[(n, 0)], pieces[(n, 1)]])
    return out


def _full_from_gathered(group, wall, shard_shapes):
    out, off = {}, 0
    for n, l in group:
        shp = tuple(shard_shapes[n])
        if n in ("mlp_w_up", "mlp_w_down"):
            out[(n, l)] = (wall, off)
        elif n == "kv_w_shared":
            out[(n, l)] = wall[:, off:off + ROWS[n]].reshape((4 * shp[0],) + shp[1:])
        else:
            out[(n, l)] = wall[:, off:off + ROWS[n]].reshape((4 * shp[1],) + shp[2:])
        off += ROWS[n]
    return out


def _grad_shards(group, g):
    return jnp.concatenate([g[(n, l)].reshape(4, ROWS[n], D) for n, l in group], axis=1)


def _rs_chip_partials(gsh, dist, tag):
    return _add_core_halves(gsh, _exchange_core_halves(gsh, tag), dist.cidx, tag)


def _rs_finish(part, slots, dist, tag):
    return _join_core_halves(_sum_slots(slots, part, dist.pos, tag), tag)


SMALL = (("ln_mix_g", 0, 2), ("ln_mix_b", 2, 2), ("ln_mlp_g", 4, 2), ("ln_mlp_b", 6, 2),
         ("swa_sinks", 8, 1), ("mla_g_q", 9, 1), ("mla_g_kv", 10, 1), ("rel_bias", 11, 1))


def _pack_small(parts):
    rows = []
    for n, _, nr in SMALL:
        a = parts[n].reshape(nr, -1).astype(F32)
        rows.append(jnp.pad(a, ((0, 0), (0, D - a.shape[1]))))
    rows.append(jnp.zeros((SMALL_ROWS - 12, D), F32))
    return jnp.concatenate(rows, axis=0)


def _unpack_small(buf, like):
    out = {}
    for n, r0, nr in SMALL:
        size = like[n].size // nr
        out[n] = buf[r0:r0 + nr, :size].reshape(like[n].shape)
    return out


def kernel(x, mla_w_in, mla_g_q, mla_g_kv, mla_w_uq, mla_w_uk, mla_w_uv, mla_w_o, kv_w_shared, swa_w_q, swa_sinks, swa_w_o, rel_bias, mlp_w_up, mlp_w_down, ln_mix_g, ln_mix_b, ln_mlp_g, ln_mlp_b, loss_target, m_mla_w_in, m_mla_g_q, m_mla_g_kv, m_mla_w_uq, m_mla_w_uk, m_mla_w_uv, m_mla_w_o, m_kv_w_shared, m_swa_w_q, m_swa_sinks, m_swa_w_o, m_rel_bias, m_mlp_w_up, m_mlp_w_down, m_ln_mix_g, m_ln_mix_b, m_ln_mlp_g, m_ln_mlp_b, v_mla_w_in, v_mla_g_q, v_mla_g_kv, v_mla_w_uq, v_mla_w_uk, v_mla_w_uv, v_mla_w_o, v_kv_w_shared, v_swa_w_q, v_swa_sinks, v_swa_w_o, v_rel_bias, v_mlp_w_up, v_mlp_w_down, v_ln_mix_g, v_ln_mix_b, v_ln_mlp_g, v_ln_mlp_b):
    names = ["mla_w_in", "mla_g_q", "mla_g_kv", "mla_w_uq", "mla_w_uk", "mla_w_uv", "mla_w_o", "kv_w_shared",
             "swa_w_q", "swa_sinks", "swa_w_o", "rel_bias", "mlp_w_up", "mlp_w_down",
             "ln_mix_g", "ln_mix_b", "ln_mlp_g", "ln_mlp_b"]
    ws = dict(zip(names, [mla_w_in, mla_g_q, mla_g_kv, mla_w_uq, mla_w_uk, mla_w_uv, mla_w_o, kv_w_shared,
                          swa_w_q, swa_sinks, swa_w_o, rel_bias, mlp_w_up, mlp_w_down,
                          ln_mix_g, ln_mix_b, ln_mlp_g, ln_mlp_b]))
    ms = dict(zip(names, [m_mla_w_in, m_mla_g_q, m_mla_g_kv, m_mla_w_uq, m_mla_w_uk, m_mla_w_uv, m_mla_w_o,
                          m_kv_w_shared, m_swa_w_q, m_swa_sinks, m_swa_w_o, m_rel_bias, m_mlp_w_up, m_mlp_w_down,
                          m_ln_mix_g, m_ln_mix_b, m_ln_mlp_g, m_ln_mlp_b]))
    vs = dict(zip(names, [v_mla_w_in, v_mla_g_q, v_mla_g_kv, v_mla_w_uq, v_mla_w_uk, v_mla_w_uv, v_mla_w_o,
                          v_kv_w_shared, v_swa_w_q, v_swa_sinks, v_swa_w_o, v_rel_bias, v_mlp_w_up, v_mlp_w_down,
                          v_ln_mix_g, v_ln_mix_b, v_ln_mlp_g, v_ln_mlp_b]))
    xi, yi, ci = _mesh_pos()
    shard = 2 * xi + yi
    shard_shapes = {n: ws[n].shape for n in ROWS}
    wbf = {n: ws[n].astype(BF16) for n in ROWS}

    early = _pack_group(AG_EARLY, wbf)
    wall = lax.dynamic_update_slice(_allgather_weights(early), early[None], (shard, 0, 0))
    w = _full_from_gathered(AG_EARLY, wall, shard_shapes)
    dist = _Dist(shard=shard, cidx=jnp.reshape(ci, (1,)).astype(jnp.int32),
                 pos=jnp.stack([shard, ci, 2 * shard + ci]).astype(jnp.int32), late_pack=_pack_group(AG_LATE, wbf),
                 shard_shapes=shard_shapes)
    gq_slot = lax.dynamic_update_slice(jnp.zeros((1, QR), F32), mla_g_q, (0, shard * (QR // 4)))
    gkv_slot = lax.dynamic_update_slice(jnp.zeros((1, KVR), F32), mla_g_kv, (0, shard * (KVR // 4)))
    gains = jnp.concatenate([jnp.pad(gq_slot, ((0, 0), (0, D - QR))), jnp.pad(gkv_slot, ((0, 0), (0, D - KVR))),
                             jnp.zeros((SMALL_ROWS - 2, D), F32)], axis=0)
    gains = _allreduce_small(gains * 0.5, "allgather_gains")
    w["mla_g_q"], w["mla_g_kv"] = gains[0, :QR], gains[1, :KVR]
    for n in ("swa_sinks", "rel_bias", "ln_mix_g", "ln_mix_b", "ln_mlp_g", "ln_mlp_b"):
        w[n] = ws[n]

    lpart, grad_x, g, reduced = _fwd_bwd(x[0], loss_target[0], w, dist)
    loss = lax.psum(0.5 * jnp.sum(lpart) / D, ("x", "y", "c"))

    part = _rs_chip_partials(_grad_shards(RS_REST, g), dist, "rest")
    reduced["rest"] = _rs_finish(part, _exchange_chip_shards(part, "rest"), dist, "rest")

    small_like = {n: g[n] for n, _, _ in SMALL}
    gsm = _unpack_small(_allreduce_small(_pack_small(g), "allreduce_small_grads"), small_like)
    gsm["mla_g_q"] = lax.dynamic_slice(gsm["mla_g_q"], (0, shard * (QR // 4)), (1, QR // 4))
    gsm["mla_g_kv"] = lax.dynamic_slice(gsm["mla_g_kv"], (0, shard * (KVR // 4)), (1, KVR // 4))

    gbig, dbig, mbig, vbig = {}, {}, {}, {}
    for n in ("mlp_w_up", "mlp_w_down"):
        off = 0 if n == "mlp_w_up" else ROWS["mlp_w_up"]
        gbig[n], dbig[n], mbig[n], vbig[n] = _adamw_layers(
            ws[n], ms[n], vs[n], reduced["mlp0"], reduced["mlp1"], off, f"adamw_{n}")
    outs = _adamw(_pack_group(RS_REST, ws), reduced["rest"], _pack_group(RS_REST, ms), _pack_group(RS_REST, vs),
                  "adamw_rest", tm=_row_tile(reduced["rest"].shape[0]))
    for dst, buf in zip((gbig, dbig, mbig, vbig), (reduced["rest"], *outs)):
        dst.update(_by_name(_unpack_group(RS_REST, buf, ws)))
    dsm, msm, vsm = _adamw(_pack_small(ws), _pack_small(gsm), _pack_small(ms), _pack_small(vs), "adamw_small", tm=16)
    grads = {**gbig, **gsm}
    delta = {**dbig, **_unpack_small(dsm, ws)}
    new_m = {**mbig, **_unpack_small(msm, ws)}
    new_v = {**vbig, **_unpack_small(vsm, ws)}
    grads = {n: grads[n].reshape(ws[n].shape) for n in names}
    return (loss, grad_x[None], *[grads[n] for n in names], *[delta[n] for n in names],
            *[new_m[n] for n in names], *[new_v[n] for n in names])
```

```python
import collections
import math

import numpy as np
import jax
import jax.numpy as jnp
from jax import lax
from jax.experimental import pallas as pl
from jax.experimental.pallas import tpu as pltpu

F32 = jnp.float32
BF16 = jnp.bfloat16
MESH = pl.DeviceIdType.MESH

D = 1024
DFF = 4096
H = 8
NOPE = 128
ROPE = 64
QR = 384
KVR = 256
RP = 128
KD = KVR + RP
HW = 768
QH = 16
KVH = 4
HD = 64
G = QH // KVH
WIN = 128
NBKT = 32
ALPHA = 4.0 ** 0.25
LN_EPS = 1e-5
RMS_EPS = 1e-6
MLA_SCALE = (NOPE + ROPE) ** -0.5
LOG2E = 1.4426950408889634
LN2 = 0.6931471805599453
QSCALE = MLA_SCALE * LOG2E
SWA_SCALE = HD ** -0.5
NEG = -1e30
LR, B1, B2, ADAM_EPS, WD, STEP = 0.001, 0.9, 0.999, 1e-8, 0.01, 10

VMEM_LIMIT = 48 * 1024 * 1024

NN = (((1,), (0,)), ((), ()))
NT = (((1,), (1,)), ((), ()))
TN = (((0,), (0,)), ((), ()))

ROWS = {"mlp_w_up": 1024, "mlp_w_down": 1024, "mla_w_o": 256, "swa_w_q": 256, "swa_w_o": 256,
        "kv_w_shared": 128, "mla_w_in": 176, "mla_w_uq": 144, "mla_w_uk": 64, "mla_w_uv": 64}
AG_EARLY = (("mla_w_in", None), ("mla_w_uq", None), ("mla_w_uk", None), ("mla_w_uv", None), ("mla_w_o", None))
AG_LATE = (("mlp_w_up", 0), ("mlp_w_up", 1), ("mlp_w_down", 0), ("mlp_w_down", 1),
           ("swa_w_q", None), ("swa_w_o", None), ("kv_w_shared", None))
RS_MID = (("mla_w_o", None), ("swa_w_q", None), ("swa_w_o", None), ("kv_w_shared", None), ("mla_w_uv", None))
RS_END = (("mla_w_in", None), ("mla_w_uq", None), ("mla_w_uk", None))
SMALL_ROWS = 16
_Dist = collections.namedtuple("_Dist", "shard pos late_pack shard_shapes")


def _cp(**kw):
    return pltpu.CompilerParams(vmem_limit_bytes=VMEM_LIMIT, **kw)


def _tile(n, pref):
    t = min(n, pref)
    while n % t:
        t -= 128
    return t


def _dot(a, b, dims):
    return lax.dot_general(a, b, dims, preferred_element_type=F32)


def _mm(a, b, mode, name, out_dtype=F32, out_t=False, addend=None, add_scale=1.0, relu2=False, gate_a=None,
        b_view=None, out_view=None, tm=1024, tn=1024, tk=1024):
    blk = 1024
    if b_view is not None:
        kind, b_off = b_view
        assert b.shape[0] == 4 and b.shape[2] == blk and b_off % blk == 0
        bshape = {("cols", "nn"): (blk, 4 * blk), ("cols", "nt"): (blk, 4 * blk),
                  ("rows", "nn"): (4 * blk, blk), ("rows", "nt"): (4 * blk, blk)}[(kind, mode)]
    else:
        bshape = b.shape
    if mode == "nn":
        (m, k), (k2, n) = a.shape, bshape
    elif mode == "nt":
        (m, k), (n, k2) = a.shape, bshape
    else:
        (k, m), (k2, n) = a.shape, bshape
    assert k == k2, (name, a.shape, b.shape)
    tm, tn, tk = _tile(m, tm), _tile(n, tn), _tile(k, tk)
    nk = k // tk
    dims = {"nn": NN, "nt": NT, "tn": TN}[mode]
    if mode == "tn":
        a_spec = pl.BlockSpec((tk, tm), lambda i, j, kk: (kk, i))
    else:
        a_spec = pl.BlockSpec((tm, tk), lambda i, j, kk: (i, kk))
    if b_view is not None:
        assert tn == blk and tk == blk
        ob = b_off // blk
        b_spec = {("cols", "nn"): pl.BlockSpec((None, tk, tn), lambda i, j, kk: (j, ob, 0)),
                  ("cols", "nt"): pl.BlockSpec((None, tn, tk), lambda i, j, kk: (kk, ob, 0)),
                  ("rows", "nn"): pl.BlockSpec((None, tk, tn), lambda i, j, kk: (kk, ob, 0)),
                  ("rows", "nt"): pl.BlockSpec((None, tn, tk), lambda i, j, kk: (j, ob, 0))}[(kind, mode)]
    elif mode == "nt":
        b_spec = pl.BlockSpec((tn, tk), lambda i, j, kk: (j, kk))
    else:
        b_spec = pl.BlockSpec((tk, tn), lambda i, j, kk: (kk, j))
    mn_spec = pl.BlockSpec((tm, tn), lambda i, j, kk: (i, j))
    ins, in_specs = [a, b], [a_spec, b_spec]
    if addend is not None:
        ins.append(addend)
        in_specs.append(mn_spec)
    if gate_a is not None:
        ins.append(gate_a)
        in_specs.append(mn_spec)
    aliases = {}
    if out_view is not None:
        okind, total_rows, o_off, buf = out_view
        assert not out_t and tm == blk and tn == blk and o_off % blk == 0
        oo = o_off // blk
        out_shape = [jax.ShapeDtypeStruct((4, total_rows, blk), out_dtype)]
        if okind == "cols":
            out_specs = [pl.BlockSpec((None, tm, tn), lambda i, j, kk: (j, oo, 0))]
        else:
            out_specs = [pl.BlockSpec((None, tm, tn), lambda i, j, kk: (i, oo, 0))]
        if buf is not None:
            aliases = {len(ins): 0}
            ins.append(buf)
            in_specs.append(pl.BlockSpec(memory_space=pl.ANY))
    elif out_t:
        out_shape = [jax.ShapeDtypeStruct((n, m), out_dtype)]
        out_specs = [pl.BlockSpec((tn, tm), lambda i, j, kk: (j, i))]
    else:
        out_shape = [jax.ShapeDtypeStruct((m, n), out_dtype)]
        out_specs = [mn_spec]
    has_add, has_gate = addend is not None, gate_a is not None

    def kern(*refs):
        a_ref, b_ref = refs[0], refs[1]
        pos = 2
        add_ref = gate_ref = None
        if has_add:
            add_ref = refs[pos]
            pos += 1
        if has_gate:
            gate_ref = refs[pos]
            pos += 1
        o_ref = refs[pos + len(aliases)]
        acc = refs[-1] if nk > 1 else None
        kk = pl.program_id(2)

        def partial():
            return _dot(a_ref[...].astype(BF16), b_ref[...].astype(BF16), dims)

        if nk > 1:
            @pl.when(kk == 0)
            def _():
                acc[...] = partial()

            @pl.when((kk > 0) & (kk < nk - 1))
            def _():
                acc[...] += partial()

        @pl.when(kk == nk - 1)
        def _():
            r = partial() + acc[...] if nk > 1 else partial()
            if has_add:
                r = r + add_scale * add_ref[...].astype(F32)
            if has_gate:
                r = r * (2.0 * jnp.sqrt(gate_ref[...].astype(F32)))
            if relu2:
                hh = jnp.maximum(r, 0.0)
                r = hh * hh
            if out_t:
                r = r.T
            o_ref[...] = r.astype(out_dtype)

    return pl.pallas_call(
        kern, out_shape=out_shape, grid=(m // tm, n // tn, nk), in_specs=in_specs, out_specs=out_specs,
        scratch_shapes=[pltpu.VMEM((tm, tn), F32)] if nk > 1 else [], input_output_aliases=aliases,
        name=name, compiler_params=_cp())(*ins)[0]


def _add_ln(res, y, g, b, name, res_affine=None, tm=512):
    t = res.shape[0]
    tm = min(tm, t)
    affine = res_affine is not None

    def kern(*refs):
        if affine:
            x_ref, y_ref, g_ref, b_ref, g0_ref, b0_ref, ob_ref, xh_ref, r_ref = refs
            x = x_ref[...] * g0_ref[...] + b0_ref[...]
        else:
            x_ref, y_ref, g_ref, b_ref, ob_ref, xh_ref, r_ref = refs
            x = x_ref[...]
        z = ALPHA * x + y_ref[...]
        mu = jnp.mean(z, axis=-1, keepdims=True)
        zc = z - mu
        var = jnp.mean(zc * zc, axis=-1, keepdims=True)
        r = lax.rsqrt(var + LN_EPS)
        xh = zc * r
        ob_ref[...] = (xh * g_ref[...] + b_ref[...]).astype(BF16)
        xh_ref[...] = xh
        r_ref[...] = r

    row = pl.BlockSpec((tm, D), lambda i: (i, 0))
    vec = pl.BlockSpec((1, D), lambda i: (0, 0))
    st = pl.BlockSpec((tm, 1), lambda i: (i, 0))
    ins = [res, y, g.reshape(1, D), b.reshape(1, D)]
    if affine:
        ins += [res_affine[0].reshape(1, D), res_affine[1].reshape(1, D)]
    return pl.pallas_call(
        kern, grid=(t // tm,), in_specs=[row, row] + [vec] * (len(ins) - 2), out_specs=[row, row, st],
        out_shape=[jax.ShapeDtypeStruct((t, D), BF16), jax.ShapeDtypeStruct((t, D), F32),
                   jax.ShapeDtypeStruct((t, 1), F32)],
        name=name, compiler_params=_cp())(*ins)


def _ln_bwd(dout, xhat, rstd, g, name, loss_b=None, tm=512):
    t = dout.shape[0]
    tm = min(tm, t)
    head = loss_b is not None

    def kern(*refs):
        if head:
            do_ref, xh_ref, r_ref, g_ref, b_ref, dz_ref, dzb_ref, dg_ref, db_ref, l_ref = refs
        else:
            do_ref, xh_ref, r_ref, g_ref, dz_ref, dzb_ref, dg_ref, db_ref = refs

        @pl.when(pl.program_id(0) == 0)
        def _():
            dg_ref[...] = jnp.zeros_like(dg_ref)
            db_ref[...] = jnp.zeros_like(db_ref)
            if head:
                l_ref[...] = jnp.zeros_like(l_ref)

        xh = xh_ref[...]
        if head:
            e = xh * g_ref[...] + b_ref[...] - do_ref[...]
            l_ref[...] += jnp.sum(e * e, axis=0, keepdims=True)
            do = e * (1.0 / D)
        else:
            do = do_ref[...]
        dxh = do * g_ref[...]
        m1 = jnp.mean(dxh, axis=-1, keepdims=True)
        m2 = jnp.mean(dxh * xh, axis=-1, keepdims=True)
        dz = r_ref[...] * (dxh - m1 - xh * m2)
        dz_ref[...] = dz
        dzb_ref[...] = dz.astype(BF16)
        dg_ref[...] += jnp.sum(do * xh, axis=0, keepdims=True)
        db_ref[...] += jnp.sum(do, axis=0, keepdims=True)

    row = pl.BlockSpec((tm, D), lambda i: (i, 0))
    vec = pl.BlockSpec((1, D), lambda i: (0, 0))
    st = pl.BlockSpec((tm, 1), lambda i: (i, 0))
    ins = [dout, xhat, rstd, g.reshape(1, D)] + ([loss_b.reshape(1, D)] if head else [])
    return pl.pallas_call(
        kern, grid=(t // tm,), in_specs=[row, row, st] + [vec] * (len(ins) - 3),
        out_specs=[row, row, vec, vec] + ([vec] if head else []),
        out_shape=[jax.ShapeDtypeStruct((t, D), F32), jax.ShapeDtypeStruct((t, D), BF16)]
        + [jax.ShapeDtypeStruct((1, D), F32)] * (3 if head else 2),
        name=name, compiler_params=_cp())(*ins)


def _rope_tables(t):
    half = ROPE // 2
    inv = 10000.0 ** (-jnp.arange(half, dtype=F32) / half)
    ang = jnp.arange(t).astype(F32)[:, None] * inv[None, :]
    cos, sin = jnp.cos(ang), jnp.sin(ang)
    z = jnp.zeros((t, RP - ROPE), F32)
    return jnp.concatenate([cos, cos, z], axis=1), jnp.concatenate([-sin, sin, z], axis=1)


def _swap_halves(x):
    lane = lax.broadcasted_iota(jnp.int32, x.shape, 1)
    return jnp.where(lane < ROPE // 2, pltpu.roll(x, RP - ROPE // 2, 1), pltpu.roll(x, ROPE // 2, 1))


def _rope(x, cos, sin):
    return x * cos + _swap_halves(x) * sin


def _rope_t(gy, cos, sin):
    return gy * cos + _swap_halves(gy * sin)


def _mla_pre(hh, g_q, g_kv, cos, sin, tm=512):
    t = hh.shape[0]
    tm = min(tm, t)

    def kern(h_ref, gq_ref, gkv_ref, c_ref, s_ref, cq_ref, k_ref):
        xq = h_ref[:, 0:QR]
        rq = lax.rsqrt(jnp.mean(xq * xq, axis=-1, keepdims=True) + RMS_EPS)
        cq_ref[...] = (xq * rq * gq_ref[...]).astype(BF16)
        xk = h_ref[:, QR:QR + KVR]
        rk = lax.rsqrt(jnp.mean(xk * xk, axis=-1, keepdims=True) + RMS_EPS)
        k_ref[:, 0:KVR] = (xk * rk * gkv_ref[...]).astype(BF16)
        k_ref[:, KVR:KD] = _rope(h_ref[:, QR + KVR:HW], c_ref[...], s_ref[...]).astype(BF16)

    return pl.pallas_call(
        kern, grid=(t // tm,),
        in_specs=[pl.BlockSpec((tm, HW), lambda i: (i, 0)), pl.BlockSpec((1, QR), lambda i: (0, 0)),
                  pl.BlockSpec((1, KVR), lambda i: (0, 0)), pl.BlockSpec((tm, RP), lambda i: (i, 0)),
                  pl.BlockSpec((tm, RP), lambda i: (i, 0))],
        out_specs=[pl.BlockSpec((tm, QR), lambda i: (i, 0)), pl.BlockSpec((tm, KD), lambda i: (i, 0))],
        out_shape=[jax.ShapeDtypeStruct((t, QR), BF16), jax.ShapeDtypeStruct((t, KD), BF16)],
        name="mla_pre", compiler_params=_cp())(hh, g_q.reshape(1, QR), g_kv.reshape(1, KVR), cos, sin)


def _mla_pre_bwd(hh, dcq, dk, g_q, g_kv, cos, sin, tm=512):
    t = hh.shape[0]
    tm = min(tm, t)

    def rms_bwd(x, dy, g):
        r = lax.rsqrt(jnp.mean(x * x, axis=-1, keepdims=True) + RMS_EPS)
        gdy = dy * g
        dx = r * gdy - x * (r * r * r) * jnp.mean(gdy * x, axis=-1, keepdims=True)
        return dx, jnp.sum(dy * x * r, axis=0, keepdims=True)

    def kern(h_ref, dcq_ref, dk_ref, gq_ref, gkv_ref, c_ref, s_ref, dh_ref, dgq_ref, dgkv_ref):
        @pl.when(pl.program_id(0) == 0)
        def _():
            dgq_ref[...] = jnp.zeros_like(dgq_ref)
            dgkv_ref[...] = jnp.zeros_like(dgkv_ref)

        dxq, dgq = rms_bwd(h_ref[:, 0:QR], dcq_ref[...], gq_ref[...])
        dxk, dgk = rms_bwd(h_ref[:, QR:QR + KVR], dk_ref[:, 0:KVR], gkv_ref[...])
        dh_ref[:, 0:QR] = dxq.astype(BF16)
        dh_ref[:, QR:QR + KVR] = dxk.astype(BF16)
        dh_ref[:, QR + KVR:HW] = _rope_t(dk_ref[:, KVR:KD], c_ref[...], s_ref[...]).astype(BF16)
        dgq_ref[...] += dgq
        dgkv_ref[...] += dgk

    return pl.pallas_call(
        kern, grid=(t // tm,),
        in_specs=[pl.BlockSpec((tm, HW), lambda i: (i, 0)), pl.BlockSpec((tm, QR), lambda i: (i, 0)),
                  pl.BlockSpec((tm, KD), lambda i: (i, 0)), pl.BlockSpec((1, QR), lambda i: (0, 0)),
                  pl.BlockSpec((1, KVR), lambda i: (0, 0)), pl.BlockSpec((tm, RP), lambda i: (i, 0)),
                  pl.BlockSpec((tm, RP), lambda i: (i, 0))],
        out_specs=[pl.BlockSpec((tm, HW), lambda i: (i, 0)), pl.BlockSpec((1, QR), lambda i: (0, 0)),
                   pl.BlockSpec((1, KVR), lambda i: (0, 0))],
        out_shape=[jax.ShapeDtypeStruct((t, HW), BF16), jax.ShapeDtypeStruct((1, QR), F32),
                   jax.ShapeDtypeStruct((1, KVR), F32)],
        name="mla_pre_bwd", compiler_params=_cp())(hh, dcq, dk, g_q.reshape(1, QR), g_kv.reshape(1, KVR), cos, sin)


def _q_prep(q2, wuk_t, cos, sin, tm=512):
    t = q2.shape[0]
    tm = min(tm, t)

    def kern(q_ref, w_ref, c_ref, s_ref, o_ref):
        cos_, sin_ = c_ref[...], s_ref[...]
        for h in range(H):
            qn = q_ref[:, h * NOPE:(h + 1) * NOPE].astype(BF16)
            o_ref[:, h * KD:h * KD + KVR] = (_dot(qn, w_ref[h], NN) * QSCALE).astype(BF16)
            qr = q_ref[:, H * NOPE + h * RP:H * NOPE + (h + 1) * RP]
            o_ref[:, h * KD + KVR:(h + 1) * KD] = (_rope(qr, cos_, sin_) * QSCALE).astype(BF16)

    return pl.pallas_call(
        kern, grid=(t // tm,),
        in_specs=[pl.BlockSpec((tm, 2 * H * NOPE), lambda i: (i, 0)), pl.BlockSpec((H, NOPE, KVR), lambda i: (0, 0, 0)),
                  pl.BlockSpec((tm, RP), lambda i: (i, 0)), pl.BlockSpec((tm, RP), lambda i: (i, 0))],
        out_specs=pl.BlockSpec((tm, H * KD), lambda i: (i, 0)),
        out_shape=jax.ShapeDtypeStruct((t, H * KD), BF16),
        name="q_prep", compiler_params=_cp())(q2, wuk_t, cos, sin)


def _q_prep_bwd(dq_cat, q2, wuk_h, cos, sin, tm=512):
    t = q2.shape[0]
    tm = min(tm, t)

    def kern(dq_ref, q_ref, w_ref, c_ref, s_ref, o_ref, dw_ref):
        @pl.when(pl.program_id(0) == 0)
        def _():
            dw_ref[...] = jnp.zeros_like(dw_ref)

        cos_, sin_ = c_ref[...], s_ref[...]
        for h in range(H):
            dql = dq_ref[:, h * KD:h * KD + KVR].astype(BF16)
            o_ref[:, h * NOPE:(h + 1) * NOPE] = _dot(dql, w_ref[h], NN).astype(BF16)
            dqr = dq_ref[:, h * KD + KVR:(h + 1) * KD]
            o_ref[:, H * NOPE + h * RP:H * NOPE + (h + 1) * RP] = _rope_t(dqr, cos_, sin_).astype(BF16)
            qn = q_ref[:, h * NOPE:(h + 1) * NOPE].astype(BF16)
            dw_ref[h] += _dot(qn, dql, TN)

    return pl.pallas_call(
        kern, grid=(t // tm,),
        in_specs=[pl.BlockSpec((tm, H * KD), lambda i: (i, 0)), pl.BlockSpec((tm, 2 * H * NOPE), lambda i: (i, 0)),
                  pl.BlockSpec((H, KVR, NOPE), lambda i: (0, 0, 0)),
                  pl.BlockSpec((tm, RP), lambda i: (i, 0)), pl.BlockSpec((tm, RP), lambda i: (i, 0))],
        out_specs=[pl.BlockSpec((tm, 2 * H * NOPE), lambda i: (i, 0)), pl.BlockSpec((H, NOPE, KVR), lambda i: (0, 0, 0))],
        out_shape=[jax.ShapeDtypeStruct((t, 2 * H * NOPE), BF16), jax.ShapeDtypeStruct((H, NOPE, KVR), F32)],
        name="q_prep_bwd", compiler_params=_cp())(dq_cat, q2, wuk_h, cos, sin)


def _o_up(o_lat, wuv_h, tm=512):
    t = o_lat.shape[0]
    tm = min(tm, t)

    def kern(x_ref, w_ref, o_ref):
        for h in range(H):
            xl = x_ref[:, h * KVR:(h + 1) * KVR].astype(BF16)
            o_ref[:, h * NOPE:(h + 1) * NOPE] = _dot(xl, w_ref[h], NN).astype(BF16)

    return pl.pallas_call(
        kern, grid=(t // tm,),
        in_specs=[pl.BlockSpec((tm, H * KVR), lambda i: (i, 0)), pl.BlockSpec((H, KVR, NOPE), lambda i: (0, 0, 0))],
        out_specs=pl.BlockSpec((tm, H * NOPE), lambda i: (i, 0)),
        out_shape=jax.ShapeDtypeStruct((t, H * NOPE), BF16),
        name="o_up", compiler_params=_cp())(o_lat, wuv_h)


def _o_up_bwd(do, o_lat, wuv_h, tm=512):
    t = do.shape[0]
    tm = min(tm, t)

    def kern(do_ref, x_ref, w_ref, dx_ref, dw_ref, dlt_ref):
        @pl.when(pl.program_id(0) == 0)
        def _():
            dw_ref[...] = jnp.zeros_like(dw_ref)

        for h in range(H):
            dh_ = do_ref[:, h * NOPE:(h + 1) * NOPE]
            x = x_ref[:, h * KVR:(h + 1) * KVR]
            dx = _dot(dh_, w_ref[h], NT)
            dx_ref[:, h * KVR:(h + 1) * KVR] = dx.astype(BF16)
            dw_ref[h] += _dot(x.astype(BF16), dh_, TN)
            dl = jnp.broadcast_to(jnp.sum(dx * x, axis=1)[:, None], (tm, 128))
            dlt_ref[h] = dl.T[0:1, :]

    return pl.pallas_call(
        kern, grid=(t // tm,),
        in_specs=[pl.BlockSpec((tm, H * NOPE), lambda i: (i, 0)), pl.BlockSpec((tm, H * KVR), lambda i: (i, 0)),
                  pl.BlockSpec((H, KVR, NOPE), lambda i: (0, 0, 0))],
        out_specs=[pl.BlockSpec((tm, H * KVR), lambda i: (i, 0)), pl.BlockSpec((H, KVR, NOPE), lambda i: (0, 0, 0)),
                   pl.BlockSpec((H, 1, tm), lambda i: (0, 0, i))],
        out_shape=[jax.ShapeDtypeStruct((t, H * KVR), BF16), jax.ShapeDtypeStruct((H, KVR, NOPE), F32),
                   jax.ShapeDtypeStruct((H, 1, t), F32)],
        name="o_up_bwd", compiler_params=_cp())(do, o_lat, wuv_h)


def _causal_pairs(nq):
    return [(i, j) for i in range(nq) for j in range(i + 1)]


def _lane_tile(stat, width):
    return jnp.tile(stat, (1, width // 128))


def _flash_fwd(qcat, kc, bq, hb, gather=None):
    t = kc.shape[0]
    nq = t // bq
    pairs = _causal_pairs(nq)
    itab = jnp.asarray(np.array([p[0] for p in pairs], np.int32))
    jtab = jnp.asarray(np.array([p[1] for p in pairs], np.int32))

    ng = H // hb
    hosting = gather is not None

    def kern(it, jt, q_ref, k_ref, *rest):
        if hosting:
            w_ref, o_ref, lset_ref, wall_ref, m_sc, l_sc, acc_sc, send_sems, recv_sems = rest
            ag_start, ag_forward, ag_finish = _allgather_schedule(w_ref, wall_ref, send_sems, recv_sems)
        else:
            o_ref, lset_ref, m_sc, l_sc, acc_sc = rest
        grp = pl.program_id(0)
        st = pl.program_id(1)
        i, j = it[st], jt[st]

        if hosting:
            @pl.when((grp == 0) & (st == 0))
            def _():
                ag_start()

        @pl.when(j == 0)
        def _():
            m_sc[...] = jnp.full_like(m_sc, NEG)
            l_sc[...] = jnp.zeros_like(l_sc)
            acc_sc[...] = jnp.zeros_like(acc_sc)

        def update(masked):
            k = k_ref[...]
            v = k[:, 0:KVR]
            if masked:
                row = lax.broadcasted_iota(jnp.int32, (bq, bq), 0)
                col = lax.broadcasted_iota(jnp.int32, (bq, bq), 1)
                keep = col <= row
            s_next = _dot(q_ref[:, 0:KD], k, NT)
            for hh in range(hb):
                s = s_next
                if hh + 1 < hb:
                    s_next = _dot(q_ref[:, (hh + 1) * KD:(hh + 2) * KD], k, NT)
                if masked:
                    s = jnp.where(keep, s, NEG)
                m_prev = m_sc[hh]
                m_next = jnp.maximum(m_prev, jnp.max(s, axis=1)[:, None])
                p = jnp.exp2(s - _lane_tile(m_next, bq))
                a = jnp.exp2(m_prev - m_next)
                l_sc[hh] = a * l_sc[hh] + jnp.sum(p, axis=1)[:, None]
                acc_sc[hh] = _lane_tile(a, KVR) * acc_sc[hh] + _dot(p.astype(BF16), v, NN)
                m_sc[hh] = m_next

        @pl.when(j < i)
        def _():
            update(False)

        @pl.when(j == i)
        def _():
            update(True)
            for hh in range(hb):
                l = l_sc[hh]
                o_ref[:, hh * KVR:(hh + 1) * KVR] = acc_sc[hh] / _lane_tile(l, KVR)
                lset_ref[hh] = (m_sc[hh] + jnp.log2(l)).T[0:1, :]

        if hosting:
            @pl.when((grp == ng - 1) & (st == 0))
            def _():
                ag_forward()

            @pl.when((grp == ng - 1) & (st == len(pairs) - 1))
            def _():
                ag_finish()

    in_specs = [pl.BlockSpec((bq, hb * KD), lambda g, s, it, jt: (it[s], g)),
                pl.BlockSpec((bq, KD), lambda g, s, it, jt: (jt[s], 0))]
    out_specs = [pl.BlockSpec((bq, hb * KVR), lambda g, s, it, jt: (it[s], g)),
                 pl.BlockSpec((hb, 1, bq), lambda g, s, it, jt: (g, 0, it[s]))]
    out_shape = [jax.ShapeDtypeStruct((t, H * KVR), F32), jax.ShapeDtypeStruct((H, 1, t), F32)]
    scratch = [pltpu.VMEM((hb, bq, 128), F32), pltpu.VMEM((hb, bq, 128), F32), pltpu.VMEM((hb, bq, KVR), F32)]
    args = [itab, jtab, qcat, kc]
    if hosting:
        in_specs.append(ANY)
        out_specs.append(ANY)
        out_shape.append(jax.ShapeDtypeStruct((4,) + gather.shape, gather.dtype))
        scratch += AG_SEMS
        args.append(gather)
    gs = pltpu.PrefetchScalarGridSpec(num_scalar_prefetch=2, grid=(ng, len(pairs)), in_specs=in_specs,
                                      out_specs=out_specs, scratch_shapes=scratch)
    return pl.pallas_call(kern, grid_spec=gs, out_shape=out_shape, name="mla_flash_fwd",
                          compiler_params=_cp())(*args)


def _flash_dkv(qcat, kc, do_lat, lse_t, delta_t, bq, hb, exchange=()):
    nx = len(exchange)
    t = kc.shape[0]
    nq = t // bq
    ng = H // hb
    npairs = nq * (nq + 1) // 2
    steps = [(j, g, i) for j in range(nq) for g in range(ng) for i in range(j, nq)]
    jtab = jnp.asarray(np.array([s[0] for s in steps], np.int32))
    gtab = jnp.asarray(np.array([s[1] for s in steps], np.int32))
    itab = jnp.asarray(np.array([s[2] for s in steps], np.int32))
    ptab = jnp.asarray(np.array([s[2] * (s[2] + 1) // 2 + s[0] for s in steps], np.int32))

    def kern(jt, gt, it, pt, q_ref, k_ref, do_ref, lset_ref, dlt_ref, *rest):
        p_refs, (dk_ref, ds_ref), slots_refs = rest[:nx], rest[nx:nx + 2], rest[nx + 2:2 * nx + 2]
        dk_sc, dv_sc = rest[2 * nx + 2:2 * nx + 4]
        sems = rest[2 * nx + 4:]
        hooks = [_device_exchange_schedule(p_refs[e], slots_refs[e], sems[2 * e], sems[2 * e + 1]) for e in range(nx)]
        st = pl.program_id(0)
        j, g, i = jt[st], gt[st], it[st]

        if nx:
            @pl.when(st == 0)
            def _():
                for start, _ in hooks:
                    start()

        @pl.when((g == 0) & (i == j))
        def _():
            dk_sc[...] = jnp.zeros_like(dk_sc)
            dv_sc[...] = jnp.zeros_like(dv_sc)

        def update(masked):
            k = k_ref[...]
            v = k[:, 0:KVR]
            if masked:
                row = lax.broadcasted_iota(jnp.int32, (bq, bq), 0)
                col = lax.broadcasted_iota(jnp.int32, (bq, bq), 1)
                keep = row <= col

            def first_matmuls(hh):
                dob = do_ref[:, hh * KVR:(hh + 1) * KVR].astype(BF16)
                return _dot(k, q_ref[:, hh * KD:(hh + 1) * KD], NT), _dot(v, dob, NT), dob

            nxt = first_matmuls(0)
            for hh in range(hb):
                s, dp, dob = nxt
                if hh + 1 < hb:
                    nxt = first_matmuls(hh + 1)
                if masked:
                    s = jnp.where(keep, s, NEG)
                p = jnp.exp2(s - lset_ref[hh])
                dv_sc[...] += _dot(p.astype(BF16), dob, NN)
                dsb = (p * (dp - dlt_ref[hh])).astype(BF16)
                ds_ref[0, 0, hh] = dsb
                dk_sc[...] += _dot(dsb, q_ref[:, hh * KD:(hh + 1) * KD], NN)

        @pl.when(i > j)
        def _():
            update(False)

        @pl.when(i == j)
        def _():
            update(True)

        @pl.when((g == ng - 1) & (i == nq - 1))
        def _():
            dk_ref[:, 0:KVR] = dk_sc[:, 0:KVR] * LN2 + dv_sc[...]
            dk_ref[:, KVR:KD] = dk_sc[:, KVR:KD] * LN2

        if nx:
            @pl.when(st == len(steps) - 1)
            def _():
                for _, finish in hooks:
                    finish()

    in_specs = [pl.BlockSpec((bq, hb * KD), lambda s, jt, gt, it, pt: (it[s], gt[s])),
                pl.BlockSpec((bq, KD), lambda s, jt, gt, it, pt: (jt[s], 0)),
                pl.BlockSpec((bq, hb * KVR), lambda s, jt, gt, it, pt: (it[s], gt[s])),
                pl.BlockSpec((hb, 1, bq), lambda s, jt, gt, it, pt: (gt[s], 0, it[s])),
                pl.BlockSpec((hb, 1, bq), lambda s, jt, gt, it, pt: (gt[s], 0, it[s]))] + [ANY] * nx
    out_specs = [pl.BlockSpec((bq, KD), lambda s, jt, gt, it, pt: (jt[s], 0)),
                 pl.BlockSpec((1, 1, hb, bq, bq), lambda s, jt, gt, it, pt: (gt[s], pt[s], 0, 0, 0))] + [ANY] * nx
    out_shape = [jax.ShapeDtypeStruct((t, KD), F32), jax.ShapeDtypeStruct((ng, npairs, hb, bq, bq), BF16)]
    out_shape += [jax.ShapeDtypeStruct((8, e.shape[1] // 2, D), e.dtype) for e in exchange]
    scratch = [pltpu.VMEM((bq, KD), F32), pltpu.VMEM((bq, KVR), F32)] + ALL_SEMS * nx
    args = [jtab, gtab, itab, ptab, qcat, kc, do_lat, lse_t, delta_t, *exchange]
    gs = pltpu.PrefetchScalarGridSpec(num_scalar_prefetch=4, grid=(len(steps),), in_specs=in_specs,
                                      out_specs=out_specs, scratch_shapes=scratch)
    return pl.pallas_call(kern, grid_spec=gs, out_shape=out_shape, name="mla_flash_dkv",
                          compiler_params=_cp())(*args)


def _flash_dq(ds_all, kc_t, bq, hb, exchange=None):
    nq = kc_t.shape[0]
    t = nq * bq
    ng = H // hb
    pairs = _causal_pairs(nq)
    itab = jnp.asarray(np.array([p[0] for p in pairs], np.int32))
    jtab = jnp.asarray(np.array([p[1] for p in pairs], np.int32))
    hosting = exchange is not None

    hh2 = hb // 2

    def kern(it, jt, dsa_ref, dsb_ref, kt_ref, *rest):
        if hosting:
            p_ref, dq_ref, slots_ref, acc_sc, send_sems, recv_sems = rest
            xc_start, xc_finish = _device_exchange_schedule(p_ref, slots_ref, send_sems, recv_sems)
        else:
            dq_ref, acc_sc = rest
        grp = pl.program_id(0)
        st = pl.program_id(1)
        i, j = it[st], jt[st]
        kt = kt_ref[...]

        def ds(hh):
            return dsa_ref[0, 0, hh] if hh < hh2 else dsb_ref[0, 0, hh - hh2]

        if hosting:
            @pl.when((grp == 0) & (st == 0))
            def _():
                xc_start()

        @pl.when(j == 0)
        def _():
            for hh in range(hb):
                acc_sc[hh] = _dot(kt, ds(hh), NN)

        @pl.when((j > 0) & (j < i))
        def _():
            for hh in range(hb):
                acc_sc[hh] += _dot(kt, ds(hh), NN)

        @pl.when(j == i)
        def _():
            for hh in range(hb):
                tot = _dot(kt, ds(hh), NN)
                tot = jnp.where(i > 0, tot + acc_sc[hh], tot)
                dq_ref[:, hh * KD:(hh + 1) * KD] = tot.T * MLA_SCALE

        if hosting:
            @pl.when((grp == ng - 1) & (st == len(pairs) - 1))
            def _():
                xc_finish()

    in_specs = [pl.BlockSpec((1, 1, hh2, bq, bq), lambda g, s, it, jt: (g, s, 0, 0, 0)),
                pl.BlockSpec((1, 1, hh2, bq, bq), lambda g, s, it, jt: (g, s, 1, 0, 0)),
                pl.BlockSpec((None, KD, bq), lambda g, s, it, jt: (jt[s], 0, 0))]
    out_specs = [pl.BlockSpec((bq, hb * KD), lambda g, s, it, jt: (it[s], g))]
    out_shape = [jax.ShapeDtypeStruct((t, H * KD), F32)]
    scratch = [pltpu.VMEM((hb, KD, bq), F32)]
    args = [itab, jtab, ds_all, ds_all, kc_t]
    if hosting:
        in_specs.append(ANY)
        out_specs.append(ANY)
        out_shape.append(jax.ShapeDtypeStruct((8, exchange.shape[1] // 2, D), exchange.dtype))
        scratch += ALL_SEMS
        args.append(exchange)
    gs = pltpu.PrefetchScalarGridSpec(num_scalar_prefetch=2, grid=(ng, len(pairs)), in_specs=in_specs,
                                      out_specs=out_specs, scratch_shapes=scratch)
    outs = pl.pallas_call(kern, grid_spec=gs, out_shape=out_shape, name="mla_flash_dq",
                          compiler_params=_cp())(*args)
    return outs if hosting else outs[0]


def _bucket_table():
    d = np.arange(WIN)
    max_exact = NBKT // 2
    nf = np.maximum(d, 1).astype(np.float32)
    large = max_exact + (np.log(nf / np.float32(max_exact)) / np.float32(math.log(WIN / max_exact))
                         * np.float32(NBKT - max_exact)).astype(np.int32)
    large = np.minimum(large, NBKT - 1)
    bucket = np.where(d < max_exact, d, large).astype(np.int32)
    jj = np.arange(2 * WIN)[:, None]
    ii = np.arange(WIN)[None, :]
    dist = ii + WIN - jj
    valid = (dist >= 0) & (dist < WIN)
    return np.where(valid, bucket[np.clip(dist, 0, WIN - 1)], -1).astype(np.int32)


def _bias_build(rel_bias, bkt):
    def kern(bk_ref, rb_ref, o_ref):
        bk = bk_ref[...]
        for hd in range(QH):
            acc = jnp.full((2 * WIN, WIN), NEG, F32)
            for b in range(NBKT):
                acc = jnp.where(bk == b, rb_ref[b, hd], acc)
            o_ref[hd] = acc

    return pl.pallas_call(
        kern, in_specs=[pl.BlockSpec(memory_space=pltpu.VMEM), pl.BlockSpec(memory_space=pltpu.SMEM)],
        out_specs=pl.BlockSpec(memory_space=pltpu.VMEM),
        out_shape=jax.ShapeDtypeStruct((QH, 2 * WIN, WIN), F32), name="swa_bias_build")(bkt, rel_bias)


def _bias_bwd(dbias, bkt):
    def kern(db_ref, bk_ref, o_ref):
        bk = bk_ref[...]
        for hd in range(QH):
            g = db_ref[hd]
            for b in range(NBKT):
                r = b * QH + hd
                o_ref[r:r + 1, :] = jnp.sum(jnp.where(bk == b, g, 0.0), axis=0, keepdims=True)

    return pl.pallas_call(
        kern, in_specs=[pl.BlockSpec(memory_space=pltpu.VMEM), pl.BlockSpec(memory_space=pltpu.VMEM)],
        out_specs=pl.BlockSpec(memory_space=pltpu.VMEM),
        out_shape=jax.ShapeDtypeStruct((NBKT * QH, WIN), F32), name="swa_bias_bwd")(dbias, bkt)


def _swa_finish_scores(raw, bias, first):
    s = raw * SWA_SCALE + bias
    if first is not None:
        row = lax.broadcasted_iota(jnp.int32, s.shape, 0)
        s = jnp.where(jnp.logical_or(jnp.logical_not(first), row >= WIN), s, NEG)
    return s


def _swa_fwd(qkv_t, bias, sinks, qb):
    t = qkv_t.shape[1]
    w = qb * WIN
    nst = t // w

    def kern(q_ref, kc_ref, kp_ref, vc_ref, vp_ref, b_ref, sk_ref, o_ref, lse_ref):
        n = pl.program_id(0)
        kfull = jnp.concatenate([kp_ref[...], kc_ref[...]], axis=1)
        vfull = jnp.concatenate([vp_ref[...], vc_ref[...]], axis=1)
        head_row = lax.broadcasted_iota(jnp.int32, (QH, WIN), 0)
        groups = [(b, kh) for b in range(qb) for kh in range(KVH)]

        def raw_scores(b, kh):
            k_band = kfull[kh * HD:(kh + 1) * HD, b * WIN:(b + 2) * WIN]
            return [_dot(k_band, q_ref[(kh * G + g) * HD:(kh * G + g + 1) * HD, b * WIN:(b + 1) * WIN], TN)
                    for g in range(G)]

        o_rows = [[] for _ in range(qb)]
        lse_tiles = [jnp.zeros((QH, WIN), F32) for _ in range(qb)]
        nxt_scores = raw_scores(*groups[0])
        for gi, (b, kh) in enumerate(groups):
            scores = nxt_scores
            if gi + 1 < len(groups):
                nxt_scores = raw_scores(*groups[gi + 1])
            v_band = vfull[kh * HD:(kh + 1) * HD, b * WIN:(b + 2) * WIN]
            for g in range(G):
                hd = kh * G + g
                s = _swa_finish_scores(scores[g], b_ref[hd], (n == 0) if b == 0 else None)
                sink = sk_ref[hd]
                m = jnp.maximum(jnp.max(s, axis=0, keepdims=True), sink)
                p = jnp.exp(s - m)
                den = jnp.sum(p, axis=0, keepdims=True) + jnp.exp(sink - m)
                p = p / den
                o_rows[b].append(_dot(v_band, p.astype(BF16), NN))
                lse_tiles[b] = jnp.where(head_row == hd, m + jnp.log(den), lse_tiles[b])
        o_ref[...] = jnp.concatenate([jnp.concatenate(rows, axis=0) for rows in o_rows], axis=1)
        lse_ref[...] = jnp.concatenate(lse_tiles, axis=1)

    prev = lambda r: (lambda n: (r, jnp.maximum(n * qb - 1, 0)))
    return pl.pallas_call(
        kern, grid=(nst,),
        in_specs=[pl.BlockSpec((QH * HD, w), lambda n: (0, n)),
                  pl.BlockSpec((KVH * HD, w), lambda n: (4, n)), pl.BlockSpec((KVH * HD, WIN), prev(4)),
                  pl.BlockSpec((KVH * HD, w), lambda n: (5, n)), pl.BlockSpec((KVH * HD, WIN), prev(5)),
                  pl.BlockSpec((QH, 2 * WIN, WIN), lambda n: (0, 0, 0)),
                  pl.BlockSpec(memory_space=pltpu.SMEM)],
        out_specs=[pl.BlockSpec((QH * HD, w), lambda n: (0, n)), pl.BlockSpec((QH, w), lambda n: (0, n))],
        out_shape=[jax.ShapeDtypeStruct((QH * HD, t), F32), jax.ShapeDtypeStruct((QH, t), F32)],
        name="swa_fwd", compiler_params=_cp())(qkv_t, qkv_t, qkv_t, qkv_t, qkv_t, bias, sinks)


def _swa_bwd(qkv_t, do_t, o_t, lse, bias, sinks, qb):
    t = qkv_t.shape[1]
    w = qb * WIN
    nst = t // w
    nblk = t // WIN

    def kern(q_ref, kc_ref, kp_ref, vc_ref, vp_ref, do_ref, o_ref, lse_ref, qn_ref, don_ref, on_ref, lsen_ref,
             b_ref, sk_ref, dqkv_ref, db_ref, dsk_ref):
        n = pl.program_id(0)

        @pl.when(n == 0)
        def _():
            db_ref[...] = jnp.zeros_like(db_ref)
            dsk_ref[...] = jnp.zeros_like(dsk_ref)

        kfull = jnp.concatenate([kp_ref[...], kc_ref[...]], axis=1)
        vfull = jnp.concatenate([vp_ref[...], vc_ref[...]], axis=1)
        head_row = lax.broadcasted_iota(jnp.int32, (QH, WIN), 0)
        db_acc = [None] * QH
        dsk_tile = jnp.zeros((QH, WIN), F32)
        prev_part = [[[None] * qb for _ in range(KVH)] for _ in range(2)]
        cur_part = [[[None] * qb for _ in range(KVH)] for _ in range(2)]
        groups = [(b, kh) for b in range(qb) for kh in range(KVH)]

        def first_matmuls(b, kh):
            k_band = kfull[kh * HD:(kh + 1) * HD, b * WIN:(b + 2) * WIN]
            v_band = vfull[kh * HD:(kh + 1) * HD, b * WIN:(b + 2) * WIN]
            out = []
            for g in range(G):
                rs = slice((kh * G + g) * HD, (kh * G + g + 1) * HD)
                dob = do_ref[rs, b * WIN:(b + 1) * WIN].astype(BF16)
                out.append((_dot(k_band, q_ref[rs, b * WIN:(b + 1) * WIN], TN), _dot(v_band, dob, TN), dob))
            return out

        dq_rows = [[] for _ in range(qb)]
        nxt_first = first_matmuls(*groups[0])
        for gi, (b, kh) in enumerate(groups):
            first = nxt_first
            if gi + 1 < len(groups):
                nxt_first = first_matmuls(*groups[gi + 1])
            cs = slice(b * WIN, (b + 1) * WIN)
            k_band = kfull[kh * HD:(kh + 1) * HD, b * WIN:(b + 2) * WIN]
            dk_b = dv_b = None
            for g in range(G):
                hd = kh * G + g
                rs = slice(hd * HD, (hd + 1) * HD)
                raw, dp, dob = first[g]
                lse_h = lse_ref[hd:hd + 1, cs]
                s = _swa_finish_scores(raw, b_ref[hd], (n == 0) if b == 0 else None)
                p = jnp.exp(s - lse_h)
                dl = jnp.sum(do_ref[rs, cs] * o_ref[rs, cs], axis=0, keepdims=True)
                ds = p * (dp - dl)
                db_acc[hd] = ds if db_acc[hd] is None else db_acc[hd] + ds
                dsk_tile = jnp.where(head_row == hd, dsk_tile - jnp.exp(sk_ref[hd] - lse_h) * dl, dsk_tile)
                dss = (ds * SWA_SCALE).astype(BF16)
                dq_rows[b].append(_dot(k_band, dss, NN).astype(BF16))
                dk_h = _dot(q_ref[rs, cs], dss, NT)
                dv_h = _dot(dob, p.astype(BF16), NT)
                dk_b = dk_h if dk_b is None else dk_b + dk_h
                dv_b = dv_h if dv_b is None else dv_b + dv_h
            for which, val in ((0, dk_b), (1, dv_b)):
                prev_part[which][kh][b] = val[:, 0:WIN]
                cur_part[which][kh][b] = val[:, WIN:2 * WIN]
        dq_cols = [jnp.concatenate(rows, axis=0) for rows in dq_rows]

        live = n < nst - 1
        ls = slice((qb - 1) * WIN, qb * WIN)
        halo = [[None] * KVH for _ in range(2)]
        for kh in range(KVH):
            k_last = kc_ref[kh * HD:(kh + 1) * HD, ls]
            v_last = vc_ref[kh * HD:(kh + 1) * HD, ls]
            dk_b = dv_b = None
            for g in range(G):
                hd = kh * G + g
                rs = slice(hd * HD, (hd + 1) * HD)
                q_t = qn_ref[rs, :]
                do = don_ref[rs, :]
                s = _dot(k_last, q_t, TN) * SWA_SCALE + b_ref[hd, 0:WIN, :]
                p = jnp.exp(s - lsen_ref[hd:hd + 1, :])
                dob = do.astype(BF16)
                dp = _dot(v_last, dob, TN)
                dl = jnp.sum(do * on_ref[rs, :], axis=0, keepdims=True)
                dss = (p * (dp - dl) * SWA_SCALE).astype(BF16)
                dk_h = _dot(q_t, dss, NT)
                dv_h = _dot(dob, p.astype(BF16), NT)
                dk_b = dk_h if dk_b is None else dk_b + dk_h
                dv_b = dv_h if dv_b is None else dv_b + dv_h
            halo[0][kh] = jnp.where(live, dk_b, 0.0)
            halo[1][kh] = jnp.where(live, dv_b, 0.0)

        kv_rows = []
        for which in range(2):
            for kh in range(KVH):
                blocks = [cur_part[which][kh][p] + (prev_part[which][kh][p + 1] if p + 1 < qb else halo[which][kh])
                          for p in range(qb)]
                kv_rows.append(jnp.concatenate(blocks, axis=1))
        dqkv_ref[...] = jnp.concatenate(
            [jnp.concatenate(dq_cols, axis=1), jnp.concatenate(kv_rows, axis=0).astype(BF16)], axis=0)
        db_ref[...] += jnp.stack(db_acc)
        dsk_ref[...] += dsk_tile

    prev = lambda r: (lambda n: (r, jnp.maximum(n * qb - 1, 0)))
    nxt = lambda n: (0, jnp.minimum((n + 1) * qb, nblk - 1))
    big = lambda: pl.BlockSpec((QH * HD, w), lambda n: (0, n))
    return pl.pallas_call(
        kern, grid=(nst,),
        in_specs=[big(),
                  pl.BlockSpec((KVH * HD, w), lambda n: (4, n)), pl.BlockSpec((KVH * HD, WIN), prev(4)),
                  pl.BlockSpec((KVH * HD, w), lambda n: (5, n)), pl.BlockSpec((KVH * HD, WIN), prev(5)),
                  big(), big(), pl.BlockSpec((QH, w), lambda n: (0, n)),
                  pl.BlockSpec((QH * HD, WIN), nxt), pl.BlockSpec((QH * HD, WIN), nxt),
                  pl.BlockSpec((QH * HD, WIN), nxt), pl.BlockSpec((QH, WIN), nxt),
                  pl.BlockSpec((QH, 2 * WIN, WIN), lambda n: (0, 0, 0)),
                  pl.BlockSpec(memory_space=pltpu.SMEM)],
        out_specs=[pl.BlockSpec(((QH + 2 * KVH) * HD, w), lambda n: (0, n)),
                   pl.BlockSpec((QH, 2 * WIN, WIN), lambda n: (0, 0, 0)),
                   pl.BlockSpec((QH, WIN), lambda n: (0, 0))],
        out_shape=[jax.ShapeDtypeStruct(((QH + 2 * KVH) * HD, t), BF16),
                   jax.ShapeDtypeStruct((QH, 2 * WIN, WIN), F32), jax.ShapeDtypeStruct((QH, WIN), F32)],
        name="swa_bwd", compiler_params=_cp())(
            qkv_t, qkv_t, qkv_t, qkv_t, qkv_t, do_t, o_t, lse, qkv_t, do_t, o_t, lse, bias, sinks)


def _adamw_math(w, g, m, v):
    nm = B1 * m + (1.0 - B1) * g
    nv = B2 * v + (1.0 - B2) * (g * g)
    mhat = nm * (1.0 / (1.0 - B1 ** STEP))
    vhat = nv * (1.0 / (1.0 - B2 ** STEP))
    return -LR * (mhat / (jnp.sqrt(vhat) + ADAM_EPS) + WD * w), nm, nv


def _adamw_layers(w, m, v, g0buf, g1buf, off, name, tm=512):
    rows = w.shape[1]
    nb, ob = rows // tm, off // tm

    def kern(w_ref, m_ref, v_ref, g0_ref, g1_ref, gr_ref, d_ref, nm_ref, nv_ref):
        g_ = jnp.where(pl.program_id(0) == 0, g0_ref[...], g1_ref[...])
        gr_ref[...] = g_
        d_ref[...], nm_ref[...], nv_ref[...] = _adamw_math(w_ref[...], g_, m_ref[...], v_ref[...])

    lay = pl.BlockSpec((None, tm, D), lambda l, i: (l, i, 0))
    gsp = pl.BlockSpec((tm, D), lambda l, i: (ob + i, 0))
    return pl.pallas_call(
        kern, grid=(2, nb), in_specs=[lay, lay, lay, gsp, gsp], out_specs=[lay] * 4,
        out_shape=[jax.ShapeDtypeStruct(w.shape, F32)] * 4, name=name, compiler_params=_cp())(w, m, v, g0buf, g1buf)


def _adamw(w, g, m, v, name, tm=544):
    r = w.shape[0]
    tm = r if r % tm else tm

    def kern(w_ref, g_ref, m_ref, v_ref, d_ref, nm_ref, nv_ref):
        d_ref[...], nm_ref[...], nv_ref[...] = _adamw_math(w_ref[...], g_ref[...], m_ref[...], v_ref[...])

    row = pl.BlockSpec((tm, D), lambda i: (i, 0))
    sds = jax.ShapeDtypeStruct((r, D), F32)
    return pl.pallas_call(kern, grid=(r // tm,), in_specs=[row] * 4, out_specs=[row] * 3, out_shape=[sds] * 3,
                          name=name, compiler_params=_cp())(w, g, m, v)


def _mesh_pos():
    return lax.axis_index("x"), lax.axis_index("y"), lax.axis_index("c")


ANY = pl.BlockSpec(memory_space=pl.ANY)


AG_SEMS = [pltpu.SemaphoreType.DMA((6,)), pltpu.SemaphoreType.DMA((6,))]


def _allgather_schedule(w_ref, out_ref, send_sems, recv_sems):
    half = w_ref.shape[0] // 2
    x, y, c = _mesh_pos()
    me, sibling = (x, y, c), (x, y, 1 - c)
    chips = [(1 - x, y), (x, 1 - y), (1 - x, 1 - y)]

    def rows(px, py, pc):
        return out_ref.at[2 * px + py, pl.ds(pc * half, half), :]

    def copy(k, block, to, src=None):
        return pltpu.make_async_remote_copy(
            src_ref=rows(*block) if src is None else src, dst_ref=rows(*block),
            send_sem=send_sems.at[k], recv_sem=recv_sems.at[k], device_id=to, device_id_type=MESH)

    def first():
        return [copy(j, me, (*chip, c), src=w_ref.at[pl.ds(c * half, half), :]) for j, chip in enumerate(chips)]

    def passed():
        return [copy(3 + j, (*chip, c), sibling) for j, chip in enumerate(chips)]

    def start():
        for cp in first():
            cp.start()

    def forward():
        for j, chip in enumerate(chips):
            copy(j, (*chip, c), me).wait_recv()
            passed()[j].start()

    def finish():
        for j, chip in enumerate(chips):
            copy(3 + j, (*chip, 1 - c), me).wait_recv()
        for cp in first() + passed():
            cp.wait_send()

    return start, forward, finish


def _allgather_weights(wpack):
    def body(w_ref, out_ref, send_sems, recv_sems):
        start, forward, finish = _allgather_schedule(w_ref, out_ref, send_sems, recv_sems)
        start()
        forward()
        finish()

    return pl.pallas_call(
        body, out_shape=jax.ShapeDtypeStruct((4,) + wpack.shape, wpack.dtype), in_specs=[ANY], out_specs=ANY,
        scratch_shapes=AG_SEMS, name="allgather_weights")(wpack)


def _row_tile(rows):
    t = min(rows, 512)
    while rows % t or t % 16:
        t -= 16
    return t


ALL_SEMS = [pltpu.SemaphoreType.DMA((7,)), pltpu.SemaphoreType.DMA((7,))]


def _device_exchange_schedule(g_ref, out_ref, send_sems, recv_sems):
    half = g_ref.shape[1] // 2
    x, y, c = _mesh_pos()
    me = 4 * x + 2 * y + c
    peers = [(x ^ (k >> 2), y ^ ((k >> 1) & 1), c ^ (k & 1)) for k in range(1, 8)]

    def sends():
        return [pltpu.make_async_remote_copy(
            src_ref=g_ref.at[2 * px + py, pl.ds(pc * half, half), :], dst_ref=out_ref.at[me],
            send_sem=send_sems.at[j], recv_sem=recv_sems.at[j], device_id=(px, py, pc), device_id_type=MESH)
            for j, (px, py, pc) in enumerate(peers)]

    def start():
        for cp in sends():
            cp.start()

    def finish():
        for j, (px, py, pc) in enumerate(peers):
            pltpu.make_async_remote_copy(
                src_ref=out_ref.at[me], dst_ref=out_ref.at[4 * px + 2 * py + pc], send_sem=send_sems.at[j],
                recv_sem=recv_sems.at[j], device_id=(px, py, pc), device_id_type=MESH).wait_recv()
        for cp in sends():
            cp.wait_send()

    return start, finish


def _sum_devices(slots, g, pos, tag):
    half = slots.shape[1]
    tm = _row_tile(half)
    nb = half // tm

    def kern(pos_ref, own_ref, *refs):
        acc = own_ref[0].astype(F32)
        for s_ref in refs[:7]:
            acc = acc + s_ref[0].astype(F32)
        refs[7][...] = acc

    def slot(k):
        return pl.BlockSpec((1, tm, D), lambda i, pos: (jnp.bitwise_xor(pos[2], k), i, 0))

    gs = pltpu.PrefetchScalarGridSpec(
        num_scalar_prefetch=1, grid=(nb,),
        in_specs=[pl.BlockSpec((1, tm, D), lambda i, pos: (pos[0], pos[1] * nb + i, 0))] + [slot(k) for k in range(1, 8)],
        out_specs=pl.BlockSpec((tm, D), lambda i, pos: (pos[1] * nb + i, 0)))
    return pl.pallas_call(kern, grid_spec=gs, out_shape=jax.ShapeDtypeStruct((2 * half, D), F32),
                          name=f"rs_sum_devices_{tag}", compiler_params=_cp())(pos, g, *([slots] * 7))


def _exchange_devices(g, tag):
    def body(g_ref, out_ref, send_sems, recv_sems):
        start, finish = _device_exchange_schedule(g_ref, out_ref, send_sems, recv_sems)
        start()
        finish()

    return pl.pallas_call(
        body, out_shape=jax.ShapeDtypeStruct((8, g.shape[1] // 2, D), g.dtype), in_specs=[ANY], out_specs=ANY,
        scratch_shapes=ALL_SEMS, name=f"rs_exchange_devices_{tag}")(g)


def _reduce_scatter_finish(slots, g, pos, tag):
    return _join_core_halves(_sum_devices(slots, g, pos, tag), tag)


def _join_core_halves(r, tag):
    half = r.shape[0] // 2

    def body(r_ref, out_ref, send_sem, recv_sem):
        x, y, c = _mesh_pos()
        mine = out_ref.at[pl.ds(c * half, half), :]
        cp = pltpu.make_async_remote_copy(
            src_ref=mine, dst_ref=mine, send_sem=send_sem, recv_sem=recv_sem,
            device_id=(x, y, 1 - c), device_id_type=MESH)
        cp.start()
        theirs = out_ref.at[pl.ds((1 - c) * half, half), :]
        pltpu.make_async_remote_copy(
            src_ref=theirs, dst_ref=theirs, send_sem=send_sem, recv_sem=recv_sem,
            device_id=(x, y, 1 - c), device_id_type=MESH).wait_recv()
        cp.wait_send()

    return pl.pallas_call(
        body, out_shape=jax.ShapeDtypeStruct(r.shape, r.dtype), in_specs=[ANY], out_specs=ANY,
        input_output_aliases={0: 0},
        scratch_shapes=[pltpu.SemaphoreType.DMA, pltpu.SemaphoreType.DMA],
        name=f"rs_join_cores_{tag}")(r)


def _allreduce_small(v, name):
    def body(v_ref, out_ref, gat, send_sems, recv_sems):
        x, y, c = _mesh_pos()
        me = 4 * x + 2 * y + c
        gat[me] = v_ref[...]
        sends = []
        for k in range(1, 8):
            peer = (x ^ (k >> 2), y ^ ((k >> 1) & 1), c ^ (k & 1))
            cp = pltpu.make_async_remote_copy(
                src_ref=v_ref, dst_ref=gat.at[me], send_sem=send_sems.at[k - 1], recv_sem=recv_sems.at[k - 1],
                device_id=peer, device_id_type=MESH)
            cp.start()
            sends.append(cp)
        for k in range(1, 8):
            px, py, pc = x ^ (k >> 2), y ^ ((k >> 1) & 1), c ^ (k & 1)
            pltpu.make_async_remote_copy(
                src_ref=v_ref, dst_ref=gat.at[4 * px + 2 * py + pc], send_sem=send_sems.at[k - 1],
                recv_sem=recv_sems.at[k - 1], device_id=(px, py, pc), device_id_type=MESH).wait_recv()
        for cp in sends:
            cp.wait_send()
        acc = gat[0]
        for d in range(1, 8):
            acc = acc + gat[d]
        out_ref[...] = acc

    return pl.pallas_call(
        body, out_shape=jax.ShapeDtypeStruct(v.shape, F32),
        in_specs=[pl.BlockSpec(memory_space=pltpu.VMEM)], out_specs=pl.BlockSpec(memory_space=pltpu.VMEM),
        scratch_shapes=[pltpu.VMEM((8,) + v.shape, F32), pltpu.SemaphoreType.DMA((7,)), pltpu.SemaphoreType.DMA((7,))],
        name=name)(v)


def _mlp_fwd(xb, w_up, w_down, tag):
    a = _mm(xb, w_up[0], "nn", f"mlp_up_{tag}", out_dtype=BF16, relu2=True, b_view=("cols", w_up[1]))
    return a, _mm(a, w_down[0], "nn", f"mlp_down_{tag}", b_view=("rows", w_down[1]))


def _mlp_bwd(dz, dzb, xb, a, w_up, w_down, tag):
    du = _mm(dzb, w_down[0], "nt", f"mlp_down_dx_{tag}", out_dtype=BF16, gate_a=a, b_view=("rows", w_down[1]))
    gsh = _mm(xb, du, "tn", f"mlp_up_dw_{tag}", out_dtype=BF16, out_view=("cols", 2 * ROWS["mlp_w_up"], 0, None))
    gsh = _mm(a, dzb, "tn", f"mlp_down_dw_{tag}", out_dtype=BF16,
              out_view=("rows", 2 * ROWS["mlp_w_up"], ROWS["mlp_w_up"], gsh))
    dx = _mm(du, w_up[0], "nt", f"mlp_up_dx_{tag}", addend=dz, add_scale=ALPHA, b_view=("cols", w_up[1]))
    return dx, gsh


def _fwd_bwd(x, target, w, dist=None, bq=512, qb=4, hb=4):
    t = x.shape[0]
    bq = min(bq, t)
    qb = min(qb, t // WIN)
    cos, sin = _rope_tables(t)
    bkt = jnp.asarray(_bucket_table())
    w_in = jnp.pad(w[("mla_w_in", None)], ((0, 0), (0, HW - (QR + KVR + ROPE))))
    wuq = w[("mla_w_uq", None)]
    wq2 = jnp.concatenate([wuq[:, :, :NOPE].reshape(QR, H * NOPE),
                           jnp.pad(wuq[:, :, NOPE:], ((0, 0), (0, 0), (0, RP - ROPE))).reshape(QR, H * RP)], axis=1)
    wuk_t = w[("mla_w_uk", None)].transpose(1, 2, 0)
    wuk_h = w[("mla_w_uk", None)].transpose(1, 0, 2)
    wuv_h = w[("mla_w_uv", None)].transpose(1, 0, 2)
    w_o = w[("mla_w_o", None)]
    sinks = w["swa_sinks"].reshape(QH)
    lnp = lambda n, l: w[n][l]
    reduced = {}

    hh = _mm(x, w_in, "nn", "mla_in")
    cq, kc = _mla_pre(hh, w["mla_g_q"], w["mla_g_kv"], cos, sin)
    q2 = _mm(cq, wq2, "nn", "mla_uq")
    qcat = _q_prep(q2, wuk_t, cos, sin)
    if dist is None:
        o_lat, lse0_t = _flash_fwd(qcat, kc, bq, hb)
    else:
        o_lat, lse0_t, wall = _flash_fwd(qcat, kc, bq, hb, gather=dist.late_pack)
        wall = lax.dynamic_update_slice(wall, dist.late_pack[None], (dist.shard, 0, 0))
        w = {**w, **_full_from_gathered(AG_LATE, wall, dist.shard_shapes)}
    wqkv = jnp.concatenate([w[("swa_w_q", None)], w[("kv_w_shared", None)]], axis=1)
    wqkv_t = wqkv.T
    wo_s = w[("swa_w_o", None)]
    o0 = _o_up(o_lat, wuv_h)
    y0 = _mm(o0, w_o, "nn", "mla_out")
    x1b, xh1, r1 = _add_ln(x, y0, lnp("ln_mix_g", 0), lnp("ln_mix_b", 0), "ln_mix_0")
    a0, f0 = _mlp_fwd(x1b, w[("mlp_w_up", 0)], w[("mlp_w_down", 0)], 0)
    x2b, xh2, r2 = _add_ln(xh1, f0, lnp("ln_mlp_g", 0), lnp("ln_mlp_b", 0), "ln_mlp_0",
                           res_affine=(lnp("ln_mix_g", 0), lnp("ln_mix_b", 0)))
    bias = _bias_build(w["rel_bias"], bkt)
    qkv_t = _mm(x2b, wqkv, "nn", "swa_qkv", out_dtype=BF16, out_t=True)
    os_t, lse1 = _swa_fwd(qkv_t, bias, sinks, qb)
    y1 = _mm(os_t, wo_s, "tn", "swa_out")
    x3b, xh3, r3 = _add_ln(xh2, y1, lnp("ln_mix_g", 1), lnp("ln_mix_b", 1), "ln_mix_1",
                           res_affine=(lnp("ln_mlp_g", 0), lnp("ln_mlp_b", 0)))
    a1, f1 = _mlp_fwd(x3b, w[("mlp_w_up", 1)], w[("mlp_w_down", 1)], 1)
    _, xh4, r4 = _add_ln(xh3, f1, lnp("ln_mlp_g", 1), lnp("ln_mlp_b", 1), "ln_mlp_1",
                         res_affine=(lnp("ln_mix_g", 1), lnp("ln_mix_b", 1)))

    g = {}
    dz4, dz4b, dg_mlp1, db_mlp1, lpart = _ln_bwd(target, xh4, r4, lnp("ln_mlp_g", 1), "ln_mlp_1_bwd",
                                                 loss_b=lnp("ln_mlp_b", 1))
    dx3, g["mlp1"] = _mlp_bwd(dz4, dz4b, x3b, a1, w[("mlp_w_up", 1)], w[("mlp_w_down", 1)], 1)
    dz3, dz3b, dg_mix1, db_mix1 = _ln_bwd(dx3, xh3, r3, lnp("ln_mix_g", 1), "ln_mix_1_bwd")
    dos_t = _mm(dz3b, wo_s, "nt", "swa_out_dx", out_t=True)
    g[("swa_w_o", None)] = _mm(os_t, dz3b, "nn", "swa_out_dw")
    dqkv_t, dbias, dsk = _swa_bwd(qkv_t, dos_t, os_t, lse1, bias, sinks, qb)
    dwqkv = _mm(dqkv_t, x2b, "nn", "swa_qkv_dw").T
    g[("swa_w_q", None)], g[("kv_w_shared", None)] = dwqkv[:, :QH * HD], dwqkv[:, QH * HD:]
    dx2 = _mm(dqkv_t, wqkv_t, "tn", "swa_qkv_dx", addend=dz3, add_scale=ALPHA)
    g["rel_bias"] = jnp.sum(_bias_bwd(dbias, bkt), axis=-1).reshape(NBKT, QH)
    g["swa_sinks"] = jnp.sum(dsk, axis=-1).reshape(1, QH)
    dz2, dz2b, dg_mlp0, db_mlp0 = _ln_bwd(dx2, xh2, r2, lnp("ln_mlp_g", 0), "ln_mlp_0_bwd")
    dx1, g["mlp0"] = _mlp_bwd(dz2, dz2b, x1b, a0, w[("mlp_w_up", 0)], w[("mlp_w_down", 0)], 0)
    dz1, dz1b, dg_mix0, db_mix0 = _ln_bwd(dx1, xh1, r1, lnp("ln_mix_g", 0), "ln_mix_0_bwd")
    do0 = _mm(dz1b, w_o, "nt", "mla_out_dx", out_dtype=BF16)
    g[("mla_w_o", None)] = _mm(o0, dz1b, "tn", "mla_out_dw")
    do_lat, dwuv, delta_t = _o_up_bwd(do0, o_lat, wuv_h)
    g[("mla_w_uv", None)] = dwuv.transpose(1, 0, 2)
    kc_t = kc.reshape(t // bq, bq, KD).transpose(0, 2, 1)
    if dist is None:
        dk, ds_all = _flash_dkv(qcat, kc, do_lat, lse0_t, delta_t, bq, hb)
        dq_cat = _flash_dq(ds_all, kc_t, bq, hb)
    else:
        g["mid"] = _grad_shards(RS_MID, g).astype(BF16)
        dk, ds_all, slots1, slots_mid = _flash_dkv(qcat, kc, do_lat, lse0_t, delta_t, bq, hb,
                                                  exchange=(g["mlp1"], g["mid"]))
        dq_cat, slots0 = _flash_dq(ds_all, kc_t, bq, hb, exchange=g["mlp0"])
        for key, slots in (("mlp1", slots1), ("mid", slots_mid), ("mlp0", slots0)):
            reduced[key] = _reduce_scatter_finish(slots, g[key], dist.pos, key)
    dq2, dwuk = _q_prep_bwd(dq_cat, q2, wuk_h, cos, sin)
    g[("mla_w_uk", None)] = dwuk.transpose(2, 0, 1)
    dcq = _mm(dq2, wq2, "nt", "mla_uq_dx")
    dwq2 = _mm(cq, dq2, "tn", "mla_uq_dw")
    g[("mla_w_uq", None)] = jnp.concatenate([dwq2[:, :H * NOPE].reshape(QR, H, NOPE),
                                             dwq2[:, H * NOPE:].reshape(QR, H, RP)[:, :, :ROPE]], axis=2)
    dh, dgq, dgkv = _mla_pre_bwd(hh, dcq, dk, w["mla_g_q"], w["mla_g_kv"], cos, sin)
    g[("mla_w_in", None)] = _mm(x, dh, "tn", "mla_in_dw")[:, :QR + KVR + ROPE]
    grad_x = _mm(dh, w_in, "nt", "mla_in_dx", addend=dz1, add_scale=ALPHA)
    g["mla_g_q"], g["mla_g_kv"] = dgq, dgkv
    g["ln_mix_g"] = jnp.concatenate([dg_mix0, dg_mix1], axis=0)
    g["ln_mix_b"] = jnp.concatenate([db_mix0, db_mix1], axis=0)
    g["ln_mlp_g"] = jnp.concatenate([dg_mlp0, dg_mlp1], axis=0)
    g["ln_mlp_b"] = jnp.concatenate([db_mlp0, db_mlp1], axis=0)
    return lpart, grad_x, g, reduced


def _rows(a):
    return a.reshape(-1, D)


def _piece(a, layer):
    return _rows(a if layer is None else a[layer])


def _pack_group(group, parts):
    return jnp.concatenate([_piece(parts[n], l) for n, l in group], axis=0)


def _unpack_group(group, buf, like):
    out, off = {}, 0
    for n, l in group:
        shp = like[n].shape if l is None else like[n].shape[1:]
        out[(n, l)] = buf[off:off + ROWS[n]].reshape(shp)
        off += ROWS[n]
    return out


def _by_name(pieces):
    out = {n: a for (n, l), a in pieces.items() if l is None}
    for n in {n for (n, l) in pieces if l is not None}:
        out[n] = jnp.stack([pieces[(n, 0)], pieces[(n, 1)]])
    return out


def _full_from_gathered(group, wall, shard_shapes):
    out, off = {}, 0
    for n, l in group:
        shp = tuple(shard_shapes[n])
        if n in ("mlp_w_up", "mlp_w_down"):
            out[(n, l)] = (wall, off)
        elif n == "kv_w_shared":
            out[(n, l)] = wall[:, off:off + ROWS[n]].reshape((4 * shp[0],) + shp[1:])
        else:
            out[(n, l)] = wall[:, off:off + ROWS[n]].reshape((4 * shp[1],) + shp[2:])
        off += ROWS[n]
    return out


def _grad_shards(group, g):
    return jnp.concatenate([g[(n, l)].reshape(4, ROWS[n], D) for n, l in group], axis=1)


SMALL = (("ln_mix_g", 0, 2), ("ln_mix_b", 2, 2), ("ln_mlp_g", 4, 2), ("ln_mlp_b", 6, 2),
         ("swa_sinks", 8, 1), ("mla_g_q", 9, 1), ("mla_g_kv", 10, 1), ("rel_bias", 11, 1))
LOSS_ROW = 12


def _pack_small(parts, extra_row=None):
    rows = []
    for n, _, nr in SMALL:
        a = parts[n].reshape(nr, -1).astype(F32)
        rows.append(jnp.pad(a, ((0, 0), (0, D - a.shape[1]))))
    if extra_row is not None:
        rows.append(extra_row)
    rows.append(jnp.zeros((SMALL_ROWS - sum(r.shape[0] for r in rows), D), F32))
    return jnp.concatenate(rows, axis=0)


def _unpack_small(buf, like):
    out = {}
    for n, r0, nr in SMALL:
        size = like[n].size // nr
        out[n] = buf[r0:r0 + nr, :size].reshape(like[n].shape)
    return out


def kernel(x, mla_w_in, mla_g_q, mla_g_kv, mla_w_uq, mla_w_uk, mla_w_uv, mla_w_o, kv_w_shared, swa_w_q, swa_sinks, swa_w_o, rel_bias, mlp_w_up, mlp_w_down, ln_mix_g, ln_mix_b, ln_mlp_g, ln_mlp_b, loss_target, m_mla_w_in, m_mla_g_q, m_mla_g_kv, m_mla_w_uq, m_mla_w_uk, m_mla_w_uv, m_mla_w_o, m_kv_w_shared, m_swa_w_q, m_swa_sinks, m_swa_w_o, m_rel_bias, m_mlp_w_up, m_mlp_w_down, m_ln_mix_g, m_ln_mix_b, m_ln_mlp_g, m_ln_mlp_b, v_mla_w_in, v_mla_g_q, v_mla_g_kv, v_mla_w_uq, v_mla_w_uk, v_mla_w_uv, v_mla_w_o, v_kv_w_shared, v_swa_w_q, v_swa_sinks, v_swa_w_o, v_rel_bias, v_mlp_w_up, v_mlp_w_down, v_ln_mix_g, v_ln_mix_b, v_ln_mlp_g, v_ln_mlp_b):
    names = ["mla_w_in", "mla_g_q", "mla_g_kv", "mla_w_uq", "mla_w_uk", "mla_w_uv", "mla_w_o", "kv_w_shared",
             "swa_w_q", "swa_sinks", "swa_w_o", "rel_bias", "mlp_w_up", "mlp_w_down",
             "ln_mix_g", "ln_mix_b", "ln_mlp_g", "ln_mlp_b"]
    ws = dict(zip(names, [mla_w_in, mla_g_q, mla_g_kv, mla_w_uq, mla_w_uk, mla_w_uv, mla_w_o, kv_w_shared,
                          swa_w_q, swa_sinks, swa_w_o, rel_bias, mlp_w_up, mlp_w_down,
                          ln_mix_g, ln_mix_b, ln_mlp_g, ln_mlp_b]))
    ms = dict(zip(names, [m_mla_w_in, m_mla_g_q, m_mla_g_kv, m_mla_w_uq, m_mla_w_uk, m_mla_w_uv, m_mla_w_o,
                          m_kv_w_shared, m_swa_w_q, m_swa_sinks, m_swa_w_o, m_rel_bias, m_mlp_w_up, m_mlp_w_down,
                          m_ln_mix_g, m_ln_mix_b, m_ln_mlp_g, m_ln_mlp_b]))
    vs = dict(zip(names, [v_mla_w_in, v_mla_g_q, v_mla_g_kv, v_mla_w_uq, v_mla_w_uk, v_mla_w_uv, v_mla_w_o,
                          v_kv_w_shared, v_swa_w_q, v_swa_sinks, v_swa_w_o, v_rel_bias, v_mlp_w_up, v_mlp_w_down,
                          v_ln_mix_g, v_ln_mix_b, v_ln_mlp_g, v_ln_mlp_b]))
    xi, yi, ci = _mesh_pos()
    shard = 2 * xi + yi
    shard_shapes = {n: ws[n].shape for n in ROWS}
    wbf = {n: ws[n].astype(BF16) for n in ROWS}

    early = _pack_group(AG_EARLY, wbf)
    wall = lax.dynamic_update_slice(_allgather_weights(early), early[None], (shard, 0, 0))
    w = _full_from_gathered(AG_EARLY, wall, shard_shapes)
    dist = _Dist(shard=shard, pos=jnp.stack([shard, ci, 2 * shard + ci]).astype(jnp.int32),
                 late_pack=_pack_group(AG_LATE, wbf), shard_shapes=shard_shapes)
    gq_slot = lax.dynamic_update_slice(jnp.zeros((1, QR), F32), mla_g_q, (0, shard * (QR // 4)))
    gkv_slot = lax.dynamic_update_slice(jnp.zeros((1, KVR), F32), mla_g_kv, (0, shard * (KVR // 4)))
    gains = jnp.concatenate([jnp.pad(gq_slot, ((0, 0), (0, D - QR))), jnp.pad(gkv_slot, ((0, 0), (0, D - KVR))),
                             jnp.zeros((SMALL_ROWS - 2, D), F32)], axis=0)
    gains = _allreduce_small(gains * 0.5, "allgather_gains")
    w["mla_g_q"], w["mla_g_kv"] = gains[0, :QR], gains[1, :KVR]
    for n in ("swa_sinks", "rel_bias", "ln_mix_g", "ln_mix_b", "ln_mlp_g", "ln_mlp_b"):
        w[n] = ws[n]

    lpart, grad_x, g, reduced = _fwd_bwd(x[0], loss_target[0], w, dist)

    g["end"] = _grad_shards(RS_END, g).astype(BF16)
    reduced["end"] = _reduce_scatter_finish(_exchange_devices(g["end"], "end"), g["end"], dist.pos, "end")
    reduced["rest"] = jnp.concatenate([reduced["mid"], reduced["end"]], axis=0)

    small_like = {n: g[n] for n, _, _ in SMALL}
    small_sum = _allreduce_small(_pack_small(g, extra_row=lpart), "allreduce_small_grads")
    loss = 0.5 * jnp.sum(small_sum[LOSS_ROW]) / D
    gsm = _unpack_small(small_sum, small_like)
    gsm["mla_g_q"] = lax.dynamic_slice(gsm["mla_g_q"], (0, shard * (QR // 4)), (1, QR // 4))
    gsm["mla_g_kv"] = lax.dynamic_slice(gsm["mla_g_kv"], (0, shard * (KVR // 4)), (1, KVR // 4))

    gbig, dbig, mbig, vbig = {}, {}, {}, {}
    for n in ("mlp_w_up", "mlp_w_down"):
        off = 0 if n == "mlp_w_up" else ROWS["mlp_w_up"]
        gbig[n], dbig[n], mbig[n], vbig[n] = _adamw_layers(
            ws[n], ms[n], vs[n], reduced["mlp0"], reduced["mlp1"], off, f"adamw_{n}")
    rest = RS_MID + RS_END
    outs = _adamw(_pack_group(rest, ws), reduced["rest"], _pack_group(rest, ms), _pack_group(rest, vs),
                  "adamw_rest", tm=_row_tile(reduced["rest"].shape[0]))
    for dst, buf in zip((gbig, dbig, mbig, vbig), (reduced["rest"], *outs)):
        dst.update(_by_name(_unpack_group(rest, buf, ws)))
    dsm, msm, vsm = _adamw(_pack_small(ws), _pack_small(gsm), _pack_small(ms), _pack_small(vs), "adamw_small", tm=16)
    grads = {**gbig, **gsm}
    delta = {**dbig, **_unpack_small(dsm, ws)}
    new_m = {**mbig, **_unpack_small(msm, ws)}
    new_v = {**vbig, **_unpack_small(vsm, ws)}
    grads = {n: grads[n].reshape(ws[n].shape) for n in names}
    return (loss, grad_x[None], *[grads[n] for n in names], *[delta[n] for n in names],
            *[new_m[n] for n in names], *[new_v[n] for n in names])
```

```python
import collections
import math

import numpy as np
import jax
import jax.numpy as jnp
from jax import lax
from jax.experimental import pallas as pl
from jax.experimental.pallas import tpu as pltpu

F32 = jnp.float32
BF16 = jnp.bfloat16
MESH = pl.DeviceIdType.MESH

D = 1024
DFF = 4096
H = 8
NOPE = 128
ROPE = 64
QR = 384
KVR = 256
RP = 128
KD = KVR + RP
HW = 768
QH = 16
KVH = 4
HD = 64
G = QH // KVH
WIN = 128
NBKT = 32
ALPHA = 4.0 ** 0.25
LN_EPS = 1e-5
RMS_EPS = 1e-6
MLA_SCALE = (NOPE + ROPE) ** -0.5
LOG2E = 1.4426950408889634
LN2 = 0.6931471805599453
QSCALE = MLA_SCALE * LOG2E
AHEAD = 1
SWA_SCALE = HD ** -0.5
NEG = -1e30
LR, B1, B2, ADAM_EPS, WD, STEP = 0.001, 0.9, 0.999, 1e-8, 0.01, 10

VMEM_LIMIT = 48 * 1024 * 1024

NN = (((1,), (0,)), ((), ()))
NT = (((1,), (1,)), ((), ()))
TN = (((0,), (0,)), ((), ()))

ROWS = {"mlp_w_up": 1024, "mlp_w_down": 1024, "mla_w_o": 256, "swa_w_q": 256, "swa_w_o": 256,
        "kv_w_shared": 128, "mla_w_in": 176, "mla_w_uq": 144, "mla_w_uk": 64, "mla_w_uv": 64}
AG_EARLY = (("mla_w_in", None), ("mla_w_uq", None), ("mla_w_uk", None), ("mla_w_uv", None), ("mla_w_o", None))
AG_LATE = (("mlp_w_up", 0), ("mlp_w_up", 1), ("mlp_w_down", 0), ("mlp_w_down", 1),
           ("swa_w_q", None), ("swa_w_o", None), ("kv_w_shared", None))
RS_MID = (("mla_w_o", None), ("swa_w_q", None), ("swa_w_o", None), ("kv_w_shared", None), ("mla_w_uv", None))
RS_END = (("mla_w_in", None), ("mla_w_uq", None), ("mla_w_uk", None))
SMALL_ROWS = 16
_Dist = collections.namedtuple("_Dist", "shard pos late_pack shard_shapes")


def _cp(**kw):
    return pltpu.CompilerParams(vmem_limit_bytes=VMEM_LIMIT, **kw)


def _tile(n, pref):
    t = min(n, pref)
    while n % t:
        t -= 128
    return t


def _dot(a, b, dims):
    return lax.dot_general(a, b, dims, preferred_element_type=F32)


def _mm(a, b, mode, name, out_dtype=F32, out_t=False, addend=None, add_scale=1.0, relu2=False, gate_a=None,
        b_view=None, out_view=None, tm=1024, tn=1024, tk=1024):
    blk = 1024
    if b_view is not None:
        kind, b_off = b_view
        assert b.shape[0] == 4 and b.shape[2] == blk and b_off % blk == 0
        bshape = {("cols", "nn"): (blk, 4 * blk), ("cols", "nt"): (blk, 4 * blk),
                  ("rows", "nn"): (4 * blk, blk), ("rows", "nt"): (4 * blk, blk)}[(kind, mode)]
    else:
        bshape = b.shape
    if mode == "nn":
        (m, k), (k2, n) = a.shape, bshape
    elif mode == "nt":
        (m, k), (n, k2) = a.shape, bshape
    else:
        (k, m), (k2, n) = a.shape, bshape
    assert k == k2, (name, a.shape, b.shape)
    tm, tn, tk = _tile(m, tm), _tile(n, tn), _tile(k, tk)
    nk = k // tk
    dims = {"nn": NN, "nt": NT, "tn": TN}[mode]
    if mode == "tn":
        a_spec = pl.BlockSpec((tk, tm), lambda i, j, kk: (kk, i))
    else:
        a_spec = pl.BlockSpec((tm, tk), lambda i, j, kk: (i, kk))
    if b_view is not None:
        assert tn == blk and tk == blk
        ob = b_off // blk
        b_spec = {("cols", "nn"): pl.BlockSpec((None, tk, tn), lambda i, j, kk: (j, ob, 0)),
                  ("cols", "nt"): pl.BlockSpec((None, tn, tk), lambda i, j, kk: (kk, ob, 0)),
                  ("rows", "nn"): pl.BlockSpec((None, tk, tn), lambda i, j, kk: (kk, ob, 0)),
                  ("rows", "nt"): pl.BlockSpec((None, tn, tk), lambda i, j, kk: (j, ob, 0))}[(kind, mode)]
    elif mode == "nt":
        b_spec = pl.BlockSpec((tn, tk), lambda i, j, kk: (j, kk))
    else:
        b_spec = pl.BlockSpec((tk, tn), lambda i, j, kk: (kk, j))
    mn_spec = pl.BlockSpec((tm, tn), lambda i, j, kk: (i, j))
    ins, in_specs = [a, b], [a_spec, b_spec]
    if addend is not None:
        ins.append(addend)
        in_specs.append(mn_spec)
    if gate_a is not None:
        ins.append(gate_a)
        in_specs.append(mn_spec)
    aliases = {}
    if out_view is not None:
        okind, total_rows, o_off, buf = out_view
        assert not out_t and tm == blk and tn == blk and o_off % blk == 0
        oo = o_off // blk
        out_shape = [jax.ShapeDtypeStruct((4, total_rows, blk), out_dtype)]
        if okind == "cols":
            out_specs = [pl.BlockSpec((None, tm, tn), lambda i, j, kk: (j, oo, 0))]
        else:
            out_specs = [pl.BlockSpec((None, tm, tn), lambda i, j, kk: (i, oo, 0))]
        if buf is not None:
            aliases = {len(ins): 0}
            ins.append(buf)
            in_specs.append(pl.BlockSpec(memory_space=pl.ANY))
    elif out_t:
        out_shape = [jax.ShapeDtypeStruct((n, m), out_dtype)]
        out_specs = [pl.BlockSpec((tn, tm), lambda i, j, kk: (j, i))]
    else:
        out_shape = [jax.ShapeDtypeStruct((m, n), out_dtype)]
        out_specs = [mn_spec]
    has_add, has_gate = addend is not None, gate_a is not None

    def kern(*refs):
        a_ref, b_ref = refs[0], refs[1]
        pos = 2
        add_ref = gate_ref = None
        if has_add:
            add_ref = refs[pos]
            pos += 1
        if has_gate:
            gate_ref = refs[pos]
            pos += 1
        o_ref = refs[pos + len(aliases)]
        acc = refs[-1] if nk > 1 else None
        kk = pl.program_id(2)

        def partial():
            return _dot(a_ref[...].astype(BF16), b_ref[...].astype(BF16), dims)

        if nk > 1:
            @pl.when(kk == 0)
            def _():
                acc[...] = partial()

            @pl.when((kk > 0) & (kk < nk - 1))
            def _():
                acc[...] += partial()

        @pl.when(kk == nk - 1)
        def _():
            r = partial() + acc[...] if nk > 1 else partial()
            if has_add:
                r = r + add_scale * add_ref[...].astype(F32)
            if has_gate:
                ga = gate_ref[...].astype(F32)
                r = r * jnp.where(ga > 0.0, (2.0 * ga) * lax.rsqrt(ga), 0.0)
            if relu2:
                hh = jnp.maximum(r, 0.0)
                r = hh * hh
            if out_t:
                r = r.T
            o_ref[...] = r.astype(out_dtype)

    return pl.pallas_call(
        kern, out_shape=out_shape, grid=(m // tm, n // tn, nk), in_specs=in_specs, out_specs=out_specs,
        scratch_shapes=[pltpu.VMEM((tm, tn), F32)] if nk > 1 else [], input_output_aliases=aliases,
        name=name, compiler_params=_cp())(*ins)[0]


def _add_ln(res, y, g, b, name, res_affine=None, tm=512):
    t = res.shape[0]
    tm = min(tm, t)
    affine = res_affine is not None

    def kern(*refs):
        if affine:
            x_ref, y_ref, g_ref, b_ref, g0_ref, b0_ref, ob_ref, xh_ref, r_ref = refs
            x = x_ref[...] * g0_ref[...] + b0_ref[...]
        else:
            x_ref, y_ref, g_ref, b_ref, ob_ref, xh_ref, r_ref = refs
            x = x_ref[...]
        z = ALPHA * x + y_ref[...]
        mu = jnp.mean(z, axis=-1, keepdims=True)
        zc = z - mu
        var = jnp.mean(zc * zc, axis=-1, keepdims=True)
        r = lax.rsqrt(var + LN_EPS)
        xh = zc * r
        ob_ref[...] = (xh * g_ref[...] + b_ref[...]).astype(BF16)
        xh_ref[...] = xh
        r_ref[...] = r

    row = pl.BlockSpec((tm, D), lambda i: (i, 0))
    vec = pl.BlockSpec((1, D), lambda i: (0, 0))
    st = pl.BlockSpec((tm, 1), lambda i: (i, 0))
    ins = [res, y, g.reshape(1, D), b.reshape(1, D)]
    if affine:
        ins += [res_affine[0].reshape(1, D), res_affine[1].reshape(1, D)]
    return pl.pallas_call(
        kern, grid=(t // tm,), in_specs=[row, row] + [vec] * (len(ins) - 2), out_specs=[row, row, st],
        out_shape=[jax.ShapeDtypeStruct((t, D), BF16), jax.ShapeDtypeStruct((t, D), F32),
                   jax.ShapeDtypeStruct((t, 1), F32)],
        name=name, compiler_params=_cp())(*ins)


def _ln_bwd(dout, xhat, rstd, g, name, loss_b=None, tm=512):
    t = dout.shape[0]
    tm = min(tm, t)
    head = loss_b is not None

    def kern(*refs):
        if head:
            do_ref, xh_ref, r_ref, g_ref, b_ref, dz_ref, dzb_ref, dg_ref, db_ref, l_ref = refs
        else:
            do_ref, xh_ref, r_ref, g_ref, dz_ref, dzb_ref, dg_ref, db_ref = refs

        @pl.when(pl.program_id(0) == 0)
        def _():
            dg_ref[...] = jnp.zeros_like(dg_ref)
            db_ref[...] = jnp.zeros_like(db_ref)
            if head:
                l_ref[...] = jnp.zeros_like(l_ref)

        xh = xh_ref[...]
        if head:
            e = xh * g_ref[...] + b_ref[...] - do_ref[...]
            l_ref[...] += jnp.sum(e * e, axis=0, keepdims=True)
            do = e * (1.0 / D)
        else:
            do = do_ref[...]
        dxh = do * g_ref[...]
        m1 = jnp.mean(dxh, axis=-1, keepdims=True)
        m2 = jnp.mean(dxh * xh, axis=-1, keepdims=True)
        dz = r_ref[...] * (dxh - m1 - xh * m2)
        dz_ref[...] = dz
        dzb_ref[...] = dz.astype(BF16)
        dg_ref[...] += jnp.sum(do * xh, axis=0, keepdims=True)
        db_ref[...] += jnp.sum(do, axis=0, keepdims=True)

    row = pl.BlockSpec((tm, D), lambda i: (i, 0))
    vec = pl.BlockSpec((1, D), lambda i: (0, 0))
    st = pl.BlockSpec((tm, 1), lambda i: (i, 0))
    ins = [dout, xhat, rstd, g.reshape(1, D)] + ([loss_b.reshape(1, D)] if head else [])
    return pl.pallas_call(
        kern, grid=(t // tm,), in_specs=[row, row, st] + [vec] * (len(ins) - 3),
        out_specs=[row, row, vec, vec] + ([vec] if head else []),
        out_shape=[jax.ShapeDtypeStruct((t, D), F32), jax.ShapeDtypeStruct((t, D), BF16)]
        + [jax.ShapeDtypeStruct((1, D), F32)] * (3 if head else 2),
        name=name, compiler_params=_cp())(*ins)


def _rope_tables(t):
    half = ROPE // 2
    inv = 10000.0 ** (-jnp.arange(half, dtype=F32) / half)
    ang = jnp.arange(t).astype(F32)[:, None] * inv[None, :]
    cos, sin = jnp.cos(ang), jnp.sin(ang)
    z = jnp.zeros((t, RP - ROPE), F32)
    return jnp.concatenate([cos, cos, z], axis=1), jnp.concatenate([-sin, sin, z], axis=1)


def _swap_halves(x):
    lane = lax.broadcasted_iota(jnp.int32, x.shape, 1)
    return jnp.where(lane < ROPE // 2, pltpu.roll(x, RP - ROPE // 2, 1), pltpu.roll(x, ROPE // 2, 1))


def _rope(x, cos, sin):
    return x * cos + _swap_halves(x) * sin


def _rope_t(gy, cos, sin):
    return gy * cos + _swap_halves(gy * sin)


def _mla_pre(hh, g_q, g_kv, cos, sin, tm=512):
    t = hh.shape[0]
    tm = min(tm, t)

    def kern(h_ref, gq_ref, gkv_ref, c_ref, s_ref, cq_ref, k_ref):
        xq = h_ref[:, 0:QR]
        rq = lax.rsqrt(jnp.mean(xq * xq, axis=-1, keepdims=True) + RMS_EPS)
        cq_ref[...] = (xq * rq * gq_ref[...]).astype(BF16)
        xk = h_ref[:, QR:QR + KVR]
        rk = lax.rsqrt(jnp.mean(xk * xk, axis=-1, keepdims=True) + RMS_EPS)
        k_ref[:, 0:KVR] = (xk * rk * gkv_ref[...]).astype(BF16)
        k_ref[:, KVR:KD] = _rope(h_ref[:, QR + KVR:HW], c_ref[...], s_ref[...]).astype(BF16)

    return pl.pallas_call(
        kern, grid=(t // tm,),
        in_specs=[pl.BlockSpec((tm, HW), lambda i: (i, 0)), pl.BlockSpec((1, QR), lambda i: (0, 0)),
                  pl.BlockSpec((1, KVR), lambda i: (0, 0)), pl.BlockSpec((tm, RP), lambda i: (i, 0)),
                  pl.BlockSpec((tm, RP), lambda i: (i, 0))],
        out_specs=[pl.BlockSpec((tm, QR), lambda i: (i, 0)), pl.BlockSpec((tm, KD), lambda i: (i, 0))],
        out_shape=[jax.ShapeDtypeStruct((t, QR), BF16), jax.ShapeDtypeStruct((t, KD), BF16)],
        name="mla_pre", compiler_params=_cp())(hh, g_q.reshape(1, QR), g_kv.reshape(1, KVR), cos, sin)


def _mla_pre_bwd(hh, dcq, dk, g_q, g_kv, cos, sin, tm=512):
    t = hh.shape[0]
    tm = min(tm, t)

    def rms_bwd(x, dy, g):
        r = lax.rsqrt(jnp.mean(x * x, axis=-1, keepdims=True) + RMS_EPS)
        gdy = dy * g
        dx = r * gdy - x * (r * r * r) * jnp.mean(gdy * x, axis=-1, keepdims=True)
        return dx, jnp.sum(dy * x * r, axis=0, keepdims=True)

    def kern(h_ref, dcq_ref, dk_ref, gq_ref, gkv_ref, c_ref, s_ref, dh_ref, dgq_ref, dgkv_ref):
        @pl.when(pl.program_id(0) == 0)
        def _():
            dgq_ref[...] = jnp.zeros_like(dgq_ref)
            dgkv_ref[...] = jnp.zeros_like(dgkv_ref)

        dxq, dgq = rms_bwd(h_ref[:, 0:QR], dcq_ref[...], gq_ref[...])
        dxk, dgk = rms_bwd(h_ref[:, QR:QR + KVR], dk_ref[:, 0:KVR], gkv_ref[...])
        dh_ref[:, 0:QR] = dxq.astype(BF16)
        dh_ref[:, QR:QR + KVR] = dxk.astype(BF16)
        dh_ref[:, QR + KVR:HW] = _rope_t(dk_ref[:, KVR:KD], c_ref[...], s_ref[...]).astype(BF16)
        dgq_ref[...] += dgq
        dgkv_ref[...] += dgk

    return pl.pallas_call(
        kern, grid=(t // tm,),
        in_specs=[pl.BlockSpec((tm, HW), lambda i: (i, 0)), pl.BlockSpec((tm, QR), lambda i: (i, 0)),
                  pl.BlockSpec((tm, KD), lambda i: (i, 0)), pl.BlockSpec((1, QR), lambda i: (0, 0)),
                  pl.BlockSpec((1, KVR), lambda i: (0, 0)), pl.BlockSpec((tm, RP), lambda i: (i, 0)),
                  pl.BlockSpec((tm, RP), lambda i: (i, 0))],
        out_specs=[pl.BlockSpec((tm, HW), lambda i: (i, 0)), pl.BlockSpec((1, QR), lambda i: (0, 0)),
                   pl.BlockSpec((1, KVR), lambda i: (0, 0))],
        out_shape=[jax.ShapeDtypeStruct((t, HW), BF16), jax.ShapeDtypeStruct((1, QR), F32),
                   jax.ShapeDtypeStruct((1, KVR), F32)],
        name="mla_pre_bwd", compiler_params=_cp())(hh, dcq, dk, g_q.reshape(1, QR), g_kv.reshape(1, KVR), cos, sin)


def _q_prep(q2, wuk_t, cos, sin, tm=512):
    t = q2.shape[0]
    tm = min(tm, t)

    def kern(q_ref, w_ref, c_ref, s_ref, o_ref):
        cos_, sin_ = c_ref[...], s_ref[...]
        for h in range(H):
            qn = q_ref[:, h * NOPE:(h + 1) * NOPE].astype(BF16)
            o_ref[:, h * KD:h * KD + KVR] = (_dot(qn, w_ref[h], NN) * QSCALE).astype(BF16)
            qr = q_ref[:, H * NOPE + h * RP:H * NOPE + (h + 1) * RP]
            o_ref[:, h * KD + KVR:(h + 1) * KD] = (_rope(qr, cos_, sin_) * QSCALE).astype(BF16)

    return pl.pallas_call(
        kern, grid=(t // tm,),
        in_specs=[pl.BlockSpec((tm, 2 * H * NOPE), lambda i: (i, 0)), pl.BlockSpec((H, NOPE, KVR), lambda i: (0, 0, 0)),
                  pl.BlockSpec((tm, RP), lambda i: (i, 0)), pl.BlockSpec((tm, RP), lambda i: (i, 0))],
        out_specs=pl.BlockSpec((tm, H * KD), lambda i: (i, 0)),
        out_shape=jax.ShapeDtypeStruct((t, H * KD), BF16),
        name="q_prep", compiler_params=_cp())(q2, wuk_t, cos, sin)


def _q_prep_bwd(dq_cat, q2, wuk_h, cos, sin, tm=512):
    t = q2.shape[0]
    tm = min(tm, t)

    def kern(dq_ref, q_ref, w_ref, c_ref, s_ref, o_ref, dw_ref):
        @pl.when(pl.program_id(0) == 0)
        def _():
            dw_ref[...] = jnp.zeros_like(dw_ref)

        cos_, sin_ = c_ref[...], s_ref[...]
        for h in range(H):
            dql = dq_ref[:, h * KD:h * KD + KVR].astype(BF16)
            o_ref[:, h * NOPE:(h + 1) * NOPE] = _dot(dql, w_ref[h], NN).astype(BF16)
            dqr = dq_ref[:, h * KD + KVR:(h + 1) * KD]
            o_ref[:, H * NOPE + h * RP:H * NOPE + (h + 1) * RP] = _rope_t(dqr, cos_, sin_).astype(BF16)
            qn = q_ref[:, h * NOPE:(h + 1) * NOPE].astype(BF16)
            dw_ref[h] += _dot(qn, dql, TN)

    return pl.pallas_call(
        kern, grid=(t // tm,),
        in_specs=[pl.BlockSpec((tm, H * KD), lambda i: (i, 0)), pl.BlockSpec((tm, 2 * H * NOPE), lambda i: (i, 0)),
                  pl.BlockSpec((H, KVR, NOPE), lambda i: (0, 0, 0)),
                  pl.BlockSpec((tm, RP), lambda i: (i, 0)), pl.BlockSpec((tm, RP), lambda i: (i, 0))],
        out_specs=[pl.BlockSpec((tm, 2 * H * NOPE), lambda i: (i, 0)), pl.BlockSpec((H, NOPE, KVR), lambda i: (0, 0, 0))],
        out_shape=[jax.ShapeDtypeStruct((t, 2 * H * NOPE), BF16), jax.ShapeDtypeStruct((H, NOPE, KVR), F32)],
        name="q_prep_bwd", compiler_params=_cp())(dq_cat, q2, wuk_h, cos, sin)


def _o_up(o_lat, wuv_h, tm=512):
    t = o_lat.shape[0]
    tm = min(tm, t)

    def kern(x_ref, w_ref, o_ref):
        for h in range(H):
            xl = x_ref[:, h * KVR:(h + 1) * KVR].astype(BF16)
            o_ref[:, h * NOPE:(h + 1) * NOPE] = _dot(xl, w_ref[h], NN).astype(BF16)

    return pl.pallas_call(
        kern, grid=(t // tm,),
        in_specs=[pl.BlockSpec((tm, H * KVR), lambda i: (i, 0)), pl.BlockSpec((H, KVR, NOPE), lambda i: (0, 0, 0))],
        out_specs=pl.BlockSpec((tm, H * NOPE), lambda i: (i, 0)),
        out_shape=jax.ShapeDtypeStruct((t, H * NOPE), BF16),
        name="o_up", compiler_params=_cp())(o_lat, wuv_h)


def _o_up_bwd(do, o_lat, wuv_h, tm=512):
    t = do.shape[0]
    tm = min(tm, t)

    def kern(do_ref, x_ref, w_ref, dx_ref, dw_ref, dlt_ref):
        @pl.when(pl.program_id(0) == 0)
        def _():
            dw_ref[...] = jnp.zeros_like(dw_ref)

        for h in range(H):
            dh_ = do_ref[:, h * NOPE:(h + 1) * NOPE]
            x = x_ref[:, h * KVR:(h + 1) * KVR]
            dx = _dot(dh_, w_ref[h], NT)
            dx_ref[:, h * KVR:(h + 1) * KVR] = dx.astype(BF16)
            dw_ref[h] += _dot(x.astype(BF16), dh_, TN)
            dl = jnp.broadcast_to(jnp.sum(dx * x, axis=1)[:, None], (tm, 128))
            dlt_ref[h] = dl.T[0:1, :]

    return pl.pallas_call(
        kern, grid=(t // tm,),
        in_specs=[pl.BlockSpec((tm, H * NOPE), lambda i: (i, 0)), pl.BlockSpec((tm, H * KVR), lambda i: (i, 0)),
                  pl.BlockSpec((H, KVR, NOPE), lambda i: (0, 0, 0))],
        out_specs=[pl.BlockSpec((tm, H * KVR), lambda i: (i, 0)), pl.BlockSpec((H, KVR, NOPE), lambda i: (0, 0, 0)),
                   pl.BlockSpec((H, 1, tm), lambda i: (0, 0, i))],
        out_shape=[jax.ShapeDtypeStruct((t, H * KVR), BF16), jax.ShapeDtypeStruct((H, KVR, NOPE), F32),
                   jax.ShapeDtypeStruct((H, 1, t), F32)],
        name="o_up_bwd", compiler_params=_cp())(do, o_lat, wuv_h)


def _causal_pairs(nq):
    return [(i, j) for i in range(nq) for j in range(i + 1)]


def _lane_tile(stat, width):
    return jnp.tile(stat, (1, width // 128))


def _flash_fwd(qcat, kc, bq, hb, gather=None):
    t = kc.shape[0]
    nq = t // bq
    pairs = _causal_pairs(nq)
    itab = jnp.asarray(np.array([p[0] for p in pairs], np.int32))
    jtab = jnp.asarray(np.array([p[1] for p in pairs], np.int32))

    ng = H // hb
    hosting = gather is not None

    def kern(it, jt, q_ref, k_ref, *rest):
        if hosting:
            w_ref, o_ref, lset_ref, wall_ref, m_sc, l_sc, acc_sc, send_sems, recv_sems = rest
            ag_start, ag_forward, ag_finish = _allgather_schedule(w_ref, wall_ref, send_sems, recv_sems)
        else:
            o_ref, lset_ref, m_sc, l_sc, acc_sc = rest
        grp = pl.program_id(0)
        st = pl.program_id(1)
        i, j = it[st], jt[st]

        if hosting:
            @pl.when((grp == 0) & (st == 0))
            def _():
                ag_start()

        @pl.when(j == 0)
        def _():
            m_sc[...] = jnp.full_like(m_sc, NEG)
            l_sc[...] = jnp.zeros_like(l_sc)
            acc_sc[...] = jnp.zeros_like(acc_sc)

        def update(masked):
            k = k_ref[...]
            v = k[:, 0:KVR]
            if masked:
                row = lax.broadcasted_iota(jnp.int32, (bq, bq), 0)
                col = lax.broadcasted_iota(jnp.int32, (bq, bq), 1)
                keep = col <= row
            pending = [_dot(q_ref[:, hh * KD:(hh + 1) * KD], k, NT) for hh in range(min(AHEAD, hb))]
            for hh in range(hb):
                s = pending.pop(0)
                if hh + AHEAD < hb:
                    pending.append(_dot(q_ref[:, (hh + AHEAD) * KD:(hh + AHEAD + 1) * KD], k, NT))
                if masked:
                    s = jnp.where(keep, s, NEG)
                m_prev = m_sc[hh]
                m_next = jnp.maximum(m_prev, jnp.max(s, axis=1)[:, None])
                p = jnp.exp2(s - _lane_tile(m_next, bq))
                a = jnp.exp2(m_prev - m_next)
                l_sc[hh] = a * l_sc[hh] + jnp.sum(p, axis=1)[:, None]
                acc_sc[hh] = _lane_tile(a, KVR) * acc_sc[hh] + _dot(p.astype(BF16), v, NN)
                m_sc[hh] = m_next

        @pl.when(j < i)
        def _():
            update(False)

        @pl.when(j == i)
        def _():
            update(True)
            for hh in range(hb):
                l = l_sc[hh]
                o_ref[:, hh * KVR:(hh + 1) * KVR] = acc_sc[hh] / _lane_tile(l, KVR)
                lset_ref[hh] = (m_sc[hh] + jnp.log2(l)).T[0:1, :]

        if hosting:
            @pl.when((grp == ng - 1) & (st == 0))
            def _():
                ag_forward()

            @pl.when((grp == ng - 1) & (st == len(pairs) - 1))
            def _():
                ag_finish()

    in_specs = [pl.BlockSpec((bq, hb * KD), lambda g, s, it, jt: (it[s], g)),
                pl.BlockSpec((bq, KD), lambda g, s, it, jt: (jt[s], 0))]
    out_specs = [pl.BlockSpec((bq, hb * KVR), lambda g, s, it, jt: (it[s], g)),
                 pl.BlockSpec((hb, 1, bq), lambda g, s, it, jt: (g, 0, it[s]))]
    out_shape = [jax.ShapeDtypeStruct((t, H * KVR), F32), jax.ShapeDtypeStruct((H, 1, t), F32)]
    scratch = [pltpu.VMEM((hb, bq, 128), F32), pltpu.VMEM((hb, bq, 128), F32), pltpu.VMEM((hb, bq, KVR), F32)]
    args = [itab, jtab, qcat, kc]
    if hosting:
        in_specs.append(ANY)
        out_specs.append(ANY)
        out_shape.append(jax.ShapeDtypeStruct((4,) + gather.shape, gather.dtype))
        scratch += AG_SEMS
        args.append(gather)
    gs = pltpu.PrefetchScalarGridSpec(num_scalar_prefetch=2, grid=(ng, len(pairs)), in_specs=in_specs,
                                      out_specs=out_specs, scratch_shapes=scratch)
    return pl.pallas_call(kern, grid_spec=gs, out_shape=out_shape, name="mla_flash_fwd",
                          compiler_params=_cp())(*args)


def _flash_dkv(qcat, kc, do_lat, lse_t, delta_t, bq, hb, exchange=()):
    nx = len(exchange)
    t = kc.shape[0]
    nq = t // bq
    ng = H // hb
    npairs = nq * (nq + 1) // 2
    steps = [(j, g, i) for j in range(nq) for g in range(ng) for i in range(j, nq)]
    jtab = jnp.asarray(np.array([s[0] for s in steps], np.int32))
    gtab = jnp.asarray(np.array([s[1] for s in steps], np.int32))
    itab = jnp.asarray(np.array([s[2] for s in steps], np.int32))
    ptab = jnp.asarray(np.array([s[2] * (s[2] + 1) // 2 + s[0] for s in steps], np.int32))

    def kern(jt, gt, it, pt, q_ref, k_ref, do_ref, lset_ref, dlt_ref, *rest):
        p_refs, (dk_ref, ds_ref), slots_refs = rest[:nx], rest[nx:nx + 2], rest[nx + 2:2 * nx + 2]
        dk_sc, dv_sc = rest[2 * nx + 2:2 * nx + 4]
        sems = rest[2 * nx + 4:]
        hooks = [_device_exchange_schedule(p_refs[e], slots_refs[e], sems[2 * e], sems[2 * e + 1]) for e in range(nx)]
        st = pl.program_id(0)
        j, g, i = jt[st], gt[st], it[st]

        if nx:
            @pl.when(st == 0)
            def _():
                for start, _ in hooks:
                    start()

        @pl.when((g == 0) & (i == j))
        def _():
            dk_sc[...] = jnp.zeros_like(dk_sc)
            dv_sc[...] = jnp.zeros_like(dv_sc)

        def update(masked):
            k = k_ref[...]
            v = k[:, 0:KVR]
            if masked:
                row = lax.broadcasted_iota(jnp.int32, (bq, bq), 0)
                col = lax.broadcasted_iota(jnp.int32, (bq, bq), 1)
                keep = row <= col

            def first_matmuls(hh):
                dob = do_ref[:, hh * KVR:(hh + 1) * KVR].astype(BF16)
                return _dot(k, q_ref[:, hh * KD:(hh + 1) * KD], NT), _dot(v, dob, NT), dob

            pending = [first_matmuls(hh) for hh in range(min(AHEAD, hb))]
            for hh in range(hb):
                s, dp, dob = pending.pop(0)
                if hh + AHEAD < hb:
                    pending.append(first_matmuls(hh + AHEAD))
                if masked:
                    s = jnp.where(keep, s, NEG)
                p = jnp.exp2(s - lset_ref[hh])
                dv_sc[...] += _dot(p.astype(BF16), dob, NN)
                dsb = (p * (dp - dlt_ref[hh])).astype(BF16)
                ds_ref[0, 0, hh] = dsb
                dk_sc[...] += _dot(dsb, q_ref[:, hh * KD:(hh + 1) * KD], NN)

        @pl.when(i > j)
        def _():
            update(False)

        @pl.when(i == j)
        def _():
            update(True)

        @pl.when((g == ng - 1) & (i == nq - 1))
        def _():
            dk_ref[:, 0:KVR] = dk_sc[:, 0:KVR] * LN2 + dv_sc[...]
            dk_ref[:, KVR:KD] = dk_sc[:, KVR:KD] * LN2

        if nx:
            @pl.when(st == len(steps) - 1)
            def _():
                for _, finish in hooks:
                    finish()

    in_specs = [pl.BlockSpec((bq, hb * KD), lambda s, jt, gt, it, pt: (it[s], gt[s])),
                pl.BlockSpec((bq, KD), lambda s, jt, gt, it, pt: (jt[s], 0)),
                pl.BlockSpec((bq, hb * KVR), lambda s, jt, gt, it, pt: (it[s], gt[s])),
                pl.BlockSpec((hb, 1, bq), lambda s, jt, gt, it, pt: (gt[s], 0, it[s])),
                pl.BlockSpec((hb, 1, bq), lambda s, jt, gt, it, pt: (gt[s], 0, it[s]))] + [ANY] * nx
    out_specs = [pl.BlockSpec((bq, KD), lambda s, jt, gt, it, pt: (jt[s], 0)),
                 pl.BlockSpec((1, 1, hb, bq, bq), lambda s, jt, gt, it, pt: (gt[s], pt[s], 0, 0, 0))] + [ANY] * nx
    out_shape = [jax.ShapeDtypeStruct((t, KD), F32), jax.ShapeDtypeStruct((ng, npairs, hb, bq, bq), BF16)]
    out_shape += [jax.ShapeDtypeStruct((8, e.shape[1] // 2, D), e.dtype) for e in exchange]
    scratch = [pltpu.VMEM((bq, KD), F32), pltpu.VMEM((bq, KVR), F32)] + ALL_SEMS * nx
    args = [jtab, gtab, itab, ptab, qcat, kc, do_lat, lse_t, delta_t, *exchange]
    gs = pltpu.PrefetchScalarGridSpec(num_scalar_prefetch=4, grid=(len(steps),), in_specs=in_specs,
                                      out_specs=out_specs, scratch_shapes=scratch)
    return pl.pallas_call(kern, grid_spec=gs, out_shape=out_shape, name="mla_flash_dkv",
                          compiler_params=_cp())(*args)


def _flash_dq(ds_all, kc_t, bq, hb, exchange=None):
    nq = kc_t.shape[0]
    t = nq * bq
    ng = H // hb
    pairs = _causal_pairs(nq)
    itab = jnp.asarray(np.array([p[0] for p in pairs], np.int32))
    jtab = jnp.asarray(np.array([p[1] for p in pairs], np.int32))
    hosting = exchange is not None

    hh2 = hb // 2

    def kern(it, jt, dsa_ref, dsb_ref, kt_ref, *rest):
        if hosting:
            p_ref, dq_ref, slots_ref, acc_sc, send_sems, recv_sems = rest
            xc_start, xc_finish = _device_exchange_schedule(p_ref, slots_ref, send_sems, recv_sems)
        else:
            dq_ref, acc_sc = rest
        grp = pl.program_id(0)
        st = pl.program_id(1)
        i, j = it[st], jt[st]
        kt = kt_ref[...]

        def ds(hh):
            return dsa_ref[0, 0, hh] if hh < hh2 else dsb_ref[0, 0, hh - hh2]

        if hosting:
            @pl.when((grp == 0) & (st == 0))
            def _():
                xc_start()

        @pl.when(j == 0)
        def _():
            for hh in range(hb):
                acc_sc[hh] = _dot(kt, ds(hh), NN)

        @pl.when((j > 0) & (j < i))
        def _():
            for hh in range(hb):
                acc_sc[hh] += _dot(kt, ds(hh), NN)

        @pl.when(j == i)
        def _():
            for hh in range(hb):
                tot = _dot(kt, ds(hh), NN)
                tot = jnp.where(i > 0, tot + acc_sc[hh], tot)
                dq_ref[:, hh * KD:(hh + 1) * KD] = tot.T * MLA_SCALE

        if hosting:
            @pl.when((grp == ng - 1) & (st == len(pairs) - 1))
            def _():
                xc_finish()

    in_specs = [pl.BlockSpec((1, 1, hh2, bq, bq), lambda g, s, it, jt: (g, s, 0, 0, 0)),
                pl.BlockSpec((1, 1, hh2, bq, bq), lambda g, s, it, jt: (g, s, 1, 0, 0)),
                pl.BlockSpec((None, KD, bq), lambda g, s, it, jt: (jt[s], 0, 0))]
    out_specs = [pl.BlockSpec((bq, hb * KD), lambda g, s, it, jt: (it[s], g))]
    out_shape = [jax.ShapeDtypeStruct((t, H * KD), F32)]
    scratch = [pltpu.VMEM((hb, KD, bq), F32)]
    args = [itab, jtab, ds_all, ds_all, kc_t]
    if hosting:
        in_specs.append(ANY)
        out_specs.append(ANY)
        out_shape.append(jax.ShapeDtypeStruct((8, exchange.shape[1] // 2, D), exchange.dtype))
        scratch += ALL_SEMS
        args.append(exchange)
    gs = pltpu.PrefetchScalarGridSpec(num_scalar_prefetch=2, grid=(ng, len(pairs)), in_specs=in_specs,
                                      out_specs=out_specs, scratch_shapes=scratch)
    outs = pl.pallas_call(kern, grid_spec=gs, out_shape=out_shape, name="mla_flash_dq",
                          compiler_params=_cp())(*args)
    return outs if hosting else outs[0]


def _bucket_table():
    d = np.arange(WIN)
    max_exact = NBKT // 2
    nf = np.maximum(d, 1).astype(np.float32)
    large = max_exact + (np.log(nf / np.float32(max_exact)) / np.float32(math.log(WIN / max_exact))
                         * np.float32(NBKT - max_exact)).astype(np.int32)
    large = np.minimum(large, NBKT - 1)
    bucket = np.where(d < max_exact, d, large).astype(np.int32)
    jj = np.arange(2 * WIN)[:, None]
    ii = np.arange(WIN)[None, :]
    dist = ii + WIN - jj
    valid = (dist >= 0) & (dist < WIN)
    return np.where(valid, bucket[np.clip(dist, 0, WIN - 1)], -1).astype(np.int32)


def _bias_build(rel_bias, bkt):
    def kern(bk_ref, rb_ref, o_ref):
        bk = bk_ref[...]
        for hd in range(QH):
            acc = jnp.full((2 * WIN, WIN), NEG, F32)
            for b in range(NBKT):
                acc = jnp.where(bk == b, rb_ref[b, hd], acc)
            o_ref[hd] = acc

    return pl.pallas_call(
        kern, in_specs=[pl.BlockSpec(memory_space=pltpu.VMEM), pl.BlockSpec(memory_space=pltpu.SMEM)],
        out_specs=pl.BlockSpec(memory_space=pltpu.VMEM),
        out_shape=jax.ShapeDtypeStruct((QH, 2 * WIN, WIN), F32), name="swa_bias_build")(bkt, rel_bias)


def _bias_bwd(dbias, bkt):
    def kern(db_ref, bk_ref, o_ref):
        bk = bk_ref[...]
        for hd in range(QH):
            g = db_ref[hd]
            for b in range(NBKT):
                r = b * QH + hd
                o_ref[r:r + 1, :] = jnp.sum(jnp.where(bk == b, g, 0.0), axis=0, keepdims=True)

    return pl.pallas_call(
        kern, in_specs=[pl.BlockSpec(memory_space=pltpu.VMEM), pl.BlockSpec(memory_space=pltpu.VMEM)],
        out_specs=pl.BlockSpec(memory_space=pltpu.VMEM),
        out_shape=jax.ShapeDtypeStruct((NBKT * QH, WIN), F32), name="swa_bias_bwd")(dbias, bkt)


def _swa_finish_scores(raw, bias, first):
    s = raw * SWA_SCALE + bias
    if first is not None:
        row = lax.broadcasted_iota(jnp.int32, s.shape, 0)
        s = jnp.where(jnp.logical_or(jnp.logical_not(first), row >= WIN), s, NEG)
    return s


def _swa_fwd(qkv_t, bias, sinks, qb):
    t = qkv_t.shape[1]
    w = qb * WIN
    nst = t // w

    def kern(q_ref, kc_ref, kp_ref, vc_ref, vp_ref, b_ref, sk_ref, o_ref, lse_ref):
        n = pl.program_id(0)
        kfull = jnp.concatenate([kp_ref[...], kc_ref[...]], axis=1)
        vfull = jnp.concatenate([vp_ref[...], vc_ref[...]], axis=1)
        head_row = lax.broadcasted_iota(jnp.int32, (QH, WIN), 0)
        groups = [(b, kh) for b in range(qb) for kh in range(KVH)]

        def raw_scores(b, kh):
            k_band = kfull[kh * HD:(kh + 1) * HD, b * WIN:(b + 2) * WIN]
            return [_dot(k_band, q_ref[(kh * G + g) * HD:(kh * G + g + 1) * HD, b * WIN:(b + 1) * WIN], TN)
                    for g in range(G)]

        o_rows = [[] for _ in range(qb)]
        lse_tiles = [jnp.zeros((QH, WIN), F32) for _ in range(qb)]
        pending = [raw_scores(*grp) for grp in groups[:AHEAD]]
        for gi, (b, kh) in enumerate(groups):
            scores = pending.pop(0)
            if gi + AHEAD < len(groups):
                pending.append(raw_scores(*groups[gi + AHEAD]))
            v_band = vfull[kh * HD:(kh + 1) * HD, b * WIN:(b + 2) * WIN]
            for g in range(G):
                hd = kh * G + g
                s = _swa_finish_scores(scores[g], b_ref[hd], (n == 0) if b == 0 else None)
                sink = sk_ref[hd]
                m = jnp.maximum(jnp.max(s, axis=0, keepdims=True), sink)
                p = jnp.exp(s - m)
                den = jnp.sum(p, axis=0, keepdims=True) + jnp.exp(sink - m)
                p = p / den
                o_rows[b].append(_dot(v_band, p.astype(BF16), NN))
                lse_tiles[b] = jnp.where(head_row == hd, m + jnp.log(den), lse_tiles[b])
        o_ref[...] = jnp.concatenate([jnp.concatenate(rows, axis=0) for rows in o_rows], axis=1)
        lse_ref[...] = jnp.concatenate(lse_tiles, axis=1)

    prev = lambda r: (lambda n: (r, jnp.maximum(n * qb - 1, 0)))
    return pl.pallas_call(
        kern, grid=(nst,),
        in_specs=[pl.BlockSpec((QH * HD, w), lambda n: (0, n)),
                  pl.BlockSpec((KVH * HD, w), lambda n: (4, n)), pl.BlockSpec((KVH * HD, WIN), prev(4)),
                  pl.BlockSpec((KVH * HD, w), lambda n: (5, n)), pl.BlockSpec((KVH * HD, WIN), prev(5)),
                  pl.BlockSpec((QH, 2 * WIN, WIN), lambda n: (0, 0, 0)),
                  pl.BlockSpec(memory_space=pltpu.SMEM)],
        out_specs=[pl.BlockSpec((QH * HD, w), lambda n: (0, n)), pl.BlockSpec((QH, w), lambda n: (0, n))],
        out_shape=[jax.ShapeDtypeStruct((QH * HD, t), F32), jax.ShapeDtypeStruct((QH, t), F32)],
        name="swa_fwd", compiler_params=_cp())(qkv_t, qkv_t, qkv_t, qkv_t, qkv_t, bias, sinks)


def _swa_bwd(qkv_t, do_t, o_t, lse, bias, sinks, qb):
    t = qkv_t.shape[1]
    w = qb * WIN
    nst = t // w
    nblk = t // WIN

    def kern(q_ref, kc_ref, kp_ref, vc_ref, vp_ref, do_ref, o_ref, lse_ref, qn_ref, don_ref, on_ref, lsen_ref,
             b_ref, sk_ref, dqkv_ref, db_ref, dsk_ref):
        n = pl.program_id(0)

        @pl.when(n == 0)
        def _():
            db_ref[...] = jnp.zeros_like(db_ref)
            dsk_ref[...] = jnp.zeros_like(dsk_ref)

        kfull = jnp.concatenate([kp_ref[...], kc_ref[...]], axis=1)
        vfull = jnp.concatenate([vp_ref[...], vc_ref[...]], axis=1)
        head_row = lax.broadcasted_iota(jnp.int32, (QH, WIN), 0)
        db_acc = [None] * QH
        dsk_tile = jnp.zeros((QH, WIN), F32)
        prev_part = [[[None] * qb for _ in range(KVH)] for _ in range(2)]
        cur_part = [[[None] * qb for _ in range(KVH)] for _ in range(2)]
        groups = [(b, kh) for b in range(qb) for kh in range(KVH)]

        def first_matmuls(b, kh):
            k_band = kfull[kh * HD:(kh + 1) * HD, b * WIN:(b + 2) * WIN]
            v_band = vfull[kh * HD:(kh + 1) * HD, b * WIN:(b + 2) * WIN]
            out = []
            for g in range(G):
                rs = slice((kh * G + g) * HD, (kh * G + g + 1) * HD)
                dob = do_ref[rs, b * WIN:(b + 1) * WIN].astype(BF16)
                out.append((_dot(k_band, q_ref[rs, b * WIN:(b + 1) * WIN], TN), _dot(v_band, dob, TN), dob))
            return out

        dq_rows = [[] for _ in range(qb)]
        pending = [first_matmuls(*grp) for grp in groups[:AHEAD]]
        for gi, (b, kh) in enumerate(groups):
            first = pending.pop(0)
            if gi + AHEAD < len(groups):
                pending.append(first_matmuls(*groups[gi + AHEAD]))
            cs = slice(b * WIN, (b + 1) * WIN)
            k_band = kfull[kh * HD:(kh + 1) * HD, b * WIN:(b + 2) * WIN]
            dk_b = dv_b = None
            for g in range(G):
                hd = kh * G + g
                rs = slice(hd * HD, (hd + 1) * HD)
                raw, dp, dob = first[g]
                lse_h = lse_ref[hd:hd + 1, cs]
                s = _swa_finish_scores(raw, b_ref[hd], (n == 0) if b == 0 else None)
                p = jnp.exp(s - lse_h)
                dl = jnp.sum(do_ref[rs, cs] * o_ref[rs, cs], axis=0, keepdims=True)
                ds = p * (dp - dl)
                db_acc[hd] = ds if db_acc[hd] is None else db_acc[hd] + ds
                dsk_tile = jnp.where(head_row == hd, dsk_tile - jnp.exp(sk_ref[hd] - lse_h) * dl, dsk_tile)
                dss = (ds * SWA_SCALE).astype(BF16)
                dq_rows[b].append(_dot(k_band, dss, NN).astype(BF16))
                dk_h = _dot(q_ref[rs, cs], dss, NT)
                dv_h = _dot(dob, p.astype(BF16), NT)
                dk_b = dk_h if dk_b is None else dk_b + dk_h
                dv_b = dv_h if dv_b is None else dv_b + dv_h
            for which, val in ((0, dk_b), (1, dv_b)):
                prev_part[which][kh][b] = val[:, 0:WIN]
                cur_part[which][kh][b] = val[:, WIN:2 * WIN]
        dq_cols = [jnp.concatenate(rows, axis=0) for rows in dq_rows]

        live = n < nst - 1
        ls = slice((qb - 1) * WIN, qb * WIN)
        halo = [[None] * KVH for _ in range(2)]
        for kh in range(KVH):
            k_last = kc_ref[kh * HD:(kh + 1) * HD, ls]
            v_last = vc_ref[kh * HD:(kh + 1) * HD, ls]
            dk_b = dv_b = None
            for g in range(G):
                hd = kh * G + g
                rs = slice(hd * HD, (hd + 1) * HD)
                q_t = qn_ref[rs, :]
                do = don_ref[rs, :]
                s = _dot(k_last, q_t, TN) * SWA_SCALE + b_ref[hd, 0:WIN, :]
                p = jnp.exp(s - lsen_ref[hd:hd + 1, :])
                dob = do.astype(BF16)
                dp = _dot(v_last, dob, TN)
                dl = jnp.sum(do * on_ref[rs, :], axis=0, keepdims=True)
                dss = (p * (dp - dl) * SWA_SCALE).astype(BF16)
                dk_h = _dot(q_t, dss, NT)
                dv_h = _dot(dob, p.astype(BF16), NT)
                dk_b = dk_h if dk_b is None else dk_b + dk_h
                dv_b = dv_h if dv_b is None else dv_b + dv_h
            halo[0][kh] = jnp.where(live, dk_b, 0.0)
            halo[1][kh] = jnp.where(live, dv_b, 0.0)

        kv_rows = []
        for which in range(2):
            for kh in range(KVH):
                blocks = [cur_part[which][kh][p] + (prev_part[which][kh][p + 1] if p + 1 < qb else halo[which][kh])
                          for p in range(qb)]
                kv_rows.append(jnp.concatenate(blocks, axis=1))
        dqkv_ref[...] = jnp.concatenate(
            [jnp.concatenate(dq_cols, axis=1), jnp.concatenate(kv_rows, axis=0).astype(BF16)], axis=0)
        db_ref[...] += jnp.stack(db_acc)
        dsk_ref[...] += dsk_tile

    prev = lambda r: (lambda n: (r, jnp.maximum(n * qb - 1, 0)))
    nxt = lambda n: (0, jnp.minimum((n + 1) * qb, nblk - 1))
    big = lambda: pl.BlockSpec((QH * HD, w), lambda n: (0, n))
    return pl.pallas_call(
        kern, grid=(nst,),
        in_specs=[big(),
                  pl.BlockSpec((KVH * HD, w), lambda n: (4, n)), pl.BlockSpec((KVH * HD, WIN), prev(4)),
                  pl.BlockSpec((KVH * HD, w), lambda n: (5, n)), pl.BlockSpec((KVH * HD, WIN), prev(5)),
                  big(), big(), pl.BlockSpec((QH, w), lambda n: (0, n)),
                  pl.BlockSpec((QH * HD, WIN), nxt), pl.BlockSpec((QH * HD, WIN), nxt),
                  pl.BlockSpec((QH * HD, WIN), nxt), pl.BlockSpec((QH, WIN), nxt),
                  pl.BlockSpec((QH, 2 * WIN, WIN), lambda n: (0, 0, 0)),
                  pl.BlockSpec(memory_space=pltpu.SMEM)],
        out_specs=[pl.BlockSpec(((QH + 2 * KVH) * HD, w), lambda n: (0, n)),
                   pl.BlockSpec((QH, 2 * WIN, WIN), lambda n: (0, 0, 0)),
                   pl.BlockSpec((QH, WIN), lambda n: (0, 0))],
        out_shape=[jax.ShapeDtypeStruct(((QH + 2 * KVH) * HD, t), BF16),
                   jax.ShapeDtypeStruct((QH, 2 * WIN, WIN), F32), jax.ShapeDtypeStruct((QH, WIN), F32)],
        name="swa_bwd", compiler_params=_cp())(
            qkv_t, qkv_t, qkv_t, qkv_t, qkv_t, do_t, o_t, lse, qkv_t, do_t, o_t, lse, bias, sinks)


def _adamw_math(w, g, m, v):
    nm = B1 * m + (1.0 - B1) * g
    nv = B2 * v + (1.0 - B2) * (g * g)
    mhat = nm * (1.0 / (1.0 - B1 ** STEP))
    vhat = nv * (1.0 / (1.0 - B2 ** STEP))
    return -LR * (mhat / (jnp.sqrt(vhat) + ADAM_EPS) + WD * w), nm, nv


def _adamw_layers(w, m, v, g0buf, g1buf, off, name, tm=512):
    rows = w.shape[1]
    nb, ob = rows // tm, off // tm

    def kern(w_ref, m_ref, v_ref, g0_ref, g1_ref, gr_ref, d_ref, nm_ref, nv_ref):
        g_ = jnp.where(pl.program_id(0) == 0, g0_ref[...], g1_ref[...])
        gr_ref[...] = g_
        d_ref[...], nm_ref[...], nv_ref[...] = _adamw_math(w_ref[...], g_, m_ref[...], v_ref[...])

    lay = pl.BlockSpec((None, tm, D), lambda l, i: (l, i, 0))
    gsp = pl.BlockSpec((tm, D), lambda l, i: (ob + i, 0))
    return pl.pallas_call(
        kern, grid=(2, nb), in_specs=[lay, lay, lay, gsp, gsp], out_specs=[lay] * 4,
        out_shape=[jax.ShapeDtypeStruct(w.shape, F32)] * 4, name=name, compiler_params=_cp())(w, m, v, g0buf, g1buf)


def _adamw(w, g, m, v, name, tm=544):
    r = w.shape[0]
    tm = r if r % tm else tm

    def kern(w_ref, g_ref, m_ref, v_ref, d_ref, nm_ref, nv_ref):
        d_ref[...], nm_ref[...], nv_ref[...] = _adamw_math(w_ref[...], g_ref[...], m_ref[...], v_ref[...])

    row = pl.BlockSpec((tm, D), lambda i: (i, 0))
    sds = jax.ShapeDtypeStruct((r, D), F32)
    return pl.pallas_call(kern, grid=(r // tm,), in_specs=[row] * 4, out_specs=[row] * 3, out_shape=[sds] * 3,
                          name=name, compiler_params=_cp())(w, g, m, v)


def _mesh_pos():
    return lax.axis_index("x"), lax.axis_index("y"), lax.axis_index("c")


ANY = pl.BlockSpec(memory_space=pl.ANY)


AG_SEMS = [pltpu.SemaphoreType.DMA((6,)), pltpu.SemaphoreType.DMA((6,))]


def _allgather_schedule(w_ref, out_ref, send_sems, recv_sems):
    half = w_ref.shape[0] // 2
    x, y, c = _mesh_pos()
    me, sibling = (x, y, c), (x, y, 1 - c)
    chips = [(1 - x, y), (x, 1 - y), (1 - x, 1 - y)]

    def rows(px, py, pc):
        return out_ref.at[2 * px + py, pl.ds(pc * half, half), :]

    def copy(k, block, to, src=None):
        return pltpu.make_async_remote_copy(
            src_ref=rows(*block) if src is None else src, dst_ref=rows(*block),
            send_sem=send_sems.at[k], recv_sem=recv_sems.at[k], device_id=to, device_id_type=MESH)

    def first():
        return [copy(j, me, (*chip, c), src=w_ref.at[pl.ds(c * half, half), :]) for j, chip in enumerate(chips)]

    def passed():
        return [copy(3 + j, (*chip, c), sibling) for j, chip in enumerate(chips)]

    def start():
        for cp in first():
            cp.start()

    def forward():
        for j, chip in enumerate(chips):
            copy(j, (*chip, c), me).wait_recv()
            passed()[j].start()

    def finish():
        for j, chip in enumerate(chips):
            copy(3 + j, (*chip, 1 - c), me).wait_recv()
        for cp in first() + passed():
            cp.wait_send()

    return start, forward, finish


def _allgather_weights(wpack):
    def body(w_ref, out_ref, send_sems, recv_sems):
        start, forward, finish = _allgather_schedule(w_ref, out_ref, send_sems, recv_sems)
        start()
        forward()
        finish()

    return pl.pallas_call(
        body, out_shape=jax.ShapeDtypeStruct((4,) + wpack.shape, wpack.dtype), in_specs=[ANY], out_specs=ANY,
        scratch_shapes=AG_SEMS, name="allgather_weights")(wpack)


def _row_tile(rows):
    t = min(rows, 512)
    while rows % t or t % 16:
        t -= 16
    return t


ALL_SEMS = [pltpu.SemaphoreType.DMA((7,)), pltpu.SemaphoreType.DMA((7,))]


def _device_exchange_schedule(g_ref, out_ref, send_sems, recv_sems):
    half = g_ref.shape[1] // 2
    x, y, c = _mesh_pos()
    me = 4 * x + 2 * y + c
    peers = [(x ^ (k >> 2), y ^ ((k >> 1) & 1), c ^ (k & 1)) for k in range(1, 8)]

    def sends():
        return [pltpu.make_async_remote_copy(
            src_ref=g_ref.at[2 * px + py, pl.ds(pc * half, half), :], dst_ref=out_ref.at[me],
            send_sem=send_sems.at[j], recv_sem=recv_sems.at[j], device_id=(px, py, pc), device_id_type=MESH)
            for j, (px, py, pc) in enumerate(peers)]

    def start():
        for cp in sends():
            cp.start()

    def finish():
        for j, (px, py, pc) in enumerate(peers):
            pltpu.make_async_remote_copy(
                src_ref=out_ref.at[me], dst_ref=out_ref.at[4 * px + 2 * py + pc], send_sem=send_sems.at[j],
                recv_sem=recv_sems.at[j], device_id=(px, py, pc), device_id_type=MESH).wait_recv()
        for cp in sends():
            cp.wait_send()

    return start, finish


def _sum_devices(slots, g, pos, tag):
    half = slots.shape[1]
    tm = _row_tile(half)
    nb = half // tm

    def kern(pos_ref, own_ref, *refs):
        acc = own_ref[0].astype(F32)
        for s_ref in refs[:7]:
            acc = acc + s_ref[0].astype(F32)
        refs[7][...] = acc

    def slot(k):
        return pl.BlockSpec((1, tm, D), lambda i, pos: (jnp.bitwise_xor(pos[2], k), i, 0))

    gs = pltpu.PrefetchScalarGridSpec(
        num_scalar_prefetch=1, grid=(nb,),
        in_specs=[pl.BlockSpec((1, tm, D), lambda i, pos: (pos[0], pos[1] * nb + i, 0))] + [slot(k) for k in range(1, 8)],
        out_specs=pl.BlockSpec((tm, D), lambda i, pos: (pos[1] * nb + i, 0)))
    return pl.pallas_call(kern, grid_spec=gs, out_shape=jax.ShapeDtypeStruct((2 * half, D), F32),
                          name=f"rs_sum_devices_{tag}", compiler_params=_cp())(pos, g, *([slots] * 7))


def _exchange_devices(g, tag):
    def body(g_ref, out_ref, send_sems, recv_sems):
        start, finish = _device_exchange_schedule(g_ref, out_ref, send_sems, recv_sems)
        start()
        finish()

    return pl.pallas_call(
        body, out_shape=jax.ShapeDtypeStruct((8, g.shape[1] // 2, D), g.dtype), in_specs=[ANY], out_specs=ANY,
        scratch_shapes=ALL_SEMS, name=f"rs_exchange_devices_{tag}")(g)


def _reduce_scatter_finish(slots, g, pos, tag):
    return _join_core_halves(_sum_devices(slots, g, pos, tag), tag)


def _join_core_halves(r, tag):
    half = r.shape[0] // 2

    def body(r_ref, out_ref, send_sem, recv_sem):
        x, y, c = _mesh_pos()
        mine = out_ref.at[pl.ds(c * half, half), :]
        cp = pltpu.make_async_remote_copy(
            src_ref=mine, dst_ref=mine, send_sem=send_sem, recv_sem=recv_sem,
            device_id=(x, y, 1 - c), device_id_type=MESH)
        cp.start()
        theirs = out_ref.at[pl.ds((1 - c) * half, half), :]
        pltpu.make_async_remote_copy(
            src_ref=theirs, dst_ref=theirs, send_sem=send_sem, recv_sem=recv_sem,
            device_id=(x, y, 1 - c), device_id_type=MESH).wait_recv()
        cp.wait_send()

    return pl.pallas_call(
        body, out_shape=jax.ShapeDtypeStruct(r.shape, r.dtype), in_specs=[ANY], out_specs=ANY,
        input_output_aliases={0: 0},
        scratch_shapes=[pltpu.SemaphoreType.DMA, pltpu.SemaphoreType.DMA],
        name=f"rs_join_cores_{tag}")(r)


def _allreduce_small(v, name):
    def body(v_ref, out_ref, gat, send_sems, recv_sems):
        x, y, c = _mesh_pos()
        me = 4 * x + 2 * y + c
        gat[me] = v_ref[...]
        sends = []
        for k in range(1, 8):
            peer = (x ^ (k >> 2), y ^ ((k >> 1) & 1), c ^ (k & 1))
            cp = pltpu.make_async_remote_copy(
                src_ref=v_ref, dst_ref=gat.at[me], send_sem=send_sems.at[k - 1], recv_sem=recv_sems.at[k - 1],
                device_id=peer, device_id_type=MESH)
            cp.start()
            sends.append(cp)
        for k in range(1, 8):
            px, py, pc = x ^ (k >> 2), y ^ ((k >> 1) & 1), c ^ (k & 1)
            pltpu.make_async_remote_copy(
                src_ref=v_ref, dst_ref=gat.at[4 * px + 2 * py + pc], send_sem=send_sems.at[k - 1],
                recv_sem=recv_sems.at[k - 1], device_id=(px, py, pc), device_id_type=MESH).wait_recv()
        for cp in sends:
            cp.wait_send()
        acc = gat[0]
        for d in range(1, 8):
            acc = acc + gat[d]
        out_ref[...] = acc

    return pl.pallas_call(
        body, out_shape=jax.ShapeDtypeStruct(v.shape, F32),
        in_specs=[pl.BlockSpec(memory_space=pltpu.VMEM)], out_specs=pl.BlockSpec(memory_space=pltpu.VMEM),
        scratch_shapes=[pltpu.VMEM((8,) + v.shape, F32), pltpu.SemaphoreType.DMA((7,)), pltpu.SemaphoreType.DMA((7,))],
        name=name)(v)


def _mlp_fwd(xb, w_up, w_down, tag):
    a = _mm(xb, w_up[0], "nn", f"mlp_up_{tag}", out_dtype=BF16, relu2=True, b_view=("cols", w_up[1]))
    return a, _mm(a, w_down[0], "nn", f"mlp_down_{tag}", b_view=("rows", w_down[1]), tm=2048)


def _mlp_bwd(dz, dzb, xb, a, w_up, w_down, tag):
    du = _mm(dzb, w_down[0], "nt", f"mlp_down_dx_{tag}", out_dtype=BF16, gate_a=a, b_view=("rows", w_down[1]),
             tm=2048)
    gsh = _mm(xb, du, "tn", f"mlp_up_dw_{tag}", out_dtype=BF16, out_view=("cols", 2 * ROWS["mlp_w_up"], 0, None))
    gsh = _mm(a, dzb, "tn", f"mlp_down_dw_{tag}", out_dtype=BF16,
              out_view=("rows", 2 * ROWS["mlp_w_up"], ROWS["mlp_w_up"], gsh))
    dx = _mm(du, w_up[0], "nt", f"mlp_up_dx_{tag}", addend=dz, add_scale=ALPHA, b_view=("cols", w_up[1]))
    return dx, gsh


def _fwd_bwd(x, target, w, dist=None, bq=512, qb=4, hb=4):
    t = x.shape[0]
    bq = min(bq, t)
    qb = min(qb, t // WIN)
    cos, sin = _rope_tables(t)
    bkt = jnp.asarray(_bucket_table())
    w_in = jnp.pad(w[("mla_w_in", None)], ((0, 0), (0, HW - (QR + KVR + ROPE))))
    wuq = w[("mla_w_uq", None)]
    wq2 = jnp.concatenate([wuq[:, :, :NOPE].reshape(QR, H * NOPE),
                           jnp.pad(wuq[:, :, NOPE:], ((0, 0), (0, 0), (0, RP - ROPE))).reshape(QR, H * RP)], axis=1)
    wuk_t = w[("mla_w_uk", None)].transpose(1, 2, 0)
    wuk_h = w[("mla_w_uk", None)].transpose(1, 0, 2)
    wuv_h = w[("mla_w_uv", None)].transpose(1, 0, 2)
    w_o = w[("mla_w_o", None)]
    sinks = w["swa_sinks"].reshape(QH)
    lnp = lambda n, l: w[n][l]
    reduced = {}

    hh = _mm(x, w_in, "nn", "mla_in")
    cq, kc = _mla_pre(hh, w["mla_g_q"], w["mla_g_kv"], cos, sin)
    q2 = _mm(cq, wq2, "nn", "mla_uq")
    qcat = _q_prep(q2, wuk_t, cos, sin)
    if dist is None:
        o_lat, lse0_t = _flash_fwd(qcat, kc, bq, hb)
    else:
        o_lat, lse0_t, wall = _flash_fwd(qcat, kc, bq, hb, gather=dist.late_pack)
        wall = lax.dynamic_update_slice(wall, dist.late_pack[None], (dist.shard, 0, 0))
        w = {**w, **_full_from_gathered(AG_LATE, wall, dist.shard_shapes)}
    wqkv = jnp.concatenate([w[("swa_w_q", None)], w[("kv_w_shared", None)]], axis=1)
    wqkv_t = wqkv.T
    wo_s = w[("swa_w_o", None)]
    o0 = _o_up(o_lat, wuv_h)
    y0 = _mm(o0, w_o, "nn", "mla_out")
    x1b, xh1, r1 = _add_ln(x, y0, lnp("ln_mix_g", 0), lnp("ln_mix_b", 0), "ln_mix_0")
    a0, f0 = _mlp_fwd(x1b, w[("mlp_w_up", 0)], w[("mlp_w_down", 0)], 0)
    x2b, xh2, r2 = _add_ln(xh1, f0, lnp("ln_mlp_g", 0), lnp("ln_mlp_b", 0), "ln_mlp_0",
                           res_affine=(lnp("ln_mix_g", 0), lnp("ln_mix_b", 0)))
    bias = _bias_build(w["rel_bias"], bkt)
    qkv_t = _mm(x2b, wqkv, "nn", "swa_qkv", out_dtype=BF16, out_t=True)
    os_t, lse1 = _swa_fwd(qkv_t, bias, sinks, qb)
    y1 = _mm(os_t, wo_s, "tn", "swa_out")
    x3b, xh3, r3 = _add_ln(xh2, y1, lnp("ln_mix_g", 1), lnp("ln_mix_b", 1), "ln_mix_1",
                           res_affine=(lnp("ln_mlp_g", 0), lnp("ln_mlp_b", 0)))
    a1, f1 = _mlp_fwd(x3b, w[("mlp_w_up", 1)], w[("mlp_w_down", 1)], 1)
    _, xh4, r4 = _add_ln(xh3, f1, lnp("ln_mlp_g", 1), lnp("ln_mlp_b", 1), "ln_mlp_1",
                         res_affine=(lnp("ln_mix_g", 1), lnp("ln_mix_b", 1)))

    g = {}
    dz4, dz4b, dg_mlp1, db_mlp1, lpart = _ln_bwd(target, xh4, r4, lnp("ln_mlp_g", 1), "ln_mlp_1_bwd",
                                                 loss_b=lnp("ln_mlp_b", 1))
    dx3, g["mlp1"] = _mlp_bwd(dz4, dz4b, x3b, a1, w[("mlp_w_up", 1)], w[("mlp_w_down", 1)], 1)
    dz3, dz3b, dg_mix1, db_mix1 = _ln_bwd(dx3, xh3, r3, lnp("ln_mix_g", 1), "ln_mix_1_bwd")
    dos_t = _mm(dz3b, wo_s, "nt", "swa_out_dx", out_t=True)
    g[("swa_w_o", None)] = _mm(os_t, dz3b, "nn", "swa_out_dw")
    dqkv_t, dbias, dsk = _swa_bwd(qkv_t, dos_t, os_t, lse1, bias, sinks, qb)
    dwqkv = _mm(dqkv_t, x2b, "nn", "swa_qkv_dw").T
    g[("swa_w_q", None)], g[("kv_w_shared", None)] = dwqkv[:, :QH * HD], dwqkv[:, QH * HD:]
    dx2 = _mm(dqkv_t, wqkv_t, "tn", "swa_qkv_dx", addend=dz3, add_scale=ALPHA)
    g["rel_bias"] = jnp.sum(_bias_bwd(dbias, bkt), axis=-1).reshape(NBKT, QH)
    g["swa_sinks"] = jnp.sum(dsk, axis=-1).reshape(1, QH)
    dz2, dz2b, dg_mlp0, db_mlp0 = _ln_bwd(dx2, xh2, r2, lnp("ln_mlp_g", 0), "ln_mlp_0_bwd")
    dx1, g["mlp0"] = _mlp_bwd(dz2, dz2b, x1b, a0, w[("mlp_w_up", 0)], w[("mlp_w_down", 0)], 0)
    dz1, dz1b, dg_mix0, db_mix0 = _ln_bwd(dx1, xh1, r1, lnp("ln_mix_g", 0), "ln_mix_0_bwd")
    do0 = _mm(dz1b, w_o, "nt", "mla_out_dx", out_dtype=BF16)
    g[("mla_w_o", None)] = _mm(o0, dz1b, "tn", "mla_out_dw")
    do_lat, dwuv, delta_t = _o_up_bwd(do0, o_lat, wuv_h)
    g[("mla_w_uv", None)] = dwuv.transpose(1, 0, 2)
    kc_t = kc.reshape(t // bq, bq, KD).transpose(0, 2, 1)
    if dist is None:
        dk, ds_all = _flash_dkv(qcat, kc, do_lat, lse0_t, delta_t, bq, hb)
        dq_cat = _flash_dq(ds_all, kc_t, bq, hb)
    else:
        g["mid"] = _grad_shards(RS_MID, g).astype(BF16)
        dk, ds_all, slots1, slots_mid = _flash_dkv(qcat, kc, do_lat, lse0_t, delta_t, bq, hb,
                                                  exchange=(g["mlp1"], g["mid"]))
        dq_cat, slots0 = _flash_dq(ds_all, kc_t, bq, hb, exchange=g["mlp0"])
        for key, slots in (("mlp1", slots1), ("mid", slots_mid), ("mlp0", slots0)):
            reduced[key] = _reduce_scatter_finish(slots, g[key], dist.pos, key)
    dq2, dwuk = _q_prep_bwd(dq_cat, q2, wuk_h, cos, sin)
    g[("mla_w_uk", None)] = dwuk.transpose(2, 0, 1)
    dcq = _mm(dq2, wq2, "nt", "mla_uq_dx")
    dwq2 = _mm(cq, dq2, "tn", "mla_uq_dw")
    g[("mla_w_uq", None)] = jnp.concatenate([dwq2[:, :H * NOPE].reshape(QR, H, NOPE),
                                             dwq2[:, H * NOPE:].reshape(QR, H, RP)[:, :, :ROPE]], axis=2)
    dh, dgq, dgkv = _mla_pre_bwd(hh, dcq, dk, w["mla_g_q"], w["mla_g_kv"], cos, sin)
    g[("mla_w_in", None)] = _mm(x, dh, "tn", "mla_in_dw")[:, :QR + KVR + ROPE]
    grad_x = _mm(dh, w_in, "nt", "mla_in_dx", addend=dz1, add_scale=ALPHA)
    g["mla_g_q"], g["mla_g_kv"] = dgq, dgkv
    g["ln_mix_g"] = jnp.concatenate([dg_mix0, dg_mix1], axis=0)
    g["ln_mix_b"] = jnp.concatenate([db_mix0, db_mix1], axis=0)
    g["ln_mlp_g"] = jnp.concatenate([dg_mlp0, dg_mlp1], axis=0)
    g["ln_mlp_b"] = jnp.concatenate([db_mlp0, db_mlp1], axis=0)
    return lpart, grad_x, g, reduced


def _rows(a):
    return a.reshape(-1, D)


def _piece(a, layer):
    return _rows(a if layer is None else a[layer])


def _pack_group(group, parts):
    return jnp.concatenate([_piece(parts[n], l) for n, l in group], axis=0)


def _unpack_group(group, buf, like):
    out, off = {}, 0
    for n, l in group:
        shp = like[n].shape if l is None else like[n].shape[1:]
        out[(n, l)] = buf[off:off + ROWS[n]].reshape(shp)
        off += ROWS[n]
    return out


def _by_name(pieces):
    out = {n: a for (n, l), a in pieces.items() if l is None}
    for n in {n for (n, l) in pieces if l is not None}:
        out[n] = jnp.stack([pieces[(n, 0)], pieces[(n, 1)]])
    return out


def _full_from_gathered(group, wall, shard_shapes):
    out, off = {}, 0
    for n, l in group:
        shp = tuple(shard_shapes[n])
        if n in ("mlp_w_up", "mlp_w_down"):
            out[(n, l)] = (wall, off)
        elif n == "kv_w_shared":
            out[(n, l)] = wall[:, off:off + ROWS[n]].reshape((4 * shp[0],) + shp[1:])
        else:
            out[(n, l)] = wall[:, off:off + ROWS[n]].reshape((4 * shp[1],) + shp[2:])
        off += ROWS[n]
    return out


def _grad_shards(group, g):
    return jnp.concatenate([g[(n, l)].reshape(4, ROWS[n], D) for n, l in group], axis=1)


SMALL = (("ln_mix_g", 0, 2), ("ln_mix_b", 2, 2), ("ln_mlp_g", 4, 2), ("ln_mlp_b", 6, 2),
         ("swa_sinks", 8, 1), ("mla_g_q", 9, 1), ("mla_g_kv", 10, 1), ("rel_bias", 11, 1))
LOSS_ROW = 12


def _pack_small(parts, extra_row=None):
    rows = []
    for n, _, nr in SMALL:
        a = parts[n].reshape(nr, -1).astype(F32)
        rows.append(jnp.pad(a, ((0, 0), (0, D - a.shape[1]))))
    if extra_row is not None:
        rows.append(extra_row)
    rows.append(jnp.zeros((SMALL_ROWS - sum(r.shape[0] for r in rows), D), F32))
    return jnp.concatenate(rows, axis=0)


def _unpack_small(buf, like):
    out = {}
    for n, r0, nr in SMALL:
        size = like[n].size // nr
        out[n] = buf[r0:r0 + nr, :size].reshape(like[n].shape)
    return out


def kernel(x, mla_w_in, mla_g_q, mla_g_kv, mla_w_uq, mla_w_uk, mla_w_uv, mla_w_o, kv_w_shared, swa_w_q, swa_sinks, swa_w_o, rel_bias, mlp_w_up, mlp_w_down, ln_mix_g, ln_mix_b, ln_mlp_g, ln_mlp_b, loss_target, m_mla_w_in, m_mla_g_q, m_mla_g_kv, m_mla_w_uq, m_mla_w_uk, m_mla_w_uv, m_mla_w_o, m_kv_w_shared, m_swa_w_q, m_swa_sinks, m_swa_w_o, m_rel_bias, m_mlp_w_up, m_mlp_w_down, m_ln_mix_g, m_ln_mix_b, m_ln_mlp_g, m_ln_mlp_b, v_mla_w_in, v_mla_g_q, v_mla_g_kv, v_mla_w_uq, v_mla_w_uk, v_mla_w_uv, v_mla_w_o, v_kv_w_shared, v_swa_w_q, v_swa_sinks, v_swa_w_o, v_rel_bias, v_mlp_w_up, v_mlp_w_down, v_ln_mix_g, v_ln_mix_b, v_ln_mlp_g, v_ln_mlp_b):
    names = ["mla_w_in", "mla_g_q", "mla_g_kv", "mla_w_uq", "mla_w_uk", "mla_w_uv", "mla_w_o", "kv_w_shared",
             "swa_w_q", "swa_sinks", "swa_w_o", "rel_bias", "mlp_w_up", "mlp_w_down",
             "ln_mix_g", "ln_mix_b", "ln_mlp_g", "ln_mlp_b"]
    ws = dict(zip(names, [mla_w_in, mla_g_q, mla_g_kv, mla_w_uq, mla_w_uk, mla_w_uv, mla_w_o, kv_w_shared,
                          swa_w_q, swa_sinks, swa_w_o, rel_bias, mlp_w_up, mlp_w_down,
                          ln_mix_g, ln_mix_b, ln_mlp_g, ln_mlp_b]))
    ms = dict(zip(names, [m_mla_w_in, m_mla_g_q, m_mla_g_kv, m_mla_w_uq, m_mla_w_uk, m_mla_w_uv, m_mla_w_o,
                          m_kv_w_shared, m_swa_w_q, m_swa_sinks, m_swa_w_o, m_rel_bias, m_mlp_w_up, m_mlp_w_down,
                          m_ln_mix_g, m_ln_mix_b, m_ln_mlp_g, m_ln_mlp_b]))
    vs = dict(zip(names, [v_mla_w_in, v_mla_g_q, v_mla_g_kv, v_mla_w_uq, v_mla_w_uk, v_mla_w_uv, v_mla_w_o,
                          v_kv_w_shared, v_swa_w_q, v_swa_sinks, v_swa_w_o, v_rel_bias, v_mlp_w_up, v_mlp_w_down,
                          v_ln_mix_g, v_ln_mix_b, v_ln_mlp_g, v_ln_mlp_b]))
    xi, yi, ci = _mesh_pos()
    shard = 2 * xi + yi
    shard_shapes = {n: ws[n].shape for n in ROWS}
    wbf = {n: ws[n].astype(BF16) for n in ROWS}

    early = _pack_group(AG_EARLY, wbf)
    wall = lax.dynamic_update_slice(_allgather_weights(early), early[None], (shard, 0, 0))
    w = _full_from_gathered(AG_EARLY, wall, shard_shapes)
    dist = _Dist(shard=shard, pos=jnp.stack([shard, ci, 2 * shard + ci]).astype(jnp.int32),
                 late_pack=_pack_group(AG_LATE, wbf), shard_shapes=shard_shapes)
    gq_slot = lax.dynamic_update_slice(jnp.zeros((1, QR), F32), mla_g_q, (0, shard * (QR // 4)))
    gkv_slot = lax.dynamic_update_slice(jnp.zeros((1, KVR), F32), mla_g_kv, (0, shard * (KVR // 4)))
    gains = jnp.concatenate([jnp.pad(gq_slot, ((0, 0), (0, D - QR))), jnp.pad(gkv_slot, ((0, 0), (0, D - KVR))),
                             jnp.zeros((SMALL_ROWS - 2, D), F32)], axis=0)
    gains = _allreduce_small(gains * 0.5, "allgather_gains")
    w["mla_g_q"], w["mla_g_kv"] = gains[0, :QR], gains[1, :KVR]
    for n in ("swa_sinks", "rel_bias", "ln_mix_g", "ln_mix_b", "ln_mlp_g", "ln_mlp_b"):
        w[n] = ws[n]

    lpart, grad_x, g, reduced = _fwd_bwd(x[0], loss_target[0], w, dist)

    g["end"] = _grad_shards(RS_END, g).astype(BF16)
    reduced["end"] = _reduce_scatter_finish(_exchange_devices(g["end"], "end"), g["end"], dist.pos, "end")
    reduced["rest"] = jnp.concatenate([reduced["mid"], reduced["end"]], axis=0)

    small_like = {n: g[n] for n, _, _ in SMALL}
    small_sum = _allreduce_small(_pack_small(g, extra_row=lpart), "allreduce_small_grads")
    loss = 0.5 * jnp.sum(small_sum[LOSS_ROW]) / D
    gsm = _unpack_small(small_sum, small_like)
    gsm["mla_g_q"] = lax.dynamic_slice(gsm["mla_g_q"], (0, shard * (QR // 4)), (1, QR // 4))
    gsm["mla_g_kv"] = lax.dynamic_slice(gsm["mla_g_kv"], (0, shard * (KVR // 4)), (1, KVR // 4))

    gbig, dbig, mbig, vbig = {}, {}, {}, {}
    for n in ("mlp_w_up", "mlp_w_down"):
        off = 0 if n == "mlp_w_up" else ROWS["mlp_w_up"]
        gbig[n], dbig[n], mbig[n], vbig[n] = _adamw_layers(
            ws[n], ms[n], vs[n], reduced["mlp0"], reduced["mlp1"], off, f"adamw_{n}")
    rest = RS_MID + RS_END
    outs = _adamw(_pack_group(rest, ws), reduced["rest"], _pack_group(rest, ms), _pack_group(rest, vs),
                  "adamw_rest", tm=_row_tile(reduced["rest"].shape[0]))
    for dst, buf in zip((gbig, dbig, mbig, vbig), (reduced["rest"], *outs)):
        dst.update(_by_name(_unpack_group(rest, buf, ws)))
    dsm, msm, vsm = _adamw(_pack_small(ws), _pack_small(gsm), _pack_small(ms), _pack_small(vs), "adamw_small", tm=16)
    grads = {**gbig, **gsm}
    delta = {**dbig, **_unpack_small(dsm, ws)}
    new_m = {**mbig, **_unpack_small(msm, ws)}
    new_v = {**vbig, **_unpack_small(vsm, ws)}
    grads = {n: grads[n].reshape(ws[n].shape) for n in names}
    return (loss, grad_x[None], *[grads[n] for n in names], *[delta[n] for n in names],
            *[new_m[n] for n in names], *[new_v[n] for n in names])
```

```python
import collections
import math

import numpy as np
import jax
import jax.numpy as jnp
from jax import lax
from jax.experimental import pallas as pl
from jax.experimental.pallas import tpu as pltpu

F32 = jnp.float32
BF16 = jnp.bfloat16
MESH = pl.DeviceIdType.MESH

D = 1024
DFF = 4096
H = 8
NOPE = 128
ROPE = 64
QR = 384
KVR = 256
RP = 128
KD = KVR + RP
HW = 768
QH = 16
KVH = 4
HD = 64
G = QH // KVH
WIN = 128
NBKT = 32
ALPHA = 4.0 ** 0.25
LN_EPS = 1e-5
RMS_EPS = 1e-6
MLA_SCALE = (NOPE + ROPE) ** -0.5
LOG2E = 1.4426950408889634
LN2 = 0.6931471805599453
QSCALE = MLA_SCALE * LOG2E
AHEAD = 1
SWA_SCALE = HD ** -0.5
NEG = -1e30
LR, B1, B2, ADAM_EPS, WD, STEP = 0.001, 0.9, 0.999, 1e-8, 0.01, 10

VMEM_LIMIT = 48 * 1024 * 1024

NN = (((1,), (0,)), ((), ()))
NT = (((1,), (1,)), ((), ()))
TN = (((0,), (0,)), ((), ()))

ROWS = {"mlp_w_up": 1024, "mlp_w_down": 1024, "mla_w_o": 256, "swa_w_q": 256, "swa_w_o": 256,
        "kv_w_shared": 128, "mla_w_in": 176, "mla_w_uq": 144, "mla_w_uk": 64, "mla_w_uv": 64}
AG_EARLY = (("mla_w_in", None), ("mla_w_uq", None), ("mla_w_uk", None), ("mla_w_uv", None), ("mla_w_o", None))
AG_LATE = (("mlp_w_up", 0), ("mlp_w_up", 1), ("mlp_w_down", 0), ("mlp_w_down", 1),
           ("swa_w_q", None), ("swa_w_o", None), ("kv_w_shared", None))
RS_MID = (("mla_w_o", None), ("swa_w_q", None), ("swa_w_o", None), ("kv_w_shared", None), ("mla_w_uv", None))
RS_END = (("mla_w_in", None), ("mla_w_uq", None), ("mla_w_uk", None))
SMALL_ROWS = 16
_Dist = collections.namedtuple("_Dist", "shard pos late_pack shard_shapes")


def _cp(**kw):
    return pltpu.CompilerParams(vmem_limit_bytes=VMEM_LIMIT, **kw)


def _tile(n, pref):
    t = min(n, pref)
    while n % t:
        t -= 128
    return t


def _dot(a, b, dims):
    return lax.dot_general(a, b, dims, preferred_element_type=F32)


def _mm(a, b, mode, name, out_dtype=F32, out_t=False, addend=None, add_scale=1.0, relu2=False, gate_a=None,
        b_view=None, out_view=None, tm=1024, tn=1024, tk=1024):
    blk = 1024
    if b_view is not None:
        kind, b_off = b_view
        assert b.shape[0] == 4 and b.shape[2] == blk and b_off % blk == 0
        bshape = {("cols", "nn"): (blk, 4 * blk), ("cols", "nt"): (blk, 4 * blk),
                  ("rows", "nn"): (4 * blk, blk), ("rows", "nt"): (4 * blk, blk)}[(kind, mode)]
    else:
        bshape = b.shape
    if mode == "nn":
        (m, k), (k2, n) = a.shape, bshape
    elif mode == "nt":
        (m, k), (n, k2) = a.shape, bshape
    else:
        (k, m), (k2, n) = a.shape, bshape
    assert k == k2, (name, a.shape, b.shape)
    tm, tn, tk = _tile(m, tm), _tile(n, tn), _tile(k, tk)
    nk = k // tk
    dims = {"nn": NN, "nt": NT, "tn": TN}[mode]
    if mode == "tn":
        a_spec = pl.BlockSpec((tk, tm), lambda i, j, kk: (kk, i))
    else:
        a_spec = pl.BlockSpec((tm, tk), lambda i, j, kk: (i, kk))
    if b_view is not None:
        assert tn == blk and tk == blk
        ob = b_off // blk
        b_spec = {("cols", "nn"): pl.BlockSpec((None, tk, tn), lambda i, j, kk: (j, ob, 0)),
                  ("cols", "nt"): pl.BlockSpec((None, tn, tk), lambda i, j, kk: (kk, ob, 0)),
                  ("rows", "nn"): pl.BlockSpec((None, tk, tn), lambda i, j, kk: (kk, ob, 0)),
                  ("rows", "nt"): pl.BlockSpec((None, tn, tk), lambda i, j, kk: (j, ob, 0))}[(kind, mode)]
    elif mode == "nt":
        b_spec = pl.BlockSpec((tn, tk), lambda i, j, kk: (j, kk))
    else:
        b_spec = pl.BlockSpec((tk, tn), lambda i, j, kk: (kk, j))
    mn_spec = pl.BlockSpec((tm, tn), lambda i, j, kk: (i, j))
    ins, in_specs = [a, b], [a_spec, b_spec]
    if addend is not None:
        ins.append(addend)
        in_specs.append(mn_spec)
    if gate_a is not None:
        ins.append(gate_a)
        in_specs.append(mn_spec)
    aliases = {}
    if out_view is not None:
        okind, total_rows, o_off, buf = out_view
        assert not out_t and tm == blk and tn == blk and o_off % blk == 0
        oo = o_off // blk
        out_shape = [jax.ShapeDtypeStruct((4, total_rows, blk), out_dtype)]
        if okind == "cols":
            out_specs = [pl.BlockSpec((None, tm, tn), lambda i, j, kk: (j, oo, 0))]
        else:
            out_specs = [pl.BlockSpec((None, tm, tn), lambda i, j, kk: (i, oo, 0))]
        if buf is not None:
            aliases = {len(ins): 0}
            ins.append(buf)
            in_specs.append(pl.BlockSpec(memory_space=pl.ANY))
    elif out_t:
        out_shape = [jax.ShapeDtypeStruct((n, m), out_dtype)]
        out_specs = [pl.BlockSpec((tn, tm), lambda i, j, kk: (j, i))]
    else:
        out_shape = [jax.ShapeDtypeStruct((m, n), out_dtype)]
        out_specs = [mn_spec]
    has_add, has_gate = addend is not None, gate_a is not None

    def kern(*refs):
        a_ref, b_ref = refs[0], refs[1]
        pos = 2
        add_ref = gate_ref = None
        if has_add:
            add_ref = refs[pos]
            pos += 1
        if has_gate:
            gate_ref = refs[pos]
            pos += 1
        o_ref = refs[pos + len(aliases)]
        acc = refs[-1] if nk > 1 else None
        kk = pl.program_id(2)

        def partial():
            return _dot(a_ref[...].astype(BF16), b_ref[...].astype(BF16), dims)

        if nk > 1:
            @pl.when(kk == 0)
            def _():
                acc[...] = partial()

            @pl.when((kk > 0) & (kk < nk - 1))
            def _():
                acc[...] += partial()

        @pl.when(kk == nk - 1)
        def _():
            r = partial() + acc[...] if nk > 1 else partial()
            if has_add:
                r = r + add_scale * add_ref[...].astype(F32)
            if has_gate:
                ga = gate_ref[...].astype(F32)
                r = r * jnp.where(ga > 0.0, (2.0 * ga) * lax.rsqrt(ga), 0.0)
            if relu2:
                hh = jnp.maximum(r, 0.0)
                r = hh * hh
            if out_t:
                r = r.T
            o_ref[...] = r.astype(out_dtype)

    return pl.pallas_call(
        kern, out_shape=out_shape, grid=(m // tm, n // tn, nk), in_specs=in_specs, out_specs=out_specs,
        scratch_shapes=[pltpu.VMEM((tm, tn), F32)] if nk > 1 else [], input_output_aliases=aliases,
        name=name, compiler_params=_cp())(*ins)[0]


def _add_ln(res, y, g, b, name, res_affine=None, tm=512):
    t = res.shape[0]
    tm = min(tm, t)
    affine = res_affine is not None

    def kern(*refs):
        if affine:
            x_ref, y_ref, g_ref, b_ref, g0_ref, b0_ref, ob_ref, xh_ref, r_ref = refs
            x = x_ref[...] * g0_ref[...] + b0_ref[...]
        else:
            x_ref, y_ref, g_ref, b_ref, ob_ref, xh_ref, r_ref = refs
            x = x_ref[...]
        z = ALPHA * x + y_ref[...]
        mu = jnp.mean(z, axis=-1, keepdims=True)
        zc = z - mu
        var = jnp.mean(zc * zc, axis=-1, keepdims=True)
        r = lax.rsqrt(var + LN_EPS)
        xh = zc * r
        ob_ref[...] = (xh * g_ref[...] + b_ref[...]).astype(BF16)
        xh_ref[...] = xh
        r_ref[...] = r

    row = pl.BlockSpec((tm, D), lambda i: (i, 0))
    vec = pl.BlockSpec((1, D), lambda i: (0, 0))
    st = pl.BlockSpec((tm, 1), lambda i: (i, 0))
    ins = [res, y, g.reshape(1, D), b.reshape(1, D)]
    if affine:
        ins += [res_affine[0].reshape(1, D), res_affine[1].reshape(1, D)]
    return pl.pallas_call(
        kern, grid=(t // tm,), in_specs=[row, row] + [vec] * (len(ins) - 2), out_specs=[row, row, st],
        out_shape=[jax.ShapeDtypeStruct((t, D), BF16), jax.ShapeDtypeStruct((t, D), F32),
                   jax.ShapeDtypeStruct((t, 1), F32)],
        name=name, compiler_params=_cp())(*ins)


def _ln_bwd(dout, xhat, rstd, g, name, loss_b=None, tm=512):
    t = dout.shape[0]
    tm = min(tm, t)
    head = loss_b is not None

    def kern(*refs):
        if head:
            do_ref, xh_ref, r_ref, g_ref, b_ref, dz_ref, dzb_ref, dg_ref, db_ref, l_ref = refs
        else:
            do_ref, xh_ref, r_ref, g_ref, dz_ref, dzb_ref, dg_ref, db_ref = refs

        @pl.when(pl.program_id(0) == 0)
        def _():
            dg_ref[...] = jnp.zeros_like(dg_ref)
            db_ref[...] = jnp.zeros_like(db_ref)
            if head:
                l_ref[...] = jnp.zeros_like(l_ref)

        xh = xh_ref[...]
        if head:
            e = xh * g_ref[...] + b_ref[...] - do_ref[...]
            l_ref[...] += jnp.sum(e * e, axis=0, keepdims=True)
            do = e * (1.0 / D)
        else:
            do = do_ref[...]
        dxh = do * g_ref[...]
        m1 = jnp.mean(dxh, axis=-1, keepdims=True)
        m2 = jnp.mean(dxh * xh, axis=-1, keepdims=True)
        dz = r_ref[...] * (dxh - m1 - xh * m2)
        dz_ref[...] = dz
        dzb_ref[...] = dz.astype(BF16)
        dg_ref[...] += jnp.sum(do * xh, axis=0, keepdims=True)
        db_ref[...] += jnp.sum(do, axis=0, keepdims=True)

    row = pl.BlockSpec((tm, D), lambda i: (i, 0))
    vec = pl.BlockSpec((1, D), lambda i: (0, 0))
    st = pl.BlockSpec((tm, 1), lambda i: (i, 0))
    ins = [dout, xhat, rstd, g.reshape(1, D)] + ([loss_b.reshape(1, D)] if head else [])
    return pl.pallas_call(
        kern, grid=(t // tm,), in_specs=[row, row, st] + [vec] * (len(ins) - 3),
        out_specs=[row, row, vec, vec] + ([vec] if head else []),
        out_shape=[jax.ShapeDtypeStruct((t, D), F32), jax.ShapeDtypeStruct((t, D), BF16)]
        + [jax.ShapeDtypeStruct((1, D), F32)] * (3 if head else 2),
        name=name, compiler_params=_cp())(*ins)


def _rope_tables(t):
    half = ROPE // 2
    inv = 10000.0 ** (-jnp.arange(half, dtype=F32) / half)
    ang = jnp.arange(t).astype(F32)[:, None] * inv[None, :]
    cos, sin = jnp.cos(ang), jnp.sin(ang)
    z = jnp.zeros((t, RP - ROPE), F32)
    return jnp.concatenate([cos, cos, z], axis=1), jnp.concatenate([-sin, sin, z], axis=1)


def _swap_halves(x):
    lane = lax.broadcasted_iota(jnp.int32, x.shape, 1)
    return jnp.where(lane < ROPE // 2, pltpu.roll(x, RP - ROPE // 2, 1), pltpu.roll(x, ROPE // 2, 1))


def _rope(x, cos, sin):
    return x * cos + _swap_halves(x) * sin


def _rope_t(gy, cos, sin):
    return gy * cos + _swap_halves(gy * sin)


def _mla_pre(hh, g_q, g_kv, cos, sin, tm=512):
    t = hh.shape[0]
    tm = min(tm, t)

    def kern(h_ref, gq_ref, gkv_ref, c_ref, s_ref, cq_ref, k_ref):
        xq = h_ref[:, 0:QR]
        rq = lax.rsqrt(jnp.mean(xq * xq, axis=-1, keepdims=True) + RMS_EPS)
        cq_ref[...] = (xq * rq * gq_ref[...]).astype(BF16)
        xk = h_ref[:, QR:QR + KVR]
        rk = lax.rsqrt(jnp.mean(xk * xk, axis=-1, keepdims=True) + RMS_EPS)
        k_ref[:, 0:KVR] = (xk * rk * gkv_ref[...]).astype(BF16)
        k_ref[:, KVR:KD] = _rope(h_ref[:, QR + KVR:HW], c_ref[...], s_ref[...]).astype(BF16)

    return pl.pallas_call(
        kern, grid=(t // tm,),
        in_specs=[pl.BlockSpec((tm, HW), lambda i: (i, 0)), pl.BlockSpec((1, QR), lambda i: (0, 0)),
                  pl.BlockSpec((1, KVR), lambda i: (0, 0)), pl.BlockSpec((tm, RP), lambda i: (i, 0)),
                  pl.BlockSpec((tm, RP), lambda i: (i, 0))],
        out_specs=[pl.BlockSpec((tm, QR), lambda i: (i, 0)), pl.BlockSpec((tm, KD), lambda i: (i, 0))],
        out_shape=[jax.ShapeDtypeStruct((t, QR), BF16), jax.ShapeDtypeStruct((t, KD), BF16)],
        name="mla_pre", compiler_params=_cp())(hh, g_q.reshape(1, QR), g_kv.reshape(1, KVR), cos, sin)


def _mla_pre_bwd(hh, dcq, dk, g_q, g_kv, cos, sin, tm=512):
    t = hh.shape[0]
    tm = min(tm, t)

    def rms_bwd(x, dy, g):
        r = lax.rsqrt(jnp.mean(x * x, axis=-1, keepdims=True) + RMS_EPS)
        gdy = dy * g
        dx = r * gdy - x * (r * r * r) * jnp.mean(gdy * x, axis=-1, keepdims=True)
        return dx, jnp.sum(dy * x * r, axis=0, keepdims=True)

    def kern(h_ref, dcq_ref, dk_ref, gq_ref, gkv_ref, c_ref, s_ref, dh_ref, dgq_ref, dgkv_ref):
        @pl.when(pl.program_id(0) == 0)
        def _():
            dgq_ref[...] = jnp.zeros_like(dgq_ref)
            dgkv_ref[...] = jnp.zeros_like(dgkv_ref)

        dxq, dgq = rms_bwd(h_ref[:, 0:QR], dcq_ref[...], gq_ref[...])
        dxk, dgk = rms_bwd(h_ref[:, QR:QR + KVR], dk_ref[:, 0:KVR], gkv_ref[...])
        dh_ref[:, 0:QR] = dxq.astype(BF16)
        dh_ref[:, QR:QR + KVR] = dxk.astype(BF16)
        dh_ref[:, QR + KVR:HW] = _rope_t(dk_ref[:, KVR:KD], c_ref[...], s_ref[...]).astype(BF16)
        dgq_ref[...] += dgq
        dgkv_ref[...] += dgk

    return pl.pallas_call(
        kern, grid=(t // tm,),
        in_specs=[pl.BlockSpec((tm, HW), lambda i: (i, 0)), pl.BlockSpec((tm, QR), lambda i: (i, 0)),
                  pl.BlockSpec((tm, KD), lambda i: (i, 0)), pl.BlockSpec((1, QR), lambda i: (0, 0)),
                  pl.BlockSpec((1, KVR), lambda i: (0, 0)), pl.BlockSpec((tm, RP), lambda i: (i, 0)),
                  pl.BlockSpec((tm, RP), lambda i: (i, 0))],
        out_specs=[pl.BlockSpec((tm, HW), lambda i: (i, 0)), pl.BlockSpec((1, QR), lambda i: (0, 0)),
                   pl.BlockSpec((1, KVR), lambda i: (0, 0))],
        out_shape=[jax.ShapeDtypeStruct((t, HW), BF16), jax.ShapeDtypeStruct((1, QR), F32),
                   jax.ShapeDtypeStruct((1, KVR), F32)],
        name="mla_pre_bwd", compiler_params=_cp())(hh, dcq, dk, g_q.reshape(1, QR), g_kv.reshape(1, KVR), cos, sin)


def _q_prep(q2, wuk_t, cos, sin, tm=512):
    t = q2.shape[0]
    tm = min(tm, t)

    def kern(q_ref, w_ref, c_ref, s_ref, o_ref):
        cos_, sin_ = c_ref[...], s_ref[...]
        for h in range(H):
            qn = q_ref[:, h * NOPE:(h + 1) * NOPE].astype(BF16)
            o_ref[:, h * KD:h * KD + KVR] = (_dot(qn, w_ref[h], NN) * QSCALE).astype(BF16)
            qr = q_ref[:, H * NOPE + h * RP:H * NOPE + (h + 1) * RP]
            o_ref[:, h * KD + KVR:(h + 1) * KD] = (_rope(qr, cos_, sin_) * QSCALE).astype(BF16)

    return pl.pallas_call(
        kern, grid=(t // tm,),
        in_specs=[pl.BlockSpec((tm, 2 * H * NOPE), lambda i: (i, 0)), pl.BlockSpec((H, NOPE, KVR), lambda i: (0, 0, 0)),
                  pl.BlockSpec((tm, RP), lambda i: (i, 0)), pl.BlockSpec((tm, RP), lambda i: (i, 0))],
        out_specs=pl.BlockSpec((tm, H * KD), lambda i: (i, 0)),
        out_shape=jax.ShapeDtypeStruct((t, H * KD), BF16),
        name="q_prep", compiler_params=_cp())(q2, wuk_t, cos, sin)


def _q_prep_bwd(dq_cat, q2, wuk_h, cos, sin, tm=512):
    t = q2.shape[0]
    tm = min(tm, t)

    def kern(dq_ref, q_ref, w_ref, c_ref, s_ref, o_ref, dw_ref):
        @pl.when(pl.program_id(0) == 0)
        def _():
            dw_ref[...] = jnp.zeros_like(dw_ref)

        cos_, sin_ = c_ref[...], s_ref[...]
        for h in range(H):
            dql = dq_ref[:, h * KD:h * KD + KVR].astype(BF16)
            o_ref[:, h * NOPE:(h + 1) * NOPE] = _dot(dql, w_ref[h], NN).astype(BF16)
            dqr = dq_ref[:, h * KD + KVR:(h + 1) * KD]
            o_ref[:, H * NOPE + h * RP:H * NOPE + (h + 1) * RP] = _rope_t(dqr, cos_, sin_).astype(BF16)
            qn = q_ref[:, h * NOPE:(h + 1) * NOPE].astype(BF16)
            dw_ref[h] += _dot(qn, dql, TN)

    return pl.pallas_call(
        kern, grid=(t // tm,),
        in_specs=[pl.BlockSpec((tm, H * KD), lambda i: (i, 0)), pl.BlockSpec((tm, 2 * H * NOPE), lambda i: (i, 0)),
                  pl.BlockSpec((H, KVR, NOPE), lambda i: (0, 0, 0)),
                  pl.BlockSpec((tm, RP), lambda i: (i, 0)), pl.BlockSpec((tm, RP), lambda i: (i, 0))],
        out_specs=[pl.BlockSpec((tm, 2 * H * NOPE), lambda i: (i, 0)), pl.BlockSpec((H, NOPE, KVR), lambda i: (0, 0, 0))],
        out_shape=[jax.ShapeDtypeStruct((t, 2 * H * NOPE), BF16), jax.ShapeDtypeStruct((H, NOPE, KVR), F32)],
        name="q_prep_bwd", compiler_params=_cp())(dq_cat, q2, wuk_h, cos, sin)


def _o_up(o_lat, wuv_h, tm=512):
    t = o_lat.shape[0]
    tm = min(tm, t)

    def kern(x_ref, w_ref, o_ref):
        for h in range(H):
            xl = x_ref[:, h * KVR:(h + 1) * KVR].astype(BF16)
            o_ref[:, h * NOPE:(h + 1) * NOPE] = _dot(xl, w_ref[h], NN).astype(BF16)

    return pl.pallas_call(
        kern, grid=(t // tm,),
        in_specs=[pl.BlockSpec((tm, H * KVR), lambda i: (i, 0)), pl.BlockSpec((H, KVR, NOPE), lambda i: (0, 0, 0))],
        out_specs=pl.BlockSpec((tm, H * NOPE), lambda i: (i, 0)),
        out_shape=jax.ShapeDtypeStruct((t, H * NOPE), BF16),
        name="o_up", compiler_params=_cp())(o_lat, wuv_h)


def _o_up_bwd(do, o_lat, wuv_h, tm=512):
    t = do.shape[0]
    tm = min(tm, t)

    def kern(do_ref, x_ref, w_ref, dx_ref, dw_ref, dlt_ref):
        @pl.when(pl.program_id(0) == 0)
        def _():
            dw_ref[...] = jnp.zeros_like(dw_ref)

        for h in range(H):
            dh_ = do_ref[:, h * NOPE:(h + 1) * NOPE]
            x = x_ref[:, h * KVR:(h + 1) * KVR]
            dx = _dot(dh_, w_ref[h], NT)
            dx_ref[:, h * KVR:(h + 1) * KVR] = dx.astype(BF16)
            dw_ref[h] += _dot(x.astype(BF16), dh_, TN)
            dl = jnp.broadcast_to(jnp.sum(dx * x, axis=1)[:, None], (tm, 128))
            dlt_ref[h] = dl.T[0:1, :]

    return pl.pallas_call(
        kern, grid=(t // tm,),
        in_specs=[pl.BlockSpec((tm, H * NOPE), lambda i: (i, 0)), pl.BlockSpec((tm, H * KVR), lambda i: (i, 0)),
                  pl.BlockSpec((H, KVR, NOPE), lambda i: (0, 0, 0))],
        out_specs=[pl.BlockSpec((tm, H * KVR), lambda i: (i, 0)), pl.BlockSpec((H, KVR, NOPE), lambda i: (0, 0, 0)),
                   pl.BlockSpec((H, 1, tm), lambda i: (0, 0, i))],
        out_shape=[jax.ShapeDtypeStruct((t, H * KVR), BF16), jax.ShapeDtypeStruct((H, KVR, NOPE), F32),
                   jax.ShapeDtypeStruct((H, 1, t), F32)],
        name="o_up_bwd", compiler_params=_cp())(do, o_lat, wuv_h)


def _causal_pairs(nq):
    return [(i, j) for i in range(nq) for j in range(i + 1)]


def _lane_tile(stat, width):
    return jnp.tile(stat, (1, width // 128))


def _flash_fwd(qcat, kc, bq, hb, gather=None):
    t = kc.shape[0]
    nq = t // bq
    pairs = _causal_pairs(nq)
    itab = jnp.asarray(np.array([p[0] for p in pairs], np.int32))
    jtab = jnp.asarray(np.array([p[1] for p in pairs], np.int32))

    ng = H // hb
    hosting = gather is not None

    def kern(it, jt, q_ref, k_ref, *rest):
        if hosting:
            w_ref, o_ref, lset_ref, wall_ref, m_sc, l_sc, acc_sc, send_sems, recv_sems = rest
            ag_start, ag_forward, ag_finish = _allgather_schedule(w_ref, wall_ref, send_sems, recv_sems)
        else:
            o_ref, lset_ref, m_sc, l_sc, acc_sc = rest
        grp = pl.program_id(0)
        st = pl.program_id(1)
        i, j = it[st], jt[st]

        if hosting:
            @pl.when((grp == 0) & (st == 0))
            def _():
                ag_start()

        @pl.when(j == 0)
        def _():
            m_sc[...] = jnp.full_like(m_sc, NEG)
            l_sc[...] = jnp.zeros_like(l_sc)
            acc_sc[...] = jnp.zeros_like(acc_sc)

        def update(masked):
            k = k_ref[...]
            v = k[:, 0:KVR]
            if masked:
                row = lax.broadcasted_iota(jnp.int32, (bq, bq), 0)
                col = lax.broadcasted_iota(jnp.int32, (bq, bq), 1)
                keep = col <= row
            pending = [_dot(q_ref[:, hh * KD:(hh + 1) * KD], k, NT) for hh in range(min(AHEAD, hb))]
            for hh in range(hb):
                s = pending.pop(0)
                if hh + AHEAD < hb:
                    pending.append(_dot(q_ref[:, (hh + AHEAD) * KD:(hh + AHEAD + 1) * KD], k, NT))
                if masked:
                    s = jnp.where(keep, s, NEG)
                m_prev = m_sc[hh]
                m_next = jnp.maximum(m_prev, jnp.max(s, axis=1)[:, None])
                p = jnp.exp2(s - _lane_tile(m_next, bq))
                a = jnp.exp2(m_prev - m_next)
                l_sc[hh] = a * l_sc[hh] + jnp.sum(p, axis=1)[:, None]
                acc_sc[hh] = _lane_tile(a, KVR) * acc_sc[hh] + _dot(p.astype(BF16), v, NN)
                m_sc[hh] = m_next

        @pl.when(j < i)
        def _():
            update(False)

        @pl.when(j == i)
        def _():
            update(True)
            for hh in range(hb):
                l = l_sc[hh]
                o_ref[:, hh * KVR:(hh + 1) * KVR] = acc_sc[hh] / _lane_tile(l, KVR)
                lset_ref[hh] = (m_sc[hh] + jnp.log2(l)).T[0:1, :]

        if hosting:
            @pl.when((grp == ng - 1) & (st == 0))
            def _():
                ag_forward()

            @pl.when((grp == ng - 1) & (st == len(pairs) - 1))
            def _():
                ag_finish()

    in_specs = [pl.BlockSpec((bq, hb * KD), lambda g, s, it, jt: (it[s], g)),
                pl.BlockSpec((bq, KD), lambda g, s, it, jt: (jt[s], 0))]
    out_specs = [pl.BlockSpec((bq, hb * KVR), lambda g, s, it, jt: (it[s], g)),
                 pl.BlockSpec((hb, 1, bq), lambda g, s, it, jt: (g, 0, it[s]))]
    out_shape = [jax.ShapeDtypeStruct((t, H * KVR), F32), jax.ShapeDtypeStruct((H, 1, t), F32)]
    scratch = [pltpu.VMEM((hb, bq, 128), F32), pltpu.VMEM((hb, bq, 128), F32), pltpu.VMEM((hb, bq, KVR), F32)]
    args = [itab, jtab, qcat, kc]
    if hosting:
        in_specs.append(ANY)
        out_specs.append(ANY)
        out_shape.append(jax.ShapeDtypeStruct((4,) + gather.shape, gather.dtype))
        scratch += AG_SEMS
        args.append(gather)
    gs = pltpu.PrefetchScalarGridSpec(num_scalar_prefetch=2, grid=(ng, len(pairs)), in_specs=in_specs,
                                      out_specs=out_specs, scratch_shapes=scratch)
    return pl.pallas_call(kern, grid_spec=gs, out_shape=out_shape, name="mla_flash_fwd",
                          compiler_params=_cp())(*args)


def _flash_dkv(qcat, kc, do_lat, lse_t, delta_t, bq, hb, exchange=()):
    nx = len(exchange)
    t = kc.shape[0]
    nq = t // bq
    ng = H // hb
    npairs = nq * (nq + 1) // 2
    steps = [(j, g, i) for j in range(nq) for g in range(ng) for i in range(j, nq)]
    jtab = jnp.asarray(np.array([s[0] for s in steps], np.int32))
    gtab = jnp.asarray(np.array([s[1] for s in steps], np.int32))
    itab = jnp.asarray(np.array([s[2] for s in steps], np.int32))
    ptab = jnp.asarray(np.array([s[2] * (s[2] + 1) // 2 + s[0] for s in steps], np.int32))

    def kern(jt, gt, it, pt, q_ref, k_ref, do_ref, lset_ref, dlt_ref, *rest):
        p_refs, (dk_ref, ds_ref), slots_refs = rest[:nx], rest[nx:nx + 2], rest[nx + 2:2 * nx + 2]
        dk_sc, dv_sc = rest[2 * nx + 2:2 * nx + 4]
        sems = rest[2 * nx + 4:]
        hooks = [_device_exchange_schedule(p_refs[e], slots_refs[e], sems[2 * e], sems[2 * e + 1]) for e in range(nx)]
        st = pl.program_id(0)
        j, g, i = jt[st], gt[st], it[st]

        if nx:
            @pl.when(st == 0)
            def _():
                for start, _ in hooks:
                    start()

        @pl.when((g == 0) & (i == j))
        def _():
            dk_sc[...] = jnp.zeros_like(dk_sc)
            dv_sc[...] = jnp.zeros_like(dv_sc)

        def update(masked):
            k = k_ref[...]
            v = k[:, 0:KVR]
            if masked:
                row = lax.broadcasted_iota(jnp.int32, (bq, bq), 0)
                col = lax.broadcasted_iota(jnp.int32, (bq, bq), 1)
                keep = row <= col

            def first_matmuls(hh):
                dob = do_ref[:, hh * KVR:(hh + 1) * KVR].astype(BF16)
                return _dot(k, q_ref[:, hh * KD:(hh + 1) * KD], NT), _dot(v, dob, NT), dob

            pending = [first_matmuls(hh) for hh in range(min(AHEAD, hb))]
            for hh in range(hb):
                s, dp, dob = pending.pop(0)
                if hh + AHEAD < hb:
                    pending.append(first_matmuls(hh + AHEAD))
                if masked:
                    s = jnp.where(keep, s, NEG)
                p = jnp.exp2(s - lset_ref[hh])
                dv_sc[...] += _dot(p.astype(BF16), dob, NN)
                dsb = (p * (dp - dlt_ref[hh])).astype(BF16)
                ds_ref[0, 0, hh] = dsb
                dk_sc[...] += _dot(dsb, q_ref[:, hh * KD:(hh + 1) * KD], NN)

        @pl.when(i > j)
        def _():
            update(False)

        @pl.when(i == j)
        def _():
            update(True)

        @pl.when((g == ng - 1) & (i == nq - 1))
        def _():
            dk_ref[:, 0:KVR] = dk_sc[:, 0:KVR] * LN2 + dv_sc[...]
            dk_ref[:, KVR:KD] = dk_sc[:, KVR:KD] * LN2

        if nx:
            @pl.when(st == len(steps) - 1)
            def _():
                for _, finish in hooks:
                    finish()

    in_specs = [pl.BlockSpec((bq, hb * KD), lambda s, jt, gt, it, pt: (it[s], gt[s])),
                pl.BlockSpec((bq, KD), lambda s, jt, gt, it, pt: (jt[s], 0)),
                pl.BlockSpec((bq, hb * KVR), lambda s, jt, gt, it, pt: (it[s], gt[s])),
                pl.BlockSpec((hb, 1, bq), lambda s, jt, gt, it, pt: (gt[s], 0, it[s])),
                pl.BlockSpec((hb, 1, bq), lambda s, jt, gt, it, pt: (gt[s], 0, it[s]))] + [ANY] * nx
    out_specs = [pl.BlockSpec((bq, KD), lambda s, jt, gt, it, pt: (jt[s], 0)),
                 pl.BlockSpec((1, 1, hb, bq, bq), lambda s, jt, gt, it, pt: (gt[s], pt[s], 0, 0, 0))] + [ANY] * nx
    out_shape = [jax.ShapeDtypeStruct((t, KD), F32), jax.ShapeDtypeStruct((ng, npairs, hb, bq, bq), BF16)]
    out_shape += [jax.ShapeDtypeStruct((8, e.shape[1] // 2, D), e.dtype) for e in exchange]
    scratch = [pltpu.VMEM((bq, KD), F32), pltpu.VMEM((bq, KVR), F32)] + ALL_SEMS * nx
    args = [jtab, gtab, itab, ptab, qcat, kc, do_lat, lse_t, delta_t, *exchange]
    gs = pltpu.PrefetchScalarGridSpec(num_scalar_prefetch=4, grid=(len(steps),), in_specs=in_specs,
                                      out_specs=out_specs, scratch_shapes=scratch)
    return pl.pallas_call(kern, grid_spec=gs, out_shape=out_shape, name="mla_flash_dkv",
                          compiler_params=_cp())(*args)


def _flash_dq(ds_all, kc_t, bq, exchange=None):
    nq = kc_t.shape[0]
    t = nq * bq
    ngrp, _, hper = ds_all.shape[:3]
    pairs = _causal_pairs(nq)
    itab = jnp.asarray(np.array([p[0] for p in pairs], np.int32))
    jtab = jnp.asarray(np.array([p[1] for p in pairs], np.int32))
    hosting = exchange is not None

    def kern(it, jt, *refs):
        ds_refs, kt_ref, rest = refs[:ngrp], refs[ngrp], refs[ngrp + 1:]
        if hosting:
            p_ref, dq_ref, slots_ref, acc_sc, send_sems, recv_sems = rest
            xc_start, xc_finish = _device_exchange_schedule(p_ref, slots_ref, send_sems, recv_sems)
        else:
            dq_ref, acc_sc = rest
        st = pl.program_id(0)
        i, j = it[st], jt[st]
        kt = kt_ref[...]

        def ds(hh):
            return ds_refs[hh // hper][0, 0, hh % hper]

        if hosting:
            @pl.when(st == 0)
            def _():
                xc_start()

        @pl.when(j == 0)
        def _():
            for hh in range(H):
                acc_sc[hh] = _dot(kt, ds(hh), NN)

        @pl.when((j > 0) & (j < i))
        def _():
            for hh in range(H):
                acc_sc[hh] += _dot(kt, ds(hh), NN)

        @pl.when(j == i)
        def _():
            for hh in range(H):
                tot = _dot(kt, ds(hh), NN)
                tot = jnp.where(i > 0, tot + acc_sc[hh], tot)
                dq_ref[:, hh * KD:(hh + 1) * KD] = tot.T * MLA_SCALE

        if hosting:
            @pl.when(st == len(pairs) - 1)
            def _():
                xc_finish()

    def group(gi):
        return pl.BlockSpec((1, 1, hper, bq, bq), lambda s, it, jt: (gi, s, 0, 0, 0))

    in_specs = [group(gi) for gi in range(ngrp)] + [pl.BlockSpec((None, KD, bq), lambda s, it, jt: (jt[s], 0, 0))]
    out_specs = [pl.BlockSpec((bq, H * KD), lambda s, it, jt: (it[s], 0))]
    out_shape = [jax.ShapeDtypeStruct((t, H * KD), F32)]
    scratch = [pltpu.VMEM((H, KD, bq), F32)]
    args = [itab, jtab] + [ds_all] * ngrp + [kc_t]
    if hosting:
        in_specs.append(ANY)
        out_specs.append(ANY)
        out_shape.append(jax.ShapeDtypeStruct((8, exchange.shape[1] // 2, D), exchange.dtype))
        scratch += ALL_SEMS
        args.append(exchange)
    gs = pltpu.PrefetchScalarGridSpec(num_scalar_prefetch=2, grid=(len(pairs),), in_specs=in_specs,
                                      out_specs=out_specs, scratch_shapes=scratch)
    outs = pl.pallas_call(kern, grid_spec=gs, out_shape=out_shape, name="mla_flash_dq",
                          compiler_params=_cp())(*args)
    return outs if hosting else outs[0]


def _bucket_table():
    d = np.arange(WIN)
    max_exact = NBKT // 2
    nf = np.maximum(d, 1).astype(np.float32)
    large = max_exact + (np.log(nf / np.float32(max_exact)) / np.float32(math.log(WIN / max_exact))
                         * np.float32(NBKT - max_exact)).astype(np.int32)
    large = np.minimum(large, NBKT - 1)
    bucket = np.where(d < max_exact, d, large).astype(np.int32)
    jj = np.arange(2 * WIN)[:, None]
    ii = np.arange(WIN)[None, :]
    dist = ii + WIN - jj
    valid = (dist >= 0) & (dist < WIN)
    return np.where(valid, bucket[np.clip(dist, 0, WIN - 1)], -1).astype(np.int32)


def _bias_build(rel_bias, bkt):
    def kern(bk_ref, rb_ref, o_ref):
        bk = bk_ref[...]
        for hd in range(QH):
            acc = jnp.full((2 * WIN, WIN), NEG, F32)
            for b in range(NBKT):
                acc = jnp.where(bk == b, rb_ref[b, hd], acc)
            o_ref[hd] = acc

    return pl.pallas_call(
        kern, in_specs=[pl.BlockSpec(memory_space=pltpu.VMEM), pl.BlockSpec(memory_space=pltpu.SMEM)],
        out_specs=pl.BlockSpec(memory_space=pltpu.VMEM),
        out_shape=jax.ShapeDtypeStruct((QH, 2 * WIN, WIN), F32), name="swa_bias_build")(bkt, rel_bias)


def _bias_bwd(dbias, bkt):
    def kern(db_ref, bk_ref, o_ref):
        bk = bk_ref[...]
        for hd in range(QH):
            g = db_ref[hd]
            for b in range(NBKT):
                r = b * QH + hd
                o_ref[r:r + 1, :] = jnp.sum(jnp.where(bk == b, g, 0.0), axis=0, keepdims=True)

    return pl.pallas_call(
        kern, in_specs=[pl.BlockSpec(memory_space=pltpu.VMEM), pl.BlockSpec(memory_space=pltpu.VMEM)],
        out_specs=pl.BlockSpec(memory_space=pltpu.VMEM),
        out_shape=jax.ShapeDtypeStruct((NBKT * QH, WIN), F32), name="swa_bias_bwd")(dbias, bkt)


def _swa_finish_scores(raw, bias, first):
    s = raw * SWA_SCALE + bias
    if first is not None:
        row = lax.broadcasted_iota(jnp.int32, s.shape, 0)
        s = jnp.where(jnp.logical_or(jnp.logical_not(first), row >= WIN), s, NEG)
    return s


def _swa_fwd(qkv_t, bias, sinks, qb):
    t = qkv_t.shape[1]
    w = qb * WIN
    nst = t // w

    def kern(q_ref, kc_ref, kp_ref, vc_ref, vp_ref, b_ref, sk_ref, o_ref, lse_ref):
        n = pl.program_id(0)
        kfull = jnp.concatenate([kp_ref[...], kc_ref[...]], axis=1)
        vfull = jnp.concatenate([vp_ref[...], vc_ref[...]], axis=1)
        head_row = lax.broadcasted_iota(jnp.int32, (QH, WIN), 0)
        groups = [(b, kh) for b in range(qb) for kh in range(KVH)]

        def raw_scores(b, kh):
            k_band = kfull[kh * HD:(kh + 1) * HD, b * WIN:(b + 2) * WIN]
            return [_dot(k_band, q_ref[(kh * G + g) * HD:(kh * G + g + 1) * HD, b * WIN:(b + 1) * WIN], TN)
                    for g in range(G)]

        o_rows = [[] for _ in range(qb)]
        lse_tiles = [jnp.zeros((QH, WIN), F32) for _ in range(qb)]
        pending = [raw_scores(*grp) for grp in groups[:AHEAD]]
        for gi, (b, kh) in enumerate(groups):
            scores = pending.pop(0)
            if gi + AHEAD < len(groups):
                pending.append(raw_scores(*groups[gi + AHEAD]))
            v_band = vfull[kh * HD:(kh + 1) * HD, b * WIN:(b + 2) * WIN]
            for g in range(G):
                hd = kh * G + g
                s = _swa_finish_scores(scores[g], b_ref[hd], (n == 0) if b == 0 else None)
                sink = sk_ref[hd]
                m = jnp.maximum(jnp.max(s, axis=0, keepdims=True), sink)
                p = jnp.exp(s - m)
                den = jnp.sum(p, axis=0, keepdims=True) + jnp.exp(sink - m)
                p = p / den
                o_rows[b].append(_dot(v_band, p.astype(BF16), NN))
                lse_tiles[b] = jnp.where(head_row == hd, m + jnp.log(den), lse_tiles[b])
        o_ref[...] = jnp.concatenate([jnp.concatenate(rows, axis=0) for rows in o_rows], axis=1)
        lse_ref[...] = jnp.concatenate(lse_tiles, axis=1)

    prev = lambda r: (lambda n: (r, jnp.maximum(n * qb - 1, 0)))
    return pl.pallas_call(
        kern, grid=(nst,),
        in_specs=[pl.BlockSpec((QH * HD, w), lambda n: (0, n)),
                  pl.BlockSpec((KVH * HD, w), lambda n: (4, n)), pl.BlockSpec((KVH * HD, WIN), prev(4)),
                  pl.BlockSpec((KVH * HD, w), lambda n: (5, n)), pl.BlockSpec((KVH * HD, WIN), prev(5)),
                  pl.BlockSpec((QH, 2 * WIN, WIN), lambda n: (0, 0, 0)),
                  pl.BlockSpec(memory_space=pltpu.SMEM)],
        out_specs=[pl.BlockSpec((QH * HD, w), lambda n: (0, n)), pl.BlockSpec((QH, w), lambda n: (0, n))],
        out_shape=[jax.ShapeDtypeStruct((QH * HD, t), F32), jax.ShapeDtypeStruct((QH, t), F32)],
        name="swa_fwd", compiler_params=_cp())(qkv_t, qkv_t, qkv_t, qkv_t, qkv_t, bias, sinks)


def _swa_bwd(qkv_t, do_t, o_t, lse, bias, sinks, qb):
    t = qkv_t.shape[1]
    w = qb * WIN
    nst = t // w
    nblk = t // WIN

    def kern(q_ref, kc_ref, kp_ref, vc_ref, vp_ref, do_ref, o_ref, lse_ref, qn_ref, don_ref, on_ref, lsen_ref,
             b_ref, sk_ref, dqkv_ref, db_ref, dsk_ref):
        n = pl.program_id(0)

        @pl.when(n == 0)
        def _():
            db_ref[...] = jnp.zeros_like(db_ref)
            dsk_ref[...] = jnp.zeros_like(dsk_ref)

        kfull = jnp.concatenate([kp_ref[...], kc_ref[...]], axis=1)
        vfull = jnp.concatenate([vp_ref[...], vc_ref[...]], axis=1)
        head_row = lax.broadcasted_iota(jnp.int32, (QH, WIN), 0)
        db_acc = [None] * QH
        dsk_tile = jnp.zeros((QH, WIN), F32)
        prev_part = [[[None] * qb for _ in range(KVH)] for _ in range(2)]
        cur_part = [[[None] * qb for _ in range(KVH)] for _ in range(2)]
        groups = [(b, kh) for b in range(qb) for kh in range(KVH)]

        def first_matmuls(b, kh):
            k_band = kfull[kh * HD:(kh + 1) * HD, b * WIN:(b + 2) * WIN]
            v_band = vfull[kh * HD:(kh + 1) * HD, b * WIN:(b + 2) * WIN]
            out = []
            for g in range(G):
                rs = slice((kh * G + g) * HD, (kh * G + g + 1) * HD)
                dob = do_ref[rs, b * WIN:(b + 1) * WIN].astype(BF16)
                out.append((_dot(k_band, q_ref[rs, b * WIN:(b + 1) * WIN], TN), _dot(v_band, dob, TN), dob))
            return out

        dq_rows = [[] for _ in range(qb)]
        pending = [first_matmuls(*grp) for grp in groups[:AHEAD]]
        for gi, (b, kh) in enumerate(groups):
            first = pending.pop(0)
            if gi + AHEAD < len(groups):
                pending.append(first_matmuls(*groups[gi + AHEAD]))
            cs = slice(b * WIN, (b + 1) * WIN)
            k_band = kfull[kh * HD:(kh + 1) * HD, b * WIN:(b + 2) * WIN]
            dk_b = dv_b = None
            for g in range(G):
                hd = kh * G + g
                rs = slice(hd * HD, (hd + 1) * HD)
                raw, dp, dob = first[g]
                lse_h = lse_ref[hd:hd + 1, cs]
                s = _swa_finish_scores(raw, b_ref[hd], (n == 0) if b == 0 else None)
                p = jnp.exp(s - lse_h)
                dl = jnp.sum(do_ref[rs, cs] * o_ref[rs, cs], axis=0, keepdims=True)
                ds = p * (dp - dl)
                db_acc[hd] = ds if db_acc[hd] is None else db_acc[hd] + ds
                dsk_tile = jnp.where(head_row == hd, dsk_tile - jnp.exp(sk_ref[hd] - lse_h) * dl, dsk_tile)
                dss = (ds * SWA_SCALE).astype(BF16)
                dq_rows[b].append(_dot(k_band, dss, NN).astype(BF16))
                dk_h = _dot(q_ref[rs, cs], dss, NT)
                dv_h = _dot(dob, p.astype(BF16), NT)
                dk_b = dk_h if dk_b is None else dk_b + dk_h
                dv_b = dv_h if dv_b is None else dv_b + dv_h
            for which, val in ((0, dk_b), (1, dv_b)):
                prev_part[which][kh][b] = val[:, 0:WIN]
                cur_part[which][kh][b] = val[:, WIN:2 * WIN]
        dq_cols = [jnp.concatenate(rows, axis=0) for rows in dq_rows]

        live = n < nst - 1
        ls = slice((qb - 1) * WIN, qb * WIN)
        halo = [[None] * KVH for _ in range(2)]
        for kh in range(KVH):
            k_last = kc_ref[kh * HD:(kh + 1) * HD, ls]
            v_last = vc_ref[kh * HD:(kh + 1) * HD, ls]
            dk_b = dv_b = None
            for g in range(G):
                hd = kh * G + g
                rs = slice(hd * HD, (hd + 1) * HD)
                q_t = qn_ref[rs, :]
                do = don_ref[rs, :]
                s = _dot(k_last, q_t, TN) * SWA_SCALE + b_ref[hd, 0:WIN, :]
                p = jnp.exp(s - lsen_ref[hd:hd + 1, :])
                dob = do.astype(BF16)
                dp = _dot(v_last, dob, TN)
                dl = jnp.sum(do * on_ref[rs, :], axis=0, keepdims=True)
                dss = (p * (dp - dl) * SWA_SCALE).astype(BF16)
                dk_h = _dot(q_t, dss, NT)
                dv_h = _dot(dob, p.astype(BF16), NT)
                dk_b = dk_h if dk_b is None else dk_b + dk_h
                dv_b = dv_h if dv_b is None else dv_b + dv_h
            halo[0][kh] = jnp.where(live, dk_b, 0.0)
            halo[1][kh] = jnp.where(live, dv_b, 0.0)

        kv_rows = []
        for which in range(2):
            for kh in range(KVH):
                blocks = [cur_part[which][kh][p] + (prev_part[which][kh][p + 1] if p + 1 < qb else halo[which][kh])
                          for p in range(qb)]
                kv_rows.append(jnp.concatenate(blocks, axis=1))
        dqkv_ref[...] = jnp.concatenate(
            [jnp.concatenate(dq_cols, axis=1), jnp.concatenate(kv_rows, axis=0).astype(BF16)], axis=0)
        db_ref[...] += jnp.stack(db_acc)
        dsk_ref[...] += dsk_tile

    prev = lambda r: (lambda n: (r, jnp.maximum(n * qb - 1, 0)))
    nxt = lambda n: (0, jnp.minimum((n + 1) * qb, nblk - 1))
    big = lambda: pl.BlockSpec((QH * HD, w), lambda n: (0, n))
    return pl.pallas_call(
        kern, grid=(nst,),
        in_specs=[big(),
                  pl.BlockSpec((KVH * HD, w), lambda n: (4, n)), pl.BlockSpec((KVH * HD, WIN), prev(4)),
                  pl.BlockSpec((KVH * HD, w), lambda n: (5, n)), pl.BlockSpec((KVH * HD, WIN), prev(5)),
                  big(), big(), pl.BlockSpec((QH, w), lambda n: (0, n)),
                  pl.BlockSpec((QH * HD, WIN), nxt), pl.BlockSpec((QH * HD, WIN), nxt),
                  pl.BlockSpec((QH * HD, WIN), nxt), pl.BlockSpec((QH, WIN), nxt),
                  pl.BlockSpec((QH, 2 * WIN, WIN), lambda n: (0, 0, 0)),
                  pl.BlockSpec(memory_space=pltpu.SMEM)],
        out_specs=[pl.BlockSpec(((QH + 2 * KVH) * HD, w), lambda n: (0, n)),
                   pl.BlockSpec((QH, 2 * WIN, WIN), lambda n: (0, 0, 0)),
                   pl.BlockSpec((QH, WIN), lambda n: (0, 0))],
        out_shape=[jax.ShapeDtypeStruct(((QH + 2 * KVH) * HD, t), BF16),
                   jax.ShapeDtypeStruct((QH, 2 * WIN, WIN), F32), jax.ShapeDtypeStruct((QH, WIN), F32)],
        name="swa_bwd", compiler_params=_cp())(
            qkv_t, qkv_t, qkv_t, qkv_t, qkv_t, do_t, o_t, lse, qkv_t, do_t, o_t, lse, bias, sinks)


def _adamw_math(w, g, m, v):
    nm = B1 * m + (1.0 - B1) * g
    nv = B2 * v + (1.0 - B2) * (g * g)
    mhat = nm * (1.0 / (1.0 - B1 ** STEP))
    vhat = nv * (1.0 / (1.0 - B2 ** STEP))
    return -LR * (mhat / (jnp.sqrt(vhat) + ADAM_EPS) + WD * w), nm, nv


def _adamw_layers(w, m, v, g0buf, g1buf, off, name, tm=512):
    rows = w.shape[1]
    nb, ob = rows // tm, off // tm

    def kern(w_ref, m_ref, v_ref, g0_ref, g1_ref, gr_ref, d_ref, nm_ref, nv_ref):
        g_ = jnp.where(pl.program_id(0) == 0, g0_ref[...], g1_ref[...])
        gr_ref[...] = g_
        d_ref[...], nm_ref[...], nv_ref[...] = _adamw_math(w_ref[...], g_, m_ref[...], v_ref[...])

    lay = pl.BlockSpec((None, tm, D), lambda l, i: (l, i, 0))
    gsp = pl.BlockSpec((tm, D), lambda l, i: (ob + i, 0))
    return pl.pallas_call(
        kern, grid=(2, nb), in_specs=[lay, lay, lay, gsp, gsp], out_specs=[lay] * 4,
        out_shape=[jax.ShapeDtypeStruct(w.shape, F32)] * 4, name=name, compiler_params=_cp())(w, m, v, g0buf, g1buf)


def _adamw(w, g, m, v, name, tm=544):
    r = w.shape[0]
    tm = r if r % tm else tm

    def kern(w_ref, g_ref, m_ref, v_ref, d_ref, nm_ref, nv_ref):
        d_ref[...], nm_ref[...], nv_ref[...] = _adamw_math(w_ref[...], g_ref[...], m_ref[...], v_ref[...])

    row = pl.BlockSpec((tm, D), lambda i: (i, 0))
    sds = jax.ShapeDtypeStruct((r, D), F32)
    return pl.pallas_call(kern, grid=(r // tm,), in_specs=[row] * 4, out_specs=[row] * 3, out_shape=[sds] * 3,
                          name=name, compiler_params=_cp())(w, g, m, v)


def _mesh_pos():
    return lax.axis_index("x"), lax.axis_index("y"), lax.axis_index("c")


ANY = pl.BlockSpec(memory_space=pl.ANY)


AG_SEMS = [pltpu.SemaphoreType.DMA((6,)), pltpu.SemaphoreType.DMA((6,))]


def _allgather_schedule(w_ref, out_ref, send_sems, recv_sems):
    half = w_ref.shape[0] // 2
    x, y, c = _mesh_pos()
    me, sibling = (x, y, c), (x, y, 1 - c)
    chips = [(1 - x, y), (x, 1 - y), (1 - x, 1 - y)]

    def rows(px, py, pc):
        return out_ref.at[2 * px + py, pl.ds(pc * half, half), :]

    def copy(k, block, to, src=None):
        return pltpu.make_async_remote_copy(
            src_ref=rows(*block) if src is None else src, dst_ref=rows(*block),
            send_sem=send_sems.at[k], recv_sem=recv_sems.at[k], device_id=to, device_id_type=MESH)

    def first():
        return [copy(j, me, (*chip, c), src=w_ref.at[pl.ds(c * half, half), :]) for j, chip in enumerate(chips)]

    def passed():
        return [copy(3 + j, (*chip, c), sibling) for j, chip in enumerate(chips)]

    def start():
        for cp in first():
            cp.start()

    def forward():
        for j, chip in enumerate(chips):
            copy(j, (*chip, c), me).wait_recv()
            passed()[j].start()

    def finish():
        for j, chip in enumerate(chips):
            copy(3 + j, (*chip, 1 - c), me).wait_recv()
        for cp in first() + passed():
            cp.wait_send()

    return start, forward, finish


def _allgather_weights(wpack):
    def body(w_ref, out_ref, send_sems, recv_sems):
        start, forward, finish = _allgather_schedule(w_ref, out_ref, send_sems, recv_sems)
        start()
        forward()
        finish()

    return pl.pallas_call(
        body, out_shape=jax.ShapeDtypeStruct((4,) + wpack.shape, wpack.dtype), in_specs=[ANY], out_specs=ANY,
        scratch_shapes=AG_SEMS, name="allgather_weights")(wpack)


def _row_tile(rows):
    t = min(rows, 512)
    while rows % t or t % 16:
        t -= 16
    return t


ALL_SEMS = [pltpu.SemaphoreType.DMA((7,)), pltpu.SemaphoreType.DMA((7,))]


def _device_exchange_schedule(g_ref, out_ref, send_sems, recv_sems):
    half = g_ref.shape[1] // 2
    x, y, c = _mesh_pos()
    me = 4 * x + 2 * y + c
    peers = [(x ^ (k >> 2), y ^ ((k >> 1) & 1), c ^ (k & 1)) for k in range(1, 8)]

    def sends():
        return [pltpu.make_async_remote_copy(
            src_ref=g_ref.at[2 * px + py, pl.ds(pc * half, half), :], dst_ref=out_ref.at[me],
            send_sem=send_sems.at[j], recv_sem=recv_sems.at[j], device_id=(px, py, pc), device_id_type=MESH)
            for j, (px, py, pc) in enumerate(peers)]

    def start():
        for cp in sends():
            cp.start()

    def finish():
        for j, (px, py, pc) in enumerate(peers):
            pltpu.make_async_remote_copy(
                src_ref=out_ref.at[me], dst_ref=out_ref.at[4 * px + 2 * py + pc], send_sem=send_sems.at[j],
                recv_sem=recv_sems.at[j], device_id=(px, py, pc), device_id_type=MESH).wait_recv()
        for cp in sends():
            cp.wait_send()

    return start, finish


def _sum_devices(slots, g, pos, tag):
    half = slots.shape[1]
    tm = _row_tile(half)
    nb = half // tm

    def kern(pos_ref, own_ref, *refs):
        acc = own_ref[0].astype(F32)
        for s_ref in refs[:7]:
            acc = acc + s_ref[0].astype(F32)
        refs[7][...] = acc

    def slot(k):
        return pl.BlockSpec((1, tm, D), lambda i, pos: (jnp.bitwise_xor(pos[2], k), i, 0))

    gs = pltpu.PrefetchScalarGridSpec(
        num_scalar_prefetch=1, grid=(nb,),
        in_specs=[pl.BlockSpec((1, tm, D), lambda i, pos: (pos[0], pos[1] * nb + i, 0))] + [slot(k) for k in range(1, 8)],
        out_specs=pl.BlockSpec((tm, D), lambda i, pos: (pos[1] * nb + i, 0)))
    return pl.pallas_call(kern, grid_spec=gs, out_shape=jax.ShapeDtypeStruct((2 * half, D), F32),
                          name=f"rs_sum_devices_{tag}", compiler_params=_cp())(pos, g, *([slots] * 7))


def _exchange_devices(g, tag):
    def body(g_ref, out_ref, send_sems, recv_sems):
        start, finish = _device_exchange_schedule(g_ref, out_ref, send_sems, recv_sems)
        start()
        finish()

    return pl.pallas_call(
        body, out_shape=jax.ShapeDtypeStruct((8, g.shape[1] // 2, D), g.dtype), in_specs=[ANY], out_specs=ANY,
        scratch_shapes=ALL_SEMS, name=f"rs_exchange_devices_{tag}")(g)


def _reduce_scatter_finish(slots, g, pos, tag):
    return _join_core_halves(_sum_devices(slots, g, pos, tag), tag)


def _join_core_halves(r, tag):
    half = r.shape[0] // 2

    def body(r_ref, out_ref, send_sem, recv_sem):
        x, y, c = _mesh_pos()
        mine = out_ref.at[pl.ds(c * half, half), :]
        cp = pltpu.make_async_remote_copy(
            src_ref=mine, dst_ref=mine, send_sem=send_sem, recv_sem=recv_sem,
            device_id=(x, y, 1 - c), device_id_type=MESH)
        cp.start()
        theirs = out_ref.at[pl.ds((1 - c) * half, half), :]
        pltpu.make_async_remote_copy(
            src_ref=theirs, dst_ref=theirs, send_sem=send_sem, recv_sem=recv_sem,
            device_id=(x, y, 1 - c), device_id_type=MESH).wait_recv()
        cp.wait_send()

    return pl.pallas_call(
        body, out_shape=jax.ShapeDtypeStruct(r.shape, r.dtype), in_specs=[ANY], out_specs=ANY,
        input_output_aliases={0: 0},
        scratch_shapes=[pltpu.SemaphoreType.DMA, pltpu.SemaphoreType.DMA],
        name=f"rs_join_cores_{tag}")(r)


def _allreduce_small(v, name):
    def body(v_ref, out_ref, gat, send_sems, recv_sems):
        x, y, c = _mesh_pos()
        me = 4 * x + 2 * y + c
        gat[me] = v_ref[...]
        sends = []
        for k in range(1, 8):
            peer = (x ^ (k >> 2), y ^ ((k >> 1) & 1), c ^ (k & 1))
            cp = pltpu.make_async_remote_copy(
                src_ref=v_ref, dst_ref=gat.at[me], send_sem=send_sems.at[k - 1], recv_sem=recv_sems.at[k - 1],
                device_id=peer, device_id_type=MESH)
            cp.start()
            sends.append(cp)
        for k in range(1, 8):
            px, py, pc = x ^ (k >> 2), y ^ ((k >> 1) & 1), c ^ (k & 1)
            pltpu.make_async_remote_copy(
                src_ref=v_ref, dst_ref=gat.at[4 * px + 2 * py + pc], send_sem=send_sems.at[k - 1],
                recv_sem=recv_sems.at[k - 1], device_id=(px, py, pc), device_id_type=MESH).wait_recv()
        for cp in sends:
            cp.wait_send()
        acc = gat[0]
        for d in range(1, 8):
            acc = acc + gat[d]
        out_ref[...] = acc

    return pl.pallas_call(
        body, out_shape=jax.ShapeDtypeStruct(v.shape, F32),
        in_specs=[pl.BlockSpec(memory_space=pltpu.VMEM)], out_specs=pl.BlockSpec(memory_space=pltpu.VMEM),
        scratch_shapes=[pltpu.VMEM((8,) + v.shape, F32), pltpu.SemaphoreType.DMA((7,)), pltpu.SemaphoreType.DMA((7,))],
        name=name)(v)


def _mlp_fwd(xb, w_up, w_down, tag):
    a = _mm(xb, w_up[0], "nn", f"mlp_up_{tag}", out_dtype=BF16, relu2=True, b_view=("cols", w_up[1]))
    return a, _mm(a, w_down[0], "nn", f"mlp_down_{tag}", b_view=("rows", w_down[1]), tm=2048)


def _mlp_bwd(dz, dzb, xb, a, w_up, w_down, tag):
    du = _mm(dzb, w_down[0], "nt", f"mlp_down_dx_{tag}", out_dtype=BF16, gate_a=a, b_view=("rows", w_down[1]),
             tm=2048)
    gsh = _mm(xb, du, "tn", f"mlp_up_dw_{tag}", out_dtype=BF16, out_view=("cols", 2 * ROWS["mlp_w_up"], 0, None))
    gsh = _mm(a, dzb, "tn", f"mlp_down_dw_{tag}", out_dtype=BF16,
              out_view=("rows", 2 * ROWS["mlp_w_up"], ROWS["mlp_w_up"], gsh))
    dx = _mm(du, w_up[0], "nt", f"mlp_up_dx_{tag}", addend=dz, add_scale=ALPHA, b_view=("cols", w_up[1]))
    return dx, gsh


def _fwd_bwd(x, target, w, dist=None, bq=512, qb=4, hb=4):
    t = x.shape[0]
    bq = min(bq, t)
    qb = min(qb, t // WIN)
    cos, sin = _rope_tables(t)
    bkt = jnp.asarray(_bucket_table())
    w_in = jnp.pad(w[("mla_w_in", None)], ((0, 0), (0, HW - (QR + KVR + ROPE))))
    wuq = w[("mla_w_uq", None)]
    wq2 = jnp.concatenate([wuq[:, :, :NOPE].reshape(QR, H * NOPE),
                           jnp.pad(wuq[:, :, NOPE:], ((0, 0), (0, 0), (0, RP - ROPE))).reshape(QR, H * RP)], axis=1)
    wuk_t = w[("mla_w_uk", None)].transpose(1, 2, 0)
    wuk_h = w[("mla_w_uk", None)].transpose(1, 0, 2)
    wuv_h = w[("mla_w_uv", None)].transpose(1, 0, 2)
    w_o = w[("mla_w_o", None)]
    sinks = w["swa_sinks"].reshape(QH)
    lnp = lambda n, l: w[n][l]
    reduced = {}

    hh = _mm(x, w_in, "nn", "mla_in")
    cq, kc = _mla_pre(hh, w["mla_g_q"], w["mla_g_kv"], cos, sin)
    q2 = _mm(cq, wq2, "nn", "mla_uq")
    qcat = _q_prep(q2, wuk_t, cos, sin)
    if dist is None:
        o_lat, lse0_t = _flash_fwd(qcat, kc, bq, hb)
    else:
        o_lat, lse0_t, wall = _flash_fwd(qcat, kc, bq, hb, gather=dist.late_pack)
        wall = lax.dynamic_update_slice(wall, dist.late_pack[None], (dist.shard, 0, 0))
        w = {**w, **_full_from_gathered(AG_LATE, wall, dist.shard_shapes)}
    wqkv = jnp.concatenate([w[("swa_w_q", None)], w[("kv_w_shared", None)]], axis=1)
    wqkv_t = wqkv.T
    wo_s = w[("swa_w_o", None)]
    o0 = _o_up(o_lat, wuv_h)
    y0 = _mm(o0, w_o, "nn", "mla_out")
    x1b, xh1, r1 = _add_ln(x, y0, lnp("ln_mix_g", 0), lnp("ln_mix_b", 0), "ln_mix_0")
    a0, f0 = _mlp_fwd(x1b, w[("mlp_w_up", 0)], w[("mlp_w_down", 0)], 0)
    x2b, xh2, r2 = _add_ln(xh1, f0, lnp("ln_mlp_g", 0), lnp("ln_mlp_b", 0), "ln_mlp_0",
                           res_affine=(lnp("ln_mix_g", 0), lnp("ln_mix_b", 0)))
    bias = _bias_build(w["rel_bias"], bkt)
    qkv_t = _mm(x2b, wqkv, "nn", "swa_qkv", out_dtype=BF16, out_t=True)
    os_t, lse1 = _swa_fwd(qkv_t, bias, sinks, qb)
    y1 = _mm(os_t, wo_s, "tn", "swa_out")
    x3b, xh3, r3 = _add_ln(xh2, y1, lnp("ln_mix_g", 1), lnp("ln_mix_b", 1), "ln_mix_1",
                           res_affine=(lnp("ln_mlp_g", 0), lnp("ln_mlp_b", 0)))
    a1, f1 = _mlp_fwd(x3b, w[("mlp_w_up", 1)], w[("mlp_w_down", 1)], 1)
    _, xh4, r4 = _add_ln(xh3, f1, lnp("ln_mlp_g", 1), lnp("ln_mlp_b", 1), "ln_mlp_1",
                         res_affine=(lnp("ln_mix_g", 1), lnp("ln_mix_b", 1)))

    g = {}
    dz4, dz4b, dg_mlp1, db_mlp1, lpart = _ln_bwd(target, xh4, r4, lnp("ln_mlp_g", 1), "ln_mlp_1_bwd",
                                                 loss_b=lnp("ln_mlp_b", 1))
    dx3, g["mlp1"] = _mlp_bwd(dz4, dz4b, x3b, a1, w[("mlp_w_up", 1)], w[("mlp_w_down", 1)], 1)
    dz3, dz3b, dg_mix1, db_mix1 = _ln_bwd(dx3, xh3, r3, lnp("ln_mix_g", 1), "ln_mix_1_bwd")
    dos_t = _mm(dz3b, wo_s, "nt", "swa_out_dx", out_t=True)
    g[("swa_w_o", None)] = _mm(os_t, dz3b, "nn", "swa_out_dw")
    dqkv_t, dbias, dsk = _swa_bwd(qkv_t, dos_t, os_t, lse1, bias, sinks, qb)
    dwqkv = _mm(dqkv_t, x2b, "nn", "swa_qkv_dw").T
    g[("swa_w_q", None)], g[("kv_w_shared", None)] = dwqkv[:, :QH * HD], dwqkv[:, QH * HD:]
    dx2 = _mm(dqkv_t, wqkv_t, "tn", "swa_qkv_dx", addend=dz3, add_scale=ALPHA)
    g["rel_bias"] = jnp.sum(_bias_bwd(dbias, bkt), axis=-1).reshape(NBKT, QH)
    g["swa_sinks"] = jnp.sum(dsk, axis=-1).reshape(1, QH)
    dz2, dz2b, dg_mlp0, db_mlp0 = _ln_bwd(dx2, xh2, r2, lnp("ln_mlp_g", 0), "ln_mlp_0_bwd")
    dx1, g["mlp0"] = _mlp_bwd(dz2, dz2b, x1b, a0, w[("mlp_w_up", 0)], w[("mlp_w_down", 0)], 0)
    dz1, dz1b, dg_mix0, db_mix0 = _ln_bwd(dx1, xh1, r1, lnp("ln_mix_g", 0), "ln_mix_0_bwd")
    do0 = _mm(dz1b, w_o, "nt", "mla_out_dx", out_dtype=BF16)
    g[("mla_w_o", None)] = _mm(o0, dz1b, "tn", "mla_out_dw")
    do_lat, dwuv, delta_t = _o_up_bwd(do0, o_lat, wuv_h)
    g[("mla_w_uv", None)] = dwuv.transpose(1, 0, 2)
    kc_t = kc.reshape(t // bq, bq, KD).transpose(0, 2, 1)
    if dist is None:
        dk, ds_all = _flash_dkv(qcat, kc, do_lat, lse0_t, delta_t, bq, hb)
        dq_cat = _flash_dq(ds_all, kc_t, bq)
    else:
        g["mid"] = _grad_shards(RS_MID, g).astype(BF16)
        dk, ds_all, slots1, slots_mid = _flash_dkv(qcat, kc, do_lat, lse0_t, delta_t, bq, hb,
                                                  exchange=(g["mlp1"], g["mid"]))
        dq_cat, slots0 = _flash_dq(ds_all, kc_t, bq, exchange=g["mlp0"])
        for key, slots in (("mlp1", slots1), ("mid", slots_mid), ("mlp0", slots0)):
            reduced[key] = _reduce_scatter_finish(slots, g[key], dist.pos, key)
    dq2, dwuk = _q_prep_bwd(dq_cat, q2, wuk_h, cos, sin)
    g[("mla_w_uk", None)] = dwuk.transpose(2, 0, 1)
    dcq = _mm(dq2, wq2, "nt", "mla_uq_dx")
    dwq2 = _mm(cq, dq2, "tn", "mla_uq_dw")
    g[("mla_w_uq", None)] = jnp.concatenate([dwq2[:, :H * NOPE].reshape(QR, H, NOPE),
                                             dwq2[:, H * NOPE:].reshape(QR, H, RP)[:, :, :ROPE]], axis=2)
    dh, dgq, dgkv = _mla_pre_bwd(hh, dcq, dk, w["mla_g_q"], w["mla_g_kv"], cos, sin)
    g[("mla_w_in", None)] = _mm(x, dh, "tn", "mla_in_dw")[:, :QR + KVR + ROPE]
    grad_x = _mm(dh, w_in, "nt", "mla_in_dx", addend=dz1, add_scale=ALPHA)
    g["mla_g_q"], g["mla_g_kv"] = dgq, dgkv
    g["ln_mix_g"] = jnp.concatenate([dg_mix0, dg_mix1], axis=0)
    g["ln_mix_b"] = jnp.concatenate([db_mix0, db_mix1], axis=0)
    g["ln_mlp_g"] = jnp.concatenate([dg_mlp0, dg_mlp1], axis=0)
    g["ln_mlp_b"] = jnp.concatenate([db_mlp0, db_mlp1], axis=0)
    return lpart, grad_x, g, reduced


def _rows(a):
    return a.reshape(-1, D)


def _piece(a, layer):
    return _rows(a if layer is None else a[layer])


def _pack_group(group, parts):
    return jnp.concatenate([_piece(parts[n], l) for n, l in group], axis=0)


def _unpack_group(group, buf, like):
    out, off = {}, 0
    for n, l in group:
        shp = like[n].shape if l is None else like[n].shape[1:]
        out[(n, l)] = buf[off:off + ROWS[n]].reshape(shp)
        off += ROWS[n]
    return out


def _by_name(pieces):
    out = {n: a for (n, l), a in pieces.items() if l is None}
    for n in {n for (n, l) in pieces if l is not None}:
        out[n] = jnp.stack([pieces[(n, 0)], pieces[(n, 1)]])
    return out


def _full_from_gathered(group, wall, shard_shapes):
    out, off = {}, 0
    for n, l in group:
        shp = tuple(shard_shapes[n])
        if n in ("mlp_w_up", "mlp_w_down"):
            out[(n, l)] = (wall, off)
        elif n == "kv_w_shared":
            out[(n, l)] = wall[:, off:off + ROWS[n]].reshape((4 * shp[0],) + shp[1:])
        else:
            out[(n, l)] = wall[:, off:off + ROWS[n]].reshape((4 * shp[1],) + shp[2:])
        off += ROWS[n]
    return out


def _grad_shards(group, g):
    return jnp.concatenate([g[(n, l)].reshape(4, ROWS[n], D) for n, l in group], axis=1)


SMALL = (("ln_mix_g", 0, 2), ("ln_mix_b", 2, 2), ("ln_mlp_g", 4, 2), ("ln_mlp_b", 6, 2),
         ("swa_sinks", 8, 1), ("mla_g_q", 9, 1), ("mla_g_kv", 10, 1), ("rel_bias", 11, 1))
LOSS_ROW = 12


def _pack_small(parts, extra_row=None):
    rows = []
    for n, _, nr in SMALL:
        a = parts[n].reshape(nr, -1).astype(F32)
        rows.append(jnp.pad(a, ((0, 0), (0, D - a.shape[1]))))
    if extra_row is not None:
        rows.append(extra_row)
    rows.append(jnp.zeros((SMALL_ROWS - sum(r.shape[0] for r in rows), D), F32))
    return jnp.concatenate(rows, axis=0)


def _unpack_small(buf, like):
    out = {}
    for n, r0, nr in SMALL:
        size = like[n].size // nr
        out[n] = buf[r0:r0 + nr, :size].reshape(like[n].shape)
    return out


def kernel(x, mla_w_in, mla_g_q, mla_g_kv, mla_w_uq, mla_w_uk, mla_w_uv, mla_w_o, kv_w_shared, swa_w_q, swa_sinks, swa_w_o, rel_bias, mlp_w_up, mlp_w_down, ln_mix_g, ln_mix_b, ln_mlp_g, ln_mlp_b, loss_target, m_mla_w_in, m_mla_g_q, m_mla_g_kv, m_mla_w_uq, m_mla_w_uk, m_mla_w_uv, m_mla_w_o, m_kv_w_shared, m_swa_w_q, m_swa_sinks, m_swa_w_o, m_rel_bias, m_mlp_w_up, m_mlp_w_down, m_ln_mix_g, m_ln_mix_b, m_ln_mlp_g, m_ln_mlp_b, v_mla_w_in, v_mla_g_q, v_mla_g_kv, v_mla_w_uq, v_mla_w_uk, v_mla_w_uv, v_mla_w_o, v_kv_w_shared, v_swa_w_q, v_swa_sinks, v_swa_w_o, v_rel_bias, v_mlp_w_up, v_mlp_w_down, v_ln_mix_g, v_ln_mix_b, v_ln_mlp_g, v_ln_mlp_b):
    names = ["mla_w_in", "mla_g_q", "mla_g_kv", "mla_w_uq", "mla_w_uk", "mla_w_uv", "mla_w_o", "kv_w_shared",
             "swa_w_q", "swa_sinks", "swa_w_o", "rel_bias", "mlp_w_up", "mlp_w_down",
             "ln_mix_g", "ln_mix_b", "ln_mlp_g", "ln_mlp_b"]
    ws = dict(zip(names, [mla_w_in, mla_g_q, mla_g_kv, mla_w_uq, mla_w_uk, mla_w_uv, mla_w_o, kv_w_shared,
                          swa_w_q, swa_sinks, swa_w_o, rel_bias, mlp_w_up, mlp_w_down,
                          ln_mix_g, ln_mix_b, ln_mlp_g, ln_mlp_b]))
    ms = dict(zip(names, [m_mla_w_in, m_mla_g_q, m_mla_g_kv, m_mla_w_uq, m_mla_w_uk, m_mla_w_uv, m_mla_w_o,
                          m_kv_w_shared, m_swa_w_q, m_swa_sinks, m_swa_w_o, m_rel_bias, m_mlp_w_up, m_mlp_w_down,
                          m_ln_mix_g, m_ln_mix_b, m_ln_mlp_g, m_ln_mlp_b]))
    vs = dict(zip(names, [v_mla_w_in, v_mla_g_q, v_mla_g_kv, v_mla_w_uq, v_mla_w_uk, v_mla_w_uv, v_mla_w_o,
                          v_kv_w_shared, v_swa_w_q, v_swa_sinks, v_swa_w_o, v_rel_bias, v_mlp_w_up, v_mlp_w_down,
                          v_ln_mix_g, v_ln_mix_b, v_ln_mlp_g, v_ln_mlp_b]))
    xi, yi, ci = _mesh_pos()
    shard = 2 * xi + yi
    shard_shapes = {n: ws[n].shape for n in ROWS}
    wbf = {n: ws[n].astype(BF16) for n in ROWS}

    early = _pack_group(AG_EARLY, wbf)
    wall = lax.dynamic_update_slice(_allgather_weights(early), early[None], (shard, 0, 0))
    w = _full_from_gathered(AG_EARLY, wall, shard_shapes)
    dist = _Dist(shard=shard, pos=jnp.stack([shard, ci, 2 * shard + ci]).astype(jnp.int32),
                 late_pack=_pack_group(AG_LATE, wbf), shard_shapes=shard_shapes)
    gq_slot = lax.dynamic_update_slice(jnp.zeros((1, QR), F32), mla_g_q, (0, shard * (QR // 4)))
    gkv_slot = lax.dynamic_update_slice(jnp.zeros((1, KVR), F32), mla_g_kv, (0, shard * (KVR // 4)))
    gains = jnp.concatenate([jnp.pad(gq_slot, ((0, 0), (0, D - QR))), jnp.pad(gkv_slot, ((0, 0), (0, D - KVR))),
                             jnp.zeros((SMALL_ROWS - 2, D), F32)], axis=0)
    gains = _allreduce_small(gains * 0.5, "allgather_gains")
    w["mla_g_q"], w["mla_g_kv"] = gains[0, :QR], gains[1, :KVR]
    for n in ("swa_sinks", "rel_bias", "ln_mix_g", "ln_mix_b", "ln_mlp_g", "ln_mlp_b"):
        w[n] = ws[n]

    lpart, grad_x, g, reduced = _fwd_bwd(x[0], loss_target[0], w, dist)

    g["end"] = _grad_shards(RS_END, g).astype(BF16)
    reduced["end"] = _reduce_scatter_finish(_exchange_devices(g["end"], "end"), g["end"], dist.pos, "end")
    reduced["rest"] = jnp.concatenate([reduced["mid"], reduced["end"]], axis=0)

    small_like = {n: g[n] for n, _, _ in SMALL}
    small_sum = _allreduce_small(_pack_small(g, extra_row=lpart), "allreduce_small_grads")
    loss = 0.5 * jnp.sum(small_sum[LOSS_ROW]) / D
    gsm = _unpack_small(small_sum, small_like)
    gsm["mla_g_q"] = lax.dynamic_slice(gsm["mla_g_q"], (0, shard * (QR // 4)), (1, QR // 4))
    gsm["mla_g_kv"] = lax.dynamic_slice(gsm["mla_g_kv"], (0, shard * (KVR // 4)), (1, KVR // 4))

    gbig, dbig, mbig, vbig = {}, {}, {}, {}
    for n in ("mlp_w_up", "mlp_w_down"):
        off = 0 if n == "mlp_w_up" else ROWS["mlp_w_up"]
        gbig[n], dbig[n], mbig[n], vbig[n] = _adamw_layers(
            ws[n], ms[n], vs[n], reduced["mlp0"], reduced["mlp1"], off, f"adamw_{n}")
    rest = RS_MID + RS_END
    outs = _adamw(_pack_group(rest, ws), reduced["rest"], _pack_group(rest, ms), _pack_group(rest, vs),
                  "adamw_rest", tm=_row_tile(reduced["rest"].shape[0]))
    for dst, buf in zip((gbig, dbig, mbig, vbig), (reduced["rest"], *outs)):
        dst.update(_by_name(_unpack_group(rest, buf, ws)))
    dsm, msm, vsm = _adamw(_pack_small(ws), _pack_small(gsm), _pack_small(ms), _pack_small(vs), "adamw_small", tm=16)
    grads = {**gbig, **gsm}
    delta = {**dbig, **_unpack_small(dsm, ws)}
    new_m = {**mbig, **_unpack_small(msm, ws)}
    new_v = {**vbig, **_unpack_small(vsm, ws)}
    grads = {n: grads[n].reshape(ws[n].shape) for n in names}
    return (loss, grad_x[None], *[grads[n] for n in names], *[delta[n] for n in names],
            *[new_m[n] for n in names], *[new_v[n] for n in names])
```

```python
import collections
import math

import numpy as np
import jax
import jax.numpy as jnp
from jax import lax
from jax.experimental import pallas as pl
from jax.experimental.pallas import tpu as pltpu

F32 = jnp.float32
BF16 = jnp.bfloat16
MESH = pl.DeviceIdType.MESH

D = 1024
DFF = 4096
H = 8
NOPE = 128
ROPE = 64
QR = 384
KVR = 256
RP = 128
KD = KVR + RP
HW = 768
QH = 16
KVH = 4
HD = 64
G = QH // KVH
WIN = 128
NBKT = 32
ALPHA = 4.0 ** 0.25
LN_EPS = 1e-5
RMS_EPS = 1e-6
MLA_SCALE = (NOPE + ROPE) ** -0.5
LOG2E = 1.4426950408889634
LN2 = 0.6931471805599453
QSCALE = MLA_SCALE * LOG2E
AHEAD = 1
SWA_SCALE = HD ** -0.5
NEG = -1e30
LR, B1, B2, ADAM_EPS, WD, STEP = 0.001, 0.9, 0.999, 1e-8, 0.01, 10

VMEM_LIMIT = 48 * 1024 * 1024

NN = (((1,), (0,)), ((), ()))
NT = (((1,), (1,)), ((), ()))
TN = (((0,), (0,)), ((), ()))

ROWS = {"mlp_w_up": 1024, "mlp_w_down": 1024, "mla_w_o": 256, "swa_w_q": 256, "swa_w_o": 256,
        "kv_w_shared": 128, "mla_w_in": 176, "mla_w_uq": 144, "mla_w_uk": 64, "mla_w_uv": 64}
AG_EARLY = (("mla_w_in", None), ("mla_w_uq", None), ("mla_w_uk", None), ("mla_w_uv", None), ("mla_w_o", None))
AG_LATE = (("mlp_w_up", 0), ("mlp_w_up", 1), ("mlp_w_down", 0), ("mlp_w_down", 1),
           ("swa_w_q", None), ("swa_w_o", None), ("kv_w_shared", None))
RS_MID = (("mla_w_o", None), ("swa_w_q", None), ("swa_w_o", None), ("kv_w_shared", None), ("mla_w_uv", None))
RS_END = (("mla_w_in", None), ("mla_w_uq", None), ("mla_w_uk", None))
SMALL_ROWS = 16
_Dist = collections.namedtuple("_Dist", "shard pos late_pack shard_shapes")


def _cp(**kw):
    return pltpu.CompilerParams(vmem_limit_bytes=VMEM_LIMIT, **kw)


def _tile(n, pref):
    t = min(n, pref)
    while n % t:
        t -= 128
    return t


def _dot(a, b, dims):
    return lax.dot_general(a, b, dims, preferred_element_type=F32)


def _mm(a, b, mode, name, out_dtype=F32, out_t=False, addend=None, add_scale=1.0, relu2=False, gate_a=None,
        b_view=None, out_view=None, tm=1024, tn=1024, tk=1024):
    blk = 1024
    if b_view is not None:
        kind, b_off = b_view
        assert b.shape[0] == 4 and b.shape[2] == blk and b_off % blk == 0
        bshape = {("cols", "nn"): (blk, 4 * blk), ("cols", "nt"): (blk, 4 * blk),
                  ("rows", "nn"): (4 * blk, blk), ("rows", "nt"): (4 * blk, blk)}[(kind, mode)]
    else:
        bshape = b.shape
    if mode == "nn":
        (m, k), (k2, n) = a.shape, bshape
    elif mode == "nt":
        (m, k), (n, k2) = a.shape, bshape
    else:
        (k, m), (k2, n) = a.shape, bshape
    assert k == k2, (name, a.shape, b.shape)
    tm, tn, tk = _tile(m, tm), _tile(n, tn), _tile(k, tk)
    nk = k // tk
    dims = {"nn": NN, "nt": NT, "tn": TN}[mode]
    if mode == "tn":
        a_spec = pl.BlockSpec((tk, tm), lambda i, j, kk: (kk, i))
    else:
        a_spec = pl.BlockSpec((tm, tk), lambda i, j, kk: (i, kk))
    if b_view is not None:
        assert tn == blk and tk == blk
        ob = b_off // blk
        b_spec = {("cols", "nn"): pl.BlockSpec((None, tk, tn), lambda i, j, kk: (j, ob, 0)),
                  ("cols", "nt"): pl.BlockSpec((None, tn, tk), lambda i, j, kk: (kk, ob, 0)),
                  ("rows", "nn"): pl.BlockSpec((None, tk, tn), lambda i, j, kk: (kk, ob, 0)),
                  ("rows", "nt"): pl.BlockSpec((None, tn, tk), lambda i, j, kk: (j, ob, 0))}[(kind, mode)]
    elif mode == "nt":
        b_spec = pl.BlockSpec((tn, tk), lambda i, j, kk: (j, kk))
    else:
        b_spec = pl.BlockSpec((tk, tn), lambda i, j, kk: (kk, j))
    mn_spec = pl.BlockSpec((tm, tn), lambda i, j, kk: (i, j))
    ins, in_specs = [a, b], [a_spec, b_spec]
    if addend is not None:
        ins.append(addend)
        in_specs.append(mn_spec)
    if gate_a is not None:
        ins.append(gate_a)
        in_specs.append(mn_spec)
    aliases = {}
    if out_view is not None:
        okind, total_rows, o_off, buf = out_view
        assert not out_t and tm == blk and tn == blk and o_off % blk == 0
        oo = o_off // blk
        out_shape = [jax.ShapeDtypeStruct((4, total_rows, blk), out_dtype)]
        if okind == "cols":
            out_specs = [pl.BlockSpec((None, tm, tn), lambda i, j, kk: (j, oo, 0))]
        else:
            out_specs = [pl.BlockSpec((None, tm, tn), lambda i, j, kk: (i, oo, 0))]
        if buf is not None:
            aliases = {len(ins): 0}
            ins.append(buf)
            in_specs.append(pl.BlockSpec(memory_space=pl.ANY))
    elif out_t:
        out_shape = [jax.ShapeDtypeStruct((n, m), out_dtype)]
        out_specs = [pl.BlockSpec((tn, tm), lambda i, j, kk: (j, i))]
    else:
        out_shape = [jax.ShapeDtypeStruct((m, n), out_dtype)]
        out_specs = [mn_spec]
    has_add, has_gate = addend is not None, gate_a is not None

    def kern(*refs):
        a_ref, b_ref = refs[0], refs[1]
        pos = 2
        add_ref = gate_ref = None
        if has_add:
            add_ref = refs[pos]
            pos += 1
        if has_gate:
            gate_ref = refs[pos]
            pos += 1
        o_ref = refs[pos + len(aliases)]
        acc = refs[-1] if nk > 1 else None
        kk = pl.program_id(2)

        def partial():
            return _dot(a_ref[...].astype(BF16), b_ref[...].astype(BF16), dims)

        if nk > 1:
            @pl.when(kk == 0)
            def _():
                acc[...] = partial()

            @pl.when((kk > 0) & (kk < nk - 1))
            def _():
                acc[...] += partial()

        @pl.when(kk == nk - 1)
        def _():
            r = partial() + acc[...] if nk > 1 else partial()
            if has_add:
                r = r + add_scale * add_ref[...].astype(F32)
            if has_gate:
                ga = gate_ref[...].astype(F32)
                r = r * jnp.where(ga > 0.0, (2.0 * ga) * lax.rsqrt(ga), 0.0)
            if relu2:
                hh = jnp.maximum(r, 0.0)
                r = hh * hh
            if out_t:
                r = r.T
            o_ref[...] = r.astype(out_dtype)

    return pl.pallas_call(
        kern, out_shape=out_shape, grid=(m // tm, n // tn, nk), in_specs=in_specs, out_specs=out_specs,
        scratch_shapes=[pltpu.VMEM((tm, tn), F32)] if nk > 1 else [], input_output_aliases=aliases,
        name=name, compiler_params=_cp())(*ins)[0]


def _add_ln(res, y, g, b, name, res_affine=None, tm=512):
    t = res.shape[0]
    tm = min(tm, t)
    affine = res_affine is not None

    def kern(*refs):
        if affine:
            x_ref, y_ref, g_ref, b_ref, g0_ref, b0_ref, ob_ref, xh_ref, r_ref = refs
            x = x_ref[...] * g0_ref[...] + b0_ref[...]
        else:
            x_ref, y_ref, g_ref, b_ref, ob_ref, xh_ref, r_ref = refs
            x = x_ref[...]
        z = ALPHA * x + y_ref[...]
        mu = jnp.mean(z, axis=-1, keepdims=True)
        zc = z - mu
        var = jnp.mean(zc * zc, axis=-1, keepdims=True)
        r = lax.rsqrt(var + LN_EPS)
        xh = zc * r
        ob_ref[...] = (xh * g_ref[...] + b_ref[...]).astype(BF16)
        xh_ref[...] = xh
        r_ref[...] = r

    row = pl.BlockSpec((tm, D), lambda i: (i, 0))
    vec = pl.BlockSpec((1, D), lambda i: (0, 0))
    st = pl.BlockSpec((tm, 1), lambda i: (i, 0))
    ins = [res, y, g.reshape(1, D), b.reshape(1, D)]
    if affine:
        ins += [res_affine[0].reshape(1, D), res_affine[1].reshape(1, D)]
    return pl.pallas_call(
        kern, grid=(t // tm,), in_specs=[row, row] + [vec] * (len(ins) - 2), out_specs=[row, row, st],
        out_shape=[jax.ShapeDtypeStruct((t, D), BF16), jax.ShapeDtypeStruct((t, D), F32),
                   jax.ShapeDtypeStruct((t, 1), F32)],
        name=name, compiler_params=_cp())(*ins)


def _ln_bwd(dout, xhat, rstd, g, name, loss_b=None, tm=512):
    t = dout.shape[0]
    tm = min(tm, t)
    head = loss_b is not None

    def kern(*refs):
        if head:
            do_ref, xh_ref, r_ref, g_ref, b_ref, dz_ref, dzb_ref, dg_ref, db_ref, l_ref = refs
        else:
            do_ref, xh_ref, r_ref, g_ref, dz_ref, dzb_ref, dg_ref, db_ref = refs

        @pl.when(pl.program_id(0) == 0)
        def _():
            dg_ref[...] = jnp.zeros_like(dg_ref)
            db_ref[...] = jnp.zeros_like(db_ref)
            if head:
                l_ref[...] = jnp.zeros_like(l_ref)

        xh = xh_ref[...]
        if head:
            e = xh * g_ref[...] + b_ref[...] - do_ref[...]
            l_ref[...] += jnp.sum(e * e, axis=0, keepdims=True)
            do = e * (1.0 / D)
        else:
            do = do_ref[...]
        dxh = do * g_ref[...]
        m1 = jnp.mean(dxh, axis=-1, keepdims=True)
        m2 = jnp.mean(dxh * xh, axis=-1, keepdims=True)
        dz = r_ref[...] * (dxh - m1 - xh * m2)
        dz_ref[...] = dz
        dzb_ref[...] = dz.astype(BF16)
        dg_ref[...] += jnp.sum(do * xh, axis=0, keepdims=True)
        db_ref[...] += jnp.sum(do, axis=0, keepdims=True)

    row = pl.BlockSpec((tm, D), lambda i: (i, 0))
    vec = pl.BlockSpec((1, D), lambda i: (0, 0))
    st = pl.BlockSpec((tm, 1), lambda i: (i, 0))
    ins = [dout, xhat, rstd, g.reshape(1, D)] + ([loss_b.reshape(1, D)] if head else [])
    return pl.pallas_call(
        kern, grid=(t // tm,), in_specs=[row, row, st] + [vec] * (len(ins) - 3),
        out_specs=[row, row, vec, vec] + ([vec] if head else []),
        out_shape=[jax.ShapeDtypeStruct((t, D), F32), jax.ShapeDtypeStruct((t, D), BF16)]
        + [jax.ShapeDtypeStruct((1, D), F32)] * (3 if head else 2),
        name=name, compiler_params=_cp())(*ins)


def _rope_tables(t):
    half = ROPE // 2
    inv = 10000.0 ** (-jnp.arange(half, dtype=F32) / half)
    ang = jnp.arange(t).astype(F32)[:, None] * inv[None, :]
    cos, sin = jnp.cos(ang), jnp.sin(ang)
    z = jnp.zeros((t, RP - ROPE), F32)
    return jnp.concatenate([cos, cos, z], axis=1), jnp.concatenate([-sin, sin, z], axis=1)


def _swap_halves(x):
    lane = lax.broadcasted_iota(jnp.int32, x.shape, 1)
    return jnp.where(lane < ROPE // 2, pltpu.roll(x, RP - ROPE // 2, 1), pltpu.roll(x, ROPE // 2, 1))


def _rope(x, cos, sin):
    return x * cos + _swap_halves(x) * sin


def _rope_t(gy, cos, sin):
    return gy * cos + _swap_halves(gy * sin)


def _mla_pre(hh, g_q, g_kv, cos, sin, tm=512):
    t = hh.shape[0]
    tm = min(tm, t)

    def kern(h_ref, gq_ref, gkv_ref, c_ref, s_ref, cq_ref, k_ref):
        xq = h_ref[:, 0:QR]
        rq = lax.rsqrt(jnp.mean(xq * xq, axis=-1, keepdims=True) + RMS_EPS)
        cq_ref[...] = (xq * rq * gq_ref[...]).astype(BF16)
        xk = h_ref[:, QR:QR + KVR]
        rk = lax.rsqrt(jnp.mean(xk * xk, axis=-1, keepdims=True) + RMS_EPS)
        k_ref[:, 0:KVR] = (xk * rk * gkv_ref[...]).astype(BF16)
        k_ref[:, KVR:KD] = _rope(h_ref[:, QR + KVR:HW], c_ref[...], s_ref[...]).astype(BF16)

    return pl.pallas_call(
        kern, grid=(t // tm,),
        in_specs=[pl.BlockSpec((tm, HW), lambda i: (i, 0)), pl.BlockSpec((1, QR), lambda i: (0, 0)),
                  pl.BlockSpec((1, KVR), lambda i: (0, 0)), pl.BlockSpec((tm, RP), lambda i: (i, 0)),
                  pl.BlockSpec((tm, RP), lambda i: (i, 0))],
        out_specs=[pl.BlockSpec((tm, QR), lambda i: (i, 0)), pl.BlockSpec((tm, KD), lambda i: (i, 0))],
        out_shape=[jax.ShapeDtypeStruct((t, QR), BF16), jax.ShapeDtypeStruct((t, KD), BF16)],
        name="mla_pre", compiler_params=_cp())(hh, g_q.reshape(1, QR), g_kv.reshape(1, KVR), cos, sin)


def _mla_pre_bwd(hh, dcq, dk, g_q, g_kv, cos, sin, tm=512):
    t = hh.shape[0]
    tm = min(tm, t)

    def rms_bwd(x, dy, g):
        r = lax.rsqrt(jnp.mean(x * x, axis=-1, keepdims=True) + RMS_EPS)
        gdy = dy * g
        dx = r * gdy - x * (r * r * r) * jnp.mean(gdy * x, axis=-1, keepdims=True)
        return dx, jnp.sum(dy * x * r, axis=0, keepdims=True)

    def kern(h_ref, dcq_ref, dk_ref, gq_ref, gkv_ref, c_ref, s_ref, dh_ref, dgq_ref, dgkv_ref):
        @pl.when(pl.program_id(0) == 0)
        def _():
            dgq_ref[...] = jnp.zeros_like(dgq_ref)
            dgkv_ref[...] = jnp.zeros_like(dgkv_ref)

        dxq, dgq = rms_bwd(h_ref[:, 0:QR], dcq_ref[...], gq_ref[...])
        dxk, dgk = rms_bwd(h_ref[:, QR:QR + KVR], dk_ref[:, 0:KVR], gkv_ref[...])
        dh_ref[:, 0:QR] = dxq.astype(BF16)
        dh_ref[:, QR:QR + KVR] = dxk.astype(BF16)
        dh_ref[:, QR + KVR:HW] = _rope_t(dk_ref[:, KVR:KD], c_ref[...], s_ref[...]).astype(BF16)
        dgq_ref[...] += dgq
        dgkv_ref[...] += dgk

    return pl.pallas_call(
        kern, grid=(t // tm,),
        in_specs=[pl.BlockSpec((tm, HW), lambda i: (i, 0)), pl.BlockSpec((tm, QR), lambda i: (i, 0)),
                  pl.BlockSpec((tm, KD), lambda i: (i, 0)), pl.BlockSpec((1, QR), lambda i: (0, 0)),
                  pl.BlockSpec((1, KVR), lambda i: (0, 0)), pl.BlockSpec((tm, RP), lambda i: (i, 0)),
                  pl.BlockSpec((tm, RP), lambda i: (i, 0))],
        out_specs=[pl.BlockSpec((tm, HW), lambda i: (i, 0)), pl.BlockSpec((1, QR), lambda i: (0, 0)),
                   pl.BlockSpec((1, KVR), lambda i: (0, 0))],
        out_shape=[jax.ShapeDtypeStruct((t, HW), BF16), jax.ShapeDtypeStruct((1, QR), F32),
                   jax.ShapeDtypeStruct((1, KVR), F32)],
        name="mla_pre_bwd", compiler_params=_cp())(hh, dcq, dk, g_q.reshape(1, QR), g_kv.reshape(1, KVR), cos, sin)


def _q_prep(q2, wuk_t, cos, sin, tm=512):
    t = q2.shape[0]
    tm = min(tm, t)

    def kern(q_ref, w_ref, c_ref, s_ref, o_ref):
        cos_, sin_ = c_ref[...], s_ref[...]
        for h in range(H):
            qn = q_ref[:, h * NOPE:(h + 1) * NOPE].astype(BF16)
            o_ref[:, h * KD:h * KD + KVR] = (_dot(qn, w_ref[h], NN) * QSCALE).astype(BF16)
            qr = q_ref[:, H * NOPE + h * RP:H * NOPE + (h + 1) * RP]
            o_ref[:, h * KD + KVR:(h + 1) * KD] = (_rope(qr, cos_, sin_) * QSCALE).astype(BF16)

    return pl.pallas_call(
        kern, grid=(t // tm,),
        in_specs=[pl.BlockSpec((tm, 2 * H * NOPE), lambda i: (i, 0)), pl.BlockSpec((H, NOPE, KVR), lambda i: (0, 0, 0)),
                  pl.BlockSpec((tm, RP), lambda i: (i, 0)), pl.BlockSpec((tm, RP), lambda i: (i, 0))],
        out_specs=pl.BlockSpec((tm, H * KD), lambda i: (i, 0)),
        out_shape=jax.ShapeDtypeStruct((t, H * KD), BF16),
        name="q_prep", compiler_params=_cp())(q2, wuk_t, cos, sin)


def _q_prep_bwd(dq_cat, q2, wuk_h, cos, sin, tm=512):
    t = q2.shape[0]
    tm = min(tm, t)

    def kern(dq_ref, q_ref, w_ref, c_ref, s_ref, o_ref, dw_ref):
        @pl.when(pl.program_id(0) == 0)
        def _():
            dw_ref[...] = jnp.zeros_like(dw_ref)

        cos_, sin_ = c_ref[...], s_ref[...]
        for h in range(H):
            dql = dq_ref[:, h * KD:h * KD + KVR].astype(BF16)
            o_ref[:, h * NOPE:(h + 1) * NOPE] = _dot(dql, w_ref[h], NN).astype(BF16)
            dqr = dq_ref[:, h * KD + KVR:(h + 1) * KD]
            o_ref[:, H * NOPE + h * RP:H * NOPE + (h + 1) * RP] = _rope_t(dqr, cos_, sin_).astype(BF16)
            qn = q_ref[:, h * NOPE:(h + 1) * NOPE].astype(BF16)
            dw_ref[h] += _dot(qn, dql, TN)

    return pl.pallas_call(
        kern, grid=(t // tm,),
        in_specs=[pl.BlockSpec((tm, H * KD), lambda i: (i, 0)), pl.BlockSpec((tm, 2 * H * NOPE), lambda i: (i, 0)),
                  pl.BlockSpec((H, KVR, NOPE), lambda i: (0, 0, 0)),
                  pl.BlockSpec((tm, RP), lambda i: (i, 0)), pl.BlockSpec((tm, RP), lambda i: (i, 0))],
        out_specs=[pl.BlockSpec((tm, 2 * H * NOPE), lambda i: (i, 0)), pl.BlockSpec((H, NOPE, KVR), lambda i: (0, 0, 0))],
        out_shape=[jax.ShapeDtypeStruct((t, 2 * H * NOPE), BF16), jax.ShapeDtypeStruct((H, NOPE, KVR), F32)],
        name="q_prep_bwd", compiler_params=_cp())(dq_cat, q2, wuk_h, cos, sin)


def _o_up(o_lat, wuv_h, tm=512):
    t = o_lat.shape[0]
    tm = min(tm, t)

    def kern(x_ref, w_ref, o_ref):
        for h in range(H):
            xl = x_ref[:, h * KVR:(h + 1) * KVR].astype(BF16)
            o_ref[:, h * NOPE:(h + 1) * NOPE] = _dot(xl, w_ref[h], NN).astype(BF16)

    return pl.pallas_call(
        kern, grid=(t // tm,),
        in_specs=[pl.BlockSpec((tm, H * KVR), lambda i: (i, 0)), pl.BlockSpec((H, KVR, NOPE), lambda i: (0, 0, 0))],
        out_specs=pl.BlockSpec((tm, H * NOPE), lambda i: (i, 0)),
        out_shape=jax.ShapeDtypeStruct((t, H * NOPE), BF16),
        name="o_up", compiler_params=_cp())(o_lat, wuv_h)


def _o_up_bwd(do, o_lat, wuv_h, tm=512):
    t = do.shape[0]
    tm = min(tm, t)

    def kern(do_ref, x_ref, w_ref, dx_ref, dw_ref, dlt_ref):
        @pl.when(pl.program_id(0) == 0)
        def _():
            dw_ref[...] = jnp.zeros_like(dw_ref)

        for h in range(H):
            dh_ = do_ref[:, h * NOPE:(h + 1) * NOPE]
            x = x_ref[:, h * KVR:(h + 1) * KVR]
            dx = _dot(dh_, w_ref[h], NT)
            dx_ref[:, h * KVR:(h + 1) * KVR] = dx.astype(BF16)
            dw_ref[h] += _dot(x.astype(BF16), dh_, TN)
            dl = jnp.broadcast_to(jnp.sum(dx * x, axis=1)[:, None], (tm, 128))
            dlt_ref[h] = dl.T[0:1, :]

    return pl.pallas_call(
        kern, grid=(t // tm,),
        in_specs=[pl.BlockSpec((tm, H * NOPE), lambda i: (i, 0)), pl.BlockSpec((tm, H * KVR), lambda i: (i, 0)),
                  pl.BlockSpec((H, KVR, NOPE), lambda i: (0, 0, 0))],
        out_specs=[pl.BlockSpec((tm, H * KVR), lambda i: (i, 0)), pl.BlockSpec((H, KVR, NOPE), lambda i: (0, 0, 0)),
                   pl.BlockSpec((H, 1, tm), lambda i: (0, 0, i))],
        out_shape=[jax.ShapeDtypeStruct((t, H * KVR), BF16), jax.ShapeDtypeStruct((H, KVR, NOPE), F32),
                   jax.ShapeDtypeStruct((H, 1, t), F32)],
        name="o_up_bwd", compiler_params=_cp())(do, o_lat, wuv_h)


def _causal_pairs(nq):
    return [(i, j) for i in range(nq) for j in range(i + 1)]


def _lane_tile(stat, width):
    return jnp.tile(stat, (1, width // 128))


def _flash_fwd(qcat, kc, bq, hb, gather=None):
    t = kc.shape[0]
    nq = t // bq
    pairs = _causal_pairs(nq)
    itab = jnp.asarray(np.array([p[0] for p in pairs], np.int32))
    jtab = jnp.asarray(np.array([p[1] for p in pairs], np.int32))

    ng = H // hb
    hosting = gather is not None

    def kern(it, jt, q_ref, k_ref, *rest):
        if hosting:
            w_ref, o_ref, lset_ref, wall_ref, m_sc, l_sc, acc_sc, send_sems, recv_sems = rest
            ag_start, ag_forward, ag_finish = _allgather_schedule(w_ref, wall_ref, send_sems, recv_sems)
        else:
            o_ref, lset_ref, m_sc, l_sc, acc_sc = rest
        grp = pl.program_id(0)
        st = pl.program_id(1)
        i, j = it[st], jt[st]

        if hosting:
            @pl.when((grp == 0) & (st == 0))
            def _():
                ag_start()

        @pl.when(j == 0)
        def _():
            m_sc[...] = jnp.full_like(m_sc, NEG)
            l_sc[...] = jnp.zeros_like(l_sc)
            acc_sc[...] = jnp.zeros_like(acc_sc)

        def update(masked):
            k = k_ref[...]
            v = k[:, 0:KVR]
            if masked:
                row = lax.broadcasted_iota(jnp.int32, (bq, bq), 0)
                col = lax.broadcasted_iota(jnp.int32, (bq, bq), 1)
                keep = col <= row
            pending = [_dot(q_ref[:, hh * KD:(hh + 1) * KD], k, NT) for hh in range(min(AHEAD, hb))]
            for hh in range(hb):
                s = pending.pop(0)
                if hh + AHEAD < hb:
                    pending.append(_dot(q_ref[:, (hh + AHEAD) * KD:(hh + AHEAD + 1) * KD], k, NT))
                if masked:
                    s = jnp.where(keep, s, NEG)
                m_prev = m_sc[hh]
                m_next = jnp.maximum(m_prev, jnp.max(s, axis=1)[:, None])
                p = jnp.exp2(s - _lane_tile(m_next, bq))
                a = jnp.exp2(m_prev - m_next)
                l_sc[hh] = a * l_sc[hh] + jnp.sum(p, axis=1)[:, None]
                acc_sc[hh] = _lane_tile(a, KVR) * acc_sc[hh] + _dot(p.astype(BF16), v, NN)
                m_sc[hh] = m_next

        @pl.when(j < i)
        def _():
            update(False)

        @pl.when(j == i)
        def _():
            update(True)
            for hh in range(hb):
                l = l_sc[hh]
                o_ref[:, hh * KVR:(hh + 1) * KVR] = acc_sc[hh] / _lane_tile(l, KVR)
                lset_ref[hh] = (m_sc[hh] + jnp.log2(l)).T[0:1, :]

        if hosting:
            half_way = (ng * len(pairs)) // 2

            @pl.when(grp * len(pairs) + st == half_way)
            def _():
                ag_forward()

            @pl.when((grp == ng - 1) & (st == len(pairs) - 1))
            def _():
                ag_finish()

    in_specs = [pl.BlockSpec((bq, hb * KD), lambda g, s, it, jt: (it[s], g)),
                pl.BlockSpec((bq, KD), lambda g, s, it, jt: (jt[s], 0))]
    out_specs = [pl.BlockSpec((bq, hb * KVR), lambda g, s, it, jt: (it[s], g)),
                 pl.BlockSpec((hb, 1, bq), lambda g, s, it, jt: (g, 0, it[s]))]
    out_shape = [jax.ShapeDtypeStruct((t, H * KVR), F32), jax.ShapeDtypeStruct((H, 1, t), F32)]
    scratch = [pltpu.VMEM((hb, bq, 128), F32), pltpu.VMEM((hb, bq, 128), F32), pltpu.VMEM((hb, bq, KVR), F32)]
    args = [itab, jtab, qcat, kc]
    if hosting:
        in_specs.append(ANY)
        out_specs.append(ANY)
        out_shape.append(jax.ShapeDtypeStruct((4,) + gather.shape, gather.dtype))
        scratch += AG_SEMS
        args.append(gather)
    gs = pltpu.PrefetchScalarGridSpec(num_scalar_prefetch=2, grid=(ng, len(pairs)), in_specs=in_specs,
                                      out_specs=out_specs, scratch_shapes=scratch)
    return pl.pallas_call(kern, grid_spec=gs, out_shape=out_shape, name="mla_flash_fwd",
                          compiler_params=_cp())(*args)


def _flash_dkv(qcat, kc, do_lat, lse_t, delta_t, bq, hb, exchange=()):
    nx = len(exchange)
    t = kc.shape[0]
    nq = t // bq
    ng = H // hb
    npairs = nq * (nq + 1) // 2
    steps = [(j, g, i) for j in range(nq) for g in range(ng) for i in range(j, nq)]
    jtab = jnp.asarray(np.array([s[0] for s in steps], np.int32))
    gtab = jnp.asarray(np.array([s[1] for s in steps], np.int32))
    itab = jnp.asarray(np.array([s[2] for s in steps], np.int32))
    ptab = jnp.asarray(np.array([s[2] * (s[2] + 1) // 2 + s[0] for s in steps], np.int32))

    def kern(jt, gt, it, pt, q_ref, k_ref, do_ref, lset_ref, dlt_ref, *rest):
        p_refs, (dk_ref, ds_ref), slots_refs = rest[:nx], rest[nx:nx + 2], rest[nx + 2:2 * nx + 2]
        dk_sc, dv_sc = rest[2 * nx + 2:2 * nx + 4]
        sems = rest[2 * nx + 4:]
        hooks = [_device_exchange_schedule(p_refs[e], slots_refs[e], sems[2 * e], sems[2 * e + 1]) for e in range(nx)]
        st = pl.program_id(0)
        j, g, i = jt[st], gt[st], it[st]

        if nx:
            @pl.when(st == 0)
            def _():
                for start, _ in hooks:
                    start()

        @pl.when((g == 0) & (i == j))
        def _():
            dk_sc[...] = jnp.zeros_like(dk_sc)
            dv_sc[...] = jnp.zeros_like(dv_sc)

        def update(masked):
            k = k_ref[...]
            v = k[:, 0:KVR]
            if masked:
                row = lax.broadcasted_iota(jnp.int32, (bq, bq), 0)
                col = lax.broadcasted_iota(jnp.int32, (bq, bq), 1)
                keep = row <= col

            def first_matmuls(hh):
                dob = do_ref[:, hh * KVR:(hh + 1) * KVR].astype(BF16)
                return _dot(k, q_ref[:, hh * KD:(hh + 1) * KD], NT), _dot(v, dob, NT), dob

            pending = [first_matmuls(hh) for hh in range(min(AHEAD, hb))]
            for hh in range(hb):
                s, dp, dob = pending.pop(0)
                if hh + AHEAD < hb:
                    pending.append(first_matmuls(hh + AHEAD))
                if masked:
                    s = jnp.where(keep, s, NEG)
                p = jnp.exp2(s - lset_ref[hh])
                dv_sc[...] += _dot(p.astype(BF16), dob, NN)
                dsb = (p * (dp - dlt_ref[hh])).astype(BF16)
                ds_ref[0, 0, hh] = dsb
                dk_sc[...] += _dot(dsb, q_ref[:, hh * KD:(hh + 1) * KD], NN)

        @pl.when(i > j)
        def _():
            update(False)

        @pl.when(i == j)
        def _():
            update(True)

        @pl.when((g == ng - 1) & (i == nq - 1))
        def _():
            dk_ref[:, 0:KVR] = dk_sc[:, 0:KVR] * LN2 + dv_sc[...]
            dk_ref[:, KVR:KD] = dk_sc[:, KVR:KD] * LN2

        if nx:
            @pl.when(st == len(steps) - 1)
            def _():
                for _, finish in hooks:
                    finish()

    in_specs = [pl.BlockSpec((bq, hb * KD), lambda s, jt, gt, it, pt: (it[s], gt[s])),
                pl.BlockSpec((bq, KD), lambda s, jt, gt, it, pt: (jt[s], 0)),
                pl.BlockSpec((bq, hb * KVR), lambda s, jt, gt, it, pt: (it[s], gt[s])),
                pl.BlockSpec((hb, 1, bq), lambda s, jt, gt, it, pt: (gt[s], 0, it[s])),
                pl.BlockSpec((hb, 1, bq), lambda s, jt, gt, it, pt: (gt[s], 0, it[s]))] + [ANY] * nx
    out_specs = [pl.BlockSpec((bq, KD), lambda s, jt, gt, it, pt: (jt[s], 0)),
                 pl.BlockSpec((1, 1, hb, bq, bq), lambda s, jt, gt, it, pt: (gt[s], pt[s], 0, 0, 0))] + [ANY] * nx
    out_shape = [jax.ShapeDtypeStruct((t, KD), F32), jax.ShapeDtypeStruct((ng, npairs, hb, bq, bq), BF16)]
    out_shape += [jax.ShapeDtypeStruct((8, e.shape[1] // 2, D), e.dtype) for e in exchange]
    scratch = [pltpu.VMEM((bq, KD), F32), pltpu.VMEM((bq, KVR), F32)] + ALL_SEMS * nx
    args = [jtab, gtab, itab, ptab, qcat, kc, do_lat, lse_t, delta_t, *exchange]
    gs = pltpu.PrefetchScalarGridSpec(num_scalar_prefetch=4, grid=(len(steps),), in_specs=in_specs,
                                      out_specs=out_specs, scratch_shapes=scratch)
    return pl.pallas_call(kern, grid_spec=gs, out_shape=out_shape, name="mla_flash_dkv",
                          compiler_params=_cp())(*args)


def _flash_dq(ds_all, kc_t, bq, exchange=None):
    nq = kc_t.shape[0]
    t = nq * bq
    ngrp, _, hper = ds_all.shape[:3]
    pairs = _causal_pairs(nq)
    itab = jnp.asarray(np.array([p[0] for p in pairs], np.int32))
    jtab = jnp.asarray(np.array([p[1] for p in pairs], np.int32))
    hosting = exchange is not None

    def kern(it, jt, *refs):
        ds_refs, kt_ref, rest = refs[:ngrp], refs[ngrp], refs[ngrp + 1:]
        if hosting:
            p_ref, dq_ref, slots_ref, acc_sc, send_sems, recv_sems = rest
            xc_start, xc_finish = _device_exchange_schedule(p_ref, slots_ref, send_sems, recv_sems)
        else:
            dq_ref, acc_sc = rest
        st = pl.program_id(0)
        i, j = it[st], jt[st]
        kt = kt_ref[...]

        def ds(hh):
            return ds_refs[hh // hper][0, 0, hh % hper]

        if hosting:
            @pl.when(st == 0)
            def _():
                xc_start()

        @pl.when(j == 0)
        def _():
            for hh in range(H):
                acc_sc[hh] = _dot(kt, ds(hh), NN)

        @pl.when((j > 0) & (j < i))
        def _():
            for hh in range(H):
                acc_sc[hh] += _dot(kt, ds(hh), NN)

        @pl.when(j == i)
        def _():
            for hh in range(H):
                tot = _dot(kt, ds(hh), NN)
                tot = jnp.where(i > 0, tot + acc_sc[hh], tot)
                dq_ref[:, hh * KD:(hh + 1) * KD] = tot.T * MLA_SCALE

        if hosting:
            @pl.when(st == len(pairs) - 1)
            def _():
                xc_finish()

    def group(gi):
        return pl.BlockSpec((1, 1, hper, bq, bq), lambda s, it, jt: (gi, s, 0, 0, 0))

    in_specs = [group(gi) for gi in range(ngrp)] + [pl.BlockSpec((None, KD, bq), lambda s, it, jt: (jt[s], 0, 0))]
    out_specs = [pl.BlockSpec((bq, H * KD), lambda s, it, jt: (it[s], 0))]
    out_shape = [jax.ShapeDtypeStruct((t, H * KD), F32)]
    scratch = [pltpu.VMEM((H, KD, bq), F32)]
    args = [itab, jtab] + [ds_all] * ngrp + [kc_t]
    if hosting:
        in_specs.append(ANY)
        out_specs.append(ANY)
        out_shape.append(jax.ShapeDtypeStruct((8, exchange.shape[1] // 2, D), exchange.dtype))
        scratch += ALL_SEMS
        args.append(exchange)
    gs = pltpu.PrefetchScalarGridSpec(num_scalar_prefetch=2, grid=(len(pairs),), in_specs=in_specs,
                                      out_specs=out_specs, scratch_shapes=scratch)
    outs = pl.pallas_call(kern, grid_spec=gs, out_shape=out_shape, name="mla_flash_dq",
                          compiler_params=_cp())(*args)
    return outs if hosting else outs[0]


def _bucket_table():
    d = np.arange(WIN)
    max_exact = NBKT // 2
    nf = np.maximum(d, 1).astype(np.float32)
    large = max_exact + (np.log(nf / np.float32(max_exact)) / np.float32(math.log(WIN / max_exact))
                         * np.float32(NBKT - max_exact)).astype(np.int32)
    large = np.minimum(large, NBKT - 1)
    bucket = np.where(d < max_exact, d, large).astype(np.int32)
    jj = np.arange(2 * WIN)[:, None]
    ii = np.arange(WIN)[None, :]
    dist = ii + WIN - jj
    valid = (dist >= 0) & (dist < WIN)
    return np.where(valid, bucket[np.clip(dist, 0, WIN - 1)], -1).astype(np.int32)


def _bias_build(rel_bias, bkt):
    def kern(bk_ref, rb_ref, o_ref):
        bk = bk_ref[...]
        for hd in range(QH):
            acc = jnp.full((2 * WIN, WIN), NEG, F32)
            for b in range(NBKT):
                acc = jnp.where(bk == b, rb_ref[b, hd], acc)
            o_ref[hd] = acc

    return pl.pallas_call(
        kern, in_specs=[pl.BlockSpec(memory_space=pltpu.VMEM), pl.BlockSpec(memory_space=pltpu.SMEM)],
        out_specs=pl.BlockSpec(memory_space=pltpu.VMEM),
        out_shape=jax.ShapeDtypeStruct((QH, 2 * WIN, WIN), F32), name="swa_bias_build")(bkt, rel_bias)


def _bias_bwd(dbias, bkt):
    def kern(db_ref, bk_ref, o_ref):
        bk = bk_ref[...]
        for hd in range(QH):
            g = db_ref[hd]
            for b in range(NBKT):
                r = b * QH + hd
                o_ref[r:r + 1, :] = jnp.sum(jnp.where(bk == b, g, 0.0), axis=0, keepdims=True)

    return pl.pallas_call(
        kern, in_specs=[pl.BlockSpec(memory_space=pltpu.VMEM), pl.BlockSpec(memory_space=pltpu.VMEM)],
        out_specs=pl.BlockSpec(memory_space=pltpu.VMEM),
        out_shape=jax.ShapeDtypeStruct((NBKT * QH, WIN), F32), name="swa_bias_bwd")(dbias, bkt)


def _swa_finish_scores(raw, bias, first):
    s = raw * SWA_SCALE + bias
    if first is not None:
        row = lax.broadcasted_iota(jnp.int32, s.shape, 0)
        s = jnp.where(jnp.logical_or(jnp.logical_not(first), row >= WIN), s, NEG)
    return s


def _swa_fwd(qkv_t, bias, sinks, qb):
    t = qkv_t.shape[1]
    w = qb * WIN
    nst = t // w

    def kern(q_ref, kc_ref, kp_ref, vc_ref, vp_ref, b_ref, sk_ref, o_ref, lse_ref):
        n = pl.program_id(0)
        kfull = jnp.concatenate([kp_ref[...], kc_ref[...]], axis=1)
        vfull = jnp.concatenate([vp_ref[...], vc_ref[...]], axis=1)
        head_row = lax.broadcasted_iota(jnp.int32, (QH, WIN), 0)
        groups = [(b, kh) for b in range(qb) for kh in range(KVH)]

        def raw_scores(b, kh):
            k_band = kfull[kh * HD:(kh + 1) * HD, b * WIN:(b + 2) * WIN]
            return [_dot(k_band, q_ref[(kh * G + g) * HD:(kh * G + g + 1) * HD, b * WIN:(b + 1) * WIN], TN)
                    for g in range(G)]

        o_rows = [[] for _ in range(qb)]
        lse_tiles = [jnp.zeros((QH, WIN), F32) for _ in range(qb)]
        pending = [raw_scores(*grp) for grp in groups[:AHEAD]]
        for gi, (b, kh) in enumerate(groups):
            scores = pending.pop(0)
            if gi + AHEAD < len(groups):
                pending.append(raw_scores(*groups[gi + AHEAD]))
            v_band = vfull[kh * HD:(kh + 1) * HD, b * WIN:(b + 2) * WIN]
            for g in range(G):
                hd = kh * G + g
                s = _swa_finish_scores(scores[g], b_ref[hd], (n == 0) if b == 0 else None)
                sink = sk_ref[hd]
                m = jnp.maximum(jnp.max(s, axis=0, keepdims=True), sink)
                p = jnp.exp(s - m)
                den = jnp.sum(p, axis=0, keepdims=True) + jnp.exp(sink - m)
                p = p / den
                o_rows[b].append(_dot(v_band, p.astype(BF16), NN))
                lse_tiles[b] = jnp.where(head_row == hd, m + jnp.log(den), lse_tiles[b])
        o_ref[...] = jnp.concatenate([jnp.concatenate(rows, axis=0) for rows in o_rows], axis=1)
        lse_ref[...] = jnp.concatenate(lse_tiles, axis=1)

    prev = lambda r: (lambda n: (r, jnp.maximum(n * qb - 1, 0)))
    return pl.pallas_call(
        kern, grid=(nst,),
        in_specs=[pl.BlockSpec((QH * HD, w), lambda n: (0, n)),
                  pl.BlockSpec((KVH * HD, w), lambda n: (4, n)), pl.BlockSpec((KVH * HD, WIN), prev(4)),
                  pl.BlockSpec((KVH * HD, w), lambda n: (5, n)), pl.BlockSpec((KVH * HD, WIN), prev(5)),
                  pl.BlockSpec((QH, 2 * WIN, WIN), lambda n: (0, 0, 0)),
                  pl.BlockSpec(memory_space=pltpu.SMEM)],
        out_specs=[pl.BlockSpec((QH * HD, w), lambda n: (0, n)), pl.BlockSpec((QH, w), lambda n: (0, n))],
        out_shape=[jax.ShapeDtypeStruct((QH * HD, t), F32), jax.ShapeDtypeStruct((QH, t), F32)],
        name="swa_fwd", compiler_params=_cp())(qkv_t, qkv_t, qkv_t, qkv_t, qkv_t, bias, sinks)


def _swa_bwd(qkv_t, do_t, o_t, lse, bias, sinks, qb):
    t = qkv_t.shape[1]
    w = qb * WIN
    nst = t // w
    nblk = t // WIN

    def kern(q_ref, kc_ref, kp_ref, vc_ref, vp_ref, do_ref, o_ref, lse_ref, qn_ref, don_ref, on_ref, lsen_ref,
             b_ref, sk_ref, dqkv_ref, db_ref, dsk_ref):
        n = pl.program_id(0)

        @pl.when(n == 0)
        def _():
            db_ref[...] = jnp.zeros_like(db_ref)
            dsk_ref[...] = jnp.zeros_like(dsk_ref)

        kfull = jnp.concatenate([kp_ref[...], kc_ref[...]], axis=1)
        vfull = jnp.concatenate([vp_ref[...], vc_ref[...]], axis=1)
        head_row = lax.broadcasted_iota(jnp.int32, (QH, WIN), 0)
        db_acc = [None] * QH
        dsk_tile = jnp.zeros((QH, WIN), F32)
        prev_part = [[[None] * qb for _ in range(KVH)] for _ in range(2)]
        cur_part = [[[None] * qb for _ in range(KVH)] for _ in range(2)]
        groups = [(b, kh) for b in range(qb) for kh in range(KVH)]

        def first_matmuls(b, kh):
            k_band = kfull[kh * HD:(kh + 1) * HD, b * WIN:(b + 2) * WIN]
            v_band = vfull[kh * HD:(kh + 1) * HD, b * WIN:(b + 2) * WIN]
            out = []
            for g in range(G):
                rs = slice((kh * G + g) * HD, (kh * G + g + 1) * HD)
                dob = do_ref[rs, b * WIN:(b + 1) * WIN].astype(BF16)
                out.append((_dot(k_band, q_ref[rs, b * WIN:(b + 1) * WIN], TN), _dot(v_band, dob, TN), dob))
            return out

        dq_rows = [[] for _ in range(qb)]
        pending = [first_matmuls(*grp) for grp in groups[:AHEAD]]
        for gi, (b, kh) in enumerate(groups):
            first = pending.pop(0)
            if gi + AHEAD < len(groups):
                pending.append(first_matmuls(*groups[gi + AHEAD]))
            cs = slice(b * WIN, (b + 1) * WIN)
            k_band = kfull[kh * HD:(kh + 1) * HD, b * WIN:(b + 2) * WIN]
            dk_b = dv_b = None
            for g in range(G):
                hd = kh * G + g
                rs = slice(hd * HD, (hd + 1) * HD)
                raw, dp, dob = first[g]
                lse_h = lse_ref[hd:hd + 1, cs]
                s = _swa_finish_scores(raw, b_ref[hd], (n == 0) if b == 0 else None)
                p = jnp.exp(s - lse_h)
                dl = jnp.sum(do_ref[rs, cs] * o_ref[rs, cs], axis=0, keepdims=True)
                ds = p * (dp - dl)
                db_acc[hd] = ds if db_acc[hd] is None else db_acc[hd] + ds
                dsk_tile = jnp.where(head_row == hd, dsk_tile - jnp.exp(sk_ref[hd] - lse_h) * dl, dsk_tile)
                dss = (ds * SWA_SCALE).astype(BF16)
                dq_rows[b].append(_dot(k_band, dss, NN).astype(BF16))
                dk_h = _dot(q_ref[rs, cs], dss, NT)
                dv_h = _dot(dob, p.astype(BF16), NT)
                dk_b = dk_h if dk_b is None else dk_b + dk_h
                dv_b = dv_h if dv_b is None else dv_b + dv_h
            for which, val in ((0, dk_b), (1, dv_b)):
                prev_part[which][kh][b] = val[:, 0:WIN]
                cur_part[which][kh][b] = val[:, WIN:2 * WIN]
        dq_cols = [jnp.concatenate(rows, axis=0) for rows in dq_rows]

        live = n < nst - 1
        ls = slice((qb - 1) * WIN, qb * WIN)
        halo = [[None] * KVH for _ in range(2)]
        for kh in range(KVH):
            k_last = kc_ref[kh * HD:(kh + 1) * HD, ls]
            v_last = vc_ref[kh * HD:(kh + 1) * HD, ls]
            dk_b = dv_b = None
            for g in range(G):
                hd = kh * G + g
                rs = slice(hd * HD, (hd + 1) * HD)
                q_t = qn_ref[rs, :]
                do = don_ref[rs, :]
                s = _dot(k_last, q_t, TN) * SWA_SCALE + b_ref[hd, 0:WIN, :]
                p = jnp.exp(s - lsen_ref[hd:hd + 1, :])
                dob = do.astype(BF16)
                dp = _dot(v_last, dob, TN)
                dl = jnp.sum(do * on_ref[rs, :], axis=0, keepdims=True)
                dss = (p * (dp - dl) * SWA_SCALE).astype(BF16)
                dk_h = _dot(q_t, dss, NT)
                dv_h = _dot(dob, p.astype(BF16), NT)
                dk_b = dk_h if dk_b is None else dk_b + dk_h
                dv_b = dv_h if dv_b is None else dv_b + dv_h
            halo[0][kh] = jnp.where(live, dk_b, 0.0)
            halo[1][kh] = jnp.where(live, dv_b, 0.0)

        kv_rows = []
        for which in range(2):
            for kh in range(KVH):
                blocks = [cur_part[which][kh][p] + (prev_part[which][kh][p + 1] if p + 1 < qb else halo[which][kh])
                          for p in range(qb)]
                kv_rows.append(jnp.concatenate(blocks, axis=1))
        dqkv_ref[...] = jnp.concatenate(
            [jnp.concatenate(dq_cols, axis=1), jnp.concatenate(kv_rows, axis=0).astype(BF16)], axis=0)
        db_ref[...] += jnp.stack(db_acc)
        dsk_ref[...] += dsk_tile

    prev = lambda r: (lambda n: (r, jnp.maximum(n * qb - 1, 0)))
    nxt = lambda n: (0, jnp.minimum((n + 1) * qb, nblk - 1))
    big = lambda: pl.BlockSpec((QH * HD, w), lambda n: (0, n))
    return pl.pallas_call(
        kern, grid=(nst,),
        in_specs=[big(),
                  pl.BlockSpec((KVH * HD, w), lambda n: (4, n)), pl.BlockSpec((KVH * HD, WIN), prev(4)),
                  pl.BlockSpec((KVH * HD, w), lambda n: (5, n)), pl.BlockSpec((KVH * HD, WIN), prev(5)),
                  big(), big(), pl.BlockSpec((QH, w), lambda n: (0, n)),
                  pl.BlockSpec((QH * HD, WIN), nxt), pl.BlockSpec((QH * HD, WIN), nxt),
                  pl.BlockSpec((QH * HD, WIN), nxt), pl.BlockSpec((QH, WIN), nxt),
                  pl.BlockSpec((QH, 2 * WIN, WIN), lambda n: (0, 0, 0)),
                  pl.BlockSpec(memory_space=pltpu.SMEM)],
        out_specs=[pl.BlockSpec(((QH + 2 * KVH) * HD, w), lambda n: (0, n)),
                   pl.BlockSpec((QH, 2 * WIN, WIN), lambda n: (0, 0, 0)),
                   pl.BlockSpec((QH, WIN), lambda n: (0, 0))],
        out_shape=[jax.ShapeDtypeStruct(((QH + 2 * KVH) * HD, t), BF16),
                   jax.ShapeDtypeStruct((QH, 2 * WIN, WIN), F32), jax.ShapeDtypeStruct((QH, WIN), F32)],
        name="swa_bwd", compiler_params=_cp())(
            qkv_t, qkv_t, qkv_t, qkv_t, qkv_t, do_t, o_t, lse, qkv_t, do_t, o_t, lse, bias, sinks)


def _adamw_math(w, g, m, v):
    nm = B1 * m + (1.0 - B1) * g
    nv = B2 * v + (1.0 - B2) * (g * g)
    mhat = nm * (1.0 / (1.0 - B1 ** STEP))
    vhat = nv * (1.0 / (1.0 - B2 ** STEP))
    return -LR * (mhat / (jnp.sqrt(vhat) + ADAM_EPS) + WD * w), nm, nv


def _adamw_layers(w, m, v, g0buf, g1buf, off, name, tm=512):
    rows = w.shape[1]
    nb, ob = rows // tm, off // tm

    def kern(w_ref, m_ref, v_ref, g0_ref, g1_ref, gr_ref, d_ref, nm_ref, nv_ref):
        g_ = jnp.where(pl.program_id(0) == 0, g0_ref[...], g1_ref[...])
        gr_ref[...] = g_
        d_ref[...], nm_ref[...], nv_ref[...] = _adamw_math(w_ref[...], g_, m_ref[...], v_ref[...])

    lay = pl.BlockSpec((None, tm, D), lambda l, i: (l, i, 0))
    gsp = pl.BlockSpec((tm, D), lambda l, i: (ob + i, 0))
    return pl.pallas_call(
        kern, grid=(2, nb), in_specs=[lay, lay, lay, gsp, gsp], out_specs=[lay] * 4,
        out_shape=[jax.ShapeDtypeStruct(w.shape, F32)] * 4, name=name, compiler_params=_cp())(w, m, v, g0buf, g1buf)


def _adamw(w, g, m, v, name, tm=544):
    r = w.shape[0]
    tm = r if r % tm else tm

    def kern(w_ref, g_ref, m_ref, v_ref, d_ref, nm_ref, nv_ref):
        d_ref[...], nm_ref[...], nv_ref[...] = _adamw_math(w_ref[...], g_ref[...], m_ref[...], v_ref[...])

    row = pl.BlockSpec((tm, D), lambda i: (i, 0))
    sds = jax.ShapeDtypeStruct((r, D), F32)
    return pl.pallas_call(kern, grid=(r // tm,), in_specs=[row] * 4, out_specs=[row] * 3, out_shape=[sds] * 3,
                          name=name, compiler_params=_cp())(w, g, m, v)


def _mesh_pos():
    return lax.axis_index("x"), lax.axis_index("y"), lax.axis_index("c")


ANY = pl.BlockSpec(memory_space=pl.ANY)


AG_SEMS = [pltpu.SemaphoreType.DMA((6,)), pltpu.SemaphoreType.DMA((6,))]


def _allgather_schedule(w_ref, out_ref, send_sems, recv_sems):
    half = w_ref.shape[0] // 2
    x, y, c = _mesh_pos()
    me, sibling = (x, y, c), (x, y, 1 - c)
    chips = [(1 - x, y), (x, 1 - y), (1 - x, 1 - y)]

    def rows(px, py, pc):
        return out_ref.at[2 * px + py, pl.ds(pc * half, half), :]

    def copy(k, block, to, src=None):
        return pltpu.make_async_remote_copy(
            src_ref=rows(*block) if src is None else src, dst_ref=rows(*block),
            send_sem=send_sems.at[k], recv_sem=recv_sems.at[k], device_id=to, device_id_type=MESH)

    def first():
        return [copy(j, me, (*chip, c), src=w_ref.at[pl.ds(c * half, half), :]) for j, chip in enumerate(chips)]

    def passed():
        return [copy(3 + j, (*chip, c), sibling) for j, chip in enumerate(chips)]

    def start():
        for cp in first():
            cp.start()

    def forward():
        for j, chip in enumerate(chips):
            copy(j, (*chip, c), me).wait_recv()
            passed()[j].start()

    def finish():
        for j, chip in enumerate(chips):
            copy(3 + j, (*chip, 1 - c), me).wait_recv()
        for cp in first() + passed():
            cp.wait_send()

    return start, forward, finish


def _allgather_weights(wpack):
    def body(w_ref, out_ref, send_sems, recv_sems):
        start, forward, finish = _allgather_schedule(w_ref, out_ref, send_sems, recv_sems)
        start()
        forward()
        finish()

    return pl.pallas_call(
        body, out_shape=jax.ShapeDtypeStruct((4,) + wpack.shape, wpack.dtype), in_specs=[ANY], out_specs=ANY,
        scratch_shapes=AG_SEMS, name="allgather_weights")(wpack)


def _row_tile(rows):
    t = min(rows, 512)
    while rows % t or t % 16:
        t -= 16
    return t


ALL_SEMS = [pltpu.SemaphoreType.DMA((7,)), pltpu.SemaphoreType.DMA((7,))]


def _device_exchange_schedule(g_ref, out_ref, send_sems, recv_sems):
    half = g_ref.shape[1] // 2
    x, y, c = _mesh_pos()
    me = 4 * x + 2 * y + c
    peers = [(x ^ (k >> 2), y ^ ((k >> 1) & 1), c ^ (k & 1)) for k in range(1, 8)]

    def sends():
        return [pltpu.make_async_remote_copy(
            src_ref=g_ref.at[2 * px + py, pl.ds(pc * half, half), :], dst_ref=out_ref.at[me],
            send_sem=send_sems.at[j], recv_sem=recv_sems.at[j], device_id=(px, py, pc), device_id_type=MESH)
            for j, (px, py, pc) in enumerate(peers)]

    def start():
        for cp in sends():
            cp.start()

    def finish():
        for j, (px, py, pc) in enumerate(peers):
            pltpu.make_async_remote_copy(
                src_ref=out_ref.at[me], dst_ref=out_ref.at[4 * px + 2 * py + pc], send_sem=send_sems.at[j],
                recv_sem=recv_sems.at[j], device_id=(px, py, pc), device_id_type=MESH).wait_recv()
        for cp in sends():
            cp.wait_send()

    return start, finish


def _sum_devices(slots, g, pos, tag):
    half = slots.shape[1]
    tm = _row_tile(half)
    nb = half // tm

    def kern(pos_ref, own_ref, *refs):
        acc = own_ref[0].astype(F32)
        for s_ref in refs[:7]:
            acc = acc + s_ref[0].astype(F32)
        refs[7][...] = acc

    def slot(k):
        return pl.BlockSpec((1, tm, D), lambda i, pos: (jnp.bitwise_xor(pos[2], k), i, 0))

    gs = pltpu.PrefetchScalarGridSpec(
        num_scalar_prefetch=1, grid=(nb,),
        in_specs=[pl.BlockSpec((1, tm, D), lambda i, pos: (pos[0], pos[1] * nb + i, 0))] + [slot(k) for k in range(1, 8)],
        out_specs=pl.BlockSpec((tm, D), lambda i, pos: (pos[1] * nb + i, 0)))
    return pl.pallas_call(kern, grid_spec=gs, out_shape=jax.ShapeDtypeStruct((2 * half, D), F32),
                          name=f"rs_sum_devices_{tag}", compiler_params=_cp())(pos, g, *([slots] * 7))


def _exchange_devices(g, tag):
    def body(g_ref, out_ref, send_sems, recv_sems):
        start, finish = _device_exchange_schedule(g_ref, out_ref, send_sems, recv_sems)
        start()
        finish()

    return pl.pallas_call(
        body, out_shape=jax.ShapeDtypeStruct((8, g.shape[1] // 2, D), g.dtype), in_specs=[ANY], out_specs=ANY,
        scratch_shapes=ALL_SEMS, name=f"rs_exchange_devices_{tag}")(g)


def _reduce_scatter_finish(slots, g, pos, tag):
    return _join_core_halves(_sum_devices(slots, g, pos, tag), tag)


def _join_core_halves(r, tag):
    half = r.shape[0] // 2

    def body(r_ref, out_ref, send_sem, recv_sem):
        x, y, c = _mesh_pos()
        mine = out_ref.at[pl.ds(c * half, half), :]
        cp = pltpu.make_async_remote_copy(
            src_ref=mine, dst_ref=mine, send_sem=send_sem, recv_sem=recv_sem,
            device_id=(x, y, 1 - c), device_id_type=MESH)
        cp.start()
        theirs = out_ref.at[pl.ds((1 - c) * half, half), :]
        pltpu.make_async_remote_copy(
            src_ref=theirs, dst_ref=theirs, send_sem=send_sem, recv_sem=recv_sem,
            device_id=(x, y, 1 - c), device_id_type=MESH).wait_recv()
        cp.wait_send()

    return pl.pallas_call(
        body, out_shape=jax.ShapeDtypeStruct(r.shape, r.dtype), in_specs=[ANY], out_specs=ANY,
        input_output_aliases={0: 0},
        scratch_shapes=[pltpu.SemaphoreType.DMA, pltpu.SemaphoreType.DMA],
        name=f"rs_join_cores_{tag}")(r)


def _allreduce_small(v, name):
    def body(v_ref, out_ref, gat, send_sems, recv_sems):
        x, y, c = _mesh_pos()
        me = 4 * x + 2 * y + c
        gat[me] = v_ref[...]
        sends = []
        for k in range(1, 8):
            peer = (x ^ (k >> 2), y ^ ((k >> 1) & 1), c ^ (k & 1))
            cp = pltpu.make_async_remote_copy(
                src_ref=v_ref, dst_ref=gat.at[me], send_sem=send_sems.at[k - 1], recv_sem=recv_sems.at[k - 1],
                device_id=peer, device_id_type=MESH)
            cp.start()
            sends.append(cp)
        for k in range(1, 8):
            px, py, pc = x ^ (k >> 2), y ^ ((k >> 1) & 1), c ^ (k & 1)
            pltpu.make_async_remote_copy(
                src_ref=v_ref, dst_ref=gat.at[4 * px + 2 * py + pc], send_sem=send_sems.at[k - 1],
                recv_sem=recv_sems.at[k - 1], device_id=(px, py, pc), device_id_type=MESH).wait_recv()
        for cp in sends:
            cp.wait_send()
        acc = gat[0]
        for d in range(1, 8):
            acc = acc + gat[d]
        out_ref[...] = acc

    return pl.pallas_call(
        body, out_shape=jax.ShapeDtypeStruct(v.shape, F32),
        in_specs=[pl.BlockSpec(memory_space=pltpu.VMEM)], out_specs=pl.BlockSpec(memory_space=pltpu.VMEM),
        scratch_shapes=[pltpu.VMEM((8,) + v.shape, F32), pltpu.SemaphoreType.DMA((7,)), pltpu.SemaphoreType.DMA((7,))],
        name=name)(v)


def _mlp_fwd(xb, w_up, w_down, tag):
    a = _mm(xb, w_up[0], "nn", f"mlp_up_{tag}", out_dtype=BF16, relu2=True, b_view=("cols", w_up[1]))
    return a, _mm(a, w_down[0], "nn", f"mlp_down_{tag}", b_view=("rows", w_down[1]), tm=2048)


def _mlp_bwd(dz, dzb, xb, a, w_up, w_down, tag):
    du = _mm(dzb, w_down[0], "nt", f"mlp_down_dx_{tag}", out_dtype=BF16, gate_a=a, b_view=("rows", w_down[1]),
             tm=2048)
    gsh = _mm(xb, du, "tn", f"mlp_up_dw_{tag}", out_dtype=BF16, out_view=("cols", 2 * ROWS["mlp_w_up"], 0, None))
    gsh = _mm(a, dzb, "tn", f"mlp_down_dw_{tag}", out_dtype=BF16,
              out_view=("rows", 2 * ROWS["mlp_w_up"], ROWS["mlp_w_up"], gsh))
    dx = _mm(du, w_up[0], "nt", f"mlp_up_dx_{tag}", addend=dz, add_scale=ALPHA, b_view=("cols", w_up[1]))
    return dx, gsh


def _fwd_bwd(x, target, w, dist=None, bq=512, qb=8, hb=8):
    t = x.shape[0]
    bq = min(bq, t)
    qb = min(qb, t // WIN)
    cos, sin = _rope_tables(t)
    bkt = jnp.asarray(_bucket_table())
    w_in = jnp.pad(w[("mla_w_in", None)], ((0, 0), (0, HW - (QR + KVR + ROPE))))
    wuq = w[("mla_w_uq", None)]
    wq2 = jnp.concatenate([wuq[:, :, :NOPE].reshape(QR, H * NOPE),
                           jnp.pad(wuq[:, :, NOPE:], ((0, 0), (0, 0), (0, RP - ROPE))).reshape(QR, H * RP)], axis=1)
    wuk_t = w[("mla_w_uk", None)].transpose(1, 2, 0)
    wuk_h = w[("mla_w_uk", None)].transpose(1, 0, 2)
    wuv_h = w[("mla_w_uv", None)].transpose(1, 0, 2)
    w_o = w[("mla_w_o", None)]
    sinks = w["swa_sinks"].reshape(QH)
    lnp = lambda n, l: w[n][l]
    reduced = {}

    hh = _mm(x, w_in, "nn", "mla_in")
    cq, kc = _mla_pre(hh, w["mla_g_q"], w["mla_g_kv"], cos, sin)
    q2 = _mm(cq, wq2, "nn", "mla_uq")
    qcat = _q_prep(q2, wuk_t, cos, sin)
    if dist is None:
        o_lat, lse0_t = _flash_fwd(qcat, kc, bq, hb)
    else:
        o_lat, lse0_t, wall = _flash_fwd(qcat, kc, bq, hb, gather=dist.late_pack)
        wall = lax.dynamic_update_slice(wall, dist.late_pack[None], (dist.shard, 0, 0))
        w = {**w, **_full_from_gathered(AG_LATE, wall, dist.shard_shapes)}
    wqkv = jnp.concatenate([w[("swa_w_q", None)], w[("kv_w_shared", None)]], axis=1)
    wqkv_t = wqkv.T
    wo_s = w[("swa_w_o", None)]
    o0 = _o_up(o_lat, wuv_h)
    y0 = _mm(o0, w_o, "nn", "mla_out")
    x1b, xh1, r1 = _add_ln(x, y0, lnp("ln_mix_g", 0), lnp("ln_mix_b", 0), "ln_mix_0")
    a0, f0 = _mlp_fwd(x1b, w[("mlp_w_up", 0)], w[("mlp_w_down", 0)], 0)
    x2b, xh2, r2 = _add_ln(xh1, f0, lnp("ln_mlp_g", 0), lnp("ln_mlp_b", 0), "ln_mlp_0",
                           res_affine=(lnp("ln_mix_g", 0), lnp("ln_mix_b", 0)))
    bias = _bias_build(w["rel_bias"], bkt)
    qkv_t = _mm(x2b, wqkv, "nn", "swa_qkv", out_dtype=BF16, out_t=True)
    os_t, lse1 = _swa_fwd(qkv_t, bias, sinks, qb)
    y1 = _mm(os_t, wo_s, "tn", "swa_out")
    x3b, xh3, r3 = _add_ln(xh2, y1, lnp("ln_mix_g", 1), lnp("ln_mix_b", 1), "ln_mix_1",
                           res_affine=(lnp("ln_mlp_g", 0), lnp("ln_mlp_b", 0)))
    a1, f1 = _mlp_fwd(x3b, w[("mlp_w_up", 1)], w[("mlp_w_down", 1)], 1)
    _, xh4, r4 = _add_ln(xh3, f1, lnp("ln_mlp_g", 1), lnp("ln_mlp_b", 1), "ln_mlp_1",
                         res_affine=(lnp("ln_mix_g", 1), lnp("ln_mix_b", 1)))

    g = {}
    dz4, dz4b, dg_mlp1, db_mlp1, lpart = _ln_bwd(target, xh4, r4, lnp("ln_mlp_g", 1), "ln_mlp_1_bwd",
                                                 loss_b=lnp("ln_mlp_b", 1))
    dx3, g["mlp1"] = _mlp_bwd(dz4, dz4b, x3b, a1, w[("mlp_w_up", 1)], w[("mlp_w_down", 1)], 1)
    dz3, dz3b, dg_mix1, db_mix1 = _ln_bwd(dx3, xh3, r3, lnp("ln_mix_g", 1), "ln_mix_1_bwd")
    dos_t = _mm(dz3b, wo_s, "nt", "swa_out_dx", out_t=True)
    g[("swa_w_o", None)] = _mm(os_t, dz3b, "nn", "swa_out_dw")
    dqkv_t, dbias, dsk = _swa_bwd(qkv_t, dos_t, os_t, lse1, bias, sinks, qb)
    dwqkv = _mm(dqkv_t, x2b, "nn", "swa_qkv_dw").T
    g[("swa_w_q", None)], g[("kv_w_shared", None)] = dwqkv[:, :QH * HD], dwqkv[:, QH * HD:]
    dx2 = _mm(dqkv_t, wqkv_t, "tn", "swa_qkv_dx", addend=dz3, add_scale=ALPHA)
    g["rel_bias"] = jnp.sum(_bias_bwd(dbias, bkt), axis=-1).reshape(NBKT, QH)
    g["swa_sinks"] = jnp.sum(dsk, axis=-1).reshape(1, QH)
    dz2, dz2b, dg_mlp0, db_mlp0 = _ln_bwd(dx2, xh2, r2, lnp("ln_mlp_g", 0), "ln_mlp_0_bwd")
    dx1, g["mlp0"] = _mlp_bwd(dz2, dz2b, x1b, a0, w[("mlp_w_up", 0)], w[("mlp_w_down", 0)], 0)
    dz1, dz1b, dg_mix0, db_mix0 = _ln_bwd(dx1, xh1, r1, lnp("ln_mix_g", 0), "ln_mix_0_bwd")
    do0 = _mm(dz1b, w_o, "nt", "mla_out_dx", out_dtype=BF16)
    g[("mla_w_o", None)] = _mm(o0, dz1b, "tn", "mla_out_dw")
    do_lat, dwuv, delta_t = _o_up_bwd(do0, o_lat, wuv_h)
    g[("mla_w_uv", None)] = dwuv.transpose(1, 0, 2)
    kc_t = kc.reshape(t // bq, bq, KD).transpose(0, 2, 1)
    if dist is None:
        dk, ds_all = _flash_dkv(qcat, kc, do_lat, lse0_t, delta_t, bq, hb)
        dq_cat = _flash_dq(ds_all, kc_t, bq)
    else:
        g["mid"] = _grad_shards(RS_MID, g).astype(BF16)
        dk, ds_all, slots1, slots_mid = _flash_dkv(qcat, kc, do_lat, lse0_t, delta_t, bq, hb,
                                                  exchange=(g["mlp1"], g["mid"]))
        dq_cat, slots0 = _flash_dq(ds_all, kc_t, bq, exchange=g["mlp0"])
        for key, slots in (("mlp1", slots1), ("mid", slots_mid), ("mlp0", slots0)):
            reduced[key] = _reduce_scatter_finish(slots, g[key], dist.pos, key)
    dq2, dwuk = _q_prep_bwd(dq_cat, q2, wuk_h, cos, sin)
    g[("mla_w_uk", None)] = dwuk.transpose(2, 0, 1)
    dcq = _mm(dq2, wq2, "nt", "mla_uq_dx")
    dwq2 = _mm(cq, dq2, "tn", "mla_uq_dw")
    g[("mla_w_uq", None)] = jnp.concatenate([dwq2[:, :H * NOPE].reshape(QR, H, NOPE),
                                             dwq2[:, H * NOPE:].reshape(QR, H, RP)[:, :, :ROPE]], axis=2)
    dh, dgq, dgkv = _mla_pre_bwd(hh, dcq, dk, w["mla_g_q"], w["mla_g_kv"], cos, sin)
    g[("mla_w_in", None)] = _mm(x, dh, "tn", "mla_in_dw")[:, :QR + KVR + ROPE]
    grad_x = _mm(dh, w_in, "nt", "mla_in_dx", addend=dz1, add_scale=ALPHA)
    g["mla_g_q"], g["mla_g_kv"] = dgq, dgkv
    g["ln_mix_g"] = jnp.concatenate([dg_mix0, dg_mix1], axis=0)
    g["ln_mix_b"] = jnp.concatenate([db_mix0, db_mix1], axis=0)
    g["ln_mlp_g"] = jnp.concatenate([dg_mlp0, dg_mlp1], axis=0)
    g["ln_mlp_b"] = jnp.concatenate([db_mlp0, db_mlp1], axis=0)
    return lpart, grad_x, g, reduced


def _rows(a):
    return a.reshape(-1, D)


def _piece(a, layer):
    return _rows(a if layer is None else a[layer])


def _pack_group(group, parts):
    return jnp.concatenate([_piece(parts[n], l) for n, l in group], axis=0)


def _unpack_group(group, buf, like):
    out, off = {}, 0
    for n, l in group:
        shp = like[n].shape if l is None else like[n].shape[1:]
        out[(n, l)] = buf[off:off + ROWS[n]].reshape(shp)
        off += ROWS[n]
    return out


def _by_name(pieces):
    out = {n: a for (n, l), a in pieces.items() if l is None}
    for n in {n for (n, l) in pieces if l is not None}:
        out[n] = jnp.stack([pieces[(n, 0)], pieces[(n, 1)]])
    return out


def _full_from_gathered(group, wall, shard_shapes):
    out, off = {}, 0
    for n, l in group:
        shp = tuple(shard_shapes[n])
        if n in ("mlp_w_up", "mlp_w_down"):
            out[(n, l)] = (wall, off)
        elif n == "kv_w_shared":
            out[(n, l)] = wall[:, off:off + ROWS[n]].reshape((4 * shp[0],) + shp[1:])
        else:
            out[(n, l)] = wall[:, off:off + ROWS[n]].reshape((4 * shp[1],) + shp[2:])
        off += ROWS[n]
    return out


def _grad_shards(group, g):
    return jnp.concatenate([g[(n, l)].reshape(4, ROWS[n], D) for n, l in group], axis=1)


SMALL = (("ln_mix_g", 0, 2), ("ln_mix_b", 2, 2), ("ln_mlp_g", 4, 2), ("ln_mlp_b", 6, 2),
         ("swa_sinks", 8, 1), ("mla_g_q", 9, 1), ("mla_g_kv", 10, 1), ("rel_bias", 11, 1))
LOSS_ROW = 12


def _pack_small(parts, extra_row=None):
    rows = []
    for n, _, nr in SMALL:
        a = parts[n].reshape(nr, -1).astype(F32)
        rows.append(jnp.pad(a, ((0, 0), (0, D - a.shape[1]))))
    if extra_row is not None:
        rows.append(extra_row)
    rows.append(jnp.zeros((SMALL_ROWS - sum(r.shape[0] for r in rows), D), F32))
    return jnp.concatenate(rows, axis=0)


def _unpack_small(buf, like):
    out = {}
    for n, r0, nr in SMALL:
        size = like[n].size // nr
        out[n] = buf[r0:r0 + nr, :size].reshape(like[n].shape)
    return out


def kernel(x, mla_w_in, mla_g_q, mla_g_kv, mla_w_uq, mla_w_uk, mla_w_uv, mla_w_o, kv_w_shared, swa_w_q, swa_sinks, swa_w_o, rel_bias, mlp_w_up, mlp_w_down, ln_mix_g, ln_mix_b, ln_mlp_g, ln_mlp_b, loss_target, m_mla_w_in, m_mla_g_q, m_mla_g_kv, m_mla_w_uq, m_mla_w_uk, m_mla_w_uv, m_mla_w_o, m_kv_w_shared, m_swa_w_q, m_swa_sinks, m_swa_w_o, m_rel_bias, m_mlp_w_up, m_mlp_w_down, m_ln_mix_g, m_ln_mix_b, m_ln_mlp_g, m_ln_mlp_b, v_mla_w_in, v_mla_g_q, v_mla_g_kv, v_mla_w_uq, v_mla_w_uk, v_mla_w_uv, v_mla_w_o, v_kv_w_shared, v_swa_w_q, v_swa_sinks, v_swa_w_o, v_rel_bias, v_mlp_w_up, v_mlp_w_down, v_ln_mix_g, v_ln_mix_b, v_ln_mlp_g, v_ln_mlp_b):
    names = ["mla_w_in", "mla_g_q", "mla_g_kv", "mla_w_uq", "mla_w_uk", "mla_w_uv", "mla_w_o", "kv_w_shared",
             "swa_w_q", "swa_sinks", "swa_w_o", "rel_bias", "mlp_w_up", "mlp_w_down",
             "ln_mix_g", "ln_mix_b", "ln_mlp_g", "ln_mlp_b"]
    ws = dict(zip(names, [mla_w_in, mla_g_q, mla_g_kv, mla_w_uq, mla_w_uk, mla_w_uv, mla_w_o, kv_w_shared,
                          swa_w_q, swa_sinks, swa_w_o, rel_bias, mlp_w_up, mlp_w_down,
                          ln_mix_g, ln_mix_b, ln_mlp_g, ln_mlp_b]))
    ms = dict(zip(names, [m_mla_w_in, m_mla_g_q, m_mla_g_kv, m_mla_w_uq, m_mla_w_uk, m_mla_w_uv, m_mla_w_o,
                          m_kv_w_shared, m_swa_w_q, m_swa_sinks, m_swa_w_o, m_rel_bias, m_mlp_w_up, m_mlp_w_down,
                          m_ln_mix_g, m_ln_mix_b, m_ln_mlp_g, m_ln_mlp_b]))
    vs = dict(zip(names, [v_mla_w_in, v_mla_g_q, v_mla_g_kv, v_mla_w_uq, v_mla_w_uk, v_mla_w_uv, v_mla_w_o,
                          v_kv_w_shared, v_swa_w_q, v_swa_sinks, v_swa_w_o, v_rel_bias, v_mlp_w_up, v_mlp_w_down,
                          v_ln_mix_g, v_ln_mix_b, v_ln_mlp_g, v_ln_mlp_b]))
    xi, yi, ci = _mesh_pos()
    shard = 2 * xi + yi
    shard_shapes = {n: ws[n].shape for n in ROWS}
    wbf = {n: ws[n].astype(BF16) for n in ROWS}

    early = _pack_group(AG_EARLY, wbf)
    wall = lax.dynamic_update_slice(_allgather_weights(early), early[None], (shard, 0, 0))
    w = _full_from_gathered(AG_EARLY, wall, shard_shapes)
    dist = _Dist(shard=shard, pos=jnp.stack([shard, ci, 2 * shard + ci]).astype(jnp.int32),
                 late_pack=_pack_group(AG_LATE, wbf), shard_shapes=shard_shapes)
    gq_slot = lax.dynamic_update_slice(jnp.zeros((1, QR), F32), mla_g_q, (0, shard * (QR // 4)))
    gkv_slot = lax.dynamic_update_slice(jnp.zeros((1, KVR), F32), mla_g_kv, (0, shard * (KVR // 4)))
    gains = jnp.concatenate([jnp.pad(gq_slot, ((0, 0), (0, D - QR))), jnp.pad(gkv_slot, ((0, 0), (0, D - KVR))),
                             jnp.zeros((SMALL_ROWS - 2, D), F32)], axis=0)
    gains = _allreduce_small(gains * 0.5, "allgather_gains")
    w["mla_g_q"], w["mla_g_kv"] = gains[0, :QR], gains[1, :KVR]
    for n in ("swa_sinks", "rel_bias", "ln_mix_g", "ln_mix_b", "ln_mlp_g", "ln_mlp_b"):
        w[n] = ws[n]

    lpart, grad_x, g, reduced = _fwd_bwd(x[0], loss_target[0], w, dist)

    g["end"] = _grad_shards(RS_END, g).astype(BF16)
    reduced["end"] = _reduce_scatter_finish(_exchange_devices(g["end"], "end"), g["end"], dist.pos, "end")
    reduced["rest"] = jnp.concatenate([reduced["mid"], reduced["end"]], axis=0)

    small_like = {n: g[n] for n, _, _ in SMALL}
    small_sum = _allreduce_small(_pack_small(g, extra_row=lpart), "allreduce_small_grads")
    loss = 0.5 * jnp.sum(small_sum[LOSS_ROW]) / D
    gsm = _unpack_small(small_sum, small_like)
    gsm["mla_g_q"] = lax.dynamic_slice(gsm["mla_g_q"], (0, shard * (QR // 4)), (1, QR // 4))
    gsm["mla_g_kv"] = lax.dynamic_slice(gsm["mla_g_kv"], (0, shard * (KVR // 4)), (1, KVR // 4))

    gbig, dbig, mbig, vbig = {}, {}, {}, {}
    for n in ("mlp_w_up", "mlp_w_down"):
        off = 0 if n == "mlp_w_up" else ROWS["mlp_w_up"]
        gbig[n], dbig[n], mbig[n], vbig[n] = _adamw_layers(
            ws[n], ms[n], vs[n], reduced["mlp0"], reduced["mlp1"], off, f"adamw_{n}")
    rest = RS_MID + RS_END
    outs = _adamw(_pack_group(rest, ws), reduced["rest"], _pack_group(rest, ms), _pack_group(rest, vs),
                  "adamw_rest", tm=_row_tile(reduced["rest"].shape[0]))
    for dst, buf in zip((gbig, dbig, mbig, vbig), (reduced["rest"], *outs)):
        dst.update(_by_name(_unpack_group(rest, buf, ws)))
    dsm, msm, vsm = _adamw(_pack_small(ws), _pack_small(gsm), _pack_small(ms), _pack_small(vs), "adamw_small", tm=16)
    grads = {**gbig, **gsm}
    delta = {**dbig, **_unpack_small(dsm, ws)}
    new_m = {**mbig, **_unpack_small(msm, ws)}
    new_v = {**vbig, **_unpack_small(vsm, ws)}
    grads = {n: grads[n].reshape(ws[n].shape) for n in names}
    return (loss, grad_x[None], *[grads[n] for n in names], *[delta[n] for n in names],
            *[new_m[n] for n in names], *[new_v[n] for n in names])
```

```python
import collections
import math

import numpy as np
import jax
import jax.numpy as jnp
from jax import lax
from jax.experimental import pallas as pl
from jax.experimental.pallas import tpu as pltpu

F32 = jnp.float32
BF16 = jnp.bfloat16
MESH = pl.DeviceIdType.MESH

D = 1024
DFF = 4096
H = 8
NOPE = 128
ROPE = 64
QR = 384
KVR = 256
RP = 128
KD = KVR + RP
HW = 768
QH = 16
KVH = 4
HD = 64
G = QH // KVH
WIN = 128
NBKT = 32
ALPHA = 4.0 ** 0.25
LN_EPS = 1e-5
RMS_EPS = 1e-6
MLA_SCALE = (NOPE + ROPE) ** -0.5
LOG2E = 1.4426950408889634
LN2 = 0.6931471805599453
QSCALE = MLA_SCALE * LOG2E
AHEAD = 1
SWA_SCALE = HD ** -0.5
NEG = -1e30
LR, B1, B2, ADAM_EPS, WD, STEP = 0.001, 0.9, 0.999, 1e-8, 0.01, 10

VMEM_LIMIT = 48 * 1024 * 1024

NN = (((1,), (0,)), ((), ()))
NT = (((1,), (1,)), ((), ()))
TN = (((0,), (0,)), ((), ()))

ROWS = {"mlp_w_up": 1024, "mlp_w_down": 1024, "mla_w_o": 256, "swa_w_q": 256, "swa_w_o": 256,
        "kv_w_shared": 128, "mla_w_in": 176, "mla_w_uq": 144, "mla_w_uk": 64, "mla_w_uv": 64}
AG_EARLY = (("mla_w_in", None), ("mla_w_uq", None), ("mla_w_uk", None), ("mla_w_uv", None), ("mla_w_o", None))
AG_LATE = (("mlp_w_up", 0), ("mlp_w_up", 1), ("mlp_w_down", 0), ("mlp_w_down", 1),
           ("swa_w_q", None), ("swa_w_o", None), ("kv_w_shared", None))
RS_MID = (("mla_w_o", None), ("swa_w_q", None), ("swa_w_o", None), ("kv_w_shared", None), ("mla_w_uv", None))
RS_END = (("mla_w_in", None), ("mla_w_uq", None), ("mla_w_uk", None))
SMALL_ROWS = 16
_Dist = collections.namedtuple("_Dist", "shard pos late_pack shard_shapes")


def _cp(**kw):
    return pltpu.CompilerParams(vmem_limit_bytes=VMEM_LIMIT, **kw)


def _tile(n, pref):
    t = min(n, pref)
    while n % t:
        t -= 128
    return t


def _dot(a, b, dims):
    return lax.dot_general(a, b, dims, preferred_element_type=F32)


def _mm(a, b, mode, name, out_dtype=F32, out_t=False, addend=None, add_scale=1.0, relu2=False, gate_a=None,
        b_view=None, out_view=None, tm=1024, tn=1024, tk=1024):
    blk = 1024
    if b_view is not None:
        kind, b_off = b_view
        assert b.shape[0] == 4 and b.shape[2] == blk and b_off % blk == 0
        bshape = {("cols", "nn"): (blk, 4 * blk), ("cols", "nt"): (blk, 4 * blk),
                  ("rows", "nn"): (4 * blk, blk), ("rows", "nt"): (4 * blk, blk)}[(kind, mode)]
    else:
        bshape = b.shape
    if mode == "nn":
        (m, k), (k2, n) = a.shape, bshape
    elif mode == "nt":
        (m, k), (n, k2) = a.shape, bshape
    else:
        (k, m), (k2, n) = a.shape, bshape
    assert k == k2, (name, a.shape, b.shape)
    tm, tn, tk = _tile(m, tm), _tile(n, tn), _tile(k, tk)
    nk = k // tk
    dims = {"nn": NN, "nt": NT, "tn": TN}[mode]
    if mode == "tn":
        a_spec = pl.BlockSpec((tk, tm), lambda i, j, kk: (kk, i))
    else:
        a_spec = pl.BlockSpec((tm, tk), lambda i, j, kk: (i, kk))
    if b_view is not None:
        assert tn == blk and tk == blk
        ob = b_off // blk
        b_spec = {("cols", "nn"): pl.BlockSpec((None, tk, tn), lambda i, j, kk: (j, ob, 0)),
                  ("cols", "nt"): pl.BlockSpec((None, tn, tk), lambda i, j, kk: (kk, ob, 0)),
                  ("rows", "nn"): pl.BlockSpec((None, tk, tn), lambda i, j, kk: (kk, ob, 0)),
                  ("rows", "nt"): pl.BlockSpec((None, tn, tk), lambda i, j, kk: (j, ob, 0))}[(kind, mode)]
    elif mode == "nt":
        b_spec = pl.BlockSpec((tn, tk), lambda i, j, kk: (j, kk))
    else:
        b_spec = pl.BlockSpec((tk, tn), lambda i, j, kk: (kk, j))
    mn_spec = pl.BlockSpec((tm, tn), lambda i, j, kk: (i, j))
    ins, in_specs = [a, b], [a_spec, b_spec]
    if addend is not None:
        ins.append(addend)
        in_specs.append(mn_spec)
    if gate_a is not None:
        ins.append(gate_a)
        in_specs.append(mn_spec)
    aliases = {}
    if out_view is not None:
        okind, total_rows, o_off, buf = out_view
        assert not out_t and tm == blk and tn == blk and o_off % blk == 0
        oo = o_off // blk
        out_shape = [jax.ShapeDtypeStruct((4, total_rows, blk), out_dtype)]
        if okind == "cols":
            out_specs = [pl.BlockSpec((None, tm, tn), lambda i, j, kk: (j, oo, 0))]
        else:
            out_specs = [pl.BlockSpec((None, tm, tn), lambda i, j, kk: (i, oo, 0))]
        if buf is not None:
            aliases = {len(ins): 0}
            ins.append(buf)
            in_specs.append(pl.BlockSpec(memory_space=pl.ANY))
    elif out_t:
        out_shape = [jax.ShapeDtypeStruct((n, m), out_dtype)]
        out_specs = [pl.BlockSpec((tn, tm), lambda i, j, kk: (j, i))]
    else:
        out_shape = [jax.ShapeDtypeStruct((m, n), out_dtype)]
        out_specs = [mn_spec]
    has_add, has_gate = addend is not None, gate_a is not None

    def kern(*refs):
        a_ref, b_ref = refs[0], refs[1]
        pos = 2
        add_ref = gate_ref = None
        if has_add:
            add_ref = refs[pos]
            pos += 1
        if has_gate:
            gate_ref = refs[pos]
            pos += 1
        o_ref = refs[pos + len(aliases)]
        acc = refs[-1] if nk > 1 else None
        kk = pl.program_id(2)

        def partial():
            return _dot(a_ref[...].astype(BF16), b_ref[...].astype(BF16), dims)

        if nk > 1:
            @pl.when(kk == 0)
            def _():
                acc[...] = partial()

            @pl.when((kk > 0) & (kk < nk - 1))
            def _():
                acc[...] += partial()

        @pl.when(kk == nk - 1)
        def _():
            r = partial() + acc[...] if nk > 1 else partial()
            if has_add:
                r = r + add_scale * add_ref[...].astype(F32)
            if has_gate:
                ga = gate_ref[...].astype(F32)
                r = r * jnp.where(ga > 0.0, (2.0 * ga) * lax.rsqrt(ga), 0.0)
            if relu2:
                hh = jnp.maximum(r, 0.0)
                r = hh * hh
            if out_t:
                r = r.T
            o_ref[...] = r.astype(out_dtype)

    return pl.pallas_call(
        kern, out_shape=out_shape, grid=(m // tm, n // tn, nk), in_specs=in_specs, out_specs=out_specs,
        scratch_shapes=[pltpu.VMEM((tm, tn), F32)] if nk > 1 else [], input_output_aliases=aliases,
        name=name, compiler_params=_cp())(*ins)[0]


def _add_ln(res, y, g, b, name, res_affine=None, tm=512):
    t = res.shape[0]
    tm = min(tm, t)
    affine = res_affine is not None

    def kern(*refs):
        if affine:
            x_ref, y_ref, g_ref, b_ref, g0_ref, b0_ref, ob_ref, xh_ref, r_ref = refs
            x = x_ref[...] * g0_ref[...] + b0_ref[...]
        else:
            x_ref, y_ref, g_ref, b_ref, ob_ref, xh_ref, r_ref = refs
            x = x_ref[...]
        z = ALPHA * x + y_ref[...]
        mu = jnp.mean(z, axis=-1, keepdims=True)
        zc = z - mu
        var = jnp.mean(zc * zc, axis=-1, keepdims=True)
        r = lax.rsqrt(var + LN_EPS)
        xh = zc * r
        ob_ref[...] = (xh * g_ref[...] + b_ref[...]).astype(BF16)
        xh_ref[...] = xh
        r_ref[...] = r

    row = pl.BlockSpec((tm, D), lambda i: (i, 0))
    vec = pl.BlockSpec((1, D), lambda i: (0, 0))
    st = pl.BlockSpec((tm, 1), lambda i: (i, 0))
    ins = [res, y, g.reshape(1, D), b.reshape(1, D)]
    if affine:
        ins += [res_affine[0].reshape(1, D), res_affine[1].reshape(1, D)]
    return pl.pallas_call(
        kern, grid=(t // tm,), in_specs=[row, row] + [vec] * (len(ins) - 2), out_specs=[row, row, st],
        out_shape=[jax.ShapeDtypeStruct((t, D), BF16), jax.ShapeDtypeStruct((t, D), F32),
                   jax.ShapeDtypeStruct((t, 1), F32)],
        name=name, compiler_params=_cp())(*ins)


def _ln_bwd(dout, xhat, rstd, g, name, loss_b=None, tm=512):
    t = dout.shape[0]
    tm = min(tm, t)
    head = loss_b is not None

    def kern(*refs):
        if head:
            do_ref, xh_ref, r_ref, g_ref, b_ref, dz_ref, dzb_ref, dg_ref, db_ref, l_ref = refs
        else:
            do_ref, xh_ref, r_ref, g_ref, dz_ref, dzb_ref, dg_ref, db_ref = refs

        @pl.when(pl.program_id(0) == 0)
        def _():
            dg_ref[...] = jnp.zeros_like(dg_ref)
            db_ref[...] = jnp.zeros_like(db_ref)
            if head:
                l_ref[...] = jnp.zeros_like(l_ref)

        xh = xh_ref[...]
        if head:
            e = xh * g_ref[...] + b_ref[...] - do_ref[...]
            l_ref[...] += jnp.sum(e * e, axis=0, keepdims=True)
            do = e * (1.0 / D)
        else:
            do = do_ref[...]
        dxh = do * g_ref[...]
        m1 = jnp.mean(dxh, axis=-1, keepdims=True)
        m2 = jnp.mean(dxh * xh, axis=-1, keepdims=True)
        dz = r_ref[...] * (dxh - m1 - xh * m2)
        dz_ref[...] = dz
        dzb_ref[...] = dz.astype(BF16)
        dg_ref[...] += jnp.sum(do * xh, axis=0, keepdims=True)
        db_ref[...] += jnp.sum(do, axis=0, keepdims=True)

    row = pl.BlockSpec((tm, D), lambda i: (i, 0))
    vec = pl.BlockSpec((1, D), lambda i: (0, 0))
    st = pl.BlockSpec((tm, 1), lambda i: (i, 0))
    ins = [dout, xhat, rstd, g.reshape(1, D)] + ([loss_b.reshape(1, D)] if head else [])
    return pl.pallas_call(
        kern, grid=(t // tm,), in_specs=[row, row, st] + [vec] * (len(ins) - 3),
        out_specs=[row, row, vec, vec] + ([vec] if head else []),
        out_shape=[jax.ShapeDtypeStruct((t, D), F32), jax.ShapeDtypeStruct((t, D), BF16)]
        + [jax.ShapeDtypeStruct((1, D), F32)] * (3 if head else 2),
        name=name, compiler_params=_cp())(*ins)


def _rope_tables(t):
    half = ROPE // 2
    inv = 10000.0 ** (-jnp.arange(half, dtype=F32) / half)
    ang = jnp.arange(t).astype(F32)[:, None] * inv[None, :]
    cos, sin = jnp.cos(ang), jnp.sin(ang)
    z = jnp.zeros((t, RP - ROPE), F32)
    return jnp.concatenate([cos, cos, z], axis=1), jnp.concatenate([-sin, sin, z], axis=1)


def _swap_halves(x):
    lane = lax.broadcasted_iota(jnp.int32, x.shape, 1)
    return jnp.where(lane < ROPE // 2, pltpu.roll(x, RP - ROPE // 2, 1), pltpu.roll(x, ROPE // 2, 1))


def _rope(x, cos, sin):
    return x * cos + _swap_halves(x) * sin


def _rope_t(gy, cos, sin):
    return gy * cos + _swap_halves(gy * sin)


def _mla_pre(hh, g_q, g_kv, cos, sin, tm=512):
    t = hh.shape[0]
    tm = min(tm, t)

    def kern(h_ref, gq_ref, gkv_ref, c_ref, s_ref, cq_ref, k_ref):
        xq = h_ref[:, 0:QR]
        rq = lax.rsqrt(jnp.mean(xq * xq, axis=-1, keepdims=True) + RMS_EPS)
        cq_ref[...] = (xq * rq * gq_ref[...]).astype(BF16)
        xk = h_ref[:, QR:QR + KVR]
        rk = lax.rsqrt(jnp.mean(xk * xk, axis=-1, keepdims=True) + RMS_EPS)
        k_ref[:, 0:KVR] = (xk * rk * gkv_ref[...]).astype(BF16)
        k_ref[:, KVR:KD] = _rope(h_ref[:, QR + KVR:HW], c_ref[...], s_ref[...]).astype(BF16)

    return pl.pallas_call(
        kern, grid=(t // tm,),
        in_specs=[pl.BlockSpec((tm, HW), lambda i: (i, 0)), pl.BlockSpec((1, QR), lambda i: (0, 0)),
                  pl.BlockSpec((1, KVR), lambda i: (0, 0)), pl.BlockSpec((tm, RP), lambda i: (i, 0)),
                  pl.BlockSpec((tm, RP), lambda i: (i, 0))],
        out_specs=[pl.BlockSpec((tm, QR), lambda i: (i, 0)), pl.BlockSpec((tm, KD), lambda i: (i, 0))],
        out_shape=[jax.ShapeDtypeStruct((t, QR), BF16), jax.ShapeDtypeStruct((t, KD), BF16)],
        name="mla_pre", compiler_params=_cp())(hh, g_q.reshape(1, QR), g_kv.reshape(1, KVR), cos, sin)


def _mla_pre_bwd(hh, dcq, dk, g_q, g_kv, cos, sin, tm=512):
    t = hh.shape[0]
    tm = min(tm, t)

    def rms_bwd(x, dy, g):
        r = lax.rsqrt(jnp.mean(x * x, axis=-1, keepdims=True) + RMS_EPS)
        gdy = dy * g
        dx = r * gdy - x * (r * r * r) * jnp.mean(gdy * x, axis=-1, keepdims=True)
        return dx, jnp.sum(dy * x * r, axis=0, keepdims=True)

    def kern(h_ref, dcq_ref, dk_ref, gq_ref, gkv_ref, c_ref, s_ref, dh_ref, dgq_ref, dgkv_ref):
        @pl.when(pl.program_id(0) == 0)
        def _():
            dgq_ref[...] = jnp.zeros_like(dgq_ref)
            dgkv_ref[...] = jnp.zeros_like(dgkv_ref)

        dxq, dgq = rms_bwd(h_ref[:, 0:QR], dcq_ref[...], gq_ref[...])
        dxk, dgk = rms_bwd(h_ref[:, QR:QR + KVR], dk_ref[:, 0:KVR], gkv_ref[...])
        dh_ref[:, 0:QR] = dxq.astype(BF16)
        dh_ref[:, QR:QR + KVR] = dxk.astype(BF16)
        dh_ref[:, QR + KVR:HW] = _rope_t(dk_ref[:, KVR:KD], c_ref[...], s_ref[...]).astype(BF16)
        dgq_ref[...] += dgq
        dgkv_ref[...] += dgk

    return pl.pallas_call(
        kern, grid=(t // tm,),
        in_specs=[pl.BlockSpec((tm, HW), lambda i: (i, 0)), pl.BlockSpec((tm, QR), lambda i: (i, 0)),
                  pl.BlockSpec((tm, KD), lambda i: (i, 0)), pl.BlockSpec((1, QR), lambda i: (0, 0)),
                  pl.BlockSpec((1, KVR), lambda i: (0, 0)), pl.BlockSpec((tm, RP), lambda i: (i, 0)),
                  pl.BlockSpec((tm, RP), lambda i: (i, 0))],
        out_specs=[pl.BlockSpec((tm, HW), lambda i: (i, 0)), pl.BlockSpec((1, QR), lambda i: (0, 0)),
                   pl.BlockSpec((1, KVR), lambda i: (0, 0))],
        out_shape=[jax.ShapeDtypeStruct((t, HW), BF16), jax.ShapeDtypeStruct((1, QR), F32),
                   jax.ShapeDtypeStruct((1, KVR), F32)],
        name="mla_pre_bwd", compiler_params=_cp())(hh, dcq, dk, g_q.reshape(1, QR), g_kv.reshape(1, KVR), cos, sin)


def _q_prep(q2, wuk_t, cos, sin, tm=512):
    t = q2.shape[0]
    tm = min(tm, t)

    def kern(q_ref, w_ref, c_ref, s_ref, o_ref):
        cos_, sin_ = c_ref[...], s_ref[...]
        for h in range(H):
            qn = q_ref[:, h * NOPE:(h + 1) * NOPE].astype(BF16)
            o_ref[:, h * KD:h * KD + KVR] = (_dot(qn, w_ref[h], NN) * QSCALE).astype(BF16)
            qr = q_ref[:, H * NOPE + h * RP:H * NOPE + (h + 1) * RP]
            o_ref[:, h * KD + KVR:(h + 1) * KD] = (_rope(qr, cos_, sin_) * QSCALE).astype(BF16)

    return pl.pallas_call(
        kern, grid=(t // tm,),
        in_specs=[pl.BlockSpec((tm, 2 * H * NOPE), lambda i: (i, 0)), pl.BlockSpec((H, NOPE, KVR), lambda i: (0, 0, 0)),
                  pl.BlockSpec((tm, RP), lambda i: (i, 0)), pl.BlockSpec((tm, RP), lambda i: (i, 0))],
        out_specs=pl.BlockSpec((tm, H * KD), lambda i: (i, 0)),
        out_shape=jax.ShapeDtypeStruct((t, H * KD), BF16),
        name="q_prep", compiler_params=_cp())(q2, wuk_t, cos, sin)


def _q_prep_bwd(dq_cat, q2, wuk_h, cos, sin, tm=512):
    t = q2.shape[0]
    tm = min(tm, t)

    def kern(dq_ref, q_ref, w_ref, c_ref, s_ref, o_ref, dw_ref):
        @pl.when(pl.program_id(0) == 0)
        def _():
            dw_ref[...] = jnp.zeros_like(dw_ref)

        cos_, sin_ = c_ref[...], s_ref[...]
        for h in range(H):
            dql = dq_ref[:, h * KD:h * KD + KVR].astype(BF16)
            o_ref[:, h * NOPE:(h + 1) * NOPE] = _dot(dql, w_ref[h], NN).astype(BF16)
            dqr = dq_ref[:, h * KD + KVR:(h + 1) * KD]
            o_ref[:, H * NOPE + h * RP:H * NOPE + (h + 1) * RP] = _rope_t(dqr, cos_, sin_).astype(BF16)
            qn = q_ref[:, h * NOPE:(h + 1) * NOPE].astype(BF16)
            dw_ref[h] += _dot(qn, dql, TN)

    return pl.pallas_call(
        kern, grid=(t // tm,),
        in_specs=[pl.BlockSpec((tm, H * KD), lambda i: (i, 0)), pl.BlockSpec((tm, 2 * H * NOPE), lambda i: (i, 0)),
                  pl.BlockSpec((H, KVR, NOPE), lambda i: (0, 0, 0)),
                  pl.BlockSpec((tm, RP), lambda i: (i, 0)), pl.BlockSpec((tm, RP), lambda i: (i, 0))],
        out_specs=[pl.BlockSpec((tm, 2 * H * NOPE), lambda i: (i, 0)), pl.BlockSpec((H, NOPE, KVR), lambda i: (0, 0, 0))],
        out_shape=[jax.ShapeDtypeStruct((t, 2 * H * NOPE), BF16), jax.ShapeDtypeStruct((H, NOPE, KVR), F32)],
        name="q_prep_bwd", compiler_params=_cp())(dq_cat, q2, wuk_h, cos, sin)


def _o_up_bwd(do, o_lat, wuv_h, tm=512):
    t = do.shape[0]
    tm = min(tm, t)

    def kern(do_ref, x_ref, w_ref, dx_ref, dw_ref, dlt_ref):
        @pl.when(pl.program_id(0) == 0)
        def _():
            dw_ref[...] = jnp.zeros_like(dw_ref)

        for h in range(H):
            dh_ = do_ref[:, h * NOPE:(h + 1) * NOPE]
            x = x_ref[:, h * KVR:(h + 1) * KVR]
            dx = _dot(dh_, w_ref[h], NT)
            dx_ref[:, h * KVR:(h + 1) * KVR] = dx.astype(BF16)
            dw_ref[h] += _dot(x.astype(BF16), dh_, TN)
            dl = jnp.broadcast_to(jnp.sum(dx * x, axis=1)[:, None], (tm, 128))
            dlt_ref[h] = dl.T[0:1, :]

    return pl.pallas_call(
        kern, grid=(t // tm,),
        in_specs=[pl.BlockSpec((tm, H * NOPE), lambda i: (i, 0)), pl.BlockSpec((tm, H * KVR), lambda i: (i, 0)),
                  pl.BlockSpec((H, KVR, NOPE), lambda i: (0, 0, 0))],
        out_specs=[pl.BlockSpec((tm, H * KVR), lambda i: (i, 0)), pl.BlockSpec((H, KVR, NOPE), lambda i: (0, 0, 0)),
                   pl.BlockSpec((H, 1, tm), lambda i: (0, 0, i))],
        out_shape=[jax.ShapeDtypeStruct((t, H * KVR), BF16), jax.ShapeDtypeStruct((H, KVR, NOPE), F32),
                   jax.ShapeDtypeStruct((H, 1, t), F32)],
        name="o_up_bwd", compiler_params=_cp())(do, o_lat, wuv_h)


def _causal_pairs(nq):
    return [(i, j) for i in range(nq) for j in range(i + 1)]


def _lane_tile(stat, width):
    return jnp.tile(stat, (1, width // 128))


def _flash_fwd(qcat, kc, wuv_h, bq, hb, gather=None):
    t = kc.shape[0]
    nq = t // bq
    pairs = _causal_pairs(nq)
    itab = jnp.asarray(np.array([p[0] for p in pairs], np.int32))
    jtab = jnp.asarray(np.array([p[1] for p in pairs], np.int32))

    ng = H // hb
    hosting = gather is not None

    def kern(it, jt, q_ref, k_ref, wuv_ref, *rest):
        if hosting:
            w_ref, o_ref, lset_ref, oup_ref, wall_ref, m_sc, l_sc, acc_sc, send_sems, recv_sems = rest
            ag_start, ag_forward, ag_finish = _allgather_schedule(w_ref, wall_ref, send_sems, recv_sems)
        else:
            o_ref, lset_ref, oup_ref, m_sc, l_sc, acc_sc = rest
        grp = pl.program_id(0)
        st = pl.program_id(1)
        i, j = it[st], jt[st]

        if hosting:
            @pl.when((grp == 0) & (st == 0))
            def _():
                ag_start()

        @pl.when(j == 0)
        def _():
            m_sc[...] = jnp.full_like(m_sc, NEG)
            l_sc[...] = jnp.zeros_like(l_sc)
            acc_sc[...] = jnp.zeros_like(acc_sc)

        def update(masked):
            k = k_ref[...]
            v = k[:, 0:KVR]
            if masked:
                row = lax.broadcasted_iota(jnp.int32, (bq, bq), 0)
                col = lax.broadcasted_iota(jnp.int32, (bq, bq), 1)
                keep = col <= row
            pending = [_dot(q_ref[:, hh * KD:(hh + 1) * KD], k, NT) for hh in range(min(AHEAD, hb))]
            for hh in range(hb):
                s = pending.pop(0)
                if hh + AHEAD < hb:
                    pending.append(_dot(q_ref[:, (hh + AHEAD) * KD:(hh + AHEAD + 1) * KD], k, NT))
                if masked:
                    s = jnp.where(keep, s, NEG)
                m_prev = m_sc[hh]
                m_next = jnp.maximum(m_prev, jnp.max(s, axis=1)[:, None])
                p = jnp.exp2(s - _lane_tile(m_next, bq))
                a = jnp.exp2(m_prev - m_next)
                l_sc[hh] = a * l_sc[hh] + jnp.sum(p, axis=1)[:, None]
                acc_sc[hh] = _lane_tile(a, KVR) * acc_sc[hh] + _dot(p.astype(BF16), v, NN)
                m_sc[hh] = m_next

        @pl.when(j < i)
        def _():
            update(False)

        @pl.when(j == i)
        def _():
            update(True)
            for hh in range(hb):
                l = l_sc[hh]
                o_h = acc_sc[hh] / _lane_tile(l, KVR)
                o_ref[:, hh * KVR:(hh + 1) * KVR] = o_h
                oup_ref[:, hh * NOPE:(hh + 1) * NOPE] = _dot(o_h.astype(BF16), wuv_ref[hh], NN).astype(BF16)
                lset_ref[hh] = (m_sc[hh] + jnp.log2(l)).T[0:1, :]

        if hosting:
            half_way = (ng * len(pairs)) // 2

            @pl.when(grp * len(pairs) + st == half_way)
            def _():
                ag_forward()

            @pl.when((grp == ng - 1) & (st == len(pairs) - 1))
            def _():
                ag_finish()

    in_specs = [pl.BlockSpec((bq, hb * KD), lambda g, s, it, jt: (it[s], g)),
                pl.BlockSpec((bq, KD), lambda g, s, it, jt: (jt[s], 0)),
                pl.BlockSpec((hb, KVR, NOPE), lambda g, s, it, jt: (g, 0, 0))]
    out_specs = [pl.BlockSpec((bq, hb * KVR), lambda g, s, it, jt: (it[s], g)),
                 pl.BlockSpec((hb, 1, bq), lambda g, s, it, jt: (g, 0, it[s])),
                 pl.BlockSpec((bq, hb * NOPE), lambda g, s, it, jt: (it[s], g))]
    out_shape = [jax.ShapeDtypeStruct((t, H * KVR), F32), jax.ShapeDtypeStruct((H, 1, t), F32),
                 jax.ShapeDtypeStruct((t, H * NOPE), BF16)]
    scratch = [pltpu.VMEM((hb, bq, 128), F32), pltpu.VMEM((hb, bq, 128), F32), pltpu.VMEM((hb, bq, KVR), F32)]
    args = [itab, jtab, qcat, kc, wuv_h]
    if hosting:
        in_specs.append(ANY)
        out_specs.append(ANY)
        out_shape.append(jax.ShapeDtypeStruct((4,) + gather.shape, gather.dtype))
        scratch += AG_SEMS
        args.append(gather)
    gs = pltpu.PrefetchScalarGridSpec(num_scalar_prefetch=2, grid=(ng, len(pairs)), in_specs=in_specs,
                                      out_specs=out_specs, scratch_shapes=scratch)
    return pl.pallas_call(kern, grid_spec=gs, out_shape=out_shape, name="mla_flash_fwd",
                          compiler_params=_cp())(*args)


def _flash_dkv(qcat, kc, do_lat, lse_t, delta_t, bq, hb, exchange=()):
    nx = len(exchange)
    t = kc.shape[0]
    nq = t // bq
    ng = H // hb
    npairs = nq * (nq + 1) // 2
    steps = [(j, g, i) for j in range(nq) for g in range(ng) for i in range(j, nq)]
    jtab = jnp.asarray(np.array([s[0] for s in steps], np.int32))
    gtab = jnp.asarray(np.array([s[1] for s in steps], np.int32))
    itab = jnp.asarray(np.array([s[2] for s in steps], np.int32))
    ptab = jnp.asarray(np.array([s[2] * (s[2] + 1) // 2 + s[0] for s in steps], np.int32))

    def kern(jt, gt, it, pt, q_ref, k_ref, do_ref, lset_ref, dlt_ref, *rest):
        p_refs, (dk_ref, ds_ref), slots_refs = rest[:nx], rest[nx:nx + 2], rest[nx + 2:2 * nx + 2]
        dk_sc, dv_sc = rest[2 * nx + 2:2 * nx + 4]
        sems = rest[2 * nx + 4:]
        hooks = [_device_exchange_schedule(p_refs[e], slots_refs[e], sems[2 * e], sems[2 * e + 1]) for e in range(nx)]
        st = pl.program_id(0)
        j, g, i = jt[st], gt[st], it[st]

        if nx:
            @pl.when(st == 0)
            def _():
                for start, _ in hooks:
                    start()

        @pl.when((g == 0) & (i == j))
        def _():
            dk_sc[...] = jnp.zeros_like(dk_sc)
            dv_sc[...] = jnp.zeros_like(dv_sc)

        def update(masked):
            k = k_ref[...]
            v = k[:, 0:KVR]
            if masked:
                row = lax.broadcasted_iota(jnp.int32, (bq, bq), 0)
                col = lax.broadcasted_iota(jnp.int32, (bq, bq), 1)
                keep = row <= col

            def first_matmuls(hh):
                dob = do_ref[:, hh * KVR:(hh + 1) * KVR].astype(BF16)
                return _dot(k, q_ref[:, hh * KD:(hh + 1) * KD], NT), _dot(v, dob, NT), dob

            pending = [first_matmuls(hh) for hh in range(min(AHEAD, hb))]
            for hh in range(hb):
                s, dp, dob = pending.pop(0)
                if hh + AHEAD < hb:
                    pending.append(first_matmuls(hh + AHEAD))
                if masked:
                    s = jnp.where(keep, s, NEG)
                p = jnp.exp2(s - lset_ref[hh])
                dv_sc[...] += _dot(p.astype(BF16), dob, NN)
                dsb = (p * (dp - dlt_ref[hh])).astype(BF16)
                ds_ref[0, 0, hh] = dsb
                dk_sc[...] += _dot(dsb, q_ref[:, hh * KD:(hh + 1) * KD], NN)

        @pl.when(i > j)
        def _():
            update(False)

        @pl.when(i == j)
        def _():
            update(True)

        @pl.when((g == ng - 1) & (i == nq - 1))
        def _():
            dk_ref[:, 0:KVR] = dk_sc[:, 0:KVR] * LN2 + dv_sc[...]
            dk_ref[:, KVR:KD] = dk_sc[:, KVR:KD] * LN2

        if nx:
            @pl.when(st == len(steps) - 1)
            def _():
                for _, finish in hooks:
                    finish()

    in_specs = [pl.BlockSpec((bq, hb * KD), lambda s, jt, gt, it, pt: (it[s], gt[s])),
                pl.BlockSpec((bq, KD), lambda s, jt, gt, it, pt: (jt[s], 0)),
                pl.BlockSpec((bq, hb * KVR), lambda s, jt, gt, it, pt: (it[s], gt[s])),
                pl.BlockSpec((hb, 1, bq), lambda s, jt, gt, it, pt: (gt[s], 0, it[s])),
                pl.BlockSpec((hb, 1, bq), lambda s, jt, gt, it, pt: (gt[s], 0, it[s]))] + [ANY] * nx
    out_specs = [pl.BlockSpec((bq, KD), lambda s, jt, gt, it, pt: (jt[s], 0)),
                 pl.BlockSpec((1, 1, hb, bq, bq), lambda s, jt, gt, it, pt: (gt[s], pt[s], 0, 0, 0))] + [ANY] * nx
    out_shape = [jax.ShapeDtypeStruct((t, KD), F32), jax.ShapeDtypeStruct((ng, npairs, hb, bq, bq), BF16)]
    out_shape += [jax.ShapeDtypeStruct((8, e.shape[1] // 2, D), e.dtype) for e in exchange]
    scratch = [pltpu.VMEM((bq, KD), F32), pltpu.VMEM((bq, KVR), F32)] + ALL_SEMS * nx
    args = [jtab, gtab, itab, ptab, qcat, kc, do_lat, lse_t, delta_t, *exchange]
    gs = pltpu.PrefetchScalarGridSpec(num_scalar_prefetch=4, grid=(len(steps),), in_specs=in_specs,
                                      out_specs=out_specs, scratch_shapes=scratch)
    return pl.pallas_call(kern, grid_spec=gs, out_shape=out_shape, name="mla_flash_dkv",
                          compiler_params=_cp())(*args)


def _flash_dq(ds_all, kc_t, bq, exchange=None):
    nq = kc_t.shape[0]
    t = nq * bq
    ngrp, _, hper = ds_all.shape[:3]
    pairs = _causal_pairs(nq)
    itab = jnp.asarray(np.array([p[0] for p in pairs], np.int32))
    jtab = jnp.asarray(np.array([p[1] for p in pairs], np.int32))
    hosting = exchange is not None

    def kern(it, jt, *refs):
        ds_refs, kt_ref, rest = refs[:ngrp], refs[ngrp], refs[ngrp + 1:]
        if hosting:
            p_ref, dq_ref, slots_ref, acc_sc, send_sems, recv_sems = rest
            xc_start, xc_finish = _device_exchange_schedule(p_ref, slots_ref, send_sems, recv_sems)
        else:
            dq_ref, acc_sc = rest
        st = pl.program_id(0)
        i, j = it[st], jt[st]
        kt = kt_ref[...]

        def ds(hh):
            return ds_refs[hh // hper][0, 0, hh % hper]

        if hosting:
            @pl.when(st == 0)
            def _():
                xc_start()

        @pl.when(j == 0)
        def _():
            for hh in range(H):
                acc_sc[hh] = _dot(kt, ds(hh), NN)

        @pl.when((j > 0) & (j < i))
        def _():
            for hh in range(H):
                acc_sc[hh] += _dot(kt, ds(hh), NN)

        @pl.when(j == i)
        def _():
            for hh in range(H):
                tot = _dot(kt, ds(hh), NN)
                tot = jnp.where(i > 0, tot + acc_sc[hh], tot)
                dq_ref[:, hh * KD:(hh + 1) * KD] = tot.T * MLA_SCALE

        if hosting:
            @pl.when(st == len(pairs) - 1)
            def _():
                xc_finish()

    def group(gi):
        return pl.BlockSpec((1, 1, hper, bq, bq), lambda s, it, jt: (gi, s, 0, 0, 0))

    in_specs = [group(gi) for gi in range(ngrp)] + [pl.BlockSpec((None, KD, bq), lambda s, it, jt: (jt[s], 0, 0))]
    out_specs = [pl.BlockSpec((bq, H * KD), lambda s, it, jt: (it[s], 0))]
    out_shape = [jax.ShapeDtypeStruct((t, H * KD), F32)]
    scratch = [pltpu.VMEM((H, KD, bq), F32)]
    args = [itab, jtab] + [ds_all] * ngrp + [kc_t]
    if hosting:
        in_specs.append(ANY)
        out_specs.append(ANY)
        out_shape.append(jax.ShapeDtypeStruct((8, exchange.shape[1] // 2, D), exchange.dtype))
        scratch += ALL_SEMS
        args.append(exchange)
    gs = pltpu.PrefetchScalarGridSpec(num_scalar_prefetch=2, grid=(len(pairs),), in_specs=in_specs,
                                      out_specs=out_specs, scratch_shapes=scratch)
    outs = pl.pallas_call(kern, grid_spec=gs, out_shape=out_shape, name="mla_flash_dq",
                          compiler_params=_cp())(*args)
    return outs if hosting else outs[0]


def _bucket_table():
    d = np.arange(WIN)
    max_exact = NBKT // 2
    nf = np.maximum(d, 1).astype(np.float32)
    large = max_exact + (np.log(nf / np.float32(max_exact)) / np.float32(math.log(WIN / max_exact))
                         * np.float32(NBKT - max_exact)).astype(np.int32)
    large = np.minimum(large, NBKT - 1)
    bucket = np.where(d < max_exact, d, large).astype(np.int32)
    jj = np.arange(2 * WIN)[:, None]
    ii = np.arange(WIN)[None, :]
    dist = ii + WIN - jj
    valid = (dist >= 0) & (dist < WIN)
    return np.where(valid, bucket[np.clip(dist, 0, WIN - 1)], -1).astype(np.int32)


def _bias_build(rel_bias, bkt):
    def kern(bk_ref, rb_ref, o_ref):
        bk = bk_ref[...]
        for hd in range(QH):
            acc = jnp.full((2 * WIN, WIN), NEG, F32)
            for b in range(NBKT):
                acc = jnp.where(bk == b, rb_ref[b, hd], acc)
            o_ref[hd] = acc

    return pl.pallas_call(
        kern, in_specs=[pl.BlockSpec(memory_space=pltpu.VMEM), pl.BlockSpec(memory_space=pltpu.SMEM)],
        out_specs=pl.BlockSpec(memory_space=pltpu.VMEM),
        out_shape=jax.ShapeDtypeStruct((QH, 2 * WIN, WIN), F32), name="swa_bias_build")(bkt, rel_bias)


def _bias_bwd(dbias, bkt):
    def kern(db_ref, bk_ref, o_ref):
        bk = bk_ref[...]
        for hd in range(QH):
            g = db_ref[hd]
            for b in range(NBKT):
                r = b * QH + hd
                o_ref[r:r + 1, :] = jnp.sum(jnp.where(bk == b, g, 0.0), axis=0, keepdims=True)

    return pl.pallas_call(
        kern, in_specs=[pl.BlockSpec(memory_space=pltpu.VMEM), pl.BlockSpec(memory_space=pltpu.VMEM)],
        out_specs=pl.BlockSpec(memory_space=pltpu.VMEM),
        out_shape=jax.ShapeDtypeStruct((NBKT * QH, WIN), F32), name="swa_bias_bwd")(dbias, bkt)


def _swa_finish_scores(raw, bias, first):
    s = raw * SWA_SCALE + bias
    if first is not None:
        row = lax.broadcasted_iota(jnp.int32, s.shape, 0)
        s = jnp.where(jnp.logical_or(jnp.logical_not(first), row >= WIN), s, NEG)
    return s


def _swa_fwd(qkv_t, bias, sinks, qb):
    t = qkv_t.shape[1]
    w = qb * WIN
    nst = t // w

    def kern(q_ref, kc_ref, kp_ref, vc_ref, vp_ref, b_ref, sk_ref, o_ref, lse_ref):
        n = pl.program_id(0)
        kfull = jnp.concatenate([kp_ref[...], kc_ref[...]], axis=1)
        vfull = jnp.concatenate([vp_ref[...], vc_ref[...]], axis=1)
        head_row = lax.broadcasted_iota(jnp.int32, (QH, WIN), 0)
        groups = [(b, kh) for b in range(qb) for kh in range(KVH)]

        def raw_scores(b, kh):
            k_band = kfull[kh * HD:(kh + 1) * HD, b * WIN:(b + 2) * WIN]
            return [_dot(k_band, q_ref[(kh * G + g) * HD:(kh * G + g + 1) * HD, b * WIN:(b + 1) * WIN], TN)
                    for g in range(G)]

        o_rows = [[] for _ in range(qb)]
        lse_tiles = [jnp.zeros((QH, WIN), F32) for _ in range(qb)]
        pending = [raw_scores(*grp) for grp in groups[:AHEAD]]
        for gi, (b, kh) in enumerate(groups):
            scores = pending.pop(0)
            if gi + AHEAD < len(groups):
                pending.append(raw_scores(*groups[gi + AHEAD]))
            v_band = vfull[kh * HD:(kh + 1) * HD, b * WIN:(b + 2) * WIN]
            for g in range(G):
                hd = kh * G + g
                s = _swa_finish_scores(scores[g], b_ref[hd], (n == 0) if b == 0 else None)
                sink = sk_ref[hd]
                m = jnp.maximum(jnp.max(s, axis=0, keepdims=True), sink)
                p = jnp.exp(s - m)
                den = jnp.sum(p, axis=0, keepdims=True) + jnp.exp(sink - m)
                p = p / den
                o_rows[b].append(_dot(v_band, p.astype(BF16), NN))
                lse_tiles[b] = jnp.where(head_row == hd, m + jnp.log(den), lse_tiles[b])
        o_ref[...] = jnp.concatenate([jnp.concatenate(rows, axis=0) for rows in o_rows], axis=1)
        lse_ref[...] = jnp.concatenate(lse_tiles, axis=1)

    prev = lambda r: (lambda n: (r, jnp.maximum(n * qb - 1, 0)))
    return pl.pallas_call(
        kern, grid=(nst,),
        in_specs=[pl.BlockSpec((QH * HD, w), lambda n: (0, n)),
                  pl.BlockSpec((KVH * HD, w), lambda n: (4, n)), pl.BlockSpec((KVH * HD, WIN), prev(4)),
                  pl.BlockSpec((KVH * HD, w), lambda n: (5, n)), pl.BlockSpec((KVH * HD, WIN), prev(5)),
                  pl.BlockSpec((QH, 2 * WIN, WIN), lambda n: (0, 0, 0)),
                  pl.BlockSpec(memory_space=pltpu.SMEM)],
        out_specs=[pl.BlockSpec((QH * HD, w), lambda n: (0, n)), pl.BlockSpec((QH, w), lambda n: (0, n))],
        out_shape=[jax.ShapeDtypeStruct((QH * HD, t), F32), jax.ShapeDtypeStruct((QH, t), F32)],
        name="swa_fwd", compiler_params=_cp())(qkv_t, qkv_t, qkv_t, qkv_t, qkv_t, bias, sinks)


def _swa_bwd(qkv_t, do_t, o_t, lse, bias, sinks, qb):
    t = qkv_t.shape[1]
    w = qb * WIN
    nst = t // w
    nblk = t // WIN

    def kern(q_ref, kc_ref, kp_ref, vc_ref, vp_ref, do_ref, o_ref, lse_ref, qn_ref, don_ref, on_ref, lsen_ref,
             b_ref, sk_ref, dqkv_ref, db_ref, dsk_ref):
        n = pl.program_id(0)

        @pl.when(n == 0)
        def _():
            db_ref[...] = jnp.zeros_like(db_ref)
            dsk_ref[...] = jnp.zeros_like(dsk_ref)

        kfull = jnp.concatenate([kp_ref[...], kc_ref[...]], axis=1)
        vfull = jnp.concatenate([vp_ref[...], vc_ref[...]], axis=1)
        head_row = lax.broadcasted_iota(jnp.int32, (QH, WIN), 0)
        db_acc = [None] * QH
        dsk_tile = jnp.zeros((QH, WIN), F32)
        prev_part = [[[None] * qb for _ in range(KVH)] for _ in range(2)]
        cur_part = [[[None] * qb for _ in range(KVH)] for _ in range(2)]
        groups = [(b, kh) for b in range(qb) for kh in range(KVH)]

        def first_matmuls(b, kh):
            k_band = kfull[kh * HD:(kh + 1) * HD, b * WIN:(b + 2) * WIN]
            v_band = vfull[kh * HD:(kh + 1) * HD, b * WIN:(b + 2) * WIN]
            out = []
            for g in range(G):
                rs = slice((kh * G + g) * HD, (kh * G + g + 1) * HD)
                dob = do_ref[rs, b * WIN:(b + 1) * WIN].astype(BF16)
                out.append((_dot(k_band, q_ref[rs, b * WIN:(b + 1) * WIN], TN), _dot(v_band, dob, TN), dob))
            return out

        dq_rows = [[] for _ in range(qb)]
        pending = [first_matmuls(*grp) for grp in groups[:AHEAD]]
        for gi, (b, kh) in enumerate(groups):
            first = pending.pop(0)
            if gi + AHEAD < len(groups):
                pending.append(first_matmuls(*groups[gi + AHEAD]))
            cs = slice(b * WIN, (b + 1) * WIN)
            k_band = kfull[kh * HD:(kh + 1) * HD, b * WIN:(b + 2) * WIN]
            dk_b = dv_b = None
            for g in range(G):
                hd = kh * G + g
                rs = slice(hd * HD, (hd + 1) * HD)
                raw, dp, dob = first[g]
                lse_h = lse_ref[hd:hd + 1, cs]
                s = _swa_finish_scores(raw, b_ref[hd], (n == 0) if b == 0 else None)
                p = jnp.exp(s - lse_h)
                dl = jnp.sum(do_ref[rs, cs] * o_ref[rs, cs], axis=0, keepdims=True)
                ds = p * (dp - dl)
                db_acc[hd] = ds if db_acc[hd] is None else db_acc[hd] + ds
                dsk_tile = jnp.where(head_row == hd, dsk_tile - jnp.exp(sk_ref[hd] - lse_h) * dl, dsk_tile)
                dss = (ds * SWA_SCALE).astype(BF16)
                dq_rows[b].append(_dot(k_band, dss, NN).astype(BF16))
                dk_h = _dot(q_ref[rs, cs], dss, NT)
                dv_h = _dot(dob, p.astype(BF16), NT)
                dk_b = dk_h if dk_b is None else dk_b + dk_h
                dv_b = dv_h if dv_b is None else dv_b + dv_h
            for which, val in ((0, dk_b), (1, dv_b)):
                prev_part[which][kh][b] = val[:, 0:WIN]
                cur_part[which][kh][b] = val[:, WIN:2 * WIN]
        dq_cols = [jnp.concatenate(rows, axis=0) for rows in dq_rows]

        live = n < nst - 1
        ls = slice((qb - 1) * WIN, qb * WIN)
        halo = [[None] * KVH for _ in range(2)]
        for kh in range(KVH):
            k_last = kc_ref[kh * HD:(kh + 1) * HD, ls]
            v_last = vc_ref[kh * HD:(kh + 1) * HD, ls]
            dk_b = dv_b = None
            for g in range(G):
                hd = kh * G + g
                rs = slice(hd * HD, (hd + 1) * HD)
                q_t = qn_ref[rs, :]
                do = don_ref[rs, :]
                s = _dot(k_last, q_t, TN) * SWA_SCALE + b_ref[hd, 0:WIN, :]
                p = jnp.exp(s - lsen_ref[hd:hd + 1, :])
                dob = do.astype(BF16)
                dp = _dot(v_last, dob, TN)
                dl = jnp.sum(do * on_ref[rs, :], axis=0, keepdims=True)
                dss = (p * (dp - dl) * SWA_SCALE).astype(BF16)
                dk_h = _dot(q_t, dss, NT)
                dv_h = _dot(dob, p.astype(BF16), NT)
                dk_b = dk_h if dk_b is None else dk_b + dk_h
                dv_b = dv_h if dv_b is None else dv_b + dv_h
            halo[0][kh] = jnp.where(live, dk_b, 0.0)
            halo[1][kh] = jnp.where(live, dv_b, 0.0)

        kv_rows = []
        for which in range(2):
            for kh in range(KVH):
                blocks = [cur_part[which][kh][p] + (prev_part[which][kh][p + 1] if p + 1 < qb else halo[which][kh])
                          for p in range(qb)]
                kv_rows.append(jnp.concatenate(blocks, axis=1))
        dqkv_ref[...] = jnp.concatenate(
            [jnp.concatenate(dq_cols, axis=1), jnp.concatenate(kv_rows, axis=0).astype(BF16)], axis=0)
        db_ref[...] += jnp.stack(db_acc)
        dsk_ref[...] += dsk_tile

    prev = lambda r: (lambda n: (r, jnp.maximum(n * qb - 1, 0)))
    nxt = lambda n: (0, jnp.minimum((n + 1) * qb, nblk - 1))
    big = lambda: pl.BlockSpec((QH * HD, w), lambda n: (0, n))
    return pl.pallas_call(
        kern, grid=(nst,),
        in_specs=[big(),
                  pl.BlockSpec((KVH * HD, w), lambda n: (4, n)), pl.BlockSpec((KVH * HD, WIN), prev(4)),
                  pl.BlockSpec((KVH * HD, w), lambda n: (5, n)), pl.BlockSpec((KVH * HD, WIN), prev(5)),
                  big(), big(), pl.BlockSpec((QH, w), lambda n: (0, n)),
                  pl.BlockSpec((QH * HD, WIN), nxt), pl.BlockSpec((QH * HD, WIN), nxt),
                  pl.BlockSpec((QH * HD, WIN), nxt), pl.BlockSpec((QH, WIN), nxt),
                  pl.BlockSpec((QH, 2 * WIN, WIN), lambda n: (0, 0, 0)),
                  pl.BlockSpec(memory_space=pltpu.SMEM)],
        out_specs=[pl.BlockSpec(((QH + 2 * KVH) * HD, w), lambda n: (0, n)),
                   pl.BlockSpec((QH, 2 * WIN, WIN), lambda n: (0, 0, 0)),
                   pl.BlockSpec((QH, WIN), lambda n: (0, 0))],
        out_shape=[jax.ShapeDtypeStruct(((QH + 2 * KVH) * HD, t), BF16),
                   jax.ShapeDtypeStruct((QH, 2 * WIN, WIN), F32), jax.ShapeDtypeStruct((QH, WIN), F32)],
        name="swa_bwd", compiler_params=_cp())(
            qkv_t, qkv_t, qkv_t, qkv_t, qkv_t, do_t, o_t, lse, qkv_t, do_t, o_t, lse, bias, sinks)


def _adamw_math(w, g, m, v):
    nm = B1 * m + (1.0 - B1) * g
    nv = B2 * v + (1.0 - B2) * (g * g)
    mhat = nm * (1.0 / (1.0 - B1 ** STEP))
    vhat = nv * (1.0 / (1.0 - B2 ** STEP))
    return -LR * (mhat / (jnp.sqrt(vhat) + ADAM_EPS) + WD * w), nm, nv


def _adamw_layers(w, m, v, g0buf, g1buf, off, name, tm=512):
    rows = w.shape[1]
    nb, ob = rows // tm, off // tm

    def kern(w_ref, m_ref, v_ref, g0_ref, g1_ref, gr_ref, d_ref, nm_ref, nv_ref):
        g_ = jnp.where(pl.program_id(0) == 0, g0_ref[...], g1_ref[...])
        gr_ref[...] = g_
        d_ref[...], nm_ref[...], nv_ref[...] = _adamw_math(w_ref[...], g_, m_ref[...], v_ref[...])

    lay = pl.BlockSpec((None, tm, D), lambda l, i: (l, i, 0))
    gsp = pl.BlockSpec((tm, D), lambda l, i: (ob + i, 0))
    return pl.pallas_call(
        kern, grid=(2, nb), in_specs=[lay, lay, lay, gsp, gsp], out_specs=[lay] * 4,
        out_shape=[jax.ShapeDtypeStruct(w.shape, F32)] * 4, name=name, compiler_params=_cp())(w, m, v, g0buf, g1buf)


def _adamw(w, g, m, v, name, tm=544):
    r = w.shape[0]
    tm = r if r % tm else tm

    def kern(w_ref, g_ref, m_ref, v_ref, d_ref, nm_ref, nv_ref):
        d_ref[...], nm_ref[...], nv_ref[...] = _adamw_math(w_ref[...], g_ref[...], m_ref[...], v_ref[...])

    row = pl.BlockSpec((tm, D), lambda i: (i, 0))
    sds = jax.ShapeDtypeStruct((r, D), F32)
    return pl.pallas_call(kern, grid=(r // tm,), in_specs=[row] * 4, out_specs=[row] * 3, out_shape=[sds] * 3,
                          name=name, compiler_params=_cp())(w, g, m, v)


def _mesh_pos():
    return lax.axis_index("x"), lax.axis_index("y"), lax.axis_index("c")


ANY = pl.BlockSpec(memory_space=pl.ANY)


AG_SEMS = [pltpu.SemaphoreType.DMA((6,)), pltpu.SemaphoreType.DMA((6,))]


def _allgather_schedule(w_ref, out_ref, send_sems, recv_sems):
    half = w_ref.shape[0] // 2
    x, y, c = _mesh_pos()
    me, sibling = (x, y, c), (x, y, 1 - c)
    chips = [(1 - x, y), (x, 1 - y), (1 - x, 1 - y)]

    def rows(px, py, pc):
        return out_ref.at[2 * px + py, pl.ds(pc * half, half), :]

    def copy(k, block, to, src=None):
        return pltpu.make_async_remote_copy(
            src_ref=rows(*block) if src is None else src, dst_ref=rows(*block),
            send_sem=send_sems.at[k], recv_sem=recv_sems.at[k], device_id=to, device_id_type=MESH)

    def first():
        return [copy(j, me, (*chip, c), src=w_ref.at[pl.ds(c * half, half), :]) for j, chip in enumerate(chips)]

    def passed():
        return [copy(3 + j, (*chip, c), sibling) for j, chip in enumerate(chips)]

    def start():
        for cp in first():
            cp.start()

    def forward():
        for j, chip in enumerate(chips):
            copy(j, (*chip, c), me).wait_recv()
            passed()[j].start()

    def finish():
        for j, chip in enumerate(chips):
            copy(3 + j, (*chip, 1 - c), me).wait_recv()
        for cp in first() + passed():
            cp.wait_send()

    return start, forward, finish


def _allgather_weights(wpack):
    def body(w_ref, out_ref, send_sems, recv_sems):
        start, forward, finish = _allgather_schedule(w_ref, out_ref, send_sems, recv_sems)
        start()
        forward()
        finish()

    return pl.pallas_call(
        body, out_shape=jax.ShapeDtypeStruct((4,) + wpack.shape, wpack.dtype), in_specs=[ANY], out_specs=ANY,
        scratch_shapes=AG_SEMS, name="allgather_weights")(wpack)


def _row_tile(rows):
    t = min(rows, 512)
    while rows % t or t % 16:
        t -= 16
    return t


ALL_SEMS = [pltpu.SemaphoreType.DMA((7,)), pltpu.SemaphoreType.DMA((7,))]


def _device_exchange_schedule(g_ref, out_ref, send_sems, recv_sems):
    half = g_ref.shape[1] // 2
    x, y, c = _mesh_pos()
    me = 4 * x + 2 * y + c
    peers = [(x ^ (k >> 2), y ^ ((k >> 1) & 1), c ^ (k & 1)) for k in range(1, 8)]

    def sends():
        return [pltpu.make_async_remote_copy(
            src_ref=g_ref.at[2 * px + py, pl.ds(pc * half, half), :], dst_ref=out_ref.at[me],
            send_sem=send_sems.at[j], recv_sem=recv_sems.at[j], device_id=(px, py, pc), device_id_type=MESH)
            for j, (px, py, pc) in enumerate(peers)]

    def start():
        for cp in sends():
            cp.start()

    def finish():
        for j, (px, py, pc) in enumerate(peers):
            pltpu.make_async_remote_copy(
                src_ref=out_ref.at[me], dst_ref=out_ref.at[4 * px + 2 * py + pc], send_sem=send_sems.at[j],
                recv_sem=recv_sems.at[j], device_id=(px, py, pc), device_id_type=MESH).wait_recv()
        for cp in sends():
            cp.wait_send()

    return start, finish


def _sum_devices(slots, g, pos, tag):
    half = slots.shape[1]
    tm = _row_tile(half)
    nb = half // tm

    def kern(pos_ref, own_ref, *refs):
        acc = own_ref[0].astype(F32)
        for s_ref in refs[:7]:
            acc = acc + s_ref[0].astype(F32)
        refs[7][...] = acc

    def slot(k):
        return pl.BlockSpec((1, tm, D), lambda i, pos: (jnp.bitwise_xor(pos[2], k), i, 0))

    gs = pltpu.PrefetchScalarGridSpec(
        num_scalar_prefetch=1, grid=(nb,),
        in_specs=[pl.BlockSpec((1, tm, D), lambda i, pos: (pos[0], pos[1] * nb + i, 0))] + [slot(k) for k in range(1, 8)],
        out_specs=pl.BlockSpec((tm, D), lambda i, pos: (pos[1] * nb + i, 0)))
    return pl.pallas_call(kern, grid_spec=gs, out_shape=jax.ShapeDtypeStruct((2 * half, D), F32),
                          name=f"rs_sum_devices_{tag}", compiler_params=_cp())(pos, g, *([slots] * 7))


def _exchange_devices(g, tag):
    def body(g_ref, out_ref, send_sems, recv_sems):
        start, finish = _device_exchange_schedule(g_ref, out_ref, send_sems, recv_sems)
        start()
        finish()

    return pl.pallas_call(
        body, out_shape=jax.ShapeDtypeStruct((8, g.shape[1] // 2, D), g.dtype), in_specs=[ANY], out_specs=ANY,
        scratch_shapes=ALL_SEMS, name=f"rs_exchange_devices_{tag}")(g)


def _reduce_scatter_finish(slots, g, pos, tag):
    return _join_core_halves(_sum_devices(slots, g, pos, tag), tag)


def _join_core_halves(r, tag):
    half = r.shape[0] // 2

    def body(r_ref, out_ref, send_sem, recv_sem):
        x, y, c = _mesh_pos()
        mine = out_ref.at[pl.ds(c * half, half), :]
        cp = pltpu.make_async_remote_copy(
            src_ref=mine, dst_ref=mine, send_sem=send_sem, recv_sem=recv_sem,
            device_id=(x, y, 1 - c), device_id_type=MESH)
        cp.start()
        theirs = out_ref.at[pl.ds((1 - c) * half, half), :]
        pltpu.make_async_remote_copy(
            src_ref=theirs, dst_ref=theirs, send_sem=send_sem, recv_sem=recv_sem,
            device_id=(x, y, 1 - c), device_id_type=MESH).wait_recv()
        cp.wait_send()

    return pl.pallas_call(
        body, out_shape=jax.ShapeDtypeStruct(r.shape, r.dtype), in_specs=[ANY], out_specs=ANY,
        input_output_aliases={0: 0},
        scratch_shapes=[pltpu.SemaphoreType.DMA, pltpu.SemaphoreType.DMA],
        name=f"rs_join_cores_{tag}")(r)


def _allreduce_small(v, name):
    def body(v_ref, out_ref, gat, send_sems, recv_sems):
        x, y, c = _mesh_pos()
        me = 4 * x + 2 * y + c
        gat[me] = v_ref[...]
        sends = []
        for k in range(1, 8):
            peer = (x ^ (k >> 2), y ^ ((k >> 1) & 1), c ^ (k & 1))
            cp = pltpu.make_async_remote_copy(
                src_ref=v_ref, dst_ref=gat.at[me], send_sem=send_sems.at[k - 1], recv_sem=recv_sems.at[k - 1],
                device_id=peer, device_id_type=MESH)
            cp.start()
            sends.append(cp)
        for k in range(1, 8):
            px, py, pc = x ^ (k >> 2), y ^ ((k >> 1) & 1), c ^ (k & 1)
            pltpu.make_async_remote_copy(
                src_ref=v_ref, dst_ref=gat.at[4 * px + 2 * py + pc], send_sem=send_sems.at[k - 1],
                recv_sem=recv_sems.at[k - 1], device_id=(px, py, pc), device_id_type=MESH).wait_recv()
        for cp in sends:
            cp.wait_send()
        acc = gat[0]
        for d in range(1, 8):
            acc = acc + gat[d]
        out_ref[...] = acc

    return pl.pallas_call(
        body, out_shape=jax.ShapeDtypeStruct(v.shape, F32),
        in_specs=[pl.BlockSpec(memory_space=pltpu.VMEM)], out_specs=pl.BlockSpec(memory_space=pltpu.VMEM),
        scratch_shapes=[pltpu.VMEM((8,) + v.shape, F32), pltpu.SemaphoreType.DMA((7,)), pltpu.SemaphoreType.DMA((7,))],
        name=name)(v)


def _mlp_fwd(xb, w_up, w_down, tag):
    a = _mm(xb, w_up[0], "nn", f"mlp_up_{tag}", out_dtype=BF16, relu2=True, b_view=("cols", w_up[1]))
    return a, _mm(a, w_down[0], "nn", f"mlp_down_{tag}", b_view=("rows", w_down[1]), tm=2048)


def _mlp_bwd(dz, dzb, xb, a, w_up, w_down, tag):
    du = _mm(dzb, w_down[0], "nt", f"mlp_down_dx_{tag}", out_dtype=BF16, gate_a=a, b_view=("rows", w_down[1]),
             tm=2048)
    gsh = _mm(xb, du, "tn", f"mlp_up_dw_{tag}", out_dtype=BF16, out_view=("cols", 2 * ROWS["mlp_w_up"], 0, None))
    gsh = _mm(a, dzb, "tn", f"mlp_down_dw_{tag}", out_dtype=BF16,
              out_view=("rows", 2 * ROWS["mlp_w_up"], ROWS["mlp_w_up"], gsh))
    dx = _mm(du, w_up[0], "nt", f"mlp_up_dx_{tag}", addend=dz, add_scale=ALPHA, b_view=("cols", w_up[1]))
    return dx, gsh


def _fwd_bwd(x, target, w, dist=None, bq=512, qb=8, hb=8):
    t = x.shape[0]
    bq = min(bq, t)
    qb = min(qb, t // WIN)
    cos, sin = _rope_tables(t)
    bkt = jnp.asarray(_bucket_table())
    w_in = jnp.pad(w[("mla_w_in", None)], ((0, 0), (0, HW - (QR + KVR + ROPE))))
    wuq = w[("mla_w_uq", None)]
    wq2 = jnp.concatenate([wuq[:, :, :NOPE].reshape(QR, H * NOPE),
                           jnp.pad(wuq[:, :, NOPE:], ((0, 0), (0, 0), (0, RP - ROPE))).reshape(QR, H * RP)], axis=1)
    wuk_t = w[("mla_w_uk", None)].transpose(1, 2, 0)
    wuk_h = w[("mla_w_uk", None)].transpose(1, 0, 2)
    wuv_h = w[("mla_w_uv", None)].transpose(1, 0, 2)
    w_o = w[("mla_w_o", None)]
    sinks = w["swa_sinks"].reshape(QH)
    lnp = lambda n, l: w[n][l]
    reduced = {}

    hh = _mm(x, w_in, "nn", "mla_in")
    cq, kc = _mla_pre(hh, w["mla_g_q"], w["mla_g_kv"], cos, sin)
    q2 = _mm(cq, wq2, "nn", "mla_uq")
    qcat = _q_prep(q2, wuk_t, cos, sin)
    if dist is None:
        o_lat, lse0_t, o0 = _flash_fwd(qcat, kc, wuv_h, bq, hb)
    else:
        o_lat, lse0_t, o0, wall = _flash_fwd(qcat, kc, wuv_h, bq, hb, gather=dist.late_pack)
        wall = lax.dynamic_update_slice(wall, dist.late_pack[None], (dist.shard, 0, 0))
        w = {**w, **_full_from_gathered(AG_LATE, wall, dist.shard_shapes)}
    wqkv = jnp.concatenate([w[("swa_w_q", None)], w[("kv_w_shared", None)]], axis=1)
    wqkv_t = wqkv.T
    wo_s = w[("swa_w_o", None)]
    y0 = _mm(o0, w_o, "nn", "mla_out")
    x1b, xh1, r1 = _add_ln(x, y0, lnp("ln_mix_g", 0), lnp("ln_mix_b", 0), "ln_mix_0")
    a0, f0 = _mlp_fwd(x1b, w[("mlp_w_up", 0)], w[("mlp_w_down", 0)], 0)
    x2b, xh2, r2 = _add_ln(xh1, f0, lnp("ln_mlp_g", 0), lnp("ln_mlp_b", 0), "ln_mlp_0",
                           res_affine=(lnp("ln_mix_g", 0), lnp("ln_mix_b", 0)))
    bias = _bias_build(w["rel_bias"], bkt)
    qkv_t = _mm(x2b, wqkv, "nn", "swa_qkv", out_dtype=BF16, out_t=True)
    os_t, lse1 = _swa_fwd(qkv_t, bias, sinks, qb)
    y1 = _mm(os_t, wo_s, "tn", "swa_out")
    x3b, xh3, r3 = _add_ln(xh2, y1, lnp("ln_mix_g", 1), lnp("ln_mix_b", 1), "ln_mix_1",
                           res_affine=(lnp("ln_mlp_g", 0), lnp("ln_mlp_b", 0)))
    a1, f1 = _mlp_fwd(x3b, w[("mlp_w_up", 1)], w[("mlp_w_down", 1)], 1)
    _, xh4, r4 = _add_ln(xh3, f1, lnp("ln_mlp_g", 1), lnp("ln_mlp_b", 1), "ln_mlp_1",
                         res_affine=(lnp("ln_mix_g", 1), lnp("ln_mix_b", 1)))

    g = {}
    dz4, dz4b, dg_mlp1, db_mlp1, lpart = _ln_bwd(target, xh4, r4, lnp("ln_mlp_g", 1), "ln_mlp_1_bwd",
                                                 loss_b=lnp("ln_mlp_b", 1))
    dx3, g["mlp1"] = _mlp_bwd(dz4, dz4b, x3b, a1, w[("mlp_w_up", 1)], w[("mlp_w_down", 1)], 1)
    dz3, dz3b, dg_mix1, db_mix1 = _ln_bwd(dx3, xh3, r3, lnp("ln_mix_g", 1), "ln_mix_1_bwd")
    dos_t = _mm(dz3b, wo_s, "nt", "swa_out_dx", out_t=True)
    g[("swa_w_o", None)] = _mm(os_t, dz3b, "nn", "swa_out_dw")
    dqkv_t, dbias, dsk = _swa_bwd(qkv_t, dos_t, os_t, lse1, bias, sinks, qb)
    dwqkv = _mm(dqkv_t, x2b, "nn", "swa_qkv_dw").T
    g[("swa_w_q", None)], g[("kv_w_shared", None)] = dwqkv[:, :QH * HD], dwqkv[:, QH * HD:]
    dx2 = _mm(dqkv_t, wqkv_t, "tn", "swa_qkv_dx", addend=dz3, add_scale=ALPHA)
    g["rel_bias"] = jnp.sum(_bias_bwd(dbias, bkt), axis=-1).reshape(NBKT, QH)
    g["swa_sinks"] = jnp.sum(dsk, axis=-1).reshape(1, QH)
    dz2, dz2b, dg_mlp0, db_mlp0 = _ln_bwd(dx2, xh2, r2, lnp("ln_mlp_g", 0), "ln_mlp_0_bwd")
    dx1, g["mlp0"] = _mlp_bwd(dz2, dz2b, x1b, a0, w[("mlp_w_up", 0)], w[("mlp_w_down", 0)], 0)
    dz1, dz1b, dg_mix0, db_mix0 = _ln_bwd(dx1, xh1, r1, lnp("ln_mix_g", 0), "ln_mix_0_bwd")
    do0 = _mm(dz1b, w_o, "nt", "mla_out_dx", out_dtype=BF16)
    g[("mla_w_o", None)] = _mm(o0, dz1b, "tn", "mla_out_dw")
    do_lat, dwuv, delta_t = _o_up_bwd(do0, o_lat, wuv_h)
    g[("mla_w_uv", None)] = dwuv.transpose(1, 0, 2)
    kc_t = kc.reshape(t // bq, bq, KD).transpose(0, 2, 1)
    if dist is None:
        dk, ds_all = _flash_dkv(qcat, kc, do_lat, lse0_t, delta_t, bq, hb)
        dq_cat = _flash_dq(ds_all, kc_t, bq)
    else:
        g["mid"] = _grad_shards(RS_MID, g).astype(BF16)
        dk, ds_all, slots1, slots_mid = _flash_dkv(qcat, kc, do_lat, lse0_t, delta_t, bq, hb,
                                                  exchange=(g["mlp1"], g["mid"]))
        dq_cat, slots0 = _flash_dq(ds_all, kc_t, bq, exchange=g["mlp0"])
        for key, slots in (("mlp1", slots1), ("mid", slots_mid), ("mlp0", slots0)):
            reduced[key] = _reduce_scatter_finish(slots, g[key], dist.pos, key)
    dq2, dwuk = _q_prep_bwd(dq_cat, q2, wuk_h, cos, sin)
    g[("mla_w_uk", None)] = dwuk.transpose(2, 0, 1)
    dcq = _mm(dq2, wq2, "nt", "mla_uq_dx")
    dwq2 = _mm(cq, dq2, "tn", "mla_uq_dw")
    g[("mla_w_uq", None)] = jnp.concatenate([dwq2[:, :H * NOPE].reshape(QR, H, NOPE),
                                             dwq2[:, H * NOPE:].reshape(QR, H, RP)[:, :, :ROPE]], axis=2)
    dh, dgq, dgkv = _mla_pre_bwd(hh, dcq, dk, w["mla_g_q"], w["mla_g_kv"], cos, sin)
    g[("mla_w_in", None)] = _mm(x, dh, "tn", "mla_in_dw")[:, :QR + KVR + ROPE]
    grad_x = _mm(dh, w_in, "nt", "mla_in_dx", addend=dz1, add_scale=ALPHA)
    g["mla_g_q"], g["mla_g_kv"] = dgq, dgkv
    g["ln_mix_g"] = jnp.concatenate([dg_mix0, dg_mix1], axis=0)
    g["ln_mix_b"] = jnp.concatenate([db_mix0, db_mix1], axis=0)
    g["ln_mlp_g"] = jnp.concatenate([dg_mlp0, dg_mlp1], axis=0)
    g["ln_mlp_b"] = jnp.concatenate([db_mlp0, db_mlp1], axis=0)
    return lpart, grad_x, g, reduced


def _rows(a):
    return a.reshape(-1, D)


def _piece(a, layer):
    return _rows(a if layer is None else a[layer])


def _pack_group(group, parts):
    return jnp.concatenate([_piece(parts[n], l) for n, l in group], axis=0)


def _unpack_group(group, buf, like):
    out, off = {}, 0
    for n, l in group:
        shp = like[n].shape if l is None else like[n].shape[1:]
        out[(n, l)] = buf[off:off + ROWS[n]].reshape(shp)
        off += ROWS[n]
    return out


def _by_name(pieces):
    out = {n: a for (n, l), a in pieces.items() if l is None}
    for n in {n for (n, l) in pieces if l is not None}:
        out[n] = jnp.stack([pieces[(n, 0)], pieces[(n, 1)]])
    return out


def _full_from_gathered(group, wall, shard_shapes):
    out, off = {}, 0
    for n, l in group:
        shp = tuple(shard_shapes[n])
        if n in ("mlp_w_up", "mlp_w_down"):
            out[(n, l)] = (wall, off)
        elif n == "kv_w_shared":
            out[(n, l)] = wall[:, off:off + ROWS[n]].reshape((4 * shp[0],) + shp[1:])
        else:
            out[(n, l)] = wall[:, off:off + ROWS[n]].reshape((4 * shp[1],) + shp[2:])
        off += ROWS[n]
    return out


def _grad_shards(group, g):
    return jnp.concatenate([g[(n, l)].reshape(4, ROWS[n], D) for n, l in group], axis=1)


SMALL = (("ln_mix_g", 0, 2), ("ln_mix_b", 2, 2), ("ln_mlp_g", 4, 2), ("ln_mlp_b", 6, 2),
         ("swa_sinks", 8, 1), ("mla_g_q", 9, 1), ("mla_g_kv", 10, 1), ("rel_bias", 11, 1))
LOSS_ROW = 12


def _pack_small(parts, extra_row=None):
    rows = []
    for n, _, nr in SMALL:
        a = parts[n].reshape(nr, -1).astype(F32)
        rows.append(jnp.pad(a, ((0, 0), (0, D - a.shape[1]))))
    if extra_row is not None:
        rows.append(extra_row)
    rows.append(jnp.zeros((SMALL_ROWS - sum(r.shape[0] for r in rows), D), F32))
    return jnp.concatenate(rows, axis=0)


def _unpack_small(buf, like):
    out = {}
    for n, r0, nr in SMALL:
        size = like[n].size // nr
        out[n] = buf[r0:r0 + nr, :size].reshape(like[n].shape)
    return out


def kernel(x, mla_w_in, mla_g_q, mla_g_kv, mla_w_uq, mla_w_uk, mla_w_uv, mla_w_o, kv_w_shared, swa_w_q, swa_sinks, swa_w_o, rel_bias, mlp_w_up, mlp_w_down, ln_mix_g, ln_mix_b, ln_mlp_g, ln_mlp_b, loss_target, m_mla_w_in, m_mla_g_q, m_mla_g_kv, m_mla_w_uq, m_mla_w_uk, m_mla_w_uv, m_mla_w_o, m_kv_w_shared, m_swa_w_q, m_swa_sinks, m_swa_w_o, m_rel_bias, m_mlp_w_up, m_mlp_w_down, m_ln_mix_g, m_ln_mix_b, m_ln_mlp_g, m_ln_mlp_b, v_mla_w_in, v_mla_g_q, v_mla_g_kv, v_mla_w_uq, v_mla_w_uk, v_mla_w_uv, v_mla_w_o, v_kv_w_shared, v_swa_w_q, v_swa_sinks, v_swa_w_o, v_rel_bias, v_mlp_w_up, v_mlp_w_down, v_ln_mix_g, v_ln_mix_b, v_ln_mlp_g, v_ln_mlp_b):
    names = ["mla_w_in", "mla_g_q", "mla_g_kv", "mla_w_uq", "mla_w_uk", "mla_w_uv", "mla_w_o", "kv_w_shared",
             "swa_w_q", "swa_sinks", "swa_w_o", "rel_bias", "mlp_w_up", "mlp_w_down",
             "ln_mix_g", "ln_mix_b", "ln_mlp_g", "ln_mlp_b"]
    ws = dict(zip(names, [mla_w_in, mla_g_q, mla_g_kv, mla_w_uq, mla_w_uk, mla_w_uv, mla_w_o, kv_w_shared,
                          swa_w_q, swa_sinks, swa_w_o, rel_bias, mlp_w_up, mlp_w_down,
                          ln_mix_g, ln_mix_b, ln_mlp_g, ln_mlp_b]))
    ms = dict(zip(names, [m_mla_w_in, m_mla_g_q, m_mla_g_kv, m_mla_w_uq, m_mla_w_uk, m_mla_w_uv, m_mla_w_o,
                          m_kv_w_shared, m_swa_w_q, m_swa_sinks, m_swa_w_o, m_rel_bias, m_mlp_w_up, m_mlp_w_down,
                          m_ln_mix_g, m_ln_mix_b, m_ln_mlp_g, m_ln_mlp_b]))
    vs = dict(zip(names, [v_mla_w_in, v_mla_g_q, v_mla_g_kv, v_mla_w_uq, v_mla_w_uk, v_mla_w_uv, v_mla_w_o,
                          v_kv_w_shared, v_swa_w_q, v_swa_sinks, v_swa_w_o, v_rel_bias, v_mlp_w_up, v_mlp_w_down,
                          v_ln_mix_g, v_ln_mix_b, v_ln_mlp_g, v_ln_mlp_b]))
    xi, yi, ci = _mesh_pos()
    shard = 2 * xi + yi
    shard_shapes = {n: ws[n].shape for n in ROWS}
    wbf = {n: ws[n].astype(BF16) for n in ROWS}

    early = _pack_group(AG_EARLY, wbf)
    wall = lax.dynamic_update_slice(_allgather_weights(early), early[None], (shard, 0, 0))
    w = _full_from_gathered(AG_EARLY, wall, shard_shapes)
    dist = _Dist(shard=shard, pos=jnp.stack([shard, ci, 2 * shard + ci]).astype(jnp.int32),
                 late_pack=_pack_group(AG_LATE, wbf), shard_shapes=shard_shapes)
    gq_slot = lax.dynamic_update_slice(jnp.zeros((1, QR), F32), mla_g_q, (0, shard * (QR // 4)))
    gkv_slot = lax.dynamic_update_slice(jnp.zeros((1, KVR), F32), mla_g_kv, (0, shard * (KVR // 4)))
    gains = jnp.concatenate([jnp.pad(gq_slot, ((0, 0), (0, D - QR))), jnp.pad(gkv_slot, ((0, 0), (0, D - KVR))),
                             jnp.zeros((SMALL_ROWS - 2, D), F32)], axis=0)
    gains = _allreduce_small(gains * 0.5, "allgather_gains")
    w["mla_g_q"], w["mla_g_kv"] = gains[0, :QR], gains[1, :KVR]
    for n in ("swa_sinks", "rel_bias", "ln_mix_g", "ln_mix_b", "ln_mlp_g", "ln_mlp_b"):
        w[n] = ws[n]

    lpart, grad_x, g, reduced = _fwd_bwd(x[0], loss_target[0], w, dist)

    g["end"] = _grad_shards(RS_END, g).astype(BF16)
    reduced["end"] = _reduce_scatter_finish(_exchange_devices(g["end"], "end"), g["end"], dist.pos, "end")
    reduced["rest"] = jnp.concatenate([reduced["mid"], reduced["end"]], axis=0)

    small_like = {n: g[n] for n, _, _ in SMALL}
    small_sum = _allreduce_small(_pack_small(g, extra_row=lpart), "allreduce_small_grads")
    loss = 0.5 * jnp.sum(small_sum[LOSS_ROW]) / D
    gsm = _unpack_small(small_sum, small_like)
    gsm["mla_g_q"] = lax.dynamic_slice(gsm["mla_g_q"], (0, shard * (QR // 4)), (1, QR // 4))
    gsm["mla_g_kv"] = lax.dynamic_slice(gsm["mla_g_kv"], (0, shard * (KVR // 4)), (1, KVR // 4))

    gbig, dbig, mbig, vbig = {}, {}, {}, {}
    for n in ("mlp_w_up", "mlp_w_down"):
        off = 0 if n == "mlp_w_up" else ROWS["mlp_w_up"]
        gbig[n], dbig[n], mbig[n], vbig[n] = _adamw_layers(
            ws[n], ms[n], vs[n], reduced["mlp0"], reduced["mlp1"], off, f"adamw_{n}")
    rest = RS_MID + RS_END
    outs = _adamw(_pack_group(rest, ws), reduced["rest"], _pack_group(rest, ms), _pack_group(rest, vs),
                  "adamw_rest", tm=_row_tile(reduced["rest"].shape[0]))
    for dst, buf in zip((gbig, dbig, mbig, vbig), (reduced["rest"], *outs)):
        dst.update(_by_name(_unpack_group(rest, buf, ws)))
    dsm, msm, vsm = _adamw(_pack_small(ws), _pack_small(gsm), _pack_small(ms), _pack_small(vs), "adamw_small", tm=16)
    grads = {**gbig, **gsm}
    delta = {**dbig, **_unpack_small(dsm, ws)}
    new_m = {**mbig, **_unpack_small(msm, ws)}
    new_v = {**vbig, **_unpack_small(vsm, ws)}
    grads = {n: grads[n].reshape(ws[n].shape) for n in names}
    return (loss, grad_x[None], *[grads[n] for n in names], *[delta[n] for n in names],
            *[new_m[n] for n in names], *[new_v[n] for n in names])
```

```python
import collections
import math

import numpy as np
import jax
import jax.numpy as jnp
from jax import lax
from jax.experimental import pallas as pl
from jax.experimental.pallas import tpu as pltpu

F32 = jnp.float32
BF16 = jnp.bfloat16
MESH = pl.DeviceIdType.MESH

D = 1024
DFF = 4096
H = 8
NOPE = 128
ROPE = 64
QR = 384
KVR = 256
RP = 128
KD = KVR + RP
HW = 768
QH = 16
KVH = 4
HD = 64
G = QH // KVH
WIN = 128
NBKT = 32
ALPHA = 4.0 ** 0.25
LN_EPS = 1e-5
RMS_EPS = 1e-6
MLA_SCALE = (NOPE + ROPE) ** -0.5
LOG2E = 1.4426950408889634
LN2 = 0.6931471805599453
QSCALE = MLA_SCALE * LOG2E
AHEAD = 1
SWA_SCALE = HD ** -0.5
NEG = -1e30
LR, B1, B2, ADAM_EPS, WD, STEP = 0.001, 0.9, 0.999, 1e-8, 0.01, 10

VMEM_LIMIT = 48 * 1024 * 1024

NN = (((1,), (0,)), ((), ()))
NT = (((1,), (1,)), ((), ()))
TN = (((0,), (0,)), ((), ()))

ROWS = {"mlp_w_up": 1024, "mlp_w_down": 1024, "mla_w_o": 256, "swa_w_q": 256, "swa_w_o": 256,
        "kv_w_shared": 128, "mla_w_in": 176, "mla_w_uq": 144, "mla_w_uk": 64, "mla_w_uv": 64}
AG_EARLY = (("mla_w_in", None), ("mla_w_uq", None), ("mla_w_uk", None), ("mla_w_uv", None), ("mla_w_o", None))
AG_LATE = (("mlp_w_up", 0), ("mlp_w_up", 1), ("mlp_w_down", 0), ("mlp_w_down", 1),
           ("swa_w_q", None), ("swa_w_o", None), ("kv_w_shared", None))
RS_MID = (("mla_w_o", None), ("swa_w_q", None), ("swa_w_o", None), ("kv_w_shared", None), ("mla_w_uv", None))
RS_END = (("mla_w_in", None), ("mla_w_uq", None), ("mla_w_uk", None))
SMALL_ROWS = 16
GAIN_ROWS = 32
_Dist =collections.namedtuple("_Dist", "shard pos late_pack shard_shapes")


def _cp(**kw):
    return pltpu.CompilerParams(vmem_limit_bytes=VMEM_LIMIT, **kw)


def _tile(n, pref):
    t = min(n, pref)
    while n % t:
        t -= 128
    return t


def _dot(a, b, dims):
    return lax.dot_general(a, b, dims, preferred_element_type=F32)


def _mm(a, b, mode, name, out_dtype=F32, out_t=False, addend=None, add_scale=1.0, relu2=False, gate_a=None,
        b_view=None, out_view=None, exchange=None, tm=1024, tn=1024, tk=1024):
    blk = 1024
    if b_view is not None:
        kind, b_off = b_view
        assert b.shape[0] == 4 and b.shape[2] == blk and b_off % blk == 0
        bshape = {("cols", "nn"): (blk, 4 * blk), ("cols", "nt"): (blk, 4 * blk),
                  ("rows", "nn"): (4 * blk, blk), ("rows", "nt"): (4 * blk, blk)}[(kind, mode)]
    else:
        bshape = b.shape
    if mode == "nn":
        (m, k), (k2, n) = a.shape, bshape
    elif mode == "nt":
        (m, k), (n, k2) = a.shape, bshape
    else:
        (k, m), (k2, n) = a.shape, bshape
    assert k == k2, (name, a.shape, b.shape)
    tm, tn, tk = _tile(m, tm), _tile(n, tn), _tile(k, tk)
    nk = k // tk
    dims = {"nn": NN, "nt": NT, "tn": TN}[mode]
    if mode == "tn":
        a_spec = pl.BlockSpec((tk, tm), lambda i, j, kk: (kk, i))
    else:
        a_spec = pl.BlockSpec((tm, tk), lambda i, j, kk: (i, kk))
    if b_view is not None:
        assert tn == blk and tk == blk
        ob = b_off // blk
        b_spec = {("cols", "nn"): pl.BlockSpec((None, tk, tn), lambda i, j, kk: (j, ob, 0)),
                  ("cols", "nt"): pl.BlockSpec((None, tn, tk), lambda i, j, kk: (kk, ob, 0)),
                  ("rows", "nn"): pl.BlockSpec((None, tk, tn), lambda i, j, kk: (kk, ob, 0)),
                  ("rows", "nt"): pl.BlockSpec((None, tn, tk), lambda i, j, kk: (j, ob, 0))}[(kind, mode)]
    elif mode == "nt":
        b_spec = pl.BlockSpec((tn, tk), lambda i, j, kk: (j, kk))
    else:
        b_spec = pl.BlockSpec((tk, tn), lambda i, j, kk: (kk, j))
    mn_spec = pl.BlockSpec((tm, tn), lambda i, j, kk: (i, j))
    ins, in_specs = [a, b], [a_spec, b_spec]
    if addend is not None:
        ins.append(addend)
        in_specs.append(mn_spec)
    if gate_a is not None:
        ins.append(gate_a)
        in_specs.append(mn_spec)
    aliases = {}
    if out_view is not None:
        okind, total_rows, o_off, buf = out_view
        assert not out_t and tm == blk and tn == blk and o_off % blk == 0
        oo = o_off // blk
        out_shape = [jax.ShapeDtypeStruct((4, total_rows, blk), out_dtype)]
        if okind == "cols":
            out_specs = [pl.BlockSpec((None, tm, tn), lambda i, j, kk: (j, oo, 0))]
        else:
            out_specs = [pl.BlockSpec((None, tm, tn), lambda i, j, kk: (i, oo, 0))]
        if buf is not None:
            aliases = {len(ins): 0}
            ins.append(buf)
            in_specs.append(pl.BlockSpec(memory_space=pl.ANY))
    elif out_t:
        out_shape = [jax.ShapeDtypeStruct((n, m), out_dtype)]
        out_specs = [pl.BlockSpec((tn, tm), lambda i, j, kk: (j, i))]
    else:
        out_shape = [jax.ShapeDtypeStruct((m, n), out_dtype)]
        out_specs = [mn_spec]
    has_add, has_gate = addend is not None, gate_a is not None
    hosting = exchange is not None
    scratch = [pltpu.VMEM((tm, tn), F32)] if nk > 1 else []
    if hosting:
        assert not aliases
        ins.append(exchange)
        in_specs.append(pl.BlockSpec(memory_space=pl.ANY))
        out_shape.append(jax.ShapeDtypeStruct((8, exchange.shape[1] // 2, D), exchange.dtype))
        out_specs.append(pl.BlockSpec(memory_space=pl.ANY))
        scratch = scratch + ALL_SEMS
    steps = (m // tm, n // tn, nk)

    def kern(*refs):
        a_ref, b_ref = refs[0], refs[1]
        pos = 2
        add_ref = gate_ref = None
        if has_add:
            add_ref = refs[pos]
            pos += 1
        if has_gate:
            gate_ref = refs[pos]
            pos += 1
        pos += len(aliases)
        if hosting:
            xc_start, xc_finish = _device_exchange_schedule(refs[pos], refs[pos + 2], refs[-2], refs[-1])
            pos += 1
        o_ref = refs[pos]
        acc = refs[pos + 1 + hosting] if nk > 1 else None
        kk = pl.program_id(2)
        if hosting:
            lin = (pl.program_id(0) * steps[1] + pl.program_id(1)) * steps[2] + kk

            @pl.when(lin == 0)
            def _():
                xc_start()

        def partial():
            return _dot(a_ref[...].astype(BF16), b_ref[...].astype(BF16), dims)

        if nk > 1:
            @pl.when(kk == 0)
            def _():
                acc[...] = partial()

            @pl.when((kk > 0) & (kk < nk - 1))
            def _():
                acc[...] += partial()

        @pl.when(kk == nk - 1)
        def _():
            r = partial() + acc[...] if nk > 1 else partial()
            if has_add:
                r = r + add_scale * add_ref[...].astype(F32)
            if has_gate:
                ga = gate_ref[...].astype(F32)
                r = r * jnp.where(ga > 0.0, (2.0 * ga) * lax.rsqrt(ga), 0.0)
            if relu2:
                hh = jnp.maximum(r, 0.0)
                r = hh * hh
            if out_t:
                r = r.T
            o_ref[...] = r.astype(out_dtype)

        if hosting:
            @pl.when(lin == steps[0] * steps[1] * steps[2] - 1)
            def _():
                xc_finish()

    outs = pl.pallas_call(
        kern, out_shape=out_shape, grid=steps, in_specs=in_specs, out_specs=out_specs,
        scratch_shapes=scratch, input_output_aliases=aliases, name=name, compiler_params=_cp())(*ins)
    return outs if hosting else outs[0]


def _add_ln(res, y, g, b, name, res_affine=None, tm=512):
    t = res.shape[0]
    tm = min(tm, t)
    affine = res_affine is not None

    def kern(*refs):
        if affine:
            x_ref, y_ref, g_ref, b_ref, g0_ref, b0_ref, ob_ref, xh_ref, r_ref = refs
            x = x_ref[...] * g0_ref[...] + b0_ref[...]
        else:
            x_ref, y_ref, g_ref, b_ref, ob_ref, xh_ref, r_ref = refs
            x = x_ref[...]
        z = ALPHA * x + y_ref[...]
        mu = jnp.mean(z, axis=-1, keepdims=True)
        zc = z - mu
        var = jnp.mean(zc * zc, axis=-1, keepdims=True)
        r = lax.rsqrt(var + LN_EPS)
        xh = zc * r
        ob_ref[...] = (xh * g_ref[...] + b_ref[...]).astype(BF16)
        xh_ref[...] = xh
        r_ref[...] = r

    row = pl.BlockSpec((tm, D), lambda i: (i, 0))
    vec = pl.BlockSpec((1, D), lambda i: (0, 0))
    st = pl.BlockSpec((tm, 1), lambda i: (i, 0))
    ins = [res, y, g.reshape(1, D), b.reshape(1, D)]
    if affine:
        ins += [res_affine[0].reshape(1, D), res_affine[1].reshape(1, D)]
    return pl.pallas_call(
        kern, grid=(t // tm,), in_specs=[row, row] + [vec] * (len(ins) - 2), out_specs=[row, row, st],
        out_shape=[jax.ShapeDtypeStruct((t, D), BF16), jax.ShapeDtypeStruct((t, D), F32),
                   jax.ShapeDtypeStruct((t, 1), F32)],
        name=name, compiler_params=_cp())(*ins)


def _ln_bwd(dout, xhat, rstd, g, name, loss_b=None, tm=512):
    t = dout.shape[0]
    tm = min(tm, t)
    head = loss_b is not None

    def kern(*refs):
        if head:
            do_ref, xh_ref, r_ref, g_ref, b_ref, dz_ref, dzb_ref, dg_ref, db_ref, l_ref = refs
        else:
            do_ref, xh_ref, r_ref, g_ref, dz_ref, dzb_ref, dg_ref, db_ref = refs

        @pl.when(pl.program_id(0) == 0)
        def _():
            dg_ref[...] = jnp.zeros_like(dg_ref)
            db_ref[...] = jnp.zeros_like(db_ref)
            if head:
                l_ref[...] = jnp.zeros_like(l_ref)

        xh = xh_ref[...]
        if head:
            e = xh * g_ref[...] + b_ref[...] - do_ref[...]
            l_ref[...] += jnp.sum(e * e, axis=0, keepdims=True)
            do = e * (1.0 / D)
        else:
            do = do_ref[...]
        dxh = do * g_ref[...]
        m1 = jnp.mean(dxh, axis=-1, keepdims=True)
        m2 = jnp.mean(dxh * xh, axis=-1, keepdims=True)
        dz = r_ref[...] * (dxh - m1 - xh * m2)
        dz_ref[...] = dz
        dzb_ref[...] = dz.astype(BF16)
        dg_ref[...] += jnp.sum(do * xh, axis=0, keepdims=True)
        db_ref[...] += jnp.sum(do, axis=0, keepdims=True)

    row = pl.BlockSpec((tm, D), lambda i: (i, 0))
    vec = pl.BlockSpec((1, D), lambda i: (0, 0))
    st = pl.BlockSpec((tm, 1), lambda i: (i, 0))
    ins = [dout, xhat, rstd, g.reshape(1, D)] + ([loss_b.reshape(1, D)] if head else [])
    return pl.pallas_call(
        kern, grid=(t // tm,), in_specs=[row, row, st] + [vec] * (len(ins) - 3),
        out_specs=[row, row, vec, vec] + ([vec] if head else []),
        out_shape=[jax.ShapeDtypeStruct((t, D), F32), jax.ShapeDtypeStruct((t, D), BF16)]
        + [jax.ShapeDtypeStruct((1, D), F32)] * (3 if head else 2),
        name=name, compiler_params=_cp())(*ins)


def _rope_tables(t):
    half = ROPE // 2
    inv = 10000.0 ** (-jnp.arange(half, dtype=F32) / half)
    ang = jnp.arange(t).astype(F32)[:, None] * inv[None, :]
    cos, sin = jnp.cos(ang), jnp.sin(ang)
    z = jnp.zeros((t, RP - ROPE), F32)
    return jnp.concatenate([cos, cos, z], axis=1), jnp.concatenate([-sin, sin, z], axis=1)


def _swap_halves(x):
    lane = lax.broadcasted_iota(jnp.int32, x.shape, 1)
    return jnp.where(lane < ROPE // 2, pltpu.roll(x, RP - ROPE // 2, 1), pltpu.roll(x, ROPE // 2, 1))


def _rope(x, cos, sin):
    return x * cos + _swap_halves(x) * sin


def _rope_t(gy, cos, sin):
    return gy * cos + _swap_halves(gy * sin)


def _mla_pre(hh, g_q, g_kv, cos, sin, tm=512):
    t = hh.shape[0]
    tm = min(tm, t)

    def kern(h_ref, gq_ref, gkv_ref, c_ref, s_ref, cq_ref, k_ref):
        xq = h_ref[:, 0:QR]
        rq = lax.rsqrt(jnp.mean(xq * xq, axis=-1, keepdims=True) + RMS_EPS)
        cq_ref[...] = (xq * rq * gq_ref[...]).astype(BF16)
        xk = h_ref[:, QR:QR + KVR]
        rk = lax.rsqrt(jnp.mean(xk * xk, axis=-1, keepdims=True) + RMS_EPS)
        k_ref[:, 0:KVR] = (xk * rk * gkv_ref[...]).astype(BF16)
        k_ref[:, KVR:KD] = _rope(h_ref[:, QR + KVR:HW], c_ref[...], s_ref[...]).astype(BF16)

    return pl.pallas_call(
        kern, grid=(t // tm,),
        in_specs=[pl.BlockSpec((tm, HW), lambda i: (i, 0)), pl.BlockSpec((1, QR), lambda i: (0, 0)),
                  pl.BlockSpec((1, KVR), lambda i: (0, 0)), pl.BlockSpec((tm, RP), lambda i: (i, 0)),
                  pl.BlockSpec((tm, RP), lambda i: (i, 0))],
        out_specs=[pl.BlockSpec((tm, QR), lambda i: (i, 0)), pl.BlockSpec((tm, KD), lambda i: (i, 0))],
        out_shape=[jax.ShapeDtypeStruct((t, QR), BF16), jax.ShapeDtypeStruct((t, KD), BF16)],
        name="mla_pre", compiler_params=_cp())(hh, g_q.reshape(1, QR), g_kv.reshape(1, KVR), cos, sin)


def _mla_pre_bwd(hh, dcq, dk, g_q, g_kv, cos, sin, tm=512):
    t = hh.shape[0]
    tm = min(tm, t)

    def rms_bwd(x, dy, g):
        r = lax.rsqrt(jnp.mean(x * x, axis=-1, keepdims=True) + RMS_EPS)
        gdy = dy * g
        dx = r * gdy - x * (r * r * r) * jnp.mean(gdy * x, axis=-1, keepdims=True)
        return dx, jnp.sum(dy * x * r, axis=0, keepdims=True)

    def kern(h_ref, dcq_ref, dk_ref, gq_ref, gkv_ref, c_ref, s_ref, dh_ref, dgq_ref, dgkv_ref):
        @pl.when(pl.program_id(0) == 0)
        def _():
            dgq_ref[...] = jnp.zeros_like(dgq_ref)
            dgkv_ref[...] = jnp.zeros_like(dgkv_ref)

        dxq, dgq = rms_bwd(h_ref[:, 0:QR], dcq_ref[...], gq_ref[...])
        dxk, dgk = rms_bwd(h_ref[:, QR:QR + KVR], dk_ref[:, 0:KVR], gkv_ref[...])
        dh_ref[:, 0:QR] = dxq.astype(BF16)
        dh_ref[:, QR:QR + KVR] = dxk.astype(BF16)
        dh_ref[:, QR + KVR:HW] = _rope_t(dk_ref[:, KVR:KD], c_ref[...], s_ref[...]).astype(BF16)
        dgq_ref[...] += dgq
        dgkv_ref[...] += dgk

    return pl.pallas_call(
        kern, grid=(t // tm,),
        in_specs=[pl.BlockSpec((tm, HW), lambda i: (i, 0)), pl.BlockSpec((tm, QR), lambda i: (i, 0)),
                  pl.BlockSpec((tm, KD), lambda i: (i, 0)), pl.BlockSpec((1, QR), lambda i: (0, 0)),
                  pl.BlockSpec((1, KVR), lambda i: (0, 0)), pl.BlockSpec((tm, RP), lambda i: (i, 0)),
                  pl.BlockSpec((tm, RP), lambda i: (i, 0))],
        out_specs=[pl.BlockSpec((tm, HW), lambda i: (i, 0)), pl.BlockSpec((1, QR), lambda i: (0, 0)),
                   pl.BlockSpec((1, KVR), lambda i: (0, 0))],
        out_shape=[jax.ShapeDtypeStruct((t, HW), BF16), jax.ShapeDtypeStruct((1, QR), F32),
                   jax.ShapeDtypeStruct((1, KVR), F32)],
        name="mla_pre_bwd", compiler_params=_cp())(hh, dcq, dk, g_q.reshape(1, QR), g_kv.reshape(1, KVR), cos, sin)


def _q_prep(q2, wuk_t, cos, sin, tm=512):
    t = q2.shape[0]
    tm = min(tm, t)

    def kern(q_ref, w_ref, c_ref, s_ref, o_ref):
        cos_, sin_ = c_ref[...], s_ref[...]
        for h in range(H):
            qn = q_ref[:, h * NOPE:(h + 1) * NOPE].astype(BF16)
            o_ref[:, h * KD:h * KD + KVR] = (_dot(qn, w_ref[h], NN) * QSCALE).astype(BF16)
            qr = q_ref[:, H * NOPE + h * RP:H * NOPE + (h + 1) * RP]
            o_ref[:, h * KD + KVR:(h + 1) * KD] = (_rope(qr, cos_, sin_) * QSCALE).astype(BF16)

    return pl.pallas_call(
        kern, grid=(t // tm,),
        in_specs=[pl.BlockSpec((tm, 2 * H * NOPE), lambda i: (i, 0)), pl.BlockSpec((H, NOPE, KVR), lambda i: (0, 0, 0)),
                  pl.BlockSpec((tm, RP), lambda i: (i, 0)), pl.BlockSpec((tm, RP), lambda i: (i, 0))],
        out_specs=pl.BlockSpec((tm, H * KD), lambda i: (i, 0)),
        out_shape=jax.ShapeDtypeStruct((t, H * KD), BF16),
        name="q_prep", compiler_params=_cp())(q2, wuk_t, cos, sin)


def _q_prep_bwd(dq_cat, q2, wuk_h, cos, sin, tm=512):
    t = q2.shape[0]
    tm = min(tm, t)

    def kern(dq_ref, q_ref, w_ref, c_ref, s_ref, o_ref, dw_ref):
        @pl.when(pl.program_id(0) == 0)
        def _():
            dw_ref[...] = jnp.zeros_like(dw_ref)

        cos_, sin_ = c_ref[...], s_ref[...]
        for h in range(H):
            dql = dq_ref[:, h * KD:h * KD + KVR].astype(BF16)
            o_ref[:, h * NOPE:(h + 1) * NOPE] = _dot(dql, w_ref[h], NN).astype(BF16)
            dqr = dq_ref[:, h * KD + KVR:(h + 1) * KD]
            o_ref[:, H * NOPE + h * RP:H * NOPE + (h + 1) * RP] = _rope_t(dqr, cos_, sin_).astype(BF16)
            qn = q_ref[:, h * NOPE:(h + 1) * NOPE].astype(BF16)
            dw_ref[h] += _dot(qn, dql, TN)

    return pl.pallas_call(
        kern, grid=(t // tm,),
        in_specs=[pl.BlockSpec((tm, H * KD), lambda i: (i, 0)), pl.BlockSpec((tm, 2 * H * NOPE), lambda i: (i, 0)),
                  pl.BlockSpec((H, KVR, NOPE), lambda i: (0, 0, 0)),
                  pl.BlockSpec((tm, RP), lambda i: (i, 0)), pl.BlockSpec((tm, RP), lambda i: (i, 0))],
        out_specs=[pl.BlockSpec((tm, 2 * H * NOPE), lambda i: (i, 0)), pl.BlockSpec((H, NOPE, KVR), lambda i: (0, 0, 0))],
        out_shape=[jax.ShapeDtypeStruct((t, 2 * H * NOPE), BF16), jax.ShapeDtypeStruct((H, NOPE, KVR), F32)],
        name="q_prep_bwd", compiler_params=_cp())(dq_cat, q2, wuk_h, cos, sin)


def _o_up_bwd(do, o_lat, wuv_h, tm=512):
    t = do.shape[0]
    tm = min(tm, t)

    def kern(do_ref, x_ref, w_ref, dx_ref, dw_ref, dlt_ref):
        @pl.when(pl.program_id(0) == 0)
        def _():
            dw_ref[...] = jnp.zeros_like(dw_ref)

        for h in range(H):
            dh_ = do_ref[:, h * NOPE:(h + 1) * NOPE]
            x = x_ref[:, h * KVR:(h + 1) * KVR]
            dx = _dot(dh_, w_ref[h], NT)
            dx_ref[:, h * KVR:(h + 1) * KVR] = dx.astype(BF16)
            dw_ref[h] += _dot(x.astype(BF16), dh_, TN)
            dl = jnp.broadcast_to(jnp.sum(dx * x, axis=1)[:, None], (tm, 128))
            dlt_ref[h] = dl.T[0:1, :]

    return pl.pallas_call(
        kern, grid=(t // tm,),
        in_specs=[pl.BlockSpec((tm, H * NOPE), lambda i: (i, 0)), pl.BlockSpec((tm, H * KVR), lambda i: (i, 0)),
                  pl.BlockSpec((H, KVR, NOPE), lambda i: (0, 0, 0))],
        out_specs=[pl.BlockSpec((tm, H * KVR), lambda i: (i, 0)), pl.BlockSpec((H, KVR, NOPE), lambda i: (0, 0, 0)),
                   pl.BlockSpec((H, 1, tm), lambda i: (0, 0, i))],
        out_shape=[jax.ShapeDtypeStruct((t, H * KVR), BF16), jax.ShapeDtypeStruct((H, KVR, NOPE), F32),
                   jax.ShapeDtypeStruct((H, 1, t), F32)],
        name="o_up_bwd", compiler_params=_cp())(do, o_lat, wuv_h)


def _causal_pairs(nq):
    return [(i, j) for i in range(nq) for j in range(i + 1)]


def _lane_tile(stat, width):
    return jnp.tile(stat, (1, width // 128))


def _flash_fwd(qcat, kc, wuv_h, bq, hb, gather=None):
    t = kc.shape[0]
    nq = t // bq
    pairs = _causal_pairs(nq)
    itab = jnp.asarray(np.array([p[0] for p in pairs], np.int32))
    jtab = jnp.asarray(np.array([p[1] for p in pairs], np.int32))

    ng = H // hb
    hosting = gather is not None

    def kern(it, jt, q_ref, k_ref, wuv_ref, *rest):
        if hosting:
            w_ref, o_ref, lset_ref, oup_ref, wall_ref, m_sc, l_sc, acc_sc, send_sems, recv_sems = rest
            ag_start, ag_forward, ag_finish = _allgather_schedule(w_ref, wall_ref, send_sems, recv_sems)
        else:
            o_ref, lset_ref, oup_ref, m_sc, l_sc, acc_sc = rest
        grp = pl.program_id(0)
        st = pl.program_id(1)
        i, j = it[st], jt[st]

        if hosting:
            @pl.when((grp == 0) & (st == 0))
            def _():
                ag_start()

        @pl.when(j == 0)
        def _():
            m_sc[...] = jnp.full_like(m_sc, NEG)
            l_sc[...] = jnp.zeros_like(l_sc)
            acc_sc[...] = jnp.zeros_like(acc_sc)

        def update(masked):
            k = k_ref[...]
            v = k[:, 0:KVR]
            if masked:
                row = lax.broadcasted_iota(jnp.int32, (bq, bq), 0)
                col = lax.broadcasted_iota(jnp.int32, (bq, bq), 1)
                keep = col <= row
            pending = [_dot(q_ref[:, hh * KD:(hh + 1) * KD], k, NT) for hh in range(min(AHEAD, hb))]
            for hh in range(hb):
                s = pending.pop(0)
                if hh + AHEAD < hb:
                    pending.append(_dot(q_ref[:, (hh + AHEAD) * KD:(hh + AHEAD + 1) * KD], k, NT))
                if masked:
                    s = jnp.where(keep, s, NEG)
                m_prev = m_sc[hh]
                m_next = jnp.maximum(m_prev, jnp.max(s, axis=1)[:, None])
                p = jnp.exp2(s - _lane_tile(m_next, bq))
                a = jnp.exp2(m_prev - m_next)
                l_sc[hh] = a * l_sc[hh] + jnp.sum(p, axis=1)[:, None]
                acc_sc[hh] = _lane_tile(a, KVR) * acc_sc[hh] + _dot(p.astype(BF16), v, NN)
                m_sc[hh] = m_next

        @pl.when(j < i)
        def _():
            update(False)

        @pl.when(j == i)
        def _():
            update(True)
            for hh in range(hb):
                l = l_sc[hh]
                o_h = acc_sc[hh] / _lane_tile(l, KVR)
                o_ref[:, hh * KVR:(hh + 1) * KVR] = o_h
                oup_ref[:, hh * NOPE:(hh + 1) * NOPE] = _dot(o_h.astype(BF16), wuv_ref[hh], NN).astype(BF16)
                lset_ref[hh] = (m_sc[hh] + jnp.log2(l)).T[0:1, :]

        if hosting:
            half_way = (ng * len(pairs)) // 2

            @pl.when(grp * len(pairs) + st == half_way)
            def _():
                ag_forward()

            @pl.when((grp == ng - 1) & (st == len(pairs) - 1))
            def _():
                ag_finish()

    in_specs = [pl.BlockSpec((bq, hb * KD), lambda g, s, it, jt: (it[s], g)),
                pl.BlockSpec((bq, KD), lambda g, s, it, jt: (jt[s], 0)),
                pl.BlockSpec((hb, KVR, NOPE), lambda g, s, it, jt: (g, 0, 0))]
    out_specs = [pl.BlockSpec((bq, hb * KVR), lambda g, s, it, jt: (it[s], g)),
                 pl.BlockSpec((hb, 1, bq), lambda g, s, it, jt: (g, 0, it[s])),
                 pl.BlockSpec((bq, hb * NOPE), lambda g, s, it, jt: (it[s], g))]
    out_shape = [jax.ShapeDtypeStruct((t, H * KVR), F32), jax.ShapeDtypeStruct((H, 1, t), F32),
                 jax.ShapeDtypeStruct((t, H * NOPE), BF16)]
    scratch = [pltpu.VMEM((hb, bq, 128), F32), pltpu.VMEM((hb, bq, 128), F32), pltpu.VMEM((hb, bq, KVR), F32)]
    args = [itab, jtab, qcat, kc, wuv_h]
    if hosting:
        in_specs.append(ANY)
        out_specs.append(ANY)
        out_shape.append(jax.ShapeDtypeStruct((4,) + gather.shape, gather.dtype))
        scratch += AG_SEMS
        args.append(gather)
    gs = pltpu.PrefetchScalarGridSpec(num_scalar_prefetch=2, grid=(ng, len(pairs)), in_specs=in_specs,
                                      out_specs=out_specs, scratch_shapes=scratch)
    return pl.pallas_call(kern, grid_spec=gs, out_shape=out_shape, name="mla_flash_fwd",
                          compiler_params=_cp())(*args)


def _flash_dkv(qcat, kc, do_lat, lse_t, delta_t, bq, hb, exchange=()):
    nx = len(exchange)
    t = kc.shape[0]
    nq = t // bq
    ng = H // hb
    npairs = nq * (nq + 1) // 2
    steps = [(j, g, i) for j in range(nq) for g in range(ng) for i in range(j, nq)]
    jtab = jnp.asarray(np.array([s[0] for s in steps], np.int32))
    gtab = jnp.asarray(np.array([s[1] for s in steps], np.int32))
    itab = jnp.asarray(np.array([s[2] for s in steps], np.int32))
    ptab = jnp.asarray(np.array([s[2] * (s[2] + 1) // 2 + s[0] for s in steps], np.int32))

    def kern(jt, gt, it, pt, q_ref, k_ref, do_ref, lset_ref, dlt_ref, *rest):
        p_refs, (dk_ref, ds_ref), slots_refs = rest[:nx], rest[nx:nx + 2], rest[nx + 2:2 * nx + 2]
        dk_sc, dv_sc = rest[2 * nx + 2:2 * nx + 4]
        sems = rest[2 * nx + 4:]
        hooks = [_device_exchange_schedule(p_refs[e], slots_refs[e], sems[2 * e], sems[2 * e + 1]) for e in range(nx)]
        st = pl.program_id(0)
        j, g, i = jt[st], gt[st], it[st]

        if nx:
            @pl.when(st == 0)
            def _():
                for start, _ in hooks:
                    start()

        @pl.when((g == 0) & (i == j))
        def _():
            dk_sc[...] = jnp.zeros_like(dk_sc)
            dv_sc[...] = jnp.zeros_like(dv_sc)

        def update(masked):
            k = k_ref[...]
            v = k[:, 0:KVR]
            if masked:
                row = lax.broadcasted_iota(jnp.int32, (bq, bq), 0)
                col = lax.broadcasted_iota(jnp.int32, (bq, bq), 1)
                keep = row <= col

            def first_matmuls(hh):
                dob = do_ref[:, hh * KVR:(hh + 1) * KVR].astype(BF16)
                return _dot(k, q_ref[:, hh * KD:(hh + 1) * KD], NT), _dot(v, dob, NT), dob

            pending = [first_matmuls(hh) for hh in range(min(AHEAD, hb))]
            for hh in range(hb):
                s, dp, dob = pending.pop(0)
                if hh + AHEAD < hb:
                    pending.append(first_matmuls(hh + AHEAD))
                if masked:
                    s = jnp.where(keep, s, NEG)
                p = jnp.exp2(s - lset_ref[hh])
                dv_sc[...] += _dot(p.astype(BF16), dob, NN)
                dsb = (p * (dp - dlt_ref[hh])).astype(BF16)
                ds_ref[0, 0, hh] = dsb
                dk_sc[...] += _dot(dsb, q_ref[:, hh * KD:(hh + 1) * KD], NN)

        @pl.when(i > j)
        def _():
            update(False)

        @pl.when(i == j)
        def _():
            update(True)

        @pl.when((g == ng - 1) & (i == nq - 1))
        def _():
            dk_ref[:, 0:KVR] = dk_sc[:, 0:KVR] * LN2 + dv_sc[...]
            dk_ref[:, KVR:KD] = dk_sc[:, KVR:KD] * LN2

        if nx:
            @pl.when(st == len(steps) - 1)
            def _():
                for _, finish in hooks:
                    finish()

    in_specs = [pl.BlockSpec((bq, hb * KD), lambda s, jt, gt, it, pt: (it[s], gt[s])),
                pl.BlockSpec((bq, KD), lambda s, jt, gt, it, pt: (jt[s], 0)),
                pl.BlockSpec((bq, hb * KVR), lambda s, jt, gt, it, pt: (it[s], gt[s])),
                pl.BlockSpec((hb, 1, bq), lambda s, jt, gt, it, pt: (gt[s], 0, it[s])),
                pl.BlockSpec((hb, 1, bq), lambda s, jt, gt, it, pt: (gt[s], 0, it[s]))] + [ANY] * nx
    out_specs = [pl.BlockSpec((bq, KD), lambda s, jt, gt, it, pt: (jt[s], 0)),
                 pl.BlockSpec((1, 1, hb, bq, bq), lambda s, jt, gt, it, pt: (gt[s], pt[s], 0, 0, 0))] + [ANY] * nx
    out_shape = [jax.ShapeDtypeStruct((t, KD), F32), jax.ShapeDtypeStruct((ng, npairs, hb, bq, bq), BF16)]
    out_shape += [jax.ShapeDtypeStruct((8, e.shape[1] // 2, D), e.dtype) for e in exchange]
    scratch = [pltpu.VMEM((bq, KD), F32), pltpu.VMEM((bq, KVR), F32)] + ALL_SEMS * nx
    args = [jtab, gtab, itab, ptab, qcat, kc, do_lat, lse_t, delta_t, *exchange]
    gs = pltpu.PrefetchScalarGridSpec(num_scalar_prefetch=4, grid=(len(steps),), in_specs=in_specs,
                                      out_specs=out_specs, scratch_shapes=scratch)
    return pl.pallas_call(kern, grid_spec=gs, out_shape=out_shape, name="mla_flash_dkv",
                          compiler_params=_cp())(*args)


def _flash_dq(ds_all, kc_t, bq, exchange=None):
    nq = kc_t.shape[0]
    t = nq * bq
    ngrp, _, hper = ds_all.shape[:3]
    pairs = _causal_pairs(nq)
    itab = jnp.asarray(np.array([p[0] for p in pairs], np.int32))
    jtab = jnp.asarray(np.array([p[1] for p in pairs], np.int32))
    hosting = exchange is not None

    def kern(it, jt, *refs):
        ds_refs, kt_ref, rest = refs[:ngrp], refs[ngrp], refs[ngrp + 1:]
        if hosting:
            p_ref, dq_ref, slots_ref, acc_sc, send_sems, recv_sems = rest
            xc_start, xc_finish = _device_exchange_schedule(p_ref, slots_ref, send_sems, recv_sems)
        else:
            dq_ref, acc_sc = rest
        st = pl.program_id(0)
        i, j = it[st], jt[st]
        kt = kt_ref[...]

        def ds(hh):
            return ds_refs[hh // hper][0, 0, hh % hper]

        if hosting:
            @pl.when(st == 0)
            def _():
                xc_start()

        @pl.when(j == 0)
        def _():
            for hh in range(H):
                acc_sc[hh] = _dot(kt, ds(hh), NN)

        @pl.when((j > 0) & (j < i))
        def _():
            for hh in range(H):
                acc_sc[hh] += _dot(kt, ds(hh), NN)

        @pl.when(j == i)
        def _():
            for hh in range(H):
                tot = _dot(kt, ds(hh), NN)
                tot = jnp.where(i > 0, tot + acc_sc[hh], tot)
                dq_ref[:, hh * KD:(hh + 1) * KD] = tot.T * MLA_SCALE

        if hosting:
            @pl.when(st == len(pairs) - 1)
            def _():
                xc_finish()

    def group(gi):
        return pl.BlockSpec((1, 1, hper, bq, bq), lambda s, it, jt: (gi, s, 0, 0, 0))

    in_specs = [group(gi) for gi in range(ngrp)] + [pl.BlockSpec((None, KD, bq), lambda s, it, jt: (jt[s], 0, 0))]
    out_specs = [pl.BlockSpec((bq, H * KD), lambda s, it, jt: (it[s], 0))]
    out_shape = [jax.ShapeDtypeStruct((t, H * KD), F32)]
    scratch = [pltpu.VMEM((H, KD, bq), F32)]
    args = [itab, jtab] + [ds_all] * ngrp + [kc_t]
    if hosting:
        in_specs.append(ANY)
        out_specs.append(ANY)
        out_shape.append(jax.ShapeDtypeStruct((8, exchange.shape[1] // 2, D), exchange.dtype))
        scratch += ALL_SEMS
        args.append(exchange)
    gs = pltpu.PrefetchScalarGridSpec(num_scalar_prefetch=2, grid=(len(pairs),), in_specs=in_specs,
                                      out_specs=out_specs, scratch_shapes=scratch)
    outs = pl.pallas_call(kern, grid_spec=gs, out_shape=out_shape, name="mla_flash_dq",
                          compiler_params=_cp())(*args)
    return outs if hosting else outs[0]


def _bucket_table():
    d = np.arange(WIN)
    max_exact = NBKT // 2
    nf = np.maximum(d, 1).astype(np.float32)
    large = max_exact + (np.log(nf / np.float32(max_exact)) / np.float32(math.log(WIN / max_exact))
                         * np.float32(NBKT - max_exact)).astype(np.int32)
    large = np.minimum(large, NBKT - 1)
    bucket = np.where(d < max_exact, d, large).astype(np.int32)
    jj = np.arange(2 * WIN)[:, None]
    ii = np.arange(WIN)[None, :]
    dist = ii + WIN - jj
    valid = (dist >= 0) & (dist < WIN)
    return np.where(valid, bucket[np.clip(dist, 0, WIN - 1)], -1).astype(np.int32)


def _bias_build(rel_bias, bkt):
    def kern(bk_ref, rb_ref, o_ref):
        bk = bk_ref[...]
        for hd in range(QH):
            acc = jnp.full((2 * WIN, WIN), NEG, F32)
            for b in range(NBKT):
                acc = jnp.where(bk == b, rb_ref[b, hd], acc)
            o_ref[hd] = acc

    return pl.pallas_call(
        kern, in_specs=[pl.BlockSpec(memory_space=pltpu.VMEM), pl.BlockSpec(memory_space=pltpu.SMEM)],
        out_specs=pl.BlockSpec(memory_space=pltpu.VMEM),
        out_shape=jax.ShapeDtypeStruct((QH, 2 * WIN, WIN), F32), name="swa_bias_build")(bkt, rel_bias)


def _bias_bwd(dbias, bkt):
    def kern(db_ref, bk_ref, o_ref):
        bk = bk_ref[...]
        for hd in range(QH):
            g = db_ref[hd]
            for b in range(NBKT):
                r = b * QH + hd
                o_ref[r:r + 1, :] = jnp.sum(jnp.where(bk == b, g, 0.0), axis=0, keepdims=True)

    return pl.pallas_call(
        kern, in_specs=[pl.BlockSpec(memory_space=pltpu.VMEM), pl.BlockSpec(memory_space=pltpu.VMEM)],
        out_specs=pl.BlockSpec(memory_space=pltpu.VMEM),
        out_shape=jax.ShapeDtypeStruct((NBKT * QH, WIN), F32), name="swa_bias_bwd")(dbias, bkt)


def _swa_finish_scores(raw, bias, first):
    s = raw * SWA_SCALE + bias
    if first is not None:
        row = lax.broadcasted_iota(jnp.int32, s.shape, 0)
        s = jnp.where(jnp.logical_or(jnp.logical_not(first), row >= WIN), s, NEG)
    return s


def _swa_fwd(qkv_t, bias, sinks, qb):
    t = qkv_t.shape[1]
    w = qb * WIN
    nst = t // w

    def kern(q_ref, kc_ref, kp_ref, vc_ref, vp_ref, b_ref, sk_ref, o_ref, lse_ref):
        n = pl.program_id(0)
        kfull = jnp.concatenate([kp_ref[...], kc_ref[...]], axis=1)
        vfull = jnp.concatenate([vp_ref[...], vc_ref[...]], axis=1)
        head_row = lax.broadcasted_iota(jnp.int32, (QH, WIN), 0)
        groups = [(b, kh) for b in range(qb) for kh in range(KVH)]

        def raw_scores(b, kh):
            k_band = kfull[kh * HD:(kh + 1) * HD, b * WIN:(b + 2) * WIN]
            return [_dot(k_band, q_ref[(kh * G + g) * HD:(kh * G + g + 1) * HD, b * WIN:(b + 1) * WIN], TN)
                    for g in range(G)]

        o_rows = [[] for _ in range(qb)]
        lse_tiles = [jnp.zeros((QH, WIN), F32) for _ in range(qb)]
        pending = [raw_scores(*grp) for grp in groups[:AHEAD]]
        for gi, (b, kh) in enumerate(groups):
            scores = pending.pop(0)
            if gi + AHEAD < len(groups):
                pending.append(raw_scores(*groups[gi + AHEAD]))
            v_band = vfull[kh * HD:(kh + 1) * HD, b * WIN:(b + 2) * WIN]
            for g in range(G):
                hd = kh * G + g
                s = _swa_finish_scores(scores[g], b_ref[hd], (n == 0) if b == 0 else None)
                sink = sk_ref[hd]
                m = jnp.maximum(jnp.max(s, axis=0, keepdims=True), sink)
                p = jnp.exp(s - m)
                den = jnp.sum(p, axis=0, keepdims=True) + jnp.exp(sink - m)
                p = p / den
                o_rows[b].append(_dot(v_band, p.astype(BF16), NN))
                lse_tiles[b] = jnp.where(head_row == hd, m + jnp.log(den), lse_tiles[b])
        o_ref[...] = jnp.concatenate([jnp.concatenate(rows, axis=0) for rows in o_rows], axis=1)
        lse_ref[...] = jnp.concatenate(lse_tiles, axis=1)

    prev = lambda r: (lambda n: (r, jnp.maximum(n * qb - 1, 0)))
    return pl.pallas_call(
        kern, grid=(nst,),
        in_specs=[pl.BlockSpec((QH * HD, w), lambda n: (0, n)),
                  pl.BlockSpec((KVH * HD, w), lambda n: (4, n)), pl.BlockSpec((KVH * HD, WIN), prev(4)),
                  pl.BlockSpec((KVH * HD, w), lambda n: (5, n)), pl.BlockSpec((KVH * HD, WIN), prev(5)),
                  pl.BlockSpec((QH, 2 * WIN, WIN), lambda n: (0, 0, 0)),
                  pl.BlockSpec(memory_space=pltpu.SMEM)],
        out_specs=[pl.BlockSpec((QH * HD, w), lambda n: (0, n)), pl.BlockSpec((QH, w), lambda n: (0, n))],
        out_shape=[jax.ShapeDtypeStruct((QH * HD, t), F32), jax.ShapeDtypeStruct((QH, t), F32)],
        name="swa_fwd", compiler_params=_cp())(qkv_t, qkv_t, qkv_t, qkv_t, qkv_t, bias, sinks)


def _swa_bwd(qkv_t, do_t, o_t, lse, bias, sinks, qb):
    t = qkv_t.shape[1]
    w = qb * WIN
    nst = t // w
    nblk = t // WIN

    def kern(q_ref, kc_ref, kp_ref, vc_ref, vp_ref, do_ref, o_ref, lse_ref, qn_ref, don_ref, on_ref, lsen_ref,
             b_ref, sk_ref, dqkv_ref, db_ref, dsk_ref):
        n = pl.program_id(0)

        @pl.when(n == 0)
        def _():
            db_ref[...] = jnp.zeros_like(db_ref)
            dsk_ref[...] = jnp.zeros_like(dsk_ref)

        kfull = jnp.concatenate([kp_ref[...], kc_ref[...]], axis=1)
        vfull = jnp.concatenate([vp_ref[...], vc_ref[...]], axis=1)
        head_row = lax.broadcasted_iota(jnp.int32, (QH, WIN), 0)
        db_acc = [None] * QH
        dsk_tile = jnp.zeros((QH, WIN), F32)
        prev_part = [[[None] * qb for _ in range(KVH)] for _ in range(2)]
        cur_part = [[[None] * qb for _ in range(KVH)] for _ in range(2)]
        groups = [(b, kh) for b in range(qb) for kh in range(KVH)]

        def first_matmuls(b, kh):
            k_band = kfull[kh * HD:(kh + 1) * HD, b * WIN:(b + 2) * WIN]
            v_band = vfull[kh * HD:(kh + 1) * HD, b * WIN:(b + 2) * WIN]
            out = []
            for g in range(G):
                rs = slice((kh * G + g) * HD, (kh * G + g + 1) * HD)
                dob = do_ref[rs, b * WIN:(b + 1) * WIN].astype(BF16)
                out.append((_dot(k_band, q_ref[rs, b * WIN:(b + 1) * WIN], TN), _dot(v_band, dob, TN), dob))
            return out

        dq_rows = [[] for _ in range(qb)]
        pending = [first_matmuls(*grp) for grp in groups[:AHEAD]]
        for gi, (b, kh) in enumerate(groups):
            first = pending.pop(0)
            if gi + AHEAD < len(groups):
                pending.append(first_matmuls(*groups[gi + AHEAD]))
            cs = slice(b * WIN, (b + 1) * WIN)
            k_band = kfull[kh * HD:(kh + 1) * HD, b * WIN:(b + 2) * WIN]
            dk_b = dv_b = None
            for g in range(G):
                hd = kh * G + g
                rs = slice(hd * HD, (hd + 1) * HD)
                raw, dp, dob = first[g]
                lse_h = lse_ref[hd:hd + 1, cs]
                s = _swa_finish_scores(raw, b_ref[hd], (n == 0) if b == 0 else None)
                p = jnp.exp(s - lse_h)
                dl = jnp.sum(do_ref[rs, cs] * o_ref[rs, cs], axis=0, keepdims=True)
                ds = p * (dp - dl)
                db_acc[hd] = ds if db_acc[hd] is None else db_acc[hd] + ds
                dsk_tile = jnp.where(head_row == hd, dsk_tile - jnp.exp(sk_ref[hd] - lse_h) * dl, dsk_tile)
                dss = (ds * SWA_SCALE).astype(BF16)
                dq_rows[b].append(_dot(k_band, dss, NN).astype(BF16))
                dk_h = _dot(q_ref[rs, cs], dss, NT)
                dv_h = _dot(dob, p.astype(BF16), NT)
                dk_b = dk_h if dk_b is None else dk_b + dk_h
                dv_b = dv_h if dv_b is None else dv_b + dv_h
            for which, val in ((0, dk_b), (1, dv_b)):
                prev_part[which][kh][b] = val[:, 0:WIN]
                cur_part[which][kh][b] = val[:, WIN:2 * WIN]
        dq_cols = [jnp.concatenate(rows, axis=0) for rows in dq_rows]

        live = n < nst - 1
        ls = slice((qb - 1) * WIN, qb * WIN)
        halo = [[None] * KVH for _ in range(2)]
        for kh in range(KVH):
            k_last = kc_ref[kh * HD:(kh + 1) * HD, ls]
            v_last = vc_ref[kh * HD:(kh + 1) * HD, ls]
            dk_b = dv_b = None
            for g in range(G):
                hd = kh * G + g
                rs = slice(hd * HD, (hd + 1) * HD)
                q_t = qn_ref[rs, :]
                do = don_ref[rs, :]
                s = _dot(k_last, q_t, TN) * SWA_SCALE + b_ref[hd, 0:WIN, :]
                p = jnp.exp(s - lsen_ref[hd:hd + 1, :])
                dob = do.astype(BF16)
                dp = _dot(v_last, dob, TN)
                dl = jnp.sum(do * on_ref[rs, :], axis=0, keepdims=True)
                dss = (p * (dp - dl) * SWA_SCALE).astype(BF16)
                dk_h = _dot(q_t, dss, NT)
                dv_h = _dot(dob, p.astype(BF16), NT)
                dk_b = dk_h if dk_b is None else dk_b + dk_h
                dv_b = dv_h if dv_b is None else dv_b + dv_h
            halo[0][kh] = jnp.where(live, dk_b, 0.0)
            halo[1][kh] = jnp.where(live, dv_b, 0.0)

        kv_rows = []
        for which in range(2):
            for kh in range(KVH):
                blocks = [cur_part[which][kh][p] + (prev_part[which][kh][p + 1] if p + 1 < qb else halo[which][kh])
                          for p in range(qb)]
                kv_rows.append(jnp.concatenate(blocks, axis=1))
        dqkv_ref[...] = jnp.concatenate(
            [jnp.concatenate(dq_cols, axis=1), jnp.concatenate(kv_rows, axis=0).astype(BF16)], axis=0)
        db_ref[...] += jnp.stack(db_acc)
        dsk_ref[...] += dsk_tile

    prev = lambda r: (lambda n: (r, jnp.maximum(n * qb - 1, 0)))
    nxt = lambda n: (0, jnp.minimum((n + 1) * qb, nblk - 1))
    big = lambda: pl.BlockSpec((QH * HD, w), lambda n: (0, n))
    return pl.pallas_call(
        kern, grid=(nst,),
        in_specs=[big(),
                  pl.BlockSpec((KVH * HD, w), lambda n: (4, n)), pl.BlockSpec((KVH * HD, WIN), prev(4)),
                  pl.BlockSpec((KVH * HD, w), lambda n: (5, n)), pl.BlockSpec((KVH * HD, WIN), prev(5)),
                  big(), big(), pl.BlockSpec((QH, w), lambda n: (0, n)),
                  pl.BlockSpec((QH * HD, WIN), nxt), pl.BlockSpec((QH * HD, WIN), nxt),
                  pl.BlockSpec((QH * HD, WIN), nxt), pl.BlockSpec((QH, WIN), nxt),
                  pl.BlockSpec((QH, 2 * WIN, WIN), lambda n: (0, 0, 0)),
                  pl.BlockSpec(memory_space=pltpu.SMEM)],
        out_specs=[pl.BlockSpec(((QH + 2 * KVH) * HD, w), lambda n: (0, n)),
                   pl.BlockSpec((QH, 2 * WIN, WIN), lambda n: (0, 0, 0)),
                   pl.BlockSpec((QH, WIN), lambda n: (0, 0))],
        out_shape=[jax.ShapeDtypeStruct(((QH + 2 * KVH) * HD, t), BF16),
                   jax.ShapeDtypeStruct((QH, 2 * WIN, WIN), F32), jax.ShapeDtypeStruct((QH, WIN), F32)],
        name="swa_bwd", compiler_params=_cp())(
            qkv_t, qkv_t, qkv_t, qkv_t, qkv_t, do_t, o_t, lse, qkv_t, do_t, o_t, lse, bias, sinks)


def _adamw_math(w, g, m, v):
    nm = B1 * m + (1.0 - B1) * g
    nv = B2 * v + (1.0 - B2) * (g * g)
    mhat = nm * (1.0 / (1.0 - B1 ** STEP))
    vhat = nv * (1.0 / (1.0 - B2 ** STEP))
    return -LR * (mhat / (jnp.sqrt(vhat) + ADAM_EPS) + WD * w), nm, nv


def _adamw_layers(w, m, v, g0buf, g1buf, off, name, tm=512):
    rows = w.shape[1]
    nb, ob = rows // tm, off // tm

    def kern(w_ref, m_ref, v_ref, g0_ref, g1_ref, gr_ref, d_ref, nm_ref, nv_ref):
        g_ = jnp.where(pl.program_id(0) == 0, g0_ref[...], g1_ref[...])
        gr_ref[...] = g_
        d_ref[...], nm_ref[...], nv_ref[...] = _adamw_math(w_ref[...], g_, m_ref[...], v_ref[...])

    lay = pl.BlockSpec((None, tm, D), lambda l, i: (l, i, 0))
    gsp = pl.BlockSpec((tm, D), lambda l, i: (ob + i, 0))
    return pl.pallas_call(
        kern, grid=(2, nb), in_specs=[lay, lay, lay, gsp, gsp], out_specs=[lay] * 4,
        out_shape=[jax.ShapeDtypeStruct(w.shape, F32)] * 4, name=name, compiler_params=_cp())(w, m, v, g0buf, g1buf)


def _adamw(w, g, m, v, name, tm=544):
    r = w.shape[0]
    tm = r if r % tm else tm

    def kern(w_ref, g_ref, m_ref, v_ref, d_ref, nm_ref, nv_ref):
        d_ref[...], nm_ref[...], nv_ref[...] = _adamw_math(w_ref[...], g_ref[...], m_ref[...], v_ref[...])

    row = pl.BlockSpec((tm, D), lambda i: (i, 0))
    sds = jax.ShapeDtypeStruct((r, D), F32)
    return pl.pallas_call(kern, grid=(r // tm,), in_specs=[row] * 4, out_specs=[row] * 3, out_shape=[sds] * 3,
                          name=name, compiler_params=_cp())(w, g, m, v)


def _mesh_pos():
    return lax.axis_index("x"), lax.axis_index("y"), lax.axis_index("c")


ANY = pl.BlockSpec(memory_space=pl.ANY)


AG_SEMS = [pltpu.SemaphoreType.DMA((6,)), pltpu.SemaphoreType.DMA((6,))]


def _allgather_schedule(w_ref, out_ref, send_sems, recv_sems):
    half = w_ref.shape[0] // 2
    x, y, c = _mesh_pos()
    me, sibling = (x, y, c), (x, y, 1 - c)
    chips = [(1 - x, y), (x, 1 - y), (1 - x, 1 - y)]

    def rows(px, py, pc):
        return out_ref.at[2 * px + py, pl.ds(pc * half, half), :]

    def copy(k, block, to, src=None):
        return pltpu.make_async_remote_copy(
            src_ref=rows(*block) if src is None else src, dst_ref=rows(*block),
            send_sem=send_sems.at[k], recv_sem=recv_sems.at[k], device_id=to, device_id_type=MESH)

    def first():
        return [copy(j, me, (*chip, c), src=w_ref.at[pl.ds(c * half, half), :]) for j, chip in enumerate(chips)]

    def passed():
        return [copy(3 + j, (*chip, c), sibling) for j, chip in enumerate(chips)]

    def start():
        for cp in first():
            cp.start()

    def forward():
        for j, chip in enumerate(chips):
            copy(j, (*chip, c), me).wait_recv()
            passed()[j].start()

    def finish():
        for j, chip in enumerate(chips):
            copy(3 + j, (*chip, 1 - c), me).wait_recv()
        for cp in first() + passed():
            cp.wait_send()

    return start, forward, finish


def _allgather_weights(wpack):
    def body(w_ref, out_ref, send_sems, recv_sems):
        start, forward, finish = _allgather_schedule(w_ref, out_ref, send_sems, recv_sems)
        start()
        forward()
        finish()

    return pl.pallas_call(
        body, out_shape=jax.ShapeDtypeStruct((4,) + wpack.shape, wpack.dtype), in_specs=[ANY], out_specs=ANY,
        scratch_shapes=AG_SEMS, name="allgather_weights")(wpack)


def _row_tile(rows):
    t = min(rows, 512)
    while rows % t or t % 16:
        t -= 16
    return t


ALL_SEMS = [pltpu.SemaphoreType.DMA((7,)), pltpu.SemaphoreType.DMA((7,))]


def _device_exchange_schedule(g_ref, out_ref, send_sems, recv_sems):
    half = g_ref.shape[1] // 2
    x, y, c = _mesh_pos()
    me = 4 * x + 2 * y + c
    peers = [(x ^ (k >> 2), y ^ ((k >> 1) & 1), c ^ (k & 1)) for k in range(1, 8)]

    def sends():
        return [pltpu.make_async_remote_copy(
            src_ref=g_ref.at[2 * px + py, pl.ds(pc * half, half), :], dst_ref=out_ref.at[me],
            send_sem=send_sems.at[j], recv_sem=recv_sems.at[j], device_id=(px, py, pc), device_id_type=MESH)
            for j, (px, py, pc) in enumerate(peers)]

    def start():
        for cp in sends():
            cp.start()

    def finish():
        for j, (px, py, pc) in enumerate(peers):
            pltpu.make_async_remote_copy(
                src_ref=out_ref.at[me], dst_ref=out_ref.at[4 * px + 2 * py + pc], send_sem=send_sems.at[j],
                recv_sem=recv_sems.at[j], device_id=(px, py, pc), device_id_type=MESH).wait_recv()
        for cp in sends():
            cp.wait_send()

    return start, finish


def _sum_devices(slots, g, pos, tag):
    half = slots.shape[1]
    tm = _row_tile(half)
    nb = half // tm

    def kern(pos_ref, own_ref, *refs):
        acc = own_ref[0].astype(F32)
        for s_ref in refs[:7]:
            acc = acc + s_ref[0].astype(F32)
        refs[7][...] = acc

    def slot(k):
        return pl.BlockSpec((1, tm, D), lambda i, pos: (jnp.bitwise_xor(pos[2], k), i, 0))

    gs = pltpu.PrefetchScalarGridSpec(
        num_scalar_prefetch=1, grid=(nb,),
        in_specs=[pl.BlockSpec((1, tm, D), lambda i, pos: (pos[0], pos[1] * nb + i, 0))] + [slot(k) for k in range(1, 8)],
        out_specs=pl.BlockSpec((tm, D), lambda i, pos: (pos[1] * nb + i, 0)))
    return pl.pallas_call(kern, grid_spec=gs, out_shape=jax.ShapeDtypeStruct((2 * half, D), F32),
                          name=f"rs_sum_devices_{tag}", compiler_params=_cp())(pos, g, *([slots] * 7))


def _reduce_scatter_finish(slots, g, pos, tag):
    return _join_core_halves(_sum_devices(slots, g, pos, tag), tag)


def _join_core_halves(r, tag):
    half = r.shape[0] // 2

    def body(r_ref, out_ref, send_sem, recv_sem):
        x, y, c = _mesh_pos()
        mine = out_ref.at[pl.ds(c * half, half), :]
        cp = pltpu.make_async_remote_copy(
            src_ref=mine, dst_ref=mine, send_sem=send_sem, recv_sem=recv_sem,
            device_id=(x, y, 1 - c), device_id_type=MESH)
        cp.start()
        theirs = out_ref.at[pl.ds((1 - c) * half, half), :]
        pltpu.make_async_remote_copy(
            src_ref=theirs, dst_ref=theirs, send_sem=send_sem, recv_sem=recv_sem,
            device_id=(x, y, 1 - c), device_id_type=MESH).wait_recv()
        cp.wait_send()

    return pl.pallas_call(
        body, out_shape=jax.ShapeDtypeStruct(r.shape, r.dtype), in_specs=[ANY], out_specs=ANY,
        input_output_aliases={0: 0},
        scratch_shapes=[pltpu.SemaphoreType.DMA, pltpu.SemaphoreType.DMA],
        name=f"rs_join_cores_{tag}")(r)


def _allreduce_small(v, name):
    def body(v_ref, out_ref, gat, send_sems, recv_sems):
        x, y, c = _mesh_pos()
        me = 4 * x + 2 * y + c
        gat[me] = v_ref[...]
        sends = []
        for k in range(1, 8):
            peer = (x ^ (k >> 2), y ^ ((k >> 1) & 1), c ^ (k & 1))
            cp = pltpu.make_async_remote_copy(
                src_ref=v_ref, dst_ref=gat.at[me], send_sem=send_sems.at[k - 1], recv_sem=recv_sems.at[k - 1],
                device_id=peer, device_id_type=MESH)
            cp.start()
            sends.append(cp)
        for k in range(1, 8):
            px, py, pc = x ^ (k >> 2), y ^ ((k >> 1) & 1), c ^ (k & 1)
            pltpu.make_async_remote_copy(
                src_ref=v_ref, dst_ref=gat.at[4 * px + 2 * py + pc], send_sem=send_sems.at[k - 1],
                recv_sem=recv_sems.at[k - 1], device_id=(px, py, pc), device_id_type=MESH).wait_recv()
        for cp in sends:
            cp.wait_send()
        acc = gat[0]
        for d in range(1, 8):
            acc = acc + gat[d]
        out_ref[...] = acc

    return pl.pallas_call(
        body, out_shape=jax.ShapeDtypeStruct(v.shape, F32),
        in_specs=[pl.BlockSpec(memory_space=pltpu.VMEM)], out_specs=pl.BlockSpec(memory_space=pltpu.VMEM),
        scratch_shapes=[pltpu.VMEM((8,) + v.shape, F32), pltpu.SemaphoreType.DMA((7,)), pltpu.SemaphoreType.DMA((7,))],
        name=name)(v)


def _mlp_fwd(xb, w_up, w_down, tag):
    a = _mm(xb, w_up[0], "nn", f"mlp_up_{tag}", out_dtype=BF16, relu2=True, b_view=("cols", w_up[1]))
    return a, _mm(a, w_down[0], "nn", f"mlp_down_{tag}", b_view=("rows", w_down[1]), tm=2048)


def _mlp_bwd(dz, dzb, xb, a, w_up, w_down, tag):
    du = _mm(dzb, w_down[0], "nt", f"mlp_down_dx_{tag}", out_dtype=BF16, gate_a=a, b_view=("rows", w_down[1]),
             tm=2048)
    gsh = _mm(xb, du, "tn", f"mlp_up_dw_{tag}", out_dtype=BF16, out_view=("cols", 2 * ROWS["mlp_w_up"], 0, None))
    gsh = _mm(a, dzb, "tn", f"mlp_down_dw_{tag}", out_dtype=BF16,
              out_view=("rows", 2 * ROWS["mlp_w_up"], ROWS["mlp_w_up"], gsh))
    dx = _mm(du, w_up[0], "nt", f"mlp_up_dx_{tag}", addend=dz, add_scale=ALPHA, b_view=("cols", w_up[1]))
    return dx, gsh


def _fwd_bwd(x, target, w, dist=None, bq=512, qb=8, hb=8):
    t = x.shape[0]
    bq = min(bq, t)
    qb = min(qb, t // WIN)
    cos, sin = _rope_tables(t)
    bkt = jnp.asarray(_bucket_table())
    w_in = jnp.pad(w[("mla_w_in", None)], ((0, 0), (0, HW - (QR + KVR + ROPE))))
    wuq = w[("mla_w_uq", None)]
    wq2 = jnp.concatenate([wuq[:, :, :NOPE].reshape(QR, H * NOPE),
                           jnp.pad(wuq[:, :, NOPE:], ((0, 0), (0, 0), (0, RP - ROPE))).reshape(QR, H * RP)], axis=1)
    wuk_t = w[("mla_w_uk", None)].transpose(1, 2, 0)
    wuk_h = w[("mla_w_uk", None)].transpose(1, 0, 2)
    wuv_h = w[("mla_w_uv", None)].transpose(1, 0, 2)
    w_o = w[("mla_w_o", None)]
    sinks = w["swa_sinks"].reshape(QH)
    lnp = lambda n, l: w[n][l]
    reduced = {}

    hh = _mm(x, w_in, "nn", "mla_in")
    cq, kc = _mla_pre(hh, w["mla_g_q"], w["mla_g_kv"], cos, sin)
    q2 = _mm(cq, wq2, "nn", "mla_uq")
    qcat = _q_prep(q2, wuk_t, cos, sin)
    if dist is None:
        o_lat, lse0_t, o0 = _flash_fwd(qcat, kc, wuv_h, bq, hb)
    else:
        o_lat, lse0_t, o0, wall = _flash_fwd(qcat, kc, wuv_h, bq, hb, gather=dist.late_pack)
        wall = lax.dynamic_update_slice(wall, dist.late_pack[None], (dist.shard, 0, 0))
        w = {**w, **_full_from_gathered(AG_LATE, wall, dist.shard_shapes)}
    wqkv = jnp.concatenate([w[("swa_w_q", None)], w[("kv_w_shared", None)]], axis=1)
    wqkv_t = wqkv.T
    wo_s = w[("swa_w_o", None)]
    y0 = _mm(o0, w_o, "nn", "mla_out")
    x1b, xh1, r1 = _add_ln(x, y0, lnp("ln_mix_g", 0), lnp("ln_mix_b", 0), "ln_mix_0")
    a0, f0 = _mlp_fwd(x1b, w[("mlp_w_up", 0)], w[("mlp_w_down", 0)], 0)
    x2b, xh2, r2 = _add_ln(xh1, f0, lnp("ln_mlp_g", 0), lnp("ln_mlp_b", 0), "ln_mlp_0",
                           res_affine=(lnp("ln_mix_g", 0), lnp("ln_mix_b", 0)))
    bias = _bias_build(w["rel_bias"], bkt)
    qkv_t = _mm(x2b, wqkv, "nn", "swa_qkv", out_dtype=BF16, out_t=True)
    os_t, lse1 = _swa_fwd(qkv_t, bias, sinks, qb)
    y1 = _mm(os_t, wo_s, "tn", "swa_out")
    x3b, xh3, r3 = _add_ln(xh2, y1, lnp("ln_mix_g", 1), lnp("ln_mix_b", 1), "ln_mix_1",
                           res_affine=(lnp("ln_mlp_g", 0), lnp("ln_mlp_b", 0)))
    a1, f1 = _mlp_fwd(x3b, w[("mlp_w_up", 1)], w[("mlp_w_down", 1)], 1)
    _, xh4, r4 = _add_ln(xh3, f1, lnp("ln_mlp_g", 1), lnp("ln_mlp_b", 1), "ln_mlp_1",
                         res_affine=(lnp("ln_mix_g", 1), lnp("ln_mix_b", 1)))

    g = {}
    dz4, dz4b, dg_mlp1, db_mlp1, lpart = _ln_bwd(target, xh4, r4, lnp("ln_mlp_g", 1), "ln_mlp_1_bwd",
                                                 loss_b=lnp("ln_mlp_b", 1))
    dx3, g["mlp1"] = _mlp_bwd(dz4, dz4b, x3b, a1, w[("mlp_w_up", 1)], w[("mlp_w_down", 1)], 1)
    dz3, dz3b, dg_mix1, db_mix1 = _ln_bwd(dx3, xh3, r3, lnp("ln_mix_g", 1), "ln_mix_1_bwd")
    dos_t = _mm(dz3b, wo_s, "nt", "swa_out_dx", out_t=True)
    g[("swa_w_o", None)] = _mm(os_t, dz3b, "nn", "swa_out_dw")
    dqkv_t, dbias, dsk = _swa_bwd(qkv_t, dos_t, os_t, lse1, bias, sinks, qb)
    dwqkv = _mm(dqkv_t, x2b, "nn", "swa_qkv_dw").T
    g[("swa_w_q", None)], g[("kv_w_shared", None)] = dwqkv[:, :QH * HD], dwqkv[:, QH * HD:]
    dx2 = _mm(dqkv_t, wqkv_t, "tn", "swa_qkv_dx", addend=dz3, add_scale=ALPHA)
    g["rel_bias"] = jnp.sum(_bias_bwd(dbias, bkt), axis=-1).reshape(NBKT, QH)
    g["swa_sinks"] = jnp.sum(dsk, axis=-1).reshape(1, QH)
    dz2, dz2b, dg_mlp0, db_mlp0 = _ln_bwd(dx2, xh2, r2, lnp("ln_mlp_g", 0), "ln_mlp_0_bwd")
    dx1, g["mlp0"] = _mlp_bwd(dz2, dz2b, x1b, a0, w[("mlp_w_up", 0)], w[("mlp_w_down", 0)], 0)
    dz1, dz1b, dg_mix0, db_mix0 = _ln_bwd(dx1, xh1, r1, lnp("ln_mix_g", 0), "ln_mix_0_bwd")
    do0 = _mm(dz1b, w_o, "nt", "mla_out_dx", out_dtype=BF16)
    g[("mla_w_o", None)] = _mm(o0, dz1b, "tn", "mla_out_dw")
    do_lat, dwuv, delta_t = _o_up_bwd(do0, o_lat, wuv_h)
    g[("mla_w_uv", None)] = dwuv.transpose(1, 0, 2)
    kc_t = kc.reshape(t // bq, bq, KD).transpose(0, 2, 1)
    if dist is None:
        dk, ds_all = _flash_dkv(qcat, kc, do_lat, lse0_t, delta_t, bq, hb)
        dq_cat = _flash_dq(ds_all, kc_t, bq)
    else:
        g["mid"] = _grad_shards(RS_MID, g).astype(BF16)
        dk, ds_all, slots1, slots_mid = _flash_dkv(qcat, kc, do_lat, lse0_t, delta_t, bq, hb,
                                                  exchange=(g["mlp1"], g["mid"]))
        dq_cat, slots0 = _flash_dq(ds_all, kc_t, bq, exchange=g["mlp0"])
        for key, slots in (("mlp1", slots1), ("mid", slots_mid), ("mlp0", slots0)):
            reduced[key] = _reduce_scatter_finish(slots, g[key], dist.pos, key)
    dq2, dwuk = _q_prep_bwd(dq_cat, q2, wuk_h, cos, sin)
    g[("mla_w_uk", None)] = dwuk.transpose(2, 0, 1)
    dcq = _mm(dq2, wq2, "nt", "mla_uq_dx")
    dwq2 = _mm(cq, dq2, "tn", "mla_uq_dw")
    g[("mla_w_uq", None)] = jnp.concatenate([dwq2[:, :H * NOPE].reshape(QR, H, NOPE),
                                             dwq2[:, H * NOPE:].reshape(QR, H, RP)[:, :, :ROPE]], axis=2)
    dh, dgq, dgkv = _mla_pre_bwd(hh, dcq, dk, w["mla_g_q"], w["mla_g_kv"], cos, sin)
    g[("mla_w_in", None)] = _mm(x, dh, "tn", "mla_in_dw")[:, :QR + KVR + ROPE]
    if dist is None:
        grad_x = _mm(dh, w_in, "nt", "mla_in_dx", addend=dz1, add_scale=ALPHA)
    else:
        g["end"] = _grad_shards(RS_END, g).astype(BF16)
        grad_x, slots_end = _mm(dh, w_in, "nt", "mla_in_dx", addend=dz1, add_scale=ALPHA, exchange=g["end"])
        reduced["end"] = _reduce_scatter_finish(slots_end, g["end"], dist.pos, "end")
    g["mla_g_q"], g["mla_g_kv"] = dgq, dgkv
    g["ln_mix_g"] = jnp.concatenate([dg_mix0, dg_mix1], axis=0)
    g["ln_mix_b"] = jnp.concatenate([db_mix0, db_mix1], axis=0)
    g["ln_mlp_g"] = jnp.concatenate([dg_mlp0, dg_mlp1], axis=0)
    g["ln_mlp_b"] = jnp.concatenate([db_mlp0, db_mlp1], axis=0)
    return lpart, grad_x, g, reduced


def _rows(a):
    return a.reshape(-1, D)


def _piece(a, layer):
    return _rows(a if layer is None else a[layer])


def _pack_group(group, parts):
    return jnp.concatenate([_piece(parts[n], l) for n, l in group], axis=0)


def _unpack_group(group, buf, like):
    out, off = {}, 0
    for n, l in group:
        shp = like[n].shape if l is None else like[n].shape[1:]
        out[(n, l)] = buf[off:off + ROWS[n]].reshape(shp)
        off += ROWS[n]
    return out


def _by_name(pieces):
    out = {n: a for (n, l), a in pieces.items() if l is None}
    for n in {n for (n, l) in pieces if l is not None}:
        out[n] = jnp.stack([pieces[(n, 0)], pieces[(n, 1)]])
    return out


def _full_from_gathered(group, wall, shard_shapes):
    out, off = {}, 0
    for n, l in group:
        shp = tuple(shard_shapes[n])
        if n in ("mlp_w_up", "mlp_w_down"):
            out[(n, l)] = (wall, off)
        elif n == "kv_w_shared":
            out[(n, l)] = wall[:, off:off + ROWS[n]].reshape((4 * shp[0],) + shp[1:])
        else:
            out[(n, l)] = wall[:, off:off + ROWS[n]].reshape((4 * shp[1],) + shp[2:])
        off += ROWS[n]
    return out


def _grad_shards(group, g):
    return jnp.concatenate([g[(n, l)].reshape(4, ROWS[n], D) for n, l in group], axis=1)


SMALL = (("ln_mix_g", 0, 2), ("ln_mix_b", 2, 2), ("ln_mlp_g", 4, 2), ("ln_mlp_b", 6, 2),
         ("swa_sinks", 8, 1), ("mla_g_q", 9, 1), ("mla_g_kv", 10, 1), ("rel_bias", 11, 1))
LOSS_ROW = 12


def _pack_small(parts, extra_row=None):
    rows = []
    for n, _, nr in SMALL:
        a = parts[n].reshape(nr, -1).astype(F32)
        rows.append(jnp.pad(a, ((0, 0), (0, D - a.shape[1]))))
    if extra_row is not None:
        rows.append(extra_row)
    rows.append(jnp.zeros((SMALL_ROWS - sum(r.shape[0] for r in rows), D), F32))
    return jnp.concatenate(rows, axis=0)


def _unpack_small(buf, like):
    out = {}
    for n, r0, nr in SMALL:
        size = like[n].size // nr
        out[n] = buf[r0:r0 + nr, :size].reshape(like[n].shape)
    return out


def kernel(x, mla_w_in, mla_g_q, mla_g_kv, mla_w_uq, mla_w_uk, mla_w_uv, mla_w_o, kv_w_shared, swa_w_q, swa_sinks, swa_w_o, rel_bias, mlp_w_up, mlp_w_down, ln_mix_g, ln_mix_b, ln_mlp_g, ln_mlp_b, loss_target, m_mla_w_in, m_mla_g_q, m_mla_g_kv, m_mla_w_uq, m_mla_w_uk, m_mla_w_uv, m_mla_w_o, m_kv_w_shared, m_swa_w_q, m_swa_sinks, m_swa_w_o, m_rel_bias, m_mlp_w_up, m_mlp_w_down, m_ln_mix_g, m_ln_mix_b, m_ln_mlp_g, m_ln_mlp_b, v_mla_w_in, v_mla_g_q, v_mla_g_kv, v_mla_w_uq, v_mla_w_uk, v_mla_w_uv, v_mla_w_o, v_kv_w_shared, v_swa_w_q, v_swa_sinks, v_swa_w_o, v_rel_bias, v_mlp_w_up, v_mlp_w_down, v_ln_mix_g, v_ln_mix_b, v_ln_mlp_g, v_ln_mlp_b):
    names = ["mla_w_in", "mla_g_q", "mla_g_kv", "mla_w_uq", "mla_w_uk", "mla_w_uv", "mla_w_o", "kv_w_shared",
             "swa_w_q", "swa_sinks", "swa_w_o", "rel_bias", "mlp_w_up", "mlp_w_down",
             "ln_mix_g", "ln_mix_b", "ln_mlp_g", "ln_mlp_b"]
    ws = dict(zip(names, [mla_w_in, mla_g_q, mla_g_kv, mla_w_uq, mla_w_uk, mla_w_uv, mla_w_o, kv_w_shared,
                          swa_w_q, swa_sinks, swa_w_o, rel_bias, mlp_w_up, mlp_w_down,
                          ln_mix_g, ln_mix_b, ln_mlp_g, ln_mlp_b]))
    ms = dict(zip(names, [m_mla_w_in, m_mla_g_q, m_mla_g_kv, m_mla_w_uq, m_mla_w_uk, m_mla_w_uv, m_mla_w_o,
                          m_kv_w_shared, m_swa_w_q, m_swa_sinks, m_swa_w_o, m_rel_bias, m_mlp_w_up, m_mlp_w_down,
                          m_ln_mix_g, m_ln_mix_b, m_ln_mlp_g, m_ln_mlp_b]))
    vs = dict(zip(names, [v_mla_w_in, v_mla_g_q, v_mla_g_kv, v_mla_w_uq, v_mla_w_uk, v_mla_w_uv, v_mla_w_o,
                          v_kv_w_shared, v_swa_w_q, v_swa_sinks, v_swa_w_o, v_rel_bias, v_mlp_w_up, v_mlp_w_down,
                          v_ln_mix_g, v_ln_mix_b, v_ln_mlp_g, v_ln_mlp_b]))
    xi, yi, ci = _mesh_pos()
    shard = 2 * xi + yi
    shard_shapes = {n: ws[n].shape for n in ROWS}
    wbf = {n: ws[n].astype(BF16) for n in ROWS}

    gains = jnp.concatenate([mla_g_q.reshape(-1), mla_g_kv.reshape(-1)])
    pieces = []
    for _ in range(3):
        head = lax.reduce_precision(gains, exponent_bits=8, mantissa_bits=7)
        pieces.append(head.astype(BF16))
        gains = gains - head
    ng = (QR + KVR) // 4
    gain_rows = jnp.pad(jnp.concatenate(pieces).reshape(1, 3 * ng), ((0, GAIN_ROWS - 1), (0, D - 3 * ng)))
    early = jnp.concatenate([_pack_group(AG_EARLY, wbf), gain_rows], axis=0)
    wall = lax.dynamic_update_slice(_allgather_weights(early), early[None], (shard, 0, 0))
    w = _full_from_gathered(AG_EARLY, wall, shard_shapes)
    gp = wall[:, early.shape[0] - GAIN_ROWS, :3 * ng].astype(F32).reshape(4, 3, ng)
    gains = (gp[:, 0] + gp[:, 1]) + gp[:, 2]
    w["mla_g_q"], w["mla_g_kv"] = gains[:, :QR // 4].reshape(QR), gains[:, QR // 4:].reshape(KVR)
    dist = _Dist(shard=shard, pos=jnp.stack([shard, ci, 2 * shard + ci]).astype(jnp.int32),
                 late_pack=_pack_group(AG_LATE, wbf), shard_shapes=shard_shapes)
    for n in ("swa_sinks", "rel_bias", "ln_mix_g", "ln_mix_b", "ln_mlp_g", "ln_mlp_b"):
        w[n] = ws[n]

    lpart, grad_x, g, reduced = _fwd_bwd(x[0], loss_target[0], w, dist)
    reduced["rest"] = jnp.concatenate([reduced["mid"], reduced["end"]], axis=0)

    small_like = {n: g[n] for n, _, _ in SMALL}
    small_sum = _allreduce_small(_pack_small(g, extra_row=lpart), "allreduce_small_grads")
    loss = 0.5 * jnp.sum(small_sum[LOSS_ROW]) / D
    gsm = _unpack_small(small_sum, small_like)
    gsm["mla_g_q"] = lax.dynamic_slice(gsm["mla_g_q"], (0, shard * (QR // 4)), (1, QR // 4))
    gsm["mla_g_kv"] = lax.dynamic_slice(gsm["mla_g_kv"], (0, shard * (KVR // 4)), (1, KVR // 4))

    gbig, dbig, mbig, vbig = {}, {}, {}, {}
    for n in ("mlp_w_up", "mlp_w_down"):
        off = 0 if n == "mlp_w_up" else ROWS["mlp_w_up"]
        gbig[n], dbig[n], mbig[n], vbig[n] = _adamw_layers(
            ws[n], ms[n], vs[n], reduced["mlp0"], reduced["mlp1"], off, f"adamw_{n}")
    rest = RS_MID + RS_END
    outs = _adamw(_pack_group(rest, ws), reduced["rest"], _pack_group(rest, ms), _pack_group(rest, vs),
                  "adamw_rest", tm=_row_tile(reduced["rest"].shape[0]))
    for dst, buf in zip((gbig, dbig, mbig, vbig), (reduced["rest"], *outs)):
        dst.update(_by_name(_unpack_group(rest, buf, ws)))
    dsm, msm, vsm = _adamw(_pack_small(ws), _pack_small(gsm), _pack_small(ms), _pack_small(vs), "adamw_small", tm=16)
    grads = {**gbig, **gsm}
    delta = {**dbig, **_unpack_small(dsm, ws)}
    new_m = {**mbig, **_unpack_small(msm, ws)}
    new_v = {**vbig, **_unpack_small(vsm, ws)}
    grads = {n: grads[n].reshape(ws[n].shape) for n in names}
    return (loss, grad_x[None], *[grads[n] for n in names], *[delta[n] for n in names],
            *[new_m[n] for n in names], *[new_v[n] for n in names])
```

```python
import collections
import math

import numpy as np
import jax
import jax.numpy as jnp
from jax import lax
from jax.experimental import pallas as pl
from jax.experimental.pallas import tpu as pltpu

F32 = jnp.float32
BF16 = jnp.bfloat16
MESH = pl.DeviceIdType.MESH

D = 1024
H = 8
NOPE = 128
ROPE = 64
QR = 384
KVR = 256
RP = 128
KD = KVR + RP
HW = 768
QH = 16
KVH = 4
HD = 64
G = QH // KVH
WIN = 128
NBKT = 32
ALPHA = 4.0 ** 0.25
LN_EPS = 1e-5
RMS_EPS = 1e-6
MLA_SCALE = (NOPE + ROPE) ** -0.5
LOG2E = 1.4426950408889634
LN2 = 0.6931471805599453
QSCALE = MLA_SCALE * LOG2E
AHEAD = 1
SWA_SCALE = HD ** -0.5
NEG = -1e30
LR, B1, B2, ADAM_EPS, WD, STEP = 0.001, 0.9, 0.999, 1e-8, 0.01, 10

VMEM_LIMIT = 48 * 1024 * 1024

NN = (((1,), (0,)), ((), ()))
NT = (((1,), (1,)), ((), ()))
TN = (((0,), (0,)), ((), ()))

ROWS = {"mlp_w_up": 1024, "mlp_w_down": 1024, "mla_w_o": 256, "swa_w_q": 256, "swa_w_o": 256,
        "kv_w_shared": 128, "mla_w_in": 176, "mla_w_uq": 144, "mla_w_uk": 64, "mla_w_uv": 64}
AG_EARLY = (("mla_w_in", None), ("mla_w_uq", None), ("mla_w_uk", None), ("mla_w_uv", None), ("mla_w_o", None))
AG_LATE = (("mlp_w_up", 0), ("mlp_w_up", 1), ("mlp_w_down", 0), ("mlp_w_down", 1),
           ("swa_w_q", None), ("swa_w_o", None), ("kv_w_shared", None))
RS_MID = (("mla_w_o", None), ("swa_w_q", None), ("swa_w_o", None), ("kv_w_shared", None), ("mla_w_uv", None))
RS_END = (("mla_w_in", None), ("mla_w_uq", None), ("mla_w_uk", None))
SMALL_ROWS = 16
GAIN_ROWS = 32
_Dist =collections.namedtuple("_Dist", "shard pos late_pack shard_shapes")


def _cp(**kw):
    return pltpu.CompilerParams(vmem_limit_bytes=VMEM_LIMIT, **kw)


def _tile(n, pref):
    t = min(n, pref)
    while n % t:
        t -= 128
    return t


def _dot(a, b, dims):
    return lax.dot_general(a, b, dims, preferred_element_type=F32)


def _mm(a, b, mode, name, out_dtype=F32, out_t=False, addend=None, add_scale=1.0, relu2=False, gate_a=None,
        b_view=None, out_view=None, exchange=None, tm=1024, tn=1024, tk=1024):
    blk = 1024
    if b_view is not None:
        kind, b_off = b_view
        assert b.shape[0] == 4 and b.shape[2] == blk and b_off % blk == 0
        bshape = {("cols", "nn"): (blk, 4 * blk), ("cols", "nt"): (blk, 4 * blk),
                  ("rows", "nn"): (4 * blk, blk), ("rows", "nt"): (4 * blk, blk)}[(kind, mode)]
    else:
        bshape = b.shape
    if mode == "nn":
        (m, k), (k2, n) = a.shape, bshape
    elif mode == "nt":
        (m, k), (n, k2) = a.shape, bshape
    else:
        (k, m), (k2, n) = a.shape, bshape
    assert k == k2, (name, a.shape, b.shape)
    tm, tn, tk = _tile(m, tm), _tile(n, tn), _tile(k, tk)
    nk = k // tk
    dims = {"nn": NN, "nt": NT, "tn": TN}[mode]
    if mode == "tn":
        a_spec = pl.BlockSpec((tk, tm), lambda i, j, kk: (kk, i))
    else:
        a_spec = pl.BlockSpec((tm, tk), lambda i, j, kk: (i, kk))
    if b_view is not None:
        assert tn == blk and tk == blk
        ob = b_off // blk
        b_spec = {("cols", "nn"): pl.BlockSpec((None, tk, tn), lambda i, j, kk: (j, ob, 0)),
                  ("cols", "nt"): pl.BlockSpec((None, tn, tk), lambda i, j, kk: (kk, ob, 0)),
                  ("rows", "nn"): pl.BlockSpec((None, tk, tn), lambda i, j, kk: (kk, ob, 0)),
                  ("rows", "nt"): pl.BlockSpec((None, tn, tk), lambda i, j, kk: (j, ob, 0))}[(kind, mode)]
    elif mode == "nt":
        b_spec = pl.BlockSpec((tn, tk), lambda i, j, kk: (j, kk))
    else:
        b_spec = pl.BlockSpec((tk, tn), lambda i, j, kk: (kk, j))
    mn_spec = pl.BlockSpec((tm, tn), lambda i, j, kk: (i, j))
    ins, in_specs = [a, b], [a_spec, b_spec]
    if addend is not None:
        ins.append(addend)
        in_specs.append(mn_spec)
    if gate_a is not None:
        ins.append(gate_a)
        in_specs.append(mn_spec)
    aliases = {}
    if out_view is not None:
        okind, total_rows, o_off, buf = out_view
        assert not out_t and tm == blk and tn == blk and o_off % blk == 0
        oo = o_off // blk
        out_shape = [jax.ShapeDtypeStruct((4, total_rows, blk), out_dtype)]
        if okind == "cols":
            out_specs = [pl.BlockSpec((None, tm, tn), lambda i, j, kk: (j, oo, 0))]
        else:
            out_specs = [pl.BlockSpec((None, tm, tn), lambda i, j, kk: (i, oo, 0))]
        if buf is not None:
            aliases = {len(ins): 0}
            ins.append(buf)
            in_specs.append(pl.BlockSpec(memory_space=pl.ANY))
    elif out_t:
        out_shape = [jax.ShapeDtypeStruct((n, m), out_dtype)]
        out_specs = [pl.BlockSpec((tn, tm), lambda i, j, kk: (j, i))]
    else:
        out_shape = [jax.ShapeDtypeStruct((m, n), out_dtype)]
        out_specs = [mn_spec]
    has_add, has_gate = addend is not None, gate_a is not None
    hosting = exchange is not None
    scratch = [pltpu.VMEM((tm, tn), F32)] if nk > 1 else []
    if hosting:
        assert not aliases
        ins.append(exchange)
        in_specs.append(pl.BlockSpec(memory_space=pl.ANY))
        out_shape.append(jax.ShapeDtypeStruct((8, exchange.shape[1] // 2, D), exchange.dtype))
        out_specs.append(pl.BlockSpec(memory_space=pl.ANY))
        scratch = scratch + ALL_SEMS
    steps = (m // tm, n // tn, nk)

    def kern(*refs):
        a_ref, b_ref = refs[0], refs[1]
        pos = 2
        add_ref = gate_ref = None
        if has_add:
            add_ref = refs[pos]
            pos += 1
        if has_gate:
            gate_ref = refs[pos]
            pos += 1
        pos += len(aliases)
        if hosting:
            xc_start, xc_finish = _device_exchange_schedule(refs[pos], refs[pos + 2], refs[-2], refs[-1])
            pos += 1
        o_ref = refs[pos]
        acc = refs[pos + 1 + hosting] if nk > 1 else None
        kk = pl.program_id(2)
        if hosting:
            lin = (pl.program_id(0) * steps[1] + pl.program_id(1)) * steps[2] + kk

            @pl.when(lin == 0)
            def _():
                xc_start()

        def partial():
            return _dot(a_ref[...].astype(BF16), b_ref[...].astype(BF16), dims)

        if nk > 1:
            @pl.when(kk == 0)
            def _():
                acc[...] = partial()

            @pl.when((kk > 0) & (kk < nk - 1))
            def _():
                acc[...] += partial()

        @pl.when(kk == nk - 1)
        def _():
            r = partial() + acc[...] if nk > 1 else partial()
            if has_add:
                r = r + add_scale * add_ref[...].astype(F32)
            if has_gate:
                ga = gate_ref[...].astype(F32)
                r = r * jnp.where(ga > 0.0, (2.0 * ga) * lax.rsqrt(ga), 0.0)
            if relu2:
                hh = jnp.maximum(r, 0.0)
                r = hh * hh
            if out_t:
                r = r.T
            o_ref[...] = r.astype(out_dtype)

        if hosting:
            @pl.when(lin == steps[0] * steps[1] * steps[2] - 1)
            def _():
                xc_finish()

    outs = pl.pallas_call(
        kern, out_shape=out_shape, grid=steps, in_specs=in_specs, out_specs=out_specs,
        scratch_shapes=scratch, input_output_aliases=aliases, name=name, compiler_params=_cp())(*ins)
    return outs if hosting else outs[0]


def _add_ln(res, y, g, b, name, res_affine=None, tm=1024):
    t = res.shape[0]
    tm = min(tm, t)
    affine = res_affine is not None

    def kern(*refs):
        if affine:
            x_ref, y_ref, g_ref, b_ref, g0_ref, b0_ref, ob_ref, xh_ref, r_ref = refs
            x = x_ref[...] * g0_ref[...] + b0_ref[...]
        else:
            x_ref, y_ref, g_ref, b_ref, ob_ref, xh_ref, r_ref = refs
            x = x_ref[...]
        z = ALPHA * x + y_ref[...]
        mu = jnp.mean(z, axis=-1, keepdims=True)
        zc = z - mu
        var = jnp.mean(zc * zc, axis=-1, keepdims=True)
        r = lax.rsqrt(var + LN_EPS)
        xh = zc * r
        ob_ref[...] = (xh * g_ref[...] + b_ref[...]).astype(BF16)
        xh_ref[...] = xh
        r_ref[...] = r

    row = pl.BlockSpec((tm, D), lambda i: (i, 0))
    vec = pl.BlockSpec((1, D), lambda i: (0, 0))
    st = pl.BlockSpec((tm, 1), lambda i: (i, 0))
    ins = [res, y, g.reshape(1, D), b.reshape(1, D)]
    if affine:
        ins += [res_affine[0].reshape(1, D), res_affine[1].reshape(1, D)]
    return pl.pallas_call(
        kern, grid=(t // tm,), in_specs=[row, row] + [vec] * (len(ins) - 2), out_specs=[row, row, st],
        out_shape=[jax.ShapeDtypeStruct((t, D), BF16), jax.ShapeDtypeStruct((t, D), F32),
                   jax.ShapeDtypeStruct((t, 1), F32)],
        name=name, compiler_params=_cp())(*ins)


def _ln_bwd(dout, xhat, rstd, g, name, loss_b=None, tm=1024):
    t = dout.shape[0]
    tm = min(tm, t)
    head = loss_b is not None

    def kern(*refs):
        if head:
            do_ref, xh_ref, r_ref, g_ref, b_ref, dz_ref, dzb_ref, dg_ref, db_ref, l_ref = refs
        else:
            do_ref, xh_ref, r_ref, g_ref, dz_ref, dzb_ref, dg_ref, db_ref = refs

        @pl.when(pl.program_id(0) == 0)
        def _():
            dg_ref[...] = jnp.zeros_like(dg_ref)
            db_ref[...] = jnp.zeros_like(db_ref)
            if head:
                l_ref[...] = jnp.zeros_like(l_ref)

        xh = xh_ref[...]
        if head:
            e = xh * g_ref[...] + b_ref[...] - do_ref[...]
            l_ref[...] += jnp.sum(e * e, axis=0, keepdims=True)
            do = e * (1.0 / D)
        else:
            do = do_ref[...]
        dxh = do * g_ref[...]
        m1 = jnp.mean(dxh, axis=-1, keepdims=True)
        m2 = jnp.mean(dxh * xh, axis=-1, keepdims=True)
        dz = r_ref[...] * (dxh - m1 - xh * m2)
        dz_ref[...] = dz
        dzb_ref[...] = dz.astype(BF16)
        dg_ref[...] += jnp.sum(do * xh, axis=0, keepdims=True)
        db_ref[...] += jnp.sum(do, axis=0, keepdims=True)

    row = pl.BlockSpec((tm, D), lambda i: (i, 0))
    vec = pl.BlockSpec((1, D), lambda i: (0, 0))
    st = pl.BlockSpec((tm, 1), lambda i: (i, 0))
    ins = [dout, xhat, rstd, g.reshape(1, D)] + ([loss_b.reshape(1, D)] if head else [])
    return pl.pallas_call(
        kern, grid=(t // tm,), in_specs=[row, row, st] + [vec] * (len(ins) - 3),
        out_specs=[row, row, vec, vec] + ([vec] if head else []),
        out_shape=[jax.ShapeDtypeStruct((t, D), F32), jax.ShapeDtypeStruct((t, D), BF16)]
        + [jax.ShapeDtypeStruct((1, D), F32)] * (3 if head else 2),
        name=name, compiler_params=_cp())(*ins)


def _rope_tables(t):
    half = ROPE // 2
    inv = 10000.0 ** (-jnp.arange(half, dtype=F32) / half)
    ang = jnp.arange(t).astype(F32)[:, None] * inv[None, :]
    cos, sin = jnp.cos(ang), jnp.sin(ang)
    z = jnp.zeros((t, RP - ROPE), F32)
    return jnp.concatenate([cos, cos, z], axis=1), jnp.concatenate([-sin, sin, z], axis=1)


def _swap_halves(x):
    lane = lax.broadcasted_iota(jnp.int32, x.shape, 1)
    return jnp.where(lane < ROPE // 2, pltpu.roll(x, RP - ROPE // 2, 1), pltpu.roll(x, ROPE // 2, 1))


def _rope(x, cos, sin):
    return x * cos + _swap_halves(x) * sin


def _rope_t(gy, cos, sin):
    return gy * cos + _swap_halves(gy * sin)


def _mla_pre(hh, g_q, g_kv, cos, sin, tm=1024):
    t = hh.shape[0]
    tm = min(tm, t)

    def kern(h_ref, gq_ref, gkv_ref, c_ref, s_ref, cq_ref, k_ref):
        xq = h_ref[:, 0:QR]
        rq = lax.rsqrt(jnp.mean(xq * xq, axis=-1, keepdims=True) + RMS_EPS)
        cq_ref[...] = (xq * rq * gq_ref[...]).astype(BF16)
        xk = h_ref[:, QR:QR + KVR]
        rk = lax.rsqrt(jnp.mean(xk * xk, axis=-1, keepdims=True) + RMS_EPS)
        k_ref[:, 0:KVR] = (xk * rk * gkv_ref[...]).astype(BF16)
        k_ref[:, KVR:KD] = _rope(h_ref[:, QR + KVR:HW], c_ref[...], s_ref[...]).astype(BF16)

    return pl.pallas_call(
        kern, grid=(t // tm,),
        in_specs=[pl.BlockSpec((tm, HW), lambda i: (i, 0)), pl.BlockSpec((1, QR), lambda i: (0, 0)),
                  pl.BlockSpec((1, KVR), lambda i: (0, 0)), pl.BlockSpec((tm, RP), lambda i: (i, 0)),
                  pl.BlockSpec((tm, RP), lambda i: (i, 0))],
        out_specs=[pl.BlockSpec((tm, QR), lambda i: (i, 0)), pl.BlockSpec((tm, KD), lambda i: (i, 0))],
        out_shape=[jax.ShapeDtypeStruct((t, QR), BF16), jax.ShapeDtypeStruct((t, KD), BF16)],
        name="mla_pre", compiler_params=_cp())(hh, g_q.reshape(1, QR), g_kv.reshape(1, KVR), cos, sin)


def _mla_pre_bwd(hh, dcq, dk, g_q, g_kv, cos, sin, tm=1024):
    t = hh.shape[0]
    tm = min(tm, t)

    def rms_bwd(x, dy, g):
        r = lax.rsqrt(jnp.mean(x * x, axis=-1, keepdims=True) + RMS_EPS)
        gdy = dy * g
        dx = r * gdy - x * (r * r * r) * jnp.mean(gdy * x, axis=-1, keepdims=True)
        return dx, jnp.sum(dy * x * r, axis=0, keepdims=True)

    def kern(h_ref, dcq_ref, dk_ref, gq_ref, gkv_ref, c_ref, s_ref, dh_ref, dgq_ref, dgkv_ref):
        @pl.when(pl.program_id(0) == 0)
        def _():
            dgq_ref[...] = jnp.zeros_like(dgq_ref)
            dgkv_ref[...] = jnp.zeros_like(dgkv_ref)

        dxq, dgq = rms_bwd(h_ref[:, 0:QR], dcq_ref[...], gq_ref[...])
        dxk, dgk = rms_bwd(h_ref[:, QR:QR + KVR], dk_ref[:, 0:KVR], gkv_ref[...])
        dh_ref[:, 0:QR] = dxq.astype(BF16)
        dh_ref[:, QR:QR + KVR] = dxk.astype(BF16)
        dh_ref[:, QR + KVR:HW] = _rope_t(dk_ref[:, KVR:KD], c_ref[...], s_ref[...]).astype(BF16)
        dgq_ref[...] += dgq
        dgkv_ref[...] += dgk

    return pl.pallas_call(
        kern, grid=(t // tm,),
        in_specs=[pl.BlockSpec((tm, HW), lambda i: (i, 0)), pl.BlockSpec((tm, QR), lambda i: (i, 0)),
                  pl.BlockSpec((tm, KD), lambda i: (i, 0)), pl.BlockSpec((1, QR), lambda i: (0, 0)),
                  pl.BlockSpec((1, KVR), lambda i: (0, 0)), pl.BlockSpec((tm, RP), lambda i: (i, 0)),
                  pl.BlockSpec((tm, RP), lambda i: (i, 0))],
        out_specs=[pl.BlockSpec((tm, HW), lambda i: (i, 0)), pl.BlockSpec((1, QR), lambda i: (0, 0)),
                   pl.BlockSpec((1, KVR), lambda i: (0, 0))],
        out_shape=[jax.ShapeDtypeStruct((t, HW), BF16), jax.ShapeDtypeStruct((1, QR), F32),
                   jax.ShapeDtypeStruct((1, KVR), F32)],
        name="mla_pre_bwd", compiler_params=_cp())(hh, dcq, dk, g_q.reshape(1, QR), g_kv.reshape(1, KVR), cos, sin)


def _q_prep(q2, wuk_t, cos, sin, tm=1024):
    t = q2.shape[0]
    tm = min(tm, t)

    def kern(q_ref, w_ref, c_ref, s_ref, o_ref):
        cos_, sin_ = c_ref[...], s_ref[...]
        for h in range(H):
            qn = q_ref[:, h * NOPE:(h + 1) * NOPE].astype(BF16)
            o_ref[:, h * KD:h * KD + KVR] = (_dot(qn, w_ref[h], NN) * QSCALE).astype(BF16)
            qr = q_ref[:, H * NOPE + h * RP:H * NOPE + (h + 1) * RP]
            o_ref[:, h * KD + KVR:(h + 1) * KD] = (_rope(qr, cos_, sin_) * QSCALE).astype(BF16)

    return pl.pallas_call(
        kern, grid=(t // tm,),
        in_specs=[pl.BlockSpec((tm, 2 * H * NOPE), lambda i: (i, 0)), pl.BlockSpec((H, NOPE, KVR), lambda i: (0, 0, 0)),
                  pl.BlockSpec((tm, RP), lambda i: (i, 0)), pl.BlockSpec((tm, RP), lambda i: (i, 0))],
        out_specs=pl.BlockSpec((tm, H * KD), lambda i: (i, 0)),
        out_shape=jax.ShapeDtypeStruct((t, H * KD), BF16),
        name="q_prep", compiler_params=_cp())(q2, wuk_t, cos, sin)


def _q_prep_bwd(dq_cat, q2, wuk_h, cos, sin, tm=512):
    t = q2.shape[0]
    tm = min(tm, t)

    def kern(dq_ref, q_ref, w_ref, c_ref, s_ref, o_ref, dw_ref):
        @pl.when(pl.program_id(0) == 0)
        def _():
            dw_ref[...] = jnp.zeros_like(dw_ref)

        cos_, sin_ = c_ref[...], s_ref[...]
        for h in range(H):
            dql = dq_ref[:, h * KD:h * KD + KVR].astype(BF16)
            o_ref[:, h * NOPE:(h + 1) * NOPE] = _dot(dql, w_ref[h], NN).astype(BF16)
            dqr = dq_ref[:, h * KD + KVR:(h + 1) * KD]
            o_ref[:, H * NOPE + h * RP:H * NOPE + (h + 1) * RP] = _rope_t(dqr, cos_, sin_).astype(BF16)
            qn = q_ref[:, h * NOPE:(h + 1) * NOPE].astype(BF16)
            dw_ref[h] += _dot(qn, dql, TN)

    return pl.pallas_call(
        kern, grid=(t // tm,),
        in_specs=[pl.BlockSpec((tm, H * KD), lambda i: (i, 0)), pl.BlockSpec((tm, 2 * H * NOPE), lambda i: (i, 0)),
                  pl.BlockSpec((H, KVR, NOPE), lambda i: (0, 0, 0)),
                  pl.BlockSpec((tm, RP), lambda i: (i, 0)), pl.BlockSpec((tm, RP), lambda i: (i, 0))],
        out_specs=[pl.BlockSpec((tm, 2 * H * NOPE), lambda i: (i, 0)), pl.BlockSpec((H, NOPE, KVR), lambda i: (0, 0, 0))],
        out_shape=[jax.ShapeDtypeStruct((t, 2 * H * NOPE), BF16), jax.ShapeDtypeStruct((H, NOPE, KVR), F32)],
        name="q_prep_bwd", compiler_params=_cp())(dq_cat, q2, wuk_h, cos, sin)


def _o_up_bwd(do, o_lat, wuv_h, tm=1024):
    t = do.shape[0]
    tm = min(tm, t)

    def kern(do_ref, x_ref, w_ref, dx_ref, dw_ref, dlt_ref):
        @pl.when(pl.program_id(0) == 0)
        def _():
            dw_ref[...] = jnp.zeros_like(dw_ref)

        for h in range(H):
            dh_ = do_ref[:, h * NOPE:(h + 1) * NOPE]
            x = x_ref[:, h * KVR:(h + 1) * KVR]
            dx = _dot(dh_, w_ref[h], NT)
            dx_ref[:, h * KVR:(h + 1) * KVR] = dx.astype(BF16)
            dw_ref[h] += _dot(x.astype(BF16), dh_, TN)
            dl = jnp.broadcast_to(jnp.sum(dx * x, axis=1)[:, None], (tm, 128))
            dlt_ref[h] = dl.T[0:1, :]

    return pl.pallas_call(
        kern, grid=(t // tm,),
        in_specs=[pl.BlockSpec((tm, H * NOPE), lambda i: (i, 0)), pl.BlockSpec((tm, H * KVR), lambda i: (i, 0)),
                  pl.BlockSpec((H, KVR, NOPE), lambda i: (0, 0, 0))],
        out_specs=[pl.BlockSpec((tm, H * KVR), lambda i: (i, 0)), pl.BlockSpec((H, KVR, NOPE), lambda i: (0, 0, 0)),
                   pl.BlockSpec((H, 1, tm), lambda i: (0, 0, i))],
        out_shape=[jax.ShapeDtypeStruct((t, H * KVR), BF16), jax.ShapeDtypeStruct((H, KVR, NOPE), F32),
                   jax.ShapeDtypeStruct((H, 1, t), F32)],
        name="o_up_bwd", compiler_params=_cp())(do, o_lat, wuv_h)


def _causal_pairs(nq):
    return [(i, j) for i in range(nq) for j in range(i + 1)]


def _lane_tile(stat, width):
    return jnp.tile(stat, (1, width // 128))


def _flash_fwd(qcat, kc, wuv_h, bq, hb, gather=None):
    t = kc.shape[0]
    nq = t // bq
    pairs = _causal_pairs(nq)
    itab = jnp.asarray(np.array([p[0] for p in pairs], np.int32))
    jtab = jnp.asarray(np.array([p[1] for p in pairs], np.int32))

    ng = H // hb
    hosting = gather is not None

    def kern(it, jt, q_ref, k_ref, wuv_ref, *rest):
        if hosting:
            w_ref, o_ref, lset_ref, oup_ref, wall_ref, m_sc, l_sc, acc_sc, send_sems, recv_sems = rest
            ag_start, ag_forward, ag_finish = _allgather_schedule(w_ref, wall_ref, send_sems, recv_sems)
        else:
            o_ref, lset_ref, oup_ref, m_sc, l_sc, acc_sc = rest
        grp = pl.program_id(0)
        st = pl.program_id(1)
        i, j = it[st], jt[st]

        if hosting:
            @pl.when((grp == 0) & (st == 0))
            def _():
                ag_start()

        @pl.when(j == 0)
        def _():
            m_sc[...] = jnp.full_like(m_sc, NEG)
            l_sc[...] = jnp.zeros_like(l_sc)
            acc_sc[...] = jnp.zeros_like(acc_sc)

        def update(masked):
            k = k_ref[...]
            v = k[:, 0:KVR]
            if masked:
                row = lax.broadcasted_iota(jnp.int32, (bq, bq), 0)
                col = lax.broadcasted_iota(jnp.int32, (bq, bq), 1)
                keep = col <= row
            pending = [_dot(q_ref[:, hh * KD:(hh + 1) * KD], k, NT) for hh in range(min(AHEAD, hb))]
            for hh in range(hb):
                s = pending.pop(0)
                if hh + AHEAD < hb:
                    pending.append(_dot(q_ref[:, (hh + AHEAD) * KD:(hh + AHEAD + 1) * KD], k, NT))
                if masked:
                    s = jnp.where(keep, s, NEG)
                m_prev = m_sc[hh]
                m_next = jnp.maximum(m_prev, jnp.max(s, axis=1)[:, None])
                p = jnp.exp2(s - _lane_tile(m_next, bq))
                a = jnp.exp2(m_prev - m_next)
                l_sc[hh] = a * l_sc[hh] + jnp.sum(p, axis=1)[:, None]
                acc_sc[hh] = _lane_tile(a, KVR) * acc_sc[hh] + _dot(p.astype(BF16), v, NN)
                m_sc[hh] = m_next

        @pl.when(j < i)
        def _():
            update(False)

        @pl.when(j == i)
        def _():
            update(True)
            for hh in range(hb):
                l = l_sc[hh]
                o_h = acc_sc[hh] / _lane_tile(l, KVR)
                o_ref[:, hh * KVR:(hh + 1) * KVR] = o_h
                oup_ref[:, hh * NOPE:(hh + 1) * NOPE] = _dot(o_h.astype(BF16), wuv_ref[hh], NN).astype(BF16)
                lset_ref[hh] = (m_sc[hh] + jnp.log2(l)).T[0:1, :]

        if hosting:
            half_way = (ng * len(pairs)) // 2

            @pl.when(grp * len(pairs) + st == half_way)
            def _():
                ag_forward()

            @pl.when((grp == ng - 1) & (st == len(pairs) - 1))
            def _():
                ag_finish()

    in_specs = [pl.BlockSpec((bq, hb * KD), lambda g, s, it, jt: (it[s], g)),
                pl.BlockSpec((bq, KD), lambda g, s, it, jt: (jt[s], 0)),
                pl.BlockSpec((hb, KVR, NOPE), lambda g, s, it, jt: (g, 0, 0))]
    out_specs = [pl.BlockSpec((bq, hb * KVR), lambda g, s, it, jt: (it[s], g)),
                 pl.BlockSpec((hb, 1, bq), lambda g, s, it, jt: (g, 0, it[s])),
                 pl.BlockSpec((bq, hb * NOPE), lambda g, s, it, jt: (it[s], g))]
    out_shape = [jax.ShapeDtypeStruct((t, H * KVR), F32), jax.ShapeDtypeStruct((H, 1, t), F32),
                 jax.ShapeDtypeStruct((t, H * NOPE), BF16)]
    scratch = [pltpu.VMEM((hb, bq, 128), F32), pltpu.VMEM((hb, bq, 128), F32), pltpu.VMEM((hb, bq, KVR), F32)]
    args = [itab, jtab, qcat, kc, wuv_h]
    if hosting:
        in_specs.append(ANY)
        out_specs.append(ANY)
        out_shape.append(jax.ShapeDtypeStruct((4,) + gather.shape, gather.dtype))
        scratch += AG_SEMS
        args.append(gather)
    gs = pltpu.PrefetchScalarGridSpec(num_scalar_prefetch=2, grid=(ng, len(pairs)), in_specs=in_specs,
                                      out_specs=out_specs, scratch_shapes=scratch)
    return pl.pallas_call(kern, grid_spec=gs, out_shape=out_shape, name="mla_flash_fwd",
                          compiler_params=_cp())(*args)


def _flash_dkv(qcat, kc, do_lat, lse_t, delta_t, bq, hb, exchange=()):
    nx = len(exchange)
    t = kc.shape[0]
    nq = t // bq
    ng = H // hb
    npairs = nq * (nq + 1) // 2
    steps = [(j, g, i) for j in range(nq) for g in range(ng) for i in range(j, nq)]
    jtab = jnp.asarray(np.array([s[0] for s in steps], np.int32))
    gtab = jnp.asarray(np.array([s[1] for s in steps], np.int32))
    itab = jnp.asarray(np.array([s[2] for s in steps], np.int32))
    ptab = jnp.asarray(np.array([s[2] * (s[2] + 1) // 2 + s[0] for s in steps], np.int32))

    def kern(jt, gt, it, pt, q_ref, k_ref, do_ref, lset_ref, dlt_ref, *rest):
        p_refs, (dk_ref, ds_ref), slots_refs = rest[:nx], rest[nx:nx + 2], rest[nx + 2:2 * nx + 2]
        dk_sc, dv_sc = rest[2 * nx + 2:2 * nx + 4]
        sems = rest[2 * nx + 4:]
        hooks = [_device_exchange_schedule(p_refs[e], slots_refs[e], sems[2 * e], sems[2 * e + 1]) for e in range(nx)]
        st = pl.program_id(0)
        j, g, i = jt[st], gt[st], it[st]

        if nx:
            @pl.when(st == 0)
            def _():
                for start, _ in hooks:
                    start()

        @pl.when((g == 0) & (i == j))
        def _():
            dk_sc[...] = jnp.zeros_like(dk_sc)
            dv_sc[...] = jnp.zeros_like(dv_sc)

        def update(masked):
            k = k_ref[...]
            v = k[:, 0:KVR]
            if masked:
                row = lax.broadcasted_iota(jnp.int32, (bq, bq), 0)
                col = lax.broadcasted_iota(jnp.int32, (bq, bq), 1)
                keep = row <= col

            def first_matmuls(hh):
                dob = do_ref[:, hh * KVR:(hh + 1) * KVR].astype(BF16)
                return _dot(k, q_ref[:, hh * KD:(hh + 1) * KD], NT), _dot(v, dob, NT), dob

            pending = [first_matmuls(hh) for hh in range(min(AHEAD, hb))]
            for hh in range(hb):
                s, dp, dob = pending.pop(0)
                if hh + AHEAD < hb:
                    pending.append(first_matmuls(hh + AHEAD))
                if masked:
                    s = jnp.where(keep, s, NEG)
                p = jnp.exp2(s - lset_ref[hh])
                dv_sc[...] += _dot(p.astype(BF16), dob, NN)
                dsb = (p * (dp - dlt_ref[hh])).astype(BF16)
                ds_ref[0, 0, hh] = dsb
                dk_sc[...] += _dot(dsb, q_ref[:, hh * KD:(hh + 1) * KD], NN)

        @pl.when(i > j)
        def _():
            update(False)

        @pl.when(i == j)
        def _():
            update(True)

        @pl.when((g == ng - 1) & (i == nq - 1))
        def _():
            dk_ref[:, 0:KVR] = dk_sc[:, 0:KVR] * LN2 + dv_sc[...]
            dk_ref[:, KVR:KD] = dk_sc[:, KVR:KD] * LN2

        if nx:
            @pl.when(st == len(steps) - 1)
            def _():
                for _, finish in hooks:
                    finish()

    in_specs = [pl.BlockSpec((bq, hb * KD), lambda s, jt, gt, it, pt: (it[s], gt[s])),
                pl.BlockSpec((bq, KD), lambda s, jt, gt, it, pt: (jt[s], 0)),
                pl.BlockSpec((bq, hb * KVR), lambda s, jt, gt, it, pt: (it[s], gt[s])),
                pl.BlockSpec((hb, 1, bq), lambda s, jt, gt, it, pt: (gt[s], 0, it[s])),
                pl.BlockSpec((hb, 1, bq), lambda s, jt, gt, it, pt: (gt[s], 0, it[s]))] + [ANY] * nx
    out_specs = [pl.BlockSpec((bq, KD), lambda s, jt, gt, it, pt: (jt[s], 0)),
                 pl.BlockSpec((1, 1, hb, bq, bq), lambda s, jt, gt, it, pt: (gt[s], pt[s], 0, 0, 0))] + [ANY] * nx
    out_shape = [jax.ShapeDtypeStruct((t, KD), F32), jax.ShapeDtypeStruct((ng, npairs, hb, bq, bq), BF16)]
    out_shape += [jax.ShapeDtypeStruct((8, e.shape[1] // 2, D), e.dtype) for e in exchange]
    scratch = [pltpu.VMEM((bq, KD), F32), pltpu.VMEM((bq, KVR), F32)] + ALL_SEMS * nx
    args = [jtab, gtab, itab, ptab, qcat, kc, do_lat, lse_t, delta_t, *exchange]
    gs = pltpu.PrefetchScalarGridSpec(num_scalar_prefetch=4, grid=(len(steps),), in_specs=in_specs,
                                      out_specs=out_specs, scratch_shapes=scratch)
    return pl.pallas_call(kern, grid_spec=gs, out_shape=out_shape, name="mla_flash_dkv",
                          compiler_params=_cp())(*args)


def _flash_dq(ds_all, kc_t, bq, exchange=None):
    nq = kc_t.shape[0]
    t = nq * bq
    ngrp, _, hper = ds_all.shape[:3]
    pairs = _causal_pairs(nq)
    itab = jnp.asarray(np.array([p[0] for p in pairs], np.int32))
    jtab = jnp.asarray(np.array([p[1] for p in pairs], np.int32))
    hosting = exchange is not None

    def kern(it, jt, *refs):
        ds_refs, kt_ref, rest = refs[:ngrp], refs[ngrp], refs[ngrp + 1:]
        if hosting:
            p_ref, dq_ref, slots_ref, acc_sc, send_sems, recv_sems = rest
            xc_start, xc_finish = _device_exchange_schedule(p_ref, slots_ref, send_sems, recv_sems)
        else:
            dq_ref, acc_sc = rest
        st = pl.program_id(0)
        i, j = it[st], jt[st]
        kt = kt_ref[...]

        def ds(hh):
            return ds_refs[hh // hper][0, 0, hh % hper]

        if hosting:
            @pl.when(st == 0)
            def _():
                xc_start()

        @pl.when(j == 0)
        def _():
            for hh in range(H):
                acc_sc[hh] = _dot(kt, ds(hh), NN)

        @pl.when((j > 0) & (j < i))
        def _():
            for hh in range(H):
                acc_sc[hh] += _dot(kt, ds(hh), NN)

        @pl.when(j == i)
        def _():
            for hh in range(H):
                tot = _dot(kt, ds(hh), NN)
                tot = jnp.where(i > 0, tot + acc_sc[hh], tot)
                dq_ref[:, hh * KD:(hh + 1) * KD] = tot.T * MLA_SCALE

        if hosting:
            @pl.when(st == len(pairs) - 1)
            def _():
                xc_finish()

    def group(gi):
        return pl.BlockSpec((1, 1, hper, bq, bq), lambda s, it, jt: (gi, s, 0, 0, 0))

    in_specs = [group(gi) for gi in range(ngrp)] + [pl.BlockSpec((None, KD, bq), lambda s, it, jt: (jt[s], 0, 0))]
    out_specs = [pl.BlockSpec((bq, H * KD), lambda s, it, jt: (it[s], 0))]
    out_shape = [jax.ShapeDtypeStruct((t, H * KD), F32)]
    scratch = [pltpu.VMEM((H, KD, bq), F32)]
    args = [itab, jtab] + [ds_all] * ngrp + [kc_t]
    if hosting:
        in_specs.append(ANY)
        out_specs.append(ANY)
        out_shape.append(jax.ShapeDtypeStruct((8, exchange.shape[1] // 2, D), exchange.dtype))
        scratch += ALL_SEMS
        args.append(exchange)
    gs = pltpu.PrefetchScalarGridSpec(num_scalar_prefetch=2, grid=(len(pairs),), in_specs=in_specs,
                                      out_specs=out_specs, scratch_shapes=scratch)
    outs = pl.pallas_call(kern, grid_spec=gs, out_shape=out_shape, name="mla_flash_dq",
                          compiler_params=_cp())(*args)
    return outs if hosting else outs[0]


def _bucket_table():
    d = np.arange(WIN)
    max_exact = NBKT // 2
    nf = np.maximum(d, 1).astype(np.float32)
    large = max_exact + (np.log(nf / np.float32(max_exact)) / np.float32(math.log(WIN / max_exact))
                         * np.float32(NBKT - max_exact)).astype(np.int32)
    large = np.minimum(large, NBKT - 1)
    bucket = np.where(d < max_exact, d, large).astype(np.int32)
    jj = np.arange(2 * WIN)[:, None]
    ii = np.arange(WIN)[None, :]
    dist = ii + WIN - jj
    valid = (dist >= 0) & (dist < WIN)
    return np.where(valid, bucket[np.clip(dist, 0, WIN - 1)], -1).astype(np.int32)


def _bias_build(rel_bias, bkt):
    def kern(bk_ref, rb_ref, o_ref):
        bk = bk_ref[...]
        for hd in range(QH):
            acc = jnp.full((2 * WIN, WIN), NEG, F32)
            for b in range(NBKT):
                acc = jnp.where(bk == b, rb_ref[b, hd], acc)
            o_ref[hd] = acc

    return pl.pallas_call(
        kern, in_specs=[pl.BlockSpec(memory_space=pltpu.VMEM), pl.BlockSpec(memory_space=pltpu.SMEM)],
        out_specs=pl.BlockSpec(memory_space=pltpu.VMEM),
        out_shape=jax.ShapeDtypeStruct((QH, 2 * WIN, WIN), F32), name="swa_bias_build")(bkt, rel_bias)


def _bias_bwd(dbias, bkt):
    def kern(db_ref, bk_ref, o_ref):
        bk = bk_ref[...]
        for hd in range(QH):
            g = db_ref[hd]
            for b in range(NBKT):
                r = b * QH + hd
                o_ref[r:r + 1, :] = jnp.sum(jnp.where(bk == b, g, 0.0), axis=0, keepdims=True)

    return pl.pallas_call(
        kern, in_specs=[pl.BlockSpec(memory_space=pltpu.VMEM), pl.BlockSpec(memory_space=pltpu.VMEM)],
        out_specs=pl.BlockSpec(memory_space=pltpu.VMEM),
        out_shape=jax.ShapeDtypeStruct((NBKT * QH, WIN), F32), name="swa_bias_bwd")(dbias, bkt)


def _swa_finish_scores(raw, bias, first):
    s = raw * SWA_SCALE + bias
    if first is not None:
        row = lax.broadcasted_iota(jnp.int32, s.shape, 0)
        s = jnp.where(jnp.logical_or(jnp.logical_not(first), row >= WIN), s, NEG)
    return s


def _swa_fwd(qkv_t, bias, sinks, qb):
    t = qkv_t.shape[1]
    w = qb * WIN
    nst = t // w

    def kern(q_ref, kc_ref, kp_ref, vc_ref, vp_ref, b_ref, sk_ref, o_ref, lse_ref):
        n = pl.program_id(0)
        kfull = jnp.concatenate([kp_ref[...], kc_ref[...]], axis=1)
        vfull = jnp.concatenate([vp_ref[...], vc_ref[...]], axis=1)
        head_row = lax.broadcasted_iota(jnp.int32, (QH, WIN), 0)
        groups = [(b, kh) for b in range(qb) for kh in range(KVH)]

        def raw_scores(b, kh):
            k_band = kfull[kh * HD:(kh + 1) * HD, b * WIN:(b + 2) * WIN]
            return [_dot(k_band, q_ref[(kh * G + g) * HD:(kh * G + g + 1) * HD, b * WIN:(b + 1) * WIN], TN)
                    for g in range(G)]

        o_rows = [[] for _ in range(qb)]
        lse_tiles = [jnp.zeros((QH, WIN), F32) for _ in range(qb)]
        pending = [raw_scores(*grp) for grp in groups[:AHEAD]]
        for gi, (b, kh) in enumerate(groups):
            scores = pending.pop(0)
            if gi + AHEAD < len(groups):
                pending.append(raw_scores(*groups[gi + AHEAD]))
            v_band = vfull[kh * HD:(kh + 1) * HD, b * WIN:(b + 2) * WIN]
            for g in range(G):
                hd = kh * G + g
                s = _swa_finish_scores(scores[g], b_ref[hd], (n == 0) if b == 0 else None)
                sink = sk_ref[hd]
                m = jnp.maximum(jnp.max(s, axis=0, keepdims=True), sink)
                p = jnp.exp(s - m)
                den = jnp.sum(p, axis=0, keepdims=True) + jnp.exp(sink - m)
                p = p / den
                o_rows[b].append(_dot(v_band, p.astype(BF16), NN))
                lse_tiles[b] = jnp.where(head_row == hd, m + jnp.log(den), lse_tiles[b])
        o_ref[...] = jnp.concatenate([jnp.concatenate(rows, axis=0) for rows in o_rows], axis=1)
        lse_ref[...] = jnp.concatenate(lse_tiles, axis=1)

    prev = lambda r: (lambda n: (r, jnp.maximum(n * qb - 1, 0)))
    return pl.pallas_call(
        kern, grid=(nst,),
        in_specs=[pl.BlockSpec((QH * HD, w), lambda n: (0, n)),
                  pl.BlockSpec((KVH * HD, w), lambda n: (4, n)), pl.BlockSpec((KVH * HD, WIN), prev(4)),
                  pl.BlockSpec((KVH * HD, w), lambda n: (5, n)), pl.BlockSpec((KVH * HD, WIN), prev(5)),
                  pl.BlockSpec((QH, 2 * WIN, WIN), lambda n: (0, 0, 0)),
                  pl.BlockSpec(memory_space=pltpu.SMEM)],
        out_specs=[pl.BlockSpec((QH * HD, w), lambda n: (0, n)), pl.BlockSpec((QH, w), lambda n: (0, n))],
        out_shape=[jax.ShapeDtypeStruct((QH * HD, t), F32), jax.ShapeDtypeStruct((QH, t), F32)],
        name="swa_fwd", compiler_params=_cp())(qkv_t, qkv_t, qkv_t, qkv_t, qkv_t, bias, sinks)


def _swa_bwd(qkv_t, do_t, o_t, lse, bias, sinks, qb):
    t = qkv_t.shape[1]
    w = qb * WIN
    nst = t // w
    nblk = t // WIN

    def kern(q_ref, kc_ref, kp_ref, vc_ref, vp_ref, do_ref, o_ref, lse_ref, qn_ref, don_ref, on_ref, lsen_ref,
             b_ref, sk_ref, dqkv_ref, db_ref, dsk_ref):
        n = pl.program_id(0)

        @pl.when(n == 0)
        def _():
            db_ref[...] = jnp.zeros_like(db_ref)
            dsk_ref[...] = jnp.zeros_like(dsk_ref)

        kfull = jnp.concatenate([kp_ref[...], kc_ref[...]], axis=1)
        vfull = jnp.concatenate([vp_ref[...], vc_ref[...]], axis=1)
        head_row = lax.broadcasted_iota(jnp.int32, (QH, WIN), 0)
        db_acc = [None] * QH
        dsk_tile = jnp.zeros((QH, WIN), F32)
        prev_part = [[[None] * qb for _ in range(KVH)] for _ in range(2)]
        cur_part = [[[None] * qb for _ in range(KVH)] for _ in range(2)]
        groups = [(b, kh) for b in range(qb) for kh in range(KVH)]

        def first_matmuls(b, kh):
            k_band = kfull[kh * HD:(kh + 1) * HD, b * WIN:(b + 2) * WIN]
            v_band = vfull[kh * HD:(kh + 1) * HD, b * WIN:(b + 2) * WIN]
            out = []
            for g in range(G):
                rs = slice((kh * G + g) * HD, (kh * G + g + 1) * HD)
                dob = do_ref[rs, b * WIN:(b + 1) * WIN].astype(BF16)
                out.append((_dot(k_band, q_ref[rs, b * WIN:(b + 1) * WIN], TN), _dot(v_band, dob, TN), dob))
            return out

        dq_rows = [[] for _ in range(qb)]
        pending = [first_matmuls(*grp) for grp in groups[:AHEAD]]
        for gi, (b, kh) in enumerate(groups):
            first = pending.pop(0)
            if gi + AHEAD < len(groups):
                pending.append(first_matmuls(*groups[gi + AHEAD]))
            cs = slice(b * WIN, (b + 1) * WIN)
            k_band = kfull[kh * HD:(kh + 1) * HD, b * WIN:(b + 2) * WIN]
            dk_b = dv_b = None
            for g in range(G):
                hd = kh * G + g
                rs = slice(hd * HD, (hd + 1) * HD)
                raw, dp, dob = first[g]
                lse_h = lse_ref[hd:hd + 1, cs]
                s = _swa_finish_scores(raw, b_ref[hd], (n == 0) if b == 0 else None)
                p = jnp.exp(s - lse_h)
                dl = jnp.sum(do_ref[rs, cs] * o_ref[rs, cs], axis=0, keepdims=True)
                ds = p * (dp - dl)
                db_acc[hd] = ds if db_acc[hd] is None else db_acc[hd] + ds
                dsk_tile = jnp.where(head_row == hd, dsk_tile - jnp.exp(sk_ref[hd] - lse_h) * dl, dsk_tile)
                dss = (ds * SWA_SCALE).astype(BF16)
                dq_rows[b].append(_dot(k_band, dss, NN).astype(BF16))
                dk_h = _dot(q_ref[rs, cs], dss, NT)
                dv_h = _dot(dob, p.astype(BF16), NT)
                dk_b = dk_h if dk_b is None else dk_b + dk_h
                dv_b = dv_h if dv_b is None else dv_b + dv_h
            for which, val in ((0, dk_b), (1, dv_b)):
                prev_part[which][kh][b] = val[:, 0:WIN]
                cur_part[which][kh][b] = val[:, WIN:2 * WIN]
        dq_cols = [jnp.concatenate(rows, axis=0) for rows in dq_rows]

        live = n < nst - 1
        ls = slice((qb - 1) * WIN, qb * WIN)
        halo = [[None] * KVH for _ in range(2)]
        for kh in range(KVH):
            k_last = kc_ref[kh * HD:(kh + 1) * HD, ls]
            v_last = vc_ref[kh * HD:(kh + 1) * HD, ls]
            dk_b = dv_b = None
            for g in range(G):
                hd = kh * G + g
                rs = slice(hd * HD, (hd + 1) * HD)
                q_t = qn_ref[rs, :]
                do = don_ref[rs, :]
                s = _dot(k_last, q_t, TN) * SWA_SCALE + b_ref[hd, 0:WIN, :]
                p = jnp.exp(s - lsen_ref[hd:hd + 1, :])
                dob = do.astype(BF16)
                dp = _dot(v_last, dob, TN)
                dl = jnp.sum(do * on_ref[rs, :], axis=0, keepdims=True)
                dss = (p * (dp - dl) * SWA_SCALE).astype(BF16)
                dk_h = _dot(q_t, dss, NT)
                dv_h = _dot(dob, p.astype(BF16), NT)
                dk_b = dk_h if dk_b is None else dk_b + dk_h
                dv_b = dv_h if dv_b is None else dv_b + dv_h
            halo[0][kh] = jnp.where(live, dk_b, 0.0)
            halo[1][kh] = jnp.where(live, dv_b, 0.0)

        kv_rows = []
        for which in range(2):
            for kh in range(KVH):
                blocks = [cur_part[which][kh][p] + (prev_part[which][kh][p + 1] if p + 1 < qb else halo[which][kh])
                          for p in range(qb)]
                kv_rows.append(jnp.concatenate(blocks, axis=1))
        dqkv_ref[...] = jnp.concatenate(
            [jnp.concatenate(dq_cols, axis=1), jnp.concatenate(kv_rows, axis=0).astype(BF16)], axis=0)
        db_ref[...] += jnp.stack(db_acc)
        dsk_ref[...] += dsk_tile

    prev = lambda r: (lambda n: (r, jnp.maximum(n * qb - 1, 0)))
    nxt = lambda n: (0, jnp.minimum((n + 1) * qb, nblk - 1))
    big = lambda: pl.BlockSpec((QH * HD, w), lambda n: (0, n))
    return pl.pallas_call(
        kern, grid=(nst,),
        in_specs=[big(),
                  pl.BlockSpec((KVH * HD, w), lambda n: (4, n)), pl.BlockSpec((KVH * HD, WIN), prev(4)),
                  pl.BlockSpec((KVH * HD, w), lambda n: (5, n)), pl.BlockSpec((KVH * HD, WIN), prev(5)),
                  big(), big(), pl.BlockSpec((QH, w), lambda n: (0, n)),
                  pl.BlockSpec((QH * HD, WIN), nxt), pl.BlockSpec((QH * HD, WIN), nxt),
                  pl.BlockSpec((QH * HD, WIN), nxt), pl.BlockSpec((QH, WIN), nxt),
                  pl.BlockSpec((QH, 2 * WIN, WIN), lambda n: (0, 0, 0)),
                  pl.BlockSpec(memory_space=pltpu.SMEM)],
        out_specs=[pl.BlockSpec(((QH + 2 * KVH) * HD, w), lambda n: (0, n)),
                   pl.BlockSpec((QH, 2 * WIN, WIN), lambda n: (0, 0, 0)),
                   pl.BlockSpec((QH, WIN), lambda n: (0, 0))],
        out_shape=[jax.ShapeDtypeStruct(((QH + 2 * KVH) * HD, t), BF16),
                   jax.ShapeDtypeStruct((QH, 2 * WIN, WIN), F32), jax.ShapeDtypeStruct((QH, WIN), F32)],
        name="swa_bwd", compiler_params=_cp())(
            qkv_t, qkv_t, qkv_t, qkv_t, qkv_t, do_t, o_t, lse, qkv_t, do_t, o_t, lse, bias, sinks)


def _adamw_math(w, g, m, v):
    nm = B1 * m + (1.0 - B1) * g
    nv = B2 * v + (1.0 - B2) * (g * g)
    mhat = nm * (1.0 / (1.0 - B1 ** STEP))
    vhat = nv * (1.0 / (1.0 - B2 ** STEP))
    return -LR * (mhat / (jnp.sqrt(vhat) + ADAM_EPS) + WD * w), nm, nv


def _adamw_layers(w, m, v, g0buf, g1buf, off, name, tm=512):
    rows = w.shape[1]
    nb, ob = rows // tm, off // tm

    def kern(w_ref, m_ref, v_ref, g0_ref, g1_ref, gr_ref, d_ref, nm_ref, nv_ref):
        g_ = jnp.where(pl.program_id(0) == 0, g0_ref[...], g1_ref[...])
        gr_ref[...] = g_
        d_ref[...], nm_ref[...], nv_ref[...] = _adamw_math(w_ref[...], g_, m_ref[...], v_ref[...])

    lay = pl.BlockSpec((None, tm, D), lambda l, i: (l, i, 0))
    gsp = pl.BlockSpec((tm, D), lambda l, i: (ob + i, 0))
    return pl.pallas_call(
        kern, grid=(2, nb), in_specs=[lay, lay, lay, gsp, gsp], out_specs=[lay] * 4,
        out_shape=[jax.ShapeDtypeStruct(w.shape, F32)] * 4, name=name, compiler_params=_cp())(w, m, v, g0buf, g1buf)


def _adamw(w, g, m, v, name, tm=544):
    r = w.shape[0]
    tm = r if r % tm else tm

    def kern(w_ref, g_ref, m_ref, v_ref, d_ref, nm_ref, nv_ref):
        d_ref[...], nm_ref[...], nv_ref[...] = _adamw_math(w_ref[...], g_ref[...], m_ref[...], v_ref[...])

    row = pl.BlockSpec((tm, D), lambda i: (i, 0))
    sds = jax.ShapeDtypeStruct((r, D), F32)
    return pl.pallas_call(kern, grid=(r // tm,), in_specs=[row] * 4, out_specs=[row] * 3, out_shape=[sds] * 3,
                          name=name, compiler_params=_cp())(w, g, m, v)


def _mesh_pos():
    return lax.axis_index("x"), lax.axis_index("y"), lax.axis_index("c")


ANY = pl.BlockSpec(memory_space=pl.ANY)


AG_SEMS = [pltpu.SemaphoreType.DMA((6,)), pltpu.SemaphoreType.DMA((6,))]


def _allgather_schedule(w_ref, out_ref, send_sems, recv_sems):
    half = w_ref.shape[0] // 2
    x, y, c = _mesh_pos()
    me, sibling = (x, y, c), (x, y, 1 - c)
    chips = [(1 - x, y), (x, 1 - y), (1 - x, 1 - y)]

    def rows(px, py, pc):
        return out_ref.at[2 * px + py, pl.ds(pc * half, half), :]

    def copy(k, block, to, src=None):
        return pltpu.make_async_remote_copy(
            src_ref=rows(*block) if src is None else src, dst_ref=rows(*block),
            send_sem=send_sems.at[k], recv_sem=recv_sems.at[k], device_id=to, device_id_type=MESH)

    def first():
        return [copy(j, me, (*chip, c), src=w_ref.at[pl.ds(c * half, half), :]) for j, chip in enumerate(chips)]

    def passed():
        return [copy(3 + j, (*chip, c), sibling) for j, chip in enumerate(chips)]

    def start():
        for cp in first():
            cp.start()

    def forward():
        for j, chip in enumerate(chips):
            copy(j, (*chip, c), me).wait_recv()
            passed()[j].start()

    def finish():
        for j, chip in enumerate(chips):
            copy(3 + j, (*chip, 1 - c), me).wait_recv()
        for cp in first() + passed():
            cp.wait_send()

    return start, forward, finish


def _allgather_weights(wpack):
    def body(w_ref, out_ref, send_sems, recv_sems):
        start, forward, finish = _allgather_schedule(w_ref, out_ref, send_sems, recv_sems)
        start()
        forward()
        finish()

    return pl.pallas_call(
        body, out_shape=jax.ShapeDtypeStruct((4,) + wpack.shape, wpack.dtype), in_specs=[ANY], out_specs=ANY,
        scratch_shapes=AG_SEMS, name="allgather_weights")(wpack)


def _row_tile(rows):
    t = min(rows, 512)
    while rows % t or t % 16:
        t -= 16
    return t


ALL_SEMS = [pltpu.SemaphoreType.DMA((7,)), pltpu.SemaphoreType.DMA((7,))]


def _device_exchange_schedule(g_ref, out_ref, send_sems, recv_sems):
    half = g_ref.shape[1] // 2
    x, y, c = _mesh_pos()
    me = 4 * x + 2 * y + c
    peers = [(x ^ (k >> 2), y ^ ((k >> 1) & 1), c ^ (k & 1)) for k in range(1, 8)]

    def sends():
        return [pltpu.make_async_remote_copy(
            src_ref=g_ref.at[2 * px + py, pl.ds(pc * half, half), :], dst_ref=out_ref.at[me],
            send_sem=send_sems.at[j], recv_sem=recv_sems.at[j], device_id=(px, py, pc), device_id_type=MESH)
            for j, (px, py, pc) in enumerate(peers)]

    def start():
        for cp in sends():
            cp.start()

    def finish():
        for j, (px, py, pc) in enumerate(peers):
            pltpu.make_async_remote_copy(
                src_ref=out_ref.at[me], dst_ref=out_ref.at[4 * px + 2 * py + pc], send_sem=send_sems.at[j],
                recv_sem=recv_sems.at[j], device_id=(px, py, pc), device_id_type=MESH).wait_recv()
        for cp in sends():
            cp.wait_send()

    return start, finish


def _sum_devices(slots, g, pos, tag):
    half = slots.shape[1]
    tm = _row_tile(half)
    nb = half // tm

    def kern(pos_ref, own_ref, *refs):
        acc = own_ref[0].astype(F32)
        for s_ref in refs[:7]:
            acc = acc + s_ref[0].astype(F32)
        refs[7][...] = acc

    def slot(k):
        return pl.BlockSpec((1, tm, D), lambda i, pos: (jnp.bitwise_xor(pos[2], k), i, 0))

    gs = pltpu.PrefetchScalarGridSpec(
        num_scalar_prefetch=1, grid=(nb,),
        in_specs=[pl.BlockSpec((1, tm, D), lambda i, pos: (pos[0], pos[1] * nb + i, 0))] + [slot(k) for k in range(1, 8)],
        out_specs=pl.BlockSpec((tm, D), lambda i, pos: (pos[1] * nb + i, 0)))
    return pl.pallas_call(kern, grid_spec=gs, out_shape=jax.ShapeDtypeStruct((2 * half, D), F32),
                          name=f"rs_sum_devices_{tag}", compiler_params=_cp())(pos, g, *([slots] * 7))


def _reduce_scatter_finish(slots, g, pos, tag):
    return _join_core_halves(_sum_devices(slots, g, pos, tag), tag)


def _join_core_halves(r, tag):
    half = r.shape[0] // 2

    def body(r_ref, out_ref, send_sem, recv_sem):
        x, y, c = _mesh_pos()
        mine = out_ref.at[pl.ds(c * half, half), :]
        cp = pltpu.make_async_remote_copy(
            src_ref=mine, dst_ref=mine, send_sem=send_sem, recv_sem=recv_sem,
            device_id=(x, y, 1 - c), device_id_type=MESH)
        cp.start()
        theirs = out_ref.at[pl.ds((1 - c) * half, half), :]
        pltpu.make_async_remote_copy(
            src_ref=theirs, dst_ref=theirs, send_sem=send_sem, recv_sem=recv_sem,
            device_id=(x, y, 1 - c), device_id_type=MESH).wait_recv()
        cp.wait_send()

    return pl.pallas_call(
        body, out_shape=jax.ShapeDtypeStruct(r.shape, r.dtype), in_specs=[ANY], out_specs=ANY,
        input_output_aliases={0: 0},
        scratch_shapes=[pltpu.SemaphoreType.DMA, pltpu.SemaphoreType.DMA],
        name=f"rs_join_cores_{tag}")(r)


def _allreduce_small(v, name):
    def body(v_ref, out_ref, gat, send_sems, recv_sems):
        x, y, c = _mesh_pos()
        me = 4 * x + 2 * y + c
        gat[me] = v_ref[...]
        sends = []
        for k in range(1, 8):
            peer = (x ^ (k >> 2), y ^ ((k >> 1) & 1), c ^ (k & 1))
            cp = pltpu.make_async_remote_copy(
                src_ref=v_ref, dst_ref=gat.at[me], send_sem=send_sems.at[k - 1], recv_sem=recv_sems.at[k - 1],
                device_id=peer, device_id_type=MESH)
            cp.start()
            sends.append(cp)
        for k in range(1, 8):
            px, py, pc = x ^ (k >> 2), y ^ ((k >> 1) & 1), c ^ (k & 1)
            pltpu.make_async_remote_copy(
                src_ref=v_ref, dst_ref=gat.at[4 * px + 2 * py + pc], send_sem=send_sems.at[k - 1],
                recv_sem=recv_sems.at[k - 1], device_id=(px, py, pc), device_id_type=MESH).wait_recv()
        for cp in sends:
            cp.wait_send()
        acc = gat[0]
        for d in range(1, 8):
            acc = acc + gat[d]
        out_ref[...] = acc

    return pl.pallas_call(
        body, out_shape=jax.ShapeDtypeStruct(v.shape, F32),
        in_specs=[pl.BlockSpec(memory_space=pltpu.VMEM)], out_specs=pl.BlockSpec(memory_space=pltpu.VMEM),
        scratch_shapes=[pltpu.VMEM((8,) + v.shape, F32), pltpu.SemaphoreType.DMA((7,)), pltpu.SemaphoreType.DMA((7,))],
        name=name)(v)


def _mlp_fwd(xb, w_up, w_down, tag):
    a = _mm(xb, w_up[0], "nn", f"mlp_up_{tag}", out_dtype=BF16, relu2=True, b_view=("cols", w_up[1]))
    return a, _mm(a, w_down[0], "nn", f"mlp_down_{tag}", b_view=("rows", w_down[1]), tm=2048)


def _mlp_bwd(dz, dzb, xb, a, w_up, w_down, tag):
    du = _mm(dzb, w_down[0], "nt", f"mlp_down_dx_{tag}", out_dtype=BF16, gate_a=a, b_view=("rows", w_down[1]),
             tm=2048)
    gsh = _mm(xb, du, "tn", f"mlp_up_dw_{tag}", out_dtype=BF16, out_view=("cols", 2 * ROWS["mlp_w_up"], 0, None))
    gsh = _mm(a, dzb, "tn", f"mlp_down_dw_{tag}", out_dtype=BF16,
              out_view=("rows", 2 * ROWS["mlp_w_up"], ROWS["mlp_w_up"], gsh))
    dx = _mm(du, w_up[0], "nt", f"mlp_up_dx_{tag}", addend=dz, add_scale=ALPHA, b_view=("cols", w_up[1]))
    return dx, gsh


def _fwd_bwd(x, target, w, dist=None, bq=512, qb=8, hb=8):
    t = x.shape[0]
    bq = min(bq, t)
    qb = min(qb, t // WIN)
    cos, sin = _rope_tables(t)
    bkt = jnp.asarray(_bucket_table())
    w_in = jnp.pad(w[("mla_w_in", None)], ((0, 0), (0, HW - (QR + KVR + ROPE))))
    wuq = w[("mla_w_uq", None)]
    wq2 = jnp.concatenate([wuq[:, :, :NOPE].reshape(QR, H * NOPE),
                           jnp.pad(wuq[:, :, NOPE:], ((0, 0), (0, 0), (0, RP - ROPE))).reshape(QR, H * RP)], axis=1)
    wuk_t = w[("mla_w_uk", None)].transpose(1, 2, 0)
    wuk_h = w[("mla_w_uk", None)].transpose(1, 0, 2)
    wuv_h = w[("mla_w_uv", None)].transpose(1, 0, 2)
    w_o = w[("mla_w_o", None)]
    sinks = w["swa_sinks"].reshape(QH)
    lnp = lambda n, l: w[n][l]
    reduced = {}

    hh = _mm(x, w_in, "nn", "mla_in")
    cq, kc = _mla_pre(hh, w["mla_g_q"], w["mla_g_kv"], cos, sin)
    q2 = _mm(cq, wq2, "nn", "mla_uq")
    qcat = _q_prep(q2, wuk_t, cos, sin)
    if dist is None:
        o_lat, lse0_t, o0 = _flash_fwd(qcat, kc, wuv_h, bq, hb)
    else:
        o_lat, lse0_t, o0, wall = _flash_fwd(qcat, kc, wuv_h, bq, hb, gather=dist.late_pack)
        wall = lax.dynamic_update_slice(wall, dist.late_pack[None], (dist.shard, 0, 0))
        w = {**w, **_full_from_gathered(AG_LATE, wall, dist.shard_shapes)}
    wqkv = jnp.concatenate([w[("swa_w_q", None)], w[("kv_w_shared", None)]], axis=1)
    wqkv_t = wqkv.T
    wo_s = w[("swa_w_o", None)]
    y0 = _mm(o0, w_o, "nn", "mla_out")
    x1b, xh1, r1 = _add_ln(x, y0, lnp("ln_mix_g", 0), lnp("ln_mix_b", 0), "ln_mix_0")
    a0, f0 = _mlp_fwd(x1b, w[("mlp_w_up", 0)], w[("mlp_w_down", 0)], 0)
    x2b, xh2, r2 = _add_ln(xh1, f0, lnp("ln_mlp_g", 0), lnp("ln_mlp_b", 0), "ln_mlp_0",
                           res_affine=(lnp("ln_mix_g", 0), lnp("ln_mix_b", 0)))
    bias = _bias_build(w["rel_bias"], bkt)
    qkv_t = _mm(x2b, wqkv, "nn", "swa_qkv", out_dtype=BF16, out_t=True)
    os_t, lse1 = _swa_fwd(qkv_t, bias, sinks, qb)
    y1 = _mm(os_t, wo_s, "tn", "swa_out")
    x3b, xh3, r3 = _add_ln(xh2, y1, lnp("ln_mix_g", 1), lnp("ln_mix_b", 1), "ln_mix_1",
                           res_affine=(lnp("ln_mlp_g", 0), lnp("ln_mlp_b", 0)))
    a1, f1 = _mlp_fwd(x3b, w[("mlp_w_up", 1)], w[("mlp_w_down", 1)], 1)
    _, xh4, r4 = _add_ln(xh3, f1, lnp("ln_mlp_g", 1), lnp("ln_mlp_b", 1), "ln_mlp_1",
                         res_affine=(lnp("ln_mix_g", 1), lnp("ln_mix_b", 1)))

    g = {}
    dz4, dz4b, dg_mlp1, db_mlp1, lpart = _ln_bwd(target, xh4, r4, lnp("ln_mlp_g", 1), "ln_mlp_1_bwd",
                                                 loss_b=lnp("ln_mlp_b", 1))
    dx3, g["mlp1"] = _mlp_bwd(dz4, dz4b, x3b, a1, w[("mlp_w_up", 1)], w[("mlp_w_down", 1)], 1)
    dz3, dz3b, dg_mix1, db_mix1 = _ln_bwd(dx3, xh3, r3, lnp("ln_mix_g", 1), "ln_mix_1_bwd")
    dos_t = _mm(dz3b, wo_s, "nt", "swa_out_dx", out_t=True)
    g[("swa_w_o", None)] = _mm(os_t, dz3b, "nn", "swa_out_dw")
    dqkv_t, dbias, dsk = _swa_bwd(qkv_t, dos_t, os_t, lse1, bias, sinks, qb)
    dwqkv = _mm(dqkv_t, x2b, "nn", "swa_qkv_dw").T
    g[("swa_w_q", None)], g[("kv_w_shared", None)] = dwqkv[:, :QH * HD], dwqkv[:, QH * HD:]
    dx2 = _mm(dqkv_t, wqkv_t, "tn", "swa_qkv_dx", addend=dz3, add_scale=ALPHA)
    g["rel_bias"] = jnp.sum(_bias_bwd(dbias, bkt), axis=-1).reshape(NBKT, QH)
    g["swa_sinks"] = jnp.sum(dsk, axis=-1).reshape(1, QH)
    dz2, dz2b, dg_mlp0, db_mlp0 = _ln_bwd(dx2, xh2, r2, lnp("ln_mlp_g", 0), "ln_mlp_0_bwd")
    dx1, g["mlp0"] = _mlp_bwd(dz2, dz2b, x1b, a0, w[("mlp_w_up", 0)], w[("mlp_w_down", 0)], 0)
    dz1, dz1b, dg_mix0, db_mix0 = _ln_bwd(dx1, xh1, r1, lnp("ln_mix_g", 0), "ln_mix_0_bwd")
    do0 = _mm(dz1b, w_o, "nt", "mla_out_dx", out_dtype=BF16)
    g[("mla_w_o", None)] = _mm(o0, dz1b, "tn", "mla_out_dw")
    do_lat, dwuv, delta_t = _o_up_bwd(do0, o_lat, wuv_h)
    g[("mla_w_uv", None)] = dwuv.transpose(1, 0, 2)
    kc_t = kc.reshape(t // bq, bq, KD).transpose(0, 2, 1)
    if dist is None:
        dk, ds_all = _flash_dkv(qcat, kc, do_lat, lse0_t, delta_t, bq, hb)
        dq_cat = _flash_dq(ds_all, kc_t, bq)
    else:
        g["mid"] = _grad_shards(RS_MID, g).astype(BF16)
        dk, ds_all, slots1, slots_mid = _flash_dkv(qcat, kc, do_lat, lse0_t, delta_t, bq, hb,
                                                  exchange=(g["mlp1"], g["mid"]))
        dq_cat, slots0 = _flash_dq(ds_all, kc_t, bq, exchange=g["mlp0"])
        for key, slots in (("mlp1", slots1), ("mid", slots_mid), ("mlp0", slots0)):
            reduced[key] = _reduce_scatter_finish(slots, g[key], dist.pos, key)
    dq2, dwuk = _q_prep_bwd(dq_cat, q2, wuk_h, cos, sin)
    g[("mla_w_uk", None)] = dwuk.transpose(2, 0, 1)
    dcq = _mm(dq2, wq2, "nt", "mla_uq_dx")
    dwq2 = _mm(cq, dq2, "tn", "mla_uq_dw")
    g[("mla_w_uq", None)] = jnp.concatenate([dwq2[:, :H * NOPE].reshape(QR, H, NOPE),
                                             dwq2[:, H * NOPE:].reshape(QR, H, RP)[:, :, :ROPE]], axis=2)
    dh, dgq, dgkv = _mla_pre_bwd(hh, dcq, dk, w["mla_g_q"], w["mla_g_kv"], cos, sin)
    g[("mla_w_in", None)] = _mm(x, dh, "tn", "mla_in_dw")[:, :QR + KVR + ROPE]
    if dist is None:
        grad_x = _mm(dh, w_in, "nt", "mla_in_dx", addend=dz1, add_scale=ALPHA)
    else:
        g["end"] = _grad_shards(RS_END, g).astype(BF16)
        grad_x, slots_end = _mm(dh, w_in, "nt", "mla_in_dx", addend=dz1, add_scale=ALPHA, exchange=g["end"])
        reduced["end"] = _reduce_scatter_finish(slots_end, g["end"], dist.pos, "end")
    g["mla_g_q"], g["mla_g_kv"] = dgq, dgkv
    g["ln_mix_g"] = jnp.concatenate([dg_mix0, dg_mix1], axis=0)
    g["ln_mix_b"] = jnp.concatenate([db_mix0, db_mix1], axis=0)
    g["ln_mlp_g"] = jnp.concatenate([dg_mlp0, dg_mlp1], axis=0)
    g["ln_mlp_b"] = jnp.concatenate([db_mlp0, db_mlp1], axis=0)
    return lpart, grad_x, g, reduced


def _rows(a):
    return a.reshape(-1, D)


def _piece(a, layer):
    return _rows(a if layer is None else a[layer])


def _pack_group(group, parts):
    return jnp.concatenate([_piece(parts[n], l) for n, l in group], axis=0)


def _unpack_group(group, buf, like):
    out, off = {}, 0
    for n, l in group:
        shp = like[n].shape if l is None else like[n].shape[1:]
        out[(n, l)] = buf[off:off + ROWS[n]].reshape(shp)
        off += ROWS[n]
    return out


def _by_name(pieces):
    out = {n: a for (n, l), a in pieces.items() if l is None}
    for n in {n for (n, l) in pieces if l is not None}:
        out[n] = jnp.stack([pieces[(n, 0)], pieces[(n, 1)]])
    return out


def _full_from_gathered(group, wall, shard_shapes):
    out, off = {}, 0
    for n, l in group:
        shp = tuple(shard_shapes[n])
        if n in ("mlp_w_up", "mlp_w_down"):
            out[(n, l)] = (wall, off)
        elif n == "kv_w_shared":
            out[(n, l)] = wall[:, off:off + ROWS[n]].reshape((4 * shp[0],) + shp[1:])
        else:
            out[(n, l)] = wall[:, off:off + ROWS[n]].reshape((4 * shp[1],) + shp[2:])
        off += ROWS[n]
    return out


def _grad_shards(group, g):
    return jnp.concatenate([g[(n, l)].reshape(4, ROWS[n], D) for n, l in group], axis=1)


SMALL = (("ln_mix_g", 0, 2), ("ln_mix_b", 2, 2), ("ln_mlp_g", 4, 2), ("ln_mlp_b", 6, 2),
         ("swa_sinks", 8, 1), ("mla_g_q", 9, 1), ("mla_g_kv", 10, 1), ("rel_bias", 11, 1))
LOSS_ROW = 12


def _pack_small(parts, extra_row=None):
    rows = []
    for n, _, nr in SMALL:
        a = parts[n].reshape(nr, -1).astype(F32)
        rows.append(jnp.pad(a, ((0, 0), (0, D - a.shape[1]))))
    if extra_row is not None:
        rows.append(extra_row)
    rows.append(jnp.zeros((SMALL_ROWS - sum(r.shape[0] for r in rows), D), F32))
    return jnp.concatenate(rows, axis=0)


def _unpack_small(buf, like):
    out = {}
    for n, r0, nr in SMALL:
        size = like[n].size // nr
        out[n] = buf[r0:r0 + nr, :size].reshape(like[n].shape)
    return out


def kernel(x, mla_w_in, mla_g_q, mla_g_kv, mla_w_uq, mla_w_uk, mla_w_uv, mla_w_o, kv_w_shared, swa_w_q, swa_sinks, swa_w_o, rel_bias, mlp_w_up, mlp_w_down, ln_mix_g, ln_mix_b, ln_mlp_g, ln_mlp_b, loss_target, m_mla_w_in, m_mla_g_q, m_mla_g_kv, m_mla_w_uq, m_mla_w_uk, m_mla_w_uv, m_mla_w_o, m_kv_w_shared, m_swa_w_q, m_swa_sinks, m_swa_w_o, m_rel_bias, m_mlp_w_up, m_mlp_w_down, m_ln_mix_g, m_ln_mix_b, m_ln_mlp_g, m_ln_mlp_b, v_mla_w_in, v_mla_g_q, v_mla_g_kv, v_mla_w_uq, v_mla_w_uk, v_mla_w_uv, v_mla_w_o, v_kv_w_shared, v_swa_w_q, v_swa_sinks, v_swa_w_o, v_rel_bias, v_mlp_w_up, v_mlp_w_down, v_ln_mix_g, v_ln_mix_b, v_ln_mlp_g, v_ln_mlp_b):
    names = ["mla_w_in", "mla_g_q", "mla_g_kv", "mla_w_uq", "mla_w_uk", "mla_w_uv", "mla_w_o", "kv_w_shared",
             "swa_w_q", "swa_sinks", "swa_w_o", "rel_bias", "mlp_w_up", "mlp_w_down",
             "ln_mix_g", "ln_mix_b", "ln_mlp_g", "ln_mlp_b"]
    ws = dict(zip(names, [mla_w_in, mla_g_q, mla_g_kv, mla_w_uq, mla_w_uk, mla_w_uv, mla_w_o, kv_w_shared,
                          swa_w_q, swa_sinks, swa_w_o, rel_bias, mlp_w_up, mlp_w_down,
                          ln_mix_g, ln_mix_b, ln_mlp_g, ln_mlp_b]))
    ms = dict(zip(names, [m_mla_w_in, m_mla_g_q, m_mla_g_kv, m_mla_w_uq, m_mla_w_uk, m_mla_w_uv, m_mla_w_o,
                          m_kv_w_shared, m_swa_w_q, m_swa_sinks, m_swa_w_o, m_rel_bias, m_mlp_w_up, m_mlp_w_down,
                          m_ln_mix_g, m_ln_mix_b, m_ln_mlp_g, m_ln_mlp_b]))
    vs = dict(zip(names, [v_mla_w_in, v_mla_g_q, v_mla_g_kv, v_mla_w_uq, v_mla_w_uk, v_mla_w_uv, v_mla_w_o,
                          v_kv_w_shared, v_swa_w_q, v_swa_sinks, v_swa_w_o, v_rel_bias, v_mlp_w_up, v_mlp_w_down,
                          v_ln_mix_g, v_ln_mix_b, v_ln_mlp_g, v_ln_mlp_b]))
    xi, yi, ci = _mesh_pos()
    shard = 2 * xi + yi
    shard_shapes = {n: ws[n].shape for n in ROWS}
    wbf = {n: ws[n].astype(BF16) for n in ROWS}

    gains = jnp.concatenate([mla_g_q.reshape(-1), mla_g_kv.reshape(-1)])
    pieces = []
    for _ in range(3):
        head = lax.reduce_precision(gains, exponent_bits=8, mantissa_bits=7)
        pieces.append(head.astype(BF16))
        gains = gains - head
    ng = (QR + KVR) // 4
    gain_rows = jnp.pad(jnp.concatenate(pieces).reshape(1, 3 * ng), ((0, GAIN_ROWS - 1), (0, D - 3 * ng)))
    early = jnp.concatenate([_pack_group(AG_EARLY, wbf), gain_rows], axis=0)
    wall = lax.dynamic_update_slice(_allgather_weights(early), early[None], (shard, 0, 0))
    w = _full_from_gathered(AG_EARLY, wall, shard_shapes)
    gp = wall[:, early.shape[0] - GAIN_ROWS, :3 * ng].astype(F32).reshape(4, 3, ng)
    gains = (gp[:, 0] + gp[:, 1]) + gp[:, 2]
    w["mla_g_q"], w["mla_g_kv"] = gains[:, :QR // 4].reshape(QR), gains[:, QR // 4:].reshape(KVR)
    dist = _Dist(shard=shard, pos=jnp.stack([shard, ci, 2 * shard + ci]).astype(jnp.int32),
                 late_pack=_pack_group(AG_LATE, wbf), shard_shapes=shard_shapes)
    for n in ("swa_sinks", "rel_bias", "ln_mix_g", "ln_mix_b", "ln_mlp_g", "ln_mlp_b"):
        w[n] = ws[n]

    lpart, grad_x, g, reduced = _fwd_bwd(x[0], loss_target[0], w, dist)
    reduced["rest"] = jnp.concatenate([reduced["mid"], reduced["end"]], axis=0)

    small_like = {n: g[n] for n, _, _ in SMALL}
    small_sum = _allreduce_small(_pack_small(g, extra_row=lpart), "allreduce_small_grads")
    loss = 0.5 * jnp.sum(small_sum[LOSS_ROW]) / D
    gsm = _unpack_small(small_sum, small_like)
    gsm["mla_g_q"] = lax.dynamic_slice(gsm["mla_g_q"], (0, shard * (QR // 4)), (1, QR // 4))
    gsm["mla_g_kv"] = lax.dynamic_slice(gsm["mla_g_kv"], (0, shard * (KVR // 4)), (1, KVR // 4))

    gbig, dbig, mbig, vbig = {}, {}, {}, {}
    for n in ("mlp_w_up", "mlp_w_down"):
        off = 0 if n == "mlp_w_up" else ROWS["mlp_w_up"]
        gbig[n], dbig[n], mbig[n], vbig[n] = _adamw_layers(
            ws[n], ms[n], vs[n], reduced["mlp0"], reduced["mlp1"], off, f"adamw_{n}")
    rest = RS_MID + RS_END
    outs = _adamw(_pack_group(rest, ws), reduced["rest"], _pack_group(rest, ms), _pack_group(rest, vs),
                  "adamw_rest", tm=_row_tile(reduced["rest"].shape[0]))
    for dst, buf in zip((gbig, dbig, mbig, vbig), (reduced["rest"], *outs)):
        dst.update(_by_name(_unpack_group(rest, buf, ws)))
    dsm, msm, vsm = _adamw(_pack_small(ws), _pack_small(gsm), _pack_small(ms), _pack_small(vs), "adamw_small", tm=16)
    grads = {**gbig, **gsm}
    delta = {**dbig, **_unpack_small(dsm, ws)}
    new_m = {**mbig, **_unpack_small(msm, ws)}
    new_v = {**vbig, **_unpack_small(vsm, ws)}
    grads = {n: grads[n].reshape(ws[n].shape) for n in names}
    return (loss, grad_x[None], *[grads[n] for n in names], *[delta[n] for n in names],
            *[new_m[n] for n in names], *[new_v[n] for n in names])
```

```python
import collections
import math

import numpy as np
import jax
import jax.numpy as jnp
from jax import lax
from jax.experimental import pallas as pl
from jax.experimental.pallas import tpu as pltpu

F32 = jnp.float32
BF16 = jnp.bfloat16
MESH = pl.DeviceIdType.MESH

D = 1024
H = 8
NOPE = 128
ROPE = 64
QR = 384
KVR = 256
RP = 128
KD = KVR + RP
HW = 768
QH = 16
KVH = 4
HD = 64
G = QH // KVH
WIN = 128
NBKT = 32
ALPHA = 4.0 ** 0.25
LN_EPS = 1e-5
RMS_EPS = 1e-6
MLA_SCALE = (NOPE + ROPE) ** -0.5
LOG2E = 1.4426950408889634
LN2 = 0.6931471805599453
QSCALE = MLA_SCALE * LOG2E
AHEAD = 1
SWA_SCALE = HD ** -0.5
NEG = -1e30
LR, B1, B2, ADAM_EPS, WD, STEP = 0.001, 0.9, 0.999, 1e-8, 0.01, 10

VMEM_LIMIT = 48 * 1024 * 1024

NN = (((1,), (0,)), ((), ()))
NT = (((1,), (1,)), ((), ()))
TN = (((0,), (0,)), ((), ()))

ROWS = {"mlp_w_up": 1024, "mlp_w_down": 1024, "mla_w_o": 256, "swa_w_q": 256, "swa_w_o": 256,
        "kv_w_shared": 128, "mla_w_in": 176, "mla_w_uq": 144, "mla_w_uk": 64, "mla_w_uv": 64}
AG_EARLY = (("mla_w_in", None), ("mla_w_uq", None), ("mla_w_uk", None), ("mla_w_uv", None), ("mla_w_o", None))
AG_LATE = (("mlp_w_up", 0), ("mlp_w_up", 1), ("mlp_w_down", 0), ("mlp_w_down", 1),
           ("swa_w_q", None), ("swa_w_o", None), ("kv_w_shared", None))
RS_MID = (("mla_w_o", None), ("swa_w_q", None), ("swa_w_o", None), ("kv_w_shared", None), ("mla_w_uv", None))
RS_END = (("mla_w_in", None), ("mla_w_uq", None), ("mla_w_uk", None))
SMALL_ROWS = 16
GAIN_ROWS = 32
_Dist =collections.namedtuple("_Dist", "shard pos late_pack shard_shapes")


def _cp(**kw):
    return pltpu.CompilerParams(vmem_limit_bytes=VMEM_LIMIT, **kw)


def _tile(n, pref):
    t = min(n, pref)
    while n % t:
        t -= 128
    return t


def _dot(a, b, dims):
    return lax.dot_general(a, b, dims, preferred_element_type=F32)


def _mm(a, b, mode, name, out_dtype=F32, out_t=False, addend=None, add_scale=1.0, relu2=False, gate_a=None,
        b_view=None, out_view=None, exchange=None, tm=1024, tn=1024, tk=1024):
    blk = 1024
    if b_view is not None:
        kind, b_off = b_view
        assert b.shape[0] == 4 and b.shape[2] == blk and b_off % blk == 0
        bshape = {("cols", "nn"): (blk, 4 * blk), ("cols", "nt"): (blk, 4 * blk),
                  ("rows", "nn"): (4 * blk, blk), ("rows", "nt"): (4 * blk, blk)}[(kind, mode)]
    else:
        bshape = b.shape
    if mode == "nn":
        (m, k), (k2, n) = a.shape, bshape
    elif mode == "nt":
        (m, k), (n, k2) = a.shape, bshape
    else:
        (k, m), (k2, n) = a.shape, bshape
    assert k == k2, (name, a.shape, b.shape)
    tm, tn, tk = _tile(m, tm), _tile(n, tn), _tile(k, tk)
    nk = k // tk
    dims = {"nn": NN, "nt": NT, "tn": TN}[mode]
    if mode == "tn":
        a_spec = pl.BlockSpec((tk, tm), lambda i, j, kk: (kk, i))
    else:
        a_spec = pl.BlockSpec((tm, tk), lambda i, j, kk: (i, kk))
    if b_view is not None:
        assert tn == blk and tk == blk
        ob = b_off // blk
        b_spec = {("cols", "nn"): pl.BlockSpec((None, tk, tn), lambda i, j, kk: (j, ob, 0)),
                  ("cols", "nt"): pl.BlockSpec((None, tn, tk), lambda i, j, kk: (kk, ob, 0)),
                  ("rows", "nn"): pl.BlockSpec((None, tk, tn), lambda i, j, kk: (kk, ob, 0)),
                  ("rows", "nt"): pl.BlockSpec((None, tn, tk), lambda i, j, kk: (j, ob, 0))}[(kind, mode)]
    elif mode == "nt":
        b_spec = pl.BlockSpec((tn, tk), lambda i, j, kk: (j, kk))
    else:
        b_spec = pl.BlockSpec((tk, tn), lambda i, j, kk: (kk, j))
    mn_spec = pl.BlockSpec((tm, tn), lambda i, j, kk: (i, j))
    ins, in_specs = [a, b], [a_spec, b_spec]
    if addend is not None:
        ins.append(addend)
        in_specs.append(mn_spec)
    if gate_a is not None:
        ins.append(gate_a)
        in_specs.append(mn_spec)
    aliases = {}
    if out_view is not None:
        okind, total_rows, o_off, buf = out_view
        assert not out_t and tm == blk and tn == blk and o_off % blk == 0
        oo = o_off // blk
        out_shape = [jax.ShapeDtypeStruct((4, total_rows, blk), out_dtype)]
        if okind == "cols":
            out_specs = [pl.BlockSpec((None, tm, tn), lambda i, j, kk: (j, oo, 0))]
        else:
            out_specs = [pl.BlockSpec((None, tm, tn), lambda i, j, kk: (i, oo, 0))]
        if buf is not None:
            aliases = {len(ins): 0}
            ins.append(buf)
            in_specs.append(pl.BlockSpec(memory_space=pl.ANY))
    elif out_t:
        out_shape = [jax.ShapeDtypeStruct((n, m), out_dtype)]
        out_specs = [pl.BlockSpec((tn, tm), lambda i, j, kk: (j, i))]
    else:
        out_shape = [jax.ShapeDtypeStruct((m, n), out_dtype)]
        out_specs = [mn_spec]
    has_add, has_gate = addend is not None, gate_a is not None
    hosting = exchange is not None
    scratch = [pltpu.VMEM((tm, tn), F32)] if nk > 1 else []
    if hosting:
        assert not aliases
        ins.append(exchange)
        in_specs.append(pl.BlockSpec(memory_space=pl.ANY))
        out_shape.append(jax.ShapeDtypeStruct((8, exchange.shape[1] // 2, D), exchange.dtype))
        out_specs.append(pl.BlockSpec(memory_space=pl.ANY))
        scratch = scratch + ALL_SEMS
    steps = (m // tm, n // tn, nk)

    def kern(*refs):
        a_ref, b_ref = refs[0], refs[1]
        pos = 2
        add_ref = gate_ref = None
        if has_add:
            add_ref = refs[pos]
            pos += 1
        if has_gate:
            gate_ref = refs[pos]
            pos += 1
        pos += len(aliases)
        if hosting:
            xc_start, xc_finish = _device_exchange_schedule(refs[pos], refs[pos + 2], refs[-2], refs[-1])
            pos += 1
        o_ref = refs[pos]
        acc = refs[pos + 1 + hosting] if nk > 1 else None
        kk = pl.program_id(2)
        if hosting:
            lin = (pl.program_id(0) * steps[1] + pl.program_id(1)) * steps[2] + kk

            @pl.when(lin == 0)
            def _():
                xc_start()

        def partial():
            return _dot(a_ref[...].astype(BF16), b_ref[...].astype(BF16), dims)

        if nk > 1:
            @pl.when(kk == 0)
            def _():
                acc[...] = partial()

            @pl.when((kk > 0) & (kk < nk - 1))
            def _():
                acc[...] += partial()

        @pl.when(kk == nk - 1)
        def _():
            r = partial() + acc[...] if nk > 1 else partial()
            if has_add:
                r = r + add_scale * add_ref[...].astype(F32)
            if has_gate:
                ga = gate_ref[...].astype(F32)
                r = r * jnp.where(ga > 0.0, (2.0 * ga) * lax.rsqrt(ga), 0.0)
            if relu2:
                hh = jnp.maximum(r, 0.0)
                r = hh * hh
            if out_t:
                r = r.T
            o_ref[...] = r.astype(out_dtype)

        if hosting:
            @pl.when(lin == steps[0] * steps[1] * steps[2] - 1)
            def _():
                xc_finish()

    outs = pl.pallas_call(
        kern, out_shape=out_shape, grid=steps, in_specs=in_specs, out_specs=out_specs,
        scratch_shapes=scratch, input_output_aliases=aliases, name=name, compiler_params=_cp())(*ins)
    return outs if hosting else outs[0]


def _add_ln(res, y, g, b, name, res_affine=None, tm=1024):
    t = res.shape[0]
    tm = min(tm, t)
    affine = res_affine is not None

    def kern(*refs):
        if affine:
            x_ref, y_ref, g_ref, b_ref, g0_ref, b0_ref, ob_ref, xh_ref, r_ref = refs
            x = x_ref[...] * g0_ref[...] + b0_ref[...]
        else:
            x_ref, y_ref, g_ref, b_ref, ob_ref, xh_ref, r_ref = refs
            x = x_ref[...]
        z = ALPHA * x + y_ref[...]
        mu = jnp.mean(z, axis=-1, keepdims=True)
        zc = z - mu
        var = jnp.mean(zc * zc, axis=-1, keepdims=True)
        r = lax.rsqrt(var + LN_EPS)
        xh = zc * r
        ob_ref[...] = (xh * g_ref[...] + b_ref[...]).astype(BF16)
        xh_ref[...] = xh
        r_ref[...] = r

    row = pl.BlockSpec((tm, D), lambda i: (i, 0))
    vec = pl.BlockSpec((1, D), lambda i: (0, 0))
    st = pl.BlockSpec((tm, 1), lambda i: (i, 0))
    ins = [res, y, g.reshape(1, D), b.reshape(1, D)]
    if affine:
        ins += [res_affine[0].reshape(1, D), res_affine[1].reshape(1, D)]
    return pl.pallas_call(
        kern, grid=(t // tm,), in_specs=[row, row] + [vec] * (len(ins) - 2), out_specs=[row, row, st],
        out_shape=[jax.ShapeDtypeStruct((t, D), BF16), jax.ShapeDtypeStruct((t, D), F32),
                   jax.ShapeDtypeStruct((t, 1), F32)],
        name=name, compiler_params=_cp())(*ins)


def _ln_bwd(dout, xhat, rstd, g, name, loss_b=None, tm=1024):
    t = dout.shape[0]
    tm = min(tm, t)
    head = loss_b is not None

    def kern(*refs):
        if head:
            do_ref, xh_ref, r_ref, g_ref, b_ref, dz_ref, dzb_ref, dg_ref, db_ref, l_ref = refs
        else:
            do_ref, xh_ref, r_ref, g_ref, dz_ref, dzb_ref, dg_ref, db_ref = refs

        @pl.when(pl.program_id(0) == 0)
        def _():
            dg_ref[...] = jnp.zeros_like(dg_ref)
            db_ref[...] = jnp.zeros_like(db_ref)
            if head:
                l_ref[...] = jnp.zeros_like(l_ref)

        xh = xh_ref[...]
        if head:
            e = xh * g_ref[...] + b_ref[...] - do_ref[...]
            l_ref[...] += jnp.sum(e * e, axis=0, keepdims=True)
            do = e * (1.0 / D)
        else:
            do = do_ref[...]
        dxh = do * g_ref[...]
        m1 = jnp.mean(dxh, axis=-1, keepdims=True)
        m2 = jnp.mean(dxh * xh, axis=-1, keepdims=True)
        dz = r_ref[...] * (dxh - m1 - xh * m2)
        dz_ref[...] = dz
        dzb_ref[...] = dz.astype(BF16)
        dg_ref[...] += jnp.sum(do * xh, axis=0, keepdims=True)
        db_ref[...] += jnp.sum(do, axis=0, keepdims=True)

    row = pl.BlockSpec((tm, D), lambda i: (i, 0))
    vec = pl.BlockSpec((1, D), lambda i: (0, 0))
    st = pl.BlockSpec((tm, 1), lambda i: (i, 0))
    ins = [dout, xhat, rstd, g.reshape(1, D)] + ([loss_b.reshape(1, D)] if head else [])
    return pl.pallas_call(
        kern, grid=(t // tm,), in_specs=[row, row, st] + [vec] * (len(ins) - 3),
        out_specs=[row, row, vec, vec] + ([vec] if head else []),
        out_shape=[jax.ShapeDtypeStruct((t, D), F32), jax.ShapeDtypeStruct((t, D), BF16)]
        + [jax.ShapeDtypeStruct((1, D), F32)] * (3 if head else 2),
        name=name, compiler_params=_cp())(*ins)


def _rope_tables(t):
    half = ROPE // 2
    inv = 10000.0 ** (-jnp.arange(half, dtype=F32) / half)
    ang = jnp.arange(t).astype(F32)[:, None] * inv[None, :]
    cos, sin = jnp.cos(ang), jnp.sin(ang)
    z = jnp.zeros((t, RP - ROPE), F32)
    return jnp.concatenate([cos, cos, z], axis=1), jnp.concatenate([-sin, sin, z], axis=1)


def _swap_halves(x):
    lane = lax.broadcasted_iota(jnp.int32, x.shape, 1)
    return jnp.where(lane < ROPE // 2, pltpu.roll(x, RP - ROPE // 2, 1), pltpu.roll(x, ROPE // 2, 1))


def _rope(x, cos, sin):
    return x * cos + _swap_halves(x) * sin


def _rope_t(gy, cos, sin):
    return gy * cos + _swap_halves(gy * sin)


def _mla_pre(hh, g_q, g_kv, cos, sin, tm=1024):
    t = hh.shape[0]
    tm = min(tm, t)

    def kern(h_ref, gq_ref, gkv_ref, c_ref, s_ref, cq_ref, k_ref):
        xq = h_ref[:, 0:QR]
        rq = lax.rsqrt(jnp.mean(xq * xq, axis=-1, keepdims=True) + RMS_EPS)
        cq_ref[...] = (xq * rq * gq_ref[...]).astype(BF16)
        xk = h_ref[:, QR:QR + KVR]
        rk = lax.rsqrt(jnp.mean(xk * xk, axis=-1, keepdims=True) + RMS_EPS)
        k_ref[:, 0:KVR] = (xk * rk * gkv_ref[...]).astype(BF16)
        k_ref[:, KVR:KD] = _rope(h_ref[:, QR + KVR:HW], c_ref[...], s_ref[...]).astype(BF16)

    return pl.pallas_call(
        kern, grid=(t // tm,),
        in_specs=[pl.BlockSpec((tm, HW), lambda i: (i, 0)), pl.BlockSpec((1, QR), lambda i: (0, 0)),
                  pl.BlockSpec((1, KVR), lambda i: (0, 0)), pl.BlockSpec((tm, RP), lambda i: (i, 0)),
                  pl.BlockSpec((tm, RP), lambda i: (i, 0))],
        out_specs=[pl.BlockSpec((tm, QR), lambda i: (i, 0)), pl.BlockSpec((tm, KD), lambda i: (i, 0))],
        out_shape=[jax.ShapeDtypeStruct((t, QR), BF16), jax.ShapeDtypeStruct((t, KD), BF16)],
        name="mla_pre", compiler_params=_cp())(hh, g_q.reshape(1, QR), g_kv.reshape(1, KVR), cos, sin)


def _mla_pre_bwd(hh, dcq, dk, g_q, g_kv, cos, sin, tm=1024):
    t = hh.shape[0]
    tm = min(tm, t)

    def rms_bwd(x, dy, g):
        r = lax.rsqrt(jnp.mean(x * x, axis=-1, keepdims=True) + RMS_EPS)
        gdy = dy * g
        dx = r * gdy - x * (r * r * r) * jnp.mean(gdy * x, axis=-1, keepdims=True)
        return dx, jnp.sum(dy * x * r, axis=0, keepdims=True)

    def kern(h_ref, dcq_ref, dk_ref, gq_ref, gkv_ref, c_ref, s_ref, dh_ref, dgq_ref, dgkv_ref):
        @pl.when(pl.program_id(0) == 0)
        def _():
            dgq_ref[...] = jnp.zeros_like(dgq_ref)
            dgkv_ref[...] = jnp.zeros_like(dgkv_ref)

        dxq, dgq = rms_bwd(h_ref[:, 0:QR], dcq_ref[...], gq_ref[...])
        dxk, dgk = rms_bwd(h_ref[:, QR:QR + KVR], dk_ref[:, 0:KVR], gkv_ref[...])
        dh_ref[:, 0:QR] = dxq.astype(BF16)
        dh_ref[:, QR:QR + KVR] = dxk.astype(BF16)
        dh_ref[:, QR + KVR:HW] = _rope_t(dk_ref[:, KVR:KD], c_ref[...], s_ref[...]).astype(BF16)
        dgq_ref[...] += dgq
        dgkv_ref[...] += dgk

    return pl.pallas_call(
        kern, grid=(t // tm,),
        in_specs=[pl.BlockSpec((tm, HW), lambda i: (i, 0)), pl.BlockSpec((tm, QR), lambda i: (i, 0)),
                  pl.BlockSpec((tm, KD), lambda i: (i, 0)), pl.BlockSpec((1, QR), lambda i: (0, 0)),
                  pl.BlockSpec((1, KVR), lambda i: (0, 0)), pl.BlockSpec((tm, RP), lambda i: (i, 0)),
                  pl.BlockSpec((tm, RP), lambda i: (i, 0))],
        out_specs=[pl.BlockSpec((tm, HW), lambda i: (i, 0)), pl.BlockSpec((1, QR), lambda i: (0, 0)),
                   pl.BlockSpec((1, KVR), lambda i: (0, 0))],
        out_shape=[jax.ShapeDtypeStruct((t, HW), BF16), jax.ShapeDtypeStruct((1, QR), F32),
                   jax.ShapeDtypeStruct((1, KVR), F32)],
        name="mla_pre_bwd", compiler_params=_cp())(hh, dcq, dk, g_q.reshape(1, QR), g_kv.reshape(1, KVR), cos, sin)


def _q_prep(q2, wuk_t, cos, sin, tm=1024):
    t = q2.shape[0]
    tm = min(tm, t)

    def kern(q_ref, w_ref, c_ref, s_ref, o_ref):
        cos_, sin_ = c_ref[...], s_ref[...]
        for h in range(H):
            qn = q_ref[:, h * NOPE:(h + 1) * NOPE].astype(BF16)
            o_ref[:, h * KD:h * KD + KVR] = (_dot(qn, w_ref[h], NN) * QSCALE).astype(BF16)
            qr = q_ref[:, H * NOPE + h * RP:H * NOPE + (h + 1) * RP]
            o_ref[:, h * KD + KVR:(h + 1) * KD] = (_rope(qr, cos_, sin_) * QSCALE).astype(BF16)

    return pl.pallas_call(
        kern, grid=(t // tm,),
        in_specs=[pl.BlockSpec((tm, 2 * H * NOPE), lambda i: (i, 0)), pl.BlockSpec((H, NOPE, KVR), lambda i: (0, 0, 0)),
                  pl.BlockSpec((tm, RP), lambda i: (i, 0)), pl.BlockSpec((tm, RP), lambda i: (i, 0))],
        out_specs=pl.BlockSpec((tm, H * KD), lambda i: (i, 0)),
        out_shape=jax.ShapeDtypeStruct((t, H * KD), BF16),
        name="q_prep", compiler_params=_cp())(q2, wuk_t, cos, sin)


def _o_up_bwd(do, o_lat, wuv_h, tm=1024):
    t = do.shape[0]
    tm = min(tm, t)

    def kern(do_ref, x_ref, w_ref, dx_ref, dw_ref, dlt_ref):
        @pl.when(pl.program_id(0) == 0)
        def _():
            dw_ref[...] = jnp.zeros_like(dw_ref)

        for h in range(H):
            dh_ = do_ref[:, h * NOPE:(h + 1) * NOPE]
            x = x_ref[:, h * KVR:(h + 1) * KVR]
            dx = _dot(dh_, w_ref[h], NT)
            dx_ref[:, h * KVR:(h + 1) * KVR] = dx.astype(BF16)
            dw_ref[h] += _dot(x.astype(BF16), dh_, TN)
            dl = jnp.broadcast_to(jnp.sum(dx * x, axis=1)[:, None], (tm, 128))
            dlt_ref[h] = dl.T[0:1, :]

    return pl.pallas_call(
        kern, grid=(t // tm,),
        in_specs=[pl.BlockSpec((tm, H * NOPE), lambda i: (i, 0)), pl.BlockSpec((tm, H * KVR), lambda i: (i, 0)),
                  pl.BlockSpec((H, KVR, NOPE), lambda i: (0, 0, 0))],
        out_specs=[pl.BlockSpec((tm, H * KVR), lambda i: (i, 0)), pl.BlockSpec((H, KVR, NOPE), lambda i: (0, 0, 0)),
                   pl.BlockSpec((H, 1, tm), lambda i: (0, 0, i))],
        out_shape=[jax.ShapeDtypeStruct((t, H * KVR), BF16), jax.ShapeDtypeStruct((H, KVR, NOPE), F32),
                   jax.ShapeDtypeStruct((H, 1, t), F32)],
        name="o_up_bwd", compiler_params=_cp())(do, o_lat, wuv_h)


def _causal_pairs(nq):
    return [(i, j) for i in range(nq) for j in range(i + 1)]


def _lane_tile(stat, width):
    return jnp.tile(stat, (1, width // 128))


def _flash_fwd(qcat, kc, wuv_h, bq, hb, gather=None):
    t = kc.shape[0]
    nq = t // bq
    pairs = _causal_pairs(nq)
    itab = jnp.asarray(np.array([p[0] for p in pairs], np.int32))
    jtab = jnp.asarray(np.array([p[1] for p in pairs], np.int32))

    ng = H // hb
    hosting = gather is not None

    def kern(it, jt, q_ref, k_ref, wuv_ref, *rest):
        if hosting:
            w_ref, o_ref, lset_ref, oup_ref, wall_ref, m_sc, l_sc, acc_sc, send_sems, recv_sems = rest
            ag_start, ag_forward, ag_finish = _allgather_schedule(w_ref, wall_ref, send_sems, recv_sems)
        else:
            o_ref, lset_ref, oup_ref, m_sc, l_sc, acc_sc = rest
        grp = pl.program_id(0)
        st = pl.program_id(1)
        i, j = it[st], jt[st]

        if hosting:
            @pl.when((grp == 0) & (st == 0))
            def _():
                ag_start()

        @pl.when(j == 0)
        def _():
            m_sc[...] = jnp.full_like(m_sc, NEG)
            l_sc[...] = jnp.zeros_like(l_sc)
            acc_sc[...] = jnp.zeros_like(acc_sc)

        def update(masked):
            k = k_ref[...]
            v = k[:, 0:KVR]
            if masked:
                row = lax.broadcasted_iota(jnp.int32, (bq, bq), 0)
                col = lax.broadcasted_iota(jnp.int32, (bq, bq), 1)
                keep = col <= row
            pending = [_dot(q_ref[:, hh * KD:(hh + 1) * KD], k, NT) for hh in range(min(AHEAD, hb))]
            for hh in range(hb):
                s = pending.pop(0)
                if hh + AHEAD < hb:
                    pending.append(_dot(q_ref[:, (hh + AHEAD) * KD:(hh + AHEAD + 1) * KD], k, NT))
                if masked:
                    s = jnp.where(keep, s, NEG)
                m_prev = m_sc[hh]
                m_next = jnp.maximum(m_prev, jnp.max(s, axis=1)[:, None])
                p = jnp.exp2(s - _lane_tile(m_next, bq))
                a = jnp.exp2(m_prev - m_next)
                l_sc[hh] = a * l_sc[hh] + jnp.sum(p, axis=1)[:, None]
                acc_sc[hh] = _lane_tile(a, KVR) * acc_sc[hh] + _dot(p.astype(BF16), v, NN)
                m_sc[hh] = m_next

        @pl.when(j < i)
        def _():
            update(False)

        @pl.when(j == i)
        def _():
            update(True)
            for hh in range(hb):
                l = l_sc[hh]
                o_h = acc_sc[hh] / _lane_tile(l, KVR)
                o_ref[:, hh * KVR:(hh + 1) * KVR] = o_h
                oup_ref[:, hh * NOPE:(hh + 1) * NOPE] = _dot(o_h.astype(BF16), wuv_ref[hh], NN).astype(BF16)
                lset_ref[hh] = (m_sc[hh] + jnp.log2(l)).T[0:1, :]

        if hosting:
            half_way = (ng * len(pairs)) // 2

            @pl.when(grp * len(pairs) + st == half_way)
            def _():
                ag_forward()

            @pl.when((grp == ng - 1) & (st == len(pairs) - 1))
            def _():
                ag_finish()

    in_specs = [pl.BlockSpec((bq, hb * KD), lambda g, s, it, jt: (it[s], g)),
                pl.BlockSpec((bq, KD), lambda g, s, it, jt: (jt[s], 0)),
                pl.BlockSpec((hb, KVR, NOPE), lambda g, s, it, jt: (g, 0, 0))]
    out_specs = [pl.BlockSpec((bq, hb * KVR), lambda g, s, it, jt: (it[s], g)),
                 pl.BlockSpec((hb, 1, bq), lambda g, s, it, jt: (g, 0, it[s])),
                 pl.BlockSpec((bq, hb * NOPE), lambda g, s, it, jt: (it[s], g))]
    out_shape = [jax.ShapeDtypeStruct((t, H * KVR), F32), jax.ShapeDtypeStruct((H, 1, t), F32),
                 jax.ShapeDtypeStruct((t, H * NOPE), BF16)]
    scratch = [pltpu.VMEM((hb, bq, 128), F32), pltpu.VMEM((hb, bq, 128), F32), pltpu.VMEM((hb, bq, KVR), F32)]
    args = [itab, jtab, qcat, kc, wuv_h]
    if hosting:
        in_specs.append(ANY)
        out_specs.append(ANY)
        out_shape.append(jax.ShapeDtypeStruct((4,) + gather.shape, gather.dtype))
        scratch += AG_SEMS
        args.append(gather)
    gs = pltpu.PrefetchScalarGridSpec(num_scalar_prefetch=2, grid=(ng, len(pairs)), in_specs=in_specs,
                                      out_specs=out_specs, scratch_shapes=scratch)
    return pl.pallas_call(kern, grid_spec=gs, out_shape=out_shape, name="mla_flash_fwd",
                          compiler_params=_cp())(*args)


def _flash_dkv(qcat, kc, do_lat, lse_t, delta_t, bq, hb, exchange=()):
    nx = len(exchange)
    t = kc.shape[0]
    nq = t // bq
    ng = H // hb
    npairs = nq * (nq + 1) // 2
    steps = [(j, g, i) for j in range(nq) for g in range(ng) for i in range(j, nq)]
    jtab = jnp.asarray(np.array([s[0] for s in steps], np.int32))
    gtab = jnp.asarray(np.array([s[1] for s in steps], np.int32))
    itab = jnp.asarray(np.array([s[2] for s in steps], np.int32))
    ptab = jnp.asarray(np.array([s[2] * (s[2] + 1) // 2 + s[0] for s in steps], np.int32))

    def kern(jt, gt, it, pt, q_ref, k_ref, do_ref, lset_ref, dlt_ref, *rest):
        p_refs, (dk_ref, ds_ref), slots_refs = rest[:nx], rest[nx:nx + 2], rest[nx + 2:2 * nx + 2]
        dk_sc, dv_sc = rest[2 * nx + 2:2 * nx + 4]
        sems = rest[2 * nx + 4:]
        hooks = [_device_exchange_schedule(p_refs[e], slots_refs[e], sems[2 * e], sems[2 * e + 1]) for e in range(nx)]
        st = pl.program_id(0)
        j, g, i = jt[st], gt[st], it[st]

        if nx:
            @pl.when(st == 0)
            def _():
                for start, _ in hooks:
                    start()

        @pl.when((g == 0) & (i == j))
        def _():
            dk_sc[...] = jnp.zeros_like(dk_sc)
            dv_sc[...] = jnp.zeros_like(dv_sc)

        def update(masked):
            k = k_ref[...]
            v = k[:, 0:KVR]
            if masked:
                row = lax.broadcasted_iota(jnp.int32, (bq, bq), 0)
                col = lax.broadcasted_iota(jnp.int32, (bq, bq), 1)
                keep = row <= col

            def first_matmuls(hh):
                dob = do_ref[:, hh * KVR:(hh + 1) * KVR].astype(BF16)
                return _dot(k, q_ref[:, hh * KD:(hh + 1) * KD], NT), _dot(v, dob, NT), dob

            pending = [first_matmuls(hh) for hh in range(min(AHEAD, hb))]
            for hh in range(hb):
                s, dp, dob = pending.pop(0)
                if hh + AHEAD < hb:
                    pending.append(first_matmuls(hh + AHEAD))
                if masked:
                    s = jnp.where(keep, s, NEG)
                p = jnp.exp2(s - lset_ref[hh])
                dv_sc[...] += _dot(p.astype(BF16), dob, NN)
                dsb = (p * (dp - dlt_ref[hh])).astype(BF16)
                ds_ref[0, 0, hh] = dsb
                dk_sc[...] += _dot(dsb, q_ref[:, hh * KD:(hh + 1) * KD], NN)

        @pl.when(i > j)
        def _():
            update(False)

        @pl.when(i == j)
        def _():
            update(True)

        @pl.when((g == ng - 1) & (i == nq - 1))
        def _():
            dk_ref[:, 0:KVR] = dk_sc[:, 0:KVR] * LN2 + dv_sc[...]
            dk_ref[:, KVR:KD] = dk_sc[:, KVR:KD] * LN2

        if nx:
            @pl.when(st == len(steps) - 1)
            def _():
                for _, finish in hooks:
                    finish()

    in_specs = [pl.BlockSpec((bq, hb * KD), lambda s, jt, gt, it, pt: (it[s], gt[s])),
                pl.BlockSpec((bq, KD), lambda s, jt, gt, it, pt: (jt[s], 0)),
                pl.BlockSpec((bq, hb * KVR), lambda s, jt, gt, it, pt: (it[s], gt[s])),
                pl.BlockSpec((hb, 1, bq), lambda s, jt, gt, it, pt: (gt[s], 0, it[s])),
                pl.BlockSpec((hb, 1, bq), lambda s, jt, gt, it, pt: (gt[s], 0, it[s]))] + [ANY] * nx
    out_specs = [pl.BlockSpec((bq, KD), lambda s, jt, gt, it, pt: (jt[s], 0)),
                 pl.BlockSpec((1, 1, hb, bq, bq), lambda s, jt, gt, it, pt: (gt[s], pt[s], 0, 0, 0))] + [ANY] * nx
    out_shape = [jax.ShapeDtypeStruct((t, KD), F32), jax.ShapeDtypeStruct((ng, npairs, hb, bq, bq), BF16)]
    out_shape += [jax.ShapeDtypeStruct((8, e.shape[1] // 2, D), e.dtype) for e in exchange]
    scratch = [pltpu.VMEM((bq, KD), F32), pltpu.VMEM((bq, KVR), F32)] + ALL_SEMS * nx
    args = [jtab, gtab, itab, ptab, qcat, kc, do_lat, lse_t, delta_t, *exchange]
    gs = pltpu.PrefetchScalarGridSpec(num_scalar_prefetch=4, grid=(len(steps),), in_specs=in_specs,
                                      out_specs=out_specs, scratch_shapes=scratch)
    return pl.pallas_call(kern, grid_spec=gs, out_shape=out_shape, name="mla_flash_dkv",
                          compiler_params=_cp())(*args)


def _flash_dq(ds_all, kc_t, q2, wuk_h, cos, sin, bq, exchange=None):
    nq = kc_t.shape[0]
    t = nq * bq
    ngrp, _, hper = ds_all.shape[:3]
    pairs = _causal_pairs(nq)
    itab = jnp.asarray(np.array([p[0] for p in pairs], np.int32))
    jtab = jnp.asarray(np.array([p[1] for p in pairs], np.int32))
    hosting = exchange is not None

    def kern(it, jt, *refs):
        ds_refs, (kt_ref, q_ref, w_ref, c_ref, s_ref), rest = refs[:ngrp], refs[ngrp:ngrp + 5], refs[ngrp + 5:]
        if hosting:
            p_ref, dq_ref, dw_ref, slots_ref, acc_sc, send_sems, recv_sems = rest
            xc_start, xc_finish = _device_exchange_schedule(p_ref, slots_ref, send_sems, recv_sems)
        else:
            dq_ref, dw_ref, acc_sc = rest
        st = pl.program_id(0)
        i, j = it[st], jt[st]
        kt = kt_ref[...]

        def ds(hh):
            return ds_refs[hh // hper][0, 0, hh % hper]

        @pl.when(st == 0)
        def _():
            dw_ref[...] = jnp.zeros_like(dw_ref)
            if hosting:
                xc_start()

        @pl.when(j == 0)
        def _():
            for hh in range(H):
                acc_sc[hh] = _dot(kt, ds(hh), NN)

        @pl.when((j > 0) & (j < i))
        def _():
            for hh in range(H):
                acc_sc[hh] += _dot(kt, ds(hh), NN)

        @pl.when(j == i)
        def _():
            cos_, sin_ = c_ref[...], s_ref[...]
            for hh in range(H):
                tot = _dot(kt, ds(hh), NN)
                tot = jnp.where(i > 0, tot + acc_sc[hh], tot)
                dq_h = tot.T * MLA_SCALE
                dql = dq_h[:, 0:KVR].astype(BF16)
                dq_ref[:, hh * NOPE:(hh + 1) * NOPE] = _dot(dql, w_ref[hh], NN).astype(BF16)
                dq_ref[:, H * NOPE + hh * RP:H * NOPE + (hh + 1) * RP] = _rope_t(dq_h[:, KVR:KD], cos_, sin_).astype(BF16)
                dw_ref[hh] += _dot(q_ref[:, hh * NOPE:(hh + 1) * NOPE].astype(BF16), dql, TN)

        if hosting:
            @pl.when(st == len(pairs) - 1)
            def _():
                xc_finish()

    def group(gi):
        return pl.BlockSpec((1, 1, hper, bq, bq), lambda s, it, jt: (gi, s, 0, 0, 0))

    in_specs = [group(gi) for gi in range(ngrp)] + [
        pl.BlockSpec((None, KD, bq), lambda s, it, jt: (jt[s], 0, 0)),
        pl.BlockSpec((bq, 2 * H * NOPE), lambda s, it, jt: (it[s], 0)),
        pl.BlockSpec((H, KVR, NOPE), lambda s, it, jt: (0, 0, 0)),
        pl.BlockSpec((bq, RP), lambda s, it, jt: (it[s], 0)), pl.BlockSpec((bq, RP), lambda s, it, jt: (it[s], 0))]
    out_specs = [pl.BlockSpec((bq, 2 * H * NOPE), lambda s, it, jt: (it[s], 0)),
                 pl.BlockSpec((H, NOPE, KVR), lambda s, it, jt: (0, 0, 0))]
    out_shape = [jax.ShapeDtypeStruct((t, 2 * H * NOPE), BF16), jax.ShapeDtypeStruct((H, NOPE, KVR), F32)]
    scratch = [pltpu.VMEM((H, KD, bq), F32)]
    args = [itab, jtab] + [ds_all] * ngrp + [kc_t, q2, wuk_h, cos, sin]
    if hosting:
        in_specs.append(ANY)
        out_specs.append(ANY)
        out_shape.append(jax.ShapeDtypeStruct((8, exchange.shape[1] // 2, D), exchange.dtype))
        scratch += ALL_SEMS
        args.append(exchange)
    gs = pltpu.PrefetchScalarGridSpec(num_scalar_prefetch=2, grid=(len(pairs),), in_specs=in_specs,
                                      out_specs=out_specs, scratch_shapes=scratch)
    return pl.pallas_call(kern, grid_spec=gs, out_shape=out_shape, name="mla_flash_dq",
                          compiler_params=_cp())(*args)


def _bucket_table():
    d = np.arange(WIN)
    max_exact = NBKT // 2
    nf = np.maximum(d, 1).astype(np.float32)
    large = max_exact + (np.log(nf / np.float32(max_exact)) / np.float32(math.log(WIN / max_exact))
                         * np.float32(NBKT - max_exact)).astype(np.int32)
    large = np.minimum(large, NBKT - 1)
    bucket = np.where(d < max_exact, d, large).astype(np.int32)
    jj = np.arange(2 * WIN)[:, None]
    ii = np.arange(WIN)[None, :]
    dist = ii + WIN - jj
    valid = (dist >= 0) & (dist < WIN)
    return np.where(valid, bucket[np.clip(dist, 0, WIN - 1)], -1).astype(np.int32)


def _bias_build(rel_bias, bkt):
    def kern(bk_ref, rb_ref, o_ref):
        bk = bk_ref[...]
        for hd in range(QH):
            acc = jnp.full((2 * WIN, WIN), NEG, F32)
            for b in range(NBKT):
                acc = jnp.where(bk == b, rb_ref[b, hd], acc)
            o_ref[hd] = acc

    return pl.pallas_call(
        kern, in_specs=[pl.BlockSpec(memory_space=pltpu.VMEM), pl.BlockSpec(memory_space=pltpu.SMEM)],
        out_specs=pl.BlockSpec(memory_space=pltpu.VMEM),
        out_shape=jax.ShapeDtypeStruct((QH, 2 * WIN, WIN), F32), name="swa_bias_build")(bkt, rel_bias)


def _bias_bwd(dbias, bkt):
    def kern(db_ref, bk_ref, o_ref):
        bk = bk_ref[...]
        for hd in range(QH):
            g = db_ref[hd]
            for b in range(NBKT):
                r = b * QH + hd
                o_ref[r:r + 1, :] = jnp.sum(jnp.where(bk == b, g, 0.0), axis=0, keepdims=True)

    return pl.pallas_call(
        kern, in_specs=[pl.BlockSpec(memory_space=pltpu.VMEM), pl.BlockSpec(memory_space=pltpu.VMEM)],
        out_specs=pl.BlockSpec(memory_space=pltpu.VMEM),
        out_shape=jax.ShapeDtypeStruct((NBKT * QH, WIN), F32), name="swa_bias_bwd")(dbias, bkt)


def _swa_finish_scores(raw, bias, first):
    s = raw * SWA_SCALE + bias
    if first is not None:
        row = lax.broadcasted_iota(jnp.int32, s.shape, 0)
        s = jnp.where(jnp.logical_or(jnp.logical_not(first), row >= WIN), s, NEG)
    return s


def _swa_fwd(qkv_t, bias, sinks, qb):
    t = qkv_t.shape[1]
    w = qb * WIN
    nst = t // w

    def kern(q_ref, kc_ref, kp_ref, vc_ref, vp_ref, b_ref, sk_ref, o_ref, lse_ref):
        n = pl.program_id(0)
        kfull = jnp.concatenate([kp_ref[...], kc_ref[...]], axis=1)
        vfull = jnp.concatenate([vp_ref[...], vc_ref[...]], axis=1)
        head_row = lax.broadcasted_iota(jnp.int32, (QH, WIN), 0)
        groups = [(b, kh) for b in range(qb) for kh in range(KVH)]

        def raw_scores(b, kh):
            k_band = kfull[kh * HD:(kh + 1) * HD, b * WIN:(b + 2) * WIN]
            return [_dot(k_band, q_ref[(kh * G + g) * HD:(kh * G + g + 1) * HD, b * WIN:(b + 1) * WIN], TN)
                    for g in range(G)]

        o_rows = [[] for _ in range(qb)]
        lse_tiles = [jnp.zeros((QH, WIN), F32) for _ in range(qb)]
        pending = [raw_scores(*grp) for grp in groups[:AHEAD]]
        for gi, (b, kh) in enumerate(groups):
            scores = pending.pop(0)
            if gi + AHEAD < len(groups):
                pending.append(raw_scores(*groups[gi + AHEAD]))
            v_band = vfull[kh * HD:(kh + 1) * HD, b * WIN:(b + 2) * WIN]
            for g in range(G):
                hd = kh * G + g
                s = _swa_finish_scores(scores[g], b_ref[hd], (n == 0) if b == 0 else None)
                sink = sk_ref[hd]
                m = jnp.maximum(jnp.max(s, axis=0, keepdims=True), sink)
                p = jnp.exp(s - m)
                den = jnp.sum(p, axis=0, keepdims=True) + jnp.exp(sink - m)
                p = p / den
                o_rows[b].append(_dot(v_band, p.astype(BF16), NN))
                lse_tiles[b] = jnp.where(head_row == hd, m + jnp.log(den), lse_tiles[b])
        o_ref[...] = jnp.concatenate([jnp.concatenate(rows, axis=0) for rows in o_rows], axis=1)
        lse_ref[...] = jnp.concatenate(lse_tiles, axis=1)

    prev = lambda r: (lambda n: (r, jnp.maximum(n * qb - 1, 0)))
    return pl.pallas_call(
        kern, grid=(nst,),
        in_specs=[pl.BlockSpec((QH * HD, w), lambda n: (0, n)),
                  pl.BlockSpec((KVH * HD, w), lambda n: (4, n)), pl.BlockSpec((KVH * HD, WIN), prev(4)),
                  pl.BlockSpec((KVH * HD, w), lambda n: (5, n)), pl.BlockSpec((KVH * HD, WIN), prev(5)),
                  pl.BlockSpec((QH, 2 * WIN, WIN), lambda n: (0, 0, 0)),
                  pl.BlockSpec(memory_space=pltpu.SMEM)],
        out_specs=[pl.BlockSpec((QH * HD, w), lambda n: (0, n)), pl.BlockSpec((QH, w), lambda n: (0, n))],
        out_shape=[jax.ShapeDtypeStruct((QH * HD, t), F32), jax.ShapeDtypeStruct((QH, t), F32)],
        name="swa_fwd", compiler_params=_cp())(qkv_t, qkv_t, qkv_t, qkv_t, qkv_t, bias, sinks)


def _swa_bwd(qkv_t, do_t, o_t, lse, bias, sinks, qb):
    t = qkv_t.shape[1]
    w = qb * WIN
    nst = t // w
    nblk = t // WIN

    def kern(q_ref, kc_ref, kp_ref, vc_ref, vp_ref, do_ref, o_ref, lse_ref, qn_ref, don_ref, on_ref, lsen_ref,
             b_ref, sk_ref, dqkv_ref, db_ref, dsk_ref):
        n = pl.program_id(0)

        @pl.when(n == 0)
        def _():
            db_ref[...] = jnp.zeros_like(db_ref)
            dsk_ref[...] = jnp.zeros_like(dsk_ref)

        kfull = jnp.concatenate([kp_ref[...], kc_ref[...]], axis=1)
        vfull = jnp.concatenate([vp_ref[...], vc_ref[...]], axis=1)
        head_row = lax.broadcasted_iota(jnp.int32, (QH, WIN), 0)
        db_acc = [None] * QH
        dsk_tile = jnp.zeros((QH, WIN), F32)
        prev_part = [[[None] * qb for _ in range(KVH)] for _ in range(2)]
        cur_part = [[[None] * qb for _ in range(KVH)] for _ in range(2)]
        groups = [(b, kh) for b in range(qb) for kh in range(KVH)]

        def first_matmuls(b, kh):
            k_band = kfull[kh * HD:(kh + 1) * HD, b * WIN:(b + 2) * WIN]
            v_band = vfull[kh * HD:(kh + 1) * HD, b * WIN:(b + 2) * WIN]
            out = []
            for g in range(G):
                rs = slice((kh * G + g) * HD, (kh * G + g + 1) * HD)
                dob = do_ref[rs, b * WIN:(b + 1) * WIN].astype(BF16)
                out.append((_dot(k_band, q_ref[rs, b * WIN:(b + 1) * WIN], TN), _dot(v_band, dob, TN), dob))
            return out

        dq_rows = [[] for _ in range(qb)]
        pending = [first_matmuls(*grp) for grp in groups[:AHEAD]]
        for gi, (b, kh) in enumerate(groups):
            first = pending.pop(0)
            if gi + AHEAD < len(groups):
                pending.append(first_matmuls(*groups[gi + AHEAD]))
            cs = slice(b * WIN, (b + 1) * WIN)
            k_band = kfull[kh * HD:(kh + 1) * HD, b * WIN:(b + 2) * WIN]
            dk_b = dv_b = None
            for g in range(G):
                hd = kh * G + g
                rs = slice(hd * HD, (hd + 1) * HD)
                raw, dp, dob = first[g]
                lse_h = lse_ref[hd:hd + 1, cs]
                s = _swa_finish_scores(raw, b_ref[hd], (n == 0) if b == 0 else None)
                p = jnp.exp(s - lse_h)
                dl = jnp.sum(do_ref[rs, cs] * o_ref[rs, cs], axis=0, keepdims=True)
                ds = p * (dp - dl)
                db_acc[hd] = ds if db_acc[hd] is None else db_acc[hd] + ds
                dsk_tile = jnp.where(head_row == hd, dsk_tile - jnp.exp(sk_ref[hd] - lse_h) * dl, dsk_tile)
                dss = (ds * SWA_SCALE).astype(BF16)
                dq_rows[b].append(_dot(k_band, dss, NN).astype(BF16))
                dk_h = _dot(q_ref[rs, cs], dss, NT)
                dv_h = _dot(dob, p.astype(BF16), NT)
                dk_b = dk_h if dk_b is None else dk_b + dk_h
                dv_b = dv_h if dv_b is None else dv_b + dv_h
            for which, val in ((0, dk_b), (1, dv_b)):
                prev_part[which][kh][b] = val[:, 0:WIN]
                cur_part[which][kh][b] = val[:, WIN:2 * WIN]
        dq_cols = [jnp.concatenate(rows, axis=0) for rows in dq_rows]

        live = n < nst - 1
        ls = slice((qb - 1) * WIN, qb * WIN)
        halo = [[None] * KVH for _ in range(2)]
        for kh in range(KVH):
            k_last = kc_ref[kh * HD:(kh + 1) * HD, ls]
            v_last = vc_ref[kh * HD:(kh + 1) * HD, ls]
            dk_b = dv_b = None
            for g in range(G):
                hd = kh * G + g
                rs = slice(hd * HD, (hd + 1) * HD)
                q_t = qn_ref[rs, :]
                do = don_ref[rs, :]
                s = _dot(k_last, q_t, TN) * SWA_SCALE + b_ref[hd, 0:WIN, :]
                p = jnp.exp(s - lsen_ref[hd:hd + 1, :])
                dob = do.astype(BF16)
                dp = _dot(v_last, dob, TN)
                dl = jnp.sum(do * on_ref[rs, :], axis=0, keepdims=True)
                dss = (p * (dp - dl) * SWA_SCALE).astype(BF16)
                dk_h = _dot(q_t, dss, NT)
                dv_h = _dot(dob, p.astype(BF16), NT)
                dk_b = dk_h if dk_b is None else dk_b + dk_h
                dv_b = dv_h if dv_b is None else dv_b + dv_h
            halo[0][kh] = jnp.where(live, dk_b, 0.0)
            halo[1][kh] = jnp.where(live, dv_b, 0.0)

        kv_rows = []
        for which in range(2):
            for kh in range(KVH):
                blocks = [cur_part[which][kh][p] + (prev_part[which][kh][p + 1] if p + 1 < qb else halo[which][kh])
                          for p in range(qb)]
                kv_rows.append(jnp.concatenate(blocks, axis=1))
        dqkv_ref[...] = jnp.concatenate(
            [jnp.concatenate(dq_cols, axis=1), jnp.concatenate(kv_rows, axis=0).astype(BF16)], axis=0)
        db_ref[...] += jnp.stack(db_acc)
        dsk_ref[...] += dsk_tile

    prev = lambda r: (lambda n: (r, jnp.maximum(n * qb - 1, 0)))
    nxt = lambda n: (0, jnp.minimum((n + 1) * qb, nblk - 1))
    big = lambda: pl.BlockSpec((QH * HD, w), lambda n: (0, n))
    return pl.pallas_call(
        kern, grid=(nst,),
        in_specs=[big(),
                  pl.BlockSpec((KVH * HD, w), lambda n: (4, n)), pl.BlockSpec((KVH * HD, WIN), prev(4)),
                  pl.BlockSpec((KVH * HD, w), lambda n: (5, n)), pl.BlockSpec((KVH * HD, WIN), prev(5)),
                  big(), big(), pl.BlockSpec((QH, w), lambda n: (0, n)),
                  pl.BlockSpec((QH * HD, WIN), nxt), pl.BlockSpec((QH * HD, WIN), nxt),
                  pl.BlockSpec((QH * HD, WIN), nxt), pl.BlockSpec((QH, WIN), nxt),
                  pl.BlockSpec((QH, 2 * WIN, WIN), lambda n: (0, 0, 0)),
                  pl.BlockSpec(memory_space=pltpu.SMEM)],
        out_specs=[pl.BlockSpec(((QH + 2 * KVH) * HD, w), lambda n: (0, n)),
                   pl.BlockSpec((QH, 2 * WIN, WIN), lambda n: (0, 0, 0)),
                   pl.BlockSpec((QH, WIN), lambda n: (0, 0))],
        out_shape=[jax.ShapeDtypeStruct(((QH + 2 * KVH) * HD, t), BF16),
                   jax.ShapeDtypeStruct((QH, 2 * WIN, WIN), F32), jax.ShapeDtypeStruct((QH, WIN), F32)],
        name="swa_bwd", compiler_params=_cp())(
            qkv_t, qkv_t, qkv_t, qkv_t, qkv_t, do_t, o_t, lse, qkv_t, do_t, o_t, lse, bias, sinks)


def _adamw_math(w, g, m, v):
    nm = B1 * m + (1.0 - B1) * g
    nv = B2 * v + (1.0 - B2) * (g * g)
    mhat = nm * (1.0 / (1.0 - B1 ** STEP))
    vhat = nv * (1.0 / (1.0 - B2 ** STEP))
    return -LR * (mhat / (jnp.sqrt(vhat) + ADAM_EPS) + WD * w), nm, nv


def _adamw_layers(w, m, v, g0buf, g1buf, off, name, tm=512):
    rows = w.shape[1]
    nb, ob = rows // tm, off // tm

    def kern(w_ref, m_ref, v_ref, g0_ref, g1_ref, gr_ref, d_ref, nm_ref, nv_ref):
        g_ = jnp.where(pl.program_id(0) == 0, g0_ref[...], g1_ref[...])
        gr_ref[...] = g_
        d_ref[...], nm_ref[...], nv_ref[...] = _adamw_math(w_ref[...], g_, m_ref[...], v_ref[...])

    lay = pl.BlockSpec((None, tm, D), lambda l, i: (l, i, 0))
    gsp = pl.BlockSpec((tm, D), lambda l, i: (ob + i, 0))
    return pl.pallas_call(
        kern, grid=(2, nb), in_specs=[lay, lay, lay, gsp, gsp], out_specs=[lay] * 4,
        out_shape=[jax.ShapeDtypeStruct(w.shape, F32)] * 4, name=name, compiler_params=_cp())(w, m, v, g0buf, g1buf)


def _adamw(w, g, m, v, name, tm=544):
    r = w.shape[0]
    tm = r if r % tm else tm

    def kern(w_ref, g_ref, m_ref, v_ref, d_ref, nm_ref, nv_ref):
        d_ref[...], nm_ref[...], nv_ref[...] = _adamw_math(w_ref[...], g_ref[...], m_ref[...], v_ref[...])

    row = pl.BlockSpec((tm, D), lambda i: (i, 0))
    sds = jax.ShapeDtypeStruct((r, D), F32)
    return pl.pallas_call(kern, grid=(r // tm,), in_specs=[row] * 4, out_specs=[row] * 3, out_shape=[sds] * 3,
                          name=name, compiler_params=_cp())(w, g, m, v)


def _mesh_pos():
    return lax.axis_index("x"), lax.axis_index("y"), lax.axis_index("c")


ANY = pl.BlockSpec(memory_space=pl.ANY)


AG_SEMS = [pltpu.SemaphoreType.DMA((6,)), pltpu.SemaphoreType.DMA((6,))]


def _allgather_schedule(w_ref, out_ref, send_sems, recv_sems):
    half = w_ref.shape[0] // 2
    x, y, c = _mesh_pos()
    me, sibling = (x, y, c), (x, y, 1 - c)
    chips = [(1 - x, y), (x, 1 - y), (1 - x, 1 - y)]

    def rows(px, py, pc):
        return out_ref.at[2 * px + py, pl.ds(pc * half, half), :]

    def copy(k, block, to, src=None):
        return pltpu.make_async_remote_copy(
            src_ref=rows(*block) if src is None else src, dst_ref=rows(*block),
            send_sem=send_sems.at[k], recv_sem=recv_sems.at[k], device_id=to, device_id_type=MESH)

    def first():
        return [copy(j, me, (*chip, c), src=w_ref.at[pl.ds(c * half, half), :]) for j, chip in enumerate(chips)]

    def passed():
        return [copy(3 + j, (*chip, c), sibling) for j, chip in enumerate(chips)]

    def start():
        for cp in first():
            cp.start()

    def forward():
        for j, chip in enumerate(chips):
            copy(j, (*chip, c), me).wait_recv()
            passed()[j].start()

    def finish():
        for j, chip in enumerate(chips):
            copy(3 + j, (*chip, 1 - c), me).wait_recv()
        for cp in first() + passed():
            cp.wait_send()

    return start, forward, finish


def _allgather_weights(wpack):
    def body(w_ref, out_ref, send_sems, recv_sems):
        start, forward, finish = _allgather_schedule(w_ref, out_ref, send_sems, recv_sems)
        start()
        forward()
        finish()

    return pl.pallas_call(
        body, out_shape=jax.ShapeDtypeStruct((4,) + wpack.shape, wpack.dtype), in_specs=[ANY], out_specs=ANY,
        scratch_shapes=AG_SEMS, name="allgather_weights")(wpack)


def _row_tile(rows):
    t = min(rows, 512)
    while rows % t or t % 16:
        t -= 16
    return t


ALL_SEMS = [pltpu.SemaphoreType.DMA((7,)), pltpu.SemaphoreType.DMA((7,))]


def _device_exchange_schedule(g_ref, out_ref, send_sems, recv_sems):
    half = g_ref.shape[1] // 2
    x, y, c = _mesh_pos()
    me = 4 * x + 2 * y + c
    peers = [(x ^ (k >> 2), y ^ ((k >> 1) & 1), c ^ (k & 1)) for k in range(1, 8)]

    def sends():
        return [pltpu.make_async_remote_copy(
            src_ref=g_ref.at[2 * px + py, pl.ds(pc * half, half), :], dst_ref=out_ref.at[me],
            send_sem=send_sems.at[j], recv_sem=recv_sems.at[j], device_id=(px, py, pc), device_id_type=MESH)
            for j, (px, py, pc) in enumerate(peers)]

    def start():
        for cp in sends():
            cp.start()

    def finish():
        for j, (px, py, pc) in enumerate(peers):
            pltpu.make_async_remote_copy(
                src_ref=out_ref.at[me], dst_ref=out_ref.at[4 * px + 2 * py + pc], send_sem=send_sems.at[j],
                recv_sem=recv_sems.at[j], device_id=(px, py, pc), device_id_type=MESH).wait_recv()
        for cp in sends():
            cp.wait_send()

    return start, finish


def _sum_devices(slots, g, pos, tag):
    half = slots.shape[1]
    tm = _row_tile(half)
    nb = half // tm

    def kern(pos_ref, own_ref, *refs):
        acc = own_ref[0].astype(F32)
        for s_ref in refs[:7]:
            acc = acc + s_ref[0].astype(F32)
        refs[7][...] = acc

    def slot(k):
        return pl.BlockSpec((1, tm, D), lambda i, pos: (jnp.bitwise_xor(pos[2], k), i, 0))

    gs = pltpu.PrefetchScalarGridSpec(
        num_scalar_prefetch=1, grid=(nb,),
        in_specs=[pl.BlockSpec((1, tm, D), lambda i, pos: (pos[0], pos[1] * nb + i, 0))] + [slot(k) for k in range(1, 8)],
        out_specs=pl.BlockSpec((tm, D), lambda i, pos: (pos[1] * nb + i, 0)))
    return pl.pallas_call(kern, grid_spec=gs, out_shape=jax.ShapeDtypeStruct((2 * half, D), F32),
                          name=f"rs_sum_devices_{tag}", compiler_params=_cp())(pos, g, *([slots] * 7))


def _reduce_scatter_finish(slots, g, pos, tag):
    return _join_core_halves(_sum_devices(slots, g, pos, tag), tag)


def _join_core_halves(r, tag):
    half = r.shape[0] // 2

    def body(r_ref, out_ref, send_sem, recv_sem):
        x, y, c = _mesh_pos()
        mine = out_ref.at[pl.ds(c * half, half), :]
        cp = pltpu.make_async_remote_copy(
            src_ref=mine, dst_ref=mine, send_sem=send_sem, recv_sem=recv_sem,
            device_id=(x, y, 1 - c), device_id_type=MESH)
        cp.start()
        theirs = out_ref.at[pl.ds((1 - c) * half, half), :]
        pltpu.make_async_remote_copy(
            src_ref=theirs, dst_ref=theirs, send_sem=send_sem, recv_sem=recv_sem,
            device_id=(x, y, 1 - c), device_id_type=MESH).wait_recv()
        cp.wait_send()

    return pl.pallas_call(
        body, out_shape=jax.ShapeDtypeStruct(r.shape, r.dtype), in_specs=[ANY], out_specs=ANY,
        input_output_aliases={0: 0},
        scratch_shapes=[pltpu.SemaphoreType.DMA, pltpu.SemaphoreType.DMA],
        name=f"rs_join_cores_{tag}")(r)


def _allreduce_small(v, name):
    def body(v_ref, out_ref, gat, send_sems, recv_sems):
        x, y, c = _mesh_pos()
        me = 4 * x + 2 * y + c
        gat[me] = v_ref[...]
        sends = []
        for k in range(1, 8):
            peer = (x ^ (k >> 2), y ^ ((k >> 1) & 1), c ^ (k & 1))
            cp = pltpu.make_async_remote_copy(
                src_ref=v_ref, dst_ref=gat.at[me], send_sem=send_sems.at[k - 1], recv_sem=recv_sems.at[k - 1],
                device_id=peer, device_id_type=MESH)
            cp.start()
            sends.append(cp)
        for k in range(1, 8):
            px, py, pc = x ^ (k >> 2), y ^ ((k >> 1) & 1), c ^ (k & 1)
            pltpu.make_async_remote_copy(
                src_ref=v_ref, dst_ref=gat.at[4 * px + 2 * py + pc], send_sem=send_sems.at[k - 1],
                recv_sem=recv_sems.at[k - 1], device_id=(px, py, pc), device_id_type=MESH).wait_recv()
        for cp in sends:
            cp.wait_send()
        acc = gat[0]
        for d in range(1, 8):
            acc = acc + gat[d]
        out_ref[...] = acc

    return pl.pallas_call(
        body, out_shape=jax.ShapeDtypeStruct(v.shape, F32),
        in_specs=[pl.BlockSpec(memory_space=pltpu.VMEM)], out_specs=pl.BlockSpec(memory_space=pltpu.VMEM),
        scratch_shapes=[pltpu.VMEM((8,) + v.shape, F32), pltpu.SemaphoreType.DMA((7,)), pltpu.SemaphoreType.DMA((7,))],
        name=name)(v)


def _mlp_fwd(xb, w_up, w_down, tag):
    a = _mm(xb, w_up[0], "nn", f"mlp_up_{tag}", out_dtype=BF16, relu2=True, b_view=("cols", w_up[1]))
    return a, _mm(a, w_down[0], "nn", f"mlp_down_{tag}", b_view=("rows", w_down[1]), tm=2048)


def _mlp_bwd(dz, dzb, xb, a, w_up, w_down, tag):
    du = _mm(dzb, w_down[0], "nt", f"mlp_down_dx_{tag}", out_dtype=BF16, gate_a=a, b_view=("rows", w_down[1]),
             tm=2048)
    gsh = _mm(xb, du, "tn", f"mlp_up_dw_{tag}", out_dtype=BF16, out_view=("cols", 2 * ROWS["mlp_w_up"], 0, None))
    gsh = _mm(a, dzb, "tn", f"mlp_down_dw_{tag}", out_dtype=BF16,
              out_view=("rows", 2 * ROWS["mlp_w_up"], ROWS["mlp_w_up"], gsh))
    dx = _mm(du, w_up[0], "nt", f"mlp_up_dx_{tag}", addend=dz, add_scale=ALPHA, b_view=("cols", w_up[1]))
    return dx, gsh


def _fwd_bwd(x, target, w, dist=None, bq=512, qb=8, hb=8):
    t = x.shape[0]
    bq = min(bq, t)
    qb = min(qb, t // WIN)
    cos, sin = _rope_tables(t)
    bkt = jnp.asarray(_bucket_table())
    w_in = jnp.pad(w[("mla_w_in", None)], ((0, 0), (0, HW - (QR + KVR + ROPE))))
    wuq = w[("mla_w_uq", None)]
    wq2 = jnp.concatenate([wuq[:, :, :NOPE].reshape(QR, H * NOPE),
                           jnp.pad(wuq[:, :, NOPE:], ((0, 0), (0, 0), (0, RP - ROPE))).reshape(QR, H * RP)], axis=1)
    wuk_t = w[("mla_w_uk", None)].transpose(1, 2, 0)
    wuk_h = w[("mla_w_uk", None)].transpose(1, 0, 2)
    wuv_h = w[("mla_w_uv", None)].transpose(1, 0, 2)
    w_o = w[("mla_w_o", None)]
    sinks = w["swa_sinks"].reshape(QH)
    lnp = lambda n, l: w[n][l]
    reduced = {}

    hh = _mm(x, w_in, "nn", "mla_in")
    cq, kc = _mla_pre(hh, w["mla_g_q"], w["mla_g_kv"], cos, sin)
    q2 = _mm(cq, wq2, "nn", "mla_uq")
    qcat = _q_prep(q2, wuk_t, cos, sin)
    if dist is None:
        o_lat, lse0_t, o0 = _flash_fwd(qcat, kc, wuv_h, bq, hb)
    else:
        o_lat, lse0_t, o0, wall = _flash_fwd(qcat, kc, wuv_h, bq, hb, gather=dist.late_pack)
        wall = lax.dynamic_update_slice(wall, dist.late_pack[None], (dist.shard, 0, 0))
        w = {**w, **_full_from_gathered(AG_LATE, wall, dist.shard_shapes)}
    wqkv = jnp.concatenate([w[("swa_w_q", None)], w[("kv_w_shared", None)]], axis=1)
    wqkv_t = wqkv.T
    wo_s = w[("swa_w_o", None)]
    y0 = _mm(o0, w_o, "nn", "mla_out")
    x1b, xh1, r1 = _add_ln(x, y0, lnp("ln_mix_g", 0), lnp("ln_mix_b", 0), "ln_mix_0")
    a0, f0 = _mlp_fwd(x1b, w[("mlp_w_up", 0)], w[("mlp_w_down", 0)], 0)
    x2b, xh2, r2 = _add_ln(xh1, f0, lnp("ln_mlp_g", 0), lnp("ln_mlp_b", 0), "ln_mlp_0",
                           res_affine=(lnp("ln_mix_g", 0), lnp("ln_mix_b", 0)))
    bias = _bias_build(w["rel_bias"], bkt)
    qkv_t = _mm(x2b, wqkv, "nn", "swa_qkv", out_dtype=BF16, out_t=True)
    os_t, lse1 = _swa_fwd(qkv_t, bias, sinks, qb)
    y1 = _mm(os_t, wo_s, "tn", "swa_out")
    x3b, xh3, r3 = _add_ln(xh2, y1, lnp("ln_mix_g", 1), lnp("ln_mix_b", 1), "ln_mix_1",
                           res_affine=(lnp("ln_mlp_g", 0), lnp("ln_mlp_b", 0)))
    a1, f1 = _mlp_fwd(x3b, w[("mlp_w_up", 1)], w[("mlp_w_down", 1)], 1)
    _, xh4, r4 = _add_ln(xh3, f1, lnp("ln_mlp_g", 1), lnp("ln_mlp_b", 1), "ln_mlp_1",
                         res_affine=(lnp("ln_mix_g", 1), lnp("ln_mix_b", 1)))

    g = {}
    dz4, dz4b, dg_mlp1, db_mlp1, lpart = _ln_bwd(target, xh4, r4, lnp("ln_mlp_g", 1), "ln_mlp_1_bwd",
                                                 loss_b=lnp("ln_mlp_b", 1))
    dx3, g["mlp1"] = _mlp_bwd(dz4, dz4b, x3b, a1, w[("mlp_w_up", 1)], w[("mlp_w_down", 1)], 1)
    dz3, dz3b, dg_mix1, db_mix1 = _ln_bwd(dx3, xh3, r3, lnp("ln_mix_g", 1), "ln_mix_1_bwd")
    dos_t = _mm(dz3b, wo_s, "nt", "swa_out_dx", out_t=True)
    g[("swa_w_o", None)] = _mm(os_t, dz3b, "nn", "swa_out_dw")
    dqkv_t, dbias, dsk = _swa_bwd(qkv_t, dos_t, os_t, lse1, bias, sinks, qb)
    dwqkv = _mm(dqkv_t, x2b, "nn", "swa_qkv_dw").T
    g[("swa_w_q", None)], g[("kv_w_shared", None)] = dwqkv[:, :QH * HD], dwqkv[:, QH * HD:]
    dx2 = _mm(dqkv_t, wqkv_t, "tn", "swa_qkv_dx", addend=dz3, add_scale=ALPHA)
    g["rel_bias"] = jnp.sum(_bias_bwd(dbias, bkt), axis=-1).reshape(NBKT, QH)
    g["swa_sinks"] = jnp.sum(dsk, axis=-1).reshape(1, QH)
    dz2, dz2b, dg_mlp0, db_mlp0 = _ln_bwd(dx2, xh2, r2, lnp("ln_mlp_g", 0), "ln_mlp_0_bwd")
    dx1, g["mlp0"] = _mlp_bwd(dz2, dz2b, x1b, a0, w[("mlp_w_up", 0)], w[("mlp_w_down", 0)], 0)
    dz1, dz1b, dg_mix0, db_mix0 = _ln_bwd(dx1, xh1, r1, lnp("ln_mix_g", 0), "ln_mix_0_bwd")
    do0 = _mm(dz1b, w_o, "nt", "mla_out_dx", out_dtype=BF16)
    g[("mla_w_o", None)] = _mm(o0, dz1b, "tn", "mla_out_dw")
    do_lat, dwuv, delta_t = _o_up_bwd(do0, o_lat, wuv_h)
    g[("mla_w_uv", None)] = dwuv.transpose(1, 0, 2)
    kc_t = kc.reshape(t // bq, bq, KD).transpose(0, 2, 1)
    if dist is None:
        dk, ds_all = _flash_dkv(qcat, kc, do_lat, lse0_t, delta_t, bq, hb)
        dq2, dwuk = _flash_dq(ds_all, kc_t, q2, wuk_h, cos, sin, bq)
    else:
        g["mid"] = _grad_shards(RS_MID, g).astype(BF16)
        dk, ds_all, slots1, slots_mid = _flash_dkv(qcat, kc, do_lat, lse0_t, delta_t, bq, hb,
                                                  exchange=(g["mlp1"], g["mid"]))
        dq2, dwuk, slots0 = _flash_dq(ds_all, kc_t, q2, wuk_h, cos, sin, bq, exchange=g["mlp0"])
        for key, slots in (("mlp1", slots1), ("mid", slots_mid), ("mlp0", slots0)):
            reduced[key] = _reduce_scatter_finish(slots, g[key], dist.pos, key)
    g[("mla_w_uk", None)] = dwuk.transpose(2, 0, 1)
    dcq = _mm(dq2, wq2, "nt", "mla_uq_dx")
    dwq2 = _mm(cq, dq2, "tn", "mla_uq_dw")
    g[("mla_w_uq", None)] = jnp.concatenate([dwq2[:, :H * NOPE].reshape(QR, H, NOPE),
                                             dwq2[:, H * NOPE:].reshape(QR, H, RP)[:, :, :ROPE]], axis=2)
    dh, dgq, dgkv = _mla_pre_bwd(hh, dcq, dk, w["mla_g_q"], w["mla_g_kv"], cos, sin)
    g[("mla_w_in", None)] = _mm(x, dh, "tn", "mla_in_dw")[:, :QR + KVR + ROPE]
    if dist is None:
        grad_x = _mm(dh, w_in, "nt", "mla_in_dx", addend=dz1, add_scale=ALPHA)
    else:
        g["end"] = _grad_shards(RS_END, g).astype(BF16)
        grad_x, slots_end = _mm(dh, w_in, "nt", "mla_in_dx", addend=dz1, add_scale=ALPHA, exchange=g["end"])
        reduced["end"] = _reduce_scatter_finish(slots_end, g["end"], dist.pos, "end")
    g["mla_g_q"], g["mla_g_kv"] = dgq, dgkv
    g["ln_mix_g"] = jnp.concatenate([dg_mix0, dg_mix1], axis=0)
    g["ln_mix_b"] = jnp.concatenate([db_mix0, db_mix1], axis=0)
    g["ln_mlp_g"] = jnp.concatenate([dg_mlp0, dg_mlp1], axis=0)
    g["ln_mlp_b"] = jnp.concatenate([db_mlp0, db_mlp1], axis=0)
    return lpart, grad_x, g, reduced


def _rows(a):
    return a.reshape(-1, D)


def _piece(a, layer):
    return _rows(a if layer is None else a[layer])


def _pack_group(group, parts):
    return jnp.concatenate([_piece(parts[n], l) for n, l in group], axis=0)


def _unpack_group(group, buf, like):
    out, off = {}, 0
    for n, l in group:
        shp = like[n].shape if l is None else like[n].shape[1:]
        out[(n, l)] = buf[off:off + ROWS[n]].reshape(shp)
        off += ROWS[n]
    return out


def _by_name(pieces):
    out = {n: a for (n, l), a in pieces.items() if l is None}
    for n in {n for (n, l) in pieces if l is not None}:
        out[n] = jnp.stack([pieces[(n, 0)], pieces[(n, 1)]])
    return out


def _full_from_gathered(group, wall, shard_shapes):
    out, off = {}, 0
    for n, l in group:
        shp = tuple(shard_shapes[n])
        if n in ("mlp_w_up", "mlp_w_down"):
            out[(n, l)] = (wall, off)
        elif n == "kv_w_shared":
            out[(n, l)] = wall[:, off:off + ROWS[n]].reshape((4 * shp[0],) + shp[1:])
        else:
            out[(n, l)] = wall[:, off:off + ROWS[n]].reshape((4 * shp[1],) + shp[2:])
        off += ROWS[n]
    return out


def _grad_shards(group, g):
    return jnp.concatenate([g[(n, l)].reshape(4, ROWS[n], D) for n, l in group], axis=1)


SMALL = (("ln_mix_g", 0, 2), ("ln_mix_b", 2, 2), ("ln_mlp_g", 4, 2), ("ln_mlp_b", 6, 2),
         ("swa_sinks", 8, 1), ("mla_g_q", 9, 1), ("mla_g_kv", 10, 1), ("rel_bias", 11, 1))
LOSS_ROW = 12


def _pack_small(parts, extra_row=None):
    rows = []
    for n, _, nr in SMALL:
        a = parts[n].reshape(nr, -1).astype(F32)
        rows.append(jnp.pad(a, ((0, 0), (0, D - a.shape[1]))))
    if extra_row is not None:
        rows.append(extra_row)
    rows.append(jnp.zeros((SMALL_ROWS - sum(r.shape[0] for r in rows), D), F32))
    return jnp.concatenate(rows, axis=0)


def _unpack_small(buf, like):
    out = {}
    for n, r0, nr in SMALL:
        size = like[n].size // nr
        out[n] = buf[r0:r0 + nr, :size].reshape(like[n].shape)
    return out


def kernel(x, mla_w_in, mla_g_q, mla_g_kv, mla_w_uq, mla_w_uk, mla_w_uv, mla_w_o, kv_w_shared, swa_w_q, swa_sinks, swa_w_o, rel_bias, mlp_w_up, mlp_w_down, ln_mix_g, ln_mix_b, ln_mlp_g, ln_mlp_b, loss_target, m_mla_w_in, m_mla_g_q, m_mla_g_kv, m_mla_w_uq, m_mla_w_uk, m_mla_w_uv, m_mla_w_o, m_kv_w_shared, m_swa_w_q, m_swa_sinks, m_swa_w_o, m_rel_bias, m_mlp_w_up, m_mlp_w_down, m_ln_mix_g, m_ln_mix_b, m_ln_mlp_g, m_ln_mlp_b, v_mla_w_in, v_mla_g_q, v_mla_g_kv, v_mla_w_uq, v_mla_w_uk, v_mla_w_uv, v_mla_w_o, v_kv_w_shared, v_swa_w_q, v_swa_sinks, v_swa_w_o, v_rel_bias, v_mlp_w_up, v_mlp_w_down, v_ln_mix_g, v_ln_mix_b, v_ln_mlp_g, v_ln_mlp_b):
    names = ["mla_w_in", "mla_g_q", "mla_g_kv", "mla_w_uq", "mla_w_uk", "mla_w_uv", "mla_w_o", "kv_w_shared",
             "swa_w_q", "swa_sinks", "swa_w_o", "rel_bias", "mlp_w_up", "mlp_w_down",
             "ln_mix_g", "ln_mix_b", "ln_mlp_g", "ln_mlp_b"]
    ws = dict(zip(names, [mla_w_in, mla_g_q, mla_g_kv, mla_w_uq, mla_w_uk, mla_w_uv, mla_w_o, kv_w_shared,
                          swa_w_q, swa_sinks, swa_w_o, rel_bias, mlp_w_up, mlp_w_down,
                          ln_mix_g, ln_mix_b, ln_mlp_g, ln_mlp_b]))
    ms = dict(zip(names, [m_mla_w_in, m_mla_g_q, m_mla_g_kv, m_mla_w_uq, m_mla_w_uk, m_mla_w_uv, m_mla_w_o,
                          m_kv_w_shared, m_swa_w_q, m_swa_sinks, m_swa_w_o, m_rel_bias, m_mlp_w_up, m_mlp_w_down,
                          m_ln_mix_g, m_ln_mix_b, m_ln_mlp_g, m_ln_mlp_b]))
    vs = dict(zip(names, [v_mla_w_in, v_mla_g_q, v_mla_g_kv, v_mla_w_uq, v_mla_w_uk, v_mla_w_uv, v_mla_w_o,
                          v_kv_w_shared, v_swa_w_q, v_swa_sinks, v_swa_w_o, v_rel_bias, v_mlp_w_up, v_mlp_w_down,
                          v_ln_mix_g, v_ln_mix_b, v_ln_mlp_g, v_ln_mlp_b]))
    xi, yi, ci = _mesh_pos()
    shard = 2 * xi + yi
    shard_shapes = {n: ws[n].shape for n in ROWS}
    wbf = {n: ws[n].astype(BF16) for n in ROWS}

    gains = jnp.concatenate([mla_g_q.reshape(-1), mla_g_kv.reshape(-1)])
    pieces = []
    for _ in range(3):
        head = lax.reduce_precision(gains, exponent_bits=8, mantissa_bits=7)
        pieces.append(head.astype(BF16))
        gains = gains - head
    ng = (QR + KVR) // 4
    gain_rows = jnp.pad(jnp.concatenate(pieces).reshape(1, 3 * ng), ((0, GAIN_ROWS - 1), (0, D - 3 * ng)))
    early = jnp.concatenate([_pack_group(AG_EARLY, wbf), gain_rows], axis=0)
    wall = lax.dynamic_update_slice(_allgather_weights(early), early[None], (shard, 0, 0))
    w = _full_from_gathered(AG_EARLY, wall, shard_shapes)
    gp = wall[:, early.shape[0] - GAIN_ROWS, :3 * ng].astype(F32).reshape(4, 3, ng)
    gains = (gp[:, 0] + gp[:, 1]) + gp[:, 2]
    w["mla_g_q"], w["mla_g_kv"] = gains[:, :QR // 4].reshape(QR), gains[:, QR // 4:].reshape(KVR)
    dist = _Dist(shard=shard, pos=jnp.stack([shard, ci, 2 * shard + ci]).astype(jnp.int32),
                 late_pack=_pack_group(AG_LATE, wbf), shard_shapes=shard_shapes)
    for n in ("swa_sinks", "rel_bias", "ln_mix_g", "ln_mix_b", "ln_mlp_g", "ln_mlp_b"):
        w[n] = ws[n]

    lpart, grad_x, g, reduced = _fwd_bwd(x[0], loss_target[0], w, dist)
    reduced["rest"] = jnp.concatenate([reduced["mid"], reduced["end"]], axis=0)

    small_like = {n: g[n] for n, _, _ in SMALL}
    small_sum = _allreduce_small(_pack_small(g, extra_row=lpart), "allreduce_small_grads")
    loss = 0.5 * jnp.sum(small_sum[LOSS_ROW]) / D
    gsm = _unpack_small(small_sum, small_like)
    gsm["mla_g_q"] = lax.dynamic_slice(gsm["mla_g_q"], (0, shard * (QR // 4)), (1, QR // 4))
    gsm["mla_g_kv"] = lax.dynamic_slice(gsm["mla_g_kv"], (0, shard * (KVR // 4)), (1, KVR // 4))

    gbig, dbig, mbig, vbig = {}, {}, {}, {}
    for n in ("mlp_w_up", "mlp_w_down"):
        off = 0 if n == "mlp_w_up" else ROWS["mlp_w_up"]
        gbig[n], dbig[n], mbig[n], vbig[n] = _adamw_layers(
            ws[n], ms[n], vs[n], reduced["mlp0"], reduced["mlp1"], off, f"adamw_{n}")
    rest = RS_MID + RS_END
    outs = _adamw(_pack_group(rest, ws), reduced["rest"], _pack_group(rest, ms), _pack_group(rest, vs),
                  "adamw_rest", tm=_row_tile(reduced["rest"].shape[0]))
    for dst, buf in zip((gbig, dbig, mbig, vbig), (reduced["rest"], *outs)):
        dst.update(_by_name(_unpack_group(rest, buf, ws)))
    dsm, msm, vsm = _adamw(_pack_small(ws), _pack_small(gsm), _pack_small(ms), _pack_small(vs), "adamw_small", tm=16)
    grads = {**gbig, **gsm}
    delta = {**dbig, **_unpack_small(dsm, ws)}
    new_m = {**mbig, **_unpack_small(msm, ws)}
    new_v = {**vbig, **_unpack_small(vsm, ws)}
    grads = {n: grads[n].reshape(ws[n].shape) for n in names}
    return (loss, grad_x[None], *[grads[n] for n in names], *[delta[n] for n in names],
            *[new_m[n] for n in names], *[new_v[n] for n in names])
```

```python
import collections
import math

import numpy as np
import jax
import jax.numpy as jnp
from jax import lax
from jax.experimental import pallas as pl
from jax.experimental.pallas import tpu as pltpu

F32 = jnp.float32
BF16 = jnp.bfloat16
MESH = pl.DeviceIdType.MESH

D = 1024
H = 8
NOPE = 128
ROPE = 64
QR = 384
KVR = 256
RP = 128
KD = KVR + RP
HW = 768
QH = 16
KVH = 4
HD = 64
G = QH // KVH
WIN = 128
NBKT = 32
ALPHA = 4.0 ** 0.25
LN_EPS = 1e-5
RMS_EPS = 1e-6
MLA_SCALE = (NOPE + ROPE) ** -0.5
LOG2E = 1.4426950408889634
LN2 = 0.6931471805599453
QSCALE = MLA_SCALE * LOG2E
AHEAD = 1
SWA_SCALE = HD ** -0.5
NEG = -1e30
LR, B1, B2, ADAM_EPS, WD, STEP = 0.001, 0.9, 0.999, 1e-8, 0.01, 10

VMEM_LIMIT = 48 * 1024 * 1024

NN = (((1,), (0,)), ((), ()))
NT = (((1,), (1,)), ((), ()))
TN = (((0,), (0,)), ((), ()))

ROWS = {"mlp_w_up": 1024, "mlp_w_down": 1024, "mla_w_o": 256, "swa_w_q": 256, "swa_w_o": 256,
        "kv_w_shared": 128, "mla_w_in": 176, "mla_w_uq": 144, "mla_w_uk": 64, "mla_w_uv": 64}
AG_EARLY = (("mla_w_in", None), ("mla_w_uq", None), ("mla_w_uk", None), ("mla_w_uv", None), ("mla_w_o", None))
AG_LATE = (("mlp_w_up", 0), ("mlp_w_up", 1), ("mlp_w_down", 0), ("mlp_w_down", 1),
           ("swa_w_q", None), ("swa_w_o", None), ("kv_w_shared", None))
RS_MID = (("mla_w_o", None), ("swa_w_q", None), ("swa_w_o", None), ("kv_w_shared", None), ("mla_w_uv", None))
RS_END = (("mla_w_in", None), ("mla_w_uq", None), ("mla_w_uk", None))
SMALL_ROWS = 16
GAIN_ROWS = 32
_Dist =collections.namedtuple("_Dist", "shard pos late_pack shard_shapes")


def _cp(**kw):
    return pltpu.CompilerParams(vmem_limit_bytes=VMEM_LIMIT, **kw)


def _tile(n, pref):
    t = min(n, pref)
    while n % t:
        t -= 128
    return t


def _dot(a, b, dims):
    return lax.dot_general(a, b, dims, preferred_element_type=F32)


def _mm(a, b, mode, name, out_dtype=F32, out_t=False, addend=None, add_scale=1.0, relu2=False, gate_a=None,
        b_view=None, out_view=None, exchange=None, tm=1024, tn=1024, tk=1024):
    blk = 1024
    if b_view is not None:
        kind, b_off = b_view
        assert b.shape[0] == 4 and b.shape[2] == blk and b_off % blk == 0
        bshape = {("cols", "nn"): (blk, 4 * blk), ("cols", "nt"): (blk, 4 * blk),
                  ("rows", "nn"): (4 * blk, blk), ("rows", "nt"): (4 * blk, blk)}[(kind, mode)]
    else:
        bshape = b.shape
    if mode == "nn":
        (m, k), (k2, n) = a.shape, bshape
    elif mode == "nt":
        (m, k), (n, k2) = a.shape, bshape
    else:
        (k, m), (k2, n) = a.shape, bshape
    assert k == k2, (name, a.shape, b.shape)
    tm, tn, tk = _tile(m, tm), _tile(n, tn), _tile(k, tk)
    nk = k // tk
    dims = {"nn": NN, "nt": NT, "tn": TN}[mode]
    if mode == "tn":
        a_spec = pl.BlockSpec((tk, tm), lambda i, j, kk: (kk, i))
    else:
        a_spec = pl.BlockSpec((tm, tk), lambda i, j, kk: (i, kk))
    if b_view is not None:
        assert tn == blk and tk == blk
        ob = b_off // blk
        b_spec = {("cols", "nn"): pl.BlockSpec((None, tk, tn), lambda i, j, kk: (j, ob, 0)),
                  ("cols", "nt"): pl.BlockSpec((None, tn, tk), lambda i, j, kk: (kk, ob, 0)),
                  ("rows", "nn"): pl.BlockSpec((None, tk, tn), lambda i, j, kk: (kk, ob, 0)),
                  ("rows", "nt"): pl.BlockSpec((None, tn, tk), lambda i, j, kk: (j, ob, 0))}[(kind, mode)]
    elif mode == "nt":
        b_spec = pl.BlockSpec((tn, tk), lambda i, j, kk: (j, kk))
    else:
        b_spec = pl.BlockSpec((tk, tn), lambda i, j, kk: (kk, j))
    mn_spec = pl.BlockSpec((tm, tn), lambda i, j, kk: (i, j))
    ins, in_specs = [a, b], [a_spec, b_spec]
    if addend is not None:
        ins.append(addend)
        in_specs.append(mn_spec)
    if gate_a is not None:
        ins.append(gate_a)
        in_specs.append(mn_spec)
    aliases = {}
    if out_view is not None:
        okind, total_rows, o_off, buf = out_view
        assert not out_t and tm == blk and tn == blk and o_off % blk == 0
        oo = o_off // blk
        out_shape = [jax.ShapeDtypeStruct((4, total_rows, blk), out_dtype)]
        if okind == "cols":
            out_specs = [pl.BlockSpec((None, tm, tn), lambda i, j, kk: (j, oo, 0))]
        else:
            out_specs = [pl.BlockSpec((None, tm, tn), lambda i, j, kk: (i, oo, 0))]
        if buf is not None:
            aliases = {len(ins): 0}
            ins.append(buf)
            in_specs.append(pl.BlockSpec(memory_space=pl.ANY))
    elif out_t:
        out_shape = [jax.ShapeDtypeStruct((n, m), out_dtype)]
        out_specs = [pl.BlockSpec((tn, tm), lambda i, j, kk: (j, i))]
    else:
        out_shape = [jax.ShapeDtypeStruct((m, n), out_dtype)]
        out_specs = [mn_spec]
    has_add, has_gate = addend is not None, gate_a is not None
    hosting = exchange is not None
    scratch = [pltpu.VMEM((tm, tn), F32)] if nk > 1 else []
    if hosting:
        assert not aliases
        ins.append(exchange)
        in_specs.append(pl.BlockSpec(memory_space=pl.ANY))
        out_shape.append(jax.ShapeDtypeStruct((8, exchange.shape[1] // 2, D), exchange.dtype))
        out_specs.append(pl.BlockSpec(memory_space=pl.ANY))
        scratch = scratch + ALL_SEMS
    steps = (m // tm, n // tn, nk)

    def kern(*refs):
        a_ref, b_ref = refs[0], refs[1]
        pos = 2
        add_ref = gate_ref = None
        if has_add:
            add_ref = refs[pos]
            pos += 1
        if has_gate:
            gate_ref = refs[pos]
            pos += 1
        pos += len(aliases)
        if hosting:
            xc_start, xc_finish = _device_exchange_schedule(refs[pos], refs[pos + 2], refs[-2], refs[-1])
            pos += 1
        o_ref = refs[pos]
        acc = refs[pos + 1 + hosting] if nk > 1 else None
        kk = pl.program_id(2)
        if hosting:
            lin = (pl.program_id(0) * steps[1] + pl.program_id(1)) * steps[2] + kk

            @pl.when(lin == 0)
            def _():
                xc_start()

        def partial():
            return _dot(a_ref[...].astype(BF16), b_ref[...].astype(BF16), dims)

        if nk > 1:
            @pl.when(kk == 0)
            def _():
                acc[...] = partial()

            @pl.when((kk > 0) & (kk < nk - 1))
            def _():
                acc[...] += partial()

        @pl.when(kk == nk - 1)
        def _():
            r = partial() + acc[...] if nk > 1 else partial()
            if has_add:
                r = r + add_scale * add_ref[...].astype(F32)
            if has_gate:
                ga = gate_ref[...].astype(F32)
                r = r * jnp.where(ga > 0.0, (2.0 * ga) * lax.rsqrt(ga), 0.0)
            if relu2:
                hh = jnp.maximum(r, 0.0)
                r = hh * hh
            if out_t:
                r = r.T
            o_ref[...] = r.astype(out_dtype)

        if hosting:
            @pl.when(lin == steps[0] * steps[1] * steps[2] - 1)
            def _():
                xc_finish()

    outs = pl.pallas_call(
        kern, out_shape=out_shape, grid=steps, in_specs=in_specs, out_specs=out_specs,
        scratch_shapes=scratch, input_output_aliases=aliases, name=name, compiler_params=_cp())(*ins)
    return outs if hosting else outs[0]


def _add_ln(res, y, g, b, name, res_affine=None, tm=1024):
    t = res.shape[0]
    tm = min(tm, t)
    affine = res_affine is not None

    def kern(*refs):
        if affine:
            x_ref, y_ref, g_ref, b_ref, g0_ref, b0_ref, ob_ref, xh_ref, r_ref = refs
            x = x_ref[...] * g0_ref[...] + b0_ref[...]
        else:
            x_ref, y_ref, g_ref, b_ref, ob_ref, xh_ref, r_ref = refs
            x = x_ref[...]
        z = ALPHA * x + y_ref[...]
        mu = jnp.mean(z, axis=-1, keepdims=True)
        zc = z - mu
        var = jnp.mean(zc * zc, axis=-1, keepdims=True)
        r = lax.rsqrt(var + LN_EPS)
        xh = zc * r
        ob_ref[...] = (xh * g_ref[...] + b_ref[...]).astype(BF16)
        xh_ref[...] = xh
        r_ref[...] = r

    row = pl.BlockSpec((tm, D), lambda i: (i, 0))
    vec = pl.BlockSpec((1, D), lambda i: (0, 0))
    st = pl.BlockSpec((tm, 1), lambda i: (i, 0))
    ins = [res, y, g.reshape(1, D), b.reshape(1, D)]
    if affine:
        ins += [res_affine[0].reshape(1, D), res_affine[1].reshape(1, D)]
    return pl.pallas_call(
        kern, grid=(t // tm,), in_specs=[row, row] + [vec] * (len(ins) - 2), out_specs=[row, row, st],
        out_shape=[jax.ShapeDtypeStruct((t, D), BF16), jax.ShapeDtypeStruct((t, D), F32),
                   jax.ShapeDtypeStruct((t, 1), F32)],
        name=name, compiler_params=_cp())(*ins)


def _ln_bwd(dout, xhat, rstd, g, name, loss_b=None, tm=1024):
    t = dout.shape[0]
    tm = min(tm, t)
    head = loss_b is not None

    def kern(*refs):
        if head:
            do_ref, xh_ref, r_ref, g_ref, b_ref, dz_ref, dzb_ref, dg_ref, db_ref, l_ref = refs
        else:
            do_ref, xh_ref, r_ref, g_ref, dz_ref, dzb_ref, dg_ref, db_ref = refs

        @pl.when(pl.program_id(0) == 0)
        def _():
            dg_ref[...] = jnp.zeros_like(dg_ref)
            db_ref[...] = jnp.zeros_like(db_ref)
            if head:
                l_ref[...] = jnp.zeros_like(l_ref)

        xh = xh_ref[...]
        if head:
            e = xh * g_ref[...] + b_ref[...] - do_ref[...]
            l_ref[...] += jnp.sum(e * e, axis=0, keepdims=True)
            do = e * (1.0 / D)
        else:
            do = do_ref[...]
        dxh = do * g_ref[...]
        m1 = jnp.mean(dxh, axis=-1, keepdims=True)
        m2 = jnp.mean(dxh * xh, axis=-1, keepdims=True)
        dz = r_ref[...] * (dxh - m1 - xh * m2)
        dz_ref[...] = dz
        dzb_ref[...] = dz.astype(BF16)
        dg_ref[...] += jnp.sum(do * xh, axis=0, keepdims=True)
        db_ref[...] += jnp.sum(do, axis=0, keepdims=True)

    row = pl.BlockSpec((tm, D), lambda i: (i, 0))
    vec = pl.BlockSpec((1, D), lambda i: (0, 0))
    st = pl.BlockSpec((tm, 1), lambda i: (i, 0))
    ins = [dout, xhat, rstd, g.reshape(1, D)] + ([loss_b.reshape(1, D)] if head else [])
    return pl.pallas_call(
        kern, grid=(t // tm,), in_specs=[row, row, st] + [vec] * (len(ins) - 3),
        out_specs=[row, row, vec, vec] + ([vec] if head else []),
        out_shape=[jax.ShapeDtypeStruct((t, D), F32), jax.ShapeDtypeStruct((t, D), BF16)]
        + [jax.ShapeDtypeStruct((1, D), F32)] * (3 if head else 2),
        name=name, compiler_params=_cp())(*ins)


def _rope_tables(t):
    half = ROPE // 2
    inv = 10000.0 ** (-jnp.arange(half, dtype=F32) / half)
    ang = jnp.arange(t).astype(F32)[:, None] * inv[None, :]
    cos, sin = jnp.cos(ang), jnp.sin(ang)
    z = jnp.zeros((t, RP - ROPE), F32)
    return jnp.concatenate([cos, cos, z], axis=1), jnp.concatenate([-sin, sin, z], axis=1)


def _swap_halves(x):
    lane = lax.broadcasted_iota(jnp.int32, x.shape, 1)
    return jnp.where(lane < ROPE // 2, pltpu.roll(x, RP - ROPE // 2, 1), pltpu.roll(x, ROPE // 2, 1))


def _rope(x, cos, sin):
    return x * cos + _swap_halves(x) * sin


def _rope_t(gy, cos, sin):
    return gy * cos + _swap_halves(gy * sin)


def _mla_pre(hh, g_q, g_kv, cos, sin, tm=1024):
    t = hh.shape[0]
    tm = min(tm, t)

    def kern(h_ref, gq_ref, gkv_ref, c_ref, s_ref, cq_ref, k_ref):
        xq = h_ref[:, 0:QR]
        rq = lax.rsqrt(jnp.mean(xq * xq, axis=-1, keepdims=True) + RMS_EPS)
        cq_ref[...] = (xq * rq * gq_ref[...]).astype(BF16)
        xk = h_ref[:, QR:QR + KVR]
        rk = lax.rsqrt(jnp.mean(xk * xk, axis=-1, keepdims=True) + RMS_EPS)
        k_ref[:, 0:KVR] = (xk * rk * gkv_ref[...]).astype(BF16)
        k_ref[:, KVR:KD] = _rope(h_ref[:, QR + KVR:HW], c_ref[...], s_ref[...]).astype(BF16)

    return pl.pallas_call(
        kern, grid=(t // tm,),
        in_specs=[pl.BlockSpec((tm, HW), lambda i: (i, 0)), pl.BlockSpec((1, QR), lambda i: (0, 0)),
                  pl.BlockSpec((1, KVR), lambda i: (0, 0)), pl.BlockSpec((tm, RP), lambda i: (i, 0)),
                  pl.BlockSpec((tm, RP), lambda i: (i, 0))],
        out_specs=[pl.BlockSpec((tm, QR), lambda i: (i, 0)), pl.BlockSpec((tm, KD), lambda i: (i, 0))],
        out_shape=[jax.ShapeDtypeStruct((t, QR), BF16), jax.ShapeDtypeStruct((t, KD), BF16)],
        name="mla_pre", compiler_params=_cp())(hh, g_q.reshape(1, QR), g_kv.reshape(1, KVR), cos, sin)


def _mla_pre_bwd(hh, dcq, dk, g_q, g_kv, cos, sin, tm=1024):
    t = hh.shape[0]
    tm = min(tm, t)

    def rms_bwd(x, dy, g):
        r = lax.rsqrt(jnp.mean(x * x, axis=-1, keepdims=True) + RMS_EPS)
        gdy = dy * g
        dx = r * gdy - x * (r * r * r) * jnp.mean(gdy * x, axis=-1, keepdims=True)
        return dx, jnp.sum(dy * x * r, axis=0, keepdims=True)

    def kern(h_ref, dcq_ref, dk_ref, gq_ref, gkv_ref, c_ref, s_ref, dh_ref, dgq_ref, dgkv_ref):
        @pl.when(pl.program_id(0) == 0)
        def _():
            dgq_ref[...] = jnp.zeros_like(dgq_ref)
            dgkv_ref[...] = jnp.zeros_like(dgkv_ref)

        dxq, dgq = rms_bwd(h_ref[:, 0:QR], dcq_ref[...], gq_ref[...])
        dxk, dgk = rms_bwd(h_ref[:, QR:QR + KVR], dk_ref[:, 0:KVR], gkv_ref[...])
        dh_ref[:, 0:QR] = dxq.astype(BF16)
        dh_ref[:, QR:QR + KVR] = dxk.astype(BF16)
        dh_ref[:, QR + KVR:HW] = _rope_t(dk_ref[:, KVR:KD], c_ref[...], s_ref[...]).astype(BF16)
        dgq_ref[...] += dgq
        dgkv_ref[...] += dgk

    return pl.pallas_call(
        kern, grid=(t // tm,),
        in_specs=[pl.BlockSpec((tm, HW), lambda i: (i, 0)), pl.BlockSpec((tm, QR), lambda i: (i, 0)),
                  pl.BlockSpec((tm, KD), lambda i: (i, 0)), pl.BlockSpec((1, QR), lambda i: (0, 0)),
                  pl.BlockSpec((1, KVR), lambda i: (0, 0)), pl.BlockSpec((tm, RP), lambda i: (i, 0)),
                  pl.BlockSpec((tm, RP), lambda i: (i, 0))],
        out_specs=[pl.BlockSpec((tm, HW), lambda i: (i, 0)), pl.BlockSpec((1, QR), lambda i: (0, 0)),
                   pl.BlockSpec((1, KVR), lambda i: (0, 0))],
        out_shape=[jax.ShapeDtypeStruct((t, HW), BF16), jax.ShapeDtypeStruct((1, QR), F32),
                   jax.ShapeDtypeStruct((1, KVR), F32)],
        name="mla_pre_bwd", compiler_params=_cp())(hh, dcq, dk, g_q.reshape(1, QR), g_kv.reshape(1, KVR), cos, sin)


def _q_prep(q2, wuk_t, cos, sin, tm=1024):
    t = q2.shape[0]
    tm = min(tm, t)

    def kern(q_ref, w_ref, c_ref, s_ref, o_ref):
        cos_, sin_ = c_ref[...], s_ref[...]
        for h in range(H):
            qn = q_ref[:, h * NOPE:(h + 1) * NOPE].astype(BF16)
            o_ref[:, h * KD:h * KD + KVR] = (_dot(qn, w_ref[h], NN) * QSCALE).astype(BF16)
            qr = q_ref[:, H * NOPE + h * RP:H * NOPE + (h + 1) * RP]
            o_ref[:, h * KD + KVR:(h + 1) * KD] = (_rope(qr, cos_, sin_) * QSCALE).astype(BF16)

    return pl.pallas_call(
        kern, grid=(t // tm,),
        in_specs=[pl.BlockSpec((tm, 2 * H * NOPE), lambda i: (i, 0)), pl.BlockSpec((H, NOPE, KVR), lambda i: (0, 0, 0)),
                  pl.BlockSpec((tm, RP), lambda i: (i, 0)), pl.BlockSpec((tm, RP), lambda i: (i, 0))],
        out_specs=pl.BlockSpec((tm, H * KD), lambda i: (i, 0)),
        out_shape=jax.ShapeDtypeStruct((t, H * KD), BF16),
        name="q_prep", compiler_params=_cp())(q2, wuk_t, cos, sin)


def _o_up_bwd(do, o_lat, wuv_h, tm=1024):
    t = do.shape[0]
    tm = min(tm, t)

    def kern(do_ref, x_ref, w_ref, dx_ref, dw_ref, dlt_ref):
        @pl.when(pl.program_id(0) == 0)
        def _():
            dw_ref[...] = jnp.zeros_like(dw_ref)

        for h in range(H):
            dh_ = do_ref[:, h * NOPE:(h + 1) * NOPE]
            x = x_ref[:, h * KVR:(h + 1) * KVR]
            dx = _dot(dh_, w_ref[h], NT)
            dx_ref[:, h * KVR:(h + 1) * KVR] = dx.astype(BF16)
            dw_ref[h] += _dot(x.astype(BF16), dh_, TN)
            dl = jnp.broadcast_to(jnp.sum(dx * x, axis=1)[:, None], (tm, 128))
            dlt_ref[h] = dl.T[0:1, :]

    return pl.pallas_call(
        kern, grid=(t // tm,),
        in_specs=[pl.BlockSpec((tm, H * NOPE), lambda i: (i, 0)), pl.BlockSpec((tm, H * KVR), lambda i: (i, 0)),
                  pl.BlockSpec((H, KVR, NOPE), lambda i: (0, 0, 0))],
        out_specs=[pl.BlockSpec((tm, H * KVR), lambda i: (i, 0)), pl.BlockSpec((H, KVR, NOPE), lambda i: (0, 0, 0)),
                   pl.BlockSpec((H, 1, tm), lambda i: (0, 0, i))],
        out_shape=[jax.ShapeDtypeStruct((t, H * KVR), BF16), jax.ShapeDtypeStruct((H, KVR, NOPE), F32),
                   jax.ShapeDtypeStruct((H, 1, t), F32)],
        name="o_up_bwd", compiler_params=_cp())(do, o_lat, wuv_h)


def _causal_pairs(nq):
    return [(i, j) for i in range(nq) for j in range(i + 1)]


def _lane_tile(stat, width):
    return jnp.tile(stat, (1, width // 128))


def _flash_fwd(qcat, kc, wuv_h, bq, hb, gather=None):
    t = kc.shape[0]
    nq = t // bq
    pairs = _causal_pairs(nq)
    itab = jnp.asarray(np.array([p[0] for p in pairs], np.int32))
    jtab = jnp.asarray(np.array([p[1] for p in pairs], np.int32))

    ng = H // hb
    hosting = gather is not None

    def kern(it, jt, q_ref, k_ref, wuv_ref, *rest):
        if hosting:
            w_ref, o_ref, lset_ref, oup_ref, wall_ref, m_sc, l_sc, acc_sc, send_sems, recv_sems = rest
            ag_start, ag_forward, ag_finish = _allgather_schedule(w_ref, wall_ref, send_sems, recv_sems)
        else:
            o_ref, lset_ref, oup_ref, m_sc, l_sc, acc_sc = rest
        grp = pl.program_id(0)
        st = pl.program_id(1)
        i, j = it[st], jt[st]

        if hosting:
            @pl.when((grp == 0) & (st == 0))
            def _():
                ag_start()

        @pl.when(j == 0)
        def _():
            m_sc[...] = jnp.full_like(m_sc, NEG)
            l_sc[...] = jnp.zeros_like(l_sc)
            acc_sc[...] = jnp.zeros_like(acc_sc)

        def update(masked):
            k = k_ref[...]
            v = k[:, 0:KVR]
            if masked:
                row = lax.broadcasted_iota(jnp.int32, (bq, bq), 0)
                col = lax.broadcasted_iota(jnp.int32, (bq, bq), 1)
                keep = col <= row
            pending = [_dot(q_ref[:, hh * KD:(hh + 1) * KD], k, NT) for hh in range(min(AHEAD, hb))]
            for hh in range(hb):
                s = pending.pop(0)
                if hh + AHEAD < hb:
                    pending.append(_dot(q_ref[:, (hh + AHEAD) * KD:(hh + AHEAD + 1) * KD], k, NT))
                if masked:
                    s = jnp.where(keep, s, NEG)
                m_prev = m_sc[hh]
                m_next = jnp.maximum(m_prev, jnp.max(s, axis=1)[:, None])
                p = jnp.exp2(s - _lane_tile(m_next, bq))
                a = jnp.exp2(m_prev - m_next)
                l_sc[hh] = a * l_sc[hh] + jnp.sum(p, axis=1)[:, None]
                acc_sc[hh] = _lane_tile(a, KVR) * acc_sc[hh] + _dot(p.astype(BF16), v, NN)
                m_sc[hh] = m_next

        @pl.when(j < i)
        def _():
            update(False)

        @pl.when(j == i)
        def _():
            update(True)
            for hh in range(hb):
                l = l_sc[hh]
                o_h = acc_sc[hh] / _lane_tile(l, KVR)
                o_ref[:, hh * KVR:(hh + 1) * KVR] = o_h
                oup_ref[:, hh * NOPE:(hh + 1) * NOPE] = _dot(o_h.astype(BF16), wuv_ref[hh], NN).astype(BF16)
                lset_ref[hh] = (m_sc[hh] + jnp.log2(l)).T[0:1, :]

        if hosting:
            half_way = (ng * len(pairs)) // 2

            @pl.when(grp * len(pairs) + st == half_way)
            def _():
                ag_forward()

            @pl.when((grp == ng - 1) & (st == len(pairs) - 1))
            def _():
                ag_finish()

    in_specs = [pl.BlockSpec((bq, hb * KD), lambda g, s, it, jt: (it[s], g)),
                pl.BlockSpec((bq, KD), lambda g, s, it, jt: (jt[s], 0)),
                pl.BlockSpec((hb, KVR, NOPE), lambda g, s, it, jt: (g, 0, 0))]
    out_specs = [pl.BlockSpec((bq, hb * KVR), lambda g, s, it, jt: (it[s], g)),
                 pl.BlockSpec((hb, 1, bq), lambda g, s, it, jt: (g, 0, it[s])),
                 pl.BlockSpec((bq, hb * NOPE), lambda g, s, it, jt: (it[s], g))]
    out_shape = [jax.ShapeDtypeStruct((t, H * KVR), F32), jax.ShapeDtypeStruct((H, 1, t), F32),
                 jax.ShapeDtypeStruct((t, H * NOPE), BF16)]
    scratch = [pltpu.VMEM((hb, bq, 128), F32), pltpu.VMEM((hb, bq, 128), F32), pltpu.VMEM((hb, bq, KVR), F32)]
    args = [itab, jtab, qcat, kc, wuv_h]
    if hosting:
        in_specs.append(ANY)
        out_specs.append(ANY)
        out_shape.append(jax.ShapeDtypeStruct((4,) + gather.shape, gather.dtype))
        scratch += AG_SEMS
        args.append(gather)
    gs = pltpu.PrefetchScalarGridSpec(num_scalar_prefetch=2, grid=(ng, len(pairs)), in_specs=in_specs,
                                      out_specs=out_specs, scratch_shapes=scratch)
    return pl.pallas_call(kern, grid_spec=gs, out_shape=out_shape, name="mla_flash_fwd",
                          compiler_params=_cp())(*args)


def _flash_dkv(qcat, kc, do_lat, lse_t, delta_t, bq, hb, exchange=()):
    nx = len(exchange)
    t = kc.shape[0]
    nq = t // bq
    ng = H // hb
    npairs = nq * (nq + 1) // 2
    steps = [(j, g, i) for j in range(nq) for g in range(ng) for i in range(j, nq)]
    jtab = jnp.asarray(np.array([s[0] for s in steps], np.int32))
    gtab = jnp.asarray(np.array([s[1] for s in steps], np.int32))
    itab = jnp.asarray(np.array([s[2] for s in steps], np.int32))
    ptab = jnp.asarray(np.array([s[2] * (s[2] + 1) // 2 + s[0] for s in steps], np.int32))

    def kern(jt, gt, it, pt, q_ref, k_ref, do_ref, lset_ref, dlt_ref, *rest):
        p_refs, (dk_ref, ds_ref), slots_refs = rest[:nx], rest[nx:nx + 2], rest[nx + 2:2 * nx + 2]
        dk_sc, dv_sc = rest[2 * nx + 2:2 * nx + 4]
        sems = rest[2 * nx + 4:]
        hooks = [_device_exchange_schedule(p_refs[e], slots_refs[e], sems[2 * e], sems[2 * e + 1]) for e in range(nx)]
        st = pl.program_id(0)
        j, g, i = jt[st], gt[st], it[st]

        if nx:
            @pl.when(st == 0)
            def _():
                for start, _ in hooks:
                    start()

        @pl.when((g == 0) & (i == j))
        def _():
            dk_sc[...] = jnp.zeros_like(dk_sc)
            dv_sc[...] = jnp.zeros_like(dv_sc)

        def update(masked):
            k = k_ref[...]
            v = k[:, 0:KVR]
            if masked:
                row = lax.broadcasted_iota(jnp.int32, (bq, bq), 0)
                col = lax.broadcasted_iota(jnp.int32, (bq, bq), 1)
                keep = row <= col

            def first_matmuls(hh):
                dob = do_ref[:, hh * KVR:(hh + 1) * KVR].astype(BF16)
                return _dot(k, q_ref[:, hh * KD:(hh + 1) * KD], NT), _dot(v, dob, NT), dob

            pending = [first_matmuls(hh) for hh in range(min(AHEAD, hb))]
            for hh in range(hb):
                s, dp, dob = pending.pop(0)
                if hh + AHEAD < hb:
                    pending.append(first_matmuls(hh + AHEAD))
                if masked:
                    s = jnp.where(keep, s, NEG)
                p = jnp.exp2(s - lset_ref[hh])
                dv_sc[...] += _dot(p.astype(BF16), dob, NN)
                dsb = (p * (dp - dlt_ref[hh])).astype(BF16)
                ds_ref[0, 0, hh] = dsb
                dk_sc[...] += _dot(dsb, q_ref[:, hh * KD:(hh + 1) * KD], NN)

        @pl.when(i > j)
        def _():
            update(False)

        @pl.when(i == j)
        def _():
            update(True)

        @pl.when((g == ng - 1) & (i == nq - 1))
        def _():
            dk_ref[:, 0:KVR] = dk_sc[:, 0:KVR] * LN2 + dv_sc[...]
            dk_ref[:, KVR:KD] = dk_sc[:, KVR:KD] * LN2

        if nx:
            @pl.when(st == len(steps) - 1)
            def _():
                for _, finish in hooks:
                    finish()

    in_specs = [pl.BlockSpec((bq, hb * KD), lambda s, jt, gt, it, pt: (it[s], gt[s])),
                pl.BlockSpec((bq, KD), lambda s, jt, gt, it, pt: (jt[s], 0)),
                pl.BlockSpec((bq, hb * KVR), lambda s, jt, gt, it, pt: (it[s], gt[s])),
                pl.BlockSpec((hb, 1, bq), lambda s, jt, gt, it, pt: (gt[s], 0, it[s])),
                pl.BlockSpec((hb, 1, bq), lambda s, jt, gt, it, pt: (gt[s], 0, it[s]))] + [ANY] * nx
    out_specs = [pl.BlockSpec((bq, KD), lambda s, jt, gt, it, pt: (jt[s], 0)),
                 pl.BlockSpec((1, 1, hb, bq, bq), lambda s, jt, gt, it, pt: (gt[s], pt[s], 0, 0, 0))] + [ANY] * nx
    out_shape = [jax.ShapeDtypeStruct((t, KD), F32), jax.ShapeDtypeStruct((ng, npairs, hb, bq, bq), BF16)]
    out_shape += [jax.ShapeDtypeStruct((8, e.shape[1] // 2, D), e.dtype) for e in exchange]
    scratch = [pltpu.VMEM((bq, KD), F32), pltpu.VMEM((bq, KVR), F32)] + ALL_SEMS * nx
    args = [jtab, gtab, itab, ptab, qcat, kc, do_lat, lse_t, delta_t, *exchange]
    gs = pltpu.PrefetchScalarGridSpec(num_scalar_prefetch=4, grid=(len(steps),), in_specs=in_specs,
                                      out_specs=out_specs, scratch_shapes=scratch)
    return pl.pallas_call(kern, grid_spec=gs, out_shape=out_shape, name="mla_flash_dkv",
                          compiler_params=_cp())(*args)


def _flash_dq(ds_all, kc_t, q2, wuk_h, cos, sin, bq, exchange=None):
    nq = kc_t.shape[0]
    t = nq * bq
    ngrp, _, hper = ds_all.shape[:3]
    pairs = _causal_pairs(nq)
    itab = jnp.asarray(np.array([p[0] for p in pairs], np.int32))
    jtab = jnp.asarray(np.array([p[1] for p in pairs], np.int32))
    hosting = exchange is not None

    def kern(it, jt, *refs):
        ds_refs, (kt_ref, q_ref, w_ref, c_ref, s_ref), rest = refs[:ngrp], refs[ngrp:ngrp + 5], refs[ngrp + 5:]
        if hosting:
            p_ref, dq_ref, dw_ref, slots_ref, acc_sc, send_sems, recv_sems = rest
            xc_start, xc_finish = _device_exchange_schedule(p_ref, slots_ref, send_sems, recv_sems)
        else:
            dq_ref, dw_ref, acc_sc = rest
        st = pl.program_id(0)
        i, j = it[st], jt[st]
        kt = kt_ref[...]

        def ds(hh):
            return ds_refs[hh // hper][0, 0, hh % hper]

        @pl.when(st == 0)
        def _():
            dw_ref[...] = jnp.zeros_like(dw_ref)
            if hosting:
                xc_start()

        @pl.when(j == 0)
        def _():
            for hh in range(H):
                acc_sc[hh] = _dot(kt, ds(hh), NN)

        @pl.when((j > 0) & (j < i))
        def _():
            for hh in range(H):
                acc_sc[hh] += _dot(kt, ds(hh), NN)

        @pl.when(j == i)
        def _():
            cos_, sin_ = c_ref[...], s_ref[...]
            for hh in range(H):
                tot = _dot(kt, ds(hh), NN)
                tot = jnp.where(i > 0, tot + acc_sc[hh], tot)
                dq_h = tot.T * MLA_SCALE
                dql = dq_h[:, 0:KVR].astype(BF16)
                dq_ref[:, hh * NOPE:(hh + 1) * NOPE] = _dot(dql, w_ref[hh], NN).astype(BF16)
                dq_ref[:, H * NOPE + hh * RP:H * NOPE + (hh + 1) * RP] = _rope_t(dq_h[:, KVR:KD], cos_, sin_).astype(BF16)
                dw_ref[hh] += _dot(q_ref[:, hh * NOPE:(hh + 1) * NOPE].astype(BF16), dql, TN)

        if hosting:
            @pl.when(st == len(pairs) - 1)
            def _():
                xc_finish()

    def group(gi):
        return pl.BlockSpec((1, 1, hper, bq, bq), lambda s, it, jt: (gi, s, 0, 0, 0))

    in_specs = [group(gi) for gi in range(ngrp)] + [
        pl.BlockSpec((None, KD, bq), lambda s, it, jt: (jt[s], 0, 0)),
        pl.BlockSpec((bq, 2 * H * NOPE), lambda s, it, jt: (it[s], 0)),
        pl.BlockSpec((H, KVR, NOPE), lambda s, it, jt: (0, 0, 0)),
        pl.BlockSpec((bq, RP), lambda s, it, jt: (it[s], 0)), pl.BlockSpec((bq, RP), lambda s, it, jt: (it[s], 0))]
    out_specs = [pl.BlockSpec((bq, 2 * H * NOPE), lambda s, it, jt: (it[s], 0)),
                 pl.BlockSpec((H, NOPE, KVR), lambda s, it, jt: (0, 0, 0))]
    out_shape = [jax.ShapeDtypeStruct((t, 2 * H * NOPE), BF16), jax.ShapeDtypeStruct((H, NOPE, KVR), F32)]
    scratch = [pltpu.VMEM((H, KD, bq), F32)]
    args = [itab, jtab] + [ds_all] * ngrp + [kc_t, q2, wuk_h, cos, sin]
    if hosting:
        in_specs.append(ANY)
        out_specs.append(ANY)
        out_shape.append(jax.ShapeDtypeStruct((8, exchange.shape[1] // 2, D), exchange.dtype))
        scratch += ALL_SEMS
        args.append(exchange)
    gs = pltpu.PrefetchScalarGridSpec(num_scalar_prefetch=2, grid=(len(pairs),), in_specs=in_specs,
                                      out_specs=out_specs, scratch_shapes=scratch)
    return pl.pallas_call(kern, grid_spec=gs, out_shape=out_shape, name="mla_flash_dq",
                          compiler_params=_cp())(*args)


def _bucket_table():
    d = np.arange(WIN)
    max_exact = NBKT // 2
    nf = np.maximum(d, 1).astype(np.float32)
    large = max_exact + (np.log(nf / np.float32(max_exact)) / np.float32(math.log(WIN / max_exact))
                         * np.float32(NBKT - max_exact)).astype(np.int32)
    large = np.minimum(large, NBKT - 1)
    bucket = np.where(d < max_exact, d, large).astype(np.int32)
    jj = np.arange(2 * WIN)[:, None]
    ii = np.arange(WIN)[None, :]
    dist = ii + WIN - jj
    valid = (dist >= 0) & (dist < WIN)
    return np.where(valid, bucket[np.clip(dist, 0, WIN - 1)], -1).astype(np.int32)


def _bias_build(rel_bias, bkt):
    def kern(bk_ref, rb_ref, o_ref):
        bk = bk_ref[...]
        for hd in range(QH):
            acc = jnp.full((2 * WIN, WIN), NEG, F32)
            for b in range(NBKT):
                acc = jnp.where(bk == b, rb_ref[b, hd], acc)
            o_ref[hd] = acc

    return pl.pallas_call(
        kern, in_specs=[pl.BlockSpec(memory_space=pltpu.VMEM), pl.BlockSpec(memory_space=pltpu.SMEM)],
        out_specs=pl.BlockSpec(memory_space=pltpu.VMEM),
        out_shape=jax.ShapeDtypeStruct((QH, 2 * WIN, WIN), F32), name="swa_bias_build")(bkt, rel_bias)


def _bias_bwd(dbias, bkt):
    def kern(db_ref, bk_ref, o_ref):
        bk = bk_ref[...]
        for hd in range(QH):
            g = db_ref[hd]
            for b in range(NBKT):
                r = b * QH + hd
                o_ref[r:r + 1, :] = jnp.sum(jnp.where(bk == b, g, 0.0), axis=0, keepdims=True)

    return pl.pallas_call(
        kern, in_specs=[pl.BlockSpec(memory_space=pltpu.VMEM), pl.BlockSpec(memory_space=pltpu.VMEM)],
        out_specs=pl.BlockSpec(memory_space=pltpu.VMEM),
        out_shape=jax.ShapeDtypeStruct((NBKT * QH, WIN), F32), name="swa_bias_bwd")(dbias, bkt)


def _swa_finish_scores(raw, bias, first):
    s = raw * SWA_SCALE + bias
    if first is not None:
        row = lax.broadcasted_iota(jnp.int32, s.shape, 0)
        s = jnp.where(jnp.logical_or(jnp.logical_not(first), row >= WIN), s, NEG)
    return s


def _swa_fwd(qkv_t, bias, sinks, qb):
    t = qkv_t.shape[1]
    w = qb * WIN
    nst = t // w

    def kern(q_ref, kc_ref, kp_ref, vc_ref, vp_ref, b_ref, sk_ref, o_ref, lse_ref):
        n = pl.program_id(0)
        kfull = jnp.concatenate([kp_ref[...], kc_ref[...]], axis=1)
        vfull = jnp.concatenate([vp_ref[...], vc_ref[...]], axis=1)
        head_row = lax.broadcasted_iota(jnp.int32, (QH, WIN), 0)
        groups = [(b, kh) for b in range(qb) for kh in range(KVH)]

        def raw_scores(b, kh):
            k_band = kfull[kh * HD:(kh + 1) * HD, b * WIN:(b + 2) * WIN]
            return [_dot(k_band, q_ref[(kh * G + g) * HD:(kh * G + g + 1) * HD, b * WIN:(b + 1) * WIN], TN)
                    for g in range(G)]

        o_rows = [[] for _ in range(qb)]
        lse_tiles = [jnp.zeros((QH, WIN), F32) for _ in range(qb)]
        pending = [raw_scores(*grp) for grp in groups[:AHEAD]]
        for gi, (b, kh) in enumerate(groups):
            scores = pending.pop(0)
            if gi + AHEAD < len(groups):
                pending.append(raw_scores(*groups[gi + AHEAD]))
            v_band = vfull[kh * HD:(kh + 1) * HD, b * WIN:(b + 2) * WIN]
            for g in range(G):
                hd = kh * G + g
                s = _swa_finish_scores(scores[g], b_ref[hd], (n == 0) if b == 0 else None)
                sink = sk_ref[hd]
                m = jnp.maximum(jnp.max(s, axis=0, keepdims=True), sink)
                p = jnp.exp(s - m)
                den = jnp.sum(p, axis=0, keepdims=True) + jnp.exp(sink - m)
                p = p / den
                o_rows[b].append(_dot(v_band, p.astype(BF16), NN))
                lse_tiles[b] = jnp.where(head_row == hd, m + jnp.log(den), lse_tiles[b])
        o_ref[...] = jnp.concatenate([jnp.concatenate(rows, axis=0) for rows in o_rows], axis=1)
        lse_ref[...] = jnp.concatenate(lse_tiles, axis=1)

    prev = lambda r: (lambda n: (r, jnp.maximum(n * qb - 1, 0)))
    return pl.pallas_call(
        kern, grid=(nst,),
        in_specs=[pl.BlockSpec((QH * HD, w), lambda n: (0, n)),
                  pl.BlockSpec((KVH * HD, w), lambda n: (4, n)), pl.BlockSpec((KVH * HD, WIN), prev(4)),
                  pl.BlockSpec((KVH * HD, w), lambda n: (5, n)), pl.BlockSpec((KVH * HD, WIN), prev(5)),
                  pl.BlockSpec((QH, 2 * WIN, WIN), lambda n: (0, 0, 0)),
                  pl.BlockSpec(memory_space=pltpu.SMEM)],
        out_specs=[pl.BlockSpec((QH * HD, w), lambda n: (0, n)), pl.BlockSpec((QH, w), lambda n: (0, n))],
        out_shape=[jax.ShapeDtypeStruct((QH * HD, t), F32), jax.ShapeDtypeStruct((QH, t), F32)],
        name="swa_fwd", compiler_params=_cp())(qkv_t, qkv_t, qkv_t, qkv_t, qkv_t, bias, sinks)


def _swa_bwd(qkv_t, do_t, o_t, lse, bias, sinks, qb):
    t = qkv_t.shape[1]
    w = qb * WIN
    nst = t // w
    nblk = t // WIN

    def kern(q_ref, kc_ref, kp_ref, vc_ref, vp_ref, do_ref, o_ref, lse_ref, qn_ref, don_ref, on_ref, lsen_ref,
             b_ref, sk_ref, dqkv_ref, db_ref, dsk_ref):
        n = pl.program_id(0)

        @pl.when(n == 0)
        def _():
            db_ref[...] = jnp.zeros_like(db_ref)
            dsk_ref[...] = jnp.zeros_like(dsk_ref)

        kfull = jnp.concatenate([kp_ref[...], kc_ref[...]], axis=1)
        vfull = jnp.concatenate([vp_ref[...], vc_ref[...]], axis=1)
        head_row = lax.broadcasted_iota(jnp.int32, (QH, WIN), 0)
        db_acc = [None] * QH
        dsk_tile = jnp.zeros((QH, WIN), F32)
        prev_part = [[[None] * qb for _ in range(KVH)] for _ in range(2)]
        cur_part = [[[None] * qb for _ in range(KVH)] for _ in range(2)]
        groups = [(b, kh) for b in range(qb) for kh in range(KVH)]

        def first_matmuls(b, kh):
            k_band = kfull[kh * HD:(kh + 1) * HD, b * WIN:(b + 2) * WIN]
            v_band = vfull[kh * HD:(kh + 1) * HD, b * WIN:(b + 2) * WIN]
            out = []
            for g in range(G):
                rs = slice((kh * G + g) * HD, (kh * G + g + 1) * HD)
                dob = do_ref[rs, b * WIN:(b + 1) * WIN].astype(BF16)
                out.append((_dot(k_band, q_ref[rs, b * WIN:(b + 1) * WIN], TN), _dot(v_band, dob, TN), dob))
            return out

        dq_rows = [[] for _ in range(qb)]
        pending = [first_matmuls(*grp) for grp in groups[:AHEAD]]
        for gi, (b, kh) in enumerate(groups):
            first = pending.pop(0)
            if gi + AHEAD < len(groups):
                pending.append(first_matmuls(*groups[gi + AHEAD]))
            cs = slice(b * WIN, (b + 1) * WIN)
            k_band = kfull[kh * HD:(kh + 1) * HD, b * WIN:(b + 2) * WIN]
            dk_b = dv_b = None
            for g in range(G):
                hd = kh * G + g
                rs = slice(hd * HD, (hd + 1) * HD)
                raw, dp, dob = first[g]
                lse_h = lse_ref[hd:hd + 1, cs]
                s = _swa_finish_scores(raw, b_ref[hd], (n == 0) if b == 0 else None)
                p = jnp.exp(s - lse_h)
                dl = jnp.sum(do_ref[rs, cs] * o_ref[rs, cs], axis=0, keepdims=True)
                ds = p * (dp - dl)
                db_acc[hd] = ds if db_acc[hd] is None else db_acc[hd] + ds
                dsk_tile = jnp.where(head_row == hd, dsk_tile - jnp.exp(sk_ref[hd] - lse_h) * dl, dsk_tile)
                dss = (ds * SWA_SCALE).astype(BF16)
                dq_rows[b].append(_dot(k_band, dss, NN).astype(BF16))
                dk_h = _dot(q_ref[rs, cs], dss, NT)
                dv_h = _dot(dob, p.astype(BF16), NT)
                dk_b = dk_h if dk_b is None else dk_b + dk_h
                dv_b = dv_h if dv_b is None else dv_b + dv_h
            for which, val in ((0, dk_b), (1, dv_b)):
                prev_part[which][kh][b] = val[:, 0:WIN]
                cur_part[which][kh][b] = val[:, WIN:2 * WIN]
        dq_cols = [jnp.concatenate(rows, axis=0) for rows in dq_rows]

        live = n < nst - 1
        ls = slice((qb - 1) * WIN, qb * WIN)
        halo = [[None] * KVH for _ in range(2)]
        for kh in range(KVH):
            k_last = kc_ref[kh * HD:(kh + 1) * HD, ls]
            v_last = vc_ref[kh * HD:(kh + 1) * HD, ls]
            dk_b = dv_b = None
            for g in range(G):
                hd = kh * G + g
                rs = slice(hd * HD, (hd + 1) * HD)
                q_t = qn_ref[rs, :]
                do = don_ref[rs, :]
                s = _dot(k_last, q_t, TN) * SWA_SCALE + b_ref[hd, 0:WIN, :]
                p = jnp.exp(s - lsen_ref[hd:hd + 1, :])
                dob = do.astype(BF16)
                dp = _dot(v_last, dob, TN)
                dl = jnp.sum(do * on_ref[rs, :], axis=0, keepdims=True)
                dss = (p * (dp - dl) * SWA_SCALE).astype(BF16)
                dk_h = _dot(q_t, dss, NT)
                dv_h = _dot(dob, p.astype(BF16), NT)
                dk_b = dk_h if dk_b is None else dk_b + dk_h
                dv_b = dv_h if dv_b is None else dv_b + dv_h
            halo[0][kh] = jnp.where(live, dk_b, 0.0)
            halo[1][kh] = jnp.where(live, dv_b, 0.0)

        kv_rows = []
        for which in range(2):
            for kh in range(KVH):
                blocks = [cur_part[which][kh][p] + (prev_part[which][kh][p + 1] if p + 1 < qb else halo[which][kh])
                          for p in range(qb)]
                kv_rows.append(jnp.concatenate(blocks, axis=1))
        dqkv_ref[...] = jnp.concatenate(
            [jnp.concatenate(dq_cols, axis=1), jnp.concatenate(kv_rows, axis=0).astype(BF16)], axis=0)
        db_ref[...] += jnp.stack(db_acc)
        dsk_ref[...] += dsk_tile

    prev = lambda r: (lambda n: (r, jnp.maximum(n * qb - 1, 0)))
    nxt = lambda n: (0, jnp.minimum((n + 1) * qb, nblk - 1))
    big = lambda: pl.BlockSpec((QH * HD, w), lambda n: (0, n))
    return pl.pallas_call(
        kern, grid=(nst,),
        in_specs=[big(),
                  pl.BlockSpec((KVH * HD, w), lambda n: (4, n)), pl.BlockSpec((KVH * HD, WIN), prev(4)),
                  pl.BlockSpec((KVH * HD, w), lambda n: (5, n)), pl.BlockSpec((KVH * HD, WIN), prev(5)),
                  big(), big(), pl.BlockSpec((QH, w), lambda n: (0, n)),
                  pl.BlockSpec((QH * HD, WIN), nxt), pl.BlockSpec((QH * HD, WIN), nxt),
                  pl.BlockSpec((QH * HD, WIN), nxt), pl.BlockSpec((QH, WIN), nxt),
                  pl.BlockSpec((QH, 2 * WIN, WIN), lambda n: (0, 0, 0)),
                  pl.BlockSpec(memory_space=pltpu.SMEM)],
        out_specs=[pl.BlockSpec(((QH + 2 * KVH) * HD, w), lambda n: (0, n)),
                   pl.BlockSpec((QH, 2 * WIN, WIN), lambda n: (0, 0, 0)),
                   pl.BlockSpec((QH, WIN), lambda n: (0, 0))],
        out_shape=[jax.ShapeDtypeStruct(((QH + 2 * KVH) * HD, t), BF16),
                   jax.ShapeDtypeStruct((QH, 2 * WIN, WIN), F32), jax.ShapeDtypeStruct((QH, WIN), F32)],
        name="swa_bwd", compiler_params=_cp())(
            qkv_t, qkv_t, qkv_t, qkv_t, qkv_t, do_t, o_t, lse, qkv_t, do_t, o_t, lse, bias, sinks)


def _adamw_math(w, g, m, v):
    nm = B1 * m + (1.0 - B1) * g
    nv = B2 * v + (1.0 - B2) * (g * g)
    mhat = nm * (1.0 / (1.0 - B1 ** STEP))
    vhat = nv * (1.0 / (1.0 - B2 ** STEP))
    return -LR * (mhat / (jnp.sqrt(vhat) + ADAM_EPS) + WD * w), nm, nv


def _adamw_layers(w, m, v, g0buf, g1buf, off, name, tm=512):
    rows = w.shape[1]
    nb, ob = rows // tm, off // tm

    def kern(w_ref, m_ref, v_ref, g0_ref, g1_ref, gr_ref, d_ref, nm_ref, nv_ref):
        g_ = jnp.where(pl.program_id(0) == 0, g0_ref[...], g1_ref[...])
        gr_ref[...] = g_
        d_ref[...], nm_ref[...], nv_ref[...] = _adamw_math(w_ref[...], g_, m_ref[...], v_ref[...])

    lay = pl.BlockSpec((None, tm, D), lambda l, i: (l, i, 0))
    gsp = pl.BlockSpec((tm, D), lambda l, i: (ob + i, 0))
    return pl.pallas_call(
        kern, grid=(2, nb), in_specs=[lay, lay, lay, gsp, gsp], out_specs=[lay] * 4,
        out_shape=[jax.ShapeDtypeStruct(w.shape, F32)] * 4, name=name, compiler_params=_cp())(w, m, v, g0buf, g1buf)


def _adamw(w, g, m, v, name, tm=544):
    r = w.shape[0]
    tm = r if r % tm else tm

    def kern(w_ref, g_ref, m_ref, v_ref, d_ref, nm_ref, nv_ref):
        d_ref[...], nm_ref[...], nv_ref[...] = _adamw_math(w_ref[...], g_ref[...], m_ref[...], v_ref[...])

    row = pl.BlockSpec((tm, D), lambda i: (i, 0))
    sds = jax.ShapeDtypeStruct((r, D), F32)
    return pl.pallas_call(kern, grid=(r // tm,), in_specs=[row] * 4, out_specs=[row] * 3, out_shape=[sds] * 3,
                          name=name, compiler_params=_cp())(w, g, m, v)


def _mesh_pos():
    return lax.axis_index("x"), lax.axis_index("y"), lax.axis_index("c")


ANY = pl.BlockSpec(memory_space=pl.ANY)


AG_SEMS = [pltpu.SemaphoreType.DMA((6,)), pltpu.SemaphoreType.DMA((6,))]


def _allgather_schedule(w_ref, out_ref, send_sems, recv_sems):
    half = w_ref.shape[0] // 2
    x, y, c = _mesh_pos()
    me, sibling = (x, y, c), (x, y, 1 - c)
    chips = [(1 - x, y), (x, 1 - y), (1 - x, 1 - y)]

    def rows(px, py, pc):
        return out_ref.at[2 * px + py, pl.ds(pc * half, half), :]

    def copy(k, block, to, src=None):
        return pltpu.make_async_remote_copy(
            src_ref=rows(*block) if src is None else src, dst_ref=rows(*block),
            send_sem=send_sems.at[k], recv_sem=recv_sems.at[k], device_id=to, device_id_type=MESH)

    def first():
        return [copy(j, me, (*chip, c), src=w_ref.at[pl.ds(c * half, half), :]) for j, chip in enumerate(chips)]

    def passed():
        return [copy(3 + j, (*chip, c), sibling) for j, chip in enumerate(chips)]

    def start():
        for cp in first():
            cp.start()

    def forward():
        for j, chip in enumerate(chips):
            copy(j, (*chip, c), me).wait_recv()
            passed()[j].start()

    def finish():
        for j, chip in enumerate(chips):
            copy(3 + j, (*chip, 1 - c), me).wait_recv()
        for cp in first() + passed():
            cp.wait_send()

    return start, forward, finish


def _allgather_weights(wpack):
    def body(w_ref, out_ref, send_sems, recv_sems):
        start, forward, finish = _allgather_schedule(w_ref, out_ref, send_sems, recv_sems)
        start()
        forward()
        finish()

    return pl.pallas_call(
        body, out_shape=jax.ShapeDtypeStruct((4,) + wpack.shape, wpack.dtype), in_specs=[ANY], out_specs=ANY,
        scratch_shapes=AG_SEMS, name="allgather_weights")(wpack)


def _row_tile(rows):
    t = min(rows, 512)
    while rows % t or t % 16:
        t -= 16
    return t


ALL_SEMS = [pltpu.SemaphoreType.DMA((7,)), pltpu.SemaphoreType.DMA((7,))]


def _device_exchange_schedule(g_ref, out_ref, send_sems, recv_sems):
    half = g_ref.shape[1] // 2
    x, y, c = _mesh_pos()
    me = 4 * x + 2 * y + c
    peers = [(x ^ (k >> 2), y ^ ((k >> 1) & 1), c ^ (k & 1)) for k in range(1, 8)]

    def sends():
        return [pltpu.make_async_remote_copy(
            src_ref=g_ref.at[2 * px + py, pl.ds(pc * half, half), :], dst_ref=out_ref.at[me],
            send_sem=send_sems.at[j], recv_sem=recv_sems.at[j], device_id=(px, py, pc), device_id_type=MESH)
            for j, (px, py, pc) in enumerate(peers)]

    def start():
        for cp in sends():
            cp.start()

    def finish():
        for j, (px, py, pc) in enumerate(peers):
            pltpu.make_async_remote_copy(
                src_ref=out_ref.at[me], dst_ref=out_ref.at[4 * px + 2 * py + pc], send_sem=send_sems.at[j],
                recv_sem=recv_sems.at[j], device_id=(px, py, pc), device_id_type=MESH).wait_recv()
        for cp in sends():
            cp.wait_send()

    return start, finish


def _sum_devices(slots, g, pos, tag):
    half = slots.shape[1]
    tm = _row_tile(half)
    nb = half // tm

    def kern(pos_ref, own_ref, *refs):
        acc = own_ref[0].astype(F32)
        for s_ref in refs[:7]:
            acc = acc + s_ref[0].astype(F32)
        refs[7][...] = acc

    def slot(k):
        return pl.BlockSpec((1, tm, D), lambda i, pos: (jnp.bitwise_xor(pos[2], k), i, 0))

    gs = pltpu.PrefetchScalarGridSpec(
        num_scalar_prefetch=1, grid=(nb,),
        in_specs=[pl.BlockSpec((1, tm, D), lambda i, pos: (pos[0], pos[1] * nb + i, 0))] + [slot(k) for k in range(1, 8)],
        out_specs=pl.BlockSpec((tm, D), lambda i, pos: (pos[1] * nb + i, 0)))
    return pl.pallas_call(kern, grid_spec=gs, out_shape=jax.ShapeDtypeStruct((2 * half, D), F32),
                          name=f"rs_sum_devices_{tag}", compiler_params=_cp())(pos, g, *([slots] * 7))


def _reduce_scatter_finish(slots, g, pos, tag):
    return _join_core_halves(_sum_devices(slots, g, pos, tag), tag)


def _join_core_halves(r, tag):
    half = r.shape[0] // 2

    def body(r_ref, out_ref, send_sem, recv_sem):
        x, y, c = _mesh_pos()
        mine = out_ref.at[pl.ds(c * half, half), :]
        cp = pltpu.make_async_remote_copy(
            src_ref=mine, dst_ref=mine, send_sem=send_sem, recv_sem=recv_sem,
            device_id=(x, y, 1 - c), device_id_type=MESH)
        cp.start()
        theirs = out_ref.at[pl.ds((1 - c) * half, half), :]
        pltpu.make_async_remote_copy(
            src_ref=theirs, dst_ref=theirs, send_sem=send_sem, recv_sem=recv_sem,
            device_id=(x, y, 1 - c), device_id_type=MESH).wait_recv()
        cp.wait_send()

    return pl.pallas_call(
        body, out_shape=jax.ShapeDtypeStruct(r.shape, r.dtype), in_specs=[ANY], out_specs=ANY,
        input_output_aliases={0: 0},
        scratch_shapes=[pltpu.SemaphoreType.DMA, pltpu.SemaphoreType.DMA],
        name=f"rs_join_cores_{tag}")(r)


def _allreduce_small(v, name):
    def body(v_ref, out_ref, gat, send_sems, recv_sems):
        x, y, c = _mesh_pos()
        me = 4 * x + 2 * y + c
        gat[me] = v_ref[...]
        sends = []
        for k in range(1, 8):
            peer = (x ^ (k >> 2), y ^ ((k >> 1) & 1), c ^ (k & 1))
            cp = pltpu.make_async_remote_copy(
                src_ref=v_ref, dst_ref=gat.at[me], send_sem=send_sems.at[k - 1], recv_sem=recv_sems.at[k - 1],
                device_id=peer, device_id_type=MESH)
            cp.start()
            sends.append(cp)
        for k in range(1, 8):
            px, py, pc = x ^ (k >> 2), y ^ ((k >> 1) & 1), c ^ (k & 1)
            pltpu.make_async_remote_copy(
                src_ref=v_ref, dst_ref=gat.at[4 * px + 2 * py + pc], send_sem=send_sems.at[k - 1],
                recv_sem=recv_sems.at[k - 1], device_id=(px, py, pc), device_id_type=MESH).wait_recv()
        for cp in sends:
            cp.wait_send()
        acc = gat[0]
        for d in range(1, 8):
            acc = acc + gat[d]
        out_ref[...] = acc

    return pl.pallas_call(
        body, out_shape=jax.ShapeDtypeStruct(v.shape, F32),
        in_specs=[pl.BlockSpec(memory_space=pltpu.VMEM)], out_specs=pl.BlockSpec(memory_space=pltpu.VMEM),
        scratch_shapes=[pltpu.VMEM((8,) + v.shape, F32), pltpu.SemaphoreType.DMA((7,)), pltpu.SemaphoreType.DMA((7,))],
        name=name)(v)


def _mlp_fwd(xb, w_up, w_down, tag):
    a = _mm(xb, w_up[0], "nn", f"mlp_up_{tag}", out_dtype=BF16, relu2=True, b_view=("cols", w_up[1]), tm=2048)
    return a, _mm(a, w_down[0], "nn", f"mlp_down_{tag}", b_view=("rows", w_down[1]), tm=2048)


def _mlp_bwd(dz, dzb, xb, a, w_up, w_down, tag):
    du = _mm(dzb, w_down[0], "nt", f"mlp_down_dx_{tag}", out_dtype=BF16, gate_a=a, b_view=("rows", w_down[1]),
             tm=2048)
    gsh = _mm(xb, du, "tn", f"mlp_up_dw_{tag}", out_dtype=BF16, out_view=("cols", 2 * ROWS["mlp_w_up"], 0, None),
              tk=2048)
    gsh = _mm(a, dzb, "tn", f"mlp_down_dw_{tag}", out_dtype=BF16,
              out_view=("rows", 2 * ROWS["mlp_w_up"], ROWS["mlp_w_up"], gsh), tk=2048)
    dx = _mm(du, w_up[0], "nt", f"mlp_up_dx_{tag}", addend=dz, add_scale=ALPHA, b_view=("cols", w_up[1]))
    return dx, gsh


def _fwd_bwd(x, target, w, dist=None, bq=512, qb=8, hb=8):
    t = x.shape[0]
    bq = min(bq, t)
    qb = min(qb, t // WIN)
    cos, sin = _rope_tables(t)
    bkt = jnp.asarray(_bucket_table())
    w_in = jnp.pad(w[("mla_w_in", None)], ((0, 0), (0, HW - (QR + KVR + ROPE))))
    wuq = w[("mla_w_uq", None)]
    wq2 = jnp.concatenate([wuq[:, :, :NOPE].reshape(QR, H * NOPE),
                           jnp.pad(wuq[:, :, NOPE:], ((0, 0), (0, 0), (0, RP - ROPE))).reshape(QR, H * RP)], axis=1)
    wuk_t = w[("mla_w_uk", None)].transpose(1, 2, 0)
    wuk_h = w[("mla_w_uk", None)].transpose(1, 0, 2)
    wuv_h = w[("mla_w_uv", None)].transpose(1, 0, 2)
    w_o = w[("mla_w_o", None)]
    sinks = w["swa_sinks"].reshape(QH)
    lnp = lambda n, l: w[n][l]
    reduced = {}

    hh = _mm(x, w_in, "nn", "mla_in")
    cq, kc = _mla_pre(hh, w["mla_g_q"], w["mla_g_kv"], cos, sin)
    q2 = _mm(cq, wq2, "nn", "mla_uq")
    qcat = _q_prep(q2, wuk_t, cos, sin)
    if dist is None:
        o_lat, lse0_t, o0 = _flash_fwd(qcat, kc, wuv_h, bq, hb)
    else:
        o_lat, lse0_t, o0, wall = _flash_fwd(qcat, kc, wuv_h, bq, hb, gather=dist.late_pack)
        wall = lax.dynamic_update_slice(wall, dist.late_pack[None], (dist.shard, 0, 0))
        w = {**w, **_full_from_gathered(AG_LATE, wall, dist.shard_shapes)}
    wqkv = jnp.concatenate([w[("swa_w_q", None)], w[("kv_w_shared", None)]], axis=1)
    wqkv_t = wqkv.T
    wo_s = w[("swa_w_o", None)]
    y0 = _mm(o0, w_o, "nn", "mla_out")
    x1b, xh1, r1 = _add_ln(x, y0, lnp("ln_mix_g", 0), lnp("ln_mix_b", 0), "ln_mix_0")
    a0, f0 = _mlp_fwd(x1b, w[("mlp_w_up", 0)], w[("mlp_w_down", 0)], 0)
    x2b, xh2, r2 = _add_ln(xh1, f0, lnp("ln_mlp_g", 0), lnp("ln_mlp_b", 0), "ln_mlp_0",
                           res_affine=(lnp("ln_mix_g", 0), lnp("ln_mix_b", 0)))
    bias = _bias_build(w["rel_bias"], bkt)
    qkv_t = _mm(x2b, wqkv, "nn", "swa_qkv", out_dtype=BF16, out_t=True)
    os_t, lse1 = _swa_fwd(qkv_t, bias, sinks, qb)
    y1 = _mm(os_t, wo_s, "tn", "swa_out")
    x3b, xh3, r3 = _add_ln(xh2, y1, lnp("ln_mix_g", 1), lnp("ln_mix_b", 1), "ln_mix_1",
                           res_affine=(lnp("ln_mlp_g", 0), lnp("ln_mlp_b", 0)))
    a1, f1 = _mlp_fwd(x3b, w[("mlp_w_up", 1)], w[("mlp_w_down", 1)], 1)
    _, xh4, r4 = _add_ln(xh3, f1, lnp("ln_mlp_g", 1), lnp("ln_mlp_b", 1), "ln_mlp_1",
                         res_affine=(lnp("ln_mix_g", 1), lnp("ln_mix_b", 1)))

    g = {}
    dz4, dz4b, dg_mlp1, db_mlp1, lpart = _ln_bwd(target, xh4, r4, lnp("ln_mlp_g", 1), "ln_mlp_1_bwd",
                                                 loss_b=lnp("ln_mlp_b", 1))
    dx3, g["mlp1"] = _mlp_bwd(dz4, dz4b, x3b, a1, w[("mlp_w_up", 1)], w[("mlp_w_down", 1)], 1)
    dz3, dz3b, dg_mix1, db_mix1 = _ln_bwd(dx3, xh3, r3, lnp("ln_mix_g", 1), "ln_mix_1_bwd")
    dos_t = _mm(dz3b, wo_s, "nt", "swa_out_dx", out_t=True)
    g[("swa_w_o", None)] = _mm(os_t, dz3b, "nn", "swa_out_dw")
    dqkv_t, dbias, dsk = _swa_bwd(qkv_t, dos_t, os_t, lse1, bias, sinks, qb)
    dwqkv = _mm(dqkv_t, x2b, "nn", "swa_qkv_dw").T
    g[("swa_w_q", None)], g[("kv_w_shared", None)] = dwqkv[:, :QH * HD], dwqkv[:, QH * HD:]
    dx2 = _mm(dqkv_t, wqkv_t, "tn", "swa_qkv_dx", addend=dz3, add_scale=ALPHA)
    g["rel_bias"] = jnp.sum(_bias_bwd(dbias, bkt), axis=-1).reshape(NBKT, QH)
    g["swa_sinks"] = jnp.sum(dsk, axis=-1).reshape(1, QH)
    dz2, dz2b, dg_mlp0, db_mlp0 = _ln_bwd(dx2, xh2, r2, lnp("ln_mlp_g", 0), "ln_mlp_0_bwd")
    dx1, g["mlp0"] = _mlp_bwd(dz2, dz2b, x1b, a0, w[("mlp_w_up", 0)], w[("mlp_w_down", 0)], 0)
    dz1, dz1b, dg_mix0, db_mix0 = _ln_bwd(dx1, xh1, r1, lnp("ln_mix_g", 0), "ln_mix_0_bwd")
    do0 = _mm(dz1b, w_o, "nt", "mla_out_dx", out_dtype=BF16)
    g[("mla_w_o", None)] = _mm(o0, dz1b, "tn", "mla_out_dw")
    do_lat, dwuv, delta_t = _o_up_bwd(do0, o_lat, wuv_h)
    g[("mla_w_uv", None)] = dwuv.transpose(1, 0, 2)
    kc_t = kc.reshape(t // bq, bq, KD).transpose(0, 2, 1)
    if dist is None:
        dk, ds_all = _flash_dkv(qcat, kc, do_lat, lse0_t, delta_t, bq, hb)
        dq2, dwuk = _flash_dq(ds_all, kc_t, q2, wuk_h, cos, sin, bq)
    else:
        g["mid"] = _grad_shards(RS_MID, g).astype(BF16)
        dk, ds_all, slots1, slots_mid = _flash_dkv(qcat, kc, do_lat, lse0_t, delta_t, bq, hb,
                                                  exchange=(g["mlp1"], g["mid"]))
        dq2, dwuk, slots0 = _flash_dq(ds_all, kc_t, q2, wuk_h, cos, sin, bq, exchange=g["mlp0"])
        for key, slots in (("mlp1", slots1), ("mid", slots_mid), ("mlp0", slots0)):
            reduced[key] = _reduce_scatter_finish(slots, g[key], dist.pos, key)
    g[("mla_w_uk", None)] = dwuk.transpose(2, 0, 1)
    dcq = _mm(dq2, wq2, "nt", "mla_uq_dx")
    dwq2 = _mm(cq, dq2, "tn", "mla_uq_dw")
    g[("mla_w_uq", None)] = jnp.concatenate([dwq2[:, :H * NOPE].reshape(QR, H, NOPE),
                                             dwq2[:, H * NOPE:].reshape(QR, H, RP)[:, :, :ROPE]], axis=2)
    dh, dgq, dgkv = _mla_pre_bwd(hh, dcq, dk, w["mla_g_q"], w["mla_g_kv"], cos, sin)
    g[("mla_w_in", None)] = _mm(x, dh, "tn", "mla_in_dw")[:, :QR + KVR + ROPE]
    if dist is None:
        grad_x = _mm(dh, w_in, "nt", "mla_in_dx", addend=dz1, add_scale=ALPHA)
    else:
        g["end"] = _grad_shards(RS_END, g).astype(BF16)
        grad_x, slots_end = _mm(dh, w_in, "nt", "mla_in_dx", addend=dz1, add_scale=ALPHA, exchange=g["end"])
        reduced["end"] = _reduce_scatter_finish(slots_end, g["end"], dist.pos, "end")
    g["mla_g_q"], g["mla_g_kv"] = dgq, dgkv
    g["ln_mix_g"] = jnp.concatenate([dg_mix0, dg_mix1], axis=0)
    g["ln_mix_b"] = jnp.concatenate([db_mix0, db_mix1], axis=0)
    g["ln_mlp_g"] = jnp.concatenate([dg_mlp0, dg_mlp1], axis=0)
    g["ln_mlp_b"] = jnp.concatenate([db_mlp0, db_mlp1], axis=0)
    return lpart, grad_x, g, reduced


def _rows(a):
    return a.reshape(-1, D)


def _piece(a, layer):
    return _rows(a if layer is None else a[layer])


def _pack_group(group, parts):
    return jnp.concatenate([_piece(parts[n], l) for n, l in group], axis=0)


def _unpack_group(group, buf, like):
    out, off = {}, 0
    for n, l in group:
        shp = like[n].shape if l is None else like[n].shape[1:]
        out[(n, l)] = buf[off:off + ROWS[n]].reshape(shp)
        off += ROWS[n]
    return out


def _by_name(pieces):
    out = {n: a for (n, l), a in pieces.items() if l is None}
    for n in {n for (n, l) in pieces if l is not None}:
        out[n] = jnp.stack([pieces[(n, 0)], pieces[(n, 1)]])
    return out


def _full_from_gathered(group, wall, shard_shapes):
    out, off = {}, 0
    for n, l in group:
        shp = tuple(shard_shapes[n])
        if n in ("mlp_w_up", "mlp_w_down"):
            out[(n, l)] = (wall, off)
        elif n == "kv_w_shared":
            out[(n, l)] = wall[:, off:off + ROWS[n]].reshape((4 * shp[0],) + shp[1:])
        else:
            out[(n, l)] = wall[:, off:off + ROWS[n]].reshape((4 * shp[1],) + shp[2:])
        off += ROWS[n]
    return out


def _grad_shards(group, g):
    return jnp.concatenate([g[(n, l)].reshape(4, ROWS[n], D) for n, l in group], axis=1)


SMALL = (("ln_mix_g", 0, 2), ("ln_mix_b", 2, 2), ("ln_mlp_g", 4, 2), ("ln_mlp_b", 6, 2),
         ("swa_sinks", 8, 1), ("mla_g_q", 9, 1), ("mla_g_kv", 10, 1), ("rel_bias", 11, 1))
LOSS_ROW = 12


def _pack_small(parts, extra_row=None):
    rows = []
    for n, _, nr in SMALL:
        a = parts[n].reshape(nr, -1).astype(F32)
        rows.append(jnp.pad(a, ((0, 0), (0, D - a.shape[1]))))
    if extra_row is not None:
        rows.append(extra_row)
    rows.append(jnp.zeros((SMALL_ROWS - sum(r.shape[0] for r in rows), D), F32))
    return jnp.concatenate(rows, axis=0)


def _unpack_small(buf, like):
    out = {}
    for n, r0, nr in SMALL:
        size = like[n].size // nr
        out[n] = buf[r0:r0 + nr, :size].reshape(like[n].shape)
    return out


def kernel(x, mla_w_in, mla_g_q, mla_g_kv, mla_w_uq, mla_w_uk, mla_w_uv, mla_w_o, kv_w_shared, swa_w_q, swa_sinks, swa_w_o, rel_bias, mlp_w_up, mlp_w_down, ln_mix_g, ln_mix_b, ln_mlp_g, ln_mlp_b, loss_target, m_mla_w_in, m_mla_g_q, m_mla_g_kv, m_mla_w_uq, m_mla_w_uk, m_mla_w_uv, m_mla_w_o, m_kv_w_shared, m_swa_w_q, m_swa_sinks, m_swa_w_o, m_rel_bias, m_mlp_w_up, m_mlp_w_down, m_ln_mix_g, m_ln_mix_b, m_ln_mlp_g, m_ln_mlp_b, v_mla_w_in, v_mla_g_q, v_mla_g_kv, v_mla_w_uq, v_mla_w_uk, v_mla_w_uv, v_mla_w_o, v_kv_w_shared, v_swa_w_q, v_swa_sinks, v_swa_w_o, v_rel_bias, v_mlp_w_up, v_mlp_w_down, v_ln_mix_g, v_ln_mix_b, v_ln_mlp_g, v_ln_mlp_b):
    names = ["mla_w_in", "mla_g_q", "mla_g_kv", "mla_w_uq", "mla_w_uk", "mla_w_uv", "mla_w_o", "kv_w_shared",
             "swa_w_q", "swa_sinks", "swa_w_o", "rel_bias", "mlp_w_up", "mlp_w_down",
             "ln_mix_g", "ln_mix_b", "ln_mlp_g", "ln_mlp_b"]
    ws = dict(zip(names, [mla_w_in, mla_g_q, mla_g_kv, mla_w_uq, mla_w_uk, mla_w_uv, mla_w_o, kv_w_shared,
                          swa_w_q, swa_sinks, swa_w_o, rel_bias, mlp_w_up, mlp_w_down,
                          ln_mix_g, ln_mix_b, ln_mlp_g, ln_mlp_b]))
    ms = dict(zip(names, [m_mla_w_in, m_mla_g_q, m_mla_g_kv, m_mla_w_uq, m_mla_w_uk, m_mla_w_uv, m_mla_w_o,
                          m_kv_w_shared, m_swa_w_q, m_swa_sinks, m_swa_w_o, m_rel_bias, m_mlp_w_up, m_mlp_w_down,
                          m_ln_mix_g, m_ln_mix_b, m_ln_mlp_g, m_ln_mlp_b]))
    vs = dict(zip(names, [v_mla_w_in, v_mla_g_q, v_mla_g_kv, v_mla_w_uq, v_mla_w_uk, v_mla_w_uv, v_mla_w_o,
                          v_kv_w_shared, v_swa_w_q, v_swa_sinks, v_swa_w_o, v_rel_bias, v_mlp_w_up, v_mlp_w_down,
                          v_ln_mix_g, v_ln_mix_b, v_ln_mlp_g, v_ln_mlp_b]))
    xi, yi, ci = _mesh_pos()
    shard = 2 * xi + yi
    shard_shapes = {n: ws[n].shape for n in ROWS}
    wbf = {n: ws[n].astype(BF16) for n in ROWS}

    gains = jnp.concatenate([mla_g_q.reshape(-1), mla_g_kv.reshape(-1)])
    pieces = []
    for _ in range(3):
        head = lax.reduce_precision(gains, exponent_bits=8, mantissa_bits=7)
        pieces.append(head.astype(BF16))
        gains = gains - head
    ng = (QR + KVR) // 4
    gain_rows = jnp.pad(jnp.concatenate(pieces).reshape(1, 3 * ng), ((0, GAIN_ROWS - 1), (0, D - 3 * ng)))
    early = jnp.concatenate([_pack_group(AG_EARLY, wbf), gain_rows], axis=0)
    wall = lax.dynamic_update_slice(_allgather_weights(early), early[None], (shard, 0, 0))
    w = _full_from_gathered(AG_EARLY, wall, shard_shapes)
    gp = wall[:, early.shape[0] - GAIN_ROWS, :3 * ng].astype(F32).reshape(4, 3, ng)
    gains = (gp[:, 0] + gp[:, 1]) + gp[:, 2]
    w["mla_g_q"], w["mla_g_kv"] = gains[:, :QR // 4].reshape(QR), gains[:, QR // 4:].reshape(KVR)
    dist = _Dist(shard=shard, pos=jnp.stack([shard, ci, 2 * shard + ci]).astype(jnp.int32),
                 late_pack=_pack_group(AG_LATE, wbf), shard_shapes=shard_shapes)
    for n in ("swa_sinks", "rel_bias", "ln_mix_g", "ln_mix_b", "ln_mlp_g", "ln_mlp_b"):
        w[n] = ws[n]

    lpart, grad_x, g, reduced = _fwd_bwd(x[0], loss_target[0], w, dist)
    reduced["rest"] = jnp.concatenate([reduced["mid"], reduced["end"]], axis=0)

    small_like = {n: g[n] for n, _, _ in SMALL}
    small_sum = _allreduce_small(_pack_small(g, extra_row=lpart), "allreduce_small_grads")
    loss = 0.5 * jnp.sum(small_sum[LOSS_ROW]) / D
    gsm = _unpack_small(small_sum, small_like)
    gsm["mla_g_q"] = lax.dynamic_slice(gsm["mla_g_q"], (0, shard * (QR // 4)), (1, QR // 4))
    gsm["mla_g_kv"] = lax.dynamic_slice(gsm["mla_g_kv"], (0, shard * (KVR // 4)), (1, KVR // 4))

    gbig, dbig, mbig, vbig = {}, {}, {}, {}
    for n in ("mlp_w_up", "mlp_w_down"):
        off = 0 if n == "mlp_w_up" else ROWS["mlp_w_up"]
        gbig[n], dbig[n], mbig[n], vbig[n] = _adamw_layers(
            ws[n], ms[n], vs[n], reduced["mlp0"], reduced["mlp1"], off, f"adamw_{n}")
    rest = RS_MID + RS_END
    outs = _adamw(_pack_group(rest, ws), reduced["rest"], _pack_group(rest, ms), _pack_group(rest, vs),
                  "adamw_rest", tm=_row_tile(reduced["rest"].shape[0]))
    for dst, buf in zip((gbig, dbig, mbig, vbig), (reduced["rest"], *outs)):
        dst.update(_by_name(_unpack_group(rest, buf, ws)))
    dsm, msm, vsm = _adamw(_pack_small(ws), _pack_small(gsm), _pack_small(ms), _pack_small(vs), "adamw_small", tm=16)
    grads = {**gbig, **gsm}
    delta = {**dbig, **_unpack_small(dsm, ws)}
    new_m = {**mbig, **_unpack_small(msm, ws)}
    new_v = {**vbig, **_unpack_small(vsm, ws)}
    grads = {n: grads[n].reshape(ws[n].shape) for n in names}
    return (loss, grad_x[None], *[grads[n] for n in names], *[delta[n] for n in names],
            *[new_m[n] for n in names], *[new_v[n] for n in names])
```

```python
import collections
import math

import numpy as np
import jax
import jax.numpy as jnp
from jax import lax
from jax.experimental import pallas as pl
from jax.experimental.pallas import tpu as pltpu

F32 = jnp.float32
BF16 = jnp.bfloat16
MESH = pl.DeviceIdType.MESH

D = 1024
H = 8
NOPE = 128
ROPE = 64
QR = 384
KVR = 256
RP = 128
KD = KVR + RP
HW = 768
QH = 16
KVH = 4
HD = 64
G = QH // KVH
WIN = 128
NBKT = 32
ALPHA = 4.0 ** 0.25
LN_EPS = 1e-5
RMS_EPS = 1e-6
MLA_SCALE = (NOPE + ROPE) ** -0.5
LOG2E = 1.4426950408889634
LN2 = 0.6931471805599453
QSCALE = MLA_SCALE * LOG2E
AHEAD = 1
SWA_SCALE = HD ** -0.5
NEG = -1e30
LR, B1, B2, ADAM_EPS, WD, STEP = 0.001, 0.9, 0.999, 1e-8, 0.01, 10

VMEM_LIMIT = 48 * 1024 * 1024

NN = (((1,), (0,)), ((), ()))
NT = (((1,), (1,)), ((), ()))
TN = (((0,), (0,)), ((), ()))

ROWS = {"mlp_w_up": 1024, "mlp_w_down": 1024, "mla_w_o": 256, "swa_w_q": 256, "swa_w_o": 256,
        "kv_w_shared": 128, "mla_w_in": 176, "mla_w_uq": 144, "mla_w_uk": 64, "mla_w_uv": 64}
AG_EARLY = (("mla_w_in", None), ("mla_w_uq", None), ("mla_w_uk", None), ("mla_w_uv", None), ("mla_w_o", None))
AG_LATE = (("mlp_w_up", 0), ("mlp_w_up", 1), ("mlp_w_down", 0), ("mlp_w_down", 1),
           ("swa_w_q", None), ("swa_w_o", None), ("kv_w_shared", None))
RS_MID = (("mla_w_o", None), ("swa_w_q", None), ("swa_w_o", None), ("kv_w_shared", None), ("mla_w_uv", None))
RS_END = (("mla_w_in", None), ("mla_w_uq", None), ("mla_w_uk", None))
SMALL_ROWS = 16
GAIN_ROWS = 32
_Dist =collections.namedtuple("_Dist", "shard pos late_pack shard_shapes")


def _cp(**kw):
    return pltpu.CompilerParams(vmem_limit_bytes=VMEM_LIMIT, **kw)


def _tile(n, pref):
    t = min(n, pref)
    while n % t:
        t -= 128
    return t


def _dot(a, b, dims):
    return lax.dot_general(a, b, dims, preferred_element_type=F32)


def _mm(a, b, mode, name, out_dtype=F32, out_t=False, addend=None, add_scale=1.0, relu2=False, gate_a=None,
        b_view=None, out_view=None, exchange=None, tm=1024, tn=1024, tk=1024):
    blk = 1024
    if b_view is not None:
        kind, b_off = b_view
        assert b.shape[0] == 4 and b.shape[2] == blk and b_off % blk == 0
        bshape = {("cols", "nn"): (blk, 4 * blk), ("cols", "nt"): (blk, 4 * blk),
                  ("rows", "nn"): (4 * blk, blk), ("rows", "nt"): (4 * blk, blk)}[(kind, mode)]
    else:
        bshape = b.shape
    if mode == "nn":
        (m, k), (k2, n) = a.shape, bshape
    elif mode == "nt":
        (m, k), (n, k2) = a.shape, bshape
    else:
        (k, m), (k2, n) = a.shape, bshape
    assert k == k2, (name, a.shape, b.shape)
    tm, tn, tk = _tile(m, tm), _tile(n, tn), _tile(k, tk)
    nk = k // tk
    dims = {"nn": NN, "nt": NT, "tn": TN}[mode]
    if mode == "tn":
        a_spec = pl.BlockSpec((tk, tm), lambda i, j, kk: (kk, i))
    else:
        a_spec = pl.BlockSpec((tm, tk), lambda i, j, kk: (i, kk))
    if b_view is not None:
        assert tn == blk and tk == blk
        ob = b_off // blk
        b_spec = {("cols", "nn"): pl.BlockSpec((None, tk, tn), lambda i, j, kk: (j, ob, 0)),
                  ("cols", "nt"): pl.BlockSpec((None, tn, tk), lambda i, j, kk: (kk, ob, 0)),
                  ("rows", "nn"): pl.BlockSpec((None, tk, tn), lambda i, j, kk: (kk, ob, 0)),
                  ("rows", "nt"): pl.BlockSpec((None, tn, tk), lambda i, j, kk: (j, ob, 0))}[(kind, mode)]
    elif mode == "nt":
        b_spec = pl.BlockSpec((tn, tk), lambda i, j, kk: (j, kk))
    else:
        b_spec = pl.BlockSpec((tk, tn), lambda i, j, kk: (kk, j))
    mn_spec = pl.BlockSpec((tm, tn), lambda i, j, kk: (i, j))
    ins, in_specs = [a, b], [a_spec, b_spec]
    if addend is not None:
        ins.append(addend)
        in_specs.append(mn_spec)
    if gate_a is not None:
        ins.append(gate_a)
        in_specs.append(mn_spec)
    aliases = {}
    if out_view is not None:
        okind, total_rows, o_off, buf = out_view
        assert not out_t and tm == blk and tn == blk and o_off % blk == 0
        oo = o_off // blk
        out_shape = [jax.ShapeDtypeStruct((4, total_rows, blk), out_dtype)]
        if okind == "cols":
            out_specs = [pl.BlockSpec((None, tm, tn), lambda i, j, kk: (j, oo, 0))]
        else:
            out_specs = [pl.BlockSpec((None, tm, tn), lambda i, j, kk: (i, oo, 0))]
        if buf is not None:
            aliases = {len(ins): 0}
            ins.append(buf)
            in_specs.append(pl.BlockSpec(memory_space=pl.ANY))
    elif out_t:
        out_shape = [jax.ShapeDtypeStruct((n, m), out_dtype)]
        out_specs = [pl.BlockSpec((tn, tm), lambda i, j, kk: (j, i))]
    else:
        out_shape = [jax.ShapeDtypeStruct((m, n), out_dtype)]
        out_specs = [mn_spec]
    has_add, has_gate = addend is not None, gate_a is not None
    hosting = exchange is not None
    scratch = [pltpu.VMEM((tm, tn), F32)] if nk > 1 else []
    if hosting:
        assert not aliases
        ins.append(exchange)
        in_specs.append(pl.BlockSpec(memory_space=pl.ANY))
        out_shape.append(jax.ShapeDtypeStruct((8, exchange.shape[1] // 2, D), exchange.dtype))
        out_specs.append(pl.BlockSpec(memory_space=pl.ANY))
        scratch = scratch + ALL_SEMS
    steps = (m // tm, n // tn, nk)

    def kern(*refs):
        a_ref, b_ref = refs[0], refs[1]
        pos = 2
        add_ref = gate_ref = None
        if has_add:
            add_ref = refs[pos]
            pos += 1
        if has_gate:
            gate_ref = refs[pos]
            pos += 1
        pos += len(aliases)
        if hosting:
            xc_start, xc_finish = _device_exchange_schedule(refs[pos], refs[pos + 2], refs[-2], refs[-1])
            pos += 1
        o_ref = refs[pos]
        acc = refs[pos + 1 + hosting] if nk > 1 else None
        kk = pl.program_id(2)
        if hosting:
            lin = (pl.program_id(0) * steps[1] + pl.program_id(1)) * steps[2] + kk

            @pl.when(lin == 0)
            def _():
                xc_start()

        def partial():
            return _dot(a_ref[...].astype(BF16), b_ref[...].astype(BF16), dims)

        if nk > 1:
            @pl.when(kk == 0)
            def _():
                acc[...] = partial()

            @pl.when((kk > 0) & (kk < nk - 1))
            def _():
                acc[...] += partial()

        @pl.when(kk == nk - 1)
        def _():
            r = partial() + acc[...] if nk > 1 else partial()
            if has_add:
                r = r + add_scale * add_ref[...].astype(F32)
            if has_gate:
                ga = gate_ref[...].astype(F32)
                r = r * jnp.where(ga > 0.0, (2.0 * ga) * lax.rsqrt(ga), 0.0)
            if relu2:
                hh = jnp.maximum(r, 0.0)
                r = hh * hh
            if out_t:
                r = r.T
            o_ref[...] = r.astype(out_dtype)

        if hosting:
            @pl.when(lin == steps[0] * steps[1] * steps[2] - 1)
            def _():
                xc_finish()

    outs = pl.pallas_call(
        kern, out_shape=out_shape, grid=steps, in_specs=in_specs, out_specs=out_specs,
        scratch_shapes=scratch, input_output_aliases=aliases, name=name, compiler_params=_cp())(*ins)
    return outs if hosting else outs[0]


def _add_ln(res, y, g, b, name, res_affine=None, tm=1024):
    t = res.shape[0]
    tm = min(tm, t)
    affine = res_affine is not None

    def kern(*refs):
        if affine:
            x_ref, y_ref, g_ref, b_ref, g0_ref, b0_ref, ob_ref, xh_ref, r_ref = refs
            x = x_ref[...] * g0_ref[...] + b0_ref[...]
        else:
            x_ref, y_ref, g_ref, b_ref, ob_ref, xh_ref, r_ref = refs
            x = x_ref[...]
        z = ALPHA * x + y_ref[...]
        mu = jnp.mean(z, axis=-1, keepdims=True)
        zc = z - mu
        var = jnp.mean(zc * zc, axis=-1, keepdims=True)
        r = lax.rsqrt(var + LN_EPS)
        xh = zc * r
        ob_ref[...] = (xh * g_ref[...] + b_ref[...]).astype(BF16)
        xh_ref[...] = xh
        r_ref[...] = r

    row = pl.BlockSpec((tm, D), lambda i: (i, 0))
    vec = pl.BlockSpec((1, D), lambda i: (0, 0))
    st = pl.BlockSpec((tm, 1), lambda i: (i, 0))
    ins = [res, y, g.reshape(1, D), b.reshape(1, D)]
    if affine:
        ins += [res_affine[0].reshape(1, D), res_affine[1].reshape(1, D)]
    return pl.pallas_call(
        kern, grid=(t // tm,), in_specs=[row, row] + [vec] * (len(ins) - 2), out_specs=[row, row, st],
        out_shape=[jax.ShapeDtypeStruct((t, D), BF16), jax.ShapeDtypeStruct((t, D), F32),
                   jax.ShapeDtypeStruct((t, 1), F32)],
        name=name, compiler_params=_cp())(*ins)


def _ln_bwd(dout, xhat, rstd, g, name, loss_b=None, tm=1024):
    t = dout.shape[0]
    tm = min(tm, t)
    head = loss_b is not None

    def kern(*refs):
        if head:
            do_ref, xh_ref, r_ref, g_ref, b_ref, dz_ref, dzb_ref, dg_ref, db_ref, l_ref = refs
        else:
            do_ref, xh_ref, r_ref, g_ref, dz_ref, dzb_ref, dg_ref, db_ref = refs

        @pl.when(pl.program_id(0) == 0)
        def _():
            dg_ref[...] = jnp.zeros_like(dg_ref)
            db_ref[...] = jnp.zeros_like(db_ref)
            if head:
                l_ref[...] = jnp.zeros_like(l_ref)

        xh = xh_ref[...]
        if head:
            e = xh * g_ref[...] + b_ref[...] - do_ref[...]
            l_ref[...] += jnp.sum(e * e, axis=0, keepdims=True)
            do = e * (1.0 / D)
        else:
            do = do_ref[...]
        dxh = do * g_ref[...]
        m1 = jnp.mean(dxh, axis=-1, keepdims=True)
        m2 = jnp.mean(dxh * xh, axis=-1, keepdims=True)
        dz = r_ref[...] * (dxh - m1 - xh * m2)
        dz_ref[...] = dz
        dzb_ref[...] = dz.astype(BF16)
        dg_ref[...] += jnp.sum(do * xh, axis=0, keepdims=True)
        db_ref[...] += jnp.sum(do, axis=0, keepdims=True)

    row = pl.BlockSpec((tm, D), lambda i: (i, 0))
    vec = pl.BlockSpec((1, D), lambda i: (0, 0))
    st = pl.BlockSpec((tm, 1), lambda i: (i, 0))
    ins = [dout, xhat, rstd, g.reshape(1, D)] + ([loss_b.reshape(1, D)] if head else [])
    return pl.pallas_call(
        kern, grid=(t // tm,), in_specs=[row, row, st] + [vec] * (len(ins) - 3),
        out_specs=[row, row, vec, vec] + ([vec] if head else []),
        out_shape=[jax.ShapeDtypeStruct((t, D), F32), jax.ShapeDtypeStruct((t, D), BF16)]
        + [jax.ShapeDtypeStruct((1, D), F32)] * (3 if head else 2),
        name=name, compiler_params=_cp())(*ins)


def _rope_tables(t):
    half = ROPE // 2
    inv = 10000.0 ** (-jnp.arange(half, dtype=F32) / half)
    ang = jnp.arange(t).astype(F32)[:, None] * inv[None, :]
    cos, sin = jnp.cos(ang), jnp.sin(ang)
    z = jnp.zeros((t, RP - ROPE), F32)
    return jnp.concatenate([cos, cos, z], axis=1), jnp.concatenate([-sin, sin, z], axis=1)


def _swap_halves(x):
    lane = lax.broadcasted_iota(jnp.int32, x.shape, 1)
    return jnp.where(lane < ROPE // 2, pltpu.roll(x, RP - ROPE // 2, 1), pltpu.roll(x, ROPE // 2, 1))


def _rope(x, cos, sin):
    return x * cos + _swap_halves(x) * sin


def _rope_t(gy, cos, sin):
    return gy * cos + _swap_halves(gy * sin)


def _mla_pre(hh, g_q, g_kv, cos, sin, tm=1024):
    t = hh.shape[0]
    tm = min(tm, t)

    def kern(h_ref, gq_ref, gkv_ref, c_ref, s_ref, cq_ref, k_ref):
        xq = h_ref[:, 0:QR]
        rq = lax.rsqrt(jnp.mean(xq * xq, axis=-1, keepdims=True) + RMS_EPS)
        cq_ref[...] = (xq * rq * gq_ref[...]).astype(BF16)
        xk = h_ref[:, QR:QR + KVR]
        rk = lax.rsqrt(jnp.mean(xk * xk, axis=-1, keepdims=True) + RMS_EPS)
        k_ref[:, 0:KVR] = (xk * rk * gkv_ref[...]).astype(BF16)
        k_ref[:, KVR:KD] = _rope(h_ref[:, QR + KVR:HW], c_ref[...], s_ref[...]).astype(BF16)

    return pl.pallas_call(
        kern, grid=(t // tm,),
        in_specs=[pl.BlockSpec((tm, HW), lambda i: (i, 0)), pl.BlockSpec((1, QR), lambda i: (0, 0)),
                  pl.BlockSpec((1, KVR), lambda i: (0, 0)), pl.BlockSpec((tm, RP), lambda i: (i, 0)),
                  pl.BlockSpec((tm, RP), lambda i: (i, 0))],
        out_specs=[pl.BlockSpec((tm, QR), lambda i: (i, 0)), pl.BlockSpec((tm, KD), lambda i: (i, 0))],
        out_shape=[jax.ShapeDtypeStruct((t, QR), BF16), jax.ShapeDtypeStruct((t, KD), BF16)],
        name="mla_pre", compiler_params=_cp())(hh, g_q.reshape(1, QR), g_kv.reshape(1, KVR), cos, sin)


def _mla_pre_bwd(hh, dcq, dk, g_q, g_kv, cos, sin, tm=1024):
    t = hh.shape[0]
    tm = min(tm, t)

    def rms_bwd(x, dy, g):
        r = lax.rsqrt(jnp.mean(x * x, axis=-1, keepdims=True) + RMS_EPS)
        gdy = dy * g
        dx = r * gdy - x * (r * r * r) * jnp.mean(gdy * x, axis=-1, keepdims=True)
        return dx, jnp.sum(dy * x * r, axis=0, keepdims=True)

    def kern(h_ref, dcq_ref, dk_ref, gq_ref, gkv_ref, c_ref, s_ref, dh_ref, dgq_ref, dgkv_ref):
        @pl.when(pl.program_id(0) == 0)
        def _():
            dgq_ref[...] = jnp.zeros_like(dgq_ref)
            dgkv_ref[...] = jnp.zeros_like(dgkv_ref)

        dxq, dgq = rms_bwd(h_ref[:, 0:QR], dcq_ref[...], gq_ref[...])
        dxk, dgk = rms_bwd(h_ref[:, QR:QR + KVR], dk_ref[:, 0:KVR], gkv_ref[...])
        dh_ref[:, 0:QR] = dxq.astype(BF16)
        dh_ref[:, QR:QR + KVR] = dxk.astype(BF16)
        dh_ref[:, QR + KVR:HW] = _rope_t(dk_ref[:, KVR:KD], c_ref[...], s_ref[...]).astype(BF16)
        dgq_ref[...] += dgq
        dgkv_ref[...] += dgk

    return pl.pallas_call(
        kern, grid=(t // tm,),
        in_specs=[pl.BlockSpec((tm, HW), lambda i: (i, 0)), pl.BlockSpec((tm, QR), lambda i: (i, 0)),
                  pl.BlockSpec((tm, KD), lambda i: (i, 0)), pl.BlockSpec((1, QR), lambda i: (0, 0)),
                  pl.BlockSpec((1, KVR), lambda i: (0, 0)), pl.BlockSpec((tm, RP), lambda i: (i, 0)),
                  pl.BlockSpec((tm, RP), lambda i: (i, 0))],
        out_specs=[pl.BlockSpec((tm, HW), lambda i: (i, 0)), pl.BlockSpec((1, QR), lambda i: (0, 0)),
                   pl.BlockSpec((1, KVR), lambda i: (0, 0))],
        out_shape=[jax.ShapeDtypeStruct((t, HW), BF16), jax.ShapeDtypeStruct((1, QR), F32),
                   jax.ShapeDtypeStruct((1, KVR), F32)],
        name="mla_pre_bwd", compiler_params=_cp())(hh, dcq, dk, g_q.reshape(1, QR), g_kv.reshape(1, KVR), cos, sin)


def _q_prep(q2, wuk_t, cos, sin, tm=1024):
    t = q2.shape[0]
    tm = min(tm, t)

    def kern(q_ref, w_ref, c_ref, s_ref, o_ref):
        cos_, sin_ = c_ref[...], s_ref[...]
        for h in range(H):
            qn = q_ref[:, h * NOPE:(h + 1) * NOPE].astype(BF16)
            o_ref[:, h * KD:h * KD + KVR] = (_dot(qn, w_ref[h], NN) * QSCALE).astype(BF16)
            qr = q_ref[:, H * NOPE + h * RP:H * NOPE + (h + 1) * RP]
            o_ref[:, h * KD + KVR:(h + 1) * KD] = (_rope(qr, cos_, sin_) * QSCALE).astype(BF16)

    return pl.pallas_call(
        kern, grid=(t // tm,),
        in_specs=[pl.BlockSpec((tm, 2 * H * NOPE), lambda i: (i, 0)), pl.BlockSpec((H, NOPE, KVR), lambda i: (0, 0, 0)),
                  pl.BlockSpec((tm, RP), lambda i: (i, 0)), pl.BlockSpec((tm, RP), lambda i: (i, 0))],
        out_specs=pl.BlockSpec((tm, H * KD), lambda i: (i, 0)),
        out_shape=jax.ShapeDtypeStruct((t, H * KD), BF16),
        name="q_prep", compiler_params=_cp())(q2, wuk_t, cos, sin)


def _o_up_bwd(do, o_lat, wuv_h, tm=1024):
    t = do.shape[0]
    tm = min(tm, t)

    def kern(do_ref, x_ref, w_ref, dx_ref, dw_ref, dlt_ref):
        @pl.when(pl.program_id(0) == 0)
        def _():
            dw_ref[...] = jnp.zeros_like(dw_ref)

        for h in range(H):
            dh_ = do_ref[:, h * NOPE:(h + 1) * NOPE]
            x = x_ref[:, h * KVR:(h + 1) * KVR]
            dx = _dot(dh_, w_ref[h], NT)
            dx_ref[:, h * KVR:(h + 1) * KVR] = dx.astype(BF16)
            dw_ref[h] += _dot(x.astype(BF16), dh_, TN)
            dl = jnp.broadcast_to(jnp.sum(dx * x, axis=1)[:, None], (tm, 128))
            dlt_ref[h] = dl.T[0:1, :]

    return pl.pallas_call(
        kern, grid=(t // tm,),
        in_specs=[pl.BlockSpec((tm, H * NOPE), lambda i: (i, 0)), pl.BlockSpec((tm, H * KVR), lambda i: (i, 0)),
                  pl.BlockSpec((H, KVR, NOPE), lambda i: (0, 0, 0))],
        out_specs=[pl.BlockSpec((tm, H * KVR), lambda i: (i, 0)), pl.BlockSpec((H, KVR, NOPE), lambda i: (0, 0, 0)),
                   pl.BlockSpec((H, 1, tm), lambda i: (0, 0, i))],
        out_shape=[jax.ShapeDtypeStruct((t, H * KVR), BF16), jax.ShapeDtypeStruct((H, KVR, NOPE), F32),
                   jax.ShapeDtypeStruct((H, 1, t), F32)],
        name="o_up_bwd", compiler_params=_cp())(do, o_lat, wuv_h)


def _causal_pairs(nq):
    return [(i, j) for i in range(nq) for j in range(i + 1)]


def _lane_tile(stat, width):
    return jnp.tile(stat, (1, width // 128))


def _flash_fwd(qcat, kc, wuv_h, bq, hb, gather=None):
    t = kc.shape[0]
    nq = t // bq
    pairs = _causal_pairs(nq)
    itab = jnp.asarray(np.array([p[0] for p in pairs], np.int32))
    jtab = jnp.asarray(np.array([p[1] for p in pairs], np.int32))

    ng = H // hb
    hosting = gather is not None

    def kern(it, jt, q_ref, k_ref, wuv_ref, *rest):
        if hosting:
            w_ref, o_ref, lset_ref, oup_ref, wall_ref, m_sc, l_sc, acc_sc, send_sems, recv_sems = rest
            ag_start, ag_forward, ag_finish = _allgather_schedule(w_ref, wall_ref, send_sems, recv_sems)
        else:
            o_ref, lset_ref, oup_ref, m_sc, l_sc, acc_sc = rest
        grp = pl.program_id(0)
        st = pl.program_id(1)
        i, j = it[st], jt[st]

        if hosting:
            @pl.when((grp == 0) & (st == 0))
            def _():
                ag_start()

        @pl.when(j == 0)
        def _():
            m_sc[...] = jnp.full_like(m_sc, NEG)
            l_sc[...] = jnp.zeros_like(l_sc)
            acc_sc[...] = jnp.zeros_like(acc_sc)

        def update(masked):
            k = k_ref[...]
            v = k[:, 0:KVR]
            if masked:
                row = lax.broadcasted_iota(jnp.int32, (bq, bq), 0)
                col = lax.broadcasted_iota(jnp.int32, (bq, bq), 1)
                keep = col <= row
            pending = [_dot(q_ref[:, hh * KD:(hh + 1) * KD], k, NT) for hh in range(min(AHEAD, hb))]
            for hh in range(hb):
                s = pending.pop(0)
                if hh + AHEAD < hb:
                    pending.append(_dot(q_ref[:, (hh + AHEAD) * KD:(hh + AHEAD + 1) * KD], k, NT))
                if masked:
                    s = jnp.where(keep, s, NEG)
                m_prev = m_sc[hh]
                m_next = jnp.maximum(m_prev, jnp.max(s, axis=1)[:, None])
                p = jnp.exp2(s - _lane_tile(m_next, bq))
                a = jnp.exp2(m_prev - m_next)
                l_sc[hh] = a * l_sc[hh] + jnp.sum(p, axis=1)[:, None]
                acc_sc[hh] = _lane_tile(a, KVR) * acc_sc[hh] + _dot(p.astype(BF16), v, NN)
                m_sc[hh] = m_next

        @pl.when(j < i)
        def _():
            update(False)

        @pl.when(j == i)
        def _():
            update(True)
            for hh in range(hb):
                l = l_sc[hh]
                o_h = acc_sc[hh] / _lane_tile(l, KVR)
                o_ref[:, hh * KVR:(hh + 1) * KVR] = o_h
                oup_ref[:, hh * NOPE:(hh + 1) * NOPE] = _dot(o_h.astype(BF16), wuv_ref[hh], NN).astype(BF16)
                lset_ref[hh] = (m_sc[hh] + jnp.log2(l)).T[0:1, :]

        if hosting:
            half_way = (ng * len(pairs)) // 2

            @pl.when(grp * len(pairs) + st == half_way)
            def _():
                ag_forward()

            @pl.when((grp == ng - 1) & (st == len(pairs) - 1))
            def _():
                ag_finish()

    in_specs = [pl.BlockSpec((bq, hb * KD), lambda g, s, it, jt: (it[s], g)),
                pl.BlockSpec((bq, KD), lambda g, s, it, jt: (jt[s], 0)),
                pl.BlockSpec((hb, KVR, NOPE), lambda g, s, it, jt: (g, 0, 0))]
    out_specs = [pl.BlockSpec((bq, hb * KVR), lambda g, s, it, jt: (it[s], g)),
                 pl.BlockSpec((hb, 1, bq), lambda g, s, it, jt: (g, 0, it[s])),
                 pl.BlockSpec((bq, hb * NOPE), lambda g, s, it, jt: (it[s], g))]
    out_shape = [jax.ShapeDtypeStruct((t, H * KVR), F32), jax.ShapeDtypeStruct((H, 1, t), F32),
                 jax.ShapeDtypeStruct((t, H * NOPE), BF16)]
    scratch = [pltpu.VMEM((hb, bq, 128), F32), pltpu.VMEM((hb, bq, 128), F32), pltpu.VMEM((hb, bq, KVR), F32)]
    args = [itab, jtab, qcat, kc, wuv_h]
    if hosting:
        in_specs.append(ANY)
        out_specs.append(ANY)
        out_shape.append(jax.ShapeDtypeStruct((4,) + gather.shape, gather.dtype))
        scratch += AG_SEMS
        args.append(gather)
    gs = pltpu.PrefetchScalarGridSpec(num_scalar_prefetch=2, grid=(ng, len(pairs)), in_specs=in_specs,
                                      out_specs=out_specs, scratch_shapes=scratch)
    return pl.pallas_call(kern, grid_spec=gs, out_shape=out_shape, name="mla_flash_fwd",
                          compiler_params=_cp())(*args)


def _flash_dkv(qcat, kc, do_lat, lse_t, delta_t, bq, hb, exchange=()):
    nx = len(exchange)
    t = kc.shape[0]
    nq = t // bq
    ng = H // hb
    npairs = nq * (nq + 1) // 2
    steps = [(j, g, i) for j in range(nq) for g in range(ng) for i in range(j, nq)]
    jtab = jnp.asarray(np.array([s[0] for s in steps], np.int32))
    gtab = jnp.asarray(np.array([s[1] for s in steps], np.int32))
    itab = jnp.asarray(np.array([s[2] for s in steps], np.int32))
    ptab = jnp.asarray(np.array([s[2] * (s[2] + 1) // 2 + s[0] for s in steps], np.int32))

    def kern(jt, gt, it, pt, q_ref, k_ref, do_ref, lset_ref, dlt_ref, *rest):
        p_refs, (dk_ref, ds_ref), slots_refs = rest[:nx], rest[nx:nx + 2], rest[nx + 2:2 * nx + 2]
        dk_sc, dv_sc = rest[2 * nx + 2:2 * nx + 4]
        sems = rest[2 * nx + 4:]
        hooks = [_device_exchange_schedule(p_refs[e], slots_refs[e], sems[2 * e], sems[2 * e + 1]) for e in range(nx)]
        st = pl.program_id(0)
        j, g, i = jt[st], gt[st], it[st]

        if nx:
            @pl.when(st == 0)
            def _():
                for start, _ in hooks:
                    start()

        @pl.when((g == 0) & (i == j))
        def _():
            dk_sc[...] = jnp.zeros_like(dk_sc)
            dv_sc[...] = jnp.zeros_like(dv_sc)

        def update(masked):
            k = k_ref[...]
            v = k[:, 0:KVR]
            if masked:
                row = lax.broadcasted_iota(jnp.int32, (bq, bq), 0)
                col = lax.broadcasted_iota(jnp.int32, (bq, bq), 1)
                keep = row <= col

            def first_matmuls(hh):
                dob = do_ref[:, hh * KVR:(hh + 1) * KVR].astype(BF16)
                return _dot(k, q_ref[:, hh * KD:(hh + 1) * KD], NT), _dot(v, dob, NT), dob

            pending = [first_matmuls(hh) for hh in range(min(AHEAD, hb))]
            for hh in range(hb):
                s, dp, dob = pending.pop(0)
                if hh + AHEAD < hb:
                    pending.append(first_matmuls(hh + AHEAD))
                if masked:
                    s = jnp.where(keep, s, NEG)
                p = jnp.exp2(s - lset_ref[hh])
                dv_sc[...] += _dot(p.astype(BF16), dob, NN)
                dsb = (p * (dp - dlt_ref[hh])).astype(BF16)
                ds_ref[0, 0, hh] = dsb
                dk_sc[...] += _dot(dsb, q_ref[:, hh * KD:(hh + 1) * KD], NN)

        @pl.when(i > j)
        def _():
            update(False)

        @pl.when(i == j)
        def _():
            update(True)

        @pl.when((g == ng - 1) & (i == nq - 1))
        def _():
            dk_ref[:, 0:KVR] = dk_sc[:, 0:KVR] * LN2 + dv_sc[...]
            dk_ref[:, KVR:KD] = dk_sc[:, KVR:KD] * LN2

        if nx:
            @pl.when(st == len(steps) - 1)
            def _():
                for _, finish in hooks:
                    finish()

    in_specs = [pl.BlockSpec((bq, hb * KD), lambda s, jt, gt, it, pt: (it[s], gt[s])),
                pl.BlockSpec((bq, KD), lambda s, jt, gt, it, pt: (jt[s], 0)),
                pl.BlockSpec((bq, hb * KVR), lambda s, jt, gt, it, pt: (it[s], gt[s])),
                pl.BlockSpec((hb, 1, bq), lambda s, jt, gt, it, pt: (gt[s], 0, it[s])),
                pl.BlockSpec((hb, 1, bq), lambda s, jt, gt, it, pt: (gt[s], 0, it[s]))] + [ANY] * nx
    out_specs = [pl.BlockSpec((bq, KD), lambda s, jt, gt, it, pt: (jt[s], 0)),
                 pl.BlockSpec((1, 1, hb, bq, bq), lambda s, jt, gt, it, pt: (gt[s], pt[s], 0, 0, 0))] + [ANY] * nx
    out_shape = [jax.ShapeDtypeStruct((t, KD), F32), jax.ShapeDtypeStruct((ng, npairs, hb, bq, bq), BF16)]
    out_shape += [jax.ShapeDtypeStruct((8, e.shape[1] // 2, D), e.dtype) for e in exchange]
    scratch = [pltpu.VMEM((bq, KD), F32), pltpu.VMEM((bq, KVR), F32)] + ALL_SEMS * nx
    args = [jtab, gtab, itab, ptab, qcat, kc, do_lat, lse_t, delta_t, *exchange]
    gs = pltpu.PrefetchScalarGridSpec(num_scalar_prefetch=4, grid=(len(steps),), in_specs=in_specs,
                                      out_specs=out_specs, scratch_shapes=scratch)
    return pl.pallas_call(kern, grid_spec=gs, out_shape=out_shape, name="mla_flash_dkv",
                          compiler_params=_cp())(*args)


def _flash_dq(ds_all, kc_t, q2, wuk_h, cos, sin, bq, exchange=None):
    nq = kc_t.shape[0]
    t = nq * bq
    ngrp, _, hper = ds_all.shape[:3]
    pairs = _causal_pairs(nq)
    itab = jnp.asarray(np.array([p[0] for p in pairs], np.int32))
    jtab = jnp.asarray(np.array([p[1] for p in pairs], np.int32))
    hosting = exchange is not None

    def kern(it, jt, *refs):
        ds_refs, (kt_ref, q_ref, w_ref, c_ref, s_ref), rest = refs[:ngrp], refs[ngrp:ngrp + 5], refs[ngrp + 5:]
        if hosting:
            p_ref, dq_ref, dw_ref, slots_ref, acc_sc, send_sems, recv_sems = rest
            xc_start, xc_finish = _device_exchange_schedule(p_ref, slots_ref, send_sems, recv_sems)
        else:
            dq_ref, dw_ref, acc_sc = rest
        st = pl.program_id(0)
        i, j = it[st], jt[st]
        kt = kt_ref[...]

        def ds(hh):
            return ds_refs[hh // hper][0, 0, hh % hper]

        @pl.when(st == 0)
        def _():
            dw_ref[...] = jnp.zeros_like(dw_ref)
            if hosting:
                xc_start()

        @pl.when(j == 0)
        def _():
            for hh in range(H):
                acc_sc[hh] = _dot(kt, ds(hh), NN)

        @pl.when((j > 0) & (j < i))
        def _():
            for hh in range(H):
                acc_sc[hh] += _dot(kt, ds(hh), NN)

        @pl.when(j == i)
        def _():
            cos_, sin_ = c_ref[...], s_ref[...]
            for hh in range(H):
                tot = _dot(kt, ds(hh), NN)
                tot = jnp.where(i > 0, tot + acc_sc[hh], tot)
                dq_h = tot.T * MLA_SCALE
                dql = dq_h[:, 0:KVR].astype(BF16)
                dq_ref[:, hh * NOPE:(hh + 1) * NOPE] = _dot(dql, w_ref[hh], NN).astype(BF16)
                dq_ref[:, H * NOPE + hh * RP:H * NOPE + (hh + 1) * RP] = _rope_t(dq_h[:, KVR:KD], cos_, sin_).astype(BF16)
                dw_ref[hh] += _dot(q_ref[:, hh * NOPE:(hh + 1) * NOPE].astype(BF16), dql, TN)

        if hosting:
            @pl.when(st == len(pairs) - 1)
            def _():
                xc_finish()

    def group(gi):
        return pl.BlockSpec((1, 1, hper, bq, bq), lambda s, it, jt: (gi, s, 0, 0, 0))

    in_specs = [group(gi) for gi in range(ngrp)] + [
        pl.BlockSpec((None, KD, bq), lambda s, it, jt: (jt[s], 0, 0)),
        pl.BlockSpec((bq, 2 * H * NOPE), lambda s, it, jt: (it[s], 0)),
        pl.BlockSpec((H, KVR, NOPE), lambda s, it, jt: (0, 0, 0)),
        pl.BlockSpec((bq, RP), lambda s, it, jt: (it[s], 0)), pl.BlockSpec((bq, RP), lambda s, it, jt: (it[s], 0))]
    out_specs = [pl.BlockSpec((bq, 2 * H * NOPE), lambda s, it, jt: (it[s], 0)),
                 pl.BlockSpec((H, NOPE, KVR), lambda s, it, jt: (0, 0, 0))]
    out_shape = [jax.ShapeDtypeStruct((t, 2 * H * NOPE), BF16), jax.ShapeDtypeStruct((H, NOPE, KVR), F32)]
    scratch = [pltpu.VMEM((H, KD, bq), F32)]
    args = [itab, jtab] + [ds_all] * ngrp + [kc_t, q2, wuk_h, cos, sin]
    if hosting:
        in_specs.append(ANY)
        out_specs.append(ANY)
        out_shape.append(jax.ShapeDtypeStruct((8, exchange.shape[1] // 2, D), exchange.dtype))
        scratch += ALL_SEMS
        args.append(exchange)
    gs = pltpu.PrefetchScalarGridSpec(num_scalar_prefetch=2, grid=(len(pairs),), in_specs=in_specs,
                                      out_specs=out_specs, scratch_shapes=scratch)
    return pl.pallas_call(kern, grid_spec=gs, out_shape=out_shape, name="mla_flash_dq",
                          compiler_params=_cp())(*args)


def _bucket_table():
    d = np.arange(WIN)
    max_exact = NBKT // 2
    nf = np.maximum(d, 1).astype(np.float32)
    large = max_exact + (np.log(nf / np.float32(max_exact)) / np.float32(math.log(WIN / max_exact))
                         * np.float32(NBKT - max_exact)).astype(np.int32)
    large = np.minimum(large, NBKT - 1)
    bucket = np.where(d < max_exact, d, large).astype(np.int32)
    jj = np.arange(2 * WIN)[:, None]
    ii = np.arange(WIN)[None, :]
    dist = ii + WIN - jj
    valid = (dist >= 0) & (dist < WIN)
    return np.where(valid, bucket[np.clip(dist, 0, WIN - 1)], -1).astype(np.int32)


def _bias_build(rel_bias, bkt):
    def kern(bk_ref, rb_ref, o_ref):
        bk = bk_ref[...]
        for hd in range(QH):
            acc = jnp.full((2 * WIN, WIN), NEG, F32)
            for b in range(NBKT):
                acc = jnp.where(bk == b, rb_ref[b, hd], acc)
            o_ref[hd] = acc

    return pl.pallas_call(
        kern, in_specs=[pl.BlockSpec(memory_space=pltpu.VMEM), pl.BlockSpec(memory_space=pltpu.SMEM)],
        out_specs=pl.BlockSpec(memory_space=pltpu.VMEM),
        out_shape=jax.ShapeDtypeStruct((QH, 2 * WIN, WIN), F32), name="swa_bias_build")(bkt, rel_bias)


def _bias_bwd(dbias, bkt):
    def kern(db_ref, bk_ref, o_ref):
        bk = bk_ref[...]
        for hd in range(QH):
            g = db_ref[hd]
            for b in range(NBKT):
                r = b * QH + hd
                o_ref[r:r + 1, :] = jnp.sum(jnp.where(bk == b, g, 0.0), axis=0, keepdims=True)

    return pl.pallas_call(
        kern, in_specs=[pl.BlockSpec(memory_space=pltpu.VMEM), pl.BlockSpec(memory_space=pltpu.VMEM)],
        out_specs=pl.BlockSpec(memory_space=pltpu.VMEM),
        out_shape=jax.ShapeDtypeStruct((NBKT * QH, WIN), F32), name="swa_bias_bwd")(dbias, bkt)


def _swa_finish_scores(raw, bias, first):
    s = raw * SWA_SCALE + bias
    if first is not None:
        row = lax.broadcasted_iota(jnp.int32, s.shape, 0)
        s = jnp.where(jnp.logical_or(jnp.logical_not(first), row >= WIN), s, NEG)
    return s


def _swa_fwd(qkv_t, bias, sinks, qb):
    t = qkv_t.shape[1]
    w = qb * WIN
    nst = t // w

    def kern(q_ref, kc_ref, kp_ref, vc_ref, vp_ref, b_ref, sk_ref, o_ref, lse_ref):
        n = pl.program_id(0)
        kfull = jnp.concatenate([kp_ref[...], kc_ref[...]], axis=1)
        vfull = jnp.concatenate([vp_ref[...], vc_ref[...]], axis=1)
        head_row = lax.broadcasted_iota(jnp.int32, (QH, WIN), 0)
        groups = [(b, kh) for b in range(qb) for kh in range(KVH)]

        def raw_scores(b, kh):
            k_band = kfull[kh * HD:(kh + 1) * HD, b * WIN:(b + 2) * WIN]
            return [_dot(k_band, q_ref[(kh * G + g) * HD:(kh * G + g + 1) * HD, b * WIN:(b + 1) * WIN], TN)
                    for g in range(G)]

        o_rows = [[] for _ in range(qb)]
        lse_tiles = [jnp.zeros((QH, WIN), F32) for _ in range(qb)]
        pending = [raw_scores(*grp) for grp in groups[:AHEAD]]
        for gi, (b, kh) in enumerate(groups):
            scores = pending.pop(0)
            if gi + AHEAD < len(groups):
                pending.append(raw_scores(*groups[gi + AHEAD]))
            v_band = vfull[kh * HD:(kh + 1) * HD, b * WIN:(b + 2) * WIN]
            for g in range(G):
                hd = kh * G + g
                s = _swa_finish_scores(scores[g], b_ref[hd], (n == 0) if b == 0 else None)
                sink = sk_ref[hd]
                m = jnp.maximum(jnp.max(s, axis=0, keepdims=True), sink)
                p = jnp.exp(s - m)
                den = jnp.sum(p, axis=0, keepdims=True) + jnp.exp(sink - m)
                p = p / den
                o_rows[b].append(_dot(v_band, p.astype(BF16), NN))
                lse_tiles[b] = jnp.where(head_row == hd, m + jnp.log(den), lse_tiles[b])
        o_ref[...] = jnp.concatenate([jnp.concatenate(rows, axis=0) for rows in o_rows], axis=1)
        lse_ref[...] = jnp.concatenate(lse_tiles, axis=1)

    prev = lambda r: (lambda n: (r, jnp.maximum(n * qb - 1, 0)))
    return pl.pallas_call(
        kern, grid=(nst,),
        in_specs=[pl.BlockSpec((QH * HD, w), lambda n: (0, n)),
                  pl.BlockSpec((KVH * HD, w), lambda n: (4, n)), pl.BlockSpec((KVH * HD, WIN), prev(4)),
                  pl.BlockSpec((KVH * HD, w), lambda n: (5, n)), pl.BlockSpec((KVH * HD, WIN), prev(5)),
                  pl.BlockSpec((QH, 2 * WIN, WIN), lambda n: (0, 0, 0)),
                  pl.BlockSpec(memory_space=pltpu.SMEM)],
        out_specs=[pl.BlockSpec((QH * HD, w), lambda n: (0, n)), pl.BlockSpec((QH, w), lambda n: (0, n))],
        out_shape=[jax.ShapeDtypeStruct((QH * HD, t), F32), jax.ShapeDtypeStruct((QH, t), F32)],
        name="swa_fwd", compiler_params=_cp())(qkv_t, qkv_t, qkv_t, qkv_t, qkv_t, bias, sinks)


def _swa_bwd(qkv_t, do_t, o_t, lse, bias, sinks, qb):
    t = qkv_t.shape[1]
    w = qb * WIN
    nst = t // w
    nblk = t // WIN

    def kern(q_ref, kc_ref, kp_ref, vc_ref, vp_ref, do_ref, o_ref, lse_ref, qn_ref, don_ref, on_ref, lsen_ref,
             b_ref, sk_ref, dqkv_ref, db_ref, dsk_ref):
        n = pl.program_id(0)

        @pl.when(n == 0)
        def _():
            db_ref[...] = jnp.zeros_like(db_ref)
            dsk_ref[...] = jnp.zeros_like(dsk_ref)

        kfull = jnp.concatenate([kp_ref[...], kc_ref[...]], axis=1)
        vfull = jnp.concatenate([vp_ref[...], vc_ref[...]], axis=1)
        head_row = lax.broadcasted_iota(jnp.int32, (QH, WIN), 0)
        db_acc = [None] * QH
        dsk_tile = jnp.zeros((QH, WIN), F32)
        prev_part = [[[None] * qb for _ in range(KVH)] for _ in range(2)]
        cur_part = [[[None] * qb for _ in range(KVH)] for _ in range(2)]
        groups = [(b, kh) for b in range(qb) for kh in range(KVH)]

        def first_matmuls(b, kh):
            k_band = kfull[kh * HD:(kh + 1) * HD, b * WIN:(b + 2) * WIN]
            v_band = vfull[kh * HD:(kh + 1) * HD, b * WIN:(b + 2) * WIN]
            out = []
            for g in range(G):
                rs = slice((kh * G + g) * HD, (kh * G + g + 1) * HD)
                dob = do_ref[rs, b * WIN:(b + 1) * WIN].astype(BF16)
                out.append((_dot(k_band, q_ref[rs, b * WIN:(b + 1) * WIN], TN), _dot(v_band, dob, TN), dob))
            return out

        dq_rows = [[] for _ in range(qb)]
        pending = [first_matmuls(*grp) for grp in groups[:AHEAD]]
        for gi, (b, kh) in enumerate(groups):
            first = pending.pop(0)
            if gi + AHEAD < len(groups):
                pending.append(first_matmuls(*groups[gi + AHEAD]))
            cs = slice(b * WIN, (b + 1) * WIN)
            k_band = kfull[kh * HD:(kh + 1) * HD, b * WIN:(b + 2) * WIN]
            dk_b = dv_b = None
            for g in range(G):
                hd = kh * G + g
                rs = slice(hd * HD, (hd + 1) * HD)
                raw, dp, dob = first[g]
                lse_h = lse_ref[hd:hd + 1, cs]
                s = _swa_finish_scores(raw, b_ref[hd], (n == 0) if b == 0 else None)
                p = jnp.exp(s - lse_h)
                dl = jnp.sum(do_ref[rs, cs] * o_ref[rs, cs], axis=0, keepdims=True)
                ds = p * (dp - dl)
                db_acc[hd] = ds if db_acc[hd] is None else db_acc[hd] + ds
                dsk_tile = jnp.where(head_row == hd, dsk_tile - jnp.exp(sk_ref[hd] - lse_h) * dl, dsk_tile)
                dss = (ds * SWA_SCALE).astype(BF16)
                dq_rows[b].append(_dot(k_band, dss, NN).astype(BF16))
                dk_h = _dot(q_ref[rs, cs], dss, NT)
                dv_h = _dot(dob, p.astype(BF16), NT)
                dk_b = dk_h if dk_b is None else dk_b + dk_h
                dv_b = dv_h if dv_b is None else dv_b + dv_h
            for which, val in ((0, dk_b), (1, dv_b)):
                prev_part[which][kh][b] = val[:, 0:WIN]
                cur_part[which][kh][b] = val[:, WIN:2 * WIN]
        dq_cols = [jnp.concatenate(rows, axis=0) for rows in dq_rows]

        live = n < nst - 1
        ls = slice((qb - 1) * WIN, qb * WIN)
        halo = [[None] * KVH for _ in range(2)]
        for kh in range(KVH):
            k_last = kc_ref[kh * HD:(kh + 1) * HD, ls]
            v_last = vc_ref[kh * HD:(kh + 1) * HD, ls]
            dk_b = dv_b = None
            for g in range(G):
                hd = kh * G + g
                rs = slice(hd * HD, (hd + 1) * HD)
                q_t = qn_ref[rs, :]
                do = don_ref[rs, :]
                s = _dot(k_last, q_t, TN) * SWA_SCALE + b_ref[hd, 0:WIN, :]
                p = jnp.exp(s - lsen_ref[hd:hd + 1, :])
                dob = do.astype(BF16)
                dp = _dot(v_last, dob, TN)
                dl = jnp.sum(do * on_ref[rs, :], axis=0, keepdims=True)
                dss = (p * (dp - dl) * SWA_SCALE).astype(BF16)
                dk_h = _dot(q_t, dss, NT)
                dv_h = _dot(dob, p.astype(BF16), NT)
                dk_b = dk_h if dk_b is None else dk_b + dk_h
                dv_b = dv_h if dv_b is None else dv_b + dv_h
            halo[0][kh] = jnp.where(live, dk_b, 0.0)
            halo[1][kh] = jnp.where(live, dv_b, 0.0)

        kv_rows = []
        for which in range(2):
            for kh in range(KVH):
                blocks = [cur_part[which][kh][p] + (prev_part[which][kh][p + 1] if p + 1 < qb else halo[which][kh])
                          for p in range(qb)]
                kv_rows.append(jnp.concatenate(blocks, axis=1))
        dqkv_ref[...] = jnp.concatenate(
            [jnp.concatenate(dq_cols, axis=1), jnp.concatenate(kv_rows, axis=0).astype(BF16)], axis=0)
        db_ref[...] += jnp.stack(db_acc)
        dsk_ref[...] += dsk_tile

    prev = lambda r: (lambda n: (r, jnp.maximum(n * qb - 1, 0)))
    nxt = lambda n: (0, jnp.minimum((n + 1) * qb, nblk - 1))
    big = lambda: pl.BlockSpec((QH * HD, w), lambda n: (0, n))
    return pl.pallas_call(
        kern, grid=(nst,),
        in_specs=[big(),
                  pl.BlockSpec((KVH * HD, w), lambda n: (4, n)), pl.BlockSpec((KVH * HD, WIN), prev(4)),
                  pl.BlockSpec((KVH * HD, w), lambda n: (5, n)), pl.BlockSpec((KVH * HD, WIN), prev(5)),
                  big(), big(), pl.BlockSpec((QH, w), lambda n: (0, n)),
                  pl.BlockSpec((QH * HD, WIN), nxt), pl.BlockSpec((QH * HD, WIN), nxt),
                  pl.BlockSpec((QH * HD, WIN), nxt), pl.BlockSpec((QH, WIN), nxt),
                  pl.BlockSpec((QH, 2 * WIN, WIN), lambda n: (0, 0, 0)),
                  pl.BlockSpec(memory_space=pltpu.SMEM)],
        out_specs=[pl.BlockSpec(((QH + 2 * KVH) * HD, w), lambda n: (0, n)),
                   pl.BlockSpec((QH, 2 * WIN, WIN), lambda n: (0, 0, 0)),
                   pl.BlockSpec((QH, WIN), lambda n: (0, 0))],
        out_shape=[jax.ShapeDtypeStruct(((QH + 2 * KVH) * HD, t), BF16),
                   jax.ShapeDtypeStruct((QH, 2 * WIN, WIN), F32), jax.ShapeDtypeStruct((QH, WIN), F32)],
        name="swa_bwd", compiler_params=_cp())(
            qkv_t, qkv_t, qkv_t, qkv_t, qkv_t, do_t, o_t, lse, qkv_t, do_t, o_t, lse, bias, sinks)


def _adamw_math(w, g, m, v):
    nm = B1 * m + (1.0 - B1) * g
    nv = B2 * v + (1.0 - B2) * (g * g)
    mhat = nm * (1.0 / (1.0 - B1 ** STEP))
    vhat = nv * (1.0 / (1.0 - B2 ** STEP))
    return -LR * (mhat / (jnp.sqrt(vhat) + ADAM_EPS) + WD * w), nm, nv


def _adamw_layers(w, m, v, g0buf, g1buf, off, name, tm=512):
    rows = w.shape[1]
    nb, ob = rows // tm, off // tm

    def kern(w_ref, m_ref, v_ref, g0_ref, g1_ref, gr_ref, d_ref, nm_ref, nv_ref):
        g_ = jnp.where(pl.program_id(0) == 0, g0_ref[...], g1_ref[...])
        gr_ref[...] = g_
        d_ref[...], nm_ref[...], nv_ref[...] = _adamw_math(w_ref[...], g_, m_ref[...], v_ref[...])

    lay = pl.BlockSpec((None, tm, D), lambda l, i: (l, i, 0))
    gsp = pl.BlockSpec((tm, D), lambda l, i: (ob + i, 0))
    return pl.pallas_call(
        kern, grid=(2, nb), in_specs=[lay, lay, lay, gsp, gsp], out_specs=[lay] * 4,
        out_shape=[jax.ShapeDtypeStruct(w.shape, F32)] * 4, name=name, compiler_params=_cp())(w, m, v, g0buf, g1buf)


def _adamw(w, g, m, v, name, tm=544):
    r = w.shape[0]
    tm = r if r % tm else tm

    def kern(w_ref, g_ref, m_ref, v_ref, d_ref, nm_ref, nv_ref):
        d_ref[...], nm_ref[...], nv_ref[...] = _adamw_math(w_ref[...], g_ref[...], m_ref[...], v_ref[...])

    row = pl.BlockSpec((tm, D), lambda i: (i, 0))
    sds = jax.ShapeDtypeStruct((r, D), F32)
    return pl.pallas_call(kern, grid=(r // tm,), in_specs=[row] * 4, out_specs=[row] * 3, out_shape=[sds] * 3,
                          name=name, compiler_params=_cp())(w, g, m, v)


def _mesh_pos():
    return lax.axis_index("x"), lax.axis_index("y"), lax.axis_index("c")


ANY = pl.BlockSpec(memory_space=pl.ANY)


AG_SEMS = [pltpu.SemaphoreType.DMA((6,)), pltpu.SemaphoreType.DMA((6,))]


def _allgather_schedule(w_ref, out_ref, send_sems, recv_sems):
    half = w_ref.shape[0] // 2
    x, y, c = _mesh_pos()
    me, sibling = (x, y, c), (x, y, 1 - c)
    chips = [(1 - x, y), (x, 1 - y), (1 - x, 1 - y)]

    def rows(px, py, pc):
        return out_ref.at[2 * px + py, pl.ds(pc * half, half), :]

    def copy(k, block, to, src=None):
        return pltpu.make_async_remote_copy(
            src_ref=rows(*block) if src is None else src, dst_ref=rows(*block),
            send_sem=send_sems.at[k], recv_sem=recv_sems.at[k], device_id=to, device_id_type=MESH)

    def first():
        return [copy(j, me, (*chip, c), src=w_ref.at[pl.ds(c * half, half), :]) for j, chip in enumerate(chips)]

    def passed():
        return [copy(3 + j, (*chip, c), sibling) for j, chip in enumerate(chips)]

    def start():
        for cp in first():
            cp.start()

    def forward():
        for j, chip in enumerate(chips):
            copy(j, (*chip, c), me).wait_recv()
            passed()[j].start()

    def finish():
        for j, chip in enumerate(chips):
            copy(3 + j, (*chip, 1 - c), me).wait_recv()
        for cp in first() + passed():
            cp.wait_send()

    return start, forward, finish


def _allgather_weights(wpack):
    def body(w_ref, out_ref, send_sems, recv_sems):
        start, forward, finish = _allgather_schedule(w_ref, out_ref, send_sems, recv_sems)
        start()
        forward()
        finish()

    return pl.pallas_call(
        body, out_shape=jax.ShapeDtypeStruct((4,) + wpack.shape, wpack.dtype), in_specs=[ANY], out_specs=ANY,
        scratch_shapes=AG_SEMS, name="allgather_weights")(wpack)


def _row_tile(rows):
    t = min(rows, 512)
    while rows % t or t % 16:
        t -= 16
    return t


ALL_SEMS = [pltpu.SemaphoreType.DMA((7,)), pltpu.SemaphoreType.DMA((7,))]


def _device_exchange_schedule(g_ref, out_ref, send_sems, recv_sems):
    half = g_ref.shape[1] // 2
    x, y, c = _mesh_pos()
    me = 4 * x + 2 * y + c
    peers = [(x ^ (k >> 2), y ^ ((k >> 1) & 1), c ^ (k & 1)) for k in range(1, 8)]

    def sends():
        return [pltpu.make_async_remote_copy(
            src_ref=g_ref.at[2 * px + py, pl.ds(pc * half, half), :], dst_ref=out_ref.at[me],
            send_sem=send_sems.at[j], recv_sem=recv_sems.at[j], device_id=(px, py, pc), device_id_type=MESH)
            for j, (px, py, pc) in enumerate(peers)]

    def start():
        for cp in sends():
            cp.start()

    def finish():
        for j, (px, py, pc) in enumerate(peers):
            pltpu.make_async_remote_copy(
                src_ref=out_ref.at[me], dst_ref=out_ref.at[4 * px + 2 * py + pc], send_sem=send_sems.at[j],
                recv_sem=recv_sems.at[j], device_id=(px, py, pc), device_id_type=MESH).wait_recv()
        for cp in sends():
            cp.wait_send()

    return start, finish


def _sum_devices(slots, g, pos, tag):
    half = slots.shape[1]
    tm = _row_tile(half)
    nb = half // tm

    def kern(pos_ref, own_ref, *refs):
        acc = own_ref[0].astype(F32)
        for s_ref in refs[:7]:
            acc = acc + s_ref[0].astype(F32)
        refs[7][...] = acc

    def slot(k):
        return pl.BlockSpec((1, tm, D), lambda i, pos: (jnp.bitwise_xor(pos[2], k), i, 0))

    gs = pltpu.PrefetchScalarGridSpec(
        num_scalar_prefetch=1, grid=(nb,),
        in_specs=[pl.BlockSpec((1, tm, D), lambda i, pos: (pos[0], pos[1] * nb + i, 0))] + [slot(k) for k in range(1, 8)],
        out_specs=pl.BlockSpec((tm, D), lambda i, pos: (pos[1] * nb + i, 0)))
    return pl.pallas_call(kern, grid_spec=gs, out_shape=jax.ShapeDtypeStruct((2 * half, D), F32),
                          name=f"rs_sum_devices_{tag}", compiler_params=_cp())(pos, g, *([slots] * 7))


def _reduce_scatter_finish(slots, g, pos, tag):
    return _join_core_halves(_sum_devices(slots, g, pos, tag), tag)


def _join_core_halves(r, tag):
    half = r.shape[0] // 2

    def body(r_ref, out_ref, send_sem, recv_sem):
        x, y, c = _mesh_pos()
        mine = out_ref.at[pl.ds(c * half, half), :]
        cp = pltpu.make_async_remote_copy(
            src_ref=mine, dst_ref=mine, send_sem=send_sem, recv_sem=recv_sem,
            device_id=(x, y, 1 - c), device_id_type=MESH)
        cp.start()
        theirs = out_ref.at[pl.ds((1 - c) * half, half), :]
        pltpu.make_async_remote_copy(
            src_ref=theirs, dst_ref=theirs, send_sem=send_sem, recv_sem=recv_sem,
            device_id=(x, y, 1 - c), device_id_type=MESH).wait_recv()
        cp.wait_send()

    return pl.pallas_call(
        body, out_shape=jax.ShapeDtypeStruct(r.shape, r.dtype), in_specs=[ANY], out_specs=ANY,
        input_output_aliases={0: 0},
        scratch_shapes=[pltpu.SemaphoreType.DMA, pltpu.SemaphoreType.DMA],
        name=f"rs_join_cores_{tag}")(r)


def _allreduce_small(v, name):
    def body(v_ref, out_ref, gat, send_sems, recv_sems):
        x, y, c = _mesh_pos()
        me = 4 * x + 2 * y + c
        gat[me] = v_ref[...]
        sends = []
        for k in range(1, 8):
            peer = (x ^ (k >> 2), y ^ ((k >> 1) & 1), c ^ (k & 1))
            cp = pltpu.make_async_remote_copy(
                src_ref=v_ref, dst_ref=gat.at[me], send_sem=send_sems.at[k - 1], recv_sem=recv_sems.at[k - 1],
                device_id=peer, device_id_type=MESH)
            cp.start()
            sends.append(cp)
        for k in range(1, 8):
            px, py, pc = x ^ (k >> 2), y ^ ((k >> 1) & 1), c ^ (k & 1)
            pltpu.make_async_remote_copy(
                src_ref=v_ref, dst_ref=gat.at[4 * px + 2 * py + pc], send_sem=send_sems.at[k - 1],
                recv_sem=recv_sems.at[k - 1], device_id=(px, py, pc), device_id_type=MESH).wait_recv()
        for cp in sends:
            cp.wait_send()
        acc = gat[0]
        for d in range(1, 8):
            acc = acc + gat[d]
        out_ref[...] = acc

    return pl.pallas_call(
        body, out_shape=jax.ShapeDtypeStruct(v.shape, F32),
        in_specs=[pl.BlockSpec(memory_space=pltpu.VMEM)], out_specs=pl.BlockSpec(memory_space=pltpu.VMEM),
        scratch_shapes=[pltpu.VMEM((8,) + v.shape, F32), pltpu.SemaphoreType.DMA((7,)), pltpu.SemaphoreType.DMA((7,))],
        name=name)(v)


def _mlp_fwd(xb, w_up, w_down, tag):
    a = _mm(xb, w_up[0], "nn", f"mlp_up_{tag}", out_dtype=BF16, relu2=True, b_view=("cols", w_up[1]), tm=2048)
    return a, _mm(a, w_down[0], "nn", f"mlp_down_{tag}", b_view=("rows", w_down[1]), tm=2048)


def _mlp_bwd(dz, dzb, xb, a, w_up, w_down, tag):
    du = _mm(dzb, w_down[0], "nt", f"mlp_down_dx_{tag}", out_dtype=BF16, gate_a=a, b_view=("rows", w_down[1]),
             tm=2048)
    gsh = _mm(xb, du, "tn", f"mlp_up_dw_{tag}", out_dtype=BF16, out_view=("cols", 2 * ROWS["mlp_w_up"], 0, None),
              tk=2048)
    gsh = _mm(a, dzb, "tn", f"mlp_down_dw_{tag}", out_dtype=BF16,
              out_view=("rows", 2 * ROWS["mlp_w_up"], ROWS["mlp_w_up"], gsh), tk=2048)
    dx = _mm(du, w_up[0], "nt", f"mlp_up_dx_{tag}", addend=dz, add_scale=ALPHA, b_view=("cols", w_up[1]))
    return dx, gsh


def _fwd_bwd(x, target, w, dist=None, bq=512, qb=8, hb=8):
    t = x.shape[0]
    bq = min(bq, t)
    qb = min(qb, t // WIN)
    cos, sin = _rope_tables(t)
    bkt = jnp.asarray(_bucket_table())
    w_in = jnp.pad(w[("mla_w_in", None)], ((0, 0), (0, HW - (QR + KVR + ROPE))))
    wuq = w[("mla_w_uq", None)]
    wq2 = jnp.concatenate([wuq[:, :, :NOPE].reshape(QR, H * NOPE),
                           jnp.pad(wuq[:, :, NOPE:], ((0, 0), (0, 0), (0, RP - ROPE))).reshape(QR, H * RP)], axis=1)
    wuk_t = w[("mla_w_uk", None)].transpose(1, 2, 0)
    wuk_h = w[("mla_w_uk", None)].transpose(1, 0, 2)
    wuv_h = w[("mla_w_uv", None)].transpose(1, 0, 2)
    w_o = w[("mla_w_o", None)]
    sinks = w["swa_sinks"].reshape(QH)
    lnp = lambda n, l: w[n][l]
    reduced = {}

    hh = _mm(x, w_in, "nn", "mla_in", tm=2048)
    cq, kc = _mla_pre(hh, w["mla_g_q"], w["mla_g_kv"], cos, sin)
    q2 = _mm(cq, wq2, "nn", "mla_uq", tm=2048)
    qcat = _q_prep(q2, wuk_t, cos, sin)
    if dist is None:
        o_lat, lse0_t, o0 = _flash_fwd(qcat, kc, wuv_h, bq, hb)
    else:
        o_lat, lse0_t, o0, wall = _flash_fwd(qcat, kc, wuv_h, bq, hb, gather=dist.late_pack)
        wall = lax.dynamic_update_slice(wall, dist.late_pack[None], (dist.shard, 0, 0))
        w = {**w, **_full_from_gathered(AG_LATE, wall, dist.shard_shapes)}
    wqkv = jnp.concatenate([w[("swa_w_q", None)], w[("kv_w_shared", None)]], axis=1)
    wqkv_t = wqkv.T
    wo_s = w[("swa_w_o", None)]
    y0 = _mm(o0, w_o, "nn", "mla_out", tm=2048)
    x1b, xh1, r1 = _add_ln(x, y0, lnp("ln_mix_g", 0), lnp("ln_mix_b", 0), "ln_mix_0")
    a0, f0 = _mlp_fwd(x1b, w[("mlp_w_up", 0)], w[("mlp_w_down", 0)], 0)
    x2b, xh2, r2 = _add_ln(xh1, f0, lnp("ln_mlp_g", 0), lnp("ln_mlp_b", 0), "ln_mlp_0",
                           res_affine=(lnp("ln_mix_g", 0), lnp("ln_mix_b", 0)))
    bias = _bias_build(w["rel_bias"], bkt)
    qkv_t = _mm(x2b, wqkv, "nn", "swa_qkv", out_dtype=BF16, out_t=True, tm=2048)
    os_t, lse1 = _swa_fwd(qkv_t, bias, sinks, qb)
    y1 = _mm(os_t, wo_s, "tn", "swa_out", tm=2048)
    x3b, xh3, r3 = _add_ln(xh2, y1, lnp("ln_mix_g", 1), lnp("ln_mix_b", 1), "ln_mix_1",
                           res_affine=(lnp("ln_mlp_g", 0), lnp("ln_mlp_b", 0)))
    a1, f1 = _mlp_fwd(x3b, w[("mlp_w_up", 1)], w[("mlp_w_down", 1)], 1)
    _, xh4, r4 = _add_ln(xh3, f1, lnp("ln_mlp_g", 1), lnp("ln_mlp_b", 1), "ln_mlp_1",
                         res_affine=(lnp("ln_mix_g", 1), lnp("ln_mix_b", 1)))

    g = {}
    dz4, dz4b, dg_mlp1, db_mlp1, lpart = _ln_bwd(target, xh4, r4, lnp("ln_mlp_g", 1), "ln_mlp_1_bwd",
                                                 loss_b=lnp("ln_mlp_b", 1))
    dx3, g["mlp1"] = _mlp_bwd(dz4, dz4b, x3b, a1, w[("mlp_w_up", 1)], w[("mlp_w_down", 1)], 1)
    dz3, dz3b, dg_mix1, db_mix1 = _ln_bwd(dx3, xh3, r3, lnp("ln_mix_g", 1), "ln_mix_1_bwd")
    dos_t = _mm(dz3b, wo_s, "nt", "swa_out_dx", out_t=True, tm=2048)
    g[("swa_w_o", None)] = _mm(os_t, dz3b, "nn", "swa_out_dw", tk=2048)
    dqkv_t, dbias, dsk = _swa_bwd(qkv_t, dos_t, os_t, lse1, bias, sinks, qb)
    dwqkv = _mm(dqkv_t, x2b, "nn", "swa_qkv_dw", tk=2048).T
    g[("swa_w_q", None)], g[("kv_w_shared", None)] = dwqkv[:, :QH * HD], dwqkv[:, QH * HD:]
    dx2 = _mm(dqkv_t, wqkv_t, "tn", "swa_qkv_dx", addend=dz3, add_scale=ALPHA)
    g["rel_bias"] = jnp.sum(_bias_bwd(dbias, bkt), axis=-1).reshape(NBKT, QH)
    g["swa_sinks"] = jnp.sum(dsk, axis=-1).reshape(1, QH)
    dz2, dz2b, dg_mlp0, db_mlp0 = _ln_bwd(dx2, xh2, r2, lnp("ln_mlp_g", 0), "ln_mlp_0_bwd")
    dx1, g["mlp0"] = _mlp_bwd(dz2, dz2b, x1b, a0, w[("mlp_w_up", 0)], w[("mlp_w_down", 0)], 0)
    dz1, dz1b, dg_mix0, db_mix0 = _ln_bwd(dx1, xh1, r1, lnp("ln_mix_g", 0), "ln_mix_0_bwd")
    do0 = _mm(dz1b, w_o, "nt", "mla_out_dx", out_dtype=BF16, tm=2048)
    g[("mla_w_o", None)] = _mm(o0, dz1b, "tn", "mla_out_dw", tk=2048)
    do_lat, dwuv, delta_t = _o_up_bwd(do0, o_lat, wuv_h)
    g[("mla_w_uv", None)] = dwuv.transpose(1, 0, 2)
    kc_t = kc.reshape(t // bq, bq, KD).transpose(0, 2, 1)
    if dist is None:
        dk, ds_all = _flash_dkv(qcat, kc, do_lat, lse0_t, delta_t, bq, hb)
        dq2, dwuk = _flash_dq(ds_all, kc_t, q2, wuk_h, cos, sin, bq)
    else:
        g["mid"] = _grad_shards(RS_MID, g).astype(BF16)
        dk, ds_all, slots1, slots_mid = _flash_dkv(qcat, kc, do_lat, lse0_t, delta_t, bq, hb,
                                                  exchange=(g["mlp1"], g["mid"]))
        dq2, dwuk, slots0 = _flash_dq(ds_all, kc_t, q2, wuk_h, cos, sin, bq, exchange=g["mlp0"])
        for key, slots in (("mlp1", slots1), ("mid", slots_mid), ("mlp0", slots0)):
            reduced[key] = _reduce_scatter_finish(slots, g[key], dist.pos, key)
    g[("mla_w_uk", None)] = dwuk.transpose(2, 0, 1)
    dcq = _mm(dq2, wq2, "nt", "mla_uq_dx", tm=2048)
    dwq2 = _mm(cq, dq2, "tn", "mla_uq_dw", tk=2048)
    g[("mla_w_uq", None)] = jnp.concatenate([dwq2[:, :H * NOPE].reshape(QR, H, NOPE),
                                             dwq2[:, H * NOPE:].reshape(QR, H, RP)[:, :, :ROPE]], axis=2)
    dh, dgq, dgkv = _mla_pre_bwd(hh, dcq, dk, w["mla_g_q"], w["mla_g_kv"], cos, sin)
    g[("mla_w_in", None)] = _mm(x, dh, "tn", "mla_in_dw", tk=2048)[:, :QR + KVR + ROPE]
    if dist is None:
        grad_x = _mm(dh, w_in, "nt", "mla_in_dx", addend=dz1, add_scale=ALPHA)
    else:
        g["end"] = _grad_shards(RS_END, g).astype(BF16)
        grad_x, slots_end = _mm(dh, w_in, "nt", "mla_in_dx", addend=dz1, add_scale=ALPHA, exchange=g["end"])
        reduced["end"] = _reduce_scatter_finish(slots_end, g["end"], dist.pos, "end")
    g["mla_g_q"], g["mla_g_kv"] = dgq, dgkv
    g["ln_mix_g"] = jnp.concatenate([dg_mix0, dg_mix1], axis=0)
    g["ln_mix_b"] = jnp.concatenate([db_mix0, db_mix1], axis=0)
    g["ln_mlp_g"] = jnp.concatenate([dg_mlp0, dg_mlp1], axis=0)
    g["ln_mlp_b"] = jnp.concatenate([db_mlp0, db_mlp1], axis=0)
    return lpart, grad_x, g, reduced


def _rows(a):
    return a.reshape(-1, D)


def _piece(a, layer):
    return _rows(a if layer is None else a[layer])


def _pack_group(group, parts):
    return jnp.concatenate([_piece(parts[n], l) for n, l in group], axis=0)


def _unpack_group(group, buf, like):
    out, off = {}, 0
    for n, l in group:
        shp = like[n].shape if l is None else like[n].shape[1:]
        out[(n, l)] = buf[off:off + ROWS[n]].reshape(shp)
        off += ROWS[n]
    return out


def _by_name(pieces):
    out = {n: a for (n, l), a in pieces.items() if l is None}
    for n in {n for (n, l) in pieces if l is not None}:
        out[n] = jnp.stack([pieces[(n, 0)], pieces[(n, 1)]])
    return out


def _full_from_gathered(group, wall, shard_shapes):
    out, off = {}, 0
    for n, l in group:
        shp = tuple(shard_shapes[n])
        if n in ("mlp_w_up", "mlp_w_down"):
            out[(n, l)] = (wall, off)
        elif n == "kv_w_shared":
            out[(n, l)] = wall[:, off:off + ROWS[n]].reshape((4 * shp[0],) + shp[1:])
        else:
            out[(n, l)] = wall[:, off:off + ROWS[n]].reshape((4 * shp[1],) + shp[2:])
        off += ROWS[n]
    return out


def _grad_shards(group, g):
    return jnp.concatenate([g[(n, l)].reshape(4, ROWS[n], D) for n, l in group], axis=1)


SMALL = (("ln_mix_g", 0, 2), ("ln_mix_b", 2, 2), ("ln_mlp_g", 4, 2), ("ln_mlp_b", 6, 2),
         ("swa_sinks", 8, 1), ("mla_g_q", 9, 1), ("mla_g_kv", 10, 1), ("rel_bias", 11, 1))
LOSS_ROW = 12


def _pack_small(parts, extra_row=None):
    rows = []
    for n, _, nr in SMALL:
        a = parts[n].reshape(nr, -1).astype(F32)
        rows.append(jnp.pad(a, ((0, 0), (0, D - a.shape[1]))))
    if extra_row is not None:
        rows.append(extra_row)
    rows.append(jnp.zeros((SMALL_ROWS - sum(r.shape[0] for r in rows), D), F32))
    return jnp.concatenate(rows, axis=0)


def _unpack_small(buf, like):
    out = {}
    for n, r0, nr in SMALL:
        size = like[n].size // nr
        out[n] = buf[r0:r0 + nr, :size].reshape(like[n].shape)
    return out


def kernel(x, mla_w_in, mla_g_q, mla_g_kv, mla_w_uq, mla_w_uk, mla_w_uv, mla_w_o, kv_w_shared, swa_w_q, swa_sinks, swa_w_o, rel_bias, mlp_w_up, mlp_w_down, ln_mix_g, ln_mix_b, ln_mlp_g, ln_mlp_b, loss_target, m_mla_w_in, m_mla_g_q, m_mla_g_kv, m_mla_w_uq, m_mla_w_uk, m_mla_w_uv, m_mla_w_o, m_kv_w_shared, m_swa_w_q, m_swa_sinks, m_swa_w_o, m_rel_bias, m_mlp_w_up, m_mlp_w_down, m_ln_mix_g, m_ln_mix_b, m_ln_mlp_g, m_ln_mlp_b, v_mla_w_in, v_mla_g_q, v_mla_g_kv, v_mla_w_uq, v_mla_w_uk, v_mla_w_uv, v_mla_w_o, v_kv_w_shared, v_swa_w_q, v_swa_sinks, v_swa_w_o, v_rel_bias, v_mlp_w_up, v_mlp_w_down, v_ln_mix_g, v_ln_mix_b, v_ln_mlp_g, v_ln_mlp_b):
    names = ["mla_w_in", "mla_g_q", "mla_g_kv", "mla_w_uq", "mla_w_uk", "mla_w_uv", "mla_w_o", "kv_w_shared",
             "swa_w_q", "swa_sinks", "swa_w_o", "rel_bias", "mlp_w_up", "mlp_w_down",
             "ln_mix_g", "ln_mix_b", "ln_mlp_g", "ln_mlp_b"]
    ws = dict(zip(names, [mla_w_in, mla_g_q, mla_g_kv, mla_w_uq, mla_w_uk, mla_w_uv, mla_w_o, kv_w_shared,
                          swa_w_q, swa_sinks, swa_w_o, rel_bias, mlp_w_up, mlp_w_down,
                          ln_mix_g, ln_mix_b, ln_mlp_g, ln_mlp_b]))
    ms = dict(zip(names, [m_mla_w_in, m_mla_g_q, m_mla_g_kv, m_mla_w_uq, m_mla_w_uk, m_mla_w_uv, m_mla_w_o,
                          m_kv_w_shared, m_swa_w_q, m_swa_sinks, m_swa_w_o, m_rel_bias, m_mlp_w_up, m_mlp_w_down,
                          m_ln_mix_g, m_ln_mix_b, m_ln_mlp_g, m_ln_mlp_b]))
    vs = dict(zip(names, [v_mla_w_in, v_mla_g_q, v_mla_g_kv, v_mla_w_uq, v_mla_w_uk, v_mla_w_uv, v_mla_w_o,
                          v_kv_w_shared, v_swa_w_q, v_swa_sinks, v_swa_w_o, v_rel_bias, v_mlp_w_up, v_mlp_w_down,
                          v_ln_mix_g, v_ln_mix_b, v_ln_mlp_g, v_ln_mlp_b]))
    xi, yi, ci = _mesh_pos()
    shard = 2 * xi + yi
    shard_shapes = {n: ws[n].shape for n in ROWS}
    wbf = {n: ws[n].astype(BF16) for n in ROWS}

    gains = jnp.concatenate([mla_g_q.reshape(-1), mla_g_kv.reshape(-1)])
    pieces = []
    for _ in range(3):
        head = lax.reduce_precision(gains, exponent_bits=8, mantissa_bits=7)
        pieces.append(head.astype(BF16))
        gains = gains - head
    ng = (QR + KVR) // 4
    gain_rows = jnp.pad(jnp.concatenate(pieces).reshape(1, 3 * ng), ((0, GAIN_ROWS - 1), (0, D - 3 * ng)))
    early = jnp.concatenate([_pack_group(AG_EARLY, wbf), gain_rows], axis=0)
    wall = lax.dynamic_update_slice(_allgather_weights(early), early[None], (shard, 0, 0))
    w = _full_from_gathered(AG_EARLY, wall, shard_shapes)
    gp = wall[:, early.shape[0] - GAIN_ROWS, :3 * ng].astype(F32).reshape(4, 3, ng)
    gains = (gp[:, 0] + gp[:, 1]) + gp[:, 2]
    w["mla_g_q"], w["mla_g_kv"] = gains[:, :QR // 4].reshape(QR), gains[:, QR // 4:].reshape(KVR)
    dist = _Dist(shard=shard, pos=jnp.stack([shard, ci, 2 * shard + ci]).astype(jnp.int32),
                 late_pack=_pack_group(AG_LATE, wbf), shard_shapes=shard_shapes)
    for n in ("swa_sinks", "rel_bias", "ln_mix_g", "ln_mix_b", "ln_mlp_g", "ln_mlp_b"):
        w[n] = ws[n]

    lpart, grad_x, g, reduced = _fwd_bwd(x[0], loss_target[0], w, dist)
    reduced["rest"] = jnp.concatenate([reduced["mid"], reduced["end"]], axis=0)

    small_like = {n: g[n] for n, _, _ in SMALL}
    small_sum = _allreduce_small(_pack_small(g, extra_row=lpart), "allreduce_small_grads")
    loss = 0.5 * jnp.sum(small_sum[LOSS_ROW]) / D
    gsm = _unpack_small(small_sum, small_like)
    gsm["mla_g_q"] = lax.dynamic_slice(gsm["mla_g_q"], (0, shard * (QR // 4)), (1, QR // 4))
    gsm["mla_g_kv"] = lax.dynamic_slice(gsm["mla_g_kv"], (0, shard * (KVR // 4)), (1, KVR // 4))

    gbig, dbig, mbig, vbig = {}, {}, {}, {}
    for n in ("mlp_w_up", "mlp_w_down"):
        off = 0 if n == "mlp_w_up" else ROWS["mlp_w_up"]
        gbig[n], dbig[n], mbig[n], vbig[n] = _adamw_layers(
            ws[n], ms[n], vs[n], reduced["mlp0"], reduced["mlp1"], off, f"adamw_{n}")
    rest = RS_MID + RS_END
    outs = _adamw(_pack_group(rest, ws), reduced["rest"], _pack_group(rest, ms), _pack_group(rest, vs),
                  "adamw_rest", tm=_row_tile(reduced["rest"].shape[0]))
    for dst, buf in zip((gbig, dbig, mbig, vbig), (reduced["rest"], *outs)):
        dst.update(_by_name(_unpack_group(rest, buf, ws)))
    dsm, msm, vsm = _adamw(_pack_small(ws), _pack_small(gsm), _pack_small(ms), _pack_small(vs), "adamw_small", tm=16)
    grads = {**gbig, **gsm}
    delta = {**dbig, **_unpack_small(dsm, ws)}
    new_m = {**mbig, **_unpack_small(msm, ws)}
    new_v = {**vbig, **_unpack_small(vsm, ws)}
    grads = {n: grads[n].reshape(ws[n].shape) for n in names}
    return (loss, grad_x[None], *[grads[n] for n in names], *[delta[n] for n in names],
            *[new_m[n] for n in names], *[new_v[n] for n in names])
```

```python
import collections
import math

import numpy as np
import jax
import jax.numpy as jnp
from jax import lax
from jax.experimental import pallas as pl
from jax.experimental.pallas import tpu as pltpu

F32 = jnp.float32
BF16 = jnp.bfloat16
MESH = pl.DeviceIdType.MESH

D = 1024
H = 8
NOPE = 128
ROPE = 64
QR = 384
KVR = 256
RP = 128
KD = KVR + RP
HW = 768
QH = 16
KVH = 4
HD = 64
G = QH // KVH
WIN = 128
NBKT = 32
ALPHA = 4.0 ** 0.25
LN_EPS = 1e-5
RMS_EPS = 1e-6
MLA_SCALE = (NOPE + ROPE) ** -0.5
LOG2E = 1.4426950408889634
LN2 = 0.6931471805599453
QSCALE = MLA_SCALE * LOG2E
AHEAD = 1
SWA_SCALE = HD ** -0.5
NEG = -1e30
LR, B1, B2, ADAM_EPS, WD, STEP = 0.001, 0.9, 0.999, 1e-8, 0.01, 10

VMEM_LIMIT = 48 * 1024 * 1024

NN = (((1,), (0,)), ((), ()))
NT = (((1,), (1,)), ((), ()))
TN = (((0,), (0,)), ((), ()))

ROWS = {"mlp_w_up": 1024, "mlp_w_down": 1024, "mla_w_o": 256, "swa_w_q": 256, "swa_w_o": 256,
        "kv_w_shared": 128, "mla_w_in": 176, "mla_w_uq": 144, "mla_w_uk": 64, "mla_w_uv": 64}
AG_EARLY = (("mla_w_in", None), ("mla_w_uq", None), ("mla_w_uk", None), ("mla_w_uv", None), ("mla_w_o", None))
AG_LATE = (("mlp_w_up", 0), ("mlp_w_up", 1), ("mlp_w_down", 0), ("mlp_w_down", 1),
           ("swa_w_q", None), ("swa_w_o", None), ("kv_w_shared", None))
RS_MID = (("mla_w_o", None), ("swa_w_q", None), ("swa_w_o", None), ("kv_w_shared", None), ("mla_w_uv", None))
RS_END = (("mla_w_in", None), ("mla_w_uq", None), ("mla_w_uk", None))
SMALL_ROWS = 16
GAIN_ROWS = 32
_Dist =collections.namedtuple("_Dist", "shard pos late_pack shard_shapes")


def _cp(**kw):
    return pltpu.CompilerParams(vmem_limit_bytes=VMEM_LIMIT, **kw)


def _tile(n, pref):
    t = min(n, pref)
    while n % t:
        t -= 128
    return t


def _dot(a, b, dims):
    return lax.dot_general(a, b, dims, preferred_element_type=F32)


def _mm(a, b, mode, name, out_dtype=F32, out_t=False, addend=None, add_scale=1.0, relu2=False, gate_a=None,
        b_view=None, out_view=None, exchange=None, ln=None, tm=1024, tn=1024, tk=1024):
    blk = 1024
    if b_view is not None:
        kind, b_off = b_view
        assert b.shape[0] == 4 and b.shape[2] == blk and b_off % blk == 0
        bshape = {("cols", "nn"): (blk, 4 * blk), ("cols", "nt"): (blk, 4 * blk),
                  ("rows", "nn"): (4 * blk, blk), ("rows", "nt"): (4 * blk, blk)}[(kind, mode)]
    else:
        bshape = b.shape
    if mode == "nn":
        (m, k), (k2, n) = a.shape, bshape
    elif mode == "nt":
        (m, k), (n, k2) = a.shape, bshape
    else:
        (k, m), (k2, n) = a.shape, bshape
    assert k == k2, (name, a.shape, b.shape)
    tm, tn, tk = _tile(m, tm), _tile(n, tn), _tile(k, tk)
    nk = k // tk
    dims = {"nn": NN, "nt": NT, "tn": TN}[mode]
    if mode == "tn":
        a_spec = pl.BlockSpec((tk, tm), lambda i, j, kk: (kk, i))
    else:
        a_spec = pl.BlockSpec((tm, tk), lambda i, j, kk: (i, kk))
    if b_view is not None:
        assert tn == blk and tk == blk
        ob = b_off // blk
        b_spec = {("cols", "nn"): pl.BlockSpec((None, tk, tn), lambda i, j, kk: (j, ob, 0)),
                  ("cols", "nt"): pl.BlockSpec((None, tn, tk), lambda i, j, kk: (kk, ob, 0)),
                  ("rows", "nn"): pl.BlockSpec((None, tk, tn), lambda i, j, kk: (kk, ob, 0)),
                  ("rows", "nt"): pl.BlockSpec((None, tn, tk), lambda i, j, kk: (j, ob, 0))}[(kind, mode)]
    elif mode == "nt":
        b_spec = pl.BlockSpec((tn, tk), lambda i, j, kk: (j, kk))
    else:
        b_spec = pl.BlockSpec((tk, tn), lambda i, j, kk: (kk, j))
    mn_spec = pl.BlockSpec((tm, tn), lambda i, j, kk: (i, j))
    ins, in_specs = [a, b], [a_spec, b_spec]
    if addend is not None:
        ins.append(addend)
        in_specs.append(mn_spec)
    if gate_a is not None:
        ins.append(gate_a)
        in_specs.append(mn_spec)
    aliases = {}
    if out_view is not None:
        okind, total_rows, o_off, buf = out_view
        assert not out_t and tm == blk and tn == blk and o_off % blk == 0
        oo = o_off // blk
        out_shape = [jax.ShapeDtypeStruct((4, total_rows, blk), out_dtype)]
        if okind == "cols":
            out_specs = [pl.BlockSpec((None, tm, tn), lambda i, j, kk: (j, oo, 0))]
        else:
            out_specs = [pl.BlockSpec((None, tm, tn), lambda i, j, kk: (i, oo, 0))]
        if buf is not None:
            aliases = {len(ins): 0}
            ins.append(buf)
            in_specs.append(pl.BlockSpec(memory_space=pl.ANY))
    elif out_t:
        out_shape = [jax.ShapeDtypeStruct((n, m), out_dtype)]
        out_specs = [pl.BlockSpec((tn, tm), lambda i, j, kk: (j, i))]
    else:
        out_shape = [jax.ShapeDtypeStruct((m, n), out_dtype)]
        out_specs = [mn_spec]
    has_add, has_gate = addend is not None, gate_a is not None
    hosting = exchange is not None
    has_ln = ln is not None
    ln_affine = has_ln and ln[3] is not None
    if has_ln:
        assert tn == n == D and not out_t and out_view is None and not hosting and not has_add and not has_gate
        vec = pl.BlockSpec((1, D), lambda i, j, kk: (0, 0))
        ln_vecs = [ln[1], ln[2]] + (list(ln[3]) if ln_affine else [])
        ins += [ln[0]] + [v.reshape(1, D) for v in ln_vecs]
        in_specs += [mn_spec] + [vec] * len(ln_vecs)
        out_shape = [jax.ShapeDtypeStruct((m, n), BF16), jax.ShapeDtypeStruct((m, n), F32),
                     jax.ShapeDtypeStruct((m, 1), F32)]
        out_specs = [mn_spec, mn_spec, pl.BlockSpec((tm, 1), lambda i, j, kk: (i, 0))]
    scratch = [pltpu.VMEM((tm, tn), F32)] if nk > 1 else []
    if hosting:
        assert not aliases
        ins.append(exchange)
        in_specs.append(pl.BlockSpec(memory_space=pl.ANY))
        out_shape.append(jax.ShapeDtypeStruct((8, exchange.shape[1] // 2, D), exchange.dtype))
        out_specs.append(pl.BlockSpec(memory_space=pl.ANY))
        scratch = scratch + ALL_SEMS
    steps = (m // tm, n // tn, nk)

    def kern(*refs):
        a_ref, b_ref = refs[0], refs[1]
        pos = 2
        add_ref = gate_ref = None
        if has_add:
            add_ref = refs[pos]
            pos += 1
        if has_gate:
            gate_ref = refs[pos]
            pos += 1
        if has_ln:
            ln_refs = refs[pos:pos + 3 + 2 * ln_affine]
            pos += len(ln_refs)
        pos += len(aliases)
        if hosting:
            xc_start, xc_finish = _device_exchange_schedule(refs[pos], refs[pos + 2], refs[-2], refs[-1])
            pos += 1
        o_ref = refs[pos]
        acc = refs[pos + len(out_shape)] if nk > 1 else None
        kk = pl.program_id(2)
        if hosting:
            lin = (pl.program_id(0) * steps[1] + pl.program_id(1)) * steps[2] + kk

            @pl.when(lin == 0)
            def _():
                xc_start()

        def partial():
            return _dot(a_ref[...].astype(BF16), b_ref[...].astype(BF16), dims)

        if nk > 1:
            @pl.when(kk == 0)
            def _():
                acc[...] = partial()

            @pl.when((kk > 0) & (kk < nk - 1))
            def _():
                acc[...] += partial()

        @pl.when(kk == nk - 1)
        def _():
            r = partial() + acc[...] if nk > 1 else partial()
            if has_add:
                r = r + add_scale * add_ref[...].astype(F32)
            if has_gate:
                ga = gate_ref[...].astype(F32)
                r = r * jnp.where(ga > 0.0, (2.0 * ga) * lax.rsqrt(ga), 0.0)
            if relu2:
                hh = jnp.maximum(r, 0.0)
                r = hh * hh
            if out_t:
                r = r.T
            if has_ln:
                x = ln_refs[0][...]
                if ln_affine:
                    x = x * ln_refs[3][...] + ln_refs[4][...]
                z = ALPHA * x + r
                zc = z - jnp.mean(z, axis=-1, keepdims=True)
                rs = lax.rsqrt(jnp.mean(zc * zc, axis=-1, keepdims=True) + LN_EPS)
                xh = zc * rs
                o_ref[...] = (xh * ln_refs[1][...] + ln_refs[2][...]).astype(BF16)
                refs[pos + 1][...] = xh
                refs[pos + 2][...] = rs
                return
            o_ref[...] = r.astype(out_dtype)

        if hosting:
            @pl.when(lin == steps[0] * steps[1] * steps[2] - 1)
            def _():
                xc_finish()

    outs = pl.pallas_call(
        kern, out_shape=out_shape, grid=steps, in_specs=in_specs, out_specs=out_specs,
        scratch_shapes=scratch, input_output_aliases=aliases, name=name, compiler_params=_cp())(*ins)
    return outs if hosting or has_ln else outs[0]


def _add_ln(res, y, g, b, name, res_affine=None, tm=1024):
    t = res.shape[0]
    tm = min(tm, t)
    affine = res_affine is not None

    def kern(*refs):
        if affine:
            x_ref, y_ref, g_ref, b_ref, g0_ref, b0_ref, ob_ref, xh_ref, r_ref = refs
            x = x_ref[...] * g0_ref[...] + b0_ref[...]
        else:
            x_ref, y_ref, g_ref, b_ref, ob_ref, xh_ref, r_ref = refs
            x = x_ref[...]
        z = ALPHA * x + y_ref[...]
        mu = jnp.mean(z, axis=-1, keepdims=True)
        zc = z - mu
        var = jnp.mean(zc * zc, axis=-1, keepdims=True)
        r = lax.rsqrt(var + LN_EPS)
        xh = zc * r
        ob_ref[...] = (xh * g_ref[...] + b_ref[...]).astype(BF16)
        xh_ref[...] = xh
        r_ref[...] = r

    row = pl.BlockSpec((tm, D), lambda i: (i, 0))
    vec = pl.BlockSpec((1, D), lambda i: (0, 0))
    st = pl.BlockSpec((tm, 1), lambda i: (i, 0))
    ins = [res, y, g.reshape(1, D), b.reshape(1, D)]
    if affine:
        ins += [res_affine[0].reshape(1, D), res_affine[1].reshape(1, D)]
    return pl.pallas_call(
        kern, grid=(t // tm,), in_specs=[row, row] + [vec] * (len(ins) - 2), out_specs=[row, row, st],
        out_shape=[jax.ShapeDtypeStruct((t, D), BF16), jax.ShapeDtypeStruct((t, D), F32),
                   jax.ShapeDtypeStruct((t, 1), F32)],
        name=name, compiler_params=_cp())(*ins)


def _ln_bwd(dout, xhat, rstd, g, name, loss_b=None, tm=1024):
    t = dout.shape[0]
    tm = min(tm, t)
    head = loss_b is not None

    def kern(*refs):
        if head:
            do_ref, xh_ref, r_ref, g_ref, b_ref, dz_ref, dzb_ref, dg_ref, db_ref, l_ref = refs
        else:
            do_ref, xh_ref, r_ref, g_ref, dz_ref, dzb_ref, dg_ref, db_ref = refs

        @pl.when(pl.program_id(0) == 0)
        def _():
            dg_ref[...] = jnp.zeros_like(dg_ref)
            db_ref[...] = jnp.zeros_like(db_ref)
            if head:
                l_ref[...] = jnp.zeros_like(l_ref)

        xh = xh_ref[...]
        if head:
            e = xh * g_ref[...] + b_ref[...] - do_ref[...]
            l_ref[...] += jnp.sum(e * e, axis=0, keepdims=True)
            do = e * (1.0 / D)
        else:
            do = do_ref[...]
        dxh = do * g_ref[...]
        m1 = jnp.mean(dxh, axis=-1, keepdims=True)
        m2 = jnp.mean(dxh * xh, axis=-1, keepdims=True)
        dz = r_ref[...] * (dxh - m1 - xh * m2)
        dz_ref[...] = dz
        dzb_ref[...] = dz.astype(BF16)
        dg_ref[...] += jnp.sum(do * xh, axis=0, keepdims=True)
        db_ref[...] += jnp.sum(do, axis=0, keepdims=True)

    row = pl.BlockSpec((tm, D), lambda i: (i, 0))
    vec = pl.BlockSpec((1, D), lambda i: (0, 0))
    st = pl.BlockSpec((tm, 1), lambda i: (i, 0))
    ins = [dout, xhat, rstd, g.reshape(1, D)] + ([loss_b.reshape(1, D)] if head else [])
    return pl.pallas_call(
        kern, grid=(t // tm,), in_specs=[row, row, st] + [vec] * (len(ins) - 3),
        out_specs=[row, row, vec, vec] + ([vec] if head else []),
        out_shape=[jax.ShapeDtypeStruct((t, D), F32), jax.ShapeDtypeStruct((t, D), BF16)]
        + [jax.ShapeDtypeStruct((1, D), F32)] * (3 if head else 2),
        name=name, compiler_params=_cp())(*ins)


def _rope_tables(t):
    half = ROPE // 2
    inv = 10000.0 ** (-jnp.arange(half, dtype=F32) / half)
    ang = jnp.arange(t).astype(F32)[:, None] * inv[None, :]
    cos, sin = jnp.cos(ang), jnp.sin(ang)
    z = jnp.zeros((t, RP - ROPE), F32)
    return jnp.concatenate([cos, cos, z], axis=1), jnp.concatenate([-sin, sin, z], axis=1)


def _swap_halves(x):
    lane = lax.broadcasted_iota(jnp.int32, x.shape, 1)
    return jnp.where(lane < ROPE // 2, pltpu.roll(x, RP - ROPE // 2, 1), pltpu.roll(x, ROPE // 2, 1))


def _rope(x, cos, sin):
    return x * cos + _swap_halves(x) * sin


def _rope_t(gy, cos, sin):
    return gy * cos + _swap_halves(gy * sin)


def _mla_pre(hh, g_q, g_kv, cos, sin, tm=1024):
    t = hh.shape[0]
    tm = min(tm, t)

    def kern(h_ref, gq_ref, gkv_ref, c_ref, s_ref, cq_ref, k_ref):
        xq = h_ref[:, 0:QR]
        rq = lax.rsqrt(jnp.mean(xq * xq, axis=-1, keepdims=True) + RMS_EPS)
        cq_ref[...] = (xq * rq * gq_ref[...]).astype(BF16)
        xk = h_ref[:, QR:QR + KVR]
        rk = lax.rsqrt(jnp.mean(xk * xk, axis=-1, keepdims=True) + RMS_EPS)
        k_ref[:, 0:KVR] = (xk * rk * gkv_ref[...]).astype(BF16)
        k_ref[:, KVR:KD] = _rope(h_ref[:, QR + KVR:HW], c_ref[...], s_ref[...]).astype(BF16)

    return pl.pallas_call(
        kern, grid=(t // tm,),
        in_specs=[pl.BlockSpec((tm, HW), lambda i: (i, 0)), pl.BlockSpec((1, QR), lambda i: (0, 0)),
                  pl.BlockSpec((1, KVR), lambda i: (0, 0)), pl.BlockSpec((tm, RP), lambda i: (i, 0)),
                  pl.BlockSpec((tm, RP), lambda i: (i, 0))],
        out_specs=[pl.BlockSpec((tm, QR), lambda i: (i, 0)), pl.BlockSpec((tm, KD), lambda i: (i, 0))],
        out_shape=[jax.ShapeDtypeStruct((t, QR), BF16), jax.ShapeDtypeStruct((t, KD), BF16)],
        name="mla_pre", compiler_params=_cp())(hh, g_q.reshape(1, QR), g_kv.reshape(1, KVR), cos, sin)


def _mla_pre_bwd(hh, dcq, dk, g_q, g_kv, cos, sin, tm=1024):
    t = hh.shape[0]
    tm = min(tm, t)

    def rms_bwd(x, dy, g):
        r = lax.rsqrt(jnp.mean(x * x, axis=-1, keepdims=True) + RMS_EPS)
        gdy = dy * g
        dx = r * gdy - x * (r * r * r) * jnp.mean(gdy * x, axis=-1, keepdims=True)
        return dx, jnp.sum(dy * x * r, axis=0, keepdims=True)

    def kern(h_ref, dcq_ref, dk_ref, gq_ref, gkv_ref, c_ref, s_ref, dh_ref, dgq_ref, dgkv_ref):
        @pl.when(pl.program_id(0) == 0)
        def _():
            dgq_ref[...] = jnp.zeros_like(dgq_ref)
            dgkv_ref[...] = jnp.zeros_like(dgkv_ref)

        dxq, dgq = rms_bwd(h_ref[:, 0:QR], dcq_ref[...], gq_ref[...])
        dxk, dgk = rms_bwd(h_ref[:, QR:QR + KVR], dk_ref[:, 0:KVR], gkv_ref[...])
        dh_ref[:, 0:QR] = dxq.astype(BF16)
        dh_ref[:, QR:QR + KVR] = dxk.astype(BF16)
        dh_ref[:, QR + KVR:HW] = _rope_t(dk_ref[:, KVR:KD], c_ref[...], s_ref[...]).astype(BF16)
        dgq_ref[...] += dgq
        dgkv_ref[...] += dgk

    return pl.pallas_call(
        kern, grid=(t // tm,),
        in_specs=[pl.BlockSpec((tm, HW), lambda i: (i, 0)), pl.BlockSpec((tm, QR), lambda i: (i, 0)),
                  pl.BlockSpec((tm, KD), lambda i: (i, 0)), pl.BlockSpec((1, QR), lambda i: (0, 0)),
                  pl.BlockSpec((1, KVR), lambda i: (0, 0)), pl.BlockSpec((tm, RP), lambda i: (i, 0)),
                  pl.BlockSpec((tm, RP), lambda i: (i, 0))],
        out_specs=[pl.BlockSpec((tm, HW), lambda i: (i, 0)), pl.BlockSpec((1, QR), lambda i: (0, 0)),
                   pl.BlockSpec((1, KVR), lambda i: (0, 0))],
        out_shape=[jax.ShapeDtypeStruct((t, HW), BF16), jax.ShapeDtypeStruct((1, QR), F32),
                   jax.ShapeDtypeStruct((1, KVR), F32)],
        name="mla_pre_bwd", compiler_params=_cp())(hh, dcq, dk, g_q.reshape(1, QR), g_kv.reshape(1, KVR), cos, sin)


def _q_prep(q2, wuk_t, cos, sin, tm=1024):
    t = q2.shape[0]
    tm = min(tm, t)

    def kern(q_ref, w_ref, c_ref, s_ref, o_ref):
        cos_, sin_ = c_ref[...], s_ref[...]
        for h in range(H):
            qn = q_ref[:, h * NOPE:(h + 1) * NOPE].astype(BF16)
            o_ref[:, h * KD:h * KD + KVR] = (_dot(qn, w_ref[h], NN) * QSCALE).astype(BF16)
            qr = q_ref[:, H * NOPE + h * RP:H * NOPE + (h + 1) * RP]
            o_ref[:, h * KD + KVR:(h + 1) * KD] = (_rope(qr, cos_, sin_) * QSCALE).astype(BF16)

    return pl.pallas_call(
        kern, grid=(t // tm,),
        in_specs=[pl.BlockSpec((tm, 2 * H * NOPE), lambda i: (i, 0)), pl.BlockSpec((H, NOPE, KVR), lambda i: (0, 0, 0)),
                  pl.BlockSpec((tm, RP), lambda i: (i, 0)), pl.BlockSpec((tm, RP), lambda i: (i, 0))],
        out_specs=pl.BlockSpec((tm, H * KD), lambda i: (i, 0)),
        out_shape=jax.ShapeDtypeStruct((t, H * KD), BF16),
        name="q_prep", compiler_params=_cp())(q2, wuk_t, cos, sin)


def _o_up_bwd(do, o_lat, wuv_h, tm=1024):
    t = do.shape[0]
    tm = min(tm, t)

    def kern(do_ref, x_ref, w_ref, dx_ref, dw_ref, dlt_ref):
        @pl.when(pl.program_id(0) == 0)
        def _():
            dw_ref[...] = jnp.zeros_like(dw_ref)

        for h in range(H):
            dh_ = do_ref[:, h * NOPE:(h + 1) * NOPE]
            x = x_ref[:, h * KVR:(h + 1) * KVR]
            dx = _dot(dh_, w_ref[h], NT)
            dx_ref[:, h * KVR:(h + 1) * KVR] = dx.astype(BF16)
            dw_ref[h] += _dot(x.astype(BF16), dh_, TN)
            dl = jnp.broadcast_to(jnp.sum(dx * x, axis=1)[:, None], (tm, 128))
            dlt_ref[h] = dl.T[0:1, :]

    return pl.pallas_call(
        kern, grid=(t // tm,),
        in_specs=[pl.BlockSpec((tm, H * NOPE), lambda i: (i, 0)), pl.BlockSpec((tm, H * KVR), lambda i: (i, 0)),
                  pl.BlockSpec((H, KVR, NOPE), lambda i: (0, 0, 0))],
        out_specs=[pl.BlockSpec((tm, H * KVR), lambda i: (i, 0)), pl.BlockSpec((H, KVR, NOPE), lambda i: (0, 0, 0)),
                   pl.BlockSpec((H, 1, tm), lambda i: (0, 0, i))],
        out_shape=[jax.ShapeDtypeStruct((t, H * KVR), BF16), jax.ShapeDtypeStruct((H, KVR, NOPE), F32),
                   jax.ShapeDtypeStruct((H, 1, t), F32)],
        name="o_up_bwd", compiler_params=_cp())(do, o_lat, wuv_h)


def _causal_pairs(nq):
    return [(i, j) for i in range(nq) for j in range(i + 1)]


def _lane_tile(stat, width):
    return jnp.tile(stat, (1, width // 128))


def _flash_fwd(qcat, kc, wuv_h, bq, hb, gather=None):
    t = kc.shape[0]
    nq = t // bq
    pairs = _causal_pairs(nq)
    itab = jnp.asarray(np.array([p[0] for p in pairs], np.int32))
    jtab = jnp.asarray(np.array([p[1] for p in pairs], np.int32))

    ng = H // hb
    hosting = gather is not None

    def kern(it, jt, q_ref, k_ref, wuv_ref, *rest):
        if hosting:
            w_ref, o_ref, lset_ref, oup_ref, wall_ref, m_sc, l_sc, acc_sc, send_sems, recv_sems = rest
            ag_start, ag_forward, ag_finish = _allgather_schedule(w_ref, wall_ref, send_sems, recv_sems)
        else:
            o_ref, lset_ref, oup_ref, m_sc, l_sc, acc_sc = rest
        grp = pl.program_id(0)
        st = pl.program_id(1)
        i, j = it[st], jt[st]

        if hosting:
            @pl.when((grp == 0) & (st == 0))
            def _():
                ag_start()

        @pl.when(j == 0)
        def _():
            m_sc[...] = jnp.full_like(m_sc, NEG)
            l_sc[...] = jnp.zeros_like(l_sc)
            acc_sc[...] = jnp.zeros_like(acc_sc)

        def update(masked):
            k = k_ref[...]
            v = k[:, 0:KVR]
            if masked:
                row = lax.broadcasted_iota(jnp.int32, (bq, bq), 0)
                col = lax.broadcasted_iota(jnp.int32, (bq, bq), 1)
                keep = col <= row
            pending = [_dot(q_ref[:, hh * KD:(hh + 1) * KD], k, NT) for hh in range(min(AHEAD, hb))]
            for hh in range(hb):
                s = pending.pop(0)
                if hh + AHEAD < hb:
                    pending.append(_dot(q_ref[:, (hh + AHEAD) * KD:(hh + AHEAD + 1) * KD], k, NT))
                if masked:
                    s = jnp.where(keep, s, NEG)
                m_prev = m_sc[hh]
                m_next = jnp.maximum(m_prev, jnp.max(s, axis=1)[:, None])
                p = jnp.exp2(s - _lane_tile(m_next, bq))
                a = jnp.exp2(m_prev - m_next)
                l_sc[hh] = a * l_sc[hh] + jnp.sum(p, axis=1)[:, None]
                acc_sc[hh] = _lane_tile(a, KVR) * acc_sc[hh] + _dot(p.astype(BF16), v, NN)
                m_sc[hh] = m_next

        @pl.when(j < i)
        def _():
            update(False)

        @pl.when(j == i)
        def _():
            update(True)
            for hh in range(hb):
                l = l_sc[hh]
                o_h = acc_sc[hh] / _lane_tile(l, KVR)
                o_ref[:, hh * KVR:(hh + 1) * KVR] = o_h
                oup_ref[:, hh * NOPE:(hh + 1) * NOPE] = _dot(o_h.astype(BF16), wuv_ref[hh], NN).astype(BF16)
                lset_ref[hh] = (m_sc[hh] + jnp.log2(l)).T[0:1, :]

        if hosting:
            half_way = (ng * len(pairs)) // 2

            @pl.when(grp * len(pairs) + st == half_way)
            def _():
                ag_forward()

            @pl.when((grp == ng - 1) & (st == len(pairs) - 1))
            def _():
                ag_finish()

    in_specs = [pl.BlockSpec((bq, hb * KD), lambda g, s, it, jt: (it[s], g)),
                pl.BlockSpec((bq, KD), lambda g, s, it, jt: (jt[s], 0)),
                pl.BlockSpec((hb, KVR, NOPE), lambda g, s, it, jt: (g, 0, 0))]
    out_specs = [pl.BlockSpec((bq, hb * KVR), lambda g, s, it, jt: (it[s], g)),
                 pl.BlockSpec((hb, 1, bq), lambda g, s, it, jt: (g, 0, it[s])),
                 pl.BlockSpec((bq, hb * NOPE), lambda g, s, it, jt: (it[s], g))]
    out_shape = [jax.ShapeDtypeStruct((t, H * KVR), F32), jax.ShapeDtypeStruct((H, 1, t), F32),
                 jax.ShapeDtypeStruct((t, H * NOPE), BF16)]
    scratch = [pltpu.VMEM((hb, bq, 128), F32), pltpu.VMEM((hb, bq, 128), F32), pltpu.VMEM((hb, bq, KVR), F32)]
    args = [itab, jtab, qcat, kc, wuv_h]
    if hosting:
        in_specs.append(ANY)
        out_specs.append(ANY)
        out_shape.append(jax.ShapeDtypeStruct((4,) + gather.shape, gather.dtype))
        scratch += AG_SEMS
        args.append(gather)
    gs = pltpu.PrefetchScalarGridSpec(num_scalar_prefetch=2, grid=(ng, len(pairs)), in_specs=in_specs,
                                      out_specs=out_specs, scratch_shapes=scratch)
    return pl.pallas_call(kern, grid_spec=gs, out_shape=out_shape, name="mla_flash_fwd",
                          compiler_params=_cp())(*args)


def _flash_dkv(qcat, kc, do_lat, lse_t, delta_t, bq, hb, exchange=()):
    nx = len(exchange)
    t = kc.shape[0]
    nq = t // bq
    ng = H // hb
    npairs = nq * (nq + 1) // 2
    steps = [(j, g, i) for j in range(nq) for g in range(ng) for i in range(j, nq)]
    jtab = jnp.asarray(np.array([s[0] for s in steps], np.int32))
    gtab = jnp.asarray(np.array([s[1] for s in steps], np.int32))
    itab = jnp.asarray(np.array([s[2] for s in steps], np.int32))
    ptab = jnp.asarray(np.array([s[2] * (s[2] + 1) // 2 + s[0] for s in steps], np.int32))

    def kern(jt, gt, it, pt, q_ref, k_ref, do_ref, lset_ref, dlt_ref, *rest):
        p_refs, (dk_ref, ds_ref), slots_refs = rest[:nx], rest[nx:nx + 2], rest[nx + 2:2 * nx + 2]
        dk_sc, dv_sc = rest[2 * nx + 2:2 * nx + 4]
        sems = rest[2 * nx + 4:]
        hooks = [_device_exchange_schedule(p_refs[e], slots_refs[e], sems[2 * e], sems[2 * e + 1]) for e in range(nx)]
        st = pl.program_id(0)
        j, g, i = jt[st], gt[st], it[st]

        if nx:
            @pl.when(st == 0)
            def _():
                for start, _ in hooks:
                    start()

        @pl.when((g == 0) & (i == j))
        def _():
            dk_sc[...] = jnp.zeros_like(dk_sc)
            dv_sc[...] = jnp.zeros_like(dv_sc)

        def update(masked):
            k = k_ref[...]
            v = k[:, 0:KVR]
            if masked:
                row = lax.broadcasted_iota(jnp.int32, (bq, bq), 0)
                col = lax.broadcasted_iota(jnp.int32, (bq, bq), 1)
                keep = row <= col

            def first_matmuls(hh):
                dob = do_ref[:, hh * KVR:(hh + 1) * KVR].astype(BF16)
                return _dot(k, q_ref[:, hh * KD:(hh + 1) * KD], NT), _dot(v, dob, NT), dob

            pending = [first_matmuls(hh) for hh in range(min(AHEAD, hb))]
            for hh in range(hb):
                s, dp, dob = pending.pop(0)
                if hh + AHEAD < hb:
                    pending.append(first_matmuls(hh + AHEAD))
                if masked:
                    s = jnp.where(keep, s, NEG)
                p = jnp.exp2(s - lset_ref[hh])
                dv_sc[...] += _dot(p.astype(BF16), dob, NN)
                dsb = (p * (dp - dlt_ref[hh])).astype(BF16)
                ds_ref[0, 0, hh] = dsb
                dk_sc[...] += _dot(dsb, q_ref[:, hh * KD:(hh + 1) * KD], NN)

        @pl.when(i > j)
        def _():
            update(False)

        @pl.when(i == j)
        def _():
            update(True)

        @pl.when((g == ng - 1) & (i == nq - 1))
        def _():
            dk_ref[:, 0:KVR] = dk_sc[:, 0:KVR] * LN2 + dv_sc[...]
            dk_ref[:, KVR:KD] = dk_sc[:, KVR:KD] * LN2

        if nx:
            @pl.when(st == len(steps) - 1)
            def _():
                for _, finish in hooks:
                    finish()

    in_specs = [pl.BlockSpec((bq, hb * KD), lambda s, jt, gt, it, pt: (it[s], gt[s])),
                pl.BlockSpec((bq, KD), lambda s, jt, gt, it, pt: (jt[s], 0)),
                pl.BlockSpec((bq, hb * KVR), lambda s, jt, gt, it, pt: (it[s], gt[s])),
                pl.BlockSpec((hb, 1, bq), lambda s, jt, gt, it, pt: (gt[s], 0, it[s])),
                pl.BlockSpec((hb, 1, bq), lambda s, jt, gt, it, pt: (gt[s], 0, it[s]))] + [ANY] * nx
    out_specs = [pl.BlockSpec((bq, KD), lambda s, jt, gt, it, pt: (jt[s], 0)),
                 pl.BlockSpec((1, 1, hb, bq, bq), lambda s, jt, gt, it, pt: (gt[s], pt[s], 0, 0, 0))] + [ANY] * nx
    out_shape = [jax.ShapeDtypeStruct((t, KD), F32), jax.ShapeDtypeStruct((ng, npairs, hb, bq, bq), BF16)]
    out_shape += [jax.ShapeDtypeStruct((8, e.shape[1] // 2, D), e.dtype) for e in exchange]
    scratch = [pltpu.VMEM((bq, KD), F32), pltpu.VMEM((bq, KVR), F32)] + ALL_SEMS * nx
    args = [jtab, gtab, itab, ptab, qcat, kc, do_lat, lse_t, delta_t, *exchange]
    gs = pltpu.PrefetchScalarGridSpec(num_scalar_prefetch=4, grid=(len(steps),), in_specs=in_specs,
                                      out_specs=out_specs, scratch_shapes=scratch)
    return pl.pallas_call(kern, grid_spec=gs, out_shape=out_shape, name="mla_flash_dkv",
                          compiler_params=_cp())(*args)


def _flash_dq(ds_all, kc_t, q2, wuk_h, cos, sin, bq, exchange=None):
    nq = kc_t.shape[0]
    t = nq * bq
    ngrp, _, hper = ds_all.shape[:3]
    pairs = _causal_pairs(nq)
    itab = jnp.asarray(np.array([p[0] for p in pairs], np.int32))
    jtab = jnp.asarray(np.array([p[1] for p in pairs], np.int32))
    hosting = exchange is not None

    def kern(it, jt, *refs):
        ds_refs, (kt_ref, q_ref, w_ref, c_ref, s_ref), rest = refs[:ngrp], refs[ngrp:ngrp + 5], refs[ngrp + 5:]
        if hosting:
            p_ref, dq_ref, dw_ref, slots_ref, acc_sc, send_sems, recv_sems = rest
            xc_start, xc_finish = _device_exchange_schedule(p_ref, slots_ref, send_sems, recv_sems)
        else:
            dq_ref, dw_ref, acc_sc = rest
        st = pl.program_id(0)
        i, j = it[st], jt[st]
        kt = kt_ref[...]

        def ds(hh):
            return ds_refs[hh // hper][0, 0, hh % hper]

        @pl.when(st == 0)
        def _():
            dw_ref[...] = jnp.zeros_like(dw_ref)
            if hosting:
                xc_start()

        @pl.when(j == 0)
        def _():
            for hh in range(H):
                acc_sc[hh] = _dot(kt, ds(hh), NN)

        @pl.when((j > 0) & (j < i))
        def _():
            for hh in range(H):
                acc_sc[hh] += _dot(kt, ds(hh), NN)

        @pl.when(j == i)
        def _():
            cos_, sin_ = c_ref[...], s_ref[...]
            for hh in range(H):
                tot = _dot(kt, ds(hh), NN)
                tot = jnp.where(i > 0, tot + acc_sc[hh], tot)
                dq_h = tot.T * MLA_SCALE
                dql = dq_h[:, 0:KVR].astype(BF16)
                dq_ref[:, hh * NOPE:(hh + 1) * NOPE] = _dot(dql, w_ref[hh], NN).astype(BF16)
                dq_ref[:, H * NOPE + hh * RP:H * NOPE + (hh + 1) * RP] = _rope_t(dq_h[:, KVR:KD], cos_, sin_).astype(BF16)
                dw_ref[hh] += _dot(q_ref[:, hh * NOPE:(hh + 1) * NOPE].astype(BF16), dql, TN)

        if hosting:
            @pl.when(st == len(pairs) - 1)
            def _():
                xc_finish()

    def group(gi):
        return pl.BlockSpec((1, 1, hper, bq, bq), lambda s, it, jt: (gi, s, 0, 0, 0))

    in_specs = [group(gi) for gi in range(ngrp)] + [
        pl.BlockSpec((None, KD, bq), lambda s, it, jt: (jt[s], 0, 0)),
        pl.BlockSpec((bq, 2 * H * NOPE), lambda s, it, jt: (it[s], 0)),
        pl.BlockSpec((H, KVR, NOPE), lambda s, it, jt: (0, 0, 0)),
        pl.BlockSpec((bq, RP), lambda s, it, jt: (it[s], 0)), pl.BlockSpec((bq, RP), lambda s, it, jt: (it[s], 0))]
    out_specs = [pl.BlockSpec((bq, 2 * H * NOPE), lambda s, it, jt: (it[s], 0)),
                 pl.BlockSpec((H, NOPE, KVR), lambda s, it, jt: (0, 0, 0))]
    out_shape = [jax.ShapeDtypeStruct((t, 2 * H * NOPE), BF16), jax.ShapeDtypeStruct((H, NOPE, KVR), F32)]
    scratch = [pltpu.VMEM((H, KD, bq), F32)]
    args = [itab, jtab] + [ds_all] * ngrp + [kc_t, q2, wuk_h, cos, sin]
    if hosting:
        in_specs.append(ANY)
        out_specs.append(ANY)
        out_shape.append(jax.ShapeDtypeStruct((8, exchange.shape[1] // 2, D), exchange.dtype))
        scratch += ALL_SEMS
        args.append(exchange)
    gs = pltpu.PrefetchScalarGridSpec(num_scalar_prefetch=2, grid=(len(pairs),), in_specs=in_specs,
                                      out_specs=out_specs, scratch_shapes=scratch)
    return pl.pallas_call(kern, grid_spec=gs, out_shape=out_shape, name="mla_flash_dq",
                          compiler_params=_cp())(*args)


def _bucket_table():
    d = np.arange(WIN)
    max_exact = NBKT // 2
    nf = np.maximum(d, 1).astype(np.float32)
    large = max_exact + (np.log(nf / np.float32(max_exact)) / np.float32(math.log(WIN / max_exact))
                         * np.float32(NBKT - max_exact)).astype(np.int32)
    large = np.minimum(large, NBKT - 1)
    bucket = np.where(d < max_exact, d, large).astype(np.int32)
    jj = np.arange(2 * WIN)[:, None]
    ii = np.arange(WIN)[None, :]
    dist = ii + WIN - jj
    valid = (dist >= 0) & (dist < WIN)
    return np.where(valid, bucket[np.clip(dist, 0, WIN - 1)], -1).astype(np.int32)


def _bias_build(rel_bias, bkt):
    def kern(bk_ref, rb_ref, o_ref):
        bk = bk_ref[...]
        for hd in range(QH):
            acc = jnp.full((2 * WIN, WIN), NEG, F32)
            for b in range(NBKT):
                acc = jnp.where(bk == b, rb_ref[b, hd], acc)
            o_ref[hd] = acc

    return pl.pallas_call(
        kern, in_specs=[pl.BlockSpec(memory_space=pltpu.VMEM), pl.BlockSpec(memory_space=pltpu.SMEM)],
        out_specs=pl.BlockSpec(memory_space=pltpu.VMEM),
        out_shape=jax.ShapeDtypeStruct((QH, 2 * WIN, WIN), F32), name="swa_bias_build")(bkt, rel_bias)


def _bias_bwd(dbias, bkt):
    def kern(db_ref, bk_ref, o_ref):
        bk = bk_ref[...]
        for hd in range(QH):
            g = db_ref[hd]
            for b in range(NBKT):
                r = b * QH + hd
                o_ref[r:r + 1, :] = jnp.sum(jnp.where(bk == b, g, 0.0), axis=0, keepdims=True)

    return pl.pallas_call(
        kern, in_specs=[pl.BlockSpec(memory_space=pltpu.VMEM), pl.BlockSpec(memory_space=pltpu.VMEM)],
        out_specs=pl.BlockSpec(memory_space=pltpu.VMEM),
        out_shape=jax.ShapeDtypeStruct((NBKT * QH, WIN), F32), name="swa_bias_bwd")(dbias, bkt)


def _swa_finish_scores(raw, bias, first):
    s = raw * SWA_SCALE + bias
    if first is not None:
        row = lax.broadcasted_iota(jnp.int32, s.shape, 0)
        s = jnp.where(jnp.logical_or(jnp.logical_not(first), row >= WIN), s, NEG)
    return s


def _swa_fwd(qkv_t, bias, sinks, qb):
    t = qkv_t.shape[1]
    w = qb * WIN
    nst = t // w

    def kern(q_ref, kc_ref, kp_ref, vc_ref, vp_ref, b_ref, sk_ref, o_ref, lse_ref):
        n = pl.program_id(0)
        kfull = jnp.concatenate([kp_ref[...], kc_ref[...]], axis=1)
        vfull = jnp.concatenate([vp_ref[...], vc_ref[...]], axis=1)
        head_row = lax.broadcasted_iota(jnp.int32, (QH, WIN), 0)
        groups = [(b, kh) for b in range(qb) for kh in range(KVH)]

        def raw_scores(b, kh):
            k_band = kfull[kh * HD:(kh + 1) * HD, b * WIN:(b + 2) * WIN]
            return [_dot(k_band, q_ref[(kh * G + g) * HD:(kh * G + g + 1) * HD, b * WIN:(b + 1) * WIN], TN)
                    for g in range(G)]

        o_rows = [[] for _ in range(qb)]
        lse_tiles = [jnp.zeros((QH, WIN), F32) for _ in range(qb)]
        pending = [raw_scores(*grp) for grp in groups[:AHEAD]]
        for gi, (b, kh) in enumerate(groups):
            scores = pending.pop(0)
            if gi + AHEAD < len(groups):
                pending.append(raw_scores(*groups[gi + AHEAD]))
            v_band = vfull[kh * HD:(kh + 1) * HD, b * WIN:(b + 2) * WIN]
            for g in range(G):
                hd = kh * G + g
                s = _swa_finish_scores(scores[g], b_ref[hd], (n == 0) if b == 0 else None)
                sink = sk_ref[hd]
                m = jnp.maximum(jnp.max(s, axis=0, keepdims=True), sink)
                p = jnp.exp(s - m)
                den = jnp.sum(p, axis=0, keepdims=True) + jnp.exp(sink - m)
                p = p / den
                o_rows[b].append(_dot(v_band, p.astype(BF16), NN))
                lse_tiles[b] = jnp.where(head_row == hd, m + jnp.log(den), lse_tiles[b])
        o_ref[...] = jnp.concatenate([jnp.concatenate(rows, axis=0) for rows in o_rows], axis=1)
        lse_ref[...] = jnp.concatenate(lse_tiles, axis=1)

    prev = lambda r: (lambda n: (r, jnp.maximum(n * qb - 1, 0)))
    return pl.pallas_call(
        kern, grid=(nst,),
        in_specs=[pl.BlockSpec((QH * HD, w), lambda n: (0, n)),
                  pl.BlockSpec((KVH * HD, w), lambda n: (4, n)), pl.BlockSpec((KVH * HD, WIN), prev(4)),
                  pl.BlockSpec((KVH * HD, w), lambda n: (5, n)), pl.BlockSpec((KVH * HD, WIN), prev(5)),
                  pl.BlockSpec((QH, 2 * WIN, WIN), lambda n: (0, 0, 0)),
                  pl.BlockSpec(memory_space=pltpu.SMEM)],
        out_specs=[pl.BlockSpec((QH * HD, w), lambda n: (0, n)), pl.BlockSpec((QH, w), lambda n: (0, n))],
        out_shape=[jax.ShapeDtypeStruct((QH * HD, t), F32), jax.ShapeDtypeStruct((QH, t), F32)],
        name="swa_fwd", compiler_params=_cp())(qkv_t, qkv_t, qkv_t, qkv_t, qkv_t, bias, sinks)


def _swa_bwd(qkv_t, do_t, o_t, lse, bias, sinks, qb):
    t = qkv_t.shape[1]
    w = qb * WIN
    nst = t // w
    nblk = t // WIN

    def kern(q_ref, kc_ref, kp_ref, vc_ref, vp_ref, do_ref, o_ref, lse_ref, qn_ref, don_ref, on_ref, lsen_ref,
             b_ref, sk_ref, dqkv_ref, db_ref, dsk_ref):
        n = pl.program_id(0)

        @pl.when(n == 0)
        def _():
            db_ref[...] = jnp.zeros_like(db_ref)
            dsk_ref[...] = jnp.zeros_like(dsk_ref)

        kfull = jnp.concatenate([kp_ref[...], kc_ref[...]], axis=1)
        vfull = jnp.concatenate([vp_ref[...], vc_ref[...]], axis=1)
        head_row = lax.broadcasted_iota(jnp.int32, (QH, WIN), 0)
        db_acc = [None] * QH
        dsk_tile = jnp.zeros((QH, WIN), F32)
        prev_part = [[[None] * qb for _ in range(KVH)] for _ in range(2)]
        cur_part = [[[None] * qb for _ in range(KVH)] for _ in range(2)]
        groups = [(b, kh) for b in range(qb) for kh in range(KVH)]

        def first_matmuls(b, kh):
            k_band = kfull[kh * HD:(kh + 1) * HD, b * WIN:(b + 2) * WIN]
            v_band = vfull[kh * HD:(kh + 1) * HD, b * WIN:(b + 2) * WIN]
            out = []
            for g in range(G):
                rs = slice((kh * G + g) * HD, (kh * G + g + 1) * HD)
                dob = do_ref[rs, b * WIN:(b + 1) * WIN].astype(BF16)
                out.append((_dot(k_band, q_ref[rs, b * WIN:(b + 1) * WIN], TN), _dot(v_band, dob, TN), dob))
            return out

        dq_rows = [[] for _ in range(qb)]
        pending = [first_matmuls(*grp) for grp in groups[:AHEAD]]
        for gi, (b, kh) in enumerate(groups):
            first = pending.pop(0)
            if gi + AHEAD < len(groups):
                pending.append(first_matmuls(*groups[gi + AHEAD]))
            cs = slice(b * WIN, (b + 1) * WIN)
            k_band = kfull[kh * HD:(kh + 1) * HD, b * WIN:(b + 2) * WIN]
            dk_b = dv_b = None
            for g in range(G):
                hd = kh * G + g
                rs = slice(hd * HD, (hd + 1) * HD)
                raw, dp, dob = first[g]
                lse_h = lse_ref[hd:hd + 1, cs]
                s = _swa_finish_scores(raw, b_ref[hd], (n == 0) if b == 0 else None)
                p = jnp.exp(s - lse_h)
                dl = jnp.sum(do_ref[rs, cs] * o_ref[rs, cs], axis=0, keepdims=True)
                ds = p * (dp - dl)
                db_acc[hd] = ds if db_acc[hd] is None else db_acc[hd] + ds
                dsk_tile = jnp.where(head_row == hd, dsk_tile - jnp.exp(sk_ref[hd] - lse_h) * dl, dsk_tile)
                dss = (ds * SWA_SCALE).astype(BF16)
                dq_rows[b].append(_dot(k_band, dss, NN).astype(BF16))
                dk_h = _dot(q_ref[rs, cs], dss, NT)
                dv_h = _dot(dob, p.astype(BF16), NT)
                dk_b = dk_h if dk_b is None else dk_b + dk_h
                dv_b = dv_h if dv_b is None else dv_b + dv_h
            for which, val in ((0, dk_b), (1, dv_b)):
                prev_part[which][kh][b] = val[:, 0:WIN]
                cur_part[which][kh][b] = val[:, WIN:2 * WIN]
        dq_cols = [jnp.concatenate(rows, axis=0) for rows in dq_rows]

        live = n < nst - 1
        ls = slice((qb - 1) * WIN, qb * WIN)
        halo = [[None] * KVH for _ in range(2)]
        for kh in range(KVH):
            k_last = kc_ref[kh * HD:(kh + 1) * HD, ls]
            v_last = vc_ref[kh * HD:(kh + 1) * HD, ls]
            dk_b = dv_b = None
            for g in range(G):
                hd = kh * G + g
                rs = slice(hd * HD, (hd + 1) * HD)
                q_t = qn_ref[rs, :]
                do = don_ref[rs, :]
                s = _dot(k_last, q_t, TN) * SWA_SCALE + b_ref[hd, 0:WIN, :]
                p = jnp.exp(s - lsen_ref[hd:hd + 1, :])
                dob = do.astype(BF16)
                dp = _dot(v_last, dob, TN)
                dl = jnp.sum(do * on_ref[rs, :], axis=0, keepdims=True)
                dss = (p * (dp - dl) * SWA_SCALE).astype(BF16)
                dk_h = _dot(q_t, dss, NT)
                dv_h = _dot(dob, p.astype(BF16), NT)
                dk_b = dk_h if dk_b is None else dk_b + dk_h
                dv_b = dv_h if dv_b is None else dv_b + dv_h
            halo[0][kh] = jnp.where(live, dk_b, 0.0)
            halo[1][kh] = jnp.where(live, dv_b, 0.0)

        kv_rows = []
        for which in range(2):
            for kh in range(KVH):
                blocks = [cur_part[which][kh][p] + (prev_part[which][kh][p + 1] if p + 1 < qb else halo[which][kh])
                          for p in range(qb)]
                kv_rows.append(jnp.concatenate(blocks, axis=1))
        dqkv_ref[...] = jnp.concatenate(
            [jnp.concatenate(dq_cols, axis=1), jnp.concatenate(kv_rows, axis=0).astype(BF16)], axis=0)
        db_ref[...] += jnp.stack(db_acc)
        dsk_ref[...] += dsk_tile

    prev = lambda r: (lambda n: (r, jnp.maximum(n * qb - 1, 0)))
    nxt = lambda n: (0, jnp.minimum((n + 1) * qb, nblk - 1))
    big = lambda: pl.BlockSpec((QH * HD, w), lambda n: (0, n))
    return pl.pallas_call(
        kern, grid=(nst,),
        in_specs=[big(),
                  pl.BlockSpec((KVH * HD, w), lambda n: (4, n)), pl.BlockSpec((KVH * HD, WIN), prev(4)),
                  pl.BlockSpec((KVH * HD, w), lambda n: (5, n)), pl.BlockSpec((KVH * HD, WIN), prev(5)),
                  big(), big(), pl.BlockSpec((QH, w), lambda n: (0, n)),
                  pl.BlockSpec((QH * HD, WIN), nxt), pl.BlockSpec((QH * HD, WIN), nxt),
                  pl.BlockSpec((QH * HD, WIN), nxt), pl.BlockSpec((QH, WIN), nxt),
                  pl.BlockSpec((QH, 2 * WIN, WIN), lambda n: (0, 0, 0)),
                  pl.BlockSpec(memory_space=pltpu.SMEM)],
        out_specs=[pl.BlockSpec(((QH + 2 * KVH) * HD, w), lambda n: (0, n)),
                   pl.BlockSpec((QH, 2 * WIN, WIN), lambda n: (0, 0, 0)),
                   pl.BlockSpec((QH, WIN), lambda n: (0, 0))],
        out_shape=[jax.ShapeDtypeStruct(((QH + 2 * KVH) * HD, t), BF16),
                   jax.ShapeDtypeStruct((QH, 2 * WIN, WIN), F32), jax.ShapeDtypeStruct((QH, WIN), F32)],
        name="swa_bwd", compiler_params=_cp())(
            qkv_t, qkv_t, qkv_t, qkv_t, qkv_t, do_t, o_t, lse, qkv_t, do_t, o_t, lse, bias, sinks)


def _adamw_math(w, g, m, v):
    nm = B1 * m + (1.0 - B1) * g
    nv = B2 * v + (1.0 - B2) * (g * g)
    mhat = nm * (1.0 / (1.0 - B1 ** STEP))
    vhat = nv * (1.0 / (1.0 - B2 ** STEP))
    return -LR * (mhat / (jnp.sqrt(vhat) + ADAM_EPS) + WD * w), nm, nv


def _adamw_layers(w, m, v, g0buf, g1buf, off, name, tm=512):
    rows = w.shape[1]
    nb, ob = rows // tm, off // tm

    def kern(w_ref, m_ref, v_ref, g0_ref, g1_ref, gr_ref, d_ref, nm_ref, nv_ref):
        g_ = jnp.where(pl.program_id(0) == 0, g0_ref[...], g1_ref[...])
        gr_ref[...] = g_
        d_ref[...], nm_ref[...], nv_ref[...] = _adamw_math(w_ref[...], g_, m_ref[...], v_ref[...])

    lay = pl.BlockSpec((None, tm, D), lambda l, i: (l, i, 0))
    gsp = pl.BlockSpec((tm, D), lambda l, i: (ob + i, 0))
    return pl.pallas_call(
        kern, grid=(2, nb), in_specs=[lay, lay, lay, gsp, gsp], out_specs=[lay] * 4,
        out_shape=[jax.ShapeDtypeStruct(w.shape, F32)] * 4, name=name, compiler_params=_cp())(w, m, v, g0buf, g1buf)


def _adamw(w, g, m, v, name, tm=544):
    r = w.shape[0]
    tm = r if r % tm else tm

    def kern(w_ref, g_ref, m_ref, v_ref, d_ref, nm_ref, nv_ref):
        d_ref[...], nm_ref[...], nv_ref[...] = _adamw_math(w_ref[...], g_ref[...], m_ref[...], v_ref[...])

    row = pl.BlockSpec((tm, D), lambda i: (i, 0))
    sds = jax.ShapeDtypeStruct((r, D), F32)
    return pl.pallas_call(kern, grid=(r // tm,), in_specs=[row] * 4, out_specs=[row] * 3, out_shape=[sds] * 3,
                          name=name, compiler_params=_cp())(w, g, m, v)


def _mesh_pos():
    return lax.axis_index("x"), lax.axis_index("y"), lax.axis_index("c")


ANY = pl.BlockSpec(memory_space=pl.ANY)


AG_SEMS = [pltpu.SemaphoreType.DMA((6,)), pltpu.SemaphoreType.DMA((6,))]


def _allgather_schedule(w_ref, out_ref, send_sems, recv_sems):
    half = w_ref.shape[0] // 2
    x, y, c = _mesh_pos()
    me, sibling = (x, y, c), (x, y, 1 - c)
    chips = [(1 - x, y), (x, 1 - y), (1 - x, 1 - y)]

    def rows(px, py, pc):
        return out_ref.at[2 * px + py, pl.ds(pc * half, half), :]

    def copy(k, block, to, src=None):
        return pltpu.make_async_remote_copy(
            src_ref=rows(*block) if src is None else src, dst_ref=rows(*block),
            send_sem=send_sems.at[k], recv_sem=recv_sems.at[k], device_id=to, device_id_type=MESH)

    def first():
        return [copy(j, me, (*chip, c), src=w_ref.at[pl.ds(c * half, half), :]) for j, chip in enumerate(chips)]

    def passed():
        return [copy(3 + j, (*chip, c), sibling) for j, chip in enumerate(chips)]

    def start():
        for cp in first():
            cp.start()

    def forward():
        for j, chip in enumerate(chips):
            copy(j, (*chip, c), me).wait_recv()
            passed()[j].start()

    def finish():
        for j, chip in enumerate(chips):
            copy(3 + j, (*chip, 1 - c), me).wait_recv()
        for cp in first() + passed():
            cp.wait_send()

    return start, forward, finish


def _allgather_weights(wpack):
    def body(w_ref, out_ref, send_sems, recv_sems):
        start, forward, finish = _allgather_schedule(w_ref, out_ref, send_sems, recv_sems)
        start()
        forward()
        finish()

    return pl.pallas_call(
        body, out_shape=jax.ShapeDtypeStruct((4,) + wpack.shape, wpack.dtype), in_specs=[ANY], out_specs=ANY,
        scratch_shapes=AG_SEMS, name="allgather_weights")(wpack)


def _row_tile(rows):
    t = min(rows, 512)
    while rows % t or t % 16:
        t -= 16
    return t


ALL_SEMS = [pltpu.SemaphoreType.DMA((7,)), pltpu.SemaphoreType.DMA((7,))]


def _device_exchange_schedule(g_ref, out_ref, send_sems, recv_sems):
    half = g_ref.shape[1] // 2
    x, y, c = _mesh_pos()
    me = 4 * x + 2 * y + c
    peers = [(x ^ (k >> 2), y ^ ((k >> 1) & 1), c ^ (k & 1)) for k in range(1, 8)]

    def sends():
        return [pltpu.make_async_remote_copy(
            src_ref=g_ref.at[2 * px + py, pl.ds(pc * half, half), :], dst_ref=out_ref.at[me],
            send_sem=send_sems.at[j], recv_sem=recv_sems.at[j], device_id=(px, py, pc), device_id_type=MESH)
            for j, (px, py, pc) in enumerate(peers)]

    def start():
        for cp in sends():
            cp.start()

    def finish():
        for j, (px, py, pc) in enumerate(peers):
            pltpu.make_async_remote_copy(
                src_ref=out_ref.at[me], dst_ref=out_ref.at[4 * px + 2 * py + pc], send_sem=send_sems.at[j],
                recv_sem=recv_sems.at[j], device_id=(px, py, pc), device_id_type=MESH).wait_recv()
        for cp in sends():
            cp.wait_send()

    return start, finish


def _sum_devices(slots, g, pos, tag):
    half = slots.shape[1]
    tm = _row_tile(half)
    nb = half // tm

    def kern(pos_ref, own_ref, *refs):
        acc = own_ref[0].astype(F32)
        for s_ref in refs[:7]:
            acc = acc + s_ref[0].astype(F32)
        refs[7][...] = acc

    def slot(k):
        return pl.BlockSpec((1, tm, D), lambda i, pos: (jnp.bitwise_xor(pos[2], k), i, 0))

    gs = pltpu.PrefetchScalarGridSpec(
        num_scalar_prefetch=1, grid=(nb,),
        in_specs=[pl.BlockSpec((1, tm, D), lambda i, pos: (pos[0], pos[1] * nb + i, 0))] + [slot(k) for k in range(1, 8)],
        out_specs=pl.BlockSpec((tm, D), lambda i, pos: (pos[1] * nb + i, 0)))
    return pl.pallas_call(kern, grid_spec=gs, out_shape=jax.ShapeDtypeStruct((2 * half, D), F32),
                          name=f"rs_sum_devices_{tag}", compiler_params=_cp())(pos, g, *([slots] * 7))


def _reduce_scatter_finish(slots, g, pos, tag):
    return _join_core_halves(_sum_devices(slots, g, pos, tag), tag)


def _join_core_halves(r, tag):
    half = r.shape[0] // 2

    def body(r_ref, out_ref, send_sem, recv_sem):
        x, y, c = _mesh_pos()
        mine = out_ref.at[pl.ds(c * half, half), :]
        cp = pltpu.make_async_remote_copy(
            src_ref=mine, dst_ref=mine, send_sem=send_sem, recv_sem=recv_sem,
            device_id=(x, y, 1 - c), device_id_type=MESH)
        cp.start()
        theirs = out_ref.at[pl.ds((1 - c) * half, half), :]
        pltpu.make_async_remote_copy(
            src_ref=theirs, dst_ref=theirs, send_sem=send_sem, recv_sem=recv_sem,
            device_id=(x, y, 1 - c), device_id_type=MESH).wait_recv()
        cp.wait_send()

    return pl.pallas_call(
        body, out_shape=jax.ShapeDtypeStruct(r.shape, r.dtype), in_specs=[ANY], out_specs=ANY,
        input_output_aliases={0: 0},
        scratch_shapes=[pltpu.SemaphoreType.DMA, pltpu.SemaphoreType.DMA],
        name=f"rs_join_cores_{tag}")(r)


def _allreduce_small(v, name):
    def body(v_ref, out_ref, gat, send_sems, recv_sems):
        x, y, c = _mesh_pos()
        me = 4 * x + 2 * y + c
        gat[me] = v_ref[...]
        sends = []
        for k in range(1, 8):
            peer = (x ^ (k >> 2), y ^ ((k >> 1) & 1), c ^ (k & 1))
            cp = pltpu.make_async_remote_copy(
                src_ref=v_ref, dst_ref=gat.at[me], send_sem=send_sems.at[k - 1], recv_sem=recv_sems.at[k - 1],
                device_id=peer, device_id_type=MESH)
            cp.start()
            sends.append(cp)
        for k in range(1, 8):
            px, py, pc = x ^ (k >> 2), y ^ ((k >> 1) & 1), c ^ (k & 1)
            pltpu.make_async_remote_copy(
                src_ref=v_ref, dst_ref=gat.at[4 * px + 2 * py + pc], send_sem=send_sems.at[k - 1],
                recv_sem=recv_sems.at[k - 1], device_id=(px, py, pc), device_id_type=MESH).wait_recv()
        for cp in sends:
            cp.wait_send()
        acc = gat[0]
        for d in range(1, 8):
            acc = acc + gat[d]
        out_ref[...] = acc

    return pl.pallas_call(
        body, out_shape=jax.ShapeDtypeStruct(v.shape, F32),
        in_specs=[pl.BlockSpec(memory_space=pltpu.VMEM)], out_specs=pl.BlockSpec(memory_space=pltpu.VMEM),
        scratch_shapes=[pltpu.VMEM((8,) + v.shape, F32), pltpu.SemaphoreType.DMA((7,)), pltpu.SemaphoreType.DMA((7,))],
        name=name)(v)


def _mlp_fwd(xb, w_up, w_down, tag):
    a = _mm(xb, w_up[0], "nn", f"mlp_up_{tag}", out_dtype=BF16, relu2=True, b_view=("cols", w_up[1]), tm=2048)
    return a, _mm(a, w_down[0], "nn", f"mlp_down_{tag}", b_view=("rows", w_down[1]), tm=2048)


def _mlp_bwd(dz, dzb, xb, a, w_up, w_down, tag):
    du = _mm(dzb, w_down[0], "nt", f"mlp_down_dx_{tag}", out_dtype=BF16, gate_a=a, b_view=("rows", w_down[1]),
             tm=2048)
    gsh = _mm(xb, du, "tn", f"mlp_up_dw_{tag}", out_dtype=BF16, out_view=("cols", 2 * ROWS["mlp_w_up"], 0, None),
              tk=2048)
    gsh = _mm(a, dzb, "tn", f"mlp_down_dw_{tag}", out_dtype=BF16,
              out_view=("rows", 2 * ROWS["mlp_w_up"], ROWS["mlp_w_up"], gsh), tk=2048)
    dx = _mm(du, w_up[0], "nt", f"mlp_up_dx_{tag}", addend=dz, add_scale=ALPHA, b_view=("cols", w_up[1]))
    return dx, gsh


def _fwd_bwd(x, target, w, dist=None, bq=512, qb=8, hb=8):
    t = x.shape[0]
    bq = min(bq, t)
    qb = min(qb, t // WIN)
    cos, sin = _rope_tables(t)
    bkt = jnp.asarray(_bucket_table())
    w_in = jnp.pad(w[("mla_w_in", None)], ((0, 0), (0, HW - (QR + KVR + ROPE))))
    wuq = w[("mla_w_uq", None)]
    wq2 = jnp.concatenate([wuq[:, :, :NOPE].reshape(QR, H * NOPE),
                           jnp.pad(wuq[:, :, NOPE:], ((0, 0), (0, 0), (0, RP - ROPE))).reshape(QR, H * RP)], axis=1)
    wuk_t = w[("mla_w_uk", None)].transpose(1, 2, 0)
    wuk_h = w[("mla_w_uk", None)].transpose(1, 0, 2)
    wuv_h = w[("mla_w_uv", None)].transpose(1, 0, 2)
    w_o = w[("mla_w_o", None)]
    sinks = w["swa_sinks"].reshape(QH)
    lnp = lambda n, l: w[n][l]
    reduced = {}

    hh = _mm(x, w_in, "nn", "mla_in", tm=2048)
    cq, kc = _mla_pre(hh, w["mla_g_q"], w["mla_g_kv"], cos, sin)
    q2 = _mm(cq, wq2, "nn", "mla_uq", tm=2048)
    qcat = _q_prep(q2, wuk_t, cos, sin)
    if dist is None:
        o_lat, lse0_t, o0 = _flash_fwd(qcat, kc, wuv_h, bq, hb)
    else:
        o_lat, lse0_t, o0, wall = _flash_fwd(qcat, kc, wuv_h, bq, hb, gather=dist.late_pack)
        wall = lax.dynamic_update_slice(wall, dist.late_pack[None], (dist.shard, 0, 0))
        w = {**w, **_full_from_gathered(AG_LATE, wall, dist.shard_shapes)}
    wqkv = jnp.concatenate([w[("swa_w_q", None)], w[("kv_w_shared", None)]], axis=1)
    wqkv_t = wqkv.T
    wo_s = w[("swa_w_o", None)]
    x1b, xh1, r1 = _mm(o0, w_o, "nn", "mla_out_ln", tm=512,
                       ln=(x, lnp("ln_mix_g", 0), lnp("ln_mix_b", 0), None))
    a0, f0 = _mlp_fwd(x1b, w[("mlp_w_up", 0)], w[("mlp_w_down", 0)], 0)
    x2b, xh2, r2 = _add_ln(xh1, f0, lnp("ln_mlp_g", 0), lnp("ln_mlp_b", 0), "ln_mlp_0",
                           res_affine=(lnp("ln_mix_g", 0), lnp("ln_mix_b", 0)))
    bias = _bias_build(w["rel_bias"], bkt)
    qkv_t = _mm(x2b, wqkv, "nn", "swa_qkv", out_dtype=BF16, out_t=True, tm=2048)
    os_t, lse1 = _swa_fwd(qkv_t, bias, sinks, qb)
    x3b, xh3, r3 = _mm(os_t, wo_s, "tn", "swa_out_ln", tm=512,
                       ln=(xh2, lnp("ln_mix_g", 1), lnp("ln_mix_b", 1), (lnp("ln_mlp_g", 0), lnp("ln_mlp_b", 0))))
    a1, f1 = _mlp_fwd(x3b, w[("mlp_w_up", 1)], w[("mlp_w_down", 1)], 1)
    _, xh4, r4 = _add_ln(xh3, f1, lnp("ln_mlp_g", 1), lnp("ln_mlp_b", 1), "ln_mlp_1",
                         res_affine=(lnp("ln_mix_g", 1), lnp("ln_mix_b", 1)))

    g = {}
    dz4, dz4b, dg_mlp1, db_mlp1, lpart = _ln_bwd(target, xh4, r4, lnp("ln_mlp_g", 1), "ln_mlp_1_bwd",
                                                 loss_b=lnp("ln_mlp_b", 1))
    dx3, g["mlp1"] = _mlp_bwd(dz4, dz4b, x3b, a1, w[("mlp_w_up", 1)], w[("mlp_w_down", 1)], 1)
    dz3, dz3b, dg_mix1, db_mix1 = _ln_bwd(dx3, xh3, r3, lnp("ln_mix_g", 1), "ln_mix_1_bwd")
    dos_t = _mm(dz3b, wo_s, "nt", "swa_out_dx", out_t=True, tm=2048)
    g[("swa_w_o", None)] = _mm(os_t, dz3b, "nn", "swa_out_dw", tk=2048)
    dqkv_t, dbias, dsk = _swa_bwd(qkv_t, dos_t, os_t, lse1, bias, sinks, qb)
    dwqkv = _mm(dqkv_t, x2b, "nn", "swa_qkv_dw", tk=2048).T
    g[("swa_w_q", None)], g[("kv_w_shared", None)] = dwqkv[:, :QH * HD], dwqkv[:, QH * HD:]
    dx2 = _mm(dqkv_t, wqkv_t, "tn", "swa_qkv_dx", addend=dz3, add_scale=ALPHA)
    g["rel_bias"] = jnp.sum(_bias_bwd(dbias, bkt), axis=-1).reshape(NBKT, QH)
    g["swa_sinks"] = jnp.sum(dsk, axis=-1).reshape(1, QH)
    dz2, dz2b, dg_mlp0, db_mlp0 = _ln_bwd(dx2, xh2, r2, lnp("ln_mlp_g", 0), "ln_mlp_0_bwd")
    dx1, g["mlp0"] = _mlp_bwd(dz2, dz2b, x1b, a0, w[("mlp_w_up", 0)], w[("mlp_w_down", 0)], 0)
    dz1, dz1b, dg_mix0, db_mix0 = _ln_bwd(dx1, xh1, r1, lnp("ln_mix_g", 0), "ln_mix_0_bwd")
    do0 = _mm(dz1b, w_o, "nt", "mla_out_dx", out_dtype=BF16, tm=2048)
    g[("mla_w_o", None)] = _mm(o0, dz1b, "tn", "mla_out_dw", tk=2048)
    do_lat, dwuv, delta_t = _o_up_bwd(do0, o_lat, wuv_h)
    g[("mla_w_uv", None)] = dwuv.transpose(1, 0, 2)
    kc_t = kc.reshape(t // bq, bq, KD).transpose(0, 2, 1)
    if dist is None:
        dk, ds_all = _flash_dkv(qcat, kc, do_lat, lse0_t, delta_t, bq, hb)
        dq2, dwuk = _flash_dq(ds_all, kc_t, q2, wuk_h, cos, sin, bq)
    else:
        g["mid"] = _grad_shards(RS_MID, g).astype(BF16)
        dk, ds_all, slots1, slots_mid = _flash_dkv(qcat, kc, do_lat, lse0_t, delta_t, bq, hb,
                                                  exchange=(g["mlp1"], g["mid"]))
        dq2, dwuk, slots0 = _flash_dq(ds_all, kc_t, q2, wuk_h, cos, sin, bq, exchange=g["mlp0"])
        for key, slots in (("mlp1", slots1), ("mid", slots_mid), ("mlp0", slots0)):
            reduced[key] = _reduce_scatter_finish(slots, g[key], dist.pos, key)
    g[("mla_w_uk", None)] = dwuk.transpose(2, 0, 1)
    dcq = _mm(dq2, wq2, "nt", "mla_uq_dx", tm=2048)
    dwq2 = _mm(cq, dq2, "tn", "mla_uq_dw", tk=2048)
    g[("mla_w_uq", None)] = jnp.concatenate([dwq2[:, :H * NOPE].reshape(QR, H, NOPE),
                                             dwq2[:, H * NOPE:].reshape(QR, H, RP)[:, :, :ROPE]], axis=2)
    dh, dgq, dgkv = _mla_pre_bwd(hh, dcq, dk, w["mla_g_q"], w["mla_g_kv"], cos, sin)
    g[("mla_w_in", None)] = _mm(x, dh, "tn", "mla_in_dw", tk=2048)[:, :QR + KVR + ROPE]
    if dist is None:
        grad_x = _mm(dh, w_in, "nt", "mla_in_dx", addend=dz1, add_scale=ALPHA)
    else:
        g["end"] = _grad_shards(RS_END, g).astype(BF16)
        grad_x, slots_end = _mm(dh, w_in, "nt", "mla_in_dx", addend=dz1, add_scale=ALPHA, exchange=g["end"])
        reduced["end"] = _reduce_scatter_finish(slots_end, g["end"], dist.pos, "end")
    g["mla_g_q"], g["mla_g_kv"] = dgq, dgkv
    g["ln_mix_g"] = jnp.concatenate([dg_mix0, dg_mix1], axis=0)
    g["ln_mix_b"] = jnp.concatenate([db_mix0, db_mix1], axis=0)
    g["ln_mlp_g"] = jnp.concatenate([dg_mlp0, dg_mlp1], axis=0)
    g["ln_mlp_b"] = jnp.concatenate([db_mlp0, db_mlp1], axis=0)
    return lpart, grad_x, g, reduced


def _rows(a):
    return a.reshape(-1, D)


def _piece(a, layer):
    return _rows(a if layer is None else a[layer])


def _pack_group(group, parts):
    return jnp.concatenate([_piece(parts[n], l) for n, l in group], axis=0)


def _unpack_group(group, buf, like):
    out, off = {}, 0
    for n, l in group:
        shp = like[n].shape if l is None else like[n].shape[1:]
        out[(n, l)] = buf[off:off + ROWS[n]].reshape(shp)
        off += ROWS[n]
    return out


def _by_name(pieces):
    out = {n: a for (n, l), a in pieces.items() if l is None}
    for n in {n for (n, l) in pieces if l is not None}:
        out[n] = jnp.stack([pieces[(n, 0)], pieces[(n, 1)]])
    return out


def _full_from_gathered(group, wall, shard_shapes):
    out, off = {}, 0
    for n, l in group:
        shp = tuple(shard_shapes[n])
        if n in ("mlp_w_up", "mlp_w_down"):
            out[(n, l)] = (wall, off)
        elif n == "kv_w_shared":
            out[(n, l)] = wall[:, off:off + ROWS[n]].reshape((4 * shp[0],) + shp[1:])
        else:
            out[(n, l)] = wall[:, off:off + ROWS[n]].reshape((4 * shp[1],) + shp[2:])
        off += ROWS[n]
    return out


def _grad_shards(group, g):
    return jnp.concatenate([g[(n, l)].reshape(4, ROWS[n], D) for n, l in group], axis=1)


SMALL = (("ln_mix_g", 0, 2), ("ln_mix_b", 2, 2), ("ln_mlp_g", 4, 2), ("ln_mlp_b", 6, 2),
         ("swa_sinks", 8, 1), ("mla_g_q", 9, 1), ("mla_g_kv", 10, 1), ("rel_bias", 11, 1))
LOSS_ROW = 12


def _pack_small(parts, extra_row=None):
    rows = []
    for n, _, nr in SMALL:
        a = parts[n].reshape(nr, -1).astype(F32)
        rows.append(jnp.pad(a, ((0, 0), (0, D - a.shape[1]))))
    if extra_row is not None:
        rows.append(extra_row)
    rows.append(jnp.zeros((SMALL_ROWS - sum(r.shape[0] for r in rows), D), F32))
    return jnp.concatenate(rows, axis=0)


def _unpack_small(buf, like):
    out = {}
    for n, r0, nr in SMALL:
        size = like[n].size // nr
        out[n] = buf[r0:r0 + nr, :size].reshape(like[n].shape)
    return out


def kernel(x, mla_w_in, mla_g_q, mla_g_kv, mla_w_uq, mla_w_uk, mla_w_uv, mla_w_o, kv_w_shared, swa_w_q, swa_sinks, swa_w_o, rel_bias, mlp_w_up, mlp_w_down, ln_mix_g, ln_mix_b, ln_mlp_g, ln_mlp_b, loss_target, m_mla_w_in, m_mla_g_q, m_mla_g_kv, m_mla_w_uq, m_mla_w_uk, m_mla_w_uv, m_mla_w_o, m_kv_w_shared, m_swa_w_q, m_swa_sinks, m_swa_w_o, m_rel_bias, m_mlp_w_up, m_mlp_w_down, m_ln_mix_g, m_ln_mix_b, m_ln_mlp_g, m_ln_mlp_b, v_mla_w_in, v_mla_g_q, v_mla_g_kv, v_mla_w_uq, v_mla_w_uk, v_mla_w_uv, v_mla_w_o, v_kv_w_shared, v_swa_w_q, v_swa_sinks, v_swa_w_o, v_rel_bias, v_mlp_w_up, v_mlp_w_down, v_ln_mix_g, v_ln_mix_b, v_ln_mlp_g, v_ln_mlp_b):
    names = ["mla_w_in", "mla_g_q", "mla_g_kv", "mla_w_uq", "mla_w_uk", "mla_w_uv", "mla_w_o", "kv_w_shared",
             "swa_w_q", "swa_sinks", "swa_w_o", "rel_bias", "mlp_w_up", "mlp_w_down",
             "ln_mix_g", "ln_mix_b", "ln_mlp_g", "ln_mlp_b"]
    ws = dict(zip(names, [mla_w_in, mla_g_q, mla_g_kv, mla_w_uq, mla_w_uk, mla_w_uv, mla_w_o, kv_w_shared,
                          swa_w_q, swa_sinks, swa_w_o, rel_bias, mlp_w_up, mlp_w_down,
                          ln_mix_g, ln_mix_b, ln_mlp_g, ln_mlp_b]))
    ms = dict(zip(names, [m_mla_w_in, m_mla_g_q, m_mla_g_kv, m_mla_w_uq, m_mla_w_uk, m_mla_w_uv, m_mla_w_o,
                          m_kv_w_shared, m_swa_w_q, m_swa_sinks, m_swa_w_o, m_rel_bias, m_mlp_w_up, m_mlp_w_down,
                          m_ln_mix_g, m_ln_mix_b, m_ln_mlp_g, m_ln_mlp_b]))
    vs = dict(zip(names, [v_mla_w_in, v_mla_g_q, v_mla_g_kv, v_mla_w_uq, v_mla_w_uk, v_mla_w_uv, v_mla_w_o,
                          v_kv_w_shared, v_swa_w_q, v_swa_sinks, v_swa_w_o, v_rel_bias, v_mlp_w_up, v_mlp_w_down,
                          v_ln_mix_g, v_ln_mix_b, v_ln_mlp_g, v_ln_mlp_b]))
    xi, yi, ci = _mesh_pos()
    shard = 2 * xi + yi
    shard_shapes = {n: ws[n].shape for n in ROWS}
    wbf = {n: ws[n].astype(BF16) for n in ROWS}

    gains = jnp.concatenate([mla_g_q.reshape(-1), mla_g_kv.reshape(-1)])
    pieces = []
    for _ in range(3):
        head = lax.reduce_precision(gains, exponent_bits=8, mantissa_bits=7)
        pieces.append(head.astype(BF16))
        gains = gains - head
    ng = (QR + KVR) // 4
    gain_rows = jnp.pad(jnp.concatenate(pieces).reshape(1, 3 * ng), ((0, GAIN_ROWS - 1), (0, D - 3 * ng)))
    early = jnp.concatenate([_pack_group(AG_EARLY, wbf), gain_rows], axis=0)
    wall = lax.dynamic_update_slice(_allgather_weights(early), early[None], (shard, 0, 0))
    w = _full_from_gathered(AG_EARLY, wall, shard_shapes)
    gp = wall[:, early.shape[0] - GAIN_ROWS, :3 * ng].astype(F32).reshape(4, 3, ng)
    gains = (gp[:, 0] + gp[:, 1]) + gp[:, 2]
    w["mla_g_q"], w["mla_g_kv"] = gains[:, :QR // 4].reshape(QR), gains[:, QR // 4:].reshape(KVR)
    dist = _Dist(shard=shard, pos=jnp.stack([shard, ci, 2 * shard + ci]).astype(jnp.int32),
                 late_pack=_pack_group(AG_LATE, wbf), shard_shapes=shard_shapes)
    for n in ("swa_sinks", "rel_bias", "ln_mix_g", "ln_mix_b", "ln_mlp_g", "ln_mlp_b"):
        w[n] = ws[n]

    lpart, grad_x, g, reduced = _fwd_bwd(x[0], loss_target[0], w, dist)
    reduced["rest"] = jnp.concatenate([reduced["mid"], reduced["end"]], axis=0)

    small_like = {n: g[n] for n, _, _ in SMALL}
    small_sum = _allreduce_small(_pack_small(g, extra_row=lpart), "allreduce_small_grads")
    loss = 0.5 * jnp.sum(small_sum[LOSS_ROW]) / D
    gsm = _unpack_small(small_sum, small_like)
    gsm["mla_g_q"] = lax.dynamic_slice(gsm["mla_g_q"], (0, shard * (QR // 4)), (1, QR // 4))
    gsm["mla_g_kv"] = lax.dynamic_slice(gsm["mla_g_kv"], (0, shard * (KVR // 4)), (1, KVR // 4))

    gbig, dbig, mbig, vbig = {}, {}, {}, {}
    for n in ("mlp_w_up", "mlp_w_down"):
        off = 0 if n == "mlp_w_up" else ROWS["mlp_w_up"]
        gbig[n], dbig[n], mbig[n], vbig[n] = _adamw_layers(
            ws[n], ms[n], vs[n], reduced["mlp0"], reduced["mlp1"], off, f"adamw_{n}")
    rest = RS_MID + RS_END
    outs = _adamw(_pack_group(rest, ws), reduced["rest"], _pack_group(rest, ms), _pack_group(rest, vs),
                  "adamw_rest", tm=_row_tile(reduced["rest"].shape[0]))
    for dst, buf in zip((gbig, dbig, mbig, vbig), (reduced["rest"], *outs)):
        dst.update(_by_name(_unpack_group(rest, buf, ws)))
    dsm, msm, vsm = _adamw(_pack_small(ws), _pack_small(gsm), _pack_small(ms), _pack_small(vs), "adamw_small", tm=16)
    grads = {**gbig, **gsm}
    delta = {**dbig, **_unpack_small(dsm, ws)}
    new_m = {**mbig, **_unpack_small(msm, ws)}
    new_v = {**vbig, **_unpack_small(vsm, ws)}
    grads = {n: grads[n].reshape(ws[n].shape) for n in names}
    return (loss, grad_x[None], *[grads[n] for n in names], *[delta[n] for n in names],
            *[new_m[n] for n in names], *[new_v[n] for n in names])
```
